```python
import math
import jax, jax.numpy as jnp
from jax import lax
import numpy as np

D_MODEL = 1024
BATCH = 8
SEQ = 8192
DEPTH = 1

SSM_GROUP_SIZE = 16
SSM_GROUPS = 32
SSM_WIDTH = SSM_GROUP_SIZE * SSM_GROUPS
SSM_STATE = 64
SSM_CHUNK = 128
SSM_DT_MIN = 1e-3
SSM_DT_MAX = 1e-1
ATTN_PATTERNS = ((128, 1), (512, 4), (2048, 16))
ATTN_HEADS_PER_GROUP = 4
ATTN_HEAD_DIM = 64
ATTN_HEADS = ATTN_HEADS_PER_GROUP * len(ATTN_PATTERNS)
ATTN_WIDTH = ATTN_HEADS * ATTN_HEAD_DIM
ATTN_OUT_WIDTH = ATTN_HEADS_PER_GROUP * ATTN_HEAD_DIM
MEM_LEN = 256
MEM_HEADS = 4
MEM_HEAD_DIM = 128
MEM_WIDTH = MEM_HEADS * MEM_HEAD_DIM
N_BRANCHES = 3
D_FF = 4 * D_MODEL
IN_SPLITS = (SSM_WIDTH, ATTN_WIDTH, ATTN_WIDTH, ATTN_WIDTH, MEM_WIDTH, N_BRANCHES * D_MODEL)
IN_WIDTH = sum(IN_SPLITS)
IN_OFFSETS = tuple(int(o) for o in np.cumsum(IN_SPLITS)[:-1])
RMS_EPS = 1e-6
NEG_INF = -1e30

kernel_name = "hybrid_s5_dilated_attn_memory_gated_block"


def rms_norm(x, g):
    xf = x.astype(jnp.float32)
    y = xf * lax.rsqrt(jnp.mean(xf * xf, axis=-1, keepdims=True) + RMS_EPS)
    return (y * g.astype(jnp.float32)).astype(x.dtype)


def _complex_affine_combine(e1, e2):
    a1r, a1i, b1r, b1i = e1
    a2r, a2i, b2r, b2i = e2
    ar = a2r * a1r - a2i * a1i
    ai = a2r * a1i + a2i * a1r
    br = a2r * b1r - a2i * b1i + b2r
    bi = a2r * b1i + a2i * b1r + b2i
    return ar, ai, br, bi


def s5_ssm(u, lam_re, lam_im, log_dt, b_re, b_im, c_re, c_im, d_skip):
    f32 = jnp.float32
    bsz, l, _ = u.shape
    n_chunks = l // SSM_CHUNK
    u = u.astype(f32).reshape(bsz, n_chunks, SSM_CHUNK, SSM_GROUPS, SSM_GROUP_SIZE)
    u = u.transpose(1, 0, 2, 3, 4)
    lr, li = lam_re.astype(f32), lam_im.astype(f32)
    dt = jnp.exp(log_dt.astype(f32))[:, None]
    mag = jnp.exp(lr * dt)
    a_re, a_im = mag * jnp.cos(li * dt), mag * jnp.sin(li * dt)
    nr, ni = a_re - 1.0, a_im
    den = lr * lr + li * li
    coef_re = (nr * lr + ni * li) / den
    coef_im = (ni * lr - nr * li) / den
    br_, bi_ = b_re.astype(f32), b_im.astype(f32)
    bb_re = coef_re[..., None] * br_ - coef_im[..., None] * bi_
    bb_im = coef_re[..., None] * bi_ + coef_im[..., None] * br_
    cr, ci = c_re.astype(f32), c_im.astype(f32)
    dd = d_skip.astype(f32)

    def chunk_step(carry, u_c):
        s_re0, s_im0 = carry
        bu_re = jnp.einsum('bcgh,gph->bcgp', u_c, bb_re)
        bu_im = jnp.einsum('bcgh,gph->bcgp', u_c, bb_im)
        ar = jnp.broadcast_to(a_re, bu_re.shape)
        ai = jnp.broadcast_to(a_im, bu_re.shape)
        pr, pi, hr, hi = lax.associative_scan(_complex_affine_combine, (ar, ai, bu_re, bu_im), axis=1)
        s_re = hr + pr * s_re0[:, None] - pi * s_im0[:, None]
        s_im = hi + pr * s_im0[:, None] + pi * s_re0[:, None]
        y = (jnp.einsum('bcgp,ghp->bcgh', s_re, cr)
             - jnp.einsum('bcgp,ghp->bcgh', s_im, ci)
             + dd * u_c)
        return (s_re[:, -1], s_im[:, -1]), y

    init = (jnp.zeros((bsz, SSM_GROUPS, SSM_STATE), f32), jnp.zeros((bsz, SSM_GROUPS, SSM_STATE), f32))
    _, y = lax.scan(chunk_step, init, u)
    return y.transpose(1, 0, 2, 3, 4).reshape(bsz, l, SSM_WIDTH)


def dilated_window_attention(q, k, v, window, dilation):
    f32 = jnp.float32
    b, l, h, e = q.shape
    w = window // dilation
    m = l // dilation
    nb = -(-m // w)
    mp = nb * w

    def to_sub(t):
        t = t.reshape(b, m, dilation, h, e).transpose(0, 2, 3, 1, 4)
        t = jnp.pad(t, ((0, 0), (0, 0), (0, 0), (0, mp - m), (0, 0)))
        return t.reshape(b, dilation, h, nb, w, e)

    def with_prev(t):
        prev = jnp.pad(t, ((0, 0), (0, 0), (0, 0), (1, 0), (0, 0), (0, 0)))[:, :, :, :-1]
        return jnp.concatenate([prev, t], axis=4)

    qs, ks, vs = to_sub(q), to_sub(k), to_sub(v)
    kb, vb = with_prev(ks), with_prev(vs)
    s = jnp.einsum('bdhnqe,bdhnke->bdhnqk', qs, kb).astype(f32) * (e ** -0.5)
    qi = jnp.arange(w)[:, None]
    kj = jnp.arange(2 * w)[None, :]
    dist = w + qi - kj
    blk = jnp.arange(nb)[:, None, None]
    valid = (dist >= 0) & (dist <= w) & ((blk > 0) | (kj >= w))
    s = jnp.where(valid, s, NEG_INF)
    mx = jnp.max(s, axis=-1, keepdims=True)
    p = jnp.exp(s - mx)
    den = jnp.sum(p, axis=-1)
    o = jnp.einsum('bdhnqk,bdhnke->bdhnqe', p, vb.astype(f32)) / den[..., None]
    lse = mx[..., 0] + jnp.log(den)
    o = o.reshape(b, dilation, h, mp, e)[:, :, :, :m].transpose(0, 3, 1, 2, 4).reshape(b, l, h, e)
    lse = lse.reshape(b, dilation, h, mp)[..., :m].transpose(0, 3, 1, 2).reshape(b, l, h)
    return o, lse


def memory_cross_attention(q, k, v):
    b, l, hm, e = q.shape
    s = jnp.einsum('blhe,bmhe->bhlm', q, k).astype(jnp.float32) * (e ** -0.5)
    p = jax.nn.softmax(s, axis=-1)
    o = jnp.einsum('bhlm,bmhe->blhe', p, v.astype(jnp.float32))
    return o.astype(q.dtype).reshape(b, l, hm * e)


def _fwd_setup_inputs(seed: int = 0) -> dict:
    key = jax.random.key(seed)
    ks = jax.random.split(key, 26)
    f32 = jnp.float32

    def nrm(k, shape, scale):
        return jax.random.normal(k, shape, f32) * scale

    G, P, H = SSM_GROUPS, SSM_STATE, SSM_GROUP_SIZE
    return {
        "x": nrm(ks[0], (BATCH, SEQ, D_MODEL), 1.0),
        "mem": nrm(ks[1], (BATCH, MEM_LEN, D_MODEL), 1.0),
        "norm1_g": 1.0 + nrm(ks[2], (DEPTH, D_MODEL), 0.02),
        "mem_norm_g": 1.0 + nrm(ks[3], (DEPTH, D_MODEL), 0.02),
        "w_in": nrm(ks[4], (DEPTH, D_MODEL, IN_WIDTH), D_MODEL ** -0.5),
        "b_gate": nrm(ks[5], (DEPTH, N_BRANCHES * D_MODEL), 0.02),
        "ssm_lambda_re": -0.5 + nrm(ks[6], (DEPTH, G, P), 0.01),
        "ssm_lambda_im": math.pi * jnp.arange(P, dtype=f32) + nrm(ks[7], (DEPTH, G, P), 0.01),
        "ssm_log_dt": jax.random.uniform(ks[8], (DEPTH, G), f32, math.log(SSM_DT_MIN), math.log(SSM_DT_MAX)),
        "ssm_b_re": nrm(ks[9], (DEPTH, G, P, H), (2 * H) ** -0.5),
        "ssm_b_im": nrm(ks[10], (DEPTH, G, P, H), (2 * H) ** -0.5),
        "ssm_c_re": nrm(ks[11], (DEPTH, G, H, P), P ** -0.5),
        "ssm_c_im": nrm(ks[12], (DEPTH, G, H, P), P ** -0.5),
        "ssm_d": nrm(ks[13], (DEPTH, G, H), 1.0),
        "w_glu": nrm(ks[14], (DEPTH, SSM_WIDTH, SSM_WIDTH), SSM_WIDTH ** -0.5),
        "b_glu": nrm(ks[15], (DEPTH, SSM_WIDTH), 0.02),
        "w_ssm_br": nrm(ks[16], (DEPTH, SSM_WIDTH, D_MODEL), SSM_WIDTH ** -0.5),
        "w_attn_br": nrm(ks[17], (DEPTH, ATTN_OUT_WIDTH, D_MODEL), ATTN_OUT_WIDTH ** -0.5),
        "w_mem_kv": nrm(ks[18], (DEPTH, D_MODEL, 2 * MEM_WIDTH), D_MODEL ** -0.5),
        "w_mem_br": nrm(ks[19], (DEPTH, MEM_WIDTH, D_MODEL), MEM_WIDTH ** -0.5),
        "w_o": nrm(ks[20], (DEPTH, D_MODEL, D_MODEL), D_MODEL ** -0.5),
        "norm2_g": 1.0 + nrm(ks[21], (DEPTH, D_MODEL), 0.02),
        "w_up": nrm(ks[22], (DEPTH, D_MODEL, D_FF), D_MODEL ** -0.5),
        "w_down": nrm(ks[23], (DEPTH, D_FF, D_MODEL), D_FF ** -0.5),
        "final_g": 1.0 + nrm(ks[24], (D_MODEL,), 0.02),
    }


def _fwd_reference(x, mem, norm1_g, mem_norm_g, w_in, b_gate, ssm_lambda_re, ssm_lambda_im, ssm_log_dt,
              ssm_b_re, ssm_b_im, ssm_c_re, ssm_c_im, ssm_d, w_glu, b_glu, w_ssm_br, w_attn_br,
              w_mem_kv, w_mem_br, w_o, norm2_g, w_up, w_down, final_g):
    bsz, seq, _ = x.shape
    h = x
    for i in range(DEPTH):
        n = rms_norm(h, norm1_g[i])
        z = n @ w_in[i]
        u, q, k, v, mq, zg = jnp.split(z, IN_OFFSETS, axis=-1)
        gates = jax.nn.sigmoid(zg + b_gate[i]).reshape(bsz, seq, N_BRANCHES, D_MODEL)

        y = s5_ssm(u, ssm_lambda_re[i], ssm_lambda_im[i], ssm_log_dt[i], ssm_b_re[i], ssm_b_im[i],
                   ssm_c_re[i], ssm_c_im[i], ssm_d[i]).astype(x.dtype)
        y = jax.nn.gelu(y)
        y = y * jax.nn.sigmoid(y @ w_glu[i] + b_glu[i])
        br_ssm = y @ w_ssm_br[i]

        q = q.reshape(bsz, seq, ATTN_HEADS, ATTN_HEAD_DIM)
        k = k.reshape(bsz, seq, ATTN_HEADS, ATTN_HEAD_DIM)
        v = v.reshape(bsz, seq, ATTN_HEADS, ATTN_HEAD_DIM)
        outs, lses = [], []
        for g, (window, dilation) in enumerate(ATTN_PATTERNS):
            sl = slice(g * ATTN_HEADS_PER_GROUP, (g + 1) * ATTN_HEADS_PER_GROUP)
            o_g, lse_g = dilated_window_attention(q[:, :, sl], k[:, :, sl], v[:, :, sl], window, dilation)
            outs.append(o_g)
            lses.append(lse_g)
        wts = jax.nn.softmax(jnp.stack(lses, axis=0), axis=0)
        o = jnp.sum(wts[..., None] * jnp.stack(outs, axis=0), axis=0)
        br_attn = o.astype(x.dtype).reshape(bsz, seq, ATTN_OUT_WIDTH) @ w_attn_br[i]

        kv = rms_norm(mem, mem_norm_g[i]) @ w_mem_kv[i]
        mk, mv = jnp.split(kv, 2, axis=-1)
        mk = mk.reshape(bsz, MEM_LEN, MEM_HEADS, MEM_HEAD_DIM)
        mv = mv.reshape(bsz, MEM_LEN, MEM_HEADS, MEM_HEAD_DIM)
        mq = mq.reshape(bsz, seq, MEM_HEADS, MEM_HEAD_DIM)
        br_mem = memory_cross_attention(mq, mk, mv) @ w_mem_br[i]

        merged = gates[:, :, 0] * br_ssm + gates[:, :, 1] * br_attn + gates[:, :, 2] * br_mem
        h = h + merged @ w_o[i]

        n2 = rms_norm(h, norm2_g[i])
        h = h + jnp.square(jax.nn.relu(n2 @ w_up[i])) @ w_down[i]
    return rms_norm(h, final_g)


import jax as _jax
import jax.numpy as _jnp

TWIN_FORMAT = 'train_step'
FWD_PARAMS = ['x', 'mem', 'norm1_g', 'mem_norm_g', 'w_in', 'b_gate', 'ssm_lambda_re', 'ssm_lambda_im', 'ssm_log_dt', 'ssm_b_re', 'ssm_b_im', 'ssm_c_re', 'ssm_c_im', 'ssm_d', 'w_glu', 'b_glu', 'w_ssm_br', 'w_attn_br', 'w_mem_kv', 'w_mem_br', 'w_o', 'norm2_g', 'w_up', 'w_down', 'final_g']
TWIN_WEIGHTS = ['norm1_g', 'mem_norm_g', 'w_in', 'b_gate', 'ssm_lambda_re', 'ssm_lambda_im', 'ssm_log_dt', 'ssm_b_re', 'ssm_b_im', 'ssm_c_re', 'ssm_c_im', 'ssm_d', 'w_glu', 'b_glu', 'w_ssm_br', 'w_attn_br', 'w_mem_kv', 'w_mem_br', 'w_o', 'norm2_g', 'w_up', 'w_down', 'final_g']
TWIN_DIFF_INPUT = 'x'
TWIN_INPUTS = ['x', 'mem', 'norm1_g', 'mem_norm_g', 'w_in', 'b_gate', 'ssm_lambda_re', 'ssm_lambda_im', 'ssm_log_dt', 'ssm_b_re', 'ssm_b_im', 'ssm_c_re', 'ssm_c_im', 'ssm_d', 'w_glu', 'b_glu', 'w_ssm_br', 'w_attn_br', 'w_mem_kv', 'w_mem_br', 'w_o', 'norm2_g', 'w_up', 'w_down', 'final_g', 'loss_target', 'm_norm1_g', 'm_mem_norm_g', 'm_w_in', 'm_b_gate', 'm_ssm_lambda_re', 'm_ssm_lambda_im', 'm_ssm_log_dt', 'm_ssm_b_re', 'm_ssm_b_im', 'm_ssm_c_re', 'm_ssm_c_im', 'm_ssm_d', 'm_w_glu', 'm_b_glu', 'm_w_ssm_br', 'm_w_attn_br', 'm_w_mem_kv', 'm_w_mem_br', 'm_w_o', 'm_norm2_g', 'm_w_up', 'm_w_down', 'm_final_g', 'v_norm1_g', 'v_mem_norm_g', 'v_w_in', 'v_b_gate', 'v_ssm_lambda_re', 'v_ssm_lambda_im', 'v_ssm_log_dt', 'v_ssm_b_re', 'v_ssm_b_im', 'v_ssm_c_re', 'v_ssm_c_im', 'v_ssm_d', 'v_w_glu', 'v_b_glu', 'v_w_ssm_br', 'v_w_attn_br', 'v_w_mem_kv', 'v_w_mem_br', 'v_w_o', 'v_norm2_g', 'v_w_up', 'v_w_down', 'v_final_g']
TWIN_OUTPUTS = ['loss', 'grad_x', 'grad_norm1_g', 'grad_mem_norm_g', 'grad_w_in', 'grad_b_gate', 'grad_ssm_lambda_re', 'grad_ssm_lambda_im', 'grad_ssm_log_dt', 'grad_ssm_b_re', 'grad_ssm_b_im', 'grad_ssm_c_re', 'grad_ssm_c_im', 'grad_ssm_d', 'grad_w_glu', 'grad_b_glu', 'grad_w_ssm_br', 'grad_w_attn_br', 'grad_w_mem_kv', 'grad_w_mem_br', 'grad_w_o', 'grad_norm2_g', 'grad_w_up', 'grad_w_down', 'grad_final_g', 'delta_norm1_g', 'delta_mem_norm_g', 'delta_w_in', 'delta_b_gate', 'delta_ssm_lambda_re', 'delta_ssm_lambda_im', 'delta_ssm_log_dt', 'delta_ssm_b_re', 'delta_ssm_b_im', 'delta_ssm_c_re', 'delta_ssm_c_im', 'delta_ssm_d', 'delta_w_glu', 'delta_b_glu', 'delta_w_ssm_br', 'delta_w_attn_br', 'delta_w_mem_kv', 'delta_w_mem_br', 'delta_w_o', 'delta_norm2_g', 'delta_w_up', 'delta_w_down', 'delta_final_g', 'new_m_norm1_g', 'new_m_mem_norm_g', 'new_m_w_in', 'new_m_b_gate', 'new_m_ssm_lambda_re', 'new_m_ssm_lambda_im', 'new_m_ssm_log_dt', 'new_m_ssm_b_re', 'new_m_ssm_b_im', 'new_m_ssm_c_re', 'new_m_ssm_c_im', 'new_m_ssm_d', 'new_m_w_glu', 'new_m_b_glu', 'new_m_w_ssm_br', 'new_m_w_attn_br', 'new_m_w_mem_kv', 'new_m_w_mem_br', 'new_m_w_o', 'new_m_norm2_g', 'new_m_w_up', 'new_m_w_down', 'new_m_final_g', 'new_v_norm1_g', 'new_v_mem_norm_g', 'new_v_w_in', 'new_v_b_gate', 'new_v_ssm_lambda_re', 'new_v_ssm_lambda_im', 'new_v_ssm_log_dt', 'new_v_ssm_b_re', 'new_v_ssm_b_im', 'new_v_ssm_c_re', 'new_v_ssm_c_im', 'new_v_ssm_d', 'new_v_w_glu', 'new_v_b_glu', 'new_v_w_ssm_br', 'new_v_w_attn_br', 'new_v_w_mem_kv', 'new_v_w_mem_br', 'new_v_w_o', 'new_v_norm2_g', 'new_v_w_up', 'new_v_w_down', 'new_v_final_g']
TWIN_LEAF_KINDS = {'loss': 'loss', 'grad_x': 'grad_x', 'grad_norm1_g': 'grad_w', 'grad_mem_norm_g': 'grad_w', 'grad_w_in': 'grad_w', 'grad_b_gate': 'grad_w', 'grad_ssm_lambda_re': 'grad_w', 'grad_ssm_lambda_im': 'grad_w', 'grad_ssm_log_dt': 'grad_w', 'grad_ssm_b_re': 'grad_w', 'grad_ssm_b_im': 'grad_w', 'grad_ssm_c_re': 'grad_w', 'grad_ssm_c_im': 'grad_w', 'grad_ssm_d': 'grad_w', 'grad_w_glu': 'grad_w', 'grad_b_glu': 'grad_w', 'grad_w_ssm_br': 'grad_w', 'grad_w_attn_br': 'grad_w', 'grad_w_mem_kv': 'grad_w', 'grad_w_mem_br': 'grad_w', 'grad_w_o': 'grad_w', 'grad_norm2_g': 'grad_w', 'grad_w_up': 'grad_w', 'grad_w_down': 'grad_w', 'grad_final_g': 'grad_w', 'delta_norm1_g': 'delta_w', 'delta_mem_norm_g': 'delta_w', 'delta_w_in': 'delta_w', 'delta_b_gate': 'delta_w', 'delta_ssm_lambda_re': 'delta_w', 'delta_ssm_lambda_im': 'delta_w', 'delta_ssm_log_dt': 'delta_w', 'delta_ssm_b_re': 'delta_w', 'delta_ssm_b_im': 'delta_w', 'delta_ssm_c_re': 'delta_w', 'delta_ssm_c_im': 'delta_w', 'delta_ssm_d': 'delta_w', 'delta_w_glu': 'delta_w', 'delta_b_glu': 'delta_w', 'delta_w_ssm_br': 'delta_w', 'delta_w_attn_br': 'delta_w', 'delta_w_mem_kv': 'delta_w', 'delta_w_mem_br': 'delta_w', 'delta_w_o': 'delta_w', 'delta_norm2_g': 'delta_w', 'delta_w_up': 'delta_w', 'delta_w_down': 'delta_w', 'delta_final_g': 'delta_w', 'new_m_norm1_g': 'new_m', 'new_m_mem_norm_g': 'new_m', 'new_m_w_in': 'new_m', 'new_m_b_gate': 'new_m', 'new_m_ssm_lambda_re': 'new_m', 'new_m_ssm_lambda_im': 'new_m', 'new_m_ssm_log_dt': 'new_m', 'new_m_ssm_b_re': 'new_m', 'new_m_ssm_b_im': 'new_m', 'new_m_ssm_c_re': 'new_m', 'new_m_ssm_c_im': 'new_m', 'new_m_ssm_d': 'new_m', 'new_m_w_glu': 'new_m', 'new_m_b_glu': 'new_m', 'new_m_w_ssm_br': 'new_m', 'new_m_w_attn_br': 'new_m', 'new_m_w_mem_kv': 'new_m', 'new_m_w_mem_br': 'new_m', 'new_m_w_o': 'new_m', 'new_m_norm2_g': 'new_m', 'new_m_w_up': 'new_m', 'new_m_w_down': 'new_m', 'new_m_final_g': 'new_m', 'new_v_norm1_g': 'new_v', 'new_v_mem_norm_g': 'new_v', 'new_v_w_in': 'new_v', 'new_v_b_gate': 'new_v', 'new_v_ssm_lambda_re': 'new_v', 'new_v_ssm_lambda_im': 'new_v', 'new_v_ssm_log_dt': 'new_v', 'new_v_ssm_b_re': 'new_v', 'new_v_ssm_b_im': 'new_v', 'new_v_ssm_c_re': 'new_v', 'new_v_ssm_c_im': 'new_v', 'new_v_ssm_d': 'new_v', 'new_v_w_glu': 'new_v', 'new_v_b_glu': 'new_v', 'new_v_w_ssm_br': 'new_v', 'new_v_w_attn_br': 'new_v', 'new_v_w_mem_kv': 'new_v', 'new_v_w_mem_br': 'new_v', 'new_v_w_o': 'new_v', 'new_v_norm2_g': 'new_v', 'new_v_w_up': 'new_v', 'new_v_w_down': 'new_v', 'new_v_final_g': 'new_v'}


def _forward(args):
    return _fwd_reference(*[args[k] for k in FWD_PARAMS])


def _output_shape():
    out = _jax.eval_shape(lambda: _forward(_fwd_setup_inputs(0)))
    return out.shape, out.dtype

N_MICROBATCH = 1
ADAM_LR = 0.001
ADAM_B1 = 0.9
ADAM_B2 = 0.999
ADAM_EPS = 1e-08
ADAM_WD = 0.01
ADAM_STEP = 10
PER_EXAMPLE_BATCH_AXIS = {'x': 0, 'mem': 0, 'loss_target': 0}
SHARED_INPUTS = []
_WEIGHT_DTYPES = {'norm1_g': _jnp.float32, 'mem_norm_g': _jnp.float32, 'w_in': _jnp.float32, 'b_gate': _jnp.float32, 'ssm_lambda_re': _jnp.float32, 'ssm_lambda_im': _jnp.float32, 'ssm_log_dt': _jnp.float32, 'ssm_b_re': _jnp.float32, 'ssm_b_im': _jnp.float32, 'ssm_c_re': _jnp.float32, 'ssm_c_im': _jnp.float32, 'ssm_d': _jnp.float32, 'w_glu': _jnp.float32, 'b_glu': _jnp.float32, 'w_ssm_br': _jnp.float32, 'w_attn_br': _jnp.float32, 'w_mem_kv': _jnp.float32, 'w_mem_br': _jnp.float32, 'w_o': _jnp.float32, 'norm2_g': _jnp.float32, 'w_up': _jnp.float32, 'w_down': _jnp.float32, 'final_g': _jnp.float32}
MOMENT_SCALE = {'norm1_g': 7.598377e-02, 'mem_norm_g': 2.319541e-02, 'w_in': 3.026627e-02, 'b_gate': 1.361375e-02, 'ssm_lambda_re': 7.301716e-03, 'ssm_lambda_im': 6.362663e-03, 'ssm_log_dt': 5.727046e+00, 'ssm_b_re': 3.754737e-03, 'ssm_b_im': 3.750746e-03, 'ssm_c_re': 5.144710e-03, 'ssm_c_im': 5.118836e-03, 'ssm_d': 7.887176e-02, 'w_glu': 2.241028e-02, 'b_glu': 3.394632e-02, 'w_ssm_br': 5.620694e-02, 'w_attn_br': 2.721072e-02, 'w_mem_kv': 2.173365e-02, 'w_mem_br': 1.520775e-02, 'w_o': 6.016021e-02, 'norm2_g': 2.395918e-01, 'w_up': 1.119098e-01, 'w_down': 2.137061e-01, 'final_g': 6.457905e+01}


def _to_microbatches(a, axis):
    t = _jnp.moveaxis(a, axis, 0)
    t = t.reshape((N_MICROBATCH, t.shape[0] // N_MICROBATCH) + t.shape[1:])
    return _jnp.moveaxis(t, 1, axis + 1)


def setup_inputs(seed: int = 0) -> dict:
    inp = _fwd_setup_inputs(seed)
    key = _jax.random.fold_in(_jax.random.key(seed), 7919)
    shape, _ = _output_shape()
    out = dict(inp)
    out["loss_target"] = _jax.random.normal(_jax.random.fold_in(key, 0), shape, _jnp.float32)
    for i, name in enumerate(TWIN_WEIGHTS):
        w = inp[name].astype(_jnp.float32)
        if MOMENT_SCALE is None:
            s = _jnp.sqrt(_jnp.mean(_jnp.square(w)) + 1e-30)
        else:
            s = MOMENT_SCALE[name]
        km, kv = _jax.random.split(_jax.random.fold_in(key, i + 1))
        out[name] = w
        out["m_" + name] = s * _jax.random.normal(km, w.shape, _jnp.float32)
        out["v_" + name] = (s * s) * _jax.random.uniform(kv, w.shape, _jnp.float32, 0.5, 1.5)
    if N_MICROBATCH > 1:
        for name, axis in PER_EXAMPLE_BATCH_AXIS.items():
            out[name] = _to_microbatches(out[name], axis)
    return {'x': out['x'], 'mem': out['mem'], 'norm1_g': out['norm1_g'], 'mem_norm_g': out['mem_norm_g'], 'w_in': out['w_in'], 'b_gate': out['b_gate'], 'ssm_lambda_re': out['ssm_lambda_re'], 'ssm_lambda_im': out['ssm_lambda_im'], 'ssm_log_dt': out['ssm_log_dt'], 'ssm_b_re': out['ssm_b_re'], 'ssm_b_im': out['ssm_b_im'], 'ssm_c_re': out['ssm_c_re'], 'ssm_c_im': out['ssm_c_im'], 'ssm_d': out['ssm_d'], 'w_glu': out['w_glu'], 'b_glu': out['b_glu'], 'w_ssm_br': out['w_ssm_br'], 'w_attn_br': out['w_attn_br'], 'w_mem_kv': out['w_mem_kv'], 'w_mem_br': out['w_mem_br'], 'w_o': out['w_o'], 'norm2_g': out['norm2_g'], 'w_up': out['w_up'], 'w_down': out['w_down'], 'final_g': out['final_g'], 'loss_target': out['loss_target'], 'm_norm1_g': out['m_norm1_g'], 'm_mem_norm_g': out['m_mem_norm_g'], 'm_w_in': out['m_w_in'], 'm_b_gate': out['m_b_gate'], 'm_ssm_lambda_re': out['m_ssm_lambda_re'], 'm_ssm_lambda_im': out['m_ssm_lambda_im'], 'm_ssm_log_dt': out['m_ssm_log_dt'], 'm_ssm_b_re': out['m_ssm_b_re'], 'm_ssm_b_im': out['m_ssm_b_im'], 'm_ssm_c_re': out['m_ssm_c_re'], 'm_ssm_c_im': out['m_ssm_c_im'], 'm_ssm_d': out['m_ssm_d'], 'm_w_glu': out['m_w_glu'], 'm_b_glu': out['m_b_glu'], 'm_w_ssm_br': out['m_w_ssm_br'], 'm_w_attn_br': out['m_w_attn_br'], 'm_w_mem_kv': out['m_w_mem_kv'], 'm_w_mem_br': out['m_w_mem_br'], 'm_w_o': out['m_w_o'], 'm_norm2_g': out['m_norm2_g'], 'm_w_up': out['m_w_up'], 'm_w_down': out['m_w_down'], 'm_final_g': out['m_final_g'], 'v_norm1_g': out['v_norm1_g'], 'v_mem_norm_g': out['v_mem_norm_g'], 'v_w_in': out['v_w_in'], 'v_b_gate': out['v_b_gate'], 'v_ssm_lambda_re': out['v_ssm_lambda_re'], 'v_ssm_lambda_im': out['v_ssm_lambda_im'], 'v_ssm_log_dt': out['v_ssm_log_dt'], 'v_ssm_b_re': out['v_ssm_b_re'], 'v_ssm_b_im': out['v_ssm_b_im'], 'v_ssm_c_re': out['v_ssm_c_re'], 'v_ssm_c_im': out['v_ssm_c_im'], 'v_ssm_d': out['v_ssm_d'], 'v_w_glu': out['v_w_glu'], 'v_b_glu': out['v_b_glu'], 'v_w_ssm_br': out['v_w_ssm_br'], 'v_w_attn_br': out['v_w_attn_br'], 'v_w_mem_kv': out['v_w_mem_kv'], 'v_w_mem_br': out['v_w_mem_br'], 'v_w_o': out['v_w_o'], 'v_norm2_g': out['v_norm2_g'], 'v_w_up': out['v_w_up'], 'v_w_down': out['v_w_down'], 'v_final_g': out['v_final_g']}


def _loss(weights, diff, rest, loss_target):
    with _jax.named_scope("forward"):
        args = {**rest, TWIN_DIFF_INPUT: diff, **{k: w.astype(_WEIGHT_DTYPES[k]) for k, w in weights.items()}}
        y = _forward(args)
    with _jax.named_scope("loss_head"):
        err = _jnp.square(y.astype(_jnp.float32) - loss_target)
        return 0.5 * _jnp.sum(_jnp.mean(err, axis=-1)) if err.ndim else 0.5 * err


def _adamw(w, g, m, v):
    m = ADAM_B1 * m + (1.0 - ADAM_B1) * g
    v = ADAM_B2 * v + (1.0 - ADAM_B2) * _jnp.square(g)
    m_hat = m / (1.0 - ADAM_B1 ** ADAM_STEP)
    v_hat = v / (1.0 - ADAM_B2 ** ADAM_STEP)
    delta = -ADAM_LR * (m_hat / (_jnp.sqrt(v_hat) + ADAM_EPS) + ADAM_WD * w)
    return delta, m, v


def reference(x, mem, norm1_g, mem_norm_g, w_in, b_gate, ssm_lambda_re, ssm_lambda_im, ssm_log_dt, ssm_b_re, ssm_b_im, ssm_c_re, ssm_c_im, ssm_d, w_glu, b_glu, w_ssm_br, w_attn_br, w_mem_kv, w_mem_br, w_o, norm2_g, w_up, w_down, final_g, loss_target, m_norm1_g, m_mem_norm_g, m_w_in, m_b_gate, m_ssm_lambda_re, m_ssm_lambda_im, m_ssm_log_dt, m_ssm_b_re, m_ssm_b_im, m_ssm_c_re, m_ssm_c_im, m_ssm_d, m_w_glu, m_b_glu, m_w_ssm_br, m_w_attn_br, m_w_mem_kv, m_w_mem_br, m_w_o, m_norm2_g, m_w_up, m_w_down, m_final_g, v_norm1_g, v_mem_norm_g, v_w_in, v_b_gate, v_ssm_lambda_re, v_ssm_lambda_im, v_ssm_log_dt, v_ssm_b_re, v_ssm_b_im, v_ssm_c_re, v_ssm_c_im, v_ssm_d, v_w_glu, v_b_glu, v_w_ssm_br, v_w_attn_br, v_w_mem_kv, v_w_mem_br, v_w_o, v_norm2_g, v_w_up, v_w_down, v_final_g):
    given = dict(x=x, mem=mem, norm1_g=norm1_g, mem_norm_g=mem_norm_g, w_in=w_in, b_gate=b_gate, ssm_lambda_re=ssm_lambda_re, ssm_lambda_im=ssm_lambda_im, ssm_log_dt=ssm_log_dt, ssm_b_re=ssm_b_re, ssm_b_im=ssm_b_im, ssm_c_re=ssm_c_re, ssm_c_im=ssm_c_im, ssm_d=ssm_d, w_glu=w_glu, b_glu=b_glu, w_ssm_br=w_ssm_br, w_attn_br=w_attn_br, w_mem_kv=w_mem_kv, w_mem_br=w_mem_br, w_o=w_o, norm2_g=norm2_g, w_up=w_up, w_down=w_down, final_g=final_g, loss_target=loss_target, m_norm1_g=m_norm1_g, m_mem_norm_g=m_mem_norm_g, m_w_in=m_w_in, m_b_gate=m_b_gate, m_ssm_lambda_re=m_ssm_lambda_re, m_ssm_lambda_im=m_ssm_lambda_im, m_ssm_log_dt=m_ssm_log_dt, m_ssm_b_re=m_ssm_b_re, m_ssm_b_im=m_ssm_b_im, m_ssm_c_re=m_ssm_c_re, m_ssm_c_im=m_ssm_c_im, m_ssm_d=m_ssm_d, m_w_glu=m_w_glu, m_b_glu=m_b_glu, m_w_ssm_br=m_w_ssm_br, m_w_attn_br=m_w_attn_br, m_w_mem_kv=m_w_mem_kv, m_w_mem_br=m_w_mem_br, m_w_o=m_w_o, m_norm2_g=m_norm2_g, m_w_up=m_w_up, m_w_down=m_w_down, m_final_g=m_final_g, v_norm1_g=v_norm1_g, v_mem_norm_g=v_mem_norm_g, v_w_in=v_w_in, v_b_gate=v_b_gate, v_ssm_lambda_re=v_ssm_lambda_re, v_ssm_lambda_im=v_ssm_lambda_im, v_ssm_log_dt=v_ssm_log_dt, v_ssm_b_re=v_ssm_b_re, v_ssm_b_im=v_ssm_b_im, v_ssm_c_re=v_ssm_c_re, v_ssm_c_im=v_ssm_c_im, v_ssm_d=v_ssm_d, v_w_glu=v_w_glu, v_b_glu=v_b_glu, v_w_ssm_br=v_w_ssm_br, v_w_attn_br=v_w_attn_br, v_w_mem_kv=v_w_mem_kv, v_w_mem_br=v_w_mem_br, v_w_o=v_w_o, v_norm2_g=v_norm2_g, v_w_up=v_w_up, v_w_down=v_w_down, v_final_g=v_final_g)
    weights = {n: given[n] for n in TWIN_WEIGHTS}
    shared = {n: given[n] for n in SHARED_INPUTS}
    per_example = {n: given[n] for n in ['x', 'mem']}
    grad_fn = _jax.value_and_grad(_loss, argnums=(0, 1))

    def one_microbatch(ex, loss_target):
        ex = dict(ex)
        diff = ex.pop(TWIN_DIFF_INPUT)
        return grad_fn(weights, diff, {**shared, **ex}, loss_target)

    if N_MICROBATCH == 1:
        loss, (grad_w, grad_x) = one_microbatch(per_example, given["loss_target"])
    else:
        def body(carry, xs):
            loss_sum, grad_sum = carry
            l_k, (gw_k, gx_k) = one_microbatch(xs[0], xs[1])
            with _jax.named_scope("update"):
                return (loss_sum + l_k, _jax.tree.map(_jnp.add, grad_sum, gw_k)), gx_k

        init = (_jnp.zeros((), _jnp.float32), _jax.tree.map(_jnp.zeros_like, weights))
        (loss, grad_w), grad_x = _jax.lax.scan(body, init, (per_example, given["loss_target"]))
    with _jax.named_scope("update"):
        delta_w, new_m, new_v = {}, {}, {}
        for n in TWIN_WEIGHTS:
            delta_w[n], new_m[n], new_v[n] = _adamw(weights[n], grad_w[n], given["m_" + n], given["v_" + n])
    return (loss, grad_x, *[grad_w[n] for n in TWIN_WEIGHTS], *[delta_w[n] for n in TWIN_WEIGHTS],
            *[new_m[n] for n in TWIN_WEIGHTS], *[new_v[n] for n in TWIN_WEIGHTS])
```

```python
import functools
import math

import numpy as np
import jax
import jax.numpy as jnp
from jax import lax
from jax.experimental import pallas as pl
from jax.experimental.pallas import tpu as pltpu

F32 = jnp.float32
BF16 = jnp.bfloat16

D_MODEL = 1024
SSM_GROUPS = 32
SSM_GROUP_SIZE = 16
SSM_STATE = 64
SSM_WIDTH = 512
N_STATES = SSM_GROUPS * SSM_STATE
SCAN_CB = 512
ATTN_PATTERNS = ((128, 1), (512, 4), (2048, 16))
ATTN_HEAD_DIM = 64
ATTN_Q = 128
MEM_LEN = 256
MEM_HEAD_DIM = 128
MEM_HEADS = 4
D_FF = 4096
OFF_U, OFF_QKV, OFF_MQ, OFF_ZG = 0, 512, 2816, 3328
IN_WIDTH = 6400
RMS_EPS = 1e-6
NEG_INF = -1e30
ADAM_LR, ADAM_B1, ADAM_B2, ADAM_EPS, ADAM_WD, ADAM_STEP = 0.001, 0.9, 0.999, 1e-08, 0.01, 10

VMEM_LIMIT_BYTES = 48 * 1024 * 1024
LANES = 128
MESH = pl.DeviceIdType.MESH
N_CHIPS = 4

BIG = (("w_in", 1, (1024, 6400)), ("w_glu", 0, (512, 512)), ("w_ssm_br", 1, (512, 1024)),
       ("w_attn_br", 1, (256, 1024)), ("w_mem_kv", 0, (1024, 1024)), ("w_mem_br", 1, (512, 1024)),
       ("w_o", 0, (1024, 1024)), ("w_up", 1, (1024, 4096)), ("w_down", 0, (4096, 1024)))
SMALL = (("norm1_g", (1, 1024)), ("mem_norm_g", (1, 1024)), ("b_gate", (1, 3072)),
         ("ssm_lambda_re", (1, 32, 64)), ("ssm_lambda_im", (1, 32, 64)), ("ssm_log_dt", (1, 32)),
         ("ssm_b_re", (1, 32, 64, 16)), ("ssm_b_im", (1, 32, 64, 16)), ("ssm_c_re", (1, 32, 16, 64)),
         ("ssm_c_im", (1, 32, 16, 64)), ("ssm_d", (1, 32, 16)), ("b_glu", (1, 512)),
         ("norm2_g", (1, 1024)), ("final_g", (1024,)))
WEIGHT_ORDER = ("norm1_g", "mem_norm_g", "w_in", "b_gate", "ssm_lambda_re", "ssm_lambda_im", "ssm_log_dt",
                "ssm_b_re", "ssm_b_im", "ssm_c_re", "ssm_c_im", "ssm_d", "w_glu", "b_glu", "w_ssm_br",
                "w_attn_br", "w_mem_kv", "w_mem_br", "w_o", "norm2_g", "w_up", "w_down", "final_g")
BIG_SHARD_ELEMS = sum(r * c for _, _, (r, c) in BIG) // N_CHIPS
BIG_ROWS = BIG_SHARD_ELEMS // 1024
SMALL_ELEMS = sum(int(np.prod(s)) for _, s in SMALL)
SMALL_ROWS = 64
GRAD_ROWS = BIG_ROWS + SMALL_ROWS
HALF_ROWS = GRAD_ROWS // 2


def _params(sem):
    return pltpu.CompilerParams(dimension_semantics=sem, vmem_limit_bytes=VMEM_LIMIT_BYTES)


def _sigmoid(v):
    return 1.0 / (1.0 + jnp.exp(-v))


_GELU_C = math.sqrt(2.0 / math.pi)


def _gelu(v):
    return 0.5 * v * (1.0 + jnp.tanh(_GELU_C * (v + 0.044715 * v * v * v)))


def _gelu_grad(v):
    th = jnp.tanh(_GELU_C * (v + 0.044715 * v * v * v))
    return 0.5 * (1.0 + th) + 0.5 * v * (1.0 - th * th) * _GELU_C * (1.0 + 3.0 * 0.044715 * v * v)


def _dot(a, b, ca, cb):
    return lax.dot_general(a, b, (((ca,), (cb,)), ((), ())), preferred_element_type=F32)


def _matmul(a, b, *, m, n, k, ta=False, tb=False, tm, tn, tk, out_dtypes, name,
            a_off=(0, 0), b_off=(0, 0), aux=(), epilogue=None):
    assert m % tm == 0 and n % tn == 0 and k % tk == 0, (name, m, n, k, tm, tn, tk)
    nk = k // tk
    n_aux = len(aux)
    n_out = len(out_dtypes)
    ar, ac = a_off
    br, bc = b_off
    if ta:
        a_spec = pl.BlockSpec((tk, tm), lambda i, j, kk: (kk + ar, i + ac))
    else:
        a_spec = pl.BlockSpec((tm, tk), lambda i, j, kk: (i + ar, kk + ac))
    if tb:
        b_spec = pl.BlockSpec((tn, tk), lambda i, j, kk: (j + br, kk + bc))
    else:
        b_spec = pl.BlockSpec((tk, tn), lambda i, j, kk: (kk + br, j + bc))
    aux_specs = []
    for _, kind in aux:
        if kind == "mn":
            aux_specs.append(pl.BlockSpec((tm, tn), lambda i, j, kk: (i, j)))
        else:
            aux_specs.append(pl.BlockSpec((1, tn), lambda i, j, kk: (0, j)))
    ca = 0 if ta else 1
    cb = 1 if tb else 0

    def body(a_ref, b_ref, *rest):
        aux_refs = rest[:n_aux]
        out_refs = rest[n_aux:n_aux + n_out]
        acc_ref = rest[n_aux + n_out]
        kk = pl.program_id(2)

        @pl.when(kk == 0)
        def _():
            acc_ref[...] = jnp.zeros_like(acc_ref)

        acc_ref[...] += _dot(a_ref[...].astype(BF16), b_ref[...].astype(BF16), ca, cb)

        @pl.when(kk == nk - 1)
        def _():
            acc = acc_ref[...]
            outs = (acc,) if epilogue is None else epilogue(acc, *[r[...] for r in aux_refs])
            for o_ref, o in zip(out_refs, outs):
                o_ref[...] = o.astype(o_ref.dtype)

    res = pl.pallas_call(
        body, name=name, grid=(m // tm, n // tn, nk),
        in_specs=[a_spec, b_spec] + aux_specs,
        out_specs=[pl.BlockSpec((tm, tn), lambda i, j, kk: (i, j)) for _ in range(n_out)],
        out_shape=[jax.ShapeDtypeStruct((m, n), dt) for dt in out_dtypes],
        scratch_shapes=[pltpu.VMEM((tm, tn), F32)],
        compiler_params=_params(("parallel", "parallel", "arbitrary")),
    )(a, b, *[x for x, _ in aux])
    return res[0] if n_out == 1 else tuple(res)


def _rmsnorm_fwd(x, g, *, tm, name):
    rows, d = x.shape

    def body(x_ref, g_ref, o_ref):
        xv = x_ref[...]
        r = lax.rsqrt(jnp.mean(xv * xv, axis=-1, keepdims=True) + RMS_EPS)
        o_ref[...] = (xv * r * g_ref[...]).astype(o_ref.dtype)

    return pl.pallas_call(
        body, name=name, grid=(rows // tm,),
        in_specs=[pl.BlockSpec((tm, d), lambda i: (i, 0)), pl.BlockSpec((1, d), lambda i: (0, 0))],
        out_specs=pl.BlockSpec((tm, d), lambda i: (i, 0)),
        out_shape=jax.ShapeDtypeStruct((rows, d), BF16),
        compiler_params=_params(("parallel",)),
    )(x, g)


def _rmsnorm_bwd(x, g, dy, res, *, tm, name):
    rows, d = x.shape
    has_res = res is not None

    def body(x_ref, g_ref, dy_ref, *rest):
        if has_res:
            res_ref, dx_ref, dg_ref = rest
        else:
            dx_ref, dg_ref = rest
        i = pl.program_id(0)
        xv = x_ref[...]
        r = lax.rsqrt(jnp.mean(xv * xv, axis=-1, keepdims=True) + RMS_EPS)
        xhat = xv * r
        dyv = dy_ref[...]
        dyg = dyv * g_ref[...]
        dx = r * (dyg - xhat * jnp.mean(dyg * xhat, axis=-1, keepdims=True))
        if has_res:
            dx = dx + res_ref[...]
        dx_ref[...] = dx

        @pl.when(i == 0)
        def _():
            dg_ref[...] = jnp.zeros_like(dg_ref)

        dg_ref[...] += jnp.sum(dyv * xhat, axis=0, keepdims=True)

    row_spec = pl.BlockSpec((tm, d), lambda i: (i, 0))
    vec_spec = pl.BlockSpec((1, d), lambda i: (0, 0))
    ins = [x, g, dy] + ([res] if has_res else [])
    return pl.pallas_call(
        body, name=name, grid=(rows // tm,),
        in_specs=[row_spec, vec_spec, row_spec] + ([row_spec] if has_res else []),
        out_specs=[row_spec, vec_spec],
        out_shape=[jax.ShapeDtypeStruct((rows, d), F32), jax.ShapeDtypeStruct((1, d), F32)],
        compiler_params=_params(("arbitrary",)),
    )(*ins)


def _loss_head(h, tgt, g, *, tm, name):
    rows, d = h.shape
    nsteps = rows // tm

    def body(h_ref, t_ref, g_ref, dh_ref, loss_ref, dg_ref, sq_ref):
        i = pl.program_id(0)
        xv = h_ref[...]
        gv = g_ref[...]
        r = lax.rsqrt(jnp.mean(xv * xv, axis=-1, keepdims=True) + RMS_EPS)
        xhat = xv * r
        err = xhat * gv - t_ref[...]
        dyv = err * (1.0 / d)
        dyg = dyv * gv
        dh_ref[...] = r * (dyg - xhat * jnp.mean(dyg * xhat, axis=-1, keepdims=True))

        @pl.when(i == 0)
        def _():
            dg_ref[...] = jnp.zeros_like(dg_ref)
            sq_ref[...] = jnp.zeros_like(sq_ref)

        dg_ref[...] += jnp.sum(dyv * xhat, axis=0, keepdims=True)
        sq_ref[...] += jnp.sum(err * err, axis=0, keepdims=True)

        @pl.when(i == nsteps - 1)
        def _():
            tot = jnp.sum(sq_ref[...], axis=-1, keepdims=True) * (0.5 / d)
            loss_ref[...] = jnp.broadcast_to(tot, loss_ref.shape)

    row_spec = pl.BlockSpec((tm, d), lambda i: (i, 0))
    vec_spec = pl.BlockSpec((1, d), lambda i: (0, 0))
    return pl.pallas_call(
        body, name=name, grid=(nsteps,),
        in_specs=[row_spec, row_spec, vec_spec],
        out_specs=[row_spec, pl.BlockSpec((1, LANES), lambda i: (0, 0)), vec_spec],
        out_shape=[jax.ShapeDtypeStruct((rows, d), F32), jax.ShapeDtypeStruct((1, LANES), F32),
                   jax.ShapeDtypeStruct((1, d), F32)],
        scratch_shapes=[pltpu.VMEM((1, d), F32)],
        compiler_params=_params(("arbitrary",)),
    )(h, tgt, g)


def _to_scan_layout(v):
    lead = v.shape[:-2]
    v = v.reshape(lead + (2, N_STATES // SCAN_CB, SCAN_CB))
    v = jnp.swapaxes(v, -3, -2)
    return v.reshape(lead + (2 * N_STATES,))


def _ssm_matrices(lam_re, lam_im, log_dt, b_re, b_im, c_re, c_im):
    dt = jnp.exp(log_dt)[:, None]
    mag = jnp.exp(lam_re * dt)
    a_re, a_im = mag * jnp.cos(lam_im * dt), mag * jnp.sin(lam_im * dt)
    nr, ni = a_re - 1.0, a_im
    den = lam_re * lam_re + lam_im * lam_im
    coef_re = (nr * lam_re + ni * lam_im) / den
    coef_im = (ni * lam_re - nr * lam_im) / den
    bb_re = coef_re[..., None] * b_re - coef_im[..., None] * b_im
    bb_im = coef_re[..., None] * b_im + coef_im[..., None] * b_re
    eye = jnp.eye(SSM_GROUPS, dtype=F32)
    a_lay = _to_scan_layout(jnp.stack([a_re.reshape(-1), a_im.reshape(-1)], axis=0))[None, :]

    def b_dense(bb):
        return jnp.einsum("gk,kph->ghkp", eye, bb).reshape(SSM_WIDTH, N_STATES)

    b_lay = _to_scan_layout(jnp.stack([b_dense(bb_re), b_dense(bb_im)], axis=1))

    def c_dense(cc):
        return jnp.einsum("gk,ghp->kpgh", eye, cc).reshape(N_STATES, SSM_WIDTH)

    c_lay = _to_scan_layout(jnp.stack([c_dense(c_re), -c_dense(c_im)], axis=0).transpose(2, 0, 1)).T
    return a_lay, b_lay, c_lay


def _scan(a_lay, bu, *, reverse, tt, name):
    rows, width = bu.shape
    cb = SCAN_CB
    nt = rows // tt

    def body(a_ref, bu_ref, o_ref, carry_ref):
        kk = pl.program_id(1)

        @pl.when(kk == 0)
        def _():
            carry_ref[...] = jnp.zeros_like(carry_ref)

        ar = a_ref[:, :cb]
        ai = a_ref[:, cb:]

        def step(i, st):
            sr, si = st
            t = (tt - 1 - i) if reverse else i
            row = bu_ref[pl.ds(t, 1), :]
            nr = ar * sr - ai * si + row[:, :cb]
            ni = ar * si + ai * sr + row[:, cb:]
            o_ref[pl.ds(t, 1), :] = jnp.concatenate([nr, ni], axis=1)
            return nr, ni

        sr, si = lax.fori_loop(0, tt, step, (carry_ref[0:1, :cb], carry_ref[0:1, cb:]), unroll=8)
        carry_ref[0:1, :cb] = sr
        carry_ref[0:1, cb:] = si

    tmap = (lambda j, kk: (nt - 1 - kk, j)) if reverse else (lambda j, kk: (kk, j))
    return pl.pallas_call(
        body, name=name, grid=(width // (2 * cb), nt),
        in_specs=[pl.BlockSpec((1, 2 * cb), lambda j, kk: (0, j)), pl.BlockSpec((tt, 2 * cb), tmap)],
        out_specs=pl.BlockSpec((tt, 2 * cb), tmap),
        out_shape=jax.ShapeDtypeStruct((rows, width), F32),
        scratch_shapes=[pltpu.VMEM((8, 2 * cb), F32)],
        compiler_params=_params(("parallel", "arbitrary")),
    )(a_lay, bu)


def _ssm_da(lam, s, *, tt, name):
    rows, width = s.shape
    cb = SCAN_CB

    def body(lam_ref, s_ref, da_ref, last_ref):
        kk = pl.program_id(1)

        @pl.when(kk == 0)
        def _():
            last_ref[...] = jnp.zeros_like(last_ref)
            da_ref[...] = jnp.zeros_like(da_ref)

        sv = s_ref[...]
        rolled = pltpu.roll(sv, 1, 0)
        row_id = lax.broadcasted_iota(jnp.int32, sv.shape, 0)
        sp = jnp.where(row_id == 0, jnp.broadcast_to(last_ref[0:1, :], sv.shape), rolled)
        last_ref[0:1, :] = sv[tt - 1:tt, :]
        lv = lam_ref[...]
        lr, li = lv[:, :cb], lv[:, cb:]
        pr, pi = sp[:, :cb], sp[:, cb:]
        dar = jnp.sum(lr * pr + li * pi, axis=0, keepdims=True)
        dai = jnp.sum(li * pr - lr * pi, axis=0, keepdims=True)
        da_ref[...] += jnp.concatenate([dar, dai], axis=1)

    blk = pl.BlockSpec((tt, 2 * cb), lambda j, kk: (kk, j))
    return pl.pallas_call(
        body, name=name, grid=(width // (2 * cb), rows // tt),
        in_specs=[blk, blk],
        out_specs=pl.BlockSpec((1, 2 * cb), lambda j, kk: (0, j)),
        out_shape=jax.ShapeDtypeStruct((1, width), F32),
        scratch_shapes=[pltpu.VMEM((8, 2 * cb), F32)],
        compiler_params=_params(("parallel", "arbitrary")),
    )(lam, s)


def _glu_fwd(ys, u, dd, w_glu, b_glu, *, tm, name):
    rows, w = ys.shape

    def body(ys_ref, u_ref, dd_ref, w_ref, b_ref, y0_ref, t_ref, y2_ref):
        y0 = ys_ref[...] + dd_ref[...] * u_ref[...]
        y1 = _gelu(y0)
        t = _dot(y1.astype(BF16), w_ref[...], 1, 0) + b_ref[...]
        y0_ref[...] = y0
        t_ref[...] = t
        y2_ref[...] = (y1 * _sigmoid(t)).astype(BF16)

    row = pl.BlockSpec((tm, w), lambda i: (i, 0))
    vec = pl.BlockSpec((1, w), lambda i: (0, 0))
    return pl.pallas_call(
        body, name=name, grid=(rows // tm,),
        in_specs=[row, row, vec, pl.BlockSpec((w, w), lambda i: (0, 0)), vec],
        out_specs=[row, row, row],
        out_shape=[jax.ShapeDtypeStruct((rows, w), F32), jax.ShapeDtypeStruct((rows, w), F32),
                   jax.ShapeDtypeStruct((rows, w), BF16)],
        compiler_params=_params(("parallel",)),
    )(ys, u, dd, w_glu, b_glu)


def _glu_bwd(dy2, y0, t, u, w_glu, *, tm, name):
    rows, w = y0.shape

    def body(dy2_ref, y0_ref, t_ref, u_ref, w_ref, dy0_ref, dt_ref, y1_ref, db_ref, dd_ref):
        i = pl.program_id(0)
        y0 = y0_ref[...]
        y1 = _gelu(y0)
        sg = _sigmoid(t_ref[...])
        dy2v = dy2_ref[...]
        dt = dy2v * y1 * sg * (1.0 - sg)
        dy1 = dy2v * sg + _dot(dt.astype(BF16), w_ref[...], 1, 1)
        dy0 = dy1 * _gelu_grad(y0)
        dy0_ref[...] = dy0
        dt_ref[...] = dt.astype(BF16)
        y1_ref[...] = y1.astype(BF16)

        @pl.when(i == 0)
        def _():
            db_ref[...] = jnp.zeros_like(db_ref)
            dd_ref[...] = jnp.zeros_like(dd_ref)

        db_ref[...] += jnp.sum(dt, axis=0, keepdims=True)
        dd_ref[...] += jnp.sum(dy0 * u_ref[...], axis=0, keepdims=True)

    row = pl.BlockSpec((tm, w), lambda i: (i, 0))
    vec = pl.BlockSpec((1, w), lambda i: (0, 0))
    return pl.pallas_call(
        body, name=name, grid=(rows // tm,),
        in_specs=[row, row, row, row, pl.BlockSpec((w, w), lambda i: (0, 0))],
        out_specs=[row, row, row, vec, vec],
        out_shape=[jax.ShapeDtypeStruct((rows, w), F32), jax.ShapeDtypeStruct((rows, w), BF16),
                   jax.ShapeDtypeStruct((rows, w), BF16), jax.ShapeDtypeStruct((1, w), F32),
                   jax.ShapeDtypeStruct((1, w), F32)],
        compiler_params=_params(("arbitrary",)),
    )(dy2, y0, t, u, w_glu)


def _attn_masks(n):
    qi = lax.broadcasted_iota(jnp.int32, (ATTN_Q, ATTN_Q), 0)
    kj = lax.broadcasted_iota(jnp.int32, (ATTN_Q, ATTN_Q), 1)
    head0 = kj < ATTN_HEAD_DIM
    return kj <= qi, jnp.logical_and(kj >= qi, n > 0), head0


def _rows(r, d):
    return pl.ds(0, ATTN_Q) if d == 1 else pl.ds(r, ATTN_Q, stride=d)


def _attn_fwd(qkv, g, d, *, name):
    rows = qkv.shape[0]
    sb = ATTN_Q * d
    nsb = rows // sb
    qc, kc, vc = 2 * g, 6 + 2 * g, 12 + 2 * g
    scale = ATTN_HEAD_DIM ** -0.5

    def body(q_ref, kc_ref, kp_ref, vc_ref, vp_ref, o_ref, lse_ref):
        n = pl.program_id(0)
        mask_c, mask_p, head0 = _attn_masks(n)

        def per_residue(r, carry):
            idx = _rows(r, d)
            q = q_ref[idx, :]
            k_c = kc_ref[idx, :].astype(BF16)
            k_p = kp_ref[idx, :].astype(BF16)
            v_c = vc_ref[idx, :].astype(BF16)
            v_p = vp_ref[idx, :].astype(BF16)
            o_h, lse_h = [], []
            for h in range(2):
                hm = head0 if h == 0 else jnp.logical_not(head0)
                qh = jnp.where(hm, q, 0.0).astype(BF16)
                sc = jnp.where(mask_c, _dot(qh, k_c, 1, 1) * scale, NEG_INF)
                sp = jnp.where(mask_p, _dot(qh, k_p, 1, 1) * scale, NEG_INF)
                mx = jnp.maximum(jnp.max(sc, axis=-1, keepdims=True), jnp.max(sp, axis=-1, keepdims=True))
                pc = jnp.exp(sc - mx)
                pp = jnp.exp(sp - mx)
                den = jnp.sum(pc, axis=-1, keepdims=True) + jnp.sum(pp, axis=-1, keepdims=True)
                o_h.append((_dot(pc.astype(BF16), v_c, 1, 0) + _dot(pp.astype(BF16), v_p, 1, 0)) / den)
                lse_h.append(jnp.broadcast_to(mx + jnp.log(den), (ATTN_Q, LANES)))
            o_ref[idx, :] = jnp.where(head0, o_h[0], o_h[1])
            lse_ref[idx, :] = jnp.where(head0, lse_h[0], lse_h[1])
            return carry

        lax.fori_loop(0, d, per_residue, 0)

    def spec(col, prev):
        if prev:
            return pl.BlockSpec((sb, LANES), lambda n, hp: (jnp.maximum(n - 1, 0), col + hp))
        return pl.BlockSpec((sb, LANES), lambda n, hp: (n, col + hp))

    out_spec = pl.BlockSpec((sb, LANES), lambda n, hp: (n, hp))
    return pl.pallas_call(
        body, name=name, grid=(nsb, 2),
        in_specs=[spec(qc, False), spec(kc, False), spec(kc, True), spec(vc, False), spec(vc, True)],
        out_specs=[out_spec, out_spec],
        out_shape=[jax.ShapeDtypeStruct((rows, 2 * LANES), F32), jax.ShapeDtypeStruct((rows, 2 * LANES), F32)],
        compiler_params=_params(("parallel", "parallel")),
    )(qkv, qkv, qkv, qkv, qkv)


def _attn_merge(outs, lses, *, tm, name):
    rows, w = outs[0].shape

    def body(o0, o1, o2, l0, l1, l2, o_ref, lse_ref):
        a0, a1, a2 = l0[...], l1[...], l2[...]
        mx = jnp.maximum(jnp.maximum(a0, a1), a2)
        e0, e1, e2 = jnp.exp(a0 - mx), jnp.exp(a1 - mx), jnp.exp(a2 - mx)
        den = e0 + e1 + e2
        o_ref[...] = (e0 / den) * o0[...] + (e1 / den) * o1[...] + (e2 / den) * o2[...]
        lse_ref[...] = mx + jnp.log(den)

    row = pl.BlockSpec((tm, w), lambda i: (i, 0))
    return pl.pallas_call(
        body, name=name, grid=(rows // tm,), in_specs=[row] * 6, out_specs=[row, row],
        out_shape=[jax.ShapeDtypeStruct((rows, w), F32), jax.ShapeDtypeStruct((rows, w), F32)],
        compiler_params=_params(("parallel",)),
    )(*outs, *lses)


def _attn_bwd(qkv, do, o, lse, g, d, *, name):
    rows = qkv.shape[0]
    sb = ATTN_Q * d
    nsb = rows // sb
    qc, kc, vc = 2 * g, 6 + 2 * g, 12 + 2 * g
    scale = ATTN_HEAD_DIM ** -0.5

    def body(q_ref, kc_ref, kp_ref, vc_ref, vp_ref, do_ref, o_ref, lse_ref, dq_ref, dk_ref, dv_ref, ck_ref, cv_ref):
        n = pl.program_id(1)

        @pl.when(n == 0)
        def _():
            ck_ref[...] = jnp.zeros_like(ck_ref)
            cv_ref[...] = jnp.zeros_like(cv_ref)

        @pl.when(n < nsb)
        def _():
            mask_c, mask_p, head0 = _attn_masks(n)
            lane = lax.broadcasted_iota(jnp.int32, (ATTN_Q, LANES), 1)

            def per_residue(r, carry):
                idx = _rows(r, d)
                q = q_ref[idx, :]
                k_c = kc_ref[idx, :].astype(BF16)
                k_p = kp_ref[idx, :].astype(BF16)
                v_c = vc_ref[idx, :].astype(BF16)
                v_p = vp_ref[idx, :].astype(BF16)
                dov = do_ref[idx, :]
                ov = o_ref[idx, :]
                lsev = lse_ref[idx, :]
                dq = jnp.zeros((ATTN_Q, LANES), F32)
                dkc = jnp.zeros((ATTN_Q, LANES), F32)
                dkp = jnp.zeros((ATTN_Q, LANES), F32)
                dvc = jnp.zeros((ATTN_Q, LANES), F32)
                dvp = jnp.zeros((ATTN_Q, LANES), F32)
                for h in range(2):
                    hm = head0 if h == 0 else jnp.logical_not(head0)
                    qh = jnp.where(hm, q, 0.0).astype(BF16)
                    doh = jnp.where(hm, dov, 0.0)
                    dohb = doh.astype(BF16)
                    delta = jnp.sum(doh * ov, axis=-1, keepdims=True)
                    lse_h = jnp.sum(jnp.where(lane == h * ATTN_HEAD_DIM, lsev, 0.0), axis=-1, keepdims=True)
                    sc = jnp.where(mask_c, _dot(qh, k_c, 1, 1) * scale, NEG_INF)
                    sp = jnp.where(mask_p, _dot(qh, k_p, 1, 1) * scale, NEG_INF)
                    pc = jnp.exp(sc - lse_h)
                    pp = jnp.exp(sp - lse_h)
                    dsc = (pc * (_dot(dohb, v_c, 1, 1) - delta) * scale).astype(BF16)
                    dsp = (pp * (_dot(dohb, v_p, 1, 1) - delta) * scale).astype(BF16)
                    dq = dq + jnp.where(hm, _dot(dsc, k_c, 1, 0) + _dot(dsp, k_p, 1, 0), 0.0)
                    dkc = dkc + _dot(dsc, qh, 0, 0)
                    dkp = dkp + _dot(dsp, qh, 0, 0)
                    dvc = dvc + _dot(pc.astype(BF16), dohb, 0, 0)
                    dvp = dvp + _dot(pp.astype(BF16), dohb, 0, 0)
                dq_ref[idx, :] = dq
                dk_ref[idx, :] = ck_ref[idx, :] + dkp
                dv_ref[idx, :] = cv_ref[idx, :] + dvp
                ck_ref[idx, :] = dkc
                cv_ref[idx, :] = dvc
                return carry

            lax.fori_loop(0, d, per_residue, 0)

        @pl.when(n == nsb)
        def _():
            dk_ref[...] = ck_ref[...]
            dv_ref[...] = cv_ref[...]

    def cur(n):
        return jnp.minimum(n, nsb - 1)

    def spec(col, prev):
        if prev:
            return pl.BlockSpec((sb, LANES), lambda hp, n: (jnp.maximum(cur(n) - 1, 0), col + hp))
        return pl.BlockSpec((sb, LANES), lambda hp, n: (cur(n), col + hp))

    row_spec = pl.BlockSpec((sb, LANES), lambda hp, n: (cur(n), hp))
    kv_out = pl.BlockSpec((sb, LANES), lambda hp, n: (jnp.maximum(n - 1, 0), hp))
    shape = jax.ShapeDtypeStruct((rows, 2 * LANES), F32)
    return pl.pallas_call(
        body, name=name, grid=(2, nsb + 1),
        in_specs=[spec(qc, False), spec(kc, False), spec(kc, True), spec(vc, False), spec(vc, True),
                  row_spec, row_spec, row_spec],
        out_specs=[row_spec, kv_out, kv_out],
        out_shape=[shape, shape, shape],
        scratch_shapes=[pltpu.VMEM((sb, LANES), F32), pltpu.VMEM((sb, LANES), F32)],
        compiler_params=_params(("parallel", "arbitrary")),
    )(qkv, qkv, qkv, qkv, qkv, do, o, lse)


def _mem_probs(q, k):
    s = _dot(q.astype(BF16), k.astype(BF16), 1, 1) * (MEM_HEAD_DIM ** -0.5)
    e = jnp.exp(s - jnp.max(s, axis=-1, keepdims=True))
    return e / jnp.sum(e, axis=-1, keepdims=True)


def _mem_attn_fwd(mq, kv, *, tq, name):
    rows = mq.shape[0]

    def body(q_ref, k_ref, v_ref, o_ref):
        p = _mem_probs(q_ref[...], k_ref[...])
        o_ref[...] = _dot(p.astype(BF16), v_ref[...].astype(BF16), 1, 0)

    return pl.pallas_call(
        body, name=name, grid=(rows // tq, MEM_HEADS),
        in_specs=[pl.BlockSpec((tq, LANES), lambda i, h: (i, h)),
                  pl.BlockSpec((MEM_LEN, LANES), lambda i, h: (0, h)),
                  pl.BlockSpec((MEM_LEN, LANES), lambda i, h: (0, MEM_HEADS + h))],
        out_specs=pl.BlockSpec((tq, LANES), lambda i, h: (i, h)),
        out_shape=jax.ShapeDtypeStruct((rows, MEM_HEADS * LANES), F32),
        compiler_params=_params(("parallel", "parallel")),
    )(mq, kv, kv)


def _mem_attn_bwd(mq, kv, dmo, *, tq, name):
    rows = mq.shape[0]
    scale = MEM_HEAD_DIM ** -0.5

    def body(q_ref, k_ref, v_ref, do_ref, dq_ref, dk_ref, dv_ref):
        i = pl.program_id(1)
        qb = q_ref[...].astype(BF16)
        kb = k_ref[...].astype(BF16)
        vb = v_ref[...].astype(BF16)
        dob = do_ref[...].astype(BF16)
        p = _mem_probs(q_ref[...], k_ref[...])
        dp = _dot(dob, vb, 1, 1)
        ds = (p * (dp - jnp.sum(p * dp, axis=-1, keepdims=True)) * scale).astype(BF16)
        dq_ref[...] = _dot(ds, kb, 1, 0)

        @pl.when(i == 0)
        def _():
            dk_ref[...] = jnp.zeros_like(dk_ref)
            dv_ref[...] = jnp.zeros_like(dv_ref)

        dk_ref[...] += _dot(ds, qb, 0, 0)
        dv_ref[...] += _dot(p.astype(BF16), dob, 0, 0)

    kv_out = pl.BlockSpec((MEM_LEN, LANES), lambda h, i: (0, h))
    kv_shape = jax.ShapeDtypeStruct((MEM_LEN, MEM_HEADS * LANES), F32)
    return pl.pallas_call(
        body, name=name, grid=(MEM_HEADS, rows // tq),
        in_specs=[pl.BlockSpec((tq, LANES), lambda h, i: (i, h)),
                  pl.BlockSpec((MEM_LEN, LANES), lambda h, i: (0, h)),
                  pl.BlockSpec((MEM_LEN, LANES), lambda h, i: (0, MEM_HEADS + h)),
                  pl.BlockSpec((tq, LANES), lambda h, i: (i, h))],
        out_specs=[pl.BlockSpec((tq, LANES), lambda h, i: (i, h)), kv_out, kv_out],
        out_shape=[jax.ShapeDtypeStruct((rows, MEM_HEADS * LANES), F32), kv_shape, kv_shape],
        compiler_params=_params(("parallel", "arbitrary")),
    )(mq, kv, kv, dmo)


def _gate_merge_fwd(zg, b_gate, br_s, br_a, br_m, *, tm, name):
    rows, d = br_s.shape

    def body(zg_ref, b_ref, s_ref, a_ref, m_ref, o_ref):
        gt = _sigmoid(zg_ref[...] + b_ref[...])
        o_ref[...] = (gt[:, :d] * s_ref[...] + gt[:, d:2 * d] * a_ref[...] + gt[:, 2 * d:] * m_ref[...]).astype(BF16)

    row = pl.BlockSpec((tm, d), lambda i: (i, 0))
    return pl.pallas_call(
        body, name=name, grid=(rows // tm,),
        in_specs=[pl.BlockSpec((tm, 3 * d), lambda i: (i, 0)), pl.BlockSpec((1, 3 * d), lambda i: (0, 0)), row, row, row],
        out_specs=row, out_shape=jax.ShapeDtypeStruct((rows, d), BF16),
        compiler_params=_params(("parallel",)),
    )(zg, b_gate, br_s, br_a, br_m)


def _gate_merge_bwd(dmerged, zg, b_gate, br_s, br_a, br_m, *, tm, name):
    rows, d = br_s.shape

    def body(dm_ref, zg_ref, b_ref, s_ref, a_ref, m_ref, ds_ref, da_ref, dmm_ref, dzg_ref, db_ref):
        i = pl.program_id(0)
        gt = _sigmoid(zg_ref[...] + b_ref[...])
        dm = dm_ref[...]
        parts = []
        for j, (br_ref, out_ref) in enumerate(((s_ref, ds_ref), (a_ref, da_ref), (m_ref, dmm_ref))):
            gj = gt[:, j * d:(j + 1) * d]
            out_ref[...] = (dm * gj).astype(BF16)
            parts.append(dm * br_ref[...] * gj * (1.0 - gj))
        dzg = jnp.concatenate(parts, axis=1)
        dzg_ref[...] = dzg

        @pl.when(i == 0)
        def _():
            db_ref[...] = jnp.zeros_like(db_ref)

        db_ref[...] += jnp.sum(dzg, axis=0, keepdims=True)

    row = pl.BlockSpec((tm, d), lambda i: (i, 0))
    wide = pl.BlockSpec((tm, 3 * d), lambda i: (i, 0))
    vec = pl.BlockSpec((1, 3 * d), lambda i: (0, 0))
    bshape = jax.ShapeDtypeStruct((rows, d), BF16)
    return pl.pallas_call(
        body, name=name, grid=(rows // tm,),
        in_specs=[row, wide, vec, row, row, row],
        out_specs=[row, row, row, wide, vec],
        out_shape=[bshape, bshape, bshape, jax.ShapeDtypeStruct((rows, 3 * d), F32), jax.ShapeDtypeStruct((1, 3 * d), F32)],
        compiler_params=_params(("arbitrary",)),
    )(dmerged, zg, b_gate, br_s, br_a, br_m)


def _adamw(w, g, m, v, *, tr, name):
    rows, cols = w.shape
    assert rows % tr == 0, (name, rows, tr)

    def body(w_ref, g_ref, m_ref, v_ref, d_ref, nm_ref, nv_ref):
        gv = g_ref[...]
        m2 = ADAM_B1 * m_ref[...] + (1.0 - ADAM_B1) * gv
        v2 = ADAM_B2 * v_ref[...] + (1.0 - ADAM_B2) * (gv * gv)
        m_hat = m2 / (1.0 - ADAM_B1 ** ADAM_STEP)
        v_hat = v2 / (1.0 - ADAM_B2 ** ADAM_STEP)
        d_ref[...] = -ADAM_LR * (m_hat / (jnp.sqrt(v_hat) + ADAM_EPS) + ADAM_WD * w_ref[...])
        nm_ref[...] = m2
        nv_ref[...] = v2

    blk = pl.BlockSpec((tr, cols), lambda i: (i, 0))
    shape = jax.ShapeDtypeStruct((rows, cols), F32)
    return pl.pallas_call(
        body, name=name, grid=(rows // tr,), in_specs=[blk] * 4, out_specs=[blk] * 3,
        out_shape=[shape, shape, shape], compiler_params=_params(("parallel",)),
    )(w, g, m, v)


ANY = pl.BlockSpec(memory_space=pl.ANY)


def _position():
    return lax.axis_index("x"), lax.axis_index("y"), lax.axis_index("c")


def _other_chips(x, y):
    return ((1 - x, y), (x, 1 - y), (1 - x, 1 - y))


def _remote(src, dst, send_sem, recv_sem, dev):
    return pltpu.make_async_remote_copy(src_ref=src, dst_ref=dst, send_sem=send_sem, recv_sem=recv_sem,
                                        device_id=dev, device_id_type=MESH)


def _all_gather_weights(wp):
    rows = wp.shape[0]
    half = rows // 2

    def body(w_ref, out_ref, send_sems, recv_sems, local_sem):
        x, y, c = _position()
        chip = 2 * x + y
        mine = pl.ds(c * half, half)
        theirs = pl.ds((1 - c) * half, half)
        sibling = (x, y, 1 - c)
        chips = _other_chips(x, y)
        local = pltpu.make_async_copy(w_ref, out_ref.at[chip], local_sem)
        local.start()
        sends = []
        for j, (px, py) in enumerate(chips):
            cp = _remote(w_ref.at[mine], out_ref.at[chip, mine], send_sems.at[j], recv_sems.at[j], (px, py, c))
            cp.start()
            sends.append(cp)
        for j, (px, py) in enumerate(chips):
            landed = out_ref.at[2 * px + py, mine]
            _remote(landed, landed, send_sems.at[j], recv_sems.at[j], (px, py, c)).wait_recv()
            cp = _remote(landed, landed, send_sems.at[3 + j], recv_sems.at[3 + j], sibling)
            cp.start()
            sends.append(cp)
        for j, (px, py) in enumerate(chips):
            dst = out_ref.at[2 * px + py, theirs]
            _remote(dst, dst, send_sems.at[3 + j], recv_sems.at[3 + j], sibling).wait_recv()
        for cp in sends:
            cp.wait_send()
        local.wait()

    return pl.pallas_call(
        body, name="all_gather_weights", in_specs=[ANY], out_specs=ANY,
        out_shape=jax.ShapeDtypeStruct((N_CHIPS, rows, 1024), wp.dtype),
        scratch_shapes=[pltpu.SemaphoreType.DMA((6,)), pltpu.SemaphoreType.DMA((6,)), pltpu.SemaphoreType.DMA(())],
    )(wp)


def _exchange_halves(g4):
    half = g4.shape[2]

    def body(g_ref, out_ref, send_sem, recv_sem):
        x, y, c = _position()
        cp = _remote(g_ref.at[:, 1 - c], out_ref, send_sem, recv_sem, (x, y, 1 - c))
        cp.start()
        cp.wait()

    return pl.pallas_call(
        body, name="grad_exchange_halves", in_specs=[ANY], out_specs=ANY,
        out_shape=jax.ShapeDtypeStruct((N_CHIPS, half, 1024), F32),
        scratch_shapes=[pltpu.SemaphoreType.DMA(()), pltpu.SemaphoreType.DMA(())],
    )(g4)


def _pair_sum(g4, got, c_arr, *, tr):
    half = g4.shape[2]

    def body(c_ref, g_ref, t_ref, p_ref, pb_ref):
        sm = g_ref[...] + t_ref[...]
        p_ref[...] = sm
        pb_ref[...] = sm.astype(BF16)

    grid_spec = pltpu.PrefetchScalarGridSpec(
        num_scalar_prefetch=1, grid=(N_CHIPS, half // tr),
        in_specs=[pl.BlockSpec((None, None, tr, 1024), lambda j, i, c_ref: (j, c_ref[0], i, 0)),
                  pl.BlockSpec((None, tr, 1024), lambda j, i, c_ref: (j, i, 0))],
        out_specs=[pl.BlockSpec((None, tr, 1024), lambda j, i, c_ref: (j, i, 0)),
                   pl.BlockSpec((None, tr, 1024), lambda j, i, c_ref: (j, i, 0))])
    return pl.pallas_call(
        body, name="grad_pair_sum", grid_spec=grid_spec,
        out_shape=[jax.ShapeDtypeStruct((N_CHIPS, half, 1024), F32), jax.ShapeDtypeStruct((N_CHIPS, half, 1024), BF16)],
        compiler_params=_params(("parallel", "parallel")),
    )(c_arr, g4, got)


def _scatter_to_owners(pb):
    half = pb.shape[1]

    def body(p_ref, out_ref, send_sems, recv_sems):
        x, y, c = _position()
        sends = []
        for j, (px, py) in enumerate(_other_chips(x, y)):
            cp = _remote(p_ref.at[2 * px + py], out_ref.at[j], send_sems.at[j], recv_sems.at[j], (px, py, c))
            cp.start()
            sends.append(cp)
        for cp in sends:
            cp.wait()

    return pl.pallas_call(
        body, name="grad_scatter_to_owners", in_specs=[ANY], out_specs=ANY,
        out_shape=jax.ShapeDtypeStruct((3, half, 1024), pb.dtype),
        scratch_shapes=[pltpu.SemaphoreType.DMA((3,)), pltpu.SemaphoreType.DMA((3,))],
    )(pb)


def _owner_sum(p, got, chip_arr, *, tr):
    half = p.shape[1]

    def body(chip_ref, p_ref, r_ref, o_ref):
        o_ref[...] = ((p_ref[...] + r_ref[0].astype(F32)) + r_ref[1].astype(F32)) + r_ref[2].astype(F32)

    grid_spec = pltpu.PrefetchScalarGridSpec(
        num_scalar_prefetch=1, grid=(half // tr,),
        in_specs=[pl.BlockSpec((None, tr, 1024), lambda i, chip_ref: (chip_ref[0], i, 0)),
                  pl.BlockSpec((3, tr, 1024), lambda i, chip_ref: (0, i, 0))],
        out_specs=pl.BlockSpec((tr, 1024), lambda i, chip_ref: (i, 0)))
    return pl.pallas_call(
        body, name="grad_owner_sum", grid_spec=grid_spec,
        out_shape=jax.ShapeDtypeStruct((half, 1024), F32),
        compiler_params=_params(("parallel",)),
    )(chip_arr, p, got)


def _share_reduced(tot):
    half = tot.shape[0]
    tail = pl.ds(half - SMALL_ROWS, SMALL_ROWS)

    def body(t_ref, full_ref, small_ref, send_sems, recv_sems, local_sems):
        x, y, c = _position()
        chip = 2 * x + y
        sibling = (x, y, 1 - c)
        own = pltpu.make_async_copy(t_ref, full_ref.at[c], local_sems.at[0])
        own.start()
        to_sib = _remote(t_ref, full_ref.at[c], send_sems.at[0], recv_sems.at[0], sibling)
        to_sib.start()
        _remote(t_ref, full_ref.at[1 - c], send_sems.at[0], recv_sems.at[0], sibling).wait_recv()
        own.wait()
        mine = full_ref.at[1, tail]
        keep = pltpu.make_async_copy(mine, small_ref.at[chip], local_sems.at[1])
        keep.start()
        sends = []
        for j, (px, py) in enumerate(_other_chips(x, y)):
            cp = _remote(mine, small_ref.at[chip], send_sems.at[1 + j], recv_sems.at[1 + j], (px, py, c))
            cp.start()
            sends.append(cp)
        for j, (px, py) in enumerate(_other_chips(x, y)):
            dst = small_ref.at[2 * px + py]
            _remote(dst, dst, send_sems.at[1 + j], recv_sems.at[1 + j], (px, py, c)).wait_recv()
        for cp in sends:
            cp.wait_send()
        to_sib.wait_send()
        keep.wait()

    return pl.pallas_call(
        body, name="grad_share_reduced", in_specs=[ANY], out_specs=[ANY, ANY],
        out_shape=[jax.ShapeDtypeStruct((2, half, 1024), F32), jax.ShapeDtypeStruct((N_CHIPS, SMALL_ROWS, 1024), F32)],
        scratch_shapes=[pltpu.SemaphoreType.DMA((4,)), pltpu.SemaphoreType.DMA((4,)), pltpu.SemaphoreType.DMA((2,))],
    )(tot)


def _pack_shards_bf16(shards):
    flat = jnp.concatenate([shards[name].astype(BF16).reshape(-1) for name, _, _ in BIG])
    return flat.reshape(BIG_ROWS, 1024)


def _unpack_gathered(buf):
    flat = buf.reshape(N_CHIPS, BIG_SHARD_ELEMS)
    out, off = {}, 0
    for name, axis, (r, c) in BIG:
        n = r * c // N_CHIPS
        piece = flat[:, off:off + n]
        off += n
        if axis == 0:
            out[name] = piece.reshape(r, c)
        else:
            out[name] = piece.reshape(N_CHIPS, r, c // N_CHIPS).transpose(1, 0, 2).reshape(r, c)
    return out


def _pack_grads(big, small):
    per_chip = []
    small_flat = jnp.concatenate([small[name].reshape(-1) for name, _ in SMALL])
    small_flat = jnp.pad(small_flat, (0, N_CHIPS * SMALL_ROWS * 1024 - SMALL_ELEMS)).reshape(N_CHIPS, SMALL_ROWS * 1024)
    for j in range(N_CHIPS):
        parts = []
        for name, axis, (r, c) in BIG:
            gfull = big[name]
            if axis == 0:
                parts.append(gfull[j * (r // N_CHIPS):(j + 1) * (r // N_CHIPS), :].reshape(-1))
            else:
                parts.append(gfull[:, j * (c // N_CHIPS):(j + 1) * (c // N_CHIPS)].reshape(-1))
        parts.append(small_flat[j])
        per_chip.append(jnp.concatenate(parts))
    return jnp.stack(per_chip).reshape(N_CHIPS, 2, HALF_ROWS, 1024)


def _unpack_shard_grads(full):
    flat = full.reshape(-1)
    out, off = {}, 0
    for name, axis, (r, c) in BIG:
        n = r * c // N_CHIPS
        shape = (r // N_CHIPS, c) if axis == 0 else (r, c // N_CHIPS)
        out[name] = flat[off:off + n].reshape(shape)
        off += n
    return out


def _pack_small(vals):
    flat = jnp.concatenate([vals[name].reshape(-1) for name, _ in SMALL])
    return jnp.pad(flat, (0, N_CHIPS * SMALL_ROWS * 1024 - SMALL_ELEMS)).reshape(N_CHIPS * SMALL_ROWS, 1024)


def _unpack_small(buf):
    flat = buf.reshape(-1)
    out, off = {}, 0
    for name, shape in SMALL:
        n = int(np.prod(shape))
        out[name] = flat[off:off + n].reshape(shape)
        off += n
    return out


def _device_step(x, mem, tgt, w, p):
    rows = x.shape[0]
    g1, gm, g2 = p["norm1_g"], p["mem_norm_g"], p["norm2_g"]
    gf = p["final_g"].reshape(1, D_MODEL)
    ssm_args = (p["ssm_lambda_re"][0], p["ssm_lambda_im"][0], p["ssm_log_dt"][0], p["ssm_b_re"][0],
                p["ssm_b_im"][0], p["ssm_c_re"][0], p["ssm_c_im"][0])
    (a_lay, b_lay, c_lay), ssm_vjp = jax.vjp(_ssm_matrices, *ssm_args)
    a_conj = a_lay * _to_scan_layout(jnp.stack([jnp.ones((N_STATES,), F32), -jnp.ones((N_STATES,), F32)]))[None, :]
    dd = p["ssm_d"].reshape(1, SSM_WIDTH)
    win = w["w_in"]
    mm = _matmul

    n1 = _rmsnorm_fwd(x, g1, tm=512, name="norm1")
    u = mm(n1, win, m=rows, n=512, k=1024, tm=1024, tn=512, tk=1024, out_dtypes=(F32,), name="in_u")
    qkv = mm(n1, win, m=rows, n=2304, k=1024, tm=1024, tn=256, tk=1024, b_off=(0, OFF_QKV // 256),
             out_dtypes=(F32,), name="in_qkv")
    mq = mm(n1, win, m=rows, n=512, k=1024, tm=1024, tn=256, tk=1024, b_off=(0, OFF_MQ // 256),
            out_dtypes=(F32,), name="in_mq")
    zg = mm(n1, win, m=rows, n=3072, k=1024, tm=1024, tn=256, tk=1024, b_off=(0, OFF_ZG // 256),
            out_dtypes=(F32,), name="in_zg")

    bu = mm(u, b_lay, m=rows, n=2 * N_STATES, k=512, tm=1024, tn=1024, tk=512, out_dtypes=(F32,), name="ssm_bu")
    s = _scan(a_lay, bu, reverse=False, tt=512, name="ssm_scan_fwd")
    ys = mm(s, c_lay, m=rows, n=512, k=2 * N_STATES, tm=1024, tn=512, tk=1024, out_dtypes=(F32,), name="ssm_cs")
    y0, tglu, y2 = _glu_fwd(ys, u, dd, w["w_glu"], p["b_glu"], tm=512, name="glu_fwd")
    br_s = mm(y2, w["w_ssm_br"], m=rows, n=1024, k=512, tm=1024, tn=512, tk=512, out_dtypes=(F32,), name="br_ssm")

    outs, lses = [], []
    for g, (_, d) in enumerate(ATTN_PATTERNS):
        o_g, lse_g = _attn_fwd(qkv, g, d, name=f"attn_fwd_{g}")
        outs.append(o_g)
        lses.append(lse_g)
    o, lse = _attn_merge(outs, lses, tm=1024, name="attn_merge")
    br_a = mm(o, w["w_attn_br"], m=rows, n=1024, k=256, tm=1024, tn=512, tk=256, out_dtypes=(F32,), name="br_attn")

    mn = _rmsnorm_fwd(mem, gm, tm=MEM_LEN, name="mem_norm")
    kv = mm(mn, w["w_mem_kv"], m=MEM_LEN, n=1024, k=1024, tm=MEM_LEN, tn=512, tk=1024, out_dtypes=(F32,), name="mem_kv")
    mo = _mem_attn_fwd(mq, kv, tq=1024, name="mem_attn_fwd")
    br_m = mm(mo, w["w_mem_br"], m=rows, n=1024, k=512, tm=1024, tn=512, tk=512, out_dtypes=(F32,), name="br_mem")

    merged = _gate_merge_fwd(zg, p["b_gate"], br_s, br_a, br_m, tm=256, name="gate_merge_fwd")
    add = lambda acc, r: (acc + r,)
    h1 = mm(merged, w["w_o"], m=rows, n=1024, k=1024, tm=1024, tn=512, tk=1024, out_dtypes=(F32,),
            aux=((x, "mn"),), epilogue=add, name="out_proj")
    n2 = _rmsnorm_fwd(h1, g2, tm=512, name="norm2")
    relu2 = lambda acc: (acc, jnp.square(jnp.maximum(acc, 0.0)))
    up, act = mm(n2, w["w_up"], m=rows, n=D_FF, k=1024, tm=1024, tn=512, tk=1024, out_dtypes=(F32, BF16),
                 epilogue=relu2, name="mlp_up")
    h2 = mm(act, w["w_down"], m=rows, n=1024, k=D_FF, tm=1024, tn=512, tk=1024, out_dtypes=(F32,),
            aux=((h1, "mn"),), epilogue=add, name="mlp_down")
    dh2, loss, d_gf = _loss_head(h2, tgt, gf, tm=512, name="loss_head")

    gb = {}
    gs = {"final_g": d_gf.reshape(D_MODEL)}
    drelu2 = lambda acc, upv: (acc * (2.0 * jnp.maximum(upv, 0.0)),)
    dup = mm(dh2, w["w_down"], m=rows, n=D_FF, k=1024, tb=True, tm=1024, tn=512, tk=1024, out_dtypes=(BF16,),
             aux=((up, "mn"),), epilogue=drelu2, name="d_act")
    gb["w_down"] = mm(act, dh2, m=D_FF, n=1024, k=rows, ta=True, tm=512, tn=512, tk=1024, out_dtypes=(F32,), name="dw_down")
    dn2 = mm(dup, w["w_up"], m=rows, n=1024, k=D_FF, tb=True, tm=1024, tn=512, tk=1024, out_dtypes=(F32,), name="d_n2")
    gb["w_up"] = mm(n2, dup, m=1024, n=D_FF, k=rows, ta=True, tm=512, tn=512, tk=1024, out_dtypes=(F32,), name="dw_up")
    dh1, gs["norm2_g"] = _rmsnorm_bwd(h1, g2, dn2, dh2, tm=512, name="norm2_bwd")
    dmerged = mm(dh1, w["w_o"], m=rows, n=1024, k=1024, tb=True, tm=1024, tn=512, tk=1024, out_dtypes=(F32,), name="d_merged")
    gb["w_o"] = mm(merged, dh1, m=1024, n=1024, k=rows, ta=True, tm=512, tn=512, tk=1024, out_dtypes=(F32,), name="dw_o")
    dbr_s, dbr_a, dbr_m, dzg, gs["b_gate"] = _gate_merge_bwd(dmerged, zg, p["b_gate"], br_s, br_a, br_m, tm=256,
                                                              name="gate_merge_bwd")

    dy2 = mm(dbr_s, w["w_ssm_br"], m=rows, n=512, k=1024, tb=True, tm=1024, tn=512, tk=1024, out_dtypes=(F32,), name="d_y2")
    gb["w_ssm_br"] = mm(y2, dbr_s, m=512, n=1024, k=rows, ta=True, tm=512, tn=512, tk=1024, out_dtypes=(F32,), name="dw_ssm_br")
    dy0, dt, y1, gs["b_glu"], d_dd = _glu_bwd(dy2, y0, tglu, u, w["w_glu"], tm=512, name="glu_bwd")
    gs["ssm_d"] = d_dd.reshape(1, SSM_GROUPS, SSM_GROUP_SIZE)
    gb["w_glu"] = mm(y1, dt, m=512, n=512, k=rows, ta=True, tm=512, tn=512, tk=1024, out_dtypes=(F32,), name="dw_glu")
    dsout = mm(dy0, c_lay, m=rows, n=2 * N_STATES, k=512, tb=True, tm=1024, tn=1024, tk=512, out_dtypes=(F32,), name="ssm_dsout")
    lam = _scan(a_conj, dsout, reverse=True, tt=512, name="ssm_scan_bwd")
    skip = lambda acc, dyv, ddv: (acc + ddv * dyv,)
    du = mm(lam, b_lay, m=rows, n=512, k=2 * N_STATES, tb=True, tm=1024, tn=512, tk=1024, out_dtypes=(F32,),
            aux=((dy0, "mn"), (dd, "row")), epilogue=skip, name="ssm_du")
    d_b_lay = mm(u, lam, m=512, n=2 * N_STATES, k=rows, ta=True, tm=512, tn=1024, tk=1024, out_dtypes=(F32,), name="ssm_db")
    d_c_lay = mm(s, dy0, m=2 * N_STATES, n=512, k=rows, ta=True, tm=1024, tn=512, tk=1024, out_dtypes=(F32,), name="ssm_dc")
    d_a_lay = _ssm_da(lam, s, tt=512, name="ssm_da")
    d_ssm = ssm_vjp((d_a_lay, d_b_lay, d_c_lay))
    for name, val in zip(("ssm_lambda_re", "ssm_lambda_im", "ssm_log_dt", "ssm_b_re", "ssm_b_im", "ssm_c_re", "ssm_c_im"), d_ssm):
        gs[name] = val[None]

    do = mm(dbr_a, w["w_attn_br"], m=rows, n=256, k=1024, tb=True, tm=1024, tn=256, tk=1024, out_dtypes=(F32,), name="d_o")
    gb["w_attn_br"] = mm(o, dbr_a, m=256, n=1024, k=rows, ta=True, tm=256, tn=512, tk=1024, out_dtypes=(F32,), name="dw_attn_br")
    dqs, dks, dvs = [], [], []
    for g, (_, d) in enumerate(ATTN_PATTERNS):
        dq_g, dk_g, dv_g = _attn_bwd(qkv, do, o, lse, g, d, name=f"attn_bwd_{g}")
        dqs.append(dq_g)
        dks.append(dk_g)
        dvs.append(dv_g)
    dqkv = jnp.concatenate(dqs + dks + dvs, axis=1)

    dmo = mm(dbr_m, w["w_mem_br"], m=rows, n=512, k=1024, tb=True, tm=1024, tn=512, tk=1024, out_dtypes=(F32,), name="d_mo")
    gb["w_mem_br"] = mm(mo, dbr_m, m=512, n=1024, k=rows, ta=True, tm=512, tn=512, tk=1024, out_dtypes=(F32,), name="dw_mem_br")
    dmq, dmk, dmv = _mem_attn_bwd(mq, kv, dmo, tq=1024, name="mem_attn_bwd")
    dkv = jnp.concatenate([dmk, dmv], axis=1)
    gb["w_mem_kv"] = mm(mn, dkv, m=1024, n=1024, k=MEM_LEN, ta=True, tm=512, tn=512, tk=MEM_LEN, out_dtypes=(F32,), name="dw_mem_kv")
    dmn = mm(dkv, w["w_mem_kv"], m=MEM_LEN, n=1024, k=1024, tb=True, tm=MEM_LEN, tn=512, tk=1024, out_dtypes=(F32,), name="d_mn")
    _, gs["mem_norm_g"] = _rmsnorm_bwd(mem, gm, dmn, None, tm=MEM_LEN, name="mem_norm_bwd")

    dn = mm(du, win, m=rows, n=1024, k=512, tb=True, tm=1024, tn=512, tk=512, out_dtypes=(F32,), name="d_n1_u")
    dn = mm(dqkv, win, m=rows, n=1024, k=2304, tb=True, tm=1024, tn=512, tk=256, b_off=(0, OFF_QKV // 256),
            out_dtypes=(F32,), aux=((dn, "mn"),), epilogue=add, name="d_n1_qkv")
    dn = mm(dmq, win, m=rows, n=1024, k=512, tb=True, tm=1024, tn=512, tk=256, b_off=(0, OFF_MQ // 256),
            out_dtypes=(F32,), aux=((dn, "mn"),), epilogue=add, name="d_n1_mq")
    dn = mm(dzg, win, m=rows, n=1024, k=3072, tb=True, tm=1024, tn=512, tk=256, b_off=(0, OFF_ZG // 256),
            out_dtypes=(F32,), aux=((dn, "mn"),), epilogue=add, name="d_n1_zg")
    dw_parts = [
        mm(n1, du, m=1024, n=512, k=rows, ta=True, tm=512, tn=512, tk=1024, out_dtypes=(F32,), name="dw_in_u"),
        mm(n1, dqkv, m=1024, n=2304, k=rows, ta=True, tm=512, tn=256, tk=1024, out_dtypes=(F32,), name="dw_in_qkv"),
        mm(n1, dmq, m=1024, n=512, k=rows, ta=True, tm=512, tn=512, tk=1024, out_dtypes=(F32,), name="dw_in_mq"),
        mm(n1, dzg, m=1024, n=3072, k=rows, ta=True, tm=512, tn=512, tk=1024, out_dtypes=(F32,), name="dw_in_zg"),
    ]
    gb["w_in"] = jnp.concatenate(dw_parts, axis=1)
    dx, gs["norm1_g"] = _rmsnorm_bwd(x, g1, dn, dh1, tm=512, name="norm1_bwd")
    return loss, dx, gb, gs


def kernel(x, mem, norm1_g, mem_norm_g, w_in, b_gate, ssm_lambda_re, ssm_lambda_im, ssm_log_dt, ssm_b_re, ssm_b_im, ssm_c_re, ssm_c_im, ssm_d, w_glu, b_glu, w_ssm_br, w_attn_br, w_mem_kv, w_mem_br, w_o, norm2_g, w_up, w_down, final_g, loss_target, m_norm1_g, m_mem_norm_g, m_w_in, m_b_gate, m_ssm_lambda_re, m_ssm_lambda_im, m_ssm_log_dt, m_ssm_b_re, m_ssm_b_im, m_ssm_c_re, m_ssm_c_im, m_ssm_d, m_w_glu, m_b_glu, m_w_ssm_br, m_w_attn_br, m_w_mem_kv, m_w_mem_br, m_w_o, m_norm2_g, m_w_up, m_w_down, m_final_g, v_norm1_g, v_mem_norm_g, v_w_in, v_b_gate, v_ssm_lambda_re, v_ssm_lambda_im, v_ssm_log_dt, v_ssm_b_re, v_ssm_b_im, v_ssm_c_re, v_ssm_c_im, v_ssm_d, v_w_glu, v_b_glu, v_w_ssm_br, v_w_attn_br, v_w_mem_kv, v_w_mem_br, v_w_o, v_norm2_g, v_w_up, v_w_down, v_final_g):
    env = dict(locals())
    weights = {n: env[n] for n in WEIGHT_ORDER}
    moms = {n: env["m_" + n] for n in WEIGHT_ORDER}
    vels = {n: env["v_" + n] for n in WEIGHT_ORDER}
    big_names = tuple(n for n, _, _ in BIG)

    def shard2d(a):
        return a.reshape(a.shape[-2], a.shape[-1])

    gathered = _all_gather_weights(_pack_shards_bf16({n: shard2d(weights[n]) for n in big_names}))
    w_full = _unpack_gathered(gathered)
    small = {n: weights[n] for n, _ in SMALL}

    loss, dx, gb, gs = _device_step(x[0], mem[0], loss_target[0], w_full, small)

    c_arr = lax.axis_index("c").astype(jnp.int32).reshape(1)
    chip_arr = (2 * lax.axis_index("x") + lax.axis_index("y")).astype(jnp.int32).reshape(1)
    g4 = _pack_grads(gb, gs)
    from_sibling = _exchange_halves(g4)
    pair, pair_bf16 = _pair_sum(g4, from_sibling, c_arr, tr=256)
    landed = _scatter_to_owners(pair_bf16)
    total_half = _owner_sum(pair, landed, chip_arr, tr=256)
    shard_grad, small_grad = _share_reduced(total_half)
    grads = _unpack_shard_grads(shard_grad)
    grads_small = _unpack_small(small_grad)

    delta, new_m, new_v = {}, {}, {}
    for n in big_names:
        shape = weights[n].shape
        r = shape[-2]
        dn_, nm_, nv_ = _adamw(shard2d(weights[n]), grads[n], shard2d(moms[n]), shard2d(vels[n]),
                               tr=min(r, 256), name="adamw_" + n)
        delta[n], new_m[n], new_v[n] = dn_.reshape(shape), nm_.reshape(shape), nv_.reshape(shape)
        grads[n] = grads[n].reshape(shape)
    ds_, ms_, vs_ = _adamw(_pack_small(small), small_grad.reshape(N_CHIPS * SMALL_ROWS, 1024),
                           _pack_small({n: moms[n] for n, _ in SMALL}), _pack_small({n: vels[n] for n, _ in SMALL}),
                           tr=N_CHIPS * SMALL_ROWS, name="adamw_small")
    for dst, buf in ((delta, ds_), (new_m, ms_), (new_v, vs_)):
        dst.update(_unpack_small(buf))
    grads.update(grads_small)

    total_loss = lax.psum(loss[0, 0], ("x", "y", "c"))
    return (total_loss, dx[None], *[grads[n] for n in WEIGHT_ORDER], *[delta[n] for n in WEIGHT_ORDER],
            *[new_m[n] for n in WEIGHT_ORDER], *[new_v[n] for n in WEIGHT_ORDER])
```

```python
import functools
import math

import numpy as np
import jax
import jax.numpy as jnp
from jax import lax
from jax.experimental import pallas as pl
from jax.experimental.pallas import tpu as pltpu

F32 = jnp.float32
BF16 = jnp.bfloat16

D_MODEL = 1024
SSM_GROUPS = 32
SSM_GROUP_SIZE = 16
SSM_STATE = 64
SSM_WIDTH = 512
N_STATES = SSM_GROUPS * SSM_STATE
SCAN_CB = 512
ATTN_PATTERNS = ((128, 1), (512, 4), (2048, 16))
ATTN_HEAD_DIM = 64
ATTN_Q = 128
MEM_LEN = 256
MEM_HEAD_DIM = 128
MEM_HEADS = 4
D_FF = 4096
OFF_U, OFF_QKV, OFF_MQ, OFF_ZG = 0, 512, 2816, 3328
IN_WIDTH = 6400
RMS_EPS = 1e-6
NEG_INF = -1e30
ADAM_LR, ADAM_B1, ADAM_B2, ADAM_EPS, ADAM_WD, ADAM_STEP = 0.001, 0.9, 0.999, 1e-08, 0.01, 10

VMEM_LIMIT_BYTES = 48 * 1024 * 1024
LANES = 128
MESH = pl.DeviceIdType.MESH
N_CHIPS = 4

SCAN_SEGS = 8

BIG = (("w_in", True, (6400, 1024)), ("w_glu", False, (512, 512)), ("w_ssm_br", True, (1024, 512)),
       ("w_attn_br", True, (1024, 256)), ("w_mem_kv", False, (1024, 1024)), ("w_mem_br", True, (1024, 512)),
       ("w_o", False, (1024, 1024)), ("w_up", True, (4096, 1024)), ("w_down", False, (4096, 1024)))
SMALL = (("norm1_g", (1, 1024)), ("mem_norm_g", (1, 1024)), ("b_gate", (1, 3072)),
         ("ssm_lambda_re", (1, 32, 64)), ("ssm_lambda_im", (1, 32, 64)), ("ssm_log_dt", (1, 32)),
         ("ssm_b_re", (1, 32, 64, 16)), ("ssm_b_im", (1, 32, 64, 16)), ("ssm_c_re", (1, 32, 16, 64)),
         ("ssm_c_im", (1, 32, 16, 64)), ("ssm_d", (1, 32, 16)), ("b_glu", (1, 512)),
         ("norm2_g", (1, 1024)), ("final_g", (1024,)))
WEIGHT_ORDER = ("norm1_g", "mem_norm_g", "w_in", "b_gate", "ssm_lambda_re", "ssm_lambda_im", "ssm_log_dt",
                "ssm_b_re", "ssm_b_im", "ssm_c_re", "ssm_c_im", "ssm_d", "w_glu", "b_glu", "w_ssm_br",
                "w_attn_br", "w_mem_kv", "w_mem_br", "w_o", "norm2_g", "w_up", "w_down", "final_g")
SMALL_ELEMS = sum(int(np.prod(s)) for _, s in SMALL)
SMALL_ROWS = 64


def _params(sem):
    return pltpu.CompilerParams(dimension_semantics=sem, vmem_limit_bytes=VMEM_LIMIT_BYTES)


def _sigmoid(v):
    return 1.0 / (1.0 + jnp.exp(-v))


_GELU_C = math.sqrt(2.0 / math.pi)


def _gelu(v):
    return 0.5 * v * (1.0 + jnp.tanh(_GELU_C * (v + 0.044715 * v * v * v)))


def _gelu_grad(v):
    th = jnp.tanh(_GELU_C * (v + 0.044715 * v * v * v))
    return 0.5 * (1.0 + th) + 0.5 * v * (1.0 - th * th) * _GELU_C * (1.0 + 3.0 * 0.044715 * v * v)


def _dot(a, b, ca, cb):
    return lax.dot_general(a, b, (((ca,), (cb,)), ((), ())), preferred_element_type=F32)


def _matmul(a, b, *, m, n, k, ta=False, tb=False, tm, tn, tk, out_dtypes, name,
            a_off=(0, 0), b_off=(0, 0), aux=(), epilogue=None):
    assert m % tm == 0 and n % tn == 0 and k % tk == 0, (name, m, n, k, tm, tn, tk)
    nk = k // tk
    n_aux = len(aux)
    n_out = len(out_dtypes)
    ar, ac = a_off
    br, bc = b_off
    if ta:
        a_spec = pl.BlockSpec((tk, tm), lambda i, j, kk: (kk + ar, i + ac))
    else:
        a_spec = pl.BlockSpec((tm, tk), lambda i, j, kk: (i + ar, kk + ac))
    if tb:
        b_spec = pl.BlockSpec((tn, tk), lambda i, j, kk: (j + br, kk + bc))
    else:
        b_spec = pl.BlockSpec((tk, tn), lambda i, j, kk: (kk + br, j + bc))
    aux_specs = []
    for _, kind in aux:
        if kind == "mn":
            aux_specs.append(pl.BlockSpec((tm, tn), lambda i, j, kk: (i, j)))
        else:
            aux_specs.append(pl.BlockSpec((1, tn), lambda i, j, kk: (0, j)))
    ca = 0 if ta else 1
    cb = 1 if tb else 0

    def finish(acc, aux_refs, out_refs):
        outs = (acc,) if epilogue is None else epilogue(acc, *[r[...] for r in aux_refs])
        for o_ref, o in zip(out_refs, outs):
            o_ref[...] = o.astype(o_ref.dtype)

    def body(a_ref, b_ref, *rest):
        aux_refs = rest[:n_aux]
        out_refs = rest[n_aux:n_aux + n_out]
        prod = _dot(a_ref[...].astype(BF16), b_ref[...].astype(BF16), ca, cb)
        if nk == 1:
            finish(prod, aux_refs, out_refs)
            return
        acc_ref = rest[n_aux + n_out]
        kk = pl.program_id(2)

        @pl.when(kk == 0)
        def _():
            acc_ref[...] = prod

        @pl.when(jnp.logical_and(kk > 0, kk < nk - 1))
        def _():
            acc_ref[...] += prod

        @pl.when(kk == nk - 1)
        def _():
            finish(acc_ref[...] + prod, aux_refs, out_refs)

    res = pl.pallas_call(
        body, name=name, grid=(m // tm, n // tn, nk),
        in_specs=[a_spec, b_spec] + aux_specs,
        out_specs=[pl.BlockSpec((tm, tn), lambda i, j, kk: (i, j)) for _ in range(n_out)],
        out_shape=[jax.ShapeDtypeStruct((m, n), dt) for dt in out_dtypes],
        scratch_shapes=[pltpu.VMEM((tm, tn), F32)] if nk > 1 else [],
        compiler_params=_params(("parallel", "parallel", "arbitrary")),
    )(a, b, *[x for x, _ in aux])
    return res[0] if n_out == 1 else tuple(res)


def _rmsnorm_fwd(x, g, *, tm, name):
    rows, d = x.shape

    def body(x_ref, g_ref, o_ref):
        xv = x_ref[...]
        r = lax.rsqrt(jnp.mean(xv * xv, axis=-1, keepdims=True) + RMS_EPS)
        o_ref[...] = (xv * r * g_ref[...]).astype(o_ref.dtype)

    return pl.pallas_call(
        body, name=name, grid=(rows // tm,),
        in_specs=[pl.BlockSpec((tm, d), lambda i: (i, 0)), pl.BlockSpec((1, d), lambda i: (0, 0))],
        out_specs=pl.BlockSpec((tm, d), lambda i: (i, 0)),
        out_shape=jax.ShapeDtypeStruct((rows, d), BF16),
        compiler_params=_params(("parallel",)),
    )(x, g)


def _rmsnorm_bwd(x, g, dy, res, *, tm, name):
    rows, d = x.shape
    has_res = res is not None

    def body(x_ref, g_ref, dy_ref, *rest):
        if has_res:
            res_ref, dx_ref, dg_ref = rest
        else:
            dx_ref, dg_ref = rest
        i = pl.program_id(0)
        xv = x_ref[...]
        r = lax.rsqrt(jnp.mean(xv * xv, axis=-1, keepdims=True) + RMS_EPS)
        xhat = xv * r
        dyv = dy_ref[...]
        dyg = dyv * g_ref[...]
        dx = r * (dyg - xhat * jnp.mean(dyg * xhat, axis=-1, keepdims=True))
        if has_res:
            dx = dx + res_ref[...]
        dx_ref[...] = dx

        @pl.when(i == 0)
        def _():
            dg_ref[...] = jnp.zeros_like(dg_ref)

        dg_ref[...] += jnp.sum(dyv * xhat, axis=0, keepdims=True)

    row_spec = pl.BlockSpec((tm, d), lambda i: (i, 0))
    vec_spec = pl.BlockSpec((1, d), lambda i: (0, 0))
    ins = [x, g, dy] + ([res] if has_res else [])
    return pl.pallas_call(
        body, name=name, grid=(rows // tm,),
        in_specs=[row_spec, vec_spec, row_spec] + ([row_spec] if has_res else []),
        out_specs=[row_spec, vec_spec],
        out_shape=[jax.ShapeDtypeStruct((rows, d), F32), jax.ShapeDtypeStruct((1, d), F32)],
        compiler_params=_params(("arbitrary",)),
    )(*ins)


def _loss_head(h, tgt, g, *, tm, name):
    rows, d = h.shape
    nsteps = rows // tm

    def body(h_ref, t_ref, g_ref, dh_ref, loss_ref, dg_ref, sq_ref):
        i = pl.program_id(0)
        xv = h_ref[...]
        gv = g_ref[...]
        r = lax.rsqrt(jnp.mean(xv * xv, axis=-1, keepdims=True) + RMS_EPS)
        xhat = xv * r
        err = xhat * gv - t_ref[...]
        dyv = err * (1.0 / d)
        dyg = dyv * gv
        dh_ref[...] = r * (dyg - xhat * jnp.mean(dyg * xhat, axis=-1, keepdims=True))

        @pl.when(i == 0)
        def _():
            dg_ref[...] = jnp.zeros_like(dg_ref)
            sq_ref[...] = jnp.zeros_like(sq_ref)

        dg_ref[...] += jnp.sum(dyv * xhat, axis=0, keepdims=True)
        sq_ref[...] += jnp.sum(err * err, axis=0, keepdims=True)

        @pl.when(i == nsteps - 1)
        def _():
            tot = jnp.sum(sq_ref[...], axis=-1, keepdims=True) * (0.5 / d)
            loss_ref[...] = jnp.broadcast_to(tot, loss_ref.shape)

    row_spec = pl.BlockSpec((tm, d), lambda i: (i, 0))
    vec_spec = pl.BlockSpec((1, d), lambda i: (0, 0))
    return pl.pallas_call(
        body, name=name, grid=(nsteps,),
        in_specs=[row_spec, row_spec, vec_spec],
        out_specs=[row_spec, pl.BlockSpec((1, LANES), lambda i: (0, 0)), vec_spec],
        out_shape=[jax.ShapeDtypeStruct((rows, d), F32), jax.ShapeDtypeStruct((1, LANES), F32),
                   jax.ShapeDtypeStruct((1, d), F32)],
        scratch_shapes=[pltpu.VMEM((1, d), F32)],
        compiler_params=_params(("arbitrary",)),
    )(h, tgt, g)


def _to_scan_layout(v):
    lead = v.shape[:-2]
    v = v.reshape(lead + (2, N_STATES // SCAN_CB, SCAN_CB))
    v = jnp.swapaxes(v, -3, -2)
    return v.reshape(lead + (2 * N_STATES,))


def _ssm_matrices(lam_re, lam_im, log_dt, b_re, b_im, c_re, c_im):
    dt = jnp.exp(log_dt)[:, None]
    mag = jnp.exp(lam_re * dt)
    a_re, a_im = mag * jnp.cos(lam_im * dt), mag * jnp.sin(lam_im * dt)
    nr, ni = a_re - 1.0, a_im
    den = lam_re * lam_re + lam_im * lam_im
    coef_re = (nr * lam_re + ni * lam_im) / den
    coef_im = (ni * lam_re - nr * lam_im) / den
    bb_re = coef_re[..., None] * b_re - coef_im[..., None] * b_im
    bb_im = coef_re[..., None] * b_im + coef_im[..., None] * b_re
    eye = jnp.eye(SSM_GROUPS, dtype=F32)
    a_lay = _to_scan_layout(jnp.stack([a_re.reshape(-1), a_im.reshape(-1)], axis=0))[None, :]

    def b_dense(bb):
        return jnp.einsum("gk,kph->ghkp", eye, bb).reshape(SSM_WIDTH, N_STATES)

    b_lay = _to_scan_layout(jnp.stack([b_dense(bb_re), b_dense(bb_im)], axis=1))

    def c_dense(cc):
        return jnp.einsum("gk,ghp->kpgh", eye, cc).reshape(N_STATES, SSM_WIDTH)

    c_lay = _to_scan_layout(jnp.stack([c_dense(c_re), -c_dense(c_im)], axis=0).transpose(2, 0, 1)).T
    return a_lay, b_lay, c_lay


def _interleave(v):
    rows, c = v.shape
    return v.reshape(SCAN_SEGS, rows // SCAN_SEGS, c).transpose(1, 0, 2).reshape(rows, c)


def _deinterleave(v):
    rows, c = v.shape
    return v.reshape(rows // SCAN_SEGS, SCAN_SEGS, c).transpose(1, 0, 2).reshape(rows, c)


def _scan_groups(a_ref, bu_ref, o_ref, state, *, reverse, tt):
    cb = SCAN_CB
    ar = jnp.broadcast_to(a_ref[:, :cb], (SCAN_SEGS, cb))
    ai = jnp.broadcast_to(a_ref[:, cb:], (SCAN_SEGS, cb))
    ngroups = tt // SCAN_SEGS

    def step(i, st):
        sr, si = st
        r0 = pl.multiple_of(((ngroups - 1 - i) if reverse else i) * SCAN_SEGS, SCAN_SEGS)
        blk = bu_ref[pl.ds(r0, SCAN_SEGS), :]
        nr = ar * sr - ai * si + blk[:, :cb]
        ni = ar * si + ai * sr + blk[:, cb:]
        if o_ref is not None:
            o_ref[pl.ds(r0, SCAN_SEGS), :] = jnp.concatenate([nr, ni], axis=1)
        return nr, ni

    return lax.fori_loop(0, ngroups, step, state, unroll=4)


def _scan_ends(a_lay, bu, *, reverse, tt, name):
    rows, width = bu.shape
    cb = SCAN_CB
    nt = rows // tt

    def body(a_ref, bu_ref, e_ref):
        kk = pl.program_id(1)

        @pl.when(kk == 0)
        def _():
            e_ref[...] = jnp.zeros_like(e_ref)

        sr, si = _scan_groups(a_ref, bu_ref, None, (e_ref[:, :cb], e_ref[:, cb:]), reverse=reverse, tt=tt)
        e_ref[...] = jnp.concatenate([sr, si], axis=1)

    tmap = (lambda j, kk: (nt - 1 - kk, j)) if reverse else (lambda j, kk: (kk, j))
    return pl.pallas_call(
        body, name=name, grid=(width // (2 * cb), nt),
        in_specs=[pl.BlockSpec((1, 2 * cb), lambda j, kk: (0, j)), pl.BlockSpec((tt, 2 * cb), tmap)],
        out_specs=pl.BlockSpec((SCAN_SEGS, 2 * cb), lambda j, kk: (0, j)),
        out_shape=jax.ShapeDtypeStruct((SCAN_SEGS, width), F32),
        compiler_params=_params(("parallel", "arbitrary")),
    )(a_lay, bu)


def _scan_apply(a_lay, bu, ends, *, reverse, tt, name):
    rows, width = bu.shape
    cb = SCAN_CB
    nt = rows // tt
    seg_len = rows // SCAN_SEGS
    n_sq = seg_len.bit_length() - 1
    assert 1 << n_sq == seg_len, seg_len

    def body(a_ref, e_ref, bu_ref, o_ref, init_ref, carry_ref):
        kk = pl.program_id(1)

        @pl.when(kk == 0)
        def _():
            pr, pi = a_ref[:, :cb], a_ref[:, cb:]
            for _ in range(n_sq):
                pr, pi = pr * pr - pi * pi, 2.0 * pr * pi
            cr = jnp.zeros((1, cb), F32)
            ci = jnp.zeros((1, cb), F32)
            order = range(SCAN_SEGS - 1, -1, -1) if reverse else range(SCAN_SEGS)
            for k, seg in enumerate(order):
                if k > 0:
                    prev = seg + 1 if reverse else seg - 1
                    er, ei = e_ref[prev:prev + 1, :cb], e_ref[prev:prev + 1, cb:]
                    cr, ci = pr * cr - pi * ci + er, pr * ci + pi * cr + ei
                init_ref[seg:seg + 1, :] = jnp.concatenate([cr, ci], axis=1)
            carry_ref[...] = init_ref[...]

        sr, si = _scan_groups(a_ref, bu_ref, o_ref, (carry_ref[:, :cb], carry_ref[:, cb:]), reverse=reverse, tt=tt)
        carry_ref[...] = jnp.concatenate([sr, si], axis=1)

    tmap = (lambda j, kk: (nt - 1 - kk, j)) if reverse else (lambda j, kk: (kk, j))
    seg_spec = pl.BlockSpec((SCAN_SEGS, 2 * cb), lambda j, kk: (0, j))
    return pl.pallas_call(
        body, name=name, grid=(width // (2 * cb), nt),
        in_specs=[pl.BlockSpec((1, 2 * cb), lambda j, kk: (0, j)), seg_spec, pl.BlockSpec((tt, 2 * cb), tmap)],
        out_specs=[pl.BlockSpec((tt, 2 * cb), tmap), seg_spec],
        out_shape=[jax.ShapeDtypeStruct((rows, width), F32), jax.ShapeDtypeStruct((SCAN_SEGS, width), F32)],
        scratch_shapes=[pltpu.VMEM((SCAN_SEGS, 2 * cb), F32)],
        compiler_params=_params(("parallel", "arbitrary")),
    )(a_lay, ends, bu)


def _ssm_da(lam, s, entry, *, tt, name):
    rows, width = s.shape
    cb = SCAN_CB
    ngroups = tt // SCAN_SEGS

    def body(lam_ref, s_ref, entry_ref, da_ref, prev_ref):
        kk = pl.program_id(1)

        @pl.when(kk == 0)
        def _():
            prev_ref[...] = entry_ref[...]
            da_ref[...] = jnp.zeros_like(da_ref)

        def step(i, st):
            accr, acci, pr, pi = st
            r0 = pl.multiple_of(i * SCAN_SEGS, SCAN_SEGS)
            lv = lam_ref[pl.ds(r0, SCAN_SEGS), :]
            sv = s_ref[pl.ds(r0, SCAN_SEGS), :]
            lr, li = lv[:, :cb], lv[:, cb:]
            return accr + (lr * pr + li * pi), acci + (li * pr - lr * pi), sv[:, :cb], sv[:, cb:]

        zero = jnp.zeros((SCAN_SEGS, cb), F32)
        accr, acci, pr, pi = lax.fori_loop(0, ngroups, step, (zero, zero, prev_ref[:, :cb], prev_ref[:, cb:]), unroll=4)
        prev_ref[...] = jnp.concatenate([pr, pi], axis=1)
        da_ref[...] += jnp.concatenate([jnp.sum(accr, axis=0, keepdims=True), jnp.sum(acci, axis=0, keepdims=True)], axis=1)

    blk = pl.BlockSpec((tt, 2 * cb), lambda j, kk: (kk, j))
    return pl.pallas_call(
        body, name=name, grid=(width // (2 * cb), rows // tt),
        in_specs=[blk, blk, pl.BlockSpec((SCAN_SEGS, 2 * cb), lambda j, kk: (0, j))],
        out_specs=pl.BlockSpec((1, 2 * cb), lambda j, kk: (0, j)),
        out_shape=jax.ShapeDtypeStruct((1, width), F32),
        scratch_shapes=[pltpu.VMEM((SCAN_SEGS, 2 * cb), F32)],
        compiler_params=_params(("parallel", "arbitrary")),
    )(lam, s, entry)


def _glu_fwd(ys, u, dd, w_glu, b_glu, *, tm, name):
    rows, w = ys.shape

    def body(ys_ref, u_ref, dd_ref, w_ref, b_ref, y0_ref, t_ref, y2_ref):
        y0 = ys_ref[...] + dd_ref[...] * u_ref[...]
        y1 = _gelu(y0)
        t = _dot(y1.astype(BF16), w_ref[...], 1, 0) + b_ref[...]
        y0_ref[...] = y0
        t_ref[...] = t
        y2_ref[...] = (y1 * _sigmoid(t)).astype(BF16)

    row = pl.BlockSpec((tm, w), lambda i: (i, 0))
    vec = pl.BlockSpec((1, w), lambda i: (0, 0))
    return pl.pallas_call(
        body, name=name, grid=(rows // tm,),
        in_specs=[row, row, vec, pl.BlockSpec((w, w), lambda i: (0, 0)), vec],
        out_specs=[row, row, row],
        out_shape=[jax.ShapeDtypeStruct((rows, w), F32), jax.ShapeDtypeStruct((rows, w), F32),
                   jax.ShapeDtypeStruct((rows, w), BF16)],
        compiler_params=_params(("parallel",)),
    )(ys, u, dd, w_glu, b_glu)


def _glu_bwd(dy2, y0, t, u, w_glu, *, tm, name):
    rows, w = y0.shape

    def body(dy2_ref, y0_ref, t_ref, u_ref, w_ref, dy0_ref, dt_ref, y1_ref, db_ref, dd_ref):
        i = pl.program_id(0)
        y0 = y0_ref[...]
        y1 = _gelu(y0)
        sg = _sigmoid(t_ref[...])
        dy2v = dy2_ref[...]
        dt = dy2v * y1 * sg * (1.0 - sg)
        dy1 = dy2v * sg + _dot(dt.astype(BF16), w_ref[...], 1, 1)
        dy0 = dy1 * _gelu_grad(y0)
        dy0_ref[...] = dy0
        dt_ref[...] = dt.astype(BF16)
        y1_ref[...] = y1.astype(BF16)

        @pl.when(i == 0)
        def _():
            db_ref[...] = jnp.zeros_like(db_ref)
            dd_ref[...] = jnp.zeros_like(dd_ref)

        db_ref[...] += jnp.sum(dt, axis=0, keepdims=True)
        dd_ref[...] += jnp.sum(dy0 * u_ref[...], axis=0, keepdims=True)

    row = pl.BlockSpec((tm, w), lambda i: (i, 0))
    vec = pl.BlockSpec((1, w), lambda i: (0, 0))
    return pl.pallas_call(
        body, name=name, grid=(rows // tm,),
        in_specs=[row, row, row, row, pl.BlockSpec((w, w), lambda i: (0, 0))],
        out_specs=[row, row, row, vec, vec],
        out_shape=[jax.ShapeDtypeStruct((rows, w), F32), jax.ShapeDtypeStruct((rows, w), BF16),
                   jax.ShapeDtypeStruct((rows, w), BF16), jax.ShapeDtypeStruct((1, w), F32),
                   jax.ShapeDtypeStruct((1, w), F32)],
        compiler_params=_params(("arbitrary",)),
    )(dy2, y0, t, u, w_glu)


def _attn_masks(n):
    qi = lax.broadcasted_iota(jnp.int32, (ATTN_Q, ATTN_Q), 0)
    kj = lax.broadcasted_iota(jnp.int32, (ATTN_Q, ATTN_Q), 1)
    head0 = kj < ATTN_HEAD_DIM
    return kj <= qi, jnp.logical_and(kj >= qi, n > 0), head0


def _rows(r, d):
    return pl.ds(0, ATTN_Q) if d == 1 else pl.ds(r, ATTN_Q, stride=d)


def _attn_fwd(qkv, g, d, *, name):
    rows = qkv.shape[0]
    sb = ATTN_Q * d
    nsb = rows // sb
    qc, kc, vc = 2 * g, 6 + 2 * g, 12 + 2 * g
    scale = ATTN_HEAD_DIM ** -0.5

    def body(q_ref, kc_ref, kp_ref, vc_ref, vp_ref, o_ref, lse_ref):
        n = pl.program_id(0)
        mask_c, mask_p, head0 = _attn_masks(n)

        def per_residue(r, carry):
            idx = _rows(r, d)
            q = q_ref[idx, :]
            k_c = kc_ref[idx, :].astype(BF16)
            k_p = kp_ref[idx, :].astype(BF16)
            v_c = vc_ref[idx, :].astype(BF16)
            v_p = vp_ref[idx, :].astype(BF16)
            o_h, lse_h = [], []
            for h in range(2):
                hm = head0 if h == 0 else jnp.logical_not(head0)
                qh = jnp.where(hm, q, 0.0).astype(BF16)
                sc = jnp.where(mask_c, _dot(qh, k_c, 1, 1) * scale, NEG_INF)
                sp = jnp.where(mask_p, _dot(qh, k_p, 1, 1) * scale, NEG_INF)
                mx = jnp.maximum(jnp.max(sc, axis=-1, keepdims=True), jnp.max(sp, axis=-1, keepdims=True))
                pc = jnp.exp(sc - mx)
                pp = jnp.exp(sp - mx)
                den = jnp.sum(pc, axis=-1, keepdims=True) + jnp.sum(pp, axis=-1, keepdims=True)
                o_h.append((_dot(pc.astype(BF16), v_c, 1, 0) + _dot(pp.astype(BF16), v_p, 1, 0)) / den)
                lse_h.append(jnp.broadcast_to(mx + jnp.log(den), (ATTN_Q, LANES)))
            o_ref[idx, :] = jnp.where(head0, o_h[0], o_h[1])
            lse_ref[idx, :] = jnp.where(head0, lse_h[0], lse_h[1])
            return carry

        lax.fori_loop(0, d, per_residue, 0)

    def spec(col, prev):
        if prev:
            return pl.BlockSpec((sb, LANES), lambda n, hp: (jnp.maximum(n - 1, 0), col + hp))
        return pl.BlockSpec((sb, LANES), lambda n, hp: (n, col + hp))

    out_spec = pl.BlockSpec((sb, LANES), lambda n, hp: (n, hp))
    return pl.pallas_call(
        body, name=name, grid=(nsb, 2),
        in_specs=[spec(qc, False), spec(kc, False), spec(kc, True), spec(vc, False), spec(vc, True)],
        out_specs=[out_spec, out_spec],
        out_shape=[jax.ShapeDtypeStruct((rows, 2 * LANES), F32), jax.ShapeDtypeStruct((rows, 2 * LANES), F32)],
        compiler_params=_params(("parallel", "parallel")),
    )(qkv, qkv, qkv, qkv, qkv)


def _attn_merge(outs, lses, *, tm, name):
    rows, w = outs[0].shape

    def body(o0, o1, o2, l0, l1, l2, o_ref, lse_ref):
        a0, a1, a2 = l0[...], l1[...], l2[...]
        mx = jnp.maximum(jnp.maximum(a0, a1), a2)
        e0, e1, e2 = jnp.exp(a0 - mx), jnp.exp(a1 - mx), jnp.exp(a2 - mx)
        den = e0 + e1 + e2
        o_ref[...] = (e0 / den) * o0[...] + (e1 / den) * o1[...] + (e2 / den) * o2[...]
        lse_ref[...] = mx + jnp.log(den)

    row = pl.BlockSpec((tm, w), lambda i: (i, 0))
    return pl.pallas_call(
        body, name=name, grid=(rows // tm,), in_specs=[row] * 6, out_specs=[row, row],
        out_shape=[jax.ShapeDtypeStruct((rows, w), F32), jax.ShapeDtypeStruct((rows, w), F32)],
        compiler_params=_params(("parallel",)),
    )(*outs, *lses)


def _attn_bwd(qkv, do, o, lse, g, d, prev, *, name):
    rows = qkv.shape[0]
    sb = ATTN_Q * d
    nsb = rows // sb
    qc, kc, vc = 2 * g, 6 + 2 * g, 12 + 2 * g
    scale = ATTN_HEAD_DIM ** -0.5

    def body(q_ref, kc_ref, kp_ref, vc_ref, vp_ref, do_ref, o_ref, lse_ref, dq_ref, dk_ref, dv_ref, ck_ref, cv_ref):
        n = pl.program_id(1)

        @pl.when(n == 0)
        def _():
            ck_ref[...] = jnp.zeros_like(ck_ref)
            cv_ref[...] = jnp.zeros_like(cv_ref)

        @pl.when(n < nsb)
        def _():
            mask_c, mask_p, head0 = _attn_masks(n)
            lane = lax.broadcasted_iota(jnp.int32, (ATTN_Q, LANES), 1)

            def per_residue(r, carry):
                idx = _rows(r, d)
                q = q_ref[idx, :]
                k_c = kc_ref[idx, :].astype(BF16)
                k_p = kp_ref[idx, :].astype(BF16)
                v_c = vc_ref[idx, :].astype(BF16)
                v_p = vp_ref[idx, :].astype(BF16)
                dov = do_ref[idx, :]
                ov = o_ref[idx, :]
                lsev = lse_ref[idx, :]
                dq = jnp.zeros((ATTN_Q, LANES), F32)
                dkc = jnp.zeros((ATTN_Q, LANES), F32)
                dkp = jnp.zeros((ATTN_Q, LANES), F32)
                dvc = jnp.zeros((ATTN_Q, LANES), F32)
                dvp = jnp.zeros((ATTN_Q, LANES), F32)
                for h in range(2):
                    hm = head0 if h == 0 else jnp.logical_not(head0)
                    qh = jnp.where(hm, q, 0.0).astype(BF16)
                    doh = jnp.where(hm, dov, 0.0)
                    dohb = doh.astype(BF16)
                    delta = jnp.sum(doh * ov, axis=-1, keepdims=True)
                    lse_h = jnp.sum(jnp.where(lane == h * ATTN_HEAD_DIM, lsev, 0.0), axis=-1, keepdims=True)
                    sc = jnp.where(mask_c, _dot(qh, k_c, 1, 1) * scale, NEG_INF)
                    sp = jnp.where(mask_p, _dot(qh, k_p, 1, 1) * scale, NEG_INF)
                    pc = jnp.exp(sc - lse_h)
                    pp = jnp.exp(sp - lse_h)
                    dsc = (pc * (_dot(dohb, v_c, 1, 1) - delta) * scale).astype(BF16)
                    dsp = (pp * (_dot(dohb, v_p, 1, 1) - delta) * scale).astype(BF16)
                    dq = dq + jnp.where(hm, _dot(dsc, k_c, 1, 0) + _dot(dsp, k_p, 1, 0), 0.0)
                    dkc = dkc + _dot(dsc, qh, 0, 0)
                    dkp = dkp + _dot(dsp, qh, 0, 0)
                    dvc = dvc + _dot(pc.astype(BF16), dohb, 0, 0)
                    dvp = dvp + _dot(pp.astype(BF16), dohb, 0, 0)
                dq_ref[idx, :] = dq
                dk_ref[idx, :] = ck_ref[idx, :] + dkp
                dv_ref[idx, :] = cv_ref[idx, :] + dvp
                ck_ref[idx, :] = dkc
                cv_ref[idx, :] = dvc
                return carry

            lax.fori_loop(0, d, per_residue, 0)

        @pl.when(n == nsb)
        def _():
            dk_ref[...] = ck_ref[...]
            dv_ref[...] = cv_ref[...]

    def cur(n):
        return jnp.minimum(n, nsb - 1)

    def spec(col, prev):
        if prev:
            return pl.BlockSpec((sb, LANES), lambda hp, n: (jnp.maximum(cur(n) - 1, 0), col + hp))
        return pl.BlockSpec((sb, LANES), lambda hp, n: (cur(n), col + hp))

    row_spec = pl.BlockSpec((sb, LANES), lambda hp, n: (cur(n), hp))
    dq_out = pl.BlockSpec((sb, LANES), lambda hp, n: (cur(n), 2 * g + hp))
    kv_out = pl.BlockSpec((sb, LANES), lambda hp, n: (jnp.maximum(n - 1, 0), 2 * g + hp))
    shape = jax.ShapeDtypeStruct((rows, len(ATTN_PATTERNS) * 2 * LANES), F32)
    ins = [qkv, qkv, qkv, qkv, qkv, do, o, lse]
    in_specs = [spec(qc, False), spec(kc, False), spec(kc, True), spec(vc, False), spec(vc, True),
                row_spec, row_spec, row_spec]
    aliases = {}
    if prev is not None:
        aliases = {len(ins) + t: t for t in range(3)}
        ins = ins + list(prev)
        in_specs = in_specs + [ANY] * 3
    n_in = len(ins)

    def entry(*refs):
        body(*refs[:8], *refs[n_in:])

    return pl.pallas_call(
        entry, name=name, grid=(2, nsb + 1),
        in_specs=in_specs,
        out_specs=[dq_out, kv_out, kv_out],
        out_shape=[shape, shape, shape],
        input_output_aliases=aliases,
        scratch_shapes=[pltpu.VMEM((sb, LANES), F32), pltpu.VMEM((sb, LANES), F32)],
        compiler_params=_params(("parallel", "arbitrary")),
    )(*ins)


def _mem_probs(q, k):
    s = _dot(q.astype(BF16), k.astype(BF16), 1, 1) * (MEM_HEAD_DIM ** -0.5)
    e = jnp.exp(s - jnp.max(s, axis=-1, keepdims=True))
    return e / jnp.sum(e, axis=-1, keepdims=True)


def _mem_attn_fwd(mq, kv, *, tq, name):
    rows = mq.shape[0]

    def body(q_ref, k_ref, v_ref, o_ref):
        p = _mem_probs(q_ref[...], k_ref[...])
        o_ref[...] = _dot(p.astype(BF16), v_ref[...].astype(BF16), 1, 0)

    return pl.pallas_call(
        body, name=name, grid=(rows // tq, MEM_HEADS),
        in_specs=[pl.BlockSpec((tq, LANES), lambda i, h: (i, h)),
                  pl.BlockSpec((MEM_LEN, LANES), lambda i, h: (0, h)),
                  pl.BlockSpec((MEM_LEN, LANES), lambda i, h: (0, MEM_HEADS + h))],
        out_specs=pl.BlockSpec((tq, LANES), lambda i, h: (i, h)),
        out_shape=jax.ShapeDtypeStruct((rows, MEM_HEADS * LANES), F32),
        compiler_params=_params(("parallel", "parallel")),
    )(mq, kv, kv)


def _mem_attn_bwd(mq, kv, dmo, *, tq, name):
    rows = mq.shape[0]
    scale = MEM_HEAD_DIM ** -0.5

    def body(q_ref, k_ref, v_ref, do_ref, dq_ref, dk_ref, dv_ref):
        i = pl.program_id(1)
        qb = q_ref[...].astype(BF16)
        kb = k_ref[...].astype(BF16)
        vb = v_ref[...].astype(BF16)
        dob = do_ref[...].astype(BF16)
        p = _mem_probs(q_ref[...], k_ref[...])
        dp = _dot(dob, vb, 1, 1)
        ds = (p * (dp - jnp.sum(p * dp, axis=-1, keepdims=True)) * scale).astype(BF16)
        dq_ref[...] = _dot(ds, kb, 1, 0)

        @pl.when(i == 0)
        def _():
            dk_ref[...] = jnp.zeros_like(dk_ref)
            dv_ref[...] = jnp.zeros_like(dv_ref)

        dk_ref[...] += _dot(ds, qb, 0, 0)
        dv_ref[...] += _dot(p.astype(BF16), dob, 0, 0)

    kv_out = pl.BlockSpec((MEM_LEN, LANES), lambda h, i: (0, h))
    kv_shape = jax.ShapeDtypeStruct((MEM_LEN, MEM_HEADS * LANES), F32)
    return pl.pallas_call(
        body, name=name, grid=(MEM_HEADS, rows // tq),
        in_specs=[pl.BlockSpec((tq, LANES), lambda h, i: (i, h)),
                  pl.BlockSpec((MEM_LEN, LANES), lambda h, i: (0, h)),
                  pl.BlockSpec((MEM_LEN, LANES), lambda h, i: (0, MEM_HEADS + h)),
                  pl.BlockSpec((tq, LANES), lambda h, i: (i, h))],
        out_specs=[pl.BlockSpec((tq, LANES), lambda h, i: (i, h)), kv_out, kv_out],
        out_shape=[jax.ShapeDtypeStruct((rows, MEM_HEADS * LANES), F32), kv_shape, kv_shape],
        compiler_params=_params(("parallel", "arbitrary")),
    )(mq, kv, kv, dmo)


def _gate_merge_fwd(zg, b_gate, br_s, br_a, br_m, *, tm, name):
    rows, d = br_s.shape

    def body(zg_ref, b_ref, s_ref, a_ref, m_ref, o_ref):
        gt = _sigmoid(zg_ref[...] + b_ref[...])
        o_ref[...] = (gt[:, :d] * s_ref[...] + gt[:, d:2 * d] * a_ref[...] + gt[:, 2 * d:] * m_ref[...]).astype(BF16)

    row = pl.BlockSpec((tm, d), lambda i: (i, 0))
    return pl.pallas_call(
        body, name=name, grid=(rows // tm,),
        in_specs=[pl.BlockSpec((tm, 3 * d), lambda i: (i, 0)), pl.BlockSpec((1, 3 * d), lambda i: (0, 0)), row, row, row],
        out_specs=row, out_shape=jax.ShapeDtypeStruct((rows, d), BF16),
        compiler_params=_params(("parallel",)),
    )(zg, b_gate, br_s, br_a, br_m)


def _gate_merge_bwd(dmerged, zg, b_gate, br_s, br_a, br_m, *, tm, name):
    rows, d = br_s.shape

    def body(dm_ref, zg_ref, b_ref, s_ref, a_ref, m_ref, ds_ref, da_ref, dmm_ref, dzg_ref, db_ref):
        i = pl.program_id(0)
        gt = _sigmoid(zg_ref[...] + b_ref[...])
        dm = dm_ref[...]
        parts = []
        for j, (br_ref, out_ref) in enumerate(((s_ref, ds_ref), (a_ref, da_ref), (m_ref, dmm_ref))):
            gj = gt[:, j * d:(j + 1) * d]
            out_ref[...] = (dm * gj).astype(BF16)
            parts.append(dm * br_ref[...] * gj * (1.0 - gj))
        dzg = jnp.concatenate(parts, axis=1)
        dzg_ref[...] = dzg

        @pl.when(i == 0)
        def _():
            db_ref[...] = jnp.zeros_like(db_ref)

        db_ref[...] += jnp.sum(dzg, axis=0, keepdims=True)

    row = pl.BlockSpec((tm, d), lambda i: (i, 0))
    wide = pl.BlockSpec((tm, 3 * d), lambda i: (i, 0))
    vec = pl.BlockSpec((1, 3 * d), lambda i: (0, 0))
    bshape = jax.ShapeDtypeStruct((rows, d), BF16)
    return pl.pallas_call(
        body, name=name, grid=(rows // tm,),
        in_specs=[row, wide, vec, row, row, row],
        out_specs=[row, row, row, wide, vec],
        out_shape=[bshape, bshape, bshape, jax.ShapeDtypeStruct((rows, 3 * d), F32), jax.ShapeDtypeStruct((1, 3 * d), F32)],
        compiler_params=_params(("arbitrary",)),
    )(dmerged, zg, b_gate, br_s, br_a, br_m)


def _adamw(w, g, m, v, *, tr, name):
    rows, cols = w.shape
    assert rows % tr == 0, (name, rows, tr)

    def body(w_ref, g_ref, m_ref, v_ref, d_ref, nm_ref, nv_ref):
        gv = g_ref[...]
        m2 = ADAM_B1 * m_ref[...] + (1.0 - ADAM_B1) * gv
        v2 = ADAM_B2 * v_ref[...] + (1.0 - ADAM_B2) * (gv * gv)
        m_hat = m2 / (1.0 - ADAM_B1 ** ADAM_STEP)
        v_hat = v2 / (1.0 - ADAM_B2 ** ADAM_STEP)
        d_ref[...] = -ADAM_LR * (m_hat / (jnp.sqrt(v_hat) + ADAM_EPS) + ADAM_WD * w_ref[...])
        nm_ref[...] = m2
        nv_ref[...] = v2

    blk = pl.BlockSpec((tr, cols), lambda i: (i, 0))
    shape = jax.ShapeDtypeStruct((rows, cols), F32)
    return pl.pallas_call(
        body, name=name, grid=(rows // tr,), in_specs=[blk] * 4, out_specs=[blk] * 3,
        out_shape=[shape, shape, shape], compiler_params=_params(("parallel",)),
    )(w, g, m, v)


ANY = pl.BlockSpec(memory_space=pl.ANY)


def _position():
    return lax.axis_index("x"), lax.axis_index("y"), lax.axis_index("c")


def _other_chips(x, y):
    return ((1 - x, y), (x, 1 - y), (1 - x, 1 - y))


def _remote(src, dst, send_sem, recv_sem, dev):
    return pltpu.make_async_remote_copy(src_ref=src, dst_ref=dst, send_sem=send_sem, recv_sem=recv_sem,
                                        device_id=dev, device_id_type=MESH)


def _all_gather_weights(shards):
    nb = len(shards)

    def body(*refs):
        ins, outs = refs[:nb], refs[nb:2 * nb]
        send_sems, recv_sems, local_sems = refs[2 * nb:]
        x, y, c = _position()
        chip = 2 * x + y
        sibling = (x, y, 1 - c)
        chips = _other_chips(x, y)

        def rows_of(i, owner, core):
            rs = shards[i].shape[0]
            return pl.ds(pl.multiple_of(owner * rs + core * (rs // 2), 16), rs // 2)

        local_copies, sends = [], []
        for i in range(nb):
            rs = shards[i].shape[0]
            local = pltpu.make_async_copy(ins[i], outs[i].at[pl.ds(pl.multiple_of(chip * rs, 16), rs)], local_sems.at[i])
            local.start()
            local_copies.append(local)
            for j, (px, py) in enumerate(chips):
                cp = _remote(ins[i].at[pl.ds(pl.multiple_of(c * (rs // 2), 16), rs // 2)], outs[i].at[rows_of(i, chip, c)],
                             send_sems.at[i, j], recv_sems.at[i, j], (px, py, c))
                cp.start()
                sends.append(cp)
        for i in range(nb):
            for j, (px, py) in enumerate(chips):
                landed = outs[i].at[rows_of(i, 2 * px + py, c)]
                _remote(landed, landed, send_sems.at[i, j], recv_sems.at[i, j], (px, py, c)).wait_recv()
                cp = _remote(landed, landed, send_sems.at[i, 3 + j], recv_sems.at[i, 3 + j], sibling)
                cp.start()
                sends.append(cp)
        for i in range(nb):
            for j, (px, py) in enumerate(chips):
                dst = outs[i].at[rows_of(i, 2 * px + py, 1 - c)]
                _remote(dst, dst, send_sems.at[i, 3 + j], recv_sems.at[i, 3 + j], sibling).wait_recv()
        for cp in sends:
            cp.wait_send()
        for cp in local_copies:
            cp.wait()

    return pl.pallas_call(
        body, name="all_gather_weights", in_specs=[ANY] * nb, out_specs=[ANY] * nb,
        out_shape=[jax.ShapeDtypeStruct((N_CHIPS * s.shape[0], s.shape[1]), s.dtype) for s in shards],
        scratch_shapes=[pltpu.SemaphoreType.DMA((nb, 6)), pltpu.SemaphoreType.DMA((nb, 6)), pltpu.SemaphoreType.DMA((nb,))],
    )(*shards)


def _row_tile(rows):
    return max(t for t in range(16, min(rows, 512) + 1, 16) if rows % t == 0)


def _exchange_halves(grads):
    nb = len(grads)

    def body(*refs):
        ins, outs = refs[:nb], refs[nb:2 * nb]
        send_sems, recv_sems = refs[2 * nb:]
        x, y, c = _position()
        copies = []
        for i in range(nb):
            cp = _remote(ins[i].at[:, 1 - c], outs[i], send_sems.at[i], recv_sems.at[i], (x, y, 1 - c))
            cp.start()
            copies.append(cp)
        for cp in copies:
            cp.wait()

    return pl.pallas_call(
        body, name="grad_exchange_halves", in_specs=[ANY] * nb, out_specs=[ANY] * nb,
        out_shape=[jax.ShapeDtypeStruct((N_CHIPS, g.shape[2], g.shape[3]), F32) for g in grads],
        scratch_shapes=[pltpu.SemaphoreType.DMA((nb,)), pltpu.SemaphoreType.DMA((nb,))],
    )(*grads)


def _pair_sum(g4, got, c_arr, *, name):
    _, _, half, cols = g4.shape
    tr = _row_tile(half)

    def body(c_ref, g_ref, t_ref, p_ref, pb_ref):
        sm = g_ref[...] + t_ref[...]
        p_ref[...] = sm
        pb_ref[...] = sm.astype(BF16)

    blk = pl.BlockSpec((None, tr, cols), lambda j, i, c_ref: (j, i, 0))
    grid_spec = pltpu.PrefetchScalarGridSpec(
        num_scalar_prefetch=1, grid=(N_CHIPS, half // tr),
        in_specs=[pl.BlockSpec((None, None, tr, cols), lambda j, i, c_ref: (j, c_ref[0], i, 0)), blk],
        out_specs=[blk, blk])
    return pl.pallas_call(
        body, name=name, grid_spec=grid_spec,
        out_shape=[jax.ShapeDtypeStruct((N_CHIPS, half, cols), F32), jax.ShapeDtypeStruct((N_CHIPS, half, cols), BF16)],
        compiler_params=_params(("parallel", "parallel")),
    )(c_arr, g4, got)


def _scatter_to_owners(parts):
    nb = len(parts)

    def body(*refs):
        ins, outs = refs[:nb], refs[nb:2 * nb]
        send_sems, recv_sems = refs[2 * nb:]
        x, y, c = _position()
        copies = []
        for i in range(nb):
            for j, (px, py) in enumerate(_other_chips(x, y)):
                cp = _remote(ins[i].at[2 * px + py], outs[i].at[j], send_sems.at[i, j], recv_sems.at[i, j], (px, py, c))
                cp.start()
                copies.append(cp)
        for cp in copies:
            cp.wait()

    return pl.pallas_call(
        body, name="grad_scatter_to_owners", in_specs=[ANY] * nb, out_specs=[ANY] * nb,
        out_shape=[jax.ShapeDtypeStruct((3,) + p.shape[1:], p.dtype) for p in parts],
        scratch_shapes=[pltpu.SemaphoreType.DMA((nb, 3)), pltpu.SemaphoreType.DMA((nb, 3))],
    )(*parts)


def _owner_sum(p, got, chip_arr, *, name):
    _, half, cols = p.shape
    tr = _row_tile(half)

    def body(chip_ref, p_ref, r_ref, o_ref):
        o_ref[...] = ((p_ref[...] + r_ref[0].astype(F32)) + r_ref[1].astype(F32)) + r_ref[2].astype(F32)

    grid_spec = pltpu.PrefetchScalarGridSpec(
        num_scalar_prefetch=1, grid=(half // tr,),
        in_specs=[pl.BlockSpec((None, tr, cols), lambda i, chip_ref: (chip_ref[0], i, 0)),
                  pl.BlockSpec((3, tr, cols), lambda i, chip_ref: (0, i, 0))],
        out_specs=pl.BlockSpec((tr, cols), lambda i, chip_ref: (i, 0)))
    return pl.pallas_call(
        body, name=name, grid_spec=grid_spec,
        out_shape=jax.ShapeDtypeStruct((half, cols), F32),
        compiler_params=_params(("parallel",)),
    )(chip_arr, p, got)


def _share_reduced(totals):
    nb = len(totals)

    def body(*refs):
        ins, fulls, small_ref = refs[:nb], refs[nb:2 * nb], refs[2 * nb]
        send_sems, recv_sems, local_sems = refs[2 * nb + 1:]
        x, y, c = _position()
        chip = 2 * x + y
        sibling = (x, y, 1 - c)
        local_copies, sends = [], []
        for i in range(nb):
            own = pltpu.make_async_copy(ins[i], fulls[i].at[c], local_sems.at[i])
            own.start()
            local_copies.append(own)
            cp = _remote(ins[i], fulls[i].at[c], send_sems.at[i], recv_sems.at[i], sibling)
            cp.start()
            sends.append(cp)
        for i in range(nb):
            _remote(ins[i], fulls[i].at[1 - c], send_sems.at[i], recv_sems.at[i], sibling).wait_recv()
            local_copies[i].wait()
        mine = fulls[nb - 1]
        keep = pltpu.make_async_copy(mine, small_ref.at[chip], local_sems.at[nb])
        keep.start()
        for j, (px, py) in enumerate(_other_chips(x, y)):
            cp = _remote(mine, small_ref.at[chip], send_sems.at[nb + j], recv_sems.at[nb + j], (px, py, c))
            cp.start()
            sends.append(cp)
        for j, (px, py) in enumerate(_other_chips(x, y)):
            dst = small_ref.at[2 * px + py]
            _remote(dst, dst, send_sems.at[nb + j], recv_sems.at[nb + j], (px, py, c)).wait_recv()
        for cp in sends:
            cp.wait_send()
        keep.wait()

    full_shapes = [jax.ShapeDtypeStruct((2,) + t.shape, F32) for t in totals]
    return pl.pallas_call(
        body, name="grad_share_reduced", in_specs=[ANY] * nb, out_specs=[ANY] * (nb + 1),
        out_shape=full_shapes + [jax.ShapeDtypeStruct((N_CHIPS, 2) + totals[-1].shape, F32)],
        scratch_shapes=[pltpu.SemaphoreType.DMA((nb + 3,)), pltpu.SemaphoreType.DMA((nb + 3,)),
                        pltpu.SemaphoreType.DMA((nb + 1,))],
    )(*totals)


def _pack_small(vals):
    flat = jnp.concatenate([vals[name].reshape(-1) for name, _ in SMALL])
    return jnp.pad(flat, (0, N_CHIPS * SMALL_ROWS * 1024 - SMALL_ELEMS)).reshape(N_CHIPS * SMALL_ROWS, 1024)


def _unpack_small(buf):
    flat = buf.reshape(-1)
    out, off = {}, 0
    for name, shape in SMALL:
        n = int(np.prod(shape))
        out[name] = flat[off:off + n].reshape(shape)
        off += n
    return out


def _device_step(x, mem, tgt, w, p):
    rows = x.shape[0]
    g1, gm, g2 = p["norm1_g"], p["mem_norm_g"], p["norm2_g"]
    gf = p["final_g"].reshape(1, D_MODEL)
    ssm_args = (p["ssm_lambda_re"][0], p["ssm_lambda_im"][0], p["ssm_log_dt"][0], p["ssm_b_re"][0],
                p["ssm_b_im"][0], p["ssm_c_re"][0], p["ssm_c_im"][0])
    (a_lay, b_lay, c_lay), ssm_vjp = jax.vjp(_ssm_matrices, *ssm_args)
    a_conj = a_lay * _to_scan_layout(jnp.stack([jnp.ones((N_STATES,), F32), -jnp.ones((N_STATES,), F32)]))[None, :]
    dd = p["ssm_d"].reshape(1, SSM_WIDTH)
    win_t = w["w_in"]
    mm = _matmul

    n1 = _rmsnorm_fwd(x, g1, tm=512, name="norm1")
    u = mm(n1, win_t, m=rows, n=512, k=1024, tb=True, tm=2048, tn=512, tk=1024, out_dtypes=(F32,), name="in_u")
    qkv = mm(n1, win_t, m=rows, n=2304, k=1024, tb=True, tm=2048, tn=256, tk=1024, b_off=(OFF_QKV // 256, 0),
             out_dtypes=(F32,), name="in_qkv")
    mq = mm(n1, win_t, m=rows, n=512, k=1024, tb=True, tm=2048, tn=256, tk=1024, b_off=(OFF_MQ // 256, 0),
            out_dtypes=(F32,), name="in_mq")
    zg = mm(n1, win_t, m=rows, n=3072, k=1024, tb=True, tm=2048, tn=256, tk=1024, b_off=(OFF_ZG // 256, 0),
            out_dtypes=(F32,), name="in_zg")

    u_i = _interleave(u)
    bu = mm(u_i, b_lay, m=rows, n=2 * N_STATES, k=512, tm=1024, tn=1024, tk=512, out_dtypes=(F32,), name="ssm_bu")
    ends = _scan_ends(a_lay, bu, reverse=False, tt=512, name="ssm_scan_fwd_ends")
    s, s_entry = _scan_apply(a_lay, bu, ends, reverse=False, tt=512, name="ssm_scan_fwd")
    ys = _deinterleave(mm(s, c_lay, m=rows, n=512, k=2 * N_STATES, tm=1024, tn=512, tk=1024, out_dtypes=(F32,), name="ssm_cs"))
    y0, tglu, y2 = _glu_fwd(ys, u, dd, w["w_glu"], p["b_glu"], tm=512, name="glu_fwd")
    br_s = mm(y2, w["w_ssm_br"], m=rows, n=1024, k=512, tb=True, tm=1024, tn=1024, tk=512, out_dtypes=(F32,), name="br_ssm")

    outs, lses = [], []
    for g, (_, d) in enumerate(ATTN_PATTERNS):
        o_g, lse_g = _attn_fwd(qkv, g, d, name=f"attn_fwd_{g}")
        outs.append(o_g)
        lses.append(lse_g)
    o, lse = _attn_merge(outs, lses, tm=1024, name="attn_merge")
    br_a = mm(o, w["w_attn_br"], m=rows, n=1024, k=256, tb=True, tm=1024, tn=1024, tk=256, out_dtypes=(F32,), name="br_attn")

    mn = _rmsnorm_fwd(mem, gm, tm=MEM_LEN, name="mem_norm")
    kv = mm(mn, w["w_mem_kv"], m=MEM_LEN, n=1024, k=1024, tm=MEM_LEN, tn=1024, tk=1024, out_dtypes=(F32,), name="mem_kv")
    mo = _mem_attn_fwd(mq, kv, tq=1024, name="mem_attn_fwd")
    br_m = mm(mo, w["w_mem_br"], m=rows, n=1024, k=512, tb=True, tm=1024, tn=1024, tk=512, out_dtypes=(F32,), name="br_mem")

    merged = _gate_merge_fwd(zg, p["b_gate"], br_s, br_a, br_m, tm=256, name="gate_merge_fwd")
    add = lambda acc, r: (acc + r,)
    h1 = mm(merged, w["w_o"], m=rows, n=1024, k=1024, tm=1024, tn=1024, tk=1024, out_dtypes=(F32,),
            aux=((x, "mn"),), epilogue=add, name="out_proj")
    n2 = _rmsnorm_fwd(h1, g2, tm=512, name="norm2")
    relu2 = lambda acc: (acc, jnp.square(jnp.maximum(acc, 0.0)))
    up, act = mm(n2, w["w_up"], m=rows, n=D_FF, k=1024, tb=True, tm=1024, tn=1024, tk=1024, out_dtypes=(F32, BF16),
                 epilogue=relu2, name="mlp_up")
    h2 = mm(act, w["w_down"], m=rows, n=1024, k=D_FF, tm=1024, tn=1024, tk=1024, out_dtypes=(F32,),
            aux=((h1, "mn"),), epilogue=add, name="mlp_down")
    dh2, loss, d_gf = _loss_head(h2, tgt, gf, tm=512, name="loss_head")

    gb = {}
    gs = {"final_g": d_gf.reshape(D_MODEL)}
    drelu2 = lambda acc, upv: (acc * (2.0 * jnp.maximum(upv, 0.0)),)
    dup = mm(dh2, w["w_down"], m=rows, n=D_FF, k=1024, tb=True, tm=1024, tn=1024, tk=1024, out_dtypes=(BF16,),
             aux=((up, "mn"),), epilogue=drelu2, name="d_act")
    gb["w_down"] = mm(act, dh2, m=D_FF, n=1024, k=rows, ta=True, tm=1024, tn=1024, tk=1024, out_dtypes=(F32,), name="dw_down")
    dn2 = mm(dup, w["w_up"], m=rows, n=1024, k=D_FF, tm=1024, tn=1024, tk=1024, out_dtypes=(F32,), name="d_n2")
    gb["w_up"] = mm(dup, n2, m=D_FF, n=1024, k=rows, ta=True, tm=1024, tn=1024, tk=1024, out_dtypes=(F32,), name="dw_up")
    dh1, gs["norm2_g"] = _rmsnorm_bwd(h1, g2, dn2, dh2, tm=512, name="norm2_bwd")
    dmerged = mm(dh1, w["w_o"], m=rows, n=1024, k=1024, tb=True, tm=1024, tn=1024, tk=1024, out_dtypes=(F32,), name="d_merged")
    gb["w_o"] = mm(merged, dh1, m=1024, n=1024, k=rows, ta=True, tm=1024, tn=1024, tk=1024, out_dtypes=(F32,), name="dw_o")
    dbr_s, dbr_a, dbr_m, dzg, gs["b_gate"] = _gate_merge_bwd(dmerged, zg, p["b_gate"], br_s, br_a, br_m, tm=256,
                                                              name="gate_merge_bwd")

    dy2 = mm(dbr_s, w["w_ssm_br"], m=rows, n=512, k=1024, tm=1024, tn=512, tk=1024, out_dtypes=(F32,), name="d_y2")
    gb["w_ssm_br"] = mm(dbr_s, y2, m=1024, n=512, k=rows, ta=True, tm=1024, tn=512, tk=1024, out_dtypes=(F32,), name="dw_ssm_br")
    dy0, dt, y1, gs["b_glu"], d_dd = _glu_bwd(dy2, y0, tglu, u, w["w_glu"], tm=512, name="glu_bwd")
    gs["ssm_d"] = d_dd.reshape(1, SSM_GROUPS, SSM_GROUP_SIZE)
    gb["w_glu"] = mm(y1, dt, m=512, n=512, k=rows, ta=True, tm=512, tn=512, tk=1024, out_dtypes=(F32,), name="dw_glu")
    dy0_i = _interleave(dy0)
    dsout = mm(dy0_i, c_lay, m=rows, n=2 * N_STATES, k=512, tb=True, tm=1024, tn=1024, tk=512, out_dtypes=(F32,), name="ssm_dsout")
    lam_ends = _scan_ends(a_conj, dsout, reverse=True, tt=512, name="ssm_scan_bwd_ends")
    lam, _ = _scan_apply(a_conj, dsout, lam_ends, reverse=True, tt=512, name="ssm_scan_bwd")
    skip = lambda acc, dyv, ddv: (acc + ddv * dyv,)
    du = _deinterleave(mm(lam, b_lay, m=rows, n=512, k=2 * N_STATES, tb=True, tm=1024, tn=512, tk=1024, out_dtypes=(F32,),
                          aux=((dy0_i, "mn"), (dd, "row")), epilogue=skip, name="ssm_du"))
    d_b_lay = mm(u_i, lam, m=512, n=2 * N_STATES, k=rows, ta=True, tm=512, tn=1024, tk=1024, out_dtypes=(F32,), name="ssm_db")
    d_c_lay = mm(s, dy0_i, m=2 * N_STATES, n=512, k=rows, ta=True, tm=1024, tn=512, tk=1024, out_dtypes=(F32,), name="ssm_dc")
    d_a_lay = _ssm_da(lam, s, s_entry, tt=512, name="ssm_da")
    d_ssm = ssm_vjp((d_a_lay, d_b_lay, d_c_lay))
    for name, val in zip(("ssm_lambda_re", "ssm_lambda_im", "ssm_log_dt", "ssm_b_re", "ssm_b_im", "ssm_c_re", "ssm_c_im"), d_ssm):
        gs[name] = val[None]

    do = mm(dbr_a, w["w_attn_br"], m=rows, n=256, k=1024, tm=1024, tn=256, tk=1024, out_dtypes=(F32,), name="d_o")
    gb["w_attn_br"] = mm(dbr_a, o, m=1024, n=256, k=rows, ta=True, tm=1024, tn=256, tk=1024, out_dtypes=(F32,), name="dw_attn_br")
    dqkv = None
    for g, (_, d) in enumerate(ATTN_PATTERNS):
        dqkv = _attn_bwd(qkv, do, o, lse, g, d, dqkv, name=f"attn_bwd_{g}")

    dmo = mm(dbr_m, w["w_mem_br"], m=rows, n=512, k=1024, tm=1024, tn=512, tk=1024, out_dtypes=(F32,), name="d_mo")
    gb["w_mem_br"] = mm(dbr_m, mo, m=1024, n=512, k=rows, ta=True, tm=1024, tn=512, tk=1024, out_dtypes=(F32,), name="dw_mem_br")
    dmq, dmk, dmv = _mem_attn_bwd(mq, kv, dmo, tq=1024, name="mem_attn_bwd")
    dkv = jnp.concatenate([dmk, dmv], axis=1)
    gb["w_mem_kv"] = mm(mn, dkv, m=1024, n=1024, k=MEM_LEN, ta=True, tm=1024, tn=1024, tk=MEM_LEN, out_dtypes=(F32,), name="dw_mem_kv")
    dmn = mm(dkv, w["w_mem_kv"], m=MEM_LEN, n=1024, k=1024, tb=True, tm=MEM_LEN, tn=1024, tk=1024, out_dtypes=(F32,), name="d_mn")
    _, gs["mem_norm_g"] = _rmsnorm_bwd(mem, gm, dmn, None, tm=MEM_LEN, name="mem_norm_bwd")

    pieces = ((du, OFF_U, "u"), (dqkv[0], OFF_QKV, "q"), (dqkv[1], OFF_QKV + 768, "k"), (dqkv[2], OFF_QKV + 1536, "v"),
              (dmq, OFF_MQ, "mq"), (dzg, OFF_ZG, "zg"))
    dn = None
    dw_rows = []
    for piece, off, tag in pieces:
        width = piece.shape[1]
        tk = 1024 if (width % 1024 == 0 and off % 1024 == 0) else 256
        dn = mm(piece, win_t, m=rows, n=1024, k=width, tm=1024, tn=1024, tk=tk, b_off=(off // tk, 0), out_dtypes=(F32,),
                aux=() if dn is None else ((dn, "mn"),), epilogue=None if dn is None else add, name="d_n1_" + tag)
        tmw = 1024 if width % 1024 == 0 else (768 if width == 768 else 512)
        dw_rows.append(mm(piece, n1, m=width, n=1024, k=rows, ta=True, tm=tmw, tn=1024, tk=1024, out_dtypes=(F32,),
                          name="dw_in_" + tag))
    gb["w_in"] = jnp.concatenate(dw_rows, axis=0)
    dx, gs["norm1_g"] = _rmsnorm_bwd(x, g1, dn, dh1, tm=512, name="norm1_bwd")
    return loss, dx, gb, gs


def kernel(x, mem, norm1_g, mem_norm_g, w_in, b_gate, ssm_lambda_re, ssm_lambda_im, ssm_log_dt, ssm_b_re, ssm_b_im, ssm_c_re, ssm_c_im, ssm_d, w_glu, b_glu, w_ssm_br, w_attn_br, w_mem_kv, w_mem_br, w_o, norm2_g, w_up, w_down, final_g, loss_target, m_norm1_g, m_mem_norm_g, m_w_in, m_b_gate, m_ssm_lambda_re, m_ssm_lambda_im, m_ssm_log_dt, m_ssm_b_re, m_ssm_b_im, m_ssm_c_re, m_ssm_c_im, m_ssm_d, m_w_glu, m_b_glu, m_w_ssm_br, m_w_attn_br, m_w_mem_kv, m_w_mem_br, m_w_o, m_norm2_g, m_w_up, m_w_down, m_final_g, v_norm1_g, v_mem_norm_g, v_w_in, v_b_gate, v_ssm_lambda_re, v_ssm_lambda_im, v_ssm_log_dt, v_ssm_b_re, v_ssm_b_im, v_ssm_c_re, v_ssm_c_im, v_ssm_d, v_w_glu, v_b_glu, v_w_ssm_br, v_w_attn_br, v_w_mem_kv, v_w_mem_br, v_w_o, v_norm2_g, v_w_up, v_w_down, v_final_g):
    env = dict(locals())
    weights = {n: env[n] for n in WEIGHT_ORDER}
    moms = {n: env["m_" + n] for n in WEIGHT_ORDER}
    vels = {n: env["v_" + n] for n in WEIGHT_ORDER}
    def shard2d(a):
        return a.reshape(a.shape[-2], a.shape[-1])

    wire = [shard2d(weights[n]).astype(BF16) for n, _, _ in BIG]
    wire = [s.T if tr else s for s, (_, tr, _) in zip(wire, BIG)]
    w_full = dict(zip([n for n, _, _ in BIG], _all_gather_weights(wire)))
    small = {n: weights[n] for n, _ in SMALL}

    loss, dx, gb, gs = _device_step(x[0], mem[0], loss_target[0], w_full, small)

    c_arr = lax.axis_index("c").astype(jnp.int32).reshape(1)
    chip_arr = (2 * lax.axis_index("x") + lax.axis_index("y")).astype(jnp.int32).reshape(1)
    names = [n for n, _, _ in BIG] + ["small"]
    full = [gb[n] for n, _, _ in BIG] + [_pack_small(gs)]
    full = [g.reshape(N_CHIPS, 2, g.shape[0] // (2 * N_CHIPS), g.shape[1]) for g in full]
    from_sibling = _exchange_halves(full)
    pairs = [_pair_sum(g, t, c_arr, name="grad_pair_sum_" + n) for g, t, n in zip(full, from_sibling, names)]
    landed = _scatter_to_owners([pb for _, pb in pairs])
    totals = [_owner_sum(pf, r, chip_arr, name="grad_owner_sum_" + n) for (pf, _), r, n in zip(pairs, landed, names)]
    *shards, small_grad = _share_reduced(totals)
    grads = {}
    for (n, tr, _), sh in zip(BIG, shards):
        sh = sh.reshape(2 * sh.shape[1], sh.shape[2])
        grads[n] = sh.T if tr else sh
    small_grad = small_grad.reshape(N_CHIPS * SMALL_ROWS, 1024)
    grads_small = _unpack_small(small_grad)

    delta, new_m, new_v = {}, {}, {}
    for n, _, _ in BIG:
        shape = weights[n].shape
        dn_, nm_, nv_ = _adamw(shard2d(weights[n]), grads[n], shard2d(moms[n]), shard2d(vels[n]),
                               tr=min(shape[-2], 256), name="adamw_" + n)
        delta[n], new_m[n], new_v[n] = dn_.reshape(shape), nm_.reshape(shape), nv_.reshape(shape)
        grads[n] = grads[n].reshape(shape)
    ds_, ms_, vs_ = _adamw(_pack_small(small), small_grad,
                           _pack_small({n: moms[n] for n, _ in SMALL}), _pack_small({n: vels[n] for n, _ in SMALL}),
                           tr=N_CHIPS * SMALL_ROWS, name="adamw_small")
    for dst, buf in ((delta, ds_), (new_m, ms_), (new_v, vs_)):
        dst.update(_unpack_small(buf))
    grads.update(grads_small)

    total_loss = lax.psum(loss[0, 0], ("x", "y", "c"))
    return (total_loss, dx[None], *[grads[n] for n in WEIGHT_ORDER], *[delta[n] for n in WEIGHT_ORDER],
            *[new_m[n] for n in WEIGHT_ORDER], *[new_v[n] for n in WEIGHT_ORDER])
```

```python
import functools
import math

import numpy as np
import jax
import jax.numpy as jnp
from jax import lax
from jax.experimental import pallas as pl
from jax.experimental.pallas import tpu as pltpu

F32 = jnp.float32
BF16 = jnp.bfloat16

D_MODEL = 1024
SSM_GROUPS = 32
SSM_GROUP_SIZE = 16
SSM_STATE = 64
SSM_WIDTH = 512
N_STATES = SSM_GROUPS * SSM_STATE
SCAN_CB = 512
ATTN_PATTERNS = ((128, 1), (512, 4), (2048, 16))
ATTN_HEAD_DIM = 64
ATTN_Q = 128
MEM_LEN = 256
MEM_HEAD_DIM = 128
MEM_HEADS = 4
D_FF = 4096
OFF_U, OFF_QKV, OFF_MQ, OFF_ZG = 0, 512, 2816, 3328
IN_WIDTH = 6400
RMS_EPS = 1e-6
NEG_INF = -1e30
ADAM_LR, ADAM_B1, ADAM_B2, ADAM_EPS, ADAM_WD, ADAM_STEP = 0.001, 0.9, 0.999, 1e-08, 0.01, 10

VMEM_LIMIT_BYTES = 48 * 1024 * 1024
LANES = 128
MESH = pl.DeviceIdType.MESH
N_CHIPS = 4

SCAN_SEGS = 8

BIG = (("w_in", True, (6400, 1024)), ("w_glu", False, (512, 512)), ("w_ssm_br", True, (1024, 512)),
       ("w_attn_br", True, (1024, 256)), ("w_mem_kv", False, (1024, 1024)), ("w_mem_br", True, (1024, 512)),
       ("w_o", False, (1024, 1024)), ("w_up", True, (4096, 1024)), ("w_down", False, (4096, 1024)))
SMALL = (("norm1_g", (1, 1024)), ("mem_norm_g", (1, 1024)), ("b_gate", (1, 3072)),
         ("ssm_lambda_re", (1, 32, 64)), ("ssm_lambda_im", (1, 32, 64)), ("ssm_log_dt", (1, 32)),
         ("ssm_b_re", (1, 32, 64, 16)), ("ssm_b_im", (1, 32, 64, 16)), ("ssm_c_re", (1, 32, 16, 64)),
         ("ssm_c_im", (1, 32, 16, 64)), ("ssm_d", (1, 32, 16)), ("b_glu", (1, 512)),
         ("norm2_g", (1, 1024)), ("final_g", (1024,)))
WEIGHT_ORDER = ("norm1_g", "mem_norm_g", "w_in", "b_gate", "ssm_lambda_re", "ssm_lambda_im", "ssm_log_dt",
                "ssm_b_re", "ssm_b_im", "ssm_c_re", "ssm_c_im", "ssm_d", "w_glu", "b_glu", "w_ssm_br",
                "w_attn_br", "w_mem_kv", "w_mem_br", "w_o", "norm2_g", "w_up", "w_down", "final_g")
SMALL_ELEMS = sum(int(np.prod(s)) for _, s in SMALL)
SMALL_ROWS = 64


def _params(sem):
    return pltpu.CompilerParams(dimension_semantics=sem, vmem_limit_bytes=VMEM_LIMIT_BYTES)


def _sigmoid(v):
    return 1.0 / (1.0 + jnp.exp(-v))


_GELU_C = math.sqrt(2.0 / math.pi)


def _gelu(v):
    return 0.5 * v * (1.0 + jnp.tanh(_GELU_C * (v + 0.044715 * v * v * v)))


def _gelu_grad(v):
    th = jnp.tanh(_GELU_C * (v + 0.044715 * v * v * v))
    return 0.5 * (1.0 + th) + 0.5 * v * (1.0 - th * th) * _GELU_C * (1.0 + 3.0 * 0.044715 * v * v)


def _dot(a, b, ca, cb):
    return lax.dot_general(a, b, (((ca,), (cb,)), ((), ())), preferred_element_type=F32)


def _matmul(a, b, *, m, n, k, ta=False, tb=False, tm, tn, tk, out_dtypes, name,
            a_off=(0, 0), b_off=(0, 0), aux=(), epilogue=None):
    assert m % tm == 0 and n % tn == 0 and k % tk == 0, (name, m, n, k, tm, tn, tk)
    nk = k // tk
    n_aux = len(aux)
    n_out = len(out_dtypes)
    ar, ac = a_off
    br, bc = b_off
    if ta:
        a_spec = pl.BlockSpec((tk, tm), lambda i, j, kk: (kk + ar, i + ac))
    else:
        a_spec = pl.BlockSpec((tm, tk), lambda i, j, kk: (i + ar, kk + ac))
    if tb:
        b_spec = pl.BlockSpec((tn, tk), lambda i, j, kk: (j + br, kk + bc))
    else:
        b_spec = pl.BlockSpec((tk, tn), lambda i, j, kk: (kk + br, j + bc))
    aux_specs = []
    for _, kind in aux:
        if kind == "mn":
            aux_specs.append(pl.BlockSpec((tm, tn), lambda i, j, kk: (i, j)))
        else:
            aux_specs.append(pl.BlockSpec((1, tn), lambda i, j, kk: (0, j)))
    ca = 0 if ta else 1
    cb = 1 if tb else 0

    def finish(acc, aux_refs, out_refs):
        outs = (acc,) if epilogue is None else epilogue(acc, *[r[...] for r in aux_refs])
        for o_ref, o in zip(out_refs, outs):
            o_ref[...] = o.astype(o_ref.dtype)

    def body(a_ref, b_ref, *rest):
        aux_refs = rest[:n_aux]
        out_refs = rest[n_aux:n_aux + n_out]
        prod = _dot(a_ref[...].astype(BF16), b_ref[...].astype(BF16), ca, cb)
        if nk == 1:
            finish(prod, aux_refs, out_refs)
            return
        acc_ref = rest[n_aux + n_out]
        kk = pl.program_id(2)

        @pl.when(kk == 0)
        def _():
            acc_ref[...] = prod

        @pl.when(jnp.logical_and(kk > 0, kk < nk - 1))
        def _():
            acc_ref[...] += prod

        @pl.when(kk == nk - 1)
        def _():
            finish(acc_ref[...] + prod, aux_refs, out_refs)

    res = pl.pallas_call(
        body, name=name, grid=(m // tm, n // tn, nk),
        in_specs=[a_spec, b_spec] + aux_specs,
        out_specs=[pl.BlockSpec((tm, tn), lambda i, j, kk: (i, j)) for _ in range(n_out)],
        out_shape=[jax.ShapeDtypeStruct((m, n), dt) for dt in out_dtypes],
        scratch_shapes=[pltpu.VMEM((tm, tn), F32)] if nk > 1 else [],
        compiler_params=_params(("parallel", "parallel", "arbitrary")),
    )(a, b, *[x for x, _ in aux])
    return res[0] if n_out == 1 else tuple(res)


def _rmsnorm_fwd(x, g, *, tm, name):
    rows, d = x.shape

    def body(x_ref, g_ref, o_ref):
        xv = x_ref[...]
        r = lax.rsqrt(jnp.mean(xv * xv, axis=-1, keepdims=True) + RMS_EPS)
        o_ref[...] = (xv * r * g_ref[...]).astype(o_ref.dtype)

    return pl.pallas_call(
        body, name=name, grid=(rows // tm,),
        in_specs=[pl.BlockSpec((tm, d), lambda i: (i, 0)), pl.BlockSpec((1, d), lambda i: (0, 0))],
        out_specs=pl.BlockSpec((tm, d), lambda i: (i, 0)),
        out_shape=jax.ShapeDtypeStruct((rows, d), BF16),
        compiler_params=_params(("parallel",)),
    )(x, g)


def _rmsnorm_bwd(x, g, dy, res, *, tm, name):
    rows, d = x.shape
    has_res = res is not None

    def body(x_ref, g_ref, dy_ref, *rest):
        if has_res:
            res_ref, dx_ref, dg_ref = rest
        else:
            dx_ref, dg_ref = rest
        i = pl.program_id(0)
        xv = x_ref[...]
        r = lax.rsqrt(jnp.mean(xv * xv, axis=-1, keepdims=True) + RMS_EPS)
        xhat = xv * r
        dyv = dy_ref[...]
        dyg = dyv * g_ref[...]
        dx = r * (dyg - xhat * jnp.mean(dyg * xhat, axis=-1, keepdims=True))
        if has_res:
            dx = dx + res_ref[...]
        dx_ref[...] = dx

        @pl.when(i == 0)
        def _():
            dg_ref[...] = jnp.zeros_like(dg_ref)

        dg_ref[...] += jnp.sum(dyv * xhat, axis=0, keepdims=True)

    row_spec = pl.BlockSpec((tm, d), lambda i: (i, 0))
    vec_spec = pl.BlockSpec((1, d), lambda i: (0, 0))
    ins = [x, g, dy] + ([res] if has_res else [])
    return pl.pallas_call(
        body, name=name, grid=(rows // tm,),
        in_specs=[row_spec, vec_spec, row_spec] + ([row_spec] if has_res else []),
        out_specs=[row_spec, vec_spec],
        out_shape=[jax.ShapeDtypeStruct((rows, d), F32), jax.ShapeDtypeStruct((1, d), F32)],
        compiler_params=_params(("arbitrary",)),
    )(*ins)


def _loss_head(h, tgt, g, *, tm, name):
    rows, d = h.shape
    nsteps = rows // tm

    def body(h_ref, t_ref, g_ref, dh_ref, loss_ref, dg_ref, sq_ref):
        i = pl.program_id(0)
        xv = h_ref[...]
        gv = g_ref[...]
        r = lax.rsqrt(jnp.mean(xv * xv, axis=-1, keepdims=True) + RMS_EPS)
        xhat = xv * r
        err = xhat * gv - t_ref[...]
        dyv = err * (1.0 / d)
        dyg = dyv * gv
        dh_ref[...] = r * (dyg - xhat * jnp.mean(dyg * xhat, axis=-1, keepdims=True))

        @pl.when(i == 0)
        def _():
            dg_ref[...] = jnp.zeros_like(dg_ref)
            sq_ref[...] = jnp.zeros_like(sq_ref)

        dg_ref[...] += jnp.sum(dyv * xhat, axis=0, keepdims=True)
        sq_ref[...] += jnp.sum(err * err, axis=0, keepdims=True)

        @pl.when(i == nsteps - 1)
        def _():
            tot = jnp.sum(sq_ref[...], axis=-1, keepdims=True) * (0.5 / d)
            loss_ref[...] = jnp.broadcast_to(tot, loss_ref.shape)

    row_spec = pl.BlockSpec((tm, d), lambda i: (i, 0))
    vec_spec = pl.BlockSpec((1, d), lambda i: (0, 0))
    return pl.pallas_call(
        body, name=name, grid=(nsteps,),
        in_specs=[row_spec, row_spec, vec_spec],
        out_specs=[row_spec, pl.BlockSpec((1, LANES), lambda i: (0, 0)), vec_spec],
        out_shape=[jax.ShapeDtypeStruct((rows, d), F32), jax.ShapeDtypeStruct((1, LANES), F32),
                   jax.ShapeDtypeStruct((1, d), F32)],
        scratch_shapes=[pltpu.VMEM((1, d), F32)],
        compiler_params=_params(("arbitrary",)),
    )(h, tgt, g)


def _to_scan_layout(v):
    lead = v.shape[:-2]
    v = v.reshape(lead + (2, N_STATES // SCAN_CB, SCAN_CB))
    v = jnp.swapaxes(v, -3, -2)
    return v.reshape(lead + (2 * N_STATES,))


def _ssm_matrices(lam_re, lam_im, log_dt, b_re, b_im, c_re, c_im):
    dt = jnp.exp(log_dt)[:, None]
    mag = jnp.exp(lam_re * dt)
    a_re, a_im = mag * jnp.cos(lam_im * dt), mag * jnp.sin(lam_im * dt)
    nr, ni = a_re - 1.0, a_im
    den = lam_re * lam_re + lam_im * lam_im
    coef_re = (nr * lam_re + ni * lam_im) / den
    coef_im = (ni * lam_re - nr * lam_im) / den
    bb_re = coef_re[..., None] * b_re - coef_im[..., None] * b_im
    bb_im = coef_re[..., None] * b_im + coef_im[..., None] * b_re
    eye = jnp.eye(SSM_GROUPS, dtype=F32)
    a_lay = _to_scan_layout(jnp.stack([a_re.reshape(-1), a_im.reshape(-1)], axis=0))[None, :]

    def b_dense(bb):
        return jnp.einsum("gk,kph->ghkp", eye, bb).reshape(SSM_WIDTH, N_STATES)

    b_lay = _to_scan_layout(jnp.stack([b_dense(bb_re), b_dense(bb_im)], axis=1))

    def c_dense(cc):
        return jnp.einsum("gk,ghp->kpgh", eye, cc).reshape(N_STATES, SSM_WIDTH)

    c_lay = _to_scan_layout(jnp.stack([c_dense(c_re), -c_dense(c_im)], axis=0).transpose(2, 0, 1)).T
    return a_lay, b_lay, c_lay


def _interleave(v):
    rows, c = v.shape
    return v.reshape(SCAN_SEGS, rows // SCAN_SEGS, c).transpose(1, 0, 2).reshape(rows, c)


def _deinterleave(v):
    rows, c = v.shape
    return v.reshape(rows // SCAN_SEGS, SCAN_SEGS, c).transpose(1, 0, 2).reshape(rows, c)


def _scan_groups(a_ref, bu_ref, o_ref, state, *, reverse, tt):
    cb = SCAN_CB
    ar = jnp.broadcast_to(a_ref[:, :cb], (SCAN_SEGS, cb))
    ai = jnp.broadcast_to(a_ref[:, cb:], (SCAN_SEGS, cb))
    ngroups = tt // SCAN_SEGS

    def step(i, st):
        sr, si = st
        r0 = pl.multiple_of(((ngroups - 1 - i) if reverse else i) * SCAN_SEGS, SCAN_SEGS)
        blk = bu_ref[pl.ds(r0, SCAN_SEGS), :]
        nr = ar * sr - ai * si + blk[:, :cb]
        ni = ar * si + ai * sr + blk[:, cb:]
        if o_ref is not None:
            o_ref[pl.ds(r0, SCAN_SEGS), :] = jnp.concatenate([nr, ni], axis=1)
        return nr, ni

    return lax.fori_loop(0, ngroups, step, state, unroll=4)


def _scan_ends(a_lay, bu, *, reverse, tt, name):
    rows, width = bu.shape
    cb = SCAN_CB
    nt = rows // tt

    def body(a_ref, bu_ref, e_ref):
        kk = pl.program_id(1)

        @pl.when(kk == 0)
        def _():
            e_ref[...] = jnp.zeros_like(e_ref)

        sr, si = _scan_groups(a_ref, bu_ref, None, (e_ref[:, :cb], e_ref[:, cb:]), reverse=reverse, tt=tt)
        e_ref[...] = jnp.concatenate([sr, si], axis=1)

    tmap = (lambda j, kk: (nt - 1 - kk, j)) if reverse else (lambda j, kk: (kk, j))
    return pl.pallas_call(
        body, name=name, grid=(width // (2 * cb), nt),
        in_specs=[pl.BlockSpec((1, 2 * cb), lambda j, kk: (0, j)), pl.BlockSpec((tt, 2 * cb), tmap)],
        out_specs=pl.BlockSpec((SCAN_SEGS, 2 * cb), lambda j, kk: (0, j)),
        out_shape=jax.ShapeDtypeStruct((SCAN_SEGS, width), F32),
        compiler_params=_params(("parallel", "arbitrary")),
    )(a_lay, bu)


def _scan_apply(a_lay, bu, ends, *, reverse, tt, name):
    rows, width = bu.shape
    cb = SCAN_CB
    nt = rows // tt
    seg_len = rows // SCAN_SEGS
    n_sq = seg_len.bit_length() - 1
    assert 1 << n_sq == seg_len, seg_len

    def body(a_ref, e_ref, bu_ref, o_ref, init_ref, carry_ref):
        kk = pl.program_id(1)

        @pl.when(kk == 0)
        def _():
            pr, pi = a_ref[:, :cb], a_ref[:, cb:]
            for _ in range(n_sq):
                pr, pi = pr * pr - pi * pi, 2.0 * pr * pi
            cr = jnp.zeros((1, cb), F32)
            ci = jnp.zeros((1, cb), F32)
            order = range(SCAN_SEGS - 1, -1, -1) if reverse else range(SCAN_SEGS)
            for k, seg in enumerate(order):
                if k > 0:
                    prev = seg + 1 if reverse else seg - 1
                    er, ei = e_ref[prev:prev + 1, :cb], e_ref[prev:prev + 1, cb:]
                    cr, ci = pr * cr - pi * ci + er, pr * ci + pi * cr + ei
                init_ref[seg:seg + 1, :] = jnp.concatenate([cr, ci], axis=1)
            carry_ref[...] = init_ref[...]

        sr, si = _scan_groups(a_ref, bu_ref, o_ref, (carry_ref[:, :cb], carry_ref[:, cb:]), reverse=reverse, tt=tt)
        carry_ref[...] = jnp.concatenate([sr, si], axis=1)

    tmap = (lambda j, kk: (nt - 1 - kk, j)) if reverse else (lambda j, kk: (kk, j))
    seg_spec = pl.BlockSpec((SCAN_SEGS, 2 * cb), lambda j, kk: (0, j))
    return pl.pallas_call(
        body, name=name, grid=(width // (2 * cb), nt),
        in_specs=[pl.BlockSpec((1, 2 * cb), lambda j, kk: (0, j)), seg_spec, pl.BlockSpec((tt, 2 * cb), tmap)],
        out_specs=[pl.BlockSpec((tt, 2 * cb), tmap), seg_spec],
        out_shape=[jax.ShapeDtypeStruct((rows, width), F32), jax.ShapeDtypeStruct((SCAN_SEGS, width), F32)],
        scratch_shapes=[pltpu.VMEM((SCAN_SEGS, 2 * cb), F32)],
        compiler_params=_params(("parallel", "arbitrary")),
    )(a_lay, ends, bu)


def _ssm_da(lam, s, entry, *, tt, name):
    rows, width = s.shape
    cb = SCAN_CB
    ngroups = tt // SCAN_SEGS

    def body(lam_ref, s_ref, entry_ref, da_ref, prev_ref):
        kk = pl.program_id(1)

        @pl.when(kk == 0)
        def _():
            prev_ref[...] = entry_ref[...]
            da_ref[...] = jnp.zeros_like(da_ref)

        def step(i, st):
            accr, acci, pr, pi = st
            r0 = pl.multiple_of(i * SCAN_SEGS, SCAN_SEGS)
            lv = lam_ref[pl.ds(r0, SCAN_SEGS), :]
            sv = s_ref[pl.ds(r0, SCAN_SEGS), :]
            lr, li = lv[:, :cb], lv[:, cb:]
            return accr + (lr * pr + li * pi), acci + (li * pr - lr * pi), sv[:, :cb], sv[:, cb:]

        zero = jnp.zeros((SCAN_SEGS, cb), F32)
        accr, acci, pr, pi = lax.fori_loop(0, ngroups, step, (zero, zero, prev_ref[:, :cb], prev_ref[:, cb:]), unroll=4)
        prev_ref[...] = jnp.concatenate([pr, pi], axis=1)
        da_ref[...] += jnp.concatenate([jnp.sum(accr, axis=0, keepdims=True), jnp.sum(acci, axis=0, keepdims=True)], axis=1)

    blk = pl.BlockSpec((tt, 2 * cb), lambda j, kk: (kk, j))
    return pl.pallas_call(
        body, name=name, grid=(width // (2 * cb), rows // tt),
        in_specs=[blk, blk, pl.BlockSpec((SCAN_SEGS, 2 * cb), lambda j, kk: (0, j))],
        out_specs=pl.BlockSpec((1, 2 * cb), lambda j, kk: (0, j)),
        out_shape=jax.ShapeDtypeStruct((1, width), F32),
        scratch_shapes=[pltpu.VMEM((SCAN_SEGS, 2 * cb), F32)],
        compiler_params=_params(("parallel", "arbitrary")),
    )(lam, s, entry)


def _glu_fwd(ys, u, dd, w_glu, b_glu, *, tm, name):
    rows, w = ys.shape

    def body(ys_ref, u_ref, dd_ref, w_ref, b_ref, y0_ref, t_ref, y2_ref):
        y0 = ys_ref[...] + dd_ref[...] * u_ref[...]
        y1 = _gelu(y0)
        t = _dot(y1.astype(BF16), w_ref[...], 1, 0) + b_ref[...]
        y0_ref[...] = y0
        t_ref[...] = t
        y2_ref[...] = (y1 * _sigmoid(t)).astype(BF16)

    row = pl.BlockSpec((tm, w), lambda i: (i, 0))
    vec = pl.BlockSpec((1, w), lambda i: (0, 0))
    return pl.pallas_call(
        body, name=name, grid=(rows // tm,),
        in_specs=[row, row, vec, pl.BlockSpec((w, w), lambda i: (0, 0)), vec],
        out_specs=[row, row, row],
        out_shape=[jax.ShapeDtypeStruct((rows, w), F32), jax.ShapeDtypeStruct((rows, w), F32),
                   jax.ShapeDtypeStruct((rows, w), BF16)],
        compiler_params=_params(("parallel",)),
    )(ys, u, dd, w_glu, b_glu)


def _glu_bwd(dy2, y0, t, u, w_glu, *, tm, name):
    rows, w = y0.shape

    def body(dy2_ref, y0_ref, t_ref, u_ref, w_ref, dy0_ref, dt_ref, y1_ref, db_ref, dd_ref):
        i = pl.program_id(0)
        y0 = y0_ref[...]
        y1 = _gelu(y0)
        sg = _sigmoid(t_ref[...])
        dy2v = dy2_ref[...]
        dt = dy2v * y1 * sg * (1.0 - sg)
        dy1 = dy2v * sg + _dot(dt.astype(BF16), w_ref[...], 1, 1)
        dy0 = dy1 * _gelu_grad(y0)
        dy0_ref[...] = dy0
        dt_ref[...] = dt.astype(BF16)
        y1_ref[...] = y1.astype(BF16)

        @pl.when(i == 0)
        def _():
            db_ref[...] = jnp.zeros_like(db_ref)
            dd_ref[...] = jnp.zeros_like(dd_ref)

        db_ref[...] += jnp.sum(dt, axis=0, keepdims=True)
        dd_ref[...] += jnp.sum(dy0 * u_ref[...], axis=0, keepdims=True)

    row = pl.BlockSpec((tm, w), lambda i: (i, 0))
    vec = pl.BlockSpec((1, w), lambda i: (0, 0))
    return pl.pallas_call(
        body, name=name, grid=(rows // tm,),
        in_specs=[row, row, row, row, pl.BlockSpec((w, w), lambda i: (0, 0))],
        out_specs=[row, row, row, vec, vec],
        out_shape=[jax.ShapeDtypeStruct((rows, w), F32), jax.ShapeDtypeStruct((rows, w), BF16),
                   jax.ShapeDtypeStruct((rows, w), BF16), jax.ShapeDtypeStruct((1, w), F32),
                   jax.ShapeDtypeStruct((1, w), F32)],
        compiler_params=_params(("arbitrary",)),
    )(dy2, y0, t, u, w_glu)


def _attn_masks(n):
    qi = lax.broadcasted_iota(jnp.int32, (ATTN_Q, ATTN_Q), 0)
    kj = lax.broadcasted_iota(jnp.int32, (ATTN_Q, ATTN_Q), 1)
    head0 = kj < ATTN_HEAD_DIM
    return kj <= qi, jnp.logical_and(kj >= qi, n > 0), head0


def _rows(r, d):
    return pl.ds(0, ATTN_Q) if d == 1 else pl.ds(r, ATTN_Q, stride=d)


def _attn_fwd(qkv, g, d, *, name):
    rows = qkv.shape[0]
    sb = ATTN_Q * d
    nsb = rows // sb
    qc, kc, vc = 2 * g, 6 + 2 * g, 12 + 2 * g
    scale = ATTN_HEAD_DIM ** -0.5

    def body(q_ref, kc_ref, kp_ref, vc_ref, vp_ref, o_ref, lse_ref):
        n = pl.program_id(0)
        mask_c, mask_p, head0 = _attn_masks(n)

        def per_residue(r, carry):
            idx = _rows(r, d)
            q = q_ref[idx, :]
            k_c = kc_ref[idx, :].astype(BF16)
            k_p = kp_ref[idx, :].astype(BF16)
            v_c = vc_ref[idx, :].astype(BF16)
            v_p = vp_ref[idx, :].astype(BF16)
            o_h, lse_h = [], []
            for h in range(2):
                hm = head0 if h == 0 else jnp.logical_not(head0)
                qh = jnp.where(hm, q, 0.0).astype(BF16)
                sc = jnp.where(mask_c, _dot(qh, k_c, 1, 1) * scale, NEG_INF)
                sp = jnp.where(mask_p, _dot(qh, k_p, 1, 1) * scale, NEG_INF)
                mx = jnp.maximum(jnp.max(sc, axis=-1, keepdims=True), jnp.max(sp, axis=-1, keepdims=True))
                pc = jnp.exp(sc - mx)
                pp = jnp.exp(sp - mx)
                den = jnp.sum(pc, axis=-1, keepdims=True) + jnp.sum(pp, axis=-1, keepdims=True)
                o_h.append((_dot(pc.astype(BF16), v_c, 1, 0) + _dot(pp.astype(BF16), v_p, 1, 0)) / den)
                lse_h.append(jnp.broadcast_to(mx + jnp.log(den), (ATTN_Q, LANES)))
            o_ref[idx, :] = jnp.where(head0, o_h[0], o_h[1])
            lse_ref[idx, :] = jnp.where(head0, lse_h[0], lse_h[1])
            return carry

        lax.fori_loop(0, d, per_residue, 0)

    def spec(col, prev):
        if prev:
            return pl.BlockSpec((sb, LANES), lambda n, hp: (jnp.maximum(n - 1, 0), col + hp))
        return pl.BlockSpec((sb, LANES), lambda n, hp: (n, col + hp))

    out_spec = pl.BlockSpec((sb, LANES), lambda n, hp: (n, hp))
    return pl.pallas_call(
        body, name=name, grid=(nsb, 2),
        in_specs=[spec(qc, False), spec(kc, False), spec(kc, True), spec(vc, False), spec(vc, True)],
        out_specs=[out_spec, out_spec],
        out_shape=[jax.ShapeDtypeStruct((rows, 2 * LANES), F32), jax.ShapeDtypeStruct((rows, 2 * LANES), F32)],
        compiler_params=_params(("parallel", "parallel")),
    )(qkv, qkv, qkv, qkv, qkv)


def _attn_merge(outs, lses, *, tm, name):
    rows, w = outs[0].shape

    def body(o0, o1, o2, l0, l1, l2, o_ref, lse_ref):
        a0, a1, a2 = l0[...], l1[...], l2[...]
        mx = jnp.maximum(jnp.maximum(a0, a1), a2)
        e0, e1, e2 = jnp.exp(a0 - mx), jnp.exp(a1 - mx), jnp.exp(a2 - mx)
        den = e0 + e1 + e2
        o_ref[...] = (e0 / den) * o0[...] + (e1 / den) * o1[...] + (e2 / den) * o2[...]
        lse_ref[...] = mx + jnp.log(den)

    row = pl.BlockSpec((tm, w), lambda i: (i, 0))
    return pl.pallas_call(
        body, name=name, grid=(rows // tm,), in_specs=[row] * 6, out_specs=[row, row],
        out_shape=[jax.ShapeDtypeStruct((rows, w), F32), jax.ShapeDtypeStruct((rows, w), F32)],
        compiler_params=_params(("parallel",)),
    )(*outs, *lses)


def _attn_bwd(qkv, do, o, lse, g, d, prev, *, name):
    rows = qkv.shape[0]
    sb = ATTN_Q * d
    nsb = rows // sb
    qc, kc, vc = 2 * g, 6 + 2 * g, 12 + 2 * g
    scale = ATTN_HEAD_DIM ** -0.5

    def body(q_ref, kc_ref, kp_ref, vc_ref, vp_ref, do_ref, o_ref, lse_ref, dq_ref, dk_ref, dv_ref, ck_ref, cv_ref):
        n = pl.program_id(1)

        @pl.when(n == 0)
        def _():
            ck_ref[...] = jnp.zeros_like(ck_ref)
            cv_ref[...] = jnp.zeros_like(cv_ref)

        @pl.when(n < nsb)
        def _():
            mask_c, mask_p, head0 = _attn_masks(n)
            lane = lax.broadcasted_iota(jnp.int32, (ATTN_Q, LANES), 1)

            def per_residue(r, carry):
                idx = _rows(r, d)
                q = q_ref[idx, :]
                k_c = kc_ref[idx, :].astype(BF16)
                k_p = kp_ref[idx, :].astype(BF16)
                v_c = vc_ref[idx, :].astype(BF16)
                v_p = vp_ref[idx, :].astype(BF16)
                dov = do_ref[idx, :]
                ov = o_ref[idx, :]
                lsev = lse_ref[idx, :]
                dq = jnp.zeros((ATTN_Q, LANES), F32)
                dkc = jnp.zeros((ATTN_Q, LANES), F32)
                dkp = jnp.zeros((ATTN_Q, LANES), F32)
                dvc = jnp.zeros((ATTN_Q, LANES), F32)
                dvp = jnp.zeros((ATTN_Q, LANES), F32)
                for h in range(2):
                    hm = head0 if h == 0 else jnp.logical_not(head0)
                    qh = jnp.where(hm, q, 0.0).astype(BF16)
                    doh = jnp.where(hm, dov, 0.0)
                    dohb = doh.astype(BF16)
                    delta = jnp.sum(doh * ov, axis=-1, keepdims=True)
                    lse_h = jnp.sum(jnp.where(lane == h * ATTN_HEAD_DIM, lsev, 0.0), axis=-1, keepdims=True)
                    sc = jnp.where(mask_c, _dot(qh, k_c, 1, 1) * scale, NEG_INF)
                    sp = jnp.where(mask_p, _dot(qh, k_p, 1, 1) * scale, NEG_INF)
                    pc = jnp.exp(sc - lse_h)
                    pp = jnp.exp(sp - lse_h)
                    dsc = (pc * (_dot(dohb, v_c, 1, 1) - delta) * scale).astype(BF16)
                    dsp = (pp * (_dot(dohb, v_p, 1, 1) - delta) * scale).astype(BF16)
                    dq = dq + jnp.where(hm, _dot(dsc, k_c, 1, 0) + _dot(dsp, k_p, 1, 0), 0.0)
                    dkc = dkc + _dot(dsc, qh, 0, 0)
                    dkp = dkp + _dot(dsp, qh, 0, 0)
                    dvc = dvc + _dot(pc.astype(BF16), dohb, 0, 0)
                    dvp = dvp + _dot(pp.astype(BF16), dohb, 0, 0)
                dq_ref[idx, :] = dq
                dk_ref[idx, :] = ck_ref[idx, :] + dkp
                dv_ref[idx, :] = cv_ref[idx, :] + dvp
                ck_ref[idx, :] = dkc
                cv_ref[idx, :] = dvc
                return carry

            lax.fori_loop(0, d, per_residue, 0)

        @pl.when(n == nsb)
        def _():
            dk_ref[...] = ck_ref[...]
            dv_ref[...] = cv_ref[...]

    def cur(n):
        return jnp.minimum(n, nsb - 1)

    def spec(col, prev):
        if prev:
            return pl.BlockSpec((sb, LANES), lambda hp, n: (jnp.maximum(cur(n) - 1, 0), col + hp))
        return pl.BlockSpec((sb, LANES), lambda hp, n: (cur(n), col + hp))

    row_spec = pl.BlockSpec((sb, LANES), lambda hp, n: (cur(n), hp))
    dq_out = pl.BlockSpec((sb, LANES), lambda hp, n: (cur(n), 2 * g + hp))
    kv_out = pl.BlockSpec((sb, LANES), lambda hp, n: (jnp.maximum(n - 1, 0), 2 * g + hp))
    shape = jax.ShapeDtypeStruct((rows, len(ATTN_PATTERNS) * 2 * LANES), F32)
    ins = [qkv, qkv, qkv, qkv, qkv, do, o, lse]
    in_specs = [spec(qc, False), spec(kc, False), spec(kc, True), spec(vc, False), spec(vc, True),
                row_spec, row_spec, row_spec]
    aliases = {}
    if prev is not None:
        aliases = {len(ins) + t: t for t in range(3)}
        ins = ins + list(prev)
        in_specs = in_specs + [ANY] * 3
    n_in = len(ins)

    def entry(*refs):
        body(*refs[:8], *refs[n_in:])

    return pl.pallas_call(
        entry, name=name, grid=(2, nsb + 1),
        in_specs=in_specs,
        out_specs=[dq_out, kv_out, kv_out],
        out_shape=[shape, shape, shape],
        input_output_aliases=aliases,
        scratch_shapes=[pltpu.VMEM((sb, LANES), F32), pltpu.VMEM((sb, LANES), F32)],
        compiler_params=_params(("parallel", "arbitrary")),
    )(*ins)


def _mem_probs(q, k):
    s = _dot(q.astype(BF16), k.astype(BF16), 1, 1) * (MEM_HEAD_DIM ** -0.5)
    e = jnp.exp(s - jnp.max(s, axis=-1, keepdims=True))
    return e / jnp.sum(e, axis=-1, keepdims=True)


def _mem_attn_fwd(mq, kv, *, tq, name):
    rows = mq.shape[0]

    def body(q_ref, k_ref, v_ref, o_ref):
        p = _mem_probs(q_ref[...], k_ref[...])
        o_ref[...] = _dot(p.astype(BF16), v_ref[...].astype(BF16), 1, 0)

    return pl.pallas_call(
        body, name=name, grid=(rows // tq, MEM_HEADS),
        in_specs=[pl.BlockSpec((tq, LANES), lambda i, h: (i, h)),
                  pl.BlockSpec((MEM_LEN, LANES), lambda i, h: (0, h)),
                  pl.BlockSpec((MEM_LEN, LANES), lambda i, h: (0, MEM_HEADS + h))],
        out_specs=pl.BlockSpec((tq, LANES), lambda i, h: (i, h)),
        out_shape=jax.ShapeDtypeStruct((rows, MEM_HEADS * LANES), F32),
        compiler_params=_params(("parallel", "parallel")),
    )(mq, kv, kv)


def _mem_attn_bwd(mq, kv, dmo, *, tq, name):
    rows = mq.shape[0]
    scale = MEM_HEAD_DIM ** -0.5

    def body(q_ref, k_ref, v_ref, do_ref, dq_ref, dk_ref, dv_ref):
        i = pl.program_id(1)
        qb = q_ref[...].astype(BF16)
        kb = k_ref[...].astype(BF16)
        vb = v_ref[...].astype(BF16)
        dob = do_ref[...].astype(BF16)
        p = _mem_probs(q_ref[...], k_ref[...])
        dp = _dot(dob, vb, 1, 1)
        ds = (p * (dp - jnp.sum(p * dp, axis=-1, keepdims=True)) * scale).astype(BF16)
        dq_ref[...] = _dot(ds, kb, 1, 0)

        @pl.when(i == 0)
        def _():
            dk_ref[...] = jnp.zeros_like(dk_ref)
            dv_ref[...] = jnp.zeros_like(dv_ref)

        dk_ref[...] += _dot(ds, qb, 0, 0)
        dv_ref[...] += _dot(p.astype(BF16), dob, 0, 0)

    kv_out = pl.BlockSpec((MEM_LEN, LANES), lambda h, i: (0, h))
    kv_shape = jax.ShapeDtypeStruct((MEM_LEN, MEM_HEADS * LANES), F32)
    return pl.pallas_call(
        body, name=name, grid=(MEM_HEADS, rows // tq),
        in_specs=[pl.BlockSpec((tq, LANES), lambda h, i: (i, h)),
                  pl.BlockSpec((MEM_LEN, LANES), lambda h, i: (0, h)),
                  pl.BlockSpec((MEM_LEN, LANES), lambda h, i: (0, MEM_HEADS + h)),
                  pl.BlockSpec((tq, LANES), lambda h, i: (i, h))],
        out_specs=[pl.BlockSpec((tq, LANES), lambda h, i: (i, h)), kv_out, kv_out],
        out_shape=[jax.ShapeDtypeStruct((rows, MEM_HEADS * LANES), F32), kv_shape, kv_shape],
        compiler_params=_params(("parallel", "arbitrary")),
    )(mq, kv, kv, dmo)


def _gate_merge_fwd(zg, b_gate, br_s, br_a, br_m, *, tm, name):
    rows, d = br_s.shape

    def body(zg_ref, b_ref, s_ref, a_ref, m_ref, o_ref):
        gt = _sigmoid(zg_ref[...] + b_ref[...])
        o_ref[...] = (gt[:, :d] * s_ref[...] + gt[:, d:2 * d] * a_ref[...] + gt[:, 2 * d:] * m_ref[...]).astype(BF16)

    row = pl.BlockSpec((tm, d), lambda i: (i, 0))
    return pl.pallas_call(
        body, name=name, grid=(rows // tm,),
        in_specs=[pl.BlockSpec((tm, 3 * d), lambda i: (i, 0)), pl.BlockSpec((1, 3 * d), lambda i: (0, 0)), row, row, row],
        out_specs=row, out_shape=jax.ShapeDtypeStruct((rows, d), BF16),
        compiler_params=_params(("parallel",)),
    )(zg, b_gate, br_s, br_a, br_m)


def _gate_merge_bwd(dmerged, zg, b_gate, br_s, br_a, br_m, *, tm, name):
    rows, d = br_s.shape

    def body(dm_ref, zg_ref, b_ref, s_ref, a_ref, m_ref, ds_ref, da_ref, dmm_ref, dzg_ref, db_ref):
        i = pl.program_id(0)
        gt = _sigmoid(zg_ref[...] + b_ref[...])
        dm = dm_ref[...]
        parts = []
        for j, (br_ref, out_ref) in enumerate(((s_ref, ds_ref), (a_ref, da_ref), (m_ref, dmm_ref))):
            gj = gt[:, j * d:(j + 1) * d]
            out_ref[...] = (dm * gj).astype(BF16)
            parts.append(dm * br_ref[...] * gj * (1.0 - gj))
        dzg = jnp.concatenate(parts, axis=1)
        dzg_ref[...] = dzg

        @pl.when(i == 0)
        def _():
            db_ref[...] = jnp.zeros_like(db_ref)

        db_ref[...] += jnp.sum(dzg, axis=0, keepdims=True)

    row = pl.BlockSpec((tm, d), lambda i: (i, 0))
    wide = pl.BlockSpec((tm, 3 * d), lambda i: (i, 0))
    vec = pl.BlockSpec((1, 3 * d), lambda i: (0, 0))
    bshape = jax.ShapeDtypeStruct((rows, d), BF16)
    return pl.pallas_call(
        body, name=name, grid=(rows // tm,),
        in_specs=[row, wide, vec, row, row, row],
        out_specs=[row, row, row, wide, vec],
        out_shape=[bshape, bshape, bshape, jax.ShapeDtypeStruct((rows, 3 * d), F32), jax.ShapeDtypeStruct((1, 3 * d), F32)],
        compiler_params=_params(("arbitrary",)),
    )(dmerged, zg, b_gate, br_s, br_a, br_m)


def _adamw(w, g, m, v, *, tr, name):
    rows, cols = w.shape
    assert rows % tr == 0, (name, rows, tr)

    def body(w_ref, g_ref, m_ref, v_ref, d_ref, nm_ref, nv_ref):
        gv = g_ref[...]
        m2 = ADAM_B1 * m_ref[...] + (1.0 - ADAM_B1) * gv
        v2 = ADAM_B2 * v_ref[...] + (1.0 - ADAM_B2) * (gv * gv)
        m_hat = m2 / (1.0 - ADAM_B1 ** ADAM_STEP)
        v_hat = v2 / (1.0 - ADAM_B2 ** ADAM_STEP)
        d_ref[...] = -ADAM_LR * (m_hat / (jnp.sqrt(v_hat) + ADAM_EPS) + ADAM_WD * w_ref[...])
        nm_ref[...] = m2
        nv_ref[...] = v2

    blk = pl.BlockSpec((tr, cols), lambda i: (i, 0))
    shape = jax.ShapeDtypeStruct((rows, cols), F32)
    return pl.pallas_call(
        body, name=name, grid=(rows // tr,), in_specs=[blk] * 4, out_specs=[blk] * 3,
        out_shape=[shape, shape, shape], compiler_params=_params(("parallel",)),
    )(w, g, m, v)


ANY = pl.BlockSpec(memory_space=pl.ANY)


def _position():
    return lax.axis_index("x"), lax.axis_index("y"), lax.axis_index("c")


def _other_chips(x, y):
    return ((1 - x, y), (x, 1 - y), (1 - x, 1 - y))


def _remote(src, dst, send_sem, recv_sem, dev):
    return pltpu.make_async_remote_copy(src_ref=src, dst_ref=dst, send_sem=send_sem, recv_sem=recv_sem,
                                        device_id=dev, device_id_type=MESH)


def _all_gather_weights(bufs):
    nb = len(bufs)

    def body(*refs):
        outs = refs[nb:2 * nb]
        send_sems, recv_sems = refs[2 * nb:]
        x, y, c = _position()
        chip = 2 * x + y
        sibling = (x, y, 1 - c)
        chips = _other_chips(x, y)

        def rows_of(i, owner, core):
            rs = bufs[i].shape[0] // N_CHIPS
            return pl.ds(pl.multiple_of(owner * rs + core * (rs // 2), 16), rs // 2)

        sends = []
        for i in range(nb):
            mine = outs[i].at[rows_of(i, chip, c)]
            for j, (px, py) in enumerate(chips):
                cp = _remote(mine, mine, send_sems.at[i, j], recv_sems.at[i, j], (px, py, c))
                cp.start()
                sends.append(cp)
        for i in range(nb):
            for j, (px, py) in enumerate(chips):
                landed = outs[i].at[rows_of(i, 2 * px + py, c)]
                _remote(landed, landed, send_sems.at[i, j], recv_sems.at[i, j], (px, py, c)).wait_recv()
                cp = _remote(landed, landed, send_sems.at[i, 3 + j], recv_sems.at[i, 3 + j], sibling)
                cp.start()
                sends.append(cp)
        for i in range(nb):
            for j, (px, py) in enumerate(chips):
                dst = outs[i].at[rows_of(i, 2 * px + py, 1 - c)]
                _remote(dst, dst, send_sems.at[i, 3 + j], recv_sems.at[i, 3 + j], sibling).wait_recv()
        for cp in sends:
            cp.wait_send()

    return pl.pallas_call(
        body, name="all_gather_weights", in_specs=[ANY] * nb, out_specs=[ANY] * nb,
        out_shape=[jax.ShapeDtypeStruct(b.shape, b.dtype) for b in bufs],
        input_output_aliases={i: i for i in range(nb)},
        scratch_shapes=[pltpu.SemaphoreType.DMA((nb, 6)), pltpu.SemaphoreType.DMA((nb, 6))],
    )(*bufs)


def _row_tile(rows):
    return max(t for t in range(16, min(rows, 512) + 1, 16) if rows % t == 0)


def _exchange_halves(grads):
    nb = len(grads)

    def body(*refs):
        ins, outs = refs[:nb], refs[nb:2 * nb]
        send_sems, recv_sems = refs[2 * nb:]
        x, y, c = _position()
        copies = []
        for i in range(nb):
            cp = _remote(ins[i].at[:, 1 - c], outs[i], send_sems.at[i], recv_sems.at[i], (x, y, 1 - c))
            cp.start()
            copies.append(cp)
        for cp in copies:
            cp.wait()

    return pl.pallas_call(
        body, name="grad_exchange_halves", in_specs=[ANY] * nb, out_specs=[ANY] * nb,
        out_shape=[jax.ShapeDtypeStruct((N_CHIPS, g.shape[2], g.shape[3]), F32) for g in grads],
        scratch_shapes=[pltpu.SemaphoreType.DMA((nb,)), pltpu.SemaphoreType.DMA((nb,))],
    )(*grads)


def _pair_sum(g4, got, c_arr, *, name):
    _, _, half, cols = g4.shape
    tr = _row_tile(half)

    def body(c_ref, g_ref, t_ref, p_ref, pb_ref):
        sm = g_ref[...] + t_ref[...]
        p_ref[...] = sm
        pb_ref[...] = sm.astype(BF16)

    blk = pl.BlockSpec((None, tr, cols), lambda j, i, c_ref: (j, i, 0))
    grid_spec = pltpu.PrefetchScalarGridSpec(
        num_scalar_prefetch=1, grid=(N_CHIPS, half // tr),
        in_specs=[pl.BlockSpec((None, None, tr, cols), lambda j, i, c_ref: (j, c_ref[0], i, 0)), blk],
        out_specs=[blk, blk])
    return pl.pallas_call(
        body, name=name, grid_spec=grid_spec,
        out_shape=[jax.ShapeDtypeStruct((N_CHIPS, half, cols), F32), jax.ShapeDtypeStruct((N_CHIPS, half, cols), BF16)],
        compiler_params=_params(("parallel", "parallel")),
    )(c_arr, g4, got)


def _scatter_to_owners(parts):
    nb = len(parts)

    def body(*refs):
        ins, outs = refs[:nb], refs[nb:2 * nb]
        send_sems, recv_sems = refs[2 * nb:]
        x, y, c = _position()
        copies = []
        for i in range(nb):
            for j, (px, py) in enumerate(_other_chips(x, y)):
                cp = _remote(ins[i].at[2 * px + py], outs[i].at[j], send_sems.at[i, j], recv_sems.at[i, j], (px, py, c))
                cp.start()
                copies.append(cp)
        for cp in copies:
            cp.wait()

    return pl.pallas_call(
        body, name="grad_scatter_to_owners", in_specs=[ANY] * nb, out_specs=[ANY] * nb,
        out_shape=[jax.ShapeDtypeStruct((3,) + p.shape[1:], p.dtype) for p in parts],
        scratch_shapes=[pltpu.SemaphoreType.DMA((nb, 3)), pltpu.SemaphoreType.DMA((nb, 3))],
    )(*parts)


def _owner_sum(p, got, chip_arr, c_arr, *, replicated, name):
    _, half, cols = p.shape
    tr = _row_tile(half)

    def body(chip_ref, c_ref, p_ref, r_ref, o_ref):
        o_ref[...] = ((p_ref[...] + r_ref[0].astype(F32)) + r_ref[1].astype(F32)) + r_ref[2].astype(F32)

    if replicated:
        out_spec = pl.BlockSpec((None, None, tr, cols), lambda i, chip_ref, c_ref: (chip_ref[0], c_ref[0], i, 0))
        out_shape = jax.ShapeDtypeStruct((N_CHIPS, 2, half, cols), F32)
    else:
        out_spec = pl.BlockSpec((None, tr, cols), lambda i, chip_ref, c_ref: (c_ref[0], i, 0))
        out_shape = jax.ShapeDtypeStruct((2, half, cols), F32)
    grid_spec = pltpu.PrefetchScalarGridSpec(
        num_scalar_prefetch=2, grid=(half // tr,),
        in_specs=[pl.BlockSpec((None, tr, cols), lambda i, chip_ref, c_ref: (chip_ref[0], i, 0)),
                  pl.BlockSpec((3, tr, cols), lambda i, chip_ref, c_ref: (0, i, 0))],
        out_specs=out_spec)
    return pl.pallas_call(
        body, name=name, grid_spec=grid_spec, out_shape=out_shape,
        compiler_params=_params(("parallel",)),
    )(chip_arr, c_arr, p, got)


def _share_reduced(bufs):
    nb = len(bufs) - 1

    def body(*refs):
        outs = refs[nb + 1:2 * nb + 2]
        send_sems, recv_sems = refs[2 * nb + 2:]
        x, y, c = _position()
        chip = 2 * x + y
        sends = []
        for i in range(nb):
            cp = _remote(outs[i].at[c], outs[i].at[c], send_sems.at[i], recv_sems.at[i], (x, y, 1 - c))
            cp.start()
            sends.append(cp)
        small = outs[nb]
        peers = [(fx, fy, fc) for fx in (0, 1) for fy in (0, 1) for fc in (0, 1) if fx + fy + fc > 0]
        for k, (fx, fy, fc) in enumerate(peers):
            dev = (x ^ fx, y ^ fy, c ^ fc)
            cp = _remote(small.at[chip, c], small.at[chip, c], send_sems.at[nb + k], recv_sems.at[nb + k], dev)
            cp.start()
            sends.append(cp)
        for i in range(nb):
            dst = outs[i].at[1 - c]
            _remote(dst, dst, send_sems.at[i], recv_sems.at[i], (x, y, 1 - c)).wait_recv()
        for k, (fx, fy, fc) in enumerate(peers):
            dst = small.at[2 * (x ^ fx) + (y ^ fy), c ^ fc]
            _remote(dst, dst, send_sems.at[nb + k], recv_sems.at[nb + k], (x ^ fx, y ^ fy, c ^ fc)).wait_recv()
        for cp in sends:
            cp.wait_send()

    n_all = nb + 1
    return pl.pallas_call(
        body, name="grad_share_reduced", in_specs=[ANY] * n_all, out_specs=[ANY] * n_all,
        out_shape=[jax.ShapeDtypeStruct(b.shape, b.dtype) for b in bufs],
        input_output_aliases={i: i for i in range(n_all)},
        scratch_shapes=[pltpu.SemaphoreType.DMA((nb + 7,)), pltpu.SemaphoreType.DMA((nb + 7,))],
    )(*bufs)


def _pack_small(vals):
    flat = jnp.concatenate([vals[name].reshape(-1) for name, _ in SMALL])
    return jnp.pad(flat, (0, N_CHIPS * SMALL_ROWS * 1024 - SMALL_ELEMS)).reshape(N_CHIPS * SMALL_ROWS, 1024)


def _unpack_small(buf):
    flat = buf.reshape(-1)
    out, off = {}, 0
    for name, shape in SMALL:
        n = int(np.prod(shape))
        out[name] = flat[off:off + n].reshape(shape)
        off += n
    return out


def _device_step(x, mem, tgt, w, p):
    rows = x.shape[0]
    g1, gm, g2 = p["norm1_g"], p["mem_norm_g"], p["norm2_g"]
    gf = p["final_g"].reshape(1, D_MODEL)
    ssm_args = (p["ssm_lambda_re"][0], p["ssm_lambda_im"][0], p["ssm_log_dt"][0], p["ssm_b_re"][0],
                p["ssm_b_im"][0], p["ssm_c_re"][0], p["ssm_c_im"][0])
    (a_lay, b_lay, c_lay), ssm_vjp = jax.vjp(_ssm_matrices, *ssm_args)
    a_conj = a_lay * _to_scan_layout(jnp.stack([jnp.ones((N_STATES,), F32), -jnp.ones((N_STATES,), F32)]))[None, :]
    dd = p["ssm_d"].reshape(1, SSM_WIDTH)
    win_t = w["w_in"]
    mm = _matmul

    n1 = _rmsnorm_fwd(x, g1, tm=512, name="norm1")
    u = mm(n1, win_t, m=rows, n=512, k=1024, tb=True, tm=2048, tn=512, tk=1024, out_dtypes=(F32,), name="in_u")
    qkv = mm(n1, win_t, m=rows, n=2304, k=1024, tb=True, tm=2048, tn=256, tk=1024, b_off=(OFF_QKV // 256, 0),
             out_dtypes=(F32,), name="in_qkv")
    mq = mm(n1, win_t, m=rows, n=512, k=1024, tb=True, tm=2048, tn=256, tk=1024, b_off=(OFF_MQ // 256, 0),
            out_dtypes=(F32,), name="in_mq")
    zg = mm(n1, win_t, m=rows, n=3072, k=1024, tb=True, tm=2048, tn=256, tk=1024, b_off=(OFF_ZG // 256, 0),
            out_dtypes=(F32,), name="in_zg")

    u_i = _interleave(u)
    bu = mm(u_i, b_lay, m=rows, n=2 * N_STATES, k=512, tm=1024, tn=1024, tk=512, out_dtypes=(F32,), name="ssm_bu")
    ends = _scan_ends(a_lay, bu, reverse=False, tt=512, name="ssm_scan_fwd_ends")
    s, s_entry = _scan_apply(a_lay, bu, ends, reverse=False, tt=512, name="ssm_scan_fwd")
    ys = _deinterleave(mm(s, c_lay, m=rows, n=512, k=2 * N_STATES, tm=1024, tn=512, tk=1024, out_dtypes=(F32,), name="ssm_cs"))
    y0, tglu, y2 = _glu_fwd(ys, u, dd, w["w_glu"], p["b_glu"], tm=512, name="glu_fwd")
    br_s = mm(y2, w["w_ssm_br"], m=rows, n=1024, k=512, tb=True, tm=1024, tn=1024, tk=512, out_dtypes=(F32,), name="br_ssm")

    outs, lses = [], []
    for g, (_, d) in enumerate(ATTN_PATTERNS):
        o_g, lse_g = _attn_fwd(qkv, g, d, name=f"attn_fwd_{g}")
        outs.append(o_g)
        lses.append(lse_g)
    o, lse = _attn_merge(outs, lses, tm=1024, name="attn_merge")
    br_a = mm(o, w["w_attn_br"], m=rows, n=1024, k=256, tb=True, tm=1024, tn=1024, tk=256, out_dtypes=(F32,), name="br_attn")

    mn = _rmsnorm_fwd(mem, gm, tm=MEM_LEN, name="mem_norm")
    kv = mm(mn, w["w_mem_kv"], m=MEM_LEN, n=1024, k=1024, tm=MEM_LEN, tn=1024, tk=1024, out_dtypes=(F32,), name="mem_kv")
    mo = _mem_attn_fwd(mq, kv, tq=1024, name="mem_attn_fwd")
    br_m = mm(mo, w["w_mem_br"], m=rows, n=1024, k=512, tb=True, tm=1024, tn=1024, tk=512, out_dtypes=(F32,), name="br_mem")

    merged = _gate_merge_fwd(zg, p["b_gate"], br_s, br_a, br_m, tm=256, name="gate_merge_fwd")
    add = lambda acc, r: (acc + r,)
    h1 = mm(merged, w["w_o"], m=rows, n=1024, k=1024, tm=1024, tn=1024, tk=1024, out_dtypes=(F32,),
            aux=((x, "mn"),), epilogue=add, name="out_proj")
    n2 = _rmsnorm_fwd(h1, g2, tm=512, name="norm2")
    relu2 = lambda acc: (acc, jnp.square(jnp.maximum(acc, 0.0)))
    up, act = mm(n2, w["w_up"], m=rows, n=D_FF, k=1024, tb=True, tm=1024, tn=1024, tk=1024, out_dtypes=(F32, BF16),
                 epilogue=relu2, name="mlp_up")
    h2 = mm(act, w["w_down"], m=rows, n=1024, k=D_FF, tm=1024, tn=1024, tk=1024, out_dtypes=(F32,),
            aux=((h1, "mn"),), epilogue=add, name="mlp_down")
    dh2, loss, d_gf = _loss_head(h2, tgt, gf, tm=512, name="loss_head")

    gb = {}
    gs = {"final_g": d_gf.reshape(D_MODEL)}
    drelu2 = lambda acc, upv: (acc * (2.0 * jnp.maximum(upv, 0.0)),)
    dup = mm(dh2, w["w_down"], m=rows, n=D_FF, k=1024, tb=True, tm=1024, tn=1024, tk=1024, out_dtypes=(BF16,),
             aux=((up, "mn"),), epilogue=drelu2, name="d_act")
    gb["w_down"] = mm(act, dh2, m=D_FF, n=1024, k=rows, ta=True, tm=1024, tn=1024, tk=1024, out_dtypes=(F32,), name="dw_down")
    dn2 = mm(dup, w["w_up"], m=rows, n=1024, k=D_FF, tm=1024, tn=1024, tk=1024, out_dtypes=(F32,), name="d_n2")
    gb["w_up"] = mm(dup, n2, m=D_FF, n=1024, k=rows, ta=True, tm=1024, tn=1024, tk=1024, out_dtypes=(F32,), name="dw_up")
    dh1, gs["norm2_g"] = _rmsnorm_bwd(h1, g2, dn2, dh2, tm=512, name="norm2_bwd")
    dmerged = mm(dh1, w["w_o"], m=rows, n=1024, k=1024, tb=True, tm=1024, tn=1024, tk=1024, out_dtypes=(F32,), name="d_merged")
    gb["w_o"] = mm(merged, dh1, m=1024, n=1024, k=rows, ta=True, tm=1024, tn=1024, tk=1024, out_dtypes=(F32,), name="dw_o")
    dbr_s, dbr_a, dbr_m, dzg, gs["b_gate"] = _gate_merge_bwd(dmerged, zg, p["b_gate"], br_s, br_a, br_m, tm=256,
                                                              name="gate_merge_bwd")

    dy2 = mm(dbr_s, w["w_ssm_br"], m=rows, n=512, k=1024, tm=1024, tn=512, tk=1024, out_dtypes=(F32,), name="d_y2")
    gb["w_ssm_br"] = mm(dbr_s, y2, m=1024, n=512, k=rows, ta=True, tm=1024, tn=512, tk=1024, out_dtypes=(F32,), name="dw_ssm_br")
    dy0, dt, y1, gs["b_glu"], d_dd = _glu_bwd(dy2, y0, tglu, u, w["w_glu"], tm=512, name="glu_bwd")
    gs["ssm_d"] = d_dd.reshape(1, SSM_GROUPS, SSM_GROUP_SIZE)
    gb["w_glu"] = mm(y1, dt, m=512, n=512, k=rows, ta=True, tm=512, tn=512, tk=1024, out_dtypes=(F32,), name="dw_glu")
    dy0_i = _interleave(dy0)
    dsout = mm(dy0_i, c_lay, m=rows, n=2 * N_STATES, k=512, tb=True, tm=1024, tn=1024, tk=512, out_dtypes=(F32,), name="ssm_dsout")
    lam_ends = _scan_ends(a_conj, dsout, reverse=True, tt=512, name="ssm_scan_bwd_ends")
    lam, _ = _scan_apply(a_conj, dsout, lam_ends, reverse=True, tt=512, name="ssm_scan_bwd")
    skip = lambda acc, dyv, ddv: (acc + ddv * dyv,)
    du = _deinterleave(mm(lam, b_lay, m=rows, n=512, k=2 * N_STATES, tb=True, tm=1024, tn=512, tk=1024, out_dtypes=(F32,),
                          aux=((dy0_i, "mn"), (dd, "row")), epilogue=skip, name="ssm_du"))
    d_b_lay = mm(u_i, lam, m=512, n=2 * N_STATES, k=rows, ta=True, tm=512, tn=1024, tk=1024, out_dtypes=(F32,), name="ssm_db")
    d_c_lay = mm(s, dy0_i, m=2 * N_STATES, n=512, k=rows, ta=True, tm=1024, tn=512, tk=1024, out_dtypes=(F32,), name="ssm_dc")
    d_a_lay = _ssm_da(lam, s, s_entry, tt=512, name="ssm_da")
    d_ssm = ssm_vjp((d_a_lay, d_b_lay, d_c_lay))
    for name, val in zip(("ssm_lambda_re", "ssm_lambda_im", "ssm_log_dt", "ssm_b_re", "ssm_b_im", "ssm_c_re", "ssm_c_im"), d_ssm):
        gs[name] = val[None]

    do = mm(dbr_a, w["w_attn_br"], m=rows, n=256, k=1024, tm=1024, tn=256, tk=1024, out_dtypes=(F32,), name="d_o")
    gb["w_attn_br"] = mm(dbr_a, o, m=1024, n=256, k=rows, ta=True, tm=1024, tn=256, tk=1024, out_dtypes=(F32,), name="dw_attn_br")
    dqkv = None
    for g, (_, d) in enumerate(ATTN_PATTERNS):
        dqkv = _attn_bwd(qkv, do, o, lse, g, d, dqkv, name=f"attn_bwd_{g}")

    dmo = mm(dbr_m, w["w_mem_br"], m=rows, n=512, k=1024, tm=1024, tn=512, tk=1024, out_dtypes=(F32,), name="d_mo")
    gb["w_mem_br"] = mm(dbr_m, mo, m=1024, n=512, k=rows, ta=True, tm=1024, tn=512, tk=1024, out_dtypes=(F32,), name="dw_mem_br")
    dmq, dmk, dmv = _mem_attn_bwd(mq, kv, dmo, tq=1024, name="mem_attn_bwd")
    dkv = jnp.concatenate([dmk, dmv], axis=1)
    gb["w_mem_kv"] = mm(mn, dkv, m=1024, n=1024, k=MEM_LEN, ta=True, tm=1024, tn=1024, tk=MEM_LEN, out_dtypes=(F32,), name="dw_mem_kv")
    dmn = mm(dkv, w["w_mem_kv"], m=MEM_LEN, n=1024, k=1024, tb=True, tm=MEM_LEN, tn=1024, tk=1024, out_dtypes=(F32,), name="d_mn")
    _, gs["mem_norm_g"] = _rmsnorm_bwd(mem, gm, dmn, None, tm=MEM_LEN, name="mem_norm_bwd")

    pieces = ((du, OFF_U, "u"), (dqkv[0], OFF_QKV, "q"), (dqkv[1], OFF_QKV + 768, "k"), (dqkv[2], OFF_QKV + 1536, "v"),
              (dmq, OFF_MQ, "mq"), (dzg, OFF_ZG, "zg"))
    dn = None
    dw_rows = []
    for piece, off, tag in pieces:
        width = piece.shape[1]
        tk = 1024 if (width % 1024 == 0 and off % 1024 == 0) else 256
        dn = mm(piece, win_t, m=rows, n=1024, k=width, tm=1024, tn=1024, tk=tk, b_off=(off // tk, 0), out_dtypes=(F32,),
                aux=() if dn is None else ((dn, "mn"),), epilogue=None if dn is None else add, name="d_n1_" + tag)
        tmw = 1024 if width % 1024 == 0 else (768 if width == 768 else 512)
        dw_rows.append(mm(piece, n1, m=width, n=1024, k=rows, ta=True, tm=tmw, tn=1024, tk=1024, out_dtypes=(F32,),
                          name="dw_in_" + tag))
    gb["w_in"] = jnp.concatenate(dw_rows, axis=0)
    dx, gs["norm1_g"] = _rmsnorm_bwd(x, g1, dn, dh1, tm=512, name="norm1_bwd")
    return loss, dx, gb, gs


def kernel(x, mem, norm1_g, mem_norm_g, w_in, b_gate, ssm_lambda_re, ssm_lambda_im, ssm_log_dt, ssm_b_re, ssm_b_im, ssm_c_re, ssm_c_im, ssm_d, w_glu, b_glu, w_ssm_br, w_attn_br, w_mem_kv, w_mem_br, w_o, norm2_g, w_up, w_down, final_g, loss_target, m_norm1_g, m_mem_norm_g, m_w_in, m_b_gate, m_ssm_lambda_re, m_ssm_lambda_im, m_ssm_log_dt, m_ssm_b_re, m_ssm_b_im, m_ssm_c_re, m_ssm_c_im, m_ssm_d, m_w_glu, m_b_glu, m_w_ssm_br, m_w_attn_br, m_w_mem_kv, m_w_mem_br, m_w_o, m_norm2_g, m_w_up, m_w_down, m_final_g, v_norm1_g, v_mem_norm_g, v_w_in, v_b_gate, v_ssm_lambda_re, v_ssm_lambda_im, v_ssm_log_dt, v_ssm_b_re, v_ssm_b_im, v_ssm_c_re, v_ssm_c_im, v_ssm_d, v_w_glu, v_b_glu, v_w_ssm_br, v_w_attn_br, v_w_mem_kv, v_w_mem_br, v_w_o, v_norm2_g, v_w_up, v_w_down, v_final_g):
    env = dict(locals())
    weights = {n: env[n] for n in WEIGHT_ORDER}
    moms = {n: env["m_" + n] for n in WEIGHT_ORDER}
    vels = {n: env["v_" + n] for n in WEIGHT_ORDER}
    def shard2d(a):
        return a.reshape(a.shape[-2], a.shape[-1])

    chip = 2 * lax.axis_index("x") + lax.axis_index("y")
    wire = [shard2d(weights[n]).astype(BF16) for n, _, _ in BIG]
    wire = [s.T if tr else s for s, (_, tr, _) in zip(wire, BIG)]
    wire = [lax.dynamic_update_slice(lax.empty((N_CHIPS * s.shape[0], s.shape[1]), BF16), s, (chip * s.shape[0], 0))
            for s in wire]
    w_full = dict(zip([n for n, _, _ in BIG], _all_gather_weights(wire)))
    small = {n: weights[n] for n, _ in SMALL}

    loss, dx, gb, gs = _device_step(x[0], mem[0], loss_target[0], w_full, small)

    c_arr = lax.axis_index("c").astype(jnp.int32).reshape(1)
    chip_arr = chip.astype(jnp.int32).reshape(1)
    names = [n for n, _, _ in BIG] + ["small"]
    full = [gb[n] for n, _, _ in BIG] + [_pack_small(gs)]
    full = [g.reshape(N_CHIPS, 2, g.shape[0] // (2 * N_CHIPS), g.shape[1]) for g in full]
    from_sibling = _exchange_halves(full)
    pairs = [_pair_sum(g, t, c_arr, name="grad_pair_sum_" + n) for g, t, n in zip(full, from_sibling, names)]
    landed = _scatter_to_owners([pb for _, pb in pairs])
    totals = [_owner_sum(pf, r, chip_arr, c_arr, replicated=(n == "small"), name="grad_owner_sum_" + n)
              for (pf, _), r, n in zip(pairs, landed, names)]
    *shards, small_grad = _share_reduced(totals)
    grads = {}
    for (n, tr, _), sh in zip(BIG, shards):
        sh = sh.reshape(2 * sh.shape[1], sh.shape[2])
        grads[n] = sh.T if tr else sh
    small_grad = small_grad.reshape(N_CHIPS * SMALL_ROWS, 1024)
    grads_small = _unpack_small(small_grad)

    delta, new_m, new_v = {}, {}, {}
    for n, _, _ in BIG:
        shape = weights[n].shape
        dn_, nm_, nv_ = _adamw(shard2d(weights[n]), grads[n], shard2d(moms[n]), shard2d(vels[n]),
                               tr=min(shape[-2], 256), name="adamw_" + n)
        delta[n], new_m[n], new_v[n] = dn_.reshape(shape), nm_.reshape(shape), nv_.reshape(shape)
        grads[n] = grads[n].reshape(shape)
    ds_, ms_, vs_ = _adamw(_pack_small(small), small_grad,
                           _pack_small({n: moms[n] for n, _ in SMALL}), _pack_small({n: vels[n] for n, _ in SMALL}),
                           tr=N_CHIPS * SMALL_ROWS, name="adamw_small")
    for dst, buf in ((delta, ds_), (new_m, ms_), (new_v, vs_)):
        dst.update(_unpack_small(buf))
    grads.update(grads_small)

    total_loss = lax.psum(loss[0, 0], ("x", "y", "c"))
    return (total_loss, dx[None], *[grads[n] for n in WEIGHT_ORDER], *[delta[n] for n in WEIGHT_ORDER],
            *[new_m[n] for n in WEIGHT_ORDER], *[new_v[n] for n in WEIGHT_ORDER])
```

```python
import functools
import math

import numpy as np
import jax
import jax.numpy as jnp
from jax import lax
from jax.experimental import pallas as pl
from jax.experimental.pallas import tpu as pltpu

F32 = jnp.float32
BF16 = jnp.bfloat16

D_MODEL = 1024
SSM_GROUPS = 32
SSM_GROUP_SIZE = 16
SSM_STATE = 64
SSM_WIDTH = 512
N_STATES = SSM_GROUPS * SSM_STATE
SCAN_CB = 512
ATTN_PATTERNS = ((128, 1), (512, 4), (2048, 16))
ATTN_HEAD_DIM = 64
ATTN_Q = 128
MEM_LEN = 256
MEM_HEAD_DIM = 128
MEM_HEADS = 4
D_FF = 4096
OFF_U, OFF_QKV, OFF_MQ, OFF_ZG = 0, 512, 2816, 3328
IN_WIDTH = 6400
RMS_EPS = 1e-6
NEG_INF = -1e30
ADAM_LR, ADAM_B1, ADAM_B2, ADAM_EPS, ADAM_WD, ADAM_STEP = 0.001, 0.9, 0.999, 1e-08, 0.01, 10

VMEM_LIMIT_BYTES = 48 * 1024 * 1024
LANES = 128
MESH = pl.DeviceIdType.MESH
N_CHIPS = 4

SCAN_SEGS = 8

BIG = (("w_in", True, (6400, 1024)), ("w_glu", False, (512, 512)), ("w_ssm_br", True, (1024, 512)),
       ("w_attn_br", True, (1024, 256)), ("w_mem_kv", False, (1024, 1024)), ("w_mem_br", True, (1024, 512)),
       ("w_o", False, (1024, 1024)), ("w_up", True, (4096, 1024)), ("w_down", False, (4096, 1024)))
SMALL = (("norm1_g", (1, 1024)), ("mem_norm_g", (1, 1024)), ("b_gate", (1, 3072)),
         ("ssm_lambda_re", (1, 32, 64)), ("ssm_lambda_im", (1, 32, 64)), ("ssm_log_dt", (1, 32)),
         ("ssm_b_re", (1, 32, 64, 16)), ("ssm_b_im", (1, 32, 64, 16)), ("ssm_c_re", (1, 32, 16, 64)),
         ("ssm_c_im", (1, 32, 16, 64)), ("ssm_d", (1, 32, 16)), ("b_glu", (1, 512)),
         ("norm2_g", (1, 1024)), ("final_g", (1024,)))
WEIGHT_ORDER = ("norm1_g", "mem_norm_g", "w_in", "b_gate", "ssm_lambda_re", "ssm_lambda_im", "ssm_log_dt",
                "ssm_b_re", "ssm_b_im", "ssm_c_re", "ssm_c_im", "ssm_d", "w_glu", "b_glu", "w_ssm_br",
                "w_attn_br", "w_mem_kv", "w_mem_br", "w_o", "norm2_g", "w_up", "w_down", "final_g")
SMALL_ELEMS = sum(int(np.prod(s)) for _, s in SMALL)
SMALL_ROWS = 64


def _params(sem):
    return pltpu.CompilerParams(dimension_semantics=sem, vmem_limit_bytes=VMEM_LIMIT_BYTES)


def _sigmoid(v):
    return 1.0 / (1.0 + jnp.exp(-v))


_GELU_C = math.sqrt(2.0 / math.pi)


def _gelu(v):
    return 0.5 * v * (1.0 + jnp.tanh(_GELU_C * (v + 0.044715 * v * v * v)))


def _gelu_grad(v):
    th = jnp.tanh(_GELU_C * (v + 0.044715 * v * v * v))
    return 0.5 * (1.0 + th) + 0.5 * v * (1.0 - th * th) * _GELU_C * (1.0 + 3.0 * 0.044715 * v * v)


def _dot(a, b, ca, cb):
    return lax.dot_general(a, b, (((ca,), (cb,)), ((), ())), preferred_element_type=F32)


def _matmul(a, b, *, m, n, k, ta=False, tb=False, tm, tn, tk, out_dtypes, name,
            a_off=(0, 0), b_off=(0, 0), aux=(), epilogue=None):
    assert m % tm == 0 and n % tn == 0 and k % tk == 0, (name, m, n, k, tm, tn, tk)
    nk = k // tk
    n_aux = len(aux)
    n_out = len(out_dtypes)
    ar, ac = a_off
    br, bc = b_off
    if ta:
        a_spec = pl.BlockSpec((tk, tm), lambda i, j, kk: (kk + ar, i + ac))
    else:
        a_spec = pl.BlockSpec((tm, tk), lambda i, j, kk: (i + ar, kk + ac))
    if tb:
        b_spec = pl.BlockSpec((tn, tk), lambda i, j, kk: (j + br, kk + bc))
    else:
        b_spec = pl.BlockSpec((tk, tn), lambda i, j, kk: (kk + br, j + bc))
    aux_specs = []
    for _, kind in aux:
        if kind == "mn":
            aux_specs.append(pl.BlockSpec((tm, tn), lambda i, j, kk: (i, j)))
        else:
            aux_specs.append(pl.BlockSpec((1, tn), lambda i, j, kk: (0, j)))
    ca = 0 if ta else 1
    cb = 1 if tb else 0

    def finish(acc, aux_refs, out_refs):
        outs = (acc,) if epilogue is None else epilogue(acc, *[r[...] for r in aux_refs])
        for o_ref, o in zip(out_refs, outs):
            o_ref[...] = o.astype(o_ref.dtype)

    def body(a_ref, b_ref, *rest):
        aux_refs = rest[:n_aux]
        out_refs = rest[n_aux:n_aux + n_out]
        prod = _dot(a_ref[...].astype(BF16), b_ref[...].astype(BF16), ca, cb)
        if nk == 1:
            finish(prod, aux_refs, out_refs)
            return
        acc_ref = rest[n_aux + n_out]
        kk = pl.program_id(2)

        @pl.when(kk == 0)
        def _():
            acc_ref[...] = prod

        @pl.when(jnp.logical_and(kk > 0, kk < nk - 1))
        def _():
            acc_ref[...] += prod

        @pl.when(kk == nk - 1)
        def _():
            finish(acc_ref[...] + prod, aux_refs, out_refs)

    res = pl.pallas_call(
        body, name=name, grid=(m // tm, n // tn, nk),
        in_specs=[a_spec, b_spec] + aux_specs,
        out_specs=[pl.BlockSpec((tm, tn), lambda i, j, kk: (i, j)) for _ in range(n_out)],
        out_shape=[jax.ShapeDtypeStruct((m, n), dt) for dt in out_dtypes],
        scratch_shapes=[pltpu.VMEM((tm, tn), F32)] if nk > 1 else [],
        compiler_params=_params(("parallel", "parallel", "arbitrary")),
    )(a, b, *[x for x, _ in aux])
    return res[0] if n_out == 1 else tuple(res)


def _rmsnorm_fwd(x, g, *, tm, name):
    rows, d = x.shape

    def body(x_ref, g_ref, o_ref):
        xv = x_ref[...]
        r = lax.rsqrt(jnp.mean(xv * xv, axis=-1, keepdims=True) + RMS_EPS)
        o_ref[...] = (xv * r * g_ref[...]).astype(o_ref.dtype)

    return pl.pallas_call(
        body, name=name, grid=(rows // tm,),
        in_specs=[pl.BlockSpec((tm, d), lambda i: (i, 0)), pl.BlockSpec((1, d), lambda i: (0, 0))],
        out_specs=pl.BlockSpec((tm, d), lambda i: (i, 0)),
        out_shape=jax.ShapeDtypeStruct((rows, d), BF16),
        compiler_params=_params(("parallel",)),
    )(x, g)


def _rmsnorm_bwd(x, g, dy, res, *, tm, name):
    rows, d = x.shape
    has_res = res is not None

    def body(x_ref, g_ref, dy_ref, *rest):
        if has_res:
            res_ref, dx_ref, dg_ref = rest
        else:
            dx_ref, dg_ref = rest
        i = pl.program_id(0)
        xv = x_ref[...]
        r = lax.rsqrt(jnp.mean(xv * xv, axis=-1, keepdims=True) + RMS_EPS)
        xhat = xv * r
        dyv = dy_ref[...]
        dyg = dyv * g_ref[...]
        dx = r * (dyg - xhat * jnp.mean(dyg * xhat, axis=-1, keepdims=True))
        if has_res:
            dx = dx + res_ref[...]
        dx_ref[...] = dx

        @pl.when(i == 0)
        def _():
            dg_ref[...] = jnp.zeros_like(dg_ref)

        dg_ref[...] += jnp.sum(dyv * xhat, axis=0, keepdims=True)

    row_spec = pl.BlockSpec((tm, d), lambda i: (i, 0))
    vec_spec = pl.BlockSpec((1, d), lambda i: (0, 0))
    ins = [x, g, dy] + ([res] if has_res else [])
    return pl.pallas_call(
        body, name=name, grid=(rows // tm,),
        in_specs=[row_spec, vec_spec, row_spec] + ([row_spec] if has_res else []),
        out_specs=[row_spec, vec_spec],
        out_shape=[jax.ShapeDtypeStruct((rows, d), F32), jax.ShapeDtypeStruct((1, d), F32)],
        compiler_params=_params(("arbitrary",)),
    )(*ins)


def _loss_head(h, tgt, g, *, tm, name):
    rows, d = h.shape
    nsteps = rows // tm

    def body(h_ref, t_ref, g_ref, dh_ref, loss_ref, dg_ref, sq_ref):
        i = pl.program_id(0)
        xv = h_ref[...]
        gv = g_ref[...]
        r = lax.rsqrt(jnp.mean(xv * xv, axis=-1, keepdims=True) + RMS_EPS)
        xhat = xv * r
        err = xhat * gv - t_ref[...]
        dyv = err * (1.0 / d)
        dyg = dyv * gv
        dh_ref[...] = r * (dyg - xhat * jnp.mean(dyg * xhat, axis=-1, keepdims=True))

        @pl.when(i == 0)
        def _():
            dg_ref[...] = jnp.zeros_like(dg_ref)
            sq_ref[...] = jnp.zeros_like(sq_ref)

        dg_ref[...] += jnp.sum(dyv * xhat, axis=0, keepdims=True)
        sq_ref[...] += jnp.sum(err * err, axis=0, keepdims=True)

        @pl.when(i == nsteps - 1)
        def _():
            tot = jnp.sum(sq_ref[...], axis=-1, keepdims=True) * (0.5 / d)
            loss_ref[...] = jnp.broadcast_to(tot, loss_ref.shape)

    row_spec = pl.BlockSpec((tm, d), lambda i: (i, 0))
    vec_spec = pl.BlockSpec((1, d), lambda i: (0, 0))
    return pl.pallas_call(
        body, name=name, grid=(nsteps,),
        in_specs=[row_spec, row_spec, vec_spec],
        out_specs=[row_spec, pl.BlockSpec((1, LANES), lambda i: (0, 0)), vec_spec],
        out_shape=[jax.ShapeDtypeStruct((rows, d), F32), jax.ShapeDtypeStruct((1, LANES), F32),
                   jax.ShapeDtypeStruct((1, d), F32)],
        scratch_shapes=[pltpu.VMEM((1, d), F32)],
        compiler_params=_params(("arbitrary",)),
    )(h, tgt, g)


def _to_scan_layout(v):
    lead = v.shape[:-2]
    v = v.reshape(lead + (2, N_STATES // SCAN_CB, SCAN_CB))
    v = jnp.swapaxes(v, -3, -2)
    return v.reshape(lead + (2 * N_STATES,))


def _ssm_matrices(lam_re, lam_im, log_dt, b_re, b_im, c_re, c_im):
    dt = jnp.exp(log_dt)[:, None]
    mag = jnp.exp(lam_re * dt)
    a_re, a_im = mag * jnp.cos(lam_im * dt), mag * jnp.sin(lam_im * dt)
    nr, ni = a_re - 1.0, a_im
    den = lam_re * lam_re + lam_im * lam_im
    coef_re = (nr * lam_re + ni * lam_im) / den
    coef_im = (ni * lam_re - nr * lam_im) / den
    bb_re = coef_re[..., None] * b_re - coef_im[..., None] * b_im
    bb_im = coef_re[..., None] * b_im + coef_im[..., None] * b_re
    eye = jnp.eye(SSM_GROUPS, dtype=F32)
    a_lay = _to_scan_layout(jnp.stack([a_re.reshape(-1), a_im.reshape(-1)], axis=0))[None, :]

    def b_dense(bb):
        return jnp.einsum("gk,kph->ghkp", eye, bb).reshape(SSM_WIDTH, N_STATES)

    b_lay = _to_scan_layout(jnp.stack([b_dense(bb_re), b_dense(bb_im)], axis=1))

    def c_dense(cc):
        return jnp.einsum("gk,ghp->kpgh", eye, cc).reshape(N_STATES, SSM_WIDTH)

    c_lay = _to_scan_layout(jnp.stack([c_dense(c_re), -c_dense(c_im)], axis=0).transpose(2, 0, 1)).T
    return a_lay, b_lay, c_lay


def _interleave(v):
    rows, c = v.shape
    return v.reshape(SCAN_SEGS, rows // SCAN_SEGS, c).transpose(1, 0, 2).reshape(rows, c)


def _deinterleave(v):
    rows, c = v.shape
    return v.reshape(rows // SCAN_SEGS, SCAN_SEGS, c).transpose(1, 0, 2).reshape(rows, c)


def _scan_groups(a_ref, bu_ref, o_ref, state, *, reverse, tt):
    cb = SCAN_CB
    ar = jnp.broadcast_to(a_ref[:, :cb], (SCAN_SEGS, cb))
    ai = jnp.broadcast_to(a_ref[:, cb:], (SCAN_SEGS, cb))
    ngroups = tt // SCAN_SEGS

    def step(i, st):
        sr, si = st
        r0 = pl.multiple_of(((ngroups - 1 - i) if reverse else i) * SCAN_SEGS, SCAN_SEGS)
        blk = bu_ref[pl.ds(r0, SCAN_SEGS), :]
        nr = ar * sr - ai * si + blk[:, :cb]
        ni = ar * si + ai * sr + blk[:, cb:]
        if o_ref is not None:
            o_ref[pl.ds(r0, SCAN_SEGS), :] = jnp.concatenate([nr, ni], axis=1)
        return nr, ni

    return lax.fori_loop(0, ngroups, step, state, unroll=4)


def _scan_ends(a_lay, bu, *, reverse, tt, name):
    rows, width = bu.shape
    cb = SCAN_CB
    nt = rows // tt

    def body(a_ref, bu_ref, e_ref):
        kk = pl.program_id(1)

        @pl.when(kk == 0)
        def _():
            e_ref[...] = jnp.zeros_like(e_ref)

        sr, si = _scan_groups(a_ref, bu_ref, None, (e_ref[:, :cb], e_ref[:, cb:]), reverse=reverse, tt=tt)
        e_ref[...] = jnp.concatenate([sr, si], axis=1)

    tmap = (lambda j, kk: (nt - 1 - kk, j)) if reverse else (lambda j, kk: (kk, j))
    return pl.pallas_call(
        body, name=name, grid=(width // (2 * cb), nt),
        in_specs=[pl.BlockSpec((1, 2 * cb), lambda j, kk: (0, j)), pl.BlockSpec((tt, 2 * cb), tmap)],
        out_specs=pl.BlockSpec((SCAN_SEGS, 2 * cb), lambda j, kk: (0, j)),
        out_shape=jax.ShapeDtypeStruct((SCAN_SEGS, width), F32),
        compiler_params=_params(("parallel", "arbitrary")),
    )(a_lay, bu)


def _scan_apply(a_lay, bu, ends, *, reverse, tt, name):
    rows, width = bu.shape
    cb = SCAN_CB
    nt = rows // tt
    seg_len = rows // SCAN_SEGS
    n_sq = seg_len.bit_length() - 1
    assert 1 << n_sq == seg_len, seg_len

    def body(a_ref, e_ref, bu_ref, o_ref, init_ref, carry_ref):
        kk = pl.program_id(1)

        @pl.when(kk == 0)
        def _():
            pr, pi = a_ref[:, :cb], a_ref[:, cb:]
            for _ in range(n_sq):
                pr, pi = pr * pr - pi * pi, 2.0 * pr * pi
            cr = jnp.zeros((1, cb), F32)
            ci = jnp.zeros((1, cb), F32)
            order = range(SCAN_SEGS - 1, -1, -1) if reverse else range(SCAN_SEGS)
            for k, seg in enumerate(order):
                if k > 0:
                    prev = seg + 1 if reverse else seg - 1
                    er, ei = e_ref[prev:prev + 1, :cb], e_ref[prev:prev + 1, cb:]
                    cr, ci = pr * cr - pi * ci + er, pr * ci + pi * cr + ei
                init_ref[seg:seg + 1, :] = jnp.concatenate([cr, ci], axis=1)
            carry_ref[...] = init_ref[...]

        sr, si = _scan_groups(a_ref, bu_ref, o_ref, (carry_ref[:, :cb], carry_ref[:, cb:]), reverse=reverse, tt=tt)
        carry_ref[...] = jnp.concatenate([sr, si], axis=1)

    tmap = (lambda j, kk: (nt - 1 - kk, j)) if reverse else (lambda j, kk: (kk, j))
    seg_spec = pl.BlockSpec((SCAN_SEGS, 2 * cb), lambda j, kk: (0, j))
    return pl.pallas_call(
        body, name=name, grid=(width // (2 * cb), nt),
        in_specs=[pl.BlockSpec((1, 2 * cb), lambda j, kk: (0, j)), seg_spec, pl.BlockSpec((tt, 2 * cb), tmap)],
        out_specs=[pl.BlockSpec((tt, 2 * cb), tmap), seg_spec],
        out_shape=[jax.ShapeDtypeStruct((rows, width), F32), jax.ShapeDtypeStruct((SCAN_SEGS, width), F32)],
        scratch_shapes=[pltpu.VMEM((SCAN_SEGS, 2 * cb), F32)],
        compiler_params=_params(("parallel", "arbitrary")),
    )(a_lay, ends, bu)


def _ssm_da(lam, s, entry, *, tt, name):
    rows, width = s.shape
    cb = SCAN_CB
    ngroups = tt // SCAN_SEGS

    def body(lam_ref, s_ref, entry_ref, da_ref, prev_ref):
        kk = pl.program_id(1)

        @pl.when(kk == 0)
        def _():
            prev_ref[...] = entry_ref[...]
            da_ref[...] = jnp.zeros_like(da_ref)

        def step(i, st):
            accr, acci, pr, pi = st
            r0 = pl.multiple_of(i * SCAN_SEGS, SCAN_SEGS)
            lv = lam_ref[pl.ds(r0, SCAN_SEGS), :]
            sv = s_ref[pl.ds(r0, SCAN_SEGS), :]
            lr, li = lv[:, :cb], lv[:, cb:]
            return accr + (lr * pr + li * pi), acci + (li * pr - lr * pi), sv[:, :cb], sv[:, cb:]

        zero = jnp.zeros((SCAN_SEGS, cb), F32)
        accr, acci, pr, pi = lax.fori_loop(0, ngroups, step, (zero, zero, prev_ref[:, :cb], prev_ref[:, cb:]), unroll=4)
        prev_ref[...] = jnp.concatenate([pr, pi], axis=1)
        da_ref[...] += jnp.concatenate([jnp.sum(accr, axis=0, keepdims=True), jnp.sum(acci, axis=0, keepdims=True)], axis=1)

    blk = pl.BlockSpec((tt, 2 * cb), lambda j, kk: (kk, j))
    return pl.pallas_call(
        body, name=name, grid=(width // (2 * cb), rows // tt),
        in_specs=[blk, blk, pl.BlockSpec((SCAN_SEGS, 2 * cb), lambda j, kk: (0, j))],
        out_specs=pl.BlockSpec((1, 2 * cb), lambda j, kk: (0, j)),
        out_shape=jax.ShapeDtypeStruct((1, width), F32),
        scratch_shapes=[pltpu.VMEM((SCAN_SEGS, 2 * cb), F32)],
        compiler_params=_params(("parallel", "arbitrary")),
    )(lam, s, entry)


def _glu_fwd(ys, u, dd, w_glu, b_glu, *, tm, name):
    rows, w = ys.shape

    def body(ys_ref, u_ref, dd_ref, w_ref, b_ref, y0_ref, t_ref, y2_ref):
        y0 = ys_ref[...] + dd_ref[...] * u_ref[...]
        y1 = _gelu(y0)
        t = _dot(y1.astype(BF16), w_ref[...], 1, 0) + b_ref[...]
        y0_ref[...] = y0
        t_ref[...] = t
        y2_ref[...] = (y1 * _sigmoid(t)).astype(BF16)

    row = pl.BlockSpec((tm, w), lambda i: (i, 0))
    vec = pl.BlockSpec((1, w), lambda i: (0, 0))
    return pl.pallas_call(
        body, name=name, grid=(rows // tm,),
        in_specs=[row, row, vec, pl.BlockSpec((w, w), lambda i: (0, 0)), vec],
        out_specs=[row, row, row],
        out_shape=[jax.ShapeDtypeStruct((rows, w), F32), jax.ShapeDtypeStruct((rows, w), F32),
                   jax.ShapeDtypeStruct((rows, w), BF16)],
        compiler_params=_params(("parallel",)),
    )(ys, u, dd, w_glu, b_glu)


def _glu_bwd(dy2, y0, t, u, w_glu, *, tm, name):
    rows, w = y0.shape

    def body(dy2_ref, y0_ref, t_ref, u_ref, w_ref, dy0_ref, dt_ref, y1_ref, db_ref, dd_ref):
        i = pl.program_id(0)
        y0 = y0_ref[...]
        y1 = _gelu(y0)
        sg = _sigmoid(t_ref[...])
        dy2v = dy2_ref[...]
        dt = dy2v * y1 * sg * (1.0 - sg)
        dy1 = dy2v * sg + _dot(dt.astype(BF16), w_ref[...], 1, 1)
        dy0 = dy1 * _gelu_grad(y0)
        dy0_ref[...] = dy0
        dt_ref[...] = dt.astype(BF16)
        y1_ref[...] = y1.astype(BF16)

        @pl.when(i == 0)
        def _():
            db_ref[...] = jnp.zeros_like(db_ref)
            dd_ref[...] = jnp.zeros_like(dd_ref)

        db_ref[...] += jnp.sum(dt, axis=0, keepdims=True)
        dd_ref[...] += jnp.sum(dy0 * u_ref[...], axis=0, keepdims=True)

    row = pl.BlockSpec((tm, w), lambda i: (i, 0))
    vec = pl.BlockSpec((1, w), lambda i: (0, 0))
    return pl.pallas_call(
        body, name=name, grid=(rows // tm,),
        in_specs=[row, row, row, row, pl.BlockSpec((w, w), lambda i: (0, 0))],
        out_specs=[row, row, row, vec, vec],
        out_shape=[jax.ShapeDtypeStruct((rows, w), F32), jax.ShapeDtypeStruct((rows, w), BF16),
                   jax.ShapeDtypeStruct((rows, w), BF16), jax.ShapeDtypeStruct((1, w), F32),
                   jax.ShapeDtypeStruct((1, w), F32)],
        compiler_params=_params(("arbitrary",)),
    )(dy2, y0, t, u, w_glu)


ATTN_TILE = 2048


def _attn_geometry(rows, d):
    sb = ATTN_Q * d
    tr = max(sb, min(ATTN_TILE, rows))
    assert rows % tr == 0 and tr % sb == 0, (rows, d)
    return sb, tr, rows // tr, tr // sb


def _attn_masks():
    qi = lax.broadcasted_iota(jnp.int32, (2 * ATTN_Q, 2 * ATTN_Q), 0) % ATTN_Q
    kj = lax.broadcasted_iota(jnp.int32, (2 * ATTN_Q, 2 * ATTN_Q), 1)
    own_ok = jnp.logical_and(kj >= ATTN_Q, kj - ATTN_Q <= qi)
    prev_ok = jnp.logical_and(kj < ATTN_Q, kj >= qi)
    bias_first = jnp.where(own_ok, 0.0, NEG_INF)
    bias_other = jnp.where(jnp.logical_or(own_ok, prev_ok), 0.0, NEG_INF)
    head0 = lax.broadcasted_iota(jnp.int32, (ATTN_Q, LANES), 1) < ATTN_HEAD_DIM
    return bias_first, bias_other, head0


def _attn_rows(base, n, d):
    return pl.ds(pl.multiple_of(base, ATTN_Q), n) if d == 1 else pl.ds(base, n, stride=d)


def _stack_heads(v, head0):
    return jnp.concatenate([jnp.where(head0, v, 0.0), jnp.where(head0, 0.0, v)], axis=0)


def _unstack_heads(v, head0):
    return jnp.where(head0, v[:ATTN_Q], v[ATTN_Q:])


def _fill_keys(buf, prev_ref, cur_ref, sb):
    buf[pl.ds(0, sb), :] = prev_ref[...]
    buf[pl.ds(sb, cur_ref.shape[0]), :] = cur_ref[...]


def _attn_fwd(qkv, g, d, *, name):
    rows = qkv.shape[0]
    sb, tr, ntiles, nsub = _attn_geometry(rows, d)
    qc, kc, vc = 2 * g, 6 + 2 * g, 12 + 2 * g
    scale = ATTN_HEAD_DIM ** -0.5

    def body(q_ref, kc_ref, kp_ref, vc_ref, vp_ref, o_ref, lse_ref, kbuf, vbuf):
        n = pl.program_id(0)
        _fill_keys(kbuf, kp_ref, kc_ref, sb)
        _fill_keys(vbuf, vp_ref, vc_ref, sb)
        bias_first, bias_other, head0 = _attn_masks()

        def per_block(idx, carry):
            j, r = idx // d, idx % d
            base = j * sb + r
            bias = jnp.where(jnp.logical_and(n == 0, j == 0), bias_first, bias_other)
            qrows = _attn_rows(base, ATTN_Q, d)
            krows = _attn_rows(base, 2 * ATTN_Q, d)
            qs = _stack_heads(q_ref[qrows, :], head0).astype(BF16)
            s = _dot(qs, kbuf[krows, :].astype(BF16), 1, 1) * scale + bias
            mx = jnp.max(s, axis=-1, keepdims=True)
            p = jnp.exp(s - mx)
            den = jnp.sum(p, axis=-1, keepdims=True)
            pv = _dot(p.astype(BF16), vbuf[krows, :].astype(BF16), 1, 0) / den
            o_ref[qrows, :] = _unstack_heads(pv, head0)
            lse_ref[qrows, :] = _unstack_heads(jnp.broadcast_to(mx + jnp.log(den), (2 * ATTN_Q, LANES)), head0)
            return carry

        lax.fori_loop(0, nsub * d, per_block, 0)

    def cur(col):
        return pl.BlockSpec((tr, LANES), lambda n, hp: (n, col + hp))

    def prev(col):
        return pl.BlockSpec((sb, LANES), lambda n, hp: (jnp.maximum(n * nsub - 1, 0), col + hp))

    out_spec = pl.BlockSpec((tr, LANES), lambda n, hp: (n, hp))
    return pl.pallas_call(
        body, name=name, grid=(ntiles, 2),
        in_specs=[cur(qc), cur(kc), prev(kc), cur(vc), prev(vc)],
        out_specs=[out_spec, out_spec],
        out_shape=[jax.ShapeDtypeStruct((rows, 2 * LANES), F32), jax.ShapeDtypeStruct((rows, 2 * LANES), F32)],
        scratch_shapes=[pltpu.VMEM((sb + tr, LANES), F32), pltpu.VMEM((sb + tr, LANES), F32)],
        compiler_params=_params(("parallel", "parallel")),
    )(qkv, qkv, qkv, qkv, qkv)


def _attn_merge(outs, lses, *, tm, name):
    rows, w = outs[0].shape

    def body(o0, o1, o2, l0, l1, l2, o_ref, lse_ref):
        a0, a1, a2 = l0[...], l1[...], l2[...]
        mx = jnp.maximum(jnp.maximum(a0, a1), a2)
        e0, e1, e2 = jnp.exp(a0 - mx), jnp.exp(a1 - mx), jnp.exp(a2 - mx)
        den = e0 + e1 + e2
        o_ref[...] = (e0 / den) * o0[...] + (e1 / den) * o1[...] + (e2 / den) * o2[...]
        lse_ref[...] = mx + jnp.log(den)

    row = pl.BlockSpec((tm, w), lambda i: (i, 0))
    return pl.pallas_call(
        body, name=name, grid=(rows // tm,), in_specs=[row] * 6, out_specs=[row, row],
        out_shape=[jax.ShapeDtypeStruct((rows, w), F32), jax.ShapeDtypeStruct((rows, w), F32)],
        compiler_params=_params(("parallel",)),
    )(*outs, *lses)


def _attn_bwd(qkv, do, o, lse, g, d, prev, *, name):
    rows = qkv.shape[0]
    sb, tr, ntiles, nsub = _attn_geometry(rows, d)
    qc, kc, vc = 2 * g, 6 + 2 * g, 12 + 2 * g
    scale = ATTN_HEAD_DIM ** -0.5

    def body(q_ref, kc_ref, kp_ref, vc_ref, vp_ref, do_ref, o_ref, lse_ref, dq_ref, dk_ref, dv_ref,
             kbuf, vbuf, dk_acc, dv_acc):
        n = pl.program_id(1)

        @pl.when(n == 0)
        def _():
            dk_acc[pl.ds(0, tr), :] = jnp.zeros((tr, LANES), F32)
            dv_acc[pl.ds(0, tr), :] = jnp.zeros((tr, LANES), F32)

        @pl.when(n < ntiles)
        def _():
            dk_acc[pl.ds(tr, tr), :] = jnp.zeros((tr, LANES), F32)
            dv_acc[pl.ds(tr, tr), :] = jnp.zeros((tr, LANES), F32)
            _fill_keys(kbuf, kp_ref, kc_ref, sb)
            _fill_keys(vbuf, vp_ref, vc_ref, sb)
            bias_first, bias_other, head0 = _attn_masks()
            lane = lax.broadcasted_iota(jnp.int32, (ATTN_Q, LANES), 1)

            def per_block(idx, carry):
                j, r = idx // d, idx % d
                base = j * sb + r
                bias = jnp.where(jnp.logical_and(n == 0, j == 0), bias_first, bias_other)
                qrows = _attn_rows(base, ATTN_Q, d)
                krows = _attn_rows(base, 2 * ATTN_Q, d)
                arows = _attn_rows(base + (tr - sb), 2 * ATTN_Q, d)
                qs = _stack_heads(q_ref[qrows, :], head0).astype(BF16)
                dos = _stack_heads(do_ref[qrows, :], head0)
                dosb = dos.astype(BF16)
                ov = o_ref[qrows, :]
                delta = jnp.sum(dos * jnp.concatenate([ov, ov], axis=0), axis=-1, keepdims=True)
                lsev = lse_ref[qrows, :]
                lse_s = jnp.concatenate(
                    [jnp.sum(jnp.where(lane == h * ATTN_HEAD_DIM, lsev, 0.0), axis=-1, keepdims=True) for h in range(2)], axis=0)
                kb = kbuf[krows, :].astype(BF16)
                vb = vbuf[krows, :].astype(BF16)
                p = jnp.exp(_dot(qs, kb, 1, 1) * scale + bias - lse_s)
                ds = (p * (_dot(dosb, vb, 1, 1) - delta) * scale).astype(BF16)
                dq_ref[qrows, :] = _unstack_heads(_dot(ds, kb, 1, 0), head0)
                dk_acc[arows, :] += _dot(ds, qs, 0, 0)
                dv_acc[arows, :] += _dot(p.astype(BF16), dosb, 0, 0)
                return carry

            lax.fori_loop(0, nsub * d, per_block, 0)

        dk_ref[...] = dk_acc[pl.ds(0, tr), :]
        dv_ref[...] = dv_acc[pl.ds(0, tr), :]
        dk_acc[pl.ds(0, tr), :] = dk_acc[pl.ds(tr, tr), :]
        dv_acc[pl.ds(0, tr), :] = dv_acc[pl.ds(tr, tr), :]

    def cur(n):
        return jnp.minimum(n, ntiles - 1)

    def spec(col, prev):
        if prev:
            return pl.BlockSpec((sb, LANES), lambda hp, n: (jnp.maximum(cur(n) * nsub - 1, 0), col + hp))
        return pl.BlockSpec((tr, LANES), lambda hp, n: (cur(n), col + hp))

    row_spec = pl.BlockSpec((tr, LANES), lambda hp, n: (cur(n), hp))
    dq_out = pl.BlockSpec((tr, LANES), lambda hp, n: (cur(n), 2 * g + hp))
    kv_out = pl.BlockSpec((tr, LANES), lambda hp, n: (jnp.maximum(n - 1, 0), 2 * g + hp))
    shape = jax.ShapeDtypeStruct((rows, len(ATTN_PATTERNS) * 2 * LANES), F32)
    ins = [qkv, qkv, qkv, qkv, qkv, do, o, lse]
    in_specs = [spec(qc, False), spec(kc, False), spec(kc, True), spec(vc, False), spec(vc, True),
                row_spec, row_spec, row_spec]
    aliases = {}
    if prev is not None:
        aliases = {len(ins) + t: t for t in range(3)}
        ins = ins + list(prev)
        in_specs = in_specs + [ANY] * 3
    n_in = len(ins)

    def entry(*refs):
        body(*refs[:8], *refs[n_in:])

    return pl.pallas_call(
        entry, name=name, grid=(2, ntiles + 1),
        in_specs=in_specs,
        out_specs=[dq_out, kv_out, kv_out],
        out_shape=[shape, shape, shape],
        input_output_aliases=aliases,
        scratch_shapes=[pltpu.VMEM((sb + tr, LANES), F32), pltpu.VMEM((sb + tr, LANES), F32),
                        pltpu.VMEM((2 * tr, LANES), F32), pltpu.VMEM((2 * tr, LANES), F32)],
        compiler_params=_params(("parallel", "arbitrary")),
    )(*ins)


def _mem_probs(q, k):
    s = _dot(q.astype(BF16), k.astype(BF16), 1, 1) * (MEM_HEAD_DIM ** -0.5)
    e = jnp.exp(s - jnp.max(s, axis=-1, keepdims=True))
    return e / jnp.sum(e, axis=-1, keepdims=True)


def _mem_attn_fwd(mq, kv, *, tq, name):
    rows = mq.shape[0]

    def body(q_ref, k_ref, v_ref, o_ref):
        p = _mem_probs(q_ref[...], k_ref[...])
        o_ref[...] = _dot(p.astype(BF16), v_ref[...].astype(BF16), 1, 0)

    return pl.pallas_call(
        body, name=name, grid=(rows // tq, MEM_HEADS),
        in_specs=[pl.BlockSpec((tq, LANES), lambda i, h: (i, h)),
                  pl.BlockSpec((MEM_LEN, LANES), lambda i, h: (0, h)),
                  pl.BlockSpec((MEM_LEN, LANES), lambda i, h: (0, MEM_HEADS + h))],
        out_specs=pl.BlockSpec((tq, LANES), lambda i, h: (i, h)),
        out_shape=jax.ShapeDtypeStruct((rows, MEM_HEADS * LANES), F32),
        compiler_params=_params(("parallel", "parallel")),
    )(mq, kv, kv)


def _mem_attn_bwd(mq, kv, dmo, *, tq, name):
    rows = mq.shape[0]
    scale = MEM_HEAD_DIM ** -0.5

    def body(q_ref, k_ref, v_ref, do_ref, dq_ref, dk_ref, dv_ref):
        i = pl.program_id(1)
        qb = q_ref[...].astype(BF16)
        kb = k_ref[...].astype(BF16)
        vb = v_ref[...].astype(BF16)
        dob = do_ref[...].astype(BF16)
        p = _mem_probs(q_ref[...], k_ref[...])
        dp = _dot(dob, vb, 1, 1)
        ds = (p * (dp - jnp.sum(p * dp, axis=-1, keepdims=True)) * scale).astype(BF16)
        dq_ref[...] = _dot(ds, kb, 1, 0)

        @pl.when(i == 0)
        def _():
            dk_ref[...] = jnp.zeros_like(dk_ref)
            dv_ref[...] = jnp.zeros_like(dv_ref)

        dk_ref[...] += _dot(ds, qb, 0, 0)
        dv_ref[...] += _dot(p.astype(BF16), dob, 0, 0)

    kv_out = pl.BlockSpec((MEM_LEN, LANES), lambda h, i: (0, h))
    kv_shape = jax.ShapeDtypeStruct((MEM_LEN, MEM_HEADS * LANES), F32)
    return pl.pallas_call(
        body, name=name, grid=(MEM_HEADS, rows // tq),
        in_specs=[pl.BlockSpec((tq, LANES), lambda h, i: (i, h)),
                  pl.BlockSpec((MEM_LEN, LANES), lambda h, i: (0, h)),
                  pl.BlockSpec((MEM_LEN, LANES), lambda h, i: (0, MEM_HEADS + h)),
                  pl.BlockSpec((tq, LANES), lambda h, i: (i, h))],
        out_specs=[pl.BlockSpec((tq, LANES), lambda h, i: (i, h)), kv_out, kv_out],
        out_shape=[jax.ShapeDtypeStruct((rows, MEM_HEADS * LANES), F32), kv_shape, kv_shape],
        compiler_params=_params(("parallel", "arbitrary")),
    )(mq, kv, kv, dmo)


def _gate_merge_fwd(zg, b_gate, br_s, br_a, br_m, *, tm, name):
    rows, d = br_s.shape

    def body(zg_ref, b_ref, s_ref, a_ref, m_ref, o_ref):
        gt = _sigmoid(zg_ref[...] + b_ref[...])
        o_ref[...] = (gt[:, :d] * s_ref[...] + gt[:, d:2 * d] * a_ref[...] + gt[:, 2 * d:] * m_ref[...]).astype(BF16)

    row = pl.BlockSpec((tm, d), lambda i: (i, 0))
    return pl.pallas_call(
        body, name=name, grid=(rows // tm,),
        in_specs=[pl.BlockSpec((tm, 3 * d), lambda i: (i, 0)), pl.BlockSpec((1, 3 * d), lambda i: (0, 0)), row, row, row],
        out_specs=row, out_shape=jax.ShapeDtypeStruct((rows, d), BF16),
        compiler_params=_params(("parallel",)),
    )(zg, b_gate, br_s, br_a, br_m)


def _gate_merge_bwd(dmerged, zg, b_gate, br_s, br_a, br_m, *, tm, name):
    rows, d = br_s.shape

    def body(dm_ref, zg_ref, b_ref, s_ref, a_ref, m_ref, ds_ref, da_ref, dmm_ref, dzg_ref, db_ref):
        i = pl.program_id(0)
        gt = _sigmoid(zg_ref[...] + b_ref[...])
        dm = dm_ref[...]
        parts = []
        for j, (br_ref, out_ref) in enumerate(((s_ref, ds_ref), (a_ref, da_ref), (m_ref, dmm_ref))):
            gj = gt[:, j * d:(j + 1) * d]
            out_ref[...] = (dm * gj).astype(BF16)
            parts.append(dm * br_ref[...] * gj * (1.0 - gj))
        dzg = jnp.concatenate(parts, axis=1)
        dzg_ref[...] = dzg

        @pl.when(i == 0)
        def _():
            db_ref[...] = jnp.zeros_like(db_ref)

        db_ref[...] += jnp.sum(dzg, axis=0, keepdims=True)

    row = pl.BlockSpec((tm, d), lambda i: (i, 0))
    wide = pl.BlockSpec((tm, 3 * d), lambda i: (i, 0))
    vec = pl.BlockSpec((1, 3 * d), lambda i: (0, 0))
    bshape = jax.ShapeDtypeStruct((rows, d), BF16)
    return pl.pallas_call(
        body, name=name, grid=(rows // tm,),
        in_specs=[row, wide, vec, row, row, row],
        out_specs=[row, row, row, wide, vec],
        out_shape=[bshape, bshape, bshape, jax.ShapeDtypeStruct((rows, 3 * d), F32), jax.ShapeDtypeStruct((1, 3 * d), F32)],
        compiler_params=_params(("arbitrary",)),
    )(dmerged, zg, b_gate, br_s, br_a, br_m)


def _adamw(w, g, m, v, *, tr, name):
    rows, cols = w.shape
    assert rows % tr == 0, (name, rows, tr)

    def body(w_ref, g_ref, m_ref, v_ref, d_ref, nm_ref, nv_ref):
        gv = g_ref[...]
        m2 = ADAM_B1 * m_ref[...] + (1.0 - ADAM_B1) * gv
        v2 = ADAM_B2 * v_ref[...] + (1.0 - ADAM_B2) * (gv * gv)
        m_hat = m2 / (1.0 - ADAM_B1 ** ADAM_STEP)
        v_hat = v2 / (1.0 - ADAM_B2 ** ADAM_STEP)
        d_ref[...] = -ADAM_LR * (m_hat / (jnp.sqrt(v_hat) + ADAM_EPS) + ADAM_WD * w_ref[...])
        nm_ref[...] = m2
        nv_ref[...] = v2

    blk = pl.BlockSpec((tr, cols), lambda i: (i, 0))
    shape = jax.ShapeDtypeStruct((rows, cols), F32)
    return pl.pallas_call(
        body, name=name, grid=(rows // tr,), in_specs=[blk] * 4, out_specs=[blk] * 3,
        out_shape=[shape, shape, shape], compiler_params=_params(("parallel",)),
    )(w, g, m, v)


ANY = pl.BlockSpec(memory_space=pl.ANY)


def _position():
    return lax.axis_index("x"), lax.axis_index("y"), lax.axis_index("c")


def _other_chips(x, y):
    return ((1 - x, y), (x, 1 - y), (1 - x, 1 - y))


def _remote(src, dst, send_sem, recv_sem, dev):
    return pltpu.make_async_remote_copy(src_ref=src, dst_ref=dst, send_sem=send_sem, recv_sem=recv_sem,
                                        device_id=dev, device_id_type=MESH)


def _all_gather_weights(bufs):
    nb = len(bufs)

    def body(*refs):
        outs = refs[nb:2 * nb]
        send_sems, recv_sems = refs[2 * nb:]
        x, y, c = _position()
        chip = 2 * x + y
        sibling = (x, y, 1 - c)
        chips = _other_chips(x, y)

        def rows_of(i, owner, core):
            rs = bufs[i].shape[0] // N_CHIPS
            return pl.ds(pl.multiple_of(owner * rs + core * (rs // 2), 16), rs // 2)

        sends = []
        for i in range(nb):
            mine = outs[i].at[rows_of(i, chip, c)]
            for j, (px, py) in enumerate(chips):
                cp = _remote(mine, mine, send_sems.at[i, j], recv_sems.at[i, j], (px, py, c))
                cp.start()
                sends.append(cp)
        for i in range(nb):
            for j, (px, py) in enumerate(chips):
                landed = outs[i].at[rows_of(i, 2 * px + py, c)]
                _remote(landed, landed, send_sems.at[i, j], recv_sems.at[i, j], (px, py, c)).wait_recv()
                cp = _remote(landed, landed, send_sems.at[i, 3 + j], recv_sems.at[i, 3 + j], sibling)
                cp.start()
                sends.append(cp)
        for i in range(nb):
            for j, (px, py) in enumerate(chips):
                dst = outs[i].at[rows_of(i, 2 * px + py, 1 - c)]
                _remote(dst, dst, send_sems.at[i, 3 + j], recv_sems.at[i, 3 + j], sibling).wait_recv()
        for cp in sends:
            cp.wait_send()

    return pl.pallas_call(
        body, name="all_gather_weights", in_specs=[ANY] * nb, out_specs=[ANY] * nb,
        out_shape=[jax.ShapeDtypeStruct(b.shape, b.dtype) for b in bufs],
        input_output_aliases={i: i for i in range(nb)},
        scratch_shapes=[pltpu.SemaphoreType.DMA((nb, 6)), pltpu.SemaphoreType.DMA((nb, 6))],
    )(*bufs)


def _row_tile(rows):
    return max(t for t in range(16, min(rows, 512) + 1, 16) if rows % t == 0)


def _exchange_halves(grads):
    nb = len(grads)

    def body(*refs):
        ins, outs = refs[:nb], refs[nb:2 * nb]
        send_sems, recv_sems = refs[2 * nb:]
        x, y, c = _position()
        copies = []
        for i in range(nb):
            cp = _remote(ins[i].at[:, 1 - c], outs[i], send_sems.at[i], recv_sems.at[i], (x, y, 1 - c))
            cp.start()
            copies.append(cp)
        for cp in copies:
            cp.wait()

    return pl.pallas_call(
        body, name="grad_exchange_halves", in_specs=[ANY] * nb, out_specs=[ANY] * nb,
        out_shape=[jax.ShapeDtypeStruct((N_CHIPS, g.shape[2], g.shape[3]), F32) for g in grads],
        scratch_shapes=[pltpu.SemaphoreType.DMA((nb,)), pltpu.SemaphoreType.DMA((nb,))],
    )(*grads)


def _pair_sum(g4, got, c_arr, *, name):
    _, _, half, cols = g4.shape
    tr = _row_tile(half)

    def body(c_ref, g_ref, t_ref, p_ref, pb_ref):
        sm = g_ref[...] + t_ref[...]
        p_ref[...] = sm
        pb_ref[...] = sm.astype(BF16)

    blk = pl.BlockSpec((None, tr, cols), lambda j, i, c_ref: (j, i, 0))
    grid_spec = pltpu.PrefetchScalarGridSpec(
        num_scalar_prefetch=1, grid=(N_CHIPS, half // tr),
        in_specs=[pl.BlockSpec((None, None, tr, cols), lambda j, i, c_ref: (j, c_ref[0], i, 0)), blk],
        out_specs=[blk, blk])
    return pl.pallas_call(
        body, name=name, grid_spec=grid_spec,
        out_shape=[jax.ShapeDtypeStruct((N_CHIPS, half, cols), F32), jax.ShapeDtypeStruct((N_CHIPS, half, cols), BF16)],
        compiler_params=_params(("parallel", "parallel")),
    )(c_arr, g4, got)


def _scatter_to_owners(parts):
    nb = len(parts)

    def body(*refs):
        ins, outs = refs[:nb], refs[nb:2 * nb]
        send_sems, recv_sems = refs[2 * nb:]
        x, y, c = _position()
        copies = []
        for i in range(nb):
            for j, (px, py) in enumerate(_other_chips(x, y)):
                cp = _remote(ins[i].at[2 * px + py], outs[i].at[j], send_sems.at[i, j], recv_sems.at[i, j], (px, py, c))
                cp.start()
                copies.append(cp)
        for cp in copies:
            cp.wait()

    return pl.pallas_call(
        body, name="grad_scatter_to_owners", in_specs=[ANY] * nb, out_specs=[ANY] * nb,
        out_shape=[jax.ShapeDtypeStruct((3,) + p.shape[1:], p.dtype) for p in parts],
        scratch_shapes=[pltpu.SemaphoreType.DMA((nb, 3)), pltpu.SemaphoreType.DMA((nb, 3))],
    )(*parts)


def _owner_sum(p, got, chip_arr, c_arr, *, replicated, name):
    _, half, cols = p.shape
    tr = _row_tile(half)

    def body(chip_ref, c_ref, p_ref, r_ref, o_ref):
        o_ref[...] = ((p_ref[...] + r_ref[0].astype(F32)) + r_ref[1].astype(F32)) + r_ref[2].astype(F32)

    if replicated:
        out_spec = pl.BlockSpec((None, None, tr, cols), lambda i, chip_ref, c_ref: (chip_ref[0], c_ref[0], i, 0))
        out_shape = jax.ShapeDtypeStruct((N_CHIPS, 2, half, cols), F32)
    else:
        out_spec = pl.BlockSpec((None, tr, cols), lambda i, chip_ref, c_ref: (c_ref[0], i, 0))
        out_shape = jax.ShapeDtypeStruct((2, half, cols), F32)
    grid_spec = pltpu.PrefetchScalarGridSpec(
        num_scalar_prefetch=2, grid=(half // tr,),
        in_specs=[pl.BlockSpec((None, tr, cols), lambda i, chip_ref, c_ref: (chip_ref[0], i, 0)),
                  pl.BlockSpec((3, tr, cols), lambda i, chip_ref, c_ref: (0, i, 0))],
        out_specs=out_spec)
    return pl.pallas_call(
        body, name=name, grid_spec=grid_spec, out_shape=out_shape,
        compiler_params=_params(("parallel",)),
    )(chip_arr, c_arr, p, got)


def _share_reduced(bufs):
    nb = len(bufs) - 1

    def body(*refs):
        outs = refs[nb + 1:2 * nb + 2]
        send_sems, recv_sems = refs[2 * nb + 2:]
        x, y, c = _position()
        chip = 2 * x + y
        sends = []
        for i in range(nb):
            cp = _remote(outs[i].at[c], outs[i].at[c], send_sems.at[i], recv_sems.at[i], (x, y, 1 - c))
            cp.start()
            sends.append(cp)
        small = outs[nb]
        peers = [(fx, fy, fc) for fx in (0, 1) for fy in (0, 1) for fc in (0, 1) if fx + fy + fc > 0]
        for k, (fx, fy, fc) in enumerate(peers):
            dev = (x ^ fx, y ^ fy, c ^ fc)
            cp = _remote(small.at[chip, c], small.at[chip, c], send_sems.at[nb + k], recv_sems.at[nb + k], dev)
            cp.start()
            sends.append(cp)
        for i in range(nb):
            dst = outs[i].at[1 - c]
            _remote(dst, dst, send_sems.at[i], recv_sems.at[i], (x, y, 1 - c)).wait_recv()
        for k, (fx, fy, fc) in enumerate(peers):
            dst = small.at[2 * (x ^ fx) + (y ^ fy), c ^ fc]
            _remote(dst, dst, send_sems.at[nb + k], recv_sems.at[nb + k], (x ^ fx, y ^ fy, c ^ fc)).wait_recv()
        for cp in sends:
            cp.wait_send()

    n_all = nb + 1
    return pl.pallas_call(
        body, name="grad_share_reduced", in_specs=[ANY] * n_all, out_specs=[ANY] * n_all,
        out_shape=[jax.ShapeDtypeStruct(b.shape, b.dtype) for b in bufs],
        input_output_aliases={i: i for i in range(n_all)},
        scratch_shapes=[pltpu.SemaphoreType.DMA((nb + 7,)), pltpu.SemaphoreType.DMA((nb + 7,))],
    )(*bufs)


def _pack_small(vals):
    flat = jnp.concatenate([vals[name].reshape(-1) for name, _ in SMALL])
    return jnp.pad(flat, (0, N_CHIPS * SMALL_ROWS * 1024 - SMALL_ELEMS)).reshape(N_CHIPS * SMALL_ROWS, 1024)


def _unpack_small(buf):
    flat = buf.reshape(-1)
    out, off = {}, 0
    for name, shape in SMALL:
        n = int(np.prod(shape))
        out[name] = flat[off:off + n].reshape(shape)
        off += n
    return out


def _device_step(x, mem, tgt, w, p):
    rows = x.shape[0]
    g1, gm, g2 = p["norm1_g"], p["mem_norm_g"], p["norm2_g"]
    gf = p["final_g"].reshape(1, D_MODEL)
    ssm_args = (p["ssm_lambda_re"][0], p["ssm_lambda_im"][0], p["ssm_log_dt"][0], p["ssm_b_re"][0],
                p["ssm_b_im"][0], p["ssm_c_re"][0], p["ssm_c_im"][0])
    (a_lay, b_lay, c_lay), ssm_vjp = jax.vjp(_ssm_matrices, *ssm_args)
    a_conj = a_lay * _to_scan_layout(jnp.stack([jnp.ones((N_STATES,), F32), -jnp.ones((N_STATES,), F32)]))[None, :]
    dd = p["ssm_d"].reshape(1, SSM_WIDTH)
    win_t = w["w_in"]
    mm = _matmul

    n1 = _rmsnorm_fwd(x, g1, tm=512, name="norm1")
    u = mm(n1, win_t, m=rows, n=512, k=1024, tb=True, tm=2048, tn=512, tk=1024, out_dtypes=(F32,), name="in_u")
    qkv = mm(n1, win_t, m=rows, n=2304, k=1024, tb=True, tm=2048, tn=256, tk=1024, b_off=(OFF_QKV // 256, 0),
             out_dtypes=(F32,), name="in_qkv")
    mq = mm(n1, win_t, m=rows, n=512, k=1024, tb=True, tm=2048, tn=256, tk=1024, b_off=(OFF_MQ // 256, 0),
            out_dtypes=(F32,), name="in_mq")
    zg = mm(n1, win_t, m=rows, n=3072, k=1024, tb=True, tm=2048, tn=256, tk=1024, b_off=(OFF_ZG // 256, 0),
            out_dtypes=(F32,), name="in_zg")

    u_i = _interleave(u)
    bu = mm(u_i, b_lay, m=rows, n=2 * N_STATES, k=512, tm=1024, tn=1024, tk=512, out_dtypes=(F32,), name="ssm_bu")
    ends = _scan_ends(a_lay, bu, reverse=False, tt=512, name="ssm_scan_fwd_ends")
    s, s_entry = _scan_apply(a_lay, bu, ends, reverse=False, tt=512, name="ssm_scan_fwd")
    ys = _deinterleave(mm(s, c_lay, m=rows, n=512, k=2 * N_STATES, tm=1024, tn=512, tk=1024, out_dtypes=(F32,), name="ssm_cs"))
    y0, tglu, y2 = _glu_fwd(ys, u, dd, w["w_glu"], p["b_glu"], tm=512, name="glu_fwd")
    br_s = mm(y2, w["w_ssm_br"], m=rows, n=1024, k=512, tb=True, tm=1024, tn=1024, tk=512, out_dtypes=(F32,), name="br_ssm")

    outs, lses = [], []
    for g, (_, d) in enumerate(ATTN_PATTERNS):
        o_g, lse_g = _attn_fwd(qkv, g, d, name=f"attn_fwd_{g}")
        outs.append(o_g)
        lses.append(lse_g)
    o, lse = _attn_merge(outs, lses, tm=1024, name="attn_merge")
    br_a = mm(o, w["w_attn_br"], m=rows, n=1024, k=256, tb=True, tm=1024, tn=1024, tk=256, out_dtypes=(F32,), name="br_attn")

    mn = _rmsnorm_fwd(mem, gm, tm=MEM_LEN, name="mem_norm")
    kv = mm(mn, w["w_mem_kv"], m=MEM_LEN, n=1024, k=1024, tm=MEM_LEN, tn=1024, tk=1024, out_dtypes=(F32,), name="mem_kv")
    mo = _mem_attn_fwd(mq, kv, tq=1024, name="mem_attn_fwd")
    br_m = mm(mo, w["w_mem_br"], m=rows, n=1024, k=512, tb=True, tm=1024, tn=1024, tk=512, out_dtypes=(F32,), name="br_mem")

    merged = _gate_merge_fwd(zg, p["b_gate"], br_s, br_a, br_m, tm=256, name="gate_merge_fwd")
    add = lambda acc, r: (acc + r,)
    h1 = mm(merged, w["w_o"], m=rows, n=1024, k=1024, tm=1024, tn=1024, tk=1024, out_dtypes=(F32,),
            aux=((x, "mn"),), epilogue=add, name="out_proj")
    n2 = _rmsnorm_fwd(h1, g2, tm=512, name="norm2")
    relu2 = lambda acc: (acc, jnp.square(jnp.maximum(acc, 0.0)))
    up, act = mm(n2, w["w_up"], m=rows, n=D_FF, k=1024, tb=True, tm=1024, tn=1024, tk=1024, out_dtypes=(F32, BF16),
                 epilogue=relu2, name="mlp_up")
    h2 = mm(act, w["w_down"], m=rows, n=1024, k=D_FF, tm=1024, tn=1024, tk=1024, out_dtypes=(F32,),
            aux=((h1, "mn"),), epilogue=add, name="mlp_down")
    dh2, loss, d_gf = _loss_head(h2, tgt, gf, tm=512, name="loss_head")

    gb = {}
    gs = {"final_g": d_gf.reshape(D_MODEL)}
    drelu2 = lambda acc, upv: (acc * (2.0 * jnp.maximum(upv, 0.0)),)
    dup = mm(dh2, w["w_down"], m=rows, n=D_FF, k=1024, tb=True, tm=1024, tn=1024, tk=1024, out_dtypes=(BF16,),
             aux=((up, "mn"),), epilogue=drelu2, name="d_act")
    gb["w_down"] = mm(act, dh2, m=D_FF, n=1024, k=rows, ta=True, tm=1024, tn=1024, tk=1024, out_dtypes=(F32,), name="dw_down")
    dn2 = mm(dup, w["w_up"], m=rows, n=1024, k=D_FF, tm=1024, tn=1024, tk=1024, out_dtypes=(F32,), name="d_n2")
    gb["w_up"] = mm(dup, n2, m=D_FF, n=1024, k=rows, ta=True, tm=1024, tn=1024, tk=1024, out_dtypes=(F32,), name="dw_up")
    dh1, gs["norm2_g"] = _rmsnorm_bwd(h1, g2, dn2, dh2, tm=512, name="norm2_bwd")
    dmerged = mm(dh1, w["w_o"], m=rows, n=1024, k=1024, tb=True, tm=1024, tn=1024, tk=1024, out_dtypes=(F32,), name="d_merged")
    gb["w_o"] = mm(merged, dh1, m=1024, n=1024, k=rows, ta=True, tm=1024, tn=1024, tk=1024, out_dtypes=(F32,), name="dw_o")
    dbr_s, dbr_a, dbr_m, dzg, gs["b_gate"] = _gate_merge_bwd(dmerged, zg, p["b_gate"], br_s, br_a, br_m, tm=256,
                                                              name="gate_merge_bwd")

    dy2 = mm(dbr_s, w["w_ssm_br"], m=rows, n=512, k=1024, tm=1024, tn=512, tk=1024, out_dtypes=(F32,), name="d_y2")
    gb["w_ssm_br"] = mm(dbr_s, y2, m=1024, n=512, k=rows, ta=True, tm=1024, tn=512, tk=1024, out_dtypes=(F32,), name="dw_ssm_br")
    dy0, dt, y1, gs["b_glu"], d_dd = _glu_bwd(dy2, y0, tglu, u, w["w_glu"], tm=512, name="glu_bwd")
    gs["ssm_d"] = d_dd.reshape(1, SSM_GROUPS, SSM_GROUP_SIZE)
    gb["w_glu"] = mm(y1, dt, m=512, n=512, k=rows, ta=True, tm=512, tn=512, tk=1024, out_dtypes=(F32,), name="dw_glu")
    dy0_i = _interleave(dy0)
    dsout = mm(dy0_i, c_lay, m=rows, n=2 * N_STATES, k=512, tb=True, tm=1024, tn=1024, tk=512, out_dtypes=(F32,), name="ssm_dsout")
    lam_ends = _scan_ends(a_conj, dsout, reverse=True, tt=512, name="ssm_scan_bwd_ends")
    lam, _ = _scan_apply(a_conj, dsout, lam_ends, reverse=True, tt=512, name="ssm_scan_bwd")
    skip = lambda acc, dyv, ddv: (acc + ddv * dyv,)
    du = _deinterleave(mm(lam, b_lay, m=rows, n=512, k=2 * N_STATES, tb=True, tm=1024, tn=512, tk=1024, out_dtypes=(F32,),
                          aux=((dy0_i, "mn"), (dd, "row")), epilogue=skip, name="ssm_du"))
    d_b_lay = mm(u_i, lam, m=512, n=2 * N_STATES, k=rows, ta=True, tm=512, tn=1024, tk=1024, out_dtypes=(F32,), name="ssm_db")
    d_c_lay = mm(s, dy0_i, m=2 * N_STATES, n=512, k=rows, ta=True, tm=1024, tn=512, tk=1024, out_dtypes=(F32,), name="ssm_dc")
    d_a_lay = _ssm_da(lam, s, s_entry, tt=512, name="ssm_da")
    d_ssm = ssm_vjp((d_a_lay, d_b_lay, d_c_lay))
    for name, val in zip(("ssm_lambda_re", "ssm_lambda_im", "ssm_log_dt", "ssm_b_re", "ssm_b_im", "ssm_c_re", "ssm_c_im"), d_ssm):
        gs[name] = val[None]

    do = mm(dbr_a, w["w_attn_br"], m=rows, n=256, k=1024, tm=1024, tn=256, tk=1024, out_dtypes=(F32,), name="d_o")
    gb["w_attn_br"] = mm(dbr_a, o, m=1024, n=256, k=rows, ta=True, tm=1024, tn=256, tk=1024, out_dtypes=(F32,), name="dw_attn_br")
    dqkv = None
    for g, (_, d) in enumerate(ATTN_PATTERNS):
        dqkv = _attn_bwd(qkv, do, o, lse, g, d, dqkv, name=f"attn_bwd_{g}")

    dmo = mm(dbr_m, w["w_mem_br"], m=rows, n=512, k=1024, tm=1024, tn=512, tk=1024, out_dtypes=(F32,), name="d_mo")
    gb["w_mem_br"] = mm(dbr_m, mo, m=1024, n=512, k=rows, ta=True, tm=1024, tn=512, tk=1024, out_dtypes=(F32,), name="dw_mem_br")
    dmq, dmk, dmv = _mem_attn_bwd(mq, kv, dmo, tq=1024, name="mem_attn_bwd")
    dkv = jnp.concatenate([dmk, dmv], axis=1)
    gb["w_mem_kv"] = mm(mn, dkv, m=1024, n=1024, k=MEM_LEN, ta=True, tm=1024, tn=1024, tk=MEM_LEN, out_dtypes=(F32,), name="dw_mem_kv")
    dmn = mm(dkv, w["w_mem_kv"], m=MEM_LEN, n=1024, k=1024, tb=True, tm=MEM_LEN, tn=1024, tk=1024, out_dtypes=(F32,), name="d_mn")
    _, gs["mem_norm_g"] = _rmsnorm_bwd(mem, gm, dmn, None, tm=MEM_LEN, name="mem_norm_bwd")

    pieces = ((du, OFF_U, "u"), (dqkv[0], OFF_QKV, "q"), (dqkv[1], OFF_QKV + 768, "k"), (dqkv[2], OFF_QKV + 1536, "v"),
              (dmq, OFF_MQ, "mq"), (dzg, OFF_ZG, "zg"))
    dn = None
    dw_rows = []
    for piece, off, tag in pieces:
        width = piece.shape[1]
        tk = 1024 if (width % 1024 == 0 and off % 1024 == 0) else 256
        dn = mm(piece, win_t, m=rows, n=1024, k=width, tm=1024, tn=1024, tk=tk, b_off=(off // tk, 0), out_dtypes=(F32,),
                aux=() if dn is None else ((dn, "mn"),), epilogue=None if dn is None else add, name="d_n1_" + tag)
        tmw = 1024 if width % 1024 == 0 else (768 if width == 768 else 512)
        dw_rows.append(mm(piece, n1, m=width, n=1024, k=rows, ta=True, tm=tmw, tn=1024, tk=1024, out_dtypes=(F32,),
                          name="dw_in_" + tag))
    gb["w_in"] = jnp.concatenate(dw_rows, axis=0)
    dx, gs["norm1_g"] = _rmsnorm_bwd(x, g1, dn, dh1, tm=512, name="norm1_bwd")
    return loss, dx, gb, gs


def kernel(x, mem, norm1_g, mem_norm_g, w_in, b_gate, ssm_lambda_re, ssm_lambda_im, ssm_log_dt, ssm_b_re, ssm_b_im, ssm_c_re, ssm_c_im, ssm_d, w_glu, b_glu, w_ssm_br, w_attn_br, w_mem_kv, w_mem_br, w_o, norm2_g, w_up, w_down, final_g, loss_target, m_norm1_g, m_mem_norm_g, m_w_in, m_b_gate, m_ssm_lambda_re, m_ssm_lambda_im, m_ssm_log_dt, m_ssm_b_re, m_ssm_b_im, m_ssm_c_re, m_ssm_c_im, m_ssm_d, m_w_glu, m_b_glu, m_w_ssm_br, m_w_attn_br, m_w_mem_kv, m_w_mem_br, m_w_o, m_norm2_g, m_w_up, m_w_down, m_final_g, v_norm1_g, v_mem_norm_g, v_w_in, v_b_gate, v_ssm_lambda_re, v_ssm_lambda_im, v_ssm_log_dt, v_ssm_b_re, v_ssm_b_im, v_ssm_c_re, v_ssm_c_im, v_ssm_d, v_w_glu, v_b_glu, v_w_ssm_br, v_w_attn_br, v_w_mem_kv, v_w_mem_br, v_w_o, v_norm2_g, v_w_up, v_w_down, v_final_g):
    env = dict(locals())
    weights = {n: env[n] for n in WEIGHT_ORDER}
    moms = {n: env["m_" + n] for n in WEIGHT_ORDER}
    vels = {n: env["v_" + n] for n in WEIGHT_ORDER}
    def shard2d(a):
        return a.reshape(a.shape[-2], a.shape[-1])

    chip = 2 * lax.axis_index("x") + lax.axis_index("y")
    wire = [shard2d(weights[n]).astype(BF16) for n, _, _ in BIG]
    wire = [s.T if tr else s for s, (_, tr, _) in zip(wire, BIG)]
    wire = [lax.dynamic_update_slice(lax.empty((N_CHIPS * s.shape[0], s.shape[1]), BF16), s, (chip * s.shape[0], 0))
            for s in wire]
    w_full = dict(zip([n for n, _, _ in BIG], _all_gather_weights(wire)))
    small = {n: weights[n] for n, _ in SMALL}

    loss, dx, gb, gs = _device_step(x[0], mem[0], loss_target[0], w_full, small)

    c_arr = lax.axis_index("c").astype(jnp.int32).reshape(1)
    chip_arr = chip.astype(jnp.int32).reshape(1)
    names = [n for n, _, _ in BIG] + ["small"]
    full = [gb[n] for n, _, _ in BIG] + [_pack_small(gs)]
    full = [g.reshape(N_CHIPS, 2, g.shape[0] // (2 * N_CHIPS), g.shape[1]) for g in full]
    from_sibling = _exchange_halves(full)
    pairs = [_pair_sum(g, t, c_arr, name="grad_pair_sum_" + n) for g, t, n in zip(full, from_sibling, names)]
    landed = _scatter_to_owners([pb for _, pb in pairs])
    totals = [_owner_sum(pf, r, chip_arr, c_arr, replicated=(n == "small"), name="grad_owner_sum_" + n)
              for (pf, _), r, n in zip(pairs, landed, names)]
    *shards, small_grad = _share_reduced(totals)
    grads = {}
    for (n, tr, _), sh in zip(BIG, shards):
        sh = sh.reshape(2 * sh.shape[1], sh.shape[2])
        grads[n] = sh.T if tr else sh
    small_grad = small_grad.reshape(N_CHIPS * SMALL_ROWS, 1024)
    grads_small = _unpack_small(small_grad)

    delta, new_m, new_v = {}, {}, {}
    for n, _, _ in BIG:
        shape = weights[n].shape
        dn_, nm_, nv_ = _adamw(shard2d(weights[n]), grads[n], shard2d(moms[n]), shard2d(vels[n]),
                               tr=min(shape[-2], 256), name="adamw_" + n)
        delta[n], new_m[n], new_v[n] = dn_.reshape(shape), nm_.reshape(shape), nv_.reshape(shape)
        grads[n] = grads[n].reshape(shape)
    ds_, ms_, vs_ = _adamw(_pack_small(small), small_grad,
                           _pack_small({n: moms[n] for n, _ in SMALL}), _pack_small({n: vels[n] for n, _ in SMALL}),
                           tr=N_CHIPS * SMALL_ROWS, name="adamw_small")
    for dst, buf in ((delta, ds_), (new_m, ms_), (new_v, vs_)):
        dst.update(_unpack_small(buf))
    grads.update(grads_small)

    total_loss = lax.psum(loss[0, 0], ("x", "y", "c"))
    return (total_loss, dx[None], *[grads[n] for n in WEIGHT_ORDER], *[delta[n] for n in WEIGHT_ORDER],
            *[new_m[n] for n in WEIGHT_ORDER], *[new_v[n] for n in WEIGHT_ORDER])
```

```python
import functools
import math

import numpy as np
import jax
import jax.numpy as jnp
from jax import lax
from jax.experimental import pallas as pl
from jax.experimental.pallas import tpu as pltpu

F32 = jnp.float32
BF16 = jnp.bfloat16

D_MODEL = 1024
SSM_GROUPS = 32
SSM_GROUP_SIZE = 16
SSM_STATE = 64
SSM_WIDTH = 512
N_STATES = SSM_GROUPS * SSM_STATE
SCAN_CB = 512
ATTN_PATTERNS = ((128, 1), (512, 4), (2048, 16))
ATTN_HEAD_DIM = 64
ATTN_Q = 128
MEM_LEN = 256
MEM_HEAD_DIM = 128
MEM_HEADS = 4
D_FF = 4096
OFF_U, OFF_QKV, OFF_MQ, OFF_ZG = 0, 512, 2816, 3328
IN_WIDTH = 6400
RMS_EPS = 1e-6
NEG_INF = -1e30
ADAM_LR, ADAM_B1, ADAM_B2, ADAM_EPS, ADAM_WD, ADAM_STEP = 0.001, 0.9, 0.999, 1e-08, 0.01, 10

VMEM_LIMIT_BYTES = 48 * 1024 * 1024
LANES = 128
MESH = pl.DeviceIdType.MESH
N_CHIPS = 4

SCAN_SEGS = 8
SCAN_GROUPS = SCAN_CB // SSM_STATE

BIG = (("w_in", True, (6400, 1024)), ("w_glu", False, (512, 512)), ("w_ssm_br", True, (1024, 512)),
       ("w_attn_br", True, (1024, 256)), ("w_mem_kv", False, (1024, 1024)), ("w_mem_br", True, (1024, 512)),
       ("w_o", False, (1024, 1024)), ("w_up", True, (4096, 1024)), ("w_down", False, (4096, 1024)))
SMALL = (("norm1_g", (1, 1024)), ("mem_norm_g", (1, 1024)), ("b_gate", (1, 3072)),
         ("ssm_lambda_re", (1, 32, 64)), ("ssm_lambda_im", (1, 32, 64)), ("ssm_log_dt", (1, 32)),
         ("ssm_b_re", (1, 32, 64, 16)), ("ssm_b_im", (1, 32, 64, 16)), ("ssm_c_re", (1, 32, 16, 64)),
         ("ssm_c_im", (1, 32, 16, 64)), ("ssm_d", (1, 32, 16)), ("b_glu", (1, 512)),
         ("norm2_g", (1, 1024)), ("final_g", (1024,)))
WEIGHT_ORDER = ("norm1_g", "mem_norm_g", "w_in", "b_gate", "ssm_lambda_re", "ssm_lambda_im", "ssm_log_dt",
                "ssm_b_re", "ssm_b_im", "ssm_c_re", "ssm_c_im", "ssm_d", "w_glu", "b_glu", "w_ssm_br",
                "w_attn_br", "w_mem_kv", "w_mem_br", "w_o", "norm2_g", "w_up", "w_down", "final_g")
SMALL_ELEMS = sum(int(np.prod(s)) for _, s in SMALL)
SMALL_ROWS = 64


def _params(sem):
    return pltpu.CompilerParams(dimension_semantics=sem, vmem_limit_bytes=VMEM_LIMIT_BYTES)


def _sigmoid(v):
    return 1.0 / (1.0 + jnp.exp(-v))


_GELU_C = math.sqrt(2.0 / math.pi)


def _gelu(v):
    return 0.5 * v * (1.0 + jnp.tanh(_GELU_C * (v + 0.044715 * v * v * v)))


def _gelu_grad(v):
    th = jnp.tanh(_GELU_C * (v + 0.044715 * v * v * v))
    return 0.5 * (1.0 + th) + 0.5 * v * (1.0 - th * th) * _GELU_C * (1.0 + 3.0 * 0.044715 * v * v)


def _dot(a, b, ca, cb):
    return lax.dot_general(a, b, (((ca,), (cb,)), ((), ())), preferred_element_type=F32)


def _matmul(a, b, *, m, n, k, ta=False, tb=False, tm, tn, tk, out_dtypes, name,
            a_off=(0, 0), b_off=(0, 0), aux=(), epilogue=None):
    assert m % tm == 0 and n % tn == 0 and k % tk == 0, (name, m, n, k, tm, tn, tk)
    nk = k // tk
    n_aux = len(aux)
    n_out = len(out_dtypes)
    ar, ac = a_off
    br, bc = b_off
    if ta:
        a_spec = pl.BlockSpec((tk, tm), lambda i, j, kk: (kk + ar, i + ac))
    else:
        a_spec = pl.BlockSpec((tm, tk), lambda i, j, kk: (i + ar, kk + ac))
    if tb:
        b_spec = pl.BlockSpec((tn, tk), lambda i, j, kk: (j + br, kk + bc))
    else:
        b_spec = pl.BlockSpec((tk, tn), lambda i, j, kk: (kk + br, j + bc))
    aux_specs = []
    for _, kind in aux:
        if kind == "mn":
            aux_specs.append(pl.BlockSpec((tm, tn), lambda i, j, kk: (i, j)))
        else:
            aux_specs.append(pl.BlockSpec((1, tn), lambda i, j, kk: (0, j)))
    ca = 0 if ta else 1
    cb = 1 if tb else 0

    def finish(acc, aux_refs, out_refs):
        outs = (acc,) if epilogue is None else epilogue(acc, *[r[...] for r in aux_refs])
        for o_ref, o in zip(out_refs, outs):
            o_ref[...] = o.astype(o_ref.dtype)

    def body(a_ref, b_ref, *rest):
        aux_refs = rest[:n_aux]
        out_refs = rest[n_aux:n_aux + n_out]
        prod = _dot(a_ref[...].astype(BF16), b_ref[...].astype(BF16), ca, cb)
        if nk == 1:
            finish(prod, aux_refs, out_refs)
            return
        acc_ref = rest[n_aux + n_out]
        kk = pl.program_id(2)

        @pl.when(kk == 0)
        def _():
            acc_ref[...] = prod

        @pl.when(jnp.logical_and(kk > 0, kk < nk - 1))
        def _():
            acc_ref[...] += prod

        @pl.when(kk == nk - 1)
        def _():
            finish(acc_ref[...] + prod, aux_refs, out_refs)

    res = pl.pallas_call(
        body, name=name, grid=(m // tm, n // tn, nk),
        in_specs=[a_spec, b_spec] + aux_specs,
        out_specs=[pl.BlockSpec((tm, tn), lambda i, j, kk: (i, j)) for _ in range(n_out)],
        out_shape=[jax.ShapeDtypeStruct((m, n), dt) for dt in out_dtypes],
        scratch_shapes=[pltpu.VMEM((tm, tn), F32)] if nk > 1 else [],
        compiler_params=_params(("parallel", "parallel", "arbitrary")),
    )(a, b, *[x for x, _ in aux])
    return res[0] if n_out == 1 else tuple(res)


def _sum_matmul(pieces, b, offs, *, tm, name):
    m = pieces[0].shape[0]
    n = b.shape[1]
    npieces = len(pieces)

    def body(*refs):
        b_ref, o_ref = refs[npieces], refs[npieces + 1]
        acc = None
        for p_ref, off in zip(refs[:npieces], offs):
            part = _dot(p_ref[...].astype(BF16), b_ref[pl.ds(off, p_ref.shape[1]), :], 1, 0)
            acc = part if acc is None else acc + part
        o_ref[...] = acc

    return pl.pallas_call(
        body, name=name, grid=(m // tm,),
        in_specs=[pl.BlockSpec((tm, p.shape[1]), lambda i: (i, 0)) for p in pieces]
        + [pl.BlockSpec(b.shape, lambda i: (0, 0), pipeline_mode=pl.Buffered(1))],
        out_specs=pl.BlockSpec((tm, n), lambda i: (i, 0)),
        out_shape=jax.ShapeDtypeStruct((m, n), F32),
        compiler_params=_params(("parallel",)),
    )(*pieces, b)


def _block_matmul(a, blocks, *, transpose, tm, name, aux=(), epilogue=None):
    m = a.shape[0]
    nblk = blocks.shape[0]
    ka = a.shape[1] // nblk
    nb = blocks.shape[1] if transpose else blocks.shape[2]
    n_aux = len(aux)
    cb = 1 if transpose else 0

    def body(a_ref, b_ref, *rest):
        acc = _dot(a_ref[...].astype(BF16), b_ref[...].astype(BF16), 1, cb)
        outs = (acc,) if epilogue is None else epilogue(acc, *[r[...] for r in rest[:n_aux]])
        rest[n_aux][...] = outs[0]

    aux_specs = [pl.BlockSpec((tm, nb) if kind == "mn" else (1, nb), (lambda i, j: (i, j)) if kind == "mn" else (lambda i, j: (0, j)))
                 for _, kind in aux]
    return pl.pallas_call(
        body, name=name, grid=(m // tm, nblk),
        in_specs=[pl.BlockSpec((tm, ka), lambda i, j: (i, j)),
                  pl.BlockSpec((None,) + blocks.shape[1:], lambda i, j: (j, 0, 0))] + aux_specs,
        out_specs=pl.BlockSpec((tm, nb), lambda i, j: (i, j)),
        out_shape=jax.ShapeDtypeStruct((m, nblk * nb), F32),
        compiler_params=_params(("parallel", "parallel")),
    )(a, blocks, *[x for x, _ in aux])


def _block_outer(a, b, nblk, *, tk, name):
    k = a.shape[0]
    ma, nb = a.shape[1] // nblk, b.shape[1] // nblk

    def body(a_ref, b_ref, o_ref):
        kk = pl.program_id(1)

        @pl.when(kk == 0)
        def _():
            o_ref[...] = jnp.zeros_like(o_ref)

        o_ref[...] += _dot(a_ref[...].astype(BF16), b_ref[...].astype(BF16), 0, 0)

    return pl.pallas_call(
        body, name=name, grid=(nblk, k // tk),
        in_specs=[pl.BlockSpec((tk, ma), lambda j, kk: (kk, j)), pl.BlockSpec((tk, nb), lambda j, kk: (kk, j))],
        out_specs=pl.BlockSpec((None, ma, nb), lambda j, kk: (j, 0, 0)),
        out_shape=jax.ShapeDtypeStruct((nblk, ma, nb), F32),
        compiler_params=_params(("parallel", "arbitrary")),
    )(a, b)


def _rmsnorm_fwd(x, g, *, tm, name):
    rows, d = x.shape

    def body(x_ref, g_ref, o_ref):
        xv = x_ref[...]
        r = lax.rsqrt(jnp.mean(xv * xv, axis=-1, keepdims=True) + RMS_EPS)
        o_ref[...] = (xv * r * g_ref[...]).astype(o_ref.dtype)

    return pl.pallas_call(
        body, name=name, grid=(rows // tm,),
        in_specs=[pl.BlockSpec((tm, d), lambda i: (i, 0)), pl.BlockSpec((1, d), lambda i: (0, 0))],
        out_specs=pl.BlockSpec((tm, d), lambda i: (i, 0)),
        out_shape=jax.ShapeDtypeStruct((rows, d), BF16),
        compiler_params=_params(("parallel",)),
    )(x, g)


def _rmsnorm_bwd(x, g, dy, res, *, tm, name):
    rows, d = x.shape
    has_res = res is not None

    def body(x_ref, g_ref, dy_ref, *rest):
        if has_res:
            res_ref, dx_ref, dg_ref = rest
        else:
            dx_ref, dg_ref = rest
        i = pl.program_id(0)
        xv = x_ref[...]
        r = lax.rsqrt(jnp.mean(xv * xv, axis=-1, keepdims=True) + RMS_EPS)
        xhat = xv * r
        dyv = dy_ref[...]
        dyg = dyv * g_ref[...]
        dx = r * (dyg - xhat * jnp.mean(dyg * xhat, axis=-1, keepdims=True))
        if has_res:
            dx = dx + res_ref[...]
        dx_ref[...] = dx

        @pl.when(i == 0)
        def _():
            dg_ref[...] = jnp.zeros_like(dg_ref)

        dg_ref[...] += jnp.sum(dyv * xhat, axis=0, keepdims=True)

    row_spec = pl.BlockSpec((tm, d), lambda i: (i, 0))
    vec_spec = pl.BlockSpec((1, d), lambda i: (0, 0))
    ins = [x, g, dy] + ([res] if has_res else [])
    return pl.pallas_call(
        body, name=name, grid=(rows // tm,),
        in_specs=[row_spec, vec_spec, row_spec] + ([row_spec] if has_res else []),
        out_specs=[row_spec, vec_spec],
        out_shape=[jax.ShapeDtypeStruct((rows, d), F32), jax.ShapeDtypeStruct((1, d), F32)],
        compiler_params=_params(("arbitrary",)),
    )(*ins)


def _loss_head(h, tgt, g, *, tm, name):
    rows, d = h.shape
    nsteps = rows // tm

    def body(h_ref, t_ref, g_ref, dh_ref, loss_ref, dg_ref, sq_ref):
        i = pl.program_id(0)
        xv = h_ref[...]
        gv = g_ref[...]
        r = lax.rsqrt(jnp.mean(xv * xv, axis=-1, keepdims=True) + RMS_EPS)
        xhat = xv * r
        err = xhat * gv - t_ref[...]
        dyv = err * (1.0 / d)
        dyg = dyv * gv
        dh_ref[...] = r * (dyg - xhat * jnp.mean(dyg * xhat, axis=-1, keepdims=True))

        @pl.when(i == 0)
        def _():
            dg_ref[...] = jnp.zeros_like(dg_ref)
            sq_ref[...] = jnp.zeros_like(sq_ref)

        dg_ref[...] += jnp.sum(dyv * xhat, axis=0, keepdims=True)
        sq_ref[...] += jnp.sum(err * err, axis=0, keepdims=True)

        @pl.when(i == nsteps - 1)
        def _():
            tot = jnp.sum(sq_ref[...], axis=-1, keepdims=True) * (0.5 / d)
            loss_ref[...] = jnp.broadcast_to(tot, loss_ref.shape)

    row_spec = pl.BlockSpec((tm, d), lambda i: (i, 0))
    vec_spec = pl.BlockSpec((1, d), lambda i: (0, 0))
    return pl.pallas_call(
        body, name=name, grid=(nsteps,),
        in_specs=[row_spec, row_spec, vec_spec],
        out_specs=[row_spec, pl.BlockSpec((1, LANES), lambda i: (0, 0)), vec_spec],
        out_shape=[jax.ShapeDtypeStruct((rows, d), F32), jax.ShapeDtypeStruct((1, LANES), F32),
                   jax.ShapeDtypeStruct((1, d), F32)],
        scratch_shapes=[pltpu.VMEM((1, d), F32)],
        compiler_params=_params(("arbitrary",)),
    )(h, tgt, g)


def _to_scan_layout(v):
    lead = v.shape[:-2]
    v = v.reshape(lead + (2, N_STATES // SCAN_CB, SCAN_CB))
    v = jnp.swapaxes(v, -3, -2)
    return v.reshape(lead + (2 * N_STATES,))


def _ssm_matrices(lam_re, lam_im, log_dt, b_re, b_im, c_re, c_im):
    dt = jnp.exp(log_dt)[:, None]
    mag = jnp.exp(lam_re * dt)
    a_re, a_im = mag * jnp.cos(lam_im * dt), mag * jnp.sin(lam_im * dt)
    nr, ni = a_re - 1.0, a_im
    den = lam_re * lam_re + lam_im * lam_im
    coef_re = (nr * lam_re + ni * lam_im) / den
    coef_im = (ni * lam_re - nr * lam_im) / den
    bb_re = coef_re[..., None] * b_re - coef_im[..., None] * b_im
    bb_im = coef_re[..., None] * b_im + coef_im[..., None] * b_re
    a_lay = _to_scan_layout(jnp.stack([a_re.reshape(-1), a_im.reshape(-1)], axis=0))[None, :]
    nblk = SSM_GROUPS // SCAN_GROUPS
    eye = jnp.eye(SCAN_GROUPS, dtype=F32)

    def b_block(bb):
        bb = bb.reshape(nblk, SCAN_GROUPS, SSM_STATE, SSM_GROUP_SIZE)
        return jnp.einsum("gk,jkph->jghkp", eye, bb).reshape(nblk, SCAN_GROUPS * SSM_GROUP_SIZE, SCAN_CB)

    b_blk = jnp.concatenate([b_block(bb_re), b_block(bb_im)], axis=2)

    def c_block(cc):
        cc = cc.reshape(nblk, SCAN_GROUPS, SSM_GROUP_SIZE, SSM_STATE)
        return jnp.einsum("gk,jghp->jkpgh", eye, cc).reshape(nblk, SCAN_CB, SCAN_GROUPS * SSM_GROUP_SIZE)

    c_blk = jnp.concatenate([c_block(c_re), -c_block(c_im)], axis=1)
    return a_lay, b_blk, c_blk


def _interleave(v):
    rows, c = v.shape
    return v.reshape(SCAN_SEGS, rows // SCAN_SEGS, c).transpose(1, 0, 2).reshape(rows, c)


def _deinterleave(v):
    rows, c = v.shape
    return v.reshape(rows // SCAN_SEGS, SCAN_SEGS, c).transpose(1, 0, 2).reshape(rows, c)


def _scan_groups(a_ref, bu_ref, o_ref, state, *, reverse, tt):
    cb = SCAN_CB
    ar = jnp.broadcast_to(a_ref[:, :cb], (SCAN_SEGS, cb))
    ai = jnp.broadcast_to(a_ref[:, cb:], (SCAN_SEGS, cb))
    ngroups = tt // SCAN_SEGS

    def step(i, st):
        sr, si = st
        r0 = pl.multiple_of(((ngroups - 1 - i) if reverse else i) * SCAN_SEGS, SCAN_SEGS)
        blk = bu_ref[pl.ds(r0, SCAN_SEGS), :]
        nr = ar * sr - ai * si + blk[:, :cb]
        ni = ar * si + ai * sr + blk[:, cb:]
        if o_ref is not None:
            o_ref[pl.ds(r0, SCAN_SEGS), :] = jnp.concatenate([nr, ni], axis=1)
        return nr, ni

    return lax.fori_loop(0, ngroups, step, state, unroll=4)


def _scan_ends(a_lay, bu, *, reverse, tt, name):
    rows, width = bu.shape
    cb = SCAN_CB
    nt = rows // tt

    def body(a_ref, bu_ref, e_ref):
        kk = pl.program_id(1)

        @pl.when(kk == 0)
        def _():
            e_ref[...] = jnp.zeros_like(e_ref)

        sr, si = _scan_groups(a_ref, bu_ref, None, (e_ref[:, :cb], e_ref[:, cb:]), reverse=reverse, tt=tt)
        e_ref[...] = jnp.concatenate([sr, si], axis=1)

    tmap = (lambda j, kk: (nt - 1 - kk, j)) if reverse else (lambda j, kk: (kk, j))
    return pl.pallas_call(
        body, name=name, grid=(width // (2 * cb), nt),
        in_specs=[pl.BlockSpec((1, 2 * cb), lambda j, kk: (0, j)), pl.BlockSpec((tt, 2 * cb), tmap)],
        out_specs=pl.BlockSpec((SCAN_SEGS, 2 * cb), lambda j, kk: (0, j)),
        out_shape=jax.ShapeDtypeStruct((SCAN_SEGS, width), F32),
        compiler_params=_params(("parallel", "arbitrary")),
    )(a_lay, bu)


def _scan_apply(a_lay, bu, ends, *, reverse, tt, name):
    rows, width = bu.shape
    cb = SCAN_CB
    nt = rows // tt
    seg_len = rows // SCAN_SEGS
    n_sq = seg_len.bit_length() - 1
    assert 1 << n_sq == seg_len, seg_len

    def body(a_ref, e_ref, bu_ref, o_ref, init_ref, carry_ref):
        kk = pl.program_id(1)

        @pl.when(kk == 0)
        def _():
            pr, pi = a_ref[:, :cb], a_ref[:, cb:]
            for _ in range(n_sq):
                pr, pi = pr * pr - pi * pi, 2.0 * pr * pi
            cr = jnp.zeros((1, cb), F32)
            ci = jnp.zeros((1, cb), F32)
            order = range(SCAN_SEGS - 1, -1, -1) if reverse else range(SCAN_SEGS)
            for k, seg in enumerate(order):
                if k > 0:
                    prev = seg + 1 if reverse else seg - 1
                    er, ei = e_ref[prev:prev + 1, :cb], e_ref[prev:prev + 1, cb:]
                    cr, ci = pr * cr - pi * ci + er, pr * ci + pi * cr + ei
                init_ref[seg:seg + 1, :] = jnp.concatenate([cr, ci], axis=1)
            carry_ref[...] = init_ref[...]

        sr, si = _scan_groups(a_ref, bu_ref, o_ref, (carry_ref[:, :cb], carry_ref[:, cb:]), reverse=reverse, tt=tt)
        carry_ref[...] = jnp.concatenate([sr, si], axis=1)

    tmap = (lambda j, kk: (nt - 1 - kk, j)) if reverse else (lambda j, kk: (kk, j))
    seg_spec = pl.BlockSpec((SCAN_SEGS, 2 * cb), lambda j, kk: (0, j))
    return pl.pallas_call(
        body, name=name, grid=(width // (2 * cb), nt),
        in_specs=[pl.BlockSpec((1, 2 * cb), lambda j, kk: (0, j)), seg_spec, pl.BlockSpec((tt, 2 * cb), tmap)],
        out_specs=[pl.BlockSpec((tt, 2 * cb), tmap), seg_spec],
        out_shape=[jax.ShapeDtypeStruct((rows, width), F32), jax.ShapeDtypeStruct((SCAN_SEGS, width), F32)],
        scratch_shapes=[pltpu.VMEM((SCAN_SEGS, 2 * cb), F32)],
        compiler_params=_params(("parallel", "arbitrary")),
    )(a_lay, ends, bu)


def _ssm_da(lam, s, entry, *, tt, name):
    rows, width = s.shape
    cb = SCAN_CB
    ngroups = tt // SCAN_SEGS

    def body(lam_ref, s_ref, entry_ref, da_ref, prev_ref):
        kk = pl.program_id(1)

        @pl.when(kk == 0)
        def _():
            prev_ref[...] = entry_ref[...]
            da_ref[...] = jnp.zeros_like(da_ref)

        def step(i, st):
            accr, acci, pr, pi = st
            r0 = pl.multiple_of(i * SCAN_SEGS, SCAN_SEGS)
            lv = lam_ref[pl.ds(r0, SCAN_SEGS), :]
            sv = s_ref[pl.ds(r0, SCAN_SEGS), :]
            lr, li = lv[:, :cb], lv[:, cb:]
            return accr + (lr * pr + li * pi), acci + (li * pr - lr * pi), sv[:, :cb], sv[:, cb:]

        zero = jnp.zeros((SCAN_SEGS, cb), F32)
        accr, acci, pr, pi = lax.fori_loop(0, ngroups, step, (zero, zero, prev_ref[:, :cb], prev_ref[:, cb:]), unroll=4)
        prev_ref[...] = jnp.concatenate([pr, pi], axis=1)
        da_ref[...] += jnp.concatenate([jnp.sum(accr, axis=0, keepdims=True), jnp.sum(acci, axis=0, keepdims=True)], axis=1)

    blk = pl.BlockSpec((tt, 2 * cb), lambda j, kk: (kk, j))
    return pl.pallas_call(
        body, name=name, grid=(width // (2 * cb), rows // tt),
        in_specs=[blk, blk, pl.BlockSpec((SCAN_SEGS, 2 * cb), lambda j, kk: (0, j))],
        out_specs=pl.BlockSpec((1, 2 * cb), lambda j, kk: (0, j)),
        out_shape=jax.ShapeDtypeStruct((1, width), F32),
        scratch_shapes=[pltpu.VMEM((SCAN_SEGS, 2 * cb), F32)],
        compiler_params=_params(("parallel", "arbitrary")),
    )(lam, s, entry)


def _glu_fwd(ys, u, dd, w_glu, b_glu, *, tm, name):
    rows, w = ys.shape

    def body(ys_ref, u_ref, dd_ref, w_ref, b_ref, y0_ref, t_ref, y2_ref):
        y0 = ys_ref[...] + dd_ref[...] * u_ref[...]
        y1 = _gelu(y0)
        t = _dot(y1.astype(BF16), w_ref[...], 1, 0) + b_ref[...]
        y0_ref[...] = y0
        t_ref[...] = t
        y2_ref[...] = (y1 * _sigmoid(t)).astype(BF16)

    row = pl.BlockSpec((tm, w), lambda i: (i, 0))
    vec = pl.BlockSpec((1, w), lambda i: (0, 0))
    return pl.pallas_call(
        body, name=name, grid=(rows // tm,),
        in_specs=[row, row, vec, pl.BlockSpec((w, w), lambda i: (0, 0)), vec],
        out_specs=[row, row, row],
        out_shape=[jax.ShapeDtypeStruct((rows, w), F32), jax.ShapeDtypeStruct((rows, w), F32),
                   jax.ShapeDtypeStruct((rows, w), BF16)],
        compiler_params=_params(("parallel",)),
    )(ys, u, dd, w_glu, b_glu)


def _glu_bwd(dy2, y0, t, u, w_glu, *, tm, name):
    rows, w = y0.shape

    def body(dy2_ref, y0_ref, t_ref, u_ref, w_ref, dy0_ref, dt_ref, y1_ref, db_ref, dd_ref):
        i = pl.program_id(0)
        y0 = y0_ref[...]
        y1 = _gelu(y0)
        sg = _sigmoid(t_ref[...])
        dy2v = dy2_ref[...]
        dt = dy2v * y1 * sg * (1.0 - sg)
        dy1 = dy2v * sg + _dot(dt.astype(BF16), w_ref[...], 1, 1)
        dy0 = dy1 * _gelu_grad(y0)
        dy0_ref[...] = dy0
        dt_ref[...] = dt.astype(BF16)
        y1_ref[...] = y1.astype(BF16)

        @pl.when(i == 0)
        def _():
            db_ref[...] = jnp.zeros_like(db_ref)
            dd_ref[...] = jnp.zeros_like(dd_ref)

        db_ref[...] += jnp.sum(dt, axis=0, keepdims=True)
        dd_ref[...] += jnp.sum(dy0 * u_ref[...], axis=0, keepdims=True)

    row = pl.BlockSpec((tm, w), lambda i: (i, 0))
    vec = pl.BlockSpec((1, w), lambda i: (0, 0))
    return pl.pallas_call(
        body, name=name, grid=(rows // tm,),
        in_specs=[row, row, row, row, pl.BlockSpec((w, w), lambda i: (0, 0))],
        out_specs=[row, row, row, vec, vec],
        out_shape=[jax.ShapeDtypeStruct((rows, w), F32), jax.ShapeDtypeStruct((rows, w), BF16),
                   jax.ShapeDtypeStruct((rows, w), BF16), jax.ShapeDtypeStruct((1, w), F32),
                   jax.ShapeDtypeStruct((1, w), F32)],
        compiler_params=_params(("arbitrary",)),
    )(dy2, y0, t, u, w_glu)


ATTN_TILE = 2048


def _attn_geometry(rows, d):
    sb = ATTN_Q * d
    tr = max(sb, min(ATTN_TILE, rows))
    assert rows % tr == 0 and tr % sb == 0, (rows, d)
    return sb, tr, rows // tr, tr // sb


def _attn_masks():
    qi = lax.broadcasted_iota(jnp.int32, (2 * ATTN_Q, 2 * ATTN_Q), 0) % ATTN_Q
    kj = lax.broadcasted_iota(jnp.int32, (2 * ATTN_Q, 2 * ATTN_Q), 1)
    own_ok = jnp.logical_and(kj >= ATTN_Q, kj - ATTN_Q <= qi)
    prev_ok = jnp.logical_and(kj < ATTN_Q, kj >= qi)
    bias_first = jnp.where(own_ok, 0.0, NEG_INF)
    bias_other = jnp.where(jnp.logical_or(own_ok, prev_ok), 0.0, NEG_INF)
    head0 = lax.broadcasted_iota(jnp.int32, (ATTN_Q, LANES), 1) < ATTN_HEAD_DIM
    return bias_first, bias_other, head0


def _attn_rows(base, n, d):
    return pl.ds(pl.multiple_of(base, ATTN_Q), n) if d == 1 else pl.ds(base, n, stride=d)


def _stack_heads(v, head0):
    return jnp.concatenate([jnp.where(head0, v, 0.0), jnp.where(head0, 0.0, v)], axis=0)


def _unstack_heads(v, head0):
    return jnp.where(head0, v[:ATTN_Q], v[ATTN_Q:])


def _fill_keys(buf, prev_ref, cur_ref, sb):
    buf[pl.ds(0, sb), :] = prev_ref[...]
    buf[pl.ds(sb, cur_ref.shape[0]), :] = cur_ref[...]


def _attn_fwd(qkv, g, d, *, name):
    rows = qkv.shape[0]
    sb, tr, ntiles, nsub = _attn_geometry(rows, d)
    qc, kc, vc = 2 * g, 6 + 2 * g, 12 + 2 * g
    scale = ATTN_HEAD_DIM ** -0.5

    def body(q_ref, kc_ref, kp_ref, vc_ref, vp_ref, o_ref, lse_ref, kbuf, vbuf):
        n = pl.program_id(0)
        _fill_keys(kbuf, kp_ref, kc_ref, sb)
        _fill_keys(vbuf, vp_ref, vc_ref, sb)
        bias_first, bias_other, head0 = _attn_masks()

        def per_block(idx, carry):
            j, r = idx // d, idx % d
            base = j * sb + r
            bias = jnp.where(jnp.logical_and(n == 0, j == 0), bias_first, bias_other)
            qrows = _attn_rows(base, ATTN_Q, d)
            krows = _attn_rows(base, 2 * ATTN_Q, d)
            qs = _stack_heads(q_ref[qrows, :], head0).astype(BF16)
            s = _dot(qs, kbuf[krows, :].astype(BF16), 1, 1) * scale + bias
            mx = jnp.max(s, axis=-1, keepdims=True)
            p = jnp.exp(s - mx)
            den = jnp.sum(p, axis=-1, keepdims=True)
            pv = _dot(p.astype(BF16), vbuf[krows, :].astype(BF16), 1, 0) / den
            o_ref[qrows, :] = _unstack_heads(pv, head0)
            lse_ref[qrows, :] = _unstack_heads(jnp.broadcast_to(mx + jnp.log(den), (2 * ATTN_Q, LANES)), head0)
            return carry

        lax.fori_loop(0, nsub * d, per_block, 0)

    def cur(col):
        return pl.BlockSpec((tr, LANES), lambda n, hp: (n, col + hp))

    def prev(col):
        return pl.BlockSpec((sb, LANES), lambda n, hp: (jnp.maximum(n * nsub - 1, 0), col + hp))

    out_spec = pl.BlockSpec((tr, LANES), lambda n, hp: (n, hp))
    return pl.pallas_call(
        body, name=name, grid=(ntiles, 2),
        in_specs=[cur(qc), cur(kc), prev(kc), cur(vc), prev(vc)],
        out_specs=[out_spec, out_spec],
        out_shape=[jax.ShapeDtypeStruct((rows, 2 * LANES), F32), jax.ShapeDtypeStruct((rows, 2 * LANES), F32)],
        scratch_shapes=[pltpu.VMEM((sb + tr, LANES), F32), pltpu.VMEM((sb + tr, LANES), F32)],
        compiler_params=_params(("parallel", "parallel")),
    )(qkv, qkv, qkv, qkv, qkv)


def _attn_merge(outs, lses, *, tm, name):
    rows, w = outs[0].shape

    def body(o0, o1, o2, l0, l1, l2, o_ref, lse_ref):
        a0, a1, a2 = l0[...], l1[...], l2[...]
        mx = jnp.maximum(jnp.maximum(a0, a1), a2)
        e0, e1, e2 = jnp.exp(a0 - mx), jnp.exp(a1 - mx), jnp.exp(a2 - mx)
        den = e0 + e1 + e2
        o_ref[...] = (e0 / den) * o0[...] + (e1 / den) * o1[...] + (e2 / den) * o2[...]
        lse_ref[...] = mx + jnp.log(den)

    row = pl.BlockSpec((tm, w), lambda i: (i, 0))
    return pl.pallas_call(
        body, name=name, grid=(rows // tm,), in_specs=[row] * 6, out_specs=[row, row],
        out_shape=[jax.ShapeDtypeStruct((rows, w), F32), jax.ShapeDtypeStruct((rows, w), F32)],
        compiler_params=_params(("parallel",)),
    )(*outs, *lses)


def _attn_bwd(qkv, do, o, lse, g, d, prev, *, name):
    rows = qkv.shape[0]
    sb, tr, ntiles, nsub = _attn_geometry(rows, d)
    qc, kc, vc = 2 * g, 6 + 2 * g, 12 + 2 * g
    scale = ATTN_HEAD_DIM ** -0.5

    def body(q_ref, kc_ref, kp_ref, vc_ref, vp_ref, do_ref, o_ref, lse_ref, dq_ref, dk_ref, dv_ref,
             kbuf, vbuf, dk_acc, dv_acc):
        n = pl.program_id(1)

        @pl.when(n == 0)
        def _():
            dk_acc[pl.ds(0, tr), :] = jnp.zeros((tr, LANES), F32)
            dv_acc[pl.ds(0, tr), :] = jnp.zeros((tr, LANES), F32)

        @pl.when(n < ntiles)
        def _():
            dk_acc[pl.ds(tr, tr), :] = jnp.zeros((tr, LANES), F32)
            dv_acc[pl.ds(tr, tr), :] = jnp.zeros((tr, LANES), F32)
            _fill_keys(kbuf, kp_ref, kc_ref, sb)
            _fill_keys(vbuf, vp_ref, vc_ref, sb)
            bias_first, bias_other, head0 = _attn_masks()
            lane = lax.broadcasted_iota(jnp.int32, (ATTN_Q, LANES), 1)

            def per_block(idx, carry):
                j, r = idx // d, idx % d
                base = j * sb + r
                bias = jnp.where(jnp.logical_and(n == 0, j == 0), bias_first, bias_other)
                qrows = _attn_rows(base, ATTN_Q, d)
                krows = _attn_rows(base, 2 * ATTN_Q, d)
                arows = _attn_rows(base + (tr - sb), 2 * ATTN_Q, d)
                qs = _stack_heads(q_ref[qrows, :], head0).astype(BF16)
                dos = _stack_heads(do_ref[qrows, :], head0)
                dosb = dos.astype(BF16)
                ov = o_ref[qrows, :]
                delta = jnp.sum(dos * jnp.concatenate([ov, ov], axis=0), axis=-1, keepdims=True)
                lsev = lse_ref[qrows, :]
                lse_s = jnp.concatenate(
                    [jnp.sum(jnp.where(lane == h * ATTN_HEAD_DIM, lsev, 0.0), axis=-1, keepdims=True) for h in range(2)], axis=0)
                kb = kbuf[krows, :].astype(BF16)
                vb = vbuf[krows, :].astype(BF16)
                p = jnp.exp(_dot(qs, kb, 1, 1) * scale + bias - lse_s)
                ds = (p * (_dot(dosb, vb, 1, 1) - delta) * scale).astype(BF16)
                dq_ref[qrows, :] = _unstack_heads(_dot(ds, kb, 1, 0), head0)
                dk_acc[arows, :] += _dot(ds, qs, 0, 0)
                dv_acc[arows, :] += _dot(p.astype(BF16), dosb, 0, 0)
                return carry

            lax.fori_loop(0, nsub * d, per_block, 0)

        dk_ref[...] = dk_acc[pl.ds(0, tr), :]
        dv_ref[...] = dv_acc[pl.ds(0, tr), :]
        dk_acc[pl.ds(0, tr), :] = dk_acc[pl.ds(tr, tr), :]
        dv_acc[pl.ds(0, tr), :] = dv_acc[pl.ds(tr, tr), :]

    def cur(n):
        return jnp.minimum(n, ntiles - 1)

    def spec(col, prev):
        if prev:
            return pl.BlockSpec((sb, LANES), lambda hp, n: (jnp.maximum(cur(n) * nsub - 1, 0), col + hp))
        return pl.BlockSpec((tr, LANES), lambda hp, n: (cur(n), col + hp))

    row_spec = pl.BlockSpec((tr, LANES), lambda hp, n: (cur(n), hp))
    dq_out = pl.BlockSpec((tr, LANES), lambda hp, n: (cur(n), 2 * g + hp))
    kv_out = pl.BlockSpec((tr, LANES), lambda hp, n: (jnp.maximum(n - 1, 0), 2 * g + hp))
    shape = jax.ShapeDtypeStruct((rows, len(ATTN_PATTERNS) * 2 * LANES), F32)
    ins = [qkv, qkv, qkv, qkv, qkv, do, o, lse]
    in_specs = [spec(qc, False), spec(kc, False), spec(kc, True), spec(vc, False), spec(vc, True),
                row_spec, row_spec, row_spec]
    aliases = {}
    if prev is not None:
        aliases = {len(ins) + t: t for t in range(3)}
        ins = ins + list(prev)
        in_specs = in_specs + [ANY] * 3
    n_in = len(ins)

    def entry(*refs):
        body(*refs[:8], *refs[n_in:])

    return pl.pallas_call(
        entry, name=name, grid=(2, ntiles + 1),
        in_specs=in_specs,
        out_specs=[dq_out, kv_out, kv_out],
        out_shape=[shape, shape, shape],
        input_output_aliases=aliases,
        scratch_shapes=[pltpu.VMEM((sb + tr, LANES), F32), pltpu.VMEM((sb + tr, LANES), F32),
                        pltpu.VMEM((2 * tr, LANES), F32), pltpu.VMEM((2 * tr, LANES), F32)],
        compiler_params=_params(("parallel", "arbitrary")),
    )(*ins)


def _mem_probs(q, k):
    s = _dot(q.astype(BF16), k.astype(BF16), 1, 1) * (MEM_HEAD_DIM ** -0.5)
    e = jnp.exp(s - jnp.max(s, axis=-1, keepdims=True))
    return e / jnp.sum(e, axis=-1, keepdims=True)


def _mem_attn_fwd(mq, kv, *, tq, name):
    rows = mq.shape[0]

    def body(q_ref, k_ref, v_ref, o_ref):
        p = _mem_probs(q_ref[...], k_ref[...])
        o_ref[...] = _dot(p.astype(BF16), v_ref[...].astype(BF16), 1, 0)

    return pl.pallas_call(
        body, name=name, grid=(rows // tq, MEM_HEADS),
        in_specs=[pl.BlockSpec((tq, LANES), lambda i, h: (i, h)),
                  pl.BlockSpec((MEM_LEN, LANES), lambda i, h: (0, h)),
                  pl.BlockSpec((MEM_LEN, LANES), lambda i, h: (0, MEM_HEADS + h))],
        out_specs=pl.BlockSpec((tq, LANES), lambda i, h: (i, h)),
        out_shape=jax.ShapeDtypeStruct((rows, MEM_HEADS * LANES), F32),
        compiler_params=_params(("parallel", "parallel")),
    )(mq, kv, kv)


def _mem_attn_bwd(mq, kv, dmo, *, tq, name):
    rows = mq.shape[0]
    scale = MEM_HEAD_DIM ** -0.5

    def body(q_ref, k_ref, v_ref, do_ref, dq_ref, dk_ref, dv_ref):
        i = pl.program_id(1)
        qb = q_ref[...].astype(BF16)
        kb = k_ref[...].astype(BF16)
        vb = v_ref[...].astype(BF16)
        dob = do_ref[...].astype(BF16)
        p = _mem_probs(q_ref[...], k_ref[...])
        dp = _dot(dob, vb, 1, 1)
        ds = (p * (dp - jnp.sum(p * dp, axis=-1, keepdims=True)) * scale).astype(BF16)
        dq_ref[...] = _dot(ds, kb, 1, 0).astype(dq_ref.dtype)

        @pl.when(i == 0)
        def _():
            dk_ref[...] = jnp.zeros_like(dk_ref)
            dv_ref[...] = jnp.zeros_like(dv_ref)

        dk_ref[...] += _dot(ds, qb, 0, 0)
        dv_ref[...] += _dot(p.astype(BF16), dob, 0, 0)

    kv_out = pl.BlockSpec((MEM_LEN, LANES), lambda h, i: (0, h))
    kv_shape = jax.ShapeDtypeStruct((MEM_LEN, MEM_HEADS * LANES), F32)
    return pl.pallas_call(
        body, name=name, grid=(MEM_HEADS, rows // tq),
        in_specs=[pl.BlockSpec((tq, LANES), lambda h, i: (i, h)),
                  pl.BlockSpec((MEM_LEN, LANES), lambda h, i: (0, h)),
                  pl.BlockSpec((MEM_LEN, LANES), lambda h, i: (0, MEM_HEADS + h)),
                  pl.BlockSpec((tq, LANES), lambda h, i: (i, h))],
        out_specs=[pl.BlockSpec((tq, LANES), lambda h, i: (i, h)), kv_out, kv_out],
        out_shape=[jax.ShapeDtypeStruct((rows, MEM_HEADS * LANES), BF16), kv_shape, kv_shape],
        compiler_params=_params(("parallel", "arbitrary")),
    )(mq, kv, kv, dmo)


def _gate_merge_fwd(zg, b_gate, br_s, br_a, br_m, *, tm, name):
    rows, d = br_s.shape

    def body(zg_ref, b_ref, s_ref, a_ref, m_ref, o_ref):
        gt = _sigmoid(zg_ref[...] + b_ref[...])
        o_ref[...] = (gt[:, :d] * s_ref[...] + gt[:, d:2 * d] * a_ref[...] + gt[:, 2 * d:] * m_ref[...]).astype(BF16)

    row = pl.BlockSpec((tm, d), lambda i: (i, 0))
    return pl.pallas_call(
        body, name=name, grid=(rows // tm,),
        in_specs=[pl.BlockSpec((tm, 3 * d), lambda i: (i, 0)), pl.BlockSpec((1, 3 * d), lambda i: (0, 0)), row, row, row],
        out_specs=row, out_shape=jax.ShapeDtypeStruct((rows, d), BF16),
        compiler_params=_params(("parallel",)),
    )(zg, b_gate, br_s, br_a, br_m)


def _gate_merge_bwd(dmerged, zg, b_gate, br_s, br_a, br_m, *, tm, name):
    rows, d = br_s.shape

    def body(dm_ref, zg_ref, b_ref, s_ref, a_ref, m_ref, ds_ref, da_ref, dmm_ref, dzg_ref, db_ref):
        i = pl.program_id(0)
        gt = _sigmoid(zg_ref[...] + b_ref[...])
        dm = dm_ref[...]
        parts = []
        for j, (br_ref, out_ref) in enumerate(((s_ref, ds_ref), (a_ref, da_ref), (m_ref, dmm_ref))):
            gj = gt[:, j * d:(j + 1) * d]
            out_ref[...] = (dm * gj).astype(BF16)
            parts.append(dm * br_ref[...] * gj * (1.0 - gj))
        dzg = jnp.concatenate(parts, axis=1)
        dzg_ref[...] = dzg.astype(BF16)

        @pl.when(i == 0)
        def _():
            db_ref[...] = jnp.zeros_like(db_ref)

        db_ref[...] += jnp.sum(dzg, axis=0, keepdims=True)

    row = pl.BlockSpec((tm, d), lambda i: (i, 0))
    wide = pl.BlockSpec((tm, 3 * d), lambda i: (i, 0))
    vec = pl.BlockSpec((1, 3 * d), lambda i: (0, 0))
    bshape = jax.ShapeDtypeStruct((rows, d), BF16)
    return pl.pallas_call(
        body, name=name, grid=(rows // tm,),
        in_specs=[row, wide, vec, row, row, row],
        out_specs=[row, row, row, wide, vec],
        out_shape=[bshape, bshape, bshape, jax.ShapeDtypeStruct((rows, 3 * d), BF16), jax.ShapeDtypeStruct((1, 3 * d), F32)],
        compiler_params=_params(("arbitrary",)),
    )(dmerged, zg, b_gate, br_s, br_a, br_m)


def _adamw(w, g, m, v, *, tr, name):
    rows, cols = w.shape
    assert rows % tr == 0, (name, rows, tr)

    def body(w_ref, g_ref, m_ref, v_ref, d_ref, nm_ref, nv_ref):
        gv = g_ref[...]
        m2 = ADAM_B1 * m_ref[...] + (1.0 - ADAM_B1) * gv
        v2 = ADAM_B2 * v_ref[...] + (1.0 - ADAM_B2) * (gv * gv)
        m_hat = m2 / (1.0 - ADAM_B1 ** ADAM_STEP)
        v_hat = v2 / (1.0 - ADAM_B2 ** ADAM_STEP)
        d_ref[...] = -ADAM_LR * (m_hat / (jnp.sqrt(v_hat) + ADAM_EPS) + ADAM_WD * w_ref[...])
        nm_ref[...] = m2
        nv_ref[...] = v2

    blk = pl.BlockSpec((tr, cols), lambda i: (i, 0))
    shape = jax.ShapeDtypeStruct((rows, cols), F32)
    return pl.pallas_call(
        body, name=name, grid=(rows // tr,), in_specs=[blk] * 4, out_specs=[blk] * 3,
        out_shape=[shape, shape, shape], compiler_params=_params(("parallel",)),
    )(w, g, m, v)


ANY = pl.BlockSpec(memory_space=pl.ANY)


def _position():
    return lax.axis_index("x"), lax.axis_index("y"), lax.axis_index("c")


def _other_chips(x, y):
    return ((1 - x, y), (x, 1 - y), (1 - x, 1 - y))


def _remote(src, dst, send_sem, recv_sem, dev):
    return pltpu.make_async_remote_copy(src_ref=src, dst_ref=dst, send_sem=send_sem, recv_sem=recv_sem,
                                        device_id=dev, device_id_type=MESH)


def _all_gather_weights(bufs):
    nb = len(bufs)

    def body(*refs):
        outs = refs[nb:2 * nb]
        send_sems, recv_sems = refs[2 * nb:]
        x, y, c = _position()
        chip = 2 * x + y
        sibling = (x, y, 1 - c)
        chips = _other_chips(x, y)

        def rows_of(i, owner, core):
            rs = bufs[i].shape[0] // N_CHIPS
            return pl.ds(pl.multiple_of(owner * rs + core * (rs // 2), 16), rs // 2)

        sends = []
        for i in range(nb):
            mine = outs[i].at[rows_of(i, chip, c)]
            for j, (px, py) in enumerate(chips):
                cp = _remote(mine, mine, send_sems.at[i, j], recv_sems.at[i, j], (px, py, c))
                cp.start()
                sends.append(cp)
        for i in range(nb):
            for j, (px, py) in enumerate(chips):
                landed = outs[i].at[rows_of(i, 2 * px + py, c)]
                _remote(landed, landed, send_sems.at[i, j], recv_sems.at[i, j], (px, py, c)).wait_recv()
                cp = _remote(landed, landed, send_sems.at[i, 3 + j], recv_sems.at[i, 3 + j], sibling)
                cp.start()
                sends.append(cp)
        for i in range(nb):
            for j, (px, py) in enumerate(chips):
                dst = outs[i].at[rows_of(i, 2 * px + py, 1 - c)]
                _remote(dst, dst, send_sems.at[i, 3 + j], recv_sems.at[i, 3 + j], sibling).wait_recv()
        for cp in sends:
            cp.wait_send()

    return pl.pallas_call(
        body, name="all_gather_weights", in_specs=[ANY] * nb, out_specs=[ANY] * nb,
        out_shape=[jax.ShapeDtypeStruct(b.shape, b.dtype) for b in bufs],
        input_output_aliases={i: i for i in range(nb)},
        scratch_shapes=[pltpu.SemaphoreType.DMA((nb, 6)), pltpu.SemaphoreType.DMA((nb, 6))],
    )(*bufs)


def _row_tile(rows):
    return max(t for t in range(16, min(rows, 512) + 1, 16) if rows % t == 0)


def _exchange_halves(grads):
    nb = len(grads)

    def body(*refs):
        ins, outs = refs[:nb], refs[nb:2 * nb]
        send_sems, recv_sems = refs[2 * nb:]
        x, y, c = _position()
        copies = []
        for i in range(nb):
            cp = _remote(ins[i].at[:, 1 - c], outs[i], send_sems.at[i], recv_sems.at[i], (x, y, 1 - c))
            cp.start()
            copies.append(cp)
        for cp in copies:
            cp.wait()

    return pl.pallas_call(
        body, name="grad_exchange_halves", in_specs=[ANY] * nb, out_specs=[ANY] * nb,
        out_shape=[jax.ShapeDtypeStruct((N_CHIPS, g.shape[2], g.shape[3]), F32) for g in grads],
        scratch_shapes=[pltpu.SemaphoreType.DMA((nb,)), pltpu.SemaphoreType.DMA((nb,))],
    )(*grads)


def _pair_sum(g4, got, c_arr, *, name):
    _, _, half, cols = g4.shape
    tr = _row_tile(half)

    def body(c_ref, g_ref, t_ref, p_ref, pb_ref):
        sm = g_ref[...] + t_ref[...]
        p_ref[...] = sm
        pb_ref[...] = sm.astype(BF16)

    blk = pl.BlockSpec((None, tr, cols), lambda j, i, c_ref: (j, i, 0))
    grid_spec = pltpu.PrefetchScalarGridSpec(
        num_scalar_prefetch=1, grid=(N_CHIPS, half // tr),
        in_specs=[pl.BlockSpec((None, None, tr, cols), lambda j, i, c_ref: (j, c_ref[0], i, 0)), blk],
        out_specs=[blk, blk])
    return pl.pallas_call(
        body, name=name, grid_spec=grid_spec,
        out_shape=[jax.ShapeDtypeStruct((N_CHIPS, half, cols), F32), jax.ShapeDtypeStruct((N_CHIPS, half, cols), BF16)],
        compiler_params=_params(("parallel", "parallel")),
    )(c_arr, g4, got)


def _scatter_to_owners(parts):
    nb = len(parts)

    def body(*refs):
        ins, outs = refs[:nb], refs[nb:2 * nb]
        send_sems, recv_sems = refs[2 * nb:]
        x, y, c = _position()
        copies = []
        for i in range(nb):
            for j, (px, py) in enumerate(_other_chips(x, y)):
                cp = _remote(ins[i].at[2 * px + py], outs[i].at[j], send_sems.at[i, j], recv_sems.at[i, j], (px, py, c))
                cp.start()
                copies.append(cp)
        for cp in copies:
            cp.wait()

    return pl.pallas_call(
        body, name="grad_scatter_to_owners", in_specs=[ANY] * nb, out_specs=[ANY] * nb,
        out_shape=[jax.ShapeDtypeStruct((3,) + p.shape[1:], p.dtype) for p in parts],
        scratch_shapes=[pltpu.SemaphoreType.DMA((nb, 3)), pltpu.SemaphoreType.DMA((nb, 3))],
    )(*parts)


def _owner_sum(p, got, chip_arr, c_arr, *, replicated, name):
    _, half, cols = p.shape
    tr = _row_tile(half)

    def body(chip_ref, c_ref, p_ref, r_ref, o_ref):
        o_ref[...] = ((p_ref[...] + r_ref[0].astype(F32)) + r_ref[1].astype(F32)) + r_ref[2].astype(F32)

    if replicated:
        out_spec = pl.BlockSpec((None, None, tr, cols), lambda i, chip_ref, c_ref: (chip_ref[0], c_ref[0], i, 0))
        out_shape = jax.ShapeDtypeStruct((N_CHIPS, 2, half, cols), F32)
    else:
        out_spec = pl.BlockSpec((None, tr, cols), lambda i, chip_ref, c_ref: (c_ref[0], i, 0))
        out_shape = jax.ShapeDtypeStruct((2, half, cols), F32)
    grid_spec = pltpu.PrefetchScalarGridSpec(
        num_scalar_prefetch=2, grid=(half // tr,),
        in_specs=[pl.BlockSpec((None, tr, cols), lambda i, chip_ref, c_ref: (chip_ref[0], i, 0)),
                  pl.BlockSpec((3, tr, cols), lambda i, chip_ref, c_ref: (0, i, 0))],
        out_specs=out_spec)
    return pl.pallas_call(
        body, name=name, grid_spec=grid_spec, out_shape=out_shape,
        compiler_params=_params(("parallel",)),
    )(chip_arr, c_arr, p, got)


def _share_reduced(bufs):
    nb = len(bufs) - 1

    def body(*refs):
        outs = refs[nb + 1:2 * nb + 2]
        send_sems, recv_sems = refs[2 * nb + 2:]
        x, y, c = _position()
        chip = 2 * x + y
        sends = []
        for i in range(nb):
            cp = _remote(outs[i].at[c], outs[i].at[c], send_sems.at[i], recv_sems.at[i], (x, y, 1 - c))
            cp.start()
            sends.append(cp)
        small = outs[nb]
        peers = [(fx, fy, fc) for fx in (0, 1) for fy in (0, 1) for fc in (0, 1) if fx + fy + fc > 0]
        for k, (fx, fy, fc) in enumerate(peers):
            dev = (x ^ fx, y ^ fy, c ^ fc)
            cp = _remote(small.at[chip, c], small.at[chip, c], send_sems.at[nb + k], recv_sems.at[nb + k], dev)
            cp.start()
            sends.append(cp)
        for i in range(nb):
            dst = outs[i].at[1 - c]
            _remote(dst, dst, send_sems.at[i], recv_sems.at[i], (x, y, 1 - c)).wait_recv()
        for k, (fx, fy, fc) in enumerate(peers):
            dst = small.at[2 * (x ^ fx) + (y ^ fy), c ^ fc]
            _remote(dst, dst, send_sems.at[nb + k], recv_sems.at[nb + k], (x ^ fx, y ^ fy, c ^ fc)).wait_recv()
        for cp in sends:
            cp.wait_send()

    n_all = nb + 1
    return pl.pallas_call(
        body, name="grad_share_reduced", in_specs=[ANY] * n_all, out_specs=[ANY] * n_all,
        out_shape=[jax.ShapeDtypeStruct(b.shape, b.dtype) for b in bufs],
        input_output_aliases={i: i for i in range(n_all)},
        scratch_shapes=[pltpu.SemaphoreType.DMA((nb + 7,)), pltpu.SemaphoreType.DMA((nb + 7,))],
    )(*bufs)


def _pack_small(vals):
    flat = jnp.concatenate([vals[name].reshape(-1) for name, _ in SMALL])
    return jnp.pad(flat, (0, N_CHIPS * SMALL_ROWS * 1024 - SMALL_ELEMS)).reshape(N_CHIPS * SMALL_ROWS, 1024)


def _unpack_small(buf):
    flat = buf.reshape(-1)
    out, off = {}, 0
    for name, shape in SMALL:
        n = int(np.prod(shape))
        out[name] = flat[off:off + n].reshape(shape)
        off += n
    return out


def _device_step(x, mem, tgt, w, p):
    rows = x.shape[0]
    g1, gm, g2 = p["norm1_g"], p["mem_norm_g"], p["norm2_g"]
    gf = p["final_g"].reshape(1, D_MODEL)
    ssm_args = (p["ssm_lambda_re"][0], p["ssm_lambda_im"][0], p["ssm_log_dt"][0], p["ssm_b_re"][0],
                p["ssm_b_im"][0], p["ssm_c_re"][0], p["ssm_c_im"][0])
    (a_lay, b_blk, c_blk), ssm_vjp = jax.vjp(_ssm_matrices, *ssm_args)
    a_conj = a_lay * _to_scan_layout(jnp.stack([jnp.ones((N_STATES,), F32), -jnp.ones((N_STATES,), F32)]))[None, :]
    dd = p["ssm_d"].reshape(1, SSM_WIDTH)
    win_t = w["w_in"]
    mm = _matmul

    n1 = _rmsnorm_fwd(x, g1, tm=512, name="norm1")
    u = mm(n1, win_t, m=rows, n=512, k=1024, tb=True, tm=2048, tn=512, tk=1024, out_dtypes=(F32,), name="in_u")
    qkv = mm(n1, win_t, m=rows, n=2304, k=1024, tb=True, tm=2048, tn=256, tk=1024, b_off=(OFF_QKV // 256, 0),
             out_dtypes=(F32,), name="in_qkv")
    mq = mm(n1, win_t, m=rows, n=512, k=1024, tb=True, tm=2048, tn=256, tk=1024, b_off=(OFF_MQ // 256, 0),
            out_dtypes=(F32,), name="in_mq")
    zg = mm(n1, win_t, m=rows, n=3072, k=1024, tb=True, tm=2048, tn=256, tk=1024, b_off=(OFF_ZG // 256, 0),
            out_dtypes=(F32,), name="in_zg")

    u_i = _interleave(u)
    bu = _block_matmul(u_i, b_blk, transpose=False, tm=1024, name="ssm_bu")
    ends = _scan_ends(a_lay, bu, reverse=False, tt=512, name="ssm_scan_fwd_ends")
    s, s_entry = _scan_apply(a_lay, bu, ends, reverse=False, tt=512, name="ssm_scan_fwd")
    ys = _deinterleave(_block_matmul(s, c_blk, transpose=False, tm=1024, name="ssm_cs"))
    y0, tglu, y2 = _glu_fwd(ys, u, dd, w["w_glu"], p["b_glu"], tm=512, name="glu_fwd")
    br_s = mm(y2, w["w_ssm_br"], m=rows, n=1024, k=512, tb=True, tm=1024, tn=1024, tk=512, out_dtypes=(F32,), name="br_ssm")

    outs, lses = [], []
    for g, (_, d) in enumerate(ATTN_PATTERNS):
        o_g, lse_g = _attn_fwd(qkv, g, d, name=f"attn_fwd_{g}")
        outs.append(o_g)
        lses.append(lse_g)
    o, lse = _attn_merge(outs, lses, tm=1024, name="attn_merge")
    br_a = mm(o, w["w_attn_br"], m=rows, n=1024, k=256, tb=True, tm=1024, tn=1024, tk=256, out_dtypes=(F32,), name="br_attn")

    mn = _rmsnorm_fwd(mem, gm, tm=MEM_LEN, name="mem_norm")
    kv = mm(mn, w["w_mem_kv"], m=MEM_LEN, n=1024, k=1024, tm=MEM_LEN, tn=1024, tk=1024, out_dtypes=(F32,), name="mem_kv")
    mo = _mem_attn_fwd(mq, kv, tq=1024, name="mem_attn_fwd")
    br_m = mm(mo, w["w_mem_br"], m=rows, n=1024, k=512, tb=True, tm=1024, tn=1024, tk=512, out_dtypes=(F32,), name="br_mem")

    merged = _gate_merge_fwd(zg, p["b_gate"], br_s, br_a, br_m, tm=256, name="gate_merge_fwd")
    add = lambda acc, r: (acc + r,)
    h1 = mm(merged, w["w_o"], m=rows, n=1024, k=1024, tm=1024, tn=1024, tk=1024, out_dtypes=(F32,),
            aux=((x, "mn"),), epilogue=add, name="out_proj")
    n2 = _rmsnorm_fwd(h1, g2, tm=512, name="norm2")
    relu2 = lambda acc: (jnp.square(jnp.maximum(acc, 0.0)),)
    act = mm(n2, w["w_up"], m=rows, n=D_FF, k=1024, tb=True, tm=1024, tn=1024, tk=1024, out_dtypes=(BF16,),
             epilogue=relu2, name="mlp_up")
    h2 = mm(act, w["w_down"], m=rows, n=1024, k=D_FF, tm=1024, tn=1024, tk=1024, out_dtypes=(F32,),
            aux=((h1, "mn"),), epilogue=add, name="mlp_down")
    dh2, loss, d_gf = _loss_head(h2, tgt, gf, tm=512, name="loss_head")

    gb = {}
    gs = {"final_g": d_gf.reshape(D_MODEL)}
    drelu2 = lambda acc, actv: (acc * (2.0 * jnp.sqrt(actv.astype(F32))),)
    dup = mm(dh2, w["w_down"], m=rows, n=D_FF, k=1024, tb=True, tm=1024, tn=1024, tk=1024, out_dtypes=(BF16,),
             aux=((act, "mn"),), epilogue=drelu2, name="d_act")
    gb["w_down"] = mm(act, dh2, m=D_FF, n=1024, k=rows, ta=True, tm=1024, tn=1024, tk=1024, out_dtypes=(F32,), name="dw_down")
    dn2 = mm(dup, w["w_up"], m=rows, n=1024, k=D_FF, tm=1024, tn=1024, tk=1024, out_dtypes=(F32,), name="d_n2")
    gb["w_up"] = mm(dup, n2, m=D_FF, n=1024, k=rows, ta=True, tm=1024, tn=1024, tk=1024, out_dtypes=(F32,), name="dw_up")
    dh1, gs["norm2_g"] = _rmsnorm_bwd(h1, g2, dn2, dh2, tm=512, name="norm2_bwd")
    dmerged = mm(dh1, w["w_o"], m=rows, n=1024, k=1024, tb=True, tm=1024, tn=1024, tk=1024, out_dtypes=(F32,), name="d_merged")
    gb["w_o"] = mm(merged, dh1, m=1024, n=1024, k=rows, ta=True, tm=1024, tn=1024, tk=1024, out_dtypes=(F32,), name="dw_o")
    dbr_s, dbr_a, dbr_m, dzg, gs["b_gate"] = _gate_merge_bwd(dmerged, zg, p["b_gate"], br_s, br_a, br_m, tm=256,
                                                              name="gate_merge_bwd")

    dy2 = mm(dbr_s, w["w_ssm_br"], m=rows, n=512, k=1024, tm=1024, tn=512, tk=1024, out_dtypes=(F32,), name="d_y2")
    gb["w_ssm_br"] = mm(dbr_s, y2, m=1024, n=512, k=rows, ta=True, tm=1024, tn=512, tk=1024, out_dtypes=(F32,), name="dw_ssm_br")
    dy0, dt, y1, gs["b_glu"], d_dd = _glu_bwd(dy2, y0, tglu, u, w["w_glu"], tm=512, name="glu_bwd")
    gs["ssm_d"] = d_dd.reshape(1, SSM_GROUPS, SSM_GROUP_SIZE)
    gb["w_glu"] = mm(y1, dt, m=512, n=512, k=rows, ta=True, tm=512, tn=512, tk=1024, out_dtypes=(F32,), name="dw_glu")
    dy0_i = _interleave(dy0)
    dsout = _block_matmul(dy0_i, c_blk, transpose=True, tm=1024, name="ssm_dsout")
    lam_ends = _scan_ends(a_conj, dsout, reverse=True, tt=512, name="ssm_scan_bwd_ends")
    lam, _ = _scan_apply(a_conj, dsout, lam_ends, reverse=True, tt=512, name="ssm_scan_bwd")
    skip = lambda acc, dyv, ddv: (acc + ddv * dyv,)
    du = _deinterleave(_block_matmul(lam, b_blk, transpose=True, tm=1024, aux=((dy0_i, "mn"), (dd, "row")), epilogue=skip,
                                     name="ssm_du"))
    nblk = b_blk.shape[0]
    d_b_blk = _block_outer(u_i, lam, nblk, tk=1024, name="ssm_db")
    d_c_blk = _block_outer(s, dy0_i, nblk, tk=1024, name="ssm_dc")
    d_a_lay = _ssm_da(lam, s, s_entry, tt=512, name="ssm_da")
    d_ssm = ssm_vjp((d_a_lay, d_b_blk, d_c_blk))
    for name, val in zip(("ssm_lambda_re", "ssm_lambda_im", "ssm_log_dt", "ssm_b_re", "ssm_b_im", "ssm_c_re", "ssm_c_im"), d_ssm):
        gs[name] = val[None]

    do = mm(dbr_a, w["w_attn_br"], m=rows, n=256, k=1024, tm=1024, tn=256, tk=1024, out_dtypes=(F32,), name="d_o")
    gb["w_attn_br"] = mm(dbr_a, o, m=1024, n=256, k=rows, ta=True, tm=1024, tn=256, tk=1024, out_dtypes=(F32,), name="dw_attn_br")
    dqkv = None
    for g, (_, d) in enumerate(ATTN_PATTERNS):
        dqkv = _attn_bwd(qkv, do, o, lse, g, d, dqkv, name=f"attn_bwd_{g}")

    dmo = mm(dbr_m, w["w_mem_br"], m=rows, n=512, k=1024, tm=1024, tn=512, tk=1024, out_dtypes=(F32,), name="d_mo")
    gb["w_mem_br"] = mm(dbr_m, mo, m=1024, n=512, k=rows, ta=True, tm=1024, tn=512, tk=1024, out_dtypes=(F32,), name="dw_mem_br")
    dmq, dmk, dmv = _mem_attn_bwd(mq, kv, dmo, tq=1024, name="mem_attn_bwd")
    dkv = jnp.concatenate([dmk, dmv], axis=1)
    gb["w_mem_kv"] = mm(mn, dkv, m=1024, n=1024, k=MEM_LEN, ta=True, tm=1024, tn=1024, tk=MEM_LEN, out_dtypes=(F32,), name="dw_mem_kv")
    dmn = mm(dkv, w["w_mem_kv"], m=MEM_LEN, n=1024, k=1024, tb=True, tm=MEM_LEN, tn=1024, tk=1024, out_dtypes=(F32,), name="d_mn")
    _, gs["mem_norm_g"] = _rmsnorm_bwd(mem, gm, dmn, None, tm=MEM_LEN, name="mem_norm_bwd")

    pieces = ((du, OFF_U, "u"), (dqkv[0], OFF_QKV, "q"), (dqkv[1], OFF_QKV + 768, "k"), (dqkv[2], OFF_QKV + 1536, "v"),
              (dmq, OFF_MQ, "mq"), (dzg, OFF_ZG, "zg"))
    dn = _sum_matmul([piece for piece, _, _ in pieces], win_t, [off for _, off, _ in pieces], tm=512, name="d_n1")
    dw_rows = []
    for piece, off, tag in pieces:
        width = piece.shape[1]
        tmw = 1024 if width % 1024 == 0 else (768 if width == 768 else 512)
        dw_rows.append(mm(piece, n1, m=width, n=1024, k=rows, ta=True, tm=tmw, tn=1024, tk=1024, out_dtypes=(F32,),
                          name="dw_in_" + tag))
    gb["w_in"] = jnp.concatenate(dw_rows, axis=0)
    dx, gs["norm1_g"] = _rmsnorm_bwd(x, g1, dn, dh1, tm=512, name="norm1_bwd")
    return loss, dx, gb, gs


def kernel(x, mem, norm1_g, mem_norm_g, w_in, b_gate, ssm_lambda_re, ssm_lambda_im, ssm_log_dt, ssm_b_re, ssm_b_im, ssm_c_re, ssm_c_im, ssm_d, w_glu, b_glu, w_ssm_br, w_attn_br, w_mem_kv, w_mem_br, w_o, norm2_g, w_up, w_down, final_g, loss_target, m_norm1_g, m_mem_norm_g, m_w_in, m_b_gate, m_ssm_lambda_re, m_ssm_lambda_im, m_ssm_log_dt, m_ssm_b_re, m_ssm_b_im, m_ssm_c_re, m_ssm_c_im, m_ssm_d, m_w_glu, m_b_glu, m_w_ssm_br, m_w_attn_br, m_w_mem_kv, m_w_mem_br, m_w_o, m_norm2_g, m_w_up, m_w_down, m_final_g, v_norm1_g, v_mem_norm_g, v_w_in, v_b_gate, v_ssm_lambda_re, v_ssm_lambda_im, v_ssm_log_dt, v_ssm_b_re, v_ssm_b_im, v_ssm_c_re, v_ssm_c_im, v_ssm_d, v_w_glu, v_b_glu, v_w_ssm_br, v_w_attn_br, v_w_mem_kv, v_w_mem_br, v_w_o, v_norm2_g, v_w_up, v_w_down, v_final_g):
    env = dict(locals())
    weights = {n: env[n] for n in WEIGHT_ORDER}
    moms = {n: env["m_" + n] for n in WEIGHT_ORDER}
    vels = {n: env["v_" + n] for n in WEIGHT_ORDER}
    def shard2d(a):
        return a.reshape(a.shape[-2], a.shape[-1])

    chip = 2 * lax.axis_index("x") + lax.axis_index("y")
    wire = [shard2d(weights[n]).astype(BF16) for n, _, _ in BIG]
    wire = [s.T if tr else s for s, (_, tr, _) in zip(wire, BIG)]
    wire = [lax.dynamic_update_slice(lax.empty((N_CHIPS * s.shape[0], s.shape[1]), BF16), s, (chip * s.shape[0], 0))
            for s in wire]
    w_full = dict(zip([n for n, _, _ in BIG], _all_gather_weights(wire)))
    small = {n: weights[n] for n, _ in SMALL}

    loss, dx, gb, gs = _device_step(x[0], mem[0], loss_target[0], w_full, small)

    c_arr = lax.axis_index("c").astype(jnp.int32).reshape(1)
    chip_arr = chip.astype(jnp.int32).reshape(1)
    names = [n for n, _, _ in BIG] + ["small"]
    full = [gb[n] for n, _, _ in BIG] + [_pack_small(gs)]
    full = [g.reshape(N_CHIPS, 2, g.shape[0] // (2 * N_CHIPS), g.shape[1]) for g in full]
    from_sibling = _exchange_halves(full)
    pairs = [_pair_sum(g, t, c_arr, name="grad_pair_sum_" + n) for g, t, n in zip(full, from_sibling, names)]
    landed = _scatter_to_owners([pb for _, pb in pairs])
    totals = [_owner_sum(pf, r, chip_arr, c_arr, replicated=(n == "small"), name="grad_owner_sum_" + n)
              for (pf, _), r, n in zip(pairs, landed, names)]
    *shards, small_grad = _share_reduced(totals)
    grads = {}
    for (n, tr, _), sh in zip(BIG, shards):
        sh = sh.reshape(2 * sh.shape[1], sh.shape[2])
        grads[n] = sh.T if tr else sh
    small_grad = small_grad.reshape(N_CHIPS * SMALL_ROWS, 1024)
    grads_small = _unpack_small(small_grad)

    delta, new_m, new_v = {}, {}, {}
    for n, _, _ in BIG:
        shape = weights[n].shape
        dn_, nm_, nv_ = _adamw(shard2d(weights[n]), grads[n], shard2d(moms[n]), shard2d(vels[n]),
                               tr=min(shape[-2], 256), name="adamw_" + n)
        delta[n], new_m[n], new_v[n] = dn_.reshape(shape), nm_.reshape(shape), nv_.reshape(shape)
        grads[n] = grads[n].reshape(shape)
    ds_, ms_, vs_ = _adamw(_pack_small(small), small_grad,
                           _pack_small({n: moms[n] for n, _ in SMALL}), _pack_small({n: vels[n] for n, _ in SMALL}),
                           tr=N_CHIPS * SMALL_ROWS, name="adamw_small")
    for dst, buf in ((delta, ds_), (new_m, ms_), (new_v, vs_)):
        dst.update(_unpack_small(buf))
    grads.update(grads_small)

    total_loss = lax.psum(loss[0, 0], ("x", "y", "c"))
    return (total_loss, dx[None], *[grads[n] for n in WEIGHT_ORDER], *[delta[n] for n in WEIGHT_ORDER],
            *[new_m[n] for n in WEIGHT_ORDER], *[new_v[n] for n in WEIGHT_ORDER])
```

```python
import functools
import math

import numpy as np
import jax
import jax.numpy as jnp
from jax import lax
from jax.experimental import pallas as pl
from jax.experimental.pallas import tpu as pltpu

F32 = jnp.float32
BF16 = jnp.bfloat16

D_MODEL = 1024
SSM_GROUPS = 32
SSM_GROUP_SIZE = 16
SSM_STATE = 64
SSM_WIDTH = 512
N_STATES = SSM_GROUPS * SSM_STATE
SCAN_CB = 1024
ATTN_PATTERNS = ((128, 1), (512, 4), (2048, 16))
ATTN_HEAD_DIM = 64
ATTN_Q = 128
MEM_LEN = 256
MEM_HEAD_DIM = 128
MEM_HEADS = 4
D_FF = 4096
OFF_U, OFF_QKV, OFF_MQ, OFF_ZG = 0, 512, 2816, 3328
IN_WIDTH = 6400
RMS_EPS = 1e-6
NEG_INF = -1e30
ADAM_LR, ADAM_B1, ADAM_B2, ADAM_EPS, ADAM_WD, ADAM_STEP = 0.001, 0.9, 0.999, 1e-08, 0.01, 10

VMEM_LIMIT_BYTES = 48 * 1024 * 1024
LANES = 128
MESH = pl.DeviceIdType.MESH
N_CHIPS = 4

SCAN_SEGS = 8
SCAN_GROUPS = SCAN_CB // SSM_STATE

BIG = (("w_in", True, (6400, 1024)), ("w_glu", False, (512, 512)), ("w_ssm_br", True, (1024, 512)),
       ("w_attn_br", True, (1024, 256)), ("w_mem_kv", False, (1024, 1024)), ("w_mem_br", True, (1024, 512)),
       ("w_o", False, (1024, 1024)), ("w_up", True, (4096, 1024)), ("w_down", False, (4096, 1024)))
SMALL = (("norm1_g", (1, 1024)), ("mem_norm_g", (1, 1024)), ("b_gate", (1, 3072)),
         ("ssm_lambda_re", (1, 32, 64)), ("ssm_lambda_im", (1, 32, 64)), ("ssm_log_dt", (1, 32)),
         ("ssm_b_re", (1, 32, 64, 16)), ("ssm_b_im", (1, 32, 64, 16)), ("ssm_c_re", (1, 32, 16, 64)),
         ("ssm_c_im", (1, 32, 16, 64)), ("ssm_d", (1, 32, 16)), ("b_glu", (1, 512)),
         ("norm2_g", (1, 1024)), ("final_g", (1024,)))
WEIGHT_ORDER = ("norm1_g", "mem_norm_g", "w_in", "b_gate", "ssm_lambda_re", "ssm_lambda_im", "ssm_log_dt",
                "ssm_b_re", "ssm_b_im", "ssm_c_re", "ssm_c_im", "ssm_d", "w_glu", "b_glu", "w_ssm_br",
                "w_attn_br", "w_mem_kv", "w_mem_br", "w_o", "norm2_g", "w_up", "w_down", "final_g")
SMALL_ELEMS = sum(int(np.prod(s)) for _, s in SMALL)
SMALL_ROWS = 64


def _params(sem):
    return pltpu.CompilerParams(dimension_semantics=sem, vmem_limit_bytes=VMEM_LIMIT_BYTES)


def _sigmoid(v):
    return 1.0 / (1.0 + jnp.exp(-v))


_GELU_C = math.sqrt(2.0 / math.pi)


def _gelu(v):
    return 0.5 * v * (1.0 + jnp.tanh(_GELU_C * (v + 0.044715 * v * v * v)))


def _gelu_grad(v):
    th = jnp.tanh(_GELU_C * (v + 0.044715 * v * v * v))
    return 0.5 * (1.0 + th) + 0.5 * v * (1.0 - th * th) * _GELU_C * (1.0 + 3.0 * 0.044715 * v * v)


def _dot(a, b, ca, cb):
    return lax.dot_general(a, b, (((ca,), (cb,)), ((), ())), preferred_element_type=F32)


def _matmul(a, b, *, m, n, k, ta=False, tb=False, tm, tn, tk, out_dtypes, name,
            a_off=(0, 0), b_off=(0, 0), aux=(), epilogue=None):
    assert m % tm == 0 and n % tn == 0 and k % tk == 0, (name, m, n, k, tm, tn, tk)
    nk = k // tk
    n_aux = len(aux)
    n_out = len(out_dtypes)
    ar, ac = a_off
    br, bc = b_off
    if ta:
        a_spec = pl.BlockSpec((tk, tm), lambda i, j, kk: (kk + ar, i + ac))
    else:
        a_spec = pl.BlockSpec((tm, tk), lambda i, j, kk: (i + ar, kk + ac))
    if tb:
        b_spec = pl.BlockSpec((tn, tk), lambda i, j, kk: (j + br, kk + bc))
    else:
        b_spec = pl.BlockSpec((tk, tn), lambda i, j, kk: (kk + br, j + bc))
    aux_specs = []
    for _, kind in aux:
        if kind == "mn":
            aux_specs.append(pl.BlockSpec((tm, tn), lambda i, j, kk: (i, j)))
        else:
            aux_specs.append(pl.BlockSpec((1, tn), lambda i, j, kk: (0, j)))
    ca = 0 if ta else 1
    cb = 1 if tb else 0

    def finish(acc, aux_refs, out_refs):
        outs = (acc,) if epilogue is None else epilogue(acc, *[r[...] for r in aux_refs])
        for o_ref, o in zip(out_refs, outs):
            o_ref[...] = o.astype(o_ref.dtype)

    def body(a_ref, b_ref, *rest):
        aux_refs = rest[:n_aux]
        out_refs = rest[n_aux:n_aux + n_out]
        prod = _dot(a_ref[...].astype(BF16), b_ref[...].astype(BF16), ca, cb)
        if nk == 1:
            finish(prod, aux_refs, out_refs)
            return
        acc_ref = rest[n_aux + n_out]
        kk = pl.program_id(2)

        @pl.when(kk == 0)
        def _():
            acc_ref[...] = prod

        @pl.when(jnp.logical_and(kk > 0, kk < nk - 1))
        def _():
            acc_ref[...] += prod

        @pl.when(kk == nk - 1)
        def _():
            finish(acc_ref[...] + prod, aux_refs, out_refs)

    res = pl.pallas_call(
        body, name=name, grid=(m // tm, n // tn, nk),
        in_specs=[a_spec, b_spec] + aux_specs,
        out_specs=[pl.BlockSpec((tm, tn), lambda i, j, kk: (i, j)) for _ in range(n_out)],
        out_shape=[jax.ShapeDtypeStruct((m, n), dt) for dt in out_dtypes],
        scratch_shapes=[pltpu.VMEM((tm, tn), F32)] if nk > 1 else [],
        compiler_params=_params(("parallel", "parallel", "arbitrary")),
    )(a, b, *[x for x, _ in aux])
    return res[0] if n_out == 1 else tuple(res)


def _sum_matmul(pieces, b, offs, *, tm, name):
    m = pieces[0].shape[0]
    n = b.shape[1]
    npieces = len(pieces)

    def body(*refs):
        b_ref, o_ref = refs[npieces], refs[npieces + 1]
        acc = None
        for p_ref, off in zip(refs[:npieces], offs):
            part = _dot(p_ref[...].astype(BF16), b_ref[pl.ds(off, p_ref.shape[1]), :], 1, 0)
            acc = part if acc is None else acc + part
        o_ref[...] = acc

    return pl.pallas_call(
        body, name=name, grid=(m // tm,),
        in_specs=[pl.BlockSpec((tm, p.shape[1]), lambda i: (i, 0)) for p in pieces]
        + [pl.BlockSpec(b.shape, lambda i: (0, 0), pipeline_mode=pl.Buffered(1))],
        out_specs=pl.BlockSpec((tm, n), lambda i: (i, 0)),
        out_shape=jax.ShapeDtypeStruct((m, n), F32),
        compiler_params=_params(("parallel",)),
    )(*pieces, b)


def _rmsnorm_fwd(x, g, *, tm, name):
    rows, d = x.shape

    def body(x_ref, g_ref, o_ref):
        xv = x_ref[...]
        r = lax.rsqrt(jnp.mean(xv * xv, axis=-1, keepdims=True) + RMS_EPS)
        o_ref[...] = (xv * r * g_ref[...]).astype(o_ref.dtype)

    return pl.pallas_call(
        body, name=name, grid=(rows // tm,),
        in_specs=[pl.BlockSpec((tm, d), lambda i: (i, 0)), pl.BlockSpec((1, d), lambda i: (0, 0))],
        out_specs=pl.BlockSpec((tm, d), lambda i: (i, 0)),
        out_shape=jax.ShapeDtypeStruct((rows, d), BF16),
        compiler_params=_params(("parallel",)),
    )(x, g)


def _rmsnorm_bwd(x, g, dy, res, *, tm, name):
    rows, d = x.shape
    has_res = res is not None

    def body(x_ref, g_ref, dy_ref, *rest):
        if has_res:
            res_ref, dx_ref, dg_ref = rest
        else:
            dx_ref, dg_ref = rest
        i = pl.program_id(0)
        xv = x_ref[...]
        r = lax.rsqrt(jnp.mean(xv * xv, axis=-1, keepdims=True) + RMS_EPS)
        xhat = xv * r
        dyv = dy_ref[...]
        dyg = dyv * g_ref[...]
        dx = r * (dyg - xhat * jnp.mean(dyg * xhat, axis=-1, keepdims=True))
        if has_res:
            dx = dx + res_ref[...]
        dx_ref[...] = dx

        @pl.when(i == 0)
        def _():
            dg_ref[...] = jnp.zeros_like(dg_ref)

        dg_ref[...] += jnp.sum(dyv * xhat, axis=0, keepdims=True)

    row_spec = pl.BlockSpec((tm, d), lambda i: (i, 0))
    vec_spec = pl.BlockSpec((1, d), lambda i: (0, 0))
    ins = [x, g, dy] + ([res] if has_res else [])
    return pl.pallas_call(
        body, name=name, grid=(rows // tm,),
        in_specs=[row_spec, vec_spec, row_spec] + ([row_spec] if has_res else []),
        out_specs=[row_spec, vec_spec],
        out_shape=[jax.ShapeDtypeStruct((rows, d), F32), jax.ShapeDtypeStruct((1, d), F32)],
        compiler_params=_params(("arbitrary",)),
    )(*ins)


def _loss_head(h, tgt, g, *, tm, name):
    rows, d = h.shape
    nsteps = rows // tm

    def body(h_ref, t_ref, g_ref, dh_ref, loss_ref, dg_ref, sq_ref):
        i = pl.program_id(0)
        xv = h_ref[...]
        gv = g_ref[...]
        r = lax.rsqrt(jnp.mean(xv * xv, axis=-1, keepdims=True) + RMS_EPS)
        xhat = xv * r
        err = xhat * gv - t_ref[...]
        dyv = err * (1.0 / d)
        dyg = dyv * gv
        dh_ref[...] = r * (dyg - xhat * jnp.mean(dyg * xhat, axis=-1, keepdims=True))

        @pl.when(i == 0)
        def _():
            dg_ref[...] = jnp.zeros_like(dg_ref)
            sq_ref[...] = jnp.zeros_like(sq_ref)

        dg_ref[...] += jnp.sum(dyv * xhat, axis=0, keepdims=True)
        sq_ref[...] += jnp.sum(err * err, axis=0, keepdims=True)

        @pl.when(i == nsteps - 1)
        def _():
            tot = jnp.sum(sq_ref[...], axis=-1, keepdims=True) * (0.5 / d)
            loss_ref[...] = jnp.broadcast_to(tot, loss_ref.shape)

    row_spec = pl.BlockSpec((tm, d), lambda i: (i, 0))
    vec_spec = pl.BlockSpec((1, d), lambda i: (0, 0))
    return pl.pallas_call(
        body, name=name, grid=(nsteps,),
        in_specs=[row_spec, row_spec, vec_spec],
        out_specs=[row_spec, pl.BlockSpec((1, LANES), lambda i: (0, 0)), vec_spec],
        out_shape=[jax.ShapeDtypeStruct((rows, d), F32), jax.ShapeDtypeStruct((1, LANES), F32),
                   jax.ShapeDtypeStruct((1, d), F32)],
        scratch_shapes=[pltpu.VMEM((1, d), F32)],
        compiler_params=_params(("arbitrary",)),
    )(h, tgt, g)


def _to_scan_layout(v):
    lead = v.shape[:-2]
    v = v.reshape(lead + (2, N_STATES // SCAN_CB, SCAN_CB))
    v = jnp.swapaxes(v, -3, -2)
    return v.reshape(lead + (2 * N_STATES,))


def _ssm_matrices(lam_re, lam_im, log_dt, b_re, b_im, c_re, c_im):
    dt = jnp.exp(log_dt)[:, None]
    mag = jnp.exp(lam_re * dt)
    a_re, a_im = mag * jnp.cos(lam_im * dt), mag * jnp.sin(lam_im * dt)
    nr, ni = a_re - 1.0, a_im
    den = lam_re * lam_re + lam_im * lam_im
    coef_re = (nr * lam_re + ni * lam_im) / den
    coef_im = (ni * lam_re - nr * lam_im) / den
    bb_re = coef_re[..., None] * b_re - coef_im[..., None] * b_im
    bb_im = coef_re[..., None] * b_im + coef_im[..., None] * b_re
    a_lay = _to_scan_layout(jnp.stack([a_re.reshape(-1), a_im.reshape(-1)], axis=0))[None, :]
    nblk = SSM_GROUPS // SCAN_GROUPS
    eye = jnp.eye(SCAN_GROUPS, dtype=F32)

    def b_block(bb):
        bb = bb.reshape(nblk, SCAN_GROUPS, SSM_STATE, SSM_GROUP_SIZE)
        return jnp.einsum("gk,jkph->jghkp", eye, bb).reshape(nblk, SCAN_GROUPS * SSM_GROUP_SIZE, SCAN_CB)

    b_blk = jnp.concatenate([b_block(bb_re), b_block(bb_im)], axis=2)

    def c_block(cc):
        cc = cc.reshape(nblk, SCAN_GROUPS, SSM_GROUP_SIZE, SSM_STATE)
        return jnp.einsum("gk,jghp->jkpgh", eye, cc).reshape(nblk, SCAN_CB, SCAN_GROUPS * SSM_GROUP_SIZE)

    c_blk = jnp.concatenate([c_block(c_re), -c_block(c_im)], axis=1)
    return a_lay, b_blk, c_blk


def _interleave(v):
    rows, c = v.shape
    return v.reshape(SCAN_SEGS, rows // SCAN_SEGS, c).transpose(1, 0, 2).reshape(rows, c)


def _deinterleave(v):
    rows, c = v.shape
    return v.reshape(rows // SCAN_SEGS, SCAN_SEGS, c).transpose(1, 0, 2).reshape(rows, c)


def _scan_groups(a_ref, bu_ref, o_ref, state, *, reverse, tt):
    cb = SCAN_CB
    ar = jnp.broadcast_to(a_ref[:, :cb], (SCAN_SEGS, cb))
    ai = jnp.broadcast_to(a_ref[:, cb:], (SCAN_SEGS, cb))
    ngroups = tt // SCAN_SEGS

    def step(i, st):
        sr, si = st
        r0 = pl.multiple_of(((ngroups - 1 - i) if reverse else i) * SCAN_SEGS, SCAN_SEGS)
        blk = bu_ref[pl.ds(r0, SCAN_SEGS), :]
        nr = ar * sr - ai * si + blk[:, :cb]
        ni = ar * si + ai * sr + blk[:, cb:]
        if o_ref is not None:
            o_ref[pl.ds(r0, SCAN_SEGS), :] = jnp.concatenate([nr, ni], axis=1)
        return nr, ni

    return lax.fori_loop(0, ngroups, step, state, unroll=4)


def _segment_entries(a_ref, e_ref, init_ref, *, reverse, seg_len):
    cb = SCAN_CB
    n_sq = seg_len.bit_length() - 1
    assert 1 << n_sq == seg_len, seg_len
    pr, pi = a_ref[:, :cb], a_ref[:, cb:]
    for _ in range(n_sq):
        pr, pi = pr * pr - pi * pi, 2.0 * pr * pi
    cr = jnp.zeros((1, cb), F32)
    ci = jnp.zeros((1, cb), F32)
    order = range(SCAN_SEGS - 1, -1, -1) if reverse else range(SCAN_SEGS)
    for k, seg in enumerate(order):
        if k > 0:
            prev = seg + 1 if reverse else seg - 1
            er, ei = e_ref[prev:prev + 1, :cb], e_ref[prev:prev + 1, cb:]
            cr, ci = pr * cr - pi * ci + er, pr * ci + pi * cr + ei
        init_ref[seg:seg + 1, :] = jnp.concatenate([cr, ci], axis=1)


def _ssm_specs(nt, tt, nch, reverse):
    cb = SCAN_CB
    tmap = (lambda j, kk: (nt - 1 - kk, j)) if reverse else (lambda j, kk: (kk, j))
    return dict(a=pl.BlockSpec((1, 2 * cb), lambda j, kk: (0, j)),
                seg=pl.BlockSpec((SCAN_SEGS, 2 * cb), lambda j, kk: (0, j)),
                chan=pl.BlockSpec((tt, nch), tmap),
                state=pl.BlockSpec((tt, 2 * cb), tmap),
                b=pl.BlockSpec((None, nch, 2 * cb), lambda j, kk: (j, 0, 0)),
                c=pl.BlockSpec((None, 2 * cb, nch), lambda j, kk: (j, 0, 0)))


def _ssm_ends(a_lay, x, blocks, *, transpose, reverse, tt, name):
    rows = x.shape[0]
    nblk = blocks.shape[0]
    nch = x.shape[1] // nblk
    cb = SCAN_CB
    nt = rows // tt
    sp = _ssm_specs(nt, tt, nch, reverse)

    def body(a_ref, x_ref, w_ref, e_ref, bu_ref):
        kk = pl.program_id(1)

        @pl.when(kk == 0)
        def _():
            e_ref[...] = jnp.zeros_like(e_ref)

        bu_ref[...] = _dot(x_ref[...].astype(BF16), w_ref[...].astype(BF16), 1, 1 if transpose else 0)
        sr, si = _scan_groups(a_ref, bu_ref, None, (e_ref[:, :cb], e_ref[:, cb:]), reverse=reverse, tt=tt)
        e_ref[...] = jnp.concatenate([sr, si], axis=1)

    return pl.pallas_call(
        body, name=name, grid=(nblk, nt),
        in_specs=[sp["a"], sp["chan"], sp["c"] if transpose else sp["b"]],
        out_specs=sp["seg"],
        out_shape=jax.ShapeDtypeStruct((SCAN_SEGS, nblk * 2 * cb), F32),
        scratch_shapes=[pltpu.VMEM((tt, 2 * cb), F32)],
        compiler_params=_params(("parallel", "arbitrary")),
    )(a_lay, x, blocks)


def _ssm_fwd(a_lay, u, b_blk, c_blk, ends, *, tt, name):
    rows = u.shape[0]
    nblk = b_blk.shape[0]
    nch = u.shape[1] // nblk
    cb = SCAN_CB
    nt = rows // tt
    sp = _ssm_specs(nt, tt, nch, False)

    def body(a_ref, e_ref, u_ref, b_ref, c_ref, s_ref, y_ref, init_ref, carry_ref):
        kk = pl.program_id(1)

        @pl.when(kk == 0)
        def _():
            _segment_entries(a_ref, e_ref, init_ref, reverse=False, seg_len=rows // SCAN_SEGS)
            carry_ref[...] = init_ref[...]

        s_ref[...] = _dot(u_ref[...].astype(BF16), b_ref[...].astype(BF16), 1, 0)
        sr, si = _scan_groups(a_ref, s_ref, s_ref, (carry_ref[:, :cb], carry_ref[:, cb:]), reverse=False, tt=tt)
        carry_ref[...] = jnp.concatenate([sr, si], axis=1)
        y_ref[...] = _dot(s_ref[...].astype(BF16), c_ref[...].astype(BF16), 1, 0)

    return pl.pallas_call(
        body, name=name, grid=(nblk, nt),
        in_specs=[sp["a"], sp["seg"], sp["chan"], sp["b"], sp["c"]],
        out_specs=[sp["state"], sp["chan"], sp["seg"]],
        out_shape=[jax.ShapeDtypeStruct((rows, nblk * 2 * cb), F32), jax.ShapeDtypeStruct((rows, nblk * nch), F32),
                   jax.ShapeDtypeStruct((SCAN_SEGS, nblk * 2 * cb), F32)],
        scratch_shapes=[pltpu.VMEM((SCAN_SEGS, 2 * cb), F32)],
        compiler_params=_params(("parallel", "arbitrary")),
    )(a_lay, ends, u, b_blk, c_blk)


def _ssm_bwd(a_conj, dy, u, s, s_entry, b_blk, c_blk, dd, ends, *, tt, name):
    rows = u.shape[0]
    nblk = b_blk.shape[0]
    nch = u.shape[1] // nblk
    cb = SCAN_CB
    nt = rows // tt
    sp = _ssm_specs(nt, tt, nch, True)
    groups_per_tile = tt // SCAN_SEGS
    before = pl.BlockSpec((SCAN_SEGS, 2 * cb), lambda j, kk: (jnp.maximum((nt - 1 - kk) * groups_per_tile - 1, 0), j))

    def body(a_ref, e_ref, dy_ref, u_ref, s_ref, before_ref, entry_ref, b_ref, c_ref, dd_ref,
             du_ref, db_ref, dc_ref, da_ref, lam_ref, carry_ref):
        kk = pl.program_id(1)

        @pl.when(kk == 0)
        def _():
            _segment_entries(a_ref, e_ref, carry_ref, reverse=True, seg_len=rows // SCAN_SEGS)
            db_ref[...] = jnp.zeros_like(db_ref)
            dc_ref[...] = jnp.zeros_like(dc_ref)
            da_ref[...] = jnp.zeros_like(da_ref)

        dyv = dy_ref[...]
        dyb = dyv.astype(BF16)
        lam_ref[...] = _dot(dyb, c_ref[...].astype(BF16), 1, 1)
        lr, li = _scan_groups(a_ref, lam_ref, lam_ref, (carry_ref[:, :cb], carry_ref[:, cb:]), reverse=True, tt=tt)
        carry_ref[...] = jnp.concatenate([lr, li], axis=1)

        lamb = lam_ref[...].astype(BF16)
        du_ref[...] = _dot(lamb, b_ref[...].astype(BF16), 1, 1) + dd_ref[...] * dyv
        db_ref[...] += _dot(u_ref[...].astype(BF16), lamb, 0, 0)
        dc_ref[...] += _dot(s_ref[...].astype(BF16), dyb, 0, 0)

        first = jnp.where(kk == nt - 1, entry_ref[...], before_ref[...])
        rest = tt - SCAN_SEGS
        lam_hi = lam_ref[pl.ds(SCAN_SEGS, rest), :]
        s_lo = s_ref[pl.ds(0, rest), :]
        lam_lo = lam_ref[pl.ds(0, SCAN_SEGS), :]

        def pair(lv, pv):
            lre, lim, pre, pim = lv[:, :cb], lv[:, cb:], pv[:, :cb], pv[:, cb:]
            return (jnp.sum(lre * pre + lim * pim, axis=0, keepdims=True),
                    jnp.sum(lim * pre - lre * pim, axis=0, keepdims=True))

        r1, i1 = pair(lam_hi, s_lo)
        r0, i0 = pair(lam_lo, first)
        da_ref[...] += jnp.concatenate([r1 + r0, i1 + i0], axis=1)

    return pl.pallas_call(
        body, name=name, grid=(nblk, nt),
        in_specs=[sp["a"], sp["seg"], sp["chan"], sp["chan"], sp["state"], before, sp["seg"], sp["b"], sp["c"],
                  pl.BlockSpec((1, nch), lambda j, kk: (0, j))],
        out_specs=[sp["chan"], sp["b"], sp["c"], pl.BlockSpec((1, 2 * cb), lambda j, kk: (0, j))],
        out_shape=[jax.ShapeDtypeStruct((rows, nblk * nch), F32), jax.ShapeDtypeStruct(b_blk.shape, F32),
                   jax.ShapeDtypeStruct(c_blk.shape, F32), jax.ShapeDtypeStruct((1, nblk * 2 * cb), F32)],
        scratch_shapes=[pltpu.VMEM((tt, 2 * cb), F32), pltpu.VMEM((SCAN_SEGS, 2 * cb), F32)],
        compiler_params=_params(("parallel", "arbitrary")),
    )(a_conj, ends, dy, u, s, s, s_entry, b_blk, c_blk, dd)


def _glu_fwd(ys, u, dd, w_glu, b_glu, *, tm, name):
    rows, w = ys.shape

    def body(ys_ref, u_ref, dd_ref, w_ref, b_ref, y0_ref, t_ref, y2_ref):
        y0 = ys_ref[...] + dd_ref[...] * u_ref[...]
        y1 = _gelu(y0)
        t = _dot(y1.astype(BF16), w_ref[...], 1, 0) + b_ref[...]
        y0_ref[...] = y0
        t_ref[...] = t
        y2_ref[...] = (y1 * _sigmoid(t)).astype(BF16)

    row = pl.BlockSpec((tm, w), lambda i: (i, 0))
    vec = pl.BlockSpec((1, w), lambda i: (0, 0))
    return pl.pallas_call(
        body, name=name, grid=(rows // tm,),
        in_specs=[row, row, vec, pl.BlockSpec((w, w), lambda i: (0, 0)), vec],
        out_specs=[row, row, row],
        out_shape=[jax.ShapeDtypeStruct((rows, w), F32), jax.ShapeDtypeStruct((rows, w), F32),
                   jax.ShapeDtypeStruct((rows, w), BF16)],
        compiler_params=_params(("parallel",)),
    )(ys, u, dd, w_glu, b_glu)


def _glu_bwd(dy2, y0, t, u, w_glu, *, tm, name):
    rows, w = y0.shape

    def body(dy2_ref, y0_ref, t_ref, u_ref, w_ref, dy0_ref, dt_ref, y1_ref, db_ref, dd_ref):
        i = pl.program_id(0)
        y0 = y0_ref[...]
        y1 = _gelu(y0)
        sg = _sigmoid(t_ref[...])
        dy2v = dy2_ref[...]
        dt = dy2v * y1 * sg * (1.0 - sg)
        dy1 = dy2v * sg + _dot(dt.astype(BF16), w_ref[...], 1, 1)
        dy0 = dy1 * _gelu_grad(y0)
        dy0_ref[...] = dy0
        dt_ref[...] = dt.astype(BF16)
        y1_ref[...] = y1.astype(BF16)

        @pl.when(i == 0)
        def _():
            db_ref[...] = jnp.zeros_like(db_ref)
            dd_ref[...] = jnp.zeros_like(dd_ref)

        db_ref[...] += jnp.sum(dt, axis=0, keepdims=True)
        dd_ref[...] += jnp.sum(dy0 * u_ref[...], axis=0, keepdims=True)

    row = pl.BlockSpec((tm, w), lambda i: (i, 0))
    vec = pl.BlockSpec((1, w), lambda i: (0, 0))
    return pl.pallas_call(
        body, name=name, grid=(rows // tm,),
        in_specs=[row, row, row, row, pl.BlockSpec((w, w), lambda i: (0, 0))],
        out_specs=[row, row, row, vec, vec],
        out_shape=[jax.ShapeDtypeStruct((rows, w), F32), jax.ShapeDtypeStruct((rows, w), BF16),
                   jax.ShapeDtypeStruct((rows, w), BF16), jax.ShapeDtypeStruct((1, w), F32),
                   jax.ShapeDtypeStruct((1, w), F32)],
        compiler_params=_params(("arbitrary",)),
    )(dy2, y0, t, u, w_glu)


ATTN_TILE = 2048


def _attn_geometry(rows, d):
    sb = ATTN_Q * d
    tr = max(sb, min(ATTN_TILE, rows))
    assert rows % tr == 0 and tr % sb == 0, (rows, d)
    return sb, tr, rows // tr, tr // sb


def _attn_masks():
    qi = lax.broadcasted_iota(jnp.int32, (2 * ATTN_Q, 2 * ATTN_Q), 0) % ATTN_Q
    kj = lax.broadcasted_iota(jnp.int32, (2 * ATTN_Q, 2 * ATTN_Q), 1)
    own_ok = jnp.logical_and(kj >= ATTN_Q, kj - ATTN_Q <= qi)
    prev_ok = jnp.logical_and(kj < ATTN_Q, kj >= qi)
    bias_first = jnp.where(own_ok, 0.0, NEG_INF)
    bias_other = jnp.where(jnp.logical_or(own_ok, prev_ok), 0.0, NEG_INF)
    head0 = lax.broadcasted_iota(jnp.int32, (ATTN_Q, LANES), 1) < ATTN_HEAD_DIM
    return bias_first, bias_other, head0


def _attn_rows(base, n, d):
    return pl.ds(pl.multiple_of(base, ATTN_Q), n) if d == 1 else pl.ds(base, n, stride=d)


def _stack_heads(v, head0):
    return jnp.concatenate([jnp.where(head0, v, 0.0), jnp.where(head0, 0.0, v)], axis=0)


def _unstack_heads(v, head0):
    return jnp.where(head0, v[:ATTN_Q], v[ATTN_Q:])


def _fill_keys(buf, prev_ref, cur_ref, sb):
    buf[pl.ds(0, sb), :] = prev_ref[...]
    buf[pl.ds(sb, cur_ref.shape[0]), :] = cur_ref[...]


def _attn_fwd(qkv, g, d, *, name):
    rows = qkv.shape[0]
    sb, tr, ntiles, nsub = _attn_geometry(rows, d)
    qc, kc, vc = 2 * g, 6 + 2 * g, 12 + 2 * g
    scale = ATTN_HEAD_DIM ** -0.5

    def body(q_ref, kc_ref, kp_ref, vc_ref, vp_ref, o_ref, lse_ref, kbuf, vbuf):
        n = pl.program_id(0)
        _fill_keys(kbuf, kp_ref, kc_ref, sb)
        _fill_keys(vbuf, vp_ref, vc_ref, sb)
        bias_first, bias_other, head0 = _attn_masks()

        def per_block(idx, carry):
            j, r = idx // d, idx % d
            base = j * sb + r
            bias = jnp.where(jnp.logical_and(n == 0, j == 0), bias_first, bias_other)
            qrows = _attn_rows(base, ATTN_Q, d)
            krows = _attn_rows(base, 2 * ATTN_Q, d)
            qs = _stack_heads(q_ref[qrows, :], head0).astype(BF16)
            s = _dot(qs, kbuf[krows, :].astype(BF16), 1, 1) * scale + bias
            mx = jnp.max(s, axis=-1, keepdims=True)
            p = jnp.exp(s - mx)
            den = jnp.sum(p, axis=-1, keepdims=True)
            pv = _dot(p.astype(BF16), vbuf[krows, :].astype(BF16), 1, 0) / den
            o_ref[qrows, :] = _unstack_heads(pv, head0)
            lse_ref[qrows, :] = _unstack_heads(jnp.broadcast_to(mx + jnp.log(den), (2 * ATTN_Q, LANES)), head0)
            return carry

        lax.fori_loop(0, nsub * d, per_block, 0)

    def cur(col):
        return pl.BlockSpec((tr, LANES), lambda n, hp: (n, col + hp))

    def prev(col):
        return pl.BlockSpec((sb, LANES), lambda n, hp: (jnp.maximum(n * nsub - 1, 0), col + hp))

    out_spec = pl.BlockSpec((tr, LANES), lambda n, hp: (n, hp))
    return pl.pallas_call(
        body, name=name, grid=(ntiles, 2),
        in_specs=[cur(qc), cur(kc), prev(kc), cur(vc), prev(vc)],
        out_specs=[out_spec, out_spec],
        out_shape=[jax.ShapeDtypeStruct((rows, 2 * LANES), F32), jax.ShapeDtypeStruct((rows, 2 * LANES), F32)],
        scratch_shapes=[pltpu.VMEM((sb + tr, LANES), F32), pltpu.VMEM((sb + tr, LANES), F32)],
        compiler_params=_params(("parallel", "parallel")),
    )(qkv, qkv, qkv, qkv, qkv)


def _attn_merge(outs, lses, *, tm, name):
    rows, w = outs[0].shape

    def body(o0, o1, o2, l0, l1, l2, o_ref, lse_ref):
        a0, a1, a2 = l0[...], l1[...], l2[...]
        mx = jnp.maximum(jnp.maximum(a0, a1), a2)
        e0, e1, e2 = jnp.exp(a0 - mx), jnp.exp(a1 - mx), jnp.exp(a2 - mx)
        den = e0 + e1 + e2
        o_ref[...] = (e0 / den) * o0[...] + (e1 / den) * o1[...] + (e2 / den) * o2[...]
        lse_ref[...] = mx + jnp.log(den)

    row = pl.BlockSpec((tm, w), lambda i: (i, 0))
    return pl.pallas_call(
        body, name=name, grid=(rows // tm,), in_specs=[row] * 6, out_specs=[row, row],
        out_shape=[jax.ShapeDtypeStruct((rows, w), F32), jax.ShapeDtypeStruct((rows, w), F32)],
        compiler_params=_params(("parallel",)),
    )(*outs, *lses)


def _attn_bwd(qkv, do, o, lse, g, d, prev, *, name):
    rows = qkv.shape[0]
    sb, tr, ntiles, nsub = _attn_geometry(rows, d)
    qc, kc, vc = 2 * g, 6 + 2 * g, 12 + 2 * g
    scale = ATTN_HEAD_DIM ** -0.5

    def body(q_ref, kc_ref, kp_ref, vc_ref, vp_ref, do_ref, o_ref, lse_ref, dq_ref, dk_ref, dv_ref,
             kbuf, vbuf, dk_acc, dv_acc):
        n = pl.program_id(1)

        @pl.when(n == 0)
        def _():
            dk_acc[pl.ds(0, tr), :] = jnp.zeros((tr, LANES), F32)
            dv_acc[pl.ds(0, tr), :] = jnp.zeros((tr, LANES), F32)

        @pl.when(n < ntiles)
        def _():
            dk_acc[pl.ds(tr, tr), :] = jnp.zeros((tr, LANES), F32)
            dv_acc[pl.ds(tr, tr), :] = jnp.zeros((tr, LANES), F32)
            _fill_keys(kbuf, kp_ref, kc_ref, sb)
            _fill_keys(vbuf, vp_ref, vc_ref, sb)
            bias_first, bias_other, head0 = _attn_masks()
            lane = lax.broadcasted_iota(jnp.int32, (ATTN_Q, LANES), 1)

            def per_block(idx, carry):
                j, r = idx // d, idx % d
                base = j * sb + r
                bias = jnp.where(jnp.logical_and(n == 0, j == 0), bias_first, bias_other)
                qrows = _attn_rows(base, ATTN_Q, d)
                krows = _attn_rows(base, 2 * ATTN_Q, d)
                arows = _attn_rows(base + (tr - sb), 2 * ATTN_Q, d)
                qs = _stack_heads(q_ref[qrows, :], head0).astype(BF16)
                dos = _stack_heads(do_ref[qrows, :], head0)
                dosb = dos.astype(BF16)
                ov = o_ref[qrows, :]
                delta = jnp.sum(dos * jnp.concatenate([ov, ov], axis=0), axis=-1, keepdims=True)
                lsev = lse_ref[qrows, :]
                lse_s = jnp.concatenate(
                    [jnp.sum(jnp.where(lane == h * ATTN_HEAD_DIM, lsev, 0.0), axis=-1, keepdims=True) for h in range(2)], axis=0)
                kb = kbuf[krows, :].astype(BF16)
                vb = vbuf[krows, :].astype(BF16)
                p = jnp.exp(_dot(qs, kb, 1, 1) * scale + bias - lse_s)
                ds = (p * (_dot(dosb, vb, 1, 1) - delta) * scale).astype(BF16)
                dq_ref[qrows, :] = _unstack_heads(_dot(ds, kb, 1, 0), head0)
                dk_acc[arows, :] += _dot(ds, qs, 0, 0)
                dv_acc[arows, :] += _dot(p.astype(BF16), dosb, 0, 0)
                return carry

            lax.fori_loop(0, nsub * d, per_block, 0)

        dk_ref[...] = dk_acc[pl.ds(0, tr), :]
        dv_ref[...] = dv_acc[pl.ds(0, tr), :]
        dk_acc[pl.ds(0, tr), :] = dk_acc[pl.ds(tr, tr), :]
        dv_acc[pl.ds(0, tr), :] = dv_acc[pl.ds(tr, tr), :]

    def cur(n):
        return jnp.minimum(n, ntiles - 1)

    def spec(col, prev):
        if prev:
            return pl.BlockSpec((sb, LANES), lambda hp, n: (jnp.maximum(cur(n) * nsub - 1, 0), col + hp))
        return pl.BlockSpec((tr, LANES), lambda hp, n: (cur(n), col + hp))

    row_spec = pl.BlockSpec((tr, LANES), lambda hp, n: (cur(n), hp))
    dq_out = pl.BlockSpec((tr, LANES), lambda hp, n: (cur(n), 2 * g + hp))
    kv_out = pl.BlockSpec((tr, LANES), lambda hp, n: (jnp.maximum(n - 1, 0), 2 * g + hp))
    shape = jax.ShapeDtypeStruct((rows, len(ATTN_PATTERNS) * 2 * LANES), F32)
    ins = [qkv, qkv, qkv, qkv, qkv, do, o, lse]
    in_specs = [spec(qc, False), spec(kc, False), spec(kc, True), spec(vc, False), spec(vc, True),
                row_spec, row_spec, row_spec]
    aliases = {}
    if prev is not None:
        aliases = {len(ins) + t: t for t in range(3)}
        ins = ins + list(prev)
        in_specs = in_specs + [ANY] * 3
    n_in = len(ins)

    def entry(*refs):
        body(*refs[:8], *refs[n_in:])

    return pl.pallas_call(
        entry, name=name, grid=(2, ntiles + 1),
        in_specs=in_specs,
        out_specs=[dq_out, kv_out, kv_out],
        out_shape=[shape, shape, shape],
        input_output_aliases=aliases,
        scratch_shapes=[pltpu.VMEM((sb + tr, LANES), F32), pltpu.VMEM((sb + tr, LANES), F32),
                        pltpu.VMEM((2 * tr, LANES), F32), pltpu.VMEM((2 * tr, LANES), F32)],
        compiler_params=_params(("parallel", "arbitrary")),
    )(*ins)


def _mem_probs(q, k):
    s = _dot(q.astype(BF16), k.astype(BF16), 1, 1) * (MEM_HEAD_DIM ** -0.5)
    e = jnp.exp(s - jnp.max(s, axis=-1, keepdims=True))
    return e / jnp.sum(e, axis=-1, keepdims=True)


def _mem_attn_fwd(mq, kv, *, tq, name):
    rows = mq.shape[0]

    def body(q_ref, k_ref, v_ref, o_ref):
        p = _mem_probs(q_ref[...], k_ref[...])
        o_ref[...] = _dot(p.astype(BF16), v_ref[...].astype(BF16), 1, 0)

    return pl.pallas_call(
        body, name=name, grid=(rows // tq, MEM_HEADS),
        in_specs=[pl.BlockSpec((tq, LANES), lambda i, h: (i, h)),
                  pl.BlockSpec((MEM_LEN, LANES), lambda i, h: (0, h)),
                  pl.BlockSpec((MEM_LEN, LANES), lambda i, h: (0, MEM_HEADS + h))],
        out_specs=pl.BlockSpec((tq, LANES), lambda i, h: (i, h)),
        out_shape=jax.ShapeDtypeStruct((rows, MEM_HEADS * LANES), F32),
        compiler_params=_params(("parallel", "parallel")),
    )(mq, kv, kv)


def _mem_attn_bwd(mq, kv, dmo, *, tq, name):
    rows = mq.shape[0]
    scale = MEM_HEAD_DIM ** -0.5

    def body(q_ref, k_ref, v_ref, do_ref, dq_ref, dk_ref, dv_ref):
        i = pl.program_id(1)
        qb = q_ref[...].astype(BF16)
        kb = k_ref[...].astype(BF16)
        vb = v_ref[...].astype(BF16)
        dob = do_ref[...].astype(BF16)
        p = _mem_probs(q_ref[...], k_ref[...])
        dp = _dot(dob, vb, 1, 1)
        ds = (p * (dp - jnp.sum(p * dp, axis=-1, keepdims=True)) * scale).astype(BF16)
        dq_ref[...] = _dot(ds, kb, 1, 0).astype(dq_ref.dtype)

        @pl.when(i == 0)
        def _():
            dk_ref[...] = jnp.zeros_like(dk_ref)
            dv_ref[...] = jnp.zeros_like(dv_ref)

        dk_ref[...] += _dot(ds, qb, 0, 0)
        dv_ref[...] += _dot(p.astype(BF16), dob, 0, 0)

    kv_out = pl.BlockSpec((MEM_LEN, LANES), lambda h, i: (0, h))
    kv_shape = jax.ShapeDtypeStruct((MEM_LEN, MEM_HEADS * LANES), F32)
    return pl.pallas_call(
        body, name=name, grid=(MEM_HEADS, rows // tq),
        in_specs=[pl.BlockSpec((tq, LANES), lambda h, i: (i, h)),
                  pl.BlockSpec((MEM_LEN, LANES), lambda h, i: (0, h)),
                  pl.BlockSpec((MEM_LEN, LANES), lambda h, i: (0, MEM_HEADS + h)),
                  pl.BlockSpec((tq, LANES), lambda h, i: (i, h))],
        out_specs=[pl.BlockSpec((tq, LANES), lambda h, i: (i, h)), kv_out, kv_out],
        out_shape=[jax.ShapeDtypeStruct((rows, MEM_HEADS * LANES), BF16), kv_shape, kv_shape],
        compiler_params=_params(("parallel", "arbitrary")),
    )(mq, kv, kv, dmo)


def _gate_merge_fwd(zg, b_gate, br_s, br_a, br_m, *, tm, name):
    rows, d = br_s.shape

    def body(zg_ref, b_ref, s_ref, a_ref, m_ref, o_ref):
        gt = _sigmoid(zg_ref[...] + b_ref[...])
        o_ref[...] = (gt[:, :d] * s_ref[...] + gt[:, d:2 * d] * a_ref[...] + gt[:, 2 * d:] * m_ref[...]).astype(BF16)

    row = pl.BlockSpec((tm, d), lambda i: (i, 0))
    return pl.pallas_call(
        body, name=name, grid=(rows // tm,),
        in_specs=[pl.BlockSpec((tm, 3 * d), lambda i: (i, 0)), pl.BlockSpec((1, 3 * d), lambda i: (0, 0)), row, row, row],
        out_specs=row, out_shape=jax.ShapeDtypeStruct((rows, d), BF16),
        compiler_params=_params(("parallel",)),
    )(zg, b_gate, br_s, br_a, br_m)


def _gate_merge_bwd(dmerged, zg, b_gate, br_s, br_a, br_m, *, tm, name):
    rows, d = br_s.shape

    def body(dm_ref, zg_ref, b_ref, s_ref, a_ref, m_ref, ds_ref, da_ref, dmm_ref, dzg_ref, db_ref):
        i = pl.program_id(0)
        gt = _sigmoid(zg_ref[...] + b_ref[...])
        dm = dm_ref[...]
        parts = []
        for j, (br_ref, out_ref) in enumerate(((s_ref, ds_ref), (a_ref, da_ref), (m_ref, dmm_ref))):
            gj = gt[:, j * d:(j + 1) * d]
            out_ref[...] = (dm * gj).astype(BF16)
            parts.append(dm * br_ref[...] * gj * (1.0 - gj))
        dzg = jnp.concatenate(parts, axis=1)
        dzg_ref[...] = dzg.astype(BF16)

        @pl.when(i == 0)
        def _():
            db_ref[...] = jnp.zeros_like(db_ref)

        db_ref[...] += jnp.sum(dzg, axis=0, keepdims=True)

    row = pl.BlockSpec((tm, d), lambda i: (i, 0))
    wide = pl.BlockSpec((tm, 3 * d), lambda i: (i, 0))
    vec = pl.BlockSpec((1, 3 * d), lambda i: (0, 0))
    bshape = jax.ShapeDtypeStruct((rows, d), BF16)
    return pl.pallas_call(
        body, name=name, grid=(rows // tm,),
        in_specs=[row, wide, vec, row, row, row],
        out_specs=[row, row, row, wide, vec],
        out_shape=[bshape, bshape, bshape, jax.ShapeDtypeStruct((rows, 3 * d), BF16), jax.ShapeDtypeStruct((1, 3 * d), F32)],
        compiler_params=_params(("arbitrary",)),
    )(dmerged, zg, b_gate, br_s, br_a, br_m)


def _adamw(w, g, m, v, *, tr, name):
    rows, cols = w.shape
    assert rows % tr == 0, (name, rows, tr)

    def body(w_ref, g_ref, m_ref, v_ref, d_ref, nm_ref, nv_ref):
        gv = g_ref[...]
        m2 = ADAM_B1 * m_ref[...] + (1.0 - ADAM_B1) * gv
        v2 = ADAM_B2 * v_ref[...] + (1.0 - ADAM_B2) * (gv * gv)
        m_hat = m2 / (1.0 - ADAM_B1 ** ADAM_STEP)
        v_hat = v2 / (1.0 - ADAM_B2 ** ADAM_STEP)
        d_ref[...] = -ADAM_LR * (m_hat / (jnp.sqrt(v_hat) + ADAM_EPS) + ADAM_WD * w_ref[...])
        nm_ref[...] = m2
        nv_ref[...] = v2

    blk = pl.BlockSpec((tr, cols), lambda i: (i, 0))
    shape = jax.ShapeDtypeStruct((rows, cols), F32)
    return pl.pallas_call(
        body, name=name, grid=(rows // tr,), in_specs=[blk] * 4, out_specs=[blk] * 3,
        out_shape=[shape, shape, shape], compiler_params=_params(("parallel",)),
    )(w, g, m, v)


ANY = pl.BlockSpec(memory_space=pl.ANY)


def _position():
    return lax.axis_index("x"), lax.axis_index("y"), lax.axis_index("c")


def _other_chips(x, y):
    return ((1 - x, y), (x, 1 - y), (1 - x, 1 - y))


def _remote(src, dst, send_sem, recv_sem, dev):
    return pltpu.make_async_remote_copy(src_ref=src, dst_ref=dst, send_sem=send_sem, recv_sem=recv_sem,
                                        device_id=dev, device_id_type=MESH)


def _all_gather_weights(bufs):
    nb = len(bufs)

    def body(*refs):
        outs = refs[nb:2 * nb]
        send_sems, recv_sems = refs[2 * nb:]
        x, y, c = _position()
        chip = 2 * x + y
        sibling = (x, y, 1 - c)
        chips = _other_chips(x, y)

        def rows_of(i, owner, core):
            rs = bufs[i].shape[0] // N_CHIPS
            return pl.ds(pl.multiple_of(owner * rs + core * (rs // 2), 16), rs // 2)

        sends = []
        for i in range(nb):
            mine = outs[i].at[rows_of(i, chip, c)]
            for j, (px, py) in enumerate(chips):
                cp = _remote(mine, mine, send_sems.at[i, j], recv_sems.at[i, j], (px, py, c))
                cp.start()
                sends.append(cp)
        for i in range(nb):
            for j, (px, py) in enumerate(chips):
                landed = outs[i].at[rows_of(i, 2 * px + py, c)]
                _remote(landed, landed, send_sems.at[i, j], recv_sems.at[i, j], (px, py, c)).wait_recv()
                cp = _remote(landed, landed, send_sems.at[i, 3 + j], recv_sems.at[i, 3 + j], sibling)
                cp.start()
                sends.append(cp)
        for i in range(nb):
            for j, (px, py) in enumerate(chips):
                dst = outs[i].at[rows_of(i, 2 * px + py, 1 - c)]
                _remote(dst, dst, send_sems.at[i, 3 + j], recv_sems.at[i, 3 + j], sibling).wait_recv()
        for cp in sends:
            cp.wait_send()

    return pl.pallas_call(
        body, name="all_gather_weights", in_specs=[ANY] * nb, out_specs=[ANY] * nb,
        out_shape=[jax.ShapeDtypeStruct(b.shape, b.dtype) for b in bufs],
        input_output_aliases={i: i for i in range(nb)},
        scratch_shapes=[pltpu.SemaphoreType.DMA((nb, 6)), pltpu.SemaphoreType.DMA((nb, 6))],
    )(*bufs)


def _row_tile(rows):
    return max(t for t in range(16, min(rows, 512) + 1, 16) if rows % t == 0)


def _exchange_halves(grads):
    nb = len(grads)

    def body(*refs):
        ins, outs = refs[:nb], refs[nb:2 * nb]
        send_sems, recv_sems = refs[2 * nb:]
        x, y, c = _position()
        copies = []
        for i in range(nb):
            cp = _remote(ins[i].at[:, 1 - c], outs[i], send_sems.at[i], recv_sems.at[i], (x, y, 1 - c))
            cp.start()
            copies.append(cp)
        for cp in copies:
            cp.wait()

    return pl.pallas_call(
        body, name="grad_exchange_halves", in_specs=[ANY] * nb, out_specs=[ANY] * nb,
        out_shape=[jax.ShapeDtypeStruct((N_CHIPS, g.shape[2], g.shape[3]), F32) for g in grads],
        scratch_shapes=[pltpu.SemaphoreType.DMA((nb,)), pltpu.SemaphoreType.DMA((nb,))],
    )(*grads)


def _pair_sum(g4, got, c_arr, *, name):
    _, _, half, cols = g4.shape
    tr = _row_tile(half)

    def body(c_ref, g_ref, t_ref, p_ref, pb_ref):
        sm = g_ref[...] + t_ref[...]
        p_ref[...] = sm
        pb_ref[...] = sm.astype(BF16)

    blk = pl.BlockSpec((None, tr, cols), lambda j, i, c_ref: (j, i, 0))
    grid_spec = pltpu.PrefetchScalarGridSpec(
        num_scalar_prefetch=1, grid=(N_CHIPS, half // tr),
        in_specs=[pl.BlockSpec((None, None, tr, cols), lambda j, i, c_ref: (j, c_ref[0], i, 0)), blk],
        out_specs=[blk, blk])
    return pl.pallas_call(
        body, name=name, grid_spec=grid_spec,
        out_shape=[jax.ShapeDtypeStruct((N_CHIPS, half, cols), F32), jax.ShapeDtypeStruct((N_CHIPS, half, cols), BF16)],
        compiler_params=_params(("parallel", "parallel")),
    )(c_arr, g4, got)


def _scatter_to_owners(parts):
    nb = len(parts)

    def body(*refs):
        ins, outs = refs[:nb], refs[nb:2 * nb]
        send_sems, recv_sems = refs[2 * nb:]
        x, y, c = _position()
        copies = []
        for i in range(nb):
            for j, (px, py) in enumerate(_other_chips(x, y)):
                cp = _remote(ins[i].at[2 * px + py], outs[i].at[j], send_sems.at[i, j], recv_sems.at[i, j], (px, py, c))
                cp.start()
                copies.append(cp)
        for cp in copies:
            cp.wait()

    return pl.pallas_call(
        body, name="grad_scatter_to_owners", in_specs=[ANY] * nb, out_specs=[ANY] * nb,
        out_shape=[jax.ShapeDtypeStruct((3,) + p.shape[1:], p.dtype) for p in parts],
        scratch_shapes=[pltpu.SemaphoreType.DMA((nb, 3)), pltpu.SemaphoreType.DMA((nb, 3))],
    )(*parts)


def _owner_sum(p, got, chip_arr, c_arr, *, replicated, name):
    _, half, cols = p.shape
    tr = _row_tile(half)

    def body(chip_ref, c_ref, p_ref, r_ref, o_ref):
        o_ref[...] = ((p_ref[...] + r_ref[0].astype(F32)) + r_ref[1].astype(F32)) + r_ref[2].astype(F32)

    if replicated:
        out_spec = pl.BlockSpec((None, None, tr, cols), lambda i, chip_ref, c_ref: (chip_ref[0], c_ref[0], i, 0))
        out_shape = jax.ShapeDtypeStruct((N_CHIPS, 2, half, cols), F32)
    else:
        out_spec = pl.BlockSpec((None, tr, cols), lambda i, chip_ref, c_ref: (c_ref[0], i, 0))
        out_shape = jax.ShapeDtypeStruct((2, half, cols), F32)
    grid_spec = pltpu.PrefetchScalarGridSpec(
        num_scalar_prefetch=2, grid=(half // tr,),
        in_specs=[pl.BlockSpec((None, tr, cols), lambda i, chip_ref, c_ref: (chip_ref[0], i, 0)),
                  pl.BlockSpec((3, tr, cols), lambda i, chip_ref, c_ref: (0, i, 0))],
        out_specs=out_spec)
    return pl.pallas_call(
        body, name=name, grid_spec=grid_spec, out_shape=out_shape,
        compiler_params=_params(("parallel",)),
    )(chip_arr, c_arr, p, got)


def _share_reduced(bufs):
    nb = len(bufs) - 1

    def body(*refs):
        outs = refs[nb + 1:2 * nb + 2]
        send_sems, recv_sems = refs[2 * nb + 2:]
        x, y, c = _position()
        chip = 2 * x + y
        sends = []
        for i in range(nb):
            cp = _remote(outs[i].at[c], outs[i].at[c], send_sems.at[i], recv_sems.at[i], (x, y, 1 - c))
            cp.start()
            sends.append(cp)
        small = outs[nb]
        peers = [(fx, fy, fc) for fx in (0, 1) for fy in (0, 1) for fc in (0, 1) if fx + fy + fc > 0]
        for k, (fx, fy, fc) in enumerate(peers):
            dev = (x ^ fx, y ^ fy, c ^ fc)
            cp = _remote(small.at[chip, c], small.at[chip, c], send_sems.at[nb + k], recv_sems.at[nb + k], dev)
            cp.start()
            sends.append(cp)
        for i in range(nb):
            dst = outs[i].at[1 - c]
            _remote(dst, dst, send_sems.at[i], recv_sems.at[i], (x, y, 1 - c)).wait_recv()
        for k, (fx, fy, fc) in enumerate(peers):
            dst = small.at[2 * (x ^ fx) + (y ^ fy), c ^ fc]
            _remote(dst, dst, send_sems.at[nb + k], recv_sems.at[nb + k], (x ^ fx, y ^ fy, c ^ fc)).wait_recv()
        for cp in sends:
            cp.wait_send()

    n_all = nb + 1
    return pl.pallas_call(
        body, name="grad_share_reduced", in_specs=[ANY] * n_all, out_specs=[ANY] * n_all,
        out_shape=[jax.ShapeDtypeStruct(b.shape, b.dtype) for b in bufs],
        input_output_aliases={i: i for i in range(n_all)},
        scratch_shapes=[pltpu.SemaphoreType.DMA((nb + 7,)), pltpu.SemaphoreType.DMA((nb + 7,))],
    )(*bufs)


def _pack_small(vals):
    flat = jnp.concatenate([vals[name].reshape(-1) for name, _ in SMALL])
    return jnp.pad(flat, (0, N_CHIPS * SMALL_ROWS * 1024 - SMALL_ELEMS)).reshape(N_CHIPS * SMALL_ROWS, 1024)


def _unpack_small(buf):
    flat = buf.reshape(-1)
    out, off = {}, 0
    for name, shape in SMALL:
        n = int(np.prod(shape))
        out[name] = flat[off:off + n].reshape(shape)
        off += n
    return out


def _device_step(x, mem, tgt, w, p):
    rows = x.shape[0]
    g1, gm, g2 = p["norm1_g"], p["mem_norm_g"], p["norm2_g"]
    gf = p["final_g"].reshape(1, D_MODEL)
    ssm_args = (p["ssm_lambda_re"][0], p["ssm_lambda_im"][0], p["ssm_log_dt"][0], p["ssm_b_re"][0],
                p["ssm_b_im"][0], p["ssm_c_re"][0], p["ssm_c_im"][0])
    (a_lay, b_blk, c_blk), ssm_vjp = jax.vjp(_ssm_matrices, *ssm_args)
    a_conj = a_lay * _to_scan_layout(jnp.stack([jnp.ones((N_STATES,), F32), -jnp.ones((N_STATES,), F32)]))[None, :]
    dd = p["ssm_d"].reshape(1, SSM_WIDTH)
    win_t = w["w_in"]
    mm = _matmul

    n1 = _rmsnorm_fwd(x, g1, tm=512, name="norm1")
    u = mm(n1, win_t, m=rows, n=512, k=1024, tb=True, tm=2048, tn=512, tk=1024, out_dtypes=(F32,), name="in_u")
    qkv = mm(n1, win_t, m=rows, n=2304, k=1024, tb=True, tm=2048, tn=256, tk=1024, b_off=(OFF_QKV // 256, 0),
             out_dtypes=(F32,), name="in_qkv")
    mq = mm(n1, win_t, m=rows, n=512, k=1024, tb=True, tm=2048, tn=256, tk=1024, b_off=(OFF_MQ // 256, 0),
            out_dtypes=(F32,), name="in_mq")
    zg = mm(n1, win_t, m=rows, n=3072, k=1024, tb=True, tm=2048, tn=256, tk=1024, b_off=(OFF_ZG // 256, 0),
            out_dtypes=(F32,), name="in_zg")

    u_i = _interleave(u)
    ends = _ssm_ends(a_lay, u_i, b_blk, transpose=False, reverse=False, tt=512, name="ssm_fwd_ends")
    s, ys_i, s_entry = _ssm_fwd(a_lay, u_i, b_blk, c_blk, ends, tt=512, name="ssm_fwd")
    ys = _deinterleave(ys_i)
    y0, tglu, y2 = _glu_fwd(ys, u, dd, w["w_glu"], p["b_glu"], tm=512, name="glu_fwd")
    br_s = mm(y2, w["w_ssm_br"], m=rows, n=1024, k=512, tb=True, tm=1024, tn=1024, tk=512, out_dtypes=(F32,), name="br_ssm")

    outs, lses = [], []
    for g, (_, d) in enumerate(ATTN_PATTERNS):
        o_g, lse_g = _attn_fwd(qkv, g, d, name=f"attn_fwd_{g}")
        outs.append(o_g)
        lses.append(lse_g)
    o, lse = _attn_merge(outs, lses, tm=1024, name="attn_merge")
    br_a = mm(o, w["w_attn_br"], m=rows, n=1024, k=256, tb=True, tm=1024, tn=1024, tk=256, out_dtypes=(F32,), name="br_attn")

    mn = _rmsnorm_fwd(mem, gm, tm=MEM_LEN, name="mem_norm")
    kv = mm(mn, w["w_mem_kv"], m=MEM_LEN, n=1024, k=1024, tm=MEM_LEN, tn=1024, tk=1024, out_dtypes=(F32,), name="mem_kv")
    mo = _mem_attn_fwd(mq, kv, tq=1024, name="mem_attn_fwd")
    br_m = mm(mo, w["w_mem_br"], m=rows, n=1024, k=512, tb=True, tm=1024, tn=1024, tk=512, out_dtypes=(F32,), name="br_mem")

    merged = _gate_merge_fwd(zg, p["b_gate"], br_s, br_a, br_m, tm=256, name="gate_merge_fwd")
    add = lambda acc, r: (acc + r,)
    h1 = mm(merged, w["w_o"], m=rows, n=1024, k=1024, tm=1024, tn=1024, tk=1024, out_dtypes=(F32,),
            aux=((x, "mn"),), epilogue=add, name="out_proj")
    n2 = _rmsnorm_fwd(h1, g2, tm=512, name="norm2")
    relu2 = lambda acc: (jnp.square(jnp.maximum(acc, 0.0)),)
    act = mm(n2, w["w_up"], m=rows, n=D_FF, k=1024, tb=True, tm=1024, tn=1024, tk=1024, out_dtypes=(BF16,),
             epilogue=relu2, name="mlp_up")
    h2 = mm(act, w["w_down"], m=rows, n=1024, k=D_FF, tm=1024, tn=1024, tk=1024, out_dtypes=(F32,),
            aux=((h1, "mn"),), epilogue=add, name="mlp_down")
    dh2, loss, d_gf = _loss_head(h2, tgt, gf, tm=512, name="loss_head")

    gb = {}
    gs = {"final_g": d_gf.reshape(D_MODEL)}
    drelu2 = lambda acc, actv: (acc * (2.0 * jnp.sqrt(actv.astype(F32))),)
    dup = mm(dh2, w["w_down"], m=rows, n=D_FF, k=1024, tb=True, tm=1024, tn=1024, tk=1024, out_dtypes=(BF16,),
             aux=((act, "mn"),), epilogue=drelu2, name="d_act")
    gb["w_down"] = mm(act, dh2, m=D_FF, n=1024, k=rows, ta=True, tm=1024, tn=1024, tk=1024, out_dtypes=(F32,), name="dw_down")
    dn2 = mm(dup, w["w_up"], m=rows, n=1024, k=D_FF, tm=1024, tn=1024, tk=1024, out_dtypes=(F32,), name="d_n2")
    gb["w_up"] = mm(dup, n2, m=D_FF, n=1024, k=rows, ta=True, tm=1024, tn=1024, tk=1024, out_dtypes=(F32,), name="dw_up")
    dh1, gs["norm2_g"] = _rmsnorm_bwd(h1, g2, dn2, dh2, tm=512, name="norm2_bwd")
    dmerged = mm(dh1, w["w_o"], m=rows, n=1024, k=1024, tb=True, tm=1024, tn=1024, tk=1024, out_dtypes=(F32,), name="d_merged")
    gb["w_o"] = mm(merged, dh1, m=1024, n=1024, k=rows, ta=True, tm=1024, tn=1024, tk=1024, out_dtypes=(F32,), name="dw_o")
    dbr_s, dbr_a, dbr_m, dzg, gs["b_gate"] = _gate_merge_bwd(dmerged, zg, p["b_gate"], br_s, br_a, br_m, tm=256,
                                                              name="gate_merge_bwd")

    dy2 = mm(dbr_s, w["w_ssm_br"], m=rows, n=512, k=1024, tm=1024, tn=512, tk=1024, out_dtypes=(F32,), name="d_y2")
    gb["w_ssm_br"] = mm(dbr_s, y2, m=1024, n=512, k=rows, ta=True, tm=1024, tn=512, tk=1024, out_dtypes=(F32,), name="dw_ssm_br")
    dy0, dt, y1, gs["b_glu"], d_dd = _glu_bwd(dy2, y0, tglu, u, w["w_glu"], tm=512, name="glu_bwd")
    gs["ssm_d"] = d_dd.reshape(1, SSM_GROUPS, SSM_GROUP_SIZE)
    gb["w_glu"] = mm(y1, dt, m=512, n=512, k=rows, ta=True, tm=512, tn=512, tk=1024, out_dtypes=(F32,), name="dw_glu")
    dy0_i = _interleave(dy0)
    lam_ends = _ssm_ends(a_conj, dy0_i, c_blk, transpose=True, reverse=True, tt=512, name="ssm_bwd_ends")
    du_i, d_b_blk, d_c_blk, d_a_lay = _ssm_bwd(a_conj, dy0_i, u_i, s, s_entry, b_blk, c_blk, dd, lam_ends, tt=512,
                                                name="ssm_bwd")
    du = _deinterleave(du_i)
    d_ssm = ssm_vjp((d_a_lay, d_b_blk, d_c_blk))
    for name, val in zip(("ssm_lambda_re", "ssm_lambda_im", "ssm_log_dt", "ssm_b_re", "ssm_b_im", "ssm_c_re", "ssm_c_im"), d_ssm):
        gs[name] = val[None]

    do = mm(dbr_a, w["w_attn_br"], m=rows, n=256, k=1024, tm=1024, tn=256, tk=1024, out_dtypes=(F32,), name="d_o")
    gb["w_attn_br"] = mm(dbr_a, o, m=1024, n=256, k=rows, ta=True, tm=1024, tn=256, tk=1024, out_dtypes=(F32,), name="dw_attn_br")
    dqkv = None
    for g, (_, d) in enumerate(ATTN_PATTERNS):
        dqkv = _attn_bwd(qkv, do, o, lse, g, d, dqkv, name=f"attn_bwd_{g}")

    dmo = mm(dbr_m, w["w_mem_br"], m=rows, n=512, k=1024, tm=1024, tn=512, tk=1024, out_dtypes=(F32,), name="d_mo")
    gb["w_mem_br"] = mm(dbr_m, mo, m=1024, n=512, k=rows, ta=True, tm=1024, tn=512, tk=1024, out_dtypes=(F32,), name="dw_mem_br")
    dmq, dmk, dmv = _mem_attn_bwd(mq, kv, dmo, tq=1024, name="mem_attn_bwd")
    dkv = jnp.concatenate([dmk, dmv], axis=1)
    gb["w_mem_kv"] = mm(mn, dkv, m=1024, n=1024, k=MEM_LEN, ta=True, tm=1024, tn=1024, tk=MEM_LEN, out_dtypes=(F32,), name="dw_mem_kv")
    dmn = mm(dkv, w["w_mem_kv"], m=MEM_LEN, n=1024, k=1024, tb=True, tm=MEM_LEN, tn=1024, tk=1024, out_dtypes=(F32,), name="d_mn")
    _, gs["mem_norm_g"] = _rmsnorm_bwd(mem, gm, dmn, None, tm=MEM_LEN, name="mem_norm_bwd")

    pieces = ((du, OFF_U, "u"), (dqkv[0], OFF_QKV, "q"), (dqkv[1], OFF_QKV + 768, "k"), (dqkv[2], OFF_QKV + 1536, "v"),
              (dmq, OFF_MQ, "mq"), (dzg, OFF_ZG, "zg"))
    dn = _sum_matmul([piece for piece, _, _ in pieces], win_t, [off for _, off, _ in pieces], tm=512, name="d_n1")
    dw_rows = []
    for piece, off, tag in pieces:
        width = piece.shape[1]
        tmw = 1024 if width % 1024 == 0 else (768 if width == 768 else 512)
        dw_rows.append(mm(piece, n1, m=width, n=1024, k=rows, ta=True, tm=tmw, tn=1024, tk=1024, out_dtypes=(F32,),
                          name="dw_in_" + tag))
    gb["w_in"] = jnp.concatenate(dw_rows, axis=0)
    dx, gs["norm1_g"] = _rmsnorm_bwd(x, g1, dn, dh1, tm=512, name="norm1_bwd")
    return loss, dx, gb, gs


def kernel(x, mem, norm1_g, mem_norm_g, w_in, b_gate, ssm_lambda_re, ssm_lambda_im, ssm_log_dt, ssm_b_re, ssm_b_im, ssm_c_re, ssm_c_im, ssm_d, w_glu, b_glu, w_ssm_br, w_attn_br, w_mem_kv, w_mem_br, w_o, norm2_g, w_up, w_down, final_g, loss_target, m_norm1_g, m_mem_norm_g, m_w_in, m_b_gate, m_ssm_lambda_re, m_ssm_lambda_im, m_ssm_log_dt, m_ssm_b_re, m_ssm_b_im, m_ssm_c_re, m_ssm_c_im, m_ssm_d, m_w_glu, m_b_glu, m_w_ssm_br, m_w_attn_br, m_w_mem_kv, m_w_mem_br, m_w_o, m_norm2_g, m_w_up, m_w_down, m_final_g, v_norm1_g, v_mem_norm_g, v_w_in, v_b_gate, v_ssm_lambda_re, v_ssm_lambda_im, v_ssm_log_dt, v_ssm_b_re, v_ssm_b_im, v_ssm_c_re, v_ssm_c_im, v_ssm_d, v_w_glu, v_b_glu, v_w_ssm_br, v_w_attn_br, v_w_mem_kv, v_w_mem_br, v_w_o, v_norm2_g, v_w_up, v_w_down, v_final_g):
    env = dict(locals())
    weights = {n: env[n] for n in WEIGHT_ORDER}
    moms = {n: env["m_" + n] for n in WEIGHT_ORDER}
    vels = {n: env["v_" + n] for n in WEIGHT_ORDER}
    def shard2d(a):
        return a.reshape(a.shape[-2], a.shape[-1])

    chip = 2 * lax.axis_index("x") + lax.axis_index("y")
    wire = [shard2d(weights[n]).astype(BF16) for n, _, _ in BIG]
    wire = [s.T if tr else s for s, (_, tr, _) in zip(wire, BIG)]
    wire = [lax.dynamic_update_slice(lax.empty((N_CHIPS * s.shape[0], s.shape[1]), BF16), s, (chip * s.shape[0], 0))
            for s in wire]
    w_full = dict(zip([n for n, _, _ in BIG], _all_gather_weights(wire)))
    small = {n: weights[n] for n, _ in SMALL}

    loss, dx, gb, gs = _device_step(x[0], mem[0], loss_target[0], w_full, small)

    c_arr = lax.axis_index("c").astype(jnp.int32).reshape(1)
    chip_arr = chip.astype(jnp.int32).reshape(1)
    names = [n for n, _, _ in BIG] + ["small"]
    full = [gb[n] for n, _, _ in BIG] + [_pack_small(gs)]
    full = [g.reshape(N_CHIPS, 2, g.shape[0] // (2 * N_CHIPS), g.shape[1]) for g in full]
    from_sibling = _exchange_halves(full)
    pairs = [_pair_sum(g, t, c_arr, name="grad_pair_sum_" + n) for g, t, n in zip(full, from_sibling, names)]
    landed = _scatter_to_owners([pb for _, pb in pairs])
    totals = [_owner_sum(pf, r, chip_arr, c_arr, replicated=(n == "small"), name="grad_owner_sum_" + n)
              for (pf, _), r, n in zip(pairs, landed, names)]
    *shards, small_grad = _share_reduced(totals)
    grads = {}
    for (n, tr, _), sh in zip(BIG, shards):
        sh = sh.reshape(2 * sh.shape[1], sh.shape[2])
        grads[n] = sh.T if tr else sh
    small_grad = small_grad.reshape(N_CHIPS * SMALL_ROWS, 1024)
    grads_small = _unpack_small(small_grad)

    delta, new_m, new_v = {}, {}, {}
    for n, _, _ in BIG:
        shape = weights[n].shape
        dn_, nm_, nv_ = _adamw(shard2d(weights[n]), grads[n], shard2d(moms[n]), shard2d(vels[n]),
                               tr=min(shape[-2], 256), name="adamw_" + n)
        delta[n], new_m[n], new_v[n] = dn_.reshape(shape), nm_.reshape(shape), nv_.reshape(shape)
        grads[n] = grads[n].reshape(shape)
    ds_, ms_, vs_ = _adamw(_pack_small(small), small_grad,
                           _pack_small({n: moms[n] for n, _ in SMALL}), _pack_small({n: vels[n] for n, _ in SMALL}),
                           tr=N_CHIPS * SMALL_ROWS, name="adamw_small")
    for dst, buf in ((delta, ds_), (new_m, ms_), (new_v, vs_)):
        dst.update(_unpack_small(buf))
    grads.update(grads_small)

    total_loss = lax.psum(loss[0, 0], ("x", "y", "c"))
    return (total_loss, dx[None], *[grads[n] for n in WEIGHT_ORDER], *[delta[n] for n in WEIGHT_ORDER],
            *[new_m[n] for n in WEIGHT_ORDER], *[new_v[n] for n in WEIGHT_ORDER])
```

```python
import functools
import math

import numpy as np
import jax
import jax.numpy as jnp
from jax import lax
from jax.experimental import pallas as pl
from jax.experimental.pallas import tpu as pltpu

F32 = jnp.float32
BF16 = jnp.bfloat16

D_MODEL = 1024
SSM_GROUPS = 32
SSM_GROUP_SIZE = 16
SSM_STATE = 64
SSM_WIDTH = 512
N_STATES = SSM_GROUPS * SSM_STATE
SCAN_CB = 1024
ATTN_PATTERNS = ((128, 1), (512, 4), (2048, 16))
ATTN_HEAD_DIM = 64
ATTN_Q = 128
MEM_LEN = 256
MEM_HEAD_DIM = 128
MEM_HEADS = 4
D_FF = 4096
OFF_U, OFF_QKV, OFF_MQ, OFF_ZG = 0, 512, 2816, 3328
IN_WIDTH = 6400
RMS_EPS = 1e-6
NEG_INF = -1e30
ADAM_LR, ADAM_B1, ADAM_B2, ADAM_EPS, ADAM_WD, ADAM_STEP = 0.001, 0.9, 0.999, 1e-08, 0.01, 10

VMEM_LIMIT_BYTES = 48 * 1024 * 1024
LANES = 128
MXU_WIDTH = 256
MESH = pl.DeviceIdType.MESH
N_CHIPS = 4

SCAN_SEGS = 8
SCAN_GROUPS = SCAN_CB // SSM_STATE

BIG = (("w_in", True, (6400, 1024)), ("w_glu", False, (512, 512)), ("w_ssm_br", True, (1024, 512)),
       ("w_attn_br", True, (1024, 256)), ("w_mem_kv", False, (1024, 1024)), ("w_mem_br", True, (1024, 512)),
       ("w_o", False, (1024, 1024)), ("w_up", True, (4096, 1024)), ("w_down", False, (4096, 1024)))
SMALL = (("norm1_g", (1, 1024)), ("mem_norm_g", (1, 1024)), ("b_gate", (1, 3072)),
         ("ssm_lambda_re", (1, 32, 64)), ("ssm_lambda_im", (1, 32, 64)), ("ssm_log_dt", (1, 32)),
         ("ssm_b_re", (1, 32, 64, 16)), ("ssm_b_im", (1, 32, 64, 16)), ("ssm_c_re", (1, 32, 16, 64)),
         ("ssm_c_im", (1, 32, 16, 64)), ("ssm_d", (1, 32, 16)), ("b_glu", (1, 512)),
         ("norm2_g", (1, 1024)), ("final_g", (1024,)))
WEIGHT_ORDER = ("norm1_g", "mem_norm_g", "w_in", "b_gate", "ssm_lambda_re", "ssm_lambda_im", "ssm_log_dt",
                "ssm_b_re", "ssm_b_im", "ssm_c_re", "ssm_c_im", "ssm_d", "w_glu", "b_glu", "w_ssm_br",
                "w_attn_br", "w_mem_kv", "w_mem_br", "w_o", "norm2_g", "w_up", "w_down", "final_g")
SMALL_ELEMS = sum(int(np.prod(s)) for _, s in SMALL)
SMALL_ROWS = 64


def _params(sem):
    return pltpu.CompilerParams(dimension_semantics=sem, vmem_limit_bytes=VMEM_LIMIT_BYTES)


def _sigmoid(v):
    return 1.0 / (1.0 + jnp.exp(-v))


_GELU_C = math.sqrt(2.0 / math.pi)


def _gelu(v):
    return 0.5 * v * (1.0 + jnp.tanh(_GELU_C * (v + 0.044715 * v * v * v)))


def _gelu_grad(v):
    th = jnp.tanh(_GELU_C * (v + 0.044715 * v * v * v))
    return 0.5 * (1.0 + th) + 0.5 * v * (1.0 - th * th) * _GELU_C * (1.0 + 3.0 * 0.044715 * v * v)


def _dot(a, b, ca, cb):
    return lax.dot_general(a, b, (((ca,), (cb,)), ((), ())), preferred_element_type=F32)


def _matmul(a, b, *, m, n, k, ta=False, tb=False, tm, tn, tk, out_dtypes, name,
            a_off=(0, 0), b_off=(0, 0), aux=(), epilogue=None):
    assert m % tm == 0 and n % tn == 0 and k % tk == 0, (name, m, n, k, tm, tn, tk)
    nk = k // tk
    n_aux = len(aux)
    n_out = len(out_dtypes)
    ar, ac = a_off
    br, bc = b_off
    if ta:
        a_spec = pl.BlockSpec((tk, tm), lambda i, j, kk: (kk + ar, i + ac))
    else:
        a_spec = pl.BlockSpec((tm, tk), lambda i, j, kk: (i + ar, kk + ac))
    if tb:
        b_spec = pl.BlockSpec((tn, tk), lambda i, j, kk: (j + br, kk + bc))
    else:
        b_spec = pl.BlockSpec((tk, tn), lambda i, j, kk: (kk + br, j + bc))
    aux_specs = []
    for _, kind in aux:
        if kind == "mn":
            aux_specs.append(pl.BlockSpec((tm, tn), lambda i, j, kk: (i, j)))
        else:
            aux_specs.append(pl.BlockSpec((1, tn), lambda i, j, kk: (0, j)))
    ca = 0 if ta else 1
    cb = 1 if tb else 0

    chunk = MXU_WIDTH if tn % MXU_WIDTH == 0 else tn
    cols = [pl.ds(c0, chunk) for c0 in range(0, tn, chunk)]

    def finish(acc, aux_refs, out_refs, cs):
        auxv = [r[:, cs] for r in aux_refs]
        outs = (acc,) if epilogue is None else epilogue(acc, *auxv)
        for o_ref, o in zip(out_refs, outs):
            o_ref[:, cs] = o.astype(o_ref.dtype)

    def body(a_ref, b_ref, *rest):
        aux_refs = rest[:n_aux]
        out_refs = rest[n_aux:n_aux + n_out]

        def products():
            av = a_ref[...].astype(BF16)
            for cs in cols:
                bv = (b_ref[cs, :] if tb else b_ref[:, cs]).astype(BF16)
                yield cs, _dot(av, bv, ca, cb)

        if nk == 1:
            for cs, prod in products():
                finish(prod, aux_refs, out_refs, cs)
            return
        acc_ref = rest[n_aux + n_out]
        kk = pl.program_id(2)

        @pl.when(kk == 0)
        def _():
            for cs, prod in products():
                acc_ref[:, cs] = prod

        @pl.when(jnp.logical_and(kk > 0, kk < nk - 1))
        def _():
            for cs, prod in products():
                acc_ref[:, cs] += prod

        @pl.when(kk == nk - 1)
        def _():
            for cs, prod in products():
                finish(acc_ref[:, cs] + prod, aux_refs, out_refs, cs)

    res = pl.pallas_call(
        body, name=name, grid=(m // tm, n // tn, nk),
        in_specs=[a_spec, b_spec] + aux_specs,
        out_specs=[pl.BlockSpec((tm, tn), lambda i, j, kk: (i, j)) for _ in range(n_out)],
        out_shape=[jax.ShapeDtypeStruct((m, n), dt) for dt in out_dtypes],
        scratch_shapes=[pltpu.VMEM((tm, tn), F32)] if nk > 1 else [],
        compiler_params=_params(("parallel", "parallel", "arbitrary")),
    )(a, b, *[x for x, _ in aux])
    return res[0] if n_out == 1 else tuple(res)


def _sum_matmul(pieces, b, offs, *, tm, name):
    m = pieces[0].shape[0]
    n = b.shape[1]
    npieces = len(pieces)

    def body(*refs):
        b_ref, o_ref = refs[npieces], refs[npieces + 1]
        acc = None
        for p_ref, off in zip(refs[:npieces], offs):
            part = _dot(p_ref[...].astype(BF16), b_ref[pl.ds(off, p_ref.shape[1]), :], 1, 0)
            acc = part if acc is None else acc + part
        o_ref[...] = acc

    return pl.pallas_call(
        body, name=name, grid=(m // tm,),
        in_specs=[pl.BlockSpec((tm, p.shape[1]), lambda i: (i, 0)) for p in pieces]
        + [pl.BlockSpec(b.shape, lambda i: (0, 0), pipeline_mode=pl.Buffered(1))],
        out_specs=pl.BlockSpec((tm, n), lambda i: (i, 0)),
        out_shape=jax.ShapeDtypeStruct((m, n), F32),
        compiler_params=_params(("parallel",)),
    )(*pieces, b)


def _rmsnorm_fwd(x, g, *, tm, name):
    rows, d = x.shape

    def body(x_ref, g_ref, o_ref):
        xv = x_ref[...]
        r = lax.rsqrt(jnp.mean(xv * xv, axis=-1, keepdims=True) + RMS_EPS)
        o_ref[...] = (xv * r * g_ref[...]).astype(o_ref.dtype)

    return pl.pallas_call(
        body, name=name, grid=(rows // tm,),
        in_specs=[pl.BlockSpec((tm, d), lambda i: (i, 0)), pl.BlockSpec((1, d), lambda i: (0, 0))],
        out_specs=pl.BlockSpec((tm, d), lambda i: (i, 0)),
        out_shape=jax.ShapeDtypeStruct((rows, d), BF16),
        compiler_params=_params(("parallel",)),
    )(x, g)


def _rmsnorm_bwd(x, g, dy, res, *, tm, name):
    rows, d = x.shape
    has_res = res is not None

    def body(x_ref, g_ref, dy_ref, *rest):
        if has_res:
            res_ref, dx_ref, dg_ref = rest
        else:
            dx_ref, dg_ref = rest
        i = pl.program_id(0)
        xv = x_ref[...]
        r = lax.rsqrt(jnp.mean(xv * xv, axis=-1, keepdims=True) + RMS_EPS)
        xhat = xv * r
        dyv = dy_ref[...]
        dyg = dyv * g_ref[...]
        dx = r * (dyg - xhat * jnp.mean(dyg * xhat, axis=-1, keepdims=True))
        if has_res:
            dx = dx + res_ref[...]
        dx_ref[...] = dx

        @pl.when(i == 0)
        def _():
            dg_ref[...] = jnp.zeros_like(dg_ref)

        dg_ref[...] += jnp.sum(dyv * xhat, axis=0, keepdims=True)

    row_spec = pl.BlockSpec((tm, d), lambda i: (i, 0))
    vec_spec = pl.BlockSpec((1, d), lambda i: (0, 0))
    ins = [x, g, dy] + ([res] if has_res else [])
    return pl.pallas_call(
        body, name=name, grid=(rows // tm,),
        in_specs=[row_spec, vec_spec, row_spec] + ([row_spec] if has_res else []),
        out_specs=[row_spec, vec_spec],
        out_shape=[jax.ShapeDtypeStruct((rows, d), F32), jax.ShapeDtypeStruct((1, d), F32)],
        compiler_params=_params(("arbitrary",)),
    )(*ins)


def _loss_head(h, tgt, g, *, tm, name):
    rows, d = h.shape
    nsteps = rows // tm

    def body(h_ref, t_ref, g_ref, dh_ref, loss_ref, dg_ref, sq_ref):
        i = pl.program_id(0)
        xv = h_ref[...]
        gv = g_ref[...]
        r = lax.rsqrt(jnp.mean(xv * xv, axis=-1, keepdims=True) + RMS_EPS)
        xhat = xv * r
        err = xhat * gv - t_ref[...]
        dyv = err * (1.0 / d)
        dyg = dyv * gv
        dh_ref[...] = r * (dyg - xhat * jnp.mean(dyg * xhat, axis=-1, keepdims=True))

        @pl.when(i == 0)
        def _():
            dg_ref[...] = jnp.zeros_like(dg_ref)
            sq_ref[...] = jnp.zeros_like(sq_ref)

        dg_ref[...] += jnp.sum(dyv * xhat, axis=0, keepdims=True)
        sq_ref[...] += jnp.sum(err * err, axis=0, keepdims=True)

        @pl.when(i == nsteps - 1)
        def _():
            tot = jnp.sum(sq_ref[...], axis=-1, keepdims=True) * (0.5 / d)
            loss_ref[...] = jnp.broadcast_to(tot, loss_ref.shape)

    row_spec = pl.BlockSpec((tm, d), lambda i: (i, 0))
    vec_spec = pl.BlockSpec((1, d), lambda i: (0, 0))
    return pl.pallas_call(
        body, name=name, grid=(nsteps,),
        in_specs=[row_spec, row_spec, vec_spec],
        out_specs=[row_spec, pl.BlockSpec((1, LANES), lambda i: (0, 0)), vec_spec],
        out_shape=[jax.ShapeDtypeStruct((rows, d), F32), jax.ShapeDtypeStruct((1, LANES), F32),
                   jax.ShapeDtypeStruct((1, d), F32)],
        scratch_shapes=[pltpu.VMEM((1, d), F32)],
        compiler_params=_params(("arbitrary",)),
    )(h, tgt, g)


def _to_scan_layout(v):
    lead = v.shape[:-2]
    v = v.reshape(lead + (2, N_STATES // SCAN_CB, SCAN_CB))
    v = jnp.swapaxes(v, -3, -2)
    return v.reshape(lead + (2 * N_STATES,))


def _ssm_matrices(lam_re, lam_im, log_dt, b_re, b_im, c_re, c_im):
    dt = jnp.exp(log_dt)[:, None]
    mag = jnp.exp(lam_re * dt)
    a_re, a_im = mag * jnp.cos(lam_im * dt), mag * jnp.sin(lam_im * dt)
    nr, ni = a_re - 1.0, a_im
    den = lam_re * lam_re + lam_im * lam_im
    coef_re = (nr * lam_re + ni * lam_im) / den
    coef_im = (ni * lam_re - nr * lam_im) / den
    bb_re = coef_re[..., None] * b_re - coef_im[..., None] * b_im
    bb_im = coef_re[..., None] * b_im + coef_im[..., None] * b_re
    a_lay = _to_scan_layout(jnp.stack([a_re.reshape(-1), a_im.reshape(-1)], axis=0))[None, :]
    nblk = SSM_GROUPS // SCAN_GROUPS
    eye = jnp.eye(SCAN_GROUPS, dtype=F32)

    def b_block(bb):
        bb = bb.reshape(nblk, SCAN_GROUPS, SSM_STATE, SSM_GROUP_SIZE)
        return jnp.einsum("gk,jkph->jghkp", eye, bb).reshape(nblk, SCAN_GROUPS * SSM_GROUP_SIZE, SCAN_CB)

    b_blk = jnp.concatenate([b_block(bb_re), b_block(bb_im)], axis=2)

    def c_block(cc):
        cc = cc.reshape(nblk, SCAN_GROUPS, SSM_GROUP_SIZE, SSM_STATE)
        return jnp.einsum("gk,jghp->jkpgh", eye, cc).reshape(nblk, SCAN_CB, SCAN_GROUPS * SSM_GROUP_SIZE)

    c_blk = jnp.concatenate([c_block(c_re), -c_block(c_im)], axis=1)
    return a_lay, b_blk, c_blk


def _interleave(v):
    rows, c = v.shape
    return v.reshape(SCAN_SEGS, rows // SCAN_SEGS, c).transpose(1, 0, 2).reshape(rows, c)


def _deinterleave(v):
    rows, c = v.shape
    return v.reshape(rows // SCAN_SEGS, SCAN_SEGS, c).transpose(1, 0, 2).reshape(rows, c)


def _scan_groups(a_ref, bu_ref, o_ref, state, *, reverse, tt):
    cb = SCAN_CB
    ar = jnp.broadcast_to(a_ref[:, :cb], (SCAN_SEGS, cb))
    ai = jnp.broadcast_to(a_ref[:, cb:], (SCAN_SEGS, cb))
    ngroups = tt // SCAN_SEGS

    def step(i, st):
        sr, si = st
        r0 = pl.multiple_of(((ngroups - 1 - i) if reverse else i) * SCAN_SEGS, SCAN_SEGS)
        blk = bu_ref[pl.ds(r0, SCAN_SEGS), :]
        nr = ar * sr - ai * si + blk[:, :cb]
        ni = ar * si + ai * sr + blk[:, cb:]
        if o_ref is not None:
            o_ref[pl.ds(r0, SCAN_SEGS), :] = jnp.concatenate([nr, ni], axis=1)
        return nr, ni

    return lax.fori_loop(0, ngroups, step, state, unroll=4)


def _segment_entries(a_ref, e_ref, init_ref, *, reverse, seg_len):
    cb = SCAN_CB
    n_sq = seg_len.bit_length() - 1
    assert 1 << n_sq == seg_len, seg_len
    pr, pi = a_ref[:, :cb], a_ref[:, cb:]
    for _ in range(n_sq):
        pr, pi = pr * pr - pi * pi, 2.0 * pr * pi
    cr = jnp.zeros((1, cb), F32)
    ci = jnp.zeros((1, cb), F32)
    order = range(SCAN_SEGS - 1, -1, -1) if reverse else range(SCAN_SEGS)
    for k, seg in enumerate(order):
        if k > 0:
            prev = seg + 1 if reverse else seg - 1
            er, ei = e_ref[prev:prev + 1, :cb], e_ref[prev:prev + 1, cb:]
            cr, ci = pr * cr - pi * ci + er, pr * ci + pi * cr + ei
        init_ref[seg:seg + 1, :] = jnp.concatenate([cr, ci], axis=1)


def _ssm_specs(nt, tt, nch, reverse):
    cb = SCAN_CB
    tmap = (lambda j, kk: (nt - 1 - kk, j)) if reverse else (lambda j, kk: (kk, j))
    return dict(a=pl.BlockSpec((1, 2 * cb), lambda j, kk: (0, j)),
                seg=pl.BlockSpec((SCAN_SEGS, 2 * cb), lambda j, kk: (0, j)),
                chan=pl.BlockSpec((tt, nch), tmap),
                state=pl.BlockSpec((tt, 2 * cb), tmap),
                b=pl.BlockSpec((None, nch, 2 * cb), lambda j, kk: (j, 0, 0)),
                c=pl.BlockSpec((None, 2 * cb, nch), lambda j, kk: (j, 0, 0)))


def _ssm_ends(a_lay, x, blocks, *, transpose, reverse, tt, name):
    rows = x.shape[0]
    nblk = blocks.shape[0]
    nch = x.shape[1] // nblk
    cb = SCAN_CB
    nt = rows // tt
    sp = _ssm_specs(nt, tt, nch, reverse)

    def body(a_ref, x_ref, w_ref, e_ref, bu_ref):
        kk = pl.program_id(1)

        @pl.when(kk == 0)
        def _():
            e_ref[...] = jnp.zeros_like(e_ref)

        bu_ref[...] = _dot(x_ref[...].astype(BF16), w_ref[...].astype(BF16), 1, 1 if transpose else 0)
        sr, si = _scan_groups(a_ref, bu_ref, None, (e_ref[:, :cb], e_ref[:, cb:]), reverse=reverse, tt=tt)
        e_ref[...] = jnp.concatenate([sr, si], axis=1)

    return pl.pallas_call(
        body, name=name, grid=(nblk, nt),
        in_specs=[sp["a"], sp["chan"], sp["c"] if transpose else sp["b"]],
        out_specs=sp["seg"],
        out_shape=jax.ShapeDtypeStruct((SCAN_SEGS, nblk * 2 * cb), F32),
        scratch_shapes=[pltpu.VMEM((tt, 2 * cb), F32)],
        compiler_params=_params(("parallel", "arbitrary")),
    )(a_lay, x, blocks)


def _ssm_fwd(a_lay, u, b_blk, c_blk, ends, *, tt, name):
    rows = u.shape[0]
    nblk = b_blk.shape[0]
    nch = u.shape[1] // nblk
    cb = SCAN_CB
    nt = rows // tt
    sp = _ssm_specs(nt, tt, nch, False)

    def body(a_ref, e_ref, u_ref, b_ref, c_ref, s_ref, y_ref, init_ref, carry_ref):
        kk = pl.program_id(1)

        @pl.when(kk == 0)
        def _():
            _segment_entries(a_ref, e_ref, init_ref, reverse=False, seg_len=rows // SCAN_SEGS)
            carry_ref[...] = init_ref[...]

        s_ref[...] = _dot(u_ref[...].astype(BF16), b_ref[...].astype(BF16), 1, 0)
        sr, si = _scan_groups(a_ref, s_ref, s_ref, (carry_ref[:, :cb], carry_ref[:, cb:]), reverse=False, tt=tt)
        carry_ref[...] = jnp.concatenate([sr, si], axis=1)
        y_ref[...] = _dot(s_ref[...].astype(BF16), c_ref[...].astype(BF16), 1, 0)

    return pl.pallas_call(
        body, name=name, grid=(nblk, nt),
        in_specs=[sp["a"], sp["seg"], sp["chan"], sp["b"], sp["c"]],
        out_specs=[sp["state"], sp["chan"], sp["seg"]],
        out_shape=[jax.ShapeDtypeStruct((rows, nblk * 2 * cb), F32), jax.ShapeDtypeStruct((rows, nblk * nch), F32),
                   jax.ShapeDtypeStruct((SCAN_SEGS, nblk * 2 * cb), F32)],
        scratch_shapes=[pltpu.VMEM((SCAN_SEGS, 2 * cb), F32)],
        compiler_params=_params(("parallel", "arbitrary")),
    )(a_lay, ends, u, b_blk, c_blk)


def _ssm_bwd(a_conj, dy, u, s, s_entry, b_blk, c_blk, dd, ends, *, tt, name):
    rows = u.shape[0]
    nblk = b_blk.shape[0]
    nch = u.shape[1] // nblk
    cb = SCAN_CB
    nt = rows // tt
    sp = _ssm_specs(nt, tt, nch, True)
    groups_per_tile = tt // SCAN_SEGS
    before = pl.BlockSpec((SCAN_SEGS, 2 * cb), lambda j, kk: (jnp.maximum((nt - 1 - kk) * groups_per_tile - 1, 0), j))

    def body(a_ref, e_ref, dy_ref, u_ref, s_ref, before_ref, entry_ref, b_ref, c_ref, dd_ref,
             du_ref, db_ref, dc_ref, da_ref, lam_ref, carry_ref):
        kk = pl.program_id(1)

        @pl.when(kk == 0)
        def _():
            _segment_entries(a_ref, e_ref, carry_ref, reverse=True, seg_len=rows // SCAN_SEGS)
            db_ref[...] = jnp.zeros_like(db_ref)
            dc_ref[...] = jnp.zeros_like(dc_ref)
            da_ref[...] = jnp.zeros_like(da_ref)

        dyv = dy_ref[...]
        dyb = dyv.astype(BF16)
        lam_ref[...] = _dot(dyb, c_ref[...].astype(BF16), 1, 1)
        lr, li = _scan_groups(a_ref, lam_ref, lam_ref, (carry_ref[:, :cb], carry_ref[:, cb:]), reverse=True, tt=tt)
        carry_ref[...] = jnp.concatenate([lr, li], axis=1)

        lamb = lam_ref[...].astype(BF16)
        du_ref[...] = _dot(lamb, b_ref[...].astype(BF16), 1, 1) + dd_ref[...] * dyv
        db_ref[...] += _dot(u_ref[...].astype(BF16), lamb, 0, 0)
        dc_ref[...] += _dot(s_ref[...].astype(BF16), dyb, 0, 0)

        first = jnp.where(kk == nt - 1, entry_ref[...], before_ref[...])
        rest = tt - SCAN_SEGS
        lam_hi = lam_ref[pl.ds(SCAN_SEGS, rest), :]
        s_lo = s_ref[pl.ds(0, rest), :]
        lam_lo = lam_ref[pl.ds(0, SCAN_SEGS), :]

        def pair(lv, pv):
            lre, lim, pre, pim = lv[:, :cb], lv[:, cb:], pv[:, :cb], pv[:, cb:]
            return (jnp.sum(lre * pre + lim * pim, axis=0, keepdims=True),
                    jnp.sum(lim * pre - lre * pim, axis=0, keepdims=True))

        r1, i1 = pair(lam_hi, s_lo)
        r0, i0 = pair(lam_lo, first)
        da_ref[...] += jnp.concatenate([r1 + r0, i1 + i0], axis=1)

    return pl.pallas_call(
        body, name=name, grid=(nblk, nt),
        in_specs=[sp["a"], sp["seg"], sp["chan"], sp["chan"], sp["state"], before, sp["seg"], sp["b"], sp["c"],
                  pl.BlockSpec((1, nch), lambda j, kk: (0, j))],
        out_specs=[sp["chan"], sp["b"], sp["c"], pl.BlockSpec((1, 2 * cb), lambda j, kk: (0, j))],
        out_shape=[jax.ShapeDtypeStruct((rows, nblk * nch), F32), jax.ShapeDtypeStruct(b_blk.shape, F32),
                   jax.ShapeDtypeStruct(c_blk.shape, F32), jax.ShapeDtypeStruct((1, nblk * 2 * cb), F32)],
        scratch_shapes=[pltpu.VMEM((tt, 2 * cb), F32), pltpu.VMEM((SCAN_SEGS, 2 * cb), F32)],
        compiler_params=_params(("parallel", "arbitrary")),
    )(a_conj, ends, dy, u, s, s, s_entry, b_blk, c_blk, dd)


def _glu_fwd(ys, u, dd, w_glu, b_glu, *, tm, name):
    rows, w = ys.shape

    def body(ys_ref, u_ref, dd_ref, w_ref, b_ref, y0_ref, t_ref, y2_ref):
        y0 = ys_ref[...] + dd_ref[...] * u_ref[...]
        y1 = _gelu(y0)
        t = _dot(y1.astype(BF16), w_ref[...], 1, 0) + b_ref[...]
        y0_ref[...] = y0
        t_ref[...] = t
        y2_ref[...] = (y1 * _sigmoid(t)).astype(BF16)

    row = pl.BlockSpec((tm, w), lambda i: (i, 0))
    vec = pl.BlockSpec((1, w), lambda i: (0, 0))
    return pl.pallas_call(
        body, name=name, grid=(rows // tm,),
        in_specs=[row, row, vec, pl.BlockSpec((w, w), lambda i: (0, 0)), vec],
        out_specs=[row, row, row],
        out_shape=[jax.ShapeDtypeStruct((rows, w), F32), jax.ShapeDtypeStruct((rows, w), F32),
                   jax.ShapeDtypeStruct((rows, w), BF16)],
        compiler_params=_params(("parallel",)),
    )(ys, u, dd, w_glu, b_glu)


def _glu_bwd(dy2, y0, t, u, w_glu, *, tm, name):
    rows, w = y0.shape

    def body(dy2_ref, y0_ref, t_ref, u_ref, w_ref, dy0_ref, dt_ref, y1_ref, db_ref, dd_ref):
        i = pl.program_id(0)
        y0 = y0_ref[...]
        y1 = _gelu(y0)
        sg = _sigmoid(t_ref[...])
        dy2v = dy2_ref[...]
        dt = dy2v * y1 * sg * (1.0 - sg)
        dy1 = dy2v * sg + _dot(dt.astype(BF16), w_ref[...], 1, 1)
        dy0 = dy1 * _gelu_grad(y0)
        dy0_ref[...] = dy0
        dt_ref[...] = dt.astype(BF16)
        y1_ref[...] = y1.astype(BF16)

        @pl.when(i == 0)
        def _():
            db_ref[...] = jnp.zeros_like(db_ref)
            dd_ref[...] = jnp.zeros_like(dd_ref)

        db_ref[...] += jnp.sum(dt, axis=0, keepdims=True)
        dd_ref[...] += jnp.sum(dy0 * u_ref[...], axis=0, keepdims=True)

    row = pl.BlockSpec((tm, w), lambda i: (i, 0))
    vec = pl.BlockSpec((1, w), lambda i: (0, 0))
    return pl.pallas_call(
        body, name=name, grid=(rows // tm,),
        in_specs=[row, row, row, row, pl.BlockSpec((w, w), lambda i: (0, 0))],
        out_specs=[row, row, row, vec, vec],
        out_shape=[jax.ShapeDtypeStruct((rows, w), F32), jax.ShapeDtypeStruct((rows, w), BF16),
                   jax.ShapeDtypeStruct((rows, w), BF16), jax.ShapeDtypeStruct((1, w), F32),
                   jax.ShapeDtypeStruct((1, w), F32)],
        compiler_params=_params(("arbitrary",)),
    )(dy2, y0, t, u, w_glu)


ATTN_TILE = 2048


def _attn_geometry(rows, d):
    sb = ATTN_Q * d
    tr = max(sb, min(ATTN_TILE, rows))
    assert rows % tr == 0 and tr % sb == 0, (rows, d)
    return sb, tr, rows // tr, tr // sb


def _attn_masks():
    qi = lax.broadcasted_iota(jnp.int32, (2 * ATTN_Q, 2 * ATTN_Q), 0) % ATTN_Q
    kj = lax.broadcasted_iota(jnp.int32, (2 * ATTN_Q, 2 * ATTN_Q), 1)
    own_ok = jnp.logical_and(kj >= ATTN_Q, kj - ATTN_Q <= qi)
    prev_ok = jnp.logical_and(kj < ATTN_Q, kj >= qi)
    bias_first = jnp.where(own_ok, 0.0, NEG_INF)
    bias_other = jnp.where(jnp.logical_or(own_ok, prev_ok), 0.0, NEG_INF)
    head0 = lax.broadcasted_iota(jnp.int32, (ATTN_Q, LANES), 1) < ATTN_HEAD_DIM
    return bias_first, bias_other, head0


def _attn_rows(base, n, d):
    return pl.ds(pl.multiple_of(base, ATTN_Q), n) if d == 1 else pl.ds(base, n, stride=d)


def _stack_heads(v, head0):
    return jnp.concatenate([jnp.where(head0, v, 0.0), jnp.where(head0, 0.0, v)], axis=0)


def _unstack_heads(v, head0):
    return jnp.where(head0, v[:ATTN_Q], v[ATTN_Q:])


def _fill_keys(buf, prev_ref, cur_ref, sb):
    buf[pl.ds(0, sb), :] = prev_ref[...]
    buf[pl.ds(sb, cur_ref.shape[0]), :] = cur_ref[...]


def _attn_fwd(qkv, g, d, *, name):
    rows = qkv.shape[0]
    sb, tr, ntiles, nsub = _attn_geometry(rows, d)
    qc, kc, vc = 2 * g, 6 + 2 * g, 12 + 2 * g
    scale = ATTN_HEAD_DIM ** -0.5

    def body(q_ref, kc_ref, kp_ref, vc_ref, vp_ref, o_ref, lse_ref, kbuf, vbuf):
        n = pl.program_id(0)
        _fill_keys(kbuf, kp_ref, kc_ref, sb)
        _fill_keys(vbuf, vp_ref, vc_ref, sb)
        bias_first, bias_other, head0 = _attn_masks()

        def per_block(idx, carry):
            j, r = idx // d, idx % d
            base = j * sb + r
            bias = jnp.where(jnp.logical_and(n == 0, j == 0), bias_first, bias_other)
            qrows = _attn_rows(base, ATTN_Q, d)
            krows = _attn_rows(base, 2 * ATTN_Q, d)
            qs = _stack_heads(q_ref[qrows, :], head0).astype(BF16)
            s = _dot(qs, kbuf[krows, :].astype(BF16), 1, 1) * scale + bias
            mx = jnp.max(s, axis=-1, keepdims=True)
            p = jnp.exp(s - mx)
            den = jnp.sum(p, axis=-1, keepdims=True)
            pv = _dot(p.astype(BF16), vbuf[krows, :].astype(BF16), 1, 0) / den
            o_ref[qrows, :] = _unstack_heads(pv, head0)
            lse_ref[qrows, :] = _unstack_heads(jnp.broadcast_to(mx + jnp.log(den), (2 * ATTN_Q, LANES)), head0)
            return carry

        lax.fori_loop(0, nsub * d, per_block, 0, unroll=8)

    def cur(col):
        return pl.BlockSpec((tr, LANES), lambda n, hp: (n, col + hp))

    def prev(col):
        return pl.BlockSpec((sb, LANES), lambda n, hp: (jnp.maximum(n * nsub - 1, 0), col + hp))

    out_spec = pl.BlockSpec((tr, LANES), lambda n, hp: (n, hp))
    return pl.pallas_call(
        body, name=name, grid=(ntiles, 2),
        in_specs=[cur(qc), cur(kc), prev(kc), cur(vc), prev(vc)],
        out_specs=[out_spec, out_spec],
        out_shape=[jax.ShapeDtypeStruct((rows, 2 * LANES), F32), jax.ShapeDtypeStruct((rows, 2 * LANES), F32)],
        scratch_shapes=[pltpu.VMEM((sb + tr, LANES), F32), pltpu.VMEM((sb + tr, LANES), F32)],
        compiler_params=_params(("parallel", "parallel")),
    )(qkv, qkv, qkv, qkv, qkv)


def _attn_merge(outs, lses, *, tm, name):
    rows, w = outs[0].shape

    def body(o0, o1, o2, l0, l1, l2, o_ref, lse_ref):
        a0, a1, a2 = l0[...], l1[...], l2[...]
        mx = jnp.maximum(jnp.maximum(a0, a1), a2)
        e0, e1, e2 = jnp.exp(a0 - mx), jnp.exp(a1 - mx), jnp.exp(a2 - mx)
        den = e0 + e1 + e2
        o_ref[...] = (e0 / den) * o0[...] + (e1 / den) * o1[...] + (e2 / den) * o2[...]
        lse_ref[...] = mx + jnp.log(den)

    row = pl.BlockSpec((tm, w), lambda i: (i, 0))
    return pl.pallas_call(
        body, name=name, grid=(rows // tm,), in_specs=[row] * 6, out_specs=[row, row],
        out_shape=[jax.ShapeDtypeStruct((rows, w), F32), jax.ShapeDtypeStruct((rows, w), F32)],
        compiler_params=_params(("parallel",)),
    )(*outs, *lses)


def _attn_bwd(qkv, do, o, lse, g, d, prev, *, name):
    rows = qkv.shape[0]
    sb, tr, ntiles, nsub = _attn_geometry(rows, d)
    qc, kc, vc = 2 * g, 6 + 2 * g, 12 + 2 * g
    scale = ATTN_HEAD_DIM ** -0.5

    def body(q_ref, kc_ref, kp_ref, vc_ref, vp_ref, do_ref, o_ref, lse_ref, dq_ref, dk_ref, dv_ref,
             kbuf, vbuf, dk_acc, dv_acc):
        n = pl.program_id(1)

        @pl.when(n == 0)
        def _():
            dk_acc[pl.ds(0, tr), :] = jnp.zeros((tr, LANES), F32)
            dv_acc[pl.ds(0, tr), :] = jnp.zeros((tr, LANES), F32)

        @pl.when(n < ntiles)
        def _():
            dk_acc[pl.ds(tr, tr), :] = jnp.zeros((tr, LANES), F32)
            dv_acc[pl.ds(tr, tr), :] = jnp.zeros((tr, LANES), F32)
            _fill_keys(kbuf, kp_ref, kc_ref, sb)
            _fill_keys(vbuf, vp_ref, vc_ref, sb)
            bias_first, bias_other, head0 = _attn_masks()
            lane = lax.broadcasted_iota(jnp.int32, (ATTN_Q, LANES), 1)

            def per_block(idx, carry):
                j, r = idx // d, idx % d
                base = j * sb + r
                bias = jnp.where(jnp.logical_and(n == 0, j == 0), bias_first, bias_other)
                qrows = _attn_rows(base, ATTN_Q, d)
                krows = _attn_rows(base, 2 * ATTN_Q, d)
                arows = _attn_rows(base + (tr - sb), 2 * ATTN_Q, d)
                qs = _stack_heads(q_ref[qrows, :], head0).astype(BF16)
                dos = _stack_heads(do_ref[qrows, :], head0)
                dosb = dos.astype(BF16)
                ov = o_ref[qrows, :]
                delta = jnp.sum(dos * jnp.concatenate([ov, ov], axis=0), axis=-1, keepdims=True)
                lsev = lse_ref[qrows, :]
                lse_s = jnp.concatenate(
                    [jnp.sum(jnp.where(lane == h * ATTN_HEAD_DIM, lsev, 0.0), axis=-1, keepdims=True) for h in range(2)], axis=0)
                kb = kbuf[krows, :].astype(BF16)
                vb = vbuf[krows, :].astype(BF16)
                p = jnp.exp(_dot(qs, kb, 1, 1) * scale + bias - lse_s)
                ds = (p * (_dot(dosb, vb, 1, 1) - delta) * scale).astype(BF16)
                dq_ref[qrows, :] = _unstack_heads(_dot(ds, kb, 1, 0), head0)
                dk_acc[arows, :] += _dot(ds, qs, 0, 0)
                dv_acc[arows, :] += _dot(p.astype(BF16), dosb, 0, 0)
                return carry

            lax.fori_loop(0, nsub * d, per_block, 0, unroll=4)

        dk_ref[...] = dk_acc[pl.ds(0, tr), :]
        dv_ref[...] = dv_acc[pl.ds(0, tr), :]
        dk_acc[pl.ds(0, tr), :] = dk_acc[pl.ds(tr, tr), :]
        dv_acc[pl.ds(0, tr), :] = dv_acc[pl.ds(tr, tr), :]

    def cur(n):
        return jnp.minimum(n, ntiles - 1)

    def spec(col, prev):
        if prev:
            return pl.BlockSpec((sb, LANES), lambda hp, n: (jnp.maximum(cur(n) * nsub - 1, 0), col + hp))
        return pl.BlockSpec((tr, LANES), lambda hp, n: (cur(n), col + hp))

    row_spec = pl.BlockSpec((tr, LANES), lambda hp, n: (cur(n), hp))
    dq_out = pl.BlockSpec((tr, LANES), lambda hp, n: (cur(n), 2 * g + hp))
    kv_out = pl.BlockSpec((tr, LANES), lambda hp, n: (jnp.maximum(n - 1, 0), 2 * g + hp))
    shape = jax.ShapeDtypeStruct((rows, len(ATTN_PATTERNS) * 2 * LANES), F32)
    ins = [qkv, qkv, qkv, qkv, qkv, do, o, lse]
    in_specs = [spec(qc, False), spec(kc, False), spec(kc, True), spec(vc, False), spec(vc, True),
                row_spec, row_spec, row_spec]
    aliases = {}
    if prev is not None:
        aliases = {len(ins) + t: t for t in range(3)}
        ins = ins + list(prev)
        in_specs = in_specs + [ANY] * 3
    n_in = len(ins)

    def entry(*refs):
        body(*refs[:8], *refs[n_in:])

    return pl.pallas_call(
        entry, name=name, grid=(2, ntiles + 1),
        in_specs=in_specs,
        out_specs=[dq_out, kv_out, kv_out],
        out_shape=[shape, shape, shape],
        input_output_aliases=aliases,
        scratch_shapes=[pltpu.VMEM((sb + tr, LANES), F32), pltpu.VMEM((sb + tr, LANES), F32),
                        pltpu.VMEM((2 * tr, LANES), F32), pltpu.VMEM((2 * tr, LANES), F32)],
        compiler_params=_params(("parallel", "arbitrary")),
    )(*ins)


def _mem_probs(q, k):
    s = _dot(q.astype(BF16), k.astype(BF16), 1, 1) * (MEM_HEAD_DIM ** -0.5)
    e = jnp.exp(s - jnp.max(s, axis=-1, keepdims=True))
    return e / jnp.sum(e, axis=-1, keepdims=True)


def _mem_attn_fwd(mq, kv, *, tq, name):
    rows = mq.shape[0]

    def body(q_ref, k_ref, v_ref, o_ref):
        p = _mem_probs(q_ref[...], k_ref[...])
        o_ref[...] = _dot(p.astype(BF16), v_ref[...].astype(BF16), 1, 0)

    return pl.pallas_call(
        body, name=name, grid=(rows // tq, MEM_HEADS),
        in_specs=[pl.BlockSpec((tq, LANES), lambda i, h: (i, h)),
                  pl.BlockSpec((MEM_LEN, LANES), lambda i, h: (0, h)),
                  pl.BlockSpec((MEM_LEN, LANES), lambda i, h: (0, MEM_HEADS + h))],
        out_specs=pl.BlockSpec((tq, LANES), lambda i, h: (i, h)),
        out_shape=jax.ShapeDtypeStruct((rows, MEM_HEADS * LANES), F32),
        compiler_params=_params(("parallel", "parallel")),
    )(mq, kv, kv)


def _mem_attn_bwd(mq, kv, dmo, *, tq, name):
    rows = mq.shape[0]
    scale = MEM_HEAD_DIM ** -0.5

    def body(q_ref, k_ref, v_ref, do_ref, dq_ref, dk_ref, dv_ref):
        i = pl.program_id(1)
        qb = q_ref[...].astype(BF16)
        kb = k_ref[...].astype(BF16)
        vb = v_ref[...].astype(BF16)
        dob = do_ref[...].astype(BF16)
        p = _mem_probs(q_ref[...], k_ref[...])
        dp = _dot(dob, vb, 1, 1)
        ds = (p * (dp - jnp.sum(p * dp, axis=-1, keepdims=True)) * scale).astype(BF16)
        dq_ref[...] = _dot(ds, kb, 1, 0).astype(dq_ref.dtype)

        @pl.when(i == 0)
        def _():
            dk_ref[...] = jnp.zeros_like(dk_ref)
            dv_ref[...] = jnp.zeros_like(dv_ref)

        dk_ref[...] += _dot(ds, qb, 0, 0)
        dv_ref[...] += _dot(p.astype(BF16), dob, 0, 0)

    kv_out = pl.BlockSpec((MEM_LEN, LANES), lambda h, i: (0, h))
    kv_shape = jax.ShapeDtypeStruct((MEM_LEN, MEM_HEADS * LANES), F32)
    return pl.pallas_call(
        body, name=name, grid=(MEM_HEADS, rows // tq),
        in_specs=[pl.BlockSpec((tq, LANES), lambda h, i: (i, h)),
                  pl.BlockSpec((MEM_LEN, LANES), lambda h, i: (0, h)),
                  pl.BlockSpec((MEM_LEN, LANES), lambda h, i: (0, MEM_HEADS + h)),
                  pl.BlockSpec((tq, LANES), lambda h, i: (i, h))],
        out_specs=[pl.BlockSpec((tq, LANES), lambda h, i: (i, h)), kv_out, kv_out],
        out_shape=[jax.ShapeDtypeStruct((rows, MEM_HEADS * LANES), BF16), kv_shape, kv_shape],
        compiler_params=_params(("parallel", "arbitrary")),
    )(mq, kv, kv, dmo)


def _resident(shape):
    return pl.BlockSpec(shape, lambda i: (0, 0), pipeline_mode=pl.Buffered(1))


def _branch_merge_fwd(acts, wts, zg, b_gate, *, tm, name):
    rows = zg.shape[0]
    d = wts[0].shape[0]

    def body(s_ref, a_ref, m_ref, ws_ref, wa_ref, wm_ref, zg_ref, b_ref, o_ref):
        gt = _sigmoid(zg_ref[...] + b_ref[...])
        acc = None
        for k, (x_ref, w_ref) in enumerate(((s_ref, ws_ref), (a_ref, wa_ref), (m_ref, wm_ref))):
            term = gt[:, k * d:(k + 1) * d] * _dot(x_ref[...].astype(BF16), w_ref[...], 1, 1)
            acc = term if acc is None else acc + term
        o_ref[...] = acc.astype(BF16)

    return pl.pallas_call(
        body, name=name, grid=(rows // tm,),
        in_specs=[pl.BlockSpec((tm, x.shape[1]), lambda i: (i, 0)) for x in acts] + [_resident(w.shape) for w in wts]
        + [pl.BlockSpec((tm, 3 * d), lambda i: (i, 0)), pl.BlockSpec((1, 3 * d), lambda i: (0, 0))],
        out_specs=pl.BlockSpec((tm, d), lambda i: (i, 0)), out_shape=jax.ShapeDtypeStruct((rows, d), BF16),
        compiler_params=_params(("parallel",)),
    )(*acts, *wts, zg, b_gate)


def _branch_merge_bwd(dmerged, acts, wts, zg, b_gate, *, tm, name):
    rows = zg.shape[0]
    d = wts[0].shape[0]

    def body(dm_ref, s_ref, a_ref, m_ref, ws_ref, wa_ref, wm_ref, zg_ref, b_ref,
             ds_ref, da_ref, dmm_ref, dws_ref, dwa_ref, dwm_ref, dzg_ref, db_ref):
        i = pl.program_id(0)

        @pl.when(i == 0)
        def _():
            for r in (dws_ref, dwa_ref, dwm_ref, db_ref):
                r[...] = jnp.zeros_like(r)

        gt = _sigmoid(zg_ref[...] + b_ref[...])
        dm = dm_ref[...]
        groups = ((s_ref, ws_ref, ds_ref, dws_ref), (a_ref, wa_ref, da_ref, dwa_ref), (m_ref, wm_ref, dmm_ref, dwm_ref))
        for k, (x_ref, w_ref, dx_ref, dw_ref) in enumerate(groups):
            cs = pl.ds(k * d, d)
            gk = gt[:, k * d:(k + 1) * d]
            xb = x_ref[...].astype(BF16)
            br = _dot(xb, w_ref[...], 1, 1)
            dbr = (dm * gk).astype(BF16)
            dx_ref[...] = _dot(dbr, w_ref[...], 1, 0)
            dw_ref[...] += _dot(dbr, xb, 0, 0)
            dzg = dm * br * gk * (1.0 - gk)
            dzg_ref[:, cs] = dzg.astype(BF16)
            db_ref[:, cs] += jnp.sum(dzg, axis=0, keepdims=True)

    row = lambda w: pl.BlockSpec((tm, w), lambda i: (i, 0))
    whole = lambda shape: pl.BlockSpec(shape, lambda i: (0, 0))
    return pl.pallas_call(
        body, name=name, grid=(rows // tm,),
        in_specs=[row(d)] + [row(x.shape[1]) for x in acts] + [_resident(w.shape) for w in wts] + [row(3 * d), whole((1, 3 * d))],
        out_specs=[row(x.shape[1]) for x in acts] + [whole(w.shape) for w in wts] + [row(3 * d), whole((1, 3 * d))],
        out_shape=[jax.ShapeDtypeStruct(x.shape, F32) for x in acts] + [jax.ShapeDtypeStruct(w.shape, F32) for w in wts]
        + [jax.ShapeDtypeStruct((rows, 3 * d), BF16), jax.ShapeDtypeStruct((1, 3 * d), F32)],
        compiler_params=_params(("arbitrary",)),
    )(dmerged, *acts, *wts, zg, b_gate)


def _adamw(w, g, m, v, *, tr, name):
    rows, cols = w.shape
    assert rows % tr == 0, (name, rows, tr)

    def body(w_ref, g_ref, m_ref, v_ref, d_ref, nm_ref, nv_ref):
        gv = g_ref[...]
        m2 = ADAM_B1 * m_ref[...] + (1.0 - ADAM_B1) * gv
        v2 = ADAM_B2 * v_ref[...] + (1.0 - ADAM_B2) * (gv * gv)
        m_hat = m2 / (1.0 - ADAM_B1 ** ADAM_STEP)
        v_hat = v2 / (1.0 - ADAM_B2 ** ADAM_STEP)
        d_ref[...] = -ADAM_LR * (m_hat / (jnp.sqrt(v_hat) + ADAM_EPS) + ADAM_WD * w_ref[...])
        nm_ref[...] = m2
        nv_ref[...] = v2

    blk = pl.BlockSpec((tr, cols), lambda i: (i, 0))
    shape = jax.ShapeDtypeStruct((rows, cols), F32)
    return pl.pallas_call(
        body, name=name, grid=(rows // tr,), in_specs=[blk] * 4, out_specs=[blk] * 3,
        out_shape=[shape, shape, shape], compiler_params=_params(("parallel",)),
    )(w, g, m, v)


ANY = pl.BlockSpec(memory_space=pl.ANY)


def _position():
    return lax.axis_index("x"), lax.axis_index("y"), lax.axis_index("c")


def _other_chips(x, y):
    return ((1 - x, y), (x, 1 - y), (1 - x, 1 - y))


def _remote(src, dst, send_sem, recv_sem, dev):
    return pltpu.make_async_remote_copy(src_ref=src, dst_ref=dst, send_sem=send_sem, recv_sem=recv_sem,
                                        device_id=dev, device_id_type=MESH)


def _all_gather_weights(bufs):
    nb = len(bufs)

    def body(*refs):
        outs = refs[nb:2 * nb]
        send_sems, recv_sems = refs[2 * nb:]
        x, y, c = _position()
        chip = 2 * x + y
        sibling = (x, y, 1 - c)
        chips = _other_chips(x, y)

        def rows_of(i, owner, core):
            rs = bufs[i].shape[0] // N_CHIPS
            return pl.ds(pl.multiple_of(owner * rs + core * (rs // 2), 16), rs // 2)

        sends = []
        for i in range(nb):
            mine = outs[i].at[rows_of(i, chip, c)]
            for j, (px, py) in enumerate(chips):
                cp = _remote(mine, mine, send_sems.at[i, j], recv_sems.at[i, j], (px, py, c))
                cp.start()
                sends.append(cp)
        for i in range(nb):
            for j, (px, py) in enumerate(chips):
                landed = outs[i].at[rows_of(i, 2 * px + py, c)]
                _remote(landed, landed, send_sems.at[i, j], recv_sems.at[i, j], (px, py, c)).wait_recv()
                cp = _remote(landed, landed, send_sems.at[i, 3 + j], recv_sems.at[i, 3 + j], sibling)
                cp.start()
                sends.append(cp)
        for i in range(nb):
            for j, (px, py) in enumerate(chips):
                dst = outs[i].at[rows_of(i, 2 * px + py, 1 - c)]
                _remote(dst, dst, send_sems.at[i, 3 + j], recv_sems.at[i, 3 + j], sibling).wait_recv()
        for cp in sends:
            cp.wait_send()

    return pl.pallas_call(
        body, name="all_gather_weights", in_specs=[ANY] * nb, out_specs=[ANY] * nb,
        out_shape=[jax.ShapeDtypeStruct(b.shape, b.dtype) for b in bufs],
        input_output_aliases={i: i for i in range(nb)},
        scratch_shapes=[pltpu.SemaphoreType.DMA((nb, 6)), pltpu.SemaphoreType.DMA((nb, 6))],
    )(*bufs)


def _row_tile(rows):
    return max(t for t in range(16, min(rows, 512) + 1, 16) if rows % t == 0)


def _exchange_halves(grads):
    nb = len(grads)

    def body(*refs):
        ins, outs = refs[:nb], refs[nb:2 * nb]
        send_sems, recv_sems = refs[2 * nb:]
        x, y, c = _position()
        copies = []
        for i in range(nb):
            cp = _remote(ins[i].at[:, 1 - c], outs[i], send_sems.at[i], recv_sems.at[i], (x, y, 1 - c))
            cp.start()
            copies.append(cp)
        for cp in copies:
            cp.wait()

    return pl.pallas_call(
        body, name="grad_exchange_halves", in_specs=[ANY] * nb, out_specs=[ANY] * nb,
        out_shape=[jax.ShapeDtypeStruct((N_CHIPS, g.shape[2], g.shape[3]), F32) for g in grads],
        scratch_shapes=[pltpu.SemaphoreType.DMA((nb,)), pltpu.SemaphoreType.DMA((nb,))],
    )(*grads)


def _pair_sum(g4, got, c_arr, *, name):
    _, _, half, cols = g4.shape
    tr = _row_tile(half)

    def body(c_ref, g_ref, t_ref, p_ref, pb_ref):
        sm = g_ref[...] + t_ref[...]
        p_ref[...] = sm
        pb_ref[...] = sm.astype(BF16)

    blk = pl.BlockSpec((None, tr, cols), lambda j, i, c_ref: (j, i, 0))
    grid_spec = pltpu.PrefetchScalarGridSpec(
        num_scalar_prefetch=1, grid=(N_CHIPS, half // tr),
        in_specs=[pl.BlockSpec((None, None, tr, cols), lambda j, i, c_ref: (j, c_ref[0], i, 0)), blk],
        out_specs=[blk, blk])
    return pl.pallas_call(
        body, name=name, grid_spec=grid_spec,
        out_shape=[jax.ShapeDtypeStruct((N_CHIPS, half, cols), F32), jax.ShapeDtypeStruct((N_CHIPS, half, cols), BF16)],
        compiler_params=_params(("parallel", "parallel")),
    )(c_arr, g4, got)


def _scatter_to_owners(parts):
    nb = len(parts)

    def body(*refs):
        ins, outs = refs[:nb], refs[nb:2 * nb]
        send_sems, recv_sems = refs[2 * nb:]
        x, y, c = _position()
        copies = []
        for i in range(nb):
            for j, (px, py) in enumerate(_other_chips(x, y)):
                cp = _remote(ins[i].at[2 * px + py], outs[i].at[j], send_sems.at[i, j], recv_sems.at[i, j], (px, py, c))
                cp.start()
                copies.append(cp)
        for cp in copies:
            cp.wait()

    return pl.pallas_call(
        body, name="grad_scatter_to_owners", in_specs=[ANY] * nb, out_specs=[ANY] * nb,
        out_shape=[jax.ShapeDtypeStruct((3,) + p.shape[1:], p.dtype) for p in parts],
        scratch_shapes=[pltpu.SemaphoreType.DMA((nb, 3)), pltpu.SemaphoreType.DMA((nb, 3))],
    )(*parts)


def _owner_sum(p, got, chip_arr, c_arr, *, replicated, name):
    _, half, cols = p.shape
    tr = _row_tile(half)

    def body(chip_ref, c_ref, p_ref, r_ref, o_ref):
        o_ref[...] = ((p_ref[...] + r_ref[0].astype(F32)) + r_ref[1].astype(F32)) + r_ref[2].astype(F32)

    if replicated:
        out_spec = pl.BlockSpec((None, None, tr, cols), lambda i, chip_ref, c_ref: (chip_ref[0], c_ref[0], i, 0))
        out_shape = jax.ShapeDtypeStruct((N_CHIPS, 2, half, cols), F32)
    else:
        out_spec = pl.BlockSpec((None, tr, cols), lambda i, chip_ref, c_ref: (c_ref[0], i, 0))
        out_shape = jax.ShapeDtypeStruct((2, half, cols), F32)
    grid_spec = pltpu.PrefetchScalarGridSpec(
        num_scalar_prefetch=2, grid=(half // tr,),
        in_specs=[pl.BlockSpec((None, tr, cols), lambda i, chip_ref, c_ref: (chip_ref[0], i, 0)),
                  pl.BlockSpec((3, tr, cols), lambda i, chip_ref, c_ref: (0, i, 0))],
        out_specs=out_spec)
    return pl.pallas_call(
        body, name=name, grid_spec=grid_spec, out_shape=out_shape,
        compiler_params=_params(("parallel",)),
    )(chip_arr, c_arr, p, got)


def _share_reduced(bufs):
    nb = len(bufs) - 1

    def body(*refs):
        outs = refs[nb + 1:2 * nb + 2]
        send_sems, recv_sems = refs[2 * nb + 2:]
        x, y, c = _position()
        chip = 2 * x + y
        sends = []
        for i in range(nb):
            cp = _remote(outs[i].at[c], outs[i].at[c], send_sems.at[i], recv_sems.at[i], (x, y, 1 - c))
            cp.start()
            sends.append(cp)
        small = outs[nb]
        peers = [(fx, fy, fc) for fx in (0, 1) for fy in (0, 1) for fc in (0, 1) if fx + fy + fc > 0]
        for k, (fx, fy, fc) in enumerate(peers):
            dev = (x ^ fx, y ^ fy, c ^ fc)
            cp = _remote(small.at[chip, c], small.at[chip, c], send_sems.at[nb + k], recv_sems.at[nb + k], dev)
            cp.start()
            sends.append(cp)
        for i in range(nb):
            dst = outs[i].at[1 - c]
            _remote(dst, dst, send_sems.at[i], recv_sems.at[i], (x, y, 1 - c)).wait_recv()
        for k, (fx, fy, fc) in enumerate(peers):
            dst = small.at[2 * (x ^ fx) + (y ^ fy), c ^ fc]
            _remote(dst, dst, send_sems.at[nb + k], recv_sems.at[nb + k], (x ^ fx, y ^ fy, c ^ fc)).wait_recv()
        for cp in sends:
            cp.wait_send()

    n_all = nb + 1
    return pl.pallas_call(
        body, name="grad_share_reduced", in_specs=[ANY] * n_all, out_specs=[ANY] * n_all,
        out_shape=[jax.ShapeDtypeStruct(b.shape, b.dtype) for b in bufs],
        input_output_aliases={i: i for i in range(n_all)},
        scratch_shapes=[pltpu.SemaphoreType.DMA((nb + 7,)), pltpu.SemaphoreType.DMA((nb + 7,))],
    )(*bufs)


def _pack_small(vals):
    flat = jnp.concatenate([vals[name].reshape(-1) for name, _ in SMALL])
    return jnp.pad(flat, (0, N_CHIPS * SMALL_ROWS * 1024 - SMALL_ELEMS)).reshape(N_CHIPS * SMALL_ROWS, 1024)


def _unpack_small(buf):
    flat = buf.reshape(-1)
    out, off = {}, 0
    for name, shape in SMALL:
        n = int(np.prod(shape))
        out[name] = flat[off:off + n].reshape(shape)
        off += n
    return out


def _device_step(x, mem, tgt, w, p):
    rows = x.shape[0]
    g1, gm, g2 = p["norm1_g"], p["mem_norm_g"], p["norm2_g"]
    gf = p["final_g"].reshape(1, D_MODEL)
    ssm_args = (p["ssm_lambda_re"][0], p["ssm_lambda_im"][0], p["ssm_log_dt"][0], p["ssm_b_re"][0],
                p["ssm_b_im"][0], p["ssm_c_re"][0], p["ssm_c_im"][0])
    (a_lay, b_blk, c_blk), ssm_vjp = jax.vjp(_ssm_matrices, *ssm_args)
    a_conj = a_lay * _to_scan_layout(jnp.stack([jnp.ones((N_STATES,), F32), -jnp.ones((N_STATES,), F32)]))[None, :]
    dd = p["ssm_d"].reshape(1, SSM_WIDTH)
    win_t = w["w_in"]
    mm = _matmul

    n1 = _rmsnorm_fwd(x, g1, tm=512, name="norm1")
    u = mm(n1, win_t, m=rows, n=512, k=1024, tb=True, tm=2048, tn=512, tk=1024, out_dtypes=(F32,), name="in_u")
    qkv = mm(n1, win_t, m=rows, n=2304, k=1024, tb=True, tm=2048, tn=256, tk=1024, b_off=(OFF_QKV // 256, 0),
             out_dtypes=(F32,), name="in_qkv")
    mq = mm(n1, win_t, m=rows, n=512, k=1024, tb=True, tm=2048, tn=256, tk=1024, b_off=(OFF_MQ // 256, 0),
            out_dtypes=(F32,), name="in_mq")
    zg = mm(n1, win_t, m=rows, n=3072, k=1024, tb=True, tm=2048, tn=256, tk=1024, b_off=(OFF_ZG // 256, 0),
            out_dtypes=(F32,), name="in_zg")

    u_i = _interleave(u)
    ends = _ssm_ends(a_lay, u_i, b_blk, transpose=False, reverse=False, tt=512, name="ssm_fwd_ends")
    s, ys_i, s_entry = _ssm_fwd(a_lay, u_i, b_blk, c_blk, ends, tt=512, name="ssm_fwd")
    ys = _deinterleave(ys_i)
    y0, tglu, y2 = _glu_fwd(ys, u, dd, w["w_glu"], p["b_glu"], tm=512, name="glu_fwd")

    outs, lses = [], []
    for g, (_, d) in enumerate(ATTN_PATTERNS):
        o_g, lse_g = _attn_fwd(qkv, g, d, name=f"attn_fwd_{g}")
        outs.append(o_g)
        lses.append(lse_g)
    o, lse = _attn_merge(outs, lses, tm=1024, name="attn_merge")

    mn = _rmsnorm_fwd(mem, gm, tm=MEM_LEN, name="mem_norm")
    kv = mm(mn, w["w_mem_kv"], m=MEM_LEN, n=1024, k=1024, tm=MEM_LEN, tn=1024, tk=1024, out_dtypes=(F32,), name="mem_kv")
    mo = _mem_attn_fwd(mq, kv, tq=1024, name="mem_attn_fwd")

    branch_acts = (y2, o, mo)
    branch_wts = (w["w_ssm_br"], w["w_attn_br"], w["w_mem_br"])
    merged = _branch_merge_fwd(branch_acts, branch_wts, zg, p["b_gate"], tm=256, name="branch_merge_fwd")
    add = lambda acc, r: (acc + r,)
    h1 = mm(merged, w["w_o"], m=rows, n=1024, k=1024, tm=1024, tn=1024, tk=1024, out_dtypes=(F32,),
            aux=((x, "mn"),), epilogue=add, name="out_proj")
    n2 = _rmsnorm_fwd(h1, g2, tm=512, name="norm2")
    relu2 = lambda acc: (jnp.square(jnp.maximum(acc, 0.0)),)
    act = mm(n2, w["w_up"], m=rows, n=D_FF, k=1024, tb=True, tm=1024, tn=1024, tk=1024, out_dtypes=(BF16,),
             epilogue=relu2, name="mlp_up")
    h2 = mm(act, w["w_down"], m=rows, n=1024, k=D_FF, tm=1024, tn=1024, tk=1024, out_dtypes=(F32,),
            aux=((h1, "mn"),), epilogue=add, name="mlp_down")
    dh2, loss, d_gf = _loss_head(h2, tgt, gf, tm=512, name="loss_head")

    gb = {}
    gs = {"final_g": d_gf.reshape(D_MODEL)}
    drelu2 = lambda acc, actv: (acc * (2.0 * jnp.sqrt(actv.astype(F32))),)
    dup = mm(dh2, w["w_down"], m=rows, n=D_FF, k=1024, tb=True, tm=1024, tn=1024, tk=1024, out_dtypes=(BF16,),
             aux=((act, "mn"),), epilogue=drelu2, name="d_act")
    gb["w_down"] = mm(act, dh2, m=D_FF, n=1024, k=rows, ta=True, tm=1024, tn=1024, tk=1024, out_dtypes=(F32,), name="dw_down")
    dn2 = mm(dup, w["w_up"], m=rows, n=1024, k=D_FF, tm=1024, tn=1024, tk=1024, out_dtypes=(F32,), name="d_n2")
    gb["w_up"] = mm(dup, n2, m=D_FF, n=1024, k=rows, ta=True, tm=1024, tn=1024, tk=1024, out_dtypes=(F32,), name="dw_up")
    dh1, gs["norm2_g"] = _rmsnorm_bwd(h1, g2, dn2, dh2, tm=512, name="norm2_bwd")
    dmerged = mm(dh1, w["w_o"], m=rows, n=1024, k=1024, tb=True, tm=1024, tn=1024, tk=1024, out_dtypes=(F32,), name="d_merged")
    gb["w_o"] = mm(merged, dh1, m=1024, n=1024, k=rows, ta=True, tm=1024, tn=1024, tk=1024, out_dtypes=(F32,), name="dw_o")
    (dy2, do, dmo, gb["w_ssm_br"], gb["w_attn_br"], gb["w_mem_br"], dzg, gs["b_gate"]) = _branch_merge_bwd(
        dmerged, branch_acts, branch_wts, zg, p["b_gate"], tm=256, name="branch_merge_bwd")

    dy0, dt, y1, gs["b_glu"], d_dd = _glu_bwd(dy2, y0, tglu, u, w["w_glu"], tm=512, name="glu_bwd")
    gs["ssm_d"] = d_dd.reshape(1, SSM_GROUPS, SSM_GROUP_SIZE)
    gb["w_glu"] = mm(y1, dt, m=512, n=512, k=rows, ta=True, tm=512, tn=512, tk=1024, out_dtypes=(F32,), name="dw_glu")
    dy0_i = _interleave(dy0)
    lam_ends = _ssm_ends(a_conj, dy0_i, c_blk, transpose=True, reverse=True, tt=512, name="ssm_bwd_ends")
    du_i, d_b_blk, d_c_blk, d_a_lay = _ssm_bwd(a_conj, dy0_i, u_i, s, s_entry, b_blk, c_blk, dd, lam_ends, tt=512,
                                                name="ssm_bwd")
    du = _deinterleave(du_i)
    d_ssm = ssm_vjp((d_a_lay, d_b_blk, d_c_blk))
    for name, val in zip(("ssm_lambda_re", "ssm_lambda_im", "ssm_log_dt", "ssm_b_re", "ssm_b_im", "ssm_c_re", "ssm_c_im"), d_ssm):
        gs[name] = val[None]

    dqkv = None
    for g, (_, d) in enumerate(ATTN_PATTERNS):
        dqkv = _attn_bwd(qkv, do, o, lse, g, d, dqkv, name=f"attn_bwd_{g}")

    dmq, dmk, dmv = _mem_attn_bwd(mq, kv, dmo, tq=1024, name="mem_attn_bwd")
    dkv = jnp.concatenate([dmk, dmv], axis=1)
    gb["w_mem_kv"] = mm(mn, dkv, m=1024, n=1024, k=MEM_LEN, ta=True, tm=1024, tn=1024, tk=MEM_LEN, out_dtypes=(F32,), name="dw_mem_kv")
    dmn = mm(dkv, w["w_mem_kv"], m=MEM_LEN, n=1024, k=1024, tb=True, tm=MEM_LEN, tn=1024, tk=1024, out_dtypes=(F32,), name="d_mn")
    _, gs["mem_norm_g"] = _rmsnorm_bwd(mem, gm, dmn, None, tm=MEM_LEN, name="mem_norm_bwd")

    pieces = ((du, OFF_U, "u"), (dqkv[0], OFF_QKV, "q"), (dqkv[1], OFF_QKV + 768, "k"), (dqkv[2], OFF_QKV + 1536, "v"),
              (dmq, OFF_MQ, "mq"), (dzg, OFF_ZG, "zg"))
    dn = _sum_matmul([piece for piece, _, _ in pieces], win_t, [off for _, off, _ in pieces], tm=512, name="d_n1")
    dw_rows = []
    for piece, off, tag in pieces:
        width = piece.shape[1]
        tmw = 1024 if width % 1024 == 0 else (768 if width == 768 else 512)
        dw_rows.append(mm(piece, n1, m=width, n=1024, k=rows, ta=True, tm=tmw, tn=1024, tk=1024, out_dtypes=(F32,),
                          name="dw_in_" + tag))
    gb["w_in"] = jnp.concatenate(dw_rows, axis=0)
    dx, gs["norm1_g"] = _rmsnorm_bwd(x, g1, dn, dh1, tm=512, name="norm1_bwd")
    return loss, dx, gb, gs


def kernel(x, mem, norm1_g, mem_norm_g, w_in, b_gate, ssm_lambda_re, ssm_lambda_im, ssm_log_dt, ssm_b_re, ssm_b_im, ssm_c_re, ssm_c_im, ssm_d, w_glu, b_glu, w_ssm_br, w_attn_br, w_mem_kv, w_mem_br, w_o, norm2_g, w_up, w_down, final_g, loss_target, m_norm1_g, m_mem_norm_g, m_w_in, m_b_gate, m_ssm_lambda_re, m_ssm_lambda_im, m_ssm_log_dt, m_ssm_b_re, m_ssm_b_im, m_ssm_c_re, m_ssm_c_im, m_ssm_d, m_w_glu, m_b_glu, m_w_ssm_br, m_w_attn_br, m_w_mem_kv, m_w_mem_br, m_w_o, m_norm2_g, m_w_up, m_w_down, m_final_g, v_norm1_g, v_mem_norm_g, v_w_in, v_b_gate, v_ssm_lambda_re, v_ssm_lambda_im, v_ssm_log_dt, v_ssm_b_re, v_ssm_b_im, v_ssm_c_re, v_ssm_c_im, v_ssm_d, v_w_glu, v_b_glu, v_w_ssm_br, v_w_attn_br, v_w_mem_kv, v_w_mem_br, v_w_o, v_norm2_g, v_w_up, v_w_down, v_final_g):
    env = dict(locals())
    weights = {n: env[n] for n in WEIGHT_ORDER}
    moms = {n: env["m_" + n] for n in WEIGHT_ORDER}
    vels = {n: env["v_" + n] for n in WEIGHT_ORDER}
    def shard2d(a):
        return a.reshape(a.shape[-2], a.shape[-1])

    chip = 2 * lax.axis_index("x") + lax.axis_index("y")
    wire = [shard2d(weights[n]).astype(BF16) for n, _, _ in BIG]
    wire = [s.T if tr else s for s, (_, tr, _) in zip(wire, BIG)]
    wire = [lax.dynamic_update_slice(lax.empty((N_CHIPS * s.shape[0], s.shape[1]), BF16), s, (chip * s.shape[0], 0))
            for s in wire]
    w_full = dict(zip([n for n, _, _ in BIG], _all_gather_weights(wire)))
    small = {n: weights[n] for n, _ in SMALL}

    loss, dx, gb, gs = _device_step(x[0], mem[0], loss_target[0], w_full, small)

    c_arr = lax.axis_index("c").astype(jnp.int32).reshape(1)
    chip_arr = chip.astype(jnp.int32).reshape(1)
    names = [n for n, _, _ in BIG] + ["small"]
    full = [gb[n] for n, _, _ in BIG] + [_pack_small(gs)]
    full = [g.reshape(N_CHIPS, 2, g.shape[0] // (2 * N_CHIPS), g.shape[1]) for g in full]
    from_sibling = _exchange_halves(full)
    pairs = [_pair_sum(g, t, c_arr, name="grad_pair_sum_" + n) for g, t, n in zip(full, from_sibling, names)]
    landed = _scatter_to_owners([pb for _, pb in pairs])
    totals = [_owner_sum(pf, r, chip_arr, c_arr, replicated=(n == "small"), name="grad_owner_sum_" + n)
              for (pf, _), r, n in zip(pairs, landed, names)]
    *shards, small_grad = _share_reduced(totals)
    grads = {}
    for (n, tr, _), sh in zip(BIG, shards):
        sh = sh.reshape(2 * sh.shape[1], sh.shape[2])
        grads[n] = sh.T if tr else sh
    small_grad = small_grad.reshape(N_CHIPS * SMALL_ROWS, 1024)
    grads_small = _unpack_small(small_grad)

    delta, new_m, new_v = {}, {}, {}
    for n, _, _ in BIG:
        shape = weights[n].shape
        dn_, nm_, nv_ = _adamw(shard2d(weights[n]), grads[n], shard2d(moms[n]), shard2d(vels[n]),
                               tr=min(shape[-2], 256), name="adamw_" + n)
        delta[n], new_m[n], new_v[n] = dn_.reshape(shape), nm_.reshape(shape), nv_.reshape(shape)
        grads[n] = grads[n].reshape(shape)
    ds_, ms_, vs_ = _adamw(_pack_small(small), small_grad,
                           _pack_small({n: moms[n] for n, _ in SMALL}), _pack_small({n: vels[n] for n, _ in SMALL}),
                           tr=N_CHIPS * SMALL_ROWS, name="adamw_small")
    for dst, buf in ((delta, ds_), (new_m, ms_), (new_v, vs_)):
        dst.update(_unpack_small(buf))
    grads.update(grads_small)

    total_loss = lax.psum(loss[0, 0], ("x", "y", "c"))
    return (total_loss, dx[None], *[grads[n] for n in WEIGHT_ORDER], *[delta[n] for n in WEIGHT_ORDER],
            *[new_m[n] for n in WEIGHT_ORDER], *[new_v[n] for n in WEIGHT_ORDER])
```

```python
import functools
import math

import numpy as np
import jax
import jax.numpy as jnp
from jax import lax
from jax.experimental import pallas as pl
from jax.experimental.pallas import tpu as pltpu

F32 = jnp.float32
BF16 = jnp.bfloat16

D_MODEL = 1024
SSM_GROUPS = 32
SSM_GROUP_SIZE = 16
SSM_STATE = 64
SSM_WIDTH = 512
N_STATES = SSM_GROUPS * SSM_STATE
SCAN_CB = 1024
ATTN_PATTERNS = ((128, 1), (512, 4), (2048, 16))
ATTN_HEAD_DIM = 64
ATTN_Q = 128
MEM_LEN = 256
MEM_HEAD_DIM = 128
MEM_HEADS = 4
D_FF = 4096
OFF_U, OFF_QKV, OFF_MQ, OFF_ZG = 0, 512, 2816, 3328
IN_WIDTH = 6400
RMS_EPS = 1e-6
NEG_INF = -1e30
ADAM_LR, ADAM_B1, ADAM_B2, ADAM_EPS, ADAM_WD, ADAM_STEP = 0.001, 0.9, 0.999, 1e-08, 0.01, 10

VMEM_LIMIT_BYTES = 48 * 1024 * 1024
LANES = 128
MXU_WIDTH = 256
MESH = pl.DeviceIdType.MESH
N_CHIPS = 4

SCAN_SEGS = 8
SCAN_GROUPS = SCAN_CB // SSM_STATE

BIG = (("w_in", True, (6400, 1024)), ("w_glu", False, (512, 512)), ("w_ssm_br", True, (1024, 512)),
       ("w_attn_br", True, (1024, 256)), ("w_mem_kv", False, (1024, 1024)), ("w_mem_br", True, (1024, 512)),
       ("w_o", False, (1024, 1024)), ("w_up", True, (4096, 1024)), ("w_down", False, (4096, 1024)))
SMALL = (("norm1_g", (1, 1024)), ("mem_norm_g", (1, 1024)), ("b_gate", (1, 3072)),
         ("ssm_lambda_re", (1, 32, 64)), ("ssm_lambda_im", (1, 32, 64)), ("ssm_log_dt", (1, 32)),
         ("ssm_b_re", (1, 32, 64, 16)), ("ssm_b_im", (1, 32, 64, 16)), ("ssm_c_re", (1, 32, 16, 64)),
         ("ssm_c_im", (1, 32, 16, 64)), ("ssm_d", (1, 32, 16)), ("b_glu", (1, 512)),
         ("norm2_g", (1, 1024)), ("final_g", (1024,)))
WEIGHT_ORDER = ("norm1_g", "mem_norm_g", "w_in", "b_gate", "ssm_lambda_re", "ssm_lambda_im", "ssm_log_dt",
                "ssm_b_re", "ssm_b_im", "ssm_c_re", "ssm_c_im", "ssm_d", "w_glu", "b_glu", "w_ssm_br",
                "w_attn_br", "w_mem_kv", "w_mem_br", "w_o", "norm2_g", "w_up", "w_down", "final_g")
SMALL_ELEMS = sum(int(np.prod(s)) for _, s in SMALL)
SMALL_ROWS = 64


def _params(sem):
    return pltpu.CompilerParams(dimension_semantics=sem, vmem_limit_bytes=VMEM_LIMIT_BYTES)


def _sigmoid(v):
    return 1.0 / (1.0 + jnp.exp(-v))


_GELU_C = math.sqrt(2.0 / math.pi)


def _gelu(v):
    return 0.5 * v * (1.0 + jnp.tanh(_GELU_C * (v + 0.044715 * v * v * v)))


def _gelu_grad(v):
    th = jnp.tanh(_GELU_C * (v + 0.044715 * v * v * v))
    return 0.5 * (1.0 + th) + 0.5 * v * (1.0 - th * th) * _GELU_C * (1.0 + 3.0 * 0.044715 * v * v)


def _dot(a, b, ca, cb):
    return lax.dot_general(a, b, (((ca,), (cb,)), ((), ())), preferred_element_type=F32)


class _Exchange:
    def __init__(self, ins, outs, aliases, sems, start, finish):
        self.ins, self.outs, self.aliases, self.sems, self.start, self.finish = ins, outs, aliases, sems, start, finish


def _matmul(a, b, *, m, n, k, ta=False, tb=False, tm, tn, tk, out_dtypes, name,
            a_off=(0, 0), b_off=(0, 0), aux=(), epilogue=None, carry=None):
    assert m % tm == 0 and n % tn == 0 and k % tk == 0, (name, m, n, k, tm, tn, tk)
    nk = k // tk
    n_aux = len(aux)
    n_out = len(out_dtypes)
    ar, ac = a_off
    br, bc = b_off
    if ta:
        a_spec = pl.BlockSpec((tk, tm), lambda i, j, kk: (kk + ar, i + ac))
    else:
        a_spec = pl.BlockSpec((tm, tk), lambda i, j, kk: (i + ar, kk + ac))
    if tb:
        b_spec = pl.BlockSpec((tn, tk), lambda i, j, kk: (j + br, kk + bc))
    else:
        b_spec = pl.BlockSpec((tk, tn), lambda i, j, kk: (kk + br, j + bc))
    aux_specs = []
    for _, kind in aux:
        if kind == "mn":
            aux_specs.append(pl.BlockSpec((tm, tn), lambda i, j, kk: (i, j)))
        else:
            aux_specs.append(pl.BlockSpec((1, tn), lambda i, j, kk: (0, j)))
    ca = 0 if ta else 1
    cb = 1 if tb else 0

    chunk = MXU_WIDTH if tn % MXU_WIDTH == 0 else tn
    cols = [pl.ds(c0, chunk) for c0 in range(0, tn, chunk)]

    def finish(acc, aux_refs, out_refs, cs):
        auxv = [r[:, cs] for r in aux_refs]
        outs = (acc,) if epilogue is None else epilogue(acc, *auxv)
        for o_ref, o in zip(out_refs, outs):
            o_ref[:, cs] = o.astype(o_ref.dtype)

    def body(a_ref, b_ref, *rest):
        aux_refs = rest[:n_aux]
        out_refs = rest[n_aux:n_aux + n_out]

        def products():
            av = a_ref[...].astype(BF16)
            for cs in cols:
                bv = (b_ref[cs, :] if tb else b_ref[:, cs]).astype(BF16)
                yield cs, _dot(av, bv, ca, cb)

        if nk == 1:
            for cs, prod in products():
                finish(prod, aux_refs, out_refs, cs)
            return
        acc_ref = rest[n_aux + n_out]
        kk = pl.program_id(2)

        @pl.when(kk == 0)
        def _():
            for cs, prod in products():
                acc_ref[:, cs] = prod

        @pl.when(jnp.logical_and(kk > 0, kk < nk - 1))
        def _():
            for cs, prod in products():
                acc_ref[:, cs] += prod

        @pl.when(kk == nk - 1)
        def _():
            for cs, prod in products():
                finish(acc_ref[:, cs] + prod, aux_refs, out_refs, cs)

    grid = (m // tm, n // tn, nk)
    in_specs = [a_spec, b_spec] + aux_specs
    out_specs = [pl.BlockSpec((tm, tn), lambda i, j, kk: (i, j)) for _ in range(n_out)]
    out_shape = [jax.ShapeDtypeStruct((m, n), dt) for dt in out_dtypes]
    scratch = [pltpu.VMEM((tm, tn), F32)] if nk > 1 else []
    operands = [a, b] + [x for x, _ in aux]
    if carry is None:
        res = pl.pallas_call(
            body, name=name, grid=grid, in_specs=in_specs, out_specs=out_specs, out_shape=out_shape,
            scratch_shapes=scratch, compiler_params=_params(("parallel", "parallel", "arbitrary")),
        )(*operands)
        return res[0] if n_out == 1 else tuple(res)

    n_in, n_cin, n_cout, n_scr = len(operands), len(carry.ins), len(carry.outs), len(scratch)

    def hosted(*refs):
        main_in, c_in = refs[:n_in], refs[n_in:n_in + n_cin]
        main_out = refs[n_in + n_cin:n_in + n_cin + n_out]
        c_out = refs[n_in + n_cin + n_out:n_in + n_cin + n_out + n_cout]
        rest = refs[n_in + n_cin + n_out + n_cout:]
        ids = [pl.program_id(t) for t in range(3)]
        first = functools.reduce(jnp.logical_and, [i == 0 for i in ids])
        last = functools.reduce(jnp.logical_and, [i == g - 1 for i, g in zip(ids, grid)])

        @pl.when(first)
        def _():
            carry.start(c_in, c_out, *rest[n_scr:])

        body(*main_in, *main_out, *rest[:n_scr])

        @pl.when(last)
        def _():
            carry.finish(c_in, c_out, *rest[n_scr:])

    res = pl.pallas_call(
        hosted, name=name, grid=grid,
        in_specs=in_specs + [ANY] * n_cin, out_specs=out_specs + [ANY] * n_cout,
        out_shape=out_shape + list(carry.outs),
        input_output_aliases={n_in + i: n_out + o for i, o in carry.aliases.items()},
        scratch_shapes=scratch + [pltpu.SemaphoreType.DMA(s) for s in carry.sems],
        compiler_params=_params(("arbitrary", "arbitrary", "arbitrary")),
    )(*operands, *carry.ins)
    main = res[0] if n_out == 1 else tuple(res[:n_out])
    return main, list(res[n_out:])


def _sum_matmul(pieces, b, offs, *, tm, name):
    m = pieces[0].shape[0]
    n = b.shape[1]
    npieces = len(pieces)

    def body(*refs):
        b_ref, o_ref = refs[npieces], refs[npieces + 1]
        acc = None
        for p_ref, off in zip(refs[:npieces], offs):
            part = _dot(p_ref[...].astype(BF16), b_ref[pl.ds(off, p_ref.shape[1]), :], 1, 0)
            acc = part if acc is None else acc + part
        o_ref[...] = acc

    return pl.pallas_call(
        body, name=name, grid=(m // tm,),
        in_specs=[pl.BlockSpec((tm, p.shape[1]), lambda i: (i, 0)) for p in pieces]
        + [pl.BlockSpec(b.shape, lambda i: (0, 0), pipeline_mode=pl.Buffered(1))],
        out_specs=pl.BlockSpec((tm, n), lambda i: (i, 0)),
        out_shape=jax.ShapeDtypeStruct((m, n), F32),
        compiler_params=_params(("parallel",)),
    )(*pieces, b)


def _rmsnorm_fwd(x, g, *, tm, name):
    rows, d = x.shape

    def body(x_ref, g_ref, o_ref):
        xv = x_ref[...]
        r = lax.rsqrt(jnp.mean(xv * xv, axis=-1, keepdims=True) + RMS_EPS)
        o_ref[...] = (xv * r * g_ref[...]).astype(o_ref.dtype)

    return pl.pallas_call(
        body, name=name, grid=(rows // tm,),
        in_specs=[pl.BlockSpec((tm, d), lambda i: (i, 0)), pl.BlockSpec((1, d), lambda i: (0, 0))],
        out_specs=pl.BlockSpec((tm, d), lambda i: (i, 0)),
        out_shape=jax.ShapeDtypeStruct((rows, d), BF16),
        compiler_params=_params(("parallel",)),
    )(x, g)


def _rmsnorm_bwd(x, g, dy, res, *, tm, name):
    rows, d = x.shape
    has_res = res is not None

    def body(x_ref, g_ref, dy_ref, *rest):
        if has_res:
            res_ref, dx_ref, dg_ref = rest
        else:
            dx_ref, dg_ref = rest
        i = pl.program_id(0)
        xv = x_ref[...]
        r = lax.rsqrt(jnp.mean(xv * xv, axis=-1, keepdims=True) + RMS_EPS)
        xhat = xv * r
        dyv = dy_ref[...]
        dyg = dyv * g_ref[...]
        dx = r * (dyg - xhat * jnp.mean(dyg * xhat, axis=-1, keepdims=True))
        if has_res:
            dx = dx + res_ref[...]
        dx_ref[...] = dx

        @pl.when(i == 0)
        def _():
            dg_ref[...] = jnp.zeros_like(dg_ref)

        dg_ref[...] += jnp.sum(dyv * xhat, axis=0, keepdims=True)

    row_spec = pl.BlockSpec((tm, d), lambda i: (i, 0))
    vec_spec = pl.BlockSpec((1, d), lambda i: (0, 0))
    ins = [x, g, dy] + ([res] if has_res else [])
    return pl.pallas_call(
        body, name=name, grid=(rows // tm,),
        in_specs=[row_spec, vec_spec, row_spec] + ([row_spec] if has_res else []),
        out_specs=[row_spec, vec_spec],
        out_shape=[jax.ShapeDtypeStruct((rows, d), F32), jax.ShapeDtypeStruct((1, d), F32)],
        compiler_params=_params(("arbitrary",)),
    )(*ins)


def _loss_head(h, tgt, g, *, tm, name):
    rows, d = h.shape
    nsteps = rows // tm

    def body(h_ref, t_ref, g_ref, dh_ref, loss_ref, dg_ref, sq_ref):
        i = pl.program_id(0)
        xv = h_ref[...]
        gv = g_ref[...]
        r = lax.rsqrt(jnp.mean(xv * xv, axis=-1, keepdims=True) + RMS_EPS)
        xhat = xv * r
        err = xhat * gv - t_ref[...]
        dyv = err * (1.0 / d)
        dyg = dyv * gv
        dh_ref[...] = r * (dyg - xhat * jnp.mean(dyg * xhat, axis=-1, keepdims=True))

        @pl.when(i == 0)
        def _():
            dg_ref[...] = jnp.zeros_like(dg_ref)
            sq_ref[...] = jnp.zeros_like(sq_ref)

        dg_ref[...] += jnp.sum(dyv * xhat, axis=0, keepdims=True)
        sq_ref[...] += jnp.sum(err * err, axis=0, keepdims=True)

        @pl.when(i == nsteps - 1)
        def _():
            tot = jnp.sum(sq_ref[...], axis=-1, keepdims=True) * (0.5 / d)
            loss_ref[...] = jnp.broadcast_to(tot, loss_ref.shape)

    row_spec = pl.BlockSpec((tm, d), lambda i: (i, 0))
    vec_spec = pl.BlockSpec((1, d), lambda i: (0, 0))
    return pl.pallas_call(
        body, name=name, grid=(nsteps,),
        in_specs=[row_spec, row_spec, vec_spec],
        out_specs=[row_spec, pl.BlockSpec((1, LANES), lambda i: (0, 0)), vec_spec],
        out_shape=[jax.ShapeDtypeStruct((rows, d), F32), jax.ShapeDtypeStruct((1, LANES), F32),
                   jax.ShapeDtypeStruct((1, d), F32)],
        scratch_shapes=[pltpu.VMEM((1, d), F32)],
        compiler_params=_params(("arbitrary",)),
    )(h, tgt, g)


def _to_scan_layout(v):
    lead = v.shape[:-2]
    v = v.reshape(lead + (2, N_STATES // SCAN_CB, SCAN_CB))
    v = jnp.swapaxes(v, -3, -2)
    return v.reshape(lead + (2 * N_STATES,))


def _ssm_matrices(lam_re, lam_im, log_dt, b_re, b_im, c_re, c_im):
    dt = jnp.exp(log_dt)[:, None]
    mag = jnp.exp(lam_re * dt)
    a_re, a_im = mag * jnp.cos(lam_im * dt), mag * jnp.sin(lam_im * dt)
    nr, ni = a_re - 1.0, a_im
    den = lam_re * lam_re + lam_im * lam_im
    coef_re = (nr * lam_re + ni * lam_im) / den
    coef_im = (ni * lam_re - nr * lam_im) / den
    bb_re = coef_re[..., None] * b_re - coef_im[..., None] * b_im
    bb_im = coef_re[..., None] * b_im + coef_im[..., None] * b_re
    a_lay = _to_scan_layout(jnp.stack([a_re.reshape(-1), a_im.reshape(-1)], axis=0))[None, :]
    nblk = SSM_GROUPS // SCAN_GROUPS
    eye = jnp.eye(SCAN_GROUPS, dtype=F32)

    def b_block(bb):
        bb = bb.reshape(nblk, SCAN_GROUPS, SSM_STATE, SSM_GROUP_SIZE)
        return jnp.einsum("gk,jkph->jghkp", eye, bb).reshape(nblk, SCAN_GROUPS * SSM_GROUP_SIZE, SCAN_CB)

    b_blk = jnp.concatenate([b_block(bb_re), b_block(bb_im)], axis=2)

    def c_block(cc):
        cc = cc.reshape(nblk, SCAN_GROUPS, SSM_GROUP_SIZE, SSM_STATE)
        return jnp.einsum("gk,jghp->jkpgh", eye, cc).reshape(nblk, SCAN_CB, SCAN_GROUPS * SSM_GROUP_SIZE)

    c_blk = jnp.concatenate([c_block(c_re), -c_block(c_im)], axis=1)
    return a_lay, b_blk, c_blk


def _interleave(v):
    rows, c = v.shape
    return v.reshape(SCAN_SEGS, rows // SCAN_SEGS, c).transpose(1, 0, 2).reshape(rows, c)


def _deinterleave(v):
    rows, c = v.shape
    return v.reshape(rows // SCAN_SEGS, SCAN_SEGS, c).transpose(1, 0, 2).reshape(rows, c)


def _scan_groups(a_ref, bu_ref, o_ref, state, *, reverse, tt):
    cb = SCAN_CB
    ar = jnp.broadcast_to(a_ref[:, :cb], (SCAN_SEGS, cb))
    ai = jnp.broadcast_to(a_ref[:, cb:], (SCAN_SEGS, cb))
    ngroups = tt // SCAN_SEGS

    def step(i, st):
        sr, si = st
        r0 = pl.multiple_of(((ngroups - 1 - i) if reverse else i) * SCAN_SEGS, SCAN_SEGS)
        blk = bu_ref[pl.ds(r0, SCAN_SEGS), :]
        nr = ar * sr - ai * si + blk[:, :cb]
        ni = ar * si + ai * sr + blk[:, cb:]
        if o_ref is not None:
            o_ref[pl.ds(r0, SCAN_SEGS), :] = jnp.concatenate([nr, ni], axis=1)
        return nr, ni

    return lax.fori_loop(0, ngroups, step, state, unroll=4)


def _segment_entries(a_ref, e_ref, init_ref, *, reverse, seg_len):
    cb = SCAN_CB
    n_sq = seg_len.bit_length() - 1
    assert 1 << n_sq == seg_len, seg_len
    pr, pi = a_ref[:, :cb], a_ref[:, cb:]
    for _ in range(n_sq):
        pr, pi = pr * pr - pi * pi, 2.0 * pr * pi
    cr = jnp.zeros((1, cb), F32)
    ci = jnp.zeros((1, cb), F32)
    order = range(SCAN_SEGS - 1, -1, -1) if reverse else range(SCAN_SEGS)
    for k, seg in enumerate(order):
        if k > 0:
            prev = seg + 1 if reverse else seg - 1
            er, ei = e_ref[prev:prev + 1, :cb], e_ref[prev:prev + 1, cb:]
            cr, ci = pr * cr - pi * ci + er, pr * ci + pi * cr + ei
        init_ref[seg:seg + 1, :] = jnp.concatenate([cr, ci], axis=1)


def _ssm_specs(nt, tt, nch, reverse):
    cb = SCAN_CB
    tmap = (lambda j, kk: (nt - 1 - kk, j)) if reverse else (lambda j, kk: (kk, j))
    return dict(a=pl.BlockSpec((1, 2 * cb), lambda j, kk: (0, j)),
                seg=pl.BlockSpec((SCAN_SEGS, 2 * cb), lambda j, kk: (0, j)),
                chan=pl.BlockSpec((tt, nch), tmap),
                state=pl.BlockSpec((tt, 2 * cb), tmap),
                b=pl.BlockSpec((None, nch, 2 * cb), lambda j, kk: (j, 0, 0)),
                c=pl.BlockSpec((None, 2 * cb, nch), lambda j, kk: (j, 0, 0)))


def _ssm_ends(a_lay, x, blocks, *, transpose, reverse, tt, name):
    rows = x.shape[0]
    nblk = blocks.shape[0]
    nch = x.shape[1] // nblk
    cb = SCAN_CB
    nt = rows // tt
    sp = _ssm_specs(nt, tt, nch, reverse)

    def body(a_ref, x_ref, w_ref, e_ref, bu_ref):
        kk = pl.program_id(1)

        @pl.when(kk == 0)
        def _():
            e_ref[...] = jnp.zeros_like(e_ref)

        bu_ref[...] = _dot(x_ref[...].astype(BF16), w_ref[...].astype(BF16), 1, 1 if transpose else 0)
        sr, si = _scan_groups(a_ref, bu_ref, None, (e_ref[:, :cb], e_ref[:, cb:]), reverse=reverse, tt=tt)
        e_ref[...] = jnp.concatenate([sr, si], axis=1)

    return pl.pallas_call(
        body, name=name, grid=(nblk, nt),
        in_specs=[sp["a"], sp["chan"], sp["c"] if transpose else sp["b"]],
        out_specs=sp["seg"],
        out_shape=jax.ShapeDtypeStruct((SCAN_SEGS, nblk * 2 * cb), F32),
        scratch_shapes=[pltpu.VMEM((tt, 2 * cb), F32)],
        compiler_params=_params(("parallel", "arbitrary")),
    )(a_lay, x, blocks)


def _ssm_fwd(a_lay, u, b_blk, c_blk, ends, *, tt, name):
    rows = u.shape[0]
    nblk = b_blk.shape[0]
    nch = u.shape[1] // nblk
    cb = SCAN_CB
    nt = rows // tt
    sp = _ssm_specs(nt, tt, nch, False)

    def body(a_ref, e_ref, u_ref, b_ref, c_ref, s_ref, y_ref, init_ref, carry_ref):
        kk = pl.program_id(1)

        @pl.when(kk == 0)
        def _():
            _segment_entries(a_ref, e_ref, init_ref, reverse=False, seg_len=rows // SCAN_SEGS)
            carry_ref[...] = init_ref[...]

        s_ref[...] = _dot(u_ref[...].astype(BF16), b_ref[...].astype(BF16), 1, 0)
        sr, si = _scan_groups(a_ref, s_ref, s_ref, (carry_ref[:, :cb], carry_ref[:, cb:]), reverse=False, tt=tt)
        carry_ref[...] = jnp.concatenate([sr, si], axis=1)
        y_ref[...] = _dot(s_ref[...].astype(BF16), c_ref[...].astype(BF16), 1, 0)

    return pl.pallas_call(
        body, name=name, grid=(nblk, nt),
        in_specs=[sp["a"], sp["seg"], sp["chan"], sp["b"], sp["c"]],
        out_specs=[sp["state"], sp["chan"], sp["seg"]],
        out_shape=[jax.ShapeDtypeStruct((rows, nblk * 2 * cb), F32), jax.ShapeDtypeStruct((rows, nblk * nch), F32),
                   jax.ShapeDtypeStruct((SCAN_SEGS, nblk * 2 * cb), F32)],
        scratch_shapes=[pltpu.VMEM((SCAN_SEGS, 2 * cb), F32)],
        compiler_params=_params(("parallel", "arbitrary")),
    )(a_lay, ends, u, b_blk, c_blk)


def _ssm_bwd(a_conj, dy, u, s, s_entry, b_blk, c_blk, dd, ends, *, tt, name):
    rows = u.shape[0]
    nblk = b_blk.shape[0]
    nch = u.shape[1] // nblk
    cb = SCAN_CB
    nt = rows // tt
    sp = _ssm_specs(nt, tt, nch, True)
    groups_per_tile = tt // SCAN_SEGS
    before = pl.BlockSpec((SCAN_SEGS, 2 * cb), lambda j, kk: (jnp.maximum((nt - 1 - kk) * groups_per_tile - 1, 0), j))

    def body(a_ref, e_ref, dy_ref, u_ref, s_ref, before_ref, entry_ref, b_ref, c_ref, dd_ref,
             du_ref, db_ref, dc_ref, da_ref, lam_ref, carry_ref):
        kk = pl.program_id(1)

        @pl.when(kk == 0)
        def _():
            _segment_entries(a_ref, e_ref, carry_ref, reverse=True, seg_len=rows // SCAN_SEGS)
            db_ref[...] = jnp.zeros_like(db_ref)
            dc_ref[...] = jnp.zeros_like(dc_ref)
            da_ref[...] = jnp.zeros_like(da_ref)

        dyv = dy_ref[...]
        dyb = dyv.astype(BF16)
        lam_ref[...] = _dot(dyb, c_ref[...].astype(BF16), 1, 1)
        lr, li = _scan_groups(a_ref, lam_ref, lam_ref, (carry_ref[:, :cb], carry_ref[:, cb:]), reverse=True, tt=tt)
        carry_ref[...] = jnp.concatenate([lr, li], axis=1)

        lamb = lam_ref[...].astype(BF16)
        du_ref[...] = _dot(lamb, b_ref[...].astype(BF16), 1, 1) + dd_ref[...] * dyv
        db_ref[...] += _dot(u_ref[...].astype(BF16), lamb, 0, 0)
        dc_ref[...] += _dot(s_ref[...].astype(BF16), dyb, 0, 0)

        first = jnp.where(kk == nt - 1, entry_ref[...], before_ref[...])
        rest = tt - SCAN_SEGS
        lam_hi = lam_ref[pl.ds(SCAN_SEGS, rest), :]
        s_lo = s_ref[pl.ds(0, rest), :]
        lam_lo = lam_ref[pl.ds(0, SCAN_SEGS), :]

        def pair(lv, pv):
            lre, lim, pre, pim = lv[:, :cb], lv[:, cb:], pv[:, :cb], pv[:, cb:]
            return (jnp.sum(lre * pre + lim * pim, axis=0, keepdims=True),
                    jnp.sum(lim * pre - lre * pim, axis=0, keepdims=True))

        r1, i1 = pair(lam_hi, s_lo)
        r0, i0 = pair(lam_lo, first)
        da_ref[...] += jnp.concatenate([r1 + r0, i1 + i0], axis=1)

    return pl.pallas_call(
        body, name=name, grid=(nblk, nt),
        in_specs=[sp["a"], sp["seg"], sp["chan"], sp["chan"], sp["state"], before, sp["seg"], sp["b"], sp["c"],
                  pl.BlockSpec((1, nch), lambda j, kk: (0, j))],
        out_specs=[sp["chan"], sp["b"], sp["c"], pl.BlockSpec((1, 2 * cb), lambda j, kk: (0, j))],
        out_shape=[jax.ShapeDtypeStruct((rows, nblk * nch), F32), jax.ShapeDtypeStruct(b_blk.shape, F32),
                   jax.ShapeDtypeStruct(c_blk.shape, F32), jax.ShapeDtypeStruct((1, nblk * 2 * cb), F32)],
        scratch_shapes=[pltpu.VMEM((tt, 2 * cb), F32), pltpu.VMEM((SCAN_SEGS, 2 * cb), F32)],
        compiler_params=_params(("parallel", "arbitrary")),
    )(a_conj, ends, dy, u, s, s, s_entry, b_blk, c_blk, dd)


def _glu_fwd(ys, u, dd, w_glu, b_glu, *, tm, name):
    rows, w = ys.shape

    def body(ys_ref, u_ref, dd_ref, w_ref, b_ref, y0_ref, t_ref, y2_ref):
        y0 = ys_ref[...] + dd_ref[...] * u_ref[...]
        y1 = _gelu(y0)
        t = _dot(y1.astype(BF16), w_ref[...], 1, 0) + b_ref[...]
        y0_ref[...] = y0
        t_ref[...] = t
        y2_ref[...] = (y1 * _sigmoid(t)).astype(BF16)

    row = pl.BlockSpec((tm, w), lambda i: (i, 0))
    vec = pl.BlockSpec((1, w), lambda i: (0, 0))
    return pl.pallas_call(
        body, name=name, grid=(rows // tm,),
        in_specs=[row, row, vec, pl.BlockSpec((w, w), lambda i: (0, 0)), vec],
        out_specs=[row, row, row],
        out_shape=[jax.ShapeDtypeStruct((rows, w), F32), jax.ShapeDtypeStruct((rows, w), F32),
                   jax.ShapeDtypeStruct((rows, w), BF16)],
        compiler_params=_params(("parallel",)),
    )(ys, u, dd, w_glu, b_glu)


def _glu_bwd(dy2, y0, t, u, w_glu, *, tm, name):
    rows, w = y0.shape

    def body(dy2_ref, y0_ref, t_ref, u_ref, w_ref, dy0_ref, dt_ref, y1_ref, db_ref, dd_ref):
        i = pl.program_id(0)
        y0 = y0_ref[...]
        y1 = _gelu(y0)
        sg = _sigmoid(t_ref[...])
        dy2v = dy2_ref[...]
        dt = dy2v * y1 * sg * (1.0 - sg)
        dy1 = dy2v * sg + _dot(dt.astype(BF16), w_ref[...], 1, 1)
        dy0 = dy1 * _gelu_grad(y0)
        dy0_ref[...] = dy0
        dt_ref[...] = dt.astype(BF16)
        y1_ref[...] = y1.astype(BF16)

        @pl.when(i == 0)
        def _():
            db_ref[...] = jnp.zeros_like(db_ref)
            dd_ref[...] = jnp.zeros_like(dd_ref)

        db_ref[...] += jnp.sum(dt, axis=0, keepdims=True)
        dd_ref[...] += jnp.sum(dy0 * u_ref[...], axis=0, keepdims=True)

    row = pl.BlockSpec((tm, w), lambda i: (i, 0))
    vec = pl.BlockSpec((1, w), lambda i: (0, 0))
    return pl.pallas_call(
        body, name=name, grid=(rows // tm,),
        in_specs=[row, row, row, row, pl.BlockSpec((w, w), lambda i: (0, 0))],
        out_specs=[row, row, row, vec, vec],
        out_shape=[jax.ShapeDtypeStruct((rows, w), F32), jax.ShapeDtypeStruct((rows, w), BF16),
                   jax.ShapeDtypeStruct((rows, w), BF16), jax.ShapeDtypeStruct((1, w), F32),
                   jax.ShapeDtypeStruct((1, w), F32)],
        compiler_params=_params(("arbitrary",)),
    )(dy2, y0, t, u, w_glu)


ATTN_TILE = 2048


def _attn_geometry(rows, d):
    sb = ATTN_Q * d
    tr = max(sb, min(ATTN_TILE, rows))
    assert rows % tr == 0 and tr % sb == 0, (rows, d)
    return sb, tr, rows // tr, tr // sb


def _attn_masks():
    qi = lax.broadcasted_iota(jnp.int32, (2 * ATTN_Q, 2 * ATTN_Q), 0) % ATTN_Q
    kj = lax.broadcasted_iota(jnp.int32, (2 * ATTN_Q, 2 * ATTN_Q), 1)
    own_ok = jnp.logical_and(kj >= ATTN_Q, kj - ATTN_Q <= qi)
    prev_ok = jnp.logical_and(kj < ATTN_Q, kj >= qi)
    bias_first = jnp.where(own_ok, 0.0, NEG_INF)
    bias_other = jnp.where(jnp.logical_or(own_ok, prev_ok), 0.0, NEG_INF)
    head0 = lax.broadcasted_iota(jnp.int32, (ATTN_Q, LANES), 1) < ATTN_HEAD_DIM
    return bias_first, bias_other, head0


def _attn_rows(base, n, d):
    return pl.ds(pl.multiple_of(base, ATTN_Q), n) if d == 1 else pl.ds(base, n, stride=d)


def _stack_heads(v, head0):
    return jnp.concatenate([jnp.where(head0, v, 0.0), jnp.where(head0, 0.0, v)], axis=0)


def _unstack_heads(v, head0):
    return jnp.where(head0, v[:ATTN_Q], v[ATTN_Q:])


def _fill_keys(buf, prev_ref, cur_ref, sb):
    buf[pl.ds(0, sb), :] = prev_ref[...]
    buf[pl.ds(sb, cur_ref.shape[0]), :] = cur_ref[...]


def _attn_fwd(qkv, g, d, *, name):
    rows = qkv.shape[0]
    sb, tr, ntiles, nsub = _attn_geometry(rows, d)
    qc, kc, vc = 2 * g, 6 + 2 * g, 12 + 2 * g
    scale = ATTN_HEAD_DIM ** -0.5

    def body(q_ref, kc_ref, kp_ref, vc_ref, vp_ref, o_ref, lse_ref, kbuf, vbuf):
        n = pl.program_id(0)
        _fill_keys(kbuf, kp_ref, kc_ref, sb)
        _fill_keys(vbuf, vp_ref, vc_ref, sb)
        bias_first, bias_other, head0 = _attn_masks()

        def per_block(idx, carry):
            j, r = idx // d, idx % d
            base = j * sb + r
            bias = jnp.where(jnp.logical_and(n == 0, j == 0), bias_first, bias_other)
            qrows = _attn_rows(base, ATTN_Q, d)
            krows = _attn_rows(base, 2 * ATTN_Q, d)
            qs = _stack_heads(q_ref[qrows, :], head0).astype(BF16)
            s = _dot(qs, kbuf[krows, :].astype(BF16), 1, 1) * scale + bias
            mx = jnp.max(s, axis=-1, keepdims=True)
            p = jnp.exp(s - mx)
            den = jnp.sum(p, axis=-1, keepdims=True)
            pv = _dot(p.astype(BF16), vbuf[krows, :].astype(BF16), 1, 0) / den
            o_ref[qrows, :] = _unstack_heads(pv, head0)
            lse_ref[qrows, :] = _unstack_heads(jnp.broadcast_to(mx + jnp.log(den), (2 * ATTN_Q, LANES)), head0)
            return carry

        lax.fori_loop(0, nsub * d, per_block, 0, unroll=8)

    def cur(col):
        return pl.BlockSpec((tr, LANES), lambda n, hp: (n, col + hp))

    def prev(col):
        return pl.BlockSpec((sb, LANES), lambda n, hp: (jnp.maximum(n * nsub - 1, 0), col + hp))

    out_spec = pl.BlockSpec((tr, LANES), lambda n, hp: (n, hp))
    return pl.pallas_call(
        body, name=name, grid=(ntiles, 2),
        in_specs=[cur(qc), cur(kc), prev(kc), cur(vc), prev(vc)],
        out_specs=[out_spec, out_spec],
        out_shape=[jax.ShapeDtypeStruct((rows, 2 * LANES), F32), jax.ShapeDtypeStruct((rows, 2 * LANES), F32)],
        scratch_shapes=[pltpu.VMEM((sb + tr, LANES), F32), pltpu.VMEM((sb + tr, LANES), F32)],
        compiler_params=_params(("parallel", "parallel")),
    )(qkv, qkv, qkv, qkv, qkv)


def _attn_merge(outs, lses, *, tm, name):
    rows, w = outs[0].shape

    def body(o0, o1, o2, l0, l1, l2, o_ref, lse_ref):
        a0, a1, a2 = l0[...], l1[...], l2[...]
        mx = jnp.maximum(jnp.maximum(a0, a1), a2)
        e0, e1, e2 = jnp.exp(a0 - mx), jnp.exp(a1 - mx), jnp.exp(a2 - mx)
        den = e0 + e1 + e2
        o_ref[...] = (e0 / den) * o0[...] + (e1 / den) * o1[...] + (e2 / den) * o2[...]
        lse_ref[...] = mx + jnp.log(den)

    row = pl.BlockSpec((tm, w), lambda i: (i, 0))
    return pl.pallas_call(
        body, name=name, grid=(rows // tm,), in_specs=[row] * 6, out_specs=[row, row],
        out_shape=[jax.ShapeDtypeStruct((rows, w), F32), jax.ShapeDtypeStruct((rows, w), F32)],
        compiler_params=_params(("parallel",)),
    )(*outs, *lses)


def _attn_bwd(qkv, do, o, lse, g, d, prev, *, name):
    rows = qkv.shape[0]
    sb, tr, ntiles, nsub = _attn_geometry(rows, d)
    qc, kc, vc = 2 * g, 6 + 2 * g, 12 + 2 * g
    scale = ATTN_HEAD_DIM ** -0.5

    def body(q_ref, kc_ref, kp_ref, vc_ref, vp_ref, do_ref, o_ref, lse_ref, dq_ref, dk_ref, dv_ref,
             kbuf, vbuf, dk_acc, dv_acc):
        n = pl.program_id(1)

        @pl.when(n == 0)
        def _():
            dk_acc[pl.ds(0, tr), :] = jnp.zeros((tr, LANES), F32)
            dv_acc[pl.ds(0, tr), :] = jnp.zeros((tr, LANES), F32)

        @pl.when(n < ntiles)
        def _():
            dk_acc[pl.ds(tr, tr), :] = jnp.zeros((tr, LANES), F32)
            dv_acc[pl.ds(tr, tr), :] = jnp.zeros((tr, LANES), F32)
            _fill_keys(kbuf, kp_ref, kc_ref, sb)
            _fill_keys(vbuf, vp_ref, vc_ref, sb)
            bias_first, bias_other, head0 = _attn_masks()
            lane = lax.broadcasted_iota(jnp.int32, (ATTN_Q, LANES), 1)

            def per_block(idx, carry):
                j, r = idx // d, idx % d
                base = j * sb + r
                bias = jnp.where(jnp.logical_and(n == 0, j == 0), bias_first, bias_other)
                qrows = _attn_rows(base, ATTN_Q, d)
                krows = _attn_rows(base, 2 * ATTN_Q, d)
                arows = _attn_rows(base + (tr - sb), 2 * ATTN_Q, d)
                qs = _stack_heads(q_ref[qrows, :], head0).astype(BF16)
                dos = _stack_heads(do_ref[qrows, :], head0)
                dosb = dos.astype(BF16)
                ov = o_ref[qrows, :]
                delta = jnp.sum(dos * jnp.concatenate([ov, ov], axis=0), axis=-1, keepdims=True)
                lsev = lse_ref[qrows, :]
                lse_s = jnp.concatenate(
                    [jnp.sum(jnp.where(lane == h * ATTN_HEAD_DIM, lsev, 0.0), axis=-1, keepdims=True) for h in range(2)], axis=0)
                kb = kbuf[krows, :].astype(BF16)
                vb = vbuf[krows, :].astype(BF16)
                p = jnp.exp(_dot(qs, kb, 1, 1) * scale + bias - lse_s)
                ds = (p * (_dot(dosb, vb, 1, 1) - delta) * scale).astype(BF16)
                dq_ref[qrows, :] = _unstack_heads(_dot(ds, kb, 1, 0), head0)
                dk_acc[arows, :] += _dot(ds, qs, 0, 0)
                dv_acc[arows, :] += _dot(p.astype(BF16), dosb, 0, 0)
                return carry

            lax.fori_loop(0, nsub * d, per_block, 0, unroll=4)

        dk_ref[...] = dk_acc[pl.ds(0, tr), :]
        dv_ref[...] = dv_acc[pl.ds(0, tr), :]
        dk_acc[pl.ds(0, tr), :] = dk_acc[pl.ds(tr, tr), :]
        dv_acc[pl.ds(0, tr), :] = dv_acc[pl.ds(tr, tr), :]

    def cur(n):
        return jnp.minimum(n, ntiles - 1)

    def spec(col, prev):
        if prev:
            return pl.BlockSpec((sb, LANES), lambda hp, n: (jnp.maximum(cur(n) * nsub - 1, 0), col + hp))
        return pl.BlockSpec((tr, LANES), lambda hp, n: (cur(n), col + hp))

    row_spec = pl.BlockSpec((tr, LANES), lambda hp, n: (cur(n), hp))
    dq_out = pl.BlockSpec((tr, LANES), lambda hp, n: (cur(n), 2 * g + hp))
    kv_out = pl.BlockSpec((tr, LANES), lambda hp, n: (jnp.maximum(n - 1, 0), 2 * g + hp))
    shape = jax.ShapeDtypeStruct((rows, len(ATTN_PATTERNS) * 2 * LANES), F32)
    ins = [qkv, qkv, qkv, qkv, qkv, do, o, lse]
    in_specs = [spec(qc, False), spec(kc, False), spec(kc, True), spec(vc, False), spec(vc, True),
                row_spec, row_spec, row_spec]
    aliases = {}
    if prev is not None:
        aliases = {len(ins) + t: t for t in range(3)}
        ins = ins + list(prev)
        in_specs = in_specs + [ANY] * 3
    n_in = len(ins)

    def entry(*refs):
        body(*refs[:8], *refs[n_in:])

    return pl.pallas_call(
        entry, name=name, grid=(2, ntiles + 1),
        in_specs=in_specs,
        out_specs=[dq_out, kv_out, kv_out],
        out_shape=[shape, shape, shape],
        input_output_aliases=aliases,
        scratch_shapes=[pltpu.VMEM((sb + tr, LANES), F32), pltpu.VMEM((sb + tr, LANES), F32),
                        pltpu.VMEM((2 * tr, LANES), F32), pltpu.VMEM((2 * tr, LANES), F32)],
        compiler_params=_params(("parallel", "arbitrary")),
    )(*ins)


def _mem_probs(q, k):
    s = _dot(q.astype(BF16), k.astype(BF16), 1, 1) * (MEM_HEAD_DIM ** -0.5)
    e = jnp.exp(s - jnp.max(s, axis=-1, keepdims=True))
    return e / jnp.sum(e, axis=-1, keepdims=True)


def _mem_attn_fwd(mq, kv, *, tq, name):
    rows = mq.shape[0]

    def body(q_ref, k_ref, v_ref, o_ref):
        p = _mem_probs(q_ref[...], k_ref[...])
        o_ref[...] = _dot(p.astype(BF16), v_ref[...].astype(BF16), 1, 0)

    return pl.pallas_call(
        body, name=name, grid=(rows // tq, MEM_HEADS),
        in_specs=[pl.BlockSpec((tq, LANES), lambda i, h: (i, h)),
                  pl.BlockSpec((MEM_LEN, LANES), lambda i, h: (0, h)),
                  pl.BlockSpec((MEM_LEN, LANES), lambda i, h: (0, MEM_HEADS + h))],
        out_specs=pl.BlockSpec((tq, LANES), lambda i, h: (i, h)),
        out_shape=jax.ShapeDtypeStruct((rows, MEM_HEADS * LANES), F32),
        compiler_params=_params(("parallel", "parallel")),
    )(mq, kv, kv)


def _mem_attn_bwd(mq, kv, dmo, *, tq, name):
    rows = mq.shape[0]
    scale = MEM_HEAD_DIM ** -0.5

    def body(q_ref, k_ref, v_ref, do_ref, dq_ref, dk_ref, dv_ref):
        i = pl.program_id(1)
        qb = q_ref[...].astype(BF16)
        kb = k_ref[...].astype(BF16)
        vb = v_ref[...].astype(BF16)
        dob = do_ref[...].astype(BF16)
        p = _mem_probs(q_ref[...], k_ref[...])
        dp = _dot(dob, vb, 1, 1)
        ds = (p * (dp - jnp.sum(p * dp, axis=-1, keepdims=True)) * scale).astype(BF16)
        dq_ref[...] = _dot(ds, kb, 1, 0).astype(dq_ref.dtype)

        @pl.when(i == 0)
        def _():
            dk_ref[...] = jnp.zeros_like(dk_ref)
            dv_ref[...] = jnp.zeros_like(dv_ref)

        dk_ref[...] += _dot(ds, qb, 0, 0)
        dv_ref[...] += _dot(p.astype(BF16), dob, 0, 0)

    kv_out = pl.BlockSpec((MEM_LEN, LANES), lambda h, i: (0, h))
    kv_shape = jax.ShapeDtypeStruct((MEM_LEN, MEM_HEADS * LANES), F32)
    return pl.pallas_call(
        body, name=name, grid=(MEM_HEADS, rows // tq),
        in_specs=[pl.BlockSpec((tq, LANES), lambda h, i: (i, h)),
                  pl.BlockSpec((MEM_LEN, LANES), lambda h, i: (0, h)),
                  pl.BlockSpec((MEM_LEN, LANES), lambda h, i: (0, MEM_HEADS + h)),
                  pl.BlockSpec((tq, LANES), lambda h, i: (i, h))],
        out_specs=[pl.BlockSpec((tq, LANES), lambda h, i: (i, h)), kv_out, kv_out],
        out_shape=[jax.ShapeDtypeStruct((rows, MEM_HEADS * LANES), BF16), kv_shape, kv_shape],
        compiler_params=_params(("parallel", "arbitrary")),
    )(mq, kv, kv, dmo)


def _resident(shape):
    return pl.BlockSpec(shape, lambda i: (0, 0), pipeline_mode=pl.Buffered(1))


def _branch_merge_fwd(acts, wts, zg, b_gate, *, tm, name):
    rows = zg.shape[0]
    d = wts[0].shape[0]

    def body(s_ref, a_ref, m_ref, ws_ref, wa_ref, wm_ref, zg_ref, b_ref, o_ref):
        gt = _sigmoid(zg_ref[...] + b_ref[...])
        acc = None
        for k, (x_ref, w_ref) in enumerate(((s_ref, ws_ref), (a_ref, wa_ref), (m_ref, wm_ref))):
            term = gt[:, k * d:(k + 1) * d] * _dot(x_ref[...].astype(BF16), w_ref[...], 1, 1)
            acc = term if acc is None else acc + term
        o_ref[...] = acc.astype(BF16)

    return pl.pallas_call(
        body, name=name, grid=(rows // tm,),
        in_specs=[pl.BlockSpec((tm, x.shape[1]), lambda i: (i, 0)) for x in acts] + [_resident(w.shape) for w in wts]
        + [pl.BlockSpec((tm, 3 * d), lambda i: (i, 0)), pl.BlockSpec((1, 3 * d), lambda i: (0, 0))],
        out_specs=pl.BlockSpec((tm, d), lambda i: (i, 0)), out_shape=jax.ShapeDtypeStruct((rows, d), BF16),
        compiler_params=_params(("parallel",)),
    )(*acts, *wts, zg, b_gate)


def _branch_merge_bwd(dmerged, acts, wts, zg, b_gate, *, tm, name):
    rows = zg.shape[0]
    d = wts[0].shape[0]

    def body(dm_ref, s_ref, a_ref, m_ref, ws_ref, wa_ref, wm_ref, zg_ref, b_ref,
             ds_ref, da_ref, dmm_ref, dws_ref, dwa_ref, dwm_ref, dzg_ref, db_ref):
        i = pl.program_id(0)

        @pl.when(i == 0)
        def _():
            for r in (dws_ref, dwa_ref, dwm_ref, db_ref):
                r[...] = jnp.zeros_like(r)

        gt = _sigmoid(zg_ref[...] + b_ref[...])
        dm = dm_ref[...]
        groups = ((s_ref, ws_ref, ds_ref, dws_ref), (a_ref, wa_ref, da_ref, dwa_ref), (m_ref, wm_ref, dmm_ref, dwm_ref))
        for k, (x_ref, w_ref, dx_ref, dw_ref) in enumerate(groups):
            cs = pl.ds(k * d, d)
            gk = gt[:, k * d:(k + 1) * d]
            xb = x_ref[...].astype(BF16)
            br = _dot(xb, w_ref[...], 1, 1)
            dbr = (dm * gk).astype(BF16)
            dx_ref[...] = _dot(dbr, w_ref[...], 1, 0)
            dw_ref[...] += _dot(dbr, xb, 0, 0)
            dzg = dm * br * gk * (1.0 - gk)
            dzg_ref[:, cs] = dzg.astype(BF16)
            db_ref[:, cs] += jnp.sum(dzg, axis=0, keepdims=True)

    row = lambda w: pl.BlockSpec((tm, w), lambda i: (i, 0))
    whole = lambda shape: pl.BlockSpec(shape, lambda i: (0, 0))
    return pl.pallas_call(
        body, name=name, grid=(rows // tm,),
        in_specs=[row(d)] + [row(x.shape[1]) for x in acts] + [_resident(w.shape) for w in wts] + [row(3 * d), whole((1, 3 * d))],
        out_specs=[row(x.shape[1]) for x in acts] + [whole(w.shape) for w in wts] + [row(3 * d), whole((1, 3 * d))],
        out_shape=[jax.ShapeDtypeStruct(x.shape, F32) for x in acts] + [jax.ShapeDtypeStruct(w.shape, F32) for w in wts]
        + [jax.ShapeDtypeStruct((rows, 3 * d), BF16), jax.ShapeDtypeStruct((1, 3 * d), F32)],
        compiler_params=_params(("arbitrary",)),
    )(dmerged, *acts, *wts, zg, b_gate)


def _adamw(w, g, m, v, *, tr, name):
    rows, cols = w.shape
    assert rows % tr == 0, (name, rows, tr)

    def body(w_ref, g_ref, m_ref, v_ref, d_ref, nm_ref, nv_ref):
        gv = g_ref[...]
        m2 = ADAM_B1 * m_ref[...] + (1.0 - ADAM_B1) * gv
        v2 = ADAM_B2 * v_ref[...] + (1.0 - ADAM_B2) * (gv * gv)
        m_hat = m2 / (1.0 - ADAM_B1 ** ADAM_STEP)
        v_hat = v2 / (1.0 - ADAM_B2 ** ADAM_STEP)
        d_ref[...] = -ADAM_LR * (m_hat / (jnp.sqrt(v_hat) + ADAM_EPS) + ADAM_WD * w_ref[...])
        nm_ref[...] = m2
        nv_ref[...] = v2

    blk = pl.BlockSpec((tr, cols), lambda i: (i, 0))
    shape = jax.ShapeDtypeStruct((rows, cols), F32)
    return pl.pallas_call(
        body, name=name, grid=(rows // tr,), in_specs=[blk] * 4, out_specs=[blk] * 3,
        out_shape=[shape, shape, shape], compiler_params=_params(("parallel",)),
    )(w, g, m, v)


ANY = pl.BlockSpec(memory_space=pl.ANY)


def _position():
    return lax.axis_index("x"), lax.axis_index("y"), lax.axis_index("c")


def _other_chips(x, y):
    return ((1 - x, y), (x, 1 - y), (1 - x, 1 - y))


def _remote(src, dst, send_sem, recv_sem, dev):
    return pltpu.make_async_remote_copy(src_ref=src, dst_ref=dst, send_sem=send_sem, recv_sem=recv_sem,
                                        device_id=dev, device_id_type=MESH)


def _gather_exchange(bufs):
    nb = len(bufs)

    def rows_of(i, owner, core):
        rs = bufs[i].shape[0] // N_CHIPS
        return pl.ds(pl.multiple_of(owner * rs + core * (rs // 2), 16), rs // 2)

    def first_leg(outs, send_sems, recv_sems, i, j):
        x, y, c = _position()
        px, py = _other_chips(x, y)[j]
        mine = outs[i].at[rows_of(i, 2 * x + y, c)]
        return _remote(mine, mine, send_sems.at[i, j], recv_sems.at[i, j], (px, py, c))

    def passed_on(outs, send_sems, recv_sems, i, j, core):
        x, y, c = _position()
        px, py = _other_chips(x, y)[j]
        rows = outs[i].at[rows_of(i, 2 * px + py, core)]
        return _remote(rows, rows, send_sems.at[i, 3 + j], recv_sems.at[i, 3 + j], (x, y, 1 - c))

    def start(_, outs, send_sems, recv_sems):
        for i in range(nb):
            for j in range(3):
                first_leg(outs, send_sems, recv_sems, i, j).start()

    def finish(_, outs, send_sems, recv_sems):
        x, y, c = _position()
        for i in range(nb):
            for j, (px, py) in enumerate(_other_chips(x, y)):
                landed = outs[i].at[rows_of(i, 2 * px + py, c)]
                _remote(landed, landed, send_sems.at[i, j], recv_sems.at[i, j], (px, py, c)).wait_recv()
                passed_on(outs, send_sems, recv_sems, i, j, c).start()
        for i in range(nb):
            for j in range(3):
                passed_on(outs, send_sems, recv_sems, i, j, 1 - c).wait_recv()
        for i in range(nb):
            for j in range(3):
                first_leg(outs, send_sems, recv_sems, i, j).wait_send()
                passed_on(outs, send_sems, recv_sems, i, j, c).wait_send()

    return _Exchange(ins=list(bufs), outs=[jax.ShapeDtypeStruct(b.shape, b.dtype) for b in bufs],
                     aliases={i: i for i in range(nb)}, sems=[(nb, 6), (nb, 6)], start=start, finish=finish)


def _run_exchange(ex, *, name):
    n_in, n_out = len(ex.ins), len(ex.outs)

    def body(*refs):
        c_in, c_out, sems = refs[:n_in], refs[n_in:n_in + n_out], refs[n_in + n_out:]
        ex.start(c_in, c_out, *sems)
        ex.finish(c_in, c_out, *sems)

    return pl.pallas_call(
        body, name=name, in_specs=[ANY] * n_in, out_specs=[ANY] * n_out, out_shape=list(ex.outs),
        input_output_aliases=dict(ex.aliases),
        scratch_shapes=[pltpu.SemaphoreType.DMA(s) for s in ex.sems],
    )(*ex.ins)


def _row_tile(rows):
    return max(t for t in range(16, min(rows, 512) + 1, 16) if rows % t == 0)


def _exchange_halves(grads, *, name):
    nb = len(grads)

    def body(*refs):
        ins, outs = refs[:nb], refs[nb:2 * nb]
        send_sems, recv_sems = refs[2 * nb:]
        x, y, c = _position()
        copies = []
        for i in range(nb):
            cp = _remote(ins[i].at[:, 1 - c], outs[i], send_sems.at[i], recv_sems.at[i], (x, y, 1 - c))
            cp.start()
            copies.append(cp)
        for cp in copies:
            cp.wait()

    return pl.pallas_call(
        body, name=name, in_specs=[ANY] * nb, out_specs=[ANY] * nb,
        out_shape=[jax.ShapeDtypeStruct((N_CHIPS, g.shape[2], g.shape[3]), F32) for g in grads],
        scratch_shapes=[pltpu.SemaphoreType.DMA((nb,)), pltpu.SemaphoreType.DMA((nb,))],
    )(*grads)


def _pair_sum(g4, got, c_arr, *, name):
    _, _, half, cols = g4.shape
    tr = _row_tile(half)

    def body(c_ref, g_ref, t_ref, p_ref, pb_ref):
        sm = g_ref[...] + t_ref[...]
        p_ref[...] = sm
        pb_ref[...] = sm.astype(BF16)

    blk = pl.BlockSpec((None, tr, cols), lambda j, i, c_ref: (j, i, 0))
    grid_spec = pltpu.PrefetchScalarGridSpec(
        num_scalar_prefetch=1, grid=(N_CHIPS, half // tr),
        in_specs=[pl.BlockSpec((None, None, tr, cols), lambda j, i, c_ref: (j, c_ref[0], i, 0)), blk],
        out_specs=[blk, blk])
    return pl.pallas_call(
        body, name=name, grid_spec=grid_spec,
        out_shape=[jax.ShapeDtypeStruct((N_CHIPS, half, cols), F32), jax.ShapeDtypeStruct((N_CHIPS, half, cols), BF16)],
        compiler_params=_params(("parallel", "parallel")),
    )(c_arr, g4, got)


def _scatter_exchange(parts):
    nb = len(parts)

    def copies(ins, outs, send_sems, recv_sems):
        x, y, c = _position()
        return [_remote(ins[i].at[2 * px + py], outs[i].at[j], send_sems.at[i, j], recv_sems.at[i, j], (px, py, c))
                for i in range(nb) for j, (px, py) in enumerate(_other_chips(x, y))]

    def start(ins, outs, send_sems, recv_sems):
        for cp in copies(ins, outs, send_sems, recv_sems):
            cp.start()

    def finish(ins, outs, send_sems, recv_sems):
        for cp in copies(ins, outs, send_sems, recv_sems):
            cp.wait()

    return _Exchange(ins=list(parts), outs=[jax.ShapeDtypeStruct((3,) + p.shape[1:], p.dtype) for p in parts],
                     aliases={}, sems=[(nb, 3), (nb, 3)], start=start, finish=finish)


def _owner_sum(p, got, chip_arr, c_arr, *, replicated, name):
    _, half, cols = p.shape
    tr = _row_tile(half)

    def body(chip_ref, c_ref, p_ref, r_ref, o_ref):
        o_ref[...] = ((p_ref[...] + r_ref[0].astype(F32)) + r_ref[1].astype(F32)) + r_ref[2].astype(F32)

    if replicated:
        out_spec = pl.BlockSpec((None, None, tr, cols), lambda i, chip_ref, c_ref: (chip_ref[0], c_ref[0], i, 0))
        out_shape = jax.ShapeDtypeStruct((N_CHIPS, 2, half, cols), F32)
    else:
        out_spec = pl.BlockSpec((None, tr, cols), lambda i, chip_ref, c_ref: (c_ref[0], i, 0))
        out_shape = jax.ShapeDtypeStruct((2, half, cols), F32)
    grid_spec = pltpu.PrefetchScalarGridSpec(
        num_scalar_prefetch=2, grid=(half // tr,),
        in_specs=[pl.BlockSpec((None, tr, cols), lambda i, chip_ref, c_ref: (chip_ref[0], i, 0)),
                  pl.BlockSpec((3, tr, cols), lambda i, chip_ref, c_ref: (0, i, 0))],
        out_specs=out_spec)
    return pl.pallas_call(
        body, name=name, grid_spec=grid_spec, out_shape=out_shape,
        compiler_params=_params(("parallel",)),
    )(chip_arr, c_arr, p, got)


def _share_reduced(bufs):
    nb = len(bufs) - 1

    def body(*refs):
        outs = refs[nb + 1:2 * nb + 2]
        send_sems, recv_sems = refs[2 * nb + 2:]
        x, y, c = _position()
        chip = 2 * x + y
        sends = []
        for i in range(nb):
            cp = _remote(outs[i].at[c], outs[i].at[c], send_sems.at[i], recv_sems.at[i], (x, y, 1 - c))
            cp.start()
            sends.append(cp)
        small = outs[nb]
        peers = [(fx, fy, fc) for fx in (0, 1) for fy in (0, 1) for fc in (0, 1) if fx + fy + fc > 0]
        for k, (fx, fy, fc) in enumerate(peers):
            dev = (x ^ fx, y ^ fy, c ^ fc)
            cp = _remote(small.at[chip, c], small.at[chip, c], send_sems.at[nb + k], recv_sems.at[nb + k], dev)
            cp.start()
            sends.append(cp)
        for i in range(nb):
            dst = outs[i].at[1 - c]
            _remote(dst, dst, send_sems.at[i], recv_sems.at[i], (x, y, 1 - c)).wait_recv()
        for k, (fx, fy, fc) in enumerate(peers):
            dst = small.at[2 * (x ^ fx) + (y ^ fy), c ^ fc]
            _remote(dst, dst, send_sems.at[nb + k], recv_sems.at[nb + k], (x ^ fx, y ^ fy, c ^ fc)).wait_recv()
        for cp in sends:
            cp.wait_send()

    n_all = nb + 1
    return pl.pallas_call(
        body, name="grad_share_reduced", in_specs=[ANY] * n_all, out_specs=[ANY] * n_all,
        out_shape=[jax.ShapeDtypeStruct(b.shape, b.dtype) for b in bufs],
        input_output_aliases={i: i for i in range(n_all)},
        scratch_shapes=[pltpu.SemaphoreType.DMA((nb + 7,)), pltpu.SemaphoreType.DMA((nb + 7,))],
    )(*bufs)


class _GradReducer:
    def __init__(self, c_arr, chip_arr):
        self.c_arr, self.chip_arr = c_arr, chip_arr
        self.pairs, self.landed = {}, {}

    def _pair_sums(self, names, grads):
        full = [g.reshape(N_CHIPS, 2, g.shape[0] // (2 * N_CHIPS), g.shape[1]) for g in grads]
        got = _exchange_halves(full, name="grad_exchange_" + names[0])
        for n, g, t in zip(names, full, got):
            self.pairs[n] = _pair_sum(g, t, self.c_arr, name="grad_pair_sum_" + n)

    def scatter(self, names, grads):
        self._pair_sums(names, grads)
        return _scatter_exchange([self.pairs[n][1] for n in names])

    def collect(self, names, bufs):
        self.landed.update(zip(names, bufs))

    def finish(self, names, grads, order):
        self.collect(names, _run_exchange(self.scatter(names, grads), name="grad_scatter_" + names[0]))
        totals = [_owner_sum(self.pairs[n][0], self.landed[n], self.chip_arr, self.c_arr, replicated=(n == order[-1]),
                             name="grad_owner_sum_" + n) for n in order]
        return _share_reduced(totals)


def _pack_small(vals):
    flat = jnp.concatenate([vals[name].reshape(-1) for name, _ in SMALL])
    return jnp.pad(flat, (0, N_CHIPS * SMALL_ROWS * 1024 - SMALL_ELEMS)).reshape(N_CHIPS * SMALL_ROWS, 1024)


def _unpack_small(buf):
    flat = buf.reshape(-1)
    out, off = {}, 0
    for name, shape in SMALL:
        n = int(np.prod(shape))
        out[name] = flat[off:off + n].reshape(shape)
        off += n
    return out


EARLY_REDUCED = (("w_down",), ("w_up",), ("w_o", "w_ssm_br", "w_attn_br", "w_mem_br", "w_glu", "w_mem_kv"))


def _device_step(x, mem, tgt, w, p, *, gather_pending, reducer):
    rows = x.shape[0]
    w = dict(w)
    early = EARLY_REDUCED
    gb = {}

    def reducing(names):
        return reducer.scatter(names, [gb[n] for n in names]) if (reducer is not None and names) else None

    def reduced(names, res):
        if reducer is None or not names:
            return res
        reducer.collect(names, res[1])
        return res[0]

    def fetching(names):
        return _gather_exchange([w[n] for n in names]) if gather_pending else None

    def fetched(names, res):
        if not gather_pending:
            return res
        w.update(zip(names, res[1]))
        return res[0]

    first_use = (("w_glu", "w_ssm_br", "w_attn_br", "w_mem_kv", "w_mem_br", "w_o"), ("w_up",), ("w_down",))
    g1, gm, g2 = p["norm1_g"], p["mem_norm_g"], p["norm2_g"]
    gf = p["final_g"].reshape(1, D_MODEL)
    ssm_args = (p["ssm_lambda_re"][0], p["ssm_lambda_im"][0], p["ssm_log_dt"][0], p["ssm_b_re"][0],
                p["ssm_b_im"][0], p["ssm_c_re"][0], p["ssm_c_im"][0])
    (a_lay, b_blk, c_blk), ssm_vjp = jax.vjp(_ssm_matrices, *ssm_args)
    a_conj = a_lay * _to_scan_layout(jnp.stack([jnp.ones((N_STATES,), F32), -jnp.ones((N_STATES,), F32)]))[None, :]
    dd = p["ssm_d"].reshape(1, SSM_WIDTH)
    win_t = w["w_in"]
    mm = _matmul

    n1 = _rmsnorm_fwd(x, g1, tm=512, name="norm1")
    u = mm(n1, win_t, m=rows, n=512, k=1024, tb=True, tm=2048, tn=512, tk=1024, out_dtypes=(F32,), name="in_u")
    qkv = fetched(first_use[0], mm(n1, win_t, m=rows, n=2304, k=1024, tb=True, tm=2048, tn=256, tk=1024,
                                   b_off=(OFF_QKV // 256, 0), out_dtypes=(F32,), carry=fetching(first_use[0]), name="in_qkv"))
    mq = mm(n1, win_t, m=rows, n=512, k=1024, tb=True, tm=2048, tn=256, tk=1024, b_off=(OFF_MQ // 256, 0),
            out_dtypes=(F32,), name="in_mq")
    zg = fetched(first_use[1], mm(n1, win_t, m=rows, n=3072, k=1024, tb=True, tm=2048, tn=256, tk=1024,
                                  b_off=(OFF_ZG // 256, 0), out_dtypes=(F32,), carry=fetching(first_use[1]), name="in_zg"))

    u_i = _interleave(u)
    ends = _ssm_ends(a_lay, u_i, b_blk, transpose=False, reverse=False, tt=512, name="ssm_fwd_ends")
    s, ys_i, s_entry = _ssm_fwd(a_lay, u_i, b_blk, c_blk, ends, tt=512, name="ssm_fwd")
    ys = _deinterleave(ys_i)
    y0, tglu, y2 = _glu_fwd(ys, u, dd, w["w_glu"], p["b_glu"], tm=512, name="glu_fwd")

    outs, lses = [], []
    for g, (_, d) in enumerate(ATTN_PATTERNS):
        o_g, lse_g = _attn_fwd(qkv, g, d, name=f"attn_fwd_{g}")
        outs.append(o_g)
        lses.append(lse_g)
    o, lse = _attn_merge(outs, lses, tm=1024, name="attn_merge")

    mn = _rmsnorm_fwd(mem, gm, tm=MEM_LEN, name="mem_norm")
    kv = mm(mn, w["w_mem_kv"], m=MEM_LEN, n=1024, k=1024, tm=MEM_LEN, tn=1024, tk=1024, out_dtypes=(F32,), name="mem_kv")
    mo = _mem_attn_fwd(mq, kv, tq=1024, name="mem_attn_fwd")

    branch_acts = (y2, o, mo)
    branch_wts = (w["w_ssm_br"], w["w_attn_br"], w["w_mem_br"])
    merged = _branch_merge_fwd(branch_acts, branch_wts, zg, p["b_gate"], tm=256, name="branch_merge_fwd")
    add = lambda acc, r: (acc + r,)
    h1 = mm(merged, w["w_o"], m=rows, n=1024, k=1024, tm=1024, tn=1024, tk=1024, out_dtypes=(F32,),
            aux=((x, "mn"),), epilogue=add, name="out_proj")
    n2 = _rmsnorm_fwd(h1, g2, tm=512, name="norm2")
    relu2 = lambda acc: (jnp.square(jnp.maximum(acc, 0.0)),)
    act = fetched(first_use[2], mm(n2, w["w_up"], m=rows, n=D_FF, k=1024, tb=True, tm=1024, tn=1024, tk=1024,
                                   out_dtypes=(BF16,), epilogue=relu2, carry=fetching(first_use[2]), name="mlp_up"))
    h2 = mm(act, w["w_down"], m=rows, n=1024, k=D_FF, tm=1024, tn=1024, tk=1024, out_dtypes=(F32,),
            aux=((h1, "mn"),), epilogue=add, name="mlp_down")
    dh2, loss, d_gf = _loss_head(h2, tgt, gf, tm=512, name="loss_head")

    gs = {"final_g": d_gf.reshape(D_MODEL)}
    drelu2 = lambda acc, actv: (acc * (2.0 * jnp.sqrt(actv.astype(F32))),)
    dup = mm(dh2, w["w_down"], m=rows, n=D_FF, k=1024, tb=True, tm=1024, tn=1024, tk=1024, out_dtypes=(BF16,),
             aux=((act, "mn"),), epilogue=drelu2, name="d_act")
    gb["w_down"] = mm(act, dh2, m=D_FF, n=1024, k=rows, ta=True, tm=1024, tn=1024, tk=1024, out_dtypes=(F32,), name="dw_down")
    gb["w_up"] = reduced(early[0], mm(dup, n2, m=D_FF, n=1024, k=rows, ta=True, tm=1024, tn=1024, tk=1024,
                                      out_dtypes=(F32,), carry=reducing(early[0]), name="dw_up"))
    dn2 = reduced(early[1], mm(dup, w["w_up"], m=rows, n=1024, k=D_FF, tm=1024, tn=1024, tk=1024, out_dtypes=(F32,),
                               carry=reducing(early[1]), name="d_n2"))
    dh1, gs["norm2_g"] = _rmsnorm_bwd(h1, g2, dn2, dh2, tm=512, name="norm2_bwd")
    dmerged = mm(dh1, w["w_o"], m=rows, n=1024, k=1024, tb=True, tm=1024, tn=1024, tk=1024, out_dtypes=(F32,), name="d_merged")
    gb["w_o"] = mm(merged, dh1, m=1024, n=1024, k=rows, ta=True, tm=1024, tn=1024, tk=1024, out_dtypes=(F32,), name="dw_o")
    (dy2, do, dmo, gb["w_ssm_br"], gb["w_attn_br"], gb["w_mem_br"], dzg, gs["b_gate"]) = _branch_merge_bwd(
        dmerged, branch_acts, branch_wts, zg, p["b_gate"], tm=256, name="branch_merge_bwd")

    dy0, dt, y1, gs["b_glu"], d_dd = _glu_bwd(dy2, y0, tglu, u, w["w_glu"], tm=512, name="glu_bwd")
    gs["ssm_d"] = d_dd.reshape(1, SSM_GROUPS, SSM_GROUP_SIZE)
    gb["w_glu"] = mm(y1, dt, m=512, n=512, k=rows, ta=True, tm=512, tn=512, tk=1024, out_dtypes=(F32,), name="dw_glu")
    dy0_i = _interleave(dy0)
    lam_ends = _ssm_ends(a_conj, dy0_i, c_blk, transpose=True, reverse=True, tt=512, name="ssm_bwd_ends")
    du_i, d_b_blk, d_c_blk, d_a_lay = _ssm_bwd(a_conj, dy0_i, u_i, s, s_entry, b_blk, c_blk, dd, lam_ends, tt=512,
                                                name="ssm_bwd")
    du = _deinterleave(du_i)
    d_ssm = ssm_vjp((d_a_lay, d_b_blk, d_c_blk))
    for name, val in zip(("ssm_lambda_re", "ssm_lambda_im", "ssm_log_dt", "ssm_b_re", "ssm_b_im", "ssm_c_re", "ssm_c_im"), d_ssm):
        gs[name] = val[None]

    dqkv = None
    for g, (_, d) in enumerate(ATTN_PATTERNS):
        dqkv = _attn_bwd(qkv, do, o, lse, g, d, dqkv, name=f"attn_bwd_{g}")

    dmq, dmk, dmv = _mem_attn_bwd(mq, kv, dmo, tq=1024, name="mem_attn_bwd")
    dkv = jnp.concatenate([dmk, dmv], axis=1)
    gb["w_mem_kv"] = mm(mn, dkv, m=1024, n=1024, k=MEM_LEN, ta=True, tm=1024, tn=1024, tk=MEM_LEN, out_dtypes=(F32,), name="dw_mem_kv")
    dmn = mm(dkv, w["w_mem_kv"], m=MEM_LEN, n=1024, k=1024, tb=True, tm=MEM_LEN, tn=1024, tk=1024, out_dtypes=(F32,), name="d_mn")
    _, gs["mem_norm_g"] = _rmsnorm_bwd(mem, gm, dmn, None, tm=MEM_LEN, name="mem_norm_bwd")

    pieces = ((du, OFF_U, "u"), (dqkv[0], OFF_QKV, "q"), (dqkv[1], OFF_QKV + 768, "k"), (dqkv[2], OFF_QKV + 1536, "v"),
              (dmq, OFF_MQ, "mq"), (dzg, OFF_ZG, "zg"))
    dn = _sum_matmul([piece for piece, _, _ in pieces], win_t, [off for _, off, _ in pieces], tm=512, name="d_n1")
    dw_rows = []
    for piece, off, tag in pieces:
        width = piece.shape[1]
        tmw = 1024 if width % 1024 == 0 else (768 if width == 768 else 512)
        rides = early[2] if tag == "zg" else ()
        dw_rows.append(reduced(rides, mm(piece, n1, m=width, n=1024, k=rows, ta=True, tm=tmw, tn=1024, tk=1024,
                                         out_dtypes=(F32,), carry=reducing(rides), name="dw_in_" + tag)))
    gb["w_in"] = jnp.concatenate(dw_rows, axis=0)
    dx, gs["norm1_g"] = _rmsnorm_bwd(x, g1, dn, dh1, tm=512, name="norm1_bwd")
    return loss, dx, gb, gs


def kernel(x, mem, norm1_g, mem_norm_g, w_in, b_gate, ssm_lambda_re, ssm_lambda_im, ssm_log_dt, ssm_b_re, ssm_b_im, ssm_c_re, ssm_c_im, ssm_d, w_glu, b_glu, w_ssm_br, w_attn_br, w_mem_kv, w_mem_br, w_o, norm2_g, w_up, w_down, final_g, loss_target, m_norm1_g, m_mem_norm_g, m_w_in, m_b_gate, m_ssm_lambda_re, m_ssm_lambda_im, m_ssm_log_dt, m_ssm_b_re, m_ssm_b_im, m_ssm_c_re, m_ssm_c_im, m_ssm_d, m_w_glu, m_b_glu, m_w_ssm_br, m_w_attn_br, m_w_mem_kv, m_w_mem_br, m_w_o, m_norm2_g, m_w_up, m_w_down, m_final_g, v_norm1_g, v_mem_norm_g, v_w_in, v_b_gate, v_ssm_lambda_re, v_ssm_lambda_im, v_ssm_log_dt, v_ssm_b_re, v_ssm_b_im, v_ssm_c_re, v_ssm_c_im, v_ssm_d, v_w_glu, v_b_glu, v_w_ssm_br, v_w_attn_br, v_w_mem_kv, v_w_mem_br, v_w_o, v_norm2_g, v_w_up, v_w_down, v_final_g):
    env = dict(locals())
    weights = {n: env[n] for n in WEIGHT_ORDER}
    moms = {n: env["m_" + n] for n in WEIGHT_ORDER}
    vels = {n: env["v_" + n] for n in WEIGHT_ORDER}
    def shard2d(a):
        return a.reshape(a.shape[-2], a.shape[-1])

    chip = 2 * lax.axis_index("x") + lax.axis_index("y")
    wire = [shard2d(weights[n]).astype(BF16) for n, _, _ in BIG]
    wire = [s.T if tr else s for s, (_, tr, _) in zip(wire, BIG)]
    wire = [lax.dynamic_update_slice(lax.empty((N_CHIPS * s.shape[0], s.shape[1]), BF16), s, (chip * s.shape[0], 0))
            for s in wire]
    w_full = dict(zip([n for n, _, _ in BIG], wire))
    w_full["w_in"] = _run_exchange(_gather_exchange([w_full["w_in"]]), name="all_gather_w_in")[0]
    small = {n: weights[n] for n, _ in SMALL}

    reducer = _GradReducer(lax.axis_index("c").astype(jnp.int32).reshape(1), chip.astype(jnp.int32).reshape(1))
    loss, dx, gb, gs = _device_step(x[0], mem[0], loss_target[0], w_full, small, gather_pending=True, reducer=reducer)
    *shards, small_grad = reducer.finish(["w_in", "small"], [gb["w_in"], _pack_small(gs)],
                                         [n for n, _, _ in BIG] + ["small"])
    grads = {}
    for (n, tr, _), sh in zip(BIG, shards):
        sh = sh.reshape(2 * sh.shape[1], sh.shape[2])
        grads[n] = sh.T if tr else sh
    small_grad = small_grad.reshape(N_CHIPS * SMALL_ROWS, 1024)
    grads_small = _unpack_small(small_grad)

    delta, new_m, new_v = {}, {}, {}
    for n, _, _ in BIG:
        shape = weights[n].shape
        dn_, nm_, nv_ = _adamw(shard2d(weights[n]), grads[n], shard2d(moms[n]), shard2d(vels[n]),
                               tr=min(shape[-2], 256), name="adamw_" + n)
        delta[n], new_m[n], new_v[n] = dn_.reshape(shape), nm_.reshape(shape), nv_.reshape(shape)
        grads[n] = grads[n].reshape(shape)
    ds_, ms_, vs_ = _adamw(_pack_small(small), small_grad,
                           _pack_small({n: moms[n] for n, _ in SMALL}), _pack_small({n: vels[n] for n, _ in SMALL}),
                           tr=N_CHIPS * SMALL_ROWS, name="adamw_small")
    for dst, buf in ((delta, ds_), (new_m, ms_), (new_v, vs_)):
        dst.update(_unpack_small(buf))
    grads.update(grads_small)

    total_loss = lax.psum(loss[0, 0], ("x", "y", "c"))
    return (total_loss, dx[None], *[grads[n] for n in WEIGHT_ORDER], *[delta[n] for n in WEIGHT_ORDER],
            *[new_m[n] for n in WEIGHT_ORDER], *[new_v[n] for n in WEIGHT_ORDER])
```

```python
import functools
import math

import numpy as np
import jax
import jax.numpy as jnp
from jax import lax
from jax.experimental import pallas as pl
from jax.experimental.pallas import tpu as pltpu

F32 = jnp.float32
BF16 = jnp.bfloat16

D_MODEL = 1024
SSM_GROUPS = 32
SSM_GROUP_SIZE = 16
SSM_STATE = 64
SSM_WIDTH = 512
N_STATES = SSM_GROUPS * SSM_STATE
SCAN_CB = 1024
ATTN_PATTERNS = ((128, 1), (512, 4), (2048, 16))
ATTN_HEAD_DIM = 64
ATTN_Q = 128
MEM_LEN = 256
MEM_HEAD_DIM = 128
MEM_HEADS = 4
D_FF = 4096
OFF_U, OFF_QKV, OFF_MQ, OFF_ZG = 0, 512, 2816, 3328
IN_WIDTH = 6400
RMS_EPS = 1e-6
NEG_INF = -1e30
ADAM_LR, ADAM_B1, ADAM_B2, ADAM_EPS, ADAM_WD, ADAM_STEP = 0.001, 0.9, 0.999, 1e-08, 0.01, 10

VMEM_LIMIT_BYTES = 48 * 1024 * 1024
LANES = 128
MXU_WIDTH = 256
MESH = pl.DeviceIdType.MESH
N_CHIPS = 4

SCAN_SEGS = 8
SCAN_GROUPS = SCAN_CB // SSM_STATE

BIG = (("w_in", True, (6400, 1024)), ("w_glu", False, (512, 512)), ("w_ssm_br", True, (1024, 512)),
       ("w_attn_br", True, (1024, 256)), ("w_mem_kv", False, (1024, 1024)), ("w_mem_br", True, (1024, 512)),
       ("w_o", False, (1024, 1024)), ("w_up", True, (4096, 1024)), ("w_down", False, (4096, 1024)))
SMALL = (("norm1_g", (1, 1024)), ("mem_norm_g", (1, 1024)), ("b_gate", (1, 3072)),
         ("ssm_lambda_re", (1, 32, 64)), ("ssm_lambda_im", (1, 32, 64)), ("ssm_log_dt", (1, 32)),
         ("ssm_b_re", (1, 32, 64, 16)), ("ssm_b_im", (1, 32, 64, 16)), ("ssm_c_re", (1, 32, 16, 64)),
         ("ssm_c_im", (1, 32, 16, 64)), ("ssm_d", (1, 32, 16)), ("b_glu", (1, 512)),
         ("norm2_g", (1, 1024)), ("final_g", (1024,)))
WEIGHT_ORDER = ("norm1_g", "mem_norm_g", "w_in", "b_gate", "ssm_lambda_re", "ssm_lambda_im", "ssm_log_dt",
                "ssm_b_re", "ssm_b_im", "ssm_c_re", "ssm_c_im", "ssm_d", "w_glu", "b_glu", "w_ssm_br",
                "w_attn_br", "w_mem_kv", "w_mem_br", "w_o", "norm2_g", "w_up", "w_down", "final_g")
SMALL_ELEMS = sum(int(np.prod(s)) for _, s in SMALL)
SMALL_ROWS = 64


def _params(sem):
    return pltpu.CompilerParams(dimension_semantics=sem, vmem_limit_bytes=VMEM_LIMIT_BYTES)


def _sigmoid(v):
    return 1.0 / (1.0 + jnp.exp(-v))


_GELU_C = math.sqrt(2.0 / math.pi)


def _gelu(v):
    return 0.5 * v * (1.0 + jnp.tanh(_GELU_C * (v + 0.044715 * v * v * v)))


def _gelu_grad(v):
    th = jnp.tanh(_GELU_C * (v + 0.044715 * v * v * v))
    return 0.5 * (1.0 + th) + 0.5 * v * (1.0 - th * th) * _GELU_C * (1.0 + 3.0 * 0.044715 * v * v)


def _dot(a, b, ca, cb):
    return lax.dot_general(a, b, (((ca,), (cb,)), ((), ())), preferred_element_type=F32)


class _Exchange:
    def __init__(self, ins, outs, aliases, sems, start, finish):
        self.ins, self.outs, self.aliases, self.sems, self.start, self.finish = ins, outs, aliases, sems, start, finish


def _matmul(a, b, *, m, n, k, ta=False, tb=False, tm, tn, tk, out_dtypes, name,
            a_off=(0, 0), b_off=(0, 0), aux=(), epilogue=None, carry=None):
    assert m % tm == 0 and n % tn == 0 and k % tk == 0, (name, m, n, k, tm, tn, tk)
    nk = k // tk
    n_aux = len(aux)
    n_out = len(out_dtypes)
    ar, ac = a_off
    br, bc = b_off
    if ta:
        a_spec = pl.BlockSpec((tk, tm), lambda i, j, kk: (kk + ar, i + ac))
    else:
        a_spec = pl.BlockSpec((tm, tk), lambda i, j, kk: (i + ar, kk + ac))
    if tb:
        b_spec = pl.BlockSpec((tn, tk), lambda i, j, kk: (j + br, kk + bc))
    else:
        b_spec = pl.BlockSpec((tk, tn), lambda i, j, kk: (kk + br, j + bc))
    aux_specs = []
    for _, kind in aux:
        if kind == "mn":
            aux_specs.append(pl.BlockSpec((tm, tn), lambda i, j, kk: (i, j)))
        else:
            aux_specs.append(pl.BlockSpec((1, tn), lambda i, j, kk: (0, j)))
    ca = 0 if ta else 1
    cb = 1 if tb else 0

    chunk = MXU_WIDTH if tn % MXU_WIDTH == 0 else tn
    cols = [pl.ds(c0, chunk) for c0 in range(0, tn, chunk)]

    def finish(acc, aux_refs, out_refs, cs):
        auxv = [r[:, cs] for r in aux_refs]
        outs = (acc,) if epilogue is None else epilogue(acc, *auxv)
        for o_ref, o in zip(out_refs, outs):
            o_ref[:, cs] = o.astype(o_ref.dtype)

    def body(a_ref, b_ref, *rest):
        aux_refs = rest[:n_aux]
        out_refs = rest[n_aux:n_aux + n_out]

        def products():
            av = a_ref[...].astype(BF16)
            for cs in cols:
                bv = (b_ref[cs, :] if tb else b_ref[:, cs]).astype(BF16)
                yield cs, _dot(av, bv, ca, cb)

        if nk == 1:
            for cs, prod in products():
                finish(prod, aux_refs, out_refs, cs)
            return
        acc_ref = rest[n_aux + n_out]
        kk = pl.program_id(2)

        @pl.when(kk == 0)
        def _():
            for cs, prod in products():
                acc_ref[:, cs] = prod

        @pl.when(jnp.logical_and(kk > 0, kk < nk - 1))
        def _():
            for cs, prod in products():
                acc_ref[:, cs] += prod

        @pl.when(kk == nk - 1)
        def _():
            for cs, prod in products():
                finish(acc_ref[:, cs] + prod, aux_refs, out_refs, cs)

    res = _call_with_carry(
        body, carry, name=name, grid=(m // tm, n // tn, nk), in_specs=[a_spec, b_spec] + aux_specs,
        out_specs=[pl.BlockSpec((tm, tn), lambda i, j, kk: (i, j)) for _ in range(n_out)],
        out_shape=[jax.ShapeDtypeStruct((m, n), dt) for dt in out_dtypes],
        scratch=[pltpu.VMEM((tm, tn), F32)] if nk > 1 else [], operands=[a, b] + [x for x, _ in aux],
        semantics=("parallel", "parallel", "arbitrary"))
    main = res[0] if n_out == 1 else tuple(res[:n_out])
    return main if carry is None else (main, list(res[n_out:]))


def _call_with_carry(body, carry, *, name, grid, in_specs, out_specs, out_shape, scratch, operands, semantics):
    if carry is None:
        return pl.pallas_call(body, name=name, grid=grid, in_specs=in_specs, out_specs=out_specs, out_shape=out_shape,
                              scratch_shapes=scratch, compiler_params=_params(semantics))(*operands)
    n_in, n_cin, n_out, n_cout, n_scr = len(operands), len(carry.ins), len(out_shape), len(carry.outs), len(scratch)

    def hosted(*refs):
        main_in, c_in = refs[:n_in], refs[n_in:n_in + n_cin]
        main_out = refs[n_in + n_cin:n_in + n_cin + n_out]
        c_out = refs[n_in + n_cin + n_out:n_in + n_cin + n_out + n_cout]
        rest = refs[n_in + n_cin + n_out + n_cout:]
        ids = [pl.program_id(t) for t in range(len(grid))]
        first = functools.reduce(jnp.logical_and, [i == 0 for i in ids])
        last = functools.reduce(jnp.logical_and, [i == g - 1 for i, g in zip(ids, grid)])

        @pl.when(first)
        def _():
            carry.start(c_in, c_out, *rest[n_scr:])

        body(*main_in, *main_out, *rest[:n_scr])

        @pl.when(last)
        def _():
            carry.finish(c_in, c_out, *rest[n_scr:])

    return pl.pallas_call(
        hosted, name=name, grid=grid,
        in_specs=list(in_specs) + [ANY] * n_cin, out_specs=list(out_specs) + [ANY] * n_cout,
        out_shape=list(out_shape) + list(carry.outs),
        input_output_aliases={n_in + i: n_out + o for i, o in carry.aliases.items()},
        scratch_shapes=list(scratch) + [pltpu.SemaphoreType.DMA(s) for s in carry.sems],
        compiler_params=_params(("arbitrary",) * len(grid)),
    )(*operands, *carry.ins)


def _sum_matmul(pieces, b, offs, *, tm, name, carry=None):
    m = pieces[0].shape[0]
    n = b.shape[1]
    npieces = len(pieces)

    def body(*refs):
        b_ref, o_ref = refs[npieces], refs[npieces + 1]
        acc = None
        for p_ref, off in zip(refs[:npieces], offs):
            part = _dot(p_ref[...].astype(BF16), b_ref[pl.ds(off, p_ref.shape[1]), :], 1, 0)
            acc = part if acc is None else acc + part
        o_ref[...] = acc

    res = _call_with_carry(
        body, carry, name=name, grid=(m // tm,),
        in_specs=[pl.BlockSpec((tm, p.shape[1]), lambda i: (i, 0)) for p in pieces]
        + [pl.BlockSpec(b.shape, lambda i: (0, 0), pipeline_mode=pl.Buffered(1))],
        out_specs=[pl.BlockSpec((tm, n), lambda i: (i, 0))], out_shape=[jax.ShapeDtypeStruct((m, n), F32)],
        scratch=[], operands=list(pieces) + [b], semantics=("parallel",))
    return res[0] if carry is None else (res[0], list(res[1:]))


def _rmsnorm_fwd(x, g, *, tm, name):
    rows, d = x.shape

    def body(x_ref, g_ref, o_ref):
        xv = x_ref[...]
        r = lax.rsqrt(jnp.mean(xv * xv, axis=-1, keepdims=True) + RMS_EPS)
        o_ref[...] = (xv * r * g_ref[...]).astype(o_ref.dtype)

    return pl.pallas_call(
        body, name=name, grid=(rows // tm,),
        in_specs=[pl.BlockSpec((tm, d), lambda i: (i, 0)), pl.BlockSpec((1, d), lambda i: (0, 0))],
        out_specs=pl.BlockSpec((tm, d), lambda i: (i, 0)),
        out_shape=jax.ShapeDtypeStruct((rows, d), BF16),
        compiler_params=_params(("parallel",)),
    )(x, g)


def _rmsnorm_bwd(x, g, dy, res, *, tm, name):
    rows, d = x.shape
    has_res = res is not None

    def body(x_ref, g_ref, dy_ref, *rest):
        if has_res:
            res_ref, dx_ref, dg_ref = rest
        else:
            dx_ref, dg_ref = rest
        i = pl.program_id(0)
        xv = x_ref[...]
        r = lax.rsqrt(jnp.mean(xv * xv, axis=-1, keepdims=True) + RMS_EPS)
        xhat = xv * r
        dyv = dy_ref[...]
        dyg = dyv * g_ref[...]
        dx = r * (dyg - xhat * jnp.mean(dyg * xhat, axis=-1, keepdims=True))
        if has_res:
            dx = dx + res_ref[...]
        dx_ref[...] = dx

        @pl.when(i == 0)
        def _():
            dg_ref[...] = jnp.zeros_like(dg_ref)

        dg_ref[...] += jnp.sum(dyv * xhat, axis=0, keepdims=True)

    row_spec = pl.BlockSpec((tm, d), lambda i: (i, 0))
    vec_spec = pl.BlockSpec((1, d), lambda i: (0, 0))
    ins = [x, g, dy] + ([res] if has_res else [])
    return pl.pallas_call(
        body, name=name, grid=(rows // tm,),
        in_specs=[row_spec, vec_spec, row_spec] + ([row_spec] if has_res else []),
        out_specs=[row_spec, vec_spec],
        out_shape=[jax.ShapeDtypeStruct((rows, d), F32), jax.ShapeDtypeStruct((1, d), F32)],
        compiler_params=_params(("arbitrary",)),
    )(*ins)


def _loss_head(h, tgt, g, *, tm, name):
    rows, d = h.shape
    nsteps = rows // tm

    def body(h_ref, t_ref, g_ref, dh_ref, loss_ref, dg_ref, sq_ref):
        i = pl.program_id(0)
        xv = h_ref[...]
        gv = g_ref[...]
        r = lax.rsqrt(jnp.mean(xv * xv, axis=-1, keepdims=True) + RMS_EPS)
        xhat = xv * r
        err = xhat * gv - t_ref[...]
        dyv = err * (1.0 / d)
        dyg = dyv * gv
        dh_ref[...] = r * (dyg - xhat * jnp.mean(dyg * xhat, axis=-1, keepdims=True))

        @pl.when(i == 0)
        def _():
            dg_ref[...] = jnp.zeros_like(dg_ref)
            sq_ref[...] = jnp.zeros_like(sq_ref)

        dg_ref[...] += jnp.sum(dyv * xhat, axis=0, keepdims=True)
        sq_ref[...] += jnp.sum(err * err, axis=0, keepdims=True)

        @pl.when(i == nsteps - 1)
        def _():
            tot = jnp.sum(sq_ref[...], axis=-1, keepdims=True) * (0.5 / d)
            loss_ref[...] = jnp.broadcast_to(tot, loss_ref.shape)

    row_spec = pl.BlockSpec((tm, d), lambda i: (i, 0))
    vec_spec = pl.BlockSpec((1, d), lambda i: (0, 0))
    return pl.pallas_call(
        body, name=name, grid=(nsteps,),
        in_specs=[row_spec, row_spec, vec_spec],
        out_specs=[row_spec, pl.BlockSpec((1, LANES), lambda i: (0, 0)), vec_spec],
        out_shape=[jax.ShapeDtypeStruct((rows, d), F32), jax.ShapeDtypeStruct((1, LANES), F32),
                   jax.ShapeDtypeStruct((1, d), F32)],
        scratch_shapes=[pltpu.VMEM((1, d), F32)],
        compiler_params=_params(("arbitrary",)),
    )(h, tgt, g)


def _to_scan_layout(v):
    lead = v.shape[:-2]
    v = v.reshape(lead + (2, N_STATES // SCAN_CB, SCAN_CB))
    v = jnp.swapaxes(v, -3, -2)
    return v.reshape(lead + (2 * N_STATES,))


def _ssm_matrices(lam_re, lam_im, log_dt, b_re, b_im, c_re, c_im):
    dt = jnp.exp(log_dt)[:, None]
    mag = jnp.exp(lam_re * dt)
    a_re, a_im = mag * jnp.cos(lam_im * dt), mag * jnp.sin(lam_im * dt)
    nr, ni = a_re - 1.0, a_im
    den = lam_re * lam_re + lam_im * lam_im
    coef_re = (nr * lam_re + ni * lam_im) / den
    coef_im = (ni * lam_re - nr * lam_im) / den
    bb_re = coef_re[..., None] * b_re - coef_im[..., None] * b_im
    bb_im = coef_re[..., None] * b_im + coef_im[..., None] * b_re
    a_lay = _to_scan_layout(jnp.stack([a_re.reshape(-1), a_im.reshape(-1)], axis=0))[None, :]
    nblk = SSM_GROUPS // SCAN_GROUPS
    eye = jnp.eye(SCAN_GROUPS, dtype=F32)

    def b_block(bb):
        bb = bb.reshape(nblk, SCAN_GROUPS, SSM_STATE, SSM_GROUP_SIZE)
        return jnp.einsum("gk,jkph->jghkp", eye, bb).reshape(nblk, SCAN_GROUPS * SSM_GROUP_SIZE, SCAN_CB)

    b_blk = jnp.concatenate([b_block(bb_re), b_block(bb_im)], axis=2)

    def c_block(cc):
        cc = cc.reshape(nblk, SCAN_GROUPS, SSM_GROUP_SIZE, SSM_STATE)
        return jnp.einsum("gk,jghp->jkpgh", eye, cc).reshape(nblk, SCAN_CB, SCAN_GROUPS * SSM_GROUP_SIZE)

    c_blk = jnp.concatenate([c_block(c_re), -c_block(c_im)], axis=1)
    return a_lay, b_blk, c_blk


def _interleave(v):
    rows, c = v.shape
    return v.reshape(SCAN_SEGS, rows // SCAN_SEGS, c).transpose(1, 0, 2).reshape(rows, c)


def _deinterleave(v):
    rows, c = v.shape
    return v.reshape(rows // SCAN_SEGS, SCAN_SEGS, c).transpose(1, 0, 2).reshape(rows, c)


def _scan_groups(a_ref, bu_ref, o_ref, state, *, reverse, tt):
    cb = SCAN_CB
    ar = jnp.broadcast_to(a_ref[:, :cb], (SCAN_SEGS, cb))
    ai = jnp.broadcast_to(a_ref[:, cb:], (SCAN_SEGS, cb))
    ngroups = tt // SCAN_SEGS

    def step(i, st):
        sr, si = st
        r0 = pl.multiple_of(((ngroups - 1 - i) if reverse else i) * SCAN_SEGS, SCAN_SEGS)
        blk = bu_ref[pl.ds(r0, SCAN_SEGS), :]
        nr = ar * sr - ai * si + blk[:, :cb]
        ni = ar * si + ai * sr + blk[:, cb:]
        if o_ref is not None:
            o_ref[pl.ds(r0, SCAN_SEGS), :] = jnp.concatenate([nr, ni], axis=1)
        return nr, ni

    return lax.fori_loop(0, ngroups, step, state, unroll=4)


def _segment_entries(a_ref, e_ref, init_ref, *, reverse, seg_len):
    cb = SCAN_CB
    n_sq = seg_len.bit_length() - 1
    assert 1 << n_sq == seg_len, seg_len
    pr, pi = a_ref[:, :cb], a_ref[:, cb:]
    for _ in range(n_sq):
        pr, pi = pr * pr - pi * pi, 2.0 * pr * pi
    cr = jnp.zeros((1, cb), F32)
    ci = jnp.zeros((1, cb), F32)
    order = range(SCAN_SEGS - 1, -1, -1) if reverse else range(SCAN_SEGS)
    for k, seg in enumerate(order):
        if k > 0:
            prev = seg + 1 if reverse else seg - 1
            er, ei = e_ref[prev:prev + 1, :cb], e_ref[prev:prev + 1, cb:]
            cr, ci = pr * cr - pi * ci + er, pr * ci + pi * cr + ei
        init_ref[seg:seg + 1, :] = jnp.concatenate([cr, ci], axis=1)


def _ssm_specs(nt, tt, nch, reverse):
    cb = SCAN_CB
    tmap = (lambda j, kk: (nt - 1 - kk, j)) if reverse else (lambda j, kk: (kk, j))
    return dict(a=pl.BlockSpec((1, 2 * cb), lambda j, kk: (0, j)),
                seg=pl.BlockSpec((SCAN_SEGS, 2 * cb), lambda j, kk: (0, j)),
                chan=pl.BlockSpec((tt, nch), tmap),
                state=pl.BlockSpec((tt, 2 * cb), tmap),
                b=pl.BlockSpec((None, nch, 2 * cb), lambda j, kk: (j, 0, 0)),
                c=pl.BlockSpec((None, 2 * cb, nch), lambda j, kk: (j, 0, 0)))


def _ssm_ends(a_lay, x, blocks, *, transpose, reverse, tt, name):
    rows = x.shape[0]
    nblk = blocks.shape[0]
    nch = x.shape[1] // nblk
    cb = SCAN_CB
    nt = rows // tt
    sp = _ssm_specs(nt, tt, nch, reverse)

    def body(a_ref, x_ref, w_ref, e_ref, bu_ref):
        kk = pl.program_id(1)

        @pl.when(kk == 0)
        def _():
            e_ref[...] = jnp.zeros_like(e_ref)

        bu_ref[...] = _dot(x_ref[...].astype(BF16), w_ref[...].astype(BF16), 1, 1 if transpose else 0)
        sr, si = _scan_groups(a_ref, bu_ref, None, (e_ref[:, :cb], e_ref[:, cb:]), reverse=reverse, tt=tt)
        e_ref[...] = jnp.concatenate([sr, si], axis=1)

    return pl.pallas_call(
        body, name=name, grid=(nblk, nt),
        in_specs=[sp["a"], sp["chan"], sp["c"] if transpose else sp["b"]],
        out_specs=sp["seg"],
        out_shape=jax.ShapeDtypeStruct((SCAN_SEGS, nblk * 2 * cb), F32),
        scratch_shapes=[pltpu.VMEM((tt, 2 * cb), F32)],
        compiler_params=_params(("parallel", "arbitrary")),
    )(a_lay, x, blocks)


def _ssm_fwd(a_lay, u, b_blk, c_blk, ends, *, tt, name):
    rows = u.shape[0]
    nblk = b_blk.shape[0]
    nch = u.shape[1] // nblk
    cb = SCAN_CB
    nt = rows // tt
    sp = _ssm_specs(nt, tt, nch, False)

    def body(a_ref, e_ref, u_ref, b_ref, c_ref, s_ref, y_ref, init_ref, carry_ref):
        kk = pl.program_id(1)

        @pl.when(kk == 0)
        def _():
            _segment_entries(a_ref, e_ref, init_ref, reverse=False, seg_len=rows // SCAN_SEGS)
            carry_ref[...] = init_ref[...]

        s_ref[...] = _dot(u_ref[...].astype(BF16), b_ref[...].astype(BF16), 1, 0)
        sr, si = _scan_groups(a_ref, s_ref, s_ref, (carry_ref[:, :cb], carry_ref[:, cb:]), reverse=False, tt=tt)
        carry_ref[...] = jnp.concatenate([sr, si], axis=1)
        y_ref[...] = _dot(s_ref[...].astype(BF16), c_ref[...].astype(BF16), 1, 0)

    return pl.pallas_call(
        body, name=name, grid=(nblk, nt),
        in_specs=[sp["a"], sp["seg"], sp["chan"], sp["b"], sp["c"]],
        out_specs=[sp["state"], sp["chan"], sp["seg"]],
        out_shape=[jax.ShapeDtypeStruct((rows, nblk * 2 * cb), F32), jax.ShapeDtypeStruct((rows, nblk * nch), F32),
                   jax.ShapeDtypeStruct((SCAN_SEGS, nblk * 2 * cb), F32)],
        scratch_shapes=[pltpu.VMEM((SCAN_SEGS, 2 * cb), F32)],
        compiler_params=_params(("parallel", "arbitrary")),
    )(a_lay, ends, u, b_blk, c_blk)


def _ssm_bwd(a_conj, dy, u, s, s_entry, b_blk, c_blk, dd, ends, *, tt, name):
    rows = u.shape[0]
    nblk = b_blk.shape[0]
    nch = u.shape[1] // nblk
    cb = SCAN_CB
    nt = rows // tt
    sp = _ssm_specs(nt, tt, nch, True)
    groups_per_tile = tt // SCAN_SEGS
    before = pl.BlockSpec((SCAN_SEGS, 2 * cb), lambda j, kk: (jnp.maximum((nt - 1 - kk) * groups_per_tile - 1, 0), j))

    def body(a_ref, e_ref, dy_ref, u_ref, s_ref, before_ref, entry_ref, b_ref, c_ref, dd_ref,
             du_ref, db_ref, dc_ref, da_ref, lam_ref, carry_ref):
        kk = pl.program_id(1)

        @pl.when(kk == 0)
        def _():
            _segment_entries(a_ref, e_ref, carry_ref, reverse=True, seg_len=rows // SCAN_SEGS)
            db_ref[...] = jnp.zeros_like(db_ref)
            dc_ref[...] = jnp.zeros_like(dc_ref)
            da_ref[...] = jnp.zeros_like(da_ref)

        dyv = dy_ref[...]
        dyb = dyv.astype(BF16)
        lam_ref[...] = _dot(dyb, c_ref[...].astype(BF16), 1, 1)
        lr, li = _scan_groups(a_ref, lam_ref, lam_ref, (carry_ref[:, :cb], carry_ref[:, cb:]), reverse=True, tt=tt)
        carry_ref[...] = jnp.concatenate([lr, li], axis=1)

        lamb = lam_ref[...].astype(BF16)
        du_ref[...] = _dot(lamb, b_ref[...].astype(BF16), 1, 1) + dd_ref[...] * dyv
        db_ref[...] += _dot(u_ref[...].astype(BF16), lamb, 0, 0)
        dc_ref[...] += _dot(s_ref[...].astype(BF16), dyb, 0, 0)

        first = jnp.where(kk == nt - 1, entry_ref[...], before_ref[...])
        rest = tt - SCAN_SEGS
        lam_hi = lam_ref[pl.ds(SCAN_SEGS, rest), :]
        s_lo = s_ref[pl.ds(0, rest), :]
        lam_lo = lam_ref[pl.ds(0, SCAN_SEGS), :]

        def pair(lv, pv):
            lre, lim, pre, pim = lv[:, :cb], lv[:, cb:], pv[:, :cb], pv[:, cb:]
            return (jnp.sum(lre * pre + lim * pim, axis=0, keepdims=True),
                    jnp.sum(lim * pre - lre * pim, axis=0, keepdims=True))

        r1, i1 = pair(lam_hi, s_lo)
        r0, i0 = pair(lam_lo, first)
        da_ref[...] += jnp.concatenate([r1 + r0, i1 + i0], axis=1)

    return pl.pallas_call(
        body, name=name, grid=(nblk, nt),
        in_specs=[sp["a"], sp["seg"], sp["chan"], sp["chan"], sp["state"], before, sp["seg"], sp["b"], sp["c"],
                  pl.BlockSpec((1, nch), lambda j, kk: (0, j))],
        out_specs=[sp["chan"], sp["b"], sp["c"], pl.BlockSpec((1, 2 * cb), lambda j, kk: (0, j))],
        out_shape=[jax.ShapeDtypeStruct((rows, nblk * nch), F32), jax.ShapeDtypeStruct(b_blk.shape, F32),
                   jax.ShapeDtypeStruct(c_blk.shape, F32), jax.ShapeDtypeStruct((1, nblk * 2 * cb), F32)],
        scratch_shapes=[pltpu.VMEM((tt, 2 * cb), F32), pltpu.VMEM((SCAN_SEGS, 2 * cb), F32)],
        compiler_params=_params(("parallel", "arbitrary")),
    )(a_conj, ends, dy, u, s, s, s_entry, b_blk, c_blk, dd)


def _glu_fwd(ys, u, dd, w_glu, b_glu, *, tm, name):
    rows, w = ys.shape

    def body(ys_ref, u_ref, dd_ref, w_ref, b_ref, y0_ref, t_ref, y2_ref):
        y0 = ys_ref[...] + dd_ref[...] * u_ref[...]
        y1 = _gelu(y0)
        t = _dot(y1.astype(BF16), w_ref[...], 1, 0) + b_ref[...]
        y0_ref[...] = y0
        t_ref[...] = t
        y2_ref[...] = (y1 * _sigmoid(t)).astype(BF16)

    row = pl.BlockSpec((tm, w), lambda i: (i, 0))
    vec = pl.BlockSpec((1, w), lambda i: (0, 0))
    return pl.pallas_call(
        body, name=name, grid=(rows // tm,),
        in_specs=[row, row, vec, pl.BlockSpec((w, w), lambda i: (0, 0)), vec],
        out_specs=[row, row, row],
        out_shape=[jax.ShapeDtypeStruct((rows, w), F32), jax.ShapeDtypeStruct((rows, w), F32),
                   jax.ShapeDtypeStruct((rows, w), BF16)],
        compiler_params=_params(("parallel",)),
    )(ys, u, dd, w_glu, b_glu)


def _glu_bwd(dy2, y0, t, u, w_glu, *, tm, name):
    rows, w = y0.shape

    def body(dy2_ref, y0_ref, t_ref, u_ref, w_ref, dy0_ref, dt_ref, y1_ref, db_ref, dd_ref):
        i = pl.program_id(0)
        y0 = y0_ref[...]
        y1 = _gelu(y0)
        sg = _sigmoid(t_ref[...])
        dy2v = dy2_ref[...]
        dt = dy2v * y1 * sg * (1.0 - sg)
        dy1 = dy2v * sg + _dot(dt.astype(BF16), w_ref[...], 1, 1)
        dy0 = dy1 * _gelu_grad(y0)
        dy0_ref[...] = dy0
        dt_ref[...] = dt.astype(BF16)
        y1_ref[...] = y1.astype(BF16)

        @pl.when(i == 0)
        def _():
            db_ref[...] = jnp.zeros_like(db_ref)
            dd_ref[...] = jnp.zeros_like(dd_ref)

        db_ref[...] += jnp.sum(dt, axis=0, keepdims=True)
        dd_ref[...] += jnp.sum(dy0 * u_ref[...], axis=0, keepdims=True)

    row = pl.BlockSpec((tm, w), lambda i: (i, 0))
    vec = pl.BlockSpec((1, w), lambda i: (0, 0))
    return pl.pallas_call(
        body, name=name, grid=(rows // tm,),
        in_specs=[row, row, row, row, pl.BlockSpec((w, w), lambda i: (0, 0))],
        out_specs=[row, row, row, vec, vec],
        out_shape=[jax.ShapeDtypeStruct((rows, w), F32), jax.ShapeDtypeStruct((rows, w), BF16),
                   jax.ShapeDtypeStruct((rows, w), BF16), jax.ShapeDtypeStruct((1, w), F32),
                   jax.ShapeDtypeStruct((1, w), F32)],
        compiler_params=_params(("arbitrary",)),
    )(dy2, y0, t, u, w_glu)


ATTN_TILE = 2048


def _attn_geometry(rows, d):
    sb = ATTN_Q * d
    tr = max(sb, min(ATTN_TILE, rows))
    assert rows % tr == 0 and tr % sb == 0, (rows, d)
    return sb, tr, rows // tr, tr // sb


def _attn_masks():
    qi = lax.broadcasted_iota(jnp.int32, (2 * ATTN_Q, 2 * ATTN_Q), 0) % ATTN_Q
    kj = lax.broadcasted_iota(jnp.int32, (2 * ATTN_Q, 2 * ATTN_Q), 1)
    own_ok = jnp.logical_and(kj >= ATTN_Q, kj - ATTN_Q <= qi)
    prev_ok = jnp.logical_and(kj < ATTN_Q, kj >= qi)
    bias_first = jnp.where(own_ok, 0.0, NEG_INF)
    bias_other = jnp.where(jnp.logical_or(own_ok, prev_ok), 0.0, NEG_INF)
    head0 = lax.broadcasted_iota(jnp.int32, (ATTN_Q, LANES), 1) < ATTN_HEAD_DIM
    return bias_first, bias_other, head0


def _attn_rows(base, n, d):
    return pl.ds(pl.multiple_of(base, ATTN_Q), n) if d == 1 else pl.ds(base, n, stride=d)


def _stack_heads(v, head0):
    return jnp.concatenate([jnp.where(head0, v, 0.0), jnp.where(head0, 0.0, v)], axis=0)


def _unstack_heads(v, head0):
    return jnp.where(head0, v[:ATTN_Q], v[ATTN_Q:])


def _fill_keys(buf, prev_ref, cur_ref, sb):
    buf[pl.ds(0, sb), :] = prev_ref[...]
    buf[pl.ds(sb, cur_ref.shape[0]), :] = cur_ref[...]


def _attn_fwd(qkv, g, d, *, name):
    rows = qkv.shape[0]
    sb, tr, ntiles, nsub = _attn_geometry(rows, d)
    qc, kc, vc = 2 * g, 6 + 2 * g, 12 + 2 * g
    scale = ATTN_HEAD_DIM ** -0.5

    def body(q_ref, kc_ref, kp_ref, vc_ref, vp_ref, o_ref, lse_ref, kbuf, vbuf):
        n = pl.program_id(0)
        _fill_keys(kbuf, kp_ref, kc_ref, sb)
        _fill_keys(vbuf, vp_ref, vc_ref, sb)
        bias_first, bias_other, head0 = _attn_masks()

        def per_block(idx, carry):
            j, r = idx // d, idx % d
            base = j * sb + r
            bias = jnp.where(jnp.logical_and(n == 0, j == 0), bias_first, bias_other)
            qrows = _attn_rows(base, ATTN_Q, d)
            krows = _attn_rows(base, 2 * ATTN_Q, d)
            qs = _stack_heads(q_ref[qrows, :], head0).astype(BF16)
            s = _dot(qs, kbuf[krows, :].astype(BF16), 1, 1) * scale + bias
            mx = jnp.max(s, axis=-1, keepdims=True)
            p = jnp.exp(s - mx)
            den = jnp.sum(p, axis=-1, keepdims=True)
            pv = _dot(p.astype(BF16), vbuf[krows, :].astype(BF16), 1, 0) / den
            o_ref[qrows, :] = _unstack_heads(pv, head0)
            lse_ref[qrows, :] = _unstack_heads(jnp.broadcast_to(mx + jnp.log(den), (2 * ATTN_Q, LANES)), head0)
            return carry

        lax.fori_loop(0, nsub * d, per_block, 0, unroll=8)

    def cur(col):
        return pl.BlockSpec((tr, LANES), lambda n, hp: (n, col + hp))

    def prev(col):
        return pl.BlockSpec((sb, LANES), lambda n, hp: (jnp.maximum(n * nsub - 1, 0), col + hp))

    out_spec = pl.BlockSpec((tr, LANES), lambda n, hp: (n, hp))
    return pl.pallas_call(
        body, name=name, grid=(ntiles, 2),
        in_specs=[cur(qc), cur(kc), prev(kc), cur(vc), prev(vc)],
        out_specs=[out_spec, out_spec],
        out_shape=[jax.ShapeDtypeStruct((rows, 2 * LANES), F32), jax.ShapeDtypeStruct((rows, 2 * LANES), F32)],
        scratch_shapes=[pltpu.VMEM((sb + tr, LANES), F32), pltpu.VMEM((sb + tr, LANES), F32)],
        compiler_params=_params(("parallel", "parallel")),
    )(qkv, qkv, qkv, qkv, qkv)


def _attn_merge(outs, lses, *, tm, name):
    rows, w = outs[0].shape

    def body(o0, o1, o2, l0, l1, l2, o_ref, lse_ref):
        a0, a1, a2 = l0[...], l1[...], l2[...]
        mx = jnp.maximum(jnp.maximum(a0, a1), a2)
        e0, e1, e2 = jnp.exp(a0 - mx), jnp.exp(a1 - mx), jnp.exp(a2 - mx)
        den = e0 + e1 + e2
        o_ref[...] = (e0 / den) * o0[...] + (e1 / den) * o1[...] + (e2 / den) * o2[...]
        lse_ref[...] = mx + jnp.log(den)

    row = pl.BlockSpec((tm, w), lambda i: (i, 0))
    return pl.pallas_call(
        body, name=name, grid=(rows // tm,), in_specs=[row] * 6, out_specs=[row, row],
        out_shape=[jax.ShapeDtypeStruct((rows, w), F32), jax.ShapeDtypeStruct((rows, w), F32)],
        compiler_params=_params(("parallel",)),
    )(*outs, *lses)


def _attn_bwd(qkv, do, o, lse, g, d, prev, *, name):
    rows = qkv.shape[0]
    sb, tr, ntiles, nsub = _attn_geometry(rows, d)
    qc, kc, vc = 2 * g, 6 + 2 * g, 12 + 2 * g
    scale = ATTN_HEAD_DIM ** -0.5

    def body(q_ref, kc_ref, kp_ref, vc_ref, vp_ref, do_ref, o_ref, lse_ref, dq_ref, dk_ref, dv_ref,
             kbuf, vbuf, dk_acc, dv_acc):
        n = pl.program_id(1)

        @pl.when(n == 0)
        def _():
            dk_acc[pl.ds(0, tr), :] = jnp.zeros((tr, LANES), F32)
            dv_acc[pl.ds(0, tr), :] = jnp.zeros((tr, LANES), F32)

        @pl.when(n < ntiles)
        def _():
            dk_acc[pl.ds(tr, tr), :] = jnp.zeros((tr, LANES), F32)
            dv_acc[pl.ds(tr, tr), :] = jnp.zeros((tr, LANES), F32)
            _fill_keys(kbuf, kp_ref, kc_ref, sb)
            _fill_keys(vbuf, vp_ref, vc_ref, sb)
            bias_first, bias_other, head0 = _attn_masks()
            lane = lax.broadcasted_iota(jnp.int32, (ATTN_Q, LANES), 1)

            def per_block(idx, carry):
                j, r = idx // d, idx % d
                base = j * sb + r
                bias = jnp.where(jnp.logical_and(n == 0, j == 0), bias_first, bias_other)
                qrows = _attn_rows(base, ATTN_Q, d)
                krows = _attn_rows(base, 2 * ATTN_Q, d)
                arows = _attn_rows(base + (tr - sb), 2 * ATTN_Q, d)
                qs = _stack_heads(q_ref[qrows, :], head0).astype(BF16)
                dos = _stack_heads(do_ref[qrows, :], head0)
                dosb = dos.astype(BF16)
                ov = o_ref[qrows, :]
                delta = jnp.sum(dos * jnp.concatenate([ov, ov], axis=0), axis=-1, keepdims=True)
                lsev = lse_ref[qrows, :]
                lse_s = jnp.concatenate(
                    [jnp.sum(jnp.where(lane == h * ATTN_HEAD_DIM, lsev, 0.0), axis=-1, keepdims=True) for h in range(2)], axis=0)
                kb = kbuf[krows, :].astype(BF16)
                vb = vbuf[krows, :].astype(BF16)
                p = jnp.exp(_dot(qs, kb, 1, 1) * scale + bias - lse_s)
                ds = (p * (_dot(dosb, vb, 1, 1) - delta) * scale).astype(BF16)
                dq_ref[qrows, :] = _unstack_heads(_dot(ds, kb, 1, 0), head0)
                dk_acc[arows, :] += _dot(ds, qs, 0, 0)
                dv_acc[arows, :] += _dot(p.astype(BF16), dosb, 0, 0)
                return carry

            lax.fori_loop(0, nsub * d, per_block, 0, unroll=4)

        dk_ref[...] = dk_acc[pl.ds(0, tr), :]
        dv_ref[...] = dv_acc[pl.ds(0, tr), :]
        dk_acc[pl.ds(0, tr), :] = dk_acc[pl.ds(tr, tr), :]
        dv_acc[pl.ds(0, tr), :] = dv_acc[pl.ds(tr, tr), :]

    def cur(n):
        return jnp.minimum(n, ntiles - 1)

    def spec(col, prev):
        if prev:
            return pl.BlockSpec((sb, LANES), lambda hp, n: (jnp.maximum(cur(n) * nsub - 1, 0), col + hp))
        return pl.BlockSpec((tr, LANES), lambda hp, n: (cur(n), col + hp))

    row_spec = pl.BlockSpec((tr, LANES), lambda hp, n: (cur(n), hp))
    dq_out = pl.BlockSpec((tr, LANES), lambda hp, n: (cur(n), 2 * g + hp))
    kv_out = pl.BlockSpec((tr, LANES), lambda hp, n: (jnp.maximum(n - 1, 0), 2 * g + hp))
    shape = jax.ShapeDtypeStruct((rows, len(ATTN_PATTERNS) * 2 * LANES), F32)
    ins = [qkv, qkv, qkv, qkv, qkv, do, o, lse]
    in_specs = [spec(qc, False), spec(kc, False), spec(kc, True), spec(vc, False), spec(vc, True),
                row_spec, row_spec, row_spec]
    aliases = {}
    if prev is not None:
        aliases = {len(ins) + t: t for t in range(3)}
        ins = ins + list(prev)
        in_specs = in_specs + [ANY] * 3
    n_in = len(ins)

    def entry(*refs):
        body(*refs[:8], *refs[n_in:])

    return pl.pallas_call(
        entry, name=name, grid=(2, ntiles + 1),
        in_specs=in_specs,
        out_specs=[dq_out, kv_out, kv_out],
        out_shape=[shape, shape, shape],
        input_output_aliases=aliases,
        scratch_shapes=[pltpu.VMEM((sb + tr, LANES), F32), pltpu.VMEM((sb + tr, LANES), F32),
                        pltpu.VMEM((2 * tr, LANES), F32), pltpu.VMEM((2 * tr, LANES), F32)],
        compiler_params=_params(("parallel", "arbitrary")),
    )(*ins)


def _mem_probs(q, k):
    s = _dot(q.astype(BF16), k.astype(BF16), 1, 1) * (MEM_HEAD_DIM ** -0.5)
    e = jnp.exp(s - jnp.max(s, axis=-1, keepdims=True))
    return e / jnp.sum(e, axis=-1, keepdims=True)


def _mem_attn_fwd(mq, kv, *, tq, name):
    rows = mq.shape[0]

    def body(q_ref, k_ref, v_ref, o_ref):
        p = _mem_probs(q_ref[...], k_ref[...])
        o_ref[...] = _dot(p.astype(BF16), v_ref[...].astype(BF16), 1, 0)

    return pl.pallas_call(
        body, name=name, grid=(rows // tq, MEM_HEADS),
        in_specs=[pl.BlockSpec((tq, LANES), lambda i, h: (i, h)),
                  pl.BlockSpec((MEM_LEN, LANES), lambda i, h: (0, h)),
                  pl.BlockSpec((MEM_LEN, LANES), lambda i, h: (0, MEM_HEADS + h))],
        out_specs=pl.BlockSpec((tq, LANES), lambda i, h: (i, h)),
        out_shape=jax.ShapeDtypeStruct((rows, MEM_HEADS * LANES), F32),
        compiler_params=_params(("parallel", "parallel")),
    )(mq, kv, kv)


def _mem_attn_bwd(mq, kv, dmo, *, tq, name):
    rows = mq.shape[0]
    scale = MEM_HEAD_DIM ** -0.5

    def body(q_ref, k_ref, v_ref, do_ref, dq_ref, dk_ref, dv_ref):
        i = pl.program_id(1)
        qb = q_ref[...].astype(BF16)
        kb = k_ref[...].astype(BF16)
        vb = v_ref[...].astype(BF16)
        dob = do_ref[...].astype(BF16)
        p = _mem_probs(q_ref[...], k_ref[...])
        dp = _dot(dob, vb, 1, 1)
        ds = (p * (dp - jnp.sum(p * dp, axis=-1, keepdims=True)) * scale).astype(BF16)
        dq_ref[...] = _dot(ds, kb, 1, 0).astype(dq_ref.dtype)

        @pl.when(i == 0)
        def _():
            dk_ref[...] = jnp.zeros_like(dk_ref)
            dv_ref[...] = jnp.zeros_like(dv_ref)

        dk_ref[...] += _dot(ds, qb, 0, 0)
        dv_ref[...] += _dot(p.astype(BF16), dob, 0, 0)

    kv_out = pl.BlockSpec((MEM_LEN, LANES), lambda h, i: (0, h))
    kv_shape = jax.ShapeDtypeStruct((MEM_LEN, MEM_HEADS * LANES), F32)
    return pl.pallas_call(
        body, name=name, grid=(MEM_HEADS, rows // tq),
        in_specs=[pl.BlockSpec((tq, LANES), lambda h, i: (i, h)),
                  pl.BlockSpec((MEM_LEN, LANES), lambda h, i: (0, h)),
                  pl.BlockSpec((MEM_LEN, LANES), lambda h, i: (0, MEM_HEADS + h)),
                  pl.BlockSpec((tq, LANES), lambda h, i: (i, h))],
        out_specs=[pl.BlockSpec((tq, LANES), lambda h, i: (i, h)), kv_out, kv_out],
        out_shape=[jax.ShapeDtypeStruct((rows, MEM_HEADS * LANES), BF16), kv_shape, kv_shape],
        compiler_params=_params(("parallel", "arbitrary")),
    )(mq, kv, kv, dmo)


def _resident(shape):
    return pl.BlockSpec(shape, lambda i: (0, 0), pipeline_mode=pl.Buffered(1))


def _branch_merge_fwd(acts, wts, zg, b_gate, *, tm, name):
    rows = zg.shape[0]
    d = wts[0].shape[0]

    def body(s_ref, a_ref, m_ref, ws_ref, wa_ref, wm_ref, zg_ref, b_ref, o_ref):
        gt = _sigmoid(zg_ref[...] + b_ref[...])
        acc = None
        for k, (x_ref, w_ref) in enumerate(((s_ref, ws_ref), (a_ref, wa_ref), (m_ref, wm_ref))):
            term = gt[:, k * d:(k + 1) * d] * _dot(x_ref[...].astype(BF16), w_ref[...], 1, 1)
            acc = term if acc is None else acc + term
        o_ref[...] = acc.astype(BF16)

    return pl.pallas_call(
        body, name=name, grid=(rows // tm,),
        in_specs=[pl.BlockSpec((tm, x.shape[1]), lambda i: (i, 0)) for x in acts] + [_resident(w.shape) for w in wts]
        + [pl.BlockSpec((tm, 3 * d), lambda i: (i, 0)), pl.BlockSpec((1, 3 * d), lambda i: (0, 0))],
        out_specs=pl.BlockSpec((tm, d), lambda i: (i, 0)), out_shape=jax.ShapeDtypeStruct((rows, d), BF16),
        compiler_params=_params(("parallel",)),
    )(*acts, *wts, zg, b_gate)


def _branch_merge_bwd(dmerged, acts, wts, zg, b_gate, *, tm, name):
    rows = zg.shape[0]
    d = wts[0].shape[0]

    def body(dm_ref, s_ref, a_ref, m_ref, ws_ref, wa_ref, wm_ref, zg_ref, b_ref,
             ds_ref, da_ref, dmm_ref, dws_ref, dwa_ref, dwm_ref, dzg_ref, db_ref):
        i = pl.program_id(0)

        @pl.when(i == 0)
        def _():
            for r in (dws_ref, dwa_ref, dwm_ref, db_ref):
                r[...] = jnp.zeros_like(r)

        gt = _sigmoid(zg_ref[...] + b_ref[...])
        dm = dm_ref[...]
        groups = ((s_ref, ws_ref, ds_ref, dws_ref), (a_ref, wa_ref, da_ref, dwa_ref), (m_ref, wm_ref, dmm_ref, dwm_ref))
        for k, (x_ref, w_ref, dx_ref, dw_ref) in enumerate(groups):
            cs = pl.ds(k * d, d)
            gk = gt[:, k * d:(k + 1) * d]
            xb = x_ref[...].astype(BF16)
            br = _dot(xb, w_ref[...], 1, 1)
            dbr = (dm * gk).astype(BF16)
            dx_ref[...] = _dot(dbr, w_ref[...], 1, 0)
            dw_ref[...] += _dot(dbr, xb, 0, 0)
            dzg = dm * br * gk * (1.0 - gk)
            dzg_ref[:, cs] = dzg.astype(BF16)
            db_ref[:, cs] += jnp.sum(dzg, axis=0, keepdims=True)

    row = lambda w: pl.BlockSpec((tm, w), lambda i: (i, 0))
    whole = lambda shape: pl.BlockSpec(shape, lambda i: (0, 0))
    return pl.pallas_call(
        body, name=name, grid=(rows // tm,),
        in_specs=[row(d)] + [row(x.shape[1]) for x in acts] + [_resident(w.shape) for w in wts] + [row(3 * d), whole((1, 3 * d))],
        out_specs=[row(x.shape[1]) for x in acts] + [whole(w.shape) for w in wts] + [row(3 * d), whole((1, 3 * d))],
        out_shape=[jax.ShapeDtypeStruct(x.shape, F32) for x in acts] + [jax.ShapeDtypeStruct(w.shape, F32) for w in wts]
        + [jax.ShapeDtypeStruct((rows, 3 * d), BF16), jax.ShapeDtypeStruct((1, 3 * d), F32)],
        compiler_params=_params(("arbitrary",)),
    )(dmerged, *acts, *wts, zg, b_gate)


def _adamw(w, g, m, v, *, tr, name):
    rows, cols = w.shape
    assert rows % tr == 0, (name, rows, tr)

    def body(w_ref, g_ref, m_ref, v_ref, d_ref, nm_ref, nv_ref):
        gv = g_ref[...]
        m2 = ADAM_B1 * m_ref[...] + (1.0 - ADAM_B1) * gv
        v2 = ADAM_B2 * v_ref[...] + (1.0 - ADAM_B2) * (gv * gv)
        m_hat = m2 / (1.0 - ADAM_B1 ** ADAM_STEP)
        v_hat = v2 / (1.0 - ADAM_B2 ** ADAM_STEP)
        d_ref[...] = -ADAM_LR * (m_hat / (jnp.sqrt(v_hat) + ADAM_EPS) + ADAM_WD * w_ref[...])
        nm_ref[...] = m2
        nv_ref[...] = v2

    blk = pl.BlockSpec((tr, cols), lambda i: (i, 0))
    shape = jax.ShapeDtypeStruct((rows, cols), F32)
    return pl.pallas_call(
        body, name=name, grid=(rows // tr,), in_specs=[blk] * 4, out_specs=[blk] * 3,
        out_shape=[shape, shape, shape], compiler_params=_params(("parallel",)),
    )(w, g, m, v)


ANY = pl.BlockSpec(memory_space=pl.ANY)


def _position():
    return lax.axis_index("x"), lax.axis_index("y"), lax.axis_index("c")


def _other_chips(x, y):
    return ((1 - x, y), (x, 1 - y), (1 - x, 1 - y))


def _remote(src, dst, send_sem, recv_sem, dev):
    return pltpu.make_async_remote_copy(src_ref=src, dst_ref=dst, send_sem=send_sem, recv_sem=recv_sem,
                                        device_id=dev, device_id_type=MESH)


def _gather_exchange(bufs):
    nb = len(bufs)

    def rows_of(i, owner, core):
        rs = bufs[i].shape[0] // N_CHIPS
        return pl.ds(pl.multiple_of(owner * rs + core * (rs // 2), 16), rs // 2)

    def first_leg(outs, send_sems, recv_sems, i, j):
        x, y, c = _position()
        px, py = _other_chips(x, y)[j]
        mine = outs[i].at[rows_of(i, 2 * x + y, c)]
        return _remote(mine, mine, send_sems.at[i, j], recv_sems.at[i, j], (px, py, c))

    def passed_on(outs, send_sems, recv_sems, i, j, core):
        x, y, c = _position()
        px, py = _other_chips(x, y)[j]
        rows = outs[i].at[rows_of(i, 2 * px + py, core)]
        return _remote(rows, rows, send_sems.at[i, 3 + j], recv_sems.at[i, 3 + j], (x, y, 1 - c))

    def start(_, outs, send_sems, recv_sems):
        for i in range(nb):
            for j in range(3):
                first_leg(outs, send_sems, recv_sems, i, j).start()

    def finish(_, outs, send_sems, recv_sems):
        x, y, c = _position()
        for i in range(nb):
            for j, (px, py) in enumerate(_other_chips(x, y)):
                landed = outs[i].at[rows_of(i, 2 * px + py, c)]
                _remote(landed, landed, send_sems.at[i, j], recv_sems.at[i, j], (px, py, c)).wait_recv()
                passed_on(outs, send_sems, recv_sems, i, j, c).start()
        for i in range(nb):
            for j in range(3):
                passed_on(outs, send_sems, recv_sems, i, j, 1 - c).wait_recv()
        for i in range(nb):
            for j in range(3):
                first_leg(outs, send_sems, recv_sems, i, j).wait_send()
                passed_on(outs, send_sems, recv_sems, i, j, c).wait_send()

    return _Exchange(ins=list(bufs), outs=[jax.ShapeDtypeStruct(b.shape, b.dtype) for b in bufs],
                     aliases={i: i for i in range(nb)}, sems=[(nb, 6), (nb, 6)], start=start, finish=finish)


def _run_exchange(ex, *, name):
    n_in, n_out = len(ex.ins), len(ex.outs)

    def body(*refs):
        c_in, c_out, sems = refs[:n_in], refs[n_in:n_in + n_out], refs[n_in + n_out:]
        ex.start(c_in, c_out, *sems)
        ex.finish(c_in, c_out, *sems)

    return pl.pallas_call(
        body, name=name, in_specs=[ANY] * n_in, out_specs=[ANY] * n_out, out_shape=list(ex.outs),
        input_output_aliases=dict(ex.aliases),
        scratch_shapes=[pltpu.SemaphoreType.DMA(s) for s in ex.sems],
    )(*ex.ins)


def _row_tile(rows):
    return max(t for t in range(16, min(rows, 512) + 1, 16) if rows % t == 0)


def _exchange_halves(grads, *, name):
    nb = len(grads)

    def body(*refs):
        ins, outs = refs[:nb], refs[nb:2 * nb]
        send_sems, recv_sems = refs[2 * nb:]
        x, y, c = _position()
        copies = []
        for i in range(nb):
            cp = _remote(ins[i].at[:, 1 - c], outs[i], send_sems.at[i], recv_sems.at[i], (x, y, 1 - c))
            cp.start()
            copies.append(cp)
        for cp in copies:
            cp.wait()

    return pl.pallas_call(
        body, name=name, in_specs=[ANY] * nb, out_specs=[ANY] * nb,
        out_shape=[jax.ShapeDtypeStruct((N_CHIPS, g.shape[2], g.shape[3]), F32) for g in grads],
        scratch_shapes=[pltpu.SemaphoreType.DMA((nb,)), pltpu.SemaphoreType.DMA((nb,))],
    )(*grads)


def _pair_sum(g4, got, c_arr, *, name):
    _, _, half, cols = g4.shape
    tr = _row_tile(half)

    def body(c_ref, g_ref, t_ref, p_ref, pb_ref):
        sm = g_ref[...] + t_ref[...]
        p_ref[...] = sm
        pb_ref[...] = sm.astype(BF16)

    blk = pl.BlockSpec((None, tr, cols), lambda j, i, c_ref: (j, i, 0))
    grid_spec = pltpu.PrefetchScalarGridSpec(
        num_scalar_prefetch=1, grid=(N_CHIPS, half // tr),
        in_specs=[pl.BlockSpec((None, None, tr, cols), lambda j, i, c_ref: (j, c_ref[0], i, 0)), blk],
        out_specs=[blk, blk])
    return pl.pallas_call(
        body, name=name, grid_spec=grid_spec,
        out_shape=[jax.ShapeDtypeStruct((N_CHIPS, half, cols), F32), jax.ShapeDtypeStruct((N_CHIPS, half, cols), BF16)],
        compiler_params=_params(("parallel", "parallel")),
    )(c_arr, g4, got)


def _scatter_exchange(parts):
    nb = len(parts)

    def copies(ins, outs, send_sems, recv_sems):
        x, y, c = _position()
        return [_remote(ins[i].at[2 * px + py], outs[i].at[j], send_sems.at[i, j], recv_sems.at[i, j], (px, py, c))
                for i in range(nb) for j, (px, py) in enumerate(_other_chips(x, y))]

    def start(ins, outs, send_sems, recv_sems):
        for cp in copies(ins, outs, send_sems, recv_sems):
            cp.start()

    def finish(ins, outs, send_sems, recv_sems):
        for cp in copies(ins, outs, send_sems, recv_sems):
            cp.wait()

    return _Exchange(ins=list(parts), outs=[jax.ShapeDtypeStruct((3,) + p.shape[1:], p.dtype) for p in parts],
                     aliases={}, sems=[(nb, 3), (nb, 3)], start=start, finish=finish)


def _owner_sum(p, got, chip_arr, c_arr, *, replicated, name):
    _, half, cols = p.shape
    tr = _row_tile(half)

    def body(chip_ref, c_ref, p_ref, r_ref, o_ref):
        o_ref[...] = ((p_ref[...] + r_ref[0].astype(F32)) + r_ref[1].astype(F32)) + r_ref[2].astype(F32)

    if replicated:
        out_spec = pl.BlockSpec((None, None, tr, cols), lambda i, chip_ref, c_ref: (chip_ref[0], c_ref[0], i, 0))
        out_shape = jax.ShapeDtypeStruct((N_CHIPS, 2, half, cols), F32)
    else:
        out_spec = pl.BlockSpec((None, tr, cols), lambda i, chip_ref, c_ref: (c_ref[0], i, 0))
        out_shape = jax.ShapeDtypeStruct((2, half, cols), F32)
    grid_spec = pltpu.PrefetchScalarGridSpec(
        num_scalar_prefetch=2, grid=(half // tr,),
        in_specs=[pl.BlockSpec((None, tr, cols), lambda i, chip_ref, c_ref: (chip_ref[0], i, 0)),
                  pl.BlockSpec((3, tr, cols), lambda i, chip_ref, c_ref: (0, i, 0))],
        out_specs=out_spec)
    return pl.pallas_call(
        body, name=name, grid_spec=grid_spec, out_shape=out_shape,
        compiler_params=_params(("parallel",)),
    )(chip_arr, c_arr, p, got)


def _share_reduced(bufs):
    nb = len(bufs) - 1

    def body(*refs):
        outs = refs[nb + 1:2 * nb + 2]
        send_sems, recv_sems = refs[2 * nb + 2:]
        x, y, c = _position()
        chip = 2 * x + y
        sends = []
        for i in range(nb):
            cp = _remote(outs[i].at[c], outs[i].at[c], send_sems.at[i], recv_sems.at[i], (x, y, 1 - c))
            cp.start()
            sends.append(cp)
        small = outs[nb]
        peers = [(fx, fy, fc) for fx in (0, 1) for fy in (0, 1) for fc in (0, 1) if fx + fy + fc > 0]
        for k, (fx, fy, fc) in enumerate(peers):
            dev = (x ^ fx, y ^ fy, c ^ fc)
            cp = _remote(small.at[chip, c], small.at[chip, c], send_sems.at[nb + k], recv_sems.at[nb + k], dev)
            cp.start()
            sends.append(cp)
        for i in range(nb):
            dst = outs[i].at[1 - c]
            _remote(dst, dst, send_sems.at[i], recv_sems.at[i], (x, y, 1 - c)).wait_recv()
        for k, (fx, fy, fc) in enumerate(peers):
            dst = small.at[2 * (x ^ fx) + (y ^ fy), c ^ fc]
            _remote(dst, dst, send_sems.at[nb + k], recv_sems.at[nb + k], (x ^ fx, y ^ fy, c ^ fc)).wait_recv()
        for cp in sends:
            cp.wait_send()

    n_all = nb + 1
    return pl.pallas_call(
        body, name="grad_share_reduced", in_specs=[ANY] * n_all, out_specs=[ANY] * n_all,
        out_shape=[jax.ShapeDtypeStruct(b.shape, b.dtype) for b in bufs],
        input_output_aliases={i: i for i in range(n_all)},
        scratch_shapes=[pltpu.SemaphoreType.DMA((nb + 7,)), pltpu.SemaphoreType.DMA((nb + 7,))],
    )(*bufs)


class _GradReducer:
    def __init__(self, c_arr, chip_arr):
        self.c_arr, self.chip_arr = c_arr, chip_arr
        self.pairs, self.landed = {}, {}

    def _pair_sums(self, names, grads):
        full = [g.reshape(N_CHIPS, 2, g.shape[0] // (2 * N_CHIPS), g.shape[1]) for g in grads]
        got = _exchange_halves(full, name="grad_exchange_" + names[0])
        for n, g, t in zip(names, full, got):
            self.pairs[n] = _pair_sum(g, t, self.c_arr, name="grad_pair_sum_" + n)

    def scatter(self, names, grads):
        self._pair_sums(names, grads)
        return _scatter_exchange([self.pairs[n][1] for n in names])

    def collect(self, names, bufs):
        self.landed.update(zip(names, bufs))

    def finish(self, names, grads, order):
        self.collect(names, _run_exchange(self.scatter(names, grads), name="grad_scatter_" + names[0]))
        totals = [_owner_sum(self.pairs[n][0], self.landed[n], self.chip_arr, self.c_arr, replicated=(n == order[-1]),
                             name="grad_owner_sum_" + n) for n in order]
        return _share_reduced(totals)


def _pack_small(vals):
    flat = jnp.concatenate([vals[name].reshape(-1) for name, _ in SMALL])
    return jnp.pad(flat, (0, N_CHIPS * SMALL_ROWS * 1024 - SMALL_ELEMS)).reshape(N_CHIPS * SMALL_ROWS, 1024)


def _unpack_small(buf):
    flat = buf.reshape(-1)
    out, off = {}, 0
    for name, shape in SMALL:
        n = int(np.prod(shape))
        out[name] = flat[off:off + n].reshape(shape)
        off += n
    return out


EARLY_REDUCED = (("w_down",), ("w_up",), ("w_o", "w_ssm_br", "w_attn_br", "w_mem_br", "w_glu", "w_mem_kv"), ("w_in",))


def _device_step(x, mem, tgt, w, p, *, gather_pending, reducer):
    rows = x.shape[0]
    w = dict(w)
    early = EARLY_REDUCED
    gb = {}

    def reducing(names):
        return reducer.scatter(names, [gb[n] for n in names]) if (reducer is not None and names) else None

    def reduced(names, res):
        if reducer is None or not names:
            return res
        reducer.collect(names, res[1])
        return res[0]

    def fetching(names):
        return _gather_exchange([w[n] for n in names]) if gather_pending else None

    def fetched(names, res):
        if not gather_pending:
            return res
        w.update(zip(names, res[1]))
        return res[0]

    first_use = (("w_glu", "w_ssm_br", "w_attn_br", "w_mem_kv", "w_mem_br", "w_o"), ("w_up",), ("w_down",))
    g1, gm, g2 = p["norm1_g"], p["mem_norm_g"], p["norm2_g"]
    gf = p["final_g"].reshape(1, D_MODEL)
    ssm_args = (p["ssm_lambda_re"][0], p["ssm_lambda_im"][0], p["ssm_log_dt"][0], p["ssm_b_re"][0],
                p["ssm_b_im"][0], p["ssm_c_re"][0], p["ssm_c_im"][0])
    (a_lay, b_blk, c_blk), ssm_vjp = jax.vjp(_ssm_matrices, *ssm_args)
    a_conj = a_lay * _to_scan_layout(jnp.stack([jnp.ones((N_STATES,), F32), -jnp.ones((N_STATES,), F32)]))[None, :]
    dd = p["ssm_d"].reshape(1, SSM_WIDTH)
    win_t = w["w_in"]
    mm = _matmul

    n1 = _rmsnorm_fwd(x, g1, tm=512, name="norm1")
    u = mm(n1, win_t, m=rows, n=512, k=1024, tb=True, tm=2048, tn=512, tk=1024, out_dtypes=(F32,), name="in_u")
    qkv = fetched(first_use[0], mm(n1, win_t, m=rows, n=2304, k=1024, tb=True, tm=2048, tn=256, tk=1024,
                                   b_off=(OFF_QKV // 256, 0), out_dtypes=(F32,), carry=fetching(first_use[0]), name="in_qkv"))
    mq = mm(n1, win_t, m=rows, n=512, k=1024, tb=True, tm=2048, tn=256, tk=1024, b_off=(OFF_MQ // 256, 0),
            out_dtypes=(F32,), name="in_mq")
    zg = fetched(first_use[1], mm(n1, win_t, m=rows, n=3072, k=1024, tb=True, tm=2048, tn=256, tk=1024,
                                  b_off=(OFF_ZG // 256, 0), out_dtypes=(F32,), carry=fetching(first_use[1]), name="in_zg"))

    u_i = _interleave(u)
    ends = _ssm_ends(a_lay, u_i, b_blk, transpose=False, reverse=False, tt=512, name="ssm_fwd_ends")
    s, ys_i, s_entry = _ssm_fwd(a_lay, u_i, b_blk, c_blk, ends, tt=512, name="ssm_fwd")
    ys = _deinterleave(ys_i)
    y0, tglu, y2 = _glu_fwd(ys, u, dd, w["w_glu"], p["b_glu"], tm=512, name="glu_fwd")

    outs, lses = [], []
    for g, (_, d) in enumerate(ATTN_PATTERNS):
        o_g, lse_g = _attn_fwd(qkv, g, d, name=f"attn_fwd_{g}")
        outs.append(o_g)
        lses.append(lse_g)
    o, lse = _attn_merge(outs, lses, tm=1024, name="attn_merge")

    mn = _rmsnorm_fwd(mem, gm, tm=MEM_LEN, name="mem_norm")
    kv = mm(mn, w["w_mem_kv"], m=MEM_LEN, n=1024, k=1024, tm=MEM_LEN, tn=1024, tk=1024, out_dtypes=(F32,), name="mem_kv")
    mo = _mem_attn_fwd(mq, kv, tq=1024, name="mem_attn_fwd")

    branch_acts = (y2, o, mo)
    branch_wts = (w["w_ssm_br"], w["w_attn_br"], w["w_mem_br"])
    merged = _branch_merge_fwd(branch_acts, branch_wts, zg, p["b_gate"], tm=256, name="branch_merge_fwd")
    add = lambda acc, r: (acc + r,)
    h1 = mm(merged, w["w_o"], m=rows, n=1024, k=1024, tm=1024, tn=1024, tk=1024, out_dtypes=(F32,),
            aux=((x, "mn"),), epilogue=add, name="out_proj")
    n2 = _rmsnorm_fwd(h1, g2, tm=512, name="norm2")
    relu2 = lambda acc: (jnp.square(jnp.maximum(acc, 0.0)),)
    act = fetched(first_use[2], mm(n2, w["w_up"], m=rows, n=D_FF, k=1024, tb=True, tm=1024, tn=1024, tk=1024,
                                   out_dtypes=(BF16,), epilogue=relu2, carry=fetching(first_use[2]), name="mlp_up"))
    h2 = mm(act, w["w_down"], m=rows, n=1024, k=D_FF, tm=1024, tn=1024, tk=1024, out_dtypes=(F32,),
            aux=((h1, "mn"),), epilogue=add, name="mlp_down")
    dh2, loss, d_gf = _loss_head(h2, tgt, gf, tm=512, name="loss_head")

    gs = {"final_g": d_gf.reshape(D_MODEL)}
    drelu2 = lambda acc, actv: (acc * (2.0 * jnp.sqrt(actv.astype(F32))),)
    dup = mm(dh2, w["w_down"], m=rows, n=D_FF, k=1024, tb=True, tm=1024, tn=2048, tk=1024, out_dtypes=(BF16,),
             aux=((act, "mn"),), epilogue=drelu2, name="d_act")
    gb["w_down"] = mm(act, dh2, m=D_FF, n=1024, k=rows, ta=True, tm=1024, tn=1024, tk=1024, out_dtypes=(F32,), name="dw_down")
    gb["w_up"] = reduced(early[0], mm(dup, n2, m=D_FF, n=1024, k=rows, ta=True, tm=1024, tn=1024, tk=1024,
                                      out_dtypes=(F32,), carry=reducing(early[0]), name="dw_up"))
    dn2 = reduced(early[1], mm(dup, w["w_up"], m=rows, n=1024, k=D_FF, tm=1024, tn=1024, tk=1024, out_dtypes=(F32,),
                               carry=reducing(early[1]), name="d_n2"))
    dh1, gs["norm2_g"] = _rmsnorm_bwd(h1, g2, dn2, dh2, tm=512, name="norm2_bwd")
    dmerged = mm(dh1, w["w_o"], m=rows, n=1024, k=1024, tb=True, tm=1024, tn=1024, tk=1024, out_dtypes=(F32,), name="d_merged")
    gb["w_o"] = mm(merged, dh1, m=1024, n=1024, k=rows, ta=True, tm=1024, tn=1024, tk=1024, out_dtypes=(F32,), name="dw_o")
    (dy2, do, dmo, gb["w_ssm_br"], gb["w_attn_br"], gb["w_mem_br"], dzg, gs["b_gate"]) = _branch_merge_bwd(
        dmerged, branch_acts, branch_wts, zg, p["b_gate"], tm=256, name="branch_merge_bwd")

    dy0, dt, y1, gs["b_glu"], d_dd = _glu_bwd(dy2, y0, tglu, u, w["w_glu"], tm=512, name="glu_bwd")
    gs["ssm_d"] = d_dd.reshape(1, SSM_GROUPS, SSM_GROUP_SIZE)
    gb["w_glu"] = mm(y1, dt, m=512, n=512, k=rows, ta=True, tm=512, tn=512, tk=1024, out_dtypes=(F32,), name="dw_glu")
    dy0_i = _interleave(dy0)
    lam_ends = _ssm_ends(a_conj, dy0_i, c_blk, transpose=True, reverse=True, tt=512, name="ssm_bwd_ends")
    du_i, d_b_blk, d_c_blk, d_a_lay = _ssm_bwd(a_conj, dy0_i, u_i, s, s_entry, b_blk, c_blk, dd, lam_ends, tt=512,
                                                name="ssm_bwd")
    du = _deinterleave(du_i)
    d_ssm = ssm_vjp((d_a_lay, d_b_blk, d_c_blk))
    for name, val in zip(("ssm_lambda_re", "ssm_lambda_im", "ssm_log_dt", "ssm_b_re", "ssm_b_im", "ssm_c_re", "ssm_c_im"), d_ssm):
        gs[name] = val[None]

    dqkv = None
    for g, (_, d) in enumerate(ATTN_PATTERNS):
        dqkv = _attn_bwd(qkv, do, o, lse, g, d, dqkv, name=f"attn_bwd_{g}")

    dmq, dmk, dmv = _mem_attn_bwd(mq, kv, dmo, tq=1024, name="mem_attn_bwd")
    dkv = jnp.concatenate([dmk, dmv], axis=1)
    gb["w_mem_kv"] = mm(mn, dkv, m=1024, n=1024, k=MEM_LEN, ta=True, tm=1024, tn=1024, tk=MEM_LEN, out_dtypes=(F32,), name="dw_mem_kv")
    dmn = mm(dkv, w["w_mem_kv"], m=MEM_LEN, n=1024, k=1024, tb=True, tm=MEM_LEN, tn=1024, tk=1024, out_dtypes=(F32,), name="d_mn")
    _, gs["mem_norm_g"] = _rmsnorm_bwd(mem, gm, dmn, None, tm=MEM_LEN, name="mem_norm_bwd")

    pieces = ((du, OFF_U, "u"), (dqkv[0], OFF_QKV, "q"), (dqkv[1], OFF_QKV + 768, "k"), (dqkv[2], OFF_QKV + 1536, "v"),
              (dmq, OFF_MQ, "mq"), (dzg, OFF_ZG, "zg"))
    dw_rows = []
    for piece, off, tag in pieces:
        width = piece.shape[1]
        tmw = 1024 if width % 1024 == 0 else (768 if width == 768 else 512)
        rides = early[2] if tag == "zg" else ()
        dw_rows.append(reduced(rides, mm(piece, n1, m=width, n=1024, k=rows, ta=True, tm=tmw, tn=1024, tk=1024,
                                         out_dtypes=(F32,), carry=reducing(rides), name="dw_in_" + tag)))
    gb["w_in"] = jnp.concatenate(dw_rows, axis=0)
    dn = reduced(early[3], _sum_matmul([piece for piece, _, _ in pieces], win_t, [off for _, off, _ in pieces], tm=512,
                                       carry=reducing(early[3]), name="d_n1"))
    dx, gs["norm1_g"] = _rmsnorm_bwd(x, g1, dn, dh1, tm=512, name="norm1_bwd")
    return loss, dx, gb, gs


def kernel(x, mem, norm1_g, mem_norm_g, w_in, b_gate, ssm_lambda_re, ssm_lambda_im, ssm_log_dt, ssm_b_re, ssm_b_im, ssm_c_re, ssm_c_im, ssm_d, w_glu, b_glu, w_ssm_br, w_attn_br, w_mem_kv, w_mem_br, w_o, norm2_g, w_up, w_down, final_g, loss_target, m_norm1_g, m_mem_norm_g, m_w_in, m_b_gate, m_ssm_lambda_re, m_ssm_lambda_im, m_ssm_log_dt, m_ssm_b_re, m_ssm_b_im, m_ssm_c_re, m_ssm_c_im, m_ssm_d, m_w_glu, m_b_glu, m_w_ssm_br, m_w_attn_br, m_w_mem_kv, m_w_mem_br, m_w_o, m_norm2_g, m_w_up, m_w_down, m_final_g, v_norm1_g, v_mem_norm_g, v_w_in, v_b_gate, v_ssm_lambda_re, v_ssm_lambda_im, v_ssm_log_dt, v_ssm_b_re, v_ssm_b_im, v_ssm_c_re, v_ssm_c_im, v_ssm_d, v_w_glu, v_b_glu, v_w_ssm_br, v_w_attn_br, v_w_mem_kv, v_w_mem_br, v_w_o, v_norm2_g, v_w_up, v_w_down, v_final_g):
    env = dict(locals())
    weights = {n: env[n] for n in WEIGHT_ORDER}
    moms = {n: env["m_" + n] for n in WEIGHT_ORDER}
    vels = {n: env["v_" + n] for n in WEIGHT_ORDER}
    def shard2d(a):
        return a.reshape(a.shape[-2], a.shape[-1])

    chip = 2 * lax.axis_index("x") + lax.axis_index("y")
    wire = [shard2d(weights[n]).astype(BF16) for n, _, _ in BIG]
    wire = [s.T if tr else s for s, (_, tr, _) in zip(wire, BIG)]
    wire = [lax.dynamic_update_slice(lax.empty((N_CHIPS * s.shape[0], s.shape[1]), BF16), s, (chip * s.shape[0], 0))
            for s in wire]
    w_full = dict(zip([n for n, _, _ in BIG], wire))
    w_full["w_in"] = _run_exchange(_gather_exchange([w_full["w_in"]]), name="all_gather_w_in")[0]
    small = {n: weights[n] for n, _ in SMALL}

    reducer = _GradReducer(lax.axis_index("c").astype(jnp.int32).reshape(1), chip.astype(jnp.int32).reshape(1))
    loss, dx, gb, gs = _device_step(x[0], mem[0], loss_target[0], w_full, small, gather_pending=True, reducer=reducer)
    *shards, small_grad = reducer.finish(["small"], [_pack_small(gs)], [n for n, _, _ in BIG] + ["small"])
    grads = {}
    for (n, tr, _), sh in zip(BIG, shards):
        sh = sh.reshape(2 * sh.shape[1], sh.shape[2])
        grads[n] = sh.T if tr else sh
    small_grad = small_grad.reshape(N_CHIPS * SMALL_ROWS, 1024)
    grads_small = _unpack_small(small_grad)

    delta, new_m, new_v = {}, {}, {}
    for n, _, _ in BIG:
        shape = weights[n].shape
        dn_, nm_, nv_ = _adamw(shard2d(weights[n]), grads[n], shard2d(moms[n]), shard2d(vels[n]),
                               tr=min(shape[-2], 256), name="adamw_" + n)
        delta[n], new_m[n], new_v[n] = dn_.reshape(shape), nm_.reshape(shape), nv_.reshape(shape)
        grads[n] = grads[n].reshape(shape)
    ds_, ms_, vs_ = _adamw(_pack_small(small), small_grad,
                           _pack_small({n: moms[n] for n, _ in SMALL}), _pack_small({n: vels[n] for n, _ in SMALL}),
                           tr=N_CHIPS * SMALL_ROWS, name="adamw_small")
    for dst, buf in ((delta, ds_), (new_m, ms_), (new_v, vs_)):
        dst.update(_unpack_small(buf))
    grads.update(grads_small)

    total_loss = lax.psum(loss[0, 0], ("x", "y", "c"))
    return (total_loss, dx[None], *[grads[n] for n in WEIGHT_ORDER], *[delta[n] for n in WEIGHT_ORDER],
            *[new_m[n] for n in WEIGHT_ORDER], *[new_v[n] for n in WEIGHT_ORDER])
```

```python
import functools
import math

import numpy as np
import jax
import jax.numpy as jnp
from jax import lax
from jax.experimental import pallas as pl
from jax.experimental.pallas import tpu as pltpu

F32 = jnp.float32
BF16 = jnp.bfloat16

D_MODEL = 1024
SSM_GROUPS = 32
SSM_GROUP_SIZE = 16
SSM_STATE = 64
SSM_WIDTH = 512
N_STATES = SSM_GROUPS * SSM_STATE
SCAN_CB = 1024
ATTN_PATTERNS = ((128, 1), (512, 4), (2048, 16))
ATTN_HEAD_DIM = 64
ATTN_Q = 128
MEM_LEN = 256
MEM_HEAD_DIM = 128
MEM_HEADS = 4
D_FF = 4096
OFF_U, OFF_QKV, OFF_MQ, OFF_ZG = 0, 512, 2816, 3328
IN_WIDTH = 6400
RMS_EPS = 1e-6
NEG_INF = -1e30
ADAM_LR, ADAM_B1, ADAM_B2, ADAM_EPS, ADAM_WD, ADAM_STEP = 0.001, 0.9, 0.999, 1e-08, 0.01, 10

VMEM_LIMIT_BYTES = 48 * 1024 * 1024
LANES = 128
MXU_WIDTH = 256
MESH = pl.DeviceIdType.MESH
N_CHIPS = 4

SCAN_SEGS = 8
SCAN_GROUPS = SCAN_CB // SSM_STATE

BIG = (("w_in", True, (6400, 1024)), ("w_glu", False, (512, 512)), ("w_ssm_br", True, (1024, 512)),
       ("w_attn_br", True, (1024, 256)), ("w_mem_kv", False, (1024, 1024)), ("w_mem_br", True, (1024, 512)),
       ("w_o", False, (1024, 1024)), ("w_up", True, (4096, 1024)), ("w_down", False, (4096, 1024)))
SMALL = (("norm1_g", (1, 1024)), ("mem_norm_g", (1, 1024)), ("b_gate", (1, 3072)),
         ("ssm_lambda_re", (1, 32, 64)), ("ssm_lambda_im", (1, 32, 64)), ("ssm_log_dt", (1, 32)),
         ("ssm_b_re", (1, 32, 64, 16)), ("ssm_b_im", (1, 32, 64, 16)), ("ssm_c_re", (1, 32, 16, 64)),
         ("ssm_c_im", (1, 32, 16, 64)), ("ssm_d", (1, 32, 16)), ("b_glu", (1, 512)),
         ("norm2_g", (1, 1024)), ("final_g", (1024,)))
WEIGHT_ORDER = ("norm1_g", "mem_norm_g", "w_in", "b_gate", "ssm_lambda_re", "ssm_lambda_im", "ssm_log_dt",
                "ssm_b_re", "ssm_b_im", "ssm_c_re", "ssm_c_im", "ssm_d", "w_glu", "b_glu", "w_ssm_br",
                "w_attn_br", "w_mem_kv", "w_mem_br", "w_o", "norm2_g", "w_up", "w_down", "final_g")
SMALL_ELEMS = sum(int(np.prod(s)) for _, s in SMALL)
SMALL_ROWS = 64


def _params(sem):
    return pltpu.CompilerParams(dimension_semantics=sem, vmem_limit_bytes=VMEM_LIMIT_BYTES)


def _sigmoid(v):
    return 1.0 / (1.0 + jnp.exp(-v))


_GELU_C = math.sqrt(2.0 / math.pi)


def _gelu(v):
    return 0.5 * v * (1.0 + jnp.tanh(_GELU_C * (v + 0.044715 * v * v * v)))


def _gelu_grad(v):
    th = jnp.tanh(_GELU_C * (v + 0.044715 * v * v * v))
    return 0.5 * (1.0 + th) + 0.5 * v * (1.0 - th * th) * _GELU_C * (1.0 + 3.0 * 0.044715 * v * v)


def _dot(a, b, ca, cb):
    return lax.dot_general(a, b, (((ca,), (cb,)), ((), ())), preferred_element_type=F32)


class _Exchange:
    def __init__(self, ins, outs, aliases, sems, start, finish):
        self.ins, self.outs, self.aliases, self.sems, self.start, self.finish = ins, outs, aliases, sems, start, finish


def _matmul(a, b, *, m, n, k, ta=False, tb=False, tm, tn, tk, out_dtypes, name,
            a_off=(0, 0), b_off=(0, 0), aux=(), epilogue=None, carry=None):
    assert m % tm == 0 and n % tn == 0 and k % tk == 0, (name, m, n, k, tm, tn, tk)
    nk = k // tk
    n_aux = len(aux)
    n_out = len(out_dtypes)
    ar, ac = a_off
    br, bc = b_off
    if ta:
        a_spec = pl.BlockSpec((tk, tm), lambda i, j, kk: (kk + ar, i + ac))
    else:
        a_spec = pl.BlockSpec((tm, tk), lambda i, j, kk: (i + ar, kk + ac))
    if tb:
        b_spec = pl.BlockSpec((tn, tk), lambda i, j, kk: (j + br, kk + bc))
    else:
        b_spec = pl.BlockSpec((tk, tn), lambda i, j, kk: (kk + br, j + bc))
    aux_specs = []
    for _, kind in aux:
        if kind == "mn":
            aux_specs.append(pl.BlockSpec((tm, tn), lambda i, j, kk: (i, j)))
        else:
            aux_specs.append(pl.BlockSpec((1, tn), lambda i, j, kk: (0, j)))
    ca = 0 if ta else 1
    cb = 1 if tb else 0

    chunk = MXU_WIDTH if tn % MXU_WIDTH == 0 else tn
    cols = [pl.ds(c0, chunk) for c0 in range(0, tn, chunk)]

    def finish(acc, aux_refs, out_refs, cs):
        auxv = [r[:, cs] for r in aux_refs]
        outs = (acc,) if epilogue is None else epilogue(acc, *auxv)
        for o_ref, o in zip(out_refs, outs):
            o_ref[:, cs] = o.astype(o_ref.dtype)

    def body(a_ref, b_ref, *rest):
        aux_refs = rest[:n_aux]
        out_refs = rest[n_aux:n_aux + n_out]

        def products():
            av = a_ref[...].astype(BF16)
            for cs in cols:
                bv = (b_ref[cs, :] if tb else b_ref[:, cs]).astype(BF16)
                yield cs, _dot(av, bv, ca, cb)

        if nk == 1:
            for cs, prod in products():
                finish(prod, aux_refs, out_refs, cs)
            return
        acc_ref = rest[n_aux + n_out]
        kk = pl.program_id(2)

        @pl.when(kk == 0)
        def _():
            for cs, prod in products():
                acc_ref[:, cs] = prod

        @pl.when(jnp.logical_and(kk > 0, kk < nk - 1))
        def _():
            for cs, prod in products():
                acc_ref[:, cs] += prod

        @pl.when(kk == nk - 1)
        def _():
            for cs, prod in products():
                finish(acc_ref[:, cs] + prod, aux_refs, out_refs, cs)

    res = _call_with_carry(
        body, carry, name=name, grid=(m // tm, n // tn, nk), in_specs=[a_spec, b_spec] + aux_specs,
        out_specs=[pl.BlockSpec((tm, tn), lambda i, j, kk: (i, j)) for _ in range(n_out)],
        out_shape=[jax.ShapeDtypeStruct((m, n), dt) for dt in out_dtypes],
        scratch=[pltpu.VMEM((tm, tn), F32)] if nk > 1 else [], operands=[a, b] + [x for x, _ in aux],
        semantics=("parallel", "parallel", "arbitrary"))
    main = res[0] if n_out == 1 else tuple(res[:n_out])
    return main if carry is None else (main, list(res[n_out:]))


def _call_with_carry(body, carry, *, name, grid, in_specs, out_specs, out_shape, scratch, operands, semantics):
    if carry is None:
        return pl.pallas_call(body, name=name, grid=grid, in_specs=in_specs, out_specs=out_specs, out_shape=out_shape,
                              scratch_shapes=scratch, compiler_params=_params(semantics))(*operands)
    n_in, n_cin, n_out, n_cout, n_scr = len(operands), len(carry.ins), len(out_shape), len(carry.outs), len(scratch)

    def hosted(*refs):
        main_in, c_in = refs[:n_in], refs[n_in:n_in + n_cin]
        main_out = refs[n_in + n_cin:n_in + n_cin + n_out]
        c_out = refs[n_in + n_cin + n_out:n_in + n_cin + n_out + n_cout]
        rest = refs[n_in + n_cin + n_out + n_cout:]
        ids = [pl.program_id(t) for t in range(len(grid))]
        first = functools.reduce(jnp.logical_and, [i == 0 for i in ids])
        last = functools.reduce(jnp.logical_and, [i == g - 1 for i, g in zip(ids, grid)])

        @pl.when(first)
        def _():
            carry.start(c_in, c_out, *rest[n_scr:])

        body(*main_in, *main_out, *rest[:n_scr])

        @pl.when(last)
        def _():
            carry.finish(c_in, c_out, *rest[n_scr:])

    return pl.pallas_call(
        hosted, name=name, grid=grid,
        in_specs=list(in_specs) + [ANY] * n_cin, out_specs=list(out_specs) + [ANY] * n_cout,
        out_shape=list(out_shape) + list(carry.outs),
        input_output_aliases={n_in + i: n_out + o for i, o in carry.aliases.items()},
        scratch_shapes=list(scratch) + [pltpu.SemaphoreType.DMA(s) for s in carry.sems],
        compiler_params=_params(("arbitrary",) * len(grid)),
    )(*operands, *carry.ins)


def _sum_matmul(pieces, b, offs, *, tm, name, carry=None):
    m = pieces[0].shape[0]
    n = b.shape[1]
    npieces = len(pieces)

    def body(*refs):
        b_ref, o_ref = refs[npieces], refs[npieces + 1]
        acc = None
        for p_ref, off in zip(refs[:npieces], offs):
            part = _dot(p_ref[...].astype(BF16), b_ref[pl.ds(off, p_ref.shape[1]), :], 1, 0)
            acc = part if acc is None else acc + part
        o_ref[...] = acc

    res = _call_with_carry(
        body, carry, name=name, grid=(m // tm,),
        in_specs=[pl.BlockSpec((tm, p.shape[1]), lambda i: (i, 0)) for p in pieces]
        + [pl.BlockSpec(b.shape, lambda i: (0, 0), pipeline_mode=pl.Buffered(1))],
        out_specs=[pl.BlockSpec((tm, n), lambda i: (i, 0))], out_shape=[jax.ShapeDtypeStruct((m, n), F32)],
        scratch=[], operands=list(pieces) + [b], semantics=("parallel",))
    return res[0] if carry is None else (res[0], list(res[1:]))


def _rmsnorm_fwd(x, g, *, tm, name):
    rows, d = x.shape

    def body(x_ref, g_ref, o_ref):
        xv = x_ref[...]
        r = lax.rsqrt(jnp.mean(xv * xv, axis=-1, keepdims=True) + RMS_EPS)
        o_ref[...] = (xv * r * g_ref[...]).astype(o_ref.dtype)

    return pl.pallas_call(
        body, name=name, grid=(rows // tm,),
        in_specs=[pl.BlockSpec((tm, d), lambda i: (i, 0)), pl.BlockSpec((1, d), lambda i: (0, 0))],
        out_specs=pl.BlockSpec((tm, d), lambda i: (i, 0)),
        out_shape=jax.ShapeDtypeStruct((rows, d), BF16),
        compiler_params=_params(("parallel",)),
    )(x, g)


def _rmsnorm_bwd(x, g, dy, res, *, tm, name):
    rows, d = x.shape
    has_res = res is not None

    def body(x_ref, g_ref, dy_ref, *rest):
        if has_res:
            res_ref, dx_ref, dg_ref = rest
        else:
            dx_ref, dg_ref = rest
        i = pl.program_id(0)
        xv = x_ref[...]
        r = lax.rsqrt(jnp.mean(xv * xv, axis=-1, keepdims=True) + RMS_EPS)
        xhat = xv * r
        dyv = dy_ref[...]
        dyg = dyv * g_ref[...]
        dx = r * (dyg - xhat * jnp.mean(dyg * xhat, axis=-1, keepdims=True))
        if has_res:
            dx = dx + res_ref[...]
        dx_ref[...] = dx

        @pl.when(i == 0)
        def _():
            dg_ref[...] = jnp.zeros_like(dg_ref)

        dg_ref[...] += jnp.sum(dyv * xhat, axis=0, keepdims=True)

    row_spec = pl.BlockSpec((tm, d), lambda i: (i, 0))
    vec_spec = pl.BlockSpec((1, d), lambda i: (0, 0))
    ins = [x, g, dy] + ([res] if has_res else [])
    return pl.pallas_call(
        body, name=name, grid=(rows // tm,),
        in_specs=[row_spec, vec_spec, row_spec] + ([row_spec] if has_res else []),
        out_specs=[row_spec, vec_spec],
        out_shape=[jax.ShapeDtypeStruct((rows, d), F32), jax.ShapeDtypeStruct((1, d), F32)],
        compiler_params=_params(("arbitrary",)),
    )(*ins)


def _loss_head(h, tgt, g, *, tm, name):
    rows, d = h.shape
    nsteps = rows // tm

    def body(h_ref, t_ref, g_ref, dh_ref, loss_ref, dg_ref, sq_ref):
        i = pl.program_id(0)
        xv = h_ref[...]
        gv = g_ref[...]
        r = lax.rsqrt(jnp.mean(xv * xv, axis=-1, keepdims=True) + RMS_EPS)
        xhat = xv * r
        err = xhat * gv - t_ref[...]
        dyv = err * (1.0 / d)
        dyg = dyv * gv
        dh_ref[...] = r * (dyg - xhat * jnp.mean(dyg * xhat, axis=-1, keepdims=True))

        @pl.when(i == 0)
        def _():
            dg_ref[...] = jnp.zeros_like(dg_ref)
            sq_ref[...] = jnp.zeros_like(sq_ref)

        dg_ref[...] += jnp.sum(dyv * xhat, axis=0, keepdims=True)
        sq_ref[...] += jnp.sum(err * err, axis=0, keepdims=True)

        @pl.when(i == nsteps - 1)
        def _():
            tot = jnp.sum(sq_ref[...], axis=-1, keepdims=True) * (0.5 / d)
            loss_ref[...] = jnp.broadcast_to(tot, loss_ref.shape)

    row_spec = pl.BlockSpec((tm, d), lambda i: (i, 0))
    vec_spec = pl.BlockSpec((1, d), lambda i: (0, 0))
    return pl.pallas_call(
        body, name=name, grid=(nsteps,),
        in_specs=[row_spec, row_spec, vec_spec],
        out_specs=[row_spec, pl.BlockSpec((1, LANES), lambda i: (0, 0)), vec_spec],
        out_shape=[jax.ShapeDtypeStruct((rows, d), F32), jax.ShapeDtypeStruct((1, LANES), F32),
                   jax.ShapeDtypeStruct((1, d), F32)],
        scratch_shapes=[pltpu.VMEM((1, d), F32)],
        compiler_params=_params(("arbitrary",)),
    )(h, tgt, g)


def _to_scan_layout(v):
    lead = v.shape[:-2]
    v = v.reshape(lead + (2, N_STATES // SCAN_CB, SCAN_CB))
    v = jnp.swapaxes(v, -3, -2)
    return v.reshape(lead + (2 * N_STATES,))


def _ssm_matrices(lam_re, lam_im, log_dt, b_re, b_im, c_re, c_im):
    dt = jnp.exp(log_dt)[:, None]
    mag = jnp.exp(lam_re * dt)
    a_re, a_im = mag * jnp.cos(lam_im * dt), mag * jnp.sin(lam_im * dt)
    nr, ni = a_re - 1.0, a_im
    den = lam_re * lam_re + lam_im * lam_im
    coef_re = (nr * lam_re + ni * lam_im) / den
    coef_im = (ni * lam_re - nr * lam_im) / den
    bb_re = coef_re[..., None] * b_re - coef_im[..., None] * b_im
    bb_im = coef_re[..., None] * b_im + coef_im[..., None] * b_re
    a_lay = _to_scan_layout(jnp.stack([a_re.reshape(-1), a_im.reshape(-1)], axis=0))[None, :]
    nblk = SSM_GROUPS // SCAN_GROUPS
    eye = jnp.eye(SCAN_GROUPS, dtype=F32)

    def b_block(bb):
        bb = bb.reshape(nblk, SCAN_GROUPS, SSM_STATE, SSM_GROUP_SIZE)
        return jnp.einsum("gk,jkph->jghkp", eye, bb).reshape(nblk, SCAN_GROUPS * SSM_GROUP_SIZE, SCAN_CB)

    b_blk = jnp.concatenate([b_block(bb_re), b_block(bb_im)], axis=2)

    def c_block(cc):
        cc = cc.reshape(nblk, SCAN_GROUPS, SSM_GROUP_SIZE, SSM_STATE)
        return jnp.einsum("gk,jghp->jkpgh", eye, cc).reshape(nblk, SCAN_CB, SCAN_GROUPS * SSM_GROUP_SIZE)

    c_blk = jnp.concatenate([c_block(c_re), -c_block(c_im)], axis=1)
    return a_lay, b_blk, c_blk


def _interleave(v):
    rows, c = v.shape
    return v.reshape(SCAN_SEGS, rows // SCAN_SEGS, c).transpose(1, 0, 2).reshape(rows, c)


def _deinterleave(v):
    rows, c = v.shape
    return v.reshape(rows // SCAN_SEGS, SCAN_SEGS, c).transpose(1, 0, 2).reshape(rows, c)


def _scan_groups(a_ref, bu_ref, o_ref, state, *, reverse, tt):
    cb = SCAN_CB
    ar = jnp.broadcast_to(a_ref[:, :cb], (SCAN_SEGS, cb))
    ai = jnp.broadcast_to(a_ref[:, cb:], (SCAN_SEGS, cb))
    ngroups = tt // SCAN_SEGS

    def step(i, st):
        sr, si = st
        r0 = pl.multiple_of(((ngroups - 1 - i) if reverse else i) * SCAN_SEGS, SCAN_SEGS)
        blk = bu_ref[pl.ds(r0, SCAN_SEGS), :]
        nr = ar * sr - ai * si + blk[:, :cb]
        ni = ar * si + ai * sr + blk[:, cb:]
        if o_ref is not None:
            o_ref[pl.ds(r0, SCAN_SEGS), :] = jnp.concatenate([nr, ni], axis=1)
        return nr, ni

    return lax.fori_loop(0, ngroups, step, state, unroll=4)


def _segment_entries(a_ref, e_ref, init_ref, *, reverse, seg_len):
    cb = SCAN_CB
    n_sq = seg_len.bit_length() - 1
    assert 1 << n_sq == seg_len, seg_len
    pr, pi = a_ref[:, :cb], a_ref[:, cb:]
    for _ in range(n_sq):
        pr, pi = pr * pr - pi * pi, 2.0 * pr * pi
    cr = jnp.zeros((1, cb), F32)
    ci = jnp.zeros((1, cb), F32)
    order = range(SCAN_SEGS - 1, -1, -1) if reverse else range(SCAN_SEGS)
    for k, seg in enumerate(order):
        if k > 0:
            prev = seg + 1 if reverse else seg - 1
            er, ei = e_ref[prev:prev + 1, :cb], e_ref[prev:prev + 1, cb:]
            cr, ci = pr * cr - pi * ci + er, pr * ci + pi * cr + ei
        init_ref[seg:seg + 1, :] = jnp.concatenate([cr, ci], axis=1)


def _ssm_specs(nt, tt, nch, reverse):
    cb = SCAN_CB
    tmap = (lambda j, kk: (nt - 1 - kk, j)) if reverse else (lambda j, kk: (kk, j))
    return dict(a=pl.BlockSpec((1, 2 * cb), lambda j, kk: (0, j)),
                seg=pl.BlockSpec((SCAN_SEGS, 2 * cb), lambda j, kk: (0, j)),
                chan=pl.BlockSpec((tt, nch), tmap),
                state=pl.BlockSpec((tt, 2 * cb), tmap),
                b=pl.BlockSpec((None, nch, 2 * cb), lambda j, kk: (j, 0, 0)),
                c=pl.BlockSpec((None, 2 * cb, nch), lambda j, kk: (j, 0, 0)))


def _ssm_ends(a_lay, x, blocks, *, transpose, reverse, tt, name):
    rows = x.shape[0]
    nblk = blocks.shape[0]
    nch = x.shape[1] // nblk
    cb = SCAN_CB
    nt = rows // tt
    sp = _ssm_specs(nt, tt, nch, reverse)

    def body(a_ref, x_ref, w_ref, e_ref, bu_ref):
        kk = pl.program_id(1)

        @pl.when(kk == 0)
        def _():
            e_ref[...] = jnp.zeros_like(e_ref)

        bu_ref[...] = _dot(x_ref[...].astype(BF16), w_ref[...].astype(BF16), 1, 1 if transpose else 0)
        sr, si = _scan_groups(a_ref, bu_ref, None, (e_ref[:, :cb], e_ref[:, cb:]), reverse=reverse, tt=tt)
        e_ref[...] = jnp.concatenate([sr, si], axis=1)

    return pl.pallas_call(
        body, name=name, grid=(nblk, nt),
        in_specs=[sp["a"], sp["chan"], sp["c"] if transpose else sp["b"]],
        out_specs=sp["seg"],
        out_shape=jax.ShapeDtypeStruct((SCAN_SEGS, nblk * 2 * cb), F32),
        scratch_shapes=[pltpu.VMEM((tt, 2 * cb), F32)],
        compiler_params=_params(("parallel", "arbitrary")),
    )(a_lay, x, blocks)


def _ssm_fwd(a_lay, u, b_blk, c_blk, ends, *, tt, name):
    rows = u.shape[0]
    nblk = b_blk.shape[0]
    nch = u.shape[1] // nblk
    cb = SCAN_CB
    nt = rows // tt
    sp = _ssm_specs(nt, tt, nch, False)

    def body(a_ref, e_ref, u_ref, b_ref, c_ref, s_ref, y_ref, init_ref, carry_ref):
        kk = pl.program_id(1)

        @pl.when(kk == 0)
        def _():
            _segment_entries(a_ref, e_ref, init_ref, reverse=False, seg_len=rows // SCAN_SEGS)
            carry_ref[...] = init_ref[...]

        s_ref[...] = _dot(u_ref[...].astype(BF16), b_ref[...].astype(BF16), 1, 0)
        sr, si = _scan_groups(a_ref, s_ref, s_ref, (carry_ref[:, :cb], carry_ref[:, cb:]), reverse=False, tt=tt)
        carry_ref[...] = jnp.concatenate([sr, si], axis=1)
        y_ref[...] = _dot(s_ref[...].astype(BF16), c_ref[...].astype(BF16), 1, 0)

    return pl.pallas_call(
        body, name=name, grid=(nblk, nt),
        in_specs=[sp["a"], sp["seg"], sp["chan"], sp["b"], sp["c"]],
        out_specs=[sp["state"], sp["chan"], sp["seg"]],
        out_shape=[jax.ShapeDtypeStruct((rows, nblk * 2 * cb), F32), jax.ShapeDtypeStruct((rows, nblk * nch), F32),
                   jax.ShapeDtypeStruct((SCAN_SEGS, nblk * 2 * cb), F32)],
        scratch_shapes=[pltpu.VMEM((SCAN_SEGS, 2 * cb), F32)],
        compiler_params=_params(("parallel", "arbitrary")),
    )(a_lay, ends, u, b_blk, c_blk)


def _ssm_bwd(a_conj, dy, u, s, s_entry, b_blk, c_blk, dd, ends, *, tt, name):
    rows = u.shape[0]
    nblk = b_blk.shape[0]
    nch = u.shape[1] // nblk
    cb = SCAN_CB
    nt = rows // tt
    sp = _ssm_specs(nt, tt, nch, True)
    groups_per_tile = tt // SCAN_SEGS
    before = pl.BlockSpec((SCAN_SEGS, 2 * cb), lambda j, kk: (jnp.maximum((nt - 1 - kk) * groups_per_tile - 1, 0), j))

    def body(a_ref, e_ref, dy_ref, u_ref, s_ref, before_ref, entry_ref, b_ref, c_ref, dd_ref,
             du_ref, db_ref, dc_ref, da_ref, lam_ref, carry_ref):
        kk = pl.program_id(1)

        @pl.when(kk == 0)
        def _():
            _segment_entries(a_ref, e_ref, carry_ref, reverse=True, seg_len=rows // SCAN_SEGS)
            db_ref[...] = jnp.zeros_like(db_ref)
            dc_ref[...] = jnp.zeros_like(dc_ref)
            da_ref[...] = jnp.zeros_like(da_ref)

        dyv = dy_ref[...]
        dyb = dyv.astype(BF16)
        lam_ref[...] = _dot(dyb, c_ref[...].astype(BF16), 1, 1)
        lr, li = _scan_groups(a_ref, lam_ref, lam_ref, (carry_ref[:, :cb], carry_ref[:, cb:]), reverse=True, tt=tt)
        carry_ref[...] = jnp.concatenate([lr, li], axis=1)

        first = jnp.where(kk == nt - 1, entry_ref[...], before_ref[...])
        rest = tt - SCAN_SEGS
        lam_hi = lam_ref[pl.ds(SCAN_SEGS, rest), :]
        s_lo = s_ref[pl.ds(0, rest), :]
        lam_lo = lam_ref[pl.ds(0, SCAN_SEGS), :]

        def pair(lv, pv):
            lre, lim, pre, pim = lv[:, :cb], lv[:, cb:], pv[:, :cb], pv[:, cb:]
            return (jnp.sum(lre * pre + lim * pim, axis=0, keepdims=True),
                    jnp.sum(lim * pre - lre * pim, axis=0, keepdims=True))

        r1, i1 = pair(lam_hi, s_lo)
        r0, i0 = pair(lam_lo, first)
        da_ref[...] += jnp.concatenate([r1 + r0, i1 + i0], axis=1)

        lamb = lam_ref[...].astype(BF16)
        du_ref[...] = _dot(lamb, b_ref[...].astype(BF16), 1, 1) + dd_ref[...] * dyv
        db_ref[...] += _dot(u_ref[...].astype(BF16), lamb, 0, 0)
        dc_ref[...] += _dot(s_ref[...].astype(BF16), dyb, 0, 0)

    return pl.pallas_call(
        body, name=name, grid=(nblk, nt),
        in_specs=[sp["a"], sp["seg"], sp["chan"], sp["chan"], sp["state"], before, sp["seg"], sp["b"], sp["c"],
                  pl.BlockSpec((1, nch), lambda j, kk: (0, j))],
        out_specs=[sp["chan"], sp["b"], sp["c"], pl.BlockSpec((1, 2 * cb), lambda j, kk: (0, j))],
        out_shape=[jax.ShapeDtypeStruct((rows, nblk * nch), F32), jax.ShapeDtypeStruct(b_blk.shape, F32),
                   jax.ShapeDtypeStruct(c_blk.shape, F32), jax.ShapeDtypeStruct((1, nblk * 2 * cb), F32)],
        scratch_shapes=[pltpu.VMEM((tt, 2 * cb), F32), pltpu.VMEM((SCAN_SEGS, 2 * cb), F32)],
        compiler_params=_params(("parallel", "arbitrary")),
    )(a_conj, ends, dy, u, s, s, s_entry, b_blk, c_blk, dd)


def _glu_fwd(ys, u, dd, w_glu, b_glu, *, tm, name):
    rows, w = ys.shape

    def body(ys_ref, u_ref, dd_ref, w_ref, b_ref, y0_ref, t_ref, y2_ref):
        y0 = ys_ref[...] + dd_ref[...] * u_ref[...]
        y1 = _gelu(y0)
        t = _dot(y1.astype(BF16), w_ref[...], 1, 0) + b_ref[...]
        y0_ref[...] = y0
        t_ref[...] = t
        y2_ref[...] = (y1 * _sigmoid(t)).astype(BF16)

    row = pl.BlockSpec((tm, w), lambda i: (i, 0))
    vec = pl.BlockSpec((1, w), lambda i: (0, 0))
    return pl.pallas_call(
        body, name=name, grid=(rows // tm,),
        in_specs=[row, row, vec, pl.BlockSpec((w, w), lambda i: (0, 0)), vec],
        out_specs=[row, row, row],
        out_shape=[jax.ShapeDtypeStruct((rows, w), F32), jax.ShapeDtypeStruct((rows, w), F32),
                   jax.ShapeDtypeStruct((rows, w), BF16)],
        compiler_params=_params(("parallel",)),
    )(ys, u, dd, w_glu, b_glu)


def _glu_bwd(dy2, y0, t, u, w_glu, *, tm, name):
    rows, w = y0.shape

    def body(dy2_ref, y0_ref, t_ref, u_ref, w_ref, dy0_ref, dt_ref, y1_ref, db_ref, dd_ref):
        i = pl.program_id(0)
        y0 = y0_ref[...]
        y1 = _gelu(y0)
        sg = _sigmoid(t_ref[...])
        dy2v = dy2_ref[...]
        dt = dy2v * y1 * sg * (1.0 - sg)
        dy1 = dy2v * sg + _dot(dt.astype(BF16), w_ref[...], 1, 1)
        dy0 = dy1 * _gelu_grad(y0)
        dy0_ref[...] = dy0
        dt_ref[...] = dt.astype(BF16)
        y1_ref[...] = y1.astype(BF16)

        @pl.when(i == 0)
        def _():
            db_ref[...] = jnp.zeros_like(db_ref)
            dd_ref[...] = jnp.zeros_like(dd_ref)

        db_ref[...] += jnp.sum(dt, axis=0, keepdims=True)
        dd_ref[...] += jnp.sum(dy0 * u_ref[...], axis=0, keepdims=True)

    row = pl.BlockSpec((tm, w), lambda i: (i, 0))
    vec = pl.BlockSpec((1, w), lambda i: (0, 0))
    return pl.pallas_call(
        body, name=name, grid=(rows // tm,),
        in_specs=[row, row, row, row, pl.BlockSpec((w, w), lambda i: (0, 0))],
        out_specs=[row, row, row, vec, vec],
        out_shape=[jax.ShapeDtypeStruct((rows, w), F32), jax.ShapeDtypeStruct((rows, w), BF16),
                   jax.ShapeDtypeStruct((rows, w), BF16), jax.ShapeDtypeStruct((1, w), F32),
                   jax.ShapeDtypeStruct((1, w), F32)],
        compiler_params=_params(("arbitrary",)),
    )(dy2, y0, t, u, w_glu)


ATTN_TILE = 2048


def _attn_geometry(rows, d):
    sb = ATTN_Q * d
    tr = max(sb, min(ATTN_TILE, rows))
    assert rows % tr == 0 and tr % sb == 0, (rows, d)
    return sb, tr, rows // tr, tr // sb


def _attn_masks():
    qi = lax.broadcasted_iota(jnp.int32, (2 * ATTN_Q, 2 * ATTN_Q), 0) % ATTN_Q
    kj = lax.broadcasted_iota(jnp.int32, (2 * ATTN_Q, 2 * ATTN_Q), 1)
    own_ok = jnp.logical_and(kj >= ATTN_Q, kj - ATTN_Q <= qi)
    prev_ok = jnp.logical_and(kj < ATTN_Q, kj >= qi)
    bias_first = jnp.where(own_ok, 0.0, NEG_INF)
    bias_other = jnp.where(jnp.logical_or(own_ok, prev_ok), 0.0, NEG_INF)
    head0 = lax.broadcasted_iota(jnp.int32, (ATTN_Q, LANES), 1) < ATTN_HEAD_DIM
    return bias_first, bias_other, head0


def _attn_rows(base, n, d):
    return pl.ds(pl.multiple_of(base, ATTN_Q), n) if d == 1 else pl.ds(base, n, stride=d)


def _stack_heads(v, head0):
    return jnp.concatenate([jnp.where(head0, v, 0.0), jnp.where(head0, 0.0, v)], axis=0)


def _unstack_heads(v, head0):
    return jnp.where(head0, v[:ATTN_Q], v[ATTN_Q:])


def _fill_keys(buf, prev_ref, cur_ref, sb):
    buf[pl.ds(0, sb), :] = prev_ref[...]
    buf[pl.ds(sb, cur_ref.shape[0]), :] = cur_ref[...]


def _attn_fwd(qkv, g, d, *, name):
    rows = qkv.shape[0]
    sb, tr, ntiles, nsub = _attn_geometry(rows, d)
    qc, kc, vc = 2 * g, 6 + 2 * g, 12 + 2 * g
    scale = ATTN_HEAD_DIM ** -0.5

    def body(q_ref, kc_ref, kp_ref, vc_ref, vp_ref, o_ref, lse_ref, kbuf, vbuf):
        n = pl.program_id(0)
        _fill_keys(kbuf, kp_ref, kc_ref, sb)
        _fill_keys(vbuf, vp_ref, vc_ref, sb)
        bias_first, bias_other, head0 = _attn_masks()

        def per_block(idx, carry):
            j, r = idx // d, idx % d
            base = j * sb + r
            bias = jnp.where(jnp.logical_and(n == 0, j == 0), bias_first, bias_other)
            qrows = _attn_rows(base, ATTN_Q, d)
            krows = _attn_rows(base, 2 * ATTN_Q, d)
            qs = _stack_heads(q_ref[qrows, :], head0).astype(BF16)
            s = _dot(qs, kbuf[krows, :].astype(BF16), 1, 1) * scale + bias
            mx = jnp.max(s, axis=-1, keepdims=True)
            p = jnp.exp(s - mx)
            den = jnp.sum(p, axis=-1, keepdims=True)
            pv = _dot(p.astype(BF16), vbuf[krows, :].astype(BF16), 1, 0) / den
            o_ref[qrows, :] = _unstack_heads(pv, head0)
            lse_ref[qrows, :] = _unstack_heads(jnp.broadcast_to(mx + jnp.log(den), (2 * ATTN_Q, LANES)), head0)
            return carry

        lax.fori_loop(0, nsub * d, per_block, 0, unroll=8)

    def cur(col):
        return pl.BlockSpec((tr, LANES), lambda n, hp: (n, col + hp))

    def prev(col):
        return pl.BlockSpec((sb, LANES), lambda n, hp: (jnp.maximum(n * nsub - 1, 0), col + hp))

    out_spec = pl.BlockSpec((tr, LANES), lambda n, hp: (n, hp))
    return pl.pallas_call(
        body, name=name, grid=(ntiles, 2),
        in_specs=[cur(qc), cur(kc), prev(kc), cur(vc), prev(vc)],
        out_specs=[out_spec, out_spec],
        out_shape=[jax.ShapeDtypeStruct((rows, 2 * LANES), F32), jax.ShapeDtypeStruct((rows, 2 * LANES), F32)],
        scratch_shapes=[pltpu.VMEM((sb + tr, LANES), F32), pltpu.VMEM((sb + tr, LANES), F32)],
        compiler_params=_params(("parallel", "parallel")),
    )(qkv, qkv, qkv, qkv, qkv)


def _attn_merge(outs, lses, *, tm, name):
    rows, w = outs[0].shape

    def body(o0, o1, o2, l0, l1, l2, o_ref, lse_ref):
        a0, a1, a2 = l0[...], l1[...], l2[...]
        mx = jnp.maximum(jnp.maximum(a0, a1), a2)
        e0, e1, e2 = jnp.exp(a0 - mx), jnp.exp(a1 - mx), jnp.exp(a2 - mx)
        den = e0 + e1 + e2
        o_ref[...] = (e0 / den) * o0[...] + (e1 / den) * o1[...] + (e2 / den) * o2[...]
        lse_ref[...] = mx + jnp.log(den)

    row = pl.BlockSpec((tm, w), lambda i: (i, 0))
    return pl.pallas_call(
        body, name=name, grid=(rows // tm,), in_specs=[row] * 6, out_specs=[row, row],
        out_shape=[jax.ShapeDtypeStruct((rows, w), F32), jax.ShapeDtypeStruct((rows, w), F32)],
        compiler_params=_params(("parallel",)),
    )(*outs, *lses)


def _attn_bwd(qkv, do, o, lse, g, d, prev, *, name):
    rows = qkv.shape[0]
    sb, tr, ntiles, nsub = _attn_geometry(rows, d)
    qc, kc, vc = 2 * g, 6 + 2 * g, 12 + 2 * g
    scale = ATTN_HEAD_DIM ** -0.5

    def body(q_ref, kc_ref, kp_ref, vc_ref, vp_ref, do_ref, o_ref, lse_ref, dq_ref, dk_ref, dv_ref,
             kbuf, vbuf, dk_acc, dv_acc):
        n = pl.program_id(1)

        @pl.when(n == 0)
        def _():
            dk_acc[pl.ds(0, tr), :] = jnp.zeros((tr, LANES), F32)
            dv_acc[pl.ds(0, tr), :] = jnp.zeros((tr, LANES), F32)

        @pl.when(n < ntiles)
        def _():
            dk_acc[pl.ds(tr, tr), :] = jnp.zeros((tr, LANES), F32)
            dv_acc[pl.ds(tr, tr), :] = jnp.zeros((tr, LANES), F32)
            _fill_keys(kbuf, kp_ref, kc_ref, sb)
            _fill_keys(vbuf, vp_ref, vc_ref, sb)
            bias_first, bias_other, head0 = _attn_masks()
            lane = lax.broadcasted_iota(jnp.int32, (ATTN_Q, LANES), 1)

            def per_block(idx, carry):
                j, r = idx // d, idx % d
                base = j * sb + r
                bias = jnp.where(jnp.logical_and(n == 0, j == 0), bias_first, bias_other)
                qrows = _attn_rows(base, ATTN_Q, d)
                krows = _attn_rows(base, 2 * ATTN_Q, d)
                arows = _attn_rows(base + (tr - sb), 2 * ATTN_Q, d)
                qs = _stack_heads(q_ref[qrows, :], head0).astype(BF16)
                dos = _stack_heads(do_ref[qrows, :], head0)
                dosb = dos.astype(BF16)
                ov = o_ref[qrows, :]
                delta = jnp.sum(dos * jnp.concatenate([ov, ov], axis=0), axis=-1, keepdims=True)
                lsev = lse_ref[qrows, :]
                lse_s = jnp.concatenate(
                    [jnp.sum(jnp.where(lane == h * ATTN_HEAD_DIM, lsev, 0.0), axis=-1, keepdims=True) for h in range(2)], axis=0)
                kb = kbuf[krows, :].astype(BF16)
                vb = vbuf[krows, :].astype(BF16)
                p = jnp.exp(_dot(qs, kb, 1, 1) * scale + bias - lse_s)
                ds = (p * (_dot(dosb, vb, 1, 1) - delta) * scale).astype(BF16)
                dq_ref[qrows, :] = _unstack_heads(_dot(ds, kb, 1, 0), head0)
                dk_acc[arows, :] += _dot(ds, qs, 0, 0)
                dv_acc[arows, :] += _dot(p.astype(BF16), dosb, 0, 0)
                return carry

            lax.fori_loop(0, nsub * d, per_block, 0, unroll=4)

        dk_ref[...] = dk_acc[pl.ds(0, tr), :]
        dv_ref[...] = dv_acc[pl.ds(0, tr), :]
        dk_acc[pl.ds(0, tr), :] = dk_acc[pl.ds(tr, tr), :]
        dv_acc[pl.ds(0, tr), :] = dv_acc[pl.ds(tr, tr), :]

    def cur(n):
        return jnp.minimum(n, ntiles - 1)

    def spec(col, prev):
        if prev:
            return pl.BlockSpec((sb, LANES), lambda hp, n: (jnp.maximum(cur(n) * nsub - 1, 0), col + hp))
        return pl.BlockSpec((tr, LANES), lambda hp, n: (cur(n), col + hp))

    row_spec = pl.BlockSpec((tr, LANES), lambda hp, n: (cur(n), hp))
    dq_out = pl.BlockSpec((tr, LANES), lambda hp, n: (cur(n), 2 * g + hp))
    kv_out = pl.BlockSpec((tr, LANES), lambda hp, n: (jnp.maximum(n - 1, 0), 2 * g + hp))
    shape = jax.ShapeDtypeStruct((rows, len(ATTN_PATTERNS) * 2 * LANES), F32)
    ins = [qkv, qkv, qkv, qkv, qkv, do, o, lse]
    in_specs = [spec(qc, False), spec(kc, False), spec(kc, True), spec(vc, False), spec(vc, True),
                row_spec, row_spec, row_spec]
    aliases = {}
    if prev is not None:
        aliases = {len(ins) + t: t for t in range(3)}
        ins = ins + list(prev)
        in_specs = in_specs + [ANY] * 3
    n_in = len(ins)

    def entry(*refs):
        body(*refs[:8], *refs[n_in:])

    return pl.pallas_call(
        entry, name=name, grid=(2, ntiles + 1),
        in_specs=in_specs,
        out_specs=[dq_out, kv_out, kv_out],
        out_shape=[shape, shape, shape],
        input_output_aliases=aliases,
        scratch_shapes=[pltpu.VMEM((sb + tr, LANES), F32), pltpu.VMEM((sb + tr, LANES), F32),
                        pltpu.VMEM((2 * tr, LANES), F32), pltpu.VMEM((2 * tr, LANES), F32)],
        compiler_params=_params(("parallel", "arbitrary")),
    )(*ins)


def _mem_probs(q, k):
    s = _dot(q.astype(BF16), k.astype(BF16), 1, 1) * (MEM_HEAD_DIM ** -0.5)
    e = jnp.exp(s - jnp.max(s, axis=-1, keepdims=True))
    return e / jnp.sum(e, axis=-1, keepdims=True)


def _mem_attn_fwd(mq, kv, *, tq, name):
    rows = mq.shape[0]

    def body(q_ref, k_ref, v_ref, o_ref):
        p = _mem_probs(q_ref[...], k_ref[...])
        o_ref[...] = _dot(p.astype(BF16), v_ref[...].astype(BF16), 1, 0)

    return pl.pallas_call(
        body, name=name, grid=(rows // tq, MEM_HEADS),
        in_specs=[pl.BlockSpec((tq, LANES), lambda i, h: (i, h)),
                  pl.BlockSpec((MEM_LEN, LANES), lambda i, h: (0, h)),
                  pl.BlockSpec((MEM_LEN, LANES), lambda i, h: (0, MEM_HEADS + h))],
        out_specs=pl.BlockSpec((tq, LANES), lambda i, h: (i, h)),
        out_shape=jax.ShapeDtypeStruct((rows, MEM_HEADS * LANES), F32),
        compiler_params=_params(("parallel", "parallel")),
    )(mq, kv, kv)


def _mem_attn_bwd(mq, kv, dmo, *, tq, name):
    rows = mq.shape[0]
    scale = MEM_HEAD_DIM ** -0.5

    def body(q_ref, k_ref, v_ref, do_ref, dq_ref, dk_ref, dv_ref):
        i = pl.program_id(1)
        qb = q_ref[...].astype(BF16)
        kb = k_ref[...].astype(BF16)
        vb = v_ref[...].astype(BF16)
        dob = do_ref[...].astype(BF16)
        p = _mem_probs(q_ref[...], k_ref[...])
        dp = _dot(dob, vb, 1, 1)
        ds = (p * (dp - jnp.sum(p * dp, axis=-1, keepdims=True)) * scale).astype(BF16)
        dq_ref[...] = _dot(ds, kb, 1, 0).astype(dq_ref.dtype)

        @pl.when(i == 0)
        def _():
            dk_ref[...] = jnp.zeros_like(dk_ref)
            dv_ref[...] = jnp.zeros_like(dv_ref)

        dk_ref[...] += _dot(ds, qb, 0, 0)
        dv_ref[...] += _dot(p.astype(BF16), dob, 0, 0)

    kv_out = pl.BlockSpec((MEM_LEN, LANES), lambda h, i: (0, h))
    kv_shape = jax.ShapeDtypeStruct((MEM_LEN, MEM_HEADS * LANES), F32)
    return pl.pallas_call(
        body, name=name, grid=(MEM_HEADS, rows // tq),
        in_specs=[pl.BlockSpec((tq, LANES), lambda h, i: (i, h)),
                  pl.BlockSpec((MEM_LEN, LANES), lambda h, i: (0, h)),
                  pl.BlockSpec((MEM_LEN, LANES), lambda h, i: (0, MEM_HEADS + h)),
                  pl.BlockSpec((tq, LANES), lambda h, i: (i, h))],
        out_specs=[pl.BlockSpec((tq, LANES), lambda h, i: (i, h)), kv_out, kv_out],
        out_shape=[jax.ShapeDtypeStruct((rows, MEM_HEADS * LANES), BF16), kv_shape, kv_shape],
        compiler_params=_params(("parallel", "arbitrary")),
    )(mq, kv, kv, dmo)


def _resident(shape):
    return pl.BlockSpec(shape, lambda i: (0, 0), pipeline_mode=pl.Buffered(1))


def _branch_merge_fwd(acts, wts, zg, b_gate, *, tm, name):
    rows = zg.shape[0]
    d = wts[0].shape[0]

    def body(s_ref, a_ref, m_ref, ws_ref, wa_ref, wm_ref, zg_ref, b_ref, o_ref):
        gt = _sigmoid(zg_ref[...] + b_ref[...])
        acc = None
        for k, (x_ref, w_ref) in enumerate(((s_ref, ws_ref), (a_ref, wa_ref), (m_ref, wm_ref))):
            term = gt[:, k * d:(k + 1) * d] * _dot(x_ref[...].astype(BF16), w_ref[...], 1, 1)
            acc = term if acc is None else acc + term
        o_ref[...] = acc.astype(BF16)

    return pl.pallas_call(
        body, name=name, grid=(rows // tm,),
        in_specs=[pl.BlockSpec((tm, x.shape[1]), lambda i: (i, 0)) for x in acts] + [_resident(w.shape) for w in wts]
        + [pl.BlockSpec((tm, 3 * d), lambda i: (i, 0)), pl.BlockSpec((1, 3 * d), lambda i: (0, 0))],
        out_specs=pl.BlockSpec((tm, d), lambda i: (i, 0)), out_shape=jax.ShapeDtypeStruct((rows, d), BF16),
        compiler_params=_params(("parallel",)),
    )(*acts, *wts, zg, b_gate)


def _branch_merge_bwd(dmerged, acts, wts, zg, b_gate, *, tm, name):
    rows = zg.shape[0]
    d = wts[0].shape[0]

    def body(dm_ref, s_ref, a_ref, m_ref, ws_ref, wa_ref, wm_ref, zg_ref, b_ref,
             ds_ref, da_ref, dmm_ref, dws_ref, dwa_ref, dwm_ref, dzg_ref, db_ref):
        i = pl.program_id(0)

        @pl.when(i == 0)
        def _():
            for r in (dws_ref, dwa_ref, dwm_ref, db_ref):
                r[...] = jnp.zeros_like(r)

        gt = _sigmoid(zg_ref[...] + b_ref[...])
        dm = dm_ref[...]
        groups = ((s_ref, ws_ref, ds_ref, dws_ref), (a_ref, wa_ref, da_ref, dwa_ref), (m_ref, wm_ref, dmm_ref, dwm_ref))
        for k, (x_ref, w_ref, dx_ref, dw_ref) in enumerate(groups):
            cs = pl.ds(k * d, d)
            gk = gt[:, k * d:(k + 1) * d]
            xb = x_ref[...].astype(BF16)
            br = _dot(xb, w_ref[...], 1, 1)
            dbr = (dm * gk).astype(BF16)
            dx_ref[...] = _dot(dbr, w_ref[...], 1, 0)
            dw_ref[...] += _dot(dbr, xb, 0, 0)
            dzg = dm * br * gk * (1.0 - gk)
            dzg_ref[:, cs] = dzg.astype(BF16)
            db_ref[:, cs] += jnp.sum(dzg, axis=0, keepdims=True)

    row = lambda w: pl.BlockSpec((tm, w), lambda i: (i, 0))
    whole = lambda shape: pl.BlockSpec(shape, lambda i: (0, 0))
    return pl.pallas_call(
        body, name=name, grid=(rows // tm,),
        in_specs=[row(d)] + [row(x.shape[1]) for x in acts] + [_resident(w.shape) for w in wts] + [row(3 * d), whole((1, 3 * d))],
        out_specs=[row(x.shape[1]) for x in acts] + [whole(w.shape) for w in wts] + [row(3 * d), whole((1, 3 * d))],
        out_shape=[jax.ShapeDtypeStruct(x.shape, F32) for x in acts] + [jax.ShapeDtypeStruct(w.shape, F32) for w in wts]
        + [jax.ShapeDtypeStruct((rows, 3 * d), BF16), jax.ShapeDtypeStruct((1, 3 * d), F32)],
        compiler_params=_params(("arbitrary",)),
    )(dmerged, *acts, *wts, zg, b_gate)


def _adamw(w, g, m, v, *, tr, name):
    rows, cols = w.shape[-2:]
    assert rows % tr == 0, (name, rows, tr)

    def body(w_ref, g_ref, m_ref, v_ref, g_out, d_ref, nm_ref, nv_ref):
        gv = g_ref[...]
        m2 = ADAM_B1 * m_ref[...] + (1.0 - ADAM_B1) * gv
        v2 = ADAM_B2 * v_ref[...] + (1.0 - ADAM_B2) * (gv * gv)
        m_hat = m2 / (1.0 - ADAM_B1 ** ADAM_STEP)
        v_hat = v2 / (1.0 - ADAM_B2 ** ADAM_STEP)
        g_out[...] = gv
        d_ref[...] = -ADAM_LR * (m_hat / (jnp.sqrt(v_hat) + ADAM_EPS) + ADAM_WD * w_ref[...])
        nm_ref[...] = m2
        nv_ref[...] = v2

    flat = pl.BlockSpec((tr, cols), lambda i: (i, 0))
    blk = flat if w.ndim == 2 else pl.BlockSpec((None, tr, cols), lambda i: (0, i, 0))
    shape = jax.ShapeDtypeStruct(w.shape, F32)
    return pl.pallas_call(
        body, name=name, grid=(rows // tr,), in_specs=[blk, flat, blk, blk], out_specs=[blk] * 4,
        out_shape=[shape] * 4, compiler_params=_params(("parallel",)),
    )(w, g, m, v)


ANY = pl.BlockSpec(memory_space=pl.ANY)


def _position():
    return lax.axis_index("x"), lax.axis_index("y"), lax.axis_index("c")


def _other_chips(x, y):
    return ((1 - x, y), (x, 1 - y), (1 - x, 1 - y))


def _remote(src, dst, send_sem, recv_sem, dev):
    return pltpu.make_async_remote_copy(src_ref=src, dst_ref=dst, send_sem=send_sem, recv_sem=recv_sem,
                                        device_id=dev, device_id_type=MESH)


def _gather_exchange(shards):
    nb = len(shards)

    def rows_of(i, owner, core):
        rs = shards[i].shape[0]
        return pl.ds(pl.multiple_of(owner * rs + core * (rs // 2), 16), rs // 2)

    def first_leg(ins, outs, send_sems, recv_sems, i, j):
        x, y, c = _position()
        px, py = _other_chips(x, y)[j]
        half = shards[i].shape[0] // 2
        mine = ins[i].at[pl.ds(pl.multiple_of(c * half, 16), half)]
        return _remote(mine, outs[i].at[rows_of(i, 2 * x + y, c)], send_sems.at[i, j], recv_sems.at[i, j], (px, py, c))

    def passed_on(outs, send_sems, recv_sems, i, j, core):
        x, y, c = _position()
        px, py = _other_chips(x, y)[j]
        rows = outs[i].at[rows_of(i, 2 * px + py, core)]
        return _remote(rows, rows, send_sems.at[i, 3 + j], recv_sems.at[i, 3 + j], (x, y, 1 - c))

    def own_block(ins, outs, send_sems, recv_sems, i):
        x, y, c = _position()
        rs = shards[i].shape[0]
        place = outs[i].at[pl.ds(pl.multiple_of((2 * x + y) * rs, 16), rs)]
        return _remote(ins[i], place, send_sems.at[i, 6], recv_sems.at[i, 6], (x, y, 1 - c))

    def start(ins, outs, send_sems, recv_sems):
        for i in range(nb):
            own_block(ins, outs, send_sems, recv_sems, i).start()
            for j in range(3):
                first_leg(ins, outs, send_sems, recv_sems, i, j).start()

    def finish(ins, outs, send_sems, recv_sems):
        x, y, c = _position()
        for i in range(nb):
            for j, (px, py) in enumerate(_other_chips(x, y)):
                landed = outs[i].at[rows_of(i, 2 * px + py, c)]
                _remote(landed, landed, send_sems.at[i, j], recv_sems.at[i, j], (px, py, c)).wait_recv()
                passed_on(outs, send_sems, recv_sems, i, j, c).start()
        for i in range(nb):
            own_block(ins, outs, send_sems, recv_sems, i).wait()
            for j in range(3):
                passed_on(outs, send_sems, recv_sems, i, j, 1 - c).wait_recv()
        for i in range(nb):
            for j in range(3):
                first_leg(ins, outs, send_sems, recv_sems, i, j).wait_send()
                passed_on(outs, send_sems, recv_sems, i, j, c).wait_send()

    return _Exchange(ins=list(shards), outs=[jax.ShapeDtypeStruct((N_CHIPS * s.shape[0], s.shape[1]), s.dtype) for s in shards],
                     aliases={}, sems=[(nb, 7), (nb, 7)], start=start, finish=finish)


def _run_exchange(ex, *, name):
    n_in, n_out = len(ex.ins), len(ex.outs)

    def body(*refs):
        c_in, c_out, sems = refs[:n_in], refs[n_in:n_in + n_out], refs[n_in + n_out:]
        ex.start(c_in, c_out, *sems)
        ex.finish(c_in, c_out, *sems)

    return pl.pallas_call(
        body, name=name, in_specs=[ANY] * n_in, out_specs=[ANY] * n_out, out_shape=list(ex.outs),
        input_output_aliases=dict(ex.aliases),
        scratch_shapes=[pltpu.SemaphoreType.DMA(s) for s in ex.sems],
    )(*ex.ins)


def _row_tile(rows):
    return max(t for t in range(16, min(rows, 512) + 1, 16) if rows % t == 0)


def _exchange_halves(grads, *, name):
    nb = len(grads)

    def body(*refs):
        ins, outs = refs[:nb], refs[nb:2 * nb]
        send_sems, recv_sems = refs[2 * nb:]
        x, y, c = _position()
        copies = []
        for i in range(nb):
            cp = _remote(ins[i].at[:, 1 - c], outs[i], send_sems.at[i], recv_sems.at[i], (x, y, 1 - c))
            cp.start()
            copies.append(cp)
        for cp in copies:
            cp.wait()

    return pl.pallas_call(
        body, name=name, in_specs=[ANY] * nb, out_specs=[ANY] * nb,
        out_shape=[jax.ShapeDtypeStruct((N_CHIPS, g.shape[2], g.shape[3]), F32) for g in grads],
        scratch_shapes=[pltpu.SemaphoreType.DMA((nb,)), pltpu.SemaphoreType.DMA((nb,))],
    )(*grads)


def _pair_sum(g4, got, c_arr, *, name):
    _, _, half, cols = g4.shape
    tr = _row_tile(half)

    def body(c_ref, g_ref, t_ref, p_ref, pb_ref):
        sm = g_ref[...] + t_ref[...]
        p_ref[...] = sm
        pb_ref[...] = sm.astype(BF16)

    blk = pl.BlockSpec((None, tr, cols), lambda j, i, c_ref: (j, i, 0))
    grid_spec = pltpu.PrefetchScalarGridSpec(
        num_scalar_prefetch=1, grid=(N_CHIPS, half // tr),
        in_specs=[pl.BlockSpec((None, None, tr, cols), lambda j, i, c_ref: (j, c_ref[0], i, 0)), blk],
        out_specs=[blk, blk])
    return pl.pallas_call(
        body, name=name, grid_spec=grid_spec,
        out_shape=[jax.ShapeDtypeStruct((N_CHIPS, half, cols), F32), jax.ShapeDtypeStruct((N_CHIPS, half, cols), BF16)],
        compiler_params=_params(("parallel", "parallel")),
    )(c_arr, g4, got)


def _scatter_exchange(parts):
    nb = len(parts)

    def copies(ins, outs, send_sems, recv_sems):
        x, y, c = _position()
        return [_remote(ins[i].at[2 * px + py], outs[i].at[j], send_sems.at[i, j], recv_sems.at[i, j], (px, py, c))
                for i in range(nb) for j, (px, py) in enumerate(_other_chips(x, y))]

    def start(ins, outs, send_sems, recv_sems):
        for cp in copies(ins, outs, send_sems, recv_sems):
            cp.start()

    def finish(ins, outs, send_sems, recv_sems):
        for cp in copies(ins, outs, send_sems, recv_sems):
            cp.wait()

    return _Exchange(ins=list(parts), outs=[jax.ShapeDtypeStruct((3,) + p.shape[1:], p.dtype) for p in parts],
                     aliases={}, sems=[(nb, 3), (nb, 3)], start=start, finish=finish)


def _owner_sum(p, got, chip_arr, c_arr, *, replicated, name):
    _, half, cols = p.shape
    tr = _row_tile(half)

    def body(chip_ref, c_ref, p_ref, r_ref, o_ref):
        o_ref[...] = ((p_ref[...] + r_ref[0].astype(F32)) + r_ref[1].astype(F32)) + r_ref[2].astype(F32)

    if replicated:
        out_spec = pl.BlockSpec((None, None, tr, cols), lambda i, chip_ref, c_ref: (chip_ref[0], c_ref[0], i, 0))
        out_shape = jax.ShapeDtypeStruct((N_CHIPS, 2, half, cols), F32)
    else:
        out_spec = pl.BlockSpec((None, tr, cols), lambda i, chip_ref, c_ref: (c_ref[0], i, 0))
        out_shape = jax.ShapeDtypeStruct((2, half, cols), F32)
    grid_spec = pltpu.PrefetchScalarGridSpec(
        num_scalar_prefetch=2, grid=(half // tr,),
        in_specs=[pl.BlockSpec((None, tr, cols), lambda i, chip_ref, c_ref: (chip_ref[0], i, 0)),
                  pl.BlockSpec((3, tr, cols), lambda i, chip_ref, c_ref: (0, i, 0))],
        out_specs=out_spec)
    return pl.pallas_call(
        body, name=name, grid_spec=grid_spec, out_shape=out_shape,
        compiler_params=_params(("parallel",)),
    )(chip_arr, c_arr, p, got)


def _share_reduced(bufs):
    nb = len(bufs) - 1

    def body(*refs):
        outs = refs[nb + 1:2 * nb + 2]
        send_sems, recv_sems = refs[2 * nb + 2:]
        x, y, c = _position()
        chip = 2 * x + y
        sends = []
        for i in range(nb):
            cp = _remote(outs[i].at[c], outs[i].at[c], send_sems.at[i], recv_sems.at[i], (x, y, 1 - c))
            cp.start()
            sends.append(cp)
        small = outs[nb]
        peers = [(fx, fy, fc) for fx in (0, 1) for fy in (0, 1) for fc in (0, 1) if fx + fy + fc > 0]
        for k, (fx, fy, fc) in enumerate(peers):
            dev = (x ^ fx, y ^ fy, c ^ fc)
            cp = _remote(small.at[chip, c], small.at[chip, c], send_sems.at[nb + k], recv_sems.at[nb + k], dev)
            cp.start()
            sends.append(cp)
        for i in range(nb):
            dst = outs[i].at[1 - c]
            _remote(dst, dst, send_sems.at[i], recv_sems.at[i], (x, y, 1 - c)).wait_recv()
        for k, (fx, fy, fc) in enumerate(peers):
            dst = small.at[2 * (x ^ fx) + (y ^ fy), c ^ fc]
            _remote(dst, dst, send_sems.at[nb + k], recv_sems.at[nb + k], (x ^ fx, y ^ fy, c ^ fc)).wait_recv()
        for cp in sends:
            cp.wait_send()

    n_all = nb + 1
    return pl.pallas_call(
        body, name="grad_share_reduced", in_specs=[ANY] * n_all, out_specs=[ANY] * n_all,
        out_shape=[jax.ShapeDtypeStruct(b.shape, b.dtype) for b in bufs],
        input_output_aliases={i: i for i in range(n_all)},
        scratch_shapes=[pltpu.SemaphoreType.DMA((nb + 7,)), pltpu.SemaphoreType.DMA((nb + 7,))],
    )(*bufs)


class _GradReducer:
    def __init__(self, c_arr, chip_arr):
        self.c_arr, self.chip_arr = c_arr, chip_arr
        self.pairs, self.landed = {}, {}

    def _pair_sums(self, names, grads):
        full = [g.reshape(N_CHIPS, 2, g.shape[0] // (2 * N_CHIPS), g.shape[1]) for g in grads]
        got = _exchange_halves(full, name="grad_exchange_" + names[0])
        for n, g, t in zip(names, full, got):
            self.pairs[n] = _pair_sum(g, t, self.c_arr, name="grad_pair_sum_" + n)

    def scatter(self, names, grads):
        self._pair_sums(names, grads)
        return _scatter_exchange([self.pairs[n][1] for n in names])

    def collect(self, names, bufs):
        self.landed.update(zip(names, bufs))

    def finish(self, names, grads, order):
        self.collect(names, _run_exchange(self.scatter(names, grads), name="grad_scatter_" + names[0]))
        totals = [_owner_sum(self.pairs[n][0], self.landed[n], self.chip_arr, self.c_arr, replicated=(n == order[-1]),
                             name="grad_owner_sum_" + n) for n in order]
        return _share_reduced(totals)


def _pack_small(vals):
    flat = jnp.concatenate([vals[name].reshape(-1) for name, _ in SMALL])
    return jnp.pad(flat, (0, N_CHIPS * SMALL_ROWS * 1024 - SMALL_ELEMS)).reshape(N_CHIPS * SMALL_ROWS, 1024)


def _unpack_small(buf):
    flat = buf.reshape(-1)
    out, off = {}, 0
    for name, shape in SMALL:
        n = int(np.prod(shape))
        out[name] = flat[off:off + n].reshape(shape)
        off += n
    return out


EARLY_REDUCED = (("w_down",), ("w_up",), ("w_o", "w_ssm_br", "w_attn_br", "w_mem_br", "w_glu", "w_mem_kv"), ("w_in",))


def _device_step(x, mem, tgt, w, p, *, shards, reducer):
    rows = x.shape[0]
    w = dict(w)
    early = EARLY_REDUCED
    gb = {}
    gather_pending = shards is not None

    def reducing(names):
        return reducer.scatter(names, [gb[n] for n in names]) if (reducer is not None and names) else None

    def reduced(names, res):
        if reducer is None or not names:
            return res
        reducer.collect(names, res[1])
        return res[0]

    def fetching(names):
        return _gather_exchange([shards[n] for n in names]) if gather_pending else None

    def fetched(names, res):
        if not gather_pending:
            return res
        w.update(zip(names, res[1]))
        return res[0]

    first_use = (("w_glu", "w_ssm_br", "w_attn_br", "w_mem_kv", "w_mem_br", "w_o"), ("w_up",), ("w_down",))
    g1, gm, g2 = p["norm1_g"], p["mem_norm_g"], p["norm2_g"]
    gf = p["final_g"].reshape(1, D_MODEL)
    ssm_args = (p["ssm_lambda_re"][0], p["ssm_lambda_im"][0], p["ssm_log_dt"][0], p["ssm_b_re"][0],
                p["ssm_b_im"][0], p["ssm_c_re"][0], p["ssm_c_im"][0])
    (a_lay, b_blk, c_blk), ssm_vjp = jax.vjp(_ssm_matrices, *ssm_args)
    a_conj = a_lay * _to_scan_layout(jnp.stack([jnp.ones((N_STATES,), F32), -jnp.ones((N_STATES,), F32)]))[None, :]
    dd = p["ssm_d"].reshape(1, SSM_WIDTH)
    win_t = w["w_in"]
    mm = _matmul

    n1 = _rmsnorm_fwd(x, g1, tm=512, name="norm1")
    u = mm(n1, win_t, m=rows, n=512, k=1024, tb=True, tm=2048, tn=512, tk=1024, out_dtypes=(F32,), name="in_u")
    qkv = fetched(first_use[0], mm(n1, win_t, m=rows, n=2304, k=1024, tb=True, tm=2048, tn=256, tk=1024,
                                   b_off=(OFF_QKV // 256, 0), out_dtypes=(F32,), carry=fetching(first_use[0]), name="in_qkv"))
    mq = mm(n1, win_t, m=rows, n=512, k=1024, tb=True, tm=2048, tn=256, tk=1024, b_off=(OFF_MQ // 256, 0),
            out_dtypes=(F32,), name="in_mq")
    zg = fetched(first_use[1], mm(n1, win_t, m=rows, n=3072, k=1024, tb=True, tm=2048, tn=256, tk=1024,
                                  b_off=(OFF_ZG // 256, 0), out_dtypes=(F32,), carry=fetching(first_use[1]), name="in_zg"))

    u_i = _interleave(u)
    ends = _ssm_ends(a_lay, u_i, b_blk, transpose=False, reverse=False, tt=512, name="ssm_fwd_ends")
    s, ys_i, s_entry = _ssm_fwd(a_lay, u_i, b_blk, c_blk, ends, tt=512, name="ssm_fwd")
    ys = _deinterleave(ys_i)
    y0, tglu, y2 = _glu_fwd(ys, u, dd, w["w_glu"], p["b_glu"], tm=512, name="glu_fwd")

    outs, lses = [], []
    for g, (_, d) in enumerate(ATTN_PATTERNS):
        o_g, lse_g = _attn_fwd(qkv, g, d, name=f"attn_fwd_{g}")
        outs.append(o_g)
        lses.append(lse_g)
    o, lse = _attn_merge(outs, lses, tm=1024, name="attn_merge")

    mn = _rmsnorm_fwd(mem, gm, tm=MEM_LEN, name="mem_norm")
    kv = mm(mn, w["w_mem_kv"], m=MEM_LEN, n=1024, k=1024, tm=MEM_LEN, tn=1024, tk=1024, out_dtypes=(F32,), name="mem_kv")
    mo = _mem_attn_fwd(mq, kv, tq=1024, name="mem_attn_fwd")

    branch_acts = (y2, o, mo)
    branch_wts = (w["w_ssm_br"], w["w_attn_br"], w["w_mem_br"])
    merged = _branch_merge_fwd(branch_acts, branch_wts, zg, p["b_gate"], tm=256, name="branch_merge_fwd")
    add = lambda acc, r: (acc + r,)
    h1 = mm(merged, w["w_o"], m=rows, n=1024, k=1024, tm=1024, tn=1024, tk=1024, out_dtypes=(F32,),
            aux=((x, "mn"),), epilogue=add, name="out_proj")
    n2 = _rmsnorm_fwd(h1, g2, tm=512, name="norm2")
    relu2 = lambda acc: (jnp.square(jnp.maximum(acc, 0.0)),)
    act = fetched(first_use[2], mm(n2, w["w_up"], m=rows, n=D_FF, k=1024, tb=True, tm=1024, tn=1024, tk=1024,
                                   out_dtypes=(BF16,), epilogue=relu2, carry=fetching(first_use[2]), name="mlp_up"))
    h2 = mm(act, w["w_down"], m=rows, n=1024, k=D_FF, tm=1024, tn=1024, tk=1024, out_dtypes=(F32,),
            aux=((h1, "mn"),), epilogue=add, name="mlp_down")
    dh2, loss, d_gf = _loss_head(h2, tgt, gf, tm=512, name="loss_head")

    gs = {"final_g": d_gf.reshape(D_MODEL)}
    drelu2 = lambda acc, actv: (acc * (2.0 * jnp.sqrt(actv.astype(F32))),)
    dup = mm(dh2, w["w_down"], m=rows, n=D_FF, k=1024, tb=True, tm=1024, tn=2048, tk=1024, out_dtypes=(BF16,),
             aux=((act, "mn"),), epilogue=drelu2, name="d_act")
    gb["w_down"] = mm(act, dh2, m=D_FF, n=1024, k=rows, ta=True, tm=1024, tn=1024, tk=1024, out_dtypes=(F32,), name="dw_down")
    gb["w_up"] = reduced(early[0], mm(dup, n2, m=D_FF, n=1024, k=rows, ta=True, tm=1024, tn=1024, tk=1024,
                                      out_dtypes=(F32,), carry=reducing(early[0]), name="dw_up"))
    dn2 = reduced(early[1], mm(dup, w["w_up"], m=rows, n=1024, k=D_FF, tm=1024, tn=1024, tk=1024, out_dtypes=(F32,),
                               carry=reducing(early[1]), name="d_n2"))
    dh1, gs["norm2_g"] = _rmsnorm_bwd(h1, g2, dn2, dh2, tm=512, name="norm2_bwd")
    dmerged = mm(dh1, w["w_o"], m=rows, n=1024, k=1024, tb=True, tm=1024, tn=1024, tk=1024, out_dtypes=(F32,), name="d_merged")
    gb["w_o"] = mm(merged, dh1, m=1024, n=1024, k=rows, ta=True, tm=1024, tn=1024, tk=1024, out_dtypes=(F32,), name="dw_o")
    (dy2, do, dmo, gb["w_ssm_br"], gb["w_attn_br"], gb["w_mem_br"], dzg, gs["b_gate"]) = _branch_merge_bwd(
        dmerged, branch_acts, branch_wts, zg, p["b_gate"], tm=256, name="branch_merge_bwd")

    dy0, dt, y1, gs["b_glu"], d_dd = _glu_bwd(dy2, y0, tglu, u, w["w_glu"], tm=512, name="glu_bwd")
    gs["ssm_d"] = d_dd.reshape(1, SSM_GROUPS, SSM_GROUP_SIZE)
    gb["w_glu"] = mm(y1, dt, m=512, n=512, k=rows, ta=True, tm=512, tn=512, tk=1024, out_dtypes=(F32,), name="dw_glu")
    dy0_i = _interleave(dy0)
    lam_ends = _ssm_ends(a_conj, dy0_i, c_blk, transpose=True, reverse=True, tt=512, name="ssm_bwd_ends")
    du_i, d_b_blk, d_c_blk, d_a_lay = _ssm_bwd(a_conj, dy0_i, u_i, s, s_entry, b_blk, c_blk, dd, lam_ends, tt=512,
                                                name="ssm_bwd")
    du = _deinterleave(du_i)
    d_ssm = ssm_vjp((d_a_lay, d_b_blk, d_c_blk))
    for name, val in zip(("ssm_lambda_re", "ssm_lambda_im", "ssm_log_dt", "ssm_b_re", "ssm_b_im", "ssm_c_re", "ssm_c_im"), d_ssm):
        gs[name] = val[None]

    dqkv = None
    for g, (_, d) in enumerate(ATTN_PATTERNS):
        dqkv = _attn_bwd(qkv, do, o, lse, g, d, dqkv, name=f"attn_bwd_{g}")

    dmq, dmk, dmv = _mem_attn_bwd(mq, kv, dmo, tq=1024, name="mem_attn_bwd")
    dkv = jnp.concatenate([dmk, dmv], axis=1)
    gb["w_mem_kv"] = mm(mn, dkv, m=1024, n=1024, k=MEM_LEN, ta=True, tm=1024, tn=1024, tk=MEM_LEN, out_dtypes=(F32,), name="dw_mem_kv")
    dmn = mm(dkv, w["w_mem_kv"], m=MEM_LEN, n=1024, k=1024, tb=True, tm=MEM_LEN, tn=1024, tk=1024, out_dtypes=(F32,), name="d_mn")
    _, gs["mem_norm_g"] = _rmsnorm_bwd(mem, gm, dmn, None, tm=MEM_LEN, name="mem_norm_bwd")

    pieces = ((du, OFF_U, "u"), (dqkv[0], OFF_QKV, "q"), (dqkv[1], OFF_QKV + 768, "k"), (dqkv[2], OFF_QKV + 1536, "v"),
              (dmq, OFF_MQ, "mq"), (dzg, OFF_ZG, "zg"))
    dw_rows = []
    for piece, off, tag in pieces:
        width = piece.shape[1]
        tmw = 1024 if width % 1024 == 0 else (768 if width == 768 else 512)
        rides = early[2] if tag == "zg" else ()
        dw_rows.append(reduced(rides, mm(piece, n1, m=width, n=1024, k=rows, ta=True, tm=tmw, tn=1024, tk=1024,
                                         out_dtypes=(F32,), carry=reducing(rides), name="dw_in_" + tag)))
    gb["w_in"] = jnp.concatenate(dw_rows, axis=0)
    dn = reduced(early[3], _sum_matmul([piece for piece, _, _ in pieces], win_t, [off for _, off, _ in pieces], tm=512,
                                       carry=reducing(early[3]), name="d_n1"))
    dx, gs["norm1_g"] = _rmsnorm_bwd(x, g1, dn, dh1, tm=512, name="norm1_bwd")
    return loss, dx, gb, gs


def kernel(x, mem, norm1_g, mem_norm_g, w_in, b_gate, ssm_lambda_re, ssm_lambda_im, ssm_log_dt, ssm_b_re, ssm_b_im, ssm_c_re, ssm_c_im, ssm_d, w_glu, b_glu, w_ssm_br, w_attn_br, w_mem_kv, w_mem_br, w_o, norm2_g, w_up, w_down, final_g, loss_target, m_norm1_g, m_mem_norm_g, m_w_in, m_b_gate, m_ssm_lambda_re, m_ssm_lambda_im, m_ssm_log_dt, m_ssm_b_re, m_ssm_b_im, m_ssm_c_re, m_ssm_c_im, m_ssm_d, m_w_glu, m_b_glu, m_w_ssm_br, m_w_attn_br, m_w_mem_kv, m_w_mem_br, m_w_o, m_norm2_g, m_w_up, m_w_down, m_final_g, v_norm1_g, v_mem_norm_g, v_w_in, v_b_gate, v_ssm_lambda_re, v_ssm_lambda_im, v_ssm_log_dt, v_ssm_b_re, v_ssm_b_im, v_ssm_c_re, v_ssm_c_im, v_ssm_d, v_w_glu, v_b_glu, v_w_ssm_br, v_w_attn_br, v_w_mem_kv, v_w_mem_br, v_w_o, v_norm2_g, v_w_up, v_w_down, v_final_g):
    env = dict(locals())
    weights = {n: env[n] for n in WEIGHT_ORDER}
    moms = {n: env["m_" + n] for n in WEIGHT_ORDER}
    vels = {n: env["v_" + n] for n in WEIGHT_ORDER}
    def shard2d(a):
        return a.reshape(a.shape[-2], a.shape[-1])

    chip = 2 * lax.axis_index("x") + lax.axis_index("y")
    wire = [shard2d(weights[n]).astype(BF16) for n, _, _ in BIG]
    wire = dict(zip([n for n, _, _ in BIG], [s.T if tr else s for s, (_, tr, _) in zip(wire, BIG)]))
    w_in_full = _run_exchange(_gather_exchange([wire.pop("w_in")]), name="all_gather_w_in")[0]
    small = {n: weights[n] for n, _ in SMALL}

    reducer = _GradReducer(lax.axis_index("c").astype(jnp.int32).reshape(1), chip.astype(jnp.int32).reshape(1))
    loss, dx, gb, gs = _device_step(x[0], mem[0], loss_target[0], {"w_in": w_in_full}, small, shards=wire, reducer=reducer)
    *shards, small_grad = reducer.finish(["small"], [_pack_small(gs)], [n for n, _, _ in BIG] + ["small"])
    grads = {}
    for (n, tr, _), sh in zip(BIG, shards):
        sh = sh.reshape(2 * sh.shape[1], sh.shape[2])
        grads[n] = sh.T if tr else sh
    small_grad = small_grad.reshape(N_CHIPS * SMALL_ROWS, 1024)
    grads_small = _unpack_small(small_grad)

    delta, new_m, new_v = {}, {}, {}
    for n, _, _ in BIG:
        grads[n], delta[n], new_m[n], new_v[n] = _adamw(weights[n], grads[n], moms[n], vels[n],
                                                        tr=min(weights[n].shape[-2], 256), name="adamw_" + n)
    _, ds_, ms_, vs_ = _adamw(_pack_small(small), small_grad,
                              _pack_small({n: moms[n] for n, _ in SMALL}), _pack_small({n: vels[n] for n, _ in SMALL}),
                              tr=N_CHIPS * SMALL_ROWS, name="adamw_small")
    for dst, buf in ((delta, ds_), (new_m, ms_), (new_v, vs_)):
        dst.update(_unpack_small(buf))
    grads.update(grads_small)

    total_loss = lax.psum(loss[0, 0], ("x", "y", "c"))
    return (total_loss, dx[None], *[grads[n] for n in WEIGHT_ORDER], *[delta[n] for n in WEIGHT_ORDER],
            *[new_m[n] for n in WEIGHT_ORDER], *[new_v[n] for n in WEIGHT_ORDER])
```

```python
import functools
import math

import numpy as np
import jax
import jax.numpy as jnp
from jax import lax
from jax.experimental import pallas as pl
from jax.experimental.pallas import tpu as pltpu

F32 = jnp.float32
BF16 = jnp.bfloat16

D_MODEL = 1024
SSM_GROUPS = 32
SSM_GROUP_SIZE = 16
SSM_STATE = 64
SSM_WIDTH = 512
N_STATES = SSM_GROUPS * SSM_STATE
SCAN_CB = 1024
ATTN_PATTERNS = ((128, 1), (512, 4), (2048, 16))
ATTN_HEAD_DIM = 64
ATTN_Q = 128
MEM_LEN = 256
MEM_HEAD_DIM = 128
MEM_HEADS = 4
D_FF = 4096
OFF_U, OFF_QKV, OFF_MQ, OFF_ZG = 0, 512, 2816, 3328
IN_WIDTH = 6400
RMS_EPS = 1e-6
NEG_INF = -1e30
ADAM_LR, ADAM_B1, ADAM_B2, ADAM_EPS, ADAM_WD, ADAM_STEP = 0.001, 0.9, 0.999, 1e-08, 0.01, 10

VMEM_LIMIT_BYTES = 56 * 1024 * 1024
LANES = 128
MESH = pl.DeviceIdType.MESH
N_CHIPS = 4

SCAN_SEGS = 8
SCAN_GROUPS = SCAN_CB // SSM_STATE

BIG = (("w_in", True, (6400, 1024)), ("w_glu", False, (512, 512)), ("w_ssm_br", True, (1024, 512)),
       ("w_attn_br", True, (1024, 256)), ("w_mem_kv", False, (1024, 1024)), ("w_mem_br", True, (1024, 512)),
       ("w_o", False, (1024, 1024)), ("w_up", True, (4096, 1024)), ("w_down", False, (4096, 1024)))
SMALL = (("norm1_g", (1, 1024)), ("mem_norm_g", (1, 1024)), ("b_gate", (1, 3072)),
         ("ssm_lambda_re", (1, 32, 64)), ("ssm_lambda_im", (1, 32, 64)), ("ssm_log_dt", (1, 32)),
         ("ssm_b_re", (1, 32, 64, 16)), ("ssm_b_im", (1, 32, 64, 16)), ("ssm_c_re", (1, 32, 16, 64)),
         ("ssm_c_im", (1, 32, 16, 64)), ("ssm_d", (1, 32, 16)), ("b_glu", (1, 512)),
         ("norm2_g", (1, 1024)), ("final_g", (1024,)))
WEIGHT_ORDER = ("norm1_g", "mem_norm_g", "w_in", "b_gate", "ssm_lambda_re", "ssm_lambda_im", "ssm_log_dt",
                "ssm_b_re", "ssm_b_im", "ssm_c_re", "ssm_c_im", "ssm_d", "w_glu", "b_glu", "w_ssm_br",
                "w_attn_br", "w_mem_kv", "w_mem_br", "w_o", "norm2_g", "w_up", "w_down", "final_g")
SMALL_ELEMS = sum(int(np.prod(s)) for _, s in SMALL)
SMALL_ROWS = 64


def _params(sem):
    return pltpu.CompilerParams(dimension_semantics=sem, vmem_limit_bytes=VMEM_LIMIT_BYTES)


def _sigmoid(v):
    return 1.0 / (1.0 + jnp.exp(-v))


_GELU_C = math.sqrt(2.0 / math.pi)


def _gelu(v):
    return 0.5 * v * (1.0 + jnp.tanh(_GELU_C * (v + 0.044715 * v * v * v)))


def _gelu_grad(v):
    th = jnp.tanh(_GELU_C * (v + 0.044715 * v * v * v))
    return 0.5 * (1.0 + th) + 0.5 * v * (1.0 - th * th) * _GELU_C * (1.0 + 3.0 * 0.044715 * v * v)


def _dot(a, b, ca, cb):
    return lax.dot_general(a, b, (((ca,), (cb,)), ((), ())), preferred_element_type=F32)


class _Exchange:
    def __init__(self, ins, outs, aliases, sems, start, finish):
        self.ins, self.outs, self.aliases, self.sems, self.start, self.finish = ins, outs, aliases, sems, start, finish


def _matmul(a, b, *, m, n, k, ta=False, tb=False, tm, tn, tk, out_dtypes, name,
            a_off=(0, 0), b_off=(0, 0), aux=(), epilogue=None, n_sums=0, carry=None):
    assert m % tm == 0 and n % tn == 0 and k % tk == 0, (name, m, n, k, tm, tn, tk)
    nk = k // tk
    n_aux = len(aux)
    n_tiles = len(out_dtypes)
    n_out = n_tiles + n_sums
    ar, ac = a_off
    br, bc = b_off
    if ta:
        a_spec = pl.BlockSpec((tk, tm), lambda i, j, kk: (kk + ar, i + ac))
    else:
        a_spec = pl.BlockSpec((tm, tk), lambda i, j, kk: (i + ar, kk + ac))
    if tb:
        b_spec = pl.BlockSpec((tn, tk), lambda i, j, kk: (j + br, kk + bc))
    else:
        b_spec = pl.BlockSpec((tk, tn), lambda i, j, kk: (kk + br, j + bc))
    aux_specs = []
    for _, kind in aux:
        if kind == "mn":
            aux_specs.append(pl.BlockSpec((tm, tn), lambda i, j, kk: (i, j)))
        else:
            aux_specs.append(pl.BlockSpec((1, tn), lambda i, j, kk: (0, j)))
    ca = 0 if ta else 1
    cb = 1 if tb else 0

    def finish(acc, aux_refs, out_refs, row_tile):
        outs = (acc,) if epilogue is None else epilogue(acc, *[r[...] for r in aux_refs])
        for o_ref, o in zip(out_refs[:n_tiles], outs[:n_tiles]):
            o_ref[...] = o.astype(o_ref.dtype)
        _accumulate_over_rows(out_refs[n_tiles:], outs[n_tiles:], row_tile)

    def body(a_ref, b_ref, *rest):
        aux_refs = rest[:n_aux]
        out_refs = rest[n_aux:n_aux + n_out]
        row_tile = pl.program_id(0)
        prod = _dot(a_ref[...].astype(BF16), b_ref[...].astype(BF16), ca, cb)
        if nk == 1:
            finish(prod, aux_refs, out_refs, row_tile)
            return
        acc_ref = rest[n_aux + n_out]
        kk = pl.program_id(2)

        @pl.when(kk == 0)
        def _():
            acc_ref[...] = prod

        @pl.when(jnp.logical_and(kk > 0, kk < nk - 1))
        def _():
            acc_ref[...] += prod

        @pl.when(kk == nk - 1)
        def _():
            finish(acc_ref[...] + prod, aux_refs, out_refs, row_tile)

    tile = pl.BlockSpec((tm, tn), lambda i, j, kk: (i, j))
    col_sum = pl.BlockSpec((1, tn), lambda i, j, kk: (0, j))
    res = _call_with_carry(
        body, carry, name=name, grid=(m // tm, n // tn, nk), in_specs=[a_spec, b_spec] + aux_specs,
        out_specs=[tile] * n_tiles + [col_sum] * n_sums,
        out_shape=[jax.ShapeDtypeStruct((m, n), dt) for dt in out_dtypes] + [jax.ShapeDtypeStruct((1, n), F32)] * n_sums,
        scratch=[pltpu.VMEM((tm, tn), F32)] if nk > 1 else [], operands=[a, b] + [x for x, _ in aux],
        semantics=("arbitrary" if n_sums else "parallel", "parallel", "arbitrary"))
    main = res[0] if n_out == 1 else tuple(res[:n_out])
    return main if carry is None else (main, list(res[n_out:]))


def _accumulate_over_rows(sum_refs, terms, row_tile):
    for s_ref, term in zip(sum_refs, terms):
        @pl.when(row_tile == 0)
        def _():
            s_ref[...] = term

        @pl.when(row_tile > 0)
        def _():
            s_ref[...] += term


def _call_with_carry(body, carry, *, name, grid, in_specs, out_specs, out_shape, scratch, operands, semantics):
    if carry is None:
        return pl.pallas_call(body, name=name, grid=grid, in_specs=in_specs, out_specs=out_specs, out_shape=out_shape,
                              scratch_shapes=scratch, compiler_params=_params(semantics))(*operands)
    n_in, n_cin, n_out, n_cout, n_scr = len(operands), len(carry.ins), len(out_shape), len(carry.outs), len(scratch)

    def hosted(*refs):
        main_in, c_in = refs[:n_in], refs[n_in:n_in + n_cin]
        main_out = refs[n_in + n_cin:n_in + n_cin + n_out]
        c_out = refs[n_in + n_cin + n_out:n_in + n_cin + n_out + n_cout]
        rest = refs[n_in + n_cin + n_out + n_cout:]
        ids = [pl.program_id(t) for t in range(len(grid))]
        first = functools.reduce(jnp.logical_and, [i == 0 for i in ids])
        last = functools.reduce(jnp.logical_and, [i == g - 1 for i, g in zip(ids, grid)])

        @pl.when(first)
        def _():
            carry.start(c_in, c_out, *rest[n_scr:])

        body(*main_in, *main_out, *rest[:n_scr])

        @pl.when(last)
        def _():
            carry.finish(c_in, c_out, *rest[n_scr:])

    return pl.pallas_call(
        hosted, name=name, grid=grid,
        in_specs=list(in_specs) + [ANY] * n_cin, out_specs=list(out_specs) + [ANY] * n_cout,
        out_shape=list(out_shape) + list(carry.outs),
        input_output_aliases={n_in + i: n_out + o for i, o in carry.aliases.items()},
        scratch_shapes=list(scratch) + [pltpu.SemaphoreType.DMA(s) for s in carry.sems],
        compiler_params=_params(("arbitrary",) * len(grid)),
    )(*operands, *carry.ins)


def _sum_matmul(pieces, b, offs, *, tm, name, aux=(), epilogue=None, n_sums=0, carry=None):
    m = pieces[0].shape[0]
    n = b.shape[1]
    npieces, n_aux = len(pieces), len(aux)

    def body(*refs):
        b_ref = refs[npieces]
        aux_refs = refs[npieces + 1:npieces + 1 + n_aux]
        out_refs = refs[npieces + 1 + n_aux:]
        acc = None
        for p_ref, off in zip(refs[:npieces], offs):
            part = _dot(p_ref[...].astype(BF16), b_ref[pl.ds(off, p_ref.shape[1]), :], 1, 0)
            acc = part if acc is None else acc + part
        outs = (acc,) if epilogue is None else epilogue(acc, *[r[...] for r in aux_refs])
        out_refs[0][...] = outs[0]
        _accumulate_over_rows(out_refs[1:], outs[1:], pl.program_id(0))

    row = pl.BlockSpec((tm, n), lambda i: (i, 0))
    vec = pl.BlockSpec((1, n), lambda i: (0, 0))
    res = _call_with_carry(
        body, carry, name=name, grid=(m // tm,),
        in_specs=[pl.BlockSpec((tm, p.shape[1]), lambda i: (i, 0)) for p in pieces] + [_resident(b.shape)]
        + [row if kind == "mn" else vec for _, kind in aux],
        out_specs=[row] + [vec] * n_sums,
        out_shape=[jax.ShapeDtypeStruct((m, n), F32)] + [jax.ShapeDtypeStruct((1, n), F32)] * n_sums,
        scratch=[], operands=list(pieces) + [b] + [x for x, _ in aux], semantics=("arbitrary" if n_sums else "parallel",))
    main = res[0] if n_sums == 0 else tuple(res[:1 + n_sums])
    return main if carry is None else (main, list(res[1 + n_sums:]))


def _rmsnorm_fwd(x, g, *, tm, name):
    rows, d = x.shape

    def body(x_ref, g_ref, o_ref):
        xv = x_ref[...]
        r = lax.rsqrt(jnp.mean(xv * xv, axis=-1, keepdims=True) + RMS_EPS)
        o_ref[...] = (xv * r * g_ref[...]).astype(o_ref.dtype)

    return pl.pallas_call(
        body, name=name, grid=(rows // tm,),
        in_specs=[pl.BlockSpec((tm, d), lambda i: (i, 0)), pl.BlockSpec((1, d), lambda i: (0, 0))],
        out_specs=pl.BlockSpec((tm, d), lambda i: (i, 0)),
        out_shape=jax.ShapeDtypeStruct((rows, d), BF16),
        compiler_params=_params(("parallel",)),
    )(x, g)


def _residual_norm_epilogue(acc, xv, gv):
    h = acc + xv
    r = lax.rsqrt(jnp.mean(h * h, axis=-1, keepdims=True) + RMS_EPS)
    return h, h * r * gv


def _rmsnorm_bwd_epilogue(dy, xv, resv, gv):
    r = lax.rsqrt(jnp.mean(xv * xv, axis=-1, keepdims=True) + RMS_EPS)
    xhat = xv * r
    dyg = dy * gv
    dx = r * (dyg - xhat * jnp.mean(dyg * xhat, axis=-1, keepdims=True)) + resv
    return dx, jnp.sum(dy * xhat, axis=0, keepdims=True)


def _rmsnorm_bwd(x, g, dy, res, *, tm, name):
    rows, d = x.shape
    has_res = res is not None

    def body(x_ref, g_ref, dy_ref, *rest):
        if has_res:
            res_ref, dx_ref, dg_ref = rest
        else:
            dx_ref, dg_ref = rest
        i = pl.program_id(0)
        xv = x_ref[...]
        r = lax.rsqrt(jnp.mean(xv * xv, axis=-1, keepdims=True) + RMS_EPS)
        xhat = xv * r
        dyv = dy_ref[...]
        dyg = dyv * g_ref[...]
        dx = r * (dyg - xhat * jnp.mean(dyg * xhat, axis=-1, keepdims=True))
        if has_res:
            dx = dx + res_ref[...]
        dx_ref[...] = dx

        @pl.when(i == 0)
        def _():
            dg_ref[...] = jnp.zeros_like(dg_ref)

        dg_ref[...] += jnp.sum(dyv * xhat, axis=0, keepdims=True)

    row_spec = pl.BlockSpec((tm, d), lambda i: (i, 0))
    vec_spec = pl.BlockSpec((1, d), lambda i: (0, 0))
    ins = [x, g, dy] + ([res] if has_res else [])
    return pl.pallas_call(
        body, name=name, grid=(rows // tm,),
        in_specs=[row_spec, vec_spec, row_spec] + ([row_spec] if has_res else []),
        out_specs=[row_spec, vec_spec],
        out_shape=[jax.ShapeDtypeStruct((rows, d), F32), jax.ShapeDtypeStruct((1, d), F32)],
        compiler_params=_params(("arbitrary",)),
    )(*ins)


def _loss_head(h, tgt, g, *, tm, name):
    rows, d = h.shape
    nsteps = rows // tm

    def body(h_ref, t_ref, g_ref, dh_ref, loss_ref, dg_ref, sq_ref):
        i = pl.program_id(0)
        xv = h_ref[...]
        gv = g_ref[...]
        r = lax.rsqrt(jnp.mean(xv * xv, axis=-1, keepdims=True) + RMS_EPS)
        xhat = xv * r
        err = xhat * gv - t_ref[...]
        dyv = err * (1.0 / d)
        dyg = dyv * gv
        dh_ref[...] = r * (dyg - xhat * jnp.mean(dyg * xhat, axis=-1, keepdims=True))

        @pl.when(i == 0)
        def _():
            dg_ref[...] = jnp.zeros_like(dg_ref)
            sq_ref[...] = jnp.zeros_like(sq_ref)

        dg_ref[...] += jnp.sum(dyv * xhat, axis=0, keepdims=True)
        sq_ref[...] += jnp.sum(err * err, axis=0, keepdims=True)

        @pl.when(i == nsteps - 1)
        def _():
            tot = jnp.sum(sq_ref[...], axis=-1, keepdims=True) * (0.5 / d)
            loss_ref[...] = jnp.broadcast_to(tot, loss_ref.shape)

    row_spec = pl.BlockSpec((tm, d), lambda i: (i, 0))
    vec_spec = pl.BlockSpec((1, d), lambda i: (0, 0))
    return pl.pallas_call(
        body, name=name, grid=(nsteps,),
        in_specs=[row_spec, row_spec, vec_spec],
        out_specs=[row_spec, pl.BlockSpec((1, LANES), lambda i: (0, 0)), vec_spec],
        out_shape=[jax.ShapeDtypeStruct((rows, d), F32), jax.ShapeDtypeStruct((1, LANES), F32),
                   jax.ShapeDtypeStruct((1, d), F32)],
        scratch_shapes=[pltpu.VMEM((1, d), F32)],
        compiler_params=_params(("arbitrary",)),
    )(h, tgt, g)


def _to_scan_layout(v):
    lead = v.shape[:-2]
    v = v.reshape(lead + (2, N_STATES // SCAN_CB, SCAN_CB))
    v = jnp.swapaxes(v, -3, -2)
    return v.reshape(lead + (2 * N_STATES,))


def _ssm_matrices(lam_re, lam_im, log_dt, b_re, b_im, c_re, c_im):
    dt = jnp.exp(log_dt)[:, None]
    mag = jnp.exp(lam_re * dt)
    a_re, a_im = mag * jnp.cos(lam_im * dt), mag * jnp.sin(lam_im * dt)
    nr, ni = a_re - 1.0, a_im
    den = lam_re * lam_re + lam_im * lam_im
    coef_re = (nr * lam_re + ni * lam_im) / den
    coef_im = (ni * lam_re - nr * lam_im) / den
    bb_re = coef_re[..., None] * b_re - coef_im[..., None] * b_im
    bb_im = coef_re[..., None] * b_im + coef_im[..., None] * b_re
    a_lay = _to_scan_layout(jnp.stack([a_re.reshape(-1), a_im.reshape(-1)], axis=0))[None, :]
    nblk = SSM_GROUPS // SCAN_GROUPS
    eye = jnp.eye(SCAN_GROUPS, dtype=F32)

    def b_block(bb):
        bb = bb.reshape(nblk, SCAN_GROUPS, SSM_STATE, SSM_GROUP_SIZE)
        return jnp.einsum("gk,jkph->jghkp", eye, bb).reshape(nblk, SCAN_GROUPS * SSM_GROUP_SIZE, SCAN_CB)

    b_blk = jnp.concatenate([b_block(bb_re), b_block(bb_im)], axis=2)

    def c_block(cc):
        cc = cc.reshape(nblk, SCAN_GROUPS, SSM_GROUP_SIZE, SSM_STATE)
        return jnp.einsum("gk,jghp->jkpgh", eye, cc).reshape(nblk, SCAN_CB, SCAN_GROUPS * SSM_GROUP_SIZE)

    c_blk = jnp.concatenate([c_block(c_re), -c_block(c_im)], axis=1)
    return a_lay, b_blk, c_blk


def _interleave(v):
    rows, c = v.shape
    return v.reshape(SCAN_SEGS, rows // SCAN_SEGS, c).transpose(1, 0, 2).reshape(rows, c)


def _deinterleave(v):
    rows, c = v.shape
    return v.reshape(rows // SCAN_SEGS, SCAN_SEGS, c).transpose(1, 0, 2).reshape(rows, c)


def _scan_groups(a_ref, bu_ref, o_ref, state, *, reverse, tt):
    cb = SCAN_CB
    ar = jnp.broadcast_to(a_ref[:, :cb], (SCAN_SEGS, cb))
    ai = jnp.broadcast_to(a_ref[:, cb:], (SCAN_SEGS, cb))
    ngroups = tt // SCAN_SEGS

    def step(i, st):
        sr, si = st
        r0 = pl.multiple_of(((ngroups - 1 - i) if reverse else i) * SCAN_SEGS, SCAN_SEGS)
        blk = bu_ref[pl.ds(r0, SCAN_SEGS), :]
        nr = ar * sr - ai * si + blk[:, :cb]
        ni = ar * si + ai * sr + blk[:, cb:]
        if o_ref is not None:
            o_ref[pl.ds(r0, SCAN_SEGS), :] = jnp.concatenate([nr, ni], axis=1)
        return nr, ni

    return lax.fori_loop(0, ngroups, step, state, unroll=4)


def _segment_entries(a_ref, e_ref, init_ref, *, reverse, seg_len):
    cb = SCAN_CB
    n_sq = seg_len.bit_length() - 1
    assert 1 << n_sq == seg_len, seg_len
    pr, pi = a_ref[:, :cb], a_ref[:, cb:]
    for _ in range(n_sq):
        pr, pi = pr * pr - pi * pi, 2.0 * pr * pi
    cr = jnp.zeros((1, cb), F32)
    ci = jnp.zeros((1, cb), F32)
    order = range(SCAN_SEGS - 1, -1, -1) if reverse else range(SCAN_SEGS)
    for k, seg in enumerate(order):
        if k > 0:
            prev = seg + 1 if reverse else seg - 1
            er, ei = e_ref[prev:prev + 1, :cb], e_ref[prev:prev + 1, cb:]
            cr, ci = pr * cr - pi * ci + er, pr * ci + pi * cr + ei
        init_ref[seg:seg + 1, :] = jnp.concatenate([cr, ci], axis=1)


def _ssm_specs(nt, tt, nch, reverse):
    cb = SCAN_CB
    tmap = (lambda j, kk: (nt - 1 - kk, j)) if reverse else (lambda j, kk: (kk, j))
    return dict(a=pl.BlockSpec((1, 2 * cb), lambda j, kk: (0, j)),
                seg=pl.BlockSpec((SCAN_SEGS, 2 * cb), lambda j, kk: (0, j)),
                chan=pl.BlockSpec((tt, nch), tmap),
                state=pl.BlockSpec((tt, 2 * cb), tmap),
                b=pl.BlockSpec((None, nch, 2 * cb), lambda j, kk: (j, 0, 0)),
                c=pl.BlockSpec((None, 2 * cb, nch), lambda j, kk: (j, 0, 0)))


def _ssm_ends(a_lay, x, blocks, *, transpose, reverse, tt, name):
    rows = x.shape[0]
    nblk = blocks.shape[0]
    nch = x.shape[1] // nblk
    cb = SCAN_CB
    nt = rows // tt
    sp = _ssm_specs(nt, tt, nch, reverse)

    def body(a_ref, x_ref, w_ref, e_ref, bu_ref):
        kk = pl.program_id(1)

        @pl.when(kk == 0)
        def _():
            e_ref[...] = jnp.zeros_like(e_ref)

        bu_ref[...] = _dot(x_ref[...].astype(BF16), w_ref[...].astype(BF16), 1, 1 if transpose else 0)
        sr, si = _scan_groups(a_ref, bu_ref, None, (e_ref[:, :cb], e_ref[:, cb:]), reverse=reverse, tt=tt)
        e_ref[...] = jnp.concatenate([sr, si], axis=1)

    return pl.pallas_call(
        body, name=name, grid=(nblk, nt),
        in_specs=[sp["a"], sp["chan"], sp["c"] if transpose else sp["b"]],
        out_specs=sp["seg"],
        out_shape=jax.ShapeDtypeStruct((SCAN_SEGS, nblk * 2 * cb), F32),
        scratch_shapes=[pltpu.VMEM((tt, 2 * cb), F32)],
        compiler_params=_params(("parallel", "arbitrary")),
    )(a_lay, x, blocks)


def _ssm_fwd(a_lay, u, b_blk, c_blk, ends, *, tt, name):
    rows = u.shape[0]
    nblk = b_blk.shape[0]
    nch = u.shape[1] // nblk
    cb = SCAN_CB
    nt = rows // tt
    sp = _ssm_specs(nt, tt, nch, False)

    def body(a_ref, e_ref, u_ref, b_ref, c_ref, s_ref, y_ref, init_ref, carry_ref):
        kk = pl.program_id(1)

        @pl.when(kk == 0)
        def _():
            _segment_entries(a_ref, e_ref, init_ref, reverse=False, seg_len=rows // SCAN_SEGS)
            carry_ref[...] = init_ref[...]

        s_ref[...] = _dot(u_ref[...].astype(BF16), b_ref[...].astype(BF16), 1, 0)
        sr, si = _scan_groups(a_ref, s_ref, s_ref, (carry_ref[:, :cb], carry_ref[:, cb:]), reverse=False, tt=tt)
        carry_ref[...] = jnp.concatenate([sr, si], axis=1)
        y_ref[...] = _dot(s_ref[...].astype(BF16), c_ref[...].astype(BF16), 1, 0)

    return pl.pallas_call(
        body, name=name, grid=(nblk, nt),
        in_specs=[sp["a"], sp["seg"], sp["chan"], sp["b"], sp["c"]],
        out_specs=[sp["state"], sp["chan"], sp["seg"]],
        out_shape=[jax.ShapeDtypeStruct((rows, nblk * 2 * cb), F32), jax.ShapeDtypeStruct((rows, nblk * nch), F32),
                   jax.ShapeDtypeStruct((SCAN_SEGS, nblk * 2 * cb), F32)],
        scratch_shapes=[pltpu.VMEM((SCAN_SEGS, 2 * cb), F32)],
        compiler_params=_params(("parallel", "arbitrary")),
    )(a_lay, ends, u, b_blk, c_blk)


def _ssm_bwd(a_conj, dy, u, s, s_entry, b_blk, c_blk, dd, ends, *, tt, name):
    rows = u.shape[0]
    nblk = b_blk.shape[0]
    nch = u.shape[1] // nblk
    cb = SCAN_CB
    nt = rows // tt
    sp = _ssm_specs(nt, tt, nch, True)
    groups_per_tile = tt // SCAN_SEGS
    before = pl.BlockSpec((SCAN_SEGS, 2 * cb), lambda j, kk: (jnp.maximum((nt - 1 - kk) * groups_per_tile - 1, 0), j))

    def body(a_ref, e_ref, dy_ref, u_ref, s_ref, before_ref, entry_ref, b_ref, c_ref, dd_ref,
             du_ref, db_ref, dc_ref, da_ref, lam_ref, carry_ref):
        kk = pl.program_id(1)

        @pl.when(kk == 0)
        def _():
            _segment_entries(a_ref, e_ref, carry_ref, reverse=True, seg_len=rows // SCAN_SEGS)
            db_ref[...] = jnp.zeros_like(db_ref)
            dc_ref[...] = jnp.zeros_like(dc_ref)
            da_ref[...] = jnp.zeros_like(da_ref)

        dyv = dy_ref[...]
        dyb = dyv.astype(BF16)
        lam_ref[...] = _dot(dyb, c_ref[...].astype(BF16), 1, 1)
        lr, li = _scan_groups(a_ref, lam_ref, lam_ref, (carry_ref[:, :cb], carry_ref[:, cb:]), reverse=True, tt=tt)
        carry_ref[...] = jnp.concatenate([lr, li], axis=1)

        first = jnp.where(kk == nt - 1, entry_ref[...], before_ref[...])
        rest = tt - SCAN_SEGS
        lam_hi = lam_ref[pl.ds(SCAN_SEGS, rest), :]
        s_lo = s_ref[pl.ds(0, rest), :]
        lam_lo = lam_ref[pl.ds(0, SCAN_SEGS), :]

        def pair(lv, pv):
            lre, lim, pre, pim = lv[:, :cb], lv[:, cb:], pv[:, :cb], pv[:, cb:]
            return (jnp.sum(lre * pre + lim * pim, axis=0, keepdims=True),
                    jnp.sum(lim * pre - lre * pim, axis=0, keepdims=True))

        r1, i1 = pair(lam_hi, s_lo)
        r0, i0 = pair(lam_lo, first)
        da_ref[...] += jnp.concatenate([r1 + r0, i1 + i0], axis=1)

        lamb = lam_ref[...].astype(BF16)
        du_ref[...] = _dot(lamb, b_ref[...].astype(BF16), 1, 1) + dd_ref[...] * dyv
        db_ref[...] += _dot(u_ref[...].astype(BF16), lamb, 0, 0)
        dc_ref[...] += _dot(s_ref[...].astype(BF16), dyb, 0, 0)

    return pl.pallas_call(
        body, name=name, grid=(nblk, nt),
        in_specs=[sp["a"], sp["seg"], sp["chan"], sp["chan"], sp["state"], before, sp["seg"], sp["b"], sp["c"],
                  pl.BlockSpec((1, nch), lambda j, kk: (0, j))],
        out_specs=[sp["chan"], sp["b"], sp["c"], pl.BlockSpec((1, 2 * cb), lambda j, kk: (0, j))],
        out_shape=[jax.ShapeDtypeStruct((rows, nblk * nch), F32), jax.ShapeDtypeStruct(b_blk.shape, F32),
                   jax.ShapeDtypeStruct(c_blk.shape, F32), jax.ShapeDtypeStruct((1, nblk * 2 * cb), F32)],
        scratch_shapes=[pltpu.VMEM((tt, 2 * cb), F32), pltpu.VMEM((SCAN_SEGS, 2 * cb), F32)],
        compiler_params=_params(("parallel", "arbitrary")),
    )(a_conj, ends, dy, u, s, s, s_entry, b_blk, c_blk, dd)


def _glu_fwd(ys, u, dd, w_glu, b_glu, *, tm, name):
    rows, w = ys.shape

    def body(ys_ref, u_ref, dd_ref, w_ref, b_ref, y0_ref, t_ref, y2_ref):
        y0 = ys_ref[...] + dd_ref[...] * u_ref[...]
        y1 = _gelu(y0)
        t = _dot(y1.astype(BF16), w_ref[...], 1, 0) + b_ref[...]
        y0_ref[...] = y0
        t_ref[...] = t
        y2_ref[...] = (y1 * _sigmoid(t)).astype(BF16)

    row = pl.BlockSpec((tm, w), lambda i: (i, 0))
    vec = pl.BlockSpec((1, w), lambda i: (0, 0))
    return pl.pallas_call(
        body, name=name, grid=(rows // tm,),
        in_specs=[row, row, vec, pl.BlockSpec((w, w), lambda i: (0, 0)), vec],
        out_specs=[row, row, row],
        out_shape=[jax.ShapeDtypeStruct((rows, w), F32), jax.ShapeDtypeStruct((rows, w), F32),
                   jax.ShapeDtypeStruct((rows, w), BF16)],
        compiler_params=_params(("parallel",)),
    )(ys, u, dd, w_glu, b_glu)


def _glu_bwd(dy2, y0, t, u, w_glu, *, tm, name):
    rows, w = y0.shape

    def body(dy2_ref, y0_ref, t_ref, u_ref, w_ref, dy0_ref, dt_ref, y1_ref, db_ref, dd_ref):
        i = pl.program_id(0)
        y0 = y0_ref[...]
        y1 = _gelu(y0)
        sg = _sigmoid(t_ref[...])
        dy2v = dy2_ref[...]
        dt = dy2v * y1 * sg * (1.0 - sg)
        dy1 = dy2v * sg + _dot(dt.astype(BF16), w_ref[...], 1, 1)
        dy0 = dy1 * _gelu_grad(y0)
        dy0_ref[...] = dy0
        dt_ref[...] = dt.astype(BF16)
        y1_ref[...] = y1.astype(BF16)

        @pl.when(i == 0)
        def _():
            db_ref[...] = jnp.zeros_like(db_ref)
            dd_ref[...] = jnp.zeros_like(dd_ref)

        db_ref[...] += jnp.sum(dt, axis=0, keepdims=True)
        dd_ref[...] += jnp.sum(dy0 * u_ref[...], axis=0, keepdims=True)

    row = pl.BlockSpec((tm, w), lambda i: (i, 0))
    vec = pl.BlockSpec((1, w), lambda i: (0, 0))
    return pl.pallas_call(
        body, name=name, grid=(rows // tm,),
        in_specs=[row, row, row, row, pl.BlockSpec((w, w), lambda i: (0, 0))],
        out_specs=[row, row, row, vec, vec],
        out_shape=[jax.ShapeDtypeStruct((rows, w), F32), jax.ShapeDtypeStruct((rows, w), BF16),
                   jax.ShapeDtypeStruct((rows, w), BF16), jax.ShapeDtypeStruct((1, w), F32),
                   jax.ShapeDtypeStruct((1, w), F32)],
        compiler_params=_params(("arbitrary",)),
    )(dy2, y0, t, u, w_glu)


ATTN_TILE = 2048


def _attn_geometry(rows, d):
    sb = ATTN_Q * d
    tr = max(sb, min(ATTN_TILE, rows))
    assert rows % tr == 0 and tr % sb == 0, (rows, d)
    return sb, tr, rows // tr, tr // sb


def _attn_masks():
    qi = lax.broadcasted_iota(jnp.int32, (2 * ATTN_Q, 2 * ATTN_Q), 0) % ATTN_Q
    kj = lax.broadcasted_iota(jnp.int32, (2 * ATTN_Q, 2 * ATTN_Q), 1)
    own_ok = jnp.logical_and(kj >= ATTN_Q, kj - ATTN_Q <= qi)
    prev_ok = jnp.logical_and(kj < ATTN_Q, kj >= qi)
    bias_first = jnp.where(own_ok, 0.0, NEG_INF)
    bias_other = jnp.where(jnp.logical_or(own_ok, prev_ok), 0.0, NEG_INF)
    head0 = lax.broadcasted_iota(jnp.int32, (ATTN_Q, LANES), 1) < ATTN_HEAD_DIM
    return bias_first, bias_other, head0


def _attn_rows(base, n, d):
    return pl.ds(pl.multiple_of(base, ATTN_Q), n) if d == 1 else pl.ds(base, n, stride=d)


def _stack_heads(v, head0):
    return jnp.concatenate([jnp.where(head0, v, 0.0), jnp.where(head0, 0.0, v)], axis=0)


def _unstack_heads(v, head0):
    return jnp.where(head0, v[:ATTN_Q], v[ATTN_Q:])


def _fill_keys(buf, prev_ref, cur_ref, sb):
    buf[pl.ds(0, sb), :] = prev_ref[...]
    buf[pl.ds(sb, cur_ref.shape[0]), :] = cur_ref[...]


def _attn_fwd(qkv, g, d, *, name):
    rows = qkv.shape[0]
    sb, tr, ntiles, nsub = _attn_geometry(rows, d)
    qc, kc, vc = 2 * g, 6 + 2 * g, 12 + 2 * g
    scale = ATTN_HEAD_DIM ** -0.5

    def body(q_ref, kc_ref, kp_ref, vc_ref, vp_ref, o_ref, lse_ref, kbuf, vbuf):
        n = pl.program_id(0)
        _fill_keys(kbuf, kp_ref, kc_ref, sb)
        _fill_keys(vbuf, vp_ref, vc_ref, sb)
        bias_first, bias_other, head0 = _attn_masks()

        def per_block(idx, carry):
            j, r = idx // d, idx % d
            base = j * sb + r
            bias = jnp.where(jnp.logical_and(n == 0, j == 0), bias_first, bias_other)
            qrows = _attn_rows(base, ATTN_Q, d)
            krows = _attn_rows(base, 2 * ATTN_Q, d)
            qs = _stack_heads(q_ref[qrows, :], head0).astype(BF16)
            s = _dot(qs, kbuf[krows, :].astype(BF16), 1, 1) * scale + bias
            mx = jnp.max(s, axis=-1, keepdims=True)
            p = jnp.exp(s - mx)
            den = jnp.sum(p, axis=-1, keepdims=True)
            pv = _dot(p.astype(BF16), vbuf[krows, :].astype(BF16), 1, 0) / den
            o_ref[qrows, :] = _unstack_heads(pv, head0)
            lse_ref[qrows, :] = _unstack_heads(jnp.broadcast_to(mx + jnp.log(den), (2 * ATTN_Q, LANES)), head0)
            return carry

        lax.fori_loop(0, nsub * d, per_block, 0, unroll=8)

    def cur(col):
        return pl.BlockSpec((tr, LANES), lambda n, hp: (n, col + hp))

    def prev(col):
        return pl.BlockSpec((sb, LANES), lambda n, hp: (jnp.maximum(n * nsub - 1, 0), col + hp))

    out_spec = pl.BlockSpec((tr, LANES), lambda n, hp: (n, hp))
    return pl.pallas_call(
        body, name=name, grid=(ntiles, 2),
        in_specs=[cur(qc), cur(kc), prev(kc), cur(vc), prev(vc)],
        out_specs=[out_spec, out_spec],
        out_shape=[jax.ShapeDtypeStruct((rows, 2 * LANES), F32), jax.ShapeDtypeStruct((rows, 2 * LANES), F32)],
        scratch_shapes=[pltpu.VMEM((sb + tr, LANES), F32), pltpu.VMEM((sb + tr, LANES), F32)],
        compiler_params=_params(("parallel", "parallel")),
    )(qkv, qkv, qkv, qkv, qkv)


def _attn_merge(outs, lses, *, tm, name):
    rows, w = outs[0].shape

    def body(o0, o1, o2, l0, l1, l2, o_ref, lse_ref):
        a0, a1, a2 = l0[...], l1[...], l2[...]
        mx = jnp.maximum(jnp.maximum(a0, a1), a2)
        e0, e1, e2 = jnp.exp(a0 - mx), jnp.exp(a1 - mx), jnp.exp(a2 - mx)
        den = e0 + e1 + e2
        o_ref[...] = (e0 / den) * o0[...] + (e1 / den) * o1[...] + (e2 / den) * o2[...]
        lse_ref[...] = mx + jnp.log(den)

    row = pl.BlockSpec((tm, w), lambda i: (i, 0))
    return pl.pallas_call(
        body, name=name, grid=(rows // tm,), in_specs=[row] * 6, out_specs=[row, row],
        out_shape=[jax.ShapeDtypeStruct((rows, w), F32), jax.ShapeDtypeStruct((rows, w), F32)],
        compiler_params=_params(("parallel",)),
    )(*outs, *lses)


def _attn_bwd(qkv, do, o, lse, g, d, prev, *, name):
    rows = qkv.shape[0]
    sb, tr, ntiles, nsub = _attn_geometry(rows, d)
    qc, kc, vc = 2 * g, 6 + 2 * g, 12 + 2 * g
    scale = ATTN_HEAD_DIM ** -0.5

    def body(q_ref, kc_ref, kp_ref, vc_ref, vp_ref, do_ref, o_ref, lse_ref, dq_ref, dk_ref, dv_ref,
             kbuf, vbuf, dk_acc, dv_acc):
        n = pl.program_id(1)

        @pl.when(n == 0)
        def _():
            dk_acc[pl.ds(0, tr), :] = jnp.zeros((tr, LANES), F32)
            dv_acc[pl.ds(0, tr), :] = jnp.zeros((tr, LANES), F32)

        @pl.when(n < ntiles)
        def _():
            dk_acc[pl.ds(tr, tr), :] = jnp.zeros((tr, LANES), F32)
            dv_acc[pl.ds(tr, tr), :] = jnp.zeros((tr, LANES), F32)
            _fill_keys(kbuf, kp_ref, kc_ref, sb)
            _fill_keys(vbuf, vp_ref, vc_ref, sb)
            bias_first, bias_other, head0 = _attn_masks()
            lane = lax.broadcasted_iota(jnp.int32, (ATTN_Q, LANES), 1)

            def per_block(idx, carry):
                j, r = idx // d, idx % d
                base = j * sb + r
                bias = jnp.where(jnp.logical_and(n == 0, j == 0), bias_first, bias_other)
                qrows = _attn_rows(base, ATTN_Q, d)
                krows = _attn_rows(base, 2 * ATTN_Q, d)
                arows = _attn_rows(base + (tr - sb), 2 * ATTN_Q, d)
                qs = _stack_heads(q_ref[qrows, :], head0).astype(BF16)
                dos = _stack_heads(do_ref[qrows, :], head0)
                dosb = dos.astype(BF16)
                ov = o_ref[qrows, :]
                delta = jnp.sum(dos * jnp.concatenate([ov, ov], axis=0), axis=-1, keepdims=True)
                lsev = lse_ref[qrows, :]
                lse_s = jnp.concatenate(
                    [jnp.sum(jnp.where(lane == h * ATTN_HEAD_DIM, lsev, 0.0), axis=-1, keepdims=True) for h in range(2)], axis=0)
                kb = kbuf[krows, :].astype(BF16)
                vb = vbuf[krows, :].astype(BF16)
                p = jnp.exp(_dot(qs, kb, 1, 1) * scale + bias - lse_s)
                ds = (p * (_dot(dosb, vb, 1, 1) - delta) * scale).astype(BF16)
                dq_ref[qrows, :] = _unstack_heads(_dot(ds, kb, 1, 0), head0)
                dk_acc[arows, :] += _dot(ds, qs, 0, 0)
                dv_acc[arows, :] += _dot(p.astype(BF16), dosb, 0, 0)
                return carry

            lax.fori_loop(0, nsub * d, per_block, 0, unroll=4)

        dk_ref[...] = dk_acc[pl.ds(0, tr), :]
        dv_ref[...] = dv_acc[pl.ds(0, tr), :]
        dk_acc[pl.ds(0, tr), :] = dk_acc[pl.ds(tr, tr), :]
        dv_acc[pl.ds(0, tr), :] = dv_acc[pl.ds(tr, tr), :]

    def cur(n):
        return jnp.minimum(n, ntiles - 1)

    def spec(col, prev):
        if prev:
            return pl.BlockSpec((sb, LANES), lambda hp, n: (jnp.maximum(cur(n) * nsub - 1, 0), col + hp))
        return pl.BlockSpec((tr, LANES), lambda hp, n: (cur(n), col + hp))

    row_spec = pl.BlockSpec((tr, LANES), lambda hp, n: (cur(n), hp))
    dq_out = pl.BlockSpec((tr, LANES), lambda hp, n: (cur(n), 2 * g + hp))
    kv_out = pl.BlockSpec((tr, LANES), lambda hp, n: (jnp.maximum(n - 1, 0), 2 * g + hp))
    shape = jax.ShapeDtypeStruct((rows, len(ATTN_PATTERNS) * 2 * LANES), F32)
    ins = [qkv, qkv, qkv, qkv, qkv, do, o, lse]
    in_specs = [spec(qc, False), spec(kc, False), spec(kc, True), spec(vc, False), spec(vc, True),
                row_spec, row_spec, row_spec]
    aliases = {}
    if prev is not None:
        aliases = {len(ins) + t: t for t in range(3)}
        ins = ins + list(prev)
        in_specs = in_specs + [ANY] * 3
    n_in = len(ins)

    def entry(*refs):
        body(*refs[:8], *refs[n_in:])

    return pl.pallas_call(
        entry, name=name, grid=(2, ntiles + 1),
        in_specs=in_specs,
        out_specs=[dq_out, kv_out, kv_out],
        out_shape=[shape, shape, shape],
        input_output_aliases=aliases,
        scratch_shapes=[pltpu.VMEM((sb + tr, LANES), F32), pltpu.VMEM((sb + tr, LANES), F32),
                        pltpu.VMEM((2 * tr, LANES), F32), pltpu.VMEM((2 * tr, LANES), F32)],
        compiler_params=_params(("parallel", "arbitrary")),
    )(*ins)


def _mem_probs(q, k):
    s = _dot(q.astype(BF16), k.astype(BF16), 1, 1) * (MEM_HEAD_DIM ** -0.5)
    e = jnp.exp(s - jnp.max(s, axis=-1, keepdims=True))
    return e / jnp.sum(e, axis=-1, keepdims=True)


def _mem_attn_fwd(mq, kv, *, tq, name):
    rows = mq.shape[0]

    def body(q_ref, k_ref, v_ref, o_ref):
        p = _mem_probs(q_ref[...], k_ref[...])
        o_ref[...] = _dot(p.astype(BF16), v_ref[...].astype(BF16), 1, 0)

    return pl.pallas_call(
        body, name=name, grid=(rows // tq, MEM_HEADS),
        in_specs=[pl.BlockSpec((tq, LANES), lambda i, h: (i, h)),
                  pl.BlockSpec((MEM_LEN, LANES), lambda i, h: (0, h)),
                  pl.BlockSpec((MEM_LEN, LANES), lambda i, h: (0, MEM_HEADS + h))],
        out_specs=pl.BlockSpec((tq, LANES), lambda i, h: (i, h)),
        out_shape=jax.ShapeDtypeStruct((rows, MEM_HEADS * LANES), F32),
        compiler_params=_params(("parallel", "parallel")),
    )(mq, kv, kv)


def _mem_attn_bwd(mq, kv, dmo, *, tq, name):
    rows = mq.shape[0]
    scale = MEM_HEAD_DIM ** -0.5

    def body(q_ref, k_ref, v_ref, do_ref, dq_ref, dk_ref, dv_ref):
        i = pl.program_id(1)
        qb = q_ref[...].astype(BF16)
        kb = k_ref[...].astype(BF16)
        vb = v_ref[...].astype(BF16)
        dob = do_ref[...].astype(BF16)
        p = _mem_probs(q_ref[...], k_ref[...])
        dp = _dot(dob, vb, 1, 1)
        ds = (p * (dp - jnp.sum(p * dp, axis=-1, keepdims=True)) * scale).astype(BF16)
        dq_ref[...] = _dot(ds, kb, 1, 0).astype(dq_ref.dtype)

        @pl.when(i == 0)
        def _():
            dk_ref[...] = jnp.zeros_like(dk_ref)
            dv_ref[...] = jnp.zeros_like(dv_ref)

        dk_ref[...] += _dot(ds, qb, 0, 0)
        dv_ref[...] += _dot(p.astype(BF16), dob, 0, 0)

    kv_out = pl.BlockSpec((MEM_LEN, LANES), lambda h, i: (0, h))
    kv_shape = jax.ShapeDtypeStruct((MEM_LEN, MEM_HEADS * LANES), F32)
    return pl.pallas_call(
        body, name=name, grid=(MEM_HEADS, rows // tq),
        in_specs=[pl.BlockSpec((tq, LANES), lambda h, i: (i, h)),
                  pl.BlockSpec((MEM_LEN, LANES), lambda h, i: (0, h)),
                  pl.BlockSpec((MEM_LEN, LANES), lambda h, i: (0, MEM_HEADS + h)),
                  pl.BlockSpec((tq, LANES), lambda h, i: (i, h))],
        out_specs=[pl.BlockSpec((tq, LANES), lambda h, i: (i, h)), kv_out, kv_out],
        out_shape=[jax.ShapeDtypeStruct((rows, MEM_HEADS * LANES), BF16), kv_shape, kv_shape],
        compiler_params=_params(("parallel", "arbitrary")),
    )(mq, kv, kv, dmo)


def _resident(shape):
    return pl.BlockSpec(shape, lambda i: (0, 0), pipeline_mode=pl.Buffered(1))


def _branch_merge_fwd(acts, wts, zg, b_gate, *, tm, name):
    rows = zg.shape[0]
    d = wts[0].shape[0]

    def body(s_ref, a_ref, m_ref, ws_ref, wa_ref, wm_ref, zg_ref, b_ref, o_ref):
        gt = _sigmoid(zg_ref[...] + b_ref[...])
        acc = None
        for k, (x_ref, w_ref) in enumerate(((s_ref, ws_ref), (a_ref, wa_ref), (m_ref, wm_ref))):
            term = gt[:, k * d:(k + 1) * d] * _dot(x_ref[...].astype(BF16), w_ref[...], 1, 1)
            acc = term if acc is None else acc + term
        o_ref[...] = acc.astype(BF16)

    return pl.pallas_call(
        body, name=name, grid=(rows // tm,),
        in_specs=[pl.BlockSpec((tm, x.shape[1]), lambda i: (i, 0)) for x in acts] + [_resident(w.shape) for w in wts]
        + [pl.BlockSpec((tm, 3 * d), lambda i: (i, 0)), pl.BlockSpec((1, 3 * d), lambda i: (0, 0))],
        out_specs=pl.BlockSpec((tm, d), lambda i: (i, 0)), out_shape=jax.ShapeDtypeStruct((rows, d), BF16),
        compiler_params=_params(("parallel",)),
    )(*acts, *wts, zg, b_gate)


def _branch_merge_bwd(dmerged, acts, wts, zg, b_gate, *, tm, name):
    rows = zg.shape[0]
    d = wts[0].shape[0]

    def body(dm_ref, s_ref, a_ref, m_ref, ws_ref, wa_ref, wm_ref, zg_ref, b_ref,
             ds_ref, da_ref, dmm_ref, dws_ref, dwa_ref, dwm_ref, dzg_ref, db_ref):
        i = pl.program_id(0)

        @pl.when(i == 0)
        def _():
            for r in (dws_ref, dwa_ref, dwm_ref, db_ref):
                r[...] = jnp.zeros_like(r)

        gt = _sigmoid(zg_ref[...] + b_ref[...])
        dm = dm_ref[...]
        groups = ((s_ref, ws_ref, ds_ref, dws_ref), (a_ref, wa_ref, da_ref, dwa_ref), (m_ref, wm_ref, dmm_ref, dwm_ref))
        for k, (x_ref, w_ref, dx_ref, dw_ref) in enumerate(groups):
            cs = pl.ds(k * d, d)
            gk = gt[:, k * d:(k + 1) * d]
            xb = x_ref[...].astype(BF16)
            br = _dot(xb, w_ref[...], 1, 1)
            dbr = (dm * gk).astype(BF16)
            dx_ref[...] = _dot(dbr, w_ref[...], 1, 0)
            dw_ref[...] += _dot(dbr, xb, 0, 0)
            dzg = dm * br * gk * (1.0 - gk)
            dzg_ref[:, cs] = dzg.astype(BF16)
            db_ref[:, cs] += jnp.sum(dzg, axis=0, keepdims=True)

    row = lambda w: pl.BlockSpec((tm, w), lambda i: (i, 0))
    whole = lambda shape: pl.BlockSpec(shape, lambda i: (0, 0))
    return pl.pallas_call(
        body, name=name, grid=(rows // tm,),
        in_specs=[row(d)] + [row(x.shape[1]) for x in acts] + [_resident(w.shape) for w in wts] + [row(3 * d), whole((1, 3 * d))],
        out_specs=[row(x.shape[1]) for x in acts] + [whole(w.shape) for w in wts] + [row(3 * d), whole((1, 3 * d))],
        out_shape=[jax.ShapeDtypeStruct(x.shape, F32) for x in acts] + [jax.ShapeDtypeStruct(w.shape, F32) for w in wts]
        + [jax.ShapeDtypeStruct((rows, 3 * d), BF16), jax.ShapeDtypeStruct((1, 3 * d), F32)],
        compiler_params=_params(("arbitrary",)),
    )(dmerged, *acts, *wts, zg, b_gate)


def _adamw(w, g, m, v, *, tr, name):
    rows, cols = w.shape[-2:]
    assert rows % tr == 0, (name, rows, tr)

    def body(w_ref, g_ref, m_ref, v_ref, g_out, d_ref, nm_ref, nv_ref):
        gv = g_ref[...]
        m2 = ADAM_B1 * m_ref[...] + (1.0 - ADAM_B1) * gv
        v2 = ADAM_B2 * v_ref[...] + (1.0 - ADAM_B2) * (gv * gv)
        m_hat = m2 / (1.0 - ADAM_B1 ** ADAM_STEP)
        v_hat = v2 / (1.0 - ADAM_B2 ** ADAM_STEP)
        g_out[...] = gv
        d_ref[...] = -ADAM_LR * (m_hat / (jnp.sqrt(v_hat) + ADAM_EPS) + ADAM_WD * w_ref[...])
        nm_ref[...] = m2
        nv_ref[...] = v2

    flat = pl.BlockSpec((tr, cols), lambda i: (i, 0))
    blk = flat if w.ndim == 2 else pl.BlockSpec((None, tr, cols), lambda i: (0, i, 0))
    shape = jax.ShapeDtypeStruct(w.shape, F32)
    return pl.pallas_call(
        body, name=name, grid=(rows // tr,), in_specs=[blk, flat, blk, blk], out_specs=[blk] * 4,
        out_shape=[shape] * 4, compiler_params=_params(("parallel",)),
    )(w, g, m, v)


ANY = pl.BlockSpec(memory_space=pl.ANY)


def _position():
    return lax.axis_index("x"), lax.axis_index("y"), lax.axis_index("c")


def _other_chips(x, y):
    return ((1 - x, y), (x, 1 - y), (1 - x, 1 - y))


def _remote(src, dst, send_sem, recv_sem, dev):
    return pltpu.make_async_remote_copy(src_ref=src, dst_ref=dst, send_sem=send_sem, recv_sem=recv_sem,
                                        device_id=dev, device_id_type=MESH)


def _gather_exchange(shards):
    nb = len(shards)

    def rows_of(i, owner, core):
        rs = shards[i].shape[0]
        return pl.ds(pl.multiple_of(owner * rs + core * (rs // 2), 16), rs // 2)

    def first_leg(ins, outs, send_sems, recv_sems, i, j):
        x, y, c = _position()
        px, py = _other_chips(x, y)[j]
        half = shards[i].shape[0] // 2
        mine = ins[i].at[pl.ds(pl.multiple_of(c * half, 16), half)]
        return _remote(mine, outs[i].at[rows_of(i, 2 * x + y, c)], send_sems.at[i, j], recv_sems.at[i, j], (px, py, c))

    def passed_on(outs, send_sems, recv_sems, i, j, core):
        x, y, c = _position()
        px, py = _other_chips(x, y)[j]
        rows = outs[i].at[rows_of(i, 2 * px + py, core)]
        return _remote(rows, rows, send_sems.at[i, 3 + j], recv_sems.at[i, 3 + j], (x, y, 1 - c))

    def own_block(ins, outs, send_sems, recv_sems, i):
        x, y, c = _position()
        rs = shards[i].shape[0]
        place = outs[i].at[pl.ds(pl.multiple_of((2 * x + y) * rs, 16), rs)]
        return _remote(ins[i], place, send_sems.at[i, 6], recv_sems.at[i, 6], (x, y, 1 - c))

    def start(ins, outs, send_sems, recv_sems):
        for i in range(nb):
            own_block(ins, outs, send_sems, recv_sems, i).start()
            for j in range(3):
                first_leg(ins, outs, send_sems, recv_sems, i, j).start()

    def finish(ins, outs, send_sems, recv_sems):
        x, y, c = _position()
        for i in range(nb):
            for j, (px, py) in enumerate(_other_chips(x, y)):
                landed = outs[i].at[rows_of(i, 2 * px + py, c)]
                _remote(landed, landed, send_sems.at[i, j], recv_sems.at[i, j], (px, py, c)).wait_recv()
                passed_on(outs, send_sems, recv_sems, i, j, c).start()
        for i in range(nb):
            own_block(ins, outs, send_sems, recv_sems, i).wait()
            for j in range(3):
                passed_on(outs, send_sems, recv_sems, i, j, 1 - c).wait_recv()
        for i in range(nb):
            for j in range(3):
                first_leg(ins, outs, send_sems, recv_sems, i, j).wait_send()
                passed_on(outs, send_sems, recv_sems, i, j, c).wait_send()

    return _Exchange(ins=list(shards), outs=[jax.ShapeDtypeStruct((N_CHIPS * s.shape[0], s.shape[1]), s.dtype) for s in shards],
                     aliases={}, sems=[(nb, 7), (nb, 7)], start=start, finish=finish)


def _run_exchange(ex, *, name):
    n_in, n_out = len(ex.ins), len(ex.outs)

    def body(*refs):
        c_in, c_out, sems = refs[:n_in], refs[n_in:n_in + n_out], refs[n_in + n_out:]
        ex.start(c_in, c_out, *sems)
        ex.finish(c_in, c_out, *sems)

    return pl.pallas_call(
        body, name=name, in_specs=[ANY] * n_in, out_specs=[ANY] * n_out, out_shape=list(ex.outs),
        input_output_aliases=dict(ex.aliases),
        scratch_shapes=[pltpu.SemaphoreType.DMA(s) for s in ex.sems],
    )(*ex.ins)


def _row_tile(rows):
    return max(t for t in range(16, min(rows, 512) + 1, 16) if rows % t == 0)


def _exchange_halves(grads, *, name):
    nb = len(grads)

    def body(*refs):
        ins, outs = refs[:nb], refs[nb:2 * nb]
        send_sems, recv_sems = refs[2 * nb:]
        x, y, c = _position()
        copies = []
        for i in range(nb):
            cp = _remote(ins[i].at[:, 1 - c], outs[i], send_sems.at[i], recv_sems.at[i], (x, y, 1 - c))
            cp.start()
            copies.append(cp)
        for cp in copies:
            cp.wait()

    return pl.pallas_call(
        body, name=name, in_specs=[ANY] * nb, out_specs=[ANY] * nb,
        out_shape=[jax.ShapeDtypeStruct((N_CHIPS, g.shape[2], g.shape[3]), F32) for g in grads],
        scratch_shapes=[pltpu.SemaphoreType.DMA((nb,)), pltpu.SemaphoreType.DMA((nb,))],
    )(*grads)


def _pair_sum(g4, got, c_arr, *, name):
    _, _, half, cols = g4.shape
    tr = _row_tile(half)

    def body(c_ref, g_ref, t_ref, p_ref, pb_ref):
        sm = g_ref[...] + t_ref[...]
        p_ref[...] = sm
        pb_ref[...] = sm.astype(BF16)

    blk = pl.BlockSpec((None, tr, cols), lambda j, i, c_ref: (j, i, 0))
    grid_spec = pltpu.PrefetchScalarGridSpec(
        num_scalar_prefetch=1, grid=(N_CHIPS, half // tr),
        in_specs=[pl.BlockSpec((None, None, tr, cols), lambda j, i, c_ref: (j, c_ref[0], i, 0)), blk],
        out_specs=[blk, blk])
    return pl.pallas_call(
        body, name=name, grid_spec=grid_spec,
        out_shape=[jax.ShapeDtypeStruct((N_CHIPS, half, cols), F32), jax.ShapeDtypeStruct((N_CHIPS, half, cols), BF16)],
        compiler_params=_params(("parallel", "parallel")),
    )(c_arr, g4, got)


def _scatter_exchange(parts):
    nb = len(parts)

    def copies(ins, outs, send_sems, recv_sems):
        x, y, c = _position()
        return [_remote(ins[i].at[2 * px + py], outs[i].at[j], send_sems.at[i, j], recv_sems.at[i, j], (px, py, c))
                for i in range(nb) for j, (px, py) in enumerate(_other_chips(x, y))]

    def start(ins, outs, send_sems, recv_sems):
        for cp in copies(ins, outs, send_sems, recv_sems):
            cp.start()

    def finish(ins, outs, send_sems, recv_sems):
        for cp in copies(ins, outs, send_sems, recv_sems):
            cp.wait()

    return _Exchange(ins=list(parts), outs=[jax.ShapeDtypeStruct((3,) + p.shape[1:], p.dtype) for p in parts],
                     aliases={}, sems=[(nb, 3), (nb, 3)], start=start, finish=finish)


def _owner_sum(p, got, chip_arr, c_arr, *, replicated, name):
    _, half, cols = p.shape
    tr = _row_tile(half)

    def body(chip_ref, c_ref, p_ref, r_ref, o_ref):
        o_ref[...] = ((p_ref[...] + r_ref[0].astype(F32)) + r_ref[1].astype(F32)) + r_ref[2].astype(F32)

    if replicated:
        out_spec = pl.BlockSpec((None, None, tr, cols), lambda i, chip_ref, c_ref: (chip_ref[0], c_ref[0], i, 0))
        out_shape = jax.ShapeDtypeStruct((N_CHIPS, 2, half, cols), F32)
    else:
        out_spec = pl.BlockSpec((None, tr, cols), lambda i, chip_ref, c_ref: (c_ref[0], i, 0))
        out_shape = jax.ShapeDtypeStruct((2, half, cols), F32)
    grid_spec = pltpu.PrefetchScalarGridSpec(
        num_scalar_prefetch=2, grid=(half // tr,),
        in_specs=[pl.BlockSpec((None, tr, cols), lambda i, chip_ref, c_ref: (chip_ref[0], i, 0)),
                  pl.BlockSpec((3, tr, cols), lambda i, chip_ref, c_ref: (0, i, 0))],
        out_specs=out_spec)
    return pl.pallas_call(
        body, name=name, grid_spec=grid_spec, out_shape=out_shape,
        compiler_params=_params(("parallel",)),
    )(chip_arr, c_arr, p, got)


def _share_reduced(bufs):
    nb = len(bufs) - 1

    def body(*refs):
        outs = refs[nb + 1:2 * nb + 2]
        send_sems, recv_sems = refs[2 * nb + 2:]
        x, y, c = _position()
        chip = 2 * x + y
        sends = []
        for i in range(nb):
            cp = _remote(outs[i].at[c], outs[i].at[c], send_sems.at[i], recv_sems.at[i], (x, y, 1 - c))
            cp.start()
            sends.append(cp)
        small = outs[nb]
        peers = [(fx, fy, fc) for fx in (0, 1) for fy in (0, 1) for fc in (0, 1) if fx + fy + fc > 0]
        for k, (fx, fy, fc) in enumerate(peers):
            dev = (x ^ fx, y ^ fy, c ^ fc)
            cp = _remote(small.at[chip, c], small.at[chip, c], send_sems.at[nb + k], recv_sems.at[nb + k], dev)
            cp.start()
            sends.append(cp)
        for i in range(nb):
            dst = outs[i].at[1 - c]
            _remote(dst, dst, send_sems.at[i], recv_sems.at[i], (x, y, 1 - c)).wait_recv()
        for k, (fx, fy, fc) in enumerate(peers):
            dst = small.at[2 * (x ^ fx) + (y ^ fy), c ^ fc]
            _remote(dst, dst, send_sems.at[nb + k], recv_sems.at[nb + k], (x ^ fx, y ^ fy, c ^ fc)).wait_recv()
        for cp in sends:
            cp.wait_send()

    n_all = nb + 1
    return pl.pallas_call(
        body, name="grad_share_reduced", in_specs=[ANY] * n_all, out_specs=[ANY] * n_all,
        out_shape=[jax.ShapeDtypeStruct(b.shape, b.dtype) for b in bufs],
        input_output_aliases={i: i for i in range(n_all)},
        scratch_shapes=[pltpu.SemaphoreType.DMA((nb + 7,)), pltpu.SemaphoreType.DMA((nb + 7,))],
    )(*bufs)


class _GradReducer:
    def __init__(self, c_arr, chip_arr):
        self.c_arr, self.chip_arr = c_arr, chip_arr
        self.pairs, self.landed = {}, {}

    def _pair_sums(self, names, grads):
        full = [g.reshape(N_CHIPS, 2, g.shape[0] // (2 * N_CHIPS), g.shape[1]) for g in grads]
        got = _exchange_halves(full, name="grad_exchange_" + names[0])
        for n, g, t in zip(names, full, got):
            self.pairs[n] = _pair_sum(g, t, self.c_arr, name="grad_pair_sum_" + n)

    def scatter(self, names, grads):
        self._pair_sums(names, grads)
        return _scatter_exchange([self.pairs[n][1] for n in names])

    def collect(self, names, bufs):
        self.landed.update(zip(names, bufs))

    def finish(self, names, grads, order):
        self.collect(names, _run_exchange(self.scatter(names, grads), name="grad_scatter_" + names[0]))
        totals = [_owner_sum(self.pairs[n][0], self.landed[n], self.chip_arr, self.c_arr, replicated=(n == order[-1]),
                             name="grad_owner_sum_" + n) for n in order]
        return _share_reduced(totals)


def _pack_small(vals):
    flat = jnp.concatenate([vals[name].reshape(-1) for name, _ in SMALL])
    return jnp.pad(flat, (0, N_CHIPS * SMALL_ROWS * 1024 - SMALL_ELEMS)).reshape(N_CHIPS * SMALL_ROWS, 1024)


def _unpack_small(buf):
    flat = buf.reshape(-1)
    out, off = {}, 0
    for name, shape in SMALL:
        n = int(np.prod(shape))
        out[name] = flat[off:off + n].reshape(shape)
        off += n
    return out


EARLY_REDUCED = (("w_down",), ("w_up",), ("w_o", "w_ssm_br", "w_attn_br", "w_mem_br", "w_glu", "w_mem_kv"), ("w_in",))


def _device_step(x, mem, tgt, w, p, *, shards, reducer):
    rows = x.shape[0]
    w = dict(w)
    early = EARLY_REDUCED
    gb = {}
    gather_pending = shards is not None

    def reducing(names):
        return reducer.scatter(names, [gb[n] for n in names]) if (reducer is not None and names) else None

    def reduced(names, res):
        if reducer is None or not names:
            return res
        reducer.collect(names, res[1])
        return res[0]

    def fetching(names):
        return _gather_exchange([shards[n] for n in names]) if gather_pending else None

    def fetched(names, res):
        if not gather_pending:
            return res
        w.update(zip(names, res[1]))
        return res[0]

    first_use = (("w_glu", "w_ssm_br", "w_attn_br", "w_mem_kv", "w_mem_br", "w_o"), ("w_up",), ("w_down",))
    g1, gm, g2 = p["norm1_g"], p["mem_norm_g"], p["norm2_g"]
    gf = p["final_g"].reshape(1, D_MODEL)
    ssm_args = (p["ssm_lambda_re"][0], p["ssm_lambda_im"][0], p["ssm_log_dt"][0], p["ssm_b_re"][0],
                p["ssm_b_im"][0], p["ssm_c_re"][0], p["ssm_c_im"][0])
    (a_lay, b_blk, c_blk), ssm_vjp = jax.vjp(_ssm_matrices, *ssm_args)
    a_conj = a_lay * _to_scan_layout(jnp.stack([jnp.ones((N_STATES,), F32), -jnp.ones((N_STATES,), F32)]))[None, :]
    dd = p["ssm_d"].reshape(1, SSM_WIDTH)
    win_t = w["w_in"]
    mm = _matmul

    n1 = _rmsnorm_fwd(x, g1, tm=512, name="norm1")
    u = mm(n1, win_t, m=rows, n=512, k=1024, tb=True, tm=2048, tn=512, tk=1024, out_dtypes=(F32,), name="in_u")
    qkv = fetched(first_use[0], mm(n1, win_t, m=rows, n=2304, k=1024, tb=True, tm=2048, tn=256, tk=1024,
                                   b_off=(OFF_QKV // 256, 0), out_dtypes=(F32,), carry=fetching(first_use[0]), name="in_qkv"))
    mq = mm(n1, win_t, m=rows, n=512, k=1024, tb=True, tm=2048, tn=256, tk=1024, b_off=(OFF_MQ // 256, 0),
            out_dtypes=(F32,), name="in_mq")
    zg = fetched(first_use[1], mm(n1, win_t, m=rows, n=3072, k=1024, tb=True, tm=2048, tn=256, tk=1024,
                                  b_off=(OFF_ZG // 256, 0), out_dtypes=(F32,), carry=fetching(first_use[1]), name="in_zg"))

    u_i = _interleave(u)
    ends = _ssm_ends(a_lay, u_i, b_blk, transpose=False, reverse=False, tt=512, name="ssm_fwd_ends")
    s, ys_i, s_entry = _ssm_fwd(a_lay, u_i, b_blk, c_blk, ends, tt=512, name="ssm_fwd")
    ys = _deinterleave(ys_i)
    y0, tglu, y2 = _glu_fwd(ys, u, dd, w["w_glu"], p["b_glu"], tm=512, name="glu_fwd")

    outs, lses = [], []
    for g, (_, d) in enumerate(ATTN_PATTERNS):
        o_g, lse_g = _attn_fwd(qkv, g, d, name=f"attn_fwd_{g}")
        outs.append(o_g)
        lses.append(lse_g)
    o, lse = _attn_merge(outs, lses, tm=1024, name="attn_merge")

    mn = _rmsnorm_fwd(mem, gm, tm=MEM_LEN, name="mem_norm")
    kv = mm(mn, w["w_mem_kv"], m=MEM_LEN, n=1024, k=1024, tm=MEM_LEN, tn=1024, tk=1024, out_dtypes=(F32,), name="mem_kv")
    mo = _mem_attn_fwd(mq, kv, tq=1024, name="mem_attn_fwd")

    branch_acts = (y2, o, mo)
    branch_wts = (w["w_ssm_br"], w["w_attn_br"], w["w_mem_br"])
    merged = _branch_merge_fwd(branch_acts, branch_wts, zg, p["b_gate"], tm=256, name="branch_merge_fwd")
    add = lambda acc, r: (acc + r,)
    h1, n2 = mm(merged, w["w_o"], m=rows, n=1024, k=1024, tm=1024, tn=1024, tk=1024, out_dtypes=(F32, BF16),
                aux=((x, "mn"), (g2, "row")), epilogue=_residual_norm_epilogue, name="out_proj")
    relu2 = lambda acc: (jnp.square(jnp.maximum(acc, 0.0)),)
    act = fetched(first_use[2], mm(n2, w["w_up"], m=rows, n=D_FF, k=1024, tb=True, tm=1024, tn=1024, tk=1024,
                                   out_dtypes=(BF16,), epilogue=relu2, carry=fetching(first_use[2]), name="mlp_up"))
    h2 = mm(act, w["w_down"], m=rows, n=1024, k=D_FF, tm=1024, tn=1024, tk=1024, out_dtypes=(F32,),
            aux=((h1, "mn"),), epilogue=add, name="mlp_down")
    dh2, loss, d_gf = _loss_head(h2, tgt, gf, tm=512, name="loss_head")

    gs = {"final_g": d_gf.reshape(D_MODEL)}
    drelu2 = lambda acc, actv: (acc * (2.0 * jnp.sqrt(actv.astype(F32))),)
    dup = mm(dh2, w["w_down"], m=rows, n=D_FF, k=1024, tb=True, tm=1024, tn=2048, tk=1024, out_dtypes=(BF16,),
             aux=((act, "mn"),), epilogue=drelu2, name="d_act")
    gb["w_down"] = mm(act, dh2, m=D_FF, n=1024, k=rows, ta=True, tm=1024, tn=1024, tk=1024, out_dtypes=(F32,), name="dw_down")
    gb["w_up"] = reduced(early[0], mm(dup, n2, m=D_FF, n=1024, k=rows, ta=True, tm=1024, tn=1024, tk=1024,
                                      out_dtypes=(F32,), carry=reducing(early[0]), name="dw_up"))
    dh1, gs["norm2_g"] = reduced(early[1], mm(dup, w["w_up"], m=rows, n=1024, k=D_FF, tm=1024, tn=1024, tk=1024,
                                              out_dtypes=(F32,), aux=((h1, "mn"), (dh2, "mn"), (g2, "row")),
                                              epilogue=_rmsnorm_bwd_epilogue, n_sums=1, carry=reducing(early[1]),
                                              name="d_n2"))
    dmerged = mm(dh1, w["w_o"], m=rows, n=1024, k=1024, tb=True, tm=1024, tn=1024, tk=1024, out_dtypes=(F32,), name="d_merged")
    gb["w_o"] = mm(merged, dh1, m=1024, n=1024, k=rows, ta=True, tm=1024, tn=1024, tk=1024, out_dtypes=(F32,), name="dw_o")
    (dy2, do, dmo, gb["w_ssm_br"], gb["w_attn_br"], gb["w_mem_br"], dzg, gs["b_gate"]) = _branch_merge_bwd(
        dmerged, branch_acts, branch_wts, zg, p["b_gate"], tm=256, name="branch_merge_bwd")

    dy0, dt, y1, gs["b_glu"], d_dd = _glu_bwd(dy2, y0, tglu, u, w["w_glu"], tm=512, name="glu_bwd")
    gs["ssm_d"] = d_dd.reshape(1, SSM_GROUPS, SSM_GROUP_SIZE)
    gb["w_glu"] = mm(y1, dt, m=512, n=512, k=rows, ta=True, tm=512, tn=512, tk=1024, out_dtypes=(F32,), name="dw_glu")
    dy0_i = _interleave(dy0)
    lam_ends = _ssm_ends(a_conj, dy0_i, c_blk, transpose=True, reverse=True, tt=512, name="ssm_bwd_ends")
    du_i, d_b_blk, d_c_blk, d_a_lay = _ssm_bwd(a_conj, dy0_i, u_i, s, s_entry, b_blk, c_blk, dd, lam_ends, tt=512,
                                                name="ssm_bwd")
    du = _deinterleave(du_i)
    d_ssm = ssm_vjp((d_a_lay, d_b_blk, d_c_blk))
    for name, val in zip(("ssm_lambda_re", "ssm_lambda_im", "ssm_log_dt", "ssm_b_re", "ssm_b_im", "ssm_c_re", "ssm_c_im"), d_ssm):
        gs[name] = val[None]

    dqkv = None
    for g, (_, d) in enumerate(ATTN_PATTERNS):
        dqkv = _attn_bwd(qkv, do, o, lse, g, d, dqkv, name=f"attn_bwd_{g}")

    dmq, dmk, dmv = _mem_attn_bwd(mq, kv, dmo, tq=1024, name="mem_attn_bwd")
    dkv = jnp.concatenate([dmk, dmv], axis=1)
    gb["w_mem_kv"] = mm(mn, dkv, m=1024, n=1024, k=MEM_LEN, ta=True, tm=1024, tn=1024, tk=MEM_LEN, out_dtypes=(F32,), name="dw_mem_kv")
    dmn = mm(dkv, w["w_mem_kv"], m=MEM_LEN, n=1024, k=1024, tb=True, tm=MEM_LEN, tn=1024, tk=1024, out_dtypes=(F32,), name="d_mn")
    _, gs["mem_norm_g"] = _rmsnorm_bwd(mem, gm, dmn, None, tm=MEM_LEN, name="mem_norm_bwd")

    pieces = ((du, OFF_U, "u"), (dqkv[0], OFF_QKV, "q"), (dqkv[1], OFF_QKV + 768, "k"), (dqkv[2], OFF_QKV + 1536, "v"),
              (dmq, OFF_MQ, "mq"), (dzg, OFF_ZG, "zg"))
    dw_rows = []
    for piece, off, tag in pieces:
        width = piece.shape[1]
        tmw = 1024 if width % 1024 == 0 else (768 if width == 768 else 512)
        rides = early[2] if tag == "zg" else ()
        dw_rows.append(reduced(rides, mm(piece, n1, m=width, n=1024, k=rows, ta=True, tm=tmw, tn=1024, tk=1024,
                                         out_dtypes=(F32,), carry=reducing(rides), name="dw_in_" + tag)))
    gb["w_in"] = jnp.concatenate(dw_rows, axis=0)
    dx, gs["norm1_g"] = reduced(early[3], _sum_matmul(
        [piece for piece, _, _ in pieces], win_t, [off for _, off, _ in pieces], tm=512,
        aux=((x, "mn"), (dh1, "mn"), (g1, "row")), epilogue=_rmsnorm_bwd_epilogue, n_sums=1,
        carry=reducing(early[3]), name="d_n1"))
    return loss, dx, gb, gs


def kernel(x, mem, norm1_g, mem_norm_g, w_in, b_gate, ssm_lambda_re, ssm_lambda_im, ssm_log_dt, ssm_b_re, ssm_b_im, ssm_c_re, ssm_c_im, ssm_d, w_glu, b_glu, w_ssm_br, w_attn_br, w_mem_kv, w_mem_br, w_o, norm2_g, w_up, w_down, final_g, loss_target, m_norm1_g, m_mem_norm_g, m_w_in, m_b_gate, m_ssm_lambda_re, m_ssm_lambda_im, m_ssm_log_dt, m_ssm_b_re, m_ssm_b_im, m_ssm_c_re, m_ssm_c_im, m_ssm_d, m_w_glu, m_b_glu, m_w_ssm_br, m_w_attn_br, m_w_mem_kv, m_w_mem_br, m_w_o, m_norm2_g, m_w_up, m_w_down, m_final_g, v_norm1_g, v_mem_norm_g, v_w_in, v_b_gate, v_ssm_lambda_re, v_ssm_lambda_im, v_ssm_log_dt, v_ssm_b_re, v_ssm_b_im, v_ssm_c_re, v_ssm_c_im, v_ssm_d, v_w_glu, v_b_glu, v_w_ssm_br, v_w_attn_br, v_w_mem_kv, v_w_mem_br, v_w_o, v_norm2_g, v_w_up, v_w_down, v_final_g):
    env = dict(locals())
    weights = {n: env[n] for n in WEIGHT_ORDER}
    moms = {n: env["m_" + n] for n in WEIGHT_ORDER}
    vels = {n: env["v_" + n] for n in WEIGHT_ORDER}
    def shard2d(a):
        return a.reshape(a.shape[-2], a.shape[-1])

    chip = 2 * lax.axis_index("x") + lax.axis_index("y")
    wire = [shard2d(weights[n]).astype(BF16) for n, _, _ in BIG]
    wire = dict(zip([n for n, _, _ in BIG], [s.T if tr else s for s, (_, tr, _) in zip(wire, BIG)]))
    w_in_full = _run_exchange(_gather_exchange([wire.pop("w_in")]), name="all_gather_w_in")[0]
    small = {n: weights[n] for n, _ in SMALL}

    reducer = _GradReducer(lax.axis_index("c").astype(jnp.int32).reshape(1), chip.astype(jnp.int32).reshape(1))
    loss, dx, gb, gs = _device_step(x[0], mem[0], loss_target[0], {"w_in": w_in_full}, small, shards=wire, reducer=reducer)
    *shards, small_grad = reducer.finish(["small"], [_pack_small(gs)], [n for n, _, _ in BIG] + ["small"])
    grads = {}
    for (n, tr, _), sh in zip(BIG, shards):
        sh = sh.reshape(2 * sh.shape[1], sh.shape[2])
        grads[n] = sh.T if tr else sh
    small_grad = small_grad.reshape(N_CHIPS * SMALL_ROWS, 1024)
    grads_small = _unpack_small(small_grad)

    delta, new_m, new_v = {}, {}, {}
    for n, _, _ in BIG:
        grads[n], delta[n], new_m[n], new_v[n] = _adamw(weights[n], grads[n], moms[n], vels[n],
                                                        tr=min(weights[n].shape[-2], 256), name="adamw_" + n)
    _, ds_, ms_, vs_ = _adamw(_pack_small(small), small_grad,
                              _pack_small({n: moms[n] for n, _ in SMALL}), _pack_small({n: vels[n] for n, _ in SMALL}),
                              tr=N_CHIPS * SMALL_ROWS, name="adamw_small")
    for dst, buf in ((delta, ds_), (new_m, ms_), (new_v, vs_)):
        dst.update(_unpack_small(buf))
    grads.update(grads_small)

    total_loss = lax.psum(loss[0, 0], ("x", "y", "c"))
    return (total_loss, dx[None], *[grads[n] for n in WEIGHT_ORDER], *[delta[n] for n in WEIGHT_ORDER],
            *[new_m[n] for n in WEIGHT_ORDER], *[new_v[n] for n in WEIGHT_ORDER])
```

```python
import functools
import math

import numpy as np
import jax
import jax.numpy as jnp
from jax import lax
from jax.experimental import pallas as pl
from jax.experimental.pallas import tpu as pltpu

F32 = jnp.float32
BF16 = jnp.bfloat16

D_MODEL = 1024
SSM_GROUPS = 32
SSM_GROUP_SIZE = 16
SSM_STATE = 64
SSM_WIDTH = 512
N_STATES = SSM_GROUPS * SSM_STATE
SCAN_CB = 1024
ATTN_PATTERNS = ((128, 1), (512, 4), (2048, 16))
ATTN_HEAD_DIM = 64
ATTN_Q = 128
MEM_LEN = 256
MEM_HEAD_DIM = 128
MEM_HEADS = 4
D_FF = 4096
OFF_U, OFF_QKV, OFF_MQ, OFF_ZG = 0, 512, 2816, 3328
IN_WIDTH = 6400
RMS_EPS = 1e-6
NEG_INF = -1e30
ADAM_LR, ADAM_B1, ADAM_B2, ADAM_EPS, ADAM_WD, ADAM_STEP = 0.001, 0.9, 0.999, 1e-08, 0.01, 10

VMEM_LIMIT_BYTES = 48 * 1024 * 1024
VMEM_LIMIT_WIDE_BYTES = 56 * 1024 * 1024
LANES = 128
MESH = pl.DeviceIdType.MESH
N_CHIPS = 4

SCAN_SEGS = 8
SCAN_GROUPS = SCAN_CB // SSM_STATE

BIG = (("w_in", True, (6400, 1024)), ("w_glu", False, (512, 512)), ("w_ssm_br", True, (1024, 512)),
       ("w_attn_br", True, (1024, 256)), ("w_mem_kv", False, (1024, 1024)), ("w_mem_br", True, (1024, 512)),
       ("w_o", False, (1024, 1024)), ("w_up", True, (4096, 1024)), ("w_down", False, (4096, 1024)))
SMALL = (("norm1_g", (1, 1024)), ("mem_norm_g", (1, 1024)), ("b_gate", (1, 3072)),
         ("ssm_lambda_re", (1, 32, 64)), ("ssm_lambda_im", (1, 32, 64)), ("ssm_log_dt", (1, 32)),
         ("ssm_b_re", (1, 32, 64, 16)), ("ssm_b_im", (1, 32, 64, 16)), ("ssm_c_re", (1, 32, 16, 64)),
         ("ssm_c_im", (1, 32, 16, 64)), ("ssm_d", (1, 32, 16)), ("b_glu", (1, 512)),
         ("norm2_g", (1, 1024)), ("final_g", (1024,)))
WEIGHT_ORDER = ("norm1_g", "mem_norm_g", "w_in", "b_gate", "ssm_lambda_re", "ssm_lambda_im", "ssm_log_dt",
                "ssm_b_re", "ssm_b_im", "ssm_c_re", "ssm_c_im", "ssm_d", "w_glu", "b_glu", "w_ssm_br",
                "w_attn_br", "w_mem_kv", "w_mem_br", "w_o", "norm2_g", "w_up", "w_down", "final_g")
SMALL_ELEMS = sum(int(np.prod(s)) for _, s in SMALL)
SMALL_ROWS = 64


def _params(sem, vmem=VMEM_LIMIT_BYTES):
    return pltpu.CompilerParams(dimension_semantics=sem, vmem_limit_bytes=vmem)


def _sigmoid(v):
    return 1.0 / (1.0 + jnp.exp(-v))


_GELU_C = math.sqrt(2.0 / math.pi)


def _gelu(v):
    return 0.5 * v * (1.0 + jnp.tanh(_GELU_C * (v + 0.044715 * v * v * v)))


def _gelu_grad(v):
    th = jnp.tanh(_GELU_C * (v + 0.044715 * v * v * v))
    return 0.5 * (1.0 + th) + 0.5 * v * (1.0 - th * th) * _GELU_C * (1.0 + 3.0 * 0.044715 * v * v)


def _dot(a, b, ca, cb):
    return lax.dot_general(a, b, (((ca,), (cb,)), ((), ())), preferred_element_type=F32)


class _Exchange:
    def __init__(self, ins, outs, aliases, sems, start, finish):
        self.ins, self.outs, self.aliases, self.sems, self.start, self.finish = ins, outs, aliases, sems, start, finish


def _matmul(a, b, *, m, n, k, ta=False, tb=False, tm, tn, tk, out_dtypes, name,
            a_off=(0, 0), b_off=(0, 0), aux=(), epilogue=None, n_sums=0, carry=None):
    assert m % tm == 0 and n % tn == 0 and k % tk == 0, (name, m, n, k, tm, tn, tk)
    nk = k // tk
    n_aux = len(aux)
    n_tiles = len(out_dtypes)
    n_out = n_tiles + n_sums
    ar, ac = a_off
    br, bc = b_off
    if ta:
        a_spec = pl.BlockSpec((tk, tm), lambda i, j, kk: (kk + ar, i + ac))
    else:
        a_spec = pl.BlockSpec((tm, tk), lambda i, j, kk: (i + ar, kk + ac))
    if tb:
        b_spec = pl.BlockSpec((tn, tk), lambda i, j, kk: (j + br, kk + bc))
    else:
        b_spec = pl.BlockSpec((tk, tn), lambda i, j, kk: (kk + br, j + bc))
    aux_specs = []
    for _, kind in aux:
        if kind == "mn":
            aux_specs.append(pl.BlockSpec((tm, tn), lambda i, j, kk: (i, j)))
        else:
            aux_specs.append(pl.BlockSpec((1, tn), lambda i, j, kk: (0, j)))
    ca = 0 if ta else 1
    cb = 1 if tb else 0

    def finish(acc, aux_refs, out_refs, row_tile):
        outs = (acc,) if epilogue is None else epilogue(acc, *[r[...] for r in aux_refs])
        for o_ref, o in zip(out_refs[:n_tiles], outs[:n_tiles]):
            o_ref[...] = o.astype(o_ref.dtype)
        _accumulate_over_rows(out_refs[n_tiles:], outs[n_tiles:], row_tile)

    def body(a_ref, b_ref, *rest):
        aux_refs = rest[:n_aux]
        out_refs = rest[n_aux:n_aux + n_out]
        row_tile = pl.program_id(0)
        prod = _dot(a_ref[...].astype(BF16), b_ref[...].astype(BF16), ca, cb)
        if nk == 1:
            finish(prod, aux_refs, out_refs, row_tile)
            return
        acc_ref = rest[n_aux + n_out]
        kk = pl.program_id(2)

        @pl.when(kk == 0)
        def _():
            acc_ref[...] = prod

        @pl.when(jnp.logical_and(kk > 0, kk < nk - 1))
        def _():
            acc_ref[...] += prod

        @pl.when(kk == nk - 1)
        def _():
            finish(acc_ref[...] + prod, aux_refs, out_refs, row_tile)

    tile = pl.BlockSpec((tm, tn), lambda i, j, kk: (i, j))
    col_sum = pl.BlockSpec((1, tn), lambda i, j, kk: (0, j))
    res = _call_with_carry(
        body, carry, name=name, grid=(m // tm, n // tn, nk), in_specs=[a_spec, b_spec] + aux_specs,
        out_specs=[tile] * n_tiles + [col_sum] * n_sums,
        out_shape=[jax.ShapeDtypeStruct((m, n), dt) for dt in out_dtypes] + [jax.ShapeDtypeStruct((1, n), F32)] * n_sums,
        scratch=[pltpu.VMEM((tm, tn), F32)] if nk > 1 else [], operands=[a, b] + [x for x, _ in aux],
        semantics=("arbitrary" if n_sums else "parallel", "parallel", "arbitrary"))
    main = res[0] if n_out == 1 else tuple(res[:n_out])
    return main if carry is None else (main, list(res[n_out:]))


def _accumulate_over_rows(sum_refs, terms, row_tile):
    for s_ref, term in zip(sum_refs, terms):
        @pl.when(row_tile == 0)
        def _():
            s_ref[...] = term

        @pl.when(row_tile > 0)
        def _():
            s_ref[...] += term


def _call_with_carry(body, carry, *, name, grid, in_specs, out_specs, out_shape, scratch, operands, semantics,
                     vmem=VMEM_LIMIT_BYTES):
    if carry is None:
        return pl.pallas_call(body, name=name, grid=grid, in_specs=in_specs, out_specs=out_specs, out_shape=out_shape,
                              scratch_shapes=scratch, compiler_params=_params(semantics, vmem))(*operands)
    n_in, n_cin, n_out, n_cout, n_scr = len(operands), len(carry.ins), len(out_shape), len(carry.outs), len(scratch)

    def hosted(*refs):
        main_in, c_in = refs[:n_in], refs[n_in:n_in + n_cin]
        main_out = refs[n_in + n_cin:n_in + n_cin + n_out]
        c_out = refs[n_in + n_cin + n_out:n_in + n_cin + n_out + n_cout]
        rest = refs[n_in + n_cin + n_out + n_cout:]
        ids = [pl.program_id(t) for t in range(len(grid))]
        first = functools.reduce(jnp.logical_and, [i == 0 for i in ids])
        last = functools.reduce(jnp.logical_and, [i == g - 1 for i, g in zip(ids, grid)])

        @pl.when(first)
        def _():
            carry.start(c_in, c_out, *rest[n_scr:])

        body(*main_in, *main_out, *rest[:n_scr])

        @pl.when(last)
        def _():
            carry.finish(c_in, c_out, *rest[n_scr:])

    return pl.pallas_call(
        hosted, name=name, grid=grid,
        in_specs=list(in_specs) + [ANY] * n_cin, out_specs=list(out_specs) + [ANY] * n_cout,
        out_shape=list(out_shape) + list(carry.outs),
        input_output_aliases={n_in + i: n_out + o for i, o in carry.aliases.items()},
        scratch_shapes=list(scratch) + [pltpu.SemaphoreType.DMA(s) for s in carry.sems],
        compiler_params=_params(("arbitrary",) * len(grid), vmem),
    )(*operands, *carry.ins)


def _sum_matmul(pieces, b, offs, *, tm, name, aux=(), epilogue=None, n_sums=0, carry=None):
    m = pieces[0].shape[0]
    n = b.shape[1]
    npieces, n_aux = len(pieces), len(aux)

    def body(*refs):
        b_ref = refs[npieces]
        aux_refs = refs[npieces + 1:npieces + 1 + n_aux]
        out_refs = refs[npieces + 1 + n_aux:]
        acc = None
        for p_ref, off in zip(refs[:npieces], offs):
            part = _dot(p_ref[...].astype(BF16), b_ref[pl.ds(off, p_ref.shape[1]), :], 1, 0)
            acc = part if acc is None else acc + part
        outs = (acc,) if epilogue is None else epilogue(acc, *[r[...] for r in aux_refs])
        out_refs[0][...] = outs[0]
        _accumulate_over_rows(out_refs[1:], outs[1:], pl.program_id(0))

    row = pl.BlockSpec((tm, n), lambda i: (i, 0))
    vec = pl.BlockSpec((1, n), lambda i: (0, 0))
    res = _call_with_carry(
        body, carry, name=name, grid=(m // tm,),
        in_specs=[pl.BlockSpec((tm, p.shape[1]), lambda i: (i, 0)) for p in pieces] + [_resident(b.shape)]
        + [row if kind == "mn" else vec for _, kind in aux],
        out_specs=[row] + [vec] * n_sums,
        out_shape=[jax.ShapeDtypeStruct((m, n), F32)] + [jax.ShapeDtypeStruct((1, n), F32)] * n_sums,
        scratch=[], operands=list(pieces) + [b] + [x for x, _ in aux], semantics=("arbitrary" if n_sums else "parallel",),
        vmem=VMEM_LIMIT_WIDE_BYTES)
    main = res[0] if n_sums == 0 else tuple(res[:1 + n_sums])
    return main if carry is None else (main, list(res[1 + n_sums:]))


def _rmsnorm_fwd(x, g, *, tm, name):
    rows, d = x.shape

    def body(x_ref, g_ref, o_ref):
        xv = x_ref[...]
        r = lax.rsqrt(jnp.mean(xv * xv, axis=-1, keepdims=True) + RMS_EPS)
        o_ref[...] = (xv * r * g_ref[...]).astype(o_ref.dtype)

    return pl.pallas_call(
        body, name=name, grid=(rows // tm,),
        in_specs=[pl.BlockSpec((tm, d), lambda i: (i, 0)), pl.BlockSpec((1, d), lambda i: (0, 0))],
        out_specs=pl.BlockSpec((tm, d), lambda i: (i, 0)),
        out_shape=jax.ShapeDtypeStruct((rows, d), BF16),
        compiler_params=_params(("parallel",)),
    )(x, g)


def _residual_norm_epilogue(acc, xv, gv):
    h = acc + xv
    r = lax.rsqrt(jnp.mean(h * h, axis=-1, keepdims=True) + RMS_EPS)
    return h, h * r * gv


def _rmsnorm_bwd_epilogue(dy, xv, resv, gv):
    r = lax.rsqrt(jnp.mean(xv * xv, axis=-1, keepdims=True) + RMS_EPS)
    xhat = xv * r
    dyg = dy * gv
    dx = r * (dyg - xhat * jnp.mean(dyg * xhat, axis=-1, keepdims=True)) + resv
    return dx, jnp.sum(dy * xhat, axis=0, keepdims=True)


def _rmsnorm_bwd(x, g, dy, res, *, tm, name):
    rows, d = x.shape
    has_res = res is not None

    def body(x_ref, g_ref, dy_ref, *rest):
        if has_res:
            res_ref, dx_ref, dg_ref = rest
        else:
            dx_ref, dg_ref = rest
        i = pl.program_id(0)
        xv = x_ref[...]
        r = lax.rsqrt(jnp.mean(xv * xv, axis=-1, keepdims=True) + RMS_EPS)
        xhat = xv * r
        dyv = dy_ref[...]
        dyg = dyv * g_ref[...]
        dx = r * (dyg - xhat * jnp.mean(dyg * xhat, axis=-1, keepdims=True))
        if has_res:
            dx = dx + res_ref[...]
        dx_ref[...] = dx

        @pl.when(i == 0)
        def _():
            dg_ref[...] = jnp.zeros_like(dg_ref)

        dg_ref[...] += jnp.sum(dyv * xhat, axis=0, keepdims=True)

    row_spec = pl.BlockSpec((tm, d), lambda i: (i, 0))
    vec_spec = pl.BlockSpec((1, d), lambda i: (0, 0))
    ins = [x, g, dy] + ([res] if has_res else [])
    return pl.pallas_call(
        body, name=name, grid=(rows // tm,),
        in_specs=[row_spec, vec_spec, row_spec] + ([row_spec] if has_res else []),
        out_specs=[row_spec, vec_spec],
        out_shape=[jax.ShapeDtypeStruct((rows, d), F32), jax.ShapeDtypeStruct((1, d), F32)],
        compiler_params=_params(("arbitrary",)),
    )(*ins)


def _loss_head(h, tgt, g, *, tm, name):
    rows, d = h.shape
    nsteps = rows // tm

    def body(h_ref, t_ref, g_ref, dh_ref, loss_ref, dg_ref, sq_ref):
        i = pl.program_id(0)
        xv = h_ref[...]
        gv = g_ref[...]
        r = lax.rsqrt(jnp.mean(xv * xv, axis=-1, keepdims=True) + RMS_EPS)
        xhat = xv * r
        err = xhat * gv - t_ref[...]
        dyv = err * (1.0 / d)
        dyg = dyv * gv
        dh_ref[...] = r * (dyg - xhat * jnp.mean(dyg * xhat, axis=-1, keepdims=True))

        @pl.when(i == 0)
        def _():
            dg_ref[...] = jnp.zeros_like(dg_ref)
            sq_ref[...] = jnp.zeros_like(sq_ref)

        dg_ref[...] += jnp.sum(dyv * xhat, axis=0, keepdims=True)
        sq_ref[...] += jnp.sum(err * err, axis=0, keepdims=True)

        @pl.when(i == nsteps - 1)
        def _():
            tot = jnp.sum(sq_ref[...], axis=-1, keepdims=True) * (0.5 / d)
            loss_ref[...] = jnp.broadcast_to(tot, loss_ref.shape)

    row_spec = pl.BlockSpec((tm, d), lambda i: (i, 0))
    vec_spec = pl.BlockSpec((1, d), lambda i: (0, 0))
    return pl.pallas_call(
        body, name=name, grid=(nsteps,),
        in_specs=[row_spec, row_spec, vec_spec],
        out_specs=[row_spec, pl.BlockSpec((1, LANES), lambda i: (0, 0)), vec_spec],
        out_shape=[jax.ShapeDtypeStruct((rows, d), F32), jax.ShapeDtypeStruct((1, LANES), F32),
                   jax.ShapeDtypeStruct((1, d), F32)],
        scratch_shapes=[pltpu.VMEM((1, d), F32)],
        compiler_params=_params(("arbitrary",)),
    )(h, tgt, g)


def _to_scan_layout(v):
    lead = v.shape[:-2]
    v = v.reshape(lead + (2, N_STATES // SCAN_CB, SCAN_CB))
    v = jnp.swapaxes(v, -3, -2)
    return v.reshape(lead + (2 * N_STATES,))


def _ssm_matrices(lam_re, lam_im, log_dt, b_re, b_im, c_re, c_im):
    dt = jnp.exp(log_dt)[:, None]
    mag = jnp.exp(lam_re * dt)
    a_re, a_im = mag * jnp.cos(lam_im * dt), mag * jnp.sin(lam_im * dt)
    nr, ni = a_re - 1.0, a_im
    den = lam_re * lam_re + lam_im * lam_im
    coef_re = (nr * lam_re + ni * lam_im) / den
    coef_im = (ni * lam_re - nr * lam_im) / den
    bb_re = coef_re[..., None] * b_re - coef_im[..., None] * b_im
    bb_im = coef_re[..., None] * b_im + coef_im[..., None] * b_re
    a_lay = _to_scan_layout(jnp.stack([a_re.reshape(-1), a_im.reshape(-1)], axis=0))[None, :]
    nblk = SSM_GROUPS // SCAN_GROUPS
    eye = jnp.eye(SCAN_GROUPS, dtype=F32)

    def b_block(bb):
        bb = bb.reshape(nblk, SCAN_GROUPS, SSM_STATE, SSM_GROUP_SIZE)
        return jnp.einsum("gk,jkph->jghkp", eye, bb).reshape(nblk, SCAN_GROUPS * SSM_GROUP_SIZE, SCAN_CB)

    b_blk = jnp.concatenate([b_block(bb_re), b_block(bb_im)], axis=2)

    def c_block(cc):
        cc = cc.reshape(nblk, SCAN_GROUPS, SSM_GROUP_SIZE, SSM_STATE)
        return jnp.einsum("gk,jghp->jkpgh", eye, cc).reshape(nblk, SCAN_CB, SCAN_GROUPS * SSM_GROUP_SIZE)

    c_blk = jnp.concatenate([c_block(c_re), -c_block(c_im)], axis=1)
    return a_lay, b_blk, c_blk


def _interleave(v):
    rows, c = v.shape
    return v.reshape(SCAN_SEGS, rows // SCAN_SEGS, c).transpose(1, 0, 2).reshape(rows, c)


def _deinterleave(v):
    rows, c = v.shape
    return v.reshape(rows // SCAN_SEGS, SCAN_SEGS, c).transpose(1, 0, 2).reshape(rows, c)


def _scan_groups(a_ref, bu_ref, o_ref, state, *, reverse, tt):
    cb = SCAN_CB
    ar = jnp.broadcast_to(a_ref[:, :cb], (SCAN_SEGS, cb))
    ai = jnp.broadcast_to(a_ref[:, cb:], (SCAN_SEGS, cb))
    ngroups = tt // SCAN_SEGS

    def step(i, st):
        sr, si = st
        r0 = pl.multiple_of(((ngroups - 1 - i) if reverse else i) * SCAN_SEGS, SCAN_SEGS)
        blk = bu_ref[pl.ds(r0, SCAN_SEGS), :]
        nr = ar * sr - ai * si + blk[:, :cb]
        ni = ar * si + ai * sr + blk[:, cb:]
        if o_ref is not None:
            o_ref[pl.ds(r0, SCAN_SEGS), :] = jnp.concatenate([nr, ni], axis=1)
        return nr, ni

    return lax.fori_loop(0, ngroups, step, state, unroll=4)


def _segment_entries(a_ref, e_ref, init_ref, *, reverse, seg_len):
    cb = SCAN_CB
    n_sq = seg_len.bit_length() - 1
    assert 1 << n_sq == seg_len, seg_len
    pr, pi = a_ref[:, :cb], a_ref[:, cb:]
    for _ in range(n_sq):
        pr, pi = pr * pr - pi * pi, 2.0 * pr * pi
    cr = jnp.zeros((1, cb), F32)
    ci = jnp.zeros((1, cb), F32)
    order = range(SCAN_SEGS - 1, -1, -1) if reverse else range(SCAN_SEGS)
    for k, seg in enumerate(order):
        if k > 0:
            prev = seg + 1 if reverse else seg - 1
            er, ei = e_ref[prev:prev + 1, :cb], e_ref[prev:prev + 1, cb:]
            cr, ci = pr * cr - pi * ci + er, pr * ci + pi * cr + ei
        init_ref[seg:seg + 1, :] = jnp.concatenate([cr, ci], axis=1)


def _ssm_specs(nt, tt, nch, reverse):
    cb = SCAN_CB
    tmap = (lambda j, kk: (nt - 1 - kk, j)) if reverse else (lambda j, kk: (kk, j))
    return dict(a=pl.BlockSpec((1, 2 * cb), lambda j, kk: (0, j)),
                seg=pl.BlockSpec((SCAN_SEGS, 2 * cb), lambda j, kk: (0, j)),
                chan=pl.BlockSpec((tt, nch), tmap),
                state=pl.BlockSpec((tt, 2 * cb), tmap),
                b=pl.BlockSpec((None, nch, 2 * cb), lambda j, kk: (j, 0, 0)),
                c=pl.BlockSpec((None, 2 * cb, nch), lambda j, kk: (j, 0, 0)))


def _ssm_ends(a_lay, x, blocks, *, transpose, reverse, tt, name):
    rows = x.shape[0]
    nblk = blocks.shape[0]
    nch = x.shape[1] // nblk
    cb = SCAN_CB
    nt = rows // tt
    sp = _ssm_specs(nt, tt, nch, reverse)

    def body(a_ref, x_ref, w_ref, e_ref, bu_ref):
        kk = pl.program_id(1)

        @pl.when(kk == 0)
        def _():
            e_ref[...] = jnp.zeros_like(e_ref)

        bu_ref[...] = _dot(x_ref[...].astype(BF16), w_ref[...].astype(BF16), 1, 1 if transpose else 0)
        sr, si = _scan_groups(a_ref, bu_ref, None, (e_ref[:, :cb], e_ref[:, cb:]), reverse=reverse, tt=tt)
        e_ref[...] = jnp.concatenate([sr, si], axis=1)

    return pl.pallas_call(
        body, name=name, grid=(nblk, nt),
        in_specs=[sp["a"], sp["chan"], sp["c"] if transpose else sp["b"]],
        out_specs=sp["seg"],
        out_shape=jax.ShapeDtypeStruct((SCAN_SEGS, nblk * 2 * cb), F32),
        scratch_shapes=[pltpu.VMEM((tt, 2 * cb), F32)],
        compiler_params=_params(("parallel", "arbitrary")),
    )(a_lay, x, blocks)


def _ssm_fwd(a_lay, u, b_blk, c_blk, ends, *, tt, name):
    rows = u.shape[0]
    nblk = b_blk.shape[0]
    nch = u.shape[1] // nblk
    cb = SCAN_CB
    nt = rows // tt
    sp = _ssm_specs(nt, tt, nch, False)

    def body(a_ref, e_ref, u_ref, b_ref, c_ref, s_ref, y_ref, init_ref, carry_ref):
        kk = pl.program_id(1)

        @pl.when(kk == 0)
        def _():
            _segment_entries(a_ref, e_ref, init_ref, reverse=False, seg_len=rows // SCAN_SEGS)
            carry_ref[...] = init_ref[...]

        s_ref[...] = _dot(u_ref[...].astype(BF16), b_ref[...].astype(BF16), 1, 0)
        sr, si = _scan_groups(a_ref, s_ref, s_ref, (carry_ref[:, :cb], carry_ref[:, cb:]), reverse=False, tt=tt)
        carry_ref[...] = jnp.concatenate([sr, si], axis=1)
        y_ref[...] = _dot(s_ref[...].astype(BF16), c_ref[...].astype(BF16), 1, 0)

    return pl.pallas_call(
        body, name=name, grid=(nblk, nt),
        in_specs=[sp["a"], sp["seg"], sp["chan"], sp["b"], sp["c"]],
        out_specs=[sp["state"], sp["chan"], sp["seg"]],
        out_shape=[jax.ShapeDtypeStruct((rows, nblk * 2 * cb), F32), jax.ShapeDtypeStruct((rows, nblk * nch), F32),
                   jax.ShapeDtypeStruct((SCAN_SEGS, nblk * 2 * cb), F32)],
        scratch_shapes=[pltpu.VMEM((SCAN_SEGS, 2 * cb), F32)],
        compiler_params=_params(("parallel", "arbitrary")),
    )(a_lay, ends, u, b_blk, c_blk)


def _ssm_bwd(a_conj, dy, u, s, s_entry, b_blk, c_blk, dd, ends, *, tt, name):
    rows = u.shape[0]
    nblk = b_blk.shape[0]
    nch = u.shape[1] // nblk
    cb = SCAN_CB
    nt = rows // tt
    sp = _ssm_specs(nt, tt, nch, True)
    groups_per_tile = tt // SCAN_SEGS
    before = pl.BlockSpec((SCAN_SEGS, 2 * cb), lambda j, kk: (jnp.maximum((nt - 1 - kk) * groups_per_tile - 1, 0), j))

    def body(a_ref, e_ref, dy_ref, u_ref, s_ref, before_ref, entry_ref, b_ref, c_ref, dd_ref,
             du_ref, db_ref, dc_ref, da_ref, lam_ref, carry_ref):
        kk = pl.program_id(1)

        @pl.when(kk == 0)
        def _():
            _segment_entries(a_ref, e_ref, carry_ref, reverse=True, seg_len=rows // SCAN_SEGS)
            db_ref[...] = jnp.zeros_like(db_ref)
            dc_ref[...] = jnp.zeros_like(dc_ref)
            da_ref[...] = jnp.zeros_like(da_ref)

        dyv = dy_ref[...]
        dyb = dyv.astype(BF16)
        lam_ref[...] = _dot(dyb, c_ref[...].astype(BF16), 1, 1)
        lr, li = _scan_groups(a_ref, lam_ref, lam_ref, (carry_ref[:, :cb], carry_ref[:, cb:]), reverse=True, tt=tt)
        carry_ref[...] = jnp.concatenate([lr, li], axis=1)

        first = jnp.where(kk == nt - 1, entry_ref[...], before_ref[...])
        rest = tt - SCAN_SEGS
        lam_hi = lam_ref[pl.ds(SCAN_SEGS, rest), :]
        s_lo = s_ref[pl.ds(0, rest), :]
        lam_lo = lam_ref[pl.ds(0, SCAN_SEGS), :]

        def pair(lv, pv):
            lre, lim, pre, pim = lv[:, :cb], lv[:, cb:], pv[:, :cb], pv[:, cb:]
            return (jnp.sum(lre * pre + lim * pim, axis=0, keepdims=True),
                    jnp.sum(lim * pre - lre * pim, axis=0, keepdims=True))

        r1, i1 = pair(lam_hi, s_lo)
        r0, i0 = pair(lam_lo, first)
        da_ref[...] += jnp.concatenate([r1 + r0, i1 + i0], axis=1)

        lamb = lam_ref[...].astype(BF16)
        du_ref[...] = _dot(lamb, b_ref[...].astype(BF16), 1, 1) + dd_ref[...] * dyv
        db_ref[...] += _dot(u_ref[...].astype(BF16), lamb, 0, 0)
        dc_ref[...] += _dot(s_ref[...].astype(BF16), dyb, 0, 0)

    return pl.pallas_call(
        body, name=name, grid=(nblk, nt),
        in_specs=[sp["a"], sp["seg"], sp["chan"], sp["chan"], sp["state"], before, sp["seg"], sp["b"], sp["c"],
                  pl.BlockSpec((1, nch), lambda j, kk: (0, j))],
        out_specs=[sp["chan"], sp["b"], sp["c"], pl.BlockSpec((1, 2 * cb), lambda j, kk: (0, j))],
        out_shape=[jax.ShapeDtypeStruct((rows, nblk * nch), F32), jax.ShapeDtypeStruct(b_blk.shape, F32),
                   jax.ShapeDtypeStruct(c_blk.shape, F32), jax.ShapeDtypeStruct((1, nblk * 2 * cb), F32)],
        scratch_shapes=[pltpu.VMEM((tt, 2 * cb), F32), pltpu.VMEM((SCAN_SEGS, 2 * cb), F32)],
        compiler_params=_params(("parallel", "arbitrary")),
    )(a_conj, ends, dy, u, s, s, s_entry, b_blk, c_blk, dd)


def _glu_fwd(ys, u, dd, w_glu, b_glu, *, tm, name):
    rows, w = ys.shape

    def body(ys_ref, u_ref, dd_ref, w_ref, b_ref, y0_ref, t_ref, y2_ref):
        y0 = ys_ref[...] + dd_ref[...] * u_ref[...]
        y1 = _gelu(y0)
        t = _dot(y1.astype(BF16), w_ref[...], 1, 0) + b_ref[...]
        y0_ref[...] = y0
        t_ref[...] = t
        y2_ref[...] = (y1 * _sigmoid(t)).astype(BF16)

    row = pl.BlockSpec((tm, w), lambda i: (i, 0))
    vec = pl.BlockSpec((1, w), lambda i: (0, 0))
    return pl.pallas_call(
        body, name=name, grid=(rows // tm,),
        in_specs=[row, row, vec, pl.BlockSpec((w, w), lambda i: (0, 0)), vec],
        out_specs=[row, row, row],
        out_shape=[jax.ShapeDtypeStruct((rows, w), F32), jax.ShapeDtypeStruct((rows, w), F32),
                   jax.ShapeDtypeStruct((rows, w), BF16)],
        compiler_params=_params(("parallel",)),
    )(ys, u, dd, w_glu, b_glu)


def _glu_bwd(dy2, y0, t, u, w_glu, *, tm, name):
    rows, w = y0.shape

    def body(dy2_ref, y0_ref, t_ref, u_ref, w_ref, dy0_ref, dt_ref, y1_ref, db_ref, dd_ref):
        i = pl.program_id(0)
        y0 = y0_ref[...]
        y1 = _gelu(y0)
        sg = _sigmoid(t_ref[...])
        dy2v = dy2_ref[...]
        dt = dy2v * y1 * sg * (1.0 - sg)
        dy1 = dy2v * sg + _dot(dt.astype(BF16), w_ref[...], 1, 1)
        dy0 = dy1 * _gelu_grad(y0)
        dy0_ref[...] = dy0
        dt_ref[...] = dt.astype(BF16)
        y1_ref[...] = y1.astype(BF16)

        @pl.when(i == 0)
        def _():
            db_ref[...] = jnp.zeros_like(db_ref)
            dd_ref[...] = jnp.zeros_like(dd_ref)

        db_ref[...] += jnp.sum(dt, axis=0, keepdims=True)
        dd_ref[...] += jnp.sum(dy0 * u_ref[...], axis=0, keepdims=True)

    row = pl.BlockSpec((tm, w), lambda i: (i, 0))
    vec = pl.BlockSpec((1, w), lambda i: (0, 0))
    return pl.pallas_call(
        body, name=name, grid=(rows // tm,),
        in_specs=[row, row, row, row, pl.BlockSpec((w, w), lambda i: (0, 0))],
        out_specs=[row, row, row, vec, vec],
        out_shape=[jax.ShapeDtypeStruct((rows, w), F32), jax.ShapeDtypeStruct((rows, w), BF16),
                   jax.ShapeDtypeStruct((rows, w), BF16), jax.ShapeDtypeStruct((1, w), F32),
                   jax.ShapeDtypeStruct((1, w), F32)],
        compiler_params=_params(("arbitrary",)),
    )(dy2, y0, t, u, w_glu)


ATTN_TILE = 2048


def _attn_geometry(rows, d):
    sb = ATTN_Q * d
    tr = max(sb, min(ATTN_TILE, rows))
    assert rows % tr == 0 and tr % sb == 0, (rows, d)
    return sb, tr, rows // tr, tr // sb


def _attn_masks():
    qi = lax.broadcasted_iota(jnp.int32, (2 * ATTN_Q, 2 * ATTN_Q), 0) % ATTN_Q
    kj = lax.broadcasted_iota(jnp.int32, (2 * ATTN_Q, 2 * ATTN_Q), 1)
    own_ok = jnp.logical_and(kj >= ATTN_Q, kj - ATTN_Q <= qi)
    prev_ok = jnp.logical_and(kj < ATTN_Q, kj >= qi)
    bias_first = jnp.where(own_ok, 0.0, NEG_INF)
    bias_other = jnp.where(jnp.logical_or(own_ok, prev_ok), 0.0, NEG_INF)
    head0 = lax.broadcasted_iota(jnp.int32, (ATTN_Q, LANES), 1) < ATTN_HEAD_DIM
    return bias_first, bias_other, head0


def _attn_rows(base, n, d):
    return pl.ds(pl.multiple_of(base, ATTN_Q), n) if d == 1 else pl.ds(base, n, stride=d)


def _stack_heads(v, head0):
    return jnp.concatenate([jnp.where(head0, v, 0.0), jnp.where(head0, 0.0, v)], axis=0)


def _unstack_heads(v, head0):
    return jnp.where(head0, v[:ATTN_Q], v[ATTN_Q:])


def _fill_keys(buf, prev_ref, cur_ref, sb):
    buf[pl.ds(0, sb), :] = prev_ref[...]
    buf[pl.ds(sb, cur_ref.shape[0]), :] = cur_ref[...]


def _attn_fwd(qkv, g, d, *, name):
    rows = qkv.shape[0]
    sb, tr, ntiles, nsub = _attn_geometry(rows, d)
    qc, kc, vc = 2 * g, 6 + 2 * g, 12 + 2 * g
    scale = ATTN_HEAD_DIM ** -0.5

    def body(q_ref, kc_ref, kp_ref, vc_ref, vp_ref, o_ref, lse_ref, kbuf, vbuf):
        n = pl.program_id(0)
        _fill_keys(kbuf, kp_ref, kc_ref, sb)
        _fill_keys(vbuf, vp_ref, vc_ref, sb)
        bias_first, bias_other, head0 = _attn_masks()

        def per_block(idx, carry):
            j, r = idx // d, idx % d
            base = j * sb + r
            bias = jnp.where(jnp.logical_and(n == 0, j == 0), bias_first, bias_other)
            qrows = _attn_rows(base, ATTN_Q, d)
            krows = _attn_rows(base, 2 * ATTN_Q, d)
            qs = _stack_heads(q_ref[qrows, :], head0).astype(BF16)
            s = _dot(qs, kbuf[krows, :].astype(BF16), 1, 1) * scale + bias
            mx = jnp.max(s, axis=-1, keepdims=True)
            p = jnp.exp(s - mx)
            den = jnp.sum(p, axis=-1, keepdims=True)
            pv = _dot(p.astype(BF16), vbuf[krows, :].astype(BF16), 1, 0) / den
            o_ref[qrows, :] = _unstack_heads(pv, head0)
            lse_ref[qrows, :] = _unstack_heads(jnp.broadcast_to(mx + jnp.log(den), (2 * ATTN_Q, LANES)), head0)
            return carry

        lax.fori_loop(0, nsub * d, per_block, 0, unroll=8)

    def cur(col):
        return pl.BlockSpec((tr, LANES), lambda n, hp: (n, col + hp))

    def prev(col):
        return pl.BlockSpec((sb, LANES), lambda n, hp: (jnp.maximum(n * nsub - 1, 0), col + hp))

    out_spec = pl.BlockSpec((tr, LANES), lambda n, hp: (n, hp))
    return pl.pallas_call(
        body, name=name, grid=(ntiles, 2),
        in_specs=[cur(qc), cur(kc), prev(kc), cur(vc), prev(vc)],
        out_specs=[out_spec, out_spec],
        out_shape=[jax.ShapeDtypeStruct((rows, 2 * LANES), F32), jax.ShapeDtypeStruct((rows, 2 * LANES), F32)],
        scratch_shapes=[pltpu.VMEM((sb + tr, LANES), F32), pltpu.VMEM((sb + tr, LANES), F32)],
        compiler_params=_params(("parallel", "parallel")),
    )(qkv, qkv, qkv, qkv, qkv)


def _attn_merge(outs, lses, *, tm, name):
    rows, w = outs[0].shape

    def body(o0, o1, o2, l0, l1, l2, o_ref, lse_ref):
        a0, a1, a2 = l0[...], l1[...], l2[...]
        mx = jnp.maximum(jnp.maximum(a0, a1), a2)
        e0, e1, e2 = jnp.exp(a0 - mx), jnp.exp(a1 - mx), jnp.exp(a2 - mx)
        den = e0 + e1 + e2
        o_ref[...] = (e0 / den) * o0[...] + (e1 / den) * o1[...] + (e2 / den) * o2[...]
        lse_ref[...] = mx + jnp.log(den)

    row = pl.BlockSpec((tm, w), lambda i: (i, 0))
    return pl.pallas_call(
        body, name=name, grid=(rows // tm,), in_specs=[row] * 6, out_specs=[row, row],
        out_shape=[jax.ShapeDtypeStruct((rows, w), F32), jax.ShapeDtypeStruct((rows, w), F32)],
        compiler_params=_params(("parallel",)),
    )(*outs, *lses)


def _attn_bwd(qkv, do, o, lse, g, d, prev, *, name):
    rows = qkv.shape[0]
    sb, tr, ntiles, nsub = _attn_geometry(rows, d)
    qc, kc, vc = 2 * g, 6 + 2 * g, 12 + 2 * g
    scale = ATTN_HEAD_DIM ** -0.5

    def body(q_ref, kc_ref, kp_ref, vc_ref, vp_ref, do_ref, o_ref, lse_ref, dq_ref, dk_ref, dv_ref,
             kbuf, vbuf, dk_acc, dv_acc):
        n = pl.program_id(1)

        @pl.when(n == 0)
        def _():
            dk_acc[pl.ds(0, tr), :] = jnp.zeros((tr, LANES), F32)
            dv_acc[pl.ds(0, tr), :] = jnp.zeros((tr, LANES), F32)

        @pl.when(n < ntiles)
        def _():
            dk_acc[pl.ds(tr, tr), :] = jnp.zeros((tr, LANES), F32)
            dv_acc[pl.ds(tr, tr), :] = jnp.zeros((tr, LANES), F32)
            _fill_keys(kbuf, kp_ref, kc_ref, sb)
            _fill_keys(vbuf, vp_ref, vc_ref, sb)
            bias_first, bias_other, head0 = _attn_masks()
            lane = lax.broadcasted_iota(jnp.int32, (ATTN_Q, LANES), 1)

            def per_block(idx, carry):
                j, r = idx // d, idx % d
                base = j * sb + r
                bias = jnp.where(jnp.logical_and(n == 0, j == 0), bias_first, bias_other)
                qrows = _attn_rows(base, ATTN_Q, d)
                krows = _attn_rows(base, 2 * ATTN_Q, d)
                arows = _attn_rows(base + (tr - sb), 2 * ATTN_Q, d)
                qs = _stack_heads(q_ref[qrows, :], head0).astype(BF16)
                dos = _stack_heads(do_ref[qrows, :], head0)
                dosb = dos.astype(BF16)
                ov = o_ref[qrows, :]
                delta = jnp.sum(dos * jnp.concatenate([ov, ov], axis=0), axis=-1, keepdims=True)
                lsev = lse_ref[qrows, :]
                lse_s = jnp.concatenate(
                    [jnp.sum(jnp.where(lane == h * ATTN_HEAD_DIM, lsev, 0.0), axis=-1, keepdims=True) for h in range(2)], axis=0)
                kb = kbuf[krows, :].astype(BF16)
                vb = vbuf[krows, :].astype(BF16)
                p = jnp.exp(_dot(qs, kb, 1, 1) * scale + bias - lse_s)
                ds = (p * (_dot(dosb, vb, 1, 1) - delta) * scale).astype(BF16)
                dq_ref[qrows, :] = _unstack_heads(_dot(ds, kb, 1, 0), head0)
                dk_acc[arows, :] += _dot(ds, qs, 0, 0)
                dv_acc[arows, :] += _dot(p.astype(BF16), dosb, 0, 0)
                return carry

            lax.fori_loop(0, nsub * d, per_block, 0, unroll=4)

        dk_ref[...] = dk_acc[pl.ds(0, tr), :]
        dv_ref[...] = dv_acc[pl.ds(0, tr), :]
        dk_acc[pl.ds(0, tr), :] = dk_acc[pl.ds(tr, tr), :]
        dv_acc[pl.ds(0, tr), :] = dv_acc[pl.ds(tr, tr), :]

    def cur(n):
        return jnp.minimum(n, ntiles - 1)

    def spec(col, prev):
        if prev:
            return pl.BlockSpec((sb, LANES), lambda hp, n: (jnp.maximum(cur(n) * nsub - 1, 0), col + hp))
        return pl.BlockSpec((tr, LANES), lambda hp, n: (cur(n), col + hp))

    row_spec = pl.BlockSpec((tr, LANES), lambda hp, n: (cur(n), hp))
    dq_out = pl.BlockSpec((tr, LANES), lambda hp, n: (cur(n), 2 * g + hp))
    kv_out = pl.BlockSpec((tr, LANES), lambda hp, n: (jnp.maximum(n - 1, 0), 2 * g + hp))
    shape = jax.ShapeDtypeStruct((rows, len(ATTN_PATTERNS) * 2 * LANES), F32)
    ins = [qkv, qkv, qkv, qkv, qkv, do, o, lse]
    in_specs = [spec(qc, False), spec(kc, False), spec(kc, True), spec(vc, False), spec(vc, True),
                row_spec, row_spec, row_spec]
    aliases = {}
    if prev is not None:
        aliases = {len(ins) + t: t for t in range(3)}
        ins = ins + list(prev)
        in_specs = in_specs + [ANY] * 3
    n_in = len(ins)

    def entry(*refs):
        body(*refs[:8], *refs[n_in:])

    return pl.pallas_call(
        entry, name=name, grid=(2, ntiles + 1),
        in_specs=in_specs,
        out_specs=[dq_out, kv_out, kv_out],
        out_shape=[shape, shape, shape],
        input_output_aliases=aliases,
        scratch_shapes=[pltpu.VMEM((sb + tr, LANES), F32), pltpu.VMEM((sb + tr, LANES), F32),
                        pltpu.VMEM((2 * tr, LANES), F32), pltpu.VMEM((2 * tr, LANES), F32)],
        compiler_params=_params(("parallel", "arbitrary")),
    )(*ins)


def _mem_probs(q, k):
    s = _dot(q.astype(BF16), k.astype(BF16), 1, 1) * (MEM_HEAD_DIM ** -0.5)
    e = jnp.exp(s - jnp.max(s, axis=-1, keepdims=True))
    return e / jnp.sum(e, axis=-1, keepdims=True)


def _mem_attn_fwd(mq, kv, *, tq, name):
    rows = mq.shape[0]

    def body(q_ref, k_ref, v_ref, o_ref):
        p = _mem_probs(q_ref[...], k_ref[...])
        o_ref[...] = _dot(p.astype(BF16), v_ref[...].astype(BF16), 1, 0)

    return pl.pallas_call(
        body, name=name, grid=(rows // tq, MEM_HEADS),
        in_specs=[pl.BlockSpec((tq, LANES), lambda i, h: (i, h)),
                  pl.BlockSpec((MEM_LEN, LANES), lambda i, h: (0, h)),
                  pl.BlockSpec((MEM_LEN, LANES), lambda i, h: (0, MEM_HEADS + h))],
        out_specs=pl.BlockSpec((tq, LANES), lambda i, h: (i, h)),
        out_shape=jax.ShapeDtypeStruct((rows, MEM_HEADS * LANES), F32),
        compiler_params=_params(("parallel", "parallel")),
    )(mq, kv, kv)


def _mem_attn_bwd(mq, kv, dmo, *, tq, name):
    rows = mq.shape[0]
    scale = MEM_HEAD_DIM ** -0.5

    def body(q_ref, k_ref, v_ref, do_ref, dq_ref, dk_ref, dv_ref):
        i = pl.program_id(1)
        qb = q_ref[...].astype(BF16)
        kb = k_ref[...].astype(BF16)
        vb = v_ref[...].astype(BF16)
        dob = do_ref[...].astype(BF16)
        p = _mem_probs(q_ref[...], k_ref[...])
        dp = _dot(dob, vb, 1, 1)
        ds = (p * (dp - jnp.sum(p * dp, axis=-1, keepdims=True)) * scale).astype(BF16)
        dq_ref[...] = _dot(ds, kb, 1, 0).astype(dq_ref.dtype)

        @pl.when(i == 0)
        def _():
            dk_ref[...] = jnp.zeros_like(dk_ref)
            dv_ref[...] = jnp.zeros_like(dv_ref)

        dk_ref[...] += _dot(ds, qb, 0, 0)
        dv_ref[...] += _dot(p.astype(BF16), dob, 0, 0)

    kv_out = pl.BlockSpec((MEM_LEN, LANES), lambda h, i: (0, h))
    kv_shape = jax.ShapeDtypeStruct((MEM_LEN, MEM_HEADS * LANES), F32)
    return pl.pallas_call(
        body, name=name, grid=(MEM_HEADS, rows // tq),
        in_specs=[pl.BlockSpec((tq, LANES), lambda h, i: (i, h)),
                  pl.BlockSpec((MEM_LEN, LANES), lambda h, i: (0, h)),
                  pl.BlockSpec((MEM_LEN, LANES), lambda h, i: (0, MEM_HEADS + h)),
                  pl.BlockSpec((tq, LANES), lambda h, i: (i, h))],
        out_specs=[pl.BlockSpec((tq, LANES), lambda h, i: (i, h)), kv_out, kv_out],
        out_shape=[jax.ShapeDtypeStruct((rows, MEM_HEADS * LANES), BF16), kv_shape, kv_shape],
        compiler_params=_params(("parallel", "arbitrary")),
    )(mq, kv, kv, dmo)


def _resident(shape):
    return pl.BlockSpec(shape, lambda i: (0, 0), pipeline_mode=pl.Buffered(1))


def _branch_merge_fwd(acts, wts, zg, b_gate, *, tm, name):
    rows = zg.shape[0]
    d = wts[0].shape[0]

    def body(s_ref, a_ref, m_ref, ws_ref, wa_ref, wm_ref, zg_ref, b_ref, o_ref):
        gt = _sigmoid(zg_ref[...] + b_ref[...])
        acc = None
        for k, (x_ref, w_ref) in enumerate(((s_ref, ws_ref), (a_ref, wa_ref), (m_ref, wm_ref))):
            term = gt[:, k * d:(k + 1) * d] * _dot(x_ref[...].astype(BF16), w_ref[...], 1, 1)
            acc = term if acc is None else acc + term
        o_ref[...] = acc.astype(BF16)

    return pl.pallas_call(
        body, name=name, grid=(rows // tm,),
        in_specs=[pl.BlockSpec((tm, x.shape[1]), lambda i: (i, 0)) for x in acts] + [_resident(w.shape) for w in wts]
        + [pl.BlockSpec((tm, 3 * d), lambda i: (i, 0)), pl.BlockSpec((1, 3 * d), lambda i: (0, 0))],
        out_specs=pl.BlockSpec((tm, d), lambda i: (i, 0)), out_shape=jax.ShapeDtypeStruct((rows, d), BF16),
        compiler_params=_params(("parallel",)),
    )(*acts, *wts, zg, b_gate)


def _branch_merge_bwd(dmerged, acts, wts, zg, b_gate, *, tm, name):
    rows = zg.shape[0]
    d = wts[0].shape[0]

    def body(dm_ref, s_ref, a_ref, m_ref, ws_ref, wa_ref, wm_ref, zg_ref, b_ref,
             ds_ref, da_ref, dmm_ref, dws_ref, dwa_ref, dwm_ref, dzg_ref, db_ref):
        i = pl.program_id(0)

        @pl.when(i == 0)
        def _():
            for r in (dws_ref, dwa_ref, dwm_ref, db_ref):
                r[...] = jnp.zeros_like(r)

        gt = _sigmoid(zg_ref[...] + b_ref[...])
        dm = dm_ref[...]
        groups = ((s_ref, ws_ref, ds_ref, dws_ref), (a_ref, wa_ref, da_ref, dwa_ref), (m_ref, wm_ref, dmm_ref, dwm_ref))
        for k, (x_ref, w_ref, dx_ref, dw_ref) in enumerate(groups):
            cs = pl.ds(k * d, d)
            gk = gt[:, k * d:(k + 1) * d]
            xb = x_ref[...].astype(BF16)
            br = _dot(xb, w_ref[...], 1, 1)
            dbr = (dm * gk).astype(BF16)
            dx_ref[...] = _dot(dbr, w_ref[...], 1, 0)
            dw_ref[...] += _dot(dbr, xb, 0, 0)
            dzg = dm * br * gk * (1.0 - gk)
            dzg_ref[:, cs] = dzg.astype(BF16)
            db_ref[:, cs] += jnp.sum(dzg, axis=0, keepdims=True)

    row = lambda w: pl.BlockSpec((tm, w), lambda i: (i, 0))
    whole = lambda shape: pl.BlockSpec(shape, lambda i: (0, 0))
    return pl.pallas_call(
        body, name=name, grid=(rows // tm,),
        in_specs=[row(d)] + [row(x.shape[1]) for x in acts] + [_resident(w.shape) for w in wts] + [row(3 * d), whole((1, 3 * d))],
        out_specs=[row(x.shape[1]) for x in acts] + [whole(w.shape) for w in wts] + [row(3 * d), whole((1, 3 * d))],
        out_shape=[jax.ShapeDtypeStruct(x.shape, F32) for x in acts] + [jax.ShapeDtypeStruct(w.shape, F32) for w in wts]
        + [jax.ShapeDtypeStruct((rows, 3 * d), BF16), jax.ShapeDtypeStruct((1, 3 * d), F32)],
        compiler_params=_params(("arbitrary",)),
    )(dmerged, *acts, *wts, zg, b_gate)


def _adamw(w, g, m, v, *, tr, name):
    rows, cols = w.shape[-2:]
    assert rows % tr == 0, (name, rows, tr)

    def body(w_ref, g_ref, m_ref, v_ref, g_out, d_ref, nm_ref, nv_ref):
        gv = g_ref[...]
        m2 = ADAM_B1 * m_ref[...] + (1.0 - ADAM_B1) * gv
        v2 = ADAM_B2 * v_ref[...] + (1.0 - ADAM_B2) * (gv * gv)
        m_hat = m2 / (1.0 - ADAM_B1 ** ADAM_STEP)
        v_hat = v2 / (1.0 - ADAM_B2 ** ADAM_STEP)
        g_out[...] = gv
        d_ref[...] = -ADAM_LR * (m_hat / (jnp.sqrt(v_hat) + ADAM_EPS) + ADAM_WD * w_ref[...])
        nm_ref[...] = m2
        nv_ref[...] = v2

    flat = pl.BlockSpec((tr, cols), lambda i: (i, 0))
    blk = flat if w.ndim == 2 else pl.BlockSpec((None, tr, cols), lambda i: (0, i, 0))
    shape = jax.ShapeDtypeStruct(w.shape, F32)
    return pl.pallas_call(
        body, name=name, grid=(rows // tr,), in_specs=[blk, flat, blk, blk], out_specs=[blk] * 4,
        out_shape=[shape] * 4, compiler_params=_params(("parallel",)),
    )(w, g, m, v)


ANY = pl.BlockSpec(memory_space=pl.ANY)


def _position():
    return lax.axis_index("x"), lax.axis_index("y"), lax.axis_index("c")


def _other_chips(x, y):
    return ((1 - x, y), (x, 1 - y), (1 - x, 1 - y))


def _remote(src, dst, send_sem, recv_sem, dev):
    return pltpu.make_async_remote_copy(src_ref=src, dst_ref=dst, send_sem=send_sem, recv_sem=recv_sem,
                                        device_id=dev, device_id_type=MESH)


def _gather_exchange(shards):
    nb = len(shards)

    def rows_of(i, owner, core):
        rs = shards[i].shape[0]
        return pl.ds(pl.multiple_of(owner * rs + core * (rs // 2), 16), rs // 2)

    def first_leg(ins, outs, send_sems, recv_sems, i, j):
        x, y, c = _position()
        px, py = _other_chips(x, y)[j]
        half = shards[i].shape[0] // 2
        mine = ins[i].at[pl.ds(pl.multiple_of(c * half, 16), half)]
        return _remote(mine, outs[i].at[rows_of(i, 2 * x + y, c)], send_sems.at[i, j], recv_sems.at[i, j], (px, py, c))

    def passed_on(outs, send_sems, recv_sems, i, j, core):
        x, y, c = _position()
        px, py = _other_chips(x, y)[j]
        rows = outs[i].at[rows_of(i, 2 * px + py, core)]
        return _remote(rows, rows, send_sems.at[i, 3 + j], recv_sems.at[i, 3 + j], (x, y, 1 - c))

    def own_block(ins, outs, send_sems, recv_sems, i):
        x, y, c = _position()
        rs = shards[i].shape[0]
        place = outs[i].at[pl.ds(pl.multiple_of((2 * x + y) * rs, 16), rs)]
        return _remote(ins[i], place, send_sems.at[i, 6], recv_sems.at[i, 6], (x, y, 1 - c))

    def start(ins, outs, send_sems, recv_sems):
        for i in range(nb):
            own_block(ins, outs, send_sems, recv_sems, i).start()
            for j in range(3):
                first_leg(ins, outs, send_sems, recv_sems, i, j).start()

    def finish(ins, outs, send_sems, recv_sems):
        x, y, c = _position()
        for i in range(nb):
            for j, (px, py) in enumerate(_other_chips(x, y)):
                landed = outs[i].at[rows_of(i, 2 * px + py, c)]
                _remote(landed, landed, send_sems.at[i, j], recv_sems.at[i, j], (px, py, c)).wait_recv()
                passed_on(outs, send_sems, recv_sems, i, j, c).start()
        for i in range(nb):
            own_block(ins, outs, send_sems, recv_sems, i).wait()
            for j in range(3):
                passed_on(outs, send_sems, recv_sems, i, j, 1 - c).wait_recv()
        for i in range(nb):
            for j in range(3):
                first_leg(ins, outs, send_sems, recv_sems, i, j).wait_send()
                passed_on(outs, send_sems, recv_sems, i, j, c).wait_send()

    return _Exchange(ins=list(shards), outs=[jax.ShapeDtypeStruct((N_CHIPS * s.shape[0], s.shape[1]), s.dtype) for s in shards],
                     aliases={}, sems=[(nb, 7), (nb, 7)], start=start, finish=finish)


def _run_exchange(ex, *, name):
    n_in, n_out = len(ex.ins), len(ex.outs)

    def body(*refs):
        c_in, c_out, sems = refs[:n_in], refs[n_in:n_in + n_out], refs[n_in + n_out:]
        ex.start(c_in, c_out, *sems)
        ex.finish(c_in, c_out, *sems)

    return pl.pallas_call(
        body, name=name, in_specs=[ANY] * n_in, out_specs=[ANY] * n_out, out_shape=list(ex.outs),
        input_output_aliases=dict(ex.aliases),
        scratch_shapes=[pltpu.SemaphoreType.DMA(s) for s in ex.sems],
    )(*ex.ins)


def _row_tile(rows):
    return max(t for t in range(16, min(rows, 512) + 1, 16) if rows % t == 0)


def _exchange_halves(grads, *, name):
    nb = len(grads)

    def body(*refs):
        ins, outs = refs[:nb], refs[nb:2 * nb]
        send_sems, recv_sems = refs[2 * nb:]
        x, y, c = _position()
        copies = []
        for i in range(nb):
            cp = _remote(ins[i].at[:, 1 - c], outs[i], send_sems.at[i], recv_sems.at[i], (x, y, 1 - c))
            cp.start()
            copies.append(cp)
        for cp in copies:
            cp.wait()

    return pl.pallas_call(
        body, name=name, in_specs=[ANY] * nb, out_specs=[ANY] * nb,
        out_shape=[jax.ShapeDtypeStruct((N_CHIPS, g.shape[2], g.shape[3]), F32) for g in grads],
        scratch_shapes=[pltpu.SemaphoreType.DMA((nb,)), pltpu.SemaphoreType.DMA((nb,))],
    )(*grads)


def _pair_sum(g4, got, c_arr, *, name):
    _, _, half, cols = g4.shape
    tr = _row_tile(half)

    def body(c_ref, g_ref, t_ref, p_ref, pb_ref):
        sm = g_ref[...] + t_ref[...]
        p_ref[...] = sm
        pb_ref[...] = sm.astype(BF16)

    blk = pl.BlockSpec((None, tr, cols), lambda j, i, c_ref: (j, i, 0))
    grid_spec = pltpu.PrefetchScalarGridSpec(
        num_scalar_prefetch=1, grid=(N_CHIPS, half // tr),
        in_specs=[pl.BlockSpec((None, None, tr, cols), lambda j, i, c_ref: (j, c_ref[0], i, 0)), blk],
        out_specs=[blk, blk])
    return pl.pallas_call(
        body, name=name, grid_spec=grid_spec,
        out_shape=[jax.ShapeDtypeStruct((N_CHIPS, half, cols), F32), jax.ShapeDtypeStruct((N_CHIPS, half, cols), BF16)],
        compiler_params=_params(("parallel", "parallel")),
    )(c_arr, g4, got)


def _scatter_exchange(parts):
    nb = len(parts)

    def copies(ins, outs, send_sems, recv_sems):
        x, y, c = _position()
        return [_remote(ins[i].at[2 * px + py], outs[i].at[j], send_sems.at[i, j], recv_sems.at[i, j], (px, py, c))
                for i in range(nb) for j, (px, py) in enumerate(_other_chips(x, y))]

    def start(ins, outs, send_sems, recv_sems):
        for cp in copies(ins, outs, send_sems, recv_sems):
            cp.start()

    def finish(ins, outs, send_sems, recv_sems):
        for cp in copies(ins, outs, send_sems, recv_sems):
            cp.wait()

    return _Exchange(ins=list(parts), outs=[jax.ShapeDtypeStruct((3,) + p.shape[1:], p.dtype) for p in parts],
                     aliases={}, sems=[(nb, 3), (nb, 3)], start=start, finish=finish)


def _owner_sum(p, got, chip_arr, c_arr, *, replicated, name):
    _, half, cols = p.shape
    tr = _row_tile(half)

    def body(chip_ref, c_ref, p_ref, r_ref, o_ref):
        o_ref[...] = ((p_ref[...] + r_ref[0].astype(F32)) + r_ref[1].astype(F32)) + r_ref[2].astype(F32)

    if replicated:
        out_spec = pl.BlockSpec((None, None, tr, cols), lambda i, chip_ref, c_ref: (chip_ref[0], c_ref[0], i, 0))
        out_shape = jax.ShapeDtypeStruct((N_CHIPS, 2, half, cols), F32)
    else:
        out_spec = pl.BlockSpec((None, tr, cols), lambda i, chip_ref, c_ref: (c_ref[0], i, 0))
        out_shape = jax.ShapeDtypeStruct((2, half, cols), F32)
    grid_spec = pltpu.PrefetchScalarGridSpec(
        num_scalar_prefetch=2, grid=(half // tr,),
        in_specs=[pl.BlockSpec((None, tr, cols), lambda i, chip_ref, c_ref: (chip_ref[0], i, 0)),
                  pl.BlockSpec((3, tr, cols), lambda i, chip_ref, c_ref: (0, i, 0))],
        out_specs=out_spec)
    return pl.pallas_call(
        body, name=name, grid_spec=grid_spec, out_shape=out_shape,
        compiler_params=_params(("parallel",)),
    )(chip_arr, c_arr, p, got)


def _share_reduced(bufs):
    nb = len(bufs) - 1

    def body(*refs):
        outs = refs[nb + 1:2 * nb + 2]
        send_sems, recv_sems = refs[2 * nb + 2:]
        x, y, c = _position()
        chip = 2 * x + y
        sends = []
        for i in range(nb):
            cp = _remote(outs[i].at[c], outs[i].at[c], send_sems.at[i], recv_sems.at[i], (x, y, 1 - c))
            cp.start()
            sends.append(cp)
        small = outs[nb]
        peers = [(fx, fy, fc) for fx in (0, 1) for fy in (0, 1) for fc in (0, 1) if fx + fy + fc > 0]
        for k, (fx, fy, fc) in enumerate(peers):
            dev = (x ^ fx, y ^ fy, c ^ fc)
            cp = _remote(small.at[chip, c], small.at[chip, c], send_sems.at[nb + k], recv_sems.at[nb + k], dev)
            cp.start()
            sends.append(cp)
        for i in range(nb):
            dst = outs[i].at[1 - c]
            _remote(dst, dst, send_sems.at[i], recv_sems.at[i], (x, y, 1 - c)).wait_recv()
        for k, (fx, fy, fc) in enumerate(peers):
            dst = small.at[2 * (x ^ fx) + (y ^ fy), c ^ fc]
            _remote(dst, dst, send_sems.at[nb + k], recv_sems.at[nb + k], (x ^ fx, y ^ fy, c ^ fc)).wait_recv()
        for cp in sends:
            cp.wait_send()

    n_all = nb + 1
    return pl.pallas_call(
        body, name="grad_share_reduced", in_specs=[ANY] * n_all, out_specs=[ANY] * n_all,
        out_shape=[jax.ShapeDtypeStruct(b.shape, b.dtype) for b in bufs],
        input_output_aliases={i: i for i in range(n_all)},
        scratch_shapes=[pltpu.SemaphoreType.DMA((nb + 7,)), pltpu.SemaphoreType.DMA((nb + 7,))],
    )(*bufs)


class _GradReducer:
    def __init__(self, c_arr, chip_arr):
        self.c_arr, self.chip_arr = c_arr, chip_arr
        self.pairs, self.landed = {}, {}

    def _pair_sums(self, names, grads):
        full = [g.reshape(N_CHIPS, 2, g.shape[0] // (2 * N_CHIPS), g.shape[1]) for g in grads]
        got = _exchange_halves(full, name="grad_exchange_" + names[0])
        for n, g, t in zip(names, full, got):
            self.pairs[n] = _pair_sum(g, t, self.c_arr, name="grad_pair_sum_" + n)

    def scatter(self, names, grads):
        self._pair_sums(names, grads)
        return _scatter_exchange([self.pairs[n][1] for n in names])

    def collect(self, names, bufs):
        self.landed.update(zip(names, bufs))

    def finish(self, names, grads, order):
        self.collect(names, _run_exchange(self.scatter(names, grads), name="grad_scatter_" + names[0]))
        totals = [_owner_sum(self.pairs[n][0], self.landed[n], self.chip_arr, self.c_arr, replicated=(n == order[-1]),
                             name="grad_owner_sum_" + n) for n in order]
        return _share_reduced(totals)


def _pack_small(vals):
    flat = jnp.concatenate([vals[name].reshape(-1) for name, _ in SMALL])
    return jnp.pad(flat, (0, N_CHIPS * SMALL_ROWS * 1024 - SMALL_ELEMS)).reshape(N_CHIPS * SMALL_ROWS, 1024)


def _unpack_small(buf):
    flat = buf.reshape(-1)
    out, off = {}, 0
    for name, shape in SMALL:
        n = int(np.prod(shape))
        out[name] = flat[off:off + n].reshape(shape)
        off += n
    return out


EARLY_REDUCED = (("w_down",), ("w_up",), ("w_o", "w_ssm_br", "w_attn_br", "w_mem_br", "w_glu", "w_mem_kv"), ("w_in",))


def _device_step(x, mem, tgt, w, p, *, shards, reducer):
    rows = x.shape[0]
    w = dict(w)
    early = EARLY_REDUCED
    gb = {}
    gather_pending = shards is not None

    def reducing(names):
        return reducer.scatter(names, [gb[n] for n in names]) if (reducer is not None and names) else None

    def reduced(names, res):
        if reducer is None or not names:
            return res
        reducer.collect(names, res[1])
        return res[0]

    def fetching(names):
        return _gather_exchange([shards[n] for n in names]) if gather_pending else None

    def fetched(names, res):
        if not gather_pending:
            return res
        w.update(zip(names, res[1]))
        return res[0]

    first_use = (("w_glu", "w_ssm_br", "w_attn_br", "w_mem_kv", "w_mem_br", "w_o"), ("w_up",), ("w_down",))
    g1, gm, g2 = p["norm1_g"], p["mem_norm_g"], p["norm2_g"]
    gf = p["final_g"].reshape(1, D_MODEL)
    ssm_args = (p["ssm_lambda_re"][0], p["ssm_lambda_im"][0], p["ssm_log_dt"][0], p["ssm_b_re"][0],
                p["ssm_b_im"][0], p["ssm_c_re"][0], p["ssm_c_im"][0])
    (a_lay, b_blk, c_blk), ssm_vjp = jax.vjp(_ssm_matrices, *ssm_args)
    a_conj = a_lay * _to_scan_layout(jnp.stack([jnp.ones((N_STATES,), F32), -jnp.ones((N_STATES,), F32)]))[None, :]
    dd = p["ssm_d"].reshape(1, SSM_WIDTH)
    win_t = w["w_in"]
    mm = _matmul

    n1 = _rmsnorm_fwd(x, g1, tm=512, name="norm1")
    u = mm(n1, win_t, m=rows, n=512, k=1024, tb=True, tm=2048, tn=512, tk=1024, out_dtypes=(F32,), name="in_u")
    qkv = fetched(first_use[0], mm(n1, win_t, m=rows, n=2304, k=1024, tb=True, tm=2048, tn=256, tk=1024,
                                   b_off=(OFF_QKV // 256, 0), out_dtypes=(F32,), carry=fetching(first_use[0]), name="in_qkv"))
    mq = mm(n1, win_t, m=rows, n=512, k=1024, tb=True, tm=2048, tn=256, tk=1024, b_off=(OFF_MQ // 256, 0),
            out_dtypes=(F32,), name="in_mq")
    zg = fetched(first_use[1], mm(n1, win_t, m=rows, n=3072, k=1024, tb=True, tm=2048, tn=256, tk=1024,
                                  b_off=(OFF_ZG // 256, 0), out_dtypes=(F32,), carry=fetching(first_use[1]), name="in_zg"))

    u_i = _interleave(u)
    ends = _ssm_ends(a_lay, u_i, b_blk, transpose=False, reverse=False, tt=512, name="ssm_fwd_ends")
    s, ys_i, s_entry = _ssm_fwd(a_lay, u_i, b_blk, c_blk, ends, tt=512, name="ssm_fwd")
    ys = _deinterleave(ys_i)
    y0, tglu, y2 = _glu_fwd(ys, u, dd, w["w_glu"], p["b_glu"], tm=512, name="glu_fwd")

    outs, lses = [], []
    for g, (_, d) in enumerate(ATTN_PATTERNS):
        o_g, lse_g = _attn_fwd(qkv, g, d, name=f"attn_fwd_{g}")
        outs.append(o_g)
        lses.append(lse_g)
    o, lse = _attn_merge(outs, lses, tm=1024, name="attn_merge")

    mn = _rmsnorm_fwd(mem, gm, tm=MEM_LEN, name="mem_norm")
    kv = mm(mn, w["w_mem_kv"], m=MEM_LEN, n=1024, k=1024, tm=MEM_LEN, tn=1024, tk=1024, out_dtypes=(F32,), name="mem_kv")
    mo = _mem_attn_fwd(mq, kv, tq=1024, name="mem_attn_fwd")

    branch_acts = (y2, o, mo)
    branch_wts = (w["w_ssm_br"], w["w_attn_br"], w["w_mem_br"])
    merged = _branch_merge_fwd(branch_acts, branch_wts, zg, p["b_gate"], tm=256, name="branch_merge_fwd")
    add = lambda acc, r: (acc + r,)
    h1, n2 = mm(merged, w["w_o"], m=rows, n=1024, k=1024, tm=1024, tn=1024, tk=1024, out_dtypes=(F32, BF16),
                aux=((x, "mn"), (g2, "row")), epilogue=_residual_norm_epilogue, name="out_proj")
    relu2 = lambda acc: (jnp.square(jnp.maximum(acc, 0.0)),)
    act = fetched(first_use[2], mm(n2, w["w_up"], m=rows, n=D_FF, k=1024, tb=True, tm=1024, tn=1024, tk=1024,
                                   out_dtypes=(BF16,), epilogue=relu2, carry=fetching(first_use[2]), name="mlp_up"))
    h2 = mm(act, w["w_down"], m=rows, n=1024, k=D_FF, tm=1024, tn=1024, tk=1024, out_dtypes=(F32,),
            aux=((h1, "mn"),), epilogue=add, name="mlp_down")
    dh2, loss, d_gf = _loss_head(h2, tgt, gf, tm=512, name="loss_head")

    gs = {"final_g": d_gf.reshape(D_MODEL)}
    drelu2 = lambda acc, actv: (acc * (2.0 * jnp.sqrt(actv.astype(F32))),)
    dup = mm(dh2, w["w_down"], m=rows, n=D_FF, k=1024, tb=True, tm=1024, tn=2048, tk=1024, out_dtypes=(BF16,),
             aux=((act, "mn"),), epilogue=drelu2, name="d_act")
    gb["w_down"] = mm(act, dh2, m=D_FF, n=1024, k=rows, ta=True, tm=1024, tn=1024, tk=1024, out_dtypes=(F32,), name="dw_down")
    gb["w_up"] = reduced(early[0], mm(dup, n2, m=D_FF, n=1024, k=rows, ta=True, tm=1024, tn=1024, tk=1024,
                                      out_dtypes=(F32,), carry=reducing(early[0]), name="dw_up"))
    dh1, gs["norm2_g"] = reduced(early[1], mm(dup, w["w_up"], m=rows, n=1024, k=D_FF, tm=1024, tn=1024, tk=1024,
                                              out_dtypes=(F32,), aux=((h1, "mn"), (dh2, "mn"), (g2, "row")),
                                              epilogue=_rmsnorm_bwd_epilogue, n_sums=1, carry=reducing(early[1]),
                                              name="d_n2"))
    dmerged = mm(dh1, w["w_o"], m=rows, n=1024, k=1024, tb=True, tm=1024, tn=1024, tk=1024, out_dtypes=(F32,), name="d_merged")
    gb["w_o"] = mm(merged, dh1, m=1024, n=1024, k=rows, ta=True, tm=1024, tn=1024, tk=1024, out_dtypes=(F32,), name="dw_o")
    (dy2, do, dmo, gb["w_ssm_br"], gb["w_attn_br"], gb["w_mem_br"], dzg, gs["b_gate"]) = _branch_merge_bwd(
        dmerged, branch_acts, branch_wts, zg, p["b_gate"], tm=256, name="branch_merge_bwd")

    dy0, dt, y1, gs["b_glu"], d_dd = _glu_bwd(dy2, y0, tglu, u, w["w_glu"], tm=512, name="glu_bwd")
    gs["ssm_d"] = d_dd.reshape(1, SSM_GROUPS, SSM_GROUP_SIZE)
    gb["w_glu"] = mm(y1, dt, m=512, n=512, k=rows, ta=True, tm=512, tn=512, tk=1024, out_dtypes=(F32,), name="dw_glu")
    dy0_i = _interleave(dy0)
    lam_ends = _ssm_ends(a_conj, dy0_i, c_blk, transpose=True, reverse=True, tt=512, name="ssm_bwd_ends")
    du_i, d_b_blk, d_c_blk, d_a_lay = _ssm_bwd(a_conj, dy0_i, u_i, s, s_entry, b_blk, c_blk, dd, lam_ends, tt=512,
                                                name="ssm_bwd")
    du = _deinterleave(du_i)
    d_ssm = ssm_vjp((d_a_lay, d_b_blk, d_c_blk))
    for name, val in zip(("ssm_lambda_re", "ssm_lambda_im", "ssm_log_dt", "ssm_b_re", "ssm_b_im", "ssm_c_re", "ssm_c_im"), d_ssm):
        gs[name] = val[None]

    dqkv = None
    for g, (_, d) in enumerate(ATTN_PATTERNS):
        dqkv = _attn_bwd(qkv, do, o, lse, g, d, dqkv, name=f"attn_bwd_{g}")

    dmq, dmk, dmv = _mem_attn_bwd(mq, kv, dmo, tq=1024, name="mem_attn_bwd")
    dkv = jnp.concatenate([dmk, dmv], axis=1)
    gb["w_mem_kv"] = mm(mn, dkv, m=1024, n=1024, k=MEM_LEN, ta=True, tm=1024, tn=1024, tk=MEM_LEN, out_dtypes=(F32,), name="dw_mem_kv")
    dmn = mm(dkv, w["w_mem_kv"], m=MEM_LEN, n=1024, k=1024, tb=True, tm=MEM_LEN, tn=1024, tk=1024, out_dtypes=(F32,), name="d_mn")
    _, gs["mem_norm_g"] = _rmsnorm_bwd(mem, gm, dmn, None, tm=MEM_LEN, name="mem_norm_bwd")

    pieces = ((du, OFF_U, "u"), (dqkv[0], OFF_QKV, "q"), (dqkv[1], OFF_QKV + 768, "k"), (dqkv[2], OFF_QKV + 1536, "v"),
              (dmq, OFF_MQ, "mq"), (dzg, OFF_ZG, "zg"))
    dw_rows = []
    for piece, off, tag in pieces:
        width = piece.shape[1]
        tmw = 1024 if width % 1024 == 0 else (768 if width == 768 else 512)
        rides = early[2] if tag == "zg" else ()
        dw_rows.append(reduced(rides, mm(piece, n1, m=width, n=1024, k=rows, ta=True, tm=tmw, tn=1024, tk=1024,
                                         out_dtypes=(F32,), carry=reducing(rides), name="dw_in_" + tag)))
    gb["w_in"] = jnp.concatenate(dw_rows, axis=0)
    dx, gs["norm1_g"] = reduced(early[3], _sum_matmul(
        [piece for piece, _, _ in pieces], win_t, [off for _, off, _ in pieces], tm=512,
        aux=((x, "mn"), (dh1, "mn"), (g1, "row")), epilogue=_rmsnorm_bwd_epilogue, n_sums=1,
        carry=reducing(early[3]), name="d_n1"))
    return loss, dx, gb, gs


def kernel(x, mem, norm1_g, mem_norm_g, w_in, b_gate, ssm_lambda_re, ssm_lambda_im, ssm_log_dt, ssm_b_re, ssm_b_im, ssm_c_re, ssm_c_im, ssm_d, w_glu, b_glu, w_ssm_br, w_attn_br, w_mem_kv, w_mem_br, w_o, norm2_g, w_up, w_down, final_g, loss_target, m_norm1_g, m_mem_norm_g, m_w_in, m_b_gate, m_ssm_lambda_re, m_ssm_lambda_im, m_ssm_log_dt, m_ssm_b_re, m_ssm_b_im, m_ssm_c_re, m_ssm_c_im, m_ssm_d, m_w_glu, m_b_glu, m_w_ssm_br, m_w_attn_br, m_w_mem_kv, m_w_mem_br, m_w_o, m_norm2_g, m_w_up, m_w_down, m_final_g, v_norm1_g, v_mem_norm_g, v_w_in, v_b_gate, v_ssm_lambda_re, v_ssm_lambda_im, v_ssm_log_dt, v_ssm_b_re, v_ssm_b_im, v_ssm_c_re, v_ssm_c_im, v_ssm_d, v_w_glu, v_b_glu, v_w_ssm_br, v_w_attn_br, v_w_mem_kv, v_w_mem_br, v_w_o, v_norm2_g, v_w_up, v_w_down, v_final_g):
    env = dict(locals())
    weights = {n: env[n] for n in WEIGHT_ORDER}
    moms = {n: env["m_" + n] for n in WEIGHT_ORDER}
    vels = {n: env["v_" + n] for n in WEIGHT_ORDER}
    def shard2d(a):
        return a.reshape(a.shape[-2], a.shape[-1])

    chip = 2 * lax.axis_index("x") + lax.axis_index("y")
    wire = [shard2d(weights[n]).astype(BF16) for n, _, _ in BIG]
    wire = dict(zip([n for n, _, _ in BIG], [s.T if tr else s for s, (_, tr, _) in zip(wire, BIG)]))
    w_in_full = _run_exchange(_gather_exchange([wire.pop("w_in")]), name="all_gather_w_in")[0]
    small = {n: weights[n] for n, _ in SMALL}

    reducer = _GradReducer(lax.axis_index("c").astype(jnp.int32).reshape(1), chip.astype(jnp.int32).reshape(1))
    loss, dx, gb, gs = _device_step(x[0], mem[0], loss_target[0], {"w_in": w_in_full}, small, shards=wire, reducer=reducer)
    *shards, small_grad = reducer.finish(["small"], [_pack_small(gs)], [n for n, _, _ in BIG] + ["small"])
    grads = {}
    for (n, tr, _), sh in zip(BIG, shards):
        sh = sh.reshape(2 * sh.shape[1], sh.shape[2])
        grads[n] = sh.T if tr else sh
    small_grad = small_grad.reshape(N_CHIPS * SMALL_ROWS, 1024)
    grads_small = _unpack_small(small_grad)

    delta, new_m, new_v = {}, {}, {}
    for n, _, _ in BIG:
        grads[n], delta[n], new_m[n], new_v[n] = _adamw(weights[n], grads[n], moms[n], vels[n],
                                                        tr=min(weights[n].shape[-2], 256), name="adamw_" + n)
    _, ds_, ms_, vs_ = _adamw(_pack_small(small), small_grad,
                              _pack_small({n: moms[n] for n, _ in SMALL}), _pack_small({n: vels[n] for n, _ in SMALL}),
                              tr=N_CHIPS * SMALL_ROWS, name="adamw_small")
    for dst, buf in ((delta, ds_), (new_m, ms_), (new_v, vs_)):
        dst.update(_unpack_small(buf))
    grads.update(grads_small)

    total_loss = lax.psum(loss[0, 0], ("x", "y", "c"))
    return (total_loss, dx[None], *[grads[n] for n in WEIGHT_ORDER], *[delta[n] for n in WEIGHT_ORDER],
            *[new_m[n] for n in WEIGHT_ORDER], *[new_v[n] for n in WEIGHT_ORDER])
```

```python
import functools
import math

import numpy as np
import jax
import jax.numpy as jnp
from jax import lax
from jax.experimental import pallas as pl
from jax.experimental.pallas import tpu as pltpu

F32 = jnp.float32
BF16 = jnp.bfloat16

D_MODEL = 1024
SSM_GROUPS = 32
SSM_GROUP_SIZE = 16
SSM_STATE = 64
SSM_WIDTH = 512
N_STATES = SSM_GROUPS * SSM_STATE
SCAN_CB = 1024
ATTN_PATTERNS = ((128, 1), (512, 4), (2048, 16))
ATTN_HEAD_DIM = 64
ATTN_Q = 128
MEM_LEN = 256
MEM_HEAD_DIM = 128
MEM_HEADS = 4
D_FF = 4096
OFF_U, OFF_QKV, OFF_MQ, OFF_ZG = 0, 512, 2816, 3328
IN_WIDTH = 6400
RMS_EPS = 1e-6
NEG_INF = -1e30
ADAM_LR, ADAM_B1, ADAM_B2, ADAM_EPS, ADAM_WD, ADAM_STEP = 0.001, 0.9, 0.999, 1e-08, 0.01, 10

VMEM_LIMIT_BYTES = 48 * 1024 * 1024
VMEM_LIMIT_WIDE_BYTES = 56 * 1024 * 1024
LANES = 128
MESH = pl.DeviceIdType.MESH
N_CHIPS = 4

SCAN_SEGS = 8
SCAN_GROUPS = SCAN_CB // SSM_STATE

BIG = (("w_in", True, (6400, 1024)), ("w_glu", False, (512, 512)), ("w_ssm_br", True, (1024, 512)),
       ("w_attn_br", True, (1024, 256)), ("w_mem_kv", False, (1024, 1024)), ("w_mem_br", True, (1024, 512)),
       ("w_o", False, (1024, 1024)), ("w_up", True, (4096, 1024)), ("w_down", False, (4096, 1024)))
SMALL = (("norm1_g", (1, 1024)), ("mem_norm_g", (1, 1024)), ("b_gate", (1, 3072)),
         ("ssm_lambda_re", (1, 32, 64)), ("ssm_lambda_im", (1, 32, 64)), ("ssm_log_dt", (1, 32)),
         ("ssm_b_re", (1, 32, 64, 16)), ("ssm_b_im", (1, 32, 64, 16)), ("ssm_c_re", (1, 32, 16, 64)),
         ("ssm_c_im", (1, 32, 16, 64)), ("ssm_d", (1, 32, 16)), ("b_glu", (1, 512)),
         ("norm2_g", (1, 1024)), ("final_g", (1024,)))
WEIGHT_ORDER = ("norm1_g", "mem_norm_g", "w_in", "b_gate", "ssm_lambda_re", "ssm_lambda_im", "ssm_log_dt",
                "ssm_b_re", "ssm_b_im", "ssm_c_re", "ssm_c_im", "ssm_d", "w_glu", "b_glu", "w_ssm_br",
                "w_attn_br", "w_mem_kv", "w_mem_br", "w_o", "norm2_g", "w_up", "w_down", "final_g")
SMALL_ELEMS = sum(int(np.prod(s)) for _, s in SMALL)
SMALL_ROWS = 64


def _params(sem, vmem=VMEM_LIMIT_BYTES):
    return pltpu.CompilerParams(dimension_semantics=sem, vmem_limit_bytes=vmem)


def _sigmoid(v):
    return 1.0 / (1.0 + jnp.exp(-v))


_GELU_C = math.sqrt(2.0 / math.pi)


def _gelu(v):
    return 0.5 * v * (1.0 + jnp.tanh(_GELU_C * (v + 0.044715 * v * v * v)))


def _gelu_grad(v):
    th = jnp.tanh(_GELU_C * (v + 0.044715 * v * v * v))
    return 0.5 * (1.0 + th) + 0.5 * v * (1.0 - th * th) * _GELU_C * (1.0 + 3.0 * 0.044715 * v * v)


def _dot(a, b, ca, cb):
    return lax.dot_general(a, b, (((ca,), (cb,)), ((), ())), preferred_element_type=F32)


class _Exchange:
    def __init__(self, ins, outs, aliases, sems, start, finish):
        self.ins, self.outs, self.aliases, self.sems, self.start, self.finish = ins, outs, aliases, sems, start, finish


def _matmul(a, b, *, m, n, k, ta=False, tb=False, tm, tn, tk, out_dtypes, name,
            a_off=(0, 0), b_off=(0, 0), aux=(), epilogue=None, n_sums=0, carry=None):
    assert m % tm == 0 and n % tn == 0 and k % tk == 0, (name, m, n, k, tm, tn, tk)
    nk = k // tk
    n_aux = len(aux)
    n_tiles = len(out_dtypes)
    n_out = n_tiles + n_sums
    ar, ac = a_off
    br, bc = b_off
    if ta:
        a_spec = pl.BlockSpec((tk, tm), lambda i, j, kk: (kk + ar, i + ac))
    else:
        a_spec = pl.BlockSpec((tm, tk), lambda i, j, kk: (i + ar, kk + ac))
    if tb:
        b_spec = pl.BlockSpec((tn, tk), lambda i, j, kk: (j + br, kk + bc))
    else:
        b_spec = pl.BlockSpec((tk, tn), lambda i, j, kk: (kk + br, j + bc))
    aux_specs = []
    for _, kind in aux:
        if kind == "mn":
            aux_specs.append(pl.BlockSpec((tm, tn), lambda i, j, kk: (i, j)))
        else:
            aux_specs.append(pl.BlockSpec((1, tn), lambda i, j, kk: (0, j)))
    ca = 0 if ta else 1
    cb = 1 if tb else 0

    def finish(acc, aux_refs, out_refs, row_tile):
        outs = (acc,) if epilogue is None else epilogue(acc, *[r[...] for r in aux_refs])
        for o_ref, o in zip(out_refs[:n_tiles], outs[:n_tiles]):
            o_ref[...] = o.astype(o_ref.dtype)
        _accumulate_over_rows(out_refs[n_tiles:], outs[n_tiles:], row_tile)

    def body(a_ref, b_ref, *rest):
        aux_refs = rest[:n_aux]
        out_refs = rest[n_aux:n_aux + n_out]
        row_tile = pl.program_id(0)
        prod = _dot(a_ref[...].astype(BF16), b_ref[...].astype(BF16), ca, cb)
        if nk == 1:
            finish(prod, aux_refs, out_refs, row_tile)
            return
        acc_ref = rest[n_aux + n_out]
        kk = pl.program_id(2)

        @pl.when(kk == 0)
        def _():
            acc_ref[...] = prod

        @pl.when(jnp.logical_and(kk > 0, kk < nk - 1))
        def _():
            acc_ref[...] += prod

        @pl.when(kk == nk - 1)
        def _():
            finish(acc_ref[...] + prod, aux_refs, out_refs, row_tile)

    tile = pl.BlockSpec((tm, tn), lambda i, j, kk: (i, j))
    col_sum = pl.BlockSpec((1, tn), lambda i, j, kk: (0, j))
    res = _call_with_carry(
        body, carry, name=name, grid=(m // tm, n // tn, nk), in_specs=[a_spec, b_spec] + aux_specs,
        out_specs=[tile] * n_tiles + [col_sum] * n_sums,
        out_shape=[jax.ShapeDtypeStruct((m, n), dt) for dt in out_dtypes] + [jax.ShapeDtypeStruct((1, n), F32)] * n_sums,
        scratch=[pltpu.VMEM((tm, tn), F32)] if nk > 1 else [], operands=[a, b] + [x for x, _ in aux],
        semantics=("arbitrary" if n_sums else "parallel", "parallel", "arbitrary"))
    main = res[0] if n_out == 1 else tuple(res[:n_out])
    return main if carry is None else (main, list(res[n_out:]))


def _accumulate_over_rows(sum_refs, terms, row_tile):
    for s_ref, term in zip(sum_refs, terms):
        @pl.when(row_tile == 0)
        def _():
            s_ref[...] = term

        @pl.when(row_tile > 0)
        def _():
            s_ref[...] += term


def _call_with_carry(body, carry, *, name, grid, in_specs, out_specs, out_shape, scratch, operands, semantics,
                     vmem=VMEM_LIMIT_BYTES):
    if carry is None:
        return pl.pallas_call(body, name=name, grid=grid, in_specs=in_specs, out_specs=out_specs, out_shape=out_shape,
                              scratch_shapes=scratch, compiler_params=_params(semantics, vmem))(*operands)
    n_in, n_cin, n_out, n_cout, n_scr = len(operands), len(carry.ins), len(out_shape), len(carry.outs), len(scratch)

    def hosted(*refs):
        main_in, c_in = refs[:n_in], refs[n_in:n_in + n_cin]
        main_out = refs[n_in + n_cin:n_in + n_cin + n_out]
        c_out = refs[n_in + n_cin + n_out:n_in + n_cin + n_out + n_cout]
        rest = refs[n_in + n_cin + n_out + n_cout:]
        ids = [pl.program_id(t) for t in range(len(grid))]
        first = functools.reduce(jnp.logical_and, [i == 0 for i in ids])
        last = functools.reduce(jnp.logical_and, [i == g - 1 for i, g in zip(ids, grid)])

        @pl.when(first)
        def _():
            carry.start(c_in, c_out, *rest[n_scr:])

        body(*main_in, *main_out, *rest[:n_scr])

        @pl.when(last)
        def _():
            carry.finish(c_in, c_out, *rest[n_scr:])

    return pl.pallas_call(
        hosted, name=name, grid=grid,
        in_specs=list(in_specs) + [ANY] * n_cin, out_specs=list(out_specs) + [ANY] * n_cout,
        out_shape=list(out_shape) + list(carry.outs),
        input_output_aliases={n_in + i: n_out + o for i, o in carry.aliases.items()},
        scratch_shapes=list(scratch) + [pltpu.SemaphoreType.DMA(s) for s in carry.sems],
        compiler_params=_params(("arbitrary",) * len(grid), vmem),
    )(*operands, *carry.ins)


def _sum_matmul(pieces, b, offs, *, tm, name, aux=(), epilogue=None, n_sums=0, carry=None):
    m = pieces[0].shape[0]
    n = b.shape[1]
    npieces, n_aux = len(pieces), len(aux)

    def body(*refs):
        b_ref = refs[npieces]
        aux_refs = refs[npieces + 1:npieces + 1 + n_aux]
        out_refs = refs[npieces + 1 + n_aux:]
        acc = None
        for p_ref, off in zip(refs[:npieces], offs):
            part = _dot(p_ref[...].astype(BF16), b_ref[pl.ds(off, p_ref.shape[1]), :], 1, 0)
            acc = part if acc is None else acc + part
        outs = (acc,) if epilogue is None else epilogue(acc, *[r[...] for r in aux_refs])
        out_refs[0][...] = outs[0]
        _accumulate_over_rows(out_refs[1:], outs[1:], pl.program_id(0))

    row = pl.BlockSpec((tm, n), lambda i: (i, 0))
    vec = pl.BlockSpec((1, n), lambda i: (0, 0))
    res = _call_with_carry(
        body, carry, name=name, grid=(m // tm,),
        in_specs=[pl.BlockSpec((tm, p.shape[1]), lambda i: (i, 0)) for p in pieces] + [_resident(b.shape)]
        + [row if kind == "mn" else vec for _, kind in aux],
        out_specs=[row] + [vec] * n_sums,
        out_shape=[jax.ShapeDtypeStruct((m, n), F32)] + [jax.ShapeDtypeStruct((1, n), F32)] * n_sums,
        scratch=[], operands=list(pieces) + [b] + [x for x, _ in aux], semantics=("arbitrary" if n_sums else "parallel",),
        vmem=VMEM_LIMIT_WIDE_BYTES)
    main = res[0] if n_sums == 0 else tuple(res[:1 + n_sums])
    return main if carry is None else (main, list(res[1 + n_sums:]))


def _rmsnorm_fwd(x, g, *, tm, name):
    rows, d = x.shape

    def body(x_ref, g_ref, o_ref):
        xv = x_ref[...]
        r = lax.rsqrt(jnp.mean(xv * xv, axis=-1, keepdims=True) + RMS_EPS)
        o_ref[...] = (xv * r * g_ref[...]).astype(o_ref.dtype)

    return pl.pallas_call(
        body, name=name, grid=(rows // tm,),
        in_specs=[pl.BlockSpec((tm, d), lambda i: (i, 0)), pl.BlockSpec((1, d), lambda i: (0, 0))],
        out_specs=pl.BlockSpec((tm, d), lambda i: (i, 0)),
        out_shape=jax.ShapeDtypeStruct((rows, d), BF16),
        compiler_params=_params(("parallel",)),
    )(x, g)


def _residual_norm_epilogue(acc, xv, gv):
    h = acc + xv
    r = lax.rsqrt(jnp.mean(h * h, axis=-1, keepdims=True) + RMS_EPS)
    return h, h * r * gv


def _rmsnorm_bwd_epilogue(dy, xv, resv, gv):
    r = lax.rsqrt(jnp.mean(xv * xv, axis=-1, keepdims=True) + RMS_EPS)
    xhat = xv * r
    dyg = dy * gv
    dx = r * (dyg - xhat * jnp.mean(dyg * xhat, axis=-1, keepdims=True)) + resv
    return dx, jnp.sum(dy * xhat, axis=0, keepdims=True)


def _rmsnorm_bwd(x, g, dy, res, *, tm, name):
    rows, d = x.shape
    has_res = res is not None

    def body(x_ref, g_ref, dy_ref, *rest):
        if has_res:
            res_ref, dx_ref, dg_ref = rest
        else:
            dx_ref, dg_ref = rest
        i = pl.program_id(0)
        xv = x_ref[...]
        r = lax.rsqrt(jnp.mean(xv * xv, axis=-1, keepdims=True) + RMS_EPS)
        xhat = xv * r
        dyv = dy_ref[...]
        dyg = dyv * g_ref[...]
        dx = r * (dyg - xhat * jnp.mean(dyg * xhat, axis=-1, keepdims=True))
        if has_res:
            dx = dx + res_ref[...]
        dx_ref[...] = dx

        @pl.when(i == 0)
        def _():
            dg_ref[...] = jnp.zeros_like(dg_ref)

        dg_ref[...] += jnp.sum(dyv * xhat, axis=0, keepdims=True)

    row_spec = pl.BlockSpec((tm, d), lambda i: (i, 0))
    vec_spec = pl.BlockSpec((1, d), lambda i: (0, 0))
    ins = [x, g, dy] + ([res] if has_res else [])
    return pl.pallas_call(
        body, name=name, grid=(rows // tm,),
        in_specs=[row_spec, vec_spec, row_spec] + ([row_spec] if has_res else []),
        out_specs=[row_spec, vec_spec],
        out_shape=[jax.ShapeDtypeStruct((rows, d), F32), jax.ShapeDtypeStruct((1, d), F32)],
        compiler_params=_params(("arbitrary",)),
    )(*ins)


def _loss_head(h, tgt, g, *, tm, name):
    rows, d = h.shape
    nsteps = rows // tm

    def body(h_ref, t_ref, g_ref, dh_ref, loss_ref, dg_ref, sq_ref):
        i = pl.program_id(0)
        xv = h_ref[...]
        gv = g_ref[...]
        r = lax.rsqrt(jnp.mean(xv * xv, axis=-1, keepdims=True) + RMS_EPS)
        xhat = xv * r
        err = xhat * gv - t_ref[...]
        dyv = err * (1.0 / d)
        dyg = dyv * gv
        dh_ref[...] = r * (dyg - xhat * jnp.mean(dyg * xhat, axis=-1, keepdims=True))

        @pl.when(i == 0)
        def _():
            dg_ref[...] = jnp.zeros_like(dg_ref)
            sq_ref[...] = jnp.zeros_like(sq_ref)

        dg_ref[...] += jnp.sum(dyv * xhat, axis=0, keepdims=True)
        sq_ref[...] += jnp.sum(err * err, axis=0, keepdims=True)

        @pl.when(i == nsteps - 1)
        def _():
            tot = jnp.sum(sq_ref[...], axis=-1, keepdims=True) * (0.5 / d)
            loss_ref[...] = jnp.broadcast_to(tot, loss_ref.shape)

    row_spec = pl.BlockSpec((tm, d), lambda i: (i, 0))
    vec_spec = pl.BlockSpec((1, d), lambda i: (0, 0))
    return pl.pallas_call(
        body, name=name, grid=(nsteps,),
        in_specs=[row_spec, row_spec, vec_spec],
        out_specs=[row_spec, pl.BlockSpec((1, LANES), lambda i: (0, 0)), vec_spec],
        out_shape=[jax.ShapeDtypeStruct((rows, d), F32), jax.ShapeDtypeStruct((1, LANES), F32),
                   jax.ShapeDtypeStruct((1, d), F32)],
        scratch_shapes=[pltpu.VMEM((1, d), F32)],
        compiler_params=_params(("arbitrary",)),
    )(h, tgt, g)


def _to_scan_layout(v):
    lead = v.shape[:-2]
    v = v.reshape(lead + (2, N_STATES // SCAN_CB, SCAN_CB))
    v = jnp.swapaxes(v, -3, -2)
    return v.reshape(lead + (2 * N_STATES,))


def _ssm_matrices(lam_re, lam_im, log_dt, b_re, b_im, c_re, c_im):
    dt = jnp.exp(log_dt)[:, None]
    mag = jnp.exp(lam_re * dt)
    a_re, a_im = mag * jnp.cos(lam_im * dt), mag * jnp.sin(lam_im * dt)
    nr, ni = a_re - 1.0, a_im
    den = lam_re * lam_re + lam_im * lam_im
    coef_re = (nr * lam_re + ni * lam_im) / den
    coef_im = (ni * lam_re - nr * lam_im) / den
    bb_re = coef_re[..., None] * b_re - coef_im[..., None] * b_im
    bb_im = coef_re[..., None] * b_im + coef_im[..., None] * b_re
    a_lay = _to_scan_layout(jnp.stack([a_re.reshape(-1), a_im.reshape(-1)], axis=0))[None, :]
    nblk = SSM_GROUPS // SCAN_GROUPS
    eye = jnp.eye(SCAN_GROUPS, dtype=F32)

    def b_block(bb):
        bb = bb.reshape(nblk, SCAN_GROUPS, SSM_STATE, SSM_GROUP_SIZE)
        return jnp.einsum("gk,jkph->jghkp", eye, bb).reshape(nblk, SCAN_GROUPS * SSM_GROUP_SIZE, SCAN_CB)

    b_blk = jnp.concatenate([b_block(bb_re), b_block(bb_im)], axis=2)

    def c_block(cc):
        cc = cc.reshape(nblk, SCAN_GROUPS, SSM_GROUP_SIZE, SSM_STATE)
        return jnp.einsum("gk,jghp->jkpgh", eye, cc).reshape(nblk, SCAN_CB, SCAN_GROUPS * SSM_GROUP_SIZE)

    c_blk = jnp.concatenate([c_block(c_re), -c_block(c_im)], axis=1)
    return a_lay, b_blk, c_blk


def _interleave(v):
    rows, c = v.shape
    return v.reshape(SCAN_SEGS, rows // SCAN_SEGS, c).transpose(1, 0, 2).reshape(rows, c)


def _deinterleave(v):
    rows, c = v.shape
    return v.reshape(rows // SCAN_SEGS, SCAN_SEGS, c).transpose(1, 0, 2).reshape(rows, c)


def _scan_groups(a_ref, bu_ref, o_ref, state, *, reverse, tt):
    cb = SCAN_CB
    ar = jnp.broadcast_to(a_ref[:, :cb], (SCAN_SEGS, cb))
    ai = jnp.broadcast_to(a_ref[:, cb:], (SCAN_SEGS, cb))
    ngroups = tt // SCAN_SEGS

    def step(i, st):
        sr, si = st
        r0 = pl.multiple_of(((ngroups - 1 - i) if reverse else i) * SCAN_SEGS, SCAN_SEGS)
        blk = bu_ref[pl.ds(r0, SCAN_SEGS), :]
        nr = ar * sr - ai * si + blk[:, :cb]
        ni = ar * si + ai * sr + blk[:, cb:]
        if o_ref is not None:
            o_ref[pl.ds(r0, SCAN_SEGS), :] = jnp.concatenate([nr, ni], axis=1)
        return nr, ni

    return lax.fori_loop(0, ngroups, step, state, unroll=4)


def _segment_entries(a_ref, e_ref, init_ref, *, reverse, seg_len):
    cb = SCAN_CB
    n_sq = seg_len.bit_length() - 1
    assert 1 << n_sq == seg_len, seg_len
    pr, pi = a_ref[:, :cb], a_ref[:, cb:]
    for _ in range(n_sq):
        pr, pi = pr * pr - pi * pi, 2.0 * pr * pi
    cr = jnp.zeros((1, cb), F32)
    ci = jnp.zeros((1, cb), F32)
    order = range(SCAN_SEGS - 1, -1, -1) if reverse else range(SCAN_SEGS)
    for k, seg in enumerate(order):
        if k > 0:
            prev = seg + 1 if reverse else seg - 1
            er, ei = e_ref[prev:prev + 1, :cb], e_ref[prev:prev + 1, cb:]
            cr, ci = pr * cr - pi * ci + er, pr * ci + pi * cr + ei
        init_ref[seg:seg + 1, :] = jnp.concatenate([cr, ci], axis=1)


def _ssm_specs(nt, tt, nch, reverse):
    cb = SCAN_CB
    tmap = (lambda j, kk: (nt - 1 - kk, j)) if reverse else (lambda j, kk: (kk, j))
    return dict(a=pl.BlockSpec((1, 2 * cb), lambda j, kk: (0, j)),
                seg=pl.BlockSpec((SCAN_SEGS, 2 * cb), lambda j, kk: (0, j)),
                chan=pl.BlockSpec((tt, nch), tmap),
                state=pl.BlockSpec((tt, 2 * cb), tmap),
                b=pl.BlockSpec((None, nch, 2 * cb), lambda j, kk: (j, 0, 0)),
                c=pl.BlockSpec((None, 2 * cb, nch), lambda j, kk: (j, 0, 0)))


def _ssm_ends(a_lay, x, blocks, *, transpose, reverse, tt, name):
    rows = x.shape[0]
    nblk = blocks.shape[0]
    nch = x.shape[1] // nblk
    cb = SCAN_CB
    nt = rows // tt
    sp = _ssm_specs(nt, tt, nch, reverse)

    def body(a_ref, x_ref, w_ref, e_ref, bu_ref):
        kk = pl.program_id(1)

        @pl.when(kk == 0)
        def _():
            e_ref[...] = jnp.zeros_like(e_ref)

        bu_ref[...] = _dot(x_ref[...].astype(BF16), w_ref[...].astype(BF16), 1, 1 if transpose else 0)
        sr, si = _scan_groups(a_ref, bu_ref, None, (e_ref[:, :cb], e_ref[:, cb:]), reverse=reverse, tt=tt)
        e_ref[...] = jnp.concatenate([sr, si], axis=1)

    return pl.pallas_call(
        body, name=name, grid=(nblk, nt),
        in_specs=[sp["a"], sp["chan"], sp["c"] if transpose else sp["b"]],
        out_specs=sp["seg"],
        out_shape=jax.ShapeDtypeStruct((SCAN_SEGS, nblk * 2 * cb), F32),
        scratch_shapes=[pltpu.VMEM((tt, 2 * cb), F32)],
        compiler_params=_params(("parallel", "arbitrary")),
    )(a_lay, x, blocks)


def _ssm_fwd(a_lay, u, b_blk, c_blk, ends, *, tt, name):
    rows = u.shape[0]
    nblk = b_blk.shape[0]
    nch = u.shape[1] // nblk
    cb = SCAN_CB
    nt = rows // tt
    sp = _ssm_specs(nt, tt, nch, False)

    def body(a_ref, e_ref, u_ref, b_ref, c_ref, s_ref, y_ref, init_ref, carry_ref):
        kk = pl.program_id(1)

        @pl.when(kk == 0)
        def _():
            _segment_entries(a_ref, e_ref, init_ref, reverse=False, seg_len=rows // SCAN_SEGS)
            carry_ref[...] = init_ref[...]

        s_ref[...] = _dot(u_ref[...].astype(BF16), b_ref[...].astype(BF16), 1, 0)
        sr, si = _scan_groups(a_ref, s_ref, s_ref, (carry_ref[:, :cb], carry_ref[:, cb:]), reverse=False, tt=tt)
        carry_ref[...] = jnp.concatenate([sr, si], axis=1)
        y_ref[...] = _dot(s_ref[...].astype(BF16), c_ref[...].astype(BF16), 1, 0)

    return pl.pallas_call(
        body, name=name, grid=(nblk, nt),
        in_specs=[sp["a"], sp["seg"], sp["chan"], sp["b"], sp["c"]],
        out_specs=[sp["state"], sp["chan"], sp["seg"]],
        out_shape=[jax.ShapeDtypeStruct((rows, nblk * 2 * cb), F32), jax.ShapeDtypeStruct((rows, nblk * nch), F32),
                   jax.ShapeDtypeStruct((SCAN_SEGS, nblk * 2 * cb), F32)],
        scratch_shapes=[pltpu.VMEM((SCAN_SEGS, 2 * cb), F32)],
        compiler_params=_params(("parallel", "arbitrary")),
    )(a_lay, ends, u, b_blk, c_blk)


def _ssm_bwd(a_conj, dy, u, s, s_entry, b_blk, c_blk, dd, ends, *, tt, name):
    rows = u.shape[0]
    nblk = b_blk.shape[0]
    nch = u.shape[1] // nblk
    cb = SCAN_CB
    nt = rows // tt
    sp = _ssm_specs(nt, tt, nch, True)
    groups_per_tile = tt // SCAN_SEGS
    before = pl.BlockSpec((SCAN_SEGS, 2 * cb), lambda j, kk: (jnp.maximum((nt - 1 - kk) * groups_per_tile - 1, 0), j))

    def body(a_ref, e_ref, dy_ref, u_ref, s_ref, before_ref, entry_ref, b_ref, c_ref, dd_ref,
             du_ref, db_ref, dc_ref, da_ref, lam_ref, carry_ref):
        kk = pl.program_id(1)

        @pl.when(kk == 0)
        def _():
            _segment_entries(a_ref, e_ref, carry_ref, reverse=True, seg_len=rows // SCAN_SEGS)
            db_ref[...] = jnp.zeros_like(db_ref)
            dc_ref[...] = jnp.zeros_like(dc_ref)
            da_ref[...] = jnp.zeros_like(da_ref)

        dyv = dy_ref[...]
        dyb = dyv.astype(BF16)
        lam_ref[...] = _dot(dyb, c_ref[...].astype(BF16), 1, 1)
        lr, li = _scan_groups(a_ref, lam_ref, lam_ref, (carry_ref[:, :cb], carry_ref[:, cb:]), reverse=True, tt=tt)
        carry_ref[...] = jnp.concatenate([lr, li], axis=1)

        first = jnp.where(kk == nt - 1, entry_ref[...], before_ref[...])
        rest = tt - SCAN_SEGS
        lam_hi = lam_ref[pl.ds(SCAN_SEGS, rest), :]
        s_lo = s_ref[pl.ds(0, rest), :]
        lam_lo = lam_ref[pl.ds(0, SCAN_SEGS), :]

        def pair(lv, pv):
            lre, lim, pre, pim = lv[:, :cb], lv[:, cb:], pv[:, :cb], pv[:, cb:]
            return (jnp.sum(lre * pre + lim * pim, axis=0, keepdims=True),
                    jnp.sum(lim * pre - lre * pim, axis=0, keepdims=True))

        r1, i1 = pair(lam_hi, s_lo)
        r0, i0 = pair(lam_lo, first)
        da_ref[...] += jnp.concatenate([r1 + r0, i1 + i0], axis=1)

        lamb = lam_ref[...].astype(BF16)
        du_ref[...] = _dot(lamb, b_ref[...].astype(BF16), 1, 1) + dd_ref[...] * dyv
        db_ref[...] += _dot(u_ref[...].astype(BF16), lamb, 0, 0)
        dc_ref[...] += _dot(s_ref[...].astype(BF16), dyb, 0, 0)

    return pl.pallas_call(
        body, name=name, grid=(nblk, nt),
        in_specs=[sp["a"], sp["seg"], sp["chan"], sp["chan"], sp["state"], before, sp["seg"], sp["b"], sp["c"],
                  pl.BlockSpec((1, nch), lambda j, kk: (0, j))],
        out_specs=[sp["chan"], sp["b"], sp["c"], pl.BlockSpec((1, 2 * cb), lambda j, kk: (0, j))],
        out_shape=[jax.ShapeDtypeStruct((rows, nblk * nch), F32), jax.ShapeDtypeStruct(b_blk.shape, F32),
                   jax.ShapeDtypeStruct(c_blk.shape, F32), jax.ShapeDtypeStruct((1, nblk * 2 * cb), F32)],
        scratch_shapes=[pltpu.VMEM((tt, 2 * cb), F32), pltpu.VMEM((SCAN_SEGS, 2 * cb), F32)],
        compiler_params=_params(("parallel", "arbitrary")),
    )(a_conj, ends, dy, u, s, s, s_entry, b_blk, c_blk, dd)


def _glu_fwd(ys, u, dd, w_glu, b_glu, *, tm, name):
    rows, w = ys.shape

    def body(ys_ref, u_ref, dd_ref, w_ref, b_ref, y0_ref, t_ref, y2_ref):
        y0 = ys_ref[...] + dd_ref[...] * u_ref[...]
        y1 = _gelu(y0)
        t = _dot(y1.astype(BF16), w_ref[...], 1, 0) + b_ref[...]
        y0_ref[...] = y0
        t_ref[...] = t
        y2_ref[...] = (y1 * _sigmoid(t)).astype(BF16)

    row = pl.BlockSpec((tm, w), lambda i: (i, 0))
    vec = pl.BlockSpec((1, w), lambda i: (0, 0))
    return pl.pallas_call(
        body, name=name, grid=(rows // tm,),
        in_specs=[row, row, vec, pl.BlockSpec((w, w), lambda i: (0, 0)), vec],
        out_specs=[row, row, row],
        out_shape=[jax.ShapeDtypeStruct((rows, w), F32), jax.ShapeDtypeStruct((rows, w), F32),
                   jax.ShapeDtypeStruct((rows, w), BF16)],
        compiler_params=_params(("parallel",)),
    )(ys, u, dd, w_glu, b_glu)


def _glu_bwd(dy2, y0, t, u, w_glu, *, tm, name):
    rows, w = y0.shape

    def body(dy2_ref, y0_ref, t_ref, u_ref, w_ref, dy0_ref, dt_ref, y1_ref, db_ref, dd_ref):
        i = pl.program_id(0)
        y0 = y0_ref[...]
        y1 = _gelu(y0)
        sg = _sigmoid(t_ref[...])
        dy2v = dy2_ref[...]
        dt = dy2v * y1 * sg * (1.0 - sg)
        dy1 = dy2v * sg + _dot(dt.astype(BF16), w_ref[...], 1, 1)
        dy0 = dy1 * _gelu_grad(y0)
        dy0_ref[...] = dy0
        dt_ref[...] = dt.astype(BF16)
        y1_ref[...] = y1.astype(BF16)

        @pl.when(i == 0)
        def _():
            db_ref[...] = jnp.zeros_like(db_ref)
            dd_ref[...] = jnp.zeros_like(dd_ref)

        db_ref[...] += jnp.sum(dt, axis=0, keepdims=True)
        dd_ref[...] += jnp.sum(dy0 * u_ref[...], axis=0, keepdims=True)

    row = pl.BlockSpec((tm, w), lambda i: (i, 0))
    vec = pl.BlockSpec((1, w), lambda i: (0, 0))
    return pl.pallas_call(
        body, name=name, grid=(rows // tm,),
        in_specs=[row, row, row, row, pl.BlockSpec((w, w), lambda i: (0, 0))],
        out_specs=[row, row, row, vec, vec],
        out_shape=[jax.ShapeDtypeStruct((rows, w), F32), jax.ShapeDtypeStruct((rows, w), BF16),
                   jax.ShapeDtypeStruct((rows, w), BF16), jax.ShapeDtypeStruct((1, w), F32),
                   jax.ShapeDtypeStruct((1, w), F32)],
        compiler_params=_params(("arbitrary",)),
    )(dy2, y0, t, u, w_glu)


ATTN_TILE = 2048


def _attn_geometry(rows, d):
    sb = ATTN_Q * d
    tr = max(sb, min(ATTN_TILE, rows))
    assert rows % tr == 0 and tr % sb == 0, (rows, d)
    return sb, tr, rows // tr, tr // sb


def _attn_masks():
    qi = lax.broadcasted_iota(jnp.int32, (2 * ATTN_Q, 2 * ATTN_Q), 0) % ATTN_Q
    kj = lax.broadcasted_iota(jnp.int32, (2 * ATTN_Q, 2 * ATTN_Q), 1)
    own_ok = jnp.logical_and(kj >= ATTN_Q, kj - ATTN_Q <= qi)
    prev_ok = jnp.logical_and(kj < ATTN_Q, kj >= qi)
    bias_first = jnp.where(own_ok, 0.0, NEG_INF)
    bias_other = jnp.where(jnp.logical_or(own_ok, prev_ok), 0.0, NEG_INF)
    head0 = lax.broadcasted_iota(jnp.int32, (ATTN_Q, LANES), 1) < ATTN_HEAD_DIM
    return bias_first, bias_other, head0


def _attn_rows(base, n, d):
    return pl.ds(pl.multiple_of(base, ATTN_Q), n) if d == 1 else pl.ds(base, n, stride=d)


def _stack_heads(v, head0):
    return jnp.concatenate([jnp.where(head0, v, 0.0), jnp.where(head0, 0.0, v)], axis=0)


def _unstack_heads(v, head0):
    return jnp.where(head0, v[:ATTN_Q], v[ATTN_Q:])


def _fill_keys(buf, prev_ref, cur_ref, sb):
    buf[pl.ds(0, sb), :] = prev_ref[...]
    buf[pl.ds(sb, cur_ref.shape[0]), :] = cur_ref[...]


def _attn_fwd(qkv, g, d, *, name):
    rows = qkv.shape[0]
    sb, tr, ntiles, nsub = _attn_geometry(rows, d)
    qc, kc, vc = 2 * g, 6 + 2 * g, 12 + 2 * g
    scale = ATTN_HEAD_DIM ** -0.5

    def body(q_ref, kc_ref, kp_ref, vc_ref, vp_ref, o_ref, lse_ref, kbuf, vbuf):
        n = pl.program_id(0)
        _fill_keys(kbuf, kp_ref, kc_ref, sb)
        _fill_keys(vbuf, vp_ref, vc_ref, sb)
        bias_first, bias_other, head0 = _attn_masks()

        def per_block(idx, carry):
            j, r = idx // d, idx % d
            base = j * sb + r
            bias = jnp.where(jnp.logical_and(n == 0, j == 0), bias_first, bias_other)
            qrows = _attn_rows(base, ATTN_Q, d)
            krows = _attn_rows(base, 2 * ATTN_Q, d)
            qs = _stack_heads(q_ref[qrows, :], head0).astype(BF16)
            s = _dot(qs, kbuf[krows, :].astype(BF16), 1, 1) * scale + bias
            mx = jnp.max(s, axis=-1, keepdims=True)
            p = jnp.exp(s - mx)
            den = jnp.sum(p, axis=-1, keepdims=True)
            pv = _dot(p.astype(BF16), vbuf[krows, :].astype(BF16), 1, 0) / den
            o_ref[qrows, :] = _unstack_heads(pv, head0)
            lse_ref[qrows, :] = _unstack_heads(jnp.broadcast_to(mx + jnp.log(den), (2 * ATTN_Q, LANES)), head0)
            return carry

        lax.fori_loop(0, nsub * d, per_block, 0, unroll=8)

    def cur(col):
        return pl.BlockSpec((tr, LANES), lambda n, hp: (n, col + hp))

    def prev(col):
        return pl.BlockSpec((sb, LANES), lambda n, hp: (jnp.maximum(n * nsub - 1, 0), col + hp))

    out_spec = pl.BlockSpec((tr, LANES), lambda n, hp: (n, hp))
    return pl.pallas_call(
        body, name=name, grid=(ntiles, 2),
        in_specs=[cur(qc), cur(kc), prev(kc), cur(vc), prev(vc)],
        out_specs=[out_spec, out_spec],
        out_shape=[jax.ShapeDtypeStruct((rows, 2 * LANES), F32), jax.ShapeDtypeStruct((rows, 2 * LANES), F32)],
        scratch_shapes=[pltpu.VMEM((sb + tr, LANES), F32), pltpu.VMEM((sb + tr, LANES), F32)],
        compiler_params=_params(("parallel", "parallel")),
    )(qkv, qkv, qkv, qkv, qkv)


def _attn_merge(outs, lses, *, tm, name):
    rows, w = outs[0].shape

    def body(o0, o1, o2, l0, l1, l2, o_ref, lse_ref):
        a0, a1, a2 = l0[...], l1[...], l2[...]
        mx = jnp.maximum(jnp.maximum(a0, a1), a2)
        e0, e1, e2 = jnp.exp(a0 - mx), jnp.exp(a1 - mx), jnp.exp(a2 - mx)
        den = e0 + e1 + e2
        o_ref[...] = (e0 / den) * o0[...] + (e1 / den) * o1[...] + (e2 / den) * o2[...]
        lse_ref[...] = mx + jnp.log(den)

    row = pl.BlockSpec((tm, w), lambda i: (i, 0))
    return pl.pallas_call(
        body, name=name, grid=(rows // tm,), in_specs=[row] * 6, out_specs=[row, row],
        out_shape=[jax.ShapeDtypeStruct((rows, w), F32), jax.ShapeDtypeStruct((rows, w), F32)],
        compiler_params=_params(("parallel",)),
    )(*outs, *lses)


def _attn_bwd(qkv, do, o, lse, g, d, prev, *, name):
    rows = qkv.shape[0]
    sb, tr, ntiles, nsub = _attn_geometry(rows, d)
    qc, kc, vc = 2 * g, 6 + 2 * g, 12 + 2 * g
    scale = ATTN_HEAD_DIM ** -0.5

    def body(q_ref, kc_ref, kp_ref, vc_ref, vp_ref, do_ref, o_ref, lse_ref, dq_ref, dk_ref, dv_ref,
             kbuf, vbuf, dk_acc, dv_acc):
        n = pl.program_id(1)

        @pl.when(n == 0)
        def _():
            dk_acc[pl.ds(0, tr), :] = jnp.zeros((tr, LANES), F32)
            dv_acc[pl.ds(0, tr), :] = jnp.zeros((tr, LANES), F32)

        @pl.when(n < ntiles)
        def _():
            dk_acc[pl.ds(tr, tr), :] = jnp.zeros((tr, LANES), F32)
            dv_acc[pl.ds(tr, tr), :] = jnp.zeros((tr, LANES), F32)
            _fill_keys(kbuf, kp_ref, kc_ref, sb)
            _fill_keys(vbuf, vp_ref, vc_ref, sb)
            bias_first, bias_other, head0 = _attn_masks()
            lane = lax.broadcasted_iota(jnp.int32, (ATTN_Q, LANES), 1)

            def per_block(idx, carry):
                j, r = idx // d, idx % d
                base = j * sb + r
                bias = jnp.where(jnp.logical_and(n == 0, j == 0), bias_first, bias_other)
                qrows = _attn_rows(base, ATTN_Q, d)
                krows = _attn_rows(base, 2 * ATTN_Q, d)
                arows = _attn_rows(base + (tr - sb), 2 * ATTN_Q, d)
                qs = _stack_heads(q_ref[qrows, :], head0).astype(BF16)
                dos = _stack_heads(do_ref[qrows, :], head0)
                dosb = dos.astype(BF16)
                ov = o_ref[qrows, :]
                delta = jnp.sum(dos * jnp.concatenate([ov, ov], axis=0), axis=-1, keepdims=True)
                lsev = lse_ref[qrows, :]
                lse_s = jnp.concatenate(
                    [jnp.sum(jnp.where(lane == h * ATTN_HEAD_DIM, lsev, 0.0), axis=-1, keepdims=True) for h in range(2)], axis=0)
                kb = kbuf[krows, :].astype(BF16)
                vb = vbuf[krows, :].astype(BF16)
                p = jnp.exp(_dot(qs, kb, 1, 1) * scale + bias - lse_s)
                ds = (p * (_dot(dosb, vb, 1, 1) - delta) * scale).astype(BF16)
                dq_ref[qrows, :] = _unstack_heads(_dot(ds, kb, 1, 0), head0)
                dk_acc[arows, :] += _dot(ds, qs, 0, 0)
                dv_acc[arows, :] += _dot(p.astype(BF16), dosb, 0, 0)
                return carry

            lax.fori_loop(0, nsub * d, per_block, 0, unroll=4)

        dk_ref[...] = dk_acc[pl.ds(0, tr), :]
        dv_ref[...] = dv_acc[pl.ds(0, tr), :]
        dk_acc[pl.ds(0, tr), :] = dk_acc[pl.ds(tr, tr), :]
        dv_acc[pl.ds(0, tr), :] = dv_acc[pl.ds(tr, tr), :]

    def cur(n):
        return jnp.minimum(n, ntiles - 1)

    def spec(col, prev):
        if prev:
            return pl.BlockSpec((sb, LANES), lambda hp, n: (jnp.maximum(cur(n) * nsub - 1, 0), col + hp))
        return pl.BlockSpec((tr, LANES), lambda hp, n: (cur(n), col + hp))

    row_spec = pl.BlockSpec((tr, LANES), lambda hp, n: (cur(n), hp))
    dq_out = pl.BlockSpec((tr, LANES), lambda hp, n: (cur(n), 2 * g + hp))
    kv_out = pl.BlockSpec((tr, LANES), lambda hp, n: (jnp.maximum(n - 1, 0), 2 * g + hp))
    shape = jax.ShapeDtypeStruct((rows, len(ATTN_PATTERNS) * 2 * LANES), F32)
    ins = [qkv, qkv, qkv, qkv, qkv, do, o, lse]
    in_specs = [spec(qc, False), spec(kc, False), spec(kc, True), spec(vc, False), spec(vc, True),
                row_spec, row_spec, row_spec]
    aliases = {}
    if prev is not None:
        aliases = {len(ins) + t: t for t in range(3)}
        ins = ins + list(prev)
        in_specs = in_specs + [ANY] * 3
    n_in = len(ins)

    def entry(*refs):
        body(*refs[:8], *refs[n_in:])

    return pl.pallas_call(
        entry, name=name, grid=(2, ntiles + 1),
        in_specs=in_specs,
        out_specs=[dq_out, kv_out, kv_out],
        out_shape=[shape, shape, shape],
        input_output_aliases=aliases,
        scratch_shapes=[pltpu.VMEM((sb + tr, LANES), F32), pltpu.VMEM((sb + tr, LANES), F32),
                        pltpu.VMEM((2 * tr, LANES), F32), pltpu.VMEM((2 * tr, LANES), F32)],
        compiler_params=_params(("parallel", "arbitrary")),
    )(*ins)


def _mem_probs(q, k):
    s = _dot(q.astype(BF16), k.astype(BF16), 1, 1) * (MEM_HEAD_DIM ** -0.5)
    e = jnp.exp(s - jnp.max(s, axis=-1, keepdims=True))
    return e / jnp.sum(e, axis=-1, keepdims=True)


def _mem_attn_fwd(mq, kv, *, tq, name):
    rows = mq.shape[0]

    def body(q_ref, k_ref, v_ref, o_ref):
        p = _mem_probs(q_ref[...], k_ref[...])
        o_ref[...] = _dot(p.astype(BF16), v_ref[...].astype(BF16), 1, 0)

    return pl.pallas_call(
        body, name=name, grid=(rows // tq, MEM_HEADS),
        in_specs=[pl.BlockSpec((tq, LANES), lambda i, h: (i, h)),
                  pl.BlockSpec((MEM_LEN, LANES), lambda i, h: (0, h)),
                  pl.BlockSpec((MEM_LEN, LANES), lambda i, h: (0, MEM_HEADS + h))],
        out_specs=pl.BlockSpec((tq, LANES), lambda i, h: (i, h)),
        out_shape=jax.ShapeDtypeStruct((rows, MEM_HEADS * LANES), F32),
        compiler_params=_params(("parallel", "parallel")),
    )(mq, kv, kv)


def _mem_attn_bwd(mq, kv, dmo, *, tq, name):
    rows = mq.shape[0]
    scale = MEM_HEAD_DIM ** -0.5

    def body(q_ref, k_ref, v_ref, do_ref, dq_ref, dk_ref, dv_ref):
        i = pl.program_id(1)
        qb = q_ref[...].astype(BF16)
        kb = k_ref[...].astype(BF16)
        vb = v_ref[...].astype(BF16)
        dob = do_ref[...].astype(BF16)
        p = _mem_probs(q_ref[...], k_ref[...])
        dp = _dot(dob, vb, 1, 1)
        ds = (p * (dp - jnp.sum(p * dp, axis=-1, keepdims=True)) * scale).astype(BF16)
        dq_ref[...] = _dot(ds, kb, 1, 0).astype(dq_ref.dtype)

        @pl.when(i == 0)
        def _():
            dk_ref[...] = jnp.zeros_like(dk_ref)
            dv_ref[...] = jnp.zeros_like(dv_ref)

        dk_ref[...] += _dot(ds, qb, 0, 0)
        dv_ref[...] += _dot(p.astype(BF16), dob, 0, 0)

    kv_out = pl.BlockSpec((MEM_LEN, LANES), lambda h, i: (0, h))
    kv_shape = jax.ShapeDtypeStruct((MEM_LEN, MEM_HEADS * LANES), F32)
    return pl.pallas_call(
        body, name=name, grid=(MEM_HEADS, rows // tq),
        in_specs=[pl.BlockSpec((tq, LANES), lambda h, i: (i, h)),
                  pl.BlockSpec((MEM_LEN, LANES), lambda h, i: (0, h)),
                  pl.BlockSpec((MEM_LEN, LANES), lambda h, i: (0, MEM_HEADS + h)),
                  pl.BlockSpec((tq, LANES), lambda h, i: (i, h))],
        out_specs=[pl.BlockSpec((tq, LANES), lambda h, i: (i, h)), kv_out, kv_out],
        out_shape=[jax.ShapeDtypeStruct((rows, MEM_HEADS * LANES), BF16), kv_shape, kv_shape],
        compiler_params=_params(("parallel", "arbitrary")),
    )(mq, kv, kv, dmo)


def _resident(shape):
    return pl.BlockSpec(shape, lambda i: (0, 0), pipeline_mode=pl.Buffered(1))


def _branch_merge_fwd(acts, wts, zg, b_gate, *, tm, name):
    rows = zg.shape[0]
    d = wts[0].shape[0]

    def body(s_ref, a_ref, m_ref, ws_ref, wa_ref, wm_ref, zg_ref, b_ref, o_ref):
        gt = _sigmoid(zg_ref[...] + b_ref[...])
        acc = None
        for k, (x_ref, w_ref) in enumerate(((s_ref, ws_ref), (a_ref, wa_ref), (m_ref, wm_ref))):
            term = gt[:, k * d:(k + 1) * d] * _dot(x_ref[...].astype(BF16), w_ref[...], 1, 1)
            acc = term if acc is None else acc + term
        o_ref[...] = acc.astype(BF16)

    return pl.pallas_call(
        body, name=name, grid=(rows // tm,),
        in_specs=[pl.BlockSpec((tm, x.shape[1]), lambda i: (i, 0)) for x in acts] + [_resident(w.shape) for w in wts]
        + [pl.BlockSpec((tm, 3 * d), lambda i: (i, 0)), pl.BlockSpec((1, 3 * d), lambda i: (0, 0))],
        out_specs=pl.BlockSpec((tm, d), lambda i: (i, 0)), out_shape=jax.ShapeDtypeStruct((rows, d), BF16),
        compiler_params=_params(("parallel",)),
    )(*acts, *wts, zg, b_gate)


def _branch_merge_bwd(dmerged, acts, wts, zg, b_gate, *, tm, name, carry=None):
    rows = zg.shape[0]
    d = wts[0].shape[0]

    def body(dm_ref, s_ref, a_ref, m_ref, ws_ref, wa_ref, wm_ref, zg_ref, b_ref,
             ds_ref, da_ref, dmm_ref, dws_ref, dwa_ref, dwm_ref, dzg_ref, db_ref):
        i = pl.program_id(0)

        @pl.when(i == 0)
        def _():
            for r in (dws_ref, dwa_ref, dwm_ref, db_ref):
                r[...] = jnp.zeros_like(r)

        gt = _sigmoid(zg_ref[...] + b_ref[...])
        dm = dm_ref[...]
        groups = ((s_ref, ws_ref, ds_ref, dws_ref), (a_ref, wa_ref, da_ref, dwa_ref), (m_ref, wm_ref, dmm_ref, dwm_ref))
        for k, (x_ref, w_ref, dx_ref, dw_ref) in enumerate(groups):
            cs = pl.ds(k * d, d)
            gk = gt[:, k * d:(k + 1) * d]
            xb = x_ref[...].astype(BF16)
            br = _dot(xb, w_ref[...], 1, 1)
            dbr = (dm * gk).astype(BF16)
            dx_ref[...] = _dot(dbr, w_ref[...], 1, 0)
            dw_ref[...] += _dot(dbr, xb, 0, 0)
            dzg = dm * br * gk * (1.0 - gk)
            dzg_ref[:, cs] = dzg.astype(BF16)
            db_ref[:, cs] += jnp.sum(dzg, axis=0, keepdims=True)

    row = lambda w: pl.BlockSpec((tm, w), lambda i: (i, 0))
    whole = lambda shape: pl.BlockSpec(shape, lambda i: (0, 0))
    res = _call_with_carry(
        body, carry, name=name, grid=(rows // tm,),
        in_specs=[row(d)] + [row(x.shape[1]) for x in acts] + [_resident(w.shape) for w in wts] + [row(3 * d), whole((1, 3 * d))],
        out_specs=[row(x.shape[1]) for x in acts] + [whole(w.shape) for w in wts] + [row(3 * d), whole((1, 3 * d))],
        out_shape=[jax.ShapeDtypeStruct(x.shape, F32) for x in acts] + [jax.ShapeDtypeStruct(w.shape, F32) for w in wts]
        + [jax.ShapeDtypeStruct((rows, 3 * d), BF16), jax.ShapeDtypeStruct((1, 3 * d), F32)],
        scratch=[], operands=[dmerged, *acts, *wts, zg, b_gate], semantics=("arbitrary",))
    return tuple(res) if carry is None else (tuple(res[:8]), list(res[8:]))


def _adamw(w, g, m, v, *, tr, name):
    rows, cols = w.shape[-2:]
    assert rows % tr == 0, (name, rows, tr)

    def body(w_ref, g_ref, m_ref, v_ref, g_out, d_ref, nm_ref, nv_ref):
        gv = g_ref[...]
        m2 = ADAM_B1 * m_ref[...] + (1.0 - ADAM_B1) * gv
        v2 = ADAM_B2 * v_ref[...] + (1.0 - ADAM_B2) * (gv * gv)
        m_hat = m2 / (1.0 - ADAM_B1 ** ADAM_STEP)
        v_hat = v2 / (1.0 - ADAM_B2 ** ADAM_STEP)
        g_out[...] = gv
        d_ref[...] = -ADAM_LR * (m_hat / (jnp.sqrt(v_hat) + ADAM_EPS) + ADAM_WD * w_ref[...])
        nm_ref[...] = m2
        nv_ref[...] = v2

    flat = pl.BlockSpec((tr, cols), lambda i: (i, 0))
    blk = flat if w.ndim == 2 else pl.BlockSpec((None, tr, cols), lambda i: (0, i, 0))
    shape = jax.ShapeDtypeStruct(w.shape, F32)
    return pl.pallas_call(
        body, name=name, grid=(rows // tr,), in_specs=[blk, flat, blk, blk], out_specs=[blk] * 4,
        out_shape=[shape] * 4, compiler_params=_params(("parallel",)),
    )(w, g, m, v)


ANY = pl.BlockSpec(memory_space=pl.ANY)


def _position():
    return lax.axis_index("x"), lax.axis_index("y"), lax.axis_index("c")


def _other_chips(x, y):
    return ((1 - x, y), (x, 1 - y), (1 - x, 1 - y))


def _remote(src, dst, send_sem, recv_sem, dev):
    return pltpu.make_async_remote_copy(src_ref=src, dst_ref=dst, send_sem=send_sem, recv_sem=recv_sem,
                                        device_id=dev, device_id_type=MESH)


def _gather_exchange(shards):
    nb = len(shards)

    def rows_of(i, owner, core):
        rs = shards[i].shape[0]
        return pl.ds(pl.multiple_of(owner * rs + core * (rs // 2), 16), rs // 2)

    def first_leg(ins, outs, send_sems, recv_sems, i, j):
        x, y, c = _position()
        px, py = _other_chips(x, y)[j]
        half = shards[i].shape[0] // 2
        mine = ins[i].at[pl.ds(pl.multiple_of(c * half, 16), half)]
        return _remote(mine, outs[i].at[rows_of(i, 2 * x + y, c)], send_sems.at[i, j], recv_sems.at[i, j], (px, py, c))

    def passed_on(outs, send_sems, recv_sems, i, j, core):
        x, y, c = _position()
        px, py = _other_chips(x, y)[j]
        rows = outs[i].at[rows_of(i, 2 * px + py, core)]
        return _remote(rows, rows, send_sems.at[i, 3 + j], recv_sems.at[i, 3 + j], (x, y, 1 - c))

    def own_block(ins, outs, send_sems, recv_sems, i):
        x, y, c = _position()
        rs = shards[i].shape[0]
        place = outs[i].at[pl.ds(pl.multiple_of((2 * x + y) * rs, 16), rs)]
        return _remote(ins[i], place, send_sems.at[i, 6], recv_sems.at[i, 6], (x, y, 1 - c))

    def start(ins, outs, send_sems, recv_sems):
        for i in range(nb):
            own_block(ins, outs, send_sems, recv_sems, i).start()
            for j in range(3):
                first_leg(ins, outs, send_sems, recv_sems, i, j).start()

    def finish(ins, outs, send_sems, recv_sems):
        x, y, c = _position()
        for i in range(nb):
            for j, (px, py) in enumerate(_other_chips(x, y)):
                landed = outs[i].at[rows_of(i, 2 * px + py, c)]
                _remote(landed, landed, send_sems.at[i, j], recv_sems.at[i, j], (px, py, c)).wait_recv()
                passed_on(outs, send_sems, recv_sems, i, j, c).start()
        for i in range(nb):
            own_block(ins, outs, send_sems, recv_sems, i).wait()
            for j in range(3):
                passed_on(outs, send_sems, recv_sems, i, j, 1 - c).wait_recv()
        for i in range(nb):
            for j in range(3):
                first_leg(ins, outs, send_sems, recv_sems, i, j).wait_send()
                passed_on(outs, send_sems, recv_sems, i, j, c).wait_send()

    return _Exchange(ins=list(shards), outs=[jax.ShapeDtypeStruct((N_CHIPS * s.shape[0], s.shape[1]), s.dtype) for s in shards],
                     aliases={}, sems=[(nb, 7), (nb, 7)], start=start, finish=finish)


def _run_exchange(ex, *, name):
    n_in, n_out = len(ex.ins), len(ex.outs)

    def body(*refs):
        c_in, c_out, sems = refs[:n_in], refs[n_in:n_in + n_out], refs[n_in + n_out:]
        ex.start(c_in, c_out, *sems)
        ex.finish(c_in, c_out, *sems)

    return pl.pallas_call(
        body, name=name, in_specs=[ANY] * n_in, out_specs=[ANY] * n_out, out_shape=list(ex.outs),
        input_output_aliases=dict(ex.aliases),
        scratch_shapes=[pltpu.SemaphoreType.DMA(s) for s in ex.sems],
    )(*ex.ins)


def _row_tile(rows):
    return max(t for t in range(16, min(rows, 512) + 1, 16) if rows % t == 0)


def _halves_exchange(grads):
    nb = len(grads)

    def copies(ins, outs, send_sems, recv_sems):
        x, y, c = _position()
        return [_remote(ins[i].at[:, 1 - c], outs[i], send_sems.at[i], recv_sems.at[i], (x, y, 1 - c)) for i in range(nb)]

    def start(ins, outs, send_sems, recv_sems):
        for cp in copies(ins, outs, send_sems, recv_sems):
            cp.start()

    def finish(ins, outs, send_sems, recv_sems):
        for cp in copies(ins, outs, send_sems, recv_sems):
            cp.wait()

    return _Exchange(ins=list(grads), outs=[jax.ShapeDtypeStruct((N_CHIPS, g.shape[2], g.shape[3]), F32) for g in grads],
                     aliases={}, sems=[(nb,), (nb,)], start=start, finish=finish)


def _join_exchanges(parts):
    assert all(not ex.aliases for ex in parts)

    def split(refs, counts):
        out, at = [], 0
        for k in counts:
            out.append(refs[at:at + k])
            at += k
        return out

    def run(which):
        def go(ins, outs, *sems):
            for ex, i, o, s in zip(parts, split(ins, [len(ex.ins) for ex in parts]), split(outs, [len(ex.outs) for ex in parts]),
                                   split(sems, [len(ex.sems) for ex in parts])):
                getattr(ex, which)(i, o, *s)
        return go

    return _Exchange(ins=[a for ex in parts for a in ex.ins], outs=[a for ex in parts for a in ex.outs], aliases={},
                     sems=[s for ex in parts for s in ex.sems], start=run("start"), finish=run("finish"))


def _pair_sum(g4, got, c_arr, *, name):
    _, _, half, cols = g4.shape
    tr = _row_tile(half)

    def body(c_ref, g_ref, t_ref, p_ref, pb_ref):
        sm = g_ref[...] + t_ref[...]
        p_ref[...] = sm
        pb_ref[...] = sm.astype(BF16)

    blk = pl.BlockSpec((None, tr, cols), lambda j, i, c_ref: (j, i, 0))
    grid_spec = pltpu.PrefetchScalarGridSpec(
        num_scalar_prefetch=1, grid=(N_CHIPS, half // tr),
        in_specs=[pl.BlockSpec((None, None, tr, cols), lambda j, i, c_ref: (j, c_ref[0], i, 0)), blk],
        out_specs=[blk, blk])
    return pl.pallas_call(
        body, name=name, grid_spec=grid_spec,
        out_shape=[jax.ShapeDtypeStruct((N_CHIPS, half, cols), F32), jax.ShapeDtypeStruct((N_CHIPS, half, cols), BF16)],
        compiler_params=_params(("parallel", "parallel")),
    )(c_arr, g4, got)


def _scatter_exchange(parts):
    nb = len(parts)

    def copies(ins, outs, send_sems, recv_sems):
        x, y, c = _position()
        return [_remote(ins[i].at[2 * px + py], outs[i].at[j], send_sems.at[i, j], recv_sems.at[i, j], (px, py, c))
                for i in range(nb) for j, (px, py) in enumerate(_other_chips(x, y))]

    def start(ins, outs, send_sems, recv_sems):
        for cp in copies(ins, outs, send_sems, recv_sems):
            cp.start()

    def finish(ins, outs, send_sems, recv_sems):
        for cp in copies(ins, outs, send_sems, recv_sems):
            cp.wait()

    return _Exchange(ins=list(parts), outs=[jax.ShapeDtypeStruct((3,) + p.shape[1:], p.dtype) for p in parts],
                     aliases={}, sems=[(nb, 3), (nb, 3)], start=start, finish=finish)


def _owner_sum(p, got, chip_arr, c_arr, *, replicated, name):
    _, half, cols = p.shape
    tr = _row_tile(half)

    def body(chip_ref, c_ref, p_ref, r_ref, o_ref):
        o_ref[...] = ((p_ref[...] + r_ref[0].astype(F32)) + r_ref[1].astype(F32)) + r_ref[2].astype(F32)

    if replicated:
        out_spec = pl.BlockSpec((None, None, tr, cols), lambda i, chip_ref, c_ref: (chip_ref[0], c_ref[0], i, 0))
        out_shape = jax.ShapeDtypeStruct((N_CHIPS, 2, half, cols), F32)
    else:
        out_spec = pl.BlockSpec((None, tr, cols), lambda i, chip_ref, c_ref: (c_ref[0], i, 0))
        out_shape = jax.ShapeDtypeStruct((2, half, cols), F32)
    grid_spec = pltpu.PrefetchScalarGridSpec(
        num_scalar_prefetch=2, grid=(half // tr,),
        in_specs=[pl.BlockSpec((None, tr, cols), lambda i, chip_ref, c_ref: (chip_ref[0], i, 0)),
                  pl.BlockSpec((3, tr, cols), lambda i, chip_ref, c_ref: (0, i, 0))],
        out_specs=out_spec)
    return pl.pallas_call(
        body, name=name, grid_spec=grid_spec, out_shape=out_shape,
        compiler_params=_params(("parallel",)),
    )(chip_arr, c_arr, p, got)


def _share_reduced(bufs):
    nb = len(bufs) - 1

    def body(*refs):
        outs = refs[nb + 1:2 * nb + 2]
        send_sems, recv_sems = refs[2 * nb + 2:]
        x, y, c = _position()
        chip = 2 * x + y
        sends = []
        for i in range(nb):
            cp = _remote(outs[i].at[c], outs[i].at[c], send_sems.at[i], recv_sems.at[i], (x, y, 1 - c))
            cp.start()
            sends.append(cp)
        small = outs[nb]
        peers = [(fx, fy, fc) for fx in (0, 1) for fy in (0, 1) for fc in (0, 1) if fx + fy + fc > 0]
        for k, (fx, fy, fc) in enumerate(peers):
            dev = (x ^ fx, y ^ fy, c ^ fc)
            cp = _remote(small.at[chip, c], small.at[chip, c], send_sems.at[nb + k], recv_sems.at[nb + k], dev)
            cp.start()
            sends.append(cp)
        for i in range(nb):
            dst = outs[i].at[1 - c]
            _remote(dst, dst, send_sems.at[i], recv_sems.at[i], (x, y, 1 - c)).wait_recv()
        for k, (fx, fy, fc) in enumerate(peers):
            dst = small.at[2 * (x ^ fx) + (y ^ fy), c ^ fc]
            _remote(dst, dst, send_sems.at[nb + k], recv_sems.at[nb + k], (x ^ fx, y ^ fy, c ^ fc)).wait_recv()
        for cp in sends:
            cp.wait_send()

    n_all = nb + 1
    return pl.pallas_call(
        body, name="grad_share_reduced", in_specs=[ANY] * n_all, out_specs=[ANY] * n_all,
        out_shape=[jax.ShapeDtypeStruct(b.shape, b.dtype) for b in bufs],
        input_output_aliases={i: i for i in range(n_all)},
        scratch_shapes=[pltpu.SemaphoreType.DMA((nb + 7,)), pltpu.SemaphoreType.DMA((nb + 7,))],
    )(*bufs)


class _GradReducer:
    def __init__(self, c_arr, chip_arr):
        self.c_arr, self.chip_arr = c_arr, chip_arr
        self.full, self.pairs, self.landed = {}, {}, {}

    def swap(self, names, grads):
        for n, g in zip(names, grads):
            self.full[n] = g.reshape(N_CHIPS, 2, g.shape[0] // (2 * N_CHIPS), g.shape[1])
        return _halves_exchange([self.full[n] for n in names])

    def swapped(self, names, bufs):
        for n, t in zip(names, bufs):
            self.pairs[n] = _pair_sum(self.full[n], t, self.c_arr, name="grad_pair_sum_" + n)

    def scatter(self, names):
        return _scatter_exchange([self.pairs[n][1] for n in names])

    def collect(self, names, bufs):
        self.landed.update(zip(names, bufs))

    def swap_now(self, names, grads):
        self.swapped(names, _run_exchange(self.swap(names, grads), name="grad_exchange_" + names[0]))

    def finish(self, names, grads, order):
        self.swap_now(names, grads)
        self.collect(names, _run_exchange(self.scatter(names), name="grad_scatter_" + names[0]))
        totals = [_owner_sum(self.pairs[n][0], self.landed[n], self.chip_arr, self.c_arr, replicated=(n == order[-1]),
                             name="grad_owner_sum_" + n) for n in order]
        return _share_reduced(totals)


def _pack_small(vals):
    flat = jnp.concatenate([vals[name].reshape(-1) for name, _ in SMALL])
    return jnp.pad(flat, (0, N_CHIPS * SMALL_ROWS * 1024 - SMALL_ELEMS)).reshape(N_CHIPS * SMALL_ROWS, 1024)


def _unpack_small(buf):
    flat = buf.reshape(-1)
    out, off = {}, 0
    for name, shape in SMALL:
        n = int(np.prod(shape))
        out[name] = flat[off:off + n].reshape(shape)
        off += n
    return out


EARLY_REDUCED = (("w_down",), ("w_up",), ("w_o", "w_ssm_br", "w_attn_br", "w_mem_br", "w_glu", "w_mem_kv"), ("w_in",))


def _device_step(x, mem, tgt, w, p, *, shards, reducer):
    rows = x.shape[0]
    w = dict(w)
    early = EARLY_REDUCED
    gb = {}
    gather_pending = shards is not None

    def riding(*stages):
        if reducer is None or not stages:
            return None
        return _join_exchanges([reducer.swap(names, [gb[n] for n in names]) if kind == "swap" else reducer.scatter(names)
                                for kind, names in stages])

    def arrived(stages, res):
        if reducer is None or not stages:
            return res
        main, bufs = res
        for kind, names in stages:
            (reducer.swapped if kind == "swap" else reducer.collect)(names, bufs[:len(names)])
            bufs = bufs[len(names):]
        return main

    def fetching(names):
        return _gather_exchange([shards[n] for n in names]) if gather_pending else None

    def fetched(names, res):
        if not gather_pending:
            return res
        w.update(zip(names, res[1]))
        return res[0]

    first_use = (("w_glu", "w_ssm_br", "w_attn_br", "w_mem_kv", "w_mem_br", "w_o"), ("w_up",), ("w_down",))
    g1, gm, g2 = p["norm1_g"], p["mem_norm_g"], p["norm2_g"]
    gf = p["final_g"].reshape(1, D_MODEL)
    ssm_args = (p["ssm_lambda_re"][0], p["ssm_lambda_im"][0], p["ssm_log_dt"][0], p["ssm_b_re"][0],
                p["ssm_b_im"][0], p["ssm_c_re"][0], p["ssm_c_im"][0])
    (a_lay, b_blk, c_blk), ssm_vjp = jax.vjp(_ssm_matrices, *ssm_args)
    a_conj = a_lay * _to_scan_layout(jnp.stack([jnp.ones((N_STATES,), F32), -jnp.ones((N_STATES,), F32)]))[None, :]
    dd = p["ssm_d"].reshape(1, SSM_WIDTH)
    win_t = w["w_in"]
    mm = _matmul

    n1 = _rmsnorm_fwd(x, g1, tm=512, name="norm1")
    u = mm(n1, win_t, m=rows, n=512, k=1024, tb=True, tm=2048, tn=512, tk=1024, out_dtypes=(F32,), name="in_u")
    qkv = fetched(first_use[0], mm(n1, win_t, m=rows, n=2304, k=1024, tb=True, tm=2048, tn=256, tk=1024,
                                   b_off=(OFF_QKV // 256, 0), out_dtypes=(F32,), carry=fetching(first_use[0]), name="in_qkv"))
    mq = mm(n1, win_t, m=rows, n=512, k=1024, tb=True, tm=2048, tn=256, tk=1024, b_off=(OFF_MQ // 256, 0),
            out_dtypes=(F32,), name="in_mq")
    zg = fetched(first_use[1], mm(n1, win_t, m=rows, n=3072, k=1024, tb=True, tm=2048, tn=256, tk=1024,
                                  b_off=(OFF_ZG // 256, 0), out_dtypes=(F32,), carry=fetching(first_use[1]), name="in_zg"))

    u_i = _interleave(u)
    ends = _ssm_ends(a_lay, u_i, b_blk, transpose=False, reverse=False, tt=512, name="ssm_fwd_ends")
    s, ys_i, s_entry = _ssm_fwd(a_lay, u_i, b_blk, c_blk, ends, tt=512, name="ssm_fwd")
    ys = _deinterleave(ys_i)
    y0, tglu, y2 = _glu_fwd(ys, u, dd, w["w_glu"], p["b_glu"], tm=512, name="glu_fwd")

    outs, lses = [], []
    for g, (_, d) in enumerate(ATTN_PATTERNS):
        o_g, lse_g = _attn_fwd(qkv, g, d, name=f"attn_fwd_{g}")
        outs.append(o_g)
        lses.append(lse_g)
    o, lse = _attn_merge(outs, lses, tm=1024, name="attn_merge")

    mn = _rmsnorm_fwd(mem, gm, tm=MEM_LEN, name="mem_norm")
    kv = mm(mn, w["w_mem_kv"], m=MEM_LEN, n=1024, k=1024, tm=MEM_LEN, tn=1024, tk=1024, out_dtypes=(F32,), name="mem_kv")
    mo = _mem_attn_fwd(mq, kv, tq=1024, name="mem_attn_fwd")

    branch_acts = (y2, o, mo)
    branch_wts = (w["w_ssm_br"], w["w_attn_br"], w["w_mem_br"])
    merged = _branch_merge_fwd(branch_acts, branch_wts, zg, p["b_gate"], tm=256, name="branch_merge_fwd")
    add = lambda acc, r: (acc + r,)
    h1, n2 = mm(merged, w["w_o"], m=rows, n=1024, k=1024, tm=1024, tn=1024, tk=1024, out_dtypes=(F32, BF16),
                aux=((x, "mn"), (g2, "row")), epilogue=_residual_norm_epilogue, name="out_proj")
    relu2 = lambda acc: (jnp.square(jnp.maximum(acc, 0.0)),)
    act = fetched(first_use[2], mm(n2, w["w_up"], m=rows, n=D_FF, k=1024, tb=True, tm=1024, tn=1024, tk=1024,
                                   out_dtypes=(BF16,), epilogue=relu2, carry=fetching(first_use[2]), name="mlp_up"))
    h2 = mm(act, w["w_down"], m=rows, n=1024, k=D_FF, tm=1024, tn=1024, tk=1024, out_dtypes=(F32,),
            aux=((h1, "mn"),), epilogue=add, name="mlp_down")
    dh2, loss, d_gf = _loss_head(h2, tgt, gf, tm=512, name="loss_head")

    gs = {"final_g": d_gf.reshape(D_MODEL)}
    drelu2 = lambda acc, actv: (acc * (2.0 * jnp.sqrt(actv.astype(F32))),)
    dup = mm(dh2, w["w_down"], m=rows, n=D_FF, k=1024, tb=True, tm=1024, tn=2048, tk=1024, out_dtypes=(BF16,),
             aux=((act, "mn"),), epilogue=drelu2, name="d_act")
    gb["w_down"] = mm(act, dh2, m=D_FF, n=1024, k=rows, ta=True, tm=1024, tn=1024, tk=1024, out_dtypes=(F32,), name="dw_down")
    stages = (("swap", early[0]),)
    gb["w_up"] = arrived(stages, mm(dup, n2, m=D_FF, n=1024, k=rows, ta=True, tm=1024, tn=1024, tk=1024,
                                    out_dtypes=(F32,), carry=riding(*stages), name="dw_up"))
    stages = (("scatter", early[0]), ("swap", early[1]))
    dh1, gs["norm2_g"] = arrived(stages, mm(dup, w["w_up"], m=rows, n=1024, k=D_FF, tm=1024, tn=1024, tk=1024,
                                            out_dtypes=(F32,), aux=((h1, "mn"), (dh2, "mn"), (g2, "row")),
                                            epilogue=_rmsnorm_bwd_epilogue, n_sums=1, carry=riding(*stages), name="d_n2"))
    dmerged = mm(dh1, w["w_o"], m=rows, n=1024, k=1024, tb=True, tm=1024, tn=1024, tk=1024, out_dtypes=(F32,), name="d_merged")
    gb["w_o"] = mm(merged, dh1, m=1024, n=1024, k=rows, ta=True, tm=1024, tn=1024, tk=1024, out_dtypes=(F32,), name="dw_o")
    stages = (("scatter", early[1]),)
    (dy2, do, dmo, gb["w_ssm_br"], gb["w_attn_br"], gb["w_mem_br"], dzg, gs["b_gate"]) = arrived(stages, _branch_merge_bwd(
        dmerged, branch_acts, branch_wts, zg, p["b_gate"], tm=256, carry=riding(*stages), name="branch_merge_bwd"))

    dy0, dt, y1, gs["b_glu"], d_dd = _glu_bwd(dy2, y0, tglu, u, w["w_glu"], tm=512, name="glu_bwd")
    gs["ssm_d"] = d_dd.reshape(1, SSM_GROUPS, SSM_GROUP_SIZE)
    gb["w_glu"] = mm(y1, dt, m=512, n=512, k=rows, ta=True, tm=512, tn=512, tk=1024, out_dtypes=(F32,), name="dw_glu")
    dy0_i = _interleave(dy0)
    lam_ends = _ssm_ends(a_conj, dy0_i, c_blk, transpose=True, reverse=True, tt=512, name="ssm_bwd_ends")
    du_i, d_b_blk, d_c_blk, d_a_lay = _ssm_bwd(a_conj, dy0_i, u_i, s, s_entry, b_blk, c_blk, dd, lam_ends, tt=512,
                                                name="ssm_bwd")
    du = _deinterleave(du_i)
    d_ssm = ssm_vjp((d_a_lay, d_b_blk, d_c_blk))
    for name, val in zip(("ssm_lambda_re", "ssm_lambda_im", "ssm_log_dt", "ssm_b_re", "ssm_b_im", "ssm_c_re", "ssm_c_im"), d_ssm):
        gs[name] = val[None]

    dqkv = None
    for g, (_, d) in enumerate(ATTN_PATTERNS):
        dqkv = _attn_bwd(qkv, do, o, lse, g, d, dqkv, name=f"attn_bwd_{g}")

    dmq, dmk, dmv = _mem_attn_bwd(mq, kv, dmo, tq=1024, name="mem_attn_bwd")
    dkv = jnp.concatenate([dmk, dmv], axis=1)
    gb["w_mem_kv"] = mm(mn, dkv, m=1024, n=1024, k=MEM_LEN, ta=True, tm=1024, tn=1024, tk=MEM_LEN, out_dtypes=(F32,), name="dw_mem_kv")
    dmn = mm(dkv, w["w_mem_kv"], m=MEM_LEN, n=1024, k=1024, tb=True, tm=MEM_LEN, tn=1024, tk=1024, out_dtypes=(F32,), name="d_mn")
    _, gs["mem_norm_g"] = _rmsnorm_bwd(mem, gm, dmn, None, tm=MEM_LEN, name="mem_norm_bwd")

    pieces = ((du, OFF_U, "u"), (dqkv[0], OFF_QKV, "q"), (dqkv[1], OFF_QKV + 768, "k"), (dqkv[2], OFF_QKV + 1536, "v"),
              (dmq, OFF_MQ, "mq"), (dzg, OFF_ZG, "zg"))
    dw_rows = []
    for piece, off, tag in pieces:
        width = piece.shape[1]
        tmw = 1024 if width % 1024 == 0 else (768 if width == 768 else 512)
        stages = {"q": (("swap", early[2]),), "zg": (("scatter", early[2]),)}.get(tag, ())
        dw_rows.append(arrived(stages, mm(piece, n1, m=width, n=1024, k=rows, ta=True, tm=tmw, tn=1024, tk=1024,
                                          out_dtypes=(F32,), carry=riding(*stages), name="dw_in_" + tag)))
    gb["w_in"] = jnp.concatenate(dw_rows, axis=0)
    if reducer is not None:
        reducer.swap_now(early[3], [gb["w_in"]])
    stages = (("scatter", early[3]),)
    dx, gs["norm1_g"] = arrived(stages, _sum_matmul(
        [piece for piece, _, _ in pieces], win_t, [off for _, off, _ in pieces], tm=512,
        aux=((x, "mn"), (dh1, "mn"), (g1, "row")), epilogue=_rmsnorm_bwd_epilogue, n_sums=1,
        carry=riding(*stages), name="d_n1"))
    return loss, dx, gb, gs


def kernel(x, mem, norm1_g, mem_norm_g, w_in, b_gate, ssm_lambda_re, ssm_lambda_im, ssm_log_dt, ssm_b_re, ssm_b_im, ssm_c_re, ssm_c_im, ssm_d, w_glu, b_glu, w_ssm_br, w_attn_br, w_mem_kv, w_mem_br, w_o, norm2_g, w_up, w_down, final_g, loss_target, m_norm1_g, m_mem_norm_g, m_w_in, m_b_gate, m_ssm_lambda_re, m_ssm_lambda_im, m_ssm_log_dt, m_ssm_b_re, m_ssm_b_im, m_ssm_c_re, m_ssm_c_im, m_ssm_d, m_w_glu, m_b_glu, m_w_ssm_br, m_w_attn_br, m_w_mem_kv, m_w_mem_br, m_w_o, m_norm2_g, m_w_up, m_w_down, m_final_g, v_norm1_g, v_mem_norm_g, v_w_in, v_b_gate, v_ssm_lambda_re, v_ssm_lambda_im, v_ssm_log_dt, v_ssm_b_re, v_ssm_b_im, v_ssm_c_re, v_ssm_c_im, v_ssm_d, v_w_glu, v_b_glu, v_w_ssm_br, v_w_attn_br, v_w_mem_kv, v_w_mem_br, v_w_o, v_norm2_g, v_w_up, v_w_down, v_final_g):
    env = dict(locals())
    weights = {n: env[n] for n in WEIGHT_ORDER}
    moms = {n: env["m_" + n] for n in WEIGHT_ORDER}
    vels = {n: env["v_" + n] for n in WEIGHT_ORDER}
    def shard2d(a):
        return a.reshape(a.shape[-2], a.shape[-1])

    chip = 2 * lax.axis_index("x") + lax.axis_index("y")
    wire = [shard2d(weights[n]).astype(BF16) for n, _, _ in BIG]
    wire = dict(zip([n for n, _, _ in BIG], [s.T if tr else s for s, (_, tr, _) in zip(wire, BIG)]))
    w_in_full = _run_exchange(_gather_exchange([wire.pop("w_in")]), name="all_gather_w_in")[0]
    small = {n: weights[n] for n, _ in SMALL}

    reducer = _GradReducer(lax.axis_index("c").astype(jnp.int32).reshape(1), chip.astype(jnp.int32).reshape(1))
    loss, dx, gb, gs = _device_step(x[0], mem[0], loss_target[0], {"w_in": w_in_full}, small, shards=wire, reducer=reducer)
    *shards, small_grad = reducer.finish(["small"], [_pack_small(gs)], [n for n, _, _ in BIG] + ["small"])
    grads = {}
    for (n, tr, _), sh in zip(BIG, shards):
        sh = sh.reshape(2 * sh.shape[1], sh.shape[2])
        grads[n] = sh.T if tr else sh
    small_grad = small_grad.reshape(N_CHIPS * SMALL_ROWS, 1024)
    grads_small = _unpack_small(small_grad)

    delta, new_m, new_v = {}, {}, {}
    for n, _, _ in BIG:
        grads[n], delta[n], new_m[n], new_v[n] = _adamw(weights[n], grads[n], moms[n], vels[n],
                                                        tr=min(weights[n].shape[-2], 256), name="adamw_" + n)
    _, ds_, ms_, vs_ = _adamw(_pack_small(small), small_grad,
                              _pack_small({n: moms[n] for n, _ in SMALL}), _pack_small({n: vels[n] for n, _ in SMALL}),
                              tr=N_CHIPS * SMALL_ROWS, name="adamw_small")
    for dst, buf in ((delta, ds_), (new_m, ms_), (new_v, vs_)):
        dst.update(_unpack_small(buf))
    grads.update(grads_small)

    total_loss = lax.psum(loss[0, 0], ("x", "y", "c"))
    return (total_loss, dx[None], *[grads[n] for n in WEIGHT_ORDER], *[delta[n] for n in WEIGHT_ORDER],
            *[new_m[n] for n in WEIGHT_ORDER], *[new_v[n] for n in WEIGHT_ORDER])
```

```python
import functools
import math

import numpy as np
import jax
import jax.numpy as jnp
from jax import lax
from jax.experimental import pallas as pl
from jax.experimental.pallas import tpu as pltpu

F32 = jnp.float32
BF16 = jnp.bfloat16

D_MODEL = 1024
SSM_GROUPS = 32
SSM_GROUP_SIZE = 16
SSM_STATE = 64
SSM_WIDTH = 512
N_STATES = SSM_GROUPS * SSM_STATE
SCAN_CB = 1024
ATTN_PATTERNS = ((128, 1), (512, 4), (2048, 16))
ATTN_HEAD_DIM = 64
ATTN_Q = 128
MEM_LEN = 256
MEM_HEAD_DIM = 128
MEM_HEADS = 4
D_FF = 4096
OFF_U, OFF_QKV, OFF_MQ, OFF_ZG = 0, 512, 2816, 3328
IN_WIDTH = 6400
RMS_EPS = 1e-6
NEG_INF = -1e30
ADAM_LR, ADAM_B1, ADAM_B2, ADAM_EPS, ADAM_WD, ADAM_STEP = 0.001, 0.9, 0.999, 1e-08, 0.01, 10

VMEM_LIMIT_BYTES = 48 * 1024 * 1024
VMEM_LIMIT_WIDE_BYTES = 56 * 1024 * 1024
LANES = 128
MESH = pl.DeviceIdType.MESH
N_CHIPS = 4

SCAN_SEGS = 8
SCAN_GROUPS = SCAN_CB // SSM_STATE

BIG = (("w_in", True, (6400, 1024)), ("w_glu", False, (512, 512)), ("w_ssm_br", True, (1024, 512)),
       ("w_attn_br", True, (1024, 256)), ("w_mem_kv", False, (1024, 1024)), ("w_mem_br", True, (1024, 512)),
       ("w_o", False, (1024, 1024)), ("w_up", True, (4096, 1024)), ("w_down", False, (4096, 1024)))
SMALL = (("norm1_g", (1, 1024)), ("mem_norm_g", (1, 1024)), ("b_gate", (1, 3072)),
         ("ssm_lambda_re", (1, 32, 64)), ("ssm_lambda_im", (1, 32, 64)), ("ssm_log_dt", (1, 32)),
         ("ssm_b_re", (1, 32, 64, 16)), ("ssm_b_im", (1, 32, 64, 16)), ("ssm_c_re", (1, 32, 16, 64)),
         ("ssm_c_im", (1, 32, 16, 64)), ("ssm_d", (1, 32, 16)), ("b_glu", (1, 512)),
         ("norm2_g", (1, 1024)), ("final_g", (1024,)))
WEIGHT_ORDER = ("norm1_g", "mem_norm_g", "w_in", "b_gate", "ssm_lambda_re", "ssm_lambda_im", "ssm_log_dt",
                "ssm_b_re", "ssm_b_im", "ssm_c_re", "ssm_c_im", "ssm_d", "w_glu", "b_glu", "w_ssm_br",
                "w_attn_br", "w_mem_kv", "w_mem_br", "w_o", "norm2_g", "w_up", "w_down", "final_g")
SMALL_ELEMS = sum(int(np.prod(s)) for _, s in SMALL)
SMALL_ROWS = 64


def _params(sem, vmem=VMEM_LIMIT_BYTES):
    return pltpu.CompilerParams(dimension_semantics=sem, vmem_limit_bytes=vmem)


def _sigmoid(v):
    return 1.0 / (1.0 + jnp.exp(-v))


_GELU_C = math.sqrt(2.0 / math.pi)


def _gelu(v):
    return 0.5 * v * (1.0 + jnp.tanh(_GELU_C * (v + 0.044715 * v * v * v)))


def _gelu_grad(v):
    th = jnp.tanh(_GELU_C * (v + 0.044715 * v * v * v))
    return 0.5 * (1.0 + th) + 0.5 * v * (1.0 - th * th) * _GELU_C * (1.0 + 3.0 * 0.044715 * v * v)


def _dot(a, b, ca, cb):
    return lax.dot_general(a, b, (((ca,), (cb,)), ((), ())), preferred_element_type=F32)


class _Exchange:
    def __init__(self, ins, outs, aliases, sems, start, finish):
        self.ins, self.outs, self.aliases, self.sems, self.start, self.finish = ins, outs, aliases, sems, start, finish


def _matmul(a, b, *, m, n, k, ta=False, tb=False, tm, tn, tk, out_dtypes, name,
            a_off=(0, 0), b_off=(0, 0), aux=(), epilogue=None, n_sums=0, carry=None):
    assert m % tm == 0 and n % tn == 0 and k % tk == 0, (name, m, n, k, tm, tn, tk)
    nk = k // tk
    n_aux = len(aux)
    n_tiles = len(out_dtypes)
    n_out = n_tiles + n_sums
    ar, ac = a_off
    br, bc = b_off
    if ta:
        a_spec = pl.BlockSpec((tk, tm), lambda i, j, kk: (kk + ar, i + ac))
    else:
        a_spec = pl.BlockSpec((tm, tk), lambda i, j, kk: (i + ar, kk + ac))
    if tb:
        b_spec = pl.BlockSpec((tn, tk), lambda i, j, kk: (j + br, kk + bc))
    else:
        b_spec = pl.BlockSpec((tk, tn), lambda i, j, kk: (kk + br, j + bc))
    aux_specs = []
    for _, kind in aux:
        if kind == "mn":
            aux_specs.append(pl.BlockSpec((tm, tn), lambda i, j, kk: (i, j)))
        else:
            aux_specs.append(pl.BlockSpec((1, tn), lambda i, j, kk: (0, j)))
    ca = 0 if ta else 1
    cb = 1 if tb else 0

    def finish(acc, aux_refs, out_refs, row_tile):
        outs = (acc,) if epilogue is None else epilogue(acc, *[r[...] for r in aux_refs])
        for o_ref, o in zip(out_refs[:n_tiles], outs[:n_tiles]):
            o_ref[...] = o.astype(o_ref.dtype)
        _accumulate_over_rows(out_refs[n_tiles:], outs[n_tiles:], row_tile)

    def body(a_ref, b_ref, *rest):
        aux_refs = rest[:n_aux]
        out_refs = rest[n_aux:n_aux + n_out]
        row_tile = pl.program_id(0)
        prod = _dot(a_ref[...].astype(BF16), b_ref[...].astype(BF16), ca, cb)
        if nk == 1:
            finish(prod, aux_refs, out_refs, row_tile)
            return
        acc_ref = rest[n_aux + n_out]
        kk = pl.program_id(2)

        @pl.when(kk == 0)
        def _():
            acc_ref[...] = prod

        @pl.when(jnp.logical_and(kk > 0, kk < nk - 1))
        def _():
            acc_ref[...] += prod

        @pl.when(kk == nk - 1)
        def _():
            finish(acc_ref[...] + prod, aux_refs, out_refs, row_tile)

    tile = pl.BlockSpec((tm, tn), lambda i, j, kk: (i, j))
    col_sum = pl.BlockSpec((1, tn), lambda i, j, kk: (0, j))
    res = _call_with_carry(
        body, carry, name=name, grid=(m // tm, n // tn, nk), in_specs=[a_spec, b_spec] + aux_specs,
        out_specs=[tile] * n_tiles + [col_sum] * n_sums,
        out_shape=[jax.ShapeDtypeStruct((m, n), dt) for dt in out_dtypes] + [jax.ShapeDtypeStruct((1, n), F32)] * n_sums,
        scratch=[pltpu.VMEM((tm, tn), F32)] if nk > 1 else [], operands=[a, b] + [x for x, _ in aux],
        semantics=("arbitrary" if n_sums else "parallel", "parallel", "arbitrary"))
    main = res[0] if n_out == 1 else tuple(res[:n_out])
    return main if carry is None else (main, list(res[n_out:]))


def _accumulate_over_rows(sum_refs, terms, row_tile):
    for s_ref, term in zip(sum_refs, terms):
        @pl.when(row_tile == 0)
        def _():
            s_ref[...] = term

        @pl.when(row_tile > 0)
        def _():
            s_ref[...] += term


def _call_with_carry(body, carry, *, name, grid, in_specs, out_specs, out_shape, scratch, operands, semantics,
                     vmem=VMEM_LIMIT_BYTES):
    if carry is None:
        return pl.pallas_call(body, name=name, grid=grid, in_specs=in_specs, out_specs=out_specs, out_shape=out_shape,
                              scratch_shapes=scratch, compiler_params=_params(semantics, vmem))(*operands)
    n_in, n_cin, n_out, n_cout, n_scr = len(operands), len(carry.ins), len(out_shape), len(carry.outs), len(scratch)

    def hosted(*refs):
        main_in, c_in = refs[:n_in], refs[n_in:n_in + n_cin]
        main_out = refs[n_in + n_cin:n_in + n_cin + n_out]
        c_out = refs[n_in + n_cin + n_out:n_in + n_cin + n_out + n_cout]
        rest = refs[n_in + n_cin + n_out + n_cout:]
        ids = [pl.program_id(t) for t in range(len(grid))]
        first = functools.reduce(jnp.logical_and, [i == 0 for i in ids])
        last = functools.reduce(jnp.logical_and, [i == g - 1 for i, g in zip(ids, grid)])

        @pl.when(first)
        def _():
            carry.start(c_in, c_out, *rest[n_scr:])

        body(*main_in, *main_out, *rest[:n_scr])

        @pl.when(last)
        def _():
            carry.finish(c_in, c_out, *rest[n_scr:])

    return pl.pallas_call(
        hosted, name=name, grid=grid,
        in_specs=list(in_specs) + [ANY] * n_cin, out_specs=list(out_specs) + [ANY] * n_cout,
        out_shape=list(out_shape) + list(carry.outs),
        input_output_aliases={n_in + i: n_out + o for i, o in carry.aliases.items()},
        scratch_shapes=list(scratch) + [pltpu.SemaphoreType.DMA(s) for s in carry.sems],
        compiler_params=_params(("arbitrary",) * len(grid), vmem),
    )(*operands, *carry.ins)


def _sum_matmul(pieces, b, offs, *, tm, name, aux=(), epilogue=None, n_sums=0, carry=None):
    m = pieces[0].shape[0]
    n = b.shape[1]
    npieces, n_aux = len(pieces), len(aux)

    def body(*refs):
        b_ref = refs[npieces]
        aux_refs = refs[npieces + 1:npieces + 1 + n_aux]
        out_refs = refs[npieces + 1 + n_aux:]
        acc = None
        for p_ref, off in zip(refs[:npieces], offs):
            part = _dot(p_ref[...].astype(BF16), b_ref[pl.ds(off, p_ref.shape[1]), :], 1, 0)
            acc = part if acc is None else acc + part
        outs = (acc,) if epilogue is None else epilogue(acc, *[r[...] for r in aux_refs])
        out_refs[0][...] = outs[0]
        _accumulate_over_rows(out_refs[1:], outs[1:], pl.program_id(0))

    row = pl.BlockSpec((tm, n), lambda i: (i, 0))
    vec = pl.BlockSpec((1, n), lambda i: (0, 0))
    res = _call_with_carry(
        body, carry, name=name, grid=(m // tm,),
        in_specs=[pl.BlockSpec((tm, p.shape[1]), lambda i: (i, 0)) for p in pieces] + [_resident(b.shape)]
        + [row if kind == "mn" else vec for _, kind in aux],
        out_specs=[row] + [vec] * n_sums,
        out_shape=[jax.ShapeDtypeStruct((m, n), F32)] + [jax.ShapeDtypeStruct((1, n), F32)] * n_sums,
        scratch=[], operands=list(pieces) + [b] + [x for x, _ in aux], semantics=("arbitrary" if n_sums else "parallel",),
        vmem=VMEM_LIMIT_WIDE_BYTES)
    main = res[0] if n_sums == 0 else tuple(res[:1 + n_sums])
    return main if carry is None else (main, list(res[1 + n_sums:]))


def _rmsnorm_fwd(x, g, *, tm, name):
    rows, d = x.shape

    def body(x_ref, g_ref, o_ref):
        xv = x_ref[...]
        r = lax.rsqrt(jnp.mean(xv * xv, axis=-1, keepdims=True) + RMS_EPS)
        o_ref[...] = (xv * r * g_ref[...]).astype(o_ref.dtype)

    return pl.pallas_call(
        body, name=name, grid=(rows // tm,),
        in_specs=[pl.BlockSpec((tm, d), lambda i: (i, 0)), pl.BlockSpec((1, d), lambda i: (0, 0))],
        out_specs=pl.BlockSpec((tm, d), lambda i: (i, 0)),
        out_shape=jax.ShapeDtypeStruct((rows, d), BF16),
        compiler_params=_params(("parallel",)),
    )(x, g)


def _residual_norm_epilogue(acc, xv, gv):
    h = acc + xv
    r = lax.rsqrt(jnp.mean(h * h, axis=-1, keepdims=True) + RMS_EPS)
    return h, h * r * gv


def _rmsnorm_bwd_epilogue(dy, xv, resv, gv):
    r = lax.rsqrt(jnp.mean(xv * xv, axis=-1, keepdims=True) + RMS_EPS)
    xhat = xv * r
    dyg = dy * gv
    dx = r * (dyg - xhat * jnp.mean(dyg * xhat, axis=-1, keepdims=True)) + resv
    return dx, jnp.sum(dy * xhat, axis=0, keepdims=True)


def _rmsnorm_bwd(x, g, dy, res, *, tm, name):
    rows, d = x.shape
    has_res = res is not None

    def body(x_ref, g_ref, dy_ref, *rest):
        if has_res:
            res_ref, dx_ref, dg_ref = rest
        else:
            dx_ref, dg_ref = rest
        i = pl.program_id(0)
        xv = x_ref[...]
        r = lax.rsqrt(jnp.mean(xv * xv, axis=-1, keepdims=True) + RMS_EPS)
        xhat = xv * r
        dyv = dy_ref[...]
        dyg = dyv * g_ref[...]
        dx = r * (dyg - xhat * jnp.mean(dyg * xhat, axis=-1, keepdims=True))
        if has_res:
            dx = dx + res_ref[...]
        dx_ref[...] = dx

        @pl.when(i == 0)
        def _():
            dg_ref[...] = jnp.zeros_like(dg_ref)

        dg_ref[...] += jnp.sum(dyv * xhat, axis=0, keepdims=True)

    row_spec = pl.BlockSpec((tm, d), lambda i: (i, 0))
    vec_spec = pl.BlockSpec((1, d), lambda i: (0, 0))
    ins = [x, g, dy] + ([res] if has_res else [])
    return pl.pallas_call(
        body, name=name, grid=(rows // tm,),
        in_specs=[row_spec, vec_spec, row_spec] + ([row_spec] if has_res else []),
        out_specs=[row_spec, vec_spec],
        out_shape=[jax.ShapeDtypeStruct((rows, d), F32), jax.ShapeDtypeStruct((1, d), F32)],
        compiler_params=_params(("arbitrary",)),
    )(*ins)


def _loss_head_epilogue(acc, hv, tgtv, gv):
    xv = acc + hv
    r = lax.rsqrt(jnp.mean(xv * xv, axis=-1, keepdims=True) + RMS_EPS)
    xhat = xv * r
    err = xhat * gv - tgtv
    dyv = err * (1.0 / D_MODEL)
    dyg = dyv * gv
    dh = r * (dyg - xhat * jnp.mean(dyg * xhat, axis=-1, keepdims=True))
    return dh, jnp.sum(dyv * xhat, axis=0, keepdims=True), jnp.sum(err * err, axis=0, keepdims=True)


def _to_scan_layout(v):
    lead = v.shape[:-2]
    v = v.reshape(lead + (2, N_STATES // SCAN_CB, SCAN_CB))
    v = jnp.swapaxes(v, -3, -2)
    return v.reshape(lead + (2 * N_STATES,))


def _ssm_matrices(lam_re, lam_im, log_dt, b_re, b_im, c_re, c_im):
    dt = jnp.exp(log_dt)[:, None]
    mag = jnp.exp(lam_re * dt)
    a_re, a_im = mag * jnp.cos(lam_im * dt), mag * jnp.sin(lam_im * dt)
    nr, ni = a_re - 1.0, a_im
    den = lam_re * lam_re + lam_im * lam_im
    coef_re = (nr * lam_re + ni * lam_im) / den
    coef_im = (ni * lam_re - nr * lam_im) / den
    bb_re = coef_re[..., None] * b_re - coef_im[..., None] * b_im
    bb_im = coef_re[..., None] * b_im + coef_im[..., None] * b_re
    a_lay = _to_scan_layout(jnp.stack([a_re.reshape(-1), a_im.reshape(-1)], axis=0))[None, :]
    nblk = SSM_GROUPS // SCAN_GROUPS
    eye = jnp.eye(SCAN_GROUPS, dtype=F32)

    def b_block(bb):
        bb = bb.reshape(nblk, SCAN_GROUPS, SSM_STATE, SSM_GROUP_SIZE)
        return jnp.einsum("gk,jkph->jghkp", eye, bb).reshape(nblk, SCAN_GROUPS * SSM_GROUP_SIZE, SCAN_CB)

    b_blk = jnp.concatenate([b_block(bb_re), b_block(bb_im)], axis=2)

    def c_block(cc):
        cc = cc.reshape(nblk, SCAN_GROUPS, SSM_GROUP_SIZE, SSM_STATE)
        return jnp.einsum("gk,jghp->jkpgh", eye, cc).reshape(nblk, SCAN_CB, SCAN_GROUPS * SSM_GROUP_SIZE)

    c_blk = jnp.concatenate([c_block(c_re), -c_block(c_im)], axis=1)
    return a_lay, b_blk, c_blk


def _interleave(v):
    rows, c = v.shape
    return v.reshape(SCAN_SEGS, rows // SCAN_SEGS, c).transpose(1, 0, 2).reshape(rows, c)


def _deinterleave(v):
    rows, c = v.shape
    return v.reshape(rows // SCAN_SEGS, SCAN_SEGS, c).transpose(1, 0, 2).reshape(rows, c)


def _scan_groups(a_ref, bu_ref, o_ref, state, *, reverse, tt):
    cb = SCAN_CB
    ar = jnp.broadcast_to(a_ref[:, :cb], (SCAN_SEGS, cb))
    ai = jnp.broadcast_to(a_ref[:, cb:], (SCAN_SEGS, cb))
    ngroups = tt // SCAN_SEGS

    def step(i, st):
        sr, si = st
        r0 = pl.multiple_of(((ngroups - 1 - i) if reverse else i) * SCAN_SEGS, SCAN_SEGS)
        blk = bu_ref[pl.ds(r0, SCAN_SEGS), :]
        nr = ar * sr - ai * si + blk[:, :cb]
        ni = ar * si + ai * sr + blk[:, cb:]
        if o_ref is not None:
            o_ref[pl.ds(r0, SCAN_SEGS), :] = jnp.concatenate([nr, ni], axis=1)
        return nr, ni

    return lax.fori_loop(0, ngroups, step, state, unroll=4)


def _segment_entries(a_ref, e_ref, init_ref, *, reverse, seg_len):
    cb = SCAN_CB
    n_sq = seg_len.bit_length() - 1
    assert 1 << n_sq == seg_len, seg_len
    pr, pi = a_ref[:, :cb], a_ref[:, cb:]
    for _ in range(n_sq):
        pr, pi = pr * pr - pi * pi, 2.0 * pr * pi
    cr = jnp.zeros((1, cb), F32)
    ci = jnp.zeros((1, cb), F32)
    order = range(SCAN_SEGS - 1, -1, -1) if reverse else range(SCAN_SEGS)
    for k, seg in enumerate(order):
        if k > 0:
            prev = seg + 1 if reverse else seg - 1
            er, ei = e_ref[prev:prev + 1, :cb], e_ref[prev:prev + 1, cb:]
            cr, ci = pr * cr - pi * ci + er, pr * ci + pi * cr + ei
        init_ref[seg:seg + 1, :] = jnp.concatenate([cr, ci], axis=1)


def _ssm_specs(nt, tt, nch, reverse):
    cb = SCAN_CB
    tmap = (lambda j, kk: (nt - 1 - kk, j)) if reverse else (lambda j, kk: (kk, j))
    return dict(a=pl.BlockSpec((1, 2 * cb), lambda j, kk: (0, j)),
                seg=pl.BlockSpec((SCAN_SEGS, 2 * cb), lambda j, kk: (0, j)),
                chan=pl.BlockSpec((tt, nch), tmap),
                state=pl.BlockSpec((tt, 2 * cb), tmap),
                b=pl.BlockSpec((None, nch, 2 * cb), lambda j, kk: (j, 0, 0)),
                c=pl.BlockSpec((None, 2 * cb, nch), lambda j, kk: (j, 0, 0)))


def _ssm_ends(a_lay, x, blocks, *, transpose, reverse, tt, name):
    rows = x.shape[0]
    nblk = blocks.shape[0]
    nch = x.shape[1] // nblk
    cb = SCAN_CB
    nt = rows // tt
    sp = _ssm_specs(nt, tt, nch, reverse)

    def body(a_ref, x_ref, w_ref, e_ref, bu_ref):
        kk = pl.program_id(1)

        @pl.when(kk == 0)
        def _():
            e_ref[...] = jnp.zeros_like(e_ref)

        bu_ref[...] = _dot(x_ref[...].astype(BF16), w_ref[...].astype(BF16), 1, 1 if transpose else 0)
        sr, si = _scan_groups(a_ref, bu_ref, None, (e_ref[:, :cb], e_ref[:, cb:]), reverse=reverse, tt=tt)
        e_ref[...] = jnp.concatenate([sr, si], axis=1)

    return pl.pallas_call(
        body, name=name, grid=(nblk, nt),
        in_specs=[sp["a"], sp["chan"], sp["c"] if transpose else sp["b"]],
        out_specs=sp["seg"],
        out_shape=jax.ShapeDtypeStruct((SCAN_SEGS, nblk * 2 * cb), F32),
        scratch_shapes=[pltpu.VMEM((tt, 2 * cb), F32)],
        compiler_params=_params(("parallel", "arbitrary")),
    )(a_lay, x, blocks)


def _ssm_fwd(a_lay, u, b_blk, c_blk, ends, *, tt, name):
    rows = u.shape[0]
    nblk = b_blk.shape[0]
    nch = u.shape[1] // nblk
    cb = SCAN_CB
    nt = rows // tt
    sp = _ssm_specs(nt, tt, nch, False)

    def body(a_ref, e_ref, u_ref, b_ref, c_ref, s_ref, y_ref, init_ref, carry_ref):
        kk = pl.program_id(1)

        @pl.when(kk == 0)
        def _():
            _segment_entries(a_ref, e_ref, init_ref, reverse=False, seg_len=rows // SCAN_SEGS)
            carry_ref[...] = init_ref[...]

        s_ref[...] = _dot(u_ref[...].astype(BF16), b_ref[...].astype(BF16), 1, 0)
        sr, si = _scan_groups(a_ref, s_ref, s_ref, (carry_ref[:, :cb], carry_ref[:, cb:]), reverse=False, tt=tt)
        carry_ref[...] = jnp.concatenate([sr, si], axis=1)
        y_ref[...] = _dot(s_ref[...].astype(BF16), c_ref[...].astype(BF16), 1, 0)

    return pl.pallas_call(
        body, name=name, grid=(nblk, nt),
        in_specs=[sp["a"], sp["seg"], sp["chan"], sp["b"], sp["c"]],
        out_specs=[sp["state"], sp["chan"], sp["seg"]],
        out_shape=[jax.ShapeDtypeStruct((rows, nblk * 2 * cb), F32), jax.ShapeDtypeStruct((rows, nblk * nch), F32),
                   jax.ShapeDtypeStruct((SCAN_SEGS, nblk * 2 * cb), F32)],
        scratch_shapes=[pltpu.VMEM((SCAN_SEGS, 2 * cb), F32)],
        compiler_params=_params(("parallel", "arbitrary")),
    )(a_lay, ends, u, b_blk, c_blk)


def _ssm_bwd(a_conj, dy, u, s, s_entry, b_blk, c_blk, dd, ends, *, tt, name):
    rows = u.shape[0]
    nblk = b_blk.shape[0]
    nch = u.shape[1] // nblk
    cb = SCAN_CB
    nt = rows // tt
    sp = _ssm_specs(nt, tt, nch, True)
    groups_per_tile = tt // SCAN_SEGS
    before = pl.BlockSpec((SCAN_SEGS, 2 * cb), lambda j, kk: (jnp.maximum((nt - 1 - kk) * groups_per_tile - 1, 0), j))

    def body(a_ref, e_ref, dy_ref, u_ref, s_ref, before_ref, entry_ref, b_ref, c_ref, dd_ref,
             du_ref, db_ref, dc_ref, da_ref, lam_ref, carry_ref):
        kk = pl.program_id(1)

        @pl.when(kk == 0)
        def _():
            _segment_entries(a_ref, e_ref, carry_ref, reverse=True, seg_len=rows // SCAN_SEGS)
            db_ref[...] = jnp.zeros_like(db_ref)
            dc_ref[...] = jnp.zeros_like(dc_ref)
            da_ref[...] = jnp.zeros_like(da_ref)

        dyv = dy_ref[...]
        dyb = dyv.astype(BF16)
        lam_ref[...] = _dot(dyb, c_ref[...].astype(BF16), 1, 1)
        lr, li = _scan_groups(a_ref, lam_ref, lam_ref, (carry_ref[:, :cb], carry_ref[:, cb:]), reverse=True, tt=tt)
        carry_ref[...] = jnp.concatenate([lr, li], axis=1)

        first = jnp.where(kk == nt - 1, entry_ref[...], before_ref[...])
        rest = tt - SCAN_SEGS
        lam_hi = lam_ref[pl.ds(SCAN_SEGS, rest), :]
        s_lo = s_ref[pl.ds(0, rest), :]
        lam_lo = lam_ref[pl.ds(0, SCAN_SEGS), :]

        def pair(lv, pv):
            lre, lim, pre, pim = lv[:, :cb], lv[:, cb:], pv[:, :cb], pv[:, cb:]
            return (jnp.sum(lre * pre + lim * pim, axis=0, keepdims=True),
                    jnp.sum(lim * pre - lre * pim, axis=0, keepdims=True))

        r1, i1 = pair(lam_hi, s_lo)
        r0, i0 = pair(lam_lo, first)
        da_ref[...] += jnp.concatenate([r1 + r0, i1 + i0], axis=1)

        lamb = lam_ref[...].astype(BF16)
        du_ref[...] = _dot(lamb, b_ref[...].astype(BF16), 1, 1) + dd_ref[...] * dyv
        db_ref[...] += _dot(u_ref[...].astype(BF16), lamb, 0, 0)
        dc_ref[...] += _dot(s_ref[...].astype(BF16), dyb, 0, 0)

    return pl.pallas_call(
        body, name=name, grid=(nblk, nt),
        in_specs=[sp["a"], sp["seg"], sp["chan"], sp["chan"], sp["state"], before, sp["seg"], sp["b"], sp["c"],
                  pl.BlockSpec((1, nch), lambda j, kk: (0, j))],
        out_specs=[sp["chan"], sp["b"], sp["c"], pl.BlockSpec((1, 2 * cb), lambda j, kk: (0, j))],
        out_shape=[jax.ShapeDtypeStruct((rows, nblk * nch), F32), jax.ShapeDtypeStruct(b_blk.shape, F32),
                   jax.ShapeDtypeStruct(c_blk.shape, F32), jax.ShapeDtypeStruct((1, nblk * 2 * cb), F32)],
        scratch_shapes=[pltpu.VMEM((tt, 2 * cb), F32), pltpu.VMEM((SCAN_SEGS, 2 * cb), F32)],
        compiler_params=_params(("parallel", "arbitrary")),
    )(a_conj, ends, dy, u, s, s, s_entry, b_blk, c_blk, dd)


def _glu_fwd(ys, u, dd, w_glu, b_glu, *, tm, name):
    rows, w = ys.shape

    def body(ys_ref, u_ref, dd_ref, w_ref, b_ref, y0_ref, t_ref, y2_ref):
        y0 = ys_ref[...] + dd_ref[...] * u_ref[...]
        y1 = _gelu(y0)
        t = _dot(y1.astype(BF16), w_ref[...], 1, 0) + b_ref[...]
        y0_ref[...] = y0
        t_ref[...] = t
        y2_ref[...] = (y1 * _sigmoid(t)).astype(BF16)

    row = pl.BlockSpec((tm, w), lambda i: (i, 0))
    vec = pl.BlockSpec((1, w), lambda i: (0, 0))
    return pl.pallas_call(
        body, name=name, grid=(rows // tm,),
        in_specs=[row, row, vec, pl.BlockSpec((w, w), lambda i: (0, 0)), vec],
        out_specs=[row, row, row],
        out_shape=[jax.ShapeDtypeStruct((rows, w), F32), jax.ShapeDtypeStruct((rows, w), F32),
                   jax.ShapeDtypeStruct((rows, w), BF16)],
        compiler_params=_params(("parallel",)),
    )(ys, u, dd, w_glu, b_glu)


def _glu_bwd(dy2, y0, t, u, w_glu, *, tm, name):
    rows, w = y0.shape

    def body(dy2_ref, y0_ref, t_ref, u_ref, w_ref, dy0_ref, dt_ref, y1_ref, db_ref, dd_ref):
        i = pl.program_id(0)
        y0 = y0_ref[...]
        y1 = _gelu(y0)
        sg = _sigmoid(t_ref[...])
        dy2v = dy2_ref[...]
        dt = dy2v * y1 * sg * (1.0 - sg)
        dy1 = dy2v * sg + _dot(dt.astype(BF16), w_ref[...], 1, 1)
        dy0 = dy1 * _gelu_grad(y0)
        dy0_ref[...] = dy0
        dt_ref[...] = dt.astype(BF16)
        y1_ref[...] = y1.astype(BF16)

        @pl.when(i == 0)
        def _():
            db_ref[...] = jnp.zeros_like(db_ref)
            dd_ref[...] = jnp.zeros_like(dd_ref)

        db_ref[...] += jnp.sum(dt, axis=0, keepdims=True)
        dd_ref[...] += jnp.sum(dy0 * u_ref[...], axis=0, keepdims=True)

    row = pl.BlockSpec((tm, w), lambda i: (i, 0))
    vec = pl.BlockSpec((1, w), lambda i: (0, 0))
    return pl.pallas_call(
        body, name=name, grid=(rows // tm,),
        in_specs=[row, row, row, row, pl.BlockSpec((w, w), lambda i: (0, 0))],
        out_specs=[row, row, row, vec, vec],
        out_shape=[jax.ShapeDtypeStruct((rows, w), F32), jax.ShapeDtypeStruct((rows, w), BF16),
                   jax.ShapeDtypeStruct((rows, w), BF16), jax.ShapeDtypeStruct((1, w), F32),
                   jax.ShapeDtypeStruct((1, w), F32)],
        compiler_params=_params(("arbitrary",)),
    )(dy2, y0, t, u, w_glu)


ATTN_TILE = 2048


def _attn_geometry(rows, d):
    sb = ATTN_Q * d
    tr = max(sb, min(ATTN_TILE, rows))
    assert rows % tr == 0 and tr % sb == 0, (rows, d)
    return sb, tr, rows // tr, tr // sb


def _attn_masks():
    qi = lax.broadcasted_iota(jnp.int32, (2 * ATTN_Q, 2 * ATTN_Q), 0) % ATTN_Q
    kj = lax.broadcasted_iota(jnp.int32, (2 * ATTN_Q, 2 * ATTN_Q), 1)
    own_ok = jnp.logical_and(kj >= ATTN_Q, kj - ATTN_Q <= qi)
    prev_ok = jnp.logical_and(kj < ATTN_Q, kj >= qi)
    bias_first = jnp.where(own_ok, 0.0, NEG_INF)
    bias_other = jnp.where(jnp.logical_or(own_ok, prev_ok), 0.0, NEG_INF)
    head0 = lax.broadcasted_iota(jnp.int32, (ATTN_Q, LANES), 1) < ATTN_HEAD_DIM
    return bias_first, bias_other, head0


def _attn_rows(base, n, d):
    return pl.ds(pl.multiple_of(base, ATTN_Q), n) if d == 1 else pl.ds(base, n, stride=d)


def _stack_heads(v, head0):
    return jnp.concatenate([jnp.where(head0, v, 0.0), jnp.where(head0, 0.0, v)], axis=0)


def _unstack_heads(v, head0):
    return jnp.where(head0, v[:ATTN_Q], v[ATTN_Q:])


def _fill_keys(buf, prev_ref, cur_ref, sb):
    buf[pl.ds(0, sb), :] = prev_ref[...]
    buf[pl.ds(sb, cur_ref.shape[0]), :] = cur_ref[...]


def _attn_fwd(qkv, g, d, *, name):
    rows = qkv.shape[0]
    sb, tr, ntiles, nsub = _attn_geometry(rows, d)
    qc, kc, vc = 2 * g, 6 + 2 * g, 12 + 2 * g
    scale = ATTN_HEAD_DIM ** -0.5

    def body(q_ref, kc_ref, kp_ref, vc_ref, vp_ref, o_ref, lse_ref, kbuf, vbuf):
        n = pl.program_id(0)
        _fill_keys(kbuf, kp_ref, kc_ref, sb)
        _fill_keys(vbuf, vp_ref, vc_ref, sb)
        bias_first, bias_other, head0 = _attn_masks()

        def per_block(idx, carry):
            j, r = idx // d, idx % d
            base = j * sb + r
            bias = jnp.where(jnp.logical_and(n == 0, j == 0), bias_first, bias_other)
            qrows = _attn_rows(base, ATTN_Q, d)
            krows = _attn_rows(base, 2 * ATTN_Q, d)
            qs = _stack_heads(q_ref[qrows, :], head0).astype(BF16)
            s = _dot(qs, kbuf[krows, :].astype(BF16), 1, 1) * scale + bias
            mx = jnp.max(s, axis=-1, keepdims=True)
            p = jnp.exp(s - mx)
            den = jnp.sum(p, axis=-1, keepdims=True)
            pv = _dot(p.astype(BF16), vbuf[krows, :].astype(BF16), 1, 0) / den
            o_ref[qrows, :] = _unstack_heads(pv, head0)
            lse_ref[qrows, :] = _unstack_heads(jnp.broadcast_to(mx + jnp.log(den), (2 * ATTN_Q, LANES)), head0)
            return carry

        lax.fori_loop(0, nsub * d, per_block, 0, unroll=8)

    def cur(col):
        return pl.BlockSpec((tr, LANES), lambda n, hp: (n, col + hp))

    def prev(col):
        return pl.BlockSpec((sb, LANES), lambda n, hp: (jnp.maximum(n * nsub - 1, 0), col + hp))

    out_spec = pl.BlockSpec((tr, LANES), lambda n, hp: (n, hp))
    return pl.pallas_call(
        body, name=name, grid=(ntiles, 2),
        in_specs=[cur(qc), cur(kc), prev(kc), cur(vc), prev(vc)],
        out_specs=[out_spec, out_spec],
        out_shape=[jax.ShapeDtypeStruct((rows, 2 * LANES), F32), jax.ShapeDtypeStruct((rows, 2 * LANES), F32)],
        scratch_shapes=[pltpu.VMEM((sb + tr, LANES), F32), pltpu.VMEM((sb + tr, LANES), F32)],
        compiler_params=_params(("parallel", "parallel")),
    )(qkv, qkv, qkv, qkv, qkv)


def _attn_merge(outs, lses, *, tm, name):
    rows, w = outs[0].shape

    def body(o0, o1, o2, l0, l1, l2, o_ref, lse_ref):
        a0, a1, a2 = l0[...], l1[...], l2[...]
        mx = jnp.maximum(jnp.maximum(a0, a1), a2)
        e0, e1, e2 = jnp.exp(a0 - mx), jnp.exp(a1 - mx), jnp.exp(a2 - mx)
        den = e0 + e1 + e2
        o_ref[...] = (e0 / den) * o0[...] + (e1 / den) * o1[...] + (e2 / den) * o2[...]
        lse_ref[...] = mx + jnp.log(den)

    row = pl.BlockSpec((tm, w), lambda i: (i, 0))
    return pl.pallas_call(
        body, name=name, grid=(rows // tm,), in_specs=[row] * 6, out_specs=[row, row],
        out_shape=[jax.ShapeDtypeStruct((rows, w), F32), jax.ShapeDtypeStruct((rows, w), F32)],
        compiler_params=_params(("parallel",)),
    )(*outs, *lses)


def _attn_bwd(qkv, do, o, lse, g, d, prev, *, name):
    rows = qkv.shape[0]
    sb, tr, ntiles, nsub = _attn_geometry(rows, d)
    qc, kc, vc = 2 * g, 6 + 2 * g, 12 + 2 * g
    scale = ATTN_HEAD_DIM ** -0.5

    def body(q_ref, kc_ref, kp_ref, vc_ref, vp_ref, do_ref, o_ref, lse_ref, dq_ref, dk_ref, dv_ref,
             kbuf, vbuf, dk_acc, dv_acc):
        n = pl.program_id(1)

        @pl.when(n == 0)
        def _():
            dk_acc[pl.ds(0, tr), :] = jnp.zeros((tr, LANES), F32)
            dv_acc[pl.ds(0, tr), :] = jnp.zeros((tr, LANES), F32)

        @pl.when(n < ntiles)
        def _():
            dk_acc[pl.ds(tr, tr), :] = jnp.zeros((tr, LANES), F32)
            dv_acc[pl.ds(tr, tr), :] = jnp.zeros((tr, LANES), F32)
            _fill_keys(kbuf, kp_ref, kc_ref, sb)
            _fill_keys(vbuf, vp_ref, vc_ref, sb)
            bias_first, bias_other, head0 = _attn_masks()
            lane = lax.broadcasted_iota(jnp.int32, (ATTN_Q, LANES), 1)

            def per_block(idx, carry):
                j, r = idx // d, idx % d
                base = j * sb + r
                bias = jnp.where(jnp.logical_and(n == 0, j == 0), bias_first, bias_other)
                qrows = _attn_rows(base, ATTN_Q, d)
                krows = _attn_rows(base, 2 * ATTN_Q, d)
                arows = _attn_rows(base + (tr - sb), 2 * ATTN_Q, d)
                qs = _stack_heads(q_ref[qrows, :], head0).astype(BF16)
                dos = _stack_heads(do_ref[qrows, :], head0)
                dosb = dos.astype(BF16)
                ov = o_ref[qrows, :]
                delta = jnp.sum(dos * jnp.concatenate([ov, ov], axis=0), axis=-1, keepdims=True)
                lsev = lse_ref[qrows, :]
                lse_s = jnp.concatenate(
                    [jnp.sum(jnp.where(lane == h * ATTN_HEAD_DIM, lsev, 0.0), axis=-1, keepdims=True) for h in range(2)], axis=0)
                kb = kbuf[krows, :].astype(BF16)
                vb = vbuf[krows, :].astype(BF16)
                p = jnp.exp(_dot(qs, kb, 1, 1) * scale + bias - lse_s)
                ds = (p * (_dot(dosb, vb, 1, 1) - delta) * scale).astype(BF16)
                dq_ref[qrows, :] = _unstack_heads(_dot(ds, kb, 1, 0), head0)
                dk_acc[arows, :] += _dot(ds, qs, 0, 0)
                dv_acc[arows, :] += _dot(p.astype(BF16), dosb, 0, 0)
                return carry

            lax.fori_loop(0, nsub * d, per_block, 0, unroll=4)

        dk_ref[...] = dk_acc[pl.ds(0, tr), :]
        dv_ref[...] = dv_acc[pl.ds(0, tr), :]
        dk_acc[pl.ds(0, tr), :] = dk_acc[pl.ds(tr, tr), :]
        dv_acc[pl.ds(0, tr), :] = dv_acc[pl.ds(tr, tr), :]

    def cur(n):
        return jnp.minimum(n, ntiles - 1)

    def spec(col, prev):
        if prev:
            return pl.BlockSpec((sb, LANES), lambda hp, n: (jnp.maximum(cur(n) * nsub - 1, 0), col + hp))
        return pl.BlockSpec((tr, LANES), lambda hp, n: (cur(n), col + hp))

    row_spec = pl.BlockSpec((tr, LANES), lambda hp, n: (cur(n), hp))
    dq_out = pl.BlockSpec((tr, LANES), lambda hp, n: (cur(n), 2 * g + hp))
    kv_out = pl.BlockSpec((tr, LANES), lambda hp, n: (jnp.maximum(n - 1, 0), 2 * g + hp))
    shape = jax.ShapeDtypeStruct((rows, len(ATTN_PATTERNS) * 2 * LANES), F32)
    ins = [qkv, qkv, qkv, qkv, qkv, do, o, lse]
    in_specs = [spec(qc, False), spec(kc, False), spec(kc, True), spec(vc, False), spec(vc, True),
                row_spec, row_spec, row_spec]
    aliases = {}
    if prev is not None:
        aliases = {len(ins) + t: t for t in range(3)}
        ins = ins + list(prev)
        in_specs = in_specs + [ANY] * 3
    n_in = len(ins)

    def entry(*refs):
        body(*refs[:8], *refs[n_in:])

    return pl.pallas_call(
        entry, name=name, grid=(2, ntiles + 1),
        in_specs=in_specs,
        out_specs=[dq_out, kv_out, kv_out],
        out_shape=[shape, shape, shape],
        input_output_aliases=aliases,
        scratch_shapes=[pltpu.VMEM((sb + tr, LANES), F32), pltpu.VMEM((sb + tr, LANES), F32),
                        pltpu.VMEM((2 * tr, LANES), F32), pltpu.VMEM((2 * tr, LANES), F32)],
        compiler_params=_params(("parallel", "arbitrary")),
    )(*ins)


def _mem_probs(q, k):
    s = _dot(q.astype(BF16), k.astype(BF16), 1, 1) * (MEM_HEAD_DIM ** -0.5)
    e = jnp.exp(s - jnp.max(s, axis=-1, keepdims=True))
    return e / jnp.sum(e, axis=-1, keepdims=True)


def _mem_attn_fwd(mq, kv, *, tq, name):
    rows = mq.shape[0]

    def body(q_ref, k_ref, v_ref, o_ref):
        p = _mem_probs(q_ref[...], k_ref[...])
        o_ref[...] = _dot(p.astype(BF16), v_ref[...].astype(BF16), 1, 0)

    return pl.pallas_call(
        body, name=name, grid=(rows // tq, MEM_HEADS),
        in_specs=[pl.BlockSpec((tq, LANES), lambda i, h: (i, h)),
                  pl.BlockSpec((MEM_LEN, LANES), lambda i, h: (0, h)),
                  pl.BlockSpec((MEM_LEN, LANES), lambda i, h: (0, MEM_HEADS + h))],
        out_specs=pl.BlockSpec((tq, LANES), lambda i, h: (i, h)),
        out_shape=jax.ShapeDtypeStruct((rows, MEM_HEADS * LANES), F32),
        compiler_params=_params(("parallel", "parallel")),
    )(mq, kv, kv)


def _mem_attn_bwd(mq, kv, dmo, *, tq, name):
    rows = mq.shape[0]
    scale = MEM_HEAD_DIM ** -0.5

    def body(q_ref, k_ref, v_ref, do_ref, dq_ref, dk_ref, dv_ref):
        i = pl.program_id(1)
        qb = q_ref[...].astype(BF16)
        kb = k_ref[...].astype(BF16)
        vb = v_ref[...].astype(BF16)
        dob = do_ref[...].astype(BF16)
        p = _mem_probs(q_ref[...], k_ref[...])
        dp = _dot(dob, vb, 1, 1)
        ds = (p * (dp - jnp.sum(p * dp, axis=-1, keepdims=True)) * scale).astype(BF16)
        dq_ref[...] = _dot(ds, kb, 1, 0).astype(dq_ref.dtype)

        @pl.when(i == 0)
        def _():
            dk_ref[...] = jnp.zeros_like(dk_ref)
            dv_ref[...] = jnp.zeros_like(dv_ref)

        dk_ref[...] += _dot(ds, qb, 0, 0)
        dv_ref[...] += _dot(p.astype(BF16), dob, 0, 0)

    kv_out = pl.BlockSpec((MEM_LEN, LANES), lambda h, i: (0, h))
    kv_shape = jax.ShapeDtypeStruct((MEM_LEN, MEM_HEADS * LANES), F32)
    return pl.pallas_call(
        body, name=name, grid=(MEM_HEADS, rows // tq),
        in_specs=[pl.BlockSpec((tq, LANES), lambda h, i: (i, h)),
                  pl.BlockSpec((MEM_LEN, LANES), lambda h, i: (0, h)),
                  pl.BlockSpec((MEM_LEN, LANES), lambda h, i: (0, MEM_HEADS + h)),
                  pl.BlockSpec((tq, LANES), lambda h, i: (i, h))],
        out_specs=[pl.BlockSpec((tq, LANES), lambda h, i: (i, h)), kv_out, kv_out],
        out_shape=[jax.ShapeDtypeStruct((rows, MEM_HEADS * LANES), BF16), kv_shape, kv_shape],
        compiler_params=_params(("parallel", "arbitrary")),
    )(mq, kv, kv, dmo)


def _resident(shape):
    return pl.BlockSpec(shape, lambda i: (0, 0), pipeline_mode=pl.Buffered(1))


def _branch_merge_fwd(acts, wts, zg, b_gate, *, tm, name):
    rows = zg.shape[0]
    d = wts[0].shape[0]

    def body(s_ref, a_ref, m_ref, ws_ref, wa_ref, wm_ref, zg_ref, b_ref, o_ref):
        gt = _sigmoid(zg_ref[...] + b_ref[...])
        acc = None
        for k, (x_ref, w_ref) in enumerate(((s_ref, ws_ref), (a_ref, wa_ref), (m_ref, wm_ref))):
            term = gt[:, k * d:(k + 1) * d] * _dot(x_ref[...].astype(BF16), w_ref[...], 1, 1)
            acc = term if acc is None else acc + term
        o_ref[...] = acc.astype(BF16)

    return pl.pallas_call(
        body, name=name, grid=(rows // tm,),
        in_specs=[pl.BlockSpec((tm, x.shape[1]), lambda i: (i, 0)) for x in acts] + [_resident(w.shape) for w in wts]
        + [pl.BlockSpec((tm, 3 * d), lambda i: (i, 0)), pl.BlockSpec((1, 3 * d), lambda i: (0, 0))],
        out_specs=pl.BlockSpec((tm, d), lambda i: (i, 0)), out_shape=jax.ShapeDtypeStruct((rows, d), BF16),
        compiler_params=_params(("parallel",)),
    )(*acts, *wts, zg, b_gate)


def _branch_merge_bwd(dmerged, acts, wts, zg, b_gate, *, tm, name, carry=None):
    rows = zg.shape[0]
    d = wts[0].shape[0]

    def body(dm_ref, s_ref, a_ref, m_ref, ws_ref, wa_ref, wm_ref, zg_ref, b_ref,
             ds_ref, da_ref, dmm_ref, dws_ref, dwa_ref, dwm_ref, dzg_ref, db_ref):
        i = pl.program_id(0)

        @pl.when(i == 0)
        def _():
            for r in (dws_ref, dwa_ref, dwm_ref, db_ref):
                r[...] = jnp.zeros_like(r)

        gt = _sigmoid(zg_ref[...] + b_ref[...])
        dm = dm_ref[...]
        groups = ((s_ref, ws_ref, ds_ref, dws_ref), (a_ref, wa_ref, da_ref, dwa_ref), (m_ref, wm_ref, dmm_ref, dwm_ref))
        for k, (x_ref, w_ref, dx_ref, dw_ref) in enumerate(groups):
            cs = pl.ds(k * d, d)
            gk = gt[:, k * d:(k + 1) * d]
            xb = x_ref[...].astype(BF16)
            br = _dot(xb, w_ref[...], 1, 1)
            dbr = (dm * gk).astype(BF16)
            dx_ref[...] = _dot(dbr, w_ref[...], 1, 0)
            dw_ref[...] += _dot(dbr, xb, 0, 0)
            dzg = dm * br * gk * (1.0 - gk)
            dzg_ref[:, cs] = dzg.astype(BF16)
            db_ref[:, cs] += jnp.sum(dzg, axis=0, keepdims=True)

    row = lambda w: pl.BlockSpec((tm, w), lambda i: (i, 0))
    whole = lambda shape: pl.BlockSpec(shape, lambda i: (0, 0))
    res = _call_with_carry(
        body, carry, name=name, grid=(rows // tm,),
        in_specs=[row(d)] + [row(x.shape[1]) for x in acts] + [_resident(w.shape) for w in wts] + [row(3 * d), whole((1, 3 * d))],
        out_specs=[row(x.shape[1]) for x in acts] + [whole(w.shape) for w in wts] + [row(3 * d), whole((1, 3 * d))],
        out_shape=[jax.ShapeDtypeStruct(x.shape, F32) for x in acts] + [jax.ShapeDtypeStruct(w.shape, F32) for w in wts]
        + [jax.ShapeDtypeStruct((rows, 3 * d), BF16), jax.ShapeDtypeStruct((1, 3 * d), F32)],
        scratch=[], operands=[dmerged, *acts, *wts, zg, b_gate], semantics=("arbitrary",))
    return tuple(res) if carry is None else (tuple(res[:8]), list(res[8:]))


def _adamw(w, g, m, v, *, tr, name):
    rows, cols = w.shape[-2:]
    assert rows % tr == 0, (name, rows, tr)

    def body(w_ref, g_ref, m_ref, v_ref, g_out, d_ref, nm_ref, nv_ref):
        gv = g_ref[...]
        m2 = ADAM_B1 * m_ref[...] + (1.0 - ADAM_B1) * gv
        v2 = ADAM_B2 * v_ref[...] + (1.0 - ADAM_B2) * (gv * gv)
        m_hat = m2 / (1.0 - ADAM_B1 ** ADAM_STEP)
        v_hat = v2 / (1.0 - ADAM_B2 ** ADAM_STEP)
        g_out[...] = gv
        d_ref[...] = -ADAM_LR * (m_hat / (jnp.sqrt(v_hat) + ADAM_EPS) + ADAM_WD * w_ref[...])
        nm_ref[...] = m2
        nv_ref[...] = v2

    flat = pl.BlockSpec((tr, cols), lambda i: (i, 0))
    blk = flat if w.ndim == 2 else pl.BlockSpec((None, tr, cols), lambda i: (0, i, 0))
    shape = jax.ShapeDtypeStruct(w.shape, F32)
    return pl.pallas_call(
        body, name=name, grid=(rows // tr,), in_specs=[blk, flat, blk, blk], out_specs=[blk] * 4,
        out_shape=[shape] * 4, compiler_params=_params(("parallel",)),
    )(w, g, m, v)


ANY = pl.BlockSpec(memory_space=pl.ANY)


def _position():
    return lax.axis_index("x"), lax.axis_index("y"), lax.axis_index("c")


def _other_chips(x, y):
    return ((1 - x, y), (x, 1 - y), (1 - x, 1 - y))


def _remote(src, dst, send_sem, recv_sem, dev):
    return pltpu.make_async_remote_copy(src_ref=src, dst_ref=dst, send_sem=send_sem, recv_sem=recv_sem,
                                        device_id=dev, device_id_type=MESH)


def _gather_exchange(shards):
    nb = len(shards)

    def rows_of(i, owner, core):
        rs = shards[i].shape[0]
        return pl.ds(pl.multiple_of(owner * rs + core * (rs // 2), 16), rs // 2)

    def first_leg(ins, outs, send_sems, recv_sems, i, j):
        x, y, c = _position()
        px, py = _other_chips(x, y)[j]
        half = shards[i].shape[0] // 2
        mine = ins[i].at[pl.ds(pl.multiple_of(c * half, 16), half)]
        return _remote(mine, outs[i].at[rows_of(i, 2 * x + y, c)], send_sems.at[i, j], recv_sems.at[i, j], (px, py, c))

    def passed_on(outs, send_sems, recv_sems, i, j, core):
        x, y, c = _position()
        px, py = _other_chips(x, y)[j]
        rows = outs[i].at[rows_of(i, 2 * px + py, core)]
        return _remote(rows, rows, send_sems.at[i, 3 + j], recv_sems.at[i, 3 + j], (x, y, 1 - c))

    def own_block(ins, outs, send_sems, recv_sems, i):
        x, y, c = _position()
        rs = shards[i].shape[0]
        place = outs[i].at[pl.ds(pl.multiple_of((2 * x + y) * rs, 16), rs)]
        return _remote(ins[i], place, send_sems.at[i, 6], recv_sems.at[i, 6], (x, y, 1 - c))

    def start(ins, outs, send_sems, recv_sems):
        for i in range(nb):
            own_block(ins, outs, send_sems, recv_sems, i).start()
            for j in range(3):
                first_leg(ins, outs, send_sems, recv_sems, i, j).start()

    def finish(ins, outs, send_sems, recv_sems):
        x, y, c = _position()
        for i in range(nb):
            for j, (px, py) in enumerate(_other_chips(x, y)):
                landed = outs[i].at[rows_of(i, 2 * px + py, c)]
                _remote(landed, landed, send_sems.at[i, j], recv_sems.at[i, j], (px, py, c)).wait_recv()
                passed_on(outs, send_sems, recv_sems, i, j, c).start()
        for i in range(nb):
            own_block(ins, outs, send_sems, recv_sems, i).wait()
            for j in range(3):
                passed_on(outs, send_sems, recv_sems, i, j, 1 - c).wait_recv()
        for i in range(nb):
            for j in range(3):
                first_leg(ins, outs, send_sems, recv_sems, i, j).wait_send()
                passed_on(outs, send_sems, recv_sems, i, j, c).wait_send()

    return _Exchange(ins=list(shards), outs=[jax.ShapeDtypeStruct((N_CHIPS * s.shape[0], s.shape[1]), s.dtype) for s in shards],
                     aliases={}, sems=[(nb, 7), (nb, 7)], start=start, finish=finish)


def _run_exchange(ex, *, name):
    n_in, n_out = len(ex.ins), len(ex.outs)

    def body(*refs):
        c_in, c_out, sems = refs[:n_in], refs[n_in:n_in + n_out], refs[n_in + n_out:]
        ex.start(c_in, c_out, *sems)
        ex.finish(c_in, c_out, *sems)

    return pl.pallas_call(
        body, name=name, in_specs=[ANY] * n_in, out_specs=[ANY] * n_out, out_shape=list(ex.outs),
        input_output_aliases=dict(ex.aliases),
        scratch_shapes=[pltpu.SemaphoreType.DMA(s) for s in ex.sems],
    )(*ex.ins)


def _row_tile(rows):
    return max(t for t in range(16, min(rows, 512) + 1, 16) if rows % t == 0)


def _halves_exchange(grads):
    nb = len(grads)

    def copies(ins, outs, send_sems, recv_sems):
        x, y, c = _position()
        return [_remote(ins[i].at[:, 1 - c], outs[i], send_sems.at[i], recv_sems.at[i], (x, y, 1 - c)) for i in range(nb)]

    def start(ins, outs, send_sems, recv_sems):
        for cp in copies(ins, outs, send_sems, recv_sems):
            cp.start()

    def finish(ins, outs, send_sems, recv_sems):
        for cp in copies(ins, outs, send_sems, recv_sems):
            cp.wait()

    return _Exchange(ins=list(grads), outs=[jax.ShapeDtypeStruct((N_CHIPS, g.shape[2], g.shape[3]), F32) for g in grads],
                     aliases={}, sems=[(nb,), (nb,)], start=start, finish=finish)


def _join_exchanges(parts):
    assert all(not ex.aliases for ex in parts)

    def split(refs, counts):
        out, at = [], 0
        for k in counts:
            out.append(refs[at:at + k])
            at += k
        return out

    def run(which):
        def go(ins, outs, *sems):
            for ex, i, o, s in zip(parts, split(ins, [len(ex.ins) for ex in parts]), split(outs, [len(ex.outs) for ex in parts]),
                                   split(sems, [len(ex.sems) for ex in parts])):
                getattr(ex, which)(i, o, *s)
        return go

    return _Exchange(ins=[a for ex in parts for a in ex.ins], outs=[a for ex in parts for a in ex.outs], aliases={},
                     sems=[s for ex in parts for s in ex.sems], start=run("start"), finish=run("finish"))


def _pair_sum(g4, got, c_arr, *, name):
    _, _, half, cols = g4.shape
    tr = _row_tile(half)

    def body(c_ref, g_ref, t_ref, p_ref, pb_ref):
        sm = g_ref[...] + t_ref[...]
        p_ref[...] = sm
        pb_ref[...] = sm.astype(BF16)

    blk = pl.BlockSpec((None, tr, cols), lambda j, i, c_ref: (j, i, 0))
    grid_spec = pltpu.PrefetchScalarGridSpec(
        num_scalar_prefetch=1, grid=(N_CHIPS, half // tr),
        in_specs=[pl.BlockSpec((None, None, tr, cols), lambda j, i, c_ref: (j, c_ref[0], i, 0)), blk],
        out_specs=[blk, blk])
    return pl.pallas_call(
        body, name=name, grid_spec=grid_spec,
        out_shape=[jax.ShapeDtypeStruct((N_CHIPS, half, cols), F32), jax.ShapeDtypeStruct((N_CHIPS, half, cols), BF16)],
        compiler_params=_params(("parallel", "parallel")),
    )(c_arr, g4, got)


def _scatter_exchange(parts):
    nb = len(parts)

    def copies(ins, outs, send_sems, recv_sems):
        x, y, c = _position()
        return [_remote(ins[i].at[2 * px + py], outs[i].at[j], send_sems.at[i, j], recv_sems.at[i, j], (px, py, c))
                for i in range(nb) for j, (px, py) in enumerate(_other_chips(x, y))]

    def start(ins, outs, send_sems, recv_sems):
        for cp in copies(ins, outs, send_sems, recv_sems):
            cp.start()

    def finish(ins, outs, send_sems, recv_sems):
        for cp in copies(ins, outs, send_sems, recv_sems):
            cp.wait()

    return _Exchange(ins=list(parts), outs=[jax.ShapeDtypeStruct((3,) + p.shape[1:], p.dtype) for p in parts],
                     aliases={}, sems=[(nb, 3), (nb, 3)], start=start, finish=finish)


def _owner_sum(p, got, chip_arr, c_arr, *, replicated, name):
    _, half, cols = p.shape
    tr = _row_tile(half)

    def body(chip_ref, c_ref, p_ref, r_ref, o_ref):
        o_ref[...] = ((p_ref[...] + r_ref[0].astype(F32)) + r_ref[1].astype(F32)) + r_ref[2].astype(F32)

    if replicated:
        out_spec = pl.BlockSpec((None, None, tr, cols), lambda i, chip_ref, c_ref: (chip_ref[0], c_ref[0], i, 0))
        out_shape = jax.ShapeDtypeStruct((N_CHIPS, 2, half, cols), F32)
    else:
        out_spec = pl.BlockSpec((None, tr, cols), lambda i, chip_ref, c_ref: (c_ref[0], i, 0))
        out_shape = jax.ShapeDtypeStruct((2, half, cols), F32)
    grid_spec = pltpu.PrefetchScalarGridSpec(
        num_scalar_prefetch=2, grid=(half // tr,),
        in_specs=[pl.BlockSpec((None, tr, cols), lambda i, chip_ref, c_ref: (chip_ref[0], i, 0)),
                  pl.BlockSpec((3, tr, cols), lambda i, chip_ref, c_ref: (0, i, 0))],
        out_specs=out_spec)
    return pl.pallas_call(
        body, name=name, grid_spec=grid_spec, out_shape=out_shape,
        compiler_params=_params(("parallel",)),
    )(chip_arr, c_arr, p, got)


def _share_reduced(bufs):
    nb = len(bufs) - 1

    def body(*refs):
        outs = refs[nb + 1:2 * nb + 2]
        send_sems, recv_sems = refs[2 * nb + 2:]
        x, y, c = _position()
        chip = 2 * x + y
        sends = []
        for i in range(nb):
            cp = _remote(outs[i].at[c], outs[i].at[c], send_sems.at[i], recv_sems.at[i], (x, y, 1 - c))
            cp.start()
            sends.append(cp)
        small = outs[nb]
        peers = [(fx, fy, fc) for fx in (0, 1) for fy in (0, 1) for fc in (0, 1) if fx + fy + fc > 0]
        for k, (fx, fy, fc) in enumerate(peers):
            dev = (x ^ fx, y ^ fy, c ^ fc)
            cp = _remote(small.at[chip, c], small.at[chip, c], send_sems.at[nb + k], recv_sems.at[nb + k], dev)
            cp.start()
            sends.append(cp)
        for i in range(nb):
            dst = outs[i].at[1 - c]
            _remote(dst, dst, send_sems.at[i], recv_sems.at[i], (x, y, 1 - c)).wait_recv()
        for k, (fx, fy, fc) in enumerate(peers):
            dst = small.at[2 * (x ^ fx) + (y ^ fy), c ^ fc]
            _remote(dst, dst, send_sems.at[nb + k], recv_sems.at[nb + k], (x ^ fx, y ^ fy, c ^ fc)).wait_recv()
        for cp in sends:
            cp.wait_send()

    n_all = nb + 1
    return pl.pallas_call(
        body, name="grad_share_reduced", in_specs=[ANY] * n_all, out_specs=[ANY] * n_all,
        out_shape=[jax.ShapeDtypeStruct(b.shape, b.dtype) for b in bufs],
        input_output_aliases={i: i for i in range(n_all)},
        scratch_shapes=[pltpu.SemaphoreType.DMA((nb + 7,)), pltpu.SemaphoreType.DMA((nb + 7,))],
    )(*bufs)


class _GradReducer:
    def __init__(self, c_arr, chip_arr):
        self.c_arr, self.chip_arr = c_arr, chip_arr
        self.full, self.pairs, self.landed = {}, {}, {}

    def swap(self, names, grads):
        for n, g in zip(names, grads):
            self.full[n] = g.reshape(N_CHIPS, 2, g.shape[0] // (2 * N_CHIPS), g.shape[1])
        return _halves_exchange([self.full[n] for n in names])

    def swapped(self, names, bufs):
        for n, t in zip(names, bufs):
            self.pairs[n] = _pair_sum(self.full[n], t, self.c_arr, name="grad_pair_sum_" + n)

    def scatter(self, names):
        return _scatter_exchange([self.pairs[n][1] for n in names])

    def collect(self, names, bufs):
        self.landed.update(zip(names, bufs))

    def swap_now(self, names, grads):
        self.swapped(names, _run_exchange(self.swap(names, grads), name="grad_exchange_" + names[0]))

    def finish(self, names, grads, order):
        self.swap_now(names, grads)
        self.collect(names, _run_exchange(self.scatter(names), name="grad_scatter_" + names[0]))
        totals = [_owner_sum(self.pairs[n][0], self.landed[n], self.chip_arr, self.c_arr, replicated=(n == order[-1]),
                             name="grad_owner_sum_" + n) for n in order]
        return _share_reduced(totals)


def _pack_small(vals):
    flat = jnp.concatenate([vals[name].reshape(-1) for name, _ in SMALL])
    return jnp.pad(flat, (0, N_CHIPS * SMALL_ROWS * 1024 - SMALL_ELEMS)).reshape(N_CHIPS * SMALL_ROWS, 1024)


def _unpack_small(buf):
    flat = buf.reshape(-1)
    out, off = {}, 0
    for name, shape in SMALL:
        n = int(np.prod(shape))
        out[name] = flat[off:off + n].reshape(shape)
        off += n
    return out


EARLY_REDUCED = (("w_down",), ("w_up",), ("w_o", "w_ssm_br", "w_attn_br", "w_mem_br", "w_glu", "w_mem_kv"), ("w_in",))


def _device_step(x, mem, tgt, w, p, *, shards, reducer):
    rows = x.shape[0]
    w = dict(w)
    early = EARLY_REDUCED
    gb = {}
    gather_pending = shards is not None

    def riding(*stages):
        if reducer is None or not stages:
            return None
        return _join_exchanges([reducer.swap(names, [gb[n] for n in names]) if kind == "swap" else reducer.scatter(names)
                                for kind, names in stages])

    def arrived(stages, res):
        if reducer is None or not stages:
            return res
        main, bufs = res
        for kind, names in stages:
            (reducer.swapped if kind == "swap" else reducer.collect)(names, bufs[:len(names)])
            bufs = bufs[len(names):]
        return main

    def fetching(names):
        return _gather_exchange([shards[n] for n in names]) if gather_pending else None

    def fetched(names, res):
        if not gather_pending:
            return res
        w.update(zip(names, res[1]))
        return res[0]

    first_use = (("w_glu", "w_ssm_br", "w_attn_br", "w_mem_kv", "w_mem_br", "w_o"), ("w_up",), ("w_down",))
    g1, gm, g2 = p["norm1_g"], p["mem_norm_g"], p["norm2_g"]
    gf = p["final_g"].reshape(1, D_MODEL)
    ssm_args = (p["ssm_lambda_re"][0], p["ssm_lambda_im"][0], p["ssm_log_dt"][0], p["ssm_b_re"][0],
                p["ssm_b_im"][0], p["ssm_c_re"][0], p["ssm_c_im"][0])
    (a_lay, b_blk, c_blk), ssm_vjp = jax.vjp(_ssm_matrices, *ssm_args)
    a_conj = a_lay * _to_scan_layout(jnp.stack([jnp.ones((N_STATES,), F32), -jnp.ones((N_STATES,), F32)]))[None, :]
    dd = p["ssm_d"].reshape(1, SSM_WIDTH)
    win_t = w["w_in"]
    mm = _matmul

    n1 = _rmsnorm_fwd(x, g1, tm=512, name="norm1")
    u = mm(n1, win_t, m=rows, n=512, k=1024, tb=True, tm=2048, tn=512, tk=1024, out_dtypes=(F32,), name="in_u")
    qkv = fetched(first_use[0], mm(n1, win_t, m=rows, n=2304, k=1024, tb=True, tm=2048, tn=256, tk=1024,
                                   b_off=(OFF_QKV // 256, 0), out_dtypes=(F32,), carry=fetching(first_use[0]), name="in_qkv"))
    mq = mm(n1, win_t, m=rows, n=512, k=1024, tb=True, tm=2048, tn=256, tk=1024, b_off=(OFF_MQ // 256, 0),
            out_dtypes=(F32,), name="in_mq")
    zg = fetched(first_use[1], mm(n1, win_t, m=rows, n=3072, k=1024, tb=True, tm=2048, tn=256, tk=1024,
                                  b_off=(OFF_ZG // 256, 0), out_dtypes=(F32,), carry=fetching(first_use[1]), name="in_zg"))

    u_i = _interleave(u)
    ends = _ssm_ends(a_lay, u_i, b_blk, transpose=False, reverse=False, tt=512, name="ssm_fwd_ends")
    s, ys_i, s_entry = _ssm_fwd(a_lay, u_i, b_blk, c_blk, ends, tt=512, name="ssm_fwd")
    ys = _deinterleave(ys_i)
    y0, tglu, y2 = _glu_fwd(ys, u, dd, w["w_glu"], p["b_glu"], tm=512, name="glu_fwd")

    outs, lses = [], []
    for g, (_, d) in enumerate(ATTN_PATTERNS):
        o_g, lse_g = _attn_fwd(qkv, g, d, name=f"attn_fwd_{g}")
        outs.append(o_g)
        lses.append(lse_g)
    o, lse = _attn_merge(outs, lses, tm=1024, name="attn_merge")

    mn = _rmsnorm_fwd(mem, gm, tm=MEM_LEN, name="mem_norm")
    kv = mm(mn, w["w_mem_kv"], m=MEM_LEN, n=1024, k=1024, tm=MEM_LEN, tn=1024, tk=1024, out_dtypes=(F32,), name="mem_kv")
    mo = _mem_attn_fwd(mq, kv, tq=1024, name="mem_attn_fwd")

    branch_acts = (y2, o, mo)
    branch_wts = (w["w_ssm_br"], w["w_attn_br"], w["w_mem_br"])
    merged = _branch_merge_fwd(branch_acts, branch_wts, zg, p["b_gate"], tm=256, name="branch_merge_fwd")
    h1, n2 = mm(merged, w["w_o"], m=rows, n=1024, k=1024, tm=1024, tn=1024, tk=1024, out_dtypes=(F32, BF16),
                aux=((x, "mn"), (g2, "row")), epilogue=_residual_norm_epilogue, name="out_proj")
    relu2 = lambda acc: (jnp.square(jnp.maximum(acc, 0.0)),)
    act = fetched(first_use[2], mm(n2, w["w_up"], m=rows, n=D_FF, k=1024, tb=True, tm=1024, tn=1024, tk=1024,
                                   out_dtypes=(BF16,), epilogue=relu2, carry=fetching(first_use[2]), name="mlp_up"))
    dh2, d_gf, sq_err = mm(act, w["w_down"], m=rows, n=1024, k=D_FF, tm=1024, tn=1024, tk=1024, out_dtypes=(F32,),
                           aux=((h1, "mn"), (tgt, "mn"), (gf, "row")), epilogue=_loss_head_epilogue, n_sums=2, name="mlp_down")
    loss = (0.5 / D_MODEL) * jnp.sum(sq_err)

    gs = {"final_g": d_gf.reshape(D_MODEL)}
    drelu2 = lambda acc, actv: (acc * (2.0 * jnp.sqrt(actv.astype(F32))),)
    dup = mm(dh2, w["w_down"], m=rows, n=D_FF, k=1024, tb=True, tm=1024, tn=2048, tk=1024, out_dtypes=(BF16,),
             aux=((act, "mn"),), epilogue=drelu2, name="d_act")
    gb["w_down"] = mm(act, dh2, m=D_FF, n=1024, k=rows, ta=True, tm=1024, tn=1024, tk=1024, out_dtypes=(F32,), name="dw_down")
    stages = (("swap", early[0]),)
    gb["w_up"] = arrived(stages, mm(dup, n2, m=D_FF, n=1024, k=rows, ta=True, tm=1024, tn=1024, tk=1024,
                                    out_dtypes=(F32,), carry=riding(*stages), name="dw_up"))
    stages = (("scatter", early[0]), ("swap", early[1]))
    dh1, gs["norm2_g"] = arrived(stages, mm(dup, w["w_up"], m=rows, n=1024, k=D_FF, tm=1024, tn=1024, tk=1024,
                                            out_dtypes=(F32,), aux=((h1, "mn"), (dh2, "mn"), (g2, "row")),
                                            epilogue=_rmsnorm_bwd_epilogue, n_sums=1, carry=riding(*stages), name="d_n2"))
    dmerged = mm(dh1, w["w_o"], m=rows, n=1024, k=1024, tb=True, tm=1024, tn=1024, tk=1024, out_dtypes=(F32,), name="d_merged")
    gb["w_o"] = mm(merged, dh1, m=1024, n=1024, k=rows, ta=True, tm=1024, tn=1024, tk=1024, out_dtypes=(F32,), name="dw_o")
    stages = (("scatter", early[1]),)
    (dy2, do, dmo, gb["w_ssm_br"], gb["w_attn_br"], gb["w_mem_br"], dzg, gs["b_gate"]) = arrived(stages, _branch_merge_bwd(
        dmerged, branch_acts, branch_wts, zg, p["b_gate"], tm=256, carry=riding(*stages), name="branch_merge_bwd"))

    dy0, dt, y1, gs["b_glu"], d_dd = _glu_bwd(dy2, y0, tglu, u, w["w_glu"], tm=512, name="glu_bwd")
    gs["ssm_d"] = d_dd.reshape(1, SSM_GROUPS, SSM_GROUP_SIZE)
    gb["w_glu"] = mm(y1, dt, m=512, n=512, k=rows, ta=True, tm=512, tn=512, tk=1024, out_dtypes=(F32,), name="dw_glu")
    dy0_i = _interleave(dy0)
    lam_ends = _ssm_ends(a_conj, dy0_i, c_blk, transpose=True, reverse=True, tt=512, name="ssm_bwd_ends")
    du_i, d_b_blk, d_c_blk, d_a_lay = _ssm_bwd(a_conj, dy0_i, u_i, s, s_entry, b_blk, c_blk, dd, lam_ends, tt=512,
                                                name="ssm_bwd")
    du = _deinterleave(du_i)
    d_ssm = ssm_vjp((d_a_lay, d_b_blk, d_c_blk))
    for name, val in zip(("ssm_lambda_re", "ssm_lambda_im", "ssm_log_dt", "ssm_b_re", "ssm_b_im", "ssm_c_re", "ssm_c_im"), d_ssm):
        gs[name] = val[None]

    dqkv = None
    for g, (_, d) in enumerate(ATTN_PATTERNS):
        dqkv = _attn_bwd(qkv, do, o, lse, g, d, dqkv, name=f"attn_bwd_{g}")

    dmq, dmk, dmv = _mem_attn_bwd(mq, kv, dmo, tq=1024, name="mem_attn_bwd")
    dkv = jnp.concatenate([dmk, dmv], axis=1)
    gb["w_mem_kv"] = mm(mn, dkv, m=1024, n=1024, k=MEM_LEN, ta=True, tm=1024, tn=1024, tk=MEM_LEN, out_dtypes=(F32,), name="dw_mem_kv")
    dmn = mm(dkv, w["w_mem_kv"], m=MEM_LEN, n=1024, k=1024, tb=True, tm=MEM_LEN, tn=1024, tk=1024, out_dtypes=(F32,), name="d_mn")
    _, gs["mem_norm_g"] = _rmsnorm_bwd(mem, gm, dmn, None, tm=MEM_LEN, name="mem_norm_bwd")

    pieces = ((du, OFF_U, "u"), (dqkv[0], OFF_QKV, "q"), (dqkv[1], OFF_QKV + 768, "k"), (dqkv[2], OFF_QKV + 1536, "v"),
              (dmq, OFF_MQ, "mq"), (dzg, OFF_ZG, "zg"))
    dw_rows = []
    for piece, off, tag in pieces:
        width = piece.shape[1]
        tmw = 1024 if width % 1024 == 0 else (768 if width == 768 else 512)
        stages = {"q": (("swap", early[2]),), "zg": (("scatter", early[2]),)}.get(tag, ())
        dw_rows.append(arrived(stages, mm(piece, n1, m=width, n=1024, k=rows, ta=True, tm=tmw, tn=1024, tk=1024,
                                          out_dtypes=(F32,), carry=riding(*stages), name="dw_in_" + tag)))
    gb["w_in"] = jnp.concatenate(dw_rows, axis=0)
    if reducer is not None:
        reducer.swap_now(early[3], [gb["w_in"]])
    stages = (("scatter", early[3]),)
    dx, gs["norm1_g"] = arrived(stages, _sum_matmul(
        [piece for piece, _, _ in pieces], win_t, [off for _, off, _ in pieces], tm=512,
        aux=((x, "mn"), (dh1, "mn"), (g1, "row")), epilogue=_rmsnorm_bwd_epilogue, n_sums=1,
        carry=riding(*stages), name="d_n1"))
    return loss, dx, gb, gs


def kernel(x, mem, norm1_g, mem_norm_g, w_in, b_gate, ssm_lambda_re, ssm_lambda_im, ssm_log_dt, ssm_b_re, ssm_b_im, ssm_c_re, ssm_c_im, ssm_d, w_glu, b_glu, w_ssm_br, w_attn_br, w_mem_kv, w_mem_br, w_o, norm2_g, w_up, w_down, final_g, loss_target, m_norm1_g, m_mem_norm_g, m_w_in, m_b_gate, m_ssm_lambda_re, m_ssm_lambda_im, m_ssm_log_dt, m_ssm_b_re, m_ssm_b_im, m_ssm_c_re, m_ssm_c_im, m_ssm_d, m_w_glu, m_b_glu, m_w_ssm_br, m_w_attn_br, m_w_mem_kv, m_w_mem_br, m_w_o, m_norm2_g, m_w_up, m_w_down, m_final_g, v_norm1_g, v_mem_norm_g, v_w_in, v_b_gate, v_ssm_lambda_re, v_ssm_lambda_im, v_ssm_log_dt, v_ssm_b_re, v_ssm_b_im, v_ssm_c_re, v_ssm_c_im, v_ssm_d, v_w_glu, v_b_glu, v_w_ssm_br, v_w_attn_br, v_w_mem_kv, v_w_mem_br, v_w_o, v_norm2_g, v_w_up, v_w_down, v_final_g):
    env = dict(locals())
    weights = {n: env[n] for n in WEIGHT_ORDER}
    moms = {n: env["m_" + n] for n in WEIGHT_ORDER}
    vels = {n: env["v_" + n] for n in WEIGHT_ORDER}
    def shard2d(a):
        return a.reshape(a.shape[-2], a.shape[-1])

    chip = 2 * lax.axis_index("x") + lax.axis_index("y")
    wire = [shard2d(weights[n]).astype(BF16) for n, _, _ in BIG]
    wire = dict(zip([n for n, _, _ in BIG], [s.T if tr else s for s, (_, tr, _) in zip(wire, BIG)]))
    w_in_full = _run_exchange(_gather_exchange([wire.pop("w_in")]), name="all_gather_w_in")[0]
    small = {n: weights[n] for n, _ in SMALL}

    reducer = _GradReducer(lax.axis_index("c").astype(jnp.int32).reshape(1), chip.astype(jnp.int32).reshape(1))
    loss, dx, gb, gs = _device_step(x[0], mem[0], loss_target[0], {"w_in": w_in_full}, small, shards=wire, reducer=reducer)
    *shards, small_grad = reducer.finish(["small"], [_pack_small(gs)], [n for n, _, _ in BIG] + ["small"])
    grads = {}
    for (n, tr, _), sh in zip(BIG, shards):
        sh = sh.reshape(2 * sh.shape[1], sh.shape[2])
        grads[n] = sh.T if tr else sh
    small_grad = small_grad.reshape(N_CHIPS * SMALL_ROWS, 1024)
    grads_small = _unpack_small(small_grad)

    delta, new_m, new_v = {}, {}, {}
    for n, _, _ in BIG:
        grads[n], delta[n], new_m[n], new_v[n] = _adamw(weights[n], grads[n], moms[n], vels[n],
                                                        tr=min(weights[n].shape[-2], 256), name="adamw_" + n)
    _, ds_, ms_, vs_ = _adamw(_pack_small(small), small_grad,
                              _pack_small({n: moms[n] for n, _ in SMALL}), _pack_small({n: vels[n] for n, _ in SMALL}),
                              tr=N_CHIPS * SMALL_ROWS, name="adamw_small")
    for dst, buf in ((delta, ds_), (new_m, ms_), (new_v, vs_)):
        dst.update(_unpack_small(buf))
    grads.update(grads_small)

    total_loss = lax.psum(loss, ("x", "y", "c"))
    return (total_loss, dx[None], *[grads[n] for n in WEIGHT_ORDER], *[delta[n] for n in WEIGHT_ORDER],
            *[new_m[n] for n in WEIGHT_ORDER], *[new_v[n] for n in WEIGHT_ORDER])
```

```python
import functools
import math

import numpy as np
import jax
import jax.numpy as jnp
from jax import lax
from jax.experimental import pallas as pl
from jax.experimental.pallas import tpu as pltpu

F32 = jnp.float32
BF16 = jnp.bfloat16

D_MODEL = 1024
SSM_GROUPS = 32
SSM_GROUP_SIZE = 16
SSM_STATE = 64
SSM_WIDTH = 512
N_STATES = SSM_GROUPS * SSM_STATE
SCAN_CB = 1024
ATTN_PATTERNS = ((128, 1), (512, 4), (2048, 16))
ATTN_HEAD_DIM = 64
ATTN_Q = 128
MEM_LEN = 256
MEM_HEAD_DIM = 128
MEM_HEADS = 4
D_FF = 4096
OFF_U, OFF_QKV, OFF_MQ, OFF_ZG = 0, 512, 2816, 3328
IN_WIDTH = 6400
RMS_EPS = 1e-6
NEG_INF = -1e30
ADAM_LR, ADAM_B1, ADAM_B2, ADAM_EPS, ADAM_WD, ADAM_STEP = 0.001, 0.9, 0.999, 1e-08, 0.01, 10

VMEM_LIMIT_BYTES = 48 * 1024 * 1024
VMEM_LIMIT_WIDE_BYTES = 56 * 1024 * 1024
LANES = 128
MESH = pl.DeviceIdType.MESH
N_CHIPS = 4

SCAN_SEGS = 8
SCAN_GROUPS = SCAN_CB // SSM_STATE

BIG = (("w_in", True, (6400, 1024)), ("w_glu", False, (512, 512)), ("w_ssm_br", True, (1024, 512)),
       ("w_attn_br", True, (1024, 256)), ("w_mem_kv", False, (1024, 1024)), ("w_mem_br", True, (1024, 512)),
       ("w_o", False, (1024, 1024)), ("w_up", True, (4096, 1024)), ("w_down", False, (4096, 1024)))
SMALL = (("norm1_g", (1, 1024)), ("mem_norm_g", (1, 1024)), ("b_gate", (1, 3072)),
         ("ssm_lambda_re", (1, 32, 64)), ("ssm_lambda_im", (1, 32, 64)), ("ssm_log_dt", (1, 32)),
         ("ssm_b_re", (1, 32, 64, 16)), ("ssm_b_im", (1, 32, 64, 16)), ("ssm_c_re", (1, 32, 16, 64)),
         ("ssm_c_im", (1, 32, 16, 64)), ("ssm_d", (1, 32, 16)), ("b_glu", (1, 512)),
         ("norm2_g", (1, 1024)), ("final_g", (1024,)))
WEIGHT_ORDER = ("norm1_g", "mem_norm_g", "w_in", "b_gate", "ssm_lambda_re", "ssm_lambda_im", "ssm_log_dt",
                "ssm_b_re", "ssm_b_im", "ssm_c_re", "ssm_c_im", "ssm_d", "w_glu", "b_glu", "w_ssm_br",
                "w_attn_br", "w_mem_kv", "w_mem_br", "w_o", "norm2_g", "w_up", "w_down", "final_g")
SMALL_ELEMS = sum(int(np.prod(s)) for _, s in SMALL)
SMALL_ROWS = 64


def _params(sem, vmem=VMEM_LIMIT_BYTES):
    return pltpu.CompilerParams(dimension_semantics=sem, vmem_limit_bytes=vmem)


def _sigmoid(v):
    return 1.0 / (1.0 + jnp.exp(-v))


_GELU_C = math.sqrt(2.0 / math.pi)


def _gelu(v):
    return 0.5 * v * (1.0 + jnp.tanh(_GELU_C * (v + 0.044715 * v * v * v)))


def _gelu_grad(v):
    th = jnp.tanh(_GELU_C * (v + 0.044715 * v * v * v))
    return 0.5 * (1.0 + th) + 0.5 * v * (1.0 - th * th) * _GELU_C * (1.0 + 3.0 * 0.044715 * v * v)


def _dot(a, b, ca, cb):
    return lax.dot_general(a, b, (((ca,), (cb,)), ((), ())), preferred_element_type=F32)


class _Exchange:
    def __init__(self, ins, outs, aliases, sems, start, finish):
        self.ins, self.outs, self.aliases, self.sems, self.start, self.finish = ins, outs, aliases, sems, start, finish


def _matmul(a, b, *, m, n, k, ta=False, tb=False, tm, tn, tk, out_dtypes, name,
            a_off=(0, 0), b_off=(0, 0), b_row0=None, aux=(), epilogue=None, n_sums=0, carry=None):
    assert m % tm == 0 and n % tn == 0 and k % tk == 0, (name, m, n, k, tm, tn, tk)
    nk = k // tk
    n_aux = len(aux)
    n_tiles = len(out_dtypes)
    n_out = n_tiles + n_sums
    ar, ac = a_off
    br, bc = b_off
    if ta:
        a_spec = pl.BlockSpec((tk, tm), lambda i, j, kk: (kk + ar, i + ac))
    else:
        a_spec = pl.BlockSpec((tm, tk), lambda i, j, kk: (i + ar, kk + ac))
    if tb:
        if b_row0 is None:
            b_spec = pl.BlockSpec((tn, tk), lambda i, j, kk: (j + br, kk + bc))
        else:
            assert b_row0 % LANES == 0 and tn % LANES == 0 and tk % LANES == 0
            b_spec = pl.BlockSpec((pl.Element(tn), pl.Element(tk)),
                                  lambda i, j, kk: (pl.multiple_of(b_row0 + j * tn, LANES), pl.multiple_of((kk + bc) * tk, LANES)))
    else:
        b_spec = pl.BlockSpec((tk, tn), lambda i, j, kk: (kk + br, j + bc))
    aux_specs = []
    for _, kind in aux:
        if kind == "mn":
            aux_specs.append(pl.BlockSpec((tm, tn), lambda i, j, kk: (i, j)))
        else:
            aux_specs.append(pl.BlockSpec((1, tn), lambda i, j, kk: (0, j)))
    ca = 0 if ta else 1
    cb = 1 if tb else 0

    def finish(acc, aux_refs, out_refs, row_tile):
        outs = (acc,) if epilogue is None else epilogue(acc, *[r[...] for r in aux_refs])
        for o_ref, o in zip(out_refs[:n_tiles], outs[:n_tiles]):
            o_ref[...] = o.astype(o_ref.dtype)
        _accumulate_over_rows(out_refs[n_tiles:], outs[n_tiles:], row_tile)

    def body(a_ref, b_ref, *rest):
        aux_refs = rest[:n_aux]
        out_refs = rest[n_aux:n_aux + n_out]
        row_tile = pl.program_id(0)
        prod = _dot(a_ref[...].astype(BF16), b_ref[...].astype(BF16), ca, cb)
        if nk == 1:
            finish(prod, aux_refs, out_refs, row_tile)
            return
        acc_ref = rest[n_aux + n_out]
        kk = pl.program_id(2)

        @pl.when(kk == 0)
        def _():
            acc_ref[...] = prod

        @pl.when(jnp.logical_and(kk > 0, kk < nk - 1))
        def _():
            acc_ref[...] += prod

        @pl.when(kk == nk - 1)
        def _():
            finish(acc_ref[...] + prod, aux_refs, out_refs, row_tile)

    tile = pl.BlockSpec((tm, tn), lambda i, j, kk: (i, j))
    col_sum = pl.BlockSpec((1, tn), lambda i, j, kk: (0, j))
    res = _call_with_carry(
        body, carry, name=name, grid=(m // tm, n // tn, nk), in_specs=[a_spec, b_spec] + aux_specs,
        out_specs=[tile] * n_tiles + [col_sum] * n_sums,
        out_shape=[jax.ShapeDtypeStruct((m, n), dt) for dt in out_dtypes] + [jax.ShapeDtypeStruct((1, n), F32)] * n_sums,
        scratch=[pltpu.VMEM((tm, tn), F32)] if nk > 1 else [], operands=[a, b] + [x for x, _ in aux],
        semantics=("arbitrary" if n_sums else "parallel", "parallel", "arbitrary"))
    main = res[0] if n_out == 1 else tuple(res[:n_out])
    return main if carry is None else (main, list(res[n_out:]))


def _accumulate_over_rows(sum_refs, terms, row_tile):
    for s_ref, term in zip(sum_refs, terms):
        @pl.when(row_tile == 0)
        def _():
            s_ref[...] = term

        @pl.when(row_tile > 0)
        def _():
            s_ref[...] += term


def _call_with_carry(body, carry, *, name, grid, in_specs, out_specs, out_shape, scratch, operands, semantics,
                     vmem=VMEM_LIMIT_BYTES):
    if carry is None:
        return pl.pallas_call(body, name=name, grid=grid, in_specs=in_specs, out_specs=out_specs, out_shape=out_shape,
                              scratch_shapes=scratch, compiler_params=_params(semantics, vmem))(*operands)
    n_in, n_cin, n_out, n_cout, n_scr = len(operands), len(carry.ins), len(out_shape), len(carry.outs), len(scratch)

    def hosted(*refs):
        main_in, c_in = refs[:n_in], refs[n_in:n_in + n_cin]
        main_out = refs[n_in + n_cin:n_in + n_cin + n_out]
        c_out = refs[n_in + n_cin + n_out:n_in + n_cin + n_out + n_cout]
        rest = refs[n_in + n_cin + n_out + n_cout:]
        ids = [pl.program_id(t) for t in range(len(grid))]
        first = functools.reduce(jnp.logical_and, [i == 0 for i in ids])
        last = functools.reduce(jnp.logical_and, [i == g - 1 for i, g in zip(ids, grid)])

        @pl.when(first)
        def _():
            carry.start(c_in, c_out, *rest[n_scr:])

        body(*main_in, *main_out, *rest[:n_scr])

        @pl.when(last)
        def _():
            carry.finish(c_in, c_out, *rest[n_scr:])

    return pl.pallas_call(
        hosted, name=name, grid=grid,
        in_specs=list(in_specs) + [ANY] * n_cin, out_specs=list(out_specs) + [ANY] * n_cout,
        out_shape=list(out_shape) + list(carry.outs),
        input_output_aliases={n_in + i: n_out + o for i, o in carry.aliases.items()},
        scratch_shapes=list(scratch) + [pltpu.SemaphoreType.DMA(s) for s in carry.sems],
        compiler_params=_params(("arbitrary",) * len(grid), vmem),
    )(*operands, *carry.ins)


def _sum_matmul(pieces, b, offs, *, tm, name, aux=(), epilogue=None, n_sums=0, carry=None):
    m = pieces[0].shape[0]
    n = b.shape[1]
    npieces, n_aux = len(pieces), len(aux)

    def body(*refs):
        b_ref = refs[npieces]
        aux_refs = refs[npieces + 1:npieces + 1 + n_aux]
        out_refs = refs[npieces + 1 + n_aux:]
        acc = None
        for p_ref, off in zip(refs[:npieces], offs):
            part = _dot(p_ref[...].astype(BF16), b_ref[pl.ds(off, p_ref.shape[1]), :], 1, 0)
            acc = part if acc is None else acc + part
        outs = (acc,) if epilogue is None else epilogue(acc, *[r[...] for r in aux_refs])
        out_refs[0][...] = outs[0]
        _accumulate_over_rows(out_refs[1:], outs[1:], pl.program_id(0))

    row = pl.BlockSpec((tm, n), lambda i: (i, 0))
    vec = pl.BlockSpec((1, n), lambda i: (0, 0))
    res = _call_with_carry(
        body, carry, name=name, grid=(m // tm,),
        in_specs=[pl.BlockSpec((tm, p.shape[1]), lambda i: (i, 0)) for p in pieces] + [_resident(b.shape)]
        + [row if kind == "mn" else vec for _, kind in aux],
        out_specs=[row] + [vec] * n_sums,
        out_shape=[jax.ShapeDtypeStruct((m, n), F32)] + [jax.ShapeDtypeStruct((1, n), F32)] * n_sums,
        scratch=[], operands=list(pieces) + [b] + [x for x, _ in aux], semantics=("arbitrary" if n_sums else "parallel",),
        vmem=VMEM_LIMIT_WIDE_BYTES)
    main = res[0] if n_sums == 0 else tuple(res[:1 + n_sums])
    return main if carry is None else (main, list(res[1 + n_sums:]))


def _rmsnorm_fwd(x, g, *, tm, name):
    rows, d = x.shape

    def body(x_ref, g_ref, o_ref):
        xv = x_ref[...]
        r = lax.rsqrt(jnp.mean(xv * xv, axis=-1, keepdims=True) + RMS_EPS)
        o_ref[...] = (xv * r * g_ref[...]).astype(o_ref.dtype)

    return pl.pallas_call(
        body, name=name, grid=(rows // tm,),
        in_specs=[pl.BlockSpec((tm, d), lambda i: (i, 0)), pl.BlockSpec((1, d), lambda i: (0, 0))],
        out_specs=pl.BlockSpec((tm, d), lambda i: (i, 0)),
        out_shape=jax.ShapeDtypeStruct((rows, d), BF16),
        compiler_params=_params(("parallel",)),
    )(x, g)


def _residual_norm_epilogue(acc, xv, gv):
    h = acc + xv
    r = lax.rsqrt(jnp.mean(h * h, axis=-1, keepdims=True) + RMS_EPS)
    return h, h * r * gv


def _rmsnorm_bwd_epilogue(dy, xv, resv, gv):
    r = lax.rsqrt(jnp.mean(xv * xv, axis=-1, keepdims=True) + RMS_EPS)
    xhat = xv * r
    dyg = dy * gv
    dx = r * (dyg - xhat * jnp.mean(dyg * xhat, axis=-1, keepdims=True)) + resv
    return dx, jnp.sum(dy * xhat, axis=0, keepdims=True)


def _rmsnorm_bwd(x, g, dy, res, *, tm, name):
    rows, d = x.shape
    has_res = res is not None

    def body(x_ref, g_ref, dy_ref, *rest):
        if has_res:
            res_ref, dx_ref, dg_ref = rest
        else:
            dx_ref, dg_ref = rest
        i = pl.program_id(0)
        xv = x_ref[...]
        r = lax.rsqrt(jnp.mean(xv * xv, axis=-1, keepdims=True) + RMS_EPS)
        xhat = xv * r
        dyv = dy_ref[...]
        dyg = dyv * g_ref[...]
        dx = r * (dyg - xhat * jnp.mean(dyg * xhat, axis=-1, keepdims=True))
        if has_res:
            dx = dx + res_ref[...]
        dx_ref[...] = dx

        @pl.when(i == 0)
        def _():
            dg_ref[...] = jnp.zeros_like(dg_ref)

        dg_ref[...] += jnp.sum(dyv * xhat, axis=0, keepdims=True)

    row_spec = pl.BlockSpec((tm, d), lambda i: (i, 0))
    vec_spec = pl.BlockSpec((1, d), lambda i: (0, 0))
    ins = [x, g, dy] + ([res] if has_res else [])
    return pl.pallas_call(
        body, name=name, grid=(rows // tm,),
        in_specs=[row_spec, vec_spec, row_spec] + ([row_spec] if has_res else []),
        out_specs=[row_spec, vec_spec],
        out_shape=[jax.ShapeDtypeStruct((rows, d), F32), jax.ShapeDtypeStruct((1, d), F32)],
        compiler_params=_params(("arbitrary",)),
    )(*ins)


def _loss_head_epilogue(acc, hv, tgtv, gv):
    xv = acc + hv
    r = lax.rsqrt(jnp.mean(xv * xv, axis=-1, keepdims=True) + RMS_EPS)
    xhat = xv * r
    err = xhat * gv - tgtv
    dyv = err * (1.0 / D_MODEL)
    dyg = dyv * gv
    dh = r * (dyg - xhat * jnp.mean(dyg * xhat, axis=-1, keepdims=True))
    return dh, jnp.sum(dyv * xhat, axis=0, keepdims=True), jnp.sum(err * err, axis=0, keepdims=True)


def _to_scan_layout(v):
    lead = v.shape[:-2]
    v = v.reshape(lead + (2, N_STATES // SCAN_CB, SCAN_CB))
    v = jnp.swapaxes(v, -3, -2)
    return v.reshape(lead + (2 * N_STATES,))


def _ssm_matrices(lam_re, lam_im, log_dt, b_re, b_im, c_re, c_im):
    dt = jnp.exp(log_dt)[:, None]
    mag = jnp.exp(lam_re * dt)
    a_re, a_im = mag * jnp.cos(lam_im * dt), mag * jnp.sin(lam_im * dt)
    nr, ni = a_re - 1.0, a_im
    den = lam_re * lam_re + lam_im * lam_im
    coef_re = (nr * lam_re + ni * lam_im) / den
    coef_im = (ni * lam_re - nr * lam_im) / den
    bb_re = coef_re[..., None] * b_re - coef_im[..., None] * b_im
    bb_im = coef_re[..., None] * b_im + coef_im[..., None] * b_re
    a_lay = _to_scan_layout(jnp.stack([a_re.reshape(-1), a_im.reshape(-1)], axis=0))[None, :]
    nblk = SSM_GROUPS // SCAN_GROUPS
    eye = jnp.eye(SCAN_GROUPS, dtype=F32)

    def b_block(bb):
        bb = bb.reshape(nblk, SCAN_GROUPS, SSM_STATE, SSM_GROUP_SIZE)
        return jnp.einsum("gk,jkph->jghkp", eye, bb).reshape(nblk, SCAN_GROUPS * SSM_GROUP_SIZE, SCAN_CB)

    b_blk = jnp.concatenate([b_block(bb_re), b_block(bb_im)], axis=2)

    def c_block(cc):
        cc = cc.reshape(nblk, SCAN_GROUPS, SSM_GROUP_SIZE, SSM_STATE)
        return jnp.einsum("gk,jghp->jkpgh", eye, cc).reshape(nblk, SCAN_CB, SCAN_GROUPS * SSM_GROUP_SIZE)

    c_blk = jnp.concatenate([c_block(c_re), -c_block(c_im)], axis=1)
    return a_lay, b_blk, c_blk


def _interleave(v):
    rows, c = v.shape
    return v.reshape(SCAN_SEGS, rows // SCAN_SEGS, c).transpose(1, 0, 2).reshape(rows, c)


def _deinterleave(v):
    rows, c = v.shape
    return v.reshape(rows // SCAN_SEGS, SCAN_SEGS, c).transpose(1, 0, 2).reshape(rows, c)


def _scan_groups(a_ref, bu_ref, o_ref, state, *, reverse, tt):
    cb = SCAN_CB
    ar = jnp.broadcast_to(a_ref[:, :cb], (SCAN_SEGS, cb))
    ai = jnp.broadcast_to(a_ref[:, cb:], (SCAN_SEGS, cb))
    ngroups = tt // SCAN_SEGS

    def step(i, st):
        sr, si = st
        r0 = pl.multiple_of(((ngroups - 1 - i) if reverse else i) * SCAN_SEGS, SCAN_SEGS)
        blk = bu_ref[pl.ds(r0, SCAN_SEGS), :]
        nr = ar * sr - ai * si + blk[:, :cb]
        ni = ar * si + ai * sr + blk[:, cb:]
        if o_ref is not None:
            o_ref[pl.ds(r0, SCAN_SEGS), :] = jnp.concatenate([nr, ni], axis=1)
        return nr, ni

    return lax.fori_loop(0, ngroups, step, state, unroll=4)


def _segment_entries(a_ref, e_ref, init_ref, *, reverse, seg_len):
    cb = SCAN_CB
    n_sq = seg_len.bit_length() - 1
    assert 1 << n_sq == seg_len, seg_len
    pr, pi = a_ref[:, :cb], a_ref[:, cb:]
    for _ in range(n_sq):
        pr, pi = pr * pr - pi * pi, 2.0 * pr * pi
    cr = jnp.zeros((1, cb), F32)
    ci = jnp.zeros((1, cb), F32)
    order = range(SCAN_SEGS - 1, -1, -1) if reverse else range(SCAN_SEGS)
    for k, seg in enumerate(order):
        if k > 0:
            prev = seg + 1 if reverse else seg - 1
            er, ei = e_ref[prev:prev + 1, :cb], e_ref[prev:prev + 1, cb:]
            cr, ci = pr * cr - pi * ci + er, pr * ci + pi * cr + ei
        init_ref[seg:seg + 1, :] = jnp.concatenate([cr, ci], axis=1)


def _ssm_specs(nt, tt, nch, reverse):
    cb = SCAN_CB
    tmap = (lambda j, kk: (nt - 1 - kk, j)) if reverse else (lambda j, kk: (kk, j))
    return dict(a=pl.BlockSpec((1, 2 * cb), lambda j, kk: (0, j)),
                seg=pl.BlockSpec((SCAN_SEGS, 2 * cb), lambda j, kk: (0, j)),
                chan=pl.BlockSpec((tt, nch), tmap),
                state=pl.BlockSpec((tt, 2 * cb), tmap),
                b=pl.BlockSpec((None, nch, 2 * cb), lambda j, kk: (j, 0, 0)),
                c=pl.BlockSpec((None, 2 * cb, nch), lambda j, kk: (j, 0, 0)))


def _ssm_ends(a_lay, x, blocks, *, transpose, reverse, tt, name):
    rows = x.shape[0]
    nblk = blocks.shape[0]
    nch = x.shape[1] // nblk
    cb = SCAN_CB
    nt = rows // tt
    sp = _ssm_specs(nt, tt, nch, reverse)

    def body(a_ref, x_ref, w_ref, e_ref, bu_ref):
        kk = pl.program_id(1)

        @pl.when(kk == 0)
        def _():
            e_ref[...] = jnp.zeros_like(e_ref)

        bu_ref[...] = _dot(x_ref[...].astype(BF16), w_ref[...].astype(BF16), 1, 1 if transpose else 0)
        sr, si = _scan_groups(a_ref, bu_ref, None, (e_ref[:, :cb], e_ref[:, cb:]), reverse=reverse, tt=tt)
        e_ref[...] = jnp.concatenate([sr, si], axis=1)

    return pl.pallas_call(
        body, name=name, grid=(nblk, nt),
        in_specs=[sp["a"], sp["chan"], sp["c"] if transpose else sp["b"]],
        out_specs=sp["seg"],
        out_shape=jax.ShapeDtypeStruct((SCAN_SEGS, nblk * 2 * cb), F32),
        scratch_shapes=[pltpu.VMEM((tt, 2 * cb), F32)],
        compiler_params=_params(("parallel", "arbitrary")),
    )(a_lay, x, blocks)


def _ssm_fwd(a_lay, u, b_blk, c_blk, ends, *, tt, name):
    rows = u.shape[0]
    nblk = b_blk.shape[0]
    nch = u.shape[1] // nblk
    cb = SCAN_CB
    nt = rows // tt
    sp = _ssm_specs(nt, tt, nch, False)

    def body(a_ref, e_ref, u_ref, b_ref, c_ref, s_ref, y_ref, init_ref, carry_ref):
        kk = pl.program_id(1)

        @pl.when(kk == 0)
        def _():
            _segment_entries(a_ref, e_ref, init_ref, reverse=False, seg_len=rows // SCAN_SEGS)
            carry_ref[...] = init_ref[...]

        s_ref[...] = _dot(u_ref[...].astype(BF16), b_ref[...].astype(BF16), 1, 0)
        sr, si = _scan_groups(a_ref, s_ref, s_ref, (carry_ref[:, :cb], carry_ref[:, cb:]), reverse=False, tt=tt)
        carry_ref[...] = jnp.concatenate([sr, si], axis=1)
        y_ref[...] = _dot(s_ref[...].astype(BF16), c_ref[...].astype(BF16), 1, 0)

    return pl.pallas_call(
        body, name=name, grid=(nblk, nt),
        in_specs=[sp["a"], sp["seg"], sp["chan"], sp["b"], sp["c"]],
        out_specs=[sp["state"], sp["chan"], sp["seg"]],
        out_shape=[jax.ShapeDtypeStruct((rows, nblk * 2 * cb), F32), jax.ShapeDtypeStruct((rows, nblk * nch), F32),
                   jax.ShapeDtypeStruct((SCAN_SEGS, nblk * 2 * cb), F32)],
        scratch_shapes=[pltpu.VMEM((SCAN_SEGS, 2 * cb), F32)],
        compiler_params=_params(("parallel", "arbitrary")),
    )(a_lay, ends, u, b_blk, c_blk)


def _ssm_bwd(a_conj, dy, u, s, s_entry, b_blk, c_blk, dd, ends, *, tt, name):
    rows = u.shape[0]
    nblk = b_blk.shape[0]
    nch = u.shape[1] // nblk
    cb = SCAN_CB
    nt = rows // tt
    sp = _ssm_specs(nt, tt, nch, True)
    groups_per_tile = tt // SCAN_SEGS
    before = pl.BlockSpec((SCAN_SEGS, 2 * cb), lambda j, kk: (jnp.maximum((nt - 1 - kk) * groups_per_tile - 1, 0), j))

    def body(a_ref, e_ref, dy_ref, u_ref, s_ref, before_ref, entry_ref, b_ref, c_ref, dd_ref,
             du_ref, db_ref, dc_ref, da_ref, lam_ref, carry_ref):
        kk = pl.program_id(1)

        @pl.when(kk == 0)
        def _():
            _segment_entries(a_ref, e_ref, carry_ref, reverse=True, seg_len=rows // SCAN_SEGS)
            db_ref[...] = jnp.zeros_like(db_ref)
            dc_ref[...] = jnp.zeros_like(dc_ref)
            da_ref[...] = jnp.zeros_like(da_ref)

        dyv = dy_ref[...]
        dyb = dyv.astype(BF16)
        lam_ref[...] = _dot(dyb, c_ref[...].astype(BF16), 1, 1)
        lr, li = _scan_groups(a_ref, lam_ref, lam_ref, (carry_ref[:, :cb], carry_ref[:, cb:]), reverse=True, tt=tt)
        carry_ref[...] = jnp.concatenate([lr, li], axis=1)

        first = jnp.where(kk == nt - 1, entry_ref[...], before_ref[...])
        rest = tt - SCAN_SEGS
        lam_hi = lam_ref[pl.ds(SCAN_SEGS, rest), :]
        s_lo = s_ref[pl.ds(0, rest), :]
        lam_lo = lam_ref[pl.ds(0, SCAN_SEGS), :]

        def pair(lv, pv):
            lre, lim, pre, pim = lv[:, :cb], lv[:, cb:], pv[:, :cb], pv[:, cb:]
            return (jnp.sum(lre * pre + lim * pim, axis=0, keepdims=True),
                    jnp.sum(lim * pre - lre * pim, axis=0, keepdims=True))

        r1, i1 = pair(lam_hi, s_lo)
        r0, i0 = pair(lam_lo, first)
        da_ref[...] += jnp.concatenate([r1 + r0, i1 + i0], axis=1)

        lamb = lam_ref[...].astype(BF16)
        du_ref[...] = _dot(lamb, b_ref[...].astype(BF16), 1, 1) + dd_ref[...] * dyv
        db_ref[...] += _dot(u_ref[...].astype(BF16), lamb, 0, 0)
        dc_ref[...] += _dot(s_ref[...].astype(BF16), dyb, 0, 0)

    return pl.pallas_call(
        body, name=name, grid=(nblk, nt),
        in_specs=[sp["a"], sp["seg"], sp["chan"], sp["chan"], sp["state"], before, sp["seg"], sp["b"], sp["c"],
                  pl.BlockSpec((1, nch), lambda j, kk: (0, j))],
        out_specs=[sp["chan"], sp["b"], sp["c"], pl.BlockSpec((1, 2 * cb), lambda j, kk: (0, j))],
        out_shape=[jax.ShapeDtypeStruct((rows, nblk * nch), F32), jax.ShapeDtypeStruct(b_blk.shape, F32),
                   jax.ShapeDtypeStruct(c_blk.shape, F32), jax.ShapeDtypeStruct((1, nblk * 2 * cb), F32)],
        scratch_shapes=[pltpu.VMEM((tt, 2 * cb), F32), pltpu.VMEM((SCAN_SEGS, 2 * cb), F32)],
        compiler_params=_params(("parallel", "arbitrary")),
    )(a_conj, ends, dy, u, s, s, s_entry, b_blk, c_blk, dd)


def _glu_fwd(ys, u, dd, w_glu, b_glu, *, tm, name):
    rows, w = ys.shape

    def body(ys_ref, u_ref, dd_ref, w_ref, b_ref, y0_ref, t_ref, y2_ref):
        y0 = ys_ref[...] + dd_ref[...] * u_ref[...]
        y1 = _gelu(y0)
        t = _dot(y1.astype(BF16), w_ref[...], 1, 0) + b_ref[...]
        y0_ref[...] = y0
        t_ref[...] = t
        y2_ref[...] = (y1 * _sigmoid(t)).astype(BF16)

    row = pl.BlockSpec((tm, w), lambda i: (i, 0))
    vec = pl.BlockSpec((1, w), lambda i: (0, 0))
    return pl.pallas_call(
        body, name=name, grid=(rows // tm,),
        in_specs=[row, row, vec, pl.BlockSpec((w, w), lambda i: (0, 0)), vec],
        out_specs=[row, row, row],
        out_shape=[jax.ShapeDtypeStruct((rows, w), F32), jax.ShapeDtypeStruct((rows, w), F32),
                   jax.ShapeDtypeStruct((rows, w), BF16)],
        compiler_params=_params(("parallel",)),
    )(ys, u, dd, w_glu, b_glu)


def _glu_bwd(dy2, y0, t, u, w_glu, *, tm, name):
    rows, w = y0.shape

    def body(dy2_ref, y0_ref, t_ref, u_ref, w_ref, dy0_ref, dt_ref, y1_ref, db_ref, dd_ref):
        i = pl.program_id(0)
        y0 = y0_ref[...]
        y1 = _gelu(y0)
        sg = _sigmoid(t_ref[...])
        dy2v = dy2_ref[...]
        dt = dy2v * y1 * sg * (1.0 - sg)
        dy1 = dy2v * sg + _dot(dt.astype(BF16), w_ref[...], 1, 1)
        dy0 = dy1 * _gelu_grad(y0)
        dy0_ref[...] = dy0
        dt_ref[...] = dt.astype(BF16)
        y1_ref[...] = y1.astype(BF16)

        @pl.when(i == 0)
        def _():
            db_ref[...] = jnp.zeros_like(db_ref)
            dd_ref[...] = jnp.zeros_like(dd_ref)

        db_ref[...] += jnp.sum(dt, axis=0, keepdims=True)
        dd_ref[...] += jnp.sum(dy0 * u_ref[...], axis=0, keepdims=True)

    row = pl.BlockSpec((tm, w), lambda i: (i, 0))
    vec = pl.BlockSpec((1, w), lambda i: (0, 0))
    return pl.pallas_call(
        body, name=name, grid=(rows // tm,),
        in_specs=[row, row, row, row, pl.BlockSpec((w, w), lambda i: (0, 0))],
        out_specs=[row, row, row, vec, vec],
        out_shape=[jax.ShapeDtypeStruct((rows, w), F32), jax.ShapeDtypeStruct((rows, w), BF16),
                   jax.ShapeDtypeStruct((rows, w), BF16), jax.ShapeDtypeStruct((1, w), F32),
                   jax.ShapeDtypeStruct((1, w), F32)],
        compiler_params=_params(("arbitrary",)),
    )(dy2, y0, t, u, w_glu)


ATTN_TILE = 2048


def _attn_geometry(rows, d):
    sb = ATTN_Q * d
    tr = max(sb, min(ATTN_TILE, rows))
    assert rows % tr == 0 and tr % sb == 0, (rows, d)
    return sb, tr, rows // tr, tr // sb


def _attn_masks():
    qi = lax.broadcasted_iota(jnp.int32, (2 * ATTN_Q, 2 * ATTN_Q), 0) % ATTN_Q
    kj = lax.broadcasted_iota(jnp.int32, (2 * ATTN_Q, 2 * ATTN_Q), 1)
    own_ok = jnp.logical_and(kj >= ATTN_Q, kj - ATTN_Q <= qi)
    prev_ok = jnp.logical_and(kj < ATTN_Q, kj >= qi)
    bias_first = jnp.where(own_ok, 0.0, NEG_INF)
    bias_other = jnp.where(jnp.logical_or(own_ok, prev_ok), 0.0, NEG_INF)
    head0 = lax.broadcasted_iota(jnp.int32, (ATTN_Q, LANES), 1) < ATTN_HEAD_DIM
    return bias_first, bias_other, head0


def _attn_rows(base, n, d):
    return pl.ds(pl.multiple_of(base, ATTN_Q), n) if d == 1 else pl.ds(base, n, stride=d)


def _stack_heads(v, head0):
    return jnp.concatenate([jnp.where(head0, v, 0.0), jnp.where(head0, 0.0, v)], axis=0)


def _unstack_heads(v, head0):
    return jnp.where(head0, v[:ATTN_Q], v[ATTN_Q:])


def _fill_keys(buf, prev_ref, cur_ref, sb):
    buf[pl.ds(0, sb), :] = prev_ref[...]
    buf[pl.ds(sb, cur_ref.shape[0]), :] = cur_ref[...]


def _attn_fwd(qkv, g, d, *, name):
    rows = qkv.shape[0]
    sb, tr, ntiles, nsub = _attn_geometry(rows, d)
    qc, kc, vc = 2 * g, 6 + 2 * g, 12 + 2 * g
    scale = ATTN_HEAD_DIM ** -0.5

    def body(q_ref, kc_ref, kp_ref, vc_ref, vp_ref, o_ref, lse_ref, kbuf, vbuf):
        n = pl.program_id(0)
        _fill_keys(kbuf, kp_ref, kc_ref, sb)
        _fill_keys(vbuf, vp_ref, vc_ref, sb)
        bias_first, bias_other, head0 = _attn_masks()

        def per_block(idx, carry):
            j, r = idx // d, idx % d
            base = j * sb + r
            bias = jnp.where(jnp.logical_and(n == 0, j == 0), bias_first, bias_other)
            qrows = _attn_rows(base, ATTN_Q, d)
            krows = _attn_rows(base, 2 * ATTN_Q, d)
            qs = _stack_heads(q_ref[qrows, :], head0).astype(BF16)
            s = _dot(qs, kbuf[krows, :].astype(BF16), 1, 1) * scale + bias
            mx = jnp.max(s, axis=-1, keepdims=True)
            p = jnp.exp(s - mx)
            den = jnp.sum(p, axis=-1, keepdims=True)
            pv = _dot(p.astype(BF16), vbuf[krows, :].astype(BF16), 1, 0) / den
            o_ref[qrows, :] = _unstack_heads(pv, head0)
            lse_ref[qrows, :] = _unstack_heads(jnp.broadcast_to(mx + jnp.log(den), (2 * ATTN_Q, LANES)), head0)
            return carry

        lax.fori_loop(0, nsub * d, per_block, 0, unroll=8)

    def cur(col):
        return pl.BlockSpec((tr, LANES), lambda n, hp: (n, col + hp))

    def prev(col):
        return pl.BlockSpec((sb, LANES), lambda n, hp: (jnp.maximum(n * nsub - 1, 0), col + hp))

    out_spec = pl.BlockSpec((tr, LANES), lambda n, hp: (n, hp))
    return pl.pallas_call(
        body, name=name, grid=(ntiles, 2),
        in_specs=[cur(qc), cur(kc), prev(kc), cur(vc), prev(vc)],
        out_specs=[out_spec, out_spec],
        out_shape=[jax.ShapeDtypeStruct((rows, 2 * LANES), F32), jax.ShapeDtypeStruct((rows, 2 * LANES), F32)],
        scratch_shapes=[pltpu.VMEM((sb + tr, LANES), F32), pltpu.VMEM((sb + tr, LANES), F32)],
        compiler_params=_params(("parallel", "parallel")),
    )(qkv, qkv, qkv, qkv, qkv)


def _attn_merge(outs, lses, *, tm, name):
    rows, w = outs[0].shape

    def body(o0, o1, o2, l0, l1, l2, o_ref, lse_ref):
        a0, a1, a2 = l0[...], l1[...], l2[...]
        mx = jnp.maximum(jnp.maximum(a0, a1), a2)
        e0, e1, e2 = jnp.exp(a0 - mx), jnp.exp(a1 - mx), jnp.exp(a2 - mx)
        den = e0 + e1 + e2
        o_ref[...] = (e0 / den) * o0[...] + (e1 / den) * o1[...] + (e2 / den) * o2[...]
        lse_ref[...] = mx + jnp.log(den)

    row = pl.BlockSpec((tm, w), lambda i: (i, 0))
    return pl.pallas_call(
        body, name=name, grid=(rows // tm,), in_specs=[row] * 6, out_specs=[row, row],
        out_shape=[jax.ShapeDtypeStruct((rows, w), F32), jax.ShapeDtypeStruct((rows, w), F32)],
        compiler_params=_params(("parallel",)),
    )(*outs, *lses)


def _attn_bwd(qkv, do, o, lse, g, d, prev, *, name):
    rows = qkv.shape[0]
    sb, tr, ntiles, nsub = _attn_geometry(rows, d)
    qc, kc, vc = 2 * g, 6 + 2 * g, 12 + 2 * g
    scale = ATTN_HEAD_DIM ** -0.5

    def body(q_ref, kc_ref, kp_ref, vc_ref, vp_ref, do_ref, o_ref, lse_ref, dq_ref, dk_ref, dv_ref,
             kbuf, vbuf, dk_acc, dv_acc):
        n = pl.program_id(1)

        @pl.when(n == 0)
        def _():
            dk_acc[pl.ds(0, tr), :] = jnp.zeros((tr, LANES), F32)
            dv_acc[pl.ds(0, tr), :] = jnp.zeros((tr, LANES), F32)

        @pl.when(n < ntiles)
        def _():
            dk_acc[pl.ds(tr, tr), :] = jnp.zeros((tr, LANES), F32)
            dv_acc[pl.ds(tr, tr), :] = jnp.zeros((tr, LANES), F32)
            _fill_keys(kbuf, kp_ref, kc_ref, sb)
            _fill_keys(vbuf, vp_ref, vc_ref, sb)
            bias_first, bias_other, head0 = _attn_masks()
            lane = lax.broadcasted_iota(jnp.int32, (ATTN_Q, LANES), 1)

            def per_block(idx, carry):
                j, r = idx // d, idx % d
                base = j * sb + r
                bias = jnp.where(jnp.logical_and(n == 0, j == 0), bias_first, bias_other)
                qrows = _attn_rows(base, ATTN_Q, d)
                krows = _attn_rows(base, 2 * ATTN_Q, d)
                arows = _attn_rows(base + (tr - sb), 2 * ATTN_Q, d)
                qs = _stack_heads(q_ref[qrows, :], head0).astype(BF16)
                dos = _stack_heads(do_ref[qrows, :], head0)
                dosb = dos.astype(BF16)
                ov = o_ref[qrows, :]
                delta = jnp.sum(dos * jnp.concatenate([ov, ov], axis=0), axis=-1, keepdims=True)
                lsev = lse_ref[qrows, :]
                lse_s = jnp.concatenate(
                    [jnp.sum(jnp.where(lane == h * ATTN_HEAD_DIM, lsev, 0.0), axis=-1, keepdims=True) for h in range(2)], axis=0)
                kb = kbuf[krows, :].astype(BF16)
                vb = vbuf[krows, :].astype(BF16)
                p = jnp.exp(_dot(qs, kb, 1, 1) * scale + bias - lse_s)
                ds = (p * (_dot(dosb, vb, 1, 1) - delta) * scale).astype(BF16)
                dq_ref[qrows, :] = _unstack_heads(_dot(ds, kb, 1, 0), head0)
                dk_acc[arows, :] += _dot(ds, qs, 0, 0)
                dv_acc[arows, :] += _dot(p.astype(BF16), dosb, 0, 0)
                return carry

            lax.fori_loop(0, nsub * d, per_block, 0, unroll=4)

        dk_ref[...] = dk_acc[pl.ds(0, tr), :]
        dv_ref[...] = dv_acc[pl.ds(0, tr), :]
        dk_acc[pl.ds(0, tr), :] = dk_acc[pl.ds(tr, tr), :]
        dv_acc[pl.ds(0, tr), :] = dv_acc[pl.ds(tr, tr), :]

    def cur(n):
        return jnp.minimum(n, ntiles - 1)

    def spec(col, prev):
        if prev:
            return pl.BlockSpec((sb, LANES), lambda hp, n: (jnp.maximum(cur(n) * nsub - 1, 0), col + hp))
        return pl.BlockSpec((tr, LANES), lambda hp, n: (cur(n), col + hp))

    row_spec = pl.BlockSpec((tr, LANES), lambda hp, n: (cur(n), hp))
    dq_out = pl.BlockSpec((tr, LANES), lambda hp, n: (cur(n), 2 * g + hp))
    kv_out = pl.BlockSpec((tr, LANES), lambda hp, n: (jnp.maximum(n - 1, 0), 2 * g + hp))
    shape = jax.ShapeDtypeStruct((rows, len(ATTN_PATTERNS) * 2 * LANES), F32)
    ins = [qkv, qkv, qkv, qkv, qkv, do, o, lse]
    in_specs = [spec(qc, False), spec(kc, False), spec(kc, True), spec(vc, False), spec(vc, True),
                row_spec, row_spec, row_spec]
    aliases = {}
    if prev is not None:
        aliases = {len(ins) + t: t for t in range(3)}
        ins = ins + list(prev)
        in_specs = in_specs + [ANY] * 3
    n_in = len(ins)

    def entry(*refs):
        body(*refs[:8], *refs[n_in:])

    return pl.pallas_call(
        entry, name=name, grid=(2, ntiles + 1),
        in_specs=in_specs,
        out_specs=[dq_out, kv_out, kv_out],
        out_shape=[shape, shape, shape],
        input_output_aliases=aliases,
        scratch_shapes=[pltpu.VMEM((sb + tr, LANES), F32), pltpu.VMEM((sb + tr, LANES), F32),
                        pltpu.VMEM((2 * tr, LANES), F32), pltpu.VMEM((2 * tr, LANES), F32)],
        compiler_params=_params(("parallel", "arbitrary")),
    )(*ins)


def _mem_probs(q, k):
    s = _dot(q.astype(BF16), k.astype(BF16), 1, 1) * (MEM_HEAD_DIM ** -0.5)
    e = jnp.exp(s - jnp.max(s, axis=-1, keepdims=True))
    return e / jnp.sum(e, axis=-1, keepdims=True)


def _mem_attn_fwd(mq, kv, *, tq, name):
    rows = mq.shape[0]

    def body(q_ref, k_ref, v_ref, o_ref):
        p = _mem_probs(q_ref[...], k_ref[...])
        o_ref[...] = _dot(p.astype(BF16), v_ref[...].astype(BF16), 1, 0)

    return pl.pallas_call(
        body, name=name, grid=(rows // tq, MEM_HEADS),
        in_specs=[pl.BlockSpec((tq, LANES), lambda i, h: (i, h)),
                  pl.BlockSpec((MEM_LEN, LANES), lambda i, h: (0, h)),
                  pl.BlockSpec((MEM_LEN, LANES), lambda i, h: (0, MEM_HEADS + h))],
        out_specs=pl.BlockSpec((tq, LANES), lambda i, h: (i, h)),
        out_shape=jax.ShapeDtypeStruct((rows, MEM_HEADS * LANES), F32),
        compiler_params=_params(("parallel", "parallel")),
    )(mq, kv, kv)


def _mem_attn_bwd(mq, kv, dmo, *, tq, name):
    rows = mq.shape[0]
    scale = MEM_HEAD_DIM ** -0.5

    def body(q_ref, k_ref, v_ref, do_ref, dq_ref, dk_ref, dv_ref):
        i = pl.program_id(1)
        qb = q_ref[...].astype(BF16)
        kb = k_ref[...].astype(BF16)
        vb = v_ref[...].astype(BF16)
        dob = do_ref[...].astype(BF16)
        p = _mem_probs(q_ref[...], k_ref[...])
        dp = _dot(dob, vb, 1, 1)
        ds = (p * (dp - jnp.sum(p * dp, axis=-1, keepdims=True)) * scale).astype(BF16)
        dq_ref[...] = _dot(ds, kb, 1, 0).astype(dq_ref.dtype)

        @pl.when(i == 0)
        def _():
            dk_ref[...] = jnp.zeros_like(dk_ref)
            dv_ref[...] = jnp.zeros_like(dv_ref)

        dk_ref[...] += _dot(ds, qb, 0, 0)
        dv_ref[...] += _dot(p.astype(BF16), dob, 0, 0)

    kv_out = pl.BlockSpec((MEM_LEN, LANES), lambda h, i: (0, h))
    kv_shape = jax.ShapeDtypeStruct((MEM_LEN, MEM_HEADS * LANES), F32)
    return pl.pallas_call(
        body, name=name, grid=(MEM_HEADS, rows // tq),
        in_specs=[pl.BlockSpec((tq, LANES), lambda h, i: (i, h)),
                  pl.BlockSpec((MEM_LEN, LANES), lambda h, i: (0, h)),
                  pl.BlockSpec((MEM_LEN, LANES), lambda h, i: (0, MEM_HEADS + h)),
                  pl.BlockSpec((tq, LANES), lambda h, i: (i, h))],
        out_specs=[pl.BlockSpec((tq, LANES), lambda h, i: (i, h)), kv_out, kv_out],
        out_shape=[jax.ShapeDtypeStruct((rows, MEM_HEADS * LANES), BF16), kv_shape, kv_shape],
        compiler_params=_params(("parallel", "arbitrary")),
    )(mq, kv, kv, dmo)


def _resident(shape):
    return pl.BlockSpec(shape, lambda i: (0, 0), pipeline_mode=pl.Buffered(1))


def _branch_merge_fwd(acts, wts, zg, b_gate, *, tm, name):
    rows = zg.shape[0]
    d = wts[0].shape[0]

    def body(s_ref, a_ref, m_ref, ws_ref, wa_ref, wm_ref, zg_ref, b_ref, o_ref):
        gt = _sigmoid(zg_ref[...] + b_ref[...])
        acc = None
        for k, (x_ref, w_ref) in enumerate(((s_ref, ws_ref), (a_ref, wa_ref), (m_ref, wm_ref))):
            term = gt[:, k * d:(k + 1) * d] * _dot(x_ref[...].astype(BF16), w_ref[...], 1, 1)
            acc = term if acc is None else acc + term
        o_ref[...] = acc.astype(BF16)

    return pl.pallas_call(
        body, name=name, grid=(rows // tm,),
        in_specs=[pl.BlockSpec((tm, x.shape[1]), lambda i: (i, 0)) for x in acts] + [_resident(w.shape) for w in wts]
        + [pl.BlockSpec((tm, 3 * d), lambda i: (i, 0)), pl.BlockSpec((1, 3 * d), lambda i: (0, 0))],
        out_specs=pl.BlockSpec((tm, d), lambda i: (i, 0)), out_shape=jax.ShapeDtypeStruct((rows, d), BF16),
        compiler_params=_params(("parallel",)),
    )(*acts, *wts, zg, b_gate)


def _branch_merge_bwd(dmerged, acts, wts, zg, b_gate, *, tm, name, carry=None):
    rows = zg.shape[0]
    d = wts[0].shape[0]

    def body(dm_ref, s_ref, a_ref, m_ref, ws_ref, wa_ref, wm_ref, zg_ref, b_ref,
             ds_ref, da_ref, dmm_ref, dws_ref, dwa_ref, dwm_ref, dzg_ref, db_ref):
        i = pl.program_id(0)

        @pl.when(i == 0)
        def _():
            for r in (dws_ref, dwa_ref, dwm_ref, db_ref):
                r[...] = jnp.zeros_like(r)

        gt = _sigmoid(zg_ref[...] + b_ref[...])
        dm = dm_ref[...]
        groups = ((s_ref, ws_ref, ds_ref, dws_ref), (a_ref, wa_ref, da_ref, dwa_ref), (m_ref, wm_ref, dmm_ref, dwm_ref))
        for k, (x_ref, w_ref, dx_ref, dw_ref) in enumerate(groups):
            cs = pl.ds(k * d, d)
            gk = gt[:, k * d:(k + 1) * d]
            xb = x_ref[...].astype(BF16)
            br = _dot(xb, w_ref[...], 1, 1)
            dbr = (dm * gk).astype(BF16)
            dx_ref[...] = _dot(dbr, w_ref[...], 1, 0)
            dw_ref[...] += _dot(dbr, xb, 0, 0)
            dzg = dm * br * gk * (1.0 - gk)
            dzg_ref[:, cs] = dzg.astype(BF16)
            db_ref[:, cs] += jnp.sum(dzg, axis=0, keepdims=True)

    row = lambda w: pl.BlockSpec((tm, w), lambda i: (i, 0))
    whole = lambda shape: pl.BlockSpec(shape, lambda i: (0, 0))
    res = _call_with_carry(
        body, carry, name=name, grid=(rows // tm,),
        in_specs=[row(d)] + [row(x.shape[1]) for x in acts] + [_resident(w.shape) for w in wts] + [row(3 * d), whole((1, 3 * d))],
        out_specs=[row(x.shape[1]) for x in acts] + [whole(w.shape) for w in wts] + [row(3 * d), whole((1, 3 * d))],
        out_shape=[jax.ShapeDtypeStruct(x.shape, F32) for x in acts] + [jax.ShapeDtypeStruct(w.shape, F32) for w in wts]
        + [jax.ShapeDtypeStruct((rows, 3 * d), BF16), jax.ShapeDtypeStruct((1, 3 * d), F32)],
        scratch=[], operands=[dmerged, *acts, *wts, zg, b_gate], semantics=("arbitrary",))
    return tuple(res) if carry is None else (tuple(res[:8]), list(res[8:]))


def _adamw(w, g, m, v, *, tr, name):
    rows, cols = w.shape[-2:]
    assert rows % tr == 0, (name, rows, tr)

    def body(w_ref, g_ref, m_ref, v_ref, g_out, d_ref, nm_ref, nv_ref):
        gv = g_ref[...]
        m2 = ADAM_B1 * m_ref[...] + (1.0 - ADAM_B1) * gv
        v2 = ADAM_B2 * v_ref[...] + (1.0 - ADAM_B2) * (gv * gv)
        m_hat = m2 / (1.0 - ADAM_B1 ** ADAM_STEP)
        v_hat = v2 / (1.0 - ADAM_B2 ** ADAM_STEP)
        g_out[...] = gv
        d_ref[...] = -ADAM_LR * (m_hat / (jnp.sqrt(v_hat) + ADAM_EPS) + ADAM_WD * w_ref[...])
        nm_ref[...] = m2
        nv_ref[...] = v2

    flat = pl.BlockSpec((tr, cols), lambda i: (i, 0))
    blk = flat if w.ndim == 2 else pl.BlockSpec((None, tr, cols), lambda i: (0, i, 0))
    shape = jax.ShapeDtypeStruct(w.shape, F32)
    return pl.pallas_call(
        body, name=name, grid=(rows // tr,), in_specs=[blk, flat, blk, blk], out_specs=[blk] * 4,
        out_shape=[shape] * 4, compiler_params=_params(("parallel",)),
    )(w, g, m, v)


ANY = pl.BlockSpec(memory_space=pl.ANY)


def _position():
    return lax.axis_index("x"), lax.axis_index("y"), lax.axis_index("c")


def _other_chips(x, y):
    return ((1 - x, y), (x, 1 - y), (1 - x, 1 - y))


def _remote(src, dst, send_sem, recv_sem, dev):
    return pltpu.make_async_remote_copy(src_ref=src, dst_ref=dst, send_sem=send_sem, recv_sem=recv_sem,
                                        device_id=dev, device_id_type=MESH)


def _gather_exchange(shards):
    nb = len(shards)

    def rows_of(i, owner, core):
        rs = shards[i].shape[0]
        return pl.ds(pl.multiple_of(owner * rs + core * (rs // 2), 16), rs // 2)

    def first_leg(ins, outs, send_sems, recv_sems, i, j):
        x, y, c = _position()
        px, py = _other_chips(x, y)[j]
        half = shards[i].shape[0] // 2
        mine = ins[i].at[pl.ds(pl.multiple_of(c * half, 16), half)]
        return _remote(mine, outs[i].at[rows_of(i, 2 * x + y, c)], send_sems.at[i, j], recv_sems.at[i, j], (px, py, c))

    def passed_on(outs, send_sems, recv_sems, i, j, core):
        x, y, c = _position()
        px, py = _other_chips(x, y)[j]
        rows = outs[i].at[rows_of(i, 2 * px + py, core)]
        return _remote(rows, rows, send_sems.at[i, 3 + j], recv_sems.at[i, 3 + j], (x, y, 1 - c))

    def own_block(ins, outs, send_sems, recv_sems, i):
        x, y, c = _position()
        rs = shards[i].shape[0]
        place = outs[i].at[pl.ds(pl.multiple_of((2 * x + y) * rs, 16), rs)]
        return _remote(ins[i], place, send_sems.at[i, 6], recv_sems.at[i, 6], (x, y, 1 - c))

    def start(ins, outs, send_sems, recv_sems):
        for i in range(nb):
            own_block(ins, outs, send_sems, recv_sems, i).start()
            for j in range(3):
                first_leg(ins, outs, send_sems, recv_sems, i, j).start()

    def finish(ins, outs, send_sems, recv_sems):
        x, y, c = _position()
        for i in range(nb):
            for j, (px, py) in enumerate(_other_chips(x, y)):
                landed = outs[i].at[rows_of(i, 2 * px + py, c)]
                _remote(landed, landed, send_sems.at[i, j], recv_sems.at[i, j], (px, py, c)).wait_recv()
                passed_on(outs, send_sems, recv_sems, i, j, c).start()
        for i in range(nb):
            own_block(ins, outs, send_sems, recv_sems, i).wait()
            for j in range(3):
                passed_on(outs, send_sems, recv_sems, i, j, 1 - c).wait_recv()
        for i in range(nb):
            for j in range(3):
                first_leg(ins, outs, send_sems, recv_sems, i, j).wait_send()
                passed_on(outs, send_sems, recv_sems, i, j, c).wait_send()

    return _Exchange(ins=list(shards), outs=[jax.ShapeDtypeStruct((N_CHIPS * s.shape[0], s.shape[1]), s.dtype) for s in shards],
                     aliases={}, sems=[(nb, 7), (nb, 7)], start=start, finish=finish)


def _run_exchange(ex, *, name):
    n_in, n_out = len(ex.ins), len(ex.outs)

    def body(*refs):
        c_in, c_out, sems = refs[:n_in], refs[n_in:n_in + n_out], refs[n_in + n_out:]
        ex.start(c_in, c_out, *sems)
        ex.finish(c_in, c_out, *sems)

    return pl.pallas_call(
        body, name=name, in_specs=[ANY] * n_in, out_specs=[ANY] * n_out, out_shape=list(ex.outs),
        input_output_aliases=dict(ex.aliases),
        scratch_shapes=[pltpu.SemaphoreType.DMA(s) for s in ex.sems],
    )(*ex.ins)


def _row_tile(rows):
    return max(t for t in range(16, min(rows, 512) + 1, 16) if rows % t == 0)


def _halves_exchange(grads):
    nb = len(grads)

    def copies(ins, outs, send_sems, recv_sems):
        x, y, c = _position()
        return [_remote(ins[i].at[:, 1 - c], outs[i], send_sems.at[i], recv_sems.at[i], (x, y, 1 - c)) for i in range(nb)]

    def start(ins, outs, send_sems, recv_sems):
        for cp in copies(ins, outs, send_sems, recv_sems):
            cp.start()

    def finish(ins, outs, send_sems, recv_sems):
        for cp in copies(ins, outs, send_sems, recv_sems):
            cp.wait()

    return _Exchange(ins=list(grads), outs=[jax.ShapeDtypeStruct((N_CHIPS, g.shape[2], g.shape[3]), F32) for g in grads],
                     aliases={}, sems=[(nb,), (nb,)], start=start, finish=finish)


def _join_exchanges(parts):
    assert all(not ex.aliases for ex in parts)

    def split(refs, counts):
        out, at = [], 0
        for k in counts:
            out.append(refs[at:at + k])
            at += k
        return out

    def run(which):
        def go(ins, outs, *sems):
            for ex, i, o, s in zip(parts, split(ins, [len(ex.ins) for ex in parts]), split(outs, [len(ex.outs) for ex in parts]),
                                   split(sems, [len(ex.sems) for ex in parts])):
                getattr(ex, which)(i, o, *s)
        return go

    return _Exchange(ins=[a for ex in parts for a in ex.ins], outs=[a for ex in parts for a in ex.outs], aliases={},
                     sems=[s for ex in parts for s in ex.sems], start=run("start"), finish=run("finish"))


def _pair_sum(g4, got, c_arr, *, name):
    _, _, half, cols = g4.shape
    tr = _row_tile(half)

    def body(c_ref, g_ref, t_ref, p_ref, pb_ref):
        sm = g_ref[...] + t_ref[...]
        p_ref[...] = sm
        pb_ref[...] = sm.astype(BF16)

    blk = pl.BlockSpec((None, tr, cols), lambda j, i, c_ref: (j, i, 0))
    grid_spec = pltpu.PrefetchScalarGridSpec(
        num_scalar_prefetch=1, grid=(N_CHIPS, half // tr),
        in_specs=[pl.BlockSpec((None, None, tr, cols), lambda j, i, c_ref: (j, c_ref[0], i, 0)), blk],
        out_specs=[blk, blk])
    return pl.pallas_call(
        body, name=name, grid_spec=grid_spec,
        out_shape=[jax.ShapeDtypeStruct((N_CHIPS, half, cols), F32), jax.ShapeDtypeStruct((N_CHIPS, half, cols), BF16)],
        compiler_params=_params(("parallel", "parallel")),
    )(c_arr, g4, got)


def _scatter_exchange(parts):
    nb = len(parts)

    def copies(ins, outs, send_sems, recv_sems):
        x, y, c = _position()
        return [_remote(ins[i].at[2 * px + py], outs[i].at[j], send_sems.at[i, j], recv_sems.at[i, j], (px, py, c))
                for i in range(nb) for j, (px, py) in enumerate(_other_chips(x, y))]

    def start(ins, outs, send_sems, recv_sems):
        for cp in copies(ins, outs, send_sems, recv_sems):
            cp.start()

    def finish(ins, outs, send_sems, recv_sems):
        for cp in copies(ins, outs, send_sems, recv_sems):
            cp.wait()

    return _Exchange(ins=list(parts), outs=[jax.ShapeDtypeStruct((3,) + p.shape[1:], p.dtype) for p in parts],
                     aliases={}, sems=[(nb, 3), (nb, 3)], start=start, finish=finish)


def _owner_sum(p, got, chip_arr, c_arr, *, replicated, name):
    _, half, cols = p.shape
    tr = _row_tile(half)

    def body(chip_ref, c_ref, p_ref, r_ref, o_ref):
        o_ref[...] = ((p_ref[...] + r_ref[0].astype(F32)) + r_ref[1].astype(F32)) + r_ref[2].astype(F32)

    if replicated:
        out_spec = pl.BlockSpec((None, None, tr, cols), lambda i, chip_ref, c_ref: (chip_ref[0], c_ref[0], i, 0))
        out_shape = jax.ShapeDtypeStruct((N_CHIPS, 2, half, cols), F32)
    else:
        out_spec = pl.BlockSpec((None, tr, cols), lambda i, chip_ref, c_ref: (c_ref[0], i, 0))
        out_shape = jax.ShapeDtypeStruct((2, half, cols), F32)
    grid_spec = pltpu.PrefetchScalarGridSpec(
        num_scalar_prefetch=2, grid=(half // tr,),
        in_specs=[pl.BlockSpec((None, tr, cols), lambda i, chip_ref, c_ref: (chip_ref[0], i, 0)),
                  pl.BlockSpec((3, tr, cols), lambda i, chip_ref, c_ref: (0, i, 0))],
        out_specs=out_spec)
    return pl.pallas_call(
        body, name=name, grid_spec=grid_spec, out_shape=out_shape,
        compiler_params=_params(("parallel",)),
    )(chip_arr, c_arr, p, got)


def _share_reduced(bufs):
    nb = len(bufs) - 1

    def body(*refs):
        outs = refs[nb + 1:2 * nb + 2]
        send_sems, recv_sems = refs[2 * nb + 2:]
        x, y, c = _position()
        chip = 2 * x + y
        sends = []
        for i in range(nb):
            cp = _remote(outs[i].at[c], outs[i].at[c], send_sems.at[i], recv_sems.at[i], (x, y, 1 - c))
            cp.start()
            sends.append(cp)
        small = outs[nb]
        peers = [(fx, fy, fc) for fx in (0, 1) for fy in (0, 1) for fc in (0, 1) if fx + fy + fc > 0]
        for k, (fx, fy, fc) in enumerate(peers):
            dev = (x ^ fx, y ^ fy, c ^ fc)
            cp = _remote(small.at[chip, c], small.at[chip, c], send_sems.at[nb + k], recv_sems.at[nb + k], dev)
            cp.start()
            sends.append(cp)
        for i in range(nb):
            dst = outs[i].at[1 - c]
            _remote(dst, dst, send_sems.at[i], recv_sems.at[i], (x, y, 1 - c)).wait_recv()
        for k, (fx, fy, fc) in enumerate(peers):
            dst = small.at[2 * (x ^ fx) + (y ^ fy), c ^ fc]
            _remote(dst, dst, send_sems.at[nb + k], recv_sems.at[nb + k], (x ^ fx, y ^ fy, c ^ fc)).wait_recv()
        for cp in sends:
            cp.wait_send()

    n_all = nb + 1
    return pl.pallas_call(
        body, name="grad_share_reduced", in_specs=[ANY] * n_all, out_specs=[ANY] * n_all,
        out_shape=[jax.ShapeDtypeStruct(b.shape, b.dtype) for b in bufs],
        input_output_aliases={i: i for i in range(n_all)},
        scratch_shapes=[pltpu.SemaphoreType.DMA((nb + 7,)), pltpu.SemaphoreType.DMA((nb + 7,))],
    )(*bufs)


class _GradReducer:
    def __init__(self, c_arr, chip_arr):
        self.c_arr, self.chip_arr = c_arr, chip_arr
        self.full, self.pairs, self.landed = {}, {}, {}

    def swap(self, names, grads):
        for n, g in zip(names, grads):
            self.full[n] = g.reshape(N_CHIPS, 2, g.shape[0] // (2 * N_CHIPS), g.shape[1])
        return _halves_exchange([self.full[n] for n in names])

    def swapped(self, names, bufs):
        for n, t in zip(names, bufs):
            self.pairs[n] = _pair_sum(self.full[n], t, self.c_arr, name="grad_pair_sum_" + n)

    def scatter(self, names):
        return _scatter_exchange([self.pairs[n][1] for n in names])

    def collect(self, names, bufs):
        self.landed.update(zip(names, bufs))

    def swap_now(self, names, grads):
        self.swapped(names, _run_exchange(self.swap(names, grads), name="grad_exchange_" + names[0]))

    def finish(self, names, grads, order):
        self.swap_now(names, grads)
        self.collect(names, _run_exchange(self.scatter(names), name="grad_scatter_" + names[0]))
        totals = [_owner_sum(self.pairs[n][0], self.landed[n], self.chip_arr, self.c_arr, replicated=(n == order[-1]),
                             name="grad_owner_sum_" + n) for n in order]
        return _share_reduced(totals)


def _pack_small(vals):
    flat = jnp.concatenate([vals[name].reshape(-1) for name, _ in SMALL])
    return jnp.pad(flat, (0, N_CHIPS * SMALL_ROWS * 1024 - SMALL_ELEMS)).reshape(N_CHIPS * SMALL_ROWS, 1024)


def _unpack_small(buf):
    flat = buf.reshape(-1)
    out, off = {}, 0
    for name, shape in SMALL:
        n = int(np.prod(shape))
        out[name] = flat[off:off + n].reshape(shape)
        off += n
    return out


EARLY_REDUCED = (("w_down",), ("w_up",), ("w_o", "w_ssm_br", "w_attn_br", "w_mem_br", "w_glu", "w_mem_kv"), ("w_in",))


def _device_step(x, mem, tgt, w, p, *, shards, reducer):
    rows = x.shape[0]
    w = dict(w)
    early = EARLY_REDUCED
    gb = {}
    gather_pending = shards is not None

    def riding(*stages):
        if reducer is None or not stages:
            return None
        return _join_exchanges([reducer.swap(names, [gb[n] for n in names]) if kind == "swap" else reducer.scatter(names)
                                for kind, names in stages])

    def arrived(stages, res):
        if reducer is None or not stages:
            return res
        main, bufs = res
        for kind, names in stages:
            (reducer.swapped if kind == "swap" else reducer.collect)(names, bufs[:len(names)])
            bufs = bufs[len(names):]
        return main

    def fetching(names):
        return _gather_exchange([shards[n] for n in names]) if gather_pending else None

    def fetched(names, res):
        if not gather_pending:
            return res
        w.update(zip(names, res[1]))
        return res[0]

    first_use = (("w_glu", "w_ssm_br", "w_attn_br", "w_mem_kv", "w_mem_br", "w_o"), ("w_up",), ("w_down",))
    g1, gm, g2 = p["norm1_g"], p["mem_norm_g"], p["norm2_g"]
    gf = p["final_g"].reshape(1, D_MODEL)
    ssm_args = (p["ssm_lambda_re"][0], p["ssm_lambda_im"][0], p["ssm_log_dt"][0], p["ssm_b_re"][0],
                p["ssm_b_im"][0], p["ssm_c_re"][0], p["ssm_c_im"][0])
    (a_lay, b_blk, c_blk), ssm_vjp = jax.vjp(_ssm_matrices, *ssm_args)
    a_conj = a_lay * _to_scan_layout(jnp.stack([jnp.ones((N_STATES,), F32), -jnp.ones((N_STATES,), F32)]))[None, :]
    dd = p["ssm_d"].reshape(1, SSM_WIDTH)
    win_t = w["w_in"]
    mm = _matmul

    n1 = _rmsnorm_fwd(x, g1, tm=512, name="norm1")
    u = mm(n1, win_t, m=rows, n=512, k=1024, tb=True, tm=2048, tn=512, tk=1024, out_dtypes=(F32,), name="in_u")
    qkv = fetched(first_use[0], mm(n1, win_t, m=rows, n=2304, k=1024, tb=True, tm=1024, tn=768, tk=1024,
                                   b_row0=OFF_QKV, out_dtypes=(F32,), carry=fetching(first_use[0]), name="in_qkv"))
    mq = mm(n1, win_t, m=rows, n=512, k=1024, tb=True, tm=2048, tn=512, tk=1024, b_row0=OFF_MQ,
            out_dtypes=(F32,), name="in_mq")
    zg = fetched(first_use[1], mm(n1, win_t, m=rows, n=3072, k=1024, tb=True, tm=1024, tn=1024, tk=1024,
                                  b_row0=OFF_ZG, out_dtypes=(F32,), carry=fetching(first_use[1]), name="in_zg"))

    u_i = _interleave(u)
    ends = _ssm_ends(a_lay, u_i, b_blk, transpose=False, reverse=False, tt=512, name="ssm_fwd_ends")
    s, ys_i, s_entry = _ssm_fwd(a_lay, u_i, b_blk, c_blk, ends, tt=512, name="ssm_fwd")
    ys = _deinterleave(ys_i)
    y0, tglu, y2 = _glu_fwd(ys, u, dd, w["w_glu"], p["b_glu"], tm=512, name="glu_fwd")

    outs, lses = [], []
    for g, (_, d) in enumerate(ATTN_PATTERNS):
        o_g, lse_g = _attn_fwd(qkv, g, d, name=f"attn_fwd_{g}")
        outs.append(o_g)
        lses.append(lse_g)
    o, lse = _attn_merge(outs, lses, tm=1024, name="attn_merge")

    mn = _rmsnorm_fwd(mem, gm, tm=MEM_LEN, name="mem_norm")
    kv = mm(mn, w["w_mem_kv"], m=MEM_LEN, n=1024, k=1024, tm=MEM_LEN, tn=1024, tk=1024, out_dtypes=(F32,), name="mem_kv")
    mo = _mem_attn_fwd(mq, kv, tq=1024, name="mem_attn_fwd")

    branch_acts = (y2, o, mo)
    branch_wts = (w["w_ssm_br"], w["w_attn_br"], w["w_mem_br"])
    merged = _branch_merge_fwd(branch_acts, branch_wts, zg, p["b_gate"], tm=256, name="branch_merge_fwd")
    h1, n2 = mm(merged, w["w_o"], m=rows, n=1024, k=1024, tm=1024, tn=1024, tk=1024, out_dtypes=(F32, BF16),
                aux=((x, "mn"), (g2, "row")), epilogue=_residual_norm_epilogue, name="out_proj")
    relu2 = lambda acc: (jnp.square(jnp.maximum(acc, 0.0)),)
    act = fetched(first_use[2], mm(n2, w["w_up"], m=rows, n=D_FF, k=1024, tb=True, tm=1024, tn=1024, tk=1024,
                                   out_dtypes=(BF16,), epilogue=relu2, carry=fetching(first_use[2]), name="mlp_up"))
    dh2, d_gf, sq_err = mm(act, w["w_down"], m=rows, n=1024, k=D_FF, tm=1024, tn=1024, tk=1024, out_dtypes=(F32,),
                           aux=((h1, "mn"), (tgt, "mn"), (gf, "row")), epilogue=_loss_head_epilogue, n_sums=2, name="mlp_down")
    loss = (0.5 / D_MODEL) * jnp.sum(sq_err)

    gs = {"final_g": d_gf.reshape(D_MODEL)}
    drelu2 = lambda acc, actv: (acc * (2.0 * jnp.sqrt(actv.astype(F32))),)
    dup = mm(dh2, w["w_down"], m=rows, n=D_FF, k=1024, tb=True, tm=1024, tn=2048, tk=1024, out_dtypes=(BF16,),
             aux=((act, "mn"),), epilogue=drelu2, name="d_act")
    gb["w_down"] = mm(act, dh2, m=D_FF, n=1024, k=rows, ta=True, tm=1024, tn=1024, tk=1024, out_dtypes=(F32,), name="dw_down")
    stages = (("swap", early[0]),)
    gb["w_up"] = arrived(stages, mm(dup, n2, m=D_FF, n=1024, k=rows, ta=True, tm=1024, tn=1024, tk=1024,
                                    out_dtypes=(F32,), carry=riding(*stages), name="dw_up"))
    stages = (("scatter", early[0]), ("swap", early[1]))
    dh1, gs["norm2_g"] = arrived(stages, mm(dup, w["w_up"], m=rows, n=1024, k=D_FF, tm=1024, tn=1024, tk=1024,
                                            out_dtypes=(F32,), aux=((h1, "mn"), (dh2, "mn"), (g2, "row")),
                                            epilogue=_rmsnorm_bwd_epilogue, n_sums=1, carry=riding(*stages), name="d_n2"))
    dmerged = mm(dh1, w["w_o"], m=rows, n=1024, k=1024, tb=True, tm=1024, tn=1024, tk=1024, out_dtypes=(F32,), name="d_merged")
    gb["w_o"] = mm(merged, dh1, m=1024, n=1024, k=rows, ta=True, tm=1024, tn=1024, tk=1024, out_dtypes=(F32,), name="dw_o")
    stages = (("scatter", early[1]),)
    (dy2, do, dmo, gb["w_ssm_br"], gb["w_attn_br"], gb["w_mem_br"], dzg, gs["b_gate"]) = arrived(stages, _branch_merge_bwd(
        dmerged, branch_acts, branch_wts, zg, p["b_gate"], tm=256, carry=riding(*stages), name="branch_merge_bwd"))

    dy0, dt, y1, gs["b_glu"], d_dd = _glu_bwd(dy2, y0, tglu, u, w["w_glu"], tm=512, name="glu_bwd")
    gs["ssm_d"] = d_dd.reshape(1, SSM_GROUPS, SSM_GROUP_SIZE)
    gb["w_glu"] = mm(y1, dt, m=512, n=512, k=rows, ta=True, tm=512, tn=512, tk=1024, out_dtypes=(F32,), name="dw_glu")
    dy0_i = _interleave(dy0)
    lam_ends = _ssm_ends(a_conj, dy0_i, c_blk, transpose=True, reverse=True, tt=512, name="ssm_bwd_ends")
    du_i, d_b_blk, d_c_blk, d_a_lay = _ssm_bwd(a_conj, dy0_i, u_i, s, s_entry, b_blk, c_blk, dd, lam_ends, tt=512,
                                                name="ssm_bwd")
    du = _deinterleave(du_i)
    d_ssm = ssm_vjp((d_a_lay, d_b_blk, d_c_blk))
    for name, val in zip(("ssm_lambda_re", "ssm_lambda_im", "ssm_log_dt", "ssm_b_re", "ssm_b_im", "ssm_c_re", "ssm_c_im"), d_ssm):
        gs[name] = val[None]

    dqkv = None
    for g, (_, d) in enumerate(ATTN_PATTERNS):
        dqkv = _attn_bwd(qkv, do, o, lse, g, d, dqkv, name=f"attn_bwd_{g}")

    dmq, dmk, dmv = _mem_attn_bwd(mq, kv, dmo, tq=1024, name="mem_attn_bwd")
    dkv = jnp.concatenate([dmk, dmv], axis=1)
    gb["w_mem_kv"] = mm(mn, dkv, m=1024, n=1024, k=MEM_LEN, ta=True, tm=1024, tn=1024, tk=MEM_LEN, out_dtypes=(F32,), name="dw_mem_kv")
    dmn = mm(dkv, w["w_mem_kv"], m=MEM_LEN, n=1024, k=1024, tb=True, tm=MEM_LEN, tn=1024, tk=1024, out_dtypes=(F32,), name="d_mn")
    _, gs["mem_norm_g"] = _rmsnorm_bwd(mem, gm, dmn, None, tm=MEM_LEN, name="mem_norm_bwd")

    pieces = ((du, OFF_U, "u"), (dqkv[0], OFF_QKV, "q"), (dqkv[1], OFF_QKV + 768, "k"), (dqkv[2], OFF_QKV + 1536, "v"),
              (dmq, OFF_MQ, "mq"), (dzg, OFF_ZG, "zg"))
    dw_rows = []
    for piece, off, tag in pieces:
        width = piece.shape[1]
        tmw = 1024 if width % 1024 == 0 else (768 if width == 768 else 512)
        stages = {"q": (("swap", early[2]),), "zg": (("scatter", early[2]),)}.get(tag, ())
        dw_rows.append(arrived(stages, mm(piece, n1, m=width, n=1024, k=rows, ta=True, tm=tmw, tn=1024, tk=1024,
                                          out_dtypes=(F32,), carry=riding(*stages), name="dw_in_" + tag)))
    gb["w_in"] = jnp.concatenate(dw_rows, axis=0)
    if reducer is not None:
        reducer.swap_now(early[3], [gb["w_in"]])
    stages = (("scatter", early[3]),)
    dx, gs["norm1_g"] = arrived(stages, _sum_matmul(
        [piece for piece, _, _ in pieces], win_t, [off for _, off, _ in pieces], tm=512,
        aux=((x, "mn"), (dh1, "mn"), (g1, "row")), epilogue=_rmsnorm_bwd_epilogue, n_sums=1,
        carry=riding(*stages), name="d_n1"))
    return loss, dx, gb, gs


def kernel(x, mem, norm1_g, mem_norm_g, w_in, b_gate, ssm_lambda_re, ssm_lambda_im, ssm_log_dt, ssm_b_re, ssm_b_im, ssm_c_re, ssm_c_im, ssm_d, w_glu, b_glu, w_ssm_br, w_attn_br, w_mem_kv, w_mem_br, w_o, norm2_g, w_up, w_down, final_g, loss_target, m_norm1_g, m_mem_norm_g, m_w_in, m_b_gate, m_ssm_lambda_re, m_ssm_lambda_im, m_ssm_log_dt, m_ssm_b_re, m_ssm_b_im, m_ssm_c_re, m_ssm_c_im, m_ssm_d, m_w_glu, m_b_glu, m_w_ssm_br, m_w_attn_br, m_w_mem_kv, m_w_mem_br, m_w_o, m_norm2_g, m_w_up, m_w_down, m_final_g, v_norm1_g, v_mem_norm_g, v_w_in, v_b_gate, v_ssm_lambda_re, v_ssm_lambda_im, v_ssm_log_dt, v_ssm_b_re, v_ssm_b_im, v_ssm_c_re, v_ssm_c_im, v_ssm_d, v_w_glu, v_b_glu, v_w_ssm_br, v_w_attn_br, v_w_mem_kv, v_w_mem_br, v_w_o, v_norm2_g, v_w_up, v_w_down, v_final_g):
    env = dict(locals())
    weights = {n: env[n] for n in WEIGHT_ORDER}
    moms = {n: env["m_" + n] for n in WEIGHT_ORDER}
    vels = {n: env["v_" + n] for n in WEIGHT_ORDER}
    def shard2d(a):
        return a.reshape(a.shape[-2], a.shape[-1])

    chip = 2 * lax.axis_index("x") + lax.axis_index("y")
    wire = [shard2d(weights[n]).astype(BF16) for n, _, _ in BIG]
    wire = dict(zip([n for n, _, _ in BIG], [s.T if tr else s for s, (_, tr, _) in zip(wire, BIG)]))
    w_in_full = _run_exchange(_gather_exchange([wire.pop("w_in")]), name="all_gather_w_in")[0]
    small = {n: weights[n] for n, _ in SMALL}

    reducer = _GradReducer(lax.axis_index("c").astype(jnp.int32).reshape(1), chip.astype(jnp.int32).reshape(1))
    loss, dx, gb, gs = _device_step(x[0], mem[0], loss_target[0], {"w_in": w_in_full}, small, shards=wire, reducer=reducer)
    *shards, small_grad = reducer.finish(["small"], [_pack_small(gs)], [n for n, _, _ in BIG] + ["small"])
    grads = {}
    for (n, tr, _), sh in zip(BIG, shards):
        sh = sh.reshape(2 * sh.shape[1], sh.shape[2])
        grads[n] = sh.T if tr else sh
    small_grad = small_grad.reshape(N_CHIPS * SMALL_ROWS, 1024)
    grads_small = _unpack_small(small_grad)

    delta, new_m, new_v = {}, {}, {}
    for n, _, _ in BIG:
        grads[n], delta[n], new_m[n], new_v[n] = _adamw(weights[n], grads[n], moms[n], vels[n],
                                                        tr=min(weights[n].shape[-2], 256), name="adamw_" + n)
    _, ds_, ms_, vs_ = _adamw(_pack_small(small), small_grad,
                              _pack_small({n: moms[n] for n, _ in SMALL}), _pack_small({n: vels[n] for n, _ in SMALL}),
                              tr=N_CHIPS * SMALL_ROWS, name="adamw_small")
    for dst, buf in ((delta, ds_), (new_m, ms_), (new_v, vs_)):
        dst.update(_unpack_small(buf))
    grads.update(grads_small)

    total_loss = lax.psum(loss, ("x", "y", "c"))
    return (total_loss, dx[None], *[grads[n] for n in WEIGHT_ORDER], *[delta[n] for n in WEIGHT_ORDER],
            *[new_m[n] for n in WEIGHT_ORDER], *[new_v[n] for n in WEIGHT_ORDER])
```

```python
import functools
import math

import numpy as np
import jax
import jax.numpy as jnp
from jax import lax
from jax.experimental import pallas as pl
from jax.experimental.pallas import tpu as pltpu

F32 = jnp.float32
BF16 = jnp.bfloat16

D_MODEL = 1024
SSM_GROUPS = 32
SSM_GROUP_SIZE = 16
SSM_STATE = 64
SSM_WIDTH = 512
N_STATES = SSM_GROUPS * SSM_STATE
SCAN_CB = 1024
ATTN_PATTERNS = ((128, 1), (512, 4), (2048, 16))
ATTN_HEAD_DIM = 64
ATTN_Q = 128
MEM_LEN = 256
MEM_HEAD_DIM = 128
MEM_HEADS = 4
D_FF = 4096
OFF_U, OFF_QKV, OFF_MQ, OFF_ZG = 0, 512, 2816, 3328
IN_WIDTH = 6400
RMS_EPS = 1e-6
NEG_INF = -1e30
ADAM_LR, ADAM_B1, ADAM_B2, ADAM_EPS, ADAM_WD, ADAM_STEP = 0.001, 0.9, 0.999, 1e-08, 0.01, 10

VMEM_LIMIT_BYTES = 48 * 1024 * 1024
VMEM_LIMIT_WIDE_BYTES = 56 * 1024 * 1024
LANES = 128
MESH = pl.DeviceIdType.MESH
N_CHIPS = 4

SCAN_SEGS = 8
SCAN_GROUPS = SCAN_CB // SSM_STATE

BIG = (("w_in", True, (6400, 1024)), ("w_glu", False, (512, 512)), ("w_ssm_br", True, (1024, 512)),
       ("w_attn_br", True, (1024, 256)), ("w_mem_kv", False, (1024, 1024)), ("w_mem_br", True, (1024, 512)),
       ("w_o", False, (1024, 1024)), ("w_up", True, (4096, 1024)), ("w_down", False, (4096, 1024)))
SMALL = (("norm1_g", (1, 1024)), ("mem_norm_g", (1, 1024)), ("b_gate", (1, 3072)),
         ("ssm_lambda_re", (1, 32, 64)), ("ssm_lambda_im", (1, 32, 64)), ("ssm_log_dt", (1, 32)),
         ("ssm_b_re", (1, 32, 64, 16)), ("ssm_b_im", (1, 32, 64, 16)), ("ssm_c_re", (1, 32, 16, 64)),
         ("ssm_c_im", (1, 32, 16, 64)), ("ssm_d", (1, 32, 16)), ("b_glu", (1, 512)),
         ("norm2_g", (1, 1024)), ("final_g", (1024,)))
WEIGHT_ORDER = ("norm1_g", "mem_norm_g", "w_in", "b_gate", "ssm_lambda_re", "ssm_lambda_im", "ssm_log_dt",
                "ssm_b_re", "ssm_b_im", "ssm_c_re", "ssm_c_im", "ssm_d", "w_glu", "b_glu", "w_ssm_br",
                "w_attn_br", "w_mem_kv", "w_mem_br", "w_o", "norm2_g", "w_up", "w_down", "final_g")
SMALL_ELEMS = sum(int(np.prod(s)) for _, s in SMALL)
SMALL_ROWS = 64


def _params(sem, vmem=VMEM_LIMIT_BYTES):
    return pltpu.CompilerParams(dimension_semantics=sem, vmem_limit_bytes=vmem)


def _sigmoid(v):
    return 0.5 * jnp.tanh(0.5 * v) + 0.5


_GELU_C = math.sqrt(2.0 / math.pi)


def _gelu(v):
    return 0.5 * v * (1.0 + jnp.tanh(_GELU_C * (v + 0.044715 * v * v * v)))


def _gelu_grad(v):
    th = jnp.tanh(_GELU_C * (v + 0.044715 * v * v * v))
    return 0.5 * (1.0 + th) + 0.5 * v * (1.0 - th * th) * _GELU_C * (1.0 + 3.0 * 0.044715 * v * v)


def _dot(a, b, ca, cb):
    return lax.dot_general(a, b, (((ca,), (cb,)), ((), ())), preferred_element_type=F32)


class _Exchange:
    def __init__(self, ins, outs, aliases, sems, start, finish):
        self.ins, self.outs, self.aliases, self.sems, self.start, self.finish = ins, outs, aliases, sems, start, finish


def _matmul(a, b, *, m, n, k, ta=False, tb=False, tm, tn, tk, out_dtypes, name,
            a_off=(0, 0), b_off=(0, 0), b_row0=None, aux=(), epilogue=None, n_sums=0, carry=None):
    assert m % tm == 0 and n % tn == 0 and k % tk == 0, (name, m, n, k, tm, tn, tk)
    nk = k // tk
    n_aux = len(aux)
    n_tiles = len(out_dtypes)
    n_out = n_tiles + n_sums
    ar, ac = a_off
    br, bc = b_off
    if ta:
        a_spec = pl.BlockSpec((tk, tm), lambda i, j, kk: (kk + ar, i + ac))
    else:
        a_spec = pl.BlockSpec((tm, tk), lambda i, j, kk: (i + ar, kk + ac))
    if tb:
        if b_row0 is None:
            b_spec = pl.BlockSpec((tn, tk), lambda i, j, kk: (j + br, kk + bc))
        else:
            assert b_row0 % LANES == 0 and tn % LANES == 0 and tk % LANES == 0
            b_spec = pl.BlockSpec((pl.Element(tn), pl.Element(tk)),
                                  lambda i, j, kk: (pl.multiple_of(b_row0 + j * tn, LANES), pl.multiple_of((kk + bc) * tk, LANES)))
    else:
        b_spec = pl.BlockSpec((tk, tn), lambda i, j, kk: (kk + br, j + bc))
    aux_specs = []
    for _, kind in aux:
        if kind == "mn":
            aux_specs.append(pl.BlockSpec((tm, tn), lambda i, j, kk: (i, j)))
        else:
            aux_specs.append(pl.BlockSpec((1, tn), lambda i, j, kk: (0, j)))
    ca = 0 if ta else 1
    cb = 1 if tb else 0

    def finish(acc, aux_refs, out_refs, row_tile):
        outs = (acc,) if epilogue is None else epilogue(acc, *[r[...] for r in aux_refs])
        for o_ref, o in zip(out_refs[:n_tiles], outs[:n_tiles]):
            o_ref[...] = o.astype(o_ref.dtype)
        _accumulate_over_rows(out_refs[n_tiles:], outs[n_tiles:], row_tile)

    def body(a_ref, b_ref, *rest):
        aux_refs = rest[:n_aux]
        out_refs = rest[n_aux:n_aux + n_out]
        row_tile = pl.program_id(0)
        prod = _dot(a_ref[...].astype(BF16), b_ref[...].astype(BF16), ca, cb)
        if nk == 1:
            finish(prod, aux_refs, out_refs, row_tile)
            return
        acc_ref = rest[n_aux + n_out]
        kk = pl.program_id(2)

        @pl.when(kk == 0)
        def _():
            acc_ref[...] = prod

        @pl.when(jnp.logical_and(kk > 0, kk < nk - 1))
        def _():
            acc_ref[...] += prod

        @pl.when(kk == nk - 1)
        def _():
            finish(acc_ref[...] + prod, aux_refs, out_refs, row_tile)

    tile = pl.BlockSpec((tm, tn), lambda i, j, kk: (i, j))
    col_sum = pl.BlockSpec((1, tn), lambda i, j, kk: (0, j))
    res = _call_with_carry(
        body, carry, name=name, grid=(m // tm, n // tn, nk), in_specs=[a_spec, b_spec] + aux_specs,
        out_specs=[tile] * n_tiles + [col_sum] * n_sums,
        out_shape=[jax.ShapeDtypeStruct((m, n), dt) for dt in out_dtypes] + [jax.ShapeDtypeStruct((1, n), F32)] * n_sums,
        scratch=[pltpu.VMEM((tm, tn), F32)] if nk > 1 else [], operands=[a, b] + [x for x, _ in aux],
        semantics=("arbitrary" if n_sums else "parallel", "parallel", "arbitrary"))
    main = res[0] if n_out == 1 else tuple(res[:n_out])
    return main if carry is None else (main, list(res[n_out:]))


def _accumulate_over_rows(sum_refs, terms, row_tile):
    for s_ref, term in zip(sum_refs, terms):
        @pl.when(row_tile == 0)
        def _():
            s_ref[...] = term

        @pl.when(row_tile > 0)
        def _():
            s_ref[...] += term


def _call_with_carry(body, carry, *, name, grid, in_specs, out_specs, out_shape, scratch, operands, semantics,
                     vmem=VMEM_LIMIT_BYTES):
    if carry is None:
        return pl.pallas_call(body, name=name, grid=grid, in_specs=in_specs, out_specs=out_specs, out_shape=out_shape,
                              scratch_shapes=scratch, compiler_params=_params(semantics, vmem))(*operands)
    n_in, n_cin, n_out, n_cout, n_scr = len(operands), len(carry.ins), len(out_shape), len(carry.outs), len(scratch)

    def hosted(*refs):
        main_in, c_in = refs[:n_in], refs[n_in:n_in + n_cin]
        main_out = refs[n_in + n_cin:n_in + n_cin + n_out]
        c_out = refs[n_in + n_cin + n_out:n_in + n_cin + n_out + n_cout]
        rest = refs[n_in + n_cin + n_out + n_cout:]
        ids = [pl.program_id(t) for t in range(len(grid))]
        first = functools.reduce(jnp.logical_and, [i == 0 for i in ids])
        last = functools.reduce(jnp.logical_and, [i == g - 1 for i, g in zip(ids, grid)])

        @pl.when(first)
        def _():
            carry.start(c_in, c_out, *rest[n_scr:])

        body(*main_in, *main_out, *rest[:n_scr])

        @pl.when(last)
        def _():
            carry.finish(c_in, c_out, *rest[n_scr:])

    return pl.pallas_call(
        hosted, name=name, grid=grid,
        in_specs=list(in_specs) + [ANY] * n_cin, out_specs=list(out_specs) + [ANY] * n_cout,
        out_shape=list(out_shape) + list(carry.outs),
        input_output_aliases={n_in + i: n_out + o for i, o in carry.aliases.items()},
        scratch_shapes=list(scratch) + [pltpu.SemaphoreType.DMA(s) for s in carry.sems],
        compiler_params=_params(("arbitrary",) * len(grid), vmem),
    )(*operands, *carry.ins)


def _sum_matmul(pieces, b, offs, *, tm, name, aux=(), epilogue=None, n_sums=0, carry=None):
    m = pieces[0].shape[0]
    n = b.shape[1]
    npieces, n_aux = len(pieces), len(aux)

    def body(*refs):
        b_ref = refs[npieces]
        aux_refs = refs[npieces + 1:npieces + 1 + n_aux]
        out_refs = refs[npieces + 1 + n_aux:]
        acc = None
        for p_ref, off in zip(refs[:npieces], offs):
            part = _dot(p_ref[...].astype(BF16), b_ref[pl.ds(off, p_ref.shape[1]), :], 1, 0)
            acc = part if acc is None else acc + part
        outs = (acc,) if epilogue is None else epilogue(acc, *[r[...] for r in aux_refs])
        out_refs[0][...] = outs[0]
        _accumulate_over_rows(out_refs[1:], outs[1:], pl.program_id(0))

    row = pl.BlockSpec((tm, n), lambda i: (i, 0))
    vec = pl.BlockSpec((1, n), lambda i: (0, 0))
    res = _call_with_carry(
        body, carry, name=name, grid=(m // tm,),
        in_specs=[pl.BlockSpec((tm, p.shape[1]), lambda i: (i, 0)) for p in pieces] + [_resident(b.shape)]
        + [row if kind == "mn" else vec for _, kind in aux],
        out_specs=[row] + [vec] * n_sums,
        out_shape=[jax.ShapeDtypeStruct((m, n), F32)] + [jax.ShapeDtypeStruct((1, n), F32)] * n_sums,
        scratch=[], operands=list(pieces) + [b] + [x for x, _ in aux], semantics=("arbitrary" if n_sums else "parallel",),
        vmem=VMEM_LIMIT_WIDE_BYTES)
    main = res[0] if n_sums == 0 else tuple(res[:1 + n_sums])
    return main if carry is None else (main, list(res[1 + n_sums:]))


def _rmsnorm_fwd(x, g, *, tm, name):
    rows, d = x.shape

    def body(x_ref, g_ref, o_ref):
        xv = x_ref[...]
        r = lax.rsqrt(jnp.mean(xv * xv, axis=-1, keepdims=True) + RMS_EPS)
        o_ref[...] = (xv * r * g_ref[...]).astype(o_ref.dtype)

    return pl.pallas_call(
        body, name=name, grid=(rows // tm,),
        in_specs=[pl.BlockSpec((tm, d), lambda i: (i, 0)), pl.BlockSpec((1, d), lambda i: (0, 0))],
        out_specs=pl.BlockSpec((tm, d), lambda i: (i, 0)),
        out_shape=jax.ShapeDtypeStruct((rows, d), BF16),
        compiler_params=_params(("parallel",)),
    )(x, g)


def _residual_norm_epilogue(acc, xv, gv):
    h = acc + xv
    r = lax.rsqrt(jnp.mean(h * h, axis=-1, keepdims=True) + RMS_EPS)
    return h, h * r * gv


def _rmsnorm_bwd_epilogue(dy, xv, resv, gv):
    r = lax.rsqrt(jnp.mean(xv * xv, axis=-1, keepdims=True) + RMS_EPS)
    xhat = xv * r
    dyg = dy * gv
    dx = r * (dyg - xhat * jnp.mean(dyg * xhat, axis=-1, keepdims=True)) + resv
    return dx, jnp.sum(dy * xhat, axis=0, keepdims=True)


def _rmsnorm_bwd(x, g, dy, res, *, tm, name):
    rows, d = x.shape
    has_res = res is not None

    def body(x_ref, g_ref, dy_ref, *rest):
        if has_res:
            res_ref, dx_ref, dg_ref = rest
        else:
            dx_ref, dg_ref = rest
        i = pl.program_id(0)
        xv = x_ref[...]
        r = lax.rsqrt(jnp.mean(xv * xv, axis=-1, keepdims=True) + RMS_EPS)
        xhat = xv * r
        dyv = dy_ref[...]
        dyg = dyv * g_ref[...]
        dx = r * (dyg - xhat * jnp.mean(dyg * xhat, axis=-1, keepdims=True))
        if has_res:
            dx = dx + res_ref[...]
        dx_ref[...] = dx

        @pl.when(i == 0)
        def _():
            dg_ref[...] = jnp.zeros_like(dg_ref)

        dg_ref[...] += jnp.sum(dyv * xhat, axis=0, keepdims=True)

    row_spec = pl.BlockSpec((tm, d), lambda i: (i, 0))
    vec_spec = pl.BlockSpec((1, d), lambda i: (0, 0))
    ins = [x, g, dy] + ([res] if has_res else [])
    return pl.pallas_call(
        body, name=name, grid=(rows // tm,),
        in_specs=[row_spec, vec_spec, row_spec] + ([row_spec] if has_res else []),
        out_specs=[row_spec, vec_spec],
        out_shape=[jax.ShapeDtypeStruct((rows, d), F32), jax.ShapeDtypeStruct((1, d), F32)],
        compiler_params=_params(("arbitrary",)),
    )(*ins)


def _loss_head_epilogue(acc, hv, tgtv, gv):
    xv = acc + hv
    r = lax.rsqrt(jnp.mean(xv * xv, axis=-1, keepdims=True) + RMS_EPS)
    xhat = xv * r
    err = xhat * gv - tgtv
    dyv = err * (1.0 / D_MODEL)
    dyg = dyv * gv
    dh = r * (dyg - xhat * jnp.mean(dyg * xhat, axis=-1, keepdims=True))
    return dh, jnp.sum(dyv * xhat, axis=0, keepdims=True), jnp.sum(err * err, axis=0, keepdims=True)


def _to_scan_layout(v):
    lead = v.shape[:-2]
    v = v.reshape(lead + (2, N_STATES // SCAN_CB, SCAN_CB))
    v = jnp.swapaxes(v, -3, -2)
    return v.reshape(lead + (2 * N_STATES,))


def _ssm_matrices(lam_re, lam_im, log_dt, b_re, b_im, c_re, c_im):
    dt = jnp.exp(log_dt)[:, None]
    mag = jnp.exp(lam_re * dt)
    a_re, a_im = mag * jnp.cos(lam_im * dt), mag * jnp.sin(lam_im * dt)
    nr, ni = a_re - 1.0, a_im
    den = lam_re * lam_re + lam_im * lam_im
    coef_re = (nr * lam_re + ni * lam_im) / den
    coef_im = (ni * lam_re - nr * lam_im) / den
    bb_re = coef_re[..., None] * b_re - coef_im[..., None] * b_im
    bb_im = coef_re[..., None] * b_im + coef_im[..., None] * b_re
    a_lay = _to_scan_layout(jnp.stack([a_re.reshape(-1), a_im.reshape(-1)], axis=0))[None, :]
    nblk = SSM_GROUPS // SCAN_GROUPS
    eye = jnp.eye(SCAN_GROUPS, dtype=F32)

    def b_block(bb):
        bb = bb.reshape(nblk, SCAN_GROUPS, SSM_STATE, SSM_GROUP_SIZE)
        return jnp.einsum("gk,jkph->jghkp", eye, bb).reshape(nblk, SCAN_GROUPS * SSM_GROUP_SIZE, SCAN_CB)

    b_blk = jnp.concatenate([b_block(bb_re), b_block(bb_im)], axis=2)

    def c_block(cc):
        cc = cc.reshape(nblk, SCAN_GROUPS, SSM_GROUP_SIZE, SSM_STATE)
        return jnp.einsum("gk,jghp->jkpgh", eye, cc).reshape(nblk, SCAN_CB, SCAN_GROUPS * SSM_GROUP_SIZE)

    c_blk = jnp.concatenate([c_block(c_re), -c_block(c_im)], axis=1)
    return a_lay, b_blk, c_blk


def _interleave(v):
    rows, c = v.shape
    return v.reshape(SCAN_SEGS, rows // SCAN_SEGS, c).transpose(1, 0, 2).reshape(rows, c)


def _deinterleave(v):
    rows, c = v.shape
    return v.reshape(rows // SCAN_SEGS, SCAN_SEGS, c).transpose(1, 0, 2).reshape(rows, c)


def _scan_groups(a_ref, bu_ref, o_ref, state, *, reverse, tt):
    cb = SCAN_CB
    ar = jnp.broadcast_to(a_ref[:, :cb], (SCAN_SEGS, cb))
    ai = jnp.broadcast_to(a_ref[:, cb:], (SCAN_SEGS, cb))
    ngroups = tt // SCAN_SEGS

    def step(i, st):
        sr, si = st
        r0 = pl.multiple_of(((ngroups - 1 - i) if reverse else i) * SCAN_SEGS, SCAN_SEGS)
        blk = bu_ref[pl.ds(r0, SCAN_SEGS), :]
        nr = ar * sr - ai * si + blk[:, :cb]
        ni = ar * si + ai * sr + blk[:, cb:]
        if o_ref is not None:
            o_ref[pl.ds(r0, SCAN_SEGS), :] = jnp.concatenate([nr, ni], axis=1)
        return nr, ni

    return lax.fori_loop(0, ngroups, step, state, unroll=4)


def _segment_entries(a_ref, e_ref, init_ref, *, reverse, seg_len):
    cb = SCAN_CB
    n_sq = seg_len.bit_length() - 1
    assert 1 << n_sq == seg_len, seg_len
    pr, pi = a_ref[:, :cb], a_ref[:, cb:]
    for _ in range(n_sq):
        pr, pi = pr * pr - pi * pi, 2.0 * pr * pi
    cr = jnp.zeros((1, cb), F32)
    ci = jnp.zeros((1, cb), F32)
    order = range(SCAN_SEGS - 1, -1, -1) if reverse else range(SCAN_SEGS)
    for k, seg in enumerate(order):
        if k > 0:
            prev = seg + 1 if reverse else seg - 1
            er, ei = e_ref[prev:prev + 1, :cb], e_ref[prev:prev + 1, cb:]
            cr, ci = pr * cr - pi * ci + er, pr * ci + pi * cr + ei
        init_ref[seg:seg + 1, :] = jnp.concatenate([cr, ci], axis=1)


def _ssm_specs(nt, tt, nch, reverse):
    cb = SCAN_CB
    tmap = (lambda j, kk: (nt - 1 - kk, j)) if reverse else (lambda j, kk: (kk, j))
    return dict(a=pl.BlockSpec((1, 2 * cb), lambda j, kk: (0, j)),
                seg=pl.BlockSpec((SCAN_SEGS, 2 * cb), lambda j, kk: (0, j)),
                chan=pl.BlockSpec((tt, nch), tmap),
                state=pl.BlockSpec((tt, 2 * cb), tmap),
                b=pl.BlockSpec((None, nch, 2 * cb), lambda j, kk: (j, 0, 0)),
                c=pl.BlockSpec((None, 2 * cb, nch), lambda j, kk: (j, 0, 0)))


def _ssm_ends(a_lay, x, blocks, *, transpose, reverse, tt, name):
    rows = x.shape[0]
    nblk = blocks.shape[0]
    nch = x.shape[1] // nblk
    cb = SCAN_CB
    nt = rows // tt
    sp = _ssm_specs(nt, tt, nch, reverse)

    def body(a_ref, x_ref, w_ref, e_ref, bu_ref):
        kk = pl.program_id(1)

        @pl.when(kk == 0)
        def _():
            e_ref[...] = jnp.zeros_like(e_ref)

        bu_ref[...] = _dot(x_ref[...].astype(BF16), w_ref[...].astype(BF16), 1, 1 if transpose else 0)
        sr, si = _scan_groups(a_ref, bu_ref, None, (e_ref[:, :cb], e_ref[:, cb:]), reverse=reverse, tt=tt)
        e_ref[...] = jnp.concatenate([sr, si], axis=1)

    return pl.pallas_call(
        body, name=name, grid=(nblk, nt),
        in_specs=[sp["a"], sp["chan"], sp["c"] if transpose else sp["b"]],
        out_specs=sp["seg"],
        out_shape=jax.ShapeDtypeStruct((SCAN_SEGS, nblk * 2 * cb), F32),
        scratch_shapes=[pltpu.VMEM((tt, 2 * cb), F32)],
        compiler_params=_params(("parallel", "arbitrary")),
    )(a_lay, x, blocks)


def _ssm_fwd(a_lay, u, b_blk, c_blk, ends, *, tt, name):
    rows = u.shape[0]
    nblk = b_blk.shape[0]
    nch = u.shape[1] // nblk
    cb = SCAN_CB
    nt = rows // tt
    sp = _ssm_specs(nt, tt, nch, False)

    def body(a_ref, e_ref, u_ref, b_ref, c_ref, s_ref, y_ref, init_ref, carry_ref):
        kk = pl.program_id(1)

        @pl.when(kk == 0)
        def _():
            _segment_entries(a_ref, e_ref, init_ref, reverse=False, seg_len=rows // SCAN_SEGS)
            carry_ref[...] = init_ref[...]

        s_ref[...] = _dot(u_ref[...].astype(BF16), b_ref[...].astype(BF16), 1, 0)
        sr, si = _scan_groups(a_ref, s_ref, s_ref, (carry_ref[:, :cb], carry_ref[:, cb:]), reverse=False, tt=tt)
        carry_ref[...] = jnp.concatenate([sr, si], axis=1)
        y_ref[...] = _dot(s_ref[...].astype(BF16), c_ref[...].astype(BF16), 1, 0)

    return pl.pallas_call(
        body, name=name, grid=(nblk, nt),
        in_specs=[sp["a"], sp["seg"], sp["chan"], sp["b"], sp["c"]],
        out_specs=[sp["state"], sp["chan"], sp["seg"]],
        out_shape=[jax.ShapeDtypeStruct((rows, nblk * 2 * cb), F32), jax.ShapeDtypeStruct((rows, nblk * nch), F32),
                   jax.ShapeDtypeStruct((SCAN_SEGS, nblk * 2 * cb), F32)],
        scratch_shapes=[pltpu.VMEM((SCAN_SEGS, 2 * cb), F32)],
        compiler_params=_params(("parallel", "arbitrary")),
    )(a_lay, ends, u, b_blk, c_blk)


def _ssm_bwd(a_conj, dy, u, s, s_entry, b_blk, c_blk, dd, ends, *, tt, name):
    rows = u.shape[0]
    nblk = b_blk.shape[0]
    nch = u.shape[1] // nblk
    cb = SCAN_CB
    nt = rows // tt
    sp = _ssm_specs(nt, tt, nch, True)
    groups_per_tile = tt // SCAN_SEGS
    before = pl.BlockSpec((SCAN_SEGS, 2 * cb), lambda j, kk: (jnp.maximum((nt - 1 - kk) * groups_per_tile - 1, 0), j))

    def body(a_ref, e_ref, dy_ref, u_ref, s_ref, before_ref, entry_ref, b_ref, c_ref, dd_ref,
             du_ref, db_ref, dc_ref, da_ref, lam_ref, carry_ref):
        kk = pl.program_id(1)

        @pl.when(kk == 0)
        def _():
            _segment_entries(a_ref, e_ref, carry_ref, reverse=True, seg_len=rows // SCAN_SEGS)
            db_ref[...] = jnp.zeros_like(db_ref)
            dc_ref[...] = jnp.zeros_like(dc_ref)
            da_ref[...] = jnp.zeros_like(da_ref)

        dyv = dy_ref[...]
        dyb = dyv.astype(BF16)
        lam_ref[...] = _dot(dyb, c_ref[...].astype(BF16), 1, 1)
        lr, li = _scan_groups(a_ref, lam_ref, lam_ref, (carry_ref[:, :cb], carry_ref[:, cb:]), reverse=True, tt=tt)
        carry_ref[...] = jnp.concatenate([lr, li], axis=1)

        first = jnp.where(kk == nt - 1, entry_ref[...], before_ref[...])
        rest = tt - SCAN_SEGS
        lam_hi = lam_ref[pl.ds(SCAN_SEGS, rest), :]
        s_lo = s_ref[pl.ds(0, rest), :]
        lam_lo = lam_ref[pl.ds(0, SCAN_SEGS), :]

        def pair(lv, pv):
            lre, lim, pre, pim = lv[:, :cb], lv[:, cb:], pv[:, :cb], pv[:, cb:]
            return (jnp.sum(lre * pre + lim * pim, axis=0, keepdims=True),
                    jnp.sum(lim * pre - lre * pim, axis=0, keepdims=True))

        r1, i1 = pair(lam_hi, s_lo)
        r0, i0 = pair(lam_lo, first)
        da_ref[...] += jnp.concatenate([r1 + r0, i1 + i0], axis=1)

        lamb = lam_ref[...].astype(BF16)
        du_ref[...] = _dot(lamb, b_ref[...].astype(BF16), 1, 1) + dd_ref[...] * dyv
        db_ref[...] += _dot(u_ref[...].astype(BF16), lamb, 0, 0)
        dc_ref[...] += _dot(s_ref[...].astype(BF16), dyb, 0, 0)

    return pl.pallas_call(
        body, name=name, grid=(nblk, nt),
        in_specs=[sp["a"], sp["seg"], sp["chan"], sp["chan"], sp["state"], before, sp["seg"], sp["b"], sp["c"],
                  pl.BlockSpec((1, nch), lambda j, kk: (0, j))],
        out_specs=[sp["chan"], sp["b"], sp["c"], pl.BlockSpec((1, 2 * cb), lambda j, kk: (0, j))],
        out_shape=[jax.ShapeDtypeStruct((rows, nblk * nch), F32), jax.ShapeDtypeStruct(b_blk.shape, F32),
                   jax.ShapeDtypeStruct(c_blk.shape, F32), jax.ShapeDtypeStruct((1, nblk * 2 * cb), F32)],
        scratch_shapes=[pltpu.VMEM((tt, 2 * cb), F32), pltpu.VMEM((SCAN_SEGS, 2 * cb), F32)],
        compiler_params=_params(("parallel", "arbitrary")),
    )(a_conj, ends, dy, u, s, s, s_entry, b_blk, c_blk, dd)


def _glu_fwd(ys, u, dd, w_glu, b_glu, *, tm, name):
    rows, w = ys.shape

    def body(ys_ref, u_ref, dd_ref, w_ref, b_ref, y0_ref, t_ref, y2_ref):
        y0 = ys_ref[...] + dd_ref[...] * u_ref[...]
        y1 = _gelu(y0)
        t = _dot(y1.astype(BF16), w_ref[...], 1, 0) + b_ref[...]
        y0_ref[...] = y0
        t_ref[...] = t
        y2_ref[...] = (y1 * _sigmoid(t)).astype(BF16)

    row = pl.BlockSpec((tm, w), lambda i: (i, 0))
    vec = pl.BlockSpec((1, w), lambda i: (0, 0))
    return pl.pallas_call(
        body, name=name, grid=(rows // tm,),
        in_specs=[row, row, vec, pl.BlockSpec((w, w), lambda i: (0, 0)), vec],
        out_specs=[row, row, row],
        out_shape=[jax.ShapeDtypeStruct((rows, w), F32), jax.ShapeDtypeStruct((rows, w), F32),
                   jax.ShapeDtypeStruct((rows, w), BF16)],
        compiler_params=_params(("parallel",)),
    )(ys, u, dd, w_glu, b_glu)


def _glu_bwd(dy2, y0, t, u, w_glu, *, tm, name):
    rows, w = y0.shape

    def body(dy2_ref, y0_ref, t_ref, u_ref, w_ref, dy0_ref, dt_ref, y1_ref, db_ref, dd_ref):
        i = pl.program_id(0)
        y0 = y0_ref[...]
        y1 = _gelu(y0)
        sg = _sigmoid(t_ref[...])
        dy2v = dy2_ref[...]
        dt = dy2v * y1 * sg * (1.0 - sg)
        dy1 = dy2v * sg + _dot(dt.astype(BF16), w_ref[...], 1, 1)
        dy0 = dy1 * _gelu_grad(y0)
        dy0_ref[...] = dy0
        dt_ref[...] = dt.astype(BF16)
        y1_ref[...] = y1.astype(BF16)

        @pl.when(i == 0)
        def _():
            db_ref[...] = jnp.zeros_like(db_ref)
            dd_ref[...] = jnp.zeros_like(dd_ref)

        db_ref[...] += jnp.sum(dt, axis=0, keepdims=True)
        dd_ref[...] += jnp.sum(dy0 * u_ref[...], axis=0, keepdims=True)

    row = pl.BlockSpec((tm, w), lambda i: (i, 0))
    vec = pl.BlockSpec((1, w), lambda i: (0, 0))
    return pl.pallas_call(
        body, name=name, grid=(rows // tm,),
        in_specs=[row, row, row, row, pl.BlockSpec((w, w), lambda i: (0, 0))],
        out_specs=[row, row, row, vec, vec],
        out_shape=[jax.ShapeDtypeStruct((rows, w), F32), jax.ShapeDtypeStruct((rows, w), BF16),
                   jax.ShapeDtypeStruct((rows, w), BF16), jax.ShapeDtypeStruct((1, w), F32),
                   jax.ShapeDtypeStruct((1, w), F32)],
        compiler_params=_params(("arbitrary",)),
    )(dy2, y0, t, u, w_glu)


ATTN_TILE = 2048


def _attn_geometry(rows, d):
    sb = ATTN_Q * d
    tr = max(sb, min(ATTN_TILE, rows))
    assert rows % tr == 0 and tr % sb == 0, (rows, d)
    return sb, tr, rows // tr, tr // sb


def _attn_masks():
    qi = lax.broadcasted_iota(jnp.int32, (2 * ATTN_Q, 2 * ATTN_Q), 0) % ATTN_Q
    kj = lax.broadcasted_iota(jnp.int32, (2 * ATTN_Q, 2 * ATTN_Q), 1)
    own_ok = jnp.logical_and(kj >= ATTN_Q, kj - ATTN_Q <= qi)
    prev_ok = jnp.logical_and(kj < ATTN_Q, kj >= qi)
    bias_first = jnp.where(own_ok, 0.0, NEG_INF)
    bias_other = jnp.where(jnp.logical_or(own_ok, prev_ok), 0.0, NEG_INF)
    head0 = lax.broadcasted_iota(jnp.int32, (ATTN_Q, LANES), 1) < ATTN_HEAD_DIM
    return bias_first, bias_other, head0


def _attn_rows(base, n, d):
    return pl.ds(pl.multiple_of(base, ATTN_Q), n) if d == 1 else pl.ds(base, n, stride=d)


def _stack_heads(v, head0):
    return jnp.concatenate([jnp.where(head0, v, 0.0), jnp.where(head0, 0.0, v)], axis=0)


def _unstack_heads(v, head0):
    return jnp.where(head0, v[:ATTN_Q], v[ATTN_Q:])


def _fill_keys(buf, prev_ref, cur_ref, sb):
    buf[pl.ds(0, sb), :] = prev_ref[...]
    buf[pl.ds(sb, cur_ref.shape[0]), :] = cur_ref[...]


def _attn_fwd(qkv, g, d, *, name):
    rows = qkv.shape[0]
    sb, tr, ntiles, nsub = _attn_geometry(rows, d)
    qc, kc, vc = 2 * g, 6 + 2 * g, 12 + 2 * g
    scale = ATTN_HEAD_DIM ** -0.5

    def body(q_ref, kc_ref, kp_ref, vc_ref, vp_ref, o_ref, lse_ref, kbuf, vbuf):
        n = pl.program_id(0)
        _fill_keys(kbuf, kp_ref, kc_ref, sb)
        _fill_keys(vbuf, vp_ref, vc_ref, sb)
        bias_first, bias_other, head0 = _attn_masks()

        def per_block(idx, carry):
            j, r = idx // d, idx % d
            base = j * sb + r
            bias = jnp.where(jnp.logical_and(n == 0, j == 0), bias_first, bias_other)
            qrows = _attn_rows(base, ATTN_Q, d)
            krows = _attn_rows(base, 2 * ATTN_Q, d)
            qs = (_stack_heads(q_ref[qrows, :], head0) * scale).astype(BF16)
            s = _dot(qs, kbuf[krows, :].astype(BF16), 1, 1) + bias
            mx = jnp.max(s, axis=-1, keepdims=True)
            p = jnp.exp(s - mx)
            den = jnp.sum(p, axis=-1, keepdims=True)
            pv = _dot(p.astype(BF16), vbuf[krows, :].astype(BF16), 1, 0) / den
            o_ref[qrows, :] = _unstack_heads(pv, head0)
            lse_ref[qrows, :] = _unstack_heads(jnp.broadcast_to(mx + jnp.log(den), (2 * ATTN_Q, LANES)), head0)
            return carry

        lax.fori_loop(0, nsub * d, per_block, 0, unroll=8)

    def cur(col):
        return pl.BlockSpec((tr, LANES), lambda n, hp: (n, col + hp))

    def prev(col):
        return pl.BlockSpec((sb, LANES), lambda n, hp: (jnp.maximum(n * nsub - 1, 0), col + hp))

    out_spec = pl.BlockSpec((tr, LANES), lambda n, hp: (n, hp))
    return pl.pallas_call(
        body, name=name, grid=(ntiles, 2),
        in_specs=[cur(qc), cur(kc), prev(kc), cur(vc), prev(vc)],
        out_specs=[out_spec, out_spec],
        out_shape=[jax.ShapeDtypeStruct((rows, 2 * LANES), F32), jax.ShapeDtypeStruct((rows, 2 * LANES), F32)],
        scratch_shapes=[pltpu.VMEM((sb + tr, LANES), F32), pltpu.VMEM((sb + tr, LANES), F32)],
        compiler_params=_params(("parallel", "parallel")),
    )(qkv, qkv, qkv, qkv, qkv)


def _attn_merge(outs, lses, *, tm, name):
    rows, w = outs[0].shape

    def body(o0, o1, o2, l0, l1, l2, o_ref, lse_ref):
        a0, a1, a2 = l0[...], l1[...], l2[...]
        mx = jnp.maximum(jnp.maximum(a0, a1), a2)
        e0, e1, e2 = jnp.exp(a0 - mx), jnp.exp(a1 - mx), jnp.exp(a2 - mx)
        den = e0 + e1 + e2
        o_ref[...] = (e0 / den) * o0[...] + (e1 / den) * o1[...] + (e2 / den) * o2[...]
        lse_ref[...] = mx + jnp.log(den)

    row = pl.BlockSpec((tm, w), lambda i: (i, 0))
    return pl.pallas_call(
        body, name=name, grid=(rows // tm,), in_specs=[row] * 6, out_specs=[row, row],
        out_shape=[jax.ShapeDtypeStruct((rows, w), F32), jax.ShapeDtypeStruct((rows, w), F32)],
        compiler_params=_params(("parallel",)),
    )(*outs, *lses)


def _attn_bwd(qkv, do, o, lse, g, d, prev, *, name):
    rows = qkv.shape[0]
    sb, tr, ntiles, nsub = _attn_geometry(rows, d)
    qc, kc, vc = 2 * g, 6 + 2 * g, 12 + 2 * g
    scale = ATTN_HEAD_DIM ** -0.5

    def body(q_ref, kc_ref, kp_ref, vc_ref, vp_ref, do_ref, o_ref, lse_ref, dq_ref, dk_ref, dv_ref,
             kbuf, vbuf, dk_acc, dv_acc):
        n = pl.program_id(1)

        @pl.when(n == 0)
        def _():
            dk_acc[pl.ds(0, tr), :] = jnp.zeros((tr, LANES), F32)
            dv_acc[pl.ds(0, tr), :] = jnp.zeros((tr, LANES), F32)

        @pl.when(n < ntiles)
        def _():
            dk_acc[pl.ds(tr, tr), :] = jnp.zeros((tr, LANES), F32)
            dv_acc[pl.ds(tr, tr), :] = jnp.zeros((tr, LANES), F32)
            _fill_keys(kbuf, kp_ref, kc_ref, sb)
            _fill_keys(vbuf, vp_ref, vc_ref, sb)
            bias_first, bias_other, head0 = _attn_masks()
            lane = lax.broadcasted_iota(jnp.int32, (ATTN_Q, LANES), 1)

            def per_block(idx, carry):
                j, r = idx // d, idx % d
                base = j * sb + r
                bias = jnp.where(jnp.logical_and(n == 0, j == 0), bias_first, bias_other)
                qrows = _attn_rows(base, ATTN_Q, d)
                krows = _attn_rows(base, 2 * ATTN_Q, d)
                arows = _attn_rows(base + (tr - sb), 2 * ATTN_Q, d)
                qs = (_stack_heads(q_ref[qrows, :], head0) * scale).astype(BF16)
                dos = _stack_heads(do_ref[qrows, :], head0)
                dosb = dos.astype(BF16)
                ov = o_ref[qrows, :]
                delta = jnp.sum(dos * jnp.concatenate([ov, ov], axis=0), axis=-1, keepdims=True)
                lsev = lse_ref[qrows, :]
                lse_s = jnp.concatenate(
                    [jnp.sum(jnp.where(lane == h * ATTN_HEAD_DIM, lsev, 0.0), axis=-1, keepdims=True) for h in range(2)], axis=0)
                kb = kbuf[krows, :].astype(BF16)
                vb = vbuf[krows, :].astype(BF16)
                p = jnp.exp(_dot(qs, kb, 1, 1) + bias - lse_s)
                ds = (p * (_dot(dosb, vb, 1, 1) - delta)).astype(BF16)
                dq_ref[qrows, :] = _unstack_heads(_dot(ds, kb, 1, 0), head0) * scale
                dk_acc[arows, :] += _dot(ds, qs, 0, 0)
                dv_acc[arows, :] += _dot(p.astype(BF16), dosb, 0, 0)
                return carry

            lax.fori_loop(0, nsub * d, per_block, 0, unroll=4)

        dk_ref[...] = dk_acc[pl.ds(0, tr), :]
        dv_ref[...] = dv_acc[pl.ds(0, tr), :]
        dk_acc[pl.ds(0, tr), :] = dk_acc[pl.ds(tr, tr), :]
        dv_acc[pl.ds(0, tr), :] = dv_acc[pl.ds(tr, tr), :]

    def cur(n):
        return jnp.minimum(n, ntiles - 1)

    def spec(col, prev):
        if prev:
            return pl.BlockSpec((sb, LANES), lambda hp, n: (jnp.maximum(cur(n) * nsub - 1, 0), col + hp))
        return pl.BlockSpec((tr, LANES), lambda hp, n: (cur(n), col + hp))

    row_spec = pl.BlockSpec((tr, LANES), lambda hp, n: (cur(n), hp))
    dq_out = pl.BlockSpec((tr, LANES), lambda hp, n: (cur(n), 2 * g + hp))
    kv_out = pl.BlockSpec((tr, LANES), lambda hp, n: (jnp.maximum(n - 1, 0), 2 * g + hp))
    shape = jax.ShapeDtypeStruct((rows, len(ATTN_PATTERNS) * 2 * LANES), F32)
    ins = [qkv, qkv, qkv, qkv, qkv, do, o, lse]
    in_specs = [spec(qc, False), spec(kc, False), spec(kc, True), spec(vc, False), spec(vc, True),
                row_spec, row_spec, row_spec]
    aliases = {}
    if prev is not None:
        aliases = {len(ins) + t: t for t in range(3)}
        ins = ins + list(prev)
        in_specs = in_specs + [ANY] * 3
    n_in = len(ins)

    def entry(*refs):
        body(*refs[:8], *refs[n_in:])

    return pl.pallas_call(
        entry, name=name, grid=(2, ntiles + 1),
        in_specs=in_specs,
        out_specs=[dq_out, kv_out, kv_out],
        out_shape=[shape, shape, shape],
        input_output_aliases=aliases,
        scratch_shapes=[pltpu.VMEM((sb + tr, LANES), F32), pltpu.VMEM((sb + tr, LANES), F32),
                        pltpu.VMEM((2 * tr, LANES), F32), pltpu.VMEM((2 * tr, LANES), F32)],
        compiler_params=_params(("parallel", "arbitrary")),
    )(*ins)


def _mem_probs(q, k):
    s = _dot(q.astype(BF16), k.astype(BF16), 1, 1) * (MEM_HEAD_DIM ** -0.5)
    e = jnp.exp(s - jnp.max(s, axis=-1, keepdims=True))
    return e / jnp.sum(e, axis=-1, keepdims=True)


def _mem_attn_fwd(mq, kv, *, tq, name):
    rows = mq.shape[0]

    def body(q_ref, k_ref, v_ref, o_ref):
        p = _mem_probs(q_ref[...], k_ref[...])
        o_ref[...] = _dot(p.astype(BF16), v_ref[...].astype(BF16), 1, 0)

    return pl.pallas_call(
        body, name=name, grid=(rows // tq, MEM_HEADS),
        in_specs=[pl.BlockSpec((tq, LANES), lambda i, h: (i, h)),
                  pl.BlockSpec((MEM_LEN, LANES), lambda i, h: (0, h)),
                  pl.BlockSpec((MEM_LEN, LANES), lambda i, h: (0, MEM_HEADS + h))],
        out_specs=pl.BlockSpec((tq, LANES), lambda i, h: (i, h)),
        out_shape=jax.ShapeDtypeStruct((rows, MEM_HEADS * LANES), F32),
        compiler_params=_params(("parallel", "parallel")),
    )(mq, kv, kv)


def _mem_attn_bwd(mq, kv, dmo, *, tq, name):
    rows = mq.shape[0]
    scale = MEM_HEAD_DIM ** -0.5

    def body(q_ref, k_ref, v_ref, do_ref, dq_ref, dk_ref, dv_ref):
        i = pl.program_id(1)
        qb = q_ref[...].astype(BF16)
        kb = k_ref[...].astype(BF16)
        vb = v_ref[...].astype(BF16)
        dob = do_ref[...].astype(BF16)
        p = _mem_probs(q_ref[...], k_ref[...])
        dp = _dot(dob, vb, 1, 1)
        ds = (p * (dp - jnp.sum(p * dp, axis=-1, keepdims=True)) * scale).astype(BF16)
        dq_ref[...] = _dot(ds, kb, 1, 0).astype(dq_ref.dtype)

        @pl.when(i == 0)
        def _():
            dk_ref[...] = jnp.zeros_like(dk_ref)
            dv_ref[...] = jnp.zeros_like(dv_ref)

        dk_ref[...] += _dot(ds, qb, 0, 0)
        dv_ref[...] += _dot(p.astype(BF16), dob, 0, 0)

    kv_out = pl.BlockSpec((MEM_LEN, LANES), lambda h, i: (0, h))
    kv_shape = jax.ShapeDtypeStruct((MEM_LEN, MEM_HEADS * LANES), F32)
    return pl.pallas_call(
        body, name=name, grid=(MEM_HEADS, rows // tq),
        in_specs=[pl.BlockSpec((tq, LANES), lambda h, i: (i, h)),
                  pl.BlockSpec((MEM_LEN, LANES), lambda h, i: (0, h)),
                  pl.BlockSpec((MEM_LEN, LANES), lambda h, i: (0, MEM_HEADS + h)),
                  pl.BlockSpec((tq, LANES), lambda h, i: (i, h))],
        out_specs=[pl.BlockSpec((tq, LANES), lambda h, i: (i, h)), kv_out, kv_out],
        out_shape=[jax.ShapeDtypeStruct((rows, MEM_HEADS * LANES), BF16), kv_shape, kv_shape],
        compiler_params=_params(("parallel", "arbitrary")),
    )(mq, kv, kv, dmo)


def _resident(shape):
    return pl.BlockSpec(shape, lambda i: (0, 0), pipeline_mode=pl.Buffered(1))


def _branch_merge_fwd(acts, wts, zg, b_gate, *, tm, name):
    rows = zg.shape[0]
    d = wts[0].shape[0]

    def body(s_ref, a_ref, m_ref, ws_ref, wa_ref, wm_ref, zg_ref, b_ref, o_ref):
        gt = _sigmoid(zg_ref[...] + b_ref[...])
        acc = None
        for k, (x_ref, w_ref) in enumerate(((s_ref, ws_ref), (a_ref, wa_ref), (m_ref, wm_ref))):
            term = gt[:, k * d:(k + 1) * d] * _dot(x_ref[...].astype(BF16), w_ref[...], 1, 1)
            acc = term if acc is None else acc + term
        o_ref[...] = acc.astype(BF16)

    return pl.pallas_call(
        body, name=name, grid=(rows // tm,),
        in_specs=[pl.BlockSpec((tm, x.shape[1]), lambda i: (i, 0)) for x in acts] + [_resident(w.shape) for w in wts]
        + [pl.BlockSpec((tm, 3 * d), lambda i: (i, 0)), pl.BlockSpec((1, 3 * d), lambda i: (0, 0))],
        out_specs=pl.BlockSpec((tm, d), lambda i: (i, 0)), out_shape=jax.ShapeDtypeStruct((rows, d), BF16),
        compiler_params=_params(("parallel",)),
    )(*acts, *wts, zg, b_gate)


def _branch_merge_bwd(dmerged, acts, wts, zg, b_gate, *, tm, name, carry=None):
    rows = zg.shape[0]
    d = wts[0].shape[0]

    def body(dm_ref, s_ref, a_ref, m_ref, ws_ref, wa_ref, wm_ref, zg_ref, b_ref,
             ds_ref, da_ref, dmm_ref, dws_ref, dwa_ref, dwm_ref, dzg_ref, db_ref):
        i = pl.program_id(0)

        @pl.when(i == 0)
        def _():
            for r in (dws_ref, dwa_ref, dwm_ref, db_ref):
                r[...] = jnp.zeros_like(r)

        gt = _sigmoid(zg_ref[...] + b_ref[...])
        dm = dm_ref[...]
        groups = ((s_ref, ws_ref, ds_ref, dws_ref), (a_ref, wa_ref, da_ref, dwa_ref), (m_ref, wm_ref, dmm_ref, dwm_ref))
        for k, (x_ref, w_ref, dx_ref, dw_ref) in enumerate(groups):
            cs = pl.ds(k * d, d)
            gk = gt[:, k * d:(k + 1) * d]
            xb = x_ref[...].astype(BF16)
            br = _dot(xb, w_ref[...], 1, 1)
            dbr = (dm * gk).astype(BF16)
            dx_ref[...] = _dot(dbr, w_ref[...], 1, 0)
            dw_ref[...] += _dot(dbr, xb, 0, 0)
            dzg = dm * br * gk * (1.0 - gk)
            dzg_ref[:, cs] = dzg.astype(BF16)
            db_ref[:, cs] += jnp.sum(dzg, axis=0, keepdims=True)

    row = lambda w: pl.BlockSpec((tm, w), lambda i: (i, 0))
    whole = lambda shape: pl.BlockSpec(shape, lambda i: (0, 0))
    res = _call_with_carry(
        body, carry, name=name, grid=(rows // tm,),
        in_specs=[row(d)] + [row(x.shape[1]) for x in acts] + [_resident(w.shape) for w in wts] + [row(3 * d), whole((1, 3 * d))],
        out_specs=[row(x.shape[1]) for x in acts] + [whole(w.shape) for w in wts] + [row(3 * d), whole((1, 3 * d))],
        out_shape=[jax.ShapeDtypeStruct(x.shape, F32) for x in acts] + [jax.ShapeDtypeStruct(w.shape, F32) for w in wts]
        + [jax.ShapeDtypeStruct((rows, 3 * d), BF16), jax.ShapeDtypeStruct((1, 3 * d), F32)],
        scratch=[], operands=[dmerged, *acts, *wts, zg, b_gate], semantics=("arbitrary",))
    return tuple(res) if carry is None else (tuple(res[:8]), list(res[8:]))


def _adamw(w, g, m, v, *, tr, name):
    rows, cols = w.shape[-2:]
    assert rows % tr == 0, (name, rows, tr)

    def body(w_ref, g_ref, m_ref, v_ref, g_out, d_ref, nm_ref, nv_ref):
        gv = g_ref[...]
        m2 = ADAM_B1 * m_ref[...] + (1.0 - ADAM_B1) * gv
        v2 = ADAM_B2 * v_ref[...] + (1.0 - ADAM_B2) * (gv * gv)
        m_hat = m2 / (1.0 - ADAM_B1 ** ADAM_STEP)
        v_hat = v2 / (1.0 - ADAM_B2 ** ADAM_STEP)
        g_out[...] = gv
        d_ref[...] = -ADAM_LR * (m_hat / (jnp.sqrt(v_hat) + ADAM_EPS) + ADAM_WD * w_ref[...])
        nm_ref[...] = m2
        nv_ref[...] = v2

    flat = pl.BlockSpec((tr, cols), lambda i: (i, 0))
    blk = flat if w.ndim == 2 else pl.BlockSpec((None, tr, cols), lambda i: (0, i, 0))
    shape = jax.ShapeDtypeStruct(w.shape, F32)
    return pl.pallas_call(
        body, name=name, grid=(rows // tr,), in_specs=[blk, flat, blk, blk], out_specs=[blk] * 4,
        out_shape=[shape] * 4, compiler_params=_params(("parallel",)),
    )(w, g, m, v)


ANY = pl.BlockSpec(memory_space=pl.ANY)


def _position():
    return lax.axis_index("x"), lax.axis_index("y"), lax.axis_index("c")


def _other_chips(x, y):
    return ((1 - x, y), (x, 1 - y), (1 - x, 1 - y))


def _remote(src, dst, send_sem, recv_sem, dev):
    return pltpu.make_async_remote_copy(src_ref=src, dst_ref=dst, send_sem=send_sem, recv_sem=recv_sem,
                                        device_id=dev, device_id_type=MESH)


def _gather_exchange(shards):
    nb = len(shards)

    def rows_of(i, owner, core):
        rs = shards[i].shape[0]
        return pl.ds(pl.multiple_of(owner * rs + core * (rs // 2), 16), rs // 2)

    def first_leg(ins, outs, send_sems, recv_sems, i, j):
        x, y, c = _position()
        px, py = _other_chips(x, y)[j]
        half = shards[i].shape[0] // 2
        mine = ins[i].at[pl.ds(pl.multiple_of(c * half, 16), half)]
        return _remote(mine, outs[i].at[rows_of(i, 2 * x + y, c)], send_sems.at[i, j], recv_sems.at[i, j], (px, py, c))

    def passed_on(outs, send_sems, recv_sems, i, j, core):
        x, y, c = _position()
        px, py = _other_chips(x, y)[j]
        rows = outs[i].at[rows_of(i, 2 * px + py, core)]
        return _remote(rows, rows, send_sems.at[i, 3 + j], recv_sems.at[i, 3 + j], (x, y, 1 - c))

    def own_block(ins, outs, send_sems, recv_sems, i):
        x, y, c = _position()
        rs = shards[i].shape[0]
        place = outs[i].at[pl.ds(pl.multiple_of((2 * x + y) * rs, 16), rs)]
        return _remote(ins[i], place, send_sems.at[i, 6], recv_sems.at[i, 6], (x, y, 1 - c))

    def start(ins, outs, send_sems, recv_sems):
        for i in range(nb):
            own_block(ins, outs, send_sems, recv_sems, i).start()
            for j in range(3):
                first_leg(ins, outs, send_sems, recv_sems, i, j).start()

    def finish(ins, outs, send_sems, recv_sems):
        x, y, c = _position()
        for i in range(nb):
            for j, (px, py) in enumerate(_other_chips(x, y)):
                landed = outs[i].at[rows_of(i, 2 * px + py, c)]
                _remote(landed, landed, send_sems.at[i, j], recv_sems.at[i, j], (px, py, c)).wait_recv()
                passed_on(outs, send_sems, recv_sems, i, j, c).start()
        for i in range(nb):
            own_block(ins, outs, send_sems, recv_sems, i).wait()
            for j in range(3):
                passed_on(outs, send_sems, recv_sems, i, j, 1 - c).wait_recv()
        for i in range(nb):
            for j in range(3):
                first_leg(ins, outs, send_sems, recv_sems, i, j).wait_send()
                passed_on(outs, send_sems, recv_sems, i, j, c).wait_send()

    return _Exchange(ins=list(shards), outs=[jax.ShapeDtypeStruct((N_CHIPS * s.shape[0], s.shape[1]), s.dtype) for s in shards],
                     aliases={}, sems=[(nb, 7), (nb, 7)], start=start, finish=finish)


def _run_exchange(ex, *, name):
    n_in, n_out = len(ex.ins), len(ex.outs)

    def body(*refs):
        c_in, c_out, sems = refs[:n_in], refs[n_in:n_in + n_out], refs[n_in + n_out:]
        ex.start(c_in, c_out, *sems)
        ex.finish(c_in, c_out, *sems)

    return pl.pallas_call(
        body, name=name, in_specs=[ANY] * n_in, out_specs=[ANY] * n_out, out_shape=list(ex.outs),
        input_output_aliases=dict(ex.aliases),
        scratch_shapes=[pltpu.SemaphoreType.DMA(s) for s in ex.sems],
    )(*ex.ins)


def _row_tile(rows):
    return max(t for t in range(16, min(rows, 512) + 1, 16) if rows % t == 0)


def _halves_exchange(grads):
    nb = len(grads)

    def copies(ins, outs, send_sems, recv_sems):
        x, y, c = _position()
        return [_remote(ins[i].at[:, 1 - c], outs[i], send_sems.at[i], recv_sems.at[i], (x, y, 1 - c)) for i in range(nb)]

    def start(ins, outs, send_sems, recv_sems):
        for cp in copies(ins, outs, send_sems, recv_sems):
            cp.start()

    def finish(ins, outs, send_sems, recv_sems):
        for cp in copies(ins, outs, send_sems, recv_sems):
            cp.wait()

    return _Exchange(ins=list(grads), outs=[jax.ShapeDtypeStruct((N_CHIPS, g.shape[2], g.shape[3]), F32) for g in grads],
                     aliases={}, sems=[(nb,), (nb,)], start=start, finish=finish)


def _join_exchanges(parts):
    assert all(not ex.aliases for ex in parts)

    def split(refs, counts):
        out, at = [], 0
        for k in counts:
            out.append(refs[at:at + k])
            at += k
        return out

    def run(which):
        def go(ins, outs, *sems):
            for ex, i, o, s in zip(parts, split(ins, [len(ex.ins) for ex in parts]), split(outs, [len(ex.outs) for ex in parts]),
                                   split(sems, [len(ex.sems) for ex in parts])):
                getattr(ex, which)(i, o, *s)
        return go

    return _Exchange(ins=[a for ex in parts for a in ex.ins], outs=[a for ex in parts for a in ex.outs], aliases={},
                     sems=[s for ex in parts for s in ex.sems], start=run("start"), finish=run("finish"))


def _pair_sum(g4, got, c_arr, *, name):
    _, _, half, cols = g4.shape
    tr = _row_tile(half)

    def body(c_ref, g_ref, t_ref, p_ref, pb_ref):
        sm = g_ref[...] + t_ref[...]
        p_ref[...] = sm
        pb_ref[...] = sm.astype(BF16)

    blk = pl.BlockSpec((None, tr, cols), lambda j, i, c_ref: (j, i, 0))
    grid_spec = pltpu.PrefetchScalarGridSpec(
        num_scalar_prefetch=1, grid=(N_CHIPS, half // tr),
        in_specs=[pl.BlockSpec((None, None, tr, cols), lambda j, i, c_ref: (j, c_ref[0], i, 0)), blk],
        out_specs=[blk, blk])
    return pl.pallas_call(
        body, name=name, grid_spec=grid_spec,
        out_shape=[jax.ShapeDtypeStruct((N_CHIPS, half, cols), F32), jax.ShapeDtypeStruct((N_CHIPS, half, cols), BF16)],
        compiler_params=_params(("parallel", "parallel")),
    )(c_arr, g4, got)


def _scatter_exchange(parts):
    nb = len(parts)

    def copies(ins, outs, send_sems, recv_sems):
        x, y, c = _position()
        return [_remote(ins[i].at[2 * px + py], outs[i].at[j], send_sems.at[i, j], recv_sems.at[i, j], (px, py, c))
                for i in range(nb) for j, (px, py) in enumerate(_other_chips(x, y))]

    def start(ins, outs, send_sems, recv_sems):
        for cp in copies(ins, outs, send_sems, recv_sems):
            cp.start()

    def finish(ins, outs, send_sems, recv_sems):
        for cp in copies(ins, outs, send_sems, recv_sems):
            cp.wait()

    return _Exchange(ins=list(parts), outs=[jax.ShapeDtypeStruct((3,) + p.shape[1:], p.dtype) for p in parts],
                     aliases={}, sems=[(nb, 3), (nb, 3)], start=start, finish=finish)


def _owner_sum(p, got, chip_arr, c_arr, *, replicated, name):
    _, half, cols = p.shape
    tr = _row_tile(half)

    def body(chip_ref, c_ref, p_ref, r_ref, o_ref):
        o_ref[...] = ((p_ref[...] + r_ref[0].astype(F32)) + r_ref[1].astype(F32)) + r_ref[2].astype(F32)

    if replicated:
        out_spec = pl.BlockSpec((None, None, tr, cols), lambda i, chip_ref, c_ref: (chip_ref[0], c_ref[0], i, 0))
        out_shape = jax.ShapeDtypeStruct((N_CHIPS, 2, half, cols), F32)
    else:
        out_spec = pl.BlockSpec((None, tr, cols), lambda i, chip_ref, c_ref: (c_ref[0], i, 0))
        out_shape = jax.ShapeDtypeStruct((2, half, cols), F32)
    grid_spec = pltpu.PrefetchScalarGridSpec(
        num_scalar_prefetch=2, grid=(half // tr,),
        in_specs=[pl.BlockSpec((None, tr, cols), lambda i, chip_ref, c_ref: (chip_ref[0], i, 0)),
                  pl.BlockSpec((3, tr, cols), lambda i, chip_ref, c_ref: (0, i, 0))],
        out_specs=out_spec)
    return pl.pallas_call(
        body, name=name, grid_spec=grid_spec, out_shape=out_shape,
        compiler_params=_params(("parallel",)),
    )(chip_arr, c_arr, p, got)


def _share_reduced(bufs):
    nb = len(bufs) - 1

    def body(*refs):
        outs = refs[nb + 1:2 * nb + 2]
        send_sems, recv_sems = refs[2 * nb + 2:]
        x, y, c = _position()
        chip = 2 * x + y
        sends = []
        for i in range(nb):
            cp = _remote(outs[i].at[c], outs[i].at[c], send_sems.at[i], recv_sems.at[i], (x, y, 1 - c))
            cp.start()
            sends.append(cp)
        small = outs[nb]
        peers = [(fx, fy, fc) for fx in (0, 1) for fy in (0, 1) for fc in (0, 1) if fx + fy + fc > 0]
        for k, (fx, fy, fc) in enumerate(peers):
            dev = (x ^ fx, y ^ fy, c ^ fc)
            cp = _remote(small.at[chip, c], small.at[chip, c], send_sems.at[nb + k], recv_sems.at[nb + k], dev)
            cp.start()
            sends.append(cp)
        for i in range(nb):
            dst = outs[i].at[1 - c]
            _remote(dst, dst, send_sems.at[i], recv_sems.at[i], (x, y, 1 - c)).wait_recv()
        for k, (fx, fy, fc) in enumerate(peers):
            dst = small.at[2 * (x ^ fx) + (y ^ fy), c ^ fc]
            _remote(dst, dst, send_sems.at[nb + k], recv_sems.at[nb + k], (x ^ fx, y ^ fy, c ^ fc)).wait_recv()
        for cp in sends:
            cp.wait_send()

    n_all = nb + 1
    return pl.pallas_call(
        body, name="grad_share_reduced", in_specs=[ANY] * n_all, out_specs=[ANY] * n_all,
        out_shape=[jax.ShapeDtypeStruct(b.shape, b.dtype) for b in bufs],
        input_output_aliases={i: i for i in range(n_all)},
        scratch_shapes=[pltpu.SemaphoreType.DMA((nb + 7,)), pltpu.SemaphoreType.DMA((nb + 7,))],
    )(*bufs)


class _GradReducer:
    def __init__(self, c_arr, chip_arr):
        self.c_arr, self.chip_arr = c_arr, chip_arr
        self.full, self.pairs, self.landed = {}, {}, {}

    def swap(self, names, grads):
        for n, g in zip(names, grads):
            self.full[n] = g.reshape(N_CHIPS, 2, g.shape[0] // (2 * N_CHIPS), g.shape[1])
        return _halves_exchange([self.full[n] for n in names])

    def swapped(self, names, bufs):
        for n, t in zip(names, bufs):
            self.pairs[n] = _pair_sum(self.full[n], t, self.c_arr, name="grad_pair_sum_" + n)

    def scatter(self, names):
        return _scatter_exchange([self.pairs[n][1] for n in names])

    def collect(self, names, bufs):
        self.landed.update(zip(names, bufs))

    def swap_now(self, names, grads):
        self.swapped(names, _run_exchange(self.swap(names, grads), name="grad_exchange_" + names[0]))

    def finish(self, names, grads, order):
        self.swap_now(names, grads)
        self.collect(names, _run_exchange(self.scatter(names), name="grad_scatter_" + names[0]))
        totals = [_owner_sum(self.pairs[n][0], self.landed[n], self.chip_arr, self.c_arr, replicated=(n == order[-1]),
                             name="grad_owner_sum_" + n) for n in order]
        return _share_reduced(totals)


def _pack_small(vals):
    flat = jnp.concatenate([vals[name].reshape(-1) for name, _ in SMALL])
    return jnp.pad(flat, (0, N_CHIPS * SMALL_ROWS * 1024 - SMALL_ELEMS)).reshape(N_CHIPS * SMALL_ROWS, 1024)


def _unpack_small(buf):
    flat = buf.reshape(-1)
    out, off = {}, 0
    for name, shape in SMALL:
        n = int(np.prod(shape))
        out[name] = flat[off:off + n].reshape(shape)
        off += n
    return out


EARLY_REDUCED = (("w_down",), ("w_up",), ("w_o", "w_ssm_br", "w_attn_br", "w_mem_br", "w_glu", "w_mem_kv"), ("w_in",))


def _device_step(x, mem, tgt, w, p, *, shards, reducer):
    rows = x.shape[0]
    w = dict(w)
    early = EARLY_REDUCED
    gb = {}
    gather_pending = shards is not None

    def riding(*stages):
        if reducer is None or not stages:
            return None
        return _join_exchanges([reducer.swap(names, [gb[n] for n in names]) if kind == "swap" else reducer.scatter(names)
                                for kind, names in stages])

    def arrived(stages, res):
        if reducer is None or not stages:
            return res
        main, bufs = res
        for kind, names in stages:
            (reducer.swapped if kind == "swap" else reducer.collect)(names, bufs[:len(names)])
            bufs = bufs[len(names):]
        return main

    def fetching(names):
        return _gather_exchange([shards[n] for n in names]) if gather_pending else None

    def fetched(names, res):
        if not gather_pending:
            return res
        w.update(zip(names, res[1]))
        return res[0]

    first_use = (("w_glu", "w_ssm_br", "w_attn_br", "w_mem_kv", "w_mem_br", "w_o"), ("w_up",), ("w_down",))
    g1, gm, g2 = p["norm1_g"], p["mem_norm_g"], p["norm2_g"]
    gf = p["final_g"].reshape(1, D_MODEL)
    ssm_args = (p["ssm_lambda_re"][0], p["ssm_lambda_im"][0], p["ssm_log_dt"][0], p["ssm_b_re"][0],
                p["ssm_b_im"][0], p["ssm_c_re"][0], p["ssm_c_im"][0])
    (a_lay, b_blk, c_blk), ssm_vjp = jax.vjp(_ssm_matrices, *ssm_args)
    a_conj = a_lay * _to_scan_layout(jnp.stack([jnp.ones((N_STATES,), F32), -jnp.ones((N_STATES,), F32)]))[None, :]
    dd = p["ssm_d"].reshape(1, SSM_WIDTH)
    win_t = w["w_in"]
    mm = _matmul

    n1 = _rmsnorm_fwd(x, g1, tm=512, name="norm1")
    u = mm(n1, win_t, m=rows, n=512, k=1024, tb=True, tm=2048, tn=512, tk=1024, out_dtypes=(F32,), name="in_u")
    qkv = fetched(first_use[0], mm(n1, win_t, m=rows, n=2304, k=1024, tb=True, tm=1024, tn=768, tk=1024,
                                   b_row0=OFF_QKV, out_dtypes=(F32,), carry=fetching(first_use[0]), name="in_qkv"))
    mq = mm(n1, win_t, m=rows, n=512, k=1024, tb=True, tm=2048, tn=512, tk=1024, b_row0=OFF_MQ,
            out_dtypes=(F32,), name="in_mq")
    zg = fetched(first_use[1], mm(n1, win_t, m=rows, n=3072, k=1024, tb=True, tm=1024, tn=1024, tk=1024,
                                  b_row0=OFF_ZG, out_dtypes=(F32,), carry=fetching(first_use[1]), name="in_zg"))

    u_i = _interleave(u)
    ends = _ssm_ends(a_lay, u_i, b_blk, transpose=False, reverse=False, tt=512, name="ssm_fwd_ends")
    s, ys_i, s_entry = _ssm_fwd(a_lay, u_i, b_blk, c_blk, ends, tt=512, name="ssm_fwd")
    ys = _deinterleave(ys_i)
    y0, tglu, y2 = _glu_fwd(ys, u, dd, w["w_glu"], p["b_glu"], tm=512, name="glu_fwd")

    outs, lses = [], []
    for g, (_, d) in enumerate(ATTN_PATTERNS):
        o_g, lse_g = _attn_fwd(qkv, g, d, name=f"attn_fwd_{g}")
        outs.append(o_g)
        lses.append(lse_g)
    o, lse = _attn_merge(outs, lses, tm=1024, name="attn_merge")

    mn = _rmsnorm_fwd(mem, gm, tm=MEM_LEN, name="mem_norm")
    kv = mm(mn, w["w_mem_kv"], m=MEM_LEN, n=1024, k=1024, tm=MEM_LEN, tn=1024, tk=1024, out_dtypes=(F32,), name="mem_kv")
    mo = _mem_attn_fwd(mq, kv, tq=1024, name="mem_attn_fwd")

    branch_acts = (y2, o, mo)
    branch_wts = (w["w_ssm_br"], w["w_attn_br"], w["w_mem_br"])
    merged = _branch_merge_fwd(branch_acts, branch_wts, zg, p["b_gate"], tm=256, name="branch_merge_fwd")
    h1, n2 = mm(merged, w["w_o"], m=rows, n=1024, k=1024, tm=1024, tn=1024, tk=1024, out_dtypes=(F32, BF16),
                aux=((x, "mn"), (g2, "row")), epilogue=_residual_norm_epilogue, name="out_proj")
    relu2 = lambda acc: (jnp.square(jnp.maximum(acc, 0.0)),)
    act = fetched(first_use[2], mm(n2, w["w_up"], m=rows, n=D_FF, k=1024, tb=True, tm=1024, tn=1024, tk=1024,
                                   out_dtypes=(BF16,), epilogue=relu2, carry=fetching(first_use[2]), name="mlp_up"))
    dh2, d_gf, sq_err = mm(act, w["w_down"], m=rows, n=1024, k=D_FF, tm=1024, tn=1024, tk=1024, out_dtypes=(F32,),
                           aux=((h1, "mn"), (tgt, "mn"), (gf, "row")), epilogue=_loss_head_epilogue, n_sums=2, name="mlp_down")
    loss = (0.5 / D_MODEL) * jnp.sum(sq_err)

    gs = {"final_g": d_gf.reshape(D_MODEL)}
    drelu2 = lambda acc, actv: (acc * (2.0 * jnp.sqrt(actv.astype(F32))),)
    dup = mm(dh2, w["w_down"], m=rows, n=D_FF, k=1024, tb=True, tm=1024, tn=2048, tk=1024, out_dtypes=(BF16,),
             aux=((act, "mn"),), epilogue=drelu2, name="d_act")
    gb["w_down"] = mm(act, dh2, m=D_FF, n=1024, k=rows, ta=True, tm=1024, tn=1024, tk=1024, out_dtypes=(F32,), name="dw_down")
    stages = (("swap", early[0]),)
    gb["w_up"] = arrived(stages, mm(dup, n2, m=D_FF, n=1024, k=rows, ta=True, tm=1024, tn=1024, tk=1024,
                                    out_dtypes=(F32,), carry=riding(*stages), name="dw_up"))
    stages = (("scatter", early[0]), ("swap", early[1]))
    dh1, gs["norm2_g"] = arrived(stages, mm(dup, w["w_up"], m=rows, n=1024, k=D_FF, tm=1024, tn=1024, tk=1024,
                                            out_dtypes=(F32,), aux=((h1, "mn"), (dh2, "mn"), (g2, "row")),
                                            epilogue=_rmsnorm_bwd_epilogue, n_sums=1, carry=riding(*stages), name="d_n2"))
    dmerged = mm(dh1, w["w_o"], m=rows, n=1024, k=1024, tb=True, tm=1024, tn=1024, tk=1024, out_dtypes=(F32,), name="d_merged")
    gb["w_o"] = mm(merged, dh1, m=1024, n=1024, k=rows, ta=True, tm=1024, tn=1024, tk=1024, out_dtypes=(F32,), name="dw_o")
    stages = (("scatter", early[1]),)
    (dy2, do, dmo, gb["w_ssm_br"], gb["w_attn_br"], gb["w_mem_br"], dzg, gs["b_gate"]) = arrived(stages, _branch_merge_bwd(
        dmerged, branch_acts, branch_wts, zg, p["b_gate"], tm=256, carry=riding(*stages), name="branch_merge_bwd"))

    dy0, dt, y1, gs["b_glu"], d_dd = _glu_bwd(dy2, y0, tglu, u, w["w_glu"], tm=512, name="glu_bwd")
    gs["ssm_d"] = d_dd.reshape(1, SSM_GROUPS, SSM_GROUP_SIZE)
    gb["w_glu"] = mm(y1, dt, m=512, n=512, k=rows, ta=True, tm=512, tn=512, tk=1024, out_dtypes=(F32,), name="dw_glu")
    dy0_i = _interleave(dy0)
    lam_ends = _ssm_ends(a_conj, dy0_i, c_blk, transpose=True, reverse=True, tt=512, name="ssm_bwd_ends")
    du_i, d_b_blk, d_c_blk, d_a_lay = _ssm_bwd(a_conj, dy0_i, u_i, s, s_entry, b_blk, c_blk, dd, lam_ends, tt=512,
                                                name="ssm_bwd")
    du = _deinterleave(du_i)
    d_ssm = ssm_vjp((d_a_lay, d_b_blk, d_c_blk))
    for name, val in zip(("ssm_lambda_re", "ssm_lambda_im", "ssm_log_dt", "ssm_b_re", "ssm_b_im", "ssm_c_re", "ssm_c_im"), d_ssm):
        gs[name] = val[None]

    dqkv = None
    for g, (_, d) in enumerate(ATTN_PATTERNS):
        dqkv = _attn_bwd(qkv, do, o, lse, g, d, dqkv, name=f"attn_bwd_{g}")

    dmq, dmk, dmv = _mem_attn_bwd(mq, kv, dmo, tq=1024, name="mem_attn_bwd")
    dkv = jnp.concatenate([dmk, dmv], axis=1)
    gb["w_mem_kv"] = mm(mn, dkv, m=1024, n=1024, k=MEM_LEN, ta=True, tm=1024, tn=1024, tk=MEM_LEN, out_dtypes=(F32,), name="dw_mem_kv")
    dmn = mm(dkv, w["w_mem_kv"], m=MEM_LEN, n=1024, k=1024, tb=True, tm=MEM_LEN, tn=1024, tk=1024, out_dtypes=(F32,), name="d_mn")
    _, gs["mem_norm_g"] = _rmsnorm_bwd(mem, gm, dmn, None, tm=MEM_LEN, name="mem_norm_bwd")

    pieces = ((du, OFF_U, "u"), (dqkv[0], OFF_QKV, "q"), (dqkv[1], OFF_QKV + 768, "k"), (dqkv[2], OFF_QKV + 1536, "v"),
              (dmq, OFF_MQ, "mq"), (dzg, OFF_ZG, "zg"))
    dw_rows = []
    for piece, off, tag in pieces:
        width = piece.shape[1]
        tmw = 1024 if width % 1024 == 0 else (768 if width == 768 else 512)
        stages = {"q": (("swap", early[2]),), "zg": (("scatter", early[2]),)}.get(tag, ())
        dw_rows.append(arrived(stages, mm(piece, n1, m=width, n=1024, k=rows, ta=True, tm=tmw, tn=1024, tk=1024,
                                          out_dtypes=(F32,), carry=riding(*stages), name="dw_in_" + tag)))
    gb["w_in"] = jnp.concatenate(dw_rows, axis=0)
    if reducer is not None:
        reducer.swap_now(early[3], [gb["w_in"]])
    stages = (("scatter", early[3]),)
    dx, gs["norm1_g"] = arrived(stages, _sum_matmul(
        [piece for piece, _, _ in pieces], win_t, [off for _, off, _ in pieces], tm=512,
        aux=((x, "mn"), (dh1, "mn"), (g1, "row")), epilogue=_rmsnorm_bwd_epilogue, n_sums=1,
        carry=riding(*stages), name="d_n1"))
    return loss, dx, gb, gs


def kernel(x, mem, norm1_g, mem_norm_g, w_in, b_gate, ssm_lambda_re, ssm_lambda_im, ssm_log_dt, ssm_b_re, ssm_b_im, ssm_c_re, ssm_c_im, ssm_d, w_glu, b_glu, w_ssm_br, w_attn_br, w_mem_kv, w_mem_br, w_o, norm2_g, w_up, w_down, final_g, loss_target, m_norm1_g, m_mem_norm_g, m_w_in, m_b_gate, m_ssm_lambda_re, m_ssm_lambda_im, m_ssm_log_dt, m_ssm_b_re, m_ssm_b_im, m_ssm_c_re, m_ssm_c_im, m_ssm_d, m_w_glu, m_b_glu, m_w_ssm_br, m_w_attn_br, m_w_mem_kv, m_w_mem_br, m_w_o, m_norm2_g, m_w_up, m_w_down, m_final_g, v_norm1_g, v_mem_norm_g, v_w_in, v_b_gate, v_ssm_lambda_re, v_ssm_lambda_im, v_ssm_log_dt, v_ssm_b_re, v_ssm_b_im, v_ssm_c_re, v_ssm_c_im, v_ssm_d, v_w_glu, v_b_glu, v_w_ssm_br, v_w_attn_br, v_w_mem_kv, v_w_mem_br, v_w_o, v_norm2_g, v_w_up, v_w_down, v_final_g):
    env = dict(locals())
    weights = {n: env[n] for n in WEIGHT_ORDER}
    moms = {n: env["m_" + n] for n in WEIGHT_ORDER}
    vels = {n: env["v_" + n] for n in WEIGHT_ORDER}
    def shard2d(a):
        return a.reshape(a.shape[-2], a.shape[-1])

    chip = 2 * lax.axis_index("x") + lax.axis_index("y")
    wire = [shard2d(weights[n]).astype(BF16) for n, _, _ in BIG]
    wire = dict(zip([n for n, _, _ in BIG], [s.T if tr else s for s, (_, tr, _) in zip(wire, BIG)]))
    w_in_full = _run_exchange(_gather_exchange([wire.pop("w_in")]), name="all_gather_w_in")[0]
    small = {n: weights[n] for n, _ in SMALL}

    reducer = _GradReducer(lax.axis_index("c").astype(jnp.int32).reshape(1), chip.astype(jnp.int32).reshape(1))
    loss, dx, gb, gs = _device_step(x[0], mem[0], loss_target[0], {"w_in": w_in_full}, small, shards=wire, reducer=reducer)
    *shards, small_grad = reducer.finish(["small"], [_pack_small(gs)], [n for n, _, _ in BIG] + ["small"])
    grads = {}
    for (n, tr, _), sh in zip(BIG, shards):
        sh = sh.reshape(2 * sh.shape[1], sh.shape[2])
        grads[n] = sh.T if tr else sh
    small_grad = small_grad.reshape(N_CHIPS * SMALL_ROWS, 1024)
    grads_small = _unpack_small(small_grad)

    delta, new_m, new_v = {}, {}, {}
    for n, _, _ in BIG:
        grads[n], delta[n], new_m[n], new_v[n] = _adamw(weights[n], grads[n], moms[n], vels[n],
                                                        tr=min(weights[n].shape[-2], 256), name="adamw_" + n)
    _, ds_, ms_, vs_ = _adamw(_pack_small(small), small_grad,
                              _pack_small({n: moms[n] for n, _ in SMALL}), _pack_small({n: vels[n] for n, _ in SMALL}),
                              tr=N_CHIPS * SMALL_ROWS, name="adamw_small")
    for dst, buf in ((delta, ds_), (new_m, ms_), (new_v, vs_)):
        dst.update(_unpack_small(buf))
    grads.update(grads_small)

    total_loss = lax.psum(loss, ("x", "y", "c"))
    return (total_loss, dx[None], *[grads[n] for n in WEIGHT_ORDER], *[delta[n] for n in WEIGHT_ORDER],
            *[new_m[n] for n in WEIGHT_ORDER], *[new_v[n] for n in WEIGHT_ORDER])
```

```python
import functools
import math

import numpy as np
import jax
import jax.numpy as jnp
from jax import lax
from jax.experimental import pallas as pl
from jax.experimental.pallas import tpu as pltpu

F32 = jnp.float32
BF16 = jnp.bfloat16

D_MODEL = 1024
SSM_GROUPS = 32
SSM_GROUP_SIZE = 16
SSM_STATE = 64
SSM_WIDTH = 512
N_STATES = SSM_GROUPS * SSM_STATE
SCAN_CB = 1024
ATTN_PATTERNS = ((128, 1), (512, 4), (2048, 16))
ATTN_HEAD_DIM = 64
ATTN_Q = 128
MEM_LEN = 256
MEM_HEAD_DIM = 128
MEM_HEADS = 4
D_FF = 4096
OFF_U, OFF_QKV, OFF_MQ, OFF_ZG = 0, 512, 2816, 3328
IN_WIDTH = 6400
RMS_EPS = 1e-6
NEG_INF = -1e30
ADAM_LR, ADAM_B1, ADAM_B2, ADAM_EPS, ADAM_WD, ADAM_STEP = 0.001, 0.9, 0.999, 1e-08, 0.01, 10

VMEM_LIMIT_BYTES = 48 * 1024 * 1024
VMEM_LIMIT_WIDE_BYTES = 56 * 1024 * 1024
LANES = 128
MESH = pl.DeviceIdType.MESH
N_CHIPS = 4

SCAN_SEGS = 8
SCAN_GROUPS = SCAN_CB // SSM_STATE

BIG = (("w_in", True, (6400, 1024)), ("w_glu", False, (512, 512)), ("w_ssm_br", True, (1024, 512)),
       ("w_attn_br", True, (1024, 256)), ("w_mem_kv", False, (1024, 1024)), ("w_mem_br", True, (1024, 512)),
       ("w_o", False, (1024, 1024)), ("w_up", True, (4096, 1024)), ("w_down", False, (4096, 1024)))
SMALL = (("norm1_g", (1, 1024)), ("mem_norm_g", (1, 1024)), ("b_gate", (1, 3072)),
         ("ssm_lambda_re", (1, 32, 64)), ("ssm_lambda_im", (1, 32, 64)), ("ssm_log_dt", (1, 32)),
         ("ssm_b_re", (1, 32, 64, 16)), ("ssm_b_im", (1, 32, 64, 16)), ("ssm_c_re", (1, 32, 16, 64)),
         ("ssm_c_im", (1, 32, 16, 64)), ("ssm_d", (1, 32, 16)), ("b_glu", (1, 512)),
         ("norm2_g", (1, 1024)), ("final_g", (1024,)))
WEIGHT_ORDER = ("norm1_g", "mem_norm_g", "w_in", "b_gate", "ssm_lambda_re", "ssm_lambda_im", "ssm_log_dt",
                "ssm_b_re", "ssm_b_im", "ssm_c_re", "ssm_c_im", "ssm_d", "w_glu", "b_glu", "w_ssm_br",
                "w_attn_br", "w_mem_kv", "w_mem_br", "w_o", "norm2_g", "w_up", "w_down", "final_g")
SMALL_ELEMS = sum(int(np.prod(s)) for _, s in SMALL)
SMALL_ROWS = 64


def _params(sem, vmem=VMEM_LIMIT_BYTES):
    return pltpu.CompilerParams(dimension_semantics=sem, vmem_limit_bytes=vmem)


def _sigmoid(v):
    return 0.5 * jnp.tanh(0.5 * v) + 0.5


_GELU_C = math.sqrt(2.0 / math.pi)


def _gelu(v):
    return 0.5 * v * (1.0 + jnp.tanh(_GELU_C * (v + 0.044715 * v * v * v)))


def _gelu_grad(v):
    th = jnp.tanh(_GELU_C * (v + 0.044715 * v * v * v))
    return 0.5 * (1.0 + th) + 0.5 * v * (1.0 - th * th) * _GELU_C * (1.0 + 3.0 * 0.044715 * v * v)


def _dot(a, b, ca, cb):
    return lax.dot_general(a, b, (((ca,), (cb,)), ((), ())), preferred_element_type=F32)


class _Exchange:
    def __init__(self, ins, outs, aliases, sems, start, finish):
        self.ins, self.outs, self.aliases, self.sems, self.start, self.finish = ins, outs, aliases, sems, start, finish


def _matmul(a, b, *, m, n, k, ta=False, tb=False, tm, tn, tk, out_dtypes, name,
            a_off=(0, 0), b_off=(0, 0), b_row0=None, aux=(), epilogue=None, n_sums=0, carry=None):
    assert m % tm == 0 and n % tn == 0 and k % tk == 0, (name, m, n, k, tm, tn, tk)
    nk = k // tk
    n_aux = len(aux)
    n_tiles = len(out_dtypes)
    n_out = n_tiles + n_sums
    ar, ac = a_off
    br, bc = b_off
    if ta:
        a_spec = pl.BlockSpec((tk, tm), lambda i, j, kk: (kk + ar, i + ac))
    else:
        a_spec = pl.BlockSpec((tm, tk), lambda i, j, kk: (i + ar, kk + ac))
    if tb:
        if b_row0 is None:
            b_spec = pl.BlockSpec((tn, tk), lambda i, j, kk: (j + br, kk + bc))
        else:
            assert b_row0 % LANES == 0 and tn % LANES == 0 and tk % LANES == 0
            b_spec = pl.BlockSpec((pl.Element(tn), pl.Element(tk)),
                                  lambda i, j, kk: (pl.multiple_of(b_row0 + j * tn, LANES), pl.multiple_of((kk + bc) * tk, LANES)))
    else:
        b_spec = pl.BlockSpec((tk, tn), lambda i, j, kk: (kk + br, j + bc))
    aux_specs = []
    for _, kind in aux:
        if kind == "mn":
            aux_specs.append(pl.BlockSpec((tm, tn), lambda i, j, kk: (i, j)))
        else:
            aux_specs.append(pl.BlockSpec((1, tn), lambda i, j, kk: (0, j)))
    ca = 0 if ta else 1
    cb = 1 if tb else 0

    def finish(acc, aux_refs, out_refs, row_tile):
        outs = (acc,) if epilogue is None else epilogue(acc, *[r[...] for r in aux_refs])
        for o_ref, o in zip(out_refs[:n_tiles], outs[:n_tiles]):
            o_ref[...] = o.astype(o_ref.dtype)
        _accumulate_over_rows(out_refs[n_tiles:], outs[n_tiles:], row_tile)

    def body(a_ref, b_ref, *rest):
        aux_refs = rest[:n_aux]
        out_refs = rest[n_aux:n_aux + n_out]
        row_tile = pl.program_id(0)
        prod = _dot(a_ref[...].astype(BF16), b_ref[...].astype(BF16), ca, cb)
        if nk == 1:
            finish(prod, aux_refs, out_refs, row_tile)
            return
        acc_ref = rest[n_aux + n_out]
        kk = pl.program_id(2)

        @pl.when(kk == 0)
        def _():
            acc_ref[...] = prod

        @pl.when(jnp.logical_and(kk > 0, kk < nk - 1))
        def _():
            acc_ref[...] += prod

        @pl.when(kk == nk - 1)
        def _():
            finish(acc_ref[...] + prod, aux_refs, out_refs, row_tile)

    tile = pl.BlockSpec((tm, tn), lambda i, j, kk: (i, j))
    col_sum = pl.BlockSpec((1, tn), lambda i, j, kk: (0, j))
    res = _call_with_carry(
        body, carry, name=name, grid=(m // tm, n // tn, nk), in_specs=[a_spec, b_spec] + aux_specs,
        out_specs=[tile] * n_tiles + [col_sum] * n_sums,
        out_shape=[jax.ShapeDtypeStruct((m, n), dt) for dt in out_dtypes] + [jax.ShapeDtypeStruct((1, n), F32)] * n_sums,
        scratch=[pltpu.VMEM((tm, tn), F32)] if nk > 1 else [], operands=[a, b] + [x for x, _ in aux],
        semantics=("arbitrary" if n_sums else "parallel", "parallel", "arbitrary"))
    main = res[0] if n_out == 1 else tuple(res[:n_out])
    return main if carry is None else (main, list(res[n_out:]))


def _accumulate_over_rows(sum_refs, terms, row_tile):
    for s_ref, term in zip(sum_refs, terms):
        @pl.when(row_tile == 0)
        def _():
            s_ref[...] = term

        @pl.when(row_tile > 0)
        def _():
            s_ref[...] += term


def _call_with_carry(body, carry, *, name, grid, in_specs, out_specs, out_shape, scratch, operands, semantics,
                     vmem=VMEM_LIMIT_BYTES):
    if carry is None:
        return pl.pallas_call(body, name=name, grid=grid, in_specs=in_specs, out_specs=out_specs, out_shape=out_shape,
                              scratch_shapes=scratch, compiler_params=_params(semantics, vmem))(*operands)
    n_in, n_cin, n_out, n_cout, n_scr = len(operands), len(carry.ins), len(out_shape), len(carry.outs), len(scratch)

    def hosted(*refs):
        main_in, c_in = refs[:n_in], refs[n_in:n_in + n_cin]
        main_out = refs[n_in + n_cin:n_in + n_cin + n_out]
        c_out = refs[n_in + n_cin + n_out:n_in + n_cin + n_out + n_cout]
        rest = refs[n_in + n_cin + n_out + n_cout:]
        ids = [pl.program_id(t) for t in range(len(grid))]
        first = functools.reduce(jnp.logical_and, [i == 0 for i in ids])
        last = functools.reduce(jnp.logical_and, [i == g - 1 for i, g in zip(ids, grid)])

        @pl.when(first)
        def _():
            carry.start(c_in, c_out, *rest[n_scr:])

        body(*main_in, *main_out, *rest[:n_scr])

        @pl.when(last)
        def _():
            carry.finish(c_in, c_out, *rest[n_scr:])

    return pl.pallas_call(
        hosted, name=name, grid=grid,
        in_specs=list(in_specs) + [ANY] * n_cin, out_specs=list(out_specs) + [ANY] * n_cout,
        out_shape=list(out_shape) + list(carry.outs),
        input_output_aliases={n_in + i: n_out + o for i, o in carry.aliases.items()},
        scratch_shapes=list(scratch) + [pltpu.SemaphoreType.DMA(s) for s in carry.sems],
        compiler_params=_params(("arbitrary",) * len(grid), vmem),
    )(*operands, *carry.ins)


def _sum_matmul(pieces, b, offs, *, tm, name, aux=(), epilogue=None, n_sums=0, carry=None, vmem=VMEM_LIMIT_BYTES):
    m = pieces[0].shape[0]
    n = b.shape[1]
    npieces, n_aux = len(pieces), len(aux)

    def body(*refs):
        b_ref = refs[npieces]
        aux_refs = refs[npieces + 1:npieces + 1 + n_aux]
        out_refs = refs[npieces + 1 + n_aux:]
        acc = None
        for p_ref, off in zip(refs[:npieces], offs):
            part = _dot(p_ref[...].astype(BF16), b_ref[pl.ds(off, p_ref.shape[1]), :], 1, 0)
            acc = part if acc is None else acc + part
        outs = (acc,) if epilogue is None else epilogue(acc, *[r[...] for r in aux_refs])
        out_refs[0][...] = outs[0]
        _accumulate_over_rows(out_refs[1:], outs[1:], pl.program_id(0))

    row = pl.BlockSpec((tm, n), lambda i: (i, 0))
    vec = pl.BlockSpec((1, n), lambda i: (0, 0))
    res = _call_with_carry(
        body, carry, name=name, grid=(m // tm,),
        in_specs=[pl.BlockSpec((tm, p.shape[1]), lambda i: (i, 0)) for p in pieces] + [_resident(b.shape)]
        + [row if kind == "mn" else vec for _, kind in aux],
        out_specs=[row] + [vec] * n_sums,
        out_shape=[jax.ShapeDtypeStruct((m, n), F32)] + [jax.ShapeDtypeStruct((1, n), F32)] * n_sums,
        scratch=[], operands=list(pieces) + [b] + [x for x, _ in aux], semantics=("arbitrary" if n_sums else "parallel",),
        vmem=vmem)
    main = res[0] if n_sums == 0 else tuple(res[:1 + n_sums])
    return main if carry is None else (main, list(res[1 + n_sums:]))


def _rmsnorm_fwd(x, g, *, tm, name):
    rows, d = x.shape

    def body(x_ref, g_ref, o_ref):
        xv = x_ref[...]
        r = lax.rsqrt(jnp.mean(xv * xv, axis=-1, keepdims=True) + RMS_EPS)
        o_ref[...] = (xv * r * g_ref[...]).astype(o_ref.dtype)

    return pl.pallas_call(
        body, name=name, grid=(rows // tm,),
        in_specs=[pl.BlockSpec((tm, d), lambda i: (i, 0)), pl.BlockSpec((1, d), lambda i: (0, 0))],
        out_specs=pl.BlockSpec((tm, d), lambda i: (i, 0)),
        out_shape=jax.ShapeDtypeStruct((rows, d), BF16),
        compiler_params=_params(("parallel",)),
    )(x, g)


def _residual_norm_epilogue(acc, xv, gv):
    h = acc + xv
    r = lax.rsqrt(jnp.mean(h * h, axis=-1, keepdims=True) + RMS_EPS)
    return h, h * r * gv


def _rmsnorm_bwd_epilogue(dy, xv, resv, gv):
    r = lax.rsqrt(jnp.mean(xv * xv, axis=-1, keepdims=True) + RMS_EPS)
    xhat = xv * r
    dyg = dy * gv
    dx = r * (dyg - xhat * jnp.mean(dyg * xhat, axis=-1, keepdims=True)) + resv
    return dx, jnp.sum(dy * xhat, axis=0, keepdims=True)


def _rmsnorm_bwd(x, g, dy, res, *, tm, name):
    rows, d = x.shape
    has_res = res is not None

    def body(x_ref, g_ref, dy_ref, *rest):
        if has_res:
            res_ref, dx_ref, dg_ref = rest
        else:
            dx_ref, dg_ref = rest
        i = pl.program_id(0)
        xv = x_ref[...]
        r = lax.rsqrt(jnp.mean(xv * xv, axis=-1, keepdims=True) + RMS_EPS)
        xhat = xv * r
        dyv = dy_ref[...]
        dyg = dyv * g_ref[...]
        dx = r * (dyg - xhat * jnp.mean(dyg * xhat, axis=-1, keepdims=True))
        if has_res:
            dx = dx + res_ref[...]
        dx_ref[...] = dx

        @pl.when(i == 0)
        def _():
            dg_ref[...] = jnp.zeros_like(dg_ref)

        dg_ref[...] += jnp.sum(dyv * xhat, axis=0, keepdims=True)

    row_spec = pl.BlockSpec((tm, d), lambda i: (i, 0))
    vec_spec = pl.BlockSpec((1, d), lambda i: (0, 0))
    ins = [x, g, dy] + ([res] if has_res else [])
    return pl.pallas_call(
        body, name=name, grid=(rows // tm,),
        in_specs=[row_spec, vec_spec, row_spec] + ([row_spec] if has_res else []),
        out_specs=[row_spec, vec_spec],
        out_shape=[jax.ShapeDtypeStruct((rows, d), F32), jax.ShapeDtypeStruct((1, d), F32)],
        compiler_params=_params(("arbitrary",)),
    )(*ins)


def _loss_head_epilogue(acc, hv, tgtv, gv):
    xv = acc + hv
    r = lax.rsqrt(jnp.mean(xv * xv, axis=-1, keepdims=True) + RMS_EPS)
    xhat = xv * r
    err = xhat * gv - tgtv
    dyv = err * (1.0 / D_MODEL)
    dyg = dyv * gv
    dh = r * (dyg - xhat * jnp.mean(dyg * xhat, axis=-1, keepdims=True))
    return dh, jnp.sum(dyv * xhat, axis=0, keepdims=True), jnp.sum(err * err, axis=0, keepdims=True)


def _to_scan_layout(v):
    lead = v.shape[:-2]
    v = v.reshape(lead + (2, N_STATES // SCAN_CB, SCAN_CB))
    v = jnp.swapaxes(v, -3, -2)
    return v.reshape(lead + (2 * N_STATES,))


def _ssm_matrices(lam_re, lam_im, log_dt, b_re, b_im, c_re, c_im):
    dt = jnp.exp(log_dt)[:, None]
    mag = jnp.exp(lam_re * dt)
    a_re, a_im = mag * jnp.cos(lam_im * dt), mag * jnp.sin(lam_im * dt)
    nr, ni = a_re - 1.0, a_im
    den = lam_re * lam_re + lam_im * lam_im
    coef_re = (nr * lam_re + ni * lam_im) / den
    coef_im = (ni * lam_re - nr * lam_im) / den
    bb_re = coef_re[..., None] * b_re - coef_im[..., None] * b_im
    bb_im = coef_re[..., None] * b_im + coef_im[..., None] * b_re
    a_lay = _to_scan_layout(jnp.stack([a_re.reshape(-1), a_im.reshape(-1)], axis=0))[None, :]
    nblk = SSM_GROUPS // SCAN_GROUPS
    eye = jnp.eye(SCAN_GROUPS, dtype=F32)

    def b_block(bb):
        bb = bb.reshape(nblk, SCAN_GROUPS, SSM_STATE, SSM_GROUP_SIZE)
        return jnp.einsum("gk,jkph->jghkp", eye, bb).reshape(nblk, SCAN_GROUPS * SSM_GROUP_SIZE, SCAN_CB)

    b_blk = jnp.concatenate([b_block(bb_re), b_block(bb_im)], axis=2)

    def c_block(cc):
        cc = cc.reshape(nblk, SCAN_GROUPS, SSM_GROUP_SIZE, SSM_STATE)
        return jnp.einsum("gk,jghp->jkpgh", eye, cc).reshape(nblk, SCAN_CB, SCAN_GROUPS * SSM_GROUP_SIZE)

    c_blk = jnp.concatenate([c_block(c_re), -c_block(c_im)], axis=1)
    return a_lay, b_blk, c_blk


def _interleave(v):
    rows, c = v.shape
    return v.reshape(SCAN_SEGS, rows // SCAN_SEGS, c).transpose(1, 0, 2).reshape(rows, c)


def _deinterleave(v):
    rows, c = v.shape
    return v.reshape(rows // SCAN_SEGS, SCAN_SEGS, c).transpose(1, 0, 2).reshape(rows, c)


def _scan_groups(a_ref, bu_ref, o_ref, state, *, reverse, tt):
    cb = SCAN_CB
    ar = jnp.broadcast_to(a_ref[:, :cb], (SCAN_SEGS, cb))
    ai = jnp.broadcast_to(a_ref[:, cb:], (SCAN_SEGS, cb))
    ngroups = tt // SCAN_SEGS

    def step(i, st):
        sr, si = st
        r0 = pl.multiple_of(((ngroups - 1 - i) if reverse else i) * SCAN_SEGS, SCAN_SEGS)
        blk = bu_ref[pl.ds(r0, SCAN_SEGS), :]
        nr = ar * sr - ai * si + blk[:, :cb]
        ni = ar * si + ai * sr + blk[:, cb:]
        if o_ref is not None:
            o_ref[pl.ds(r0, SCAN_SEGS), :] = jnp.concatenate([nr, ni], axis=1)
        return nr, ni

    return lax.fori_loop(0, ngroups, step, state, unroll=4)


def _segment_entries(a_ref, e_ref, init_ref, *, reverse, seg_len):
    cb = SCAN_CB
    n_sq = seg_len.bit_length() - 1
    assert 1 << n_sq == seg_len, seg_len
    pr, pi = a_ref[:, :cb], a_ref[:, cb:]
    for _ in range(n_sq):
        pr, pi = pr * pr - pi * pi, 2.0 * pr * pi
    cr = jnp.zeros((1, cb), F32)
    ci = jnp.zeros((1, cb), F32)
    order = range(SCAN_SEGS - 1, -1, -1) if reverse else range(SCAN_SEGS)
    for k, seg in enumerate(order):
        if k > 0:
            prev = seg + 1 if reverse else seg - 1
            er, ei = e_ref[prev:prev + 1, :cb], e_ref[prev:prev + 1, cb:]
            cr, ci = pr * cr - pi * ci + er, pr * ci + pi * cr + ei
        init_ref[seg:seg + 1, :] = jnp.concatenate([cr, ci], axis=1)


def _ssm_specs(nt, tt, nch, reverse):
    cb = SCAN_CB
    tmap = (lambda j, kk: (nt - 1 - kk, j)) if reverse else (lambda j, kk: (kk, j))
    return dict(a=pl.BlockSpec((1, 2 * cb), lambda j, kk: (0, j)),
                seg=pl.BlockSpec((SCAN_SEGS, 2 * cb), lambda j, kk: (0, j)),
                chan=pl.BlockSpec((tt, nch), tmap),
                state=pl.BlockSpec((tt, 2 * cb), tmap),
                b=pl.BlockSpec((None, nch, 2 * cb), lambda j, kk: (j, 0, 0)),
                c=pl.BlockSpec((None, 2 * cb, nch), lambda j, kk: (j, 0, 0)))


def _ssm_ends(a_lay, x, blocks, *, transpose, reverse, tt, name):
    rows = x.shape[0]
    nblk = blocks.shape[0]
    nch = x.shape[1] // nblk
    cb = SCAN_CB
    nt = rows // tt
    sp = _ssm_specs(nt, tt, nch, reverse)

    def body(a_ref, x_ref, w_ref, e_ref, bu_ref):
        kk = pl.program_id(1)

        @pl.when(kk == 0)
        def _():
            e_ref[...] = jnp.zeros_like(e_ref)

        bu_ref[...] = _dot(x_ref[...].astype(BF16), w_ref[...].astype(BF16), 1, 1 if transpose else 0)
        sr, si = _scan_groups(a_ref, bu_ref, None, (e_ref[:, :cb], e_ref[:, cb:]), reverse=reverse, tt=tt)
        e_ref[...] = jnp.concatenate([sr, si], axis=1)

    return pl.pallas_call(
        body, name=name, grid=(nblk, nt),
        in_specs=[sp["a"], sp["chan"], sp["c"] if transpose else sp["b"]],
        out_specs=sp["seg"],
        out_shape=jax.ShapeDtypeStruct((SCAN_SEGS, nblk * 2 * cb), F32),
        scratch_shapes=[pltpu.VMEM((tt, 2 * cb), F32)],
        compiler_params=_params(("parallel", "arbitrary")),
    )(a_lay, x, blocks)


def _ssm_fwd(a_lay, u, b_blk, c_blk, ends, *, tt, name):
    rows = u.shape[0]
    nblk = b_blk.shape[0]
    nch = u.shape[1] // nblk
    cb = SCAN_CB
    nt = rows // tt
    sp = _ssm_specs(nt, tt, nch, False)

    def body(a_ref, e_ref, u_ref, b_ref, c_ref, s_ref, y_ref, init_ref, carry_ref):
        kk = pl.program_id(1)

        @pl.when(kk == 0)
        def _():
            _segment_entries(a_ref, e_ref, init_ref, reverse=False, seg_len=rows // SCAN_SEGS)
            carry_ref[...] = init_ref[...]

        s_ref[...] = _dot(u_ref[...].astype(BF16), b_ref[...].astype(BF16), 1, 0)
        sr, si = _scan_groups(a_ref, s_ref, s_ref, (carry_ref[:, :cb], carry_ref[:, cb:]), reverse=False, tt=tt)
        carry_ref[...] = jnp.concatenate([sr, si], axis=1)
        y_ref[...] = _dot(s_ref[...].astype(BF16), c_ref[...].astype(BF16), 1, 0)

    return pl.pallas_call(
        body, name=name, grid=(nblk, nt),
        in_specs=[sp["a"], sp["seg"], sp["chan"], sp["b"], sp["c"]],
        out_specs=[sp["state"], sp["chan"], sp["seg"]],
        out_shape=[jax.ShapeDtypeStruct((rows, nblk * 2 * cb), F32), jax.ShapeDtypeStruct((rows, nblk * nch), F32),
                   jax.ShapeDtypeStruct((SCAN_SEGS, nblk * 2 * cb), F32)],
        scratch_shapes=[pltpu.VMEM((SCAN_SEGS, 2 * cb), F32)],
        compiler_params=_params(("parallel", "arbitrary")),
    )(a_lay, ends, u, b_blk, c_blk)


def _ssm_bwd(a_conj, dy, u, s, s_entry, b_blk, c_blk, dd, ends, *, tt, name):
    rows = u.shape[0]
    nblk = b_blk.shape[0]
    nch = u.shape[1] // nblk
    cb = SCAN_CB
    nt = rows // tt
    sp = _ssm_specs(nt, tt, nch, True)
    groups_per_tile = tt // SCAN_SEGS
    before = pl.BlockSpec((SCAN_SEGS, 2 * cb), lambda j, kk: (jnp.maximum((nt - 1 - kk) * groups_per_tile - 1, 0), j))

    def body(a_ref, e_ref, dy_ref, u_ref, s_ref, before_ref, entry_ref, b_ref, c_ref, dd_ref,
             du_ref, db_ref, dc_ref, da_ref, lam_ref, carry_ref):
        kk = pl.program_id(1)

        @pl.when(kk == 0)
        def _():
            _segment_entries(a_ref, e_ref, carry_ref, reverse=True, seg_len=rows // SCAN_SEGS)
            db_ref[...] = jnp.zeros_like(db_ref)
            dc_ref[...] = jnp.zeros_like(dc_ref)
            da_ref[...] = jnp.zeros_like(da_ref)

        dyv = dy_ref[...]
        dyb = dyv.astype(BF16)
        lam_ref[...] = _dot(dyb, c_ref[...].astype(BF16), 1, 1)
        lr, li = _scan_groups(a_ref, lam_ref, lam_ref, (carry_ref[:, :cb], carry_ref[:, cb:]), reverse=True, tt=tt)
        carry_ref[...] = jnp.concatenate([lr, li], axis=1)

        first = jnp.where(kk == nt - 1, entry_ref[...], before_ref[...])
        rest = tt - SCAN_SEGS
        lam_hi = lam_ref[pl.ds(SCAN_SEGS, rest), :]
        s_lo = s_ref[pl.ds(0, rest), :]
        lam_lo = lam_ref[pl.ds(0, SCAN_SEGS), :]

        def pair(lv, pv):
            lre, lim, pre, pim = lv[:, :cb], lv[:, cb:], pv[:, :cb], pv[:, cb:]
            return (jnp.sum(lre * pre + lim * pim, axis=0, keepdims=True),
                    jnp.sum(lim * pre - lre * pim, axis=0, keepdims=True))

        r1, i1 = pair(lam_hi, s_lo)
        r0, i0 = pair(lam_lo, first)
        da_ref[...] += jnp.concatenate([r1 + r0, i1 + i0], axis=1)

        lamb = lam_ref[...].astype(BF16)
        du_ref[...] = _dot(lamb, b_ref[...].astype(BF16), 1, 1) + dd_ref[...] * dyv
        db_ref[...] += _dot(u_ref[...].astype(BF16), lamb, 0, 0)
        dc_ref[...] += _dot(s_ref[...].astype(BF16), dyb, 0, 0)

    return pl.pallas_call(
        body, name=name, grid=(nblk, nt),
        in_specs=[sp["a"], sp["seg"], sp["chan"], sp["chan"], sp["state"], before, sp["seg"], sp["b"], sp["c"],
                  pl.BlockSpec((1, nch), lambda j, kk: (0, j))],
        out_specs=[sp["chan"], sp["b"], sp["c"], pl.BlockSpec((1, 2 * cb), lambda j, kk: (0, j))],
        out_shape=[jax.ShapeDtypeStruct((rows, nblk * nch), F32), jax.ShapeDtypeStruct(b_blk.shape, F32),
                   jax.ShapeDtypeStruct(c_blk.shape, F32), jax.ShapeDtypeStruct((1, nblk * 2 * cb), F32)],
        scratch_shapes=[pltpu.VMEM((tt, 2 * cb), F32), pltpu.VMEM((SCAN_SEGS, 2 * cb), F32)],
        compiler_params=_params(("parallel", "arbitrary")),
    )(a_conj, ends, dy, u, s, s, s_entry, b_blk, c_blk, dd)


def _glu_fwd(ys, u, dd, w_glu, b_glu, *, tm, name):
    rows, w = ys.shape

    def body(ys_ref, u_ref, dd_ref, w_ref, b_ref, y0_ref, t_ref, y2_ref):
        y0 = ys_ref[...] + dd_ref[...] * u_ref[...]
        y1 = _gelu(y0)
        t = _dot(y1.astype(BF16), w_ref[...], 1, 0) + b_ref[...]
        y0_ref[...] = y0
        t_ref[...] = t
        y2_ref[...] = (y1 * _sigmoid(t)).astype(BF16)

    row = pl.BlockSpec((tm, w), lambda i: (i, 0))
    vec = pl.BlockSpec((1, w), lambda i: (0, 0))
    return pl.pallas_call(
        body, name=name, grid=(rows // tm,),
        in_specs=[row, row, vec, pl.BlockSpec((w, w), lambda i: (0, 0)), vec],
        out_specs=[row, row, row],
        out_shape=[jax.ShapeDtypeStruct((rows, w), F32), jax.ShapeDtypeStruct((rows, w), F32),
                   jax.ShapeDtypeStruct((rows, w), BF16)],
        compiler_params=_params(("parallel",)),
    )(ys, u, dd, w_glu, b_glu)


def _glu_bwd(dy2, y0, t, u, w_glu, *, tm, name):
    rows, w = y0.shape

    def body(dy2_ref, y0_ref, t_ref, u_ref, w_ref, dy0_ref, dt_ref, y1_ref, db_ref, dd_ref):
        i = pl.program_id(0)
        y0 = y0_ref[...]
        y1 = _gelu(y0)
        sg = _sigmoid(t_ref[...])
        dy2v = dy2_ref[...]
        dt = dy2v * y1 * sg * (1.0 - sg)
        dy1 = dy2v * sg + _dot(dt.astype(BF16), w_ref[...], 1, 1)
        dy0 = dy1 * _gelu_grad(y0)
        dy0_ref[...] = dy0
        dt_ref[...] = dt.astype(BF16)
        y1_ref[...] = y1.astype(BF16)

        @pl.when(i == 0)
        def _():
            db_ref[...] = jnp.zeros_like(db_ref)
            dd_ref[...] = jnp.zeros_like(dd_ref)

        db_ref[...] += jnp.sum(dt, axis=0, keepdims=True)
        dd_ref[...] += jnp.sum(dy0 * u_ref[...], axis=0, keepdims=True)

    row = pl.BlockSpec((tm, w), lambda i: (i, 0))
    vec = pl.BlockSpec((1, w), lambda i: (0, 0))
    return pl.pallas_call(
        body, name=name, grid=(rows // tm,),
        in_specs=[row, row, row, row, pl.BlockSpec((w, w), lambda i: (0, 0))],
        out_specs=[row, row, row, vec, vec],
        out_shape=[jax.ShapeDtypeStruct((rows, w), F32), jax.ShapeDtypeStruct((rows, w), BF16),
                   jax.ShapeDtypeStruct((rows, w), BF16), jax.ShapeDtypeStruct((1, w), F32),
                   jax.ShapeDtypeStruct((1, w), F32)],
        compiler_params=_params(("arbitrary",)),
    )(dy2, y0, t, u, w_glu)


ATTN_TILE = 2048


def _attn_geometry(rows, d):
    sb = ATTN_Q * d
    tr = max(sb, min(ATTN_TILE, rows))
    assert rows % tr == 0 and tr % sb == 0, (rows, d)
    return sb, tr, rows // tr, tr // sb


def _attn_masks():
    qi = lax.broadcasted_iota(jnp.int32, (2 * ATTN_Q, 2 * ATTN_Q), 0) % ATTN_Q
    kj = lax.broadcasted_iota(jnp.int32, (2 * ATTN_Q, 2 * ATTN_Q), 1)
    own_ok = jnp.logical_and(kj >= ATTN_Q, kj - ATTN_Q <= qi)
    prev_ok = jnp.logical_and(kj < ATTN_Q, kj >= qi)
    bias_first = jnp.where(own_ok, 0.0, NEG_INF)
    bias_other = jnp.where(jnp.logical_or(own_ok, prev_ok), 0.0, NEG_INF)
    head0 = lax.broadcasted_iota(jnp.int32, (ATTN_Q, LANES), 1) < ATTN_HEAD_DIM
    return bias_first, bias_other, head0


def _attn_rows(base, n, d):
    return pl.ds(pl.multiple_of(base, ATTN_Q), n) if d == 1 else pl.ds(base, n, stride=d)


def _stack_heads(v, head0):
    return jnp.concatenate([jnp.where(head0, v, 0.0), jnp.where(head0, 0.0, v)], axis=0)


def _unstack_heads(v, head0):
    return jnp.where(head0, v[:ATTN_Q], v[ATTN_Q:])


def _fill_keys(buf, prev_ref, cur_ref, sb):
    buf[pl.ds(0, sb), :] = prev_ref[...]
    buf[pl.ds(sb, cur_ref.shape[0]), :] = cur_ref[...]


def _attn_fwd(qkv, g, d, *, name):
    rows = qkv.shape[0]
    sb, tr, ntiles, nsub = _attn_geometry(rows, d)
    qc, kc, vc = 2 * g, 6 + 2 * g, 12 + 2 * g
    scale = ATTN_HEAD_DIM ** -0.5

    def body(q_ref, kc_ref, kp_ref, vc_ref, vp_ref, o_ref, lse_ref, kbuf, vbuf):
        n = pl.program_id(0)
        _fill_keys(kbuf, kp_ref, kc_ref, sb)
        _fill_keys(vbuf, vp_ref, vc_ref, sb)
        bias_first, bias_other, head0 = _attn_masks()

        def per_block(idx, carry):
            j, r = idx // d, idx % d
            base = j * sb + r
            bias = jnp.where(jnp.logical_and(n == 0, j == 0), bias_first, bias_other)
            qrows = _attn_rows(base, ATTN_Q, d)
            krows = _attn_rows(base, 2 * ATTN_Q, d)
            qs = (_stack_heads(q_ref[qrows, :], head0) * scale).astype(BF16)
            s = _dot(qs, kbuf[krows, :].astype(BF16), 1, 1) + bias
            mx = jnp.max(s, axis=-1, keepdims=True)
            p = jnp.exp(s - mx)
            den = jnp.sum(p, axis=-1, keepdims=True)
            pv = _dot(p.astype(BF16), vbuf[krows, :].astype(BF16), 1, 0) / den
            o_ref[qrows, :] = _unstack_heads(pv, head0)
            lse_ref[qrows, :] = _unstack_heads(jnp.broadcast_to(mx + jnp.log(den), (2 * ATTN_Q, LANES)), head0)
            return carry

        lax.fori_loop(0, nsub * d, per_block, 0, unroll=8)

    def cur(col):
        return pl.BlockSpec((tr, LANES), lambda n, hp: (n, col + hp))

    def prev(col):
        return pl.BlockSpec((sb, LANES), lambda n, hp: (jnp.maximum(n * nsub - 1, 0), col + hp))

    out_spec = pl.BlockSpec((tr, LANES), lambda n, hp: (n, hp))
    return pl.pallas_call(
        body, name=name, grid=(ntiles, 2),
        in_specs=[cur(qc), cur(kc), prev(kc), cur(vc), prev(vc)],
        out_specs=[out_spec, out_spec],
        out_shape=[jax.ShapeDtypeStruct((rows, 2 * LANES), F32), jax.ShapeDtypeStruct((rows, 2 * LANES), F32)],
        scratch_shapes=[pltpu.VMEM((sb + tr, LANES), F32), pltpu.VMEM((sb + tr, LANES), F32)],
        compiler_params=_params(("parallel", "parallel")),
    )(qkv, qkv, qkv, qkv, qkv)


def _attn_merge(outs, lses, *, tm, name):
    rows, w = outs[0].shape

    def body(o0, o1, o2, l0, l1, l2, o_ref, lse_ref):
        a0, a1, a2 = l0[...], l1[...], l2[...]
        mx = jnp.maximum(jnp.maximum(a0, a1), a2)
        e0, e1, e2 = jnp.exp(a0 - mx), jnp.exp(a1 - mx), jnp.exp(a2 - mx)
        den = e0 + e1 + e2
        o_ref[...] = (e0 / den) * o0[...] + (e1 / den) * o1[...] + (e2 / den) * o2[...]
        lse_ref[...] = mx + jnp.log(den)

    row = pl.BlockSpec((tm, w), lambda i: (i, 0))
    return pl.pallas_call(
        body, name=name, grid=(rows // tm,), in_specs=[row] * 6, out_specs=[row, row],
        out_shape=[jax.ShapeDtypeStruct((rows, w), F32), jax.ShapeDtypeStruct((rows, w), F32)],
        compiler_params=_params(("parallel",)),
    )(*outs, *lses)


def _attn_bwd(qkv, do, o, lse, g, d, prev, *, name):
    rows = qkv.shape[0]
    sb, tr, ntiles, nsub = _attn_geometry(rows, d)
    qc, kc, vc = 2 * g, 6 + 2 * g, 12 + 2 * g
    scale = ATTN_HEAD_DIM ** -0.5

    def body(q_ref, kc_ref, kp_ref, vc_ref, vp_ref, do_ref, o_ref, lse_ref, dq_ref, dk_ref, dv_ref,
             kbuf, vbuf, dk_acc, dv_acc):
        n = pl.program_id(1)

        @pl.when(n == 0)
        def _():
            dk_acc[pl.ds(0, tr), :] = jnp.zeros((tr, LANES), F32)
            dv_acc[pl.ds(0, tr), :] = jnp.zeros((tr, LANES), F32)

        @pl.when(n < ntiles)
        def _():
            dk_acc[pl.ds(tr, tr), :] = jnp.zeros((tr, LANES), F32)
            dv_acc[pl.ds(tr, tr), :] = jnp.zeros((tr, LANES), F32)
            _fill_keys(kbuf, kp_ref, kc_ref, sb)
            _fill_keys(vbuf, vp_ref, vc_ref, sb)
            bias_first, bias_other, head0 = _attn_masks()
            lane = lax.broadcasted_iota(jnp.int32, (ATTN_Q, LANES), 1)

            def per_block(idx, carry):
                j, r = idx // d, idx % d
                base = j * sb + r
                bias = jnp.where(jnp.logical_and(n == 0, j == 0), bias_first, bias_other)
                qrows = _attn_rows(base, ATTN_Q, d)
                krows = _attn_rows(base, 2 * ATTN_Q, d)
                arows = _attn_rows(base + (tr - sb), 2 * ATTN_Q, d)
                qs = (_stack_heads(q_ref[qrows, :], head0) * scale).astype(BF16)
                dos = _stack_heads(do_ref[qrows, :], head0)
                dosb = dos.astype(BF16)
                ov = o_ref[qrows, :]
                delta = jnp.sum(dos * jnp.concatenate([ov, ov], axis=0), axis=-1, keepdims=True)
                lsev = lse_ref[qrows, :]
                lse_s = jnp.concatenate(
                    [jnp.sum(jnp.where(lane == h * ATTN_HEAD_DIM, lsev, 0.0), axis=-1, keepdims=True) for h in range(2)], axis=0)
                kb = kbuf[krows, :].astype(BF16)
                vb = vbuf[krows, :].astype(BF16)
                p = jnp.exp(_dot(qs, kb, 1, 1) + bias - lse_s)
                ds = (p * (_dot(dosb, vb, 1, 1) - delta)).astype(BF16)
                dq_ref[qrows, :] = _unstack_heads(_dot(ds, kb, 1, 0), head0) * scale
                dk_acc[arows, :] += _dot(ds, qs, 0, 0)
                dv_acc[arows, :] += _dot(p.astype(BF16), dosb, 0, 0)
                return carry

            lax.fori_loop(0, nsub * d, per_block, 0, unroll=4)

        dk_ref[...] = dk_acc[pl.ds(0, tr), :]
        dv_ref[...] = dv_acc[pl.ds(0, tr), :]
        dk_acc[pl.ds(0, tr), :] = dk_acc[pl.ds(tr, tr), :]
        dv_acc[pl.ds(0, tr), :] = dv_acc[pl.ds(tr, tr), :]

    def cur(n):
        return jnp.minimum(n, ntiles - 1)

    def spec(col, prev):
        if prev:
            return pl.BlockSpec((sb, LANES), lambda hp, n: (jnp.maximum(cur(n) * nsub - 1, 0), col + hp))
        return pl.BlockSpec((tr, LANES), lambda hp, n: (cur(n), col + hp))

    row_spec = pl.BlockSpec((tr, LANES), lambda hp, n: (cur(n), hp))
    dq_out = pl.BlockSpec((tr, LANES), lambda hp, n: (cur(n), 2 * g + hp))
    kv_out = pl.BlockSpec((tr, LANES), lambda hp, n: (jnp.maximum(n - 1, 0), 2 * g + hp))
    shape = jax.ShapeDtypeStruct((rows, len(ATTN_PATTERNS) * 2 * LANES), F32)
    ins = [qkv, qkv, qkv, qkv, qkv, do, o, lse]
    in_specs = [spec(qc, False), spec(kc, False), spec(kc, True), spec(vc, False), spec(vc, True),
                row_spec, row_spec, row_spec]
    aliases = {}
    if prev is not None:
        aliases = {len(ins) + t: t for t in range(3)}
        ins = ins + list(prev)
        in_specs = in_specs + [ANY] * 3
    n_in = len(ins)

    def entry(*refs):
        body(*refs[:8], *refs[n_in:])

    return pl.pallas_call(
        entry, name=name, grid=(2, ntiles + 1),
        in_specs=in_specs,
        out_specs=[dq_out, kv_out, kv_out],
        out_shape=[shape, shape, shape],
        input_output_aliases=aliases,
        scratch_shapes=[pltpu.VMEM((sb + tr, LANES), F32), pltpu.VMEM((sb + tr, LANES), F32),
                        pltpu.VMEM((2 * tr, LANES), F32), pltpu.VMEM((2 * tr, LANES), F32)],
        compiler_params=_params(("parallel", "arbitrary")),
    )(*ins)


def _mem_probs(q, k):
    s = _dot(q.astype(BF16), k.astype(BF16), 1, 1) * (MEM_HEAD_DIM ** -0.5)
    e = jnp.exp(s - jnp.max(s, axis=-1, keepdims=True))
    return e / jnp.sum(e, axis=-1, keepdims=True)


def _mem_attn_fwd(mq, kv, *, tq, name):
    rows = mq.shape[0]

    def body(q_ref, k_ref, v_ref, o_ref):
        p = _mem_probs(q_ref[...], k_ref[...])
        o_ref[...] = _dot(p.astype(BF16), v_ref[...].astype(BF16), 1, 0)

    return pl.pallas_call(
        body, name=name, grid=(rows // tq, MEM_HEADS),
        in_specs=[pl.BlockSpec((tq, LANES), lambda i, h: (i, h)),
                  pl.BlockSpec((MEM_LEN, LANES), lambda i, h: (0, h)),
                  pl.BlockSpec((MEM_LEN, LANES), lambda i, h: (0, MEM_HEADS + h))],
        out_specs=pl.BlockSpec((tq, LANES), lambda i, h: (i, h)),
        out_shape=jax.ShapeDtypeStruct((rows, MEM_HEADS * LANES), F32),
        compiler_params=_params(("parallel", "parallel")),
    )(mq, kv, kv)


def _mem_attn_bwd(mq, kv, dmo, *, tq, name):
    rows = mq.shape[0]
    scale = MEM_HEAD_DIM ** -0.5

    def body(q_ref, k_ref, v_ref, do_ref, dq_ref, dk_ref, dv_ref):
        i = pl.program_id(1)
        qb = q_ref[...].astype(BF16)
        kb = k_ref[...].astype(BF16)
        vb = v_ref[...].astype(BF16)
        dob = do_ref[...].astype(BF16)
        p = _mem_probs(q_ref[...], k_ref[...])
        dp = _dot(dob, vb, 1, 1)
        ds = (p * (dp - jnp.sum(p * dp, axis=-1, keepdims=True)) * scale).astype(BF16)
        dq_ref[...] = _dot(ds, kb, 1, 0).astype(dq_ref.dtype)

        @pl.when(i == 0)
        def _():
            dk_ref[...] = jnp.zeros_like(dk_ref)
            dv_ref[...] = jnp.zeros_like(dv_ref)

        dk_ref[...] += _dot(ds, qb, 0, 0)
        dv_ref[...] += _dot(p.astype(BF16), dob, 0, 0)

    kv_out = pl.BlockSpec((MEM_LEN, LANES), lambda h, i: (0, h))
    kv_shape = jax.ShapeDtypeStruct((MEM_LEN, MEM_HEADS * LANES), F32)
    return pl.pallas_call(
        body, name=name, grid=(MEM_HEADS, rows // tq),
        in_specs=[pl.BlockSpec((tq, LANES), lambda h, i: (i, h)),
                  pl.BlockSpec((MEM_LEN, LANES), lambda h, i: (0, h)),
                  pl.BlockSpec((MEM_LEN, LANES), lambda h, i: (0, MEM_HEADS + h)),
                  pl.BlockSpec((tq, LANES), lambda h, i: (i, h))],
        out_specs=[pl.BlockSpec((tq, LANES), lambda h, i: (i, h)), kv_out, kv_out],
        out_shape=[jax.ShapeDtypeStruct((rows, MEM_HEADS * LANES), BF16), kv_shape, kv_shape],
        compiler_params=_params(("parallel", "arbitrary")),
    )(mq, kv, kv, dmo)


def _resident(shape):
    return pl.BlockSpec(shape, lambda i: (0, 0), pipeline_mode=pl.Buffered(1))


def _branch_merge_fwd(acts, wts, zg, b_gate, *, tm, name):
    rows = zg.shape[0]
    d = wts[0].shape[0]

    def body(s_ref, a_ref, m_ref, ws_ref, wa_ref, wm_ref, zg_ref, b_ref, o_ref):
        gt = _sigmoid(zg_ref[...] + b_ref[...])
        acc = None
        for k, (x_ref, w_ref) in enumerate(((s_ref, ws_ref), (a_ref, wa_ref), (m_ref, wm_ref))):
            term = gt[:, k * d:(k + 1) * d] * _dot(x_ref[...].astype(BF16), w_ref[...], 1, 1)
            acc = term if acc is None else acc + term
        o_ref[...] = acc.astype(BF16)

    return pl.pallas_call(
        body, name=name, grid=(rows // tm,),
        in_specs=[pl.BlockSpec((tm, x.shape[1]), lambda i: (i, 0)) for x in acts] + [_resident(w.shape) for w in wts]
        + [pl.BlockSpec((tm, 3 * d), lambda i: (i, 0)), pl.BlockSpec((1, 3 * d), lambda i: (0, 0))],
        out_specs=pl.BlockSpec((tm, d), lambda i: (i, 0)), out_shape=jax.ShapeDtypeStruct((rows, d), BF16),
        compiler_params=_params(("parallel",)),
    )(*acts, *wts, zg, b_gate)


def _branch_merge_bwd(dmerged, acts, wts, zg, b_gate, *, tm, name, carry=None):
    rows = zg.shape[0]
    d = wts[0].shape[0]

    def body(dm_ref, s_ref, a_ref, m_ref, ws_ref, wa_ref, wm_ref, zg_ref, b_ref,
             ds_ref, da_ref, dmm_ref, dws_ref, dwa_ref, dwm_ref, dzg_ref, db_ref):
        i = pl.program_id(0)

        @pl.when(i == 0)
        def _():
            for r in (dws_ref, dwa_ref, dwm_ref, db_ref):
                r[...] = jnp.zeros_like(r)

        gt = _sigmoid(zg_ref[...] + b_ref[...])
        dm = dm_ref[...]
        groups = ((s_ref, ws_ref, ds_ref, dws_ref), (a_ref, wa_ref, da_ref, dwa_ref), (m_ref, wm_ref, dmm_ref, dwm_ref))
        for k, (x_ref, w_ref, dx_ref, dw_ref) in enumerate(groups):
            cs = pl.ds(k * d, d)
            gk = gt[:, k * d:(k + 1) * d]
            xb = x_ref[...].astype(BF16)
            br = _dot(xb, w_ref[...], 1, 1)
            dbr = (dm * gk).astype(BF16)
            dx_ref[...] = _dot(dbr, w_ref[...], 1, 0)
            dw_ref[...] += _dot(dbr, xb, 0, 0)
            dzg = dm * br * gk * (1.0 - gk)
            dzg_ref[:, cs] = dzg.astype(BF16)
            db_ref[:, cs] += jnp.sum(dzg, axis=0, keepdims=True)

    row = lambda w: pl.BlockSpec((tm, w), lambda i: (i, 0))
    whole = lambda shape: pl.BlockSpec(shape, lambda i: (0, 0))
    res = _call_with_carry(
        body, carry, name=name, grid=(rows // tm,),
        in_specs=[row(d)] + [row(x.shape[1]) for x in acts] + [_resident(w.shape) for w in wts] + [row(3 * d), whole((1, 3 * d))],
        out_specs=[row(x.shape[1]) for x in acts] + [whole(w.shape) for w in wts] + [row(3 * d), whole((1, 3 * d))],
        out_shape=[jax.ShapeDtypeStruct(x.shape, F32) for x in acts] + [jax.ShapeDtypeStruct(w.shape, F32) for w in wts]
        + [jax.ShapeDtypeStruct((rows, 3 * d), BF16), jax.ShapeDtypeStruct((1, 3 * d), F32)],
        scratch=[], operands=[dmerged, *acts, *wts, zg, b_gate], semantics=("arbitrary",))
    return tuple(res) if carry is None else (tuple(res[:8]), list(res[8:]))


def _adamw(w, g, m, v, *, tr, name):
    rows, cols = w.shape[-2:]
    assert rows % tr == 0, (name, rows, tr)

    def body(w_ref, g_ref, m_ref, v_ref, g_out, d_ref, nm_ref, nv_ref):
        gv = g_ref[...]
        m2 = ADAM_B1 * m_ref[...] + (1.0 - ADAM_B1) * gv
        v2 = ADAM_B2 * v_ref[...] + (1.0 - ADAM_B2) * (gv * gv)
        m_hat = m2 / (1.0 - ADAM_B1 ** ADAM_STEP)
        v_hat = v2 / (1.0 - ADAM_B2 ** ADAM_STEP)
        g_out[...] = gv
        d_ref[...] = -ADAM_LR * (m_hat / (jnp.sqrt(v_hat) + ADAM_EPS) + ADAM_WD * w_ref[...])
        nm_ref[...] = m2
        nv_ref[...] = v2

    flat = pl.BlockSpec((tr, cols), lambda i: (i, 0))
    blk = flat if w.ndim == 2 else pl.BlockSpec((None, tr, cols), lambda i: (0, i, 0))
    shape = jax.ShapeDtypeStruct(w.shape, F32)
    return pl.pallas_call(
        body, name=name, grid=(rows // tr,), in_specs=[blk, flat, blk, blk], out_specs=[blk] * 4,
        out_shape=[shape] * 4, compiler_params=_params(("parallel",)),
    )(w, g, m, v)


ANY = pl.BlockSpec(memory_space=pl.ANY)


def _position():
    return lax.axis_index("x"), lax.axis_index("y"), lax.axis_index("c")


def _other_chips(x, y):
    return ((1 - x, y), (x, 1 - y), (1 - x, 1 - y))


def _remote(src, dst, send_sem, recv_sem, dev):
    return pltpu.make_async_remote_copy(src_ref=src, dst_ref=dst, send_sem=send_sem, recv_sem=recv_sem,
                                        device_id=dev, device_id_type=MESH)


def _gather_exchange(shards):
    nb = len(shards)

    def rows_of(i, owner, core):
        rs = shards[i].shape[0]
        return pl.ds(pl.multiple_of(owner * rs + core * (rs // 2), 16), rs // 2)

    def first_leg(ins, outs, send_sems, recv_sems, i, j):
        x, y, c = _position()
        px, py = _other_chips(x, y)[j]
        half = shards[i].shape[0] // 2
        mine = ins[i].at[pl.ds(pl.multiple_of(c * half, 16), half)]
        return _remote(mine, outs[i].at[rows_of(i, 2 * x + y, c)], send_sems.at[i, j], recv_sems.at[i, j], (px, py, c))

    def passed_on(outs, send_sems, recv_sems, i, j, core):
        x, y, c = _position()
        px, py = _other_chips(x, y)[j]
        rows = outs[i].at[rows_of(i, 2 * px + py, core)]
        return _remote(rows, rows, send_sems.at[i, 3 + j], recv_sems.at[i, 3 + j], (x, y, 1 - c))

    def own_block(ins, outs, send_sems, recv_sems, i):
        x, y, c = _position()
        rs = shards[i].shape[0]
        place = outs[i].at[pl.ds(pl.multiple_of((2 * x + y) * rs, 16), rs)]
        return _remote(ins[i], place, send_sems.at[i, 6], recv_sems.at[i, 6], (x, y, 1 - c))

    def start(ins, outs, send_sems, recv_sems):
        for i in range(nb):
            own_block(ins, outs, send_sems, recv_sems, i).start()
            for j in range(3):
                first_leg(ins, outs, send_sems, recv_sems, i, j).start()

    def finish(ins, outs, send_sems, recv_sems):
        x, y, c = _position()
        for i in range(nb):
            for j, (px, py) in enumerate(_other_chips(x, y)):
                landed = outs[i].at[rows_of(i, 2 * px + py, c)]
                _remote(landed, landed, send_sems.at[i, j], recv_sems.at[i, j], (px, py, c)).wait_recv()
                passed_on(outs, send_sems, recv_sems, i, j, c).start()
        for i in range(nb):
            own_block(ins, outs, send_sems, recv_sems, i).wait()
            for j in range(3):
                passed_on(outs, send_sems, recv_sems, i, j, 1 - c).wait_recv()
        for i in range(nb):
            for j in range(3):
                first_leg(ins, outs, send_sems, recv_sems, i, j).wait_send()
                passed_on(outs, send_sems, recv_sems, i, j, c).wait_send()

    return _Exchange(ins=list(shards), outs=[jax.ShapeDtypeStruct((N_CHIPS * s.shape[0], s.shape[1]), s.dtype) for s in shards],
                     aliases={}, sems=[(nb, 7), (nb, 7)], start=start, finish=finish)


def _run_exchange(ex, *, name):
    n_in, n_out = len(ex.ins), len(ex.outs)

    def body(*refs):
        c_in, c_out, sems = refs[:n_in], refs[n_in:n_in + n_out], refs[n_in + n_out:]
        ex.start(c_in, c_out, *sems)
        ex.finish(c_in, c_out, *sems)

    return pl.pallas_call(
        body, name=name, in_specs=[ANY] * n_in, out_specs=[ANY] * n_out, out_shape=list(ex.outs),
        input_output_aliases=dict(ex.aliases),
        scratch_shapes=[pltpu.SemaphoreType.DMA(s) for s in ex.sems],
    )(*ex.ins)


def _row_tile(rows):
    return max(t for t in range(16, min(rows, 512) + 1, 16) if rows % t == 0)


def _halves_exchange(grads):
    nb = len(grads)

    def copies(ins, outs, send_sems, recv_sems):
        x, y, c = _position()
        return [_remote(ins[i].at[:, 1 - c], outs[i], send_sems.at[i], recv_sems.at[i], (x, y, 1 - c)) for i in range(nb)]

    def start(ins, outs, send_sems, recv_sems):
        for cp in copies(ins, outs, send_sems, recv_sems):
            cp.start()

    def finish(ins, outs, send_sems, recv_sems):
        for cp in copies(ins, outs, send_sems, recv_sems):
            cp.wait()

    return _Exchange(ins=list(grads), outs=[jax.ShapeDtypeStruct((N_CHIPS, g.shape[2], g.shape[3]), F32) for g in grads],
                     aliases={}, sems=[(nb,), (nb,)], start=start, finish=finish)


def _join_exchanges(parts):
    assert all(not ex.aliases for ex in parts)

    def split(refs, counts):
        out, at = [], 0
        for k in counts:
            out.append(refs[at:at + k])
            at += k
        return out

    def run(which):
        def go(ins, outs, *sems):
            for ex, i, o, s in zip(parts, split(ins, [len(ex.ins) for ex in parts]), split(outs, [len(ex.outs) for ex in parts]),
                                   split(sems, [len(ex.sems) for ex in parts])):
                getattr(ex, which)(i, o, *s)
        return go

    return _Exchange(ins=[a for ex in parts for a in ex.ins], outs=[a for ex in parts for a in ex.outs], aliases={},
                     sems=[s for ex in parts for s in ex.sems], start=run("start"), finish=run("finish"))


def _pair_sum(g4, got, c_arr, *, name):
    _, _, half, cols = g4.shape
    tr = _row_tile(half)

    def body(c_ref, g_ref, t_ref, p_ref, pb_ref):
        sm = g_ref[...] + t_ref[...]
        p_ref[...] = sm
        pb_ref[...] = sm.astype(BF16)

    blk = pl.BlockSpec((None, tr, cols), lambda j, i, c_ref: (j, i, 0))
    grid_spec = pltpu.PrefetchScalarGridSpec(
        num_scalar_prefetch=1, grid=(N_CHIPS, half // tr),
        in_specs=[pl.BlockSpec((None, None, tr, cols), lambda j, i, c_ref: (j, c_ref[0], i, 0)), blk],
        out_specs=[blk, blk])
    return pl.pallas_call(
        body, name=name, grid_spec=grid_spec,
        out_shape=[jax.ShapeDtypeStruct((N_CHIPS, half, cols), F32), jax.ShapeDtypeStruct((N_CHIPS, half, cols), BF16)],
        compiler_params=_params(("parallel", "parallel")),
    )(c_arr, g4, got)


def _scatter_exchange(parts):
    nb = len(parts)

    def copies(ins, outs, send_sems, recv_sems):
        x, y, c = _position()
        return [_remote(ins[i].at[2 * px + py], outs[i].at[j], send_sems.at[i, j], recv_sems.at[i, j], (px, py, c))
                for i in range(nb) for j, (px, py) in enumerate(_other_chips(x, y))]

    def start(ins, outs, send_sems, recv_sems):
        for cp in copies(ins, outs, send_sems, recv_sems):
            cp.start()

    def finish(ins, outs, send_sems, recv_sems):
        for cp in copies(ins, outs, send_sems, recv_sems):
            cp.wait()

    return _Exchange(ins=list(parts), outs=[jax.ShapeDtypeStruct((3,) + p.shape[1:], p.dtype) for p in parts],
                     aliases={}, sems=[(nb, 3), (nb, 3)], start=start, finish=finish)


def _owner_sum(p, got, chip_arr, c_arr, *, replicated, name):
    _, half, cols = p.shape
    tr = _row_tile(half)

    def body(chip_ref, c_ref, p_ref, r_ref, o_ref):
        o_ref[...] = ((p_ref[...] + r_ref[0].astype(F32)) + r_ref[1].astype(F32)) + r_ref[2].astype(F32)

    if replicated:
        out_spec = pl.BlockSpec((None, None, tr, cols), lambda i, chip_ref, c_ref: (chip_ref[0], c_ref[0], i, 0))
        out_shape = jax.ShapeDtypeStruct((N_CHIPS, 2, half, cols), F32)
    else:
        out_spec = pl.BlockSpec((None, tr, cols), lambda i, chip_ref, c_ref: (c_ref[0], i, 0))
        out_shape = jax.ShapeDtypeStruct((2, half, cols), F32)
    grid_spec = pltpu.PrefetchScalarGridSpec(
        num_scalar_prefetch=2, grid=(half // tr,),
        in_specs=[pl.BlockSpec((None, tr, cols), lambda i, chip_ref, c_ref: (chip_ref[0], i, 0)),
                  pl.BlockSpec((3, tr, cols), lambda i, chip_ref, c_ref: (0, i, 0))],
        out_specs=out_spec)
    return pl.pallas_call(
        body, name=name, grid_spec=grid_spec, out_shape=out_shape,
        compiler_params=_params(("parallel",)),
    )(chip_arr, c_arr, p, got)


def _share_reduced(bufs):
    nb = len(bufs) - 1

    def body(*refs):
        outs = refs[nb + 1:2 * nb + 2]
        send_sems, recv_sems = refs[2 * nb + 2:]
        x, y, c = _position()
        chip = 2 * x + y
        sends = []
        for i in range(nb):
            cp = _remote(outs[i].at[c], outs[i].at[c], send_sems.at[i], recv_sems.at[i], (x, y, 1 - c))
            cp.start()
            sends.append(cp)
        small = outs[nb]
        peers = [(fx, fy, fc) for fx in (0, 1) for fy in (0, 1) for fc in (0, 1) if fx + fy + fc > 0]
        for k, (fx, fy, fc) in enumerate(peers):
            dev = (x ^ fx, y ^ fy, c ^ fc)
            cp = _remote(small.at[chip, c], small.at[chip, c], send_sems.at[nb + k], recv_sems.at[nb + k], dev)
            cp.start()
            sends.append(cp)
        for i in range(nb):
            dst = outs[i].at[1 - c]
            _remote(dst, dst, send_sems.at[i], recv_sems.at[i], (x, y, 1 - c)).wait_recv()
        for k, (fx, fy, fc) in enumerate(peers):
            dst = small.at[2 * (x ^ fx) + (y ^ fy), c ^ fc]
            _remote(dst, dst, send_sems.at[nb + k], recv_sems.at[nb + k], (x ^ fx, y ^ fy, c ^ fc)).wait_recv()
        for cp in sends:
            cp.wait_send()

    n_all = nb + 1
    return pl.pallas_call(
        body, name="grad_share_reduced", in_specs=[ANY] * n_all, out_specs=[ANY] * n_all,
        out_shape=[jax.ShapeDtypeStruct(b.shape, b.dtype) for b in bufs],
        input_output_aliases={i: i for i in range(n_all)},
        scratch_shapes=[pltpu.SemaphoreType.DMA((nb + 7,)), pltpu.SemaphoreType.DMA((nb + 7,))],
    )(*bufs)


class _GradReducer:
    def __init__(self, c_arr, chip_arr):
        self.c_arr, self.chip_arr = c_arr, chip_arr
        self.full, self.pairs, self.landed = {}, {}, {}

    def swap(self, names, grads):
        for n, g in zip(names, grads):
            self.full[n] = g.reshape(N_CHIPS, 2, g.shape[0] // (2 * N_CHIPS), g.shape[1])
        return _halves_exchange([self.full[n] for n in names])

    def swapped(self, names, bufs):
        for n, t in zip(names, bufs):
            self.pairs[n] = _pair_sum(self.full[n], t, self.c_arr, name="grad_pair_sum_" + n)

    def scatter(self, names):
        return _scatter_exchange([self.pairs[n][1] for n in names])

    def collect(self, names, bufs):
        self.landed.update(zip(names, bufs))

    def swap_now(self, names, grads):
        self.swapped(names, _run_exchange(self.swap(names, grads), name="grad_exchange_" + names[0]))

    def finish(self, names, grads, order):
        self.swap_now(names, grads)
        self.collect(names, _run_exchange(self.scatter(names), name="grad_scatter_" + names[0]))
        totals = [_owner_sum(self.pairs[n][0], self.landed[n], self.chip_arr, self.c_arr, replicated=(n == order[-1]),
                             name="grad_owner_sum_" + n) for n in order]
        return _share_reduced(totals)


def _pack_small(vals):
    flat = jnp.concatenate([vals[name].reshape(-1) for name, _ in SMALL])
    return jnp.pad(flat, (0, N_CHIPS * SMALL_ROWS * 1024 - SMALL_ELEMS)).reshape(N_CHIPS * SMALL_ROWS, 1024)


def _unpack_small(buf):
    flat = buf.reshape(-1)
    out, off = {}, 0
    for name, shape in SMALL:
        n = int(np.prod(shape))
        out[name] = flat[off:off + n].reshape(shape)
        off += n
    return out


EARLY_REDUCED = (("w_down",), ("w_up",), ("w_o", "w_ssm_br", "w_attn_br", "w_mem_br", "w_glu", "w_mem_kv"), ("w_in",))


def _device_step(x, mem, tgt, w, p, *, shards, reducer):
    rows = x.shape[0]
    w = dict(w)
    early = EARLY_REDUCED
    gb = {}
    gather_pending = shards is not None

    def riding(*stages):
        if reducer is None or not stages:
            return None
        return _join_exchanges([reducer.swap(names, [gb[n] for n in names]) if kind == "swap" else reducer.scatter(names)
                                for kind, names in stages])

    def arrived(stages, res):
        if reducer is None or not stages:
            return res
        main, bufs = res
        for kind, names in stages:
            (reducer.swapped if kind == "swap" else reducer.collect)(names, bufs[:len(names)])
            bufs = bufs[len(names):]
        return main

    def fetching(names):
        return _gather_exchange([shards[n] for n in names]) if gather_pending else None

    def fetched(names, res):
        if not gather_pending:
            return res
        w.update(zip(names, res[1]))
        return res[0]

    first_use = (("w_glu", "w_ssm_br", "w_attn_br", "w_mem_kv", "w_mem_br", "w_o"), ("w_up",), ("w_down",))
    g1, gm, g2 = p["norm1_g"], p["mem_norm_g"], p["norm2_g"]
    gf = p["final_g"].reshape(1, D_MODEL)
    ssm_args = (p["ssm_lambda_re"][0], p["ssm_lambda_im"][0], p["ssm_log_dt"][0], p["ssm_b_re"][0],
                p["ssm_b_im"][0], p["ssm_c_re"][0], p["ssm_c_im"][0])
    (a_lay, b_blk, c_blk), ssm_vjp = jax.vjp(_ssm_matrices, *ssm_args)
    a_conj = a_lay * _to_scan_layout(jnp.stack([jnp.ones((N_STATES,), F32), -jnp.ones((N_STATES,), F32)]))[None, :]
    dd = p["ssm_d"].reshape(1, SSM_WIDTH)
    win_t = w["w_in"]
    mm = _matmul

    n1 = _rmsnorm_fwd(x, g1, tm=512, name="norm1")
    u = mm(n1, win_t, m=rows, n=512, k=1024, tb=True, tm=2048, tn=512, tk=1024, out_dtypes=(F32,), name="in_u")
    qkv = fetched(first_use[0], mm(n1, win_t, m=rows, n=2304, k=1024, tb=True, tm=1024, tn=768, tk=1024,
                                   b_row0=OFF_QKV, out_dtypes=(F32,), carry=fetching(first_use[0]), name="in_qkv"))
    mq = mm(n1, win_t, m=rows, n=512, k=1024, tb=True, tm=2048, tn=512, tk=1024, b_row0=OFF_MQ,
            out_dtypes=(F32,), name="in_mq")
    zg = fetched(first_use[1], mm(n1, win_t, m=rows, n=3072, k=1024, tb=True, tm=1024, tn=1024, tk=1024,
                                  b_row0=OFF_ZG, out_dtypes=(F32,), carry=fetching(first_use[1]), name="in_zg"))

    u_i = _interleave(u)
    ends = _ssm_ends(a_lay, u_i, b_blk, transpose=False, reverse=False, tt=512, name="ssm_fwd_ends")
    s, ys_i, s_entry = _ssm_fwd(a_lay, u_i, b_blk, c_blk, ends, tt=512, name="ssm_fwd")
    ys = _deinterleave(ys_i)
    y0, tglu, y2 = _glu_fwd(ys, u, dd, w["w_glu"], p["b_glu"], tm=512, name="glu_fwd")

    outs, lses = [], []
    for g, (_, d) in enumerate(ATTN_PATTERNS):
        o_g, lse_g = _attn_fwd(qkv, g, d, name=f"attn_fwd_{g}")
        outs.append(o_g)
        lses.append(lse_g)
    o, lse = _attn_merge(outs, lses, tm=1024, name="attn_merge")

    mn = _rmsnorm_fwd(mem, gm, tm=MEM_LEN, name="mem_norm")
    kv = mm(mn, w["w_mem_kv"], m=MEM_LEN, n=1024, k=1024, tm=MEM_LEN, tn=1024, tk=1024, out_dtypes=(F32,), name="mem_kv")
    mo = _mem_attn_fwd(mq, kv, tq=1024, name="mem_attn_fwd")

    branch_acts = (y2, o, mo)
    branch_wts = (w["w_ssm_br"], w["w_attn_br"], w["w_mem_br"])
    merged = _branch_merge_fwd(branch_acts, branch_wts, zg, p["b_gate"], tm=256, name="branch_merge_fwd")
    h1, n2 = mm(merged, w["w_o"], m=rows, n=1024, k=1024, tm=1024, tn=1024, tk=1024, out_dtypes=(F32, BF16),
                aux=((x, "mn"), (g2, "row")), epilogue=_residual_norm_epilogue, name="out_proj")
    relu2 = lambda acc: (jnp.square(jnp.maximum(acc, 0.0)),)
    act = fetched(first_use[2], mm(n2, w["w_up"], m=rows, n=D_FF, k=1024, tb=True, tm=1024, tn=1024, tk=1024,
                                   out_dtypes=(BF16,), epilogue=relu2, carry=fetching(first_use[2]), name="mlp_up"))
    dh2, d_gf, sq_err = _sum_matmul([act], w["w_down"], [0], tm=512, aux=((h1, "mn"), (tgt, "mn"), (gf, "row")),
                                    epilogue=_loss_head_epilogue, n_sums=2, name="mlp_down")
    loss = (0.5 / D_MODEL) * jnp.sum(sq_err)

    gs = {"final_g": d_gf.reshape(D_MODEL)}
    drelu2 = lambda acc, actv: (acc * (2.0 * jnp.sqrt(actv.astype(F32))),)
    dup = mm(dh2, w["w_down"], m=rows, n=D_FF, k=1024, tb=True, tm=1024, tn=2048, tk=1024, out_dtypes=(BF16,),
             aux=((act, "mn"),), epilogue=drelu2, name="d_act")
    gb["w_down"] = mm(act, dh2, m=D_FF, n=1024, k=rows, ta=True, tm=1024, tn=1024, tk=1024, out_dtypes=(F32,), name="dw_down")
    stages = (("swap", early[0]),)
    gb["w_up"] = arrived(stages, mm(dup, n2, m=D_FF, n=1024, k=rows, ta=True, tm=1024, tn=1024, tk=1024,
                                    out_dtypes=(F32,), carry=riding(*stages), name="dw_up"))
    stages = (("scatter", early[0]), ("swap", early[1]))
    dh1, gs["norm2_g"] = arrived(stages, _sum_matmul([dup], w["w_up"], [0], tm=512, aux=((h1, "mn"), (dh2, "mn"), (g2, "row")),
                                                     epilogue=_rmsnorm_bwd_epilogue, n_sums=1, carry=riding(*stages), name="d_n2"))
    dmerged = mm(dh1, w["w_o"], m=rows, n=1024, k=1024, tb=True, tm=1024, tn=1024, tk=1024, out_dtypes=(F32,), name="d_merged")
    gb["w_o"] = mm(merged, dh1, m=1024, n=1024, k=rows, ta=True, tm=1024, tn=1024, tk=1024, out_dtypes=(F32,), name="dw_o")
    stages = (("scatter", early[1]),)
    (dy2, do, dmo, gb["w_ssm_br"], gb["w_attn_br"], gb["w_mem_br"], dzg, gs["b_gate"]) = arrived(stages, _branch_merge_bwd(
        dmerged, branch_acts, branch_wts, zg, p["b_gate"], tm=256, carry=riding(*stages), name="branch_merge_bwd"))

    dy0, dt, y1, gs["b_glu"], d_dd = _glu_bwd(dy2, y0, tglu, u, w["w_glu"], tm=512, name="glu_bwd")
    gs["ssm_d"] = d_dd.reshape(1, SSM_GROUPS, SSM_GROUP_SIZE)
    gb["w_glu"] = mm(y1, dt, m=512, n=512, k=rows, ta=True, tm=512, tn=512, tk=1024, out_dtypes=(F32,), name="dw_glu")
    dy0_i = _interleave(dy0)
    lam_ends = _ssm_ends(a_conj, dy0_i, c_blk, transpose=True, reverse=True, tt=512, name="ssm_bwd_ends")
    du_i, d_b_blk, d_c_blk, d_a_lay = _ssm_bwd(a_conj, dy0_i, u_i, s, s_entry, b_blk, c_blk, dd, lam_ends, tt=512,
                                                name="ssm_bwd")
    du = _deinterleave(du_i)
    d_ssm = ssm_vjp((d_a_lay, d_b_blk, d_c_blk))
    for name, val in zip(("ssm_lambda_re", "ssm_lambda_im", "ssm_log_dt", "ssm_b_re", "ssm_b_im", "ssm_c_re", "ssm_c_im"), d_ssm):
        gs[name] = val[None]

    dqkv = None
    for g, (_, d) in enumerate(ATTN_PATTERNS):
        dqkv = _attn_bwd(qkv, do, o, lse, g, d, dqkv, name=f"attn_bwd_{g}")

    dmq, dmk, dmv = _mem_attn_bwd(mq, kv, dmo, tq=1024, name="mem_attn_bwd")
    dkv = jnp.concatenate([dmk, dmv], axis=1)
    gb["w_mem_kv"] = mm(mn, dkv, m=1024, n=1024, k=MEM_LEN, ta=True, tm=1024, tn=1024, tk=MEM_LEN, out_dtypes=(F32,), name="dw_mem_kv")
    dmn = mm(dkv, w["w_mem_kv"], m=MEM_LEN, n=1024, k=1024, tb=True, tm=MEM_LEN, tn=1024, tk=1024, out_dtypes=(F32,), name="d_mn")
    _, gs["mem_norm_g"] = _rmsnorm_bwd(mem, gm, dmn, None, tm=MEM_LEN, name="mem_norm_bwd")

    pieces = ((du, OFF_U, "u"), (dqkv[0], OFF_QKV, "q"), (dqkv[1], OFF_QKV + 768, "k"), (dqkv[2], OFF_QKV + 1536, "v"),
              (dmq, OFF_MQ, "mq"), (dzg, OFF_ZG, "zg"))
    dw_rows = []
    for piece, off, tag in pieces:
        width = piece.shape[1]
        tmw = 1024 if width % 1024 == 0 else (768 if width == 768 else 512)
        stages = {"q": (("swap", early[2]),), "zg": (("scatter", early[2]),)}.get(tag, ())
        dw_rows.append(arrived(stages, mm(piece, n1, m=width, n=1024, k=rows, ta=True, tm=tmw, tn=1024, tk=1024,
                                          out_dtypes=(F32,), carry=riding(*stages), name="dw_in_" + tag)))
    gb["w_in"] = jnp.concatenate(dw_rows, axis=0)
    if reducer is not None:
        reducer.swap_now(early[3], [gb["w_in"]])
    stages = (("scatter", early[3]),)
    dx, gs["norm1_g"] = arrived(stages, _sum_matmul(
        [piece for piece, _, _ in pieces], win_t, [off for _, off, _ in pieces], tm=512,
        aux=((x, "mn"), (dh1, "mn"), (g1, "row")), epilogue=_rmsnorm_bwd_epilogue, n_sums=1,
        carry=riding(*stages), vmem=VMEM_LIMIT_WIDE_BYTES, name="d_n1"))
    return loss, dx, gb, gs


def kernel(x, mem, norm1_g, mem_norm_g, w_in, b_gate, ssm_lambda_re, ssm_lambda_im, ssm_log_dt, ssm_b_re, ssm_b_im, ssm_c_re, ssm_c_im, ssm_d, w_glu, b_glu, w_ssm_br, w_attn_br, w_mem_kv, w_mem_br, w_o, norm2_g, w_up, w_down, final_g, loss_target, m_norm1_g, m_mem_norm_g, m_w_in, m_b_gate, m_ssm_lambda_re, m_ssm_lambda_im, m_ssm_log_dt, m_ssm_b_re, m_ssm_b_im, m_ssm_c_re, m_ssm_c_im, m_ssm_d, m_w_glu, m_b_glu, m_w_ssm_br, m_w_attn_br, m_w_mem_kv, m_w_mem_br, m_w_o, m_norm2_g, m_w_up, m_w_down, m_final_g, v_norm1_g, v_mem_norm_g, v_w_in, v_b_gate, v_ssm_lambda_re, v_ssm_lambda_im, v_ssm_log_dt, v_ssm_b_re, v_ssm_b_im, v_ssm_c_re, v_ssm_c_im, v_ssm_d, v_w_glu, v_b_glu, v_w_ssm_br, v_w_attn_br, v_w_mem_kv, v_w_mem_br, v_w_o, v_norm2_g, v_w_up, v_w_down, v_final_g):
    env = dict(locals())
    weights = {n: env[n] for n in WEIGHT_ORDER}
    moms = {n: env["m_" + n] for n in WEIGHT_ORDER}
    vels = {n: env["v_" + n] for n in WEIGHT_ORDER}
    def shard2d(a):
        return a.reshape(a.shape[-2], a.shape[-1])

    chip = 2 * lax.axis_index("x") + lax.axis_index("y")
    wire = [shard2d(weights[n]).astype(BF16) for n, _, _ in BIG]
    wire = dict(zip([n for n, _, _ in BIG], [s.T if tr else s for s, (_, tr, _) in zip(wire, BIG)]))
    w_in_full = _run_exchange(_gather_exchange([wire.pop("w_in")]), name="all_gather_w_in")[0]
    small = {n: weights[n] for n, _ in SMALL}

    reducer = _GradReducer(lax.axis_index("c").astype(jnp.int32).reshape(1), chip.astype(jnp.int32).reshape(1))
    loss, dx, gb, gs = _device_step(x[0], mem[0], loss_target[0], {"w_in": w_in_full}, small, shards=wire, reducer=reducer)
    *shards, small_grad = reducer.finish(["small"], [_pack_small(gs)], [n for n, _, _ in BIG] + ["small"])
    grads = {}
    for (n, tr, _), sh in zip(BIG, shards):
        sh = sh.reshape(2 * sh.shape[1], sh.shape[2])
        grads[n] = sh.T if tr else sh
    small_grad = small_grad.reshape(N_CHIPS * SMALL_ROWS, 1024)
    grads_small = _unpack_small(small_grad)

    delta, new_m, new_v = {}, {}, {}
    for n, _, _ in BIG:
        grads[n], delta[n], new_m[n], new_v[n] = _adamw(weights[n], grads[n], moms[n], vels[n],
                                                        tr=min(weights[n].shape[-2], 256), name="adamw_" + n)
    _, ds_, ms_, vs_ = _adamw(_pack_small(small), small_grad,
                              _pack_small({n: moms[n] for n, _ in SMALL}), _pack_small({n: vels[n] for n, _ in SMALL}),
                              tr=N_CHIPS * SMALL_ROWS, name="adamw_small")
    for dst, buf in ((delta, ds_), (new_m, ms_), (new_v, vs_)):
        dst.update(_unpack_small(buf))
    grads.update(grads_small)

    total_loss = lax.psum(loss, ("x", "y", "c"))
    return (total_loss, dx[None], *[grads[n] for n in WEIGHT_ORDER], *[delta[n] for n in WEIGHT_ORDER],
            *[new_m[n] for n in WEIGHT_ORDER], *[new_v[n] for n in WEIGHT_ORDER])
```

```python
import functools
import math

import numpy as np
import jax
import jax.numpy as jnp
from jax import lax
from jax.experimental import pallas as pl
from jax.experimental.pallas import tpu as pltpu

F32 = jnp.float32
BF16 = jnp.bfloat16

D_MODEL = 1024
SSM_GROUPS = 32
SSM_GROUP_SIZE = 16
SSM_STATE = 64
SSM_WIDTH = 512
N_STATES = SSM_GROUPS * SSM_STATE
SCAN_CB = 1024
ATTN_PATTERNS = ((128, 1), (512, 4), (2048, 16))
ATTN_HEAD_DIM = 64
ATTN_Q = 128
MEM_LEN = 256
MEM_HEAD_DIM = 128
MEM_HEADS = 4
D_FF = 4096
OFF_U, OFF_QKV, OFF_MQ, OFF_ZG = 0, 512, 2816, 3328
IN_WIDTH = 6400
RMS_EPS = 1e-6
NEG_INF = -1e30
ADAM_LR, ADAM_B1, ADAM_B2, ADAM_EPS, ADAM_WD, ADAM_STEP = 0.001, 0.9, 0.999, 1e-08, 0.01, 10

VMEM_LIMIT_BYTES = 48 * 1024 * 1024
VMEM_LIMIT_WIDE_BYTES = 56 * 1024 * 1024
LANES = 128
MESH = pl.DeviceIdType.MESH
N_CHIPS = 4

SCAN_SEGS = 8
SCAN_GROUPS = SCAN_CB // SSM_STATE

BIG = (("w_in", True, (6400, 1024)), ("w_glu", False, (512, 512)), ("w_ssm_br", True, (1024, 512)),
       ("w_attn_br", True, (1024, 256)), ("w_mem_kv", False, (1024, 1024)), ("w_mem_br", True, (1024, 512)),
       ("w_o", False, (1024, 1024)), ("w_up", True, (4096, 1024)), ("w_down", False, (4096, 1024)))
SMALL = (("norm1_g", (1, 1024)), ("mem_norm_g", (1, 1024)), ("b_gate", (1, 3072)),
         ("ssm_lambda_re", (1, 32, 64)), ("ssm_lambda_im", (1, 32, 64)), ("ssm_log_dt", (1, 32)),
         ("ssm_b_re", (1, 32, 64, 16)), ("ssm_b_im", (1, 32, 64, 16)), ("ssm_c_re", (1, 32, 16, 64)),
         ("ssm_c_im", (1, 32, 16, 64)), ("ssm_d", (1, 32, 16)), ("b_glu", (1, 512)),
         ("norm2_g", (1, 1024)), ("final_g", (1024,)))
WEIGHT_ORDER = ("norm1_g", "mem_norm_g", "w_in", "b_gate", "ssm_lambda_re", "ssm_lambda_im", "ssm_log_dt",
                "ssm_b_re", "ssm_b_im", "ssm_c_re", "ssm_c_im", "ssm_d", "w_glu", "b_glu", "w_ssm_br",
                "w_attn_br", "w_mem_kv", "w_mem_br", "w_o", "norm2_g", "w_up", "w_down", "final_g")
SMALL_ELEMS = sum(int(np.prod(s)) for _, s in SMALL)
SMALL_ROWS = 64


def _params(sem, vmem=VMEM_LIMIT_BYTES):
    return pltpu.CompilerParams(dimension_semantics=sem, vmem_limit_bytes=vmem)


def _sigmoid(v):
    return 0.5 * jnp.tanh(0.5 * v) + 0.5


_GELU_C = math.sqrt(2.0 / math.pi)


def _gelu(v):
    return 0.5 * v * (1.0 + jnp.tanh(_GELU_C * (v + 0.044715 * v * v * v)))


def _gelu_grad(v):
    th = jnp.tanh(_GELU_C * (v + 0.044715 * v * v * v))
    return 0.5 * (1.0 + th) + 0.5 * v * (1.0 - th * th) * _GELU_C * (1.0 + 3.0 * 0.044715 * v * v)


def _dot(a, b, ca, cb):
    return lax.dot_general(a, b, (((ca,), (cb,)), ((), ())), preferred_element_type=F32)


class _Exchange:
    def __init__(self, ins, outs, aliases, sems, start, finish):
        self.ins, self.outs, self.aliases, self.sems, self.start, self.finish = ins, outs, aliases, sems, start, finish


def _matmul(a, b, *, m, n, k, ta=False, tb=False, tm, tn, tk, out_dtypes, name,
            a_off=(0, 0), b_off=(0, 0), b_row0=None, aux=(), epilogue=None, n_sums=0, carry=None):
    assert m % tm == 0 and n % tn == 0 and k % tk == 0, (name, m, n, k, tm, tn, tk)
    nk = k // tk
    n_aux = len(aux)
    n_tiles = len(out_dtypes)
    n_out = n_tiles + n_sums
    ar, ac = a_off
    br, bc = b_off
    if ta:
        a_spec = pl.BlockSpec((tk, tm), lambda i, j, kk: (kk + ar, i + ac))
    else:
        a_spec = pl.BlockSpec((tm, tk), lambda i, j, kk: (i + ar, kk + ac))
    if tb:
        if b_row0 is None:
            b_spec = pl.BlockSpec((tn, tk), lambda i, j, kk: (j + br, kk + bc))
        else:
            assert b_row0 % LANES == 0 and tn % LANES == 0 and tk % LANES == 0
            b_spec = pl.BlockSpec((pl.Element(tn), pl.Element(tk)),
                                  lambda i, j, kk: (pl.multiple_of(b_row0 + j * tn, LANES), pl.multiple_of((kk + bc) * tk, LANES)))
    else:
        b_spec = pl.BlockSpec((tk, tn), lambda i, j, kk: (kk + br, j + bc))
    aux_specs = []
    for _, kind in aux:
        if kind == "mn":
            aux_specs.append(pl.BlockSpec((tm, tn), lambda i, j, kk: (i, j)))
        else:
            aux_specs.append(pl.BlockSpec((1, tn), lambda i, j, kk: (0, j)))
    ca = 0 if ta else 1
    cb = 1 if tb else 0

    def finish(acc, aux_refs, out_refs, row_tile):
        outs = (acc,) if epilogue is None else epilogue(acc, *[r[...] for r in aux_refs])
        for o_ref, o in zip(out_refs[:n_tiles], outs[:n_tiles]):
            o_ref[...] = o.astype(o_ref.dtype)
        _accumulate_over_rows(out_refs[n_tiles:], outs[n_tiles:], row_tile)

    def body(a_ref, b_ref, *rest):
        aux_refs = rest[:n_aux]
        out_refs = rest[n_aux:n_aux + n_out]
        row_tile = pl.program_id(0)
        prod = _dot(a_ref[...].astype(BF16), b_ref[...].astype(BF16), ca, cb)
        if nk == 1:
            finish(prod, aux_refs, out_refs, row_tile)
            return
        acc_ref = rest[n_aux + n_out]
        kk = pl.program_id(2)

        @pl.when(kk == 0)
        def _():
            acc_ref[...] = prod

        @pl.when(jnp.logical_and(kk > 0, kk < nk - 1))
        def _():
            acc_ref[...] += prod

        @pl.when(kk == nk - 1)
        def _():
            finish(acc_ref[...] + prod, aux_refs, out_refs, row_tile)

    tile = pl.BlockSpec((tm, tn), lambda i, j, kk: (i, j))
    col_sum = pl.BlockSpec((1, tn), lambda i, j, kk: (0, j))
    res = _call_with_carry(
        body, carry, name=name, grid=(m // tm, n // tn, nk), in_specs=[a_spec, b_spec] + aux_specs,
        out_specs=[tile] * n_tiles + [col_sum] * n_sums,
        out_shape=[jax.ShapeDtypeStruct((m, n), dt) for dt in out_dtypes] + [jax.ShapeDtypeStruct((1, n), F32)] * n_sums,
        scratch=[pltpu.VMEM((tm, tn), F32)] if nk > 1 else [], operands=[a, b] + [x for x, _ in aux],
        semantics=("arbitrary" if n_sums else "parallel", "parallel", "arbitrary"))
    main = res[0] if n_out == 1 else tuple(res[:n_out])
    return main if carry is None else (main, list(res[n_out:]))


def _accumulate_over_rows(sum_refs, terms, row_tile):
    for s_ref, term in zip(sum_refs, terms):
        @pl.when(row_tile == 0)
        def _():
            s_ref[...] = term

        @pl.when(row_tile > 0)
        def _():
            s_ref[...] += term


def _call_with_carry(body, carry, *, name, grid, in_specs, out_specs, out_shape, scratch, operands, semantics,
                     vmem=VMEM_LIMIT_BYTES):
    if carry is None:
        return pl.pallas_call(body, name=name, grid=grid, in_specs=in_specs, out_specs=out_specs, out_shape=out_shape,
                              scratch_shapes=scratch, compiler_params=_params(semantics, vmem))(*operands)
    n_in, n_cin, n_out, n_cout, n_scr = len(operands), len(carry.ins), len(out_shape), len(carry.outs), len(scratch)

    def hosted(*refs):
        main_in, c_in = refs[:n_in], refs[n_in:n_in + n_cin]
        main_out = refs[n_in + n_cin:n_in + n_cin + n_out]
        c_out = refs[n_in + n_cin + n_out:n_in + n_cin + n_out + n_cout]
        rest = refs[n_in + n_cin + n_out + n_cout:]
        ids = [pl.program_id(t) for t in range(len(grid))]
        first = functools.reduce(jnp.logical_and, [i == 0 for i in ids])
        last = functools.reduce(jnp.logical_and, [i == g - 1 for i, g in zip(ids, grid)])

        @pl.when(first)
        def _():
            carry.start(c_in, c_out, *rest[n_scr:])

        body(*main_in, *main_out, *rest[:n_scr])

        @pl.when(last)
        def _():
            carry.finish(c_in, c_out, *rest[n_scr:])

    return pl.pallas_call(
        hosted, name=name, grid=grid,
        in_specs=list(in_specs) + [ANY] * n_cin, out_specs=list(out_specs) + [ANY] * n_cout,
        out_shape=list(out_shape) + list(carry.outs),
        input_output_aliases={n_in + i: n_out + o for i, o in carry.aliases.items()},
        scratch_shapes=list(scratch) + [pltpu.SemaphoreType.DMA(s) for s in carry.sems],
        compiler_params=_params(("arbitrary",) * len(grid), vmem),
    )(*operands, *carry.ins)


def _sum_matmul(pieces, b, offs, *, tm, name, tb=False, out_dtype=F32, aux=(), epilogue=None, n_sums=0, carry=None,
                vmem=VMEM_LIMIT_BYTES):
    m = pieces[0].shape[0]
    n = b.shape[0] if tb else b.shape[1]
    npieces, n_aux = len(pieces), len(aux)
    assert not tb or npieces == 1

    def body(*refs):
        b_ref = refs[npieces]
        aux_refs = refs[npieces + 1:npieces + 1 + n_aux]
        out_refs = refs[npieces + 1 + n_aux:]
        acc = None
        for p_ref, off in zip(refs[:npieces], offs):
            lhs = p_ref[...].astype(BF16)
            part = _dot(lhs, b_ref[...], 1, 1) if tb else _dot(lhs, b_ref[pl.ds(off, p_ref.shape[1]), :], 1, 0)
            acc = part if acc is None else acc + part
        outs = (acc,) if epilogue is None else epilogue(acc, *[r[...] for r in aux_refs])
        out_refs[0][...] = outs[0].astype(out_dtype)
        _accumulate_over_rows(out_refs[1:], outs[1:], pl.program_id(0))

    row = pl.BlockSpec((tm, n), lambda i: (i, 0))
    vec = pl.BlockSpec((1, n), lambda i: (0, 0))
    res = _call_with_carry(
        body, carry, name=name, grid=(m // tm,),
        in_specs=[pl.BlockSpec((tm, p.shape[1]), lambda i: (i, 0)) for p in pieces] + [_resident(b.shape)]
        + [row if kind == "mn" else vec for _, kind in aux],
        out_specs=[row] + [vec] * n_sums,
        out_shape=[jax.ShapeDtypeStruct((m, n), out_dtype)] + [jax.ShapeDtypeStruct((1, n), F32)] * n_sums,
        scratch=[], operands=list(pieces) + [b] + [x for x, _ in aux], semantics=("arbitrary" if n_sums else "parallel",),
        vmem=vmem)
    main = res[0] if n_sums == 0 else tuple(res[:1 + n_sums])
    return main if carry is None else (main, list(res[1 + n_sums:]))


def _split_matmul(a, b_t, splits, *, tm, name, carry=None, vmem=VMEM_LIMIT_BYTES):
    m, k = a.shape

    def body(a_ref, b_ref, *out_refs):
        av = a_ref[...].astype(BF16)
        for (row0, width), o_ref in zip(splits, out_refs):
            o_ref[...] = _dot(av, b_ref[pl.ds(row0, width), :], 1, 1)

    res = _call_with_carry(
        body, carry, name=name, grid=(m // tm,),
        in_specs=[pl.BlockSpec((tm, k), lambda i: (i, 0)), _resident(b_t.shape)],
        out_specs=[pl.BlockSpec((tm, width), lambda i: (i, 0)) for _, width in splits],
        out_shape=[jax.ShapeDtypeStruct((m, width), F32) for _, width in splits],
        scratch=[], operands=[a, b_t], semantics=("parallel",), vmem=vmem)
    outs = tuple(res[:len(splits)])
    return outs if carry is None else (outs, list(res[len(splits):]))


def _rmsnorm_fwd(x, g, *, tm, name):
    rows, d = x.shape

    def body(x_ref, g_ref, o_ref):
        xv = x_ref[...]
        r = lax.rsqrt(jnp.mean(xv * xv, axis=-1, keepdims=True) + RMS_EPS)
        o_ref[...] = (xv * r * g_ref[...]).astype(o_ref.dtype)

    return pl.pallas_call(
        body, name=name, grid=(rows // tm,),
        in_specs=[pl.BlockSpec((tm, d), lambda i: (i, 0)), pl.BlockSpec((1, d), lambda i: (0, 0))],
        out_specs=pl.BlockSpec((tm, d), lambda i: (i, 0)),
        out_shape=jax.ShapeDtypeStruct((rows, d), BF16),
        compiler_params=_params(("parallel",)),
    )(x, g)


def _residual_norm_epilogue(acc, xv, gv):
    h = acc + xv
    r = lax.rsqrt(jnp.mean(h * h, axis=-1, keepdims=True) + RMS_EPS)
    return h, h * r * gv


def _rmsnorm_bwd_epilogue(dy, xv, resv, gv):
    r = lax.rsqrt(jnp.mean(xv * xv, axis=-1, keepdims=True) + RMS_EPS)
    xhat = xv * r
    dyg = dy * gv
    dx = r * (dyg - xhat * jnp.mean(dyg * xhat, axis=-1, keepdims=True)) + resv
    return dx, jnp.sum(dy * xhat, axis=0, keepdims=True)


def _rmsnorm_bwd(x, g, dy, res, *, tm, name):
    rows, d = x.shape
    has_res = res is not None

    def body(x_ref, g_ref, dy_ref, *rest):
        if has_res:
            res_ref, dx_ref, dg_ref = rest
        else:
            dx_ref, dg_ref = rest
        i = pl.program_id(0)
        xv = x_ref[...]
        r = lax.rsqrt(jnp.mean(xv * xv, axis=-1, keepdims=True) + RMS_EPS)
        xhat = xv * r
        dyv = dy_ref[...]
        dyg = dyv * g_ref[...]
        dx = r * (dyg - xhat * jnp.mean(dyg * xhat, axis=-1, keepdims=True))
        if has_res:
            dx = dx + res_ref[...]
        dx_ref[...] = dx

        @pl.when(i == 0)
        def _():
            dg_ref[...] = jnp.zeros_like(dg_ref)

        dg_ref[...] += jnp.sum(dyv * xhat, axis=0, keepdims=True)

    row_spec = pl.BlockSpec((tm, d), lambda i: (i, 0))
    vec_spec = pl.BlockSpec((1, d), lambda i: (0, 0))
    ins = [x, g, dy] + ([res] if has_res else [])
    return pl.pallas_call(
        body, name=name, grid=(rows // tm,),
        in_specs=[row_spec, vec_spec, row_spec] + ([row_spec] if has_res else []),
        out_specs=[row_spec, vec_spec],
        out_shape=[jax.ShapeDtypeStruct((rows, d), F32), jax.ShapeDtypeStruct((1, d), F32)],
        compiler_params=_params(("arbitrary",)),
    )(*ins)


def _loss_head_epilogue(acc, hv, tgtv, gv):
    xv = acc + hv
    r = lax.rsqrt(jnp.mean(xv * xv, axis=-1, keepdims=True) + RMS_EPS)
    xhat = xv * r
    err = xhat * gv - tgtv
    dyv = err * (1.0 / D_MODEL)
    dyg = dyv * gv
    dh = r * (dyg - xhat * jnp.mean(dyg * xhat, axis=-1, keepdims=True))
    return dh, jnp.sum(dyv * xhat, axis=0, keepdims=True), jnp.sum(err * err, axis=0, keepdims=True)


def _to_scan_layout(v):
    lead = v.shape[:-2]
    v = v.reshape(lead + (2, N_STATES // SCAN_CB, SCAN_CB))
    v = jnp.swapaxes(v, -3, -2)
    return v.reshape(lead + (2 * N_STATES,))


def _ssm_matrices(lam_re, lam_im, log_dt, b_re, b_im, c_re, c_im):
    dt = jnp.exp(log_dt)[:, None]
    mag = jnp.exp(lam_re * dt)
    a_re, a_im = mag * jnp.cos(lam_im * dt), mag * jnp.sin(lam_im * dt)
    nr, ni = a_re - 1.0, a_im
    den = lam_re * lam_re + lam_im * lam_im
    coef_re = (nr * lam_re + ni * lam_im) / den
    coef_im = (ni * lam_re - nr * lam_im) / den
    bb_re = coef_re[..., None] * b_re - coef_im[..., None] * b_im
    bb_im = coef_re[..., None] * b_im + coef_im[..., None] * b_re
    a_lay = _to_scan_layout(jnp.stack([a_re.reshape(-1), a_im.reshape(-1)], axis=0))[None, :]
    nblk = SSM_GROUPS // SCAN_GROUPS
    eye = jnp.eye(SCAN_GROUPS, dtype=F32)

    def b_block(bb):
        bb = bb.reshape(nblk, SCAN_GROUPS, SSM_STATE, SSM_GROUP_SIZE)
        return jnp.einsum("gk,jkph->jghkp", eye, bb).reshape(nblk, SCAN_GROUPS * SSM_GROUP_SIZE, SCAN_CB)

    b_blk = jnp.concatenate([b_block(bb_re), b_block(bb_im)], axis=2)

    def c_block(cc):
        cc = cc.reshape(nblk, SCAN_GROUPS, SSM_GROUP_SIZE, SSM_STATE)
        return jnp.einsum("gk,jghp->jkpgh", eye, cc).reshape(nblk, SCAN_CB, SCAN_GROUPS * SSM_GROUP_SIZE)

    c_blk = jnp.concatenate([c_block(c_re), -c_block(c_im)], axis=1)
    return a_lay, b_blk, c_blk


def _interleave(v):
    rows, c = v.shape
    return v.reshape(SCAN_SEGS, rows // SCAN_SEGS, c).transpose(1, 0, 2).reshape(rows, c)


def _deinterleave(v):
    rows, c = v.shape
    return v.reshape(rows // SCAN_SEGS, SCAN_SEGS, c).transpose(1, 0, 2).reshape(rows, c)


def _scan_groups(a_ref, bu_ref, o_ref, state, *, reverse, tt):
    cb = SCAN_CB
    ar = jnp.broadcast_to(a_ref[:, :cb], (SCAN_SEGS, cb))
    ai = jnp.broadcast_to(a_ref[:, cb:], (SCAN_SEGS, cb))
    ngroups = tt // SCAN_SEGS

    def step(i, st):
        sr, si = st
        r0 = pl.multiple_of(((ngroups - 1 - i) if reverse else i) * SCAN_SEGS, SCAN_SEGS)
        blk = bu_ref[pl.ds(r0, SCAN_SEGS), :]
        nr = ar * sr - ai * si + blk[:, :cb]
        ni = ar * si + ai * sr + blk[:, cb:]
        if o_ref is not None:
            o_ref[pl.ds(r0, SCAN_SEGS), :] = jnp.concatenate([nr, ni], axis=1)
        return nr, ni

    return lax.fori_loop(0, ngroups, step, state, unroll=4)


def _segment_entries(a_ref, e_ref, init_ref, *, reverse, seg_len):
    cb = SCAN_CB
    n_sq = seg_len.bit_length() - 1
    assert 1 << n_sq == seg_len, seg_len
    pr, pi = a_ref[:, :cb], a_ref[:, cb:]
    for _ in range(n_sq):
        pr, pi = pr * pr - pi * pi, 2.0 * pr * pi
    cr = jnp.zeros((1, cb), F32)
    ci = jnp.zeros((1, cb), F32)
    order = range(SCAN_SEGS - 1, -1, -1) if reverse else range(SCAN_SEGS)
    for k, seg in enumerate(order):
        if k > 0:
            prev = seg + 1 if reverse else seg - 1
            er, ei = e_ref[prev:prev + 1, :cb], e_ref[prev:prev + 1, cb:]
            cr, ci = pr * cr - pi * ci + er, pr * ci + pi * cr + ei
        init_ref[seg:seg + 1, :] = jnp.concatenate([cr, ci], axis=1)


def _ssm_specs(nt, tt, nch, reverse):
    cb = SCAN_CB
    tmap = (lambda j, kk: (nt - 1 - kk, j)) if reverse else (lambda j, kk: (kk, j))
    return dict(a=pl.BlockSpec((1, 2 * cb), lambda j, kk: (0, j)),
                seg=pl.BlockSpec((SCAN_SEGS, 2 * cb), lambda j, kk: (0, j)),
                chan=pl.BlockSpec((tt, nch), tmap),
                state=pl.BlockSpec((tt, 2 * cb), tmap),
                b=pl.BlockSpec((None, nch, 2 * cb), lambda j, kk: (j, 0, 0)),
                c=pl.BlockSpec((None, 2 * cb, nch), lambda j, kk: (j, 0, 0)))


def _ssm_ends(a_lay, x, blocks, *, transpose, reverse, tt, name):
    rows = x.shape[0]
    nblk = blocks.shape[0]
    nch = x.shape[1] // nblk
    cb = SCAN_CB
    nt = rows // tt
    sp = _ssm_specs(nt, tt, nch, reverse)

    def body(a_ref, x_ref, w_ref, e_ref, bu_ref):
        kk = pl.program_id(1)

        @pl.when(kk == 0)
        def _():
            e_ref[...] = jnp.zeros_like(e_ref)

        bu_ref[...] = _dot(x_ref[...].astype(BF16), w_ref[...].astype(BF16), 1, 1 if transpose else 0)
        sr, si = _scan_groups(a_ref, bu_ref, None, (e_ref[:, :cb], e_ref[:, cb:]), reverse=reverse, tt=tt)
        e_ref[...] = jnp.concatenate([sr, si], axis=1)

    return pl.pallas_call(
        body, name=name, grid=(nblk, nt),
        in_specs=[sp["a"], sp["chan"], sp["c"] if transpose else sp["b"]],
        out_specs=sp["seg"],
        out_shape=jax.ShapeDtypeStruct((SCAN_SEGS, nblk * 2 * cb), F32),
        scratch_shapes=[pltpu.VMEM((tt, 2 * cb), F32)],
        compiler_params=_params(("parallel", "arbitrary")),
    )(a_lay, x, blocks)


def _ssm_fwd(a_lay, u, b_blk, c_blk, ends, *, tt, name):
    rows = u.shape[0]
    nblk = b_blk.shape[0]
    nch = u.shape[1] // nblk
    cb = SCAN_CB
    nt = rows // tt
    sp = _ssm_specs(nt, tt, nch, False)

    def body(a_ref, e_ref, u_ref, b_ref, c_ref, s_ref, y_ref, init_ref, carry_ref):
        kk = pl.program_id(1)

        @pl.when(kk == 0)
        def _():
            _segment_entries(a_ref, e_ref, init_ref, reverse=False, seg_len=rows // SCAN_SEGS)
            carry_ref[...] = init_ref[...]

        s_ref[...] = _dot(u_ref[...].astype(BF16), b_ref[...].astype(BF16), 1, 0)
        sr, si = _scan_groups(a_ref, s_ref, s_ref, (carry_ref[:, :cb], carry_ref[:, cb:]), reverse=False, tt=tt)
        carry_ref[...] = jnp.concatenate([sr, si], axis=1)
        y_ref[...] = _dot(s_ref[...].astype(BF16), c_ref[...].astype(BF16), 1, 0)

    return pl.pallas_call(
        body, name=name, grid=(nblk, nt),
        in_specs=[sp["a"], sp["seg"], sp["chan"], sp["b"], sp["c"]],
        out_specs=[sp["state"], sp["chan"], sp["seg"]],
        out_shape=[jax.ShapeDtypeStruct((rows, nblk * 2 * cb), F32), jax.ShapeDtypeStruct((rows, nblk * nch), F32),
                   jax.ShapeDtypeStruct((SCAN_SEGS, nblk * 2 * cb), F32)],
        scratch_shapes=[pltpu.VMEM((SCAN_SEGS, 2 * cb), F32)],
        compiler_params=_params(("parallel", "arbitrary")),
    )(a_lay, ends, u, b_blk, c_blk)


def _ssm_bwd(a_conj, dy, u, s, s_entry, b_blk, c_blk, dd, ends, *, tt, name):
    rows = u.shape[0]
    nblk = b_blk.shape[0]
    nch = u.shape[1] // nblk
    cb = SCAN_CB
    nt = rows // tt
    sp = _ssm_specs(nt, tt, nch, True)
    groups_per_tile = tt // SCAN_SEGS
    before = pl.BlockSpec((SCAN_SEGS, 2 * cb), lambda j, kk: (jnp.maximum((nt - 1 - kk) * groups_per_tile - 1, 0), j))

    def body(a_ref, e_ref, dy_ref, u_ref, s_ref, before_ref, entry_ref, b_ref, c_ref, dd_ref,
             du_ref, db_ref, dc_ref, da_ref, lam_ref, carry_ref):
        kk = pl.program_id(1)

        @pl.when(kk == 0)
        def _():
            _segment_entries(a_ref, e_ref, carry_ref, reverse=True, seg_len=rows // SCAN_SEGS)
            db_ref[...] = jnp.zeros_like(db_ref)
            dc_ref[...] = jnp.zeros_like(dc_ref)
            da_ref[...] = jnp.zeros_like(da_ref)

        dyv = dy_ref[...]
        dyb = dyv.astype(BF16)
        lam_ref[...] = _dot(dyb, c_ref[...].astype(BF16), 1, 1)
        lr, li = _scan_groups(a_ref, lam_ref, lam_ref, (carry_ref[:, :cb], carry_ref[:, cb:]), reverse=True, tt=tt)
        carry_ref[...] = jnp.concatenate([lr, li], axis=1)

        first = jnp.where(kk == nt - 1, entry_ref[...], before_ref[...])
        rest = tt - SCAN_SEGS
        lam_hi = lam_ref[pl.ds(SCAN_SEGS, rest), :]
        s_lo = s_ref[pl.ds(0, rest), :]
        lam_lo = lam_ref[pl.ds(0, SCAN_SEGS), :]

        def pair(lv, pv):
            lre, lim, pre, pim = lv[:, :cb], lv[:, cb:], pv[:, :cb], pv[:, cb:]
            return (jnp.sum(lre * pre + lim * pim, axis=0, keepdims=True),
                    jnp.sum(lim * pre - lre * pim, axis=0, keepdims=True))

        r1, i1 = pair(lam_hi, s_lo)
        r0, i0 = pair(lam_lo, first)
        da_ref[...] += jnp.concatenate([r1 + r0, i1 + i0], axis=1)

        lamb = lam_ref[...].astype(BF16)
        du_ref[...] = _dot(lamb, b_ref[...].astype(BF16), 1, 1) + dd_ref[...] * dyv
        db_ref[...] += _dot(u_ref[...].astype(BF16), lamb, 0, 0)
        dc_ref[...] += _dot(s_ref[...].astype(BF16), dyb, 0, 0)

    return pl.pallas_call(
        body, name=name, grid=(nblk, nt),
        in_specs=[sp["a"], sp["seg"], sp["chan"], sp["chan"], sp["state"], before, sp["seg"], sp["b"], sp["c"],
                  pl.BlockSpec((1, nch), lambda j, kk: (0, j))],
        out_specs=[sp["chan"], sp["b"], sp["c"], pl.BlockSpec((1, 2 * cb), lambda j, kk: (0, j))],
        out_shape=[jax.ShapeDtypeStruct((rows, nblk * nch), F32), jax.ShapeDtypeStruct(b_blk.shape, F32),
                   jax.ShapeDtypeStruct(c_blk.shape, F32), jax.ShapeDtypeStruct((1, nblk * 2 * cb), F32)],
        scratch_shapes=[pltpu.VMEM((tt, 2 * cb), F32), pltpu.VMEM((SCAN_SEGS, 2 * cb), F32)],
        compiler_params=_params(("parallel", "arbitrary")),
    )(a_conj, ends, dy, u, s, s, s_entry, b_blk, c_blk, dd)


def _glu_fwd(ys, u, dd, w_glu, b_glu, *, tm, name):
    rows, w = ys.shape

    def body(ys_ref, u_ref, dd_ref, w_ref, b_ref, y0_ref, t_ref, y2_ref):
        y0 = ys_ref[...] + dd_ref[...] * u_ref[...]
        y1 = _gelu(y0)
        t = _dot(y1.astype(BF16), w_ref[...], 1, 0) + b_ref[...]
        y0_ref[...] = y0
        t_ref[...] = t
        y2_ref[...] = (y1 * _sigmoid(t)).astype(BF16)

    row = pl.BlockSpec((tm, w), lambda i: (i, 0))
    vec = pl.BlockSpec((1, w), lambda i: (0, 0))
    return pl.pallas_call(
        body, name=name, grid=(rows // tm,),
        in_specs=[row, row, vec, pl.BlockSpec((w, w), lambda i: (0, 0)), vec],
        out_specs=[row, row, row],
        out_shape=[jax.ShapeDtypeStruct((rows, w), F32), jax.ShapeDtypeStruct((rows, w), F32),
                   jax.ShapeDtypeStruct((rows, w), BF16)],
        compiler_params=_params(("parallel",)),
    )(ys, u, dd, w_glu, b_glu)


def _glu_bwd(dy2, y0, t, u, w_glu, *, tm, name):
    rows, w = y0.shape

    def body(dy2_ref, y0_ref, t_ref, u_ref, w_ref, dy0_ref, dt_ref, y1_ref, db_ref, dd_ref):
        i = pl.program_id(0)
        y0 = y0_ref[...]
        y1 = _gelu(y0)
        sg = _sigmoid(t_ref[...])
        dy2v = dy2_ref[...]
        dt = dy2v * y1 * sg * (1.0 - sg)
        dy1 = dy2v * sg + _dot(dt.astype(BF16), w_ref[...], 1, 1)
        dy0 = dy1 * _gelu_grad(y0)
        dy0_ref[...] = dy0
        dt_ref[...] = dt.astype(BF16)
        y1_ref[...] = y1.astype(BF16)

        @pl.when(i == 0)
        def _():
            db_ref[...] = jnp.zeros_like(db_ref)
            dd_ref[...] = jnp.zeros_like(dd_ref)

        db_ref[...] += jnp.sum(dt, axis=0, keepdims=True)
        dd_ref[...] += jnp.sum(dy0 * u_ref[...], axis=0, keepdims=True)

    row = pl.BlockSpec((tm, w), lambda i: (i, 0))
    vec = pl.BlockSpec((1, w), lambda i: (0, 0))
    return pl.pallas_call(
        body, name=name, grid=(rows // tm,),
        in_specs=[row, row, row, row, pl.BlockSpec((w, w), lambda i: (0, 0))],
        out_specs=[row, row, row, vec, vec],
        out_shape=[jax.ShapeDtypeStruct((rows, w), F32), jax.ShapeDtypeStruct((rows, w), BF16),
                   jax.ShapeDtypeStruct((rows, w), BF16), jax.ShapeDtypeStruct((1, w), F32),
                   jax.ShapeDtypeStruct((1, w), F32)],
        compiler_params=_params(("arbitrary",)),
    )(dy2, y0, t, u, w_glu)


ATTN_TILE = 2048


def _attn_geometry(rows, d):
    sb = ATTN_Q * d
    tr = max(sb, min(ATTN_TILE, rows))
    assert rows % tr == 0 and tr % sb == 0, (rows, d)
    return sb, tr, rows // tr, tr // sb


def _attn_masks():
    qi = lax.broadcasted_iota(jnp.int32, (2 * ATTN_Q, 2 * ATTN_Q), 0) % ATTN_Q
    kj = lax.broadcasted_iota(jnp.int32, (2 * ATTN_Q, 2 * ATTN_Q), 1)
    own_ok = jnp.logical_and(kj >= ATTN_Q, kj - ATTN_Q <= qi)
    prev_ok = jnp.logical_and(kj < ATTN_Q, kj >= qi)
    bias_first = jnp.where(own_ok, 0.0, NEG_INF)
    bias_other = jnp.where(jnp.logical_or(own_ok, prev_ok), 0.0, NEG_INF)
    head0 = lax.broadcasted_iota(jnp.int32, (ATTN_Q, LANES), 1) < ATTN_HEAD_DIM
    return bias_first, bias_other, head0


def _attn_rows(base, n, d):
    return pl.ds(pl.multiple_of(base, ATTN_Q), n) if d == 1 else pl.ds(base, n, stride=d)


def _stack_heads(v, head0):
    return jnp.concatenate([jnp.where(head0, v, 0.0), jnp.where(head0, 0.0, v)], axis=0)


def _unstack_heads(v, head0):
    return jnp.where(head0, v[:ATTN_Q], v[ATTN_Q:])


def _fill_keys(buf, prev_ref, cur_ref, sb):
    buf[pl.ds(0, sb), :] = prev_ref[...]
    buf[pl.ds(sb, cur_ref.shape[0]), :] = cur_ref[...]


def _attn_fwd(qkv, g, d, *, name):
    rows = qkv.shape[0]
    sb, tr, ntiles, nsub = _attn_geometry(rows, d)
    qc, kc, vc = 2 * g, 6 + 2 * g, 12 + 2 * g
    scale = ATTN_HEAD_DIM ** -0.5

    def body(q_ref, kc_ref, kp_ref, vc_ref, vp_ref, o_ref, lse_ref, kbuf, vbuf):
        n = pl.program_id(0)
        _fill_keys(kbuf, kp_ref, kc_ref, sb)
        _fill_keys(vbuf, vp_ref, vc_ref, sb)
        bias_first, bias_other, head0 = _attn_masks()

        def per_block(idx, carry):
            j, r = idx // d, idx % d
            base = j * sb + r
            bias = jnp.where(jnp.logical_and(n == 0, j == 0), bias_first, bias_other)
            qrows = _attn_rows(base, ATTN_Q, d)
            krows = _attn_rows(base, 2 * ATTN_Q, d)
            qs = (_stack_heads(q_ref[qrows, :], head0) * scale).astype(BF16)
            s = _dot(qs, kbuf[krows, :].astype(BF16), 1, 1) + bias
            mx = jnp.max(s, axis=-1, keepdims=True)
            p = jnp.exp(s - mx)
            den = jnp.sum(p, axis=-1, keepdims=True)
            pv = _dot(p.astype(BF16), vbuf[krows, :].astype(BF16), 1, 0) / den
            o_ref[qrows, :] = _unstack_heads(pv, head0)
            lse_ref[qrows, :] = _unstack_heads(jnp.broadcast_to(mx + jnp.log(den), (2 * ATTN_Q, LANES)), head0)
            return carry

        lax.fori_loop(0, nsub * d, per_block, 0, unroll=8)

    def cur(col):
        return pl.BlockSpec((tr, LANES), lambda n, hp: (n, col + hp))

    def prev(col):
        return pl.BlockSpec((sb, LANES), lambda n, hp: (jnp.maximum(n * nsub - 1, 0), col + hp))

    out_spec = pl.BlockSpec((tr, LANES), lambda n, hp: (n, hp))
    return pl.pallas_call(
        body, name=name, grid=(ntiles, 2),
        in_specs=[cur(qc), cur(kc), prev(kc), cur(vc), prev(vc)],
        out_specs=[out_spec, out_spec],
        out_shape=[jax.ShapeDtypeStruct((rows, 2 * LANES), F32), jax.ShapeDtypeStruct((rows, 2 * LANES), F32)],
        scratch_shapes=[pltpu.VMEM((sb + tr, LANES), F32), pltpu.VMEM((sb + tr, LANES), F32)],
        compiler_params=_params(("parallel", "parallel")),
    )(qkv, qkv, qkv, qkv, qkv)


def _attn_merge(outs, lses, *, tm, name):
    rows, w = outs[0].shape

    def body(o0, o1, o2, l0, l1, l2, o_ref, lse_ref):
        a0, a1, a2 = l0[...], l1[...], l2[...]
        mx = jnp.maximum(jnp.maximum(a0, a1), a2)
        e0, e1, e2 = jnp.exp(a0 - mx), jnp.exp(a1 - mx), jnp.exp(a2 - mx)
        den = e0 + e1 + e2
        o_ref[...] = (e0 / den) * o0[...] + (e1 / den) * o1[...] + (e2 / den) * o2[...]
        lse_ref[...] = mx + jnp.log(den)

    row = pl.BlockSpec((tm, w), lambda i: (i, 0))
    return pl.pallas_call(
        body, name=name, grid=(rows // tm,), in_specs=[row] * 6, out_specs=[row, row],
        out_shape=[jax.ShapeDtypeStruct((rows, w), F32), jax.ShapeDtypeStruct((rows, w), F32)],
        compiler_params=_params(("parallel",)),
    )(*outs, *lses)


def _attn_bwd(qkv, do, o, lse, g, d, prev, *, name):
    rows = qkv.shape[0]
    sb, tr, ntiles, nsub = _attn_geometry(rows, d)
    qc, kc, vc = 2 * g, 6 + 2 * g, 12 + 2 * g
    scale = ATTN_HEAD_DIM ** -0.5

    def body(q_ref, kc_ref, kp_ref, vc_ref, vp_ref, do_ref, o_ref, lse_ref, dq_ref, dk_ref, dv_ref,
             kbuf, vbuf, dk_acc, dv_acc):
        n = pl.program_id(1)

        @pl.when(n == 0)
        def _():
            dk_acc[pl.ds(0, tr), :] = jnp.zeros((tr, LANES), F32)
            dv_acc[pl.ds(0, tr), :] = jnp.zeros((tr, LANES), F32)

        @pl.when(n < ntiles)
        def _():
            dk_acc[pl.ds(tr, tr), :] = jnp.zeros((tr, LANES), F32)
            dv_acc[pl.ds(tr, tr), :] = jnp.zeros((tr, LANES), F32)
            _fill_keys(kbuf, kp_ref, kc_ref, sb)
            _fill_keys(vbuf, vp_ref, vc_ref, sb)
            bias_first, bias_other, head0 = _attn_masks()
            lane = lax.broadcasted_iota(jnp.int32, (ATTN_Q, LANES), 1)

            def per_block(idx, carry):
                j, r = idx // d, idx % d
                base = j * sb + r
                bias = jnp.where(jnp.logical_and(n == 0, j == 0), bias_first, bias_other)
                qrows = _attn_rows(base, ATTN_Q, d)
                krows = _attn_rows(base, 2 * ATTN_Q, d)
                arows = _attn_rows(base + (tr - sb), 2 * ATTN_Q, d)
                qs = (_stack_heads(q_ref[qrows, :], head0) * scale).astype(BF16)
                dos = _stack_heads(do_ref[qrows, :], head0)
                dosb = dos.astype(BF16)
                ov = o_ref[qrows, :]
                delta = jnp.sum(dos * jnp.concatenate([ov, ov], axis=0), axis=-1, keepdims=True)
                lsev = lse_ref[qrows, :]
                lse_s = jnp.concatenate(
                    [jnp.sum(jnp.where(lane == h * ATTN_HEAD_DIM, lsev, 0.0), axis=-1, keepdims=True) for h in range(2)], axis=0)
                kb = kbuf[krows, :].astype(BF16)
                vb = vbuf[krows, :].astype(BF16)
                p = jnp.exp(_dot(qs, kb, 1, 1) + bias - lse_s)
                ds = (p * (_dot(dosb, vb, 1, 1) - delta)).astype(BF16)
                dq_ref[qrows, :] = _unstack_heads(_dot(ds, kb, 1, 0), head0) * scale
                dk_acc[arows, :] += _dot(ds, qs, 0, 0)
                dv_acc[arows, :] += _dot(p.astype(BF16), dosb, 0, 0)
                return carry

            lax.fori_loop(0, nsub * d, per_block, 0, unroll=4)

        dk_ref[...] = dk_acc[pl.ds(0, tr), :]
        dv_ref[...] = dv_acc[pl.ds(0, tr), :]
        dk_acc[pl.ds(0, tr), :] = dk_acc[pl.ds(tr, tr), :]
        dv_acc[pl.ds(0, tr), :] = dv_acc[pl.ds(tr, tr), :]

    def cur(n):
        return jnp.minimum(n, ntiles - 1)

    def spec(col, prev):
        if prev:
            return pl.BlockSpec((sb, LANES), lambda hp, n: (jnp.maximum(cur(n) * nsub - 1, 0), col + hp))
        return pl.BlockSpec((tr, LANES), lambda hp, n: (cur(n), col + hp))

    row_spec = pl.BlockSpec((tr, LANES), lambda hp, n: (cur(n), hp))
    dq_out = pl.BlockSpec((tr, LANES), lambda hp, n: (cur(n), 2 * g + hp))
    kv_out = pl.BlockSpec((tr, LANES), lambda hp, n: (jnp.maximum(n - 1, 0), 2 * g + hp))
    shape = jax.ShapeDtypeStruct((rows, len(ATTN_PATTERNS) * 2 * LANES), F32)
    ins = [qkv, qkv, qkv, qkv, qkv, do, o, lse]
    in_specs = [spec(qc, False), spec(kc, False), spec(kc, True), spec(vc, False), spec(vc, True),
                row_spec, row_spec, row_spec]
    aliases = {}
    if prev is not None:
        aliases = {len(ins) + t: t for t in range(3)}
        ins = ins + list(prev)
        in_specs = in_specs + [ANY] * 3
    n_in = len(ins)

    def entry(*refs):
        body(*refs[:8], *refs[n_in:])

    return pl.pallas_call(
        entry, name=name, grid=(2, ntiles + 1),
        in_specs=in_specs,
        out_specs=[dq_out, kv_out, kv_out],
        out_shape=[shape, shape, shape],
        input_output_aliases=aliases,
        scratch_shapes=[pltpu.VMEM((sb + tr, LANES), F32), pltpu.VMEM((sb + tr, LANES), F32),
                        pltpu.VMEM((2 * tr, LANES), F32), pltpu.VMEM((2 * tr, LANES), F32)],
        compiler_params=_params(("parallel", "arbitrary")),
    )(*ins)


def _mem_probs(q, k):
    s = _dot(q.astype(BF16), k.astype(BF16), 1, 1) * (MEM_HEAD_DIM ** -0.5)
    e = jnp.exp(s - jnp.max(s, axis=-1, keepdims=True))
    return e / jnp.sum(e, axis=-1, keepdims=True)


def _mem_attn_fwd(mq, kv, *, tq, name):
    rows = mq.shape[0]

    def body(q_ref, k_ref, v_ref, o_ref):
        p = _mem_probs(q_ref[...], k_ref[...])
        o_ref[...] = _dot(p.astype(BF16), v_ref[...].astype(BF16), 1, 0)

    return pl.pallas_call(
        body, name=name, grid=(rows // tq, MEM_HEADS),
        in_specs=[pl.BlockSpec((tq, LANES), lambda i, h: (i, h)),
                  pl.BlockSpec((MEM_LEN, LANES), lambda i, h: (0, h)),
                  pl.BlockSpec((MEM_LEN, LANES), lambda i, h: (0, MEM_HEADS + h))],
        out_specs=pl.BlockSpec((tq, LANES), lambda i, h: (i, h)),
        out_shape=jax.ShapeDtypeStruct((rows, MEM_HEADS * LANES), F32),
        compiler_params=_params(("parallel", "parallel")),
    )(mq, kv, kv)


def _mem_attn_bwd(mq, kv, dmo, *, tq, name):
    rows = mq.shape[0]
    scale = MEM_HEAD_DIM ** -0.5

    def body(q_ref, k_ref, v_ref, do_ref, dq_ref, dk_ref, dv_ref):
        i = pl.program_id(1)
        qb = q_ref[...].astype(BF16)
        kb = k_ref[...].astype(BF16)
        vb = v_ref[...].astype(BF16)
        dob = do_ref[...].astype(BF16)
        p = _mem_probs(q_ref[...], k_ref[...])
        dp = _dot(dob, vb, 1, 1)
        ds = (p * (dp - jnp.sum(p * dp, axis=-1, keepdims=True)) * scale).astype(BF16)
        dq_ref[...] = _dot(ds, kb, 1, 0).astype(dq_ref.dtype)

        @pl.when(i == 0)
        def _():
            dk_ref[...] = jnp.zeros_like(dk_ref)
            dv_ref[...] = jnp.zeros_like(dv_ref)

        dk_ref[...] += _dot(ds, qb, 0, 0)
        dv_ref[...] += _dot(p.astype(BF16), dob, 0, 0)

    kv_out = pl.BlockSpec((MEM_LEN, LANES), lambda h, i: (0, h))
    kv_shape = jax.ShapeDtypeStruct((MEM_LEN, MEM_HEADS * LANES), F32)
    return pl.pallas_call(
        body, name=name, grid=(MEM_HEADS, rows // tq),
        in_specs=[pl.BlockSpec((tq, LANES), lambda h, i: (i, h)),
                  pl.BlockSpec((MEM_LEN, LANES), lambda h, i: (0, h)),
                  pl.BlockSpec((MEM_LEN, LANES), lambda h, i: (0, MEM_HEADS + h)),
                  pl.BlockSpec((tq, LANES), lambda h, i: (i, h))],
        out_specs=[pl.BlockSpec((tq, LANES), lambda h, i: (i, h)), kv_out, kv_out],
        out_shape=[jax.ShapeDtypeStruct((rows, MEM_HEADS * LANES), BF16), kv_shape, kv_shape],
        compiler_params=_params(("parallel", "arbitrary")),
    )(mq, kv, kv, dmo)


def _resident(shape):
    return pl.BlockSpec(shape, lambda i: (0, 0), pipeline_mode=pl.Buffered(1))


def _branch_merge_fwd(acts, wts, zg, b_gate, *, tm, name):
    rows = zg.shape[0]
    d = wts[0].shape[0]

    def body(s_ref, a_ref, m_ref, ws_ref, wa_ref, wm_ref, zg_ref, b_ref, o_ref):
        gt = _sigmoid(zg_ref[...] + b_ref[...])
        acc = None
        for k, (x_ref, w_ref) in enumerate(((s_ref, ws_ref), (a_ref, wa_ref), (m_ref, wm_ref))):
            term = gt[:, k * d:(k + 1) * d] * _dot(x_ref[...].astype(BF16), w_ref[...], 1, 1)
            acc = term if acc is None else acc + term
        o_ref[...] = acc.astype(BF16)

    return pl.pallas_call(
        body, name=name, grid=(rows // tm,),
        in_specs=[pl.BlockSpec((tm, x.shape[1]), lambda i: (i, 0)) for x in acts] + [_resident(w.shape) for w in wts]
        + [pl.BlockSpec((tm, 3 * d), lambda i: (i, 0)), pl.BlockSpec((1, 3 * d), lambda i: (0, 0))],
        out_specs=pl.BlockSpec((tm, d), lambda i: (i, 0)), out_shape=jax.ShapeDtypeStruct((rows, d), BF16),
        compiler_params=_params(("parallel",)),
    )(*acts, *wts, zg, b_gate)


def _branch_merge_bwd(dmerged, acts, wts, zg, b_gate, *, tm, name, carry=None):
    rows = zg.shape[0]
    d = wts[0].shape[0]

    def body(dm_ref, s_ref, a_ref, m_ref, ws_ref, wa_ref, wm_ref, zg_ref, b_ref,
             ds_ref, da_ref, dmm_ref, dws_ref, dwa_ref, dwm_ref, dzg_ref, db_ref):
        i = pl.program_id(0)

        @pl.when(i == 0)
        def _():
            for r in (dws_ref, dwa_ref, dwm_ref, db_ref):
                r[...] = jnp.zeros_like(r)

        gt = _sigmoid(zg_ref[...] + b_ref[...])
        dm = dm_ref[...]
        groups = ((s_ref, ws_ref, ds_ref, dws_ref), (a_ref, wa_ref, da_ref, dwa_ref), (m_ref, wm_ref, dmm_ref, dwm_ref))
        for k, (x_ref, w_ref, dx_ref, dw_ref) in enumerate(groups):
            cs = pl.ds(k * d, d)
            gk = gt[:, k * d:(k + 1) * d]
            xb = x_ref[...].astype(BF16)
            br = _dot(xb, w_ref[...], 1, 1)
            dbr = (dm * gk).astype(BF16)
            dx_ref[...] = _dot(dbr, w_ref[...], 1, 0)
            dw_ref[...] += _dot(dbr, xb, 0, 0)
            dzg = dm * br * gk * (1.0 - gk)
            dzg_ref[:, cs] = dzg.astype(BF16)
            db_ref[:, cs] += jnp.sum(dzg, axis=0, keepdims=True)

    row = lambda w: pl.BlockSpec((tm, w), lambda i: (i, 0))
    whole = lambda shape: pl.BlockSpec(shape, lambda i: (0, 0))
    res = _call_with_carry(
        body, carry, name=name, grid=(rows // tm,),
        in_specs=[row(d)] + [row(x.shape[1]) for x in acts] + [_resident(w.shape) for w in wts] + [row(3 * d), whole((1, 3 * d))],
        out_specs=[row(x.shape[1]) for x in acts] + [whole(w.shape) for w in wts] + [row(3 * d), whole((1, 3 * d))],
        out_shape=[jax.ShapeDtypeStruct(x.shape, F32) for x in acts] + [jax.ShapeDtypeStruct(w.shape, F32) for w in wts]
        + [jax.ShapeDtypeStruct((rows, 3 * d), BF16), jax.ShapeDtypeStruct((1, 3 * d), F32)],
        scratch=[], operands=[dmerged, *acts, *wts, zg, b_gate], semantics=("arbitrary",))
    return tuple(res) if carry is None else (tuple(res[:8]), list(res[8:]))


def _adamw(w, g, m, v, *, tr, name):
    rows, cols = w.shape[-2:]
    assert rows % tr == 0, (name, rows, tr)

    def body(w_ref, g_ref, m_ref, v_ref, g_out, d_ref, nm_ref, nv_ref):
        gv = g_ref[...]
        m2 = ADAM_B1 * m_ref[...] + (1.0 - ADAM_B1) * gv
        v2 = ADAM_B2 * v_ref[...] + (1.0 - ADAM_B2) * (gv * gv)
        m_hat = m2 / (1.0 - ADAM_B1 ** ADAM_STEP)
        v_hat = v2 / (1.0 - ADAM_B2 ** ADAM_STEP)
        g_out[...] = gv
        d_ref[...] = -ADAM_LR * (m_hat / (jnp.sqrt(v_hat) + ADAM_EPS) + ADAM_WD * w_ref[...])
        nm_ref[...] = m2
        nv_ref[...] = v2

    flat = pl.BlockSpec((tr, cols), lambda i: (i, 0))
    blk = flat if w.ndim == 2 else pl.BlockSpec((None, tr, cols), lambda i: (0, i, 0))
    shape = jax.ShapeDtypeStruct(w.shape, F32)
    return pl.pallas_call(
        body, name=name, grid=(rows // tr,), in_specs=[blk, flat, blk, blk], out_specs=[blk] * 4,
        out_shape=[shape] * 4, compiler_params=_params(("parallel",)),
    )(w, g, m, v)


ANY = pl.BlockSpec(memory_space=pl.ANY)


def _position():
    return lax.axis_index("x"), lax.axis_index("y"), lax.axis_index("c")


def _other_chips(x, y):
    return ((1 - x, y), (x, 1 - y), (1 - x, 1 - y))


def _remote(src, dst, send_sem, recv_sem, dev):
    return pltpu.make_async_remote_copy(src_ref=src, dst_ref=dst, send_sem=send_sem, recv_sem=recv_sem,
                                        device_id=dev, device_id_type=MESH)


def _gather_exchange(shards):
    nb = len(shards)

    def rows_of(i, owner, core):
        rs = shards[i].shape[0]
        return pl.ds(pl.multiple_of(owner * rs + core * (rs // 2), 16), rs // 2)

    def first_leg(ins, outs, send_sems, recv_sems, i, j):
        x, y, c = _position()
        px, py = _other_chips(x, y)[j]
        half = shards[i].shape[0] // 2
        mine = ins[i].at[pl.ds(pl.multiple_of(c * half, 16), half)]
        return _remote(mine, outs[i].at[rows_of(i, 2 * x + y, c)], send_sems.at[i, j], recv_sems.at[i, j], (px, py, c))

    def passed_on(outs, send_sems, recv_sems, i, j, core):
        x, y, c = _position()
        px, py = _other_chips(x, y)[j]
        rows = outs[i].at[rows_of(i, 2 * px + py, core)]
        return _remote(rows, rows, send_sems.at[i, 3 + j], recv_sems.at[i, 3 + j], (x, y, 1 - c))

    def own_block(ins, outs, send_sems, recv_sems, i):
        x, y, c = _position()
        rs = shards[i].shape[0]
        place = outs[i].at[pl.ds(pl.multiple_of((2 * x + y) * rs, 16), rs)]
        return _remote(ins[i], place, send_sems.at[i, 6], recv_sems.at[i, 6], (x, y, 1 - c))

    def start(ins, outs, send_sems, recv_sems):
        for i in range(nb):
            own_block(ins, outs, send_sems, recv_sems, i).start()
            for j in range(3):
                first_leg(ins, outs, send_sems, recv_sems, i, j).start()

    def finish(ins, outs, send_sems, recv_sems):
        x, y, c = _position()
        for i in range(nb):
            for j, (px, py) in enumerate(_other_chips(x, y)):
                landed = outs[i].at[rows_of(i, 2 * px + py, c)]
                _remote(landed, landed, send_sems.at[i, j], recv_sems.at[i, j], (px, py, c)).wait_recv()
                passed_on(outs, send_sems, recv_sems, i, j, c).start()
        for i in range(nb):
            own_block(ins, outs, send_sems, recv_sems, i).wait()
            for j in range(3):
                passed_on(outs, send_sems, recv_sems, i, j, 1 - c).wait_recv()
        for i in range(nb):
            for j in range(3):
                first_leg(ins, outs, send_sems, recv_sems, i, j).wait_send()
                passed_on(outs, send_sems, recv_sems, i, j, c).wait_send()

    return _Exchange(ins=list(shards), outs=[jax.ShapeDtypeStruct((N_CHIPS * s.shape[0], s.shape[1]), s.dtype) for s in shards],
                     aliases={}, sems=[(nb, 7), (nb, 7)], start=start, finish=finish)


def _run_exchange(ex, *, name):
    n_in, n_out = len(ex.ins), len(ex.outs)

    def body(*refs):
        c_in, c_out, sems = refs[:n_in], refs[n_in:n_in + n_out], refs[n_in + n_out:]
        ex.start(c_in, c_out, *sems)
        ex.finish(c_in, c_out, *sems)

    return pl.pallas_call(
        body, name=name, in_specs=[ANY] * n_in, out_specs=[ANY] * n_out, out_shape=list(ex.outs),
        input_output_aliases=dict(ex.aliases),
        scratch_shapes=[pltpu.SemaphoreType.DMA(s) for s in ex.sems],
    )(*ex.ins)


def _row_tile(rows):
    return max(t for t in range(16, min(rows, 512) + 1, 16) if rows % t == 0)


def _halves_exchange(grads):
    nb = len(grads)

    def copies(ins, outs, send_sems, recv_sems):
        x, y, c = _position()
        return [_remote(ins[i].at[:, 1 - c], outs[i], send_sems.at[i], recv_sems.at[i], (x, y, 1 - c)) for i in range(nb)]

    def start(ins, outs, send_sems, recv_sems):
        for cp in copies(ins, outs, send_sems, recv_sems):
            cp.start()

    def finish(ins, outs, send_sems, recv_sems):
        for cp in copies(ins, outs, send_sems, recv_sems):
            cp.wait()

    return _Exchange(ins=list(grads), outs=[jax.ShapeDtypeStruct((N_CHIPS, g.shape[2], g.shape[3]), F32) for g in grads],
                     aliases={}, sems=[(nb,), (nb,)], start=start, finish=finish)


def _join_exchanges(parts):
    assert all(not ex.aliases for ex in parts)

    def split(refs, counts):
        out, at = [], 0
        for k in counts:
            out.append(refs[at:at + k])
            at += k
        return out

    def run(which):
        def go(ins, outs, *sems):
            for ex, i, o, s in zip(parts, split(ins, [len(ex.ins) for ex in parts]), split(outs, [len(ex.outs) for ex in parts]),
                                   split(sems, [len(ex.sems) for ex in parts])):
                getattr(ex, which)(i, o, *s)
        return go

    return _Exchange(ins=[a for ex in parts for a in ex.ins], outs=[a for ex in parts for a in ex.outs], aliases={},
                     sems=[s for ex in parts for s in ex.sems], start=run("start"), finish=run("finish"))


def _pair_sum(g4, got, c_arr, *, name):
    _, _, half, cols = g4.shape
    tr = _row_tile(half)

    def body(c_ref, g_ref, t_ref, p_ref, pb_ref):
        sm = g_ref[...] + t_ref[...]
        p_ref[...] = sm
        pb_ref[...] = sm.astype(BF16)

    blk = pl.BlockSpec((None, tr, cols), lambda j, i, c_ref: (j, i, 0))
    grid_spec = pltpu.PrefetchScalarGridSpec(
        num_scalar_prefetch=1, grid=(N_CHIPS, half // tr),
        in_specs=[pl.BlockSpec((None, None, tr, cols), lambda j, i, c_ref: (j, c_ref[0], i, 0)), blk],
        out_specs=[blk, blk])
    return pl.pallas_call(
        body, name=name, grid_spec=grid_spec,
        out_shape=[jax.ShapeDtypeStruct((N_CHIPS, half, cols), F32), jax.ShapeDtypeStruct((N_CHIPS, half, cols), BF16)],
        compiler_params=_params(("parallel", "parallel")),
    )(c_arr, g4, got)


def _scatter_exchange(parts):
    nb = len(parts)

    def copies(ins, outs, send_sems, recv_sems):
        x, y, c = _position()
        return [_remote(ins[i].at[2 * px + py], outs[i].at[j], send_sems.at[i, j], recv_sems.at[i, j], (px, py, c))
                for i in range(nb) for j, (px, py) in enumerate(_other_chips(x, y))]

    def start(ins, outs, send_sems, recv_sems):
        for cp in copies(ins, outs, send_sems, recv_sems):
            cp.start()

    def finish(ins, outs, send_sems, recv_sems):
        for cp in copies(ins, outs, send_sems, recv_sems):
            cp.wait()

    return _Exchange(ins=list(parts), outs=[jax.ShapeDtypeStruct((3,) + p.shape[1:], p.dtype) for p in parts],
                     aliases={}, sems=[(nb, 3), (nb, 3)], start=start, finish=finish)


def _owner_sum(p, got, chip_arr, c_arr, *, replicated, name):
    _, half, cols = p.shape
    tr = _row_tile(half)

    def body(chip_ref, c_ref, p_ref, r_ref, o_ref):
        o_ref[...] = ((p_ref[...] + r_ref[0].astype(F32)) + r_ref[1].astype(F32)) + r_ref[2].astype(F32)

    if replicated:
        out_spec = pl.BlockSpec((None, None, tr, cols), lambda i, chip_ref, c_ref: (chip_ref[0], c_ref[0], i, 0))
        out_shape = jax.ShapeDtypeStruct((N_CHIPS, 2, half, cols), F32)
    else:
        out_spec = pl.BlockSpec((None, tr, cols), lambda i, chip_ref, c_ref: (c_ref[0], i, 0))
        out_shape = jax.ShapeDtypeStruct((2, half, cols), F32)
    grid_spec = pltpu.PrefetchScalarGridSpec(
        num_scalar_prefetch=2, grid=(half // tr,),
        in_specs=[pl.BlockSpec((None, tr, cols), lambda i, chip_ref, c_ref: (chip_ref[0], i, 0)),
                  pl.BlockSpec((3, tr, cols), lambda i, chip_ref, c_ref: (0, i, 0))],
        out_specs=out_spec)
    return pl.pallas_call(
        body, name=name, grid_spec=grid_spec, out_shape=out_shape,
        compiler_params=_params(("parallel",)),
    )(chip_arr, c_arr, p, got)


def _share_reduced(bufs):
    nb = len(bufs) - 1

    def body(*refs):
        outs = refs[nb + 1:2 * nb + 2]
        send_sems, recv_sems = refs[2 * nb + 2:]
        x, y, c = _position()
        chip = 2 * x + y
        sends = []
        for i in range(nb):
            cp = _remote(outs[i].at[c], outs[i].at[c], send_sems.at[i], recv_sems.at[i], (x, y, 1 - c))
            cp.start()
            sends.append(cp)
        small = outs[nb]
        peers = [(fx, fy, fc) for fx in (0, 1) for fy in (0, 1) for fc in (0, 1) if fx + fy + fc > 0]
        for k, (fx, fy, fc) in enumerate(peers):
            dev = (x ^ fx, y ^ fy, c ^ fc)
            cp = _remote(small.at[chip, c], small.at[chip, c], send_sems.at[nb + k], recv_sems.at[nb + k], dev)
            cp.start()
            sends.append(cp)
        for i in range(nb):
            dst = outs[i].at[1 - c]
            _remote(dst, dst, send_sems.at[i], recv_sems.at[i], (x, y, 1 - c)).wait_recv()
        for k, (fx, fy, fc) in enumerate(peers):
            dst = small.at[2 * (x ^ fx) + (y ^ fy), c ^ fc]
            _remote(dst, dst, send_sems.at[nb + k], recv_sems.at[nb + k], (x ^ fx, y ^ fy, c ^ fc)).wait_recv()
        for cp in sends:
            cp.wait_send()

    n_all = nb + 1
    return pl.pallas_call(
        body, name="grad_share_reduced", in_specs=[ANY] * n_all, out_specs=[ANY] * n_all,
        out_shape=[jax.ShapeDtypeStruct(b.shape, b.dtype) for b in bufs],
        input_output_aliases={i: i for i in range(n_all)},
        scratch_shapes=[pltpu.SemaphoreType.DMA((nb + 7,)), pltpu.SemaphoreType.DMA((nb + 7,))],
    )(*bufs)


class _GradReducer:
    def __init__(self, c_arr, chip_arr):
        self.c_arr, self.chip_arr = c_arr, chip_arr
        self.full, self.pairs, self.landed = {}, {}, {}

    def swap(self, names, grads):
        for n, g in zip(names, grads):
            self.full[n] = g.reshape(N_CHIPS, 2, g.shape[0] // (2 * N_CHIPS), g.shape[1])
        return _halves_exchange([self.full[n] for n in names])

    def swapped(self, names, bufs):
        for n, t in zip(names, bufs):
            self.pairs[n] = _pair_sum(self.full[n], t, self.c_arr, name="grad_pair_sum_" + n)

    def scatter(self, names):
        return _scatter_exchange([self.pairs[n][1] for n in names])

    def collect(self, names, bufs):
        self.landed.update(zip(names, bufs))

    def swap_now(self, names, grads):
        self.swapped(names, _run_exchange(self.swap(names, grads), name="grad_exchange_" + names[0]))

    def finish(self, names, grads, order):
        self.swap_now(names, grads)
        self.collect(names, _run_exchange(self.scatter(names), name="grad_scatter_" + names[0]))
        totals = [_owner_sum(self.pairs[n][0], self.landed[n], self.chip_arr, self.c_arr, replicated=(n == order[-1]),
                             name="grad_owner_sum_" + n) for n in order]
        return _share_reduced(totals)


def _pack_small(vals):
    flat = jnp.concatenate([vals[name].reshape(-1) for name, _ in SMALL])
    return jnp.pad(flat, (0, N_CHIPS * SMALL_ROWS * 1024 - SMALL_ELEMS)).reshape(N_CHIPS * SMALL_ROWS, 1024)


def _unpack_small(buf):
    flat = buf.reshape(-1)
    out, off = {}, 0
    for name, shape in SMALL:
        n = int(np.prod(shape))
        out[name] = flat[off:off + n].reshape(shape)
        off += n
    return out


EARLY_REDUCED = (("w_down",), ("w_up",), ("w_o", "w_ssm_br", "w_attn_br", "w_mem_br", "w_glu", "w_mem_kv"), ("w_in",))


def _device_step(x, mem, tgt, w, p, *, shards, reducer):
    rows = x.shape[0]
    w = dict(w)
    early = EARLY_REDUCED
    gb = {}
    gather_pending = shards is not None

    def riding(*stages):
        if reducer is None or not stages:
            return None
        return _join_exchanges([reducer.swap(names, [gb[n] for n in names]) if kind == "swap" else reducer.scatter(names)
                                for kind, names in stages])

    def arrived(stages, res):
        if reducer is None or not stages:
            return res
        main, bufs = res
        for kind, names in stages:
            (reducer.swapped if kind == "swap" else reducer.collect)(names, bufs[:len(names)])
            bufs = bufs[len(names):]
        return main

    def fetching(names):
        return _gather_exchange([shards[n] for n in names]) if gather_pending else None

    def fetched(names, res):
        if not gather_pending:
            return res
        w.update(zip(names, res[1]))
        return res[0]

    first_use = (("w_glu", "w_ssm_br", "w_attn_br", "w_mem_kv", "w_mem_br", "w_o", "w_up"), ("w_down",))
    g1, gm, g2 = p["norm1_g"], p["mem_norm_g"], p["norm2_g"]
    gf = p["final_g"].reshape(1, D_MODEL)
    ssm_args = (p["ssm_lambda_re"][0], p["ssm_lambda_im"][0], p["ssm_log_dt"][0], p["ssm_b_re"][0],
                p["ssm_b_im"][0], p["ssm_c_re"][0], p["ssm_c_im"][0])
    (a_lay, b_blk, c_blk), ssm_vjp = jax.vjp(_ssm_matrices, *ssm_args)
    a_conj = a_lay * _to_scan_layout(jnp.stack([jnp.ones((N_STATES,), F32), -jnp.ones((N_STATES,), F32)]))[None, :]
    dd = p["ssm_d"].reshape(1, SSM_WIDTH)
    win_t = w["w_in"]
    mm = _matmul

    n1 = _rmsnorm_fwd(x, g1, tm=512, name="norm1")
    splits = ((OFF_U, OFF_QKV - OFF_U), (OFF_QKV, OFF_MQ - OFF_QKV), (OFF_MQ, OFF_ZG - OFF_MQ), (OFF_ZG, IN_WIDTH - OFF_ZG))
    u, qkv, mq, zg = fetched(first_use[0], _split_matmul(n1, win_t, splits, tm=512, carry=fetching(first_use[0]),
                                                         vmem=VMEM_LIMIT_WIDE_BYTES, name="in_proj"))

    u_i = _interleave(u)
    ends = _ssm_ends(a_lay, u_i, b_blk, transpose=False, reverse=False, tt=512, name="ssm_fwd_ends")
    s, ys_i, s_entry = _ssm_fwd(a_lay, u_i, b_blk, c_blk, ends, tt=512, name="ssm_fwd")
    ys = _deinterleave(ys_i)
    y0, tglu, y2 = _glu_fwd(ys, u, dd, w["w_glu"], p["b_glu"], tm=512, name="glu_fwd")

    outs, lses = [], []
    for g, (_, d) in enumerate(ATTN_PATTERNS):
        o_g, lse_g = _attn_fwd(qkv, g, d, name=f"attn_fwd_{g}")
        outs.append(o_g)
        lses.append(lse_g)
    o, lse = _attn_merge(outs, lses, tm=1024, name="attn_merge")

    mn = _rmsnorm_fwd(mem, gm, tm=MEM_LEN, name="mem_norm")
    kv = mm(mn, w["w_mem_kv"], m=MEM_LEN, n=1024, k=1024, tm=MEM_LEN, tn=1024, tk=1024, out_dtypes=(F32,), name="mem_kv")
    mo = _mem_attn_fwd(mq, kv, tq=1024, name="mem_attn_fwd")

    branch_acts = (y2, o, mo)
    branch_wts = (w["w_ssm_br"], w["w_attn_br"], w["w_mem_br"])
    merged = _branch_merge_fwd(branch_acts, branch_wts, zg, p["b_gate"], tm=256, name="branch_merge_fwd")
    h1, n2 = mm(merged, w["w_o"], m=rows, n=1024, k=1024, tm=1024, tn=1024, tk=1024, out_dtypes=(F32, BF16),
                aux=((x, "mn"), (g2, "row")), epilogue=_residual_norm_epilogue, name="out_proj")
    relu2 = lambda acc: (jnp.square(jnp.maximum(acc, 0.0)),)
    act = fetched(first_use[1], _sum_matmul([n2], w["w_up"], [0], tb=True, tm=512, out_dtype=BF16, epilogue=relu2,
                                            carry=fetching(first_use[1]), name="mlp_up"))
    dh2, d_gf, sq_err = _sum_matmul([act], w["w_down"], [0], tm=512, aux=((h1, "mn"), (tgt, "mn"), (gf, "row")),
                                    epilogue=_loss_head_epilogue, n_sums=2, name="mlp_down")
    loss = (0.5 / D_MODEL) * jnp.sum(sq_err)

    gs = {"final_g": d_gf.reshape(D_MODEL)}
    drelu2 = lambda acc, actv: (acc * (2.0 * jnp.sqrt(actv.astype(F32))),)
    dup = _sum_matmul([dh2], w["w_down"], [0], tb=True, tm=512, out_dtype=BF16, aux=((act, "mn"),), epilogue=drelu2,
                      name="d_act")
    gb["w_down"] = mm(act, dh2, m=D_FF, n=1024, k=rows, ta=True, tm=1024, tn=1024, tk=2048, out_dtypes=(F32,), name="dw_down")
    stages = (("swap", early[0]),)
    gb["w_up"] = arrived(stages, mm(dup, n2, m=D_FF, n=1024, k=rows, ta=True, tm=1024, tn=1024, tk=2048,
                                    out_dtypes=(F32,), carry=riding(*stages), name="dw_up"))
    stages = (("scatter", early[0]), ("swap", early[1]))
    dh1, gs["norm2_g"] = arrived(stages, _sum_matmul([dup], w["w_up"], [0], tm=512, aux=((h1, "mn"), (dh2, "mn"), (g2, "row")),
                                                     epilogue=_rmsnorm_bwd_epilogue, n_sums=1, carry=riding(*stages), name="d_n2"))
    dmerged = mm(dh1, w["w_o"], m=rows, n=1024, k=1024, tb=True, tm=1024, tn=1024, tk=1024, out_dtypes=(F32,), name="d_merged")
    gb["w_o"] = mm(merged, dh1, m=1024, n=1024, k=rows, ta=True, tm=1024, tn=1024, tk=2048, out_dtypes=(F32,), name="dw_o")
    stages = (("scatter", early[1]),)
    (dy2, do, dmo, gb["w_ssm_br"], gb["w_attn_br"], gb["w_mem_br"], dzg, gs["b_gate"]) = arrived(stages, _branch_merge_bwd(
        dmerged, branch_acts, branch_wts, zg, p["b_gate"], tm=256, carry=riding(*stages), name="branch_merge_bwd"))

    dy0, dt, y1, gs["b_glu"], d_dd = _glu_bwd(dy2, y0, tglu, u, w["w_glu"], tm=512, name="glu_bwd")
    gs["ssm_d"] = d_dd.reshape(1, SSM_GROUPS, SSM_GROUP_SIZE)
    gb["w_glu"] = mm(y1, dt, m=512, n=512, k=rows, ta=True, tm=512, tn=512, tk=1024, out_dtypes=(F32,), name="dw_glu")
    dy0_i = _interleave(dy0)
    lam_ends = _ssm_ends(a_conj, dy0_i, c_blk, transpose=True, reverse=True, tt=512, name="ssm_bwd_ends")
    du_i, d_b_blk, d_c_blk, d_a_lay = _ssm_bwd(a_conj, dy0_i, u_i, s, s_entry, b_blk, c_blk, dd, lam_ends, tt=512,
                                                name="ssm_bwd")
    du = _deinterleave(du_i)
    d_ssm = ssm_vjp((d_a_lay, d_b_blk, d_c_blk))
    for name, val in zip(("ssm_lambda_re", "ssm_lambda_im", "ssm_log_dt", "ssm_b_re", "ssm_b_im", "ssm_c_re", "ssm_c_im"), d_ssm):
        gs[name] = val[None]

    dqkv = None
    for g, (_, d) in enumerate(ATTN_PATTERNS):
        dqkv = _attn_bwd(qkv, do, o, lse, g, d, dqkv, name=f"attn_bwd_{g}")

    dmq, dmk, dmv = _mem_attn_bwd(mq, kv, dmo, tq=1024, name="mem_attn_bwd")
    dkv = jnp.concatenate([dmk, dmv], axis=1)
    gb["w_mem_kv"] = mm(mn, dkv, m=1024, n=1024, k=MEM_LEN, ta=True, tm=1024, tn=1024, tk=MEM_LEN, out_dtypes=(F32,), name="dw_mem_kv")
    dmn = mm(dkv, w["w_mem_kv"], m=MEM_LEN, n=1024, k=1024, tb=True, tm=MEM_LEN, tn=1024, tk=1024, out_dtypes=(F32,), name="d_mn")
    _, gs["mem_norm_g"] = _rmsnorm_bwd(mem, gm, dmn, None, tm=MEM_LEN, name="mem_norm_bwd")

    pieces = ((du, OFF_U, "u"), (dqkv[0], OFF_QKV, "q"), (dqkv[1], OFF_QKV + 768, "k"), (dqkv[2], OFF_QKV + 1536, "v"),
              (dmq, OFF_MQ, "mq"), (dzg, OFF_ZG, "zg"))
    dw_rows = []
    for piece, off, tag in pieces:
        width = piece.shape[1]
        tmw = 1024 if width % 1024 == 0 else (768 if width == 768 else 512)
        stages = {"q": (("swap", early[2]),), "zg": (("scatter", early[2]),)}.get(tag, ())
        dw_rows.append(arrived(stages, mm(piece, n1, m=width, n=1024, k=rows, ta=True, tm=tmw, tn=1024, tk=2048,
                                          out_dtypes=(F32,), carry=riding(*stages), name="dw_in_" + tag)))
    gb["w_in"] = jnp.concatenate(dw_rows, axis=0)
    if reducer is not None:
        reducer.swap_now(early[3], [gb["w_in"]])
    stages = (("scatter", early[3]),)
    dx, gs["norm1_g"] = arrived(stages, _sum_matmul(
        [piece for piece, _, _ in pieces], win_t, [off for _, off, _ in pieces], tm=512,
        aux=((x, "mn"), (dh1, "mn"), (g1, "row")), epilogue=_rmsnorm_bwd_epilogue, n_sums=1,
        carry=riding(*stages), vmem=VMEM_LIMIT_WIDE_BYTES, name="d_n1"))
    return loss, dx, gb, gs


def kernel(x, mem, norm1_g, mem_norm_g, w_in, b_gate, ssm_lambda_re, ssm_lambda_im, ssm_log_dt, ssm_b_re, ssm_b_im, ssm_c_re, ssm_c_im, ssm_d, w_glu, b_glu, w_ssm_br, w_attn_br, w_mem_kv, w_mem_br, w_o, norm2_g, w_up, w_down, final_g, loss_target, m_norm1_g, m_mem_norm_g, m_w_in, m_b_gate, m_ssm_lambda_re, m_ssm_lambda_im, m_ssm_log_dt, m_ssm_b_re, m_ssm_b_im, m_ssm_c_re, m_ssm_c_im, m_ssm_d, m_w_glu, m_b_glu, m_w_ssm_br, m_w_attn_br, m_w_mem_kv, m_w_mem_br, m_w_o, m_norm2_g, m_w_up, m_w_down, m_final_g, v_norm1_g, v_mem_norm_g, v_w_in, v_b_gate, v_ssm_lambda_re, v_ssm_lambda_im, v_ssm_log_dt, v_ssm_b_re, v_ssm_b_im, v_ssm_c_re, v_ssm_c_im, v_ssm_d, v_w_glu, v_b_glu, v_w_ssm_br, v_w_attn_br, v_w_mem_kv, v_w_mem_br, v_w_o, v_norm2_g, v_w_up, v_w_down, v_final_g):
    env = dict(locals())
    weights = {n: env[n] for n in WEIGHT_ORDER}
    moms = {n: env["m_" + n] for n in WEIGHT_ORDER}
    vels = {n: env["v_" + n] for n in WEIGHT_ORDER}
    def shard2d(a):
        return a.reshape(a.shape[-2], a.shape[-1])

    chip = 2 * lax.axis_index("x") + lax.axis_index("y")
    wire = [shard2d(weights[n]).astype(BF16) for n, _, _ in BIG]
    wire = dict(zip([n for n, _, _ in BIG], [s.T if tr else s for s, (_, tr, _) in zip(wire, BIG)]))
    w_in_full = _run_exchange(_gather_exchange([wire.pop("w_in")]), name="all_gather_w_in")[0]
    small = {n: weights[n] for n, _ in SMALL}

    reducer = _GradReducer(lax.axis_index("c").astype(jnp.int32).reshape(1), chip.astype(jnp.int32).reshape(1))
    loss, dx, gb, gs = _device_step(x[0], mem[0], loss_target[0], {"w_in": w_in_full}, small, shards=wire, reducer=reducer)
    *shards, small_grad = reducer.finish(["small"], [_pack_small(gs)], [n for n, _, _ in BIG] + ["small"])
    grads = {}
    for (n, tr, _), sh in zip(BIG, shards):
        sh = sh.reshape(2 * sh.shape[1], sh.shape[2])
        grads[n] = sh.T if tr else sh
    small_grad = small_grad.reshape(N_CHIPS * SMALL_ROWS, 1024)
    grads_small = _unpack_small(small_grad)

    delta, new_m, new_v = {}, {}, {}
    for n, _, _ in BIG:
        grads[n], delta[n], new_m[n], new_v[n] = _adamw(weights[n], grads[n], moms[n], vels[n],
                                                        tr=min(weights[n].shape[-2], 256), name="adamw_" + n)
    _, ds_, ms_, vs_ = _adamw(_pack_small(small), small_grad,
                              _pack_small({n: moms[n] for n, _ in SMALL}), _pack_small({n: vels[n] for n, _ in SMALL}),
                              tr=N_CHIPS * SMALL_ROWS, name="adamw_small")
    for dst, buf in ((delta, ds_), (new_m, ms_), (new_v, vs_)):
        dst.update(_unpack_small(buf))
    grads.update(grads_small)

    total_loss = lax.psum(loss, ("x", "y", "c"))
    return (total_loss, dx[None], *[grads[n] for n in WEIGHT_ORDER], *[delta[n] for n in WEIGHT_ORDER],
            *[new_m[n] for n in WEIGHT_ORDER], *[new_v[n] for n in WEIGHT_ORDER])
```

```python
import functools
import math

import numpy as np
import jax
import jax.numpy as jnp
from jax import lax
from jax.experimental import pallas as pl
from jax.experimental.pallas import tpu as pltpu

F32 = jnp.float32
BF16 = jnp.bfloat16

D_MODEL = 1024
SSM_GROUPS = 32
SSM_GROUP_SIZE = 16
SSM_STATE = 64
SSM_WIDTH = 512
N_STATES = SSM_GROUPS * SSM_STATE
SCAN_CB = 1024
ATTN_PATTERNS = ((128, 1), (512, 4), (2048, 16))
ATTN_HEAD_DIM = 64
ATTN_Q = 128
MEM_LEN = 256
MEM_HEAD_DIM = 128
MEM_HEADS = 4
D_FF = 4096
OFF_U, OFF_QKV, OFF_MQ, OFF_ZG = 0, 512, 2816, 3328
IN_WIDTH = 6400
RMS_EPS = 1e-6
NEG_INF = -1e30
ADAM_LR, ADAM_B1, ADAM_B2, ADAM_EPS, ADAM_WD, ADAM_STEP = 0.001, 0.9, 0.999, 1e-08, 0.01, 10

VMEM_LIMIT_BYTES = 48 * 1024 * 1024
VMEM_LIMIT_WIDE_BYTES = 56 * 1024 * 1024
LANES = 128
MESH = pl.DeviceIdType.MESH
N_CHIPS = 4

SCAN_SEGS = 8
SCAN_GROUPS = SCAN_CB // SSM_STATE

BIG = (("w_in", True, (6400, 1024)), ("w_glu", False, (512, 512)), ("w_ssm_br", True, (1024, 512)),
       ("w_attn_br", True, (1024, 256)), ("w_mem_kv", False, (1024, 1024)), ("w_mem_br", True, (1024, 512)),
       ("w_o", False, (1024, 1024)), ("w_up", True, (4096, 1024)), ("w_down", False, (4096, 1024)))
SMALL = (("norm1_g", (1, 1024)), ("mem_norm_g", (1, 1024)), ("b_gate", (1, 3072)),
         ("ssm_lambda_re", (1, 32, 64)), ("ssm_lambda_im", (1, 32, 64)), ("ssm_log_dt", (1, 32)),
         ("ssm_b_re", (1, 32, 64, 16)), ("ssm_b_im", (1, 32, 64, 16)), ("ssm_c_re", (1, 32, 16, 64)),
         ("ssm_c_im", (1, 32, 16, 64)), ("ssm_d", (1, 32, 16)), ("b_glu", (1, 512)),
         ("norm2_g", (1, 1024)), ("final_g", (1024,)))
WEIGHT_ORDER = ("norm1_g", "mem_norm_g", "w_in", "b_gate", "ssm_lambda_re", "ssm_lambda_im", "ssm_log_dt",
                "ssm_b_re", "ssm_b_im", "ssm_c_re", "ssm_c_im", "ssm_d", "w_glu", "b_glu", "w_ssm_br",
                "w_attn_br", "w_mem_kv", "w_mem_br", "w_o", "norm2_g", "w_up", "w_down", "final_g")
SMALL_ELEMS = sum(int(np.prod(s)) for _, s in SMALL)
SMALL_ROWS = 64


def _params(sem, vmem=VMEM_LIMIT_BYTES):
    return pltpu.CompilerParams(dimension_semantics=sem, vmem_limit_bytes=vmem)


def _sigmoid(v):
    return 0.5 * jnp.tanh(0.5 * v) + 0.5


_GELU_C = math.sqrt(2.0 / math.pi)


def _gelu(v):
    return 0.5 * v * (1.0 + jnp.tanh(_GELU_C * (v + 0.044715 * v * v * v)))


def _gelu_grad(v):
    th = jnp.tanh(_GELU_C * (v + 0.044715 * v * v * v))
    return 0.5 * (1.0 + th) + 0.5 * v * (1.0 - th * th) * _GELU_C * (1.0 + 3.0 * 0.044715 * v * v)


def _dot(a, b, ca, cb):
    return lax.dot_general(a, b, (((ca,), (cb,)), ((), ())), preferred_element_type=F32)


class _Exchange:
    def __init__(self, ins, outs, aliases, sems, start, finish):
        self.ins, self.outs, self.aliases, self.sems, self.start, self.finish = ins, outs, aliases, sems, start, finish


def _matmul(a, b, *, m, n, k, ta=False, tb=False, tm, tn, tk, out_dtypes, name,
            a_off=(0, 0), b_off=(0, 0), b_row0=None, aux=(), epilogue=None, n_sums=0, carry=None):
    assert m % tm == 0 and n % tn == 0 and k % tk == 0, (name, m, n, k, tm, tn, tk)
    nk = k // tk
    n_aux = len(aux)
    n_tiles = len(out_dtypes)
    n_out = n_tiles + n_sums
    ar, ac = a_off
    br, bc = b_off
    if ta:
        a_spec = pl.BlockSpec((tk, tm), lambda i, j, kk: (kk + ar, i + ac))
    else:
        a_spec = pl.BlockSpec((tm, tk), lambda i, j, kk: (i + ar, kk + ac))
    if tb:
        if b_row0 is None:
            b_spec = pl.BlockSpec((tn, tk), lambda i, j, kk: (j + br, kk + bc))
        else:
            assert b_row0 % LANES == 0 and tn % LANES == 0 and tk % LANES == 0
            b_spec = pl.BlockSpec((pl.Element(tn), pl.Element(tk)),
                                  lambda i, j, kk: (pl.multiple_of(b_row0 + j * tn, LANES), pl.multiple_of((kk + bc) * tk, LANES)))
    else:
        b_spec = pl.BlockSpec((tk, tn), lambda i, j, kk: (kk + br, j + bc))
    aux_specs = []
    for _, kind in aux:
        if kind == "mn":
            aux_specs.append(pl.BlockSpec((tm, tn), lambda i, j, kk: (i, j)))
        else:
            aux_specs.append(pl.BlockSpec((1, tn), lambda i, j, kk: (0, j)))
    ca = 0 if ta else 1
    cb = 1 if tb else 0

    def finish(acc, aux_refs, out_refs, row_tile):
        outs = (acc,) if epilogue is None else epilogue(acc, *[r[...] for r in aux_refs])
        for o_ref, o in zip(out_refs[:n_tiles], outs[:n_tiles]):
            o_ref[...] = o.astype(o_ref.dtype)
        _accumulate_over_rows(out_refs[n_tiles:], outs[n_tiles:], row_tile)

    def body(a_ref, b_ref, *rest):
        aux_refs = rest[:n_aux]
        out_refs = rest[n_aux:n_aux + n_out]
        row_tile = pl.program_id(0)
        prod = _dot(a_ref[...].astype(BF16), b_ref[...].astype(BF16), ca, cb)
        if nk == 1:
            finish(prod, aux_refs, out_refs, row_tile)
            return
        acc_ref = rest[n_aux + n_out]
        kk = pl.program_id(2)

        @pl.when(kk == 0)
        def _():
            acc_ref[...] = prod

        @pl.when(jnp.logical_and(kk > 0, kk < nk - 1))
        def _():
            acc_ref[...] += prod

        @pl.when(kk == nk - 1)
        def _():
            finish(acc_ref[...] + prod, aux_refs, out_refs, row_tile)

    tile = pl.BlockSpec((tm, tn), lambda i, j, kk: (i, j))
    col_sum = pl.BlockSpec((1, tn), lambda i, j, kk: (0, j))
    res = _call_with_carry(
        body, carry, name=name, grid=(m // tm, n // tn, nk), in_specs=[a_spec, b_spec] + aux_specs,
        out_specs=[tile] * n_tiles + [col_sum] * n_sums,
        out_shape=[jax.ShapeDtypeStruct((m, n), dt) for dt in out_dtypes] + [jax.ShapeDtypeStruct((1, n), F32)] * n_sums,
        scratch=[pltpu.VMEM((tm, tn), F32)] if nk > 1 else [], operands=[a, b] + [x for x, _ in aux],
        semantics=("arbitrary" if n_sums else "parallel", "parallel", "arbitrary"))
    main = res[0] if n_out == 1 else tuple(res[:n_out])
    return main if carry is None else (main, list(res[n_out:]))


def _accumulate_over_rows(sum_refs, terms, row_tile):
    for s_ref, term in zip(sum_refs, terms):
        @pl.when(row_tile == 0)
        def _():
            s_ref[...] = term

        @pl.when(row_tile > 0)
        def _():
            s_ref[...] += term


def _call_with_carry(body, carry, *, name, grid, in_specs, out_specs, out_shape, scratch, operands, semantics,
                     vmem=VMEM_LIMIT_BYTES):
    if carry is None:
        return pl.pallas_call(body, name=name, grid=grid, in_specs=in_specs, out_specs=out_specs, out_shape=out_shape,
                              scratch_shapes=scratch, compiler_params=_params(semantics, vmem))(*operands)
    n_in, n_cin, n_out, n_cout, n_scr = len(operands), len(carry.ins), len(out_shape), len(carry.outs), len(scratch)

    def hosted(*refs):
        main_in, c_in = refs[:n_in], refs[n_in:n_in + n_cin]
        main_out = refs[n_in + n_cin:n_in + n_cin + n_out]
        c_out = refs[n_in + n_cin + n_out:n_in + n_cin + n_out + n_cout]
        rest = refs[n_in + n_cin + n_out + n_cout:]
        ids = [pl.program_id(t) for t in range(len(grid))]
        first = functools.reduce(jnp.logical_and, [i == 0 for i in ids])
        last = functools.reduce(jnp.logical_and, [i == g - 1 for i, g in zip(ids, grid)])

        @pl.when(first)
        def _():
            carry.start(c_in, c_out, *rest[n_scr:])

        body(*main_in, *main_out, *rest[:n_scr])

        @pl.when(last)
        def _():
            carry.finish(c_in, c_out, *rest[n_scr:])

    return pl.pallas_call(
        hosted, name=name, grid=grid,
        in_specs=list(in_specs) + [ANY] * n_cin, out_specs=list(out_specs) + [ANY] * n_cout,
        out_shape=list(out_shape) + list(carry.outs),
        input_output_aliases={n_in + i: n_out + o for i, o in carry.aliases.items()},
        scratch_shapes=list(scratch) + [pltpu.SemaphoreType.DMA(s) for s in carry.sems],
        compiler_params=_params(("arbitrary",) * len(grid), vmem),
    )(*operands, *carry.ins)


def _sum_matmul(pieces, b, offs, *, tm, name, tb=False, out_dtype=F32, aux=(), epilogue=None, n_sums=0, carry=None,
                vmem=VMEM_LIMIT_BYTES):
    m = pieces[0].shape[0]
    n = b.shape[0] if tb else b.shape[1]
    npieces, n_aux = len(pieces), len(aux)
    assert not tb or npieces == 1

    def body(*refs):
        b_ref = refs[npieces]
        aux_refs = refs[npieces + 1:npieces + 1 + n_aux]
        out_refs = refs[npieces + 1 + n_aux:]
        acc = None
        for p_ref, off in zip(refs[:npieces], offs):
            lhs = p_ref[...].astype(BF16)
            part = _dot(lhs, b_ref[...], 1, 1) if tb else _dot(lhs, b_ref[pl.ds(off, p_ref.shape[1]), :], 1, 0)
            acc = part if acc is None else acc + part
        outs = (acc,) if epilogue is None else epilogue(acc, *[r[...] for r in aux_refs])
        out_refs[0][...] = outs[0].astype(out_dtype)
        _accumulate_over_rows(out_refs[1:], outs[1:], pl.program_id(0))

    row = pl.BlockSpec((tm, n), lambda i: (i, 0))
    vec = pl.BlockSpec((1, n), lambda i: (0, 0))
    res = _call_with_carry(
        body, carry, name=name, grid=(m // tm,),
        in_specs=[pl.BlockSpec((tm, p.shape[1]), lambda i: (i, 0)) for p in pieces] + [_resident(b.shape)]
        + [row if kind == "mn" else vec for _, kind in aux],
        out_specs=[row] + [vec] * n_sums,
        out_shape=[jax.ShapeDtypeStruct((m, n), out_dtype)] + [jax.ShapeDtypeStruct((1, n), F32)] * n_sums,
        scratch=[], operands=list(pieces) + [b] + [x for x, _ in aux], semantics=("arbitrary" if n_sums else "parallel",),
        vmem=vmem)
    main = res[0] if n_sums == 0 else tuple(res[:1 + n_sums])
    return main if carry is None else (main, list(res[1 + n_sums:]))


def _split_matmul(a, b_t, splits, *, tm, name, carry=None, vmem=VMEM_LIMIT_BYTES):
    m, k = a.shape

    def body(a_ref, b_ref, *out_refs):
        av = a_ref[...].astype(BF16)
        for (row0, width), o_ref in zip(splits, out_refs):
            o_ref[...] = _dot(av, b_ref[pl.ds(row0, width), :], 1, 1)

    res = _call_with_carry(
        body, carry, name=name, grid=(m // tm,),
        in_specs=[pl.BlockSpec((tm, k), lambda i: (i, 0)), _resident(b_t.shape)],
        out_specs=[pl.BlockSpec((tm, width), lambda i: (i, 0)) for _, width in splits],
        out_shape=[jax.ShapeDtypeStruct((m, width), F32) for _, width in splits],
        scratch=[], operands=[a, b_t], semantics=("parallel",), vmem=vmem)
    outs = tuple(res[:len(splits)])
    return outs if carry is None else (outs, list(res[len(splits):]))


def _rmsnorm_fwd(x, g, *, tm, name):
    rows, d = x.shape

    def body(x_ref, g_ref, o_ref):
        xv = x_ref[...]
        r = lax.rsqrt(jnp.mean(xv * xv, axis=-1, keepdims=True) + RMS_EPS)
        o_ref[...] = (xv * r * g_ref[...]).astype(o_ref.dtype)

    return pl.pallas_call(
        body, name=name, grid=(rows // tm,),
        in_specs=[pl.BlockSpec((tm, d), lambda i: (i, 0)), pl.BlockSpec((1, d), lambda i: (0, 0))],
        out_specs=pl.BlockSpec((tm, d), lambda i: (i, 0)),
        out_shape=jax.ShapeDtypeStruct((rows, d), BF16),
        compiler_params=_params(("parallel",)),
    )(x, g)


def _residual_norm_epilogue(acc, xv, gv):
    h = acc + xv
    r = lax.rsqrt(jnp.mean(h * h, axis=-1, keepdims=True) + RMS_EPS)
    return h, h * r * gv


def _rmsnorm_bwd_epilogue(dy, xv, resv, gv):
    r = lax.rsqrt(jnp.mean(xv * xv, axis=-1, keepdims=True) + RMS_EPS)
    xhat = xv * r
    dyg = dy * gv
    dx = r * (dyg - xhat * jnp.mean(dyg * xhat, axis=-1, keepdims=True)) + resv
    return dx, jnp.sum(dy * xhat, axis=0, keepdims=True)


def _rmsnorm_bwd(x, g, dy, res, *, tm, name):
    rows, d = x.shape
    has_res = res is not None

    def body(x_ref, g_ref, dy_ref, *rest):
        if has_res:
            res_ref, dx_ref, dg_ref = rest
        else:
            dx_ref, dg_ref = rest
        i = pl.program_id(0)
        xv = x_ref[...]
        r = lax.rsqrt(jnp.mean(xv * xv, axis=-1, keepdims=True) + RMS_EPS)
        xhat = xv * r
        dyv = dy_ref[...]
        dyg = dyv * g_ref[...]
        dx = r * (dyg - xhat * jnp.mean(dyg * xhat, axis=-1, keepdims=True))
        if has_res:
            dx = dx + res_ref[...]
        dx_ref[...] = dx

        @pl.when(i == 0)
        def _():
            dg_ref[...] = jnp.zeros_like(dg_ref)

        dg_ref[...] += jnp.sum(dyv * xhat, axis=0, keepdims=True)

    row_spec = pl.BlockSpec((tm, d), lambda i: (i, 0))
    vec_spec = pl.BlockSpec((1, d), lambda i: (0, 0))
    ins = [x, g, dy] + ([res] if has_res else [])
    return pl.pallas_call(
        body, name=name, grid=(rows // tm,),
        in_specs=[row_spec, vec_spec, row_spec] + ([row_spec] if has_res else []),
        out_specs=[row_spec, vec_spec],
        out_shape=[jax.ShapeDtypeStruct((rows, d), F32), jax.ShapeDtypeStruct((1, d), F32)],
        compiler_params=_params(("arbitrary",)),
    )(*ins)


def _loss_head_epilogue(acc, hv, tgtv, gv):
    xv = acc + hv
    r = lax.rsqrt(jnp.mean(xv * xv, axis=-1, keepdims=True) + RMS_EPS)
    xhat = xv * r
    err = xhat * gv - tgtv
    dyv = err * (1.0 / D_MODEL)
    dyg = dyv * gv
    dh = r * (dyg - xhat * jnp.mean(dyg * xhat, axis=-1, keepdims=True))
    return dh, jnp.sum(dyv * xhat, axis=0, keepdims=True), jnp.sum(err * err, axis=0, keepdims=True)


def _to_scan_layout(v):
    lead = v.shape[:-2]
    v = v.reshape(lead + (2, N_STATES // SCAN_CB, SCAN_CB))
    v = jnp.swapaxes(v, -3, -2)
    return v.reshape(lead + (2 * N_STATES,))


def _ssm_matrices(lam_re, lam_im, log_dt, b_re, b_im, c_re, c_im):
    dt = jnp.exp(log_dt)[:, None]
    mag = jnp.exp(lam_re * dt)
    a_re, a_im = mag * jnp.cos(lam_im * dt), mag * jnp.sin(lam_im * dt)
    nr, ni = a_re - 1.0, a_im
    den = lam_re * lam_re + lam_im * lam_im
    coef_re = (nr * lam_re + ni * lam_im) / den
    coef_im = (ni * lam_re - nr * lam_im) / den
    bb_re = coef_re[..., None] * b_re - coef_im[..., None] * b_im
    bb_im = coef_re[..., None] * b_im + coef_im[..., None] * b_re
    a_lay = _to_scan_layout(jnp.stack([a_re.reshape(-1), a_im.reshape(-1)], axis=0))[None, :]
    nblk = SSM_GROUPS // SCAN_GROUPS
    eye = jnp.eye(SCAN_GROUPS, dtype=F32)

    def b_block(bb):
        bb = bb.reshape(nblk, SCAN_GROUPS, SSM_STATE, SSM_GROUP_SIZE)
        return jnp.einsum("gk,jkph->jghkp", eye, bb).reshape(nblk, SCAN_GROUPS * SSM_GROUP_SIZE, SCAN_CB)

    b_blk = jnp.concatenate([b_block(bb_re), b_block(bb_im)], axis=2)

    def c_block(cc):
        cc = cc.reshape(nblk, SCAN_GROUPS, SSM_GROUP_SIZE, SSM_STATE)
        return jnp.einsum("gk,jghp->jkpgh", eye, cc).reshape(nblk, SCAN_CB, SCAN_GROUPS * SSM_GROUP_SIZE)

    c_blk = jnp.concatenate([c_block(c_re), -c_block(c_im)], axis=1)
    return a_lay, b_blk, c_blk


def _interleave(v):
    rows, c = v.shape
    return v.reshape(SCAN_SEGS, rows // SCAN_SEGS, c).transpose(1, 0, 2).reshape(rows, c)


def _deinterleave(v):
    rows, c = v.shape
    return v.reshape(rows // SCAN_SEGS, SCAN_SEGS, c).transpose(1, 0, 2).reshape(rows, c)


def _scan_groups(a_ref, bu_ref, o_ref, state, *, reverse, tt):
    cb = SCAN_CB
    ar = jnp.broadcast_to(a_ref[:, :cb], (SCAN_SEGS, cb))
    ai = jnp.broadcast_to(a_ref[:, cb:], (SCAN_SEGS, cb))
    ngroups = tt // SCAN_SEGS

    def step(i, st):
        sr, si = st
        r0 = pl.multiple_of(((ngroups - 1 - i) if reverse else i) * SCAN_SEGS, SCAN_SEGS)
        blk = bu_ref[pl.ds(r0, SCAN_SEGS), :]
        nr = ar * sr - ai * si + blk[:, :cb]
        ni = ar * si + ai * sr + blk[:, cb:]
        if o_ref is not None:
            o_ref[pl.ds(r0, SCAN_SEGS), :] = jnp.concatenate([nr, ni], axis=1)
        return nr, ni

    return lax.fori_loop(0, ngroups, step, state, unroll=4)


def _segment_entries(a_ref, e_ref, init_ref, *, reverse, seg_len):
    cb = SCAN_CB
    n_sq = seg_len.bit_length() - 1
    assert 1 << n_sq == seg_len, seg_len
    pr, pi = a_ref[:, :cb], a_ref[:, cb:]
    for _ in range(n_sq):
        pr, pi = pr * pr - pi * pi, 2.0 * pr * pi
    cr = jnp.zeros((1, cb), F32)
    ci = jnp.zeros((1, cb), F32)
    order = range(SCAN_SEGS - 1, -1, -1) if reverse else range(SCAN_SEGS)
    for k, seg in enumerate(order):
        if k > 0:
            prev = seg + 1 if reverse else seg - 1
            er, ei = e_ref[prev:prev + 1, :cb], e_ref[prev:prev + 1, cb:]
            cr, ci = pr * cr - pi * ci + er, pr * ci + pi * cr + ei
        init_ref[seg:seg + 1, :] = jnp.concatenate([cr, ci], axis=1)


def _ssm_specs(nt, tt, nch, reverse):
    cb = SCAN_CB
    tmap = (lambda j, kk: (nt - 1 - kk, j)) if reverse else (lambda j, kk: (kk, j))
    return dict(a=pl.BlockSpec((1, 2 * cb), lambda j, kk: (0, j)),
                seg=pl.BlockSpec((SCAN_SEGS, 2 * cb), lambda j, kk: (0, j)),
                chan=pl.BlockSpec((tt, nch), tmap),
                state=pl.BlockSpec((tt, 2 * cb), tmap),
                b=pl.BlockSpec((None, nch, 2 * cb), lambda j, kk: (j, 0, 0)),
                c=pl.BlockSpec((None, 2 * cb, nch), lambda j, kk: (j, 0, 0)))


def _ssm_ends(a_lay, x, blocks, *, transpose, reverse, tt, name):
    rows = x.shape[0]
    nblk = blocks.shape[0]
    nch = x.shape[1] // nblk
    cb = SCAN_CB
    nt = rows // tt
    sp = _ssm_specs(nt, tt, nch, reverse)

    def body(a_ref, x_ref, w_ref, e_ref, bu_ref):
        kk = pl.program_id(1)

        @pl.when(kk == 0)
        def _():
            e_ref[...] = jnp.zeros_like(e_ref)

        bu_ref[...] = _dot(x_ref[...].astype(BF16), w_ref[...].astype(BF16), 1, 1 if transpose else 0)
        sr, si = _scan_groups(a_ref, bu_ref, None, (e_ref[:, :cb], e_ref[:, cb:]), reverse=reverse, tt=tt)
        e_ref[...] = jnp.concatenate([sr, si], axis=1)

    return pl.pallas_call(
        body, name=name, grid=(nblk, nt),
        in_specs=[sp["a"], sp["chan"], sp["c"] if transpose else sp["b"]],
        out_specs=sp["seg"],
        out_shape=jax.ShapeDtypeStruct((SCAN_SEGS, nblk * 2 * cb), F32),
        scratch_shapes=[pltpu.VMEM((tt, 2 * cb), F32)],
        compiler_params=_params(("parallel", "arbitrary")),
    )(a_lay, x, blocks)


def _ssm_fwd(a_lay, u, b_blk, c_blk, ends, *, tt, name):
    rows = u.shape[0]
    nblk = b_blk.shape[0]
    nch = u.shape[1] // nblk
    cb = SCAN_CB
    nt = rows // tt
    sp = _ssm_specs(nt, tt, nch, False)

    def body(a_ref, e_ref, u_ref, b_ref, c_ref, s_ref, y_ref, init_ref, carry_ref):
        kk = pl.program_id(1)

        @pl.when(kk == 0)
        def _():
            _segment_entries(a_ref, e_ref, init_ref, reverse=False, seg_len=rows // SCAN_SEGS)
            carry_ref[...] = init_ref[...]

        s_ref[...] = _dot(u_ref[...].astype(BF16), b_ref[...].astype(BF16), 1, 0)
        sr, si = _scan_groups(a_ref, s_ref, s_ref, (carry_ref[:, :cb], carry_ref[:, cb:]), reverse=False, tt=tt)
        carry_ref[...] = jnp.concatenate([sr, si], axis=1)
        y_ref[...] = _dot(s_ref[...].astype(BF16), c_ref[...].astype(BF16), 1, 0)

    return pl.pallas_call(
        body, name=name, grid=(nblk, nt),
        in_specs=[sp["a"], sp["seg"], sp["chan"], sp["b"], sp["c"]],
        out_specs=[sp["state"], sp["chan"], sp["seg"]],
        out_shape=[jax.ShapeDtypeStruct((rows, nblk * 2 * cb), F32), jax.ShapeDtypeStruct((rows, nblk * nch), F32),
                   jax.ShapeDtypeStruct((SCAN_SEGS, nblk * 2 * cb), F32)],
        scratch_shapes=[pltpu.VMEM((SCAN_SEGS, 2 * cb), F32)],
        compiler_params=_params(("parallel", "arbitrary")),
    )(a_lay, ends, u, b_blk, c_blk)


def _ssm_bwd(a_conj, dy, u, s, s_entry, b_blk, c_blk, dd, ends, *, tt, name):
    rows = u.shape[0]
    nblk = b_blk.shape[0]
    nch = u.shape[1] // nblk
    cb = SCAN_CB
    nt = rows // tt
    sp = _ssm_specs(nt, tt, nch, True)
    groups_per_tile = tt // SCAN_SEGS
    before = pl.BlockSpec((SCAN_SEGS, 2 * cb), lambda j, kk: (jnp.maximum((nt - 1 - kk) * groups_per_tile - 1, 0), j))

    def body(a_ref, e_ref, dy_ref, u_ref, s_ref, before_ref, entry_ref, b_ref, c_ref, dd_ref,
             du_ref, db_ref, dc_ref, da_ref, lam_ref, carry_ref):
        kk = pl.program_id(1)

        @pl.when(kk == 0)
        def _():
            _segment_entries(a_ref, e_ref, carry_ref, reverse=True, seg_len=rows // SCAN_SEGS)
            db_ref[...] = jnp.zeros_like(db_ref)
            dc_ref[...] = jnp.zeros_like(dc_ref)
            da_ref[...] = jnp.zeros_like(da_ref)

        dyv = dy_ref[...]
        dyb = dyv.astype(BF16)
        lam_ref[...] = _dot(dyb, c_ref[...].astype(BF16), 1, 1)
        lr, li = _scan_groups(a_ref, lam_ref, lam_ref, (carry_ref[:, :cb], carry_ref[:, cb:]), reverse=True, tt=tt)
        carry_ref[...] = jnp.concatenate([lr, li], axis=1)

        first = jnp.where(kk == nt - 1, entry_ref[...], before_ref[...])
        rest = tt - SCAN_SEGS
        lam_hi = lam_ref[pl.ds(SCAN_SEGS, rest), :]
        s_lo = s_ref[pl.ds(0, rest), :]
        lam_lo = lam_ref[pl.ds(0, SCAN_SEGS), :]

        def pair(lv, pv):
            lre, lim, pre, pim = lv[:, :cb], lv[:, cb:], pv[:, :cb], pv[:, cb:]
            return (jnp.sum(lre * pre + lim * pim, axis=0, keepdims=True),
                    jnp.sum(lim * pre - lre * pim, axis=0, keepdims=True))

        r1, i1 = pair(lam_hi, s_lo)
        r0, i0 = pair(lam_lo, first)
        da_ref[...] += jnp.concatenate([r1 + r0, i1 + i0], axis=1)

        lamb = lam_ref[...].astype(BF16)
        du_ref[...] = _dot(lamb, b_ref[...].astype(BF16), 1, 1) + dd_ref[...] * dyv
        db_ref[...] += _dot(u_ref[...].astype(BF16), lamb, 0, 0)
        dc_ref[...] += _dot(s_ref[...].astype(BF16), dyb, 0, 0)

    return pl.pallas_call(
        body, name=name, grid=(nblk, nt),
        in_specs=[sp["a"], sp["seg"], sp["chan"], sp["chan"], sp["state"], before, sp["seg"], sp["b"], sp["c"],
                  pl.BlockSpec((1, nch), lambda j, kk: (0, j))],
        out_specs=[sp["chan"], sp["b"], sp["c"], pl.BlockSpec((1, 2 * cb), lambda j, kk: (0, j))],
        out_shape=[jax.ShapeDtypeStruct((rows, nblk * nch), F32), jax.ShapeDtypeStruct(b_blk.shape, F32),
                   jax.ShapeDtypeStruct(c_blk.shape, F32), jax.ShapeDtypeStruct((1, nblk * 2 * cb), F32)],
        scratch_shapes=[pltpu.VMEM((tt, 2 * cb), F32), pltpu.VMEM((SCAN_SEGS, 2 * cb), F32)],
        compiler_params=_params(("parallel", "arbitrary")),
    )(a_conj, ends, dy, u, s, s, s_entry, b_blk, c_blk, dd)


def _glu_fwd(ys, u, dd, w_glu, b_glu, *, tm, name):
    rows, w = ys.shape

    def body(ys_ref, u_ref, dd_ref, w_ref, b_ref, y0_ref, t_ref, y2_ref):
        y0 = ys_ref[...] + dd_ref[...] * u_ref[...]
        y1 = _gelu(y0)
        t = _dot(y1.astype(BF16), w_ref[...], 1, 0) + b_ref[...]
        y0_ref[...] = y0
        t_ref[...] = t
        y2_ref[...] = (y1 * _sigmoid(t)).astype(BF16)

    row = pl.BlockSpec((tm, w), lambda i: (i, 0))
    vec = pl.BlockSpec((1, w), lambda i: (0, 0))
    return pl.pallas_call(
        body, name=name, grid=(rows // tm,),
        in_specs=[row, row, vec, pl.BlockSpec((w, w), lambda i: (0, 0)), vec],
        out_specs=[row, row, row],
        out_shape=[jax.ShapeDtypeStruct((rows, w), F32), jax.ShapeDtypeStruct((rows, w), F32),
                   jax.ShapeDtypeStruct((rows, w), BF16)],
        compiler_params=_params(("parallel",)),
    )(ys, u, dd, w_glu, b_glu)


def _glu_bwd(dy2, y0, t, u, w_glu, *, tm, name):
    rows, w = y0.shape

    def body(dy2_ref, y0_ref, t_ref, u_ref, w_ref, dy0_ref, dt_ref, y1_ref, db_ref, dd_ref):
        i = pl.program_id(0)
        y0 = y0_ref[...]
        y1 = _gelu(y0)
        sg = _sigmoid(t_ref[...])
        dy2v = dy2_ref[...]
        dt = dy2v * y1 * sg * (1.0 - sg)
        dy1 = dy2v * sg + _dot(dt.astype(BF16), w_ref[...], 1, 1)
        dy0 = dy1 * _gelu_grad(y0)
        dy0_ref[...] = dy0
        dt_ref[...] = dt.astype(BF16)
        y1_ref[...] = y1.astype(BF16)

        @pl.when(i == 0)
        def _():
            db_ref[...] = jnp.zeros_like(db_ref)
            dd_ref[...] = jnp.zeros_like(dd_ref)

        db_ref[...] += jnp.sum(dt, axis=0, keepdims=True)
        dd_ref[...] += jnp.sum(dy0 * u_ref[...], axis=0, keepdims=True)

    row = pl.BlockSpec((tm, w), lambda i: (i, 0))
    vec = pl.BlockSpec((1, w), lambda i: (0, 0))
    return pl.pallas_call(
        body, name=name, grid=(rows // tm,),
        in_specs=[row, row, row, row, pl.BlockSpec((w, w), lambda i: (0, 0))],
        out_specs=[row, row, row, vec, vec],
        out_shape=[jax.ShapeDtypeStruct((rows, w), F32), jax.ShapeDtypeStruct((rows, w), BF16),
                   jax.ShapeDtypeStruct((rows, w), BF16), jax.ShapeDtypeStruct((1, w), F32),
                   jax.ShapeDtypeStruct((1, w), F32)],
        compiler_params=_params(("arbitrary",)),
    )(dy2, y0, t, u, w_glu)


ATTN_TILE = 2048


def _attn_geometry(rows, d):
    sb = ATTN_Q * d
    tr = max(sb, min(ATTN_TILE, rows))
    assert rows % tr == 0 and tr % sb == 0, (rows, d)
    return sb, tr, rows // tr, tr // sb


def _attn_masks():
    qi = lax.broadcasted_iota(jnp.int32, (2 * ATTN_Q, 2 * ATTN_Q), 0) % ATTN_Q
    kj = lax.broadcasted_iota(jnp.int32, (2 * ATTN_Q, 2 * ATTN_Q), 1)
    own_ok = jnp.logical_and(kj >= ATTN_Q, kj - ATTN_Q <= qi)
    prev_ok = jnp.logical_and(kj < ATTN_Q, kj >= qi)
    bias_first = jnp.where(own_ok, 0.0, NEG_INF)
    bias_other = jnp.where(jnp.logical_or(own_ok, prev_ok), 0.0, NEG_INF)
    head0 = lax.broadcasted_iota(jnp.int32, (ATTN_Q, LANES), 1) < ATTN_HEAD_DIM
    return bias_first, bias_other, head0


def _attn_rows(base, n, d):
    return pl.ds(pl.multiple_of(base, ATTN_Q), n) if d == 1 else pl.ds(base, n, stride=d)


def _stack_heads(v, head0):
    return jnp.concatenate([jnp.where(head0, v, 0.0), jnp.where(head0, 0.0, v)], axis=0)


def _unstack_heads(v, head0):
    return jnp.where(head0, v[:ATTN_Q], v[ATTN_Q:])


def _fill_keys(buf, prev_ref, cur_ref, sb):
    buf[pl.ds(0, sb), :] = prev_ref[...]
    buf[pl.ds(sb, cur_ref.shape[0]), :] = cur_ref[...]


def _attn_fwd(qkv, g, d, *, name):
    rows = qkv.shape[0]
    sb, tr, ntiles, nsub = _attn_geometry(rows, d)
    qc, kc, vc = 2 * g, 6 + 2 * g, 12 + 2 * g
    scale = ATTN_HEAD_DIM ** -0.5

    def body(q_ref, kc_ref, kp_ref, vc_ref, vp_ref, o_ref, lse_ref, kbuf, vbuf):
        n = pl.program_id(0)
        _fill_keys(kbuf, kp_ref, kc_ref, sb)
        _fill_keys(vbuf, vp_ref, vc_ref, sb)
        bias_first, bias_other, head0 = _attn_masks()

        def per_block(idx, carry):
            j, r = idx // d, idx % d
            base = j * sb + r
            bias = jnp.where(jnp.logical_and(n == 0, j == 0), bias_first, bias_other)
            qrows = _attn_rows(base, ATTN_Q, d)
            krows = _attn_rows(base, 2 * ATTN_Q, d)
            qs = (_stack_heads(q_ref[qrows, :], head0) * scale).astype(BF16)
            s = _dot(qs, kbuf[krows, :].astype(BF16), 1, 1) + bias
            mx = jnp.max(s, axis=-1, keepdims=True)
            p = jnp.exp(s - mx)
            den = jnp.sum(p, axis=-1, keepdims=True)
            pv = _dot(p.astype(BF16), vbuf[krows, :].astype(BF16), 1, 0) / den
            o_ref[qrows, :] = _unstack_heads(pv, head0)
            lse_ref[qrows, :] = _unstack_heads(jnp.broadcast_to(mx + jnp.log(den), (2 * ATTN_Q, LANES)), head0)
            return carry

        lax.fori_loop(0, nsub * d, per_block, 0, unroll=8)

    def cur(col):
        return pl.BlockSpec((tr, LANES), lambda n, hp: (n, col + hp))

    def prev(col):
        return pl.BlockSpec((sb, LANES), lambda n, hp: (jnp.maximum(n * nsub - 1, 0), col + hp))

    out_spec = pl.BlockSpec((tr, LANES), lambda n, hp: (n, hp))
    return pl.pallas_call(
        body, name=name, grid=(ntiles, 2),
        in_specs=[cur(qc), cur(kc), prev(kc), cur(vc), prev(vc)],
        out_specs=[out_spec, out_spec],
        out_shape=[jax.ShapeDtypeStruct((rows, 2 * LANES), F32), jax.ShapeDtypeStruct((rows, 2 * LANES), F32)],
        scratch_shapes=[pltpu.VMEM((sb + tr, LANES), F32), pltpu.VMEM((sb + tr, LANES), F32)],
        compiler_params=_params(("parallel", "parallel")),
    )(qkv, qkv, qkv, qkv, qkv)


def _attn_merge(outs, lses, *, tm, name):
    rows, w = outs[0].shape

    def body(o0, o1, o2, l0, l1, l2, o_ref, lse_ref):
        a0, a1, a2 = l0[...], l1[...], l2[...]
        mx = jnp.maximum(jnp.maximum(a0, a1), a2)
        e0, e1, e2 = jnp.exp(a0 - mx), jnp.exp(a1 - mx), jnp.exp(a2 - mx)
        den = e0 + e1 + e2
        o_ref[...] = (e0 / den) * o0[...] + (e1 / den) * o1[...] + (e2 / den) * o2[...]
        lse_ref[...] = mx + jnp.log(den)

    row = pl.BlockSpec((tm, w), lambda i: (i, 0))
    return pl.pallas_call(
        body, name=name, grid=(rows // tm,), in_specs=[row] * 6, out_specs=[row, row],
        out_shape=[jax.ShapeDtypeStruct((rows, w), F32), jax.ShapeDtypeStruct((rows, w), F32)],
        compiler_params=_params(("parallel",)),
    )(*outs, *lses)


def _attn_bwd(qkv, do, o, lse, g, d, prev, *, name):
    rows = qkv.shape[0]
    sb, tr, ntiles, nsub = _attn_geometry(rows, d)
    qc, kc, vc = 2 * g, 6 + 2 * g, 12 + 2 * g
    scale = ATTN_HEAD_DIM ** -0.5

    def body(q_ref, kc_ref, kp_ref, vc_ref, vp_ref, do_ref, o_ref, lse_ref, dq_ref, dk_ref, dv_ref,
             kbuf, vbuf, dk_acc, dv_acc):
        n = pl.program_id(1)

        @pl.when(n == 0)
        def _():
            dk_acc[pl.ds(0, tr), :] = jnp.zeros((tr, LANES), F32)
            dv_acc[pl.ds(0, tr), :] = jnp.zeros((tr, LANES), F32)

        @pl.when(n < ntiles)
        def _():
            dk_acc[pl.ds(tr, tr), :] = jnp.zeros((tr, LANES), F32)
            dv_acc[pl.ds(tr, tr), :] = jnp.zeros((tr, LANES), F32)
            _fill_keys(kbuf, kp_ref, kc_ref, sb)
            _fill_keys(vbuf, vp_ref, vc_ref, sb)
            bias_first, bias_other, head0 = _attn_masks()
            lane = lax.broadcasted_iota(jnp.int32, (ATTN_Q, LANES), 1)

            def per_block(idx, carry):
                j, r = idx // d, idx % d
                base = j * sb + r
                bias = jnp.where(jnp.logical_and(n == 0, j == 0), bias_first, bias_other)
                qrows = _attn_rows(base, ATTN_Q, d)
                krows = _attn_rows(base, 2 * ATTN_Q, d)
                arows = _attn_rows(base + (tr - sb), 2 * ATTN_Q, d)
                qs = (_stack_heads(q_ref[qrows, :], head0) * scale).astype(BF16)
                dos = _stack_heads(do_ref[qrows, :], head0)
                dosb = dos.astype(BF16)
                ov = o_ref[qrows, :]
                delta = jnp.sum(dos * jnp.concatenate([ov, ov], axis=0), axis=-1, keepdims=True)
                lsev = lse_ref[qrows, :]
                lse_s = jnp.concatenate(
                    [jnp.sum(jnp.where(lane == h * ATTN_HEAD_DIM, lsev, 0.0), axis=-1, keepdims=True) for h in range(2)], axis=0)
                kb = kbuf[krows, :].astype(BF16)
                vb = vbuf[krows, :].astype(BF16)
                p = jnp.exp(_dot(qs, kb, 1, 1) + bias - lse_s)
                ds = (p * (_dot(dosb, vb, 1, 1) - delta)).astype(BF16)
                dq_ref[qrows, :] = _unstack_heads(_dot(ds, kb, 1, 0), head0) * scale
                dk_acc[arows, :] += _dot(ds, qs, 0, 0)
                dv_acc[arows, :] += _dot(p.astype(BF16), dosb, 0, 0)
                return carry

            lax.fori_loop(0, nsub * d, per_block, 0, unroll=4)

        dk_ref[...] = dk_acc[pl.ds(0, tr), :]
        dv_ref[...] = dv_acc[pl.ds(0, tr), :]
        dk_acc[pl.ds(0, tr), :] = dk_acc[pl.ds(tr, tr), :]
        dv_acc[pl.ds(0, tr), :] = dv_acc[pl.ds(tr, tr), :]

    def cur(n):
        return jnp.minimum(n, ntiles - 1)

    def spec(col, prev):
        if prev:
            return pl.BlockSpec((sb, LANES), lambda hp, n: (jnp.maximum(cur(n) * nsub - 1, 0), col + hp))
        return pl.BlockSpec((tr, LANES), lambda hp, n: (cur(n), col + hp))

    row_spec = pl.BlockSpec((tr, LANES), lambda hp, n: (cur(n), hp))
    dq_out = pl.BlockSpec((tr, LANES), lambda hp, n: (cur(n), 2 * g + hp))
    kv_out = pl.BlockSpec((tr, LANES), lambda hp, n: (jnp.maximum(n - 1, 0), 2 * g + hp))
    shape = jax.ShapeDtypeStruct((rows, len(ATTN_PATTERNS) * 2 * LANES), F32)
    ins = [qkv, qkv, qkv, qkv, qkv, do, o, lse]
    in_specs = [spec(qc, False), spec(kc, False), spec(kc, True), spec(vc, False), spec(vc, True),
                row_spec, row_spec, row_spec]
    aliases = {}
    if prev is not None:
        aliases = {len(ins) + t: t for t in range(3)}
        ins = ins + list(prev)
        in_specs = in_specs + [ANY] * 3
    n_in = len(ins)

    def entry(*refs):
        body(*refs[:8], *refs[n_in:])

    return pl.pallas_call(
        entry, name=name, grid=(2, ntiles + 1),
        in_specs=in_specs,
        out_specs=[dq_out, kv_out, kv_out],
        out_shape=[shape, shape, shape],
        input_output_aliases=aliases,
        scratch_shapes=[pltpu.VMEM((sb + tr, LANES), F32), pltpu.VMEM((sb + tr, LANES), F32),
                        pltpu.VMEM((2 * tr, LANES), F32), pltpu.VMEM((2 * tr, LANES), F32)],
        compiler_params=_params(("parallel", "arbitrary")),
    )(*ins)


def _mem_probs(q, k):
    s = _dot(q.astype(BF16), k.astype(BF16), 1, 1) * (MEM_HEAD_DIM ** -0.5)
    e = jnp.exp(s - jnp.max(s, axis=-1, keepdims=True))
    return e / jnp.sum(e, axis=-1, keepdims=True)


def _mem_attn_fwd(mq, kv, *, tq, name):
    rows = mq.shape[0]

    def body(q_ref, k_ref, v_ref, o_ref):
        p = _mem_probs(q_ref[...], k_ref[...])
        o_ref[...] = _dot(p.astype(BF16), v_ref[...].astype(BF16), 1, 0)

    return pl.pallas_call(
        body, name=name, grid=(rows // tq, MEM_HEADS),
        in_specs=[pl.BlockSpec((tq, LANES), lambda i, h: (i, h)),
                  pl.BlockSpec((MEM_LEN, LANES), lambda i, h: (0, h)),
                  pl.BlockSpec((MEM_LEN, LANES), lambda i, h: (0, MEM_HEADS + h))],
        out_specs=pl.BlockSpec((tq, LANES), lambda i, h: (i, h)),
        out_shape=jax.ShapeDtypeStruct((rows, MEM_HEADS * LANES), F32),
        compiler_params=_params(("parallel", "parallel")),
    )(mq, kv, kv)


def _mem_attn_bwd(mq, kv, dmo, *, tq, name):
    rows = mq.shape[0]
    scale = MEM_HEAD_DIM ** -0.5

    def body(q_ref, k_ref, v_ref, do_ref, dq_ref, dk_ref, dv_ref):
        i = pl.program_id(1)
        qb = q_ref[...].astype(BF16)
        kb = k_ref[...].astype(BF16)
        vb = v_ref[...].astype(BF16)
        dob = do_ref[...].astype(BF16)
        p = _mem_probs(q_ref[...], k_ref[...])
        dp = _dot(dob, vb, 1, 1)
        ds = (p * (dp - jnp.sum(p * dp, axis=-1, keepdims=True)) * scale).astype(BF16)
        dq_ref[...] = _dot(ds, kb, 1, 0).astype(dq_ref.dtype)

        @pl.when(i == 0)
        def _():
            dk_ref[...] = jnp.zeros_like(dk_ref)
            dv_ref[...] = jnp.zeros_like(dv_ref)

        dk_ref[...] += _dot(ds, qb, 0, 0)
        dv_ref[...] += _dot(p.astype(BF16), dob, 0, 0)

    kv_out = pl.BlockSpec((MEM_LEN, LANES), lambda h, i: (0, h))
    kv_shape = jax.ShapeDtypeStruct((MEM_LEN, MEM_HEADS * LANES), F32)
    return pl.pallas_call(
        body, name=name, grid=(MEM_HEADS, rows // tq),
        in_specs=[pl.BlockSpec((tq, LANES), lambda h, i: (i, h)),
                  pl.BlockSpec((MEM_LEN, LANES), lambda h, i: (0, h)),
                  pl.BlockSpec((MEM_LEN, LANES), lambda h, i: (0, MEM_HEADS + h)),
                  pl.BlockSpec((tq, LANES), lambda h, i: (i, h))],
        out_specs=[pl.BlockSpec((tq, LANES), lambda h, i: (i, h)), kv_out, kv_out],
        out_shape=[jax.ShapeDtypeStruct((rows, MEM_HEADS * LANES), BF16), kv_shape, kv_shape],
        compiler_params=_params(("parallel", "arbitrary")),
    )(mq, kv, kv, dmo)


def _resident(shape):
    return pl.BlockSpec(shape, lambda i: (0, 0), pipeline_mode=pl.Buffered(1))


def _branch_merge_fwd(acts, wts, zg, b_gate, *, tm, name):
    rows = zg.shape[0]
    d = wts[0].shape[0]

    def body(s_ref, a_ref, m_ref, ws_ref, wa_ref, wm_ref, zg_ref, b_ref, o_ref):
        gt = _sigmoid(zg_ref[...] + b_ref[...])
        acc = None
        for k, (x_ref, w_ref) in enumerate(((s_ref, ws_ref), (a_ref, wa_ref), (m_ref, wm_ref))):
            term = gt[:, k * d:(k + 1) * d] * _dot(x_ref[...].astype(BF16), w_ref[...], 1, 1)
            acc = term if acc is None else acc + term
        o_ref[...] = acc.astype(BF16)

    return pl.pallas_call(
        body, name=name, grid=(rows // tm,),
        in_specs=[pl.BlockSpec((tm, x.shape[1]), lambda i: (i, 0)) for x in acts] + [_resident(w.shape) for w in wts]
        + [pl.BlockSpec((tm, 3 * d), lambda i: (i, 0)), pl.BlockSpec((1, 3 * d), lambda i: (0, 0))],
        out_specs=pl.BlockSpec((tm, d), lambda i: (i, 0)), out_shape=jax.ShapeDtypeStruct((rows, d), BF16),
        compiler_params=_params(("parallel",)),
    )(*acts, *wts, zg, b_gate)


def _branch_merge_bwd(dmerged, acts, wts, zg, b_gate, *, tm, name, carry=None):
    rows = zg.shape[0]
    d = wts[0].shape[0]

    def body(dm_ref, s_ref, a_ref, m_ref, ws_ref, wa_ref, wm_ref, zg_ref, b_ref,
             ds_ref, da_ref, dmm_ref, dws_ref, dwa_ref, dwm_ref, dzg_ref, db_ref):
        i = pl.program_id(0)

        @pl.when(i == 0)
        def _():
            for r in (dws_ref, dwa_ref, dwm_ref, db_ref):
                r[...] = jnp.zeros_like(r)

        gt = _sigmoid(zg_ref[...] + b_ref[...])
        dm = dm_ref[...]
        groups = ((s_ref, ws_ref, ds_ref, dws_ref), (a_ref, wa_ref, da_ref, dwa_ref), (m_ref, wm_ref, dmm_ref, dwm_ref))
        for k, (x_ref, w_ref, dx_ref, dw_ref) in enumerate(groups):
            cs = pl.ds(k * d, d)
            gk = gt[:, k * d:(k + 1) * d]
            xb = x_ref[...].astype(BF16)
            br = _dot(xb, w_ref[...], 1, 1)
            dbr = (dm * gk).astype(BF16)
            dx_ref[...] = _dot(dbr, w_ref[...], 1, 0)
            dw_ref[...] += _dot(dbr, xb, 0, 0)
            dzg = dm * br * gk * (1.0 - gk)
            dzg_ref[:, cs] = dzg.astype(BF16)
            db_ref[:, cs] += jnp.sum(dzg, axis=0, keepdims=True)

    row = lambda w: pl.BlockSpec((tm, w), lambda i: (i, 0))
    whole = lambda shape: pl.BlockSpec(shape, lambda i: (0, 0))
    res = _call_with_carry(
        body, carry, name=name, grid=(rows // tm,),
        in_specs=[row(d)] + [row(x.shape[1]) for x in acts] + [_resident(w.shape) for w in wts] + [row(3 * d), whole((1, 3 * d))],
        out_specs=[row(x.shape[1]) for x in acts] + [whole(w.shape) for w in wts] + [row(3 * d), whole((1, 3 * d))],
        out_shape=[jax.ShapeDtypeStruct(x.shape, F32) for x in acts] + [jax.ShapeDtypeStruct(w.shape, F32) for w in wts]
        + [jax.ShapeDtypeStruct((rows, 3 * d), BF16), jax.ShapeDtypeStruct((1, 3 * d), F32)],
        scratch=[], operands=[dmerged, *acts, *wts, zg, b_gate], semantics=("arbitrary",))
    return tuple(res) if carry is None else (tuple(res[:8]), list(res[8:]))


def _adamw(w, g, m, v, *, tr, name):
    rows, cols = w.shape[-2:]
    assert rows % tr == 0, (name, rows, tr)

    def body(w_ref, g_ref, m_ref, v_ref, g_out, d_ref, nm_ref, nv_ref):
        gv = g_ref[...]
        m2 = ADAM_B1 * m_ref[...] + (1.0 - ADAM_B1) * gv
        v2 = ADAM_B2 * v_ref[...] + (1.0 - ADAM_B2) * (gv * gv)
        m_hat = m2 / (1.0 - ADAM_B1 ** ADAM_STEP)
        v_hat = v2 / (1.0 - ADAM_B2 ** ADAM_STEP)
        g_out[...] = gv
        d_ref[...] = -ADAM_LR * (m_hat / (jnp.sqrt(v_hat) + ADAM_EPS) + ADAM_WD * w_ref[...])
        nm_ref[...] = m2
        nv_ref[...] = v2

    flat = pl.BlockSpec((tr, cols), lambda i: (i, 0))
    blk = flat if w.ndim == 2 else pl.BlockSpec((None, tr, cols), lambda i: (0, i, 0))
    shape = jax.ShapeDtypeStruct(w.shape, F32)
    return pl.pallas_call(
        body, name=name, grid=(rows // tr,), in_specs=[blk, flat, blk, blk], out_specs=[blk] * 4,
        out_shape=[shape] * 4, compiler_params=_params(("parallel",)),
    )(w, g, m, v)


ANY = pl.BlockSpec(memory_space=pl.ANY)


def _position():
    return lax.axis_index("x"), lax.axis_index("y"), lax.axis_index("c")


def _other_chips(x, y):
    return ((1 - x, y), (x, 1 - y), (1 - x, 1 - y))


def _remote(src, dst, send_sem, recv_sem, dev):
    return pltpu.make_async_remote_copy(src_ref=src, dst_ref=dst, send_sem=send_sem, recv_sem=recv_sem,
                                        device_id=dev, device_id_type=MESH)


def _gather_exchange(shards):
    nb = len(shards)

    def rows_of(i, owner, core):
        rs = shards[i].shape[0]
        return pl.ds(pl.multiple_of(owner * rs + core * (rs // 2), 16), rs // 2)

    def first_leg(ins, outs, send_sems, recv_sems, i, j):
        x, y, c = _position()
        px, py = _other_chips(x, y)[j]
        half = shards[i].shape[0] // 2
        mine = ins[i].at[pl.ds(pl.multiple_of(c * half, 16), half)]
        return _remote(mine, outs[i].at[rows_of(i, 2 * x + y, c)], send_sems.at[i, j], recv_sems.at[i, j], (px, py, c))

    def passed_on(outs, send_sems, recv_sems, i, j, core):
        x, y, c = _position()
        px, py = _other_chips(x, y)[j]
        rows = outs[i].at[rows_of(i, 2 * px + py, core)]
        return _remote(rows, rows, send_sems.at[i, 3 + j], recv_sems.at[i, 3 + j], (x, y, 1 - c))

    def own_block(ins, outs, send_sems, recv_sems, i):
        x, y, c = _position()
        rs = shards[i].shape[0]
        place = outs[i].at[pl.ds(pl.multiple_of((2 * x + y) * rs, 16), rs)]
        return _remote(ins[i], place, send_sems.at[i, 6], recv_sems.at[i, 6], (x, y, 1 - c))

    def start(ins, outs, send_sems, recv_sems):
        for i in range(nb):
            own_block(ins, outs, send_sems, recv_sems, i).start()
            for j in range(3):
                first_leg(ins, outs, send_sems, recv_sems, i, j).start()

    def finish(ins, outs, send_sems, recv_sems):
        x, y, c = _position()
        for i in range(nb):
            for j, (px, py) in enumerate(_other_chips(x, y)):
                landed = outs[i].at[rows_of(i, 2 * px + py, c)]
                _remote(landed, landed, send_sems.at[i, j], recv_sems.at[i, j], (px, py, c)).wait_recv()
                passed_on(outs, send_sems, recv_sems, i, j, c).start()
        for i in range(nb):
            own_block(ins, outs, send_sems, recv_sems, i).wait()
            for j in range(3):
                passed_on(outs, send_sems, recv_sems, i, j, 1 - c).wait_recv()
        for i in range(nb):
            for j in range(3):
                first_leg(ins, outs, send_sems, recv_sems, i, j).wait_send()
                passed_on(outs, send_sems, recv_sems, i, j, c).wait_send()

    return _Exchange(ins=list(shards), outs=[jax.ShapeDtypeStruct((N_CHIPS * s.shape[0], s.shape[1]), s.dtype) for s in shards],
                     aliases={}, sems=[(nb, 7), (nb, 7)], start=start, finish=finish)


def _run_exchange(ex, *, name):
    n_in, n_out = len(ex.ins), len(ex.outs)

    def body(*refs):
        c_in, c_out, sems = refs[:n_in], refs[n_in:n_in + n_out], refs[n_in + n_out:]
        ex.start(c_in, c_out, *sems)
        ex.finish(c_in, c_out, *sems)

    return pl.pallas_call(
        body, name=name, in_specs=[ANY] * n_in, out_specs=[ANY] * n_out, out_shape=list(ex.outs),
        input_output_aliases=dict(ex.aliases),
        scratch_shapes=[pltpu.SemaphoreType.DMA(s) for s in ex.sems],
    )(*ex.ins)


def _row_tile(rows):
    return max(t for t in range(16, min(rows, 512) + 1, 16) if rows % t == 0)


def _halves_exchange(grads):
    nb = len(grads)

    def copies(ins, outs, send_sems, recv_sems):
        x, y, c = _position()
        return [_remote(ins[i].at[:, 1 - c], outs[i], send_sems.at[i], recv_sems.at[i], (x, y, 1 - c)) for i in range(nb)]

    def start(ins, outs, send_sems, recv_sems):
        for cp in copies(ins, outs, send_sems, recv_sems):
            cp.start()

    def finish(ins, outs, send_sems, recv_sems):
        for cp in copies(ins, outs, send_sems, recv_sems):
            cp.wait()

    return _Exchange(ins=list(grads), outs=[jax.ShapeDtypeStruct((N_CHIPS, g.shape[2], g.shape[3]), F32) for g in grads],
                     aliases={}, sems=[(nb,), (nb,)], start=start, finish=finish)


def _join_exchanges(parts):
    assert all(not ex.aliases for ex in parts)

    def split(refs, counts):
        out, at = [], 0
        for k in counts:
            out.append(refs[at:at + k])
            at += k
        return out

    def run(which):
        def go(ins, outs, *sems):
            for ex, i, o, s in zip(parts, split(ins, [len(ex.ins) for ex in parts]), split(outs, [len(ex.outs) for ex in parts]),
                                   split(sems, [len(ex.sems) for ex in parts])):
                getattr(ex, which)(i, o, *s)
        return go

    return _Exchange(ins=[a for ex in parts for a in ex.ins], outs=[a for ex in parts for a in ex.outs], aliases={},
                     sems=[s for ex in parts for s in ex.sems], start=run("start"), finish=run("finish"))


def _pair_sum(g4, got, c_arr, *, name):
    _, _, half, cols = g4.shape
    tr = _row_tile(half)

    def body(c_ref, g_ref, t_ref, p_ref, pb_ref):
        sm = g_ref[...] + t_ref[...]
        p_ref[...] = sm
        pb_ref[...] = sm.astype(BF16)

    blk = pl.BlockSpec((None, tr, cols), lambda j, i, c_ref: (j, i, 0))
    grid_spec = pltpu.PrefetchScalarGridSpec(
        num_scalar_prefetch=1, grid=(N_CHIPS, half // tr),
        in_specs=[pl.BlockSpec((None, None, tr, cols), lambda j, i, c_ref: (j, c_ref[0], i, 0)), blk],
        out_specs=[blk, blk])
    return pl.pallas_call(
        body, name=name, grid_spec=grid_spec,
        out_shape=[jax.ShapeDtypeStruct((N_CHIPS, half, cols), F32), jax.ShapeDtypeStruct((N_CHIPS, half, cols), BF16)],
        compiler_params=_params(("parallel", "parallel")),
    )(c_arr, g4, got)


def _scatter_exchange(parts):
    nb = len(parts)

    def copies(ins, outs, send_sems, recv_sems):
        x, y, c = _position()
        return [_remote(ins[i].at[2 * px + py], outs[i].at[j], send_sems.at[i, j], recv_sems.at[i, j], (px, py, c))
                for i in range(nb) for j, (px, py) in enumerate(_other_chips(x, y))]

    def start(ins, outs, send_sems, recv_sems):
        for cp in copies(ins, outs, send_sems, recv_sems):
            cp.start()

    def finish(ins, outs, send_sems, recv_sems):
        for cp in copies(ins, outs, send_sems, recv_sems):
            cp.wait()

    return _Exchange(ins=list(parts), outs=[jax.ShapeDtypeStruct((3,) + p.shape[1:], p.dtype) for p in parts],
                     aliases={}, sems=[(nb, 3), (nb, 3)], start=start, finish=finish)


def _owner_sum(p, got, chip_arr, c_arr, *, replicated, name):
    _, half, cols = p.shape
    tr = _row_tile(half)

    def body(chip_ref, c_ref, p_ref, r_ref, o_ref):
        o_ref[...] = ((p_ref[...] + r_ref[0].astype(F32)) + r_ref[1].astype(F32)) + r_ref[2].astype(F32)

    if replicated:
        out_spec = pl.BlockSpec((None, None, tr, cols), lambda i, chip_ref, c_ref: (chip_ref[0], c_ref[0], i, 0))
        out_shape = jax.ShapeDtypeStruct((N_CHIPS, 2, half, cols), F32)
    else:
        out_spec = pl.BlockSpec((None, tr, cols), lambda i, chip_ref, c_ref: (c_ref[0], i, 0))
        out_shape = jax.ShapeDtypeStruct((2, half, cols), F32)
    grid_spec = pltpu.PrefetchScalarGridSpec(
        num_scalar_prefetch=2, grid=(half // tr,),
        in_specs=[pl.BlockSpec((None, tr, cols), lambda i, chip_ref, c_ref: (chip_ref[0], i, 0)),
                  pl.BlockSpec((3, tr, cols), lambda i, chip_ref, c_ref: (0, i, 0))],
        out_specs=out_spec)
    return pl.pallas_call(
        body, name=name, grid_spec=grid_spec, out_shape=out_shape,
        compiler_params=_params(("parallel",)),
    )(chip_arr, c_arr, p, got)


def _share_reduced(bufs):
    nb = len(bufs) - 1

    def body(*refs):
        outs = refs[nb + 1:2 * nb + 2]
        send_sems, recv_sems = refs[2 * nb + 2:]
        x, y, c = _position()
        chip = 2 * x + y
        sends = []
        for i in range(nb):
            cp = _remote(outs[i].at[c], outs[i].at[c], send_sems.at[i], recv_sems.at[i], (x, y, 1 - c))
            cp.start()
            sends.append(cp)
        small = outs[nb]
        peers = [(fx, fy, fc) for fx in (0, 1) for fy in (0, 1) for fc in (0, 1) if fx + fy + fc > 0]
        for k, (fx, fy, fc) in enumerate(peers):
            dev = (x ^ fx, y ^ fy, c ^ fc)
            cp = _remote(small.at[chip, c], small.at[chip, c], send_sems.at[nb + k], recv_sems.at[nb + k], dev)
            cp.start()
            sends.append(cp)
        for i in range(nb):
            dst = outs[i].at[1 - c]
            _remote(dst, dst, send_sems.at[i], recv_sems.at[i], (x, y, 1 - c)).wait_recv()
        for k, (fx, fy, fc) in enumerate(peers):
            dst = small.at[2 * (x ^ fx) + (y ^ fy), c ^ fc]
            _remote(dst, dst, send_sems.at[nb + k], recv_sems.at[nb + k], (x ^ fx, y ^ fy, c ^ fc)).wait_recv()
        for cp in sends:
            cp.wait_send()

    n_all = nb + 1
    return pl.pallas_call(
        body, name="grad_share_reduced", in_specs=[ANY] * n_all, out_specs=[ANY] * n_all,
        out_shape=[jax.ShapeDtypeStruct(b.shape, b.dtype) for b in bufs],
        input_output_aliases={i: i for i in range(n_all)},
        scratch_shapes=[pltpu.SemaphoreType.DMA((nb + 7,)), pltpu.SemaphoreType.DMA((nb + 7,))],
    )(*bufs)


class _GradReducer:
    def __init__(self, c_arr, chip_arr):
        self.c_arr, self.chip_arr = c_arr, chip_arr
        self.full, self.pairs, self.landed = {}, {}, {}

    def swap(self, names, grads):
        for n, g in zip(names, grads):
            self.full[n] = g.reshape(N_CHIPS, 2, g.shape[0] // (2 * N_CHIPS), g.shape[1])
        return _halves_exchange([self.full[n] for n in names])

    def swapped(self, names, bufs):
        for n, t in zip(names, bufs):
            self.pairs[n] = _pair_sum(self.full[n], t, self.c_arr, name="grad_pair_sum_" + n)

    def scatter(self, names):
        return _scatter_exchange([self.pairs[n][1] for n in names])

    def collect(self, names, bufs):
        self.landed.update(zip(names, bufs))

    def swap_now(self, names, grads):
        self.swapped(names, _run_exchange(self.swap(names, grads), name="grad_exchange_" + names[0]))

    def finish(self, names, grads, order):
        self.swap_now(names, grads)
        self.collect(names, _run_exchange(self.scatter(names), name="grad_scatter_" + names[0]))
        totals = [_owner_sum(self.pairs[n][0], self.landed[n], self.chip_arr, self.c_arr, replicated=(n == order[-1]),
                             name="grad_owner_sum_" + n) for n in order]
        return _share_reduced(totals)


def _pack_small(vals):
    flat = jnp.concatenate([vals[name].reshape(-1) for name, _ in SMALL])
    return jnp.pad(flat, (0, N_CHIPS * SMALL_ROWS * 1024 - SMALL_ELEMS)).reshape(N_CHIPS * SMALL_ROWS, 1024)


def _unpack_small(buf):
    flat = buf.reshape(-1)
    out, off = {}, 0
    for name, shape in SMALL:
        n = int(np.prod(shape))
        out[name] = flat[off:off + n].reshape(shape)
        off += n
    return out


EARLY_REDUCED = (("w_down",), ("w_up",), ("w_o", "w_ssm_br", "w_attn_br", "w_mem_br", "w_glu", "w_mem_kv"), ("w_in",))


def _device_step(x, mem, tgt, w, p, *, shards, reducer):
    rows = x.shape[0]
    w = dict(w)
    early = EARLY_REDUCED
    gb = {}
    gather_pending = shards is not None

    def riding(*stages):
        if reducer is None or not stages:
            return None
        return _join_exchanges([reducer.swap(names, [gb[n] for n in names]) if kind == "swap" else reducer.scatter(names)
                                for kind, names in stages])

    def arrived(stages, res):
        if reducer is None or not stages:
            return res
        main, bufs = res
        for kind, names in stages:
            (reducer.swapped if kind == "swap" else reducer.collect)(names, bufs[:len(names)])
            bufs = bufs[len(names):]
        return main

    def fetching(names):
        return _gather_exchange([shards[n] for n in names]) if gather_pending else None

    def fetched(names, res):
        if not gather_pending:
            return res
        w.update(zip(names, res[1]))
        return res[0]

    first_use = (("w_glu", "w_ssm_br", "w_attn_br", "w_mem_kv", "w_mem_br", "w_o", "w_up"), ("w_down",))
    g1, gm, g2 = p["norm1_g"], p["mem_norm_g"], p["norm2_g"]
    gf = p["final_g"].reshape(1, D_MODEL)
    ssm_args = (p["ssm_lambda_re"][0], p["ssm_lambda_im"][0], p["ssm_log_dt"][0], p["ssm_b_re"][0],
                p["ssm_b_im"][0], p["ssm_c_re"][0], p["ssm_c_im"][0])
    (a_lay, b_blk, c_blk), ssm_vjp = jax.vjp(_ssm_matrices, *ssm_args)
    a_conj = a_lay * _to_scan_layout(jnp.stack([jnp.ones((N_STATES,), F32), -jnp.ones((N_STATES,), F32)]))[None, :]
    dd = p["ssm_d"].reshape(1, SSM_WIDTH)
    win_t = w["w_in"]
    mm = _matmul

    n1 = _rmsnorm_fwd(x, g1, tm=512, name="norm1")
    splits = ((OFF_U, OFF_QKV - OFF_U), (OFF_QKV, OFF_MQ - OFF_QKV), (OFF_MQ, OFF_ZG - OFF_MQ), (OFF_ZG, IN_WIDTH - OFF_ZG))
    u, qkv, mq, zg = fetched(first_use[0], _split_matmul(n1, win_t, splits, tm=512, carry=fetching(first_use[0]),
                                                         vmem=VMEM_LIMIT_WIDE_BYTES, name="in_proj"))

    u_i = _interleave(u)
    ends = _ssm_ends(a_lay, u_i, b_blk, transpose=False, reverse=False, tt=512, name="ssm_fwd_ends")
    s, ys_i, s_entry = _ssm_fwd(a_lay, u_i, b_blk, c_blk, ends, tt=512, name="ssm_fwd")
    ys = _deinterleave(ys_i)
    y0, tglu, y2 = _glu_fwd(ys, u, dd, w["w_glu"], p["b_glu"], tm=512, name="glu_fwd")

    outs, lses = [], []
    for g, (_, d) in enumerate(ATTN_PATTERNS):
        o_g, lse_g = _attn_fwd(qkv, g, d, name=f"attn_fwd_{g}")
        outs.append(o_g)
        lses.append(lse_g)
    o, lse = _attn_merge(outs, lses, tm=1024, name="attn_merge")

    mn = _rmsnorm_fwd(mem, gm, tm=MEM_LEN, name="mem_norm")
    kv = mm(mn, w["w_mem_kv"], m=MEM_LEN, n=1024, k=1024, tm=MEM_LEN, tn=1024, tk=1024, out_dtypes=(F32,), name="mem_kv")
    mo = _mem_attn_fwd(mq, kv, tq=1024, name="mem_attn_fwd")

    branch_acts = (y2, o, mo)
    branch_wts = (w["w_ssm_br"], w["w_attn_br"], w["w_mem_br"])
    merged = _branch_merge_fwd(branch_acts, branch_wts, zg, p["b_gate"], tm=256, name="branch_merge_fwd")
    h1, n2 = mm(merged, w["w_o"], m=rows, n=1024, k=1024, tm=1024, tn=1024, tk=1024, out_dtypes=(F32, BF16),
                aux=((x, "mn"), (g2, "row")), epilogue=_residual_norm_epilogue, name="out_proj")
    relu2 = lambda acc: (jnp.square(jnp.maximum(acc, 0.0)),)
    act = fetched(first_use[1], _sum_matmul([n2], w["w_up"], [0], tb=True, tm=512, out_dtype=BF16, epilogue=relu2,
                                            carry=fetching(first_use[1]), name="mlp_up"))
    dh2, d_gf, sq_err = _sum_matmul([act], w["w_down"], [0], tm=512, aux=((h1, "mn"), (tgt, "mn"), (gf, "row")),
                                    epilogue=_loss_head_epilogue, n_sums=2, name="mlp_down")
    loss = (0.5 / D_MODEL) * jnp.sum(sq_err)

    gs = {"final_g": d_gf.reshape(D_MODEL)}
    drelu2 = lambda acc, actv: (acc * (2.0 * jnp.sqrt(actv.astype(F32))),)
    dup = mm(dh2, w["w_down"], m=rows, n=D_FF, k=1024, tb=True, tm=1024, tn=2048, tk=1024, out_dtypes=(BF16,),
             aux=((act, "mn"),), epilogue=drelu2, name="d_act")
    gb["w_down"] = mm(act, dh2, m=D_FF, n=1024, k=rows, ta=True, tm=1024, tn=1024, tk=2048, out_dtypes=(F32,), name="dw_down")
    stages = (("swap", early[0]),)
    gb["w_up"] = arrived(stages, mm(dup, n2, m=D_FF, n=1024, k=rows, ta=True, tm=1024, tn=1024, tk=2048,
                                    out_dtypes=(F32,), carry=riding(*stages), name="dw_up"))
    stages = (("scatter", early[0]), ("swap", early[1]))
    dh1, gs["norm2_g"] = arrived(stages, _sum_matmul([dup], w["w_up"], [0], tm=512, aux=((h1, "mn"), (dh2, "mn"), (g2, "row")),
                                                     epilogue=_rmsnorm_bwd_epilogue, n_sums=1, carry=riding(*stages), name="d_n2"))
    dmerged = mm(dh1, w["w_o"], m=rows, n=1024, k=1024, tb=True, tm=1024, tn=1024, tk=1024, out_dtypes=(F32,), name="d_merged")
    gb["w_o"] = mm(merged, dh1, m=1024, n=1024, k=rows, ta=True, tm=1024, tn=1024, tk=2048, out_dtypes=(F32,), name="dw_o")
    stages = (("scatter", early[1]),)
    (dy2, do, dmo, gb["w_ssm_br"], gb["w_attn_br"], gb["w_mem_br"], dzg, gs["b_gate"]) = arrived(stages, _branch_merge_bwd(
        dmerged, branch_acts, branch_wts, zg, p["b_gate"], tm=256, carry=riding(*stages), name="branch_merge_bwd"))

    dy0, dt, y1, gs["b_glu"], d_dd = _glu_bwd(dy2, y0, tglu, u, w["w_glu"], tm=512, name="glu_bwd")
    gs["ssm_d"] = d_dd.reshape(1, SSM_GROUPS, SSM_GROUP_SIZE)
    gb["w_glu"] = mm(y1, dt, m=512, n=512, k=rows, ta=True, tm=512, tn=512, tk=1024, out_dtypes=(F32,), name="dw_glu")
    dy0_i = _interleave(dy0)
    lam_ends = _ssm_ends(a_conj, dy0_i, c_blk, transpose=True, reverse=True, tt=512, name="ssm_bwd_ends")
    du_i, d_b_blk, d_c_blk, d_a_lay = _ssm_bwd(a_conj, dy0_i, u_i, s, s_entry, b_blk, c_blk, dd, lam_ends, tt=512,
                                                name="ssm_bwd")
    du = _deinterleave(du_i)
    d_ssm = ssm_vjp((d_a_lay, d_b_blk, d_c_blk))
    for name, val in zip(("ssm_lambda_re", "ssm_lambda_im", "ssm_log_dt", "ssm_b_re", "ssm_b_im", "ssm_c_re", "ssm_c_im"), d_ssm):
        gs[name] = val[None]

    dqkv = None
    for g, (_, d) in enumerate(ATTN_PATTERNS):
        dqkv = _attn_bwd(qkv, do, o, lse, g, d, dqkv, name=f"attn_bwd_{g}")

    dmq, dmk, dmv = _mem_attn_bwd(mq, kv, dmo, tq=1024, name="mem_attn_bwd")
    dkv = jnp.concatenate([dmk, dmv], axis=1)
    gb["w_mem_kv"] = mm(mn, dkv, m=1024, n=1024, k=MEM_LEN, ta=True, tm=1024, tn=1024, tk=MEM_LEN, out_dtypes=(F32,), name="dw_mem_kv")
    dmn = mm(dkv, w["w_mem_kv"], m=MEM_LEN, n=1024, k=1024, tb=True, tm=MEM_LEN, tn=1024, tk=1024, out_dtypes=(F32,), name="d_mn")
    _, gs["mem_norm_g"] = _rmsnorm_bwd(mem, gm, dmn, None, tm=MEM_LEN, name="mem_norm_bwd")

    pieces = ((du, OFF_U, "u"), (dqkv[0], OFF_QKV, "q"), (dqkv[1], OFF_QKV + 768, "k"), (dqkv[2], OFF_QKV + 1536, "v"),
              (dmq, OFF_MQ, "mq"), (dzg, OFF_ZG, "zg"))
    dw_rows = []
    for piece, off, tag in pieces:
        width = piece.shape[1]
        tmw = 1024 if width % 1024 == 0 else (768 if width == 768 else 512)
        stages = {"q": (("swap", early[2]),), "zg": (("scatter", early[2]),)}.get(tag, ())
        dw_rows.append(arrived(stages, mm(piece, n1, m=width, n=1024, k=rows, ta=True, tm=tmw, tn=1024, tk=2048,
                                          out_dtypes=(F32,), carry=riding(*stages), name="dw_in_" + tag)))
    gb["w_in"] = jnp.concatenate(dw_rows, axis=0)
    if reducer is not None:
        reducer.swap_now(early[3], [gb["w_in"]])
    stages = (("scatter", early[3]),)
    dx, gs["norm1_g"] = arrived(stages, _sum_matmul(
        [piece for piece, _, _ in pieces], win_t, [off for _, off, _ in pieces], tm=512,
        aux=((x, "mn"), (dh1, "mn"), (g1, "row")), epilogue=_rmsnorm_bwd_epilogue, n_sums=1,
        carry=riding(*stages), vmem=VMEM_LIMIT_WIDE_BYTES, name="d_n1"))
    return loss, dx, gb, gs


def kernel(x, mem, norm1_g, mem_norm_g, w_in, b_gate, ssm_lambda_re, ssm_lambda_im, ssm_log_dt, ssm_b_re, ssm_b_im, ssm_c_re, ssm_c_im, ssm_d, w_glu, b_glu, w_ssm_br, w_attn_br, w_mem_kv, w_mem_br, w_o, norm2_g, w_up, w_down, final_g, loss_target, m_norm1_g, m_mem_norm_g, m_w_in, m_b_gate, m_ssm_lambda_re, m_ssm_lambda_im, m_ssm_log_dt, m_ssm_b_re, m_ssm_b_im, m_ssm_c_re, m_ssm_c_im, m_ssm_d, m_w_glu, m_b_glu, m_w_ssm_br, m_w_attn_br, m_w_mem_kv, m_w_mem_br, m_w_o, m_norm2_g, m_w_up, m_w_down, m_final_g, v_norm1_g, v_mem_norm_g, v_w_in, v_b_gate, v_ssm_lambda_re, v_ssm_lambda_im, v_ssm_log_dt, v_ssm_b_re, v_ssm_b_im, v_ssm_c_re, v_ssm_c_im, v_ssm_d, v_w_glu, v_b_glu, v_w_ssm_br, v_w_attn_br, v_w_mem_kv, v_w_mem_br, v_w_o, v_norm2_g, v_w_up, v_w_down, v_final_g):
    env = dict(locals())
    weights = {n: env[n] for n in WEIGHT_ORDER}
    moms = {n: env["m_" + n] for n in WEIGHT_ORDER}
    vels = {n: env["v_" + n] for n in WEIGHT_ORDER}
    def shard2d(a):
        return a.reshape(a.shape[-2], a.shape[-1])

    chip = 2 * lax.axis_index("x") + lax.axis_index("y")
    wire = [shard2d(weights[n]).astype(BF16) for n, _, _ in BIG]
    wire = dict(zip([n for n, _, _ in BIG], [s.T if tr else s for s, (_, tr, _) in zip(wire, BIG)]))
    w_in_full = _run_exchange(_gather_exchange([wire.pop("w_in")]), name="all_gather_w_in")[0]
    small = {n: weights[n] for n, _ in SMALL}

    reducer = _GradReducer(lax.axis_index("c").astype(jnp.int32).reshape(1), chip.astype(jnp.int32).reshape(1))
    loss, dx, gb, gs = _device_step(x[0], mem[0], loss_target[0], {"w_in": w_in_full}, small, shards=wire, reducer=reducer)
    *shards, small_grad = reducer.finish(["small"], [_pack_small(gs)], [n for n, _, _ in BIG] + ["small"])
    grads = {}
    for (n, tr, _), sh in zip(BIG, shards):
        sh = sh.reshape(2 * sh.shape[1], sh.shape[2])
        grads[n] = sh.T if tr else sh
    small_grad = small_grad.reshape(N_CHIPS * SMALL_ROWS, 1024)
    grads_small = _unpack_small(small_grad)

    delta, new_m, new_v = {}, {}, {}
    for n, _, _ in BIG:
        grads[n], delta[n], new_m[n], new_v[n] = _adamw(weights[n], grads[n], moms[n], vels[n],
                                                        tr=min(weights[n].shape[-2], 256), name="adamw_" + n)
    _, ds_, ms_, vs_ = _adamw(_pack_small(small), small_grad,
                              _pack_small({n: moms[n] for n, _ in SMALL}), _pack_small({n: vels[n] for n, _ in SMALL}),
                              tr=N_CHIPS * SMALL_ROWS, name="adamw_small")
    for dst, buf in ((delta, ds_), (new_m, ms_), (new_v, vs_)):
        dst.update(_unpack_small(buf))
    grads.update(grads_small)

    total_loss = lax.psum(loss, ("x", "y", "c"))
    return (total_loss, dx[None], *[grads[n] for n in WEIGHT_ORDER], *[delta[n] for n in WEIGHT_ORDER],
            *[new_m[n] for n in WEIGHT_ORDER], *[new_v[n] for n in WEIGHT_ORDER])
```

```python
import functools
import math

import numpy as np
import jax
import jax.numpy as jnp
from jax import lax
from jax.experimental import pallas as pl
from jax.experimental.pallas import tpu as pltpu

F32 = jnp.float32
BF16 = jnp.bfloat16

D_MODEL = 1024
SSM_GROUPS = 32
SSM_GROUP_SIZE = 16
SSM_STATE = 64
SSM_WIDTH = 512
N_STATES = SSM_GROUPS * SSM_STATE
SCAN_CB = 1024
ATTN_PATTERNS = ((128, 1), (512, 4), (2048, 16))
ATTN_HEAD_DIM = 64
ATTN_Q = 128
MEM_LEN = 256
MEM_HEAD_DIM = 128
MEM_HEADS = 4
D_FF = 4096
OFF_U, OFF_QKV, OFF_MQ, OFF_ZG = 0, 512, 2816, 3328
IN_WIDTH = 6400
RMS_EPS = 1e-6
NEG_INF = -1e30
ADAM_LR, ADAM_B1, ADAM_B2, ADAM_EPS, ADAM_WD, ADAM_STEP = 0.001, 0.9, 0.999, 1e-08, 0.01, 10

VMEM_LIMIT_BYTES = 48 * 1024 * 1024
VMEM_LIMIT_WIDE_BYTES = 56 * 1024 * 1024
LANES = 128
MESH = pl.DeviceIdType.MESH
N_CHIPS = 4

SCAN_SEGS = 8
SCAN_GROUPS = SCAN_CB // SSM_STATE

BIG = (("w_in", True, (6400, 1024)), ("w_glu", False, (512, 512)), ("w_ssm_br", True, (1024, 512)),
       ("w_attn_br", True, (1024, 256)), ("w_mem_kv", False, (1024, 1024)), ("w_mem_br", True, (1024, 512)),
       ("w_o", False, (1024, 1024)), ("w_up", True, (4096, 1024)), ("w_down", False, (4096, 1024)))
SMALL = (("norm1_g", (1, 1024)), ("mem_norm_g", (1, 1024)), ("b_gate", (1, 3072)),
         ("ssm_lambda_re", (1, 32, 64)), ("ssm_lambda_im", (1, 32, 64)), ("ssm_log_dt", (1, 32)),
         ("ssm_b_re", (1, 32, 64, 16)), ("ssm_b_im", (1, 32, 64, 16)), ("ssm_c_re", (1, 32, 16, 64)),
         ("ssm_c_im", (1, 32, 16, 64)), ("ssm_d", (1, 32, 16)), ("b_glu", (1, 512)),
         ("norm2_g", (1, 1024)), ("final_g", (1024,)))
WEIGHT_ORDER = ("norm1_g", "mem_norm_g", "w_in", "b_gate", "ssm_lambda_re", "ssm_lambda_im", "ssm_log_dt",
                "ssm_b_re", "ssm_b_im", "ssm_c_re", "ssm_c_im", "ssm_d", "w_glu", "b_glu", "w_ssm_br",
                "w_attn_br", "w_mem_kv", "w_mem_br", "w_o", "norm2_g", "w_up", "w_down", "final_g")
SMALL_ELEMS = sum(int(np.prod(s)) for _, s in SMALL)
SMALL_ROWS = 64


def _params(sem, vmem=VMEM_LIMIT_BYTES):
    return pltpu.CompilerParams(dimension_semantics=sem, vmem_limit_bytes=vmem)


def _sigmoid(v):
    return 0.5 * jnp.tanh(0.5 * v) + 0.5


_GELU_C = math.sqrt(2.0 / math.pi)


def _gelu(v):
    return 0.5 * v * (1.0 + jnp.tanh(_GELU_C * (v + 0.044715 * v * v * v)))


def _gelu_grad(v):
    th = jnp.tanh(_GELU_C * (v + 0.044715 * v * v * v))
    return 0.5 * (1.0 + th) + 0.5 * v * (1.0 - th * th) * _GELU_C * (1.0 + 3.0 * 0.044715 * v * v)


def _dot(a, b, ca, cb):
    return lax.dot_general(a, b, (((ca,), (cb,)), ((), ())), preferred_element_type=F32)


class _Exchange:
    def __init__(self, ins, outs, aliases, sems, start, finish):
        self.ins, self.outs, self.aliases, self.sems, self.start, self.finish = ins, outs, aliases, sems, start, finish


def _matmul(a, b, *, m, n, k, ta=False, tb=False, tm, tn, tk, out_dtypes, name,
            aux=(), epilogue=None, n_sums=0, carry=None):
    assert m % tm == 0 and n % tn == 0 and k % tk == 0, (name, m, n, k, tm, tn, tk)
    assert n_sums == 0 or tn == n, name
    nk = k // tk
    n_aux = len(aux)
    n_tiles = len(out_dtypes)
    n_out = n_tiles + n_sums
    a_spec = pl.BlockSpec((tk, tm), lambda i, j, kk: (kk, i)) if ta else pl.BlockSpec((tm, tk), lambda i, j, kk: (i, kk))
    b_spec = pl.BlockSpec((tn, tk), lambda i, j, kk: (j, kk)) if tb else pl.BlockSpec((tk, tn), lambda i, j, kk: (kk, j))
    aux_specs = []
    for _, kind in aux:
        if kind == "mn":
            aux_specs.append(pl.BlockSpec((tm, tn), lambda i, j, kk: (i, j)))
        else:
            aux_specs.append(pl.BlockSpec((1, tn), lambda i, j, kk: (0, j)))
    ca = 0 if ta else 1
    cb = 1 if tb else 0

    def finish(acc, aux_refs, out_refs, row_tile):
        outs = (acc,) if epilogue is None else epilogue(acc, *[r[...] for r in aux_refs])
        for o_ref, o in zip(out_refs[:n_tiles], outs[:n_tiles]):
            o_ref[...] = o.astype(o_ref.dtype)
        _accumulate_over_rows(out_refs[n_tiles:], outs[n_tiles:], row_tile)

    def body(a_ref, b_ref, *rest):
        aux_refs = rest[:n_aux]
        out_refs = rest[n_aux:n_aux + n_out]
        row_tile = pl.program_id(0)
        prod = _dot(a_ref[...].astype(BF16), b_ref[...].astype(BF16), ca, cb)
        if nk == 1:
            finish(prod, aux_refs, out_refs, row_tile)
            return
        acc_ref = rest[n_aux + n_out]
        kk = pl.program_id(2)

        @pl.when(kk == 0)
        def _():
            acc_ref[...] = prod

        @pl.when(jnp.logical_and(kk > 0, kk < nk - 1))
        def _():
            acc_ref[...] += prod

        @pl.when(kk == nk - 1)
        def _():
            finish(acc_ref[...] + prod, aux_refs, out_refs, row_tile)

    tile = pl.BlockSpec((tm, tn), lambda i, j, kk: (i, j))
    col_sum = pl.BlockSpec((1, tn), lambda i, j, kk: (0, j))
    res = _call_with_carry(
        body, carry, name=name, grid=(m // tm, n // tn, nk), in_specs=[a_spec, b_spec] + aux_specs,
        out_specs=[tile] * n_tiles + [col_sum] * n_sums,
        out_shape=[jax.ShapeDtypeStruct((m, n), dt) for dt in out_dtypes] + [jax.ShapeDtypeStruct((1, n), F32)] * n_sums,
        scratch=[pltpu.VMEM((tm, tn), F32)] if nk > 1 else [], operands=[a, b] + [x for x, _ in aux],
        semantics=("arbitrary" if n_sums else "parallel", "parallel", "arbitrary"))
    main = res[0] if n_out == 1 else tuple(res[:n_out])
    return main if carry is None else (main, list(res[n_out:]))


def _accumulate_over_rows(sum_refs, terms, row_tile):
    for s_ref, term in zip(sum_refs, terms):
        @pl.when(row_tile == 0)
        def _():
            s_ref[...] = term

        @pl.when(row_tile > 0)
        def _():
            s_ref[...] += term


def _call_with_carry(body, carry, *, name, grid, in_specs, out_specs, out_shape, scratch, operands, semantics,
                     vmem=VMEM_LIMIT_BYTES):
    if carry is None:
        return pl.pallas_call(body, name=name, grid=grid, in_specs=in_specs, out_specs=out_specs, out_shape=out_shape,
                              scratch_shapes=scratch, compiler_params=_params(semantics, vmem))(*operands)
    n_in, n_cin, n_out, n_cout, n_scr = len(operands), len(carry.ins), len(out_shape), len(carry.outs), len(scratch)

    def hosted(*refs):
        main_in, c_in = refs[:n_in], refs[n_in:n_in + n_cin]
        main_out = refs[n_in + n_cin:n_in + n_cin + n_out]
        c_out = refs[n_in + n_cin + n_out:n_in + n_cin + n_out + n_cout]
        rest = refs[n_in + n_cin + n_out + n_cout:]
        ids = [pl.program_id(t) for t in range(len(grid))]
        first = functools.reduce(jnp.logical_and, [i == 0 for i in ids])
        last = functools.reduce(jnp.logical_and, [i == g - 1 for i, g in zip(ids, grid)])

        @pl.when(first)
        def _():
            carry.start(c_in, c_out, *rest[n_scr:])

        body(*main_in, *main_out, *rest[:n_scr])

        @pl.when(last)
        def _():
            carry.finish(c_in, c_out, *rest[n_scr:])

    return pl.pallas_call(
        hosted, name=name, grid=grid,
        in_specs=list(in_specs) + [ANY] * n_cin, out_specs=list(out_specs) + [ANY] * n_cout,
        out_shape=list(out_shape) + list(carry.outs),
        input_output_aliases={n_in + i: n_out + o for i, o in carry.aliases.items()},
        scratch_shapes=list(scratch) + [pltpu.SemaphoreType.DMA(s) for s in carry.sems],
        compiler_params=_params(("arbitrary",) * len(grid), vmem),
    )(*operands, *carry.ins)


def _sum_matmul(pieces, b, offs, *, tm, name, tb=False, out_dtype=F32, aux=(), epilogue=None, n_sums=0, carry=None,
                vmem=VMEM_LIMIT_BYTES):
    m = pieces[0].shape[0]
    n = b.shape[0] if tb else b.shape[1]
    npieces, n_aux = len(pieces), len(aux)
    assert not tb or npieces == 1

    def body(*refs):
        b_ref = refs[npieces]
        aux_refs = refs[npieces + 1:npieces + 1 + n_aux]
        out_refs = refs[npieces + 1 + n_aux:]
        acc = None
        for p_ref, off in zip(refs[:npieces], offs):
            lhs = p_ref[...].astype(BF16)
            part = _dot(lhs, b_ref[...], 1, 1) if tb else _dot(lhs, b_ref[pl.ds(off, p_ref.shape[1]), :], 1, 0)
            acc = part if acc is None else acc + part
        outs = (acc,) if epilogue is None else epilogue(acc, *[r[...] for r in aux_refs])
        out_refs[0][...] = outs[0].astype(out_dtype)
        _accumulate_over_rows(out_refs[1:], outs[1:], pl.program_id(0))

    row = pl.BlockSpec((tm, n), lambda i: (i, 0))
    vec = pl.BlockSpec((1, n), lambda i: (0, 0))
    res = _call_with_carry(
        body, carry, name=name, grid=(m // tm,),
        in_specs=[pl.BlockSpec((tm, p.shape[1]), lambda i: (i, 0)) for p in pieces] + [_resident(b.shape)]
        + [row if kind == "mn" else vec for _, kind in aux],
        out_specs=[row] + [vec] * n_sums,
        out_shape=[jax.ShapeDtypeStruct((m, n), out_dtype)] + [jax.ShapeDtypeStruct((1, n), F32)] * n_sums,
        scratch=[], operands=list(pieces) + [b] + [x for x, _ in aux], semantics=("arbitrary" if n_sums else "parallel",),
        vmem=vmem)
    main = res[0] if n_sums == 0 else tuple(res[:1 + n_sums])
    return main if carry is None else (main, list(res[1 + n_sums:]))


def _split_matmul(a, b_t, splits, *, tm, name, carry=None, vmem=VMEM_LIMIT_BYTES):
    m, k = a.shape

    def body(a_ref, b_ref, *out_refs):
        av = a_ref[...].astype(BF16)
        for (row0, width), o_ref in zip(splits, out_refs):
            o_ref[...] = _dot(av, b_ref[pl.ds(row0, width), :], 1, 1)

    res = _call_with_carry(
        body, carry, name=name, grid=(m // tm,),
        in_specs=[pl.BlockSpec((tm, k), lambda i: (i, 0)), _resident(b_t.shape)],
        out_specs=[pl.BlockSpec((tm, width), lambda i: (i, 0)) for _, width in splits],
        out_shape=[jax.ShapeDtypeStruct((m, width), F32) for _, width in splits],
        scratch=[], operands=[a, b_t], semantics=("parallel",), vmem=vmem)
    outs = tuple(res[:len(splits)])
    return outs if carry is None else (outs, list(res[len(splits):]))


def _rmsnorm_fwd(x, g, *, tm, name, carry=None):
    rows, d = x.shape

    def body(x_ref, g_ref, o_ref):
        xv = x_ref[...]
        r = lax.rsqrt(jnp.mean(xv * xv, axis=-1, keepdims=True) + RMS_EPS)
        o_ref[...] = (xv * r * g_ref[...]).astype(o_ref.dtype)

    res = _call_with_carry(
        body, carry, name=name, grid=(rows // tm,),
        in_specs=[pl.BlockSpec((tm, d), lambda i: (i, 0)), pl.BlockSpec((1, d), lambda i: (0, 0))],
        out_specs=[pl.BlockSpec((tm, d), lambda i: (i, 0))], out_shape=[jax.ShapeDtypeStruct((rows, d), BF16)],
        scratch=[], operands=[x, g], semantics=("parallel",))
    return res[0] if carry is None else (res[0], list(res[1:]))


def _residual_norm_epilogue(acc, xv, gv):
    h = acc + xv
    r = lax.rsqrt(jnp.mean(h * h, axis=-1, keepdims=True) + RMS_EPS)
    return h, h * r * gv


def _rmsnorm_bwd_epilogue(dy, xv, resv, gv):
    r = lax.rsqrt(jnp.mean(xv * xv, axis=-1, keepdims=True) + RMS_EPS)
    xhat = xv * r
    dyg = dy * gv
    dx = r * (dyg - xhat * jnp.mean(dyg * xhat, axis=-1, keepdims=True)) + resv
    return dx, jnp.sum(dy * xhat, axis=0, keepdims=True)


def _rmsnorm_bwd(x, g, dy, res, *, tm, name):
    rows, d = x.shape
    has_res = res is not None

    def body(x_ref, g_ref, dy_ref, *rest):
        if has_res:
            res_ref, dx_ref, dg_ref = rest
        else:
            dx_ref, dg_ref = rest
        i = pl.program_id(0)
        xv = x_ref[...]
        r = lax.rsqrt(jnp.mean(xv * xv, axis=-1, keepdims=True) + RMS_EPS)
        xhat = xv * r
        dyv = dy_ref[...]
        dyg = dyv * g_ref[...]
        dx = r * (dyg - xhat * jnp.mean(dyg * xhat, axis=-1, keepdims=True))
        if has_res:
            dx = dx + res_ref[...]
        dx_ref[...] = dx

        @pl.when(i == 0)
        def _():
            dg_ref[...] = jnp.zeros_like(dg_ref)

        dg_ref[...] += jnp.sum(dyv * xhat, axis=0, keepdims=True)

    row_spec = pl.BlockSpec((tm, d), lambda i: (i, 0))
    vec_spec = pl.BlockSpec((1, d), lambda i: (0, 0))
    ins = [x, g, dy] + ([res] if has_res else [])
    return pl.pallas_call(
        body, name=name, grid=(rows // tm,),
        in_specs=[row_spec, vec_spec, row_spec] + ([row_spec] if has_res else []),
        out_specs=[row_spec, vec_spec],
        out_shape=[jax.ShapeDtypeStruct((rows, d), F32), jax.ShapeDtypeStruct((1, d), F32)],
        compiler_params=_params(("arbitrary",)),
    )(*ins)


def _loss_head_epilogue(acc, hv, tgtv, gv):
    xv = acc + hv
    r = lax.rsqrt(jnp.mean(xv * xv, axis=-1, keepdims=True) + RMS_EPS)
    xhat = xv * r
    err = xhat * gv - tgtv
    dyv = err * (1.0 / D_MODEL)
    dyg = dyv * gv
    dh = r * (dyg - xhat * jnp.mean(dyg * xhat, axis=-1, keepdims=True))
    return dh, jnp.sum(dyv * xhat, axis=0, keepdims=True), jnp.sum(err * err, axis=0, keepdims=True)


def _to_scan_layout(v):
    lead = v.shape[:-2]
    v = v.reshape(lead + (2, N_STATES // SCAN_CB, SCAN_CB))
    v = jnp.swapaxes(v, -3, -2)
    return v.reshape(lead + (2 * N_STATES,))


def _ssm_matrices(lam_re, lam_im, log_dt, b_re, b_im, c_re, c_im):
    dt = jnp.exp(log_dt)[:, None]
    mag = jnp.exp(lam_re * dt)
    a_re, a_im = mag * jnp.cos(lam_im * dt), mag * jnp.sin(lam_im * dt)
    nr, ni = a_re - 1.0, a_im
    den = lam_re * lam_re + lam_im * lam_im
    coef_re = (nr * lam_re + ni * lam_im) / den
    coef_im = (ni * lam_re - nr * lam_im) / den
    bb_re = coef_re[..., None] * b_re - coef_im[..., None] * b_im
    bb_im = coef_re[..., None] * b_im + coef_im[..., None] * b_re
    a_lay = _to_scan_layout(jnp.stack([a_re.reshape(-1), a_im.reshape(-1)], axis=0))[None, :]
    nblk = SSM_GROUPS // SCAN_GROUPS
    eye = jnp.eye(SCAN_GROUPS, dtype=F32)

    def b_block(bb):
        bb = bb.reshape(nblk, SCAN_GROUPS, SSM_STATE, SSM_GROUP_SIZE)
        return jnp.einsum("gk,jkph->jghkp", eye, bb).reshape(nblk, SCAN_GROUPS * SSM_GROUP_SIZE, SCAN_CB)

    b_blk = jnp.concatenate([b_block(bb_re), b_block(bb_im)], axis=2)

    def c_block(cc):
        cc = cc.reshape(nblk, SCAN_GROUPS, SSM_GROUP_SIZE, SSM_STATE)
        return jnp.einsum("gk,jghp->jkpgh", eye, cc).reshape(nblk, SCAN_CB, SCAN_GROUPS * SSM_GROUP_SIZE)

    c_blk = jnp.concatenate([c_block(c_re), -c_block(c_im)], axis=1)
    return a_lay, b_blk, c_blk


def _interleave(v):
    rows, c = v.shape
    return v.reshape(SCAN_SEGS, rows // SCAN_SEGS, c).transpose(1, 0, 2).reshape(rows, c)


def _deinterleave(v):
    rows, c = v.shape
    return v.reshape(rows // SCAN_SEGS, SCAN_SEGS, c).transpose(1, 0, 2).reshape(rows, c)


def _scan_groups(a_ref, bu_ref, o_ref, state, *, reverse, tt):
    cb = SCAN_CB
    ar = jnp.broadcast_to(a_ref[:, :cb], (SCAN_SEGS, cb))
    ai = jnp.broadcast_to(a_ref[:, cb:], (SCAN_SEGS, cb))
    ngroups = tt // SCAN_SEGS

    def step(i, st):
        sr, si = st
        r0 = pl.multiple_of(((ngroups - 1 - i) if reverse else i) * SCAN_SEGS, SCAN_SEGS)
        blk = bu_ref[pl.ds(r0, SCAN_SEGS), :]
        nr = ar * sr - ai * si + blk[:, :cb]
        ni = ar * si + ai * sr + blk[:, cb:]
        if o_ref is not None:
            o_ref[pl.ds(r0, SCAN_SEGS), :] = jnp.concatenate([nr, ni], axis=1)
        return nr, ni

    return lax.fori_loop(0, ngroups, step, state, unroll=4)


def _segment_entries(a_ref, e_ref, init_ref, *, reverse, seg_len):
    cb = SCAN_CB
    n_sq = seg_len.bit_length() - 1
    assert 1 << n_sq == seg_len, seg_len
    pr, pi = a_ref[:, :cb], a_ref[:, cb:]
    for _ in range(n_sq):
        pr, pi = pr * pr - pi * pi, 2.0 * pr * pi
    cr = jnp.zeros((1, cb), F32)
    ci = jnp.zeros((1, cb), F32)
    order = range(SCAN_SEGS - 1, -1, -1) if reverse else range(SCAN_SEGS)
    for k, seg in enumerate(order):
        if k > 0:
            prev = seg + 1 if reverse else seg - 1
            er, ei = e_ref[prev:prev + 1, :cb], e_ref[prev:prev + 1, cb:]
            cr, ci = pr * cr - pi * ci + er, pr * ci + pi * cr + ei
        init_ref[seg:seg + 1, :] = jnp.concatenate([cr, ci], axis=1)


def _ssm_specs(nt, tt, nch, reverse):
    cb = SCAN_CB
    tmap = (lambda j, kk: (nt - 1 - kk, j)) if reverse else (lambda j, kk: (kk, j))
    return dict(a=pl.BlockSpec((1, 2 * cb), lambda j, kk: (0, j)),
                seg=pl.BlockSpec((SCAN_SEGS, 2 * cb), lambda j, kk: (0, j)),
                chan=pl.BlockSpec((tt, nch), tmap),
                state=pl.BlockSpec((tt, 2 * cb), tmap),
                b=pl.BlockSpec((None, nch, 2 * cb), lambda j, kk: (j, 0, 0)),
                c=pl.BlockSpec((None, 2 * cb, nch), lambda j, kk: (j, 0, 0)))


def _ssm_ends(a_lay, x, blocks, *, transpose, reverse, tt, name):
    rows = x.shape[0]
    nblk = blocks.shape[0]
    nch = x.shape[1] // nblk
    cb = SCAN_CB
    nt = rows // tt
    sp = _ssm_specs(nt, tt, nch, reverse)

    def body(a_ref, x_ref, w_ref, e_ref, bu_ref):
        kk = pl.program_id(1)

        @pl.when(kk == 0)
        def _():
            e_ref[...] = jnp.zeros_like(e_ref)

        bu_ref[...] = _dot(x_ref[...].astype(BF16), w_ref[...].astype(BF16), 1, 1 if transpose else 0)
        sr, si = _scan_groups(a_ref, bu_ref, None, (e_ref[:, :cb], e_ref[:, cb:]), reverse=reverse, tt=tt)
        e_ref[...] = jnp.concatenate([sr, si], axis=1)

    return pl.pallas_call(
        body, name=name, grid=(nblk, nt),
        in_specs=[sp["a"], sp["chan"], sp["c"] if transpose else sp["b"]],
        out_specs=sp["seg"],
        out_shape=jax.ShapeDtypeStruct((SCAN_SEGS, nblk * 2 * cb), F32),
        scratch_shapes=[pltpu.VMEM((tt, 2 * cb), F32)],
        compiler_params=_params(("parallel", "arbitrary")),
    )(a_lay, x, blocks)


def _ssm_fwd(a_lay, u, b_blk, c_blk, ends, *, tt, name):
    rows = u.shape[0]
    nblk = b_blk.shape[0]
    nch = u.shape[1] // nblk
    cb = SCAN_CB
    nt = rows // tt
    sp = _ssm_specs(nt, tt, nch, False)

    def body(a_ref, e_ref, u_ref, b_ref, c_ref, s_ref, y_ref, init_ref, carry_ref):
        kk = pl.program_id(1)

        @pl.when(kk == 0)
        def _():
            _segment_entries(a_ref, e_ref, init_ref, reverse=False, seg_len=rows // SCAN_SEGS)
            carry_ref[...] = init_ref[...]

        s_ref[...] = _dot(u_ref[...].astype(BF16), b_ref[...].astype(BF16), 1, 0)
        sr, si = _scan_groups(a_ref, s_ref, s_ref, (carry_ref[:, :cb], carry_ref[:, cb:]), reverse=False, tt=tt)
        carry_ref[...] = jnp.concatenate([sr, si], axis=1)
        y_ref[...] = _dot(s_ref[...].astype(BF16), c_ref[...].astype(BF16), 1, 0)

    return pl.pallas_call(
        body, name=name, grid=(nblk, nt),
        in_specs=[sp["a"], sp["seg"], sp["chan"], sp["b"], sp["c"]],
        out_specs=[sp["state"], sp["chan"], sp["seg"]],
        out_shape=[jax.ShapeDtypeStruct((rows, nblk * 2 * cb), F32), jax.ShapeDtypeStruct((rows, nblk * nch), F32),
                   jax.ShapeDtypeStruct((SCAN_SEGS, nblk * 2 * cb), F32)],
        scratch_shapes=[pltpu.VMEM((SCAN_SEGS, 2 * cb), F32)],
        compiler_params=_params(("parallel", "arbitrary")),
    )(a_lay, ends, u, b_blk, c_blk)


def _ssm_bwd(a_conj, dy, u, s, s_entry, b_blk, c_blk, dd, ends, *, tt, name):
    rows = u.shape[0]
    nblk = b_blk.shape[0]
    nch = u.shape[1] // nblk
    cb = SCAN_CB
    nt = rows // tt
    sp = _ssm_specs(nt, tt, nch, True)
    groups_per_tile = tt // SCAN_SEGS
    before = pl.BlockSpec((SCAN_SEGS, 2 * cb), lambda j, kk: (jnp.maximum((nt - 1 - kk) * groups_per_tile - 1, 0), j))

    def body(a_ref, e_ref, dy_ref, u_ref, s_ref, before_ref, entry_ref, b_ref, c_ref, dd_ref,
             du_ref, db_ref, dc_ref, da_ref, lam_ref, carry_ref):
        kk = pl.program_id(1)

        @pl.when(kk == 0)
        def _():
            _segment_entries(a_ref, e_ref, carry_ref, reverse=True, seg_len=rows // SCAN_SEGS)
            db_ref[...] = jnp.zeros_like(db_ref)
            dc_ref[...] = jnp.zeros_like(dc_ref)
            da_ref[...] = jnp.zeros_like(da_ref)

        dyv = dy_ref[...]
        dyb = dyv.astype(BF16)
        lam_ref[...] = _dot(dyb, c_ref[...].astype(BF16), 1, 1)
        lr, li = _scan_groups(a_ref, lam_ref, lam_ref, (carry_ref[:, :cb], carry_ref[:, cb:]), reverse=True, tt=tt)
        carry_ref[...] = jnp.concatenate([lr, li], axis=1)

        first = jnp.where(kk == nt - 1, entry_ref[...], before_ref[...])
        rest = tt - SCAN_SEGS
        lam_hi = lam_ref[pl.ds(SCAN_SEGS, rest), :]
        s_lo = s_ref[pl.ds(0, rest), :]
        lam_lo = lam_ref[pl.ds(0, SCAN_SEGS), :]

        def pair(lv, pv):
            lre, lim, pre, pim = lv[:, :cb], lv[:, cb:], pv[:, :cb], pv[:, cb:]
            return (jnp.sum(lre * pre + lim * pim, axis=0, keepdims=True),
                    jnp.sum(lim * pre - lre * pim, axis=0, keepdims=True))

        r1, i1 = pair(lam_hi, s_lo)
        r0, i0 = pair(lam_lo, first)
        da_ref[...] += jnp.concatenate([r1 + r0, i1 + i0], axis=1)

        lamb = lam_ref[...].astype(BF16)
        du_ref[...] = _dot(lamb, b_ref[...].astype(BF16), 1, 1) + dd_ref[...] * dyv
        db_ref[...] += _dot(u_ref[...].astype(BF16), lamb, 0, 0)
        dc_ref[...] += _dot(s_ref[...].astype(BF16), dyb, 0, 0)

    return pl.pallas_call(
        body, name=name, grid=(nblk, nt),
        in_specs=[sp["a"], sp["seg"], sp["chan"], sp["chan"], sp["state"], before, sp["seg"], sp["b"], sp["c"],
                  pl.BlockSpec((1, nch), lambda j, kk: (0, j))],
        out_specs=[sp["chan"], sp["b"], sp["c"], pl.BlockSpec((1, 2 * cb), lambda j, kk: (0, j))],
        out_shape=[jax.ShapeDtypeStruct((rows, nblk * nch), F32), jax.ShapeDtypeStruct(b_blk.shape, F32),
                   jax.ShapeDtypeStruct(c_blk.shape, F32), jax.ShapeDtypeStruct((1, nblk * 2 * cb), F32)],
        scratch_shapes=[pltpu.VMEM((tt, 2 * cb), F32), pltpu.VMEM((SCAN_SEGS, 2 * cb), F32)],
        compiler_params=_params(("parallel", "arbitrary")),
    )(a_conj, ends, dy, u, s, s, s_entry, b_blk, c_blk, dd)


def _glu_fwd(ys, u, dd, w_glu, b_glu, *, tm, name):
    rows, w = ys.shape

    def body(ys_ref, u_ref, dd_ref, w_ref, b_ref, y0_ref, t_ref, y2_ref):
        y0 = ys_ref[...] + dd_ref[...] * u_ref[...]
        y1 = _gelu(y0)
        t = _dot(y1.astype(BF16), w_ref[...], 1, 0) + b_ref[...]
        y0_ref[...] = y0
        t_ref[...] = t
        y2_ref[...] = (y1 * _sigmoid(t)).astype(BF16)

    row = pl.BlockSpec((tm, w), lambda i: (i, 0))
    vec = pl.BlockSpec((1, w), lambda i: (0, 0))
    return pl.pallas_call(
        body, name=name, grid=(rows // tm,),
        in_specs=[row, row, vec, pl.BlockSpec((w, w), lambda i: (0, 0)), vec],
        out_specs=[row, row, row],
        out_shape=[jax.ShapeDtypeStruct((rows, w), F32), jax.ShapeDtypeStruct((rows, w), F32),
                   jax.ShapeDtypeStruct((rows, w), BF16)],
        compiler_params=_params(("parallel",)),
    )(ys, u, dd, w_glu, b_glu)


def _glu_bwd(dy2, y0, t, u, w_glu, *, tm, name):
    rows, w = y0.shape

    def body(dy2_ref, y0_ref, t_ref, u_ref, w_ref, dy0_ref, dt_ref, y1_ref, db_ref, dd_ref):
        i = pl.program_id(0)
        y0 = y0_ref[...]
        y1 = _gelu(y0)
        sg = _sigmoid(t_ref[...])
        dy2v = dy2_ref[...]
        dt = dy2v * y1 * sg * (1.0 - sg)
        dy1 = dy2v * sg + _dot(dt.astype(BF16), w_ref[...], 1, 1)
        dy0 = dy1 * _gelu_grad(y0)
        dy0_ref[...] = dy0
        dt_ref[...] = dt.astype(BF16)
        y1_ref[...] = y1.astype(BF16)

        @pl.when(i == 0)
        def _():
            db_ref[...] = jnp.zeros_like(db_ref)
            dd_ref[...] = jnp.zeros_like(dd_ref)

        db_ref[...] += jnp.sum(dt, axis=0, keepdims=True)
        dd_ref[...] += jnp.sum(dy0 * u_ref[...], axis=0, keepdims=True)

    row = pl.BlockSpec((tm, w), lambda i: (i, 0))
    vec = pl.BlockSpec((1, w), lambda i: (0, 0))
    return pl.pallas_call(
        body, name=name, grid=(rows // tm,),
        in_specs=[row, row, row, row, pl.BlockSpec((w, w), lambda i: (0, 0))],
        out_specs=[row, row, row, vec, vec],
        out_shape=[jax.ShapeDtypeStruct((rows, w), F32), jax.ShapeDtypeStruct((rows, w), BF16),
                   jax.ShapeDtypeStruct((rows, w), BF16), jax.ShapeDtypeStruct((1, w), F32),
                   jax.ShapeDtypeStruct((1, w), F32)],
        compiler_params=_params(("arbitrary",)),
    )(dy2, y0, t, u, w_glu)


ATTN_TILE = 2048


def _attn_geometry(rows, d):
    sb = ATTN_Q * d
    tr = max(sb, min(ATTN_TILE, rows))
    assert rows % tr == 0 and tr % sb == 0, (rows, d)
    return sb, tr, rows // tr, tr // sb


def _attn_masks():
    qi = lax.broadcasted_iota(jnp.int32, (2 * ATTN_Q, 2 * ATTN_Q), 0) % ATTN_Q
    kj = lax.broadcasted_iota(jnp.int32, (2 * ATTN_Q, 2 * ATTN_Q), 1)
    own_ok = jnp.logical_and(kj >= ATTN_Q, kj - ATTN_Q <= qi)
    prev_ok = jnp.logical_and(kj < ATTN_Q, kj >= qi)
    bias_first = jnp.where(own_ok, 0.0, NEG_INF)
    bias_other = jnp.where(jnp.logical_or(own_ok, prev_ok), 0.0, NEG_INF)
    head0 = lax.broadcasted_iota(jnp.int32, (ATTN_Q, LANES), 1) < ATTN_HEAD_DIM
    return bias_first, bias_other, head0


def _attn_rows(base, n, d):
    return pl.ds(pl.multiple_of(base, ATTN_Q), n) if d == 1 else pl.ds(base, n, stride=d)


def _stack_heads(v, head0):
    return jnp.concatenate([jnp.where(head0, v, 0.0), jnp.where(head0, 0.0, v)], axis=0)


def _unstack_heads(v, head0):
    return jnp.where(head0, v[:ATTN_Q], v[ATTN_Q:])


def _fill_keys(buf, prev_ref, cur_ref, sb):
    buf[pl.ds(0, sb), :] = prev_ref[...]
    buf[pl.ds(sb, cur_ref.shape[0]), :] = cur_ref[...]


def _attn_fwd(qkv, g, d, *, name):
    rows = qkv.shape[0]
    sb, tr, ntiles, nsub = _attn_geometry(rows, d)
    qc, kc, vc = 2 * g, 6 + 2 * g, 12 + 2 * g
    scale = ATTN_HEAD_DIM ** -0.5

    def body(q_ref, kc_ref, kp_ref, vc_ref, vp_ref, o_ref, lse_ref, kbuf, vbuf):
        n = pl.program_id(0)
        _fill_keys(kbuf, kp_ref, kc_ref, sb)
        _fill_keys(vbuf, vp_ref, vc_ref, sb)
        bias_first, bias_other, head0 = _attn_masks()

        def per_block(idx, carry):
            j, r = idx // d, idx % d
            base = j * sb + r
            bias = jnp.where(jnp.logical_and(n == 0, j == 0), bias_first, bias_other)
            qrows = _attn_rows(base, ATTN_Q, d)
            krows = _attn_rows(base, 2 * ATTN_Q, d)
            qs = (_stack_heads(q_ref[qrows, :], head0) * scale).astype(BF16)
            s = _dot(qs, kbuf[krows, :].astype(BF16), 1, 1) + bias
            mx = jnp.max(s, axis=-1, keepdims=True)
            p = jnp.exp(s - mx)
            den = jnp.sum(p, axis=-1, keepdims=True)
            pv = _dot(p.astype(BF16), vbuf[krows, :].astype(BF16), 1, 0) / den
            o_ref[qrows, :] = _unstack_heads(pv, head0)
            lse_ref[qrows, :] = _unstack_heads(jnp.broadcast_to(mx + jnp.log(den), (2 * ATTN_Q, LANES)), head0)
            return carry

        lax.fori_loop(0, nsub * d, per_block, 0, unroll=8)

    def cur(col):
        return pl.BlockSpec((tr, LANES), lambda n, hp: (n, col + hp))

    def prev(col):
        return pl.BlockSpec((sb, LANES), lambda n, hp: (jnp.maximum(n * nsub - 1, 0), col + hp))

    out_spec = pl.BlockSpec((tr, LANES), lambda n, hp: (n, hp))
    return pl.pallas_call(
        body, name=name, grid=(ntiles, 2),
        in_specs=[cur(qc), cur(kc), prev(kc), cur(vc), prev(vc)],
        out_specs=[out_spec, out_spec],
        out_shape=[jax.ShapeDtypeStruct((rows, 2 * LANES), F32), jax.ShapeDtypeStruct((rows, 2 * LANES), F32)],
        scratch_shapes=[pltpu.VMEM((sb + tr, LANES), F32), pltpu.VMEM((sb + tr, LANES), F32)],
        compiler_params=_params(("parallel", "parallel")),
    )(qkv, qkv, qkv, qkv, qkv)


def _attn_merge(outs, lses, *, tm, name):
    rows, w = outs[0].shape

    def body(o0, o1, o2, l0, l1, l2, o_ref, lse_ref):
        a0, a1, a2 = l0[...], l1[...], l2[...]
        mx = jnp.maximum(jnp.maximum(a0, a1), a2)
        e0, e1, e2 = jnp.exp(a0 - mx), jnp.exp(a1 - mx), jnp.exp(a2 - mx)
        den = e0 + e1 + e2
        o_ref[...] = (e0 / den) * o0[...] + (e1 / den) * o1[...] + (e2 / den) * o2[...]
        lse_ref[...] = mx + jnp.log(den)

    row = pl.BlockSpec((tm, w), lambda i: (i, 0))
    return pl.pallas_call(
        body, name=name, grid=(rows // tm,), in_specs=[row] * 6, out_specs=[row, row],
        out_shape=[jax.ShapeDtypeStruct((rows, w), F32), jax.ShapeDtypeStruct((rows, w), F32)],
        compiler_params=_params(("parallel",)),
    )(*outs, *lses)


def _attn_bwd(qkv, do, o, lse, g, d, prev, *, name):
    rows = qkv.shape[0]
    sb, tr, ntiles, nsub = _attn_geometry(rows, d)
    qc, kc, vc = 2 * g, 6 + 2 * g, 12 + 2 * g
    scale = ATTN_HEAD_DIM ** -0.5

    def body(q_ref, kc_ref, kp_ref, vc_ref, vp_ref, do_ref, o_ref, lse_ref, dq_ref, dk_ref, dv_ref,
             kbuf, vbuf, dk_acc, dv_acc):
        n = pl.program_id(1)

        @pl.when(n == 0)
        def _():
            dk_acc[pl.ds(0, tr), :] = jnp.zeros((tr, LANES), F32)
            dv_acc[pl.ds(0, tr), :] = jnp.zeros((tr, LANES), F32)

        @pl.when(n < ntiles)
        def _():
            dk_acc[pl.ds(tr, tr), :] = jnp.zeros((tr, LANES), F32)
            dv_acc[pl.ds(tr, tr), :] = jnp.zeros((tr, LANES), F32)
            _fill_keys(kbuf, kp_ref, kc_ref, sb)
            _fill_keys(vbuf, vp_ref, vc_ref, sb)
            bias_first, bias_other, head0 = _attn_masks()
            lane = lax.broadcasted_iota(jnp.int32, (ATTN_Q, LANES), 1)

            def per_block(idx, carry):
                j, r = idx // d, idx % d
                base = j * sb + r
                bias = jnp.where(jnp.logical_and(n == 0, j == 0), bias_first, bias_other)
                qrows = _attn_rows(base, ATTN_Q, d)
                krows = _attn_rows(base, 2 * ATTN_Q, d)
                arows = _attn_rows(base + (tr - sb), 2 * ATTN_Q, d)
                qs = (_stack_heads(q_ref[qrows, :], head0) * scale).astype(BF16)
                dos = _stack_heads(do_ref[qrows, :], head0)
                dosb = dos.astype(BF16)
                ov = o_ref[qrows, :]
                delta = jnp.sum(dos * jnp.concatenate([ov, ov], axis=0), axis=-1, keepdims=True)
                lsev = lse_ref[qrows, :]
                lse_s = jnp.concatenate(
                    [jnp.sum(jnp.where(lane == h * ATTN_HEAD_DIM, lsev, 0.0), axis=-1, keepdims=True) for h in range(2)], axis=0)
                kb = kbuf[krows, :].astype(BF16)
                vb = vbuf[krows, :].astype(BF16)
                p = jnp.exp(_dot(qs, kb, 1, 1) + bias - lse_s)
                ds = (p * (_dot(dosb, vb, 1, 1) - delta)).astype(BF16)
                dq_ref[qrows, :] = _unstack_heads(_dot(ds, kb, 1, 0), head0) * scale
                dk_acc[arows, :] += _dot(ds, qs, 0, 0)
                dv_acc[arows, :] += _dot(p.astype(BF16), dosb, 0, 0)
                return carry

            lax.fori_loop(0, nsub * d, per_block, 0, unroll=4)

        dk_ref[...] = dk_acc[pl.ds(0, tr), :]
        dv_ref[...] = dv_acc[pl.ds(0, tr), :]
        dk_acc[pl.ds(0, tr), :] = dk_acc[pl.ds(tr, tr), :]
        dv_acc[pl.ds(0, tr), :] = dv_acc[pl.ds(tr, tr), :]

    def cur(n):
        return jnp.minimum(n, ntiles - 1)

    def spec(col, prev):
        if prev:
            return pl.BlockSpec((sb, LANES), lambda hp, n: (jnp.maximum(cur(n) * nsub - 1, 0), col + hp))
        return pl.BlockSpec((tr, LANES), lambda hp, n: (cur(n), col + hp))

    row_spec = pl.BlockSpec((tr, LANES), lambda hp, n: (cur(n), hp))
    dq_out = pl.BlockSpec((tr, LANES), lambda hp, n: (cur(n), 2 * g + hp))
    kv_out = pl.BlockSpec((tr, LANES), lambda hp, n: (jnp.maximum(n - 1, 0), 2 * g + hp))
    shape = jax.ShapeDtypeStruct((rows, len(ATTN_PATTERNS) * 2 * LANES), F32)
    ins = [qkv, qkv, qkv, qkv, qkv, do, o, lse]
    in_specs = [spec(qc, False), spec(kc, False), spec(kc, True), spec(vc, False), spec(vc, True),
                row_spec, row_spec, row_spec]
    aliases = {}
    if prev is not None:
        aliases = {len(ins) + t: t for t in range(3)}
        ins = ins + list(prev)
        in_specs = in_specs + [ANY] * 3
    n_in = len(ins)

    def entry(*refs):
        body(*refs[:8], *refs[n_in:])

    return pl.pallas_call(
        entry, name=name, grid=(2, ntiles + 1),
        in_specs=in_specs,
        out_specs=[dq_out, kv_out, kv_out],
        out_shape=[shape, shape, shape],
        input_output_aliases=aliases,
        scratch_shapes=[pltpu.VMEM((sb + tr, LANES), F32), pltpu.VMEM((sb + tr, LANES), F32),
                        pltpu.VMEM((2 * tr, LANES), F32), pltpu.VMEM((2 * tr, LANES), F32)],
        compiler_params=_params(("parallel", "arbitrary")),
    )(*ins)


def _mem_probs(q, k):
    s = _dot(q.astype(BF16), k.astype(BF16), 1, 1) * (MEM_HEAD_DIM ** -0.5)
    e = jnp.exp(s - jnp.max(s, axis=-1, keepdims=True))
    return e / jnp.sum(e, axis=-1, keepdims=True)


def _mem_attn_fwd(mq, kv, *, tq, name):
    rows = mq.shape[0]

    def body(q_ref, k_ref, v_ref, o_ref):
        p = _mem_probs(q_ref[...], k_ref[...])
        o_ref[...] = _dot(p.astype(BF16), v_ref[...].astype(BF16), 1, 0)

    return pl.pallas_call(
        body, name=name, grid=(rows // tq, MEM_HEADS),
        in_specs=[pl.BlockSpec((tq, LANES), lambda i, h: (i, h)),
                  pl.BlockSpec((MEM_LEN, LANES), lambda i, h: (0, h)),
                  pl.BlockSpec((MEM_LEN, LANES), lambda i, h: (0, MEM_HEADS + h))],
        out_specs=pl.BlockSpec((tq, LANES), lambda i, h: (i, h)),
        out_shape=jax.ShapeDtypeStruct((rows, MEM_HEADS * LANES), F32),
        compiler_params=_params(("parallel", "parallel")),
    )(mq, kv, kv)


def _mem_attn_bwd(mq, kv, dmo, *, tq, name):
    rows = mq.shape[0]
    scale = MEM_HEAD_DIM ** -0.5

    def body(q_ref, k_ref, v_ref, do_ref, dq_ref, dk_ref, dv_ref):
        i = pl.program_id(1)
        qb = q_ref[...].astype(BF16)
        kb = k_ref[...].astype(BF16)
        vb = v_ref[...].astype(BF16)
        dob = do_ref[...].astype(BF16)
        p = _mem_probs(q_ref[...], k_ref[...])
        dp = _dot(dob, vb, 1, 1)
        ds = (p * (dp - jnp.sum(p * dp, axis=-1, keepdims=True)) * scale).astype(BF16)
        dq_ref[...] = _dot(ds, kb, 1, 0).astype(dq_ref.dtype)

        @pl.when(i == 0)
        def _():
            dk_ref[...] = jnp.zeros_like(dk_ref)
            dv_ref[...] = jnp.zeros_like(dv_ref)

        dk_ref[...] += _dot(ds, qb, 0, 0)
        dv_ref[...] += _dot(p.astype(BF16), dob, 0, 0)

    kv_out = pl.BlockSpec((MEM_LEN, LANES), lambda h, i: (0, h))
    kv_shape = jax.ShapeDtypeStruct((MEM_LEN, MEM_HEADS * LANES), F32)
    return pl.pallas_call(
        body, name=name, grid=(MEM_HEADS, rows // tq),
        in_specs=[pl.BlockSpec((tq, LANES), lambda h, i: (i, h)),
                  pl.BlockSpec((MEM_LEN, LANES), lambda h, i: (0, h)),
                  pl.BlockSpec((MEM_LEN, LANES), lambda h, i: (0, MEM_HEADS + h)),
                  pl.BlockSpec((tq, LANES), lambda h, i: (i, h))],
        out_specs=[pl.BlockSpec((tq, LANES), lambda h, i: (i, h)), kv_out, kv_out],
        out_shape=[jax.ShapeDtypeStruct((rows, MEM_HEADS * LANES), BF16), kv_shape, kv_shape],
        compiler_params=_params(("parallel", "arbitrary")),
    )(mq, kv, kv, dmo)


def _resident(shape):
    return pl.BlockSpec(shape, lambda i: (0, 0), pipeline_mode=pl.Buffered(1))


def _branch_merge_fwd(acts, wts, zg, b_gate, *, tm, name):
    rows = zg.shape[0]
    d = wts[0].shape[0]

    def body(s_ref, a_ref, m_ref, ws_ref, wa_ref, wm_ref, zg_ref, b_ref, o_ref):
        gt = _sigmoid(zg_ref[...] + b_ref[...])
        acc = None
        for k, (x_ref, w_ref) in enumerate(((s_ref, ws_ref), (a_ref, wa_ref), (m_ref, wm_ref))):
            term = gt[:, k * d:(k + 1) * d] * _dot(x_ref[...].astype(BF16), w_ref[...], 1, 1)
            acc = term if acc is None else acc + term
        o_ref[...] = acc.astype(BF16)

    return pl.pallas_call(
        body, name=name, grid=(rows // tm,),
        in_specs=[pl.BlockSpec((tm, x.shape[1]), lambda i: (i, 0)) for x in acts] + [_resident(w.shape) for w in wts]
        + [pl.BlockSpec((tm, 3 * d), lambda i: (i, 0)), pl.BlockSpec((1, 3 * d), lambda i: (0, 0))],
        out_specs=pl.BlockSpec((tm, d), lambda i: (i, 0)), out_shape=jax.ShapeDtypeStruct((rows, d), BF16),
        compiler_params=_params(("parallel",)),
    )(*acts, *wts, zg, b_gate)


def _branch_merge_bwd(dmerged, acts, wts, zg, b_gate, *, tm, name, carry=None):
    rows = zg.shape[0]
    d = wts[0].shape[0]

    def body(dm_ref, s_ref, a_ref, m_ref, ws_ref, wa_ref, wm_ref, zg_ref, b_ref,
             ds_ref, da_ref, dmm_ref, dws_ref, dwa_ref, dwm_ref, dzg_ref, db_ref):
        i = pl.program_id(0)

        @pl.when(i == 0)
        def _():
            for r in (dws_ref, dwa_ref, dwm_ref, db_ref):
                r[...] = jnp.zeros_like(r)

        gt = _sigmoid(zg_ref[...] + b_ref[...])
        dm = dm_ref[...]
        groups = ((s_ref, ws_ref, ds_ref, dws_ref), (a_ref, wa_ref, da_ref, dwa_ref), (m_ref, wm_ref, dmm_ref, dwm_ref))
        for k, (x_ref, w_ref, dx_ref, dw_ref) in enumerate(groups):
            cs = pl.ds(k * d, d)
            gk = gt[:, k * d:(k + 1) * d]
            xb = x_ref[...].astype(BF16)
            br = _dot(xb, w_ref[...], 1, 1)
            dbr = (dm * gk).astype(BF16)
            dx_ref[...] = _dot(dbr, w_ref[...], 1, 0)
            dw_ref[...] += _dot(dbr, xb, 0, 0)
            dzg = dm * br * gk * (1.0 - gk)
            dzg_ref[:, cs] = dzg.astype(BF16)
            db_ref[:, cs] += jnp.sum(dzg, axis=0, keepdims=True)

    row = lambda w: pl.BlockSpec((tm, w), lambda i: (i, 0))
    whole = lambda shape: pl.BlockSpec(shape, lambda i: (0, 0))
    res = _call_with_carry(
        body, carry, name=name, grid=(rows // tm,),
        in_specs=[row(d)] + [row(x.shape[1]) for x in acts] + [_resident(w.shape) for w in wts] + [row(3 * d), whole((1, 3 * d))],
        out_specs=[row(x.shape[1]) for x in acts] + [whole(w.shape) for w in wts] + [row(3 * d), whole((1, 3 * d))],
        out_shape=[jax.ShapeDtypeStruct(x.shape, F32) for x in acts] + [jax.ShapeDtypeStruct(w.shape, F32) for w in wts]
        + [jax.ShapeDtypeStruct((rows, 3 * d), BF16), jax.ShapeDtypeStruct((1, 3 * d), F32)],
        scratch=[], operands=[dmerged, *acts, *wts, zg, b_gate], semantics=("arbitrary",))
    return tuple(res) if carry is None else (tuple(res[:8]), list(res[8:]))


def _adamw(w, g, m, v, *, tr, name):
    rows, cols = w.shape[-2:]
    assert rows % tr == 0, (name, rows, tr)

    def body(w_ref, g_ref, m_ref, v_ref, g_out, d_ref, nm_ref, nv_ref):
        gv = g_ref[...]
        m2 = ADAM_B1 * m_ref[...] + (1.0 - ADAM_B1) * gv
        v2 = ADAM_B2 * v_ref[...] + (1.0 - ADAM_B2) * (gv * gv)
        m_hat = m2 / (1.0 - ADAM_B1 ** ADAM_STEP)
        v_hat = v2 / (1.0 - ADAM_B2 ** ADAM_STEP)
        g_out[...] = gv
        d_ref[...] = -ADAM_LR * (m_hat / (jnp.sqrt(v_hat) + ADAM_EPS) + ADAM_WD * w_ref[...])
        nm_ref[...] = m2
        nv_ref[...] = v2

    flat = pl.BlockSpec((tr, cols), lambda i: (i, 0))
    blk = flat if w.ndim == 2 else pl.BlockSpec((None, tr, cols), lambda i: (0, i, 0))
    shape = jax.ShapeDtypeStruct(w.shape, F32)
    return pl.pallas_call(
        body, name=name, grid=(rows // tr,), in_specs=[blk, flat, blk, blk], out_specs=[blk] * 4,
        out_shape=[shape] * 4, compiler_params=_params(("parallel",)),
    )(w, g, m, v)


ANY = pl.BlockSpec(memory_space=pl.ANY)


def _position():
    return lax.axis_index("x"), lax.axis_index("y"), lax.axis_index("c")


def _other_chips(x, y):
    return ((1 - x, y), (x, 1 - y), (1 - x, 1 - y))


def _remote(src, dst, send_sem, recv_sem, dev):
    return pltpu.make_async_remote_copy(src_ref=src, dst_ref=dst, send_sem=send_sem, recv_sem=recv_sem,
                                        device_id=dev, device_id_type=MESH)


def _gather_exchange(shards):
    nb = len(shards)

    def rows_of(i, owner, core):
        rs = shards[i].shape[0]
        return pl.ds(pl.multiple_of(owner * rs + core * (rs // 2), 16), rs // 2)

    def first_leg(ins, outs, send_sems, recv_sems, i, j):
        x, y, c = _position()
        px, py = _other_chips(x, y)[j]
        half = shards[i].shape[0] // 2
        mine = ins[i].at[pl.ds(pl.multiple_of(c * half, 16), half)]
        return _remote(mine, outs[i].at[rows_of(i, 2 * x + y, c)], send_sems.at[i, j], recv_sems.at[i, j], (px, py, c))

    def passed_on(outs, send_sems, recv_sems, i, j, core):
        x, y, c = _position()
        px, py = _other_chips(x, y)[j]
        rows = outs[i].at[rows_of(i, 2 * px + py, core)]
        return _remote(rows, rows, send_sems.at[i, 3 + j], recv_sems.at[i, 3 + j], (x, y, 1 - c))

    def own_block(ins, outs, send_sems, recv_sems, i):
        x, y, c = _position()
        rs = shards[i].shape[0]
        place = outs[i].at[pl.ds(pl.multiple_of((2 * x + y) * rs, 16), rs)]
        return _remote(ins[i], place, send_sems.at[i, 6], recv_sems.at[i, 6], (x, y, 1 - c))

    def start(ins, outs, send_sems, recv_sems):
        for i in range(nb):
            own_block(ins, outs, send_sems, recv_sems, i).start()
            for j in range(3):
                first_leg(ins, outs, send_sems, recv_sems, i, j).start()

    def finish(ins, outs, send_sems, recv_sems):
        x, y, c = _position()
        for i in range(nb):
            for j, (px, py) in enumerate(_other_chips(x, y)):
                landed = outs[i].at[rows_of(i, 2 * px + py, c)]
                _remote(landed, landed, send_sems.at[i, j], recv_sems.at[i, j], (px, py, c)).wait_recv()
                passed_on(outs, send_sems, recv_sems, i, j, c).start()
        for i in range(nb):
            own_block(ins, outs, send_sems, recv_sems, i).wait()
            for j in range(3):
                passed_on(outs, send_sems, recv_sems, i, j, 1 - c).wait_recv()
        for i in range(nb):
            for j in range(3):
                first_leg(ins, outs, send_sems, recv_sems, i, j).wait_send()
                passed_on(outs, send_sems, recv_sems, i, j, c).wait_send()

    return _Exchange(ins=list(shards), outs=[jax.ShapeDtypeStruct((N_CHIPS * s.shape[0], s.shape[1]), s.dtype) for s in shards],
                     aliases={}, sems=[(nb, 7), (nb, 7)], start=start, finish=finish)


def _run_exchange(ex, *, name):
    n_in, n_out = len(ex.ins), len(ex.outs)

    def body(*refs):
        c_in, c_out, sems = refs[:n_in], refs[n_in:n_in + n_out], refs[n_in + n_out:]
        ex.start(c_in, c_out, *sems)
        ex.finish(c_in, c_out, *sems)

    return pl.pallas_call(
        body, name=name, in_specs=[ANY] * n_in, out_specs=[ANY] * n_out, out_shape=list(ex.outs),
        input_output_aliases=dict(ex.aliases),
        scratch_shapes=[pltpu.SemaphoreType.DMA(s) for s in ex.sems],
    )(*ex.ins)


def _row_tile(rows):
    return max(t for t in range(16, min(rows, 512) + 1, 16) if rows % t == 0)


def _halves_exchange(grads):
    nb = len(grads)

    def copies(ins, outs, send_sems, recv_sems):
        x, y, c = _position()
        return [_remote(ins[i].at[:, 1 - c], outs[i], send_sems.at[i], recv_sems.at[i], (x, y, 1 - c)) for i in range(nb)]

    def start(ins, outs, send_sems, recv_sems):
        for cp in copies(ins, outs, send_sems, recv_sems):
            cp.start()

    def finish(ins, outs, send_sems, recv_sems):
        for cp in copies(ins, outs, send_sems, recv_sems):
            cp.wait()

    return _Exchange(ins=list(grads), outs=[jax.ShapeDtypeStruct((N_CHIPS, g.shape[2], g.shape[3]), F32) for g in grads],
                     aliases={}, sems=[(nb,), (nb,)], start=start, finish=finish)


def _join_exchanges(parts):
    assert all(not ex.aliases for ex in parts)

    def split(refs, counts):
        out, at = [], 0
        for k in counts:
            out.append(refs[at:at + k])
            at += k
        return out

    def run(which):
        def go(ins, outs, *sems):
            for ex, i, o, s in zip(parts, split(ins, [len(ex.ins) for ex in parts]), split(outs, [len(ex.outs) for ex in parts]),
                                   split(sems, [len(ex.sems) for ex in parts])):
                getattr(ex, which)(i, o, *s)
        return go

    return _Exchange(ins=[a for ex in parts for a in ex.ins], outs=[a for ex in parts for a in ex.outs], aliases={},
                     sems=[s for ex in parts for s in ex.sems], start=run("start"), finish=run("finish"))


def _pair_sum(g4, got, c_arr, *, name):
    _, _, half, cols = g4.shape
    tr = _row_tile(half)

    def body(c_ref, g_ref, t_ref, p_ref, pb_ref):
        sm = g_ref[...] + t_ref[...]
        p_ref[...] = sm
        pb_ref[...] = sm.astype(BF16)

    blk = pl.BlockSpec((None, tr, cols), lambda j, i, c_ref: (j, i, 0))
    grid_spec = pltpu.PrefetchScalarGridSpec(
        num_scalar_prefetch=1, grid=(N_CHIPS, half // tr),
        in_specs=[pl.BlockSpec((None, None, tr, cols), lambda j, i, c_ref: (j, c_ref[0], i, 0)), blk],
        out_specs=[blk, blk])
    return pl.pallas_call(
        body, name=name, grid_spec=grid_spec,
        out_shape=[jax.ShapeDtypeStruct((N_CHIPS, half, cols), F32), jax.ShapeDtypeStruct((N_CHIPS, half, cols), BF16)],
        compiler_params=_params(("parallel", "parallel")),
    )(c_arr, g4, got)


def _scatter_exchange(parts):
    nb = len(parts)

    def copies(ins, outs, send_sems, recv_sems):
        x, y, c = _position()
        return [_remote(ins[i].at[2 * px + py], outs[i].at[j], send_sems.at[i, j], recv_sems.at[i, j], (px, py, c))
                for i in range(nb) for j, (px, py) in enumerate(_other_chips(x, y))]

    def start(ins, outs, send_sems, recv_sems):
        for cp in copies(ins, outs, send_sems, recv_sems):
            cp.start()

    def finish(ins, outs, send_sems, recv_sems):
        for cp in copies(ins, outs, send_sems, recv_sems):
            cp.wait()

    return _Exchange(ins=list(parts), outs=[jax.ShapeDtypeStruct((3,) + p.shape[1:], p.dtype) for p in parts],
                     aliases={}, sems=[(nb, 3), (nb, 3)], start=start, finish=finish)


def _owner_sum(p, got, chip_arr, c_arr, *, replicated, name):
    _, half, cols = p.shape
    tr = _row_tile(half)

    def body(chip_ref, c_ref, p_ref, r_ref, o_ref):
        o_ref[...] = ((p_ref[...] + r_ref[0].astype(F32)) + r_ref[1].astype(F32)) + r_ref[2].astype(F32)

    if replicated:
        out_spec = pl.BlockSpec((None, None, tr, cols), lambda i, chip_ref, c_ref: (chip_ref[0], c_ref[0], i, 0))
        out_shape = jax.ShapeDtypeStruct((N_CHIPS, 2, half, cols), F32)
    else:
        out_spec = pl.BlockSpec((None, tr, cols), lambda i, chip_ref, c_ref: (c_ref[0], i, 0))
        out_shape = jax.ShapeDtypeStruct((2, half, cols), F32)
    grid_spec = pltpu.PrefetchScalarGridSpec(
        num_scalar_prefetch=2, grid=(half // tr,),
        in_specs=[pl.BlockSpec((None, tr, cols), lambda i, chip_ref, c_ref: (chip_ref[0], i, 0)),
                  pl.BlockSpec((3, tr, cols), lambda i, chip_ref, c_ref: (0, i, 0))],
        out_specs=out_spec)
    return pl.pallas_call(
        body, name=name, grid_spec=grid_spec, out_shape=out_shape,
        compiler_params=_params(("parallel",)),
    )(chip_arr, c_arr, p, got)


def _share_reduced(bufs):
    nb = len(bufs) - 1

    def body(*refs):
        outs = refs[nb + 1:2 * nb + 2]
        send_sems, recv_sems = refs[2 * nb + 2:]
        x, y, c = _position()
        chip = 2 * x + y
        sends = []
        for i in range(nb):
            cp = _remote(outs[i].at[c], outs[i].at[c], send_sems.at[i], recv_sems.at[i], (x, y, 1 - c))
            cp.start()
            sends.append(cp)
        small = outs[nb]
        peers = [(fx, fy, fc) for fx in (0, 1) for fy in (0, 1) for fc in (0, 1) if fx + fy + fc > 0]
        for k, (fx, fy, fc) in enumerate(peers):
            dev = (x ^ fx, y ^ fy, c ^ fc)
            cp = _remote(small.at[chip, c], small.at[chip, c], send_sems.at[nb + k], recv_sems.at[nb + k], dev)
            cp.start()
            sends.append(cp)
        for i in range(nb):
            dst = outs[i].at[1 - c]
            _remote(dst, dst, send_sems.at[i], recv_sems.at[i], (x, y, 1 - c)).wait_recv()
        for k, (fx, fy, fc) in enumerate(peers):
            dst = small.at[2 * (x ^ fx) + (y ^ fy), c ^ fc]
            _remote(dst, dst, send_sems.at[nb + k], recv_sems.at[nb + k], (x ^ fx, y ^ fy, c ^ fc)).wait_recv()
        for cp in sends:
            cp.wait_send()

    n_all = nb + 1
    return pl.pallas_call(
        body, name="grad_share_reduced", in_specs=[ANY] * n_all, out_specs=[ANY] * n_all,
        out_shape=[jax.ShapeDtypeStruct(b.shape, b.dtype) for b in bufs],
        input_output_aliases={i: i for i in range(n_all)},
        scratch_shapes=[pltpu.SemaphoreType.DMA((nb + 7,)), pltpu.SemaphoreType.DMA((nb + 7,))],
    )(*bufs)


class _GradReducer:
    def __init__(self, c_arr, chip_arr):
        self.c_arr, self.chip_arr = c_arr, chip_arr
        self.full, self.pairs, self.landed = {}, {}, {}

    def swap(self, names, grads):
        for n, g in zip(names, grads):
            self.full[n] = g.reshape(N_CHIPS, 2, g.shape[0] // (2 * N_CHIPS), g.shape[1])
        return _halves_exchange([self.full[n] for n in names])

    def swapped(self, names, bufs):
        for n, t in zip(names, bufs):
            self.pairs[n] = _pair_sum(self.full[n], t, self.c_arr, name="grad_pair_sum_" + n)

    def scatter(self, names):
        return _scatter_exchange([self.pairs[n][1] for n in names])

    def collect(self, names, bufs):
        self.landed.update(zip(names, bufs))

    def swap_now(self, names, grads):
        self.swapped(names, _run_exchange(self.swap(names, grads), name="grad_exchange_" + names[0]))

    def finish(self, names, grads, order):
        self.swap_now(names, grads)
        self.collect(names, _run_exchange(self.scatter(names), name="grad_scatter_" + names[0]))
        totals = [_owner_sum(self.pairs[n][0], self.landed[n], self.chip_arr, self.c_arr, replicated=(n == order[-1]),
                             name="grad_owner_sum_" + n) for n in order]
        return _share_reduced(totals)


def _pack_small(vals):
    flat = jnp.concatenate([vals[name].reshape(-1) for name, _ in SMALL])
    return jnp.pad(flat, (0, N_CHIPS * SMALL_ROWS * 1024 - SMALL_ELEMS)).reshape(N_CHIPS * SMALL_ROWS, 1024)


def _unpack_small(buf):
    flat = buf.reshape(-1)
    out, off = {}, 0
    for name, shape in SMALL:
        n = int(np.prod(shape))
        out[name] = flat[off:off + n].reshape(shape)
        off += n
    return out


EARLY_REDUCED = (("w_down",), ("w_up",), ("w_o", "w_ssm_br", "w_attn_br", "w_mem_br", "w_glu", "w_mem_kv"), ("w_in",))


def _device_step(x, mem, tgt, w, p, *, shards, reducer):
    rows = x.shape[0]
    w = dict(w)
    early = EARLY_REDUCED
    gb = {}
    gather_pending = shards is not None

    def riding(*stages):
        if reducer is None or not stages:
            return None
        return _join_exchanges([reducer.swap(names, [gb[n] for n in names]) if kind == "swap" else reducer.scatter(names)
                                for kind, names in stages])

    def arrived(stages, res):
        if reducer is None or not stages:
            return res
        main, bufs = res
        for kind, names in stages:
            (reducer.swapped if kind == "swap" else reducer.collect)(names, bufs[:len(names)])
            bufs = bufs[len(names):]
        return main

    def fetching(names):
        return _gather_exchange([shards[n] for n in names]) if gather_pending else None

    def fetched(names, res):
        if not gather_pending:
            return res
        w.update(zip(names, res[1]))
        return res[0]

    first_use = (("w_in",), ("w_glu", "w_ssm_br", "w_attn_br", "w_mem_kv", "w_mem_br", "w_o", "w_up"), ("w_down",))
    g1, gm, g2 = p["norm1_g"], p["mem_norm_g"], p["norm2_g"]
    gf = p["final_g"].reshape(1, D_MODEL)
    ssm_args = (p["ssm_lambda_re"][0], p["ssm_lambda_im"][0], p["ssm_log_dt"][0], p["ssm_b_re"][0],
                p["ssm_b_im"][0], p["ssm_c_re"][0], p["ssm_c_im"][0])
    (a_lay, b_blk, c_blk), ssm_vjp = jax.vjp(_ssm_matrices, *ssm_args)
    a_conj = a_lay * _to_scan_layout(jnp.stack([jnp.ones((N_STATES,), F32), -jnp.ones((N_STATES,), F32)]))[None, :]
    dd = p["ssm_d"].reshape(1, SSM_WIDTH)
    mm = _matmul

    n1 = fetched(first_use[0], _rmsnorm_fwd(x, g1, tm=512, carry=fetching(first_use[0]), name="norm1"))
    win_t = w["w_in"]
    splits = ((OFF_U, OFF_QKV - OFF_U), (OFF_QKV, OFF_MQ - OFF_QKV), (OFF_MQ, OFF_ZG - OFF_MQ), (OFF_ZG, IN_WIDTH - OFF_ZG))
    u, qkv, mq, zg = fetched(first_use[1], _split_matmul(n1, win_t, splits, tm=512, carry=fetching(first_use[1]),
                                                         vmem=VMEM_LIMIT_WIDE_BYTES, name="in_proj"))

    u_i = _interleave(u)
    ends = _ssm_ends(a_lay, u_i, b_blk, transpose=False, reverse=False, tt=512, name="ssm_fwd_ends")
    s, ys_i, s_entry = _ssm_fwd(a_lay, u_i, b_blk, c_blk, ends, tt=512, name="ssm_fwd")
    ys = _deinterleave(ys_i)
    y0, tglu, y2 = _glu_fwd(ys, u, dd, w["w_glu"], p["b_glu"], tm=512, name="glu_fwd")

    outs, lses = [], []
    for g, (_, d) in enumerate(ATTN_PATTERNS):
        o_g, lse_g = _attn_fwd(qkv, g, d, name=f"attn_fwd_{g}")
        outs.append(o_g)
        lses.append(lse_g)
    o, lse = _attn_merge(outs, lses, tm=1024, name="attn_merge")

    mn = _rmsnorm_fwd(mem, gm, tm=MEM_LEN, name="mem_norm")
    kv = mm(mn, w["w_mem_kv"], m=MEM_LEN, n=1024, k=1024, tm=MEM_LEN, tn=1024, tk=1024, out_dtypes=(F32,), name="mem_kv")
    mo = _mem_attn_fwd(mq, kv, tq=1024, name="mem_attn_fwd")

    branch_acts = (y2, o, mo)
    branch_wts = (w["w_ssm_br"], w["w_attn_br"], w["w_mem_br"])
    merged = _branch_merge_fwd(branch_acts, branch_wts, zg, p["b_gate"], tm=256, name="branch_merge_fwd")
    h1, n2 = mm(merged, w["w_o"], m=rows, n=1024, k=1024, tm=1024, tn=1024, tk=1024, out_dtypes=(F32, BF16),
                aux=((x, "mn"), (g2, "row")), epilogue=_residual_norm_epilogue, name="out_proj")
    relu2 = lambda acc: (jnp.square(jnp.maximum(acc, 0.0)),)
    act = fetched(first_use[2], _sum_matmul([n2], w["w_up"], [0], tb=True, tm=512, out_dtype=BF16, epilogue=relu2,
                                            carry=fetching(first_use[2]), name="mlp_up"))
    dh2, d_gf, sq_err = _sum_matmul([act], w["w_down"], [0], tm=512, aux=((h1, "mn"), (tgt, "mn"), (gf, "row")),
                                    epilogue=_loss_head_epilogue, n_sums=2, name="mlp_down")
    loss = (0.5 / D_MODEL) * jnp.sum(sq_err)

    gs = {"final_g": d_gf.reshape(D_MODEL)}
    drelu2 = lambda acc, actv: (acc * (2.0 * jnp.sqrt(actv.astype(F32))),)
    dup = mm(dh2, w["w_down"], m=rows, n=D_FF, k=1024, tb=True, tm=1024, tn=2048, tk=1024, out_dtypes=(BF16,),
             aux=((act, "mn"),), epilogue=drelu2, name="d_act")
    gb["w_down"] = mm(act, dh2, m=D_FF, n=1024, k=rows, ta=True, tm=1024, tn=1024, tk=2048, out_dtypes=(F32,), name="dw_down")
    stages = (("swap", early[0]),)
    gb["w_up"] = arrived(stages, mm(dup, n2, m=D_FF, n=1024, k=rows, ta=True, tm=1024, tn=1024, tk=2048,
                                    out_dtypes=(F32,), carry=riding(*stages), name="dw_up"))
    stages = (("scatter", early[0]), ("swap", early[1]))
    dh1, gs["norm2_g"] = arrived(stages, _sum_matmul([dup], w["w_up"], [0], tm=512, aux=((h1, "mn"), (dh2, "mn"), (g2, "row")),
                                                     epilogue=_rmsnorm_bwd_epilogue, n_sums=1, carry=riding(*stages), name="d_n2"))
    dmerged = mm(dh1, w["w_o"], m=rows, n=1024, k=1024, tb=True, tm=1024, tn=1024, tk=1024, out_dtypes=(F32,), name="d_merged")
    gb["w_o"] = mm(merged, dh1, m=1024, n=1024, k=rows, ta=True, tm=1024, tn=1024, tk=2048, out_dtypes=(F32,), name="dw_o")
    stages = (("scatter", early[1]),)
    (dy2, do, dmo, gb["w_ssm_br"], gb["w_attn_br"], gb["w_mem_br"], dzg, gs["b_gate"]) = arrived(stages, _branch_merge_bwd(
        dmerged, branch_acts, branch_wts, zg, p["b_gate"], tm=256, carry=riding(*stages), name="branch_merge_bwd"))

    dy0, dt, y1, gs["b_glu"], d_dd = _glu_bwd(dy2, y0, tglu, u, w["w_glu"], tm=512, name="glu_bwd")
    gs["ssm_d"] = d_dd.reshape(1, SSM_GROUPS, SSM_GROUP_SIZE)
    gb["w_glu"] = mm(y1, dt, m=512, n=512, k=rows, ta=True, tm=512, tn=512, tk=1024, out_dtypes=(F32,), name="dw_glu")
    dy0_i = _interleave(dy0)
    lam_ends = _ssm_ends(a_conj, dy0_i, c_blk, transpose=True, reverse=True, tt=512, name="ssm_bwd_ends")
    du_i, d_b_blk, d_c_blk, d_a_lay = _ssm_bwd(a_conj, dy0_i, u_i, s, s_entry, b_blk, c_blk, dd, lam_ends, tt=512,
                                                name="ssm_bwd")
    du = _deinterleave(du_i)
    d_ssm = ssm_vjp((d_a_lay, d_b_blk, d_c_blk))
    for name, val in zip(("ssm_lambda_re", "ssm_lambda_im", "ssm_log_dt", "ssm_b_re", "ssm_b_im", "ssm_c_re", "ssm_c_im"), d_ssm):
        gs[name] = val[None]

    dqkv = None
    for g, (_, d) in enumerate(ATTN_PATTERNS):
        dqkv = _attn_bwd(qkv, do, o, lse, g, d, dqkv, name=f"attn_bwd_{g}")

    dmq, dmk, dmv = _mem_attn_bwd(mq, kv, dmo, tq=1024, name="mem_attn_bwd")
    dkv = jnp.concatenate([dmk, dmv], axis=1)
    gb["w_mem_kv"] = mm(mn, dkv, m=1024, n=1024, k=MEM_LEN, ta=True, tm=1024, tn=1024, tk=MEM_LEN, out_dtypes=(F32,), name="dw_mem_kv")
    dmn = mm(dkv, w["w_mem_kv"], m=MEM_LEN, n=1024, k=1024, tb=True, tm=MEM_LEN, tn=1024, tk=1024, out_dtypes=(F32,), name="d_mn")
    _, gs["mem_norm_g"] = _rmsnorm_bwd(mem, gm, dmn, None, tm=MEM_LEN, name="mem_norm_bwd")

    pieces = ((du, OFF_U, "u"), (dqkv[0], OFF_QKV, "q"), (dqkv[1], OFF_QKV + 768, "k"), (dqkv[2], OFF_QKV + 1536, "v"),
              (dmq, OFF_MQ, "mq"), (dzg, OFF_ZG, "zg"))
    dw_rows = []
    for piece, off, tag in pieces:
        width = piece.shape[1]
        tmw = 1024 if width % 1024 == 0 else (768 if width == 768 else 512)
        stages = {"q": (("swap", early[2]),), "zg": (("scatter", early[2]),)}.get(tag, ())
        dw_rows.append(arrived(stages, mm(piece, n1, m=width, n=1024, k=rows, ta=True, tm=tmw, tn=1024, tk=2048,
                                          out_dtypes=(F32,), carry=riding(*stages), name="dw_in_" + tag)))
    gb["w_in"] = jnp.concatenate(dw_rows, axis=0)
    if reducer is not None:
        reducer.swap_now(early[3], [gb["w_in"]])
    stages = (("scatter", early[3]),)
    dx, gs["norm1_g"] = arrived(stages, _sum_matmul(
        [piece for piece, _, _ in pieces], win_t, [off for _, off, _ in pieces], tm=512,
        aux=((x, "mn"), (dh1, "mn"), (g1, "row")), epilogue=_rmsnorm_bwd_epilogue, n_sums=1,
        carry=riding(*stages), vmem=VMEM_LIMIT_WIDE_BYTES, name="d_n1"))
    return loss, dx, gb, gs


def kernel(x, mem, norm1_g, mem_norm_g, w_in, b_gate, ssm_lambda_re, ssm_lambda_im, ssm_log_dt, ssm_b_re, ssm_b_im, ssm_c_re, ssm_c_im, ssm_d, w_glu, b_glu, w_ssm_br, w_attn_br, w_mem_kv, w_mem_br, w_o, norm2_g, w_up, w_down, final_g, loss_target, m_norm1_g, m_mem_norm_g, m_w_in, m_b_gate, m_ssm_lambda_re, m_ssm_lambda_im, m_ssm_log_dt, m_ssm_b_re, m_ssm_b_im, m_ssm_c_re, m_ssm_c_im, m_ssm_d, m_w_glu, m_b_glu, m_w_ssm_br, m_w_attn_br, m_w_mem_kv, m_w_mem_br, m_w_o, m_norm2_g, m_w_up, m_w_down, m_final_g, v_norm1_g, v_mem_norm_g, v_w_in, v_b_gate, v_ssm_lambda_re, v_ssm_lambda_im, v_ssm_log_dt, v_ssm_b_re, v_ssm_b_im, v_ssm_c_re, v_ssm_c_im, v_ssm_d, v_w_glu, v_b_glu, v_w_ssm_br, v_w_attn_br, v_w_mem_kv, v_w_mem_br, v_w_o, v_norm2_g, v_w_up, v_w_down, v_final_g):
    env = dict(locals())
    weights = {n: env[n] for n in WEIGHT_ORDER}
    moms = {n: env["m_" + n] for n in WEIGHT_ORDER}
    vels = {n: env["v_" + n] for n in WEIGHT_ORDER}

    chip = 2 * lax.axis_index("x") + lax.axis_index("y")
    wire = [weights[n].reshape(weights[n].shape[-2:]).astype(BF16) for n, _, _ in BIG]
    wire = dict(zip([n for n, _, _ in BIG], [s.T if tr else s for s, (_, tr, _) in zip(wire, BIG)]))
    small = {n: weights[n] for n, _ in SMALL}

    reducer = _GradReducer(lax.axis_index("c").astype(jnp.int32).reshape(1), chip.astype(jnp.int32).reshape(1))
    loss, dx, gb, gs = _device_step(x[0], mem[0], loss_target[0], {}, small, shards=wire, reducer=reducer)
    *shards, small_grad = reducer.finish(["small"], [_pack_small(gs)], [n for n, _, _ in BIG] + ["small"])
    grads = {}
    for (n, tr, _), sh in zip(BIG, shards):
        sh = sh.reshape(2 * sh.shape[1], sh.shape[2])
        grads[n] = sh.T if tr else sh
    small_grad = small_grad.reshape(N_CHIPS * SMALL_ROWS, 1024)
    grads_small = _unpack_small(small_grad)

    delta, new_m, new_v = {}, {}, {}
    for n, _, _ in BIG:
        grads[n], delta[n], new_m[n], new_v[n] = _adamw(weights[n], grads[n], moms[n], vels[n],
                                                        tr=min(weights[n].shape[-2], 256), name="adamw_" + n)
    _, ds_, ms_, vs_ = _adamw(_pack_small(small), small_grad,
                              _pack_small({n: moms[n] for n, _ in SMALL}), _pack_small({n: vels[n] for n, _ in SMALL}),
                              tr=N_CHIPS * SMALL_ROWS, name="adamw_small")
    for dst, buf in ((delta, ds_), (new_m, ms_), (new_v, vs_)):
        dst.update(_unpack_small(buf))
    grads.update(grads_small)

    total_loss = lax.psum(loss, ("x", "y", "c"))
    return (total_loss, dx[None], *[grads[n] for n in WEIGHT_ORDER], *[delta[n] for n in WEIGHT_ORDER],
            *[new_m[n] for n in WEIGHT_ORDER], *[new_v[n] for n in WEIGHT_ORDER])
```

```python
import functools
import math

import numpy as np
import jax
import jax.numpy as jnp
from jax import lax
from jax.experimental import pallas as pl
from jax.experimental.pallas import tpu as pltpu

F32 = jnp.float32
BF16 = jnp.bfloat16

D_MODEL = 1024
SSM_GROUPS = 32
SSM_GROUP_SIZE = 16
SSM_STATE = 64
SSM_WIDTH = 512
N_STATES = SSM_GROUPS * SSM_STATE
SCAN_CB = 1024
ATTN_PATTERNS = ((128, 1), (512, 4), (2048, 16))
ATTN_HEAD_DIM = 64
ATTN_Q = 128
MEM_LEN = 256
MEM_HEAD_DIM = 128
MEM_HEADS = 4
D_FF = 4096
OFF_U, OFF_QKV, OFF_MQ, OFF_ZG = 0, 512, 2816, 3328
IN_WIDTH = 6400
RMS_EPS = 1e-6
NEG_INF = -1e30
ADAM_LR, ADAM_B1, ADAM_B2, ADAM_EPS, ADAM_WD, ADAM_STEP = 0.001, 0.9, 0.999, 1e-08, 0.01, 10

VMEM_LIMIT_BYTES = 48 * 1024 * 1024
VMEM_LIMIT_WIDE_BYTES = 56 * 1024 * 1024
LANES = 128
MESH = pl.DeviceIdType.MESH
N_CHIPS = 4

SCAN_SEGS = 8
SCAN_GROUPS = SCAN_CB // SSM_STATE

BIG = (("w_in", True, (6400, 1024)), ("w_glu", False, (512, 512)), ("w_ssm_br", True, (1024, 512)),
       ("w_attn_br", True, (1024, 256)), ("w_mem_kv", False, (1024, 1024)), ("w_mem_br", True, (1024, 512)),
       ("w_o", False, (1024, 1024)), ("w_up", True, (4096, 1024)), ("w_down", False, (4096, 1024)))
SMALL = (("norm1_g", (1, 1024)), ("mem_norm_g", (1, 1024)), ("b_gate", (1, 3072)),
         ("ssm_lambda_re", (1, 32, 64)), ("ssm_lambda_im", (1, 32, 64)), ("ssm_log_dt", (1, 32)),
         ("ssm_b_re", (1, 32, 64, 16)), ("ssm_b_im", (1, 32, 64, 16)), ("ssm_c_re", (1, 32, 16, 64)),
         ("ssm_c_im", (1, 32, 16, 64)), ("ssm_d", (1, 32, 16)), ("b_glu", (1, 512)),
         ("norm2_g", (1, 1024)), ("final_g", (1024,)))
WEIGHT_ORDER = ("norm1_g", "mem_norm_g", "w_in", "b_gate", "ssm_lambda_re", "ssm_lambda_im", "ssm_log_dt",
                "ssm_b_re", "ssm_b_im", "ssm_c_re", "ssm_c_im", "ssm_d", "w_glu", "b_glu", "w_ssm_br",
                "w_attn_br", "w_mem_kv", "w_mem_br", "w_o", "norm2_g", "w_up", "w_down", "final_g")
SMALL_ELEMS = sum(int(np.prod(s)) for _, s in SMALL)
SMALL_ROWS = 64


def _params(sem, vmem=VMEM_LIMIT_BYTES):
    return pltpu.CompilerParams(dimension_semantics=sem, vmem_limit_bytes=vmem)


def _sigmoid(v):
    return 0.5 * jnp.tanh(0.5 * v) + 0.5


_GELU_C = math.sqrt(2.0 / math.pi)


def _gelu(v):
    return 0.5 * v * (1.0 + jnp.tanh(_GELU_C * (v + 0.044715 * v * v * v)))


def _gelu_grad(v):
    th = jnp.tanh(_GELU_C * (v + 0.044715 * v * v * v))
    return 0.5 * (1.0 + th) + 0.5 * v * (1.0 - th * th) * _GELU_C * (1.0 + 3.0 * 0.044715 * v * v)


def _dot(a, b, ca, cb):
    return lax.dot_general(a, b, (((ca,), (cb,)), ((), ())), preferred_element_type=F32)


class _Exchange:
    def __init__(self, ins, outs, aliases, sems, start, finish):
        self.ins, self.outs, self.aliases, self.sems, self.start, self.finish = ins, outs, aliases, sems, start, finish


def _matmul(a, b, *, m, n, k, ta=False, tb=False, tm, tn, tk, out_dtypes, name,
            aux=(), epilogue=None, n_sums=0, carry=None):
    assert m % tm == 0 and n % tn == 0 and k % tk == 0, (name, m, n, k, tm, tn, tk)
    assert n_sums == 0 or tn == n, name
    nk = k // tk
    n_aux = len(aux)
    n_tiles = len(out_dtypes)
    n_out = n_tiles + n_sums
    a_spec = pl.BlockSpec((tk, tm), lambda i, j, kk: (kk, i)) if ta else pl.BlockSpec((tm, tk), lambda i, j, kk: (i, kk))
    b_spec = pl.BlockSpec((tn, tk), lambda i, j, kk: (j, kk)) if tb else pl.BlockSpec((tk, tn), lambda i, j, kk: (kk, j))
    aux_specs = []
    for _, kind in aux:
        if kind == "mn":
            aux_specs.append(pl.BlockSpec((tm, tn), lambda i, j, kk: (i, j)))
        else:
            aux_specs.append(pl.BlockSpec((1, tn), lambda i, j, kk: (0, j)))
    ca = 0 if ta else 1
    cb = 1 if tb else 0

    def finish(acc, aux_refs, out_refs, row_tile):
        outs = (acc,) if epilogue is None else epilogue(acc, *[r[...] for r in aux_refs])
        for o_ref, o in zip(out_refs[:n_tiles], outs[:n_tiles]):
            o_ref[...] = o.astype(o_ref.dtype)
        _accumulate_over_rows(out_refs[n_tiles:], outs[n_tiles:], row_tile)

    def body(a_ref, b_ref, *rest):
        aux_refs = rest[:n_aux]
        out_refs = rest[n_aux:n_aux + n_out]
        row_tile = pl.program_id(0)
        prod = _dot(a_ref[...].astype(BF16), b_ref[...].astype(BF16), ca, cb)
        if nk == 1:
            finish(prod, aux_refs, out_refs, row_tile)
            return
        acc_ref = rest[n_aux + n_out]
        kk = pl.program_id(2)

        @pl.when(kk == 0)
        def _():
            acc_ref[...] = prod

        @pl.when(jnp.logical_and(kk > 0, kk < nk - 1))
        def _():
            acc_ref[...] += prod

        @pl.when(kk == nk - 1)
        def _():
            finish(acc_ref[...] + prod, aux_refs, out_refs, row_tile)

    tile = pl.BlockSpec((tm, tn), lambda i, j, kk: (i, j))
    col_sum = pl.BlockSpec((1, tn), lambda i, j, kk: (0, j))
    res = _call_with_carry(
        body, carry, name=name, grid=(m // tm, n // tn, nk), in_specs=[a_spec, b_spec] + aux_specs,
        out_specs=[tile] * n_tiles + [col_sum] * n_sums,
        out_shape=[jax.ShapeDtypeStruct((m, n), dt) for dt in out_dtypes] + [jax.ShapeDtypeStruct((1, n), F32)] * n_sums,
        scratch=[pltpu.VMEM((tm, tn), F32)] if nk > 1 else [], operands=[a, b] + [x for x, _ in aux],
        semantics=("arbitrary" if n_sums else "parallel", "parallel", "arbitrary"))
    main = res[0] if n_out == 1 else tuple(res[:n_out])
    return main if carry is None else (main, list(res[n_out:]))


def _accumulate_over_rows(sum_refs, terms, row_tile):
    for s_ref, term in zip(sum_refs, terms):
        @pl.when(row_tile == 0)
        def _():
            s_ref[...] = term

        @pl.when(row_tile > 0)
        def _():
            s_ref[...] += term


def _call_with_carry(body, carry, *, name, grid, in_specs, out_specs, out_shape, scratch, operands, semantics,
                     vmem=VMEM_LIMIT_BYTES):
    if carry is None:
        return pl.pallas_call(body, name=name, grid=grid, in_specs=in_specs, out_specs=out_specs, out_shape=out_shape,
                              scratch_shapes=scratch, compiler_params=_params(semantics, vmem))(*operands)
    n_in, n_cin, n_out, n_cout, n_scr = len(operands), len(carry.ins), len(out_shape), len(carry.outs), len(scratch)

    def hosted(*refs):
        main_in, c_in = refs[:n_in], refs[n_in:n_in + n_cin]
        main_out = refs[n_in + n_cin:n_in + n_cin + n_out]
        c_out = refs[n_in + n_cin + n_out:n_in + n_cin + n_out + n_cout]
        rest = refs[n_in + n_cin + n_out + n_cout:]
        ids = [pl.program_id(t) for t in range(len(grid))]
        first = functools.reduce(jnp.logical_and, [i == 0 for i in ids])
        last = functools.reduce(jnp.logical_and, [i == g - 1 for i, g in zip(ids, grid)])

        @pl.when(first)
        def _():
            carry.start(c_in, c_out, *rest[n_scr:])

        body(*main_in, *main_out, *rest[:n_scr])

        @pl.when(last)
        def _():
            carry.finish(c_in, c_out, *rest[n_scr:])

    return pl.pallas_call(
        hosted, name=name, grid=grid,
        in_specs=list(in_specs) + [ANY] * n_cin, out_specs=list(out_specs) + [ANY] * n_cout,
        out_shape=list(out_shape) + list(carry.outs),
        input_output_aliases={n_in + i: n_out + o for i, o in carry.aliases.items()},
        scratch_shapes=list(scratch) + [pltpu.SemaphoreType.DMA(s) for s in carry.sems],
        compiler_params=_params(("arbitrary",) * len(grid), vmem),
    )(*operands, *carry.ins)


def _sum_matmul(pieces, b, offs, *, tm, name, tb=False, out_dtype=F32, aux=(), epilogue=None, n_sums=0, carry=None,
                vmem=VMEM_LIMIT_BYTES):
    m = pieces[0].shape[0]
    n = b.shape[0] if tb else b.shape[1]
    npieces, n_aux = len(pieces), len(aux)
    assert not tb or npieces == 1

    def body(*refs):
        b_ref = refs[npieces]
        aux_refs = refs[npieces + 1:npieces + 1 + n_aux]
        out_refs = refs[npieces + 1 + n_aux:]
        acc = None
        for p_ref, off in zip(refs[:npieces], offs):
            lhs = p_ref[...].astype(BF16)
            part = _dot(lhs, b_ref[...], 1, 1) if tb else _dot(lhs, b_ref[pl.ds(off, p_ref.shape[1]), :], 1, 0)
            acc = part if acc is None else acc + part
        outs = (acc,) if epilogue is None else epilogue(acc, *[r[...] for r in aux_refs])
        out_refs[0][...] = outs[0].astype(out_dtype)
        _accumulate_over_rows(out_refs[1:], outs[1:], pl.program_id(0))

    row = pl.BlockSpec((tm, n), lambda i: (i, 0))
    vec = pl.BlockSpec((1, n), lambda i: (0, 0))
    res = _call_with_carry(
        body, carry, name=name, grid=(m // tm,),
        in_specs=[pl.BlockSpec((tm, p.shape[1]), lambda i: (i, 0)) for p in pieces] + [_resident(b.shape)]
        + [row if kind == "mn" else vec for _, kind in aux],
        out_specs=[row] + [vec] * n_sums,
        out_shape=[jax.ShapeDtypeStruct((m, n), out_dtype)] + [jax.ShapeDtypeStruct((1, n), F32)] * n_sums,
        scratch=[], operands=list(pieces) + [b] + [x for x, _ in aux], semantics=("arbitrary" if n_sums else "parallel",),
        vmem=vmem)
    main = res[0] if n_sums == 0 else tuple(res[:1 + n_sums])
    return main if carry is None else (main, list(res[1 + n_sums:]))


def _split_matmul(a, b_t, splits, *, tm, name, carry=None, vmem=VMEM_LIMIT_BYTES):
    m, k = a.shape

    def body(a_ref, b_ref, *out_refs):
        av = a_ref[...].astype(BF16)
        for (row0, width), o_ref in zip(splits, out_refs):
            o_ref[...] = _dot(av, b_ref[pl.ds(row0, width), :], 1, 1)

    res = _call_with_carry(
        body, carry, name=name, grid=(m // tm,),
        in_specs=[pl.BlockSpec((tm, k), lambda i: (i, 0)), _resident(b_t.shape)],
        out_specs=[pl.BlockSpec((tm, width), lambda i: (i, 0)) for _, width in splits],
        out_shape=[jax.ShapeDtypeStruct((m, width), F32) for _, width in splits],
        scratch=[], operands=[a, b_t], semantics=("parallel",), vmem=vmem)
    outs = tuple(res[:len(splits)])
    return outs if carry is None else (outs, list(res[len(splits):]))


def _rmsnorm_fwd(x, g, *, tm, name, carry=None):
    rows, d = x.shape

    def body(x_ref, g_ref, o_ref):
        xv = x_ref[...]
        r = lax.rsqrt(jnp.mean(xv * xv, axis=-1, keepdims=True) + RMS_EPS)
        o_ref[...] = (xv * r * g_ref[...]).astype(o_ref.dtype)

    res = _call_with_carry(
        body, carry, name=name, grid=(rows // tm,),
        in_specs=[pl.BlockSpec((tm, d), lambda i: (i, 0)), pl.BlockSpec((1, d), lambda i: (0, 0))],
        out_specs=[pl.BlockSpec((tm, d), lambda i: (i, 0))], out_shape=[jax.ShapeDtypeStruct((rows, d), BF16)],
        scratch=[], operands=[x, g], semantics=("parallel",))
    return res[0] if carry is None else (res[0], list(res[1:]))


def _residual_norm_epilogue(acc, xv, gv):
    h = acc + xv
    r = lax.rsqrt(jnp.mean(h * h, axis=-1, keepdims=True) + RMS_EPS)
    return h, h * r * gv


def _rmsnorm_bwd_epilogue(dy, xv, resv, gv):
    r = lax.rsqrt(jnp.mean(xv * xv, axis=-1, keepdims=True) + RMS_EPS)
    xhat = xv * r
    dyg = dy * gv
    dx = r * (dyg - xhat * jnp.mean(dyg * xhat, axis=-1, keepdims=True)) + resv
    return dx, jnp.sum(dy * xhat, axis=0, keepdims=True)


def _rmsnorm_bwd(x, g, dy, res, *, tm, name):
    rows, d = x.shape
    has_res = res is not None

    def body(x_ref, g_ref, dy_ref, *rest):
        if has_res:
            res_ref, dx_ref, dg_ref = rest
        else:
            dx_ref, dg_ref = rest
        i = pl.program_id(0)
        xv = x_ref[...]
        r = lax.rsqrt(jnp.mean(xv * xv, axis=-1, keepdims=True) + RMS_EPS)
        xhat = xv * r
        dyv = dy_ref[...]
        dyg = dyv * g_ref[...]
        dx = r * (dyg - xhat * jnp.mean(dyg * xhat, axis=-1, keepdims=True))
        if has_res:
            dx = dx + res_ref[...]
        dx_ref[...] = dx

        @pl.when(i == 0)
        def _():
            dg_ref[...] = jnp.zeros_like(dg_ref)

        dg_ref[...] += jnp.sum(dyv * xhat, axis=0, keepdims=True)

    row_spec = pl.BlockSpec((tm, d), lambda i: (i, 0))
    vec_spec = pl.BlockSpec((1, d), lambda i: (0, 0))
    ins = [x, g, dy] + ([res] if has_res else [])
    return pl.pallas_call(
        body, name=name, grid=(rows // tm,),
        in_specs=[row_spec, vec_spec, row_spec] + ([row_spec] if has_res else []),
        out_specs=[row_spec, vec_spec],
        out_shape=[jax.ShapeDtypeStruct((rows, d), F32), jax.ShapeDtypeStruct((1, d), F32)],
        compiler_params=_params(("arbitrary",)),
    )(*ins)


def _loss_head_epilogue(acc, hv, tgtv, gv):
    xv = acc + hv
    r = lax.rsqrt(jnp.mean(xv * xv, axis=-1, keepdims=True) + RMS_EPS)
    xhat = xv * r
    err = xhat * gv - tgtv
    dyv = err * (1.0 / D_MODEL)
    dyg = dyv * gv
    dh = r * (dyg - xhat * jnp.mean(dyg * xhat, axis=-1, keepdims=True))
    return dh, jnp.sum(dyv * xhat, axis=0, keepdims=True), jnp.sum(err * err, axis=0, keepdims=True)


def _to_scan_layout(v):
    lead = v.shape[:-2]
    v = v.reshape(lead + (2, N_STATES // SCAN_CB, SCAN_CB))
    v = jnp.swapaxes(v, -3, -2)
    return v.reshape(lead + (2 * N_STATES,))


def _ssm_matrices(lam_re, lam_im, log_dt, b_re, b_im, c_re, c_im):
    dt = jnp.exp(log_dt)[:, None]
    mag = jnp.exp(lam_re * dt)
    a_re, a_im = mag * jnp.cos(lam_im * dt), mag * jnp.sin(lam_im * dt)
    nr, ni = a_re - 1.0, a_im
    den = lam_re * lam_re + lam_im * lam_im
    coef_re = (nr * lam_re + ni * lam_im) / den
    coef_im = (ni * lam_re - nr * lam_im) / den
    bb_re = coef_re[..., None] * b_re - coef_im[..., None] * b_im
    bb_im = coef_re[..., None] * b_im + coef_im[..., None] * b_re
    a_lay = _to_scan_layout(jnp.stack([a_re.reshape(-1), a_im.reshape(-1)], axis=0))[None, :]
    nblk = SSM_GROUPS // SCAN_GROUPS
    eye = jnp.eye(SCAN_GROUPS, dtype=F32)

    def b_block(bb):
        bb = bb.reshape(nblk, SCAN_GROUPS, SSM_STATE, SSM_GROUP_SIZE)
        return jnp.einsum("gk,jkph->jghkp", eye, bb).reshape(nblk, SCAN_GROUPS * SSM_GROUP_SIZE, SCAN_CB)

    b_blk = jnp.concatenate([b_block(bb_re), b_block(bb_im)], axis=2)

    def c_block(cc):
        cc = cc.reshape(nblk, SCAN_GROUPS, SSM_GROUP_SIZE, SSM_STATE)
        return jnp.einsum("gk,jghp->jkpgh", eye, cc).reshape(nblk, SCAN_CB, SCAN_GROUPS * SSM_GROUP_SIZE)

    c_blk = jnp.concatenate([c_block(c_re), -c_block(c_im)], axis=1)
    return a_lay, b_blk, c_blk


def _interleave(v):
    rows, c = v.shape
    return v.reshape(SCAN_SEGS, rows // SCAN_SEGS, c).transpose(1, 0, 2).reshape(rows, c)


def _deinterleave(v):
    rows, c = v.shape
    return v.reshape(rows // SCAN_SEGS, SCAN_SEGS, c).transpose(1, 0, 2).reshape(rows, c)


def _scan_groups(a_ref, bu_ref, o_ref, state, *, reverse, tt, unroll=4):
    cb = SCAN_CB
    ar = jnp.broadcast_to(a_ref[:, :cb], (SCAN_SEGS, cb))
    ai = jnp.broadcast_to(a_ref[:, cb:], (SCAN_SEGS, cb))
    ngroups = tt // SCAN_SEGS

    def step(i, st):
        sr, si = st
        r0 = pl.multiple_of(((ngroups - 1 - i) if reverse else i) * SCAN_SEGS, SCAN_SEGS)
        blk = bu_ref[pl.ds(r0, SCAN_SEGS), :]
        nr = ar * sr - ai * si + blk[:, :cb]
        ni = ar * si + ai * sr + blk[:, cb:]
        if o_ref is not None:
            o_ref[pl.ds(r0, SCAN_SEGS), :] = jnp.concatenate([nr, ni], axis=1)
        return nr, ni

    return lax.fori_loop(0, ngroups, step, state, unroll=unroll)


def _segment_entries(a_ref, e_ref, init_ref, *, reverse, seg_len):
    cb = SCAN_CB
    n_sq = seg_len.bit_length() - 1
    assert 1 << n_sq == seg_len, seg_len
    pr, pi = a_ref[:, :cb], a_ref[:, cb:]
    for _ in range(n_sq):
        pr, pi = pr * pr - pi * pi, 2.0 * pr * pi
    cr = jnp.zeros((1, cb), F32)
    ci = jnp.zeros((1, cb), F32)
    order = range(SCAN_SEGS - 1, -1, -1) if reverse else range(SCAN_SEGS)
    for k, seg in enumerate(order):
        if k > 0:
            prev = seg + 1 if reverse else seg - 1
            er, ei = e_ref[prev:prev + 1, :cb], e_ref[prev:prev + 1, cb:]
            cr, ci = pr * cr - pi * ci + er, pr * ci + pi * cr + ei
        init_ref[seg:seg + 1, :] = jnp.concatenate([cr, ci], axis=1)


def _ssm_specs(nt, tt, nch, reverse):
    cb = SCAN_CB
    tmap = (lambda j, kk: (nt - 1 - kk, j)) if reverse else (lambda j, kk: (kk, j))
    nmap = (lambda j, kk: (jnp.maximum(nt - 2 - kk, 0), j)) if reverse else (lambda j, kk: (jnp.minimum(kk + 1, nt - 1), j))
    return dict(a=pl.BlockSpec((1, 2 * cb), lambda j, kk: (0, j)),
                seg=pl.BlockSpec((SCAN_SEGS, 2 * cb), lambda j, kk: (0, j)),
                chan=pl.BlockSpec((tt, nch), tmap),
                next=pl.BlockSpec((tt, nch), nmap),
                state=pl.BlockSpec((tt, 2 * cb), tmap),
                b=pl.BlockSpec((None, nch, 2 * cb), lambda j, kk: (j, 0, 0)),
                c=pl.BlockSpec((None, 2 * cb, nch), lambda j, kk: (j, 0, 0)))


def _ssm_ends(a_lay, x, blocks, *, transpose, reverse, tt, name):
    rows = x.shape[0]
    nblk = blocks.shape[0]
    nch = x.shape[1] // nblk
    cb = SCAN_CB
    nt = rows // tt
    sp = _ssm_specs(nt, tt, nch, reverse)

    def body(a_ref, x_ref, xn_ref, w_ref, e_ref, even_ref, odd_ref):
        kk = pl.program_id(1)

        def product(src_ref, dst_ref):
            dst_ref[...] = _dot(src_ref[...].astype(BF16), w_ref[...].astype(BF16), 1, 1 if transpose else 0)

        @pl.when(kk == 0)
        def _():
            e_ref[...] = jnp.zeros_like(e_ref)
            product(x_ref, even_ref)

        def phase(cur_ref, next_ref):
            product(xn_ref, next_ref)
            sr, si = _scan_groups(a_ref, cur_ref, None, (e_ref[:, :cb], e_ref[:, cb:]), reverse=reverse, tt=tt, unroll=True)
            e_ref[...] = jnp.concatenate([sr, si], axis=1)

        @pl.when(kk % 2 == 0)
        def _():
            phase(even_ref, odd_ref)

        @pl.when(kk % 2 == 1)
        def _():
            phase(odd_ref, even_ref)

    return pl.pallas_call(
        body, name=name, grid=(nblk, nt),
        in_specs=[sp["a"], sp["chan"], sp["next"], sp["c"] if transpose else sp["b"]],
        out_specs=sp["seg"],
        out_shape=jax.ShapeDtypeStruct((SCAN_SEGS, nblk * 2 * cb), F32),
        scratch_shapes=[pltpu.VMEM((tt, 2 * cb), F32), pltpu.VMEM((tt, 2 * cb), F32)],
        compiler_params=_params(("parallel", "arbitrary")),
    )(a_lay, x, x, blocks)


def _ssm_fwd(a_lay, u, b_blk, c_blk, ends, *, tt, name):
    rows = u.shape[0]
    nblk = b_blk.shape[0]
    nch = u.shape[1] // nblk
    cb = SCAN_CB
    nt = rows // tt
    sp = _ssm_specs(nt, tt, nch, False)

    def body(a_ref, e_ref, u_ref, un_ref, b_ref, c_ref, s_ref, y_ref, init_ref, carry_ref, even_ref, odd_ref):
        kk = pl.program_id(1)

        def product(src_ref, dst_ref):
            dst_ref[...] = _dot(src_ref[...].astype(BF16), b_ref[...].astype(BF16), 1, 0)

        @pl.when(kk == 0)
        def _():
            _segment_entries(a_ref, e_ref, init_ref, reverse=False, seg_len=rows // SCAN_SEGS)
            carry_ref[...] = init_ref[...]
            product(u_ref, even_ref)

        def phase(cur_ref, next_ref):
            product(un_ref, next_ref)
            sr, si = _scan_groups(a_ref, cur_ref, s_ref, (carry_ref[:, :cb], carry_ref[:, cb:]), reverse=False, tt=tt,
                                  unroll=True)
            carry_ref[...] = jnp.concatenate([sr, si], axis=1)

        @pl.when(kk % 2 == 0)
        def _():
            phase(even_ref, odd_ref)

        @pl.when(kk % 2 == 1)
        def _():
            phase(odd_ref, even_ref)

        y_ref[...] = _dot(s_ref[...].astype(BF16), c_ref[...].astype(BF16), 1, 0)

    return pl.pallas_call(
        body, name=name, grid=(nblk, nt),
        in_specs=[sp["a"], sp["seg"], sp["chan"], sp["next"], sp["b"], sp["c"]],
        out_specs=[sp["state"], sp["chan"], sp["seg"]],
        out_shape=[jax.ShapeDtypeStruct((rows, nblk * 2 * cb), F32), jax.ShapeDtypeStruct((rows, nblk * nch), F32),
                   jax.ShapeDtypeStruct((SCAN_SEGS, nblk * 2 * cb), F32)],
        scratch_shapes=[pltpu.VMEM((SCAN_SEGS, 2 * cb), F32), pltpu.VMEM((tt, 2 * cb), F32), pltpu.VMEM((tt, 2 * cb), F32)],
        compiler_params=_params(("parallel", "arbitrary")),
    )(a_lay, ends, u, u, b_blk, c_blk)


def _ssm_bwd(a_conj, dy, u, s, s_entry, b_blk, c_blk, dd, ends, *, tt, name):
    rows = u.shape[0]
    nblk = b_blk.shape[0]
    nch = u.shape[1] // nblk
    cb = SCAN_CB
    nt = rows // tt
    sp = _ssm_specs(nt, tt, nch, True)
    groups_per_tile = tt // SCAN_SEGS
    before = pl.BlockSpec((SCAN_SEGS, 2 * cb), lambda j, kk: (jnp.maximum((nt - 1 - kk) * groups_per_tile - 1, 0), j))

    def body(a_ref, e_ref, dy_ref, dyn_ref, u_ref, s_ref, before_ref, entry_ref, b_ref, c_ref, dd_ref,
             du_ref, db_ref, dc_ref, da_ref, carry_ref, even_ref, odd_ref):
        kk = pl.program_id(1)

        def product(src_ref, dst_ref):
            dst_ref[...] = _dot(src_ref[...].astype(BF16), c_ref[...].astype(BF16), 1, 1)

        @pl.when(kk == 0)
        def _():
            _segment_entries(a_ref, e_ref, carry_ref, reverse=True, seg_len=rows // SCAN_SEGS)
            db_ref[...] = jnp.zeros_like(db_ref)
            dc_ref[...] = jnp.zeros_like(dc_ref)
            da_ref[...] = jnp.zeros_like(da_ref)
            product(dy_ref, even_ref)

        def pair(lv, pv):
            lre, lim, pre, pim = lv[:, :cb], lv[:, cb:], pv[:, :cb], pv[:, cb:]
            return (jnp.sum(lre * pre + lim * pim, axis=0, keepdims=True),
                    jnp.sum(lim * pre - lre * pim, axis=0, keepdims=True))

        def phase(lam_ref, next_ref):
            product(dyn_ref, next_ref)
            lr, li = _scan_groups(a_ref, lam_ref, lam_ref, (carry_ref[:, :cb], carry_ref[:, cb:]), reverse=True, tt=tt,
                                  unroll=True)
            carry_ref[...] = jnp.concatenate([lr, li], axis=1)
            first = jnp.where(kk == nt - 1, entry_ref[...], before_ref[...])
            rest = tt - SCAN_SEGS
            r1, i1 = pair(lam_ref[pl.ds(SCAN_SEGS, rest), :], s_ref[pl.ds(0, rest), :])
            r0, i0 = pair(lam_ref[pl.ds(0, SCAN_SEGS), :], first)
            da_ref[...] += jnp.concatenate([r1 + r0, i1 + i0], axis=1)
            dyv = dy_ref[...]
            lamb = lam_ref[...].astype(BF16)
            du_ref[...] = _dot(lamb, b_ref[...].astype(BF16), 1, 1) + dd_ref[...] * dyv
            db_ref[...] += _dot(u_ref[...].astype(BF16), lamb, 0, 0)
            dc_ref[...] += _dot(s_ref[...].astype(BF16), dyv.astype(BF16), 0, 0)

        @pl.when(kk % 2 == 0)
        def _():
            phase(even_ref, odd_ref)

        @pl.when(kk % 2 == 1)
        def _():
            phase(odd_ref, even_ref)

    return pl.pallas_call(
        body, name=name, grid=(nblk, nt),
        in_specs=[sp["a"], sp["seg"], sp["chan"], sp["next"], sp["chan"], sp["state"], before, sp["seg"], sp["b"], sp["c"],
                  pl.BlockSpec((1, nch), lambda j, kk: (0, j))],
        out_specs=[sp["chan"], sp["b"], sp["c"], pl.BlockSpec((1, 2 * cb), lambda j, kk: (0, j))],
        out_shape=[jax.ShapeDtypeStruct((rows, nblk * nch), F32), jax.ShapeDtypeStruct(b_blk.shape, F32),
                   jax.ShapeDtypeStruct(c_blk.shape, F32), jax.ShapeDtypeStruct((1, nblk * 2 * cb), F32)],
        scratch_shapes=[pltpu.VMEM((SCAN_SEGS, 2 * cb), F32), pltpu.VMEM((tt, 2 * cb), F32), pltpu.VMEM((tt, 2 * cb), F32)],
        compiler_params=_params(("parallel", "arbitrary")),
    )(a_conj, ends, dy, dy, u, s, s, s_entry, b_blk, c_blk, dd)


def _glu_fwd(ys, u, dd, w_glu, b_glu, *, tm, name):
    rows, w = ys.shape

    def body(ys_ref, u_ref, dd_ref, w_ref, b_ref, y0_ref, t_ref, y2_ref):
        y0 = ys_ref[...] + dd_ref[...] * u_ref[...]
        y1 = _gelu(y0)
        t = _dot(y1.astype(BF16), w_ref[...], 1, 0) + b_ref[...]
        y0_ref[...] = y0
        t_ref[...] = t
        y2_ref[...] = (y1 * _sigmoid(t)).astype(BF16)

    row = pl.BlockSpec((tm, w), lambda i: (i, 0))
    vec = pl.BlockSpec((1, w), lambda i: (0, 0))
    return pl.pallas_call(
        body, name=name, grid=(rows // tm,),
        in_specs=[row, row, vec, pl.BlockSpec((w, w), lambda i: (0, 0)), vec],
        out_specs=[row, row, row],
        out_shape=[jax.ShapeDtypeStruct((rows, w), F32), jax.ShapeDtypeStruct((rows, w), F32),
                   jax.ShapeDtypeStruct((rows, w), BF16)],
        compiler_params=_params(("parallel",)),
    )(ys, u, dd, w_glu, b_glu)


def _glu_bwd(dy2, y0, t, u, w_glu, *, tm, name):
    rows, w = y0.shape

    def body(dy2_ref, y0_ref, t_ref, u_ref, w_ref, dy0_ref, dt_ref, y1_ref, db_ref, dd_ref):
        i = pl.program_id(0)
        y0 = y0_ref[...]
        y1 = _gelu(y0)
        sg = _sigmoid(t_ref[...])
        dy2v = dy2_ref[...]
        dt = dy2v * y1 * sg * (1.0 - sg)
        dy1 = dy2v * sg + _dot(dt.astype(BF16), w_ref[...], 1, 1)
        dy0 = dy1 * _gelu_grad(y0)
        dy0_ref[...] = dy0
        dt_ref[...] = dt.astype(BF16)
        y1_ref[...] = y1.astype(BF16)

        @pl.when(i == 0)
        def _():
            db_ref[...] = jnp.zeros_like(db_ref)
            dd_ref[...] = jnp.zeros_like(dd_ref)

        db_ref[...] += jnp.sum(dt, axis=0, keepdims=True)
        dd_ref[...] += jnp.sum(dy0 * u_ref[...], axis=0, keepdims=True)

    row = pl.BlockSpec((tm, w), lambda i: (i, 0))
    vec = pl.BlockSpec((1, w), lambda i: (0, 0))
    return pl.pallas_call(
        body, name=name, grid=(rows // tm,),
        in_specs=[row, row, row, row, pl.BlockSpec((w, w), lambda i: (0, 0))],
        out_specs=[row, row, row, vec, vec],
        out_shape=[jax.ShapeDtypeStruct((rows, w), F32), jax.ShapeDtypeStruct((rows, w), BF16),
                   jax.ShapeDtypeStruct((rows, w), BF16), jax.ShapeDtypeStruct((1, w), F32),
                   jax.ShapeDtypeStruct((1, w), F32)],
        compiler_params=_params(("arbitrary",)),
    )(dy2, y0, t, u, w_glu)


ATTN_TILE = 2048


def _attn_geometry(rows, d):
    sb = ATTN_Q * d
    tr = max(sb, min(ATTN_TILE, rows))
    assert rows % tr == 0 and tr % sb == 0, (rows, d)
    return sb, tr, rows // tr, tr // sb


def _attn_masks():
    qi = lax.broadcasted_iota(jnp.int32, (2 * ATTN_Q, 2 * ATTN_Q), 0) % ATTN_Q
    kj = lax.broadcasted_iota(jnp.int32, (2 * ATTN_Q, 2 * ATTN_Q), 1)
    own_ok = jnp.logical_and(kj >= ATTN_Q, kj - ATTN_Q <= qi)
    prev_ok = jnp.logical_and(kj < ATTN_Q, kj >= qi)
    bias_first = jnp.where(own_ok, 0.0, NEG_INF)
    bias_other = jnp.where(jnp.logical_or(own_ok, prev_ok), 0.0, NEG_INF)
    head0 = lax.broadcasted_iota(jnp.int32, (ATTN_Q, LANES), 1) < ATTN_HEAD_DIM
    return bias_first, bias_other, head0


def _attn_rows(base, n, d):
    return pl.ds(pl.multiple_of(base, ATTN_Q), n) if d == 1 else pl.ds(base, n, stride=d)


def _stack_heads(v, head0):
    return jnp.concatenate([jnp.where(head0, v, 0.0), jnp.where(head0, 0.0, v)], axis=0)


def _unstack_heads(v, head0):
    return jnp.where(head0, v[:ATTN_Q], v[ATTN_Q:])


def _fill_keys(buf, prev_ref, cur_ref, sb):
    buf[pl.ds(0, sb), :] = prev_ref[...]
    buf[pl.ds(sb, cur_ref.shape[0]), :] = cur_ref[...]


def _attn_fwd(qkv, g, d, *, name):
    rows = qkv.shape[0]
    sb, tr, ntiles, nsub = _attn_geometry(rows, d)
    qc, kc, vc = 2 * g, 6 + 2 * g, 12 + 2 * g
    scale = ATTN_HEAD_DIM ** -0.5

    def body(q_ref, kc_ref, kp_ref, vc_ref, vp_ref, o_ref, lse_ref, kbuf, vbuf):
        n = pl.program_id(0)
        _fill_keys(kbuf, kp_ref, kc_ref, sb)
        _fill_keys(vbuf, vp_ref, vc_ref, sb)
        bias_first, bias_other, head0 = _attn_masks()

        def per_block(idx, carry):
            j, r = idx // d, idx % d
            base = j * sb + r
            bias = jnp.where(jnp.logical_and(n == 0, j == 0), bias_first, bias_other)
            qrows = _attn_rows(base, ATTN_Q, d)
            krows = _attn_rows(base, 2 * ATTN_Q, d)
            qs = (_stack_heads(q_ref[qrows, :], head0) * scale).astype(BF16)
            s = _dot(qs, kbuf[krows, :].astype(BF16), 1, 1) + bias
            mx = jnp.max(s, axis=-1, keepdims=True)
            p = jnp.exp(s - mx)
            den = jnp.sum(p, axis=-1, keepdims=True)
            pv = _dot(p.astype(BF16), vbuf[krows, :].astype(BF16), 1, 0) / den
            o_ref[qrows, :] = _unstack_heads(pv, head0)
            lse_ref[qrows, :] = _unstack_heads(jnp.broadcast_to(mx + jnp.log(den), (2 * ATTN_Q, LANES)), head0)
            return carry

        lax.fori_loop(0, nsub * d, per_block, 0, unroll=8)

    def cur(col):
        return pl.BlockSpec((tr, LANES), lambda n, hp: (n, col + hp))

    def prev(col):
        return pl.BlockSpec((sb, LANES), lambda n, hp: (jnp.maximum(n * nsub - 1, 0), col + hp))

    out_spec = pl.BlockSpec((tr, LANES), lambda n, hp: (n, hp))
    return pl.pallas_call(
        body, name=name, grid=(ntiles, 2),
        in_specs=[cur(qc), cur(kc), prev(kc), cur(vc), prev(vc)],
        out_specs=[out_spec, out_spec],
        out_shape=[jax.ShapeDtypeStruct((rows, 2 * LANES), F32), jax.ShapeDtypeStruct((rows, 2 * LANES), F32)],
        scratch_shapes=[pltpu.VMEM((sb + tr, LANES), F32), pltpu.VMEM((sb + tr, LANES), F32)],
        compiler_params=_params(("parallel", "parallel")),
    )(qkv, qkv, qkv, qkv, qkv)


def _attn_merge(outs, lses, *, tm, name):
    rows, w = outs[0].shape

    def body(o0, o1, o2, l0, l1, l2, o_ref, lse_ref):
        a0, a1, a2 = l0[...], l1[...], l2[...]
        mx = jnp.maximum(jnp.maximum(a0, a1), a2)
        e0, e1, e2 = jnp.exp(a0 - mx), jnp.exp(a1 - mx), jnp.exp(a2 - mx)
        den = e0 + e1 + e2
        o_ref[...] = (e0 / den) * o0[...] + (e1 / den) * o1[...] + (e2 / den) * o2[...]
        lse_ref[...] = mx + jnp.log(den)

    row = pl.BlockSpec((tm, w), lambda i: (i, 0))
    return pl.pallas_call(
        body, name=name, grid=(rows // tm,), in_specs=[row] * 6, out_specs=[row, row],
        out_shape=[jax.ShapeDtypeStruct((rows, w), F32), jax.ShapeDtypeStruct((rows, w), F32)],
        compiler_params=_params(("parallel",)),
    )(*outs, *lses)


def _attn_bwd(qkv, do, o, lse, g, d, prev, *, name):
    rows = qkv.shape[0]
    sb, tr, ntiles, nsub = _attn_geometry(rows, d)
    qc, kc, vc = 2 * g, 6 + 2 * g, 12 + 2 * g
    scale = ATTN_HEAD_DIM ** -0.5

    def body(q_ref, kc_ref, kp_ref, vc_ref, vp_ref, do_ref, o_ref, lse_ref, dq_ref, dk_ref, dv_ref,
             kbuf, vbuf, dk_acc, dv_acc):
        n = pl.program_id(1)

        @pl.when(n == 0)
        def _():
            dk_acc[pl.ds(0, tr), :] = jnp.zeros((tr, LANES), F32)
            dv_acc[pl.ds(0, tr), :] = jnp.zeros((tr, LANES), F32)

        @pl.when(n < ntiles)
        def _():
            dk_acc[pl.ds(tr, tr), :] = jnp.zeros((tr, LANES), F32)
            dv_acc[pl.ds(tr, tr), :] = jnp.zeros((tr, LANES), F32)
            _fill_keys(kbuf, kp_ref, kc_ref, sb)
            _fill_keys(vbuf, vp_ref, vc_ref, sb)
            bias_first, bias_other, head0 = _attn_masks()
            lane = lax.broadcasted_iota(jnp.int32, (ATTN_Q, LANES), 1)

            def per_block(idx, carry):
                j, r = idx // d, idx % d
                base = j * sb + r
                bias = jnp.where(jnp.logical_and(n == 0, j == 0), bias_first, bias_other)
                qrows = _attn_rows(base, ATTN_Q, d)
                krows = _attn_rows(base, 2 * ATTN_Q, d)
                arows = _attn_rows(base + (tr - sb), 2 * ATTN_Q, d)
                qs = (_stack_heads(q_ref[qrows, :], head0) * scale).astype(BF16)
                dos = _stack_heads(do_ref[qrows, :], head0)
                dosb = dos.astype(BF16)
                ov = o_ref[qrows, :]
                delta = jnp.sum(dos * jnp.concatenate([ov, ov], axis=0), axis=-1, keepdims=True)
                lsev = lse_ref[qrows, :]
                lse_s = jnp.concatenate(
                    [jnp.sum(jnp.where(lane == h * ATTN_HEAD_DIM, lsev, 0.0), axis=-1, keepdims=True) for h in range(2)], axis=0)
                kb = kbuf[krows, :].astype(BF16)
                vb = vbuf[krows, :].astype(BF16)
                p = jnp.exp(_dot(qs, kb, 1, 1) + bias - lse_s)
                ds = (p * (_dot(dosb, vb, 1, 1) - delta)).astype(BF16)
                dq_ref[qrows, :] = _unstack_heads(_dot(ds, kb, 1, 0), head0) * scale
                dk_acc[arows, :] += _dot(ds, qs, 0, 0)
                dv_acc[arows, :] += _dot(p.astype(BF16), dosb, 0, 0)
                return carry

            lax.fori_loop(0, nsub * d, per_block, 0, unroll=4)

        dk_ref[...] = dk_acc[pl.ds(0, tr), :]
        dv_ref[...] = dv_acc[pl.ds(0, tr), :]
        dk_acc[pl.ds(0, tr), :] = dk_acc[pl.ds(tr, tr), :]
        dv_acc[pl.ds(0, tr), :] = dv_acc[pl.ds(tr, tr), :]

    def cur(n):
        return jnp.minimum(n, ntiles - 1)

    def spec(col, prev):
        if prev:
            return pl.BlockSpec((sb, LANES), lambda hp, n: (jnp.maximum(cur(n) * nsub - 1, 0), col + hp))
        return pl.BlockSpec((tr, LANES), lambda hp, n: (cur(n), col + hp))

    row_spec = pl.BlockSpec((tr, LANES), lambda hp, n: (cur(n), hp))
    dq_out = pl.BlockSpec((tr, LANES), lambda hp, n: (cur(n), 2 * g + hp))
    kv_out = pl.BlockSpec((tr, LANES), lambda hp, n: (jnp.maximum(n - 1, 0), 2 * g + hp))
    shape = jax.ShapeDtypeStruct((rows, len(ATTN_PATTERNS) * 2 * LANES), F32)
    ins = [qkv, qkv, qkv, qkv, qkv, do, o, lse]
    in_specs = [spec(qc, False), spec(kc, False), spec(kc, True), spec(vc, False), spec(vc, True),
                row_spec, row_spec, row_spec]
    aliases = {}
    if prev is not None:
        aliases = {len(ins) + t: t for t in range(3)}
        ins = ins + list(prev)
        in_specs = in_specs + [ANY] * 3
    n_in = len(ins)

    def entry(*refs):
        body(*refs[:8], *refs[n_in:])

    return pl.pallas_call(
        entry, name=name, grid=(2, ntiles + 1),
        in_specs=in_specs,
        out_specs=[dq_out, kv_out, kv_out],
        out_shape=[shape, shape, shape],
        input_output_aliases=aliases,
        scratch_shapes=[pltpu.VMEM((sb + tr, LANES), F32), pltpu.VMEM((sb + tr, LANES), F32),
                        pltpu.VMEM((2 * tr, LANES), F32), pltpu.VMEM((2 * tr, LANES), F32)],
        compiler_params=_params(("parallel", "arbitrary")),
    )(*ins)


def _mem_probs(q, k):
    s = _dot(q.astype(BF16), k.astype(BF16), 1, 1) * (MEM_HEAD_DIM ** -0.5)
    e = jnp.exp(s - jnp.max(s, axis=-1, keepdims=True))
    return e / jnp.sum(e, axis=-1, keepdims=True)


def _mem_attn_fwd(mq, kv, *, tq, name):
    rows = mq.shape[0]

    def body(q_ref, k_ref, v_ref, o_ref):
        p = _mem_probs(q_ref[...], k_ref[...])
        o_ref[...] = _dot(p.astype(BF16), v_ref[...].astype(BF16), 1, 0)

    return pl.pallas_call(
        body, name=name, grid=(rows // tq, MEM_HEADS),
        in_specs=[pl.BlockSpec((tq, LANES), lambda i, h: (i, h)),
                  pl.BlockSpec((MEM_LEN, LANES), lambda i, h: (0, h)),
                  pl.BlockSpec((MEM_LEN, LANES), lambda i, h: (0, MEM_HEADS + h))],
        out_specs=pl.BlockSpec((tq, LANES), lambda i, h: (i, h)),
        out_shape=jax.ShapeDtypeStruct((rows, MEM_HEADS * LANES), F32),
        compiler_params=_params(("parallel", "parallel")),
    )(mq, kv, kv)


def _mem_attn_bwd(mq, kv, dmo, *, tq, name):
    rows = mq.shape[0]
    scale = MEM_HEAD_DIM ** -0.5

    def body(q_ref, k_ref, v_ref, do_ref, dq_ref, dk_ref, dv_ref):
        i = pl.program_id(1)
        qb = q_ref[...].astype(BF16)
        kb = k_ref[...].astype(BF16)
        vb = v_ref[...].astype(BF16)
        dob = do_ref[...].astype(BF16)
        p = _mem_probs(q_ref[...], k_ref[...])
        dp = _dot(dob, vb, 1, 1)
        ds = (p * (dp - jnp.sum(p * dp, axis=-1, keepdims=True)) * scale).astype(BF16)
        dq_ref[...] = _dot(ds, kb, 1, 0).astype(dq_ref.dtype)

        @pl.when(i == 0)
        def _():
            dk_ref[...] = jnp.zeros_like(dk_ref)
            dv_ref[...] = jnp.zeros_like(dv_ref)

        dk_ref[...] += _dot(ds, qb, 0, 0)
        dv_ref[...] += _dot(p.astype(BF16), dob, 0, 0)

    kv_out = pl.BlockSpec((MEM_LEN, LANES), lambda h, i: (0, h))
    kv_shape = jax.ShapeDtypeStruct((MEM_LEN, MEM_HEADS * LANES), F32)
    return pl.pallas_call(
        body, name=name, grid=(MEM_HEADS, rows // tq),
        in_specs=[pl.BlockSpec((tq, LANES), lambda h, i: (i, h)),
                  pl.BlockSpec((MEM_LEN, LANES), lambda h, i: (0, h)),
                  pl.BlockSpec((MEM_LEN, LANES), lambda h, i: (0, MEM_HEADS + h)),
                  pl.BlockSpec((tq, LANES), lambda h, i: (i, h))],
        out_specs=[pl.BlockSpec((tq, LANES), lambda h, i: (i, h)), kv_out, kv_out],
        out_shape=[jax.ShapeDtypeStruct((rows, MEM_HEADS * LANES), BF16), kv_shape, kv_shape],
        compiler_params=_params(("parallel", "arbitrary")),
    )(mq, kv, kv, dmo)


def _resident(shape):
    return pl.BlockSpec(shape, lambda i: (0, 0), pipeline_mode=pl.Buffered(1))


def _branch_merge_fwd(acts, wts, zg, b_gate, *, tm, name):
    rows = zg.shape[0]
    d = wts[0].shape[0]

    def body(s_ref, a_ref, m_ref, ws_ref, wa_ref, wm_ref, zg_ref, b_ref, o_ref):
        gt = _sigmoid(zg_ref[...] + b_ref[...])
        acc = None
        for k, (x_ref, w_ref) in enumerate(((s_ref, ws_ref), (a_ref, wa_ref), (m_ref, wm_ref))):
            term = gt[:, k * d:(k + 1) * d] * _dot(x_ref[...].astype(BF16), w_ref[...], 1, 1)
            acc = term if acc is None else acc + term
        o_ref[...] = acc.astype(BF16)

    return pl.pallas_call(
        body, name=name, grid=(rows // tm,),
        in_specs=[pl.BlockSpec((tm, x.shape[1]), lambda i: (i, 0)) for x in acts] + [_resident(w.shape) for w in wts]
        + [pl.BlockSpec((tm, 3 * d), lambda i: (i, 0)), pl.BlockSpec((1, 3 * d), lambda i: (0, 0))],
        out_specs=pl.BlockSpec((tm, d), lambda i: (i, 0)), out_shape=jax.ShapeDtypeStruct((rows, d), BF16),
        compiler_params=_params(("parallel",)),
    )(*acts, *wts, zg, b_gate)


def _branch_merge_bwd(dmerged, acts, wts, zg, b_gate, *, tm, name, carry=None):
    rows = zg.shape[0]
    d = wts[0].shape[0]

    def body(dm_ref, s_ref, a_ref, m_ref, ws_ref, wa_ref, wm_ref, zg_ref, b_ref,
             ds_ref, da_ref, dmm_ref, dws_ref, dwa_ref, dwm_ref, dzg_ref, db_ref):
        i = pl.program_id(0)

        @pl.when(i == 0)
        def _():
            for r in (dws_ref, dwa_ref, dwm_ref, db_ref):
                r[...] = jnp.zeros_like(r)

        gt = _sigmoid(zg_ref[...] + b_ref[...])
        dm = dm_ref[...]
        groups = ((s_ref, ws_ref, ds_ref, dws_ref), (a_ref, wa_ref, da_ref, dwa_ref), (m_ref, wm_ref, dmm_ref, dwm_ref))
        for k, (x_ref, w_ref, dx_ref, dw_ref) in enumerate(groups):
            cs = pl.ds(k * d, d)
            gk = gt[:, k * d:(k + 1) * d]
            xb = x_ref[...].astype(BF16)
            br = _dot(xb, w_ref[...], 1, 1)
            dbr = (dm * gk).astype(BF16)
            dx_ref[...] = _dot(dbr, w_ref[...], 1, 0)
            dw_ref[...] += _dot(dbr, xb, 0, 0)
            dzg = dm * br * gk * (1.0 - gk)
            dzg_ref[:, cs] = dzg.astype(BF16)
            db_ref[:, cs] += jnp.sum(dzg, axis=0, keepdims=True)

    row = lambda w: pl.BlockSpec((tm, w), lambda i: (i, 0))
    whole = lambda shape: pl.BlockSpec(shape, lambda i: (0, 0))
    res = _call_with_carry(
        body, carry, name=name, grid=(rows // tm,),
        in_specs=[row(d)] + [row(x.shape[1]) for x in acts] + [_resident(w.shape) for w in wts] + [row(3 * d), whole((1, 3 * d))],
        out_specs=[row(x.shape[1]) for x in acts] + [whole(w.shape) for w in wts] + [row(3 * d), whole((1, 3 * d))],
        out_shape=[jax.ShapeDtypeStruct(x.shape, F32) for x in acts] + [jax.ShapeDtypeStruct(w.shape, F32) for w in wts]
        + [jax.ShapeDtypeStruct((rows, 3 * d), BF16), jax.ShapeDtypeStruct((1, 3 * d), F32)],
        scratch=[], operands=[dmerged, *acts, *wts, zg, b_gate], semantics=("arbitrary",))
    return tuple(res) if carry is None else (tuple(res[:8]), list(res[8:]))


def _adamw(w, g, m, v, *, tr, name):
    rows, cols = w.shape[-2:]
    assert rows % tr == 0, (name, rows, tr)

    def body(w_ref, g_ref, m_ref, v_ref, g_out, d_ref, nm_ref, nv_ref):
        gv = g_ref[...]
        m2 = ADAM_B1 * m_ref[...] + (1.0 - ADAM_B1) * gv
        v2 = ADAM_B2 * v_ref[...] + (1.0 - ADAM_B2) * (gv * gv)
        m_hat = m2 / (1.0 - ADAM_B1 ** ADAM_STEP)
        v_hat = v2 / (1.0 - ADAM_B2 ** ADAM_STEP)
        g_out[...] = gv
        d_ref[...] = -ADAM_LR * (m_hat / (jnp.sqrt(v_hat) + ADAM_EPS) + ADAM_WD * w_ref[...])
        nm_ref[...] = m2
        nv_ref[...] = v2

    flat = pl.BlockSpec((tr, cols), lambda i: (i, 0))
    blk = flat if w.ndim == 2 else pl.BlockSpec((None, tr, cols), lambda i: (0, i, 0))
    shape = jax.ShapeDtypeStruct(w.shape, F32)
    return pl.pallas_call(
        body, name=name, grid=(rows // tr,), in_specs=[blk, flat, blk, blk], out_specs=[blk] * 4,
        out_shape=[shape] * 4, compiler_params=_params(("parallel",)),
    )(w, g, m, v)


ANY = pl.BlockSpec(memory_space=pl.ANY)


def _position():
    return lax.axis_index("x"), lax.axis_index("y"), lax.axis_index("c")


def _other_chips(x, y):
    return ((1 - x, y), (x, 1 - y), (1 - x, 1 - y))


def _remote(src, dst, send_sem, recv_sem, dev):
    return pltpu.make_async_remote_copy(src_ref=src, dst_ref=dst, send_sem=send_sem, recv_sem=recv_sem,
                                        device_id=dev, device_id_type=MESH)


def _gather_exchange(shards):
    nb = len(shards)

    def rows_of(i, owner, core):
        rs = shards[i].shape[0]
        return pl.ds(pl.multiple_of(owner * rs + core * (rs // 2), 16), rs // 2)

    def first_leg(ins, outs, send_sems, recv_sems, i, j):
        x, y, c = _position()
        px, py = _other_chips(x, y)[j]
        half = shards[i].shape[0] // 2
        mine = ins[i].at[pl.ds(pl.multiple_of(c * half, 16), half)]
        return _remote(mine, outs[i].at[rows_of(i, 2 * x + y, c)], send_sems.at[i, j], recv_sems.at[i, j], (px, py, c))

    def passed_on(outs, send_sems, recv_sems, i, j, core):
        x, y, c = _position()
        px, py = _other_chips(x, y)[j]
        rows = outs[i].at[rows_of(i, 2 * px + py, core)]
        return _remote(rows, rows, send_sems.at[i, 3 + j], recv_sems.at[i, 3 + j], (x, y, 1 - c))

    def own_block(ins, outs, send_sems, recv_sems, i):
        x, y, c = _position()
        rs = shards[i].shape[0]
        place = outs[i].at[pl.ds(pl.multiple_of((2 * x + y) * rs, 16), rs)]
        return _remote(ins[i], place, send_sems.at[i, 6], recv_sems.at[i, 6], (x, y, 1 - c))

    def start(ins, outs, send_sems, recv_sems):
        for i in range(nb):
            own_block(ins, outs, send_sems, recv_sems, i).start()
            for j in range(3):
                first_leg(ins, outs, send_sems, recv_sems, i, j).start()

    def finish(ins, outs, send_sems, recv_sems):
        x, y, c = _position()
        for i in range(nb):
            for j, (px, py) in enumerate(_other_chips(x, y)):
                landed = outs[i].at[rows_of(i, 2 * px + py, c)]
                _remote(landed, landed, send_sems.at[i, j], recv_sems.at[i, j], (px, py, c)).wait_recv()
                passed_on(outs, send_sems, recv_sems, i, j, c).start()
        for i in range(nb):
            own_block(ins, outs, send_sems, recv_sems, i).wait()
            for j in range(3):
                passed_on(outs, send_sems, recv_sems, i, j, 1 - c).wait_recv()
        for i in range(nb):
            for j in range(3):
                first_leg(ins, outs, send_sems, recv_sems, i, j).wait_send()
                passed_on(outs, send_sems, recv_sems, i, j, c).wait_send()

    return _Exchange(ins=list(shards), outs=[jax.ShapeDtypeStruct((N_CHIPS * s.shape[0], s.shape[1]), s.dtype) for s in shards],
                     aliases={}, sems=[(nb, 7), (nb, 7)], start=start, finish=finish)


def _run_exchange(ex, *, name):
    n_in, n_out = len(ex.ins), len(ex.outs)

    def body(*refs):
        c_in, c_out, sems = refs[:n_in], refs[n_in:n_in + n_out], refs[n_in + n_out:]
        ex.start(c_in, c_out, *sems)
        ex.finish(c_in, c_out, *sems)

    return pl.pallas_call(
        body, name=name, in_specs=[ANY] * n_in, out_specs=[ANY] * n_out, out_shape=list(ex.outs),
        input_output_aliases=dict(ex.aliases),
        scratch_shapes=[pltpu.SemaphoreType.DMA(s) for s in ex.sems],
    )(*ex.ins)


def _row_tile(rows):
    return max(t for t in range(16, min(rows, 512) + 1, 16) if rows % t == 0)


def _halves_exchange(grads):
    nb = len(grads)

    def copies(ins, outs, send_sems, recv_sems):
        x, y, c = _position()
        return [_remote(ins[i].at[:, 1 - c], outs[i], send_sems.at[i], recv_sems.at[i], (x, y, 1 - c)) for i in range(nb)]

    def start(ins, outs, send_sems, recv_sems):
        for cp in copies(ins, outs, send_sems, recv_sems):
            cp.start()

    def finish(ins, outs, send_sems, recv_sems):
        for cp in copies(ins, outs, send_sems, recv_sems):
            cp.wait()

    return _Exchange(ins=list(grads), outs=[jax.ShapeDtypeStruct((N_CHIPS, g.shape[2], g.shape[3]), F32) for g in grads],
                     aliases={}, sems=[(nb,), (nb,)], start=start, finish=finish)


def _join_exchanges(parts):
    assert all(not ex.aliases for ex in parts)

    def split(refs, counts):
        out, at = [], 0
        for k in counts:
            out.append(refs[at:at + k])
            at += k
        return out

    def run(which):
        def go(ins, outs, *sems):
            for ex, i, o, s in zip(parts, split(ins, [len(ex.ins) for ex in parts]), split(outs, [len(ex.outs) for ex in parts]),
                                   split(sems, [len(ex.sems) for ex in parts])):
                getattr(ex, which)(i, o, *s)
        return go

    return _Exchange(ins=[a for ex in parts for a in ex.ins], outs=[a for ex in parts for a in ex.outs], aliases={},
                     sems=[s for ex in parts for s in ex.sems], start=run("start"), finish=run("finish"))


def _pair_sum(g4, got, c_arr, *, name):
    _, _, half, cols = g4.shape
    tr = _row_tile(half)

    def body(c_ref, g_ref, t_ref, p_ref, pb_ref):
        sm = g_ref[...] + t_ref[...]
        p_ref[...] = sm
        pb_ref[...] = sm.astype(BF16)

    blk = pl.BlockSpec((None, tr, cols), lambda j, i, c_ref: (j, i, 0))
    grid_spec = pltpu.PrefetchScalarGridSpec(
        num_scalar_prefetch=1, grid=(N_CHIPS, half // tr),
        in_specs=[pl.BlockSpec((None, None, tr, cols), lambda j, i, c_ref: (j, c_ref[0], i, 0)), blk],
        out_specs=[blk, blk])
    return pl.pallas_call(
        body, name=name, grid_spec=grid_spec,
        out_shape=[jax.ShapeDtypeStruct((N_CHIPS, half, cols), F32), jax.ShapeDtypeStruct((N_CHIPS, half, cols), BF16)],
        compiler_params=_params(("parallel", "parallel")),
    )(c_arr, g4, got)


def _scatter_exchange(parts):
    nb = len(parts)

    def copies(ins, outs, send_sems, recv_sems):
        x, y, c = _position()
        return [_remote(ins[i].at[2 * px + py], outs[i].at[j], send_sems.at[i, j], recv_sems.at[i, j], (px, py, c))
                for i in range(nb) for j, (px, py) in enumerate(_other_chips(x, y))]

    def start(ins, outs, send_sems, recv_sems):
        for cp in copies(ins, outs, send_sems, recv_sems):
            cp.start()

    def finish(ins, outs, send_sems, recv_sems):
        for cp in copies(ins, outs, send_sems, recv_sems):
            cp.wait()

    return _Exchange(ins=list(parts), outs=[jax.ShapeDtypeStruct((3,) + p.shape[1:], p.dtype) for p in parts],
                     aliases={}, sems=[(nb, 3), (nb, 3)], start=start, finish=finish)


def _owner_sum(p, got, chip_arr, c_arr, *, replicated, name):
    _, half, cols = p.shape
    tr = _row_tile(half)

    def body(chip_ref, c_ref, p_ref, r_ref, o_ref):
        o_ref[...] = ((p_ref[...] + r_ref[0].astype(F32)) + r_ref[1].astype(F32)) + r_ref[2].astype(F32)

    if replicated:
        out_spec = pl.BlockSpec((None, None, tr, cols), lambda i, chip_ref, c_ref: (chip_ref[0], c_ref[0], i, 0))
        out_shape = jax.ShapeDtypeStruct((N_CHIPS, 2, half, cols), F32)
    else:
        out_spec = pl.BlockSpec((None, tr, cols), lambda i, chip_ref, c_ref: (c_ref[0], i, 0))
        out_shape = jax.ShapeDtypeStruct((2, half, cols), F32)
    grid_spec = pltpu.PrefetchScalarGridSpec(
        num_scalar_prefetch=2, grid=(half // tr,),
        in_specs=[pl.BlockSpec((None, tr, cols), lambda i, chip_ref, c_ref: (chip_ref[0], i, 0)),
                  pl.BlockSpec((3, tr, cols), lambda i, chip_ref, c_ref: (0, i, 0))],
        out_specs=out_spec)
    return pl.pallas_call(
        body, name=name, grid_spec=grid_spec, out_shape=out_shape,
        compiler_params=_params(("parallel",)),
    )(chip_arr, c_arr, p, got)


def _share_reduced(bufs):
    nb = len(bufs) - 1

    def body(*refs):
        outs = refs[nb + 1:2 * nb + 2]
        send_sems, recv_sems = refs[2 * nb + 2:]
        x, y, c = _position()
        chip = 2 * x + y
        sends = []
        for i in range(nb):
            cp = _remote(outs[i].at[c], outs[i].at[c], send_sems.at[i], recv_sems.at[i], (x, y, 1 - c))
            cp.start()
            sends.append(cp)
        small = outs[nb]
        peers = [(fx, fy, fc) for fx in (0, 1) for fy in (0, 1) for fc in (0, 1) if fx + fy + fc > 0]
        for k, (fx, fy, fc) in enumerate(peers):
            dev = (x ^ fx, y ^ fy, c ^ fc)
            cp = _remote(small.at[chip, c], small.at[chip, c], send_sems.at[nb + k], recv_sems.at[nb + k], dev)
            cp.start()
            sends.append(cp)
        for i in range(nb):
            dst = outs[i].at[1 - c]
            _remote(dst, dst, send_sems.at[i], recv_sems.at[i], (x, y, 1 - c)).wait_recv()
        for k, (fx, fy, fc) in enumerate(peers):
            dst = small.at[2 * (x ^ fx) + (y ^ fy), c ^ fc]
            _remote(dst, dst, send_sems.at[nb + k], recv_sems.at[nb + k], (x ^ fx, y ^ fy, c ^ fc)).wait_recv()
        for cp in sends:
            cp.wait_send()

    n_all = nb + 1
    return pl.pallas_call(
        body, name="grad_share_reduced", in_specs=[ANY] * n_all, out_specs=[ANY] * n_all,
        out_shape=[jax.ShapeDtypeStruct(b.shape, b.dtype) for b in bufs],
        input_output_aliases={i: i for i in range(n_all)},
        scratch_shapes=[pltpu.SemaphoreType.DMA((nb + 7,)), pltpu.SemaphoreType.DMA((nb + 7,))],
    )(*bufs)


class _GradReducer:
    def __init__(self, c_arr, chip_arr):
        self.c_arr, self.chip_arr = c_arr, chip_arr
        self.full, self.pairs, self.landed = {}, {}, {}

    def swap(self, names, grads):
        for n, g in zip(names, grads):
            self.full[n] = g.reshape(N_CHIPS, 2, g.shape[0] // (2 * N_CHIPS), g.shape[1])
        return _halves_exchange([self.full[n] for n in names])

    def swapped(self, names, bufs):
        for n, t in zip(names, bufs):
            self.pairs[n] = _pair_sum(self.full[n], t, self.c_arr, name="grad_pair_sum_" + n)

    def scatter(self, names):
        return _scatter_exchange([self.pairs[n][1] for n in names])

    def collect(self, names, bufs):
        self.landed.update(zip(names, bufs))

    def swap_now(self, names, grads):
        self.swapped(names, _run_exchange(self.swap(names, grads), name="grad_exchange_" + names[0]))

    def finish(self, names, grads, order):
        self.swap_now(names, grads)
        self.collect(names, _run_exchange(self.scatter(names), name="grad_scatter_" + names[0]))
        totals = [_owner_sum(self.pairs[n][0], self.landed[n], self.chip_arr, self.c_arr, replicated=(n == order[-1]),
                             name="grad_owner_sum_" + n) for n in order]
        return _share_reduced(totals)


def _pack_small(vals):
    flat = jnp.concatenate([vals[name].reshape(-1) for name, _ in SMALL])
    return jnp.pad(flat, (0, N_CHIPS * SMALL_ROWS * 1024 - SMALL_ELEMS)).reshape(N_CHIPS * SMALL_ROWS, 1024)


def _unpack_small(buf):
    flat = buf.reshape(-1)
    out, off = {}, 0
    for name, shape in SMALL:
        n = int(np.prod(shape))
        out[name] = flat[off:off + n].reshape(shape)
        off += n
    return out


EARLY_REDUCED = (("w_down",), ("w_up",), ("w_o", "w_ssm_br", "w_attn_br", "w_mem_br", "w_glu", "w_mem_kv"), ("w_in",))


def _device_step(x, mem, tgt, w, p, *, shards, reducer):
    rows = x.shape[0]
    w = dict(w)
    early = EARLY_REDUCED
    gb = {}
    gather_pending = shards is not None

    def riding(*stages):
        if reducer is None or not stages:
            return None
        return _join_exchanges([reducer.swap(names, [gb[n] for n in names]) if kind == "swap" else reducer.scatter(names)
                                for kind, names in stages])

    def arrived(stages, res):
        if reducer is None or not stages:
            return res
        main, bufs = res
        for kind, names in stages:
            (reducer.swapped if kind == "swap" else reducer.collect)(names, bufs[:len(names)])
            bufs = bufs[len(names):]
        return main

    def fetching(names):
        return _gather_exchange([shards[n] for n in names]) if gather_pending else None

    def fetched(names, res):
        if not gather_pending:
            return res
        w.update(zip(names, res[1]))
        return res[0]

    first_use = (("w_in",), ("w_glu", "w_ssm_br", "w_attn_br", "w_mem_kv", "w_mem_br", "w_o", "w_up"), ("w_down",))
    g1, gm, g2 = p["norm1_g"], p["mem_norm_g"], p["norm2_g"]
    gf = p["final_g"].reshape(1, D_MODEL)
    ssm_args = (p["ssm_lambda_re"][0], p["ssm_lambda_im"][0], p["ssm_log_dt"][0], p["ssm_b_re"][0],
                p["ssm_b_im"][0], p["ssm_c_re"][0], p["ssm_c_im"][0])
    (a_lay, b_blk, c_blk), ssm_vjp = jax.vjp(_ssm_matrices, *ssm_args)
    a_conj = a_lay * _to_scan_layout(jnp.stack([jnp.ones((N_STATES,), F32), -jnp.ones((N_STATES,), F32)]))[None, :]
    dd = p["ssm_d"].reshape(1, SSM_WIDTH)
    mm = _matmul

    n1 = fetched(first_use[0], _rmsnorm_fwd(x, g1, tm=512, carry=fetching(first_use[0]), name="norm1"))
    win_t = w["w_in"]
    splits = ((OFF_U, OFF_QKV - OFF_U), (OFF_QKV, OFF_MQ - OFF_QKV), (OFF_MQ, OFF_ZG - OFF_MQ), (OFF_ZG, IN_WIDTH - OFF_ZG))
    u, qkv, mq, zg = fetched(first_use[1], _split_matmul(n1, win_t, splits, tm=512, carry=fetching(first_use[1]),
                                                         vmem=VMEM_LIMIT_WIDE_BYTES, name="in_proj"))

    u_i = _interleave(u)
    ends = _ssm_ends(a_lay, u_i, b_blk, transpose=False, reverse=False, tt=512, name="ssm_fwd_ends")
    s, ys_i, s_entry = _ssm_fwd(a_lay, u_i, b_blk, c_blk, ends, tt=512, name="ssm_fwd")
    ys = _deinterleave(ys_i)
    y0, tglu, y2 = _glu_fwd(ys, u, dd, w["w_glu"], p["b_glu"], tm=512, name="glu_fwd")

    outs, lses = [], []
    for g, (_, d) in enumerate(ATTN_PATTERNS):
        o_g, lse_g = _attn_fwd(qkv, g, d, name=f"attn_fwd_{g}")
        outs.append(o_g)
        lses.append(lse_g)
    o, lse = _attn_merge(outs, lses, tm=1024, name="attn_merge")

    mn = _rmsnorm_fwd(mem, gm, tm=MEM_LEN, name="mem_norm")
    kv = mm(mn, w["w_mem_kv"], m=MEM_LEN, n=1024, k=1024, tm=MEM_LEN, tn=1024, tk=1024, out_dtypes=(F32,), name="mem_kv")
    mo = _mem_attn_fwd(mq, kv, tq=1024, name="mem_attn_fwd")

    branch_acts = (y2, o, mo)
    branch_wts = (w["w_ssm_br"], w["w_attn_br"], w["w_mem_br"])
    merged = _branch_merge_fwd(branch_acts, branch_wts, zg, p["b_gate"], tm=256, name="branch_merge_fwd")
    h1, n2 = mm(merged, w["w_o"], m=rows, n=1024, k=1024, tm=1024, tn=1024, tk=1024, out_dtypes=(F32, BF16),
                aux=((x, "mn"), (g2, "row")), epilogue=_residual_norm_epilogue, name="out_proj")
    relu2 = lambda acc: (jnp.square(jnp.maximum(acc, 0.0)),)
    act = fetched(first_use[2], _sum_matmul([n2], w["w_up"], [0], tb=True, tm=512, out_dtype=BF16, epilogue=relu2,
                                            carry=fetching(first_use[2]), name="mlp_up"))
    dh2, d_gf, sq_err = _sum_matmul([act], w["w_down"], [0], tm=512, aux=((h1, "mn"), (tgt, "mn"), (gf, "row")),
                                    epilogue=_loss_head_epilogue, n_sums=2, name="mlp_down")
    loss = (0.5 / D_MODEL) * jnp.sum(sq_err)

    gs = {"final_g": d_gf.reshape(D_MODEL)}
    drelu2 = lambda acc, actv: (acc * (2.0 * jnp.sqrt(actv.astype(F32))),)
    dup = mm(dh2, w["w_down"], m=rows, n=D_FF, k=1024, tb=True, tm=1024, tn=2048, tk=1024, out_dtypes=(BF16,),
             aux=((act, "mn"),), epilogue=drelu2, name="d_act")
    gb["w_down"] = mm(act, dh2, m=D_FF, n=1024, k=rows, ta=True, tm=1024, tn=1024, tk=2048, out_dtypes=(F32,), name="dw_down")
    stages = (("swap", early[0]),)
    gb["w_up"] = arrived(stages, mm(dup, n2, m=D_FF, n=1024, k=rows, ta=True, tm=1024, tn=1024, tk=2048,
                                    out_dtypes=(F32,), carry=riding(*stages), name="dw_up"))
    stages = (("scatter", early[0]), ("swap", early[1]))
    dh1, gs["norm2_g"] = arrived(stages, _sum_matmul([dup], w["w_up"], [0], tm=512, aux=((h1, "mn"), (dh2, "mn"), (g2, "row")),
                                                     epilogue=_rmsnorm_bwd_epilogue, n_sums=1, carry=riding(*stages), name="d_n2"))
    dmerged = mm(dh1, w["w_o"], m=rows, n=1024, k=1024, tb=True, tm=1024, tn=1024, tk=1024, out_dtypes=(F32,), name="d_merged")
    gb["w_o"] = mm(merged, dh1, m=1024, n=1024, k=rows, ta=True, tm=1024, tn=1024, tk=2048, out_dtypes=(F32,), name="dw_o")
    stages = (("scatter", early[1]),)
    (dy2, do, dmo, gb["w_ssm_br"], gb["w_attn_br"], gb["w_mem_br"], dzg, gs["b_gate"]) = arrived(stages, _branch_merge_bwd(
        dmerged, branch_acts, branch_wts, zg, p["b_gate"], tm=256, carry=riding(*stages), name="branch_merge_bwd"))

    dy0, dt, y1, gs["b_glu"], d_dd = _glu_bwd(dy2, y0, tglu, u, w["w_glu"], tm=512, name="glu_bwd")
    gs["ssm_d"] = d_dd.reshape(1, SSM_GROUPS, SSM_GROUP_SIZE)
    gb["w_glu"] = mm(y1, dt, m=512, n=512, k=rows, ta=True, tm=512, tn=512, tk=1024, out_dtypes=(F32,), name="dw_glu")
    dy0_i = _interleave(dy0)
    lam_ends = _ssm_ends(a_conj, dy0_i, c_blk, transpose=True, reverse=True, tt=512, name="ssm_bwd_ends")
    du_i, d_b_blk, d_c_blk, d_a_lay = _ssm_bwd(a_conj, dy0_i, u_i, s, s_entry, b_blk, c_blk, dd, lam_ends, tt=512,
                                                name="ssm_bwd")
    du = _deinterleave(du_i)
    d_ssm = ssm_vjp((d_a_lay, d_b_blk, d_c_blk))
    for name, val in zip(("ssm_lambda_re", "ssm_lambda_im", "ssm_log_dt", "ssm_b_re", "ssm_b_im", "ssm_c_re", "ssm_c_im"), d_ssm):
        gs[name] = val[None]

    dqkv = None
    for g, (_, d) in enumerate(ATTN_PATTERNS):
        dqkv = _attn_bwd(qkv, do, o, lse, g, d, dqkv, name=f"attn_bwd_{g}")

    dmq, dmk, dmv = _mem_attn_bwd(mq, kv, dmo, tq=1024, name="mem_attn_bwd")
    dkv = jnp.concatenate([dmk, dmv], axis=1)
    gb["w_mem_kv"] = mm(mn, dkv, m=1024, n=1024, k=MEM_LEN, ta=True, tm=1024, tn=1024, tk=MEM_LEN, out_dtypes=(F32,), name="dw_mem_kv")
    dmn = mm(dkv, w["w_mem_kv"], m=MEM_LEN, n=1024, k=1024, tb=True, tm=MEM_LEN, tn=1024, tk=1024, out_dtypes=(F32,), name="d_mn")
    _, gs["mem_norm_g"] = _rmsnorm_bwd(mem, gm, dmn, None, tm=MEM_LEN, name="mem_norm_bwd")

    pieces = ((du, OFF_U, "u"), (dqkv[0], OFF_QKV, "q"), (dqkv[1], OFF_QKV + 768, "k"), (dqkv[2], OFF_QKV + 1536, "v"),
              (dmq, OFF_MQ, "mq"), (dzg, OFF_ZG, "zg"))
    dw_rows = []
    for piece, off, tag in pieces:
        width = piece.shape[1]
        tmw = 1024 if width % 1024 == 0 else (768 if width == 768 else 512)
        stages = {"q": (("swap", early[2]),), "zg": (("scatter", early[2]),)}.get(tag, ())
        dw_rows.append(arrived(stages, mm(piece, n1, m=width, n=1024, k=rows, ta=True, tm=tmw, tn=1024, tk=2048,
                                          out_dtypes=(F32,), carry=riding(*stages), name="dw_in_" + tag)))
    gb["w_in"] = jnp.concatenate(dw_rows, axis=0)
    if reducer is not None:
        reducer.swap_now(early[3], [gb["w_in"]])
    stages = (("scatter", early[3]),)
    dx, gs["norm1_g"] = arrived(stages, _sum_matmul(
        [piece for piece, _, _ in pieces], win_t, [off for _, off, _ in pieces], tm=512,
        aux=((x, "mn"), (dh1, "mn"), (g1, "row")), epilogue=_rmsnorm_bwd_epilogue, n_sums=1,
        carry=riding(*stages), vmem=VMEM_LIMIT_WIDE_BYTES, name="d_n1"))
    return loss, dx, gb, gs


def kernel(x, mem, norm1_g, mem_norm_g, w_in, b_gate, ssm_lambda_re, ssm_lambda_im, ssm_log_dt, ssm_b_re, ssm_b_im, ssm_c_re, ssm_c_im, ssm_d, w_glu, b_glu, w_ssm_br, w_attn_br, w_mem_kv, w_mem_br, w_o, norm2_g, w_up, w_down, final_g, loss_target, m_norm1_g, m_mem_norm_g, m_w_in, m_b_gate, m_ssm_lambda_re, m_ssm_lambda_im, m_ssm_log_dt, m_ssm_b_re, m_ssm_b_im, m_ssm_c_re, m_ssm_c_im, m_ssm_d, m_w_glu, m_b_glu, m_w_ssm_br, m_w_attn_br, m_w_mem_kv, m_w_mem_br, m_w_o, m_norm2_g, m_w_up, m_w_down, m_final_g, v_norm1_g, v_mem_norm_g, v_w_in, v_b_gate, v_ssm_lambda_re, v_ssm_lambda_im, v_ssm_log_dt, v_ssm_b_re, v_ssm_b_im, v_ssm_c_re, v_ssm_c_im, v_ssm_d, v_w_glu, v_b_glu, v_w_ssm_br, v_w_attn_br, v_w_mem_kv, v_w_mem_br, v_w_o, v_norm2_g, v_w_up, v_w_down, v_final_g):
    env = dict(locals())
    weights = {n: env[n] for n in WEIGHT_ORDER}
    moms = {n: env["m_" + n] for n in WEIGHT_ORDER}
    vels = {n: env["v_" + n] for n in WEIGHT_ORDER}

    chip = 2 * lax.axis_index("x") + lax.axis_index("y")
    wire = [weights[n].reshape(weights[n].shape[-2:]).astype(BF16) for n, _, _ in BIG]
    wire = dict(zip([n for n, _, _ in BIG], [s.T if tr else s for s, (_, tr, _) in zip(wire, BIG)]))
    small = {n: weights[n] for n, _ in SMALL}

    reducer = _GradReducer(lax.axis_index("c").astype(jnp.int32).reshape(1), chip.astype(jnp.int32).reshape(1))
    loss, dx, gb, gs = _device_step(x[0], mem[0], loss_target[0], {}, small, shards=wire, reducer=reducer)
    *shards, small_grad = reducer.finish(["small"], [_pack_small(gs)], [n for n, _, _ in BIG] + ["small"])
    grads = {}
    for (n, tr, _), sh in zip(BIG, shards):
        sh = sh.reshape(2 * sh.shape[1], sh.shape[2])
        grads[n] = sh.T if tr else sh
    small_grad = small_grad.reshape(N_CHIPS * SMALL_ROWS, 1024)
    grads_small = _unpack_small(small_grad)

    delta, new_m, new_v = {}, {}, {}
    for n, _, _ in BIG:
        grads[n], delta[n], new_m[n], new_v[n] = _adamw(weights[n], grads[n], moms[n], vels[n],
                                                        tr=min(weights[n].shape[-2], 256), name="adamw_" + n)
    _, ds_, ms_, vs_ = _adamw(_pack_small(small), small_grad,
                              _pack_small({n: moms[n] for n, _ in SMALL}), _pack_small({n: vels[n] for n, _ in SMALL}),
                              tr=N_CHIPS * SMALL_ROWS, name="adamw_small")
    for dst, buf in ((delta, ds_), (new_m, ms_), (new_v, vs_)):
        dst.update(_unpack_small(buf))
    grads.update(grads_small)

    total_loss = lax.psum(loss, ("x", "y", "c"))
    return (total_loss, dx[None], *[grads[n] for n in WEIGHT_ORDER], *[delta[n] for n in WEIGHT_ORDER],
            *[new_m[n] for n in WEIGHT_ORDER], *[new_v[n] for n in WEIGHT_ORDER])
```

```python
import functools
import math

import numpy as np
import jax
import jax.numpy as jnp
from jax import lax
from jax.experimental import pallas as pl
from jax.experimental.pallas import tpu as pltpu

F32 = jnp.float32
BF16 = jnp.bfloat16

D_MODEL = 1024
SSM_GROUPS = 32
SSM_GROUP_SIZE = 16
SSM_STATE = 64
SSM_WIDTH = 512
N_STATES = SSM_GROUPS * SSM_STATE
SCAN_CB = 1024
ATTN_PATTERNS = ((128, 1), (512, 4), (2048, 16))
ATTN_HEAD_DIM = 64
ATTN_Q = 128
MEM_LEN = 256
MEM_HEAD_DIM = 128
MEM_HEADS = 4
D_FF = 4096
OFF_U, OFF_QKV, OFF_MQ, OFF_ZG = 0, 512, 2816, 3328
IN_WIDTH = 6400
RMS_EPS = 1e-6
NEG_INF = -1e30
ADAM_LR, ADAM_B1, ADAM_B2, ADAM_EPS, ADAM_WD, ADAM_STEP = 0.001, 0.9, 0.999, 1e-08, 0.01, 10

VMEM_LIMIT_BYTES = 48 * 1024 * 1024
VMEM_LIMIT_WIDE_BYTES = 56 * 1024 * 1024
LANES = 128
MESH = pl.DeviceIdType.MESH
N_CHIPS = 4

SCAN_SEGS = 8
SCAN_GROUPS = SCAN_CB // SSM_STATE

BIG = (("w_in", True, (6400, 1024)), ("w_glu", False, (512, 512)), ("w_ssm_br", True, (1024, 512)),
       ("w_attn_br", True, (1024, 256)), ("w_mem_kv", False, (1024, 1024)), ("w_mem_br", True, (1024, 512)),
       ("w_o", False, (1024, 1024)), ("w_up", True, (4096, 1024)), ("w_down", False, (4096, 1024)))
SMALL = (("norm1_g", (1, 1024)), ("mem_norm_g", (1, 1024)), ("b_gate", (1, 3072)),
         ("ssm_lambda_re", (1, 32, 64)), ("ssm_lambda_im", (1, 32, 64)), ("ssm_log_dt", (1, 32)),
         ("ssm_b_re", (1, 32, 64, 16)), ("ssm_b_im", (1, 32, 64, 16)), ("ssm_c_re", (1, 32, 16, 64)),
         ("ssm_c_im", (1, 32, 16, 64)), ("ssm_d", (1, 32, 16)), ("b_glu", (1, 512)),
         ("norm2_g", (1, 1024)), ("final_g", (1024,)))
WEIGHT_ORDER = ("norm1_g", "mem_norm_g", "w_in", "b_gate", "ssm_lambda_re", "ssm_lambda_im", "ssm_log_dt",
                "ssm_b_re", "ssm_b_im", "ssm_c_re", "ssm_c_im", "ssm_d", "w_glu", "b_glu", "w_ssm_br",
                "w_attn_br", "w_mem_kv", "w_mem_br", "w_o", "norm2_g", "w_up", "w_down", "final_g")
SMALL_ELEMS = sum(int(np.prod(s)) for _, s in SMALL)
SMALL_ROWS = 64


def _params(sem, vmem=VMEM_LIMIT_BYTES):
    return pltpu.CompilerParams(dimension_semantics=sem, vmem_limit_bytes=vmem)


def _sigmoid(v):
    return 0.5 * jnp.tanh(0.5 * v) + 0.5


_GELU_C = math.sqrt(2.0 / math.pi)


def _gelu(v):
    return 0.5 * v * (1.0 + jnp.tanh(_GELU_C * (v + 0.044715 * v * v * v)))


def _gelu_grad(v):
    th = jnp.tanh(_GELU_C * (v + 0.044715 * v * v * v))
    return 0.5 * (1.0 + th) + 0.5 * v * (1.0 - th * th) * _GELU_C * (1.0 + 3.0 * 0.044715 * v * v)


def _dot(a, b, ca, cb):
    return lax.dot_general(a, b, (((ca,), (cb,)), ((), ())), preferred_element_type=F32)


class _Exchange:
    def __init__(self, ins, outs, aliases, sems, start, finish):
        self.ins, self.outs, self.aliases, self.sems, self.start, self.finish = ins, outs, aliases, sems, start, finish


def _matmul(a, b, *, m, n, k, ta=False, tb=False, tm, tn, tk, out_dtypes, name,
            aux=(), epilogue=None, n_sums=0, carry=None):
    assert m % tm == 0 and n % tn == 0 and k % tk == 0, (name, m, n, k, tm, tn, tk)
    assert n_sums == 0 or tn == n, name
    nk = k // tk
    n_aux = len(aux)
    n_tiles = len(out_dtypes)
    n_out = n_tiles + n_sums
    a_spec = pl.BlockSpec((tk, tm), lambda i, j, kk: (kk, i)) if ta else pl.BlockSpec((tm, tk), lambda i, j, kk: (i, kk))
    b_spec = pl.BlockSpec((tn, tk), lambda i, j, kk: (j, kk)) if tb else pl.BlockSpec((tk, tn), lambda i, j, kk: (kk, j))
    aux_specs = []
    for _, kind in aux:
        if kind == "mn":
            aux_specs.append(pl.BlockSpec((tm, tn), lambda i, j, kk: (i, j)))
        else:
            aux_specs.append(pl.BlockSpec((1, tn), lambda i, j, kk: (0, j)))
    ca = 0 if ta else 1
    cb = 1 if tb else 0

    def finish(acc, aux_refs, out_refs, row_tile):
        outs = (acc,) if epilogue is None else epilogue(acc, *[r[...] for r in aux_refs])
        for o_ref, o in zip(out_refs[:n_tiles], outs[:n_tiles]):
            o_ref[...] = o.astype(o_ref.dtype)
        _accumulate_over_rows(out_refs[n_tiles:], outs[n_tiles:], row_tile)

    def body(a_ref, b_ref, *rest):
        aux_refs = rest[:n_aux]
        out_refs = rest[n_aux:n_aux + n_out]
        row_tile = pl.program_id(0)
        prod = _dot(a_ref[...].astype(BF16), b_ref[...].astype(BF16), ca, cb)
        if nk == 1:
            finish(prod, aux_refs, out_refs, row_tile)
            return
        acc_ref = rest[n_aux + n_out]
        kk = pl.program_id(2)

        @pl.when(kk == 0)
        def _():
            acc_ref[...] = prod

        @pl.when(jnp.logical_and(kk > 0, kk < nk - 1))
        def _():
            acc_ref[...] += prod

        @pl.when(kk == nk - 1)
        def _():
            finish(acc_ref[...] + prod, aux_refs, out_refs, row_tile)

    tile = pl.BlockSpec((tm, tn), lambda i, j, kk: (i, j))
    col_sum = pl.BlockSpec((1, tn), lambda i, j, kk: (0, j))
    res = _call_with_carry(
        body, carry, name=name, grid=(m // tm, n // tn, nk), in_specs=[a_spec, b_spec] + aux_specs,
        out_specs=[tile] * n_tiles + [col_sum] * n_sums,
        out_shape=[jax.ShapeDtypeStruct((m, n), dt) for dt in out_dtypes] + [jax.ShapeDtypeStruct((1, n), F32)] * n_sums,
        scratch=[pltpu.VMEM((tm, tn), F32)] if nk > 1 else [], operands=[a, b] + [x for x, _ in aux],
        semantics=("arbitrary" if n_sums else "parallel", "parallel", "arbitrary"))
    main = res[0] if n_out == 1 else tuple(res[:n_out])
    return main if carry is None else (main, list(res[n_out:]))


def _accumulate_over_rows(sum_refs, terms, row_tile):
    for s_ref, term in zip(sum_refs, terms):
        @pl.when(row_tile == 0)
        def _():
            s_ref[...] = term

        @pl.when(row_tile > 0)
        def _():
            s_ref[...] += term


def _call_with_carry(body, carry, *, name, grid, in_specs, out_specs, out_shape, scratch, operands, semantics,
                     vmem=VMEM_LIMIT_BYTES):
    if carry is None:
        return pl.pallas_call(body, name=name, grid=grid, in_specs=in_specs, out_specs=out_specs, out_shape=out_shape,
                              scratch_shapes=scratch, compiler_params=_params(semantics, vmem))(*operands)
    n_in, n_cin, n_out, n_cout, n_scr = len(operands), len(carry.ins), len(out_shape), len(carry.outs), len(scratch)

    def hosted(*refs):
        main_in, c_in = refs[:n_in], refs[n_in:n_in + n_cin]
        main_out = refs[n_in + n_cin:n_in + n_cin + n_out]
        c_out = refs[n_in + n_cin + n_out:n_in + n_cin + n_out + n_cout]
        rest = refs[n_in + n_cin + n_out + n_cout:]
        ids = [pl.program_id(t) for t in range(len(grid))]
        first = functools.reduce(jnp.logical_and, [i == 0 for i in ids])
        last = functools.reduce(jnp.logical_and, [i == g - 1 for i, g in zip(ids, grid)])

        @pl.when(first)
        def _():
            carry.start(c_in, c_out, *rest[n_scr:])

        body(*main_in, *main_out, *rest[:n_scr])

        @pl.when(last)
        def _():
            carry.finish(c_in, c_out, *rest[n_scr:])

    return pl.pallas_call(
        hosted, name=name, grid=grid,
        in_specs=list(in_specs) + [ANY] * n_cin, out_specs=list(out_specs) + [ANY] * n_cout,
        out_shape=list(out_shape) + list(carry.outs),
        input_output_aliases={n_in + i: n_out + o for i, o in carry.aliases.items()},
        scratch_shapes=list(scratch) + [pltpu.SemaphoreType.DMA(s) for s in carry.sems],
        compiler_params=_params(("arbitrary",) * len(grid), vmem),
    )(*operands, *carry.ins)


def _sum_matmul(pieces, b, offs, *, tm, name, tb=False, out_dtype=F32, aux=(), epilogue=None, n_sums=0, carry=None,
                vmem=VMEM_LIMIT_BYTES):
    m = pieces[0].shape[0]
    n = b.shape[0] if tb else b.shape[1]
    npieces, n_aux = len(pieces), len(aux)
    assert not tb or npieces == 1

    def body(*refs):
        b_ref = refs[npieces]
        aux_refs = refs[npieces + 1:npieces + 1 + n_aux]
        out_refs = refs[npieces + 1 + n_aux:]
        acc = None
        for p_ref, off in zip(refs[:npieces], offs):
            lhs = p_ref[...].astype(BF16)
            part = _dot(lhs, b_ref[...], 1, 1) if tb else _dot(lhs, b_ref[pl.ds(off, p_ref.shape[1]), :], 1, 0)
            acc = part if acc is None else acc + part
        outs = (acc,) if epilogue is None else epilogue(acc, *[r[...] for r in aux_refs])
        out_refs[0][...] = outs[0].astype(out_dtype)
        _accumulate_over_rows(out_refs[1:], outs[1:], pl.program_id(0))

    row = pl.BlockSpec((tm, n), lambda i: (i, 0))
    vec = pl.BlockSpec((1, n), lambda i: (0, 0))
    res = _call_with_carry(
        body, carry, name=name, grid=(m // tm,),
        in_specs=[pl.BlockSpec((tm, p.shape[1]), lambda i: (i, 0)) for p in pieces] + [_resident(b.shape)]
        + [row if kind == "mn" else vec for _, kind in aux],
        out_specs=[row] + [vec] * n_sums,
        out_shape=[jax.ShapeDtypeStruct((m, n), out_dtype)] + [jax.ShapeDtypeStruct((1, n), F32)] * n_sums,
        scratch=[], operands=list(pieces) + [b] + [x for x, _ in aux], semantics=("arbitrary" if n_sums else "parallel",),
        vmem=vmem)
    main = res[0] if n_sums == 0 else tuple(res[:1 + n_sums])
    return main if carry is None else (main, list(res[1 + n_sums:]))


def _split_matmul(a, b_t, splits, *, tm, name, carry=None, vmem=VMEM_LIMIT_BYTES):
    m, k = a.shape

    def body(a_ref, b_ref, *out_refs):
        av = a_ref[...].astype(BF16)
        for (row0, width), o_ref in zip(splits, out_refs):
            o_ref[...] = _dot(av, b_ref[pl.ds(row0, width), :], 1, 1)

    res = _call_with_carry(
        body, carry, name=name, grid=(m // tm,),
        in_specs=[pl.BlockSpec((tm, k), lambda i: (i, 0)), _resident(b_t.shape)],
        out_specs=[pl.BlockSpec((tm, width), lambda i: (i, 0)) for _, width in splits],
        out_shape=[jax.ShapeDtypeStruct((m, width), F32) for _, width in splits],
        scratch=[], operands=[a, b_t], semantics=("parallel",), vmem=vmem)
    outs = tuple(res[:len(splits)])
    return outs if carry is None else (outs, list(res[len(splits):]))


def _rmsnorm_fwd(x, g, *, tm, name, carry=None):
    rows, d = x.shape

    def body(x_ref, g_ref, o_ref):
        xv = x_ref[...]
        r = lax.rsqrt(jnp.mean(xv * xv, axis=-1, keepdims=True) + RMS_EPS)
        o_ref[...] = (xv * r * g_ref[...]).astype(o_ref.dtype)

    res = _call_with_carry(
        body, carry, name=name, grid=(rows // tm,),
        in_specs=[pl.BlockSpec((tm, d), lambda i: (i, 0)), pl.BlockSpec((1, d), lambda i: (0, 0))],
        out_specs=[pl.BlockSpec((tm, d), lambda i: (i, 0))], out_shape=[jax.ShapeDtypeStruct((rows, d), BF16)],
        scratch=[], operands=[x, g], semantics=("parallel",))
    return res[0] if carry is None else (res[0], list(res[1:]))


def _residual_norm_epilogue(acc, xv, gv):
    h = acc + xv
    r = lax.rsqrt(jnp.mean(h * h, axis=-1, keepdims=True) + RMS_EPS)
    return h, h * r * gv


def _rmsnorm_bwd_epilogue(dy, xv, resv, gv):
    r = lax.rsqrt(jnp.mean(xv * xv, axis=-1, keepdims=True) + RMS_EPS)
    xhat = xv * r
    dyg = dy * gv
    dx = r * (dyg - xhat * jnp.mean(dyg * xhat, axis=-1, keepdims=True)) + resv
    return dx, jnp.sum(dy * xhat, axis=0, keepdims=True)


def _rmsnorm_bwd(x, g, dy, res, *, tm, name):
    rows, d = x.shape
    has_res = res is not None

    def body(x_ref, g_ref, dy_ref, *rest):
        if has_res:
            res_ref, dx_ref, dg_ref = rest
        else:
            dx_ref, dg_ref = rest
        i = pl.program_id(0)
        xv = x_ref[...]
        r = lax.rsqrt(jnp.mean(xv * xv, axis=-1, keepdims=True) + RMS_EPS)
        xhat = xv * r
        dyv = dy_ref[...]
        dyg = dyv * g_ref[...]
        dx = r * (dyg - xhat * jnp.mean(dyg * xhat, axis=-1, keepdims=True))
        if has_res:
            dx = dx + res_ref[...]
        dx_ref[...] = dx

        @pl.when(i == 0)
        def _():
            dg_ref[...] = jnp.zeros_like(dg_ref)

        dg_ref[...] += jnp.sum(dyv * xhat, axis=0, keepdims=True)

    row_spec = pl.BlockSpec((tm, d), lambda i: (i, 0))
    vec_spec = pl.BlockSpec((1, d), lambda i: (0, 0))
    ins = [x, g, dy] + ([res] if has_res else [])
    return pl.pallas_call(
        body, name=name, grid=(rows // tm,),
        in_specs=[row_spec, vec_spec, row_spec] + ([row_spec] if has_res else []),
        out_specs=[row_spec, vec_spec],
        out_shape=[jax.ShapeDtypeStruct((rows, d), F32), jax.ShapeDtypeStruct((1, d), F32)],
        compiler_params=_params(("arbitrary",)),
    )(*ins)


def _loss_head_epilogue(acc, hv, tgtv, gv):
    xv = acc + hv
    r = lax.rsqrt(jnp.mean(xv * xv, axis=-1, keepdims=True) + RMS_EPS)
    xhat = xv * r
    err = xhat * gv - tgtv
    dyv = err * (1.0 / D_MODEL)
    dyg = dyv * gv
    dh = r * (dyg - xhat * jnp.mean(dyg * xhat, axis=-1, keepdims=True))
    return dh, jnp.sum(dyv * xhat, axis=0, keepdims=True), jnp.sum(err * err, axis=0, keepdims=True)


def _to_scan_layout(v):
    lead = v.shape[:-2]
    v = v.reshape(lead + (2, N_STATES // SCAN_CB, SCAN_CB))
    v = jnp.swapaxes(v, -3, -2)
    return v.reshape(lead + (2 * N_STATES,))


def _ssm_matrices(lam_re, lam_im, log_dt, b_re, b_im, c_re, c_im):
    dt = jnp.exp(log_dt)[:, None]
    mag = jnp.exp(lam_re * dt)
    a_re, a_im = mag * jnp.cos(lam_im * dt), mag * jnp.sin(lam_im * dt)
    nr, ni = a_re - 1.0, a_im
    den = lam_re * lam_re + lam_im * lam_im
    coef_re = (nr * lam_re + ni * lam_im) / den
    coef_im = (ni * lam_re - nr * lam_im) / den
    bb_re = coef_re[..., None] * b_re - coef_im[..., None] * b_im
    bb_im = coef_re[..., None] * b_im + coef_im[..., None] * b_re
    a_lay = _to_scan_layout(jnp.stack([a_re.reshape(-1), a_im.reshape(-1)], axis=0))[None, :]
    nblk = SSM_GROUPS // SCAN_GROUPS
    eye = jnp.eye(SCAN_GROUPS, dtype=F32)

    def b_block(bb):
        bb = bb.reshape(nblk, SCAN_GROUPS, SSM_STATE, SSM_GROUP_SIZE)
        return jnp.einsum("gk,jkph->jghkp", eye, bb).reshape(nblk, SCAN_GROUPS * SSM_GROUP_SIZE, SCAN_CB)

    b_blk = jnp.concatenate([b_block(bb_re), b_block(bb_im)], axis=2)

    def c_block(cc):
        cc = cc.reshape(nblk, SCAN_GROUPS, SSM_GROUP_SIZE, SSM_STATE)
        return jnp.einsum("gk,jghp->jkpgh", eye, cc).reshape(nblk, SCAN_CB, SCAN_GROUPS * SSM_GROUP_SIZE)

    c_blk = jnp.concatenate([c_block(c_re), -c_block(c_im)], axis=1)
    return a_lay, b_blk, c_blk


def _interleave(v):
    rows, c = v.shape
    return v.reshape(SCAN_SEGS, rows // SCAN_SEGS, c).transpose(1, 0, 2).reshape(rows, c)


def _deinterleave(v):
    rows, c = v.shape
    return v.reshape(rows // SCAN_SEGS, SCAN_SEGS, c).transpose(1, 0, 2).reshape(rows, c)


def _scan_groups(a_ref, bu_ref, o_ref, state, *, reverse, tt, unroll=4):
    cb = SCAN_CB
    ar = jnp.broadcast_to(a_ref[:, :cb], (SCAN_SEGS, cb))
    ai = jnp.broadcast_to(a_ref[:, cb:], (SCAN_SEGS, cb))
    ngroups = tt // SCAN_SEGS

    def step(i, st):
        sr, si = st
        r0 = pl.multiple_of(((ngroups - 1 - i) if reverse else i) * SCAN_SEGS, SCAN_SEGS)
        blk = bu_ref[pl.ds(r0, SCAN_SEGS), :]
        nr = ar * sr - ai * si + blk[:, :cb]
        ni = ar * si + ai * sr + blk[:, cb:]
        if o_ref is not None:
            o_ref[pl.ds(r0, SCAN_SEGS), :] = jnp.concatenate([nr, ni], axis=1)
        return nr, ni

    return lax.fori_loop(0, ngroups, step, state, unroll=unroll)


def _segment_entries(a_ref, e_ref, init_ref, *, reverse, seg_len):
    cb = SCAN_CB
    n_sq = seg_len.bit_length() - 1
    assert 1 << n_sq == seg_len, seg_len
    pr, pi = a_ref[:, :cb], a_ref[:, cb:]
    for _ in range(n_sq):
        pr, pi = pr * pr - pi * pi, 2.0 * pr * pi
    cr = jnp.zeros((1, cb), F32)
    ci = jnp.zeros((1, cb), F32)
    order = range(SCAN_SEGS - 1, -1, -1) if reverse else range(SCAN_SEGS)
    for k, seg in enumerate(order):
        if k > 0:
            prev = seg + 1 if reverse else seg - 1
            er, ei = e_ref[prev:prev + 1, :cb], e_ref[prev:prev + 1, cb:]
            cr, ci = pr * cr - pi * ci + er, pr * ci + pi * cr + ei
        init_ref[seg:seg + 1, :] = jnp.concatenate([cr, ci], axis=1)


def _ssm_specs(nt, tt, nch, reverse):
    cb = SCAN_CB
    tmap = (lambda j, kk: (nt - 1 - kk, j)) if reverse else (lambda j, kk: (kk, j))
    nmap = (lambda j, kk: (jnp.maximum(nt - 2 - kk, 0), j)) if reverse else (lambda j, kk: (jnp.minimum(kk + 1, nt - 1), j))
    return dict(a=pl.BlockSpec((1, 2 * cb), lambda j, kk: (0, j)),
                seg=pl.BlockSpec((SCAN_SEGS, 2 * cb), lambda j, kk: (0, j)),
                chan=pl.BlockSpec((tt, nch), tmap),
                next=pl.BlockSpec((tt, nch), nmap),
                state=pl.BlockSpec((tt, 2 * cb), tmap),
                b=pl.BlockSpec((None, nch, 2 * cb), lambda j, kk: (j, 0, 0)),
                c=pl.BlockSpec((None, 2 * cb, nch), lambda j, kk: (j, 0, 0)))


def _ssm_ends(a_lay, x, blocks, *, transpose, reverse, tt, name):
    rows = x.shape[0]
    nblk = blocks.shape[0]
    nch = x.shape[1] // nblk
    cb = SCAN_CB
    nt = rows // tt
    sp = _ssm_specs(nt, tt, nch, reverse)

    def body(a_ref, x_ref, xn_ref, w_ref, e_ref, even_ref, odd_ref):
        kk = pl.program_id(1)

        def product(src_ref, dst_ref):
            dst_ref[...] = _dot(src_ref[...].astype(BF16), w_ref[...].astype(BF16), 1, 1 if transpose else 0)

        @pl.when(kk == 0)
        def _():
            e_ref[...] = jnp.zeros_like(e_ref)
            product(x_ref, even_ref)

        def phase(cur_ref, next_ref):
            product(xn_ref, next_ref)
            sr, si = _scan_groups(a_ref, cur_ref, None, (e_ref[:, :cb], e_ref[:, cb:]), reverse=reverse, tt=tt, unroll=True)
            e_ref[...] = jnp.concatenate([sr, si], axis=1)

        @pl.when(kk % 2 == 0)
        def _():
            phase(even_ref, odd_ref)

        @pl.when(kk % 2 == 1)
        def _():
            phase(odd_ref, even_ref)

    return pl.pallas_call(
        body, name=name, grid=(nblk, nt),
        in_specs=[sp["a"], sp["chan"], sp["next"], sp["c"] if transpose else sp["b"]],
        out_specs=sp["seg"],
        out_shape=jax.ShapeDtypeStruct((SCAN_SEGS, nblk * 2 * cb), F32),
        scratch_shapes=[pltpu.VMEM((tt, 2 * cb), F32), pltpu.VMEM((tt, 2 * cb), F32)],
        compiler_params=_params(("parallel", "arbitrary")),
    )(a_lay, x, x, blocks)


def _ssm_fwd(a_lay, u, b_blk, c_blk, ends, *, tt, name):
    rows = u.shape[0]
    nblk = b_blk.shape[0]
    nch = u.shape[1] // nblk
    cb = SCAN_CB
    nt = rows // tt
    sp = _ssm_specs(nt, tt, nch, False)

    def body(a_ref, e_ref, u_ref, un_ref, b_ref, c_ref, s_ref, y_ref, init_ref, carry_ref, even_ref, odd_ref):
        kk = pl.program_id(1)

        def product(src_ref, dst_ref):
            dst_ref[...] = _dot(src_ref[...].astype(BF16), b_ref[...].astype(BF16), 1, 0)

        @pl.when(kk == 0)
        def _():
            _segment_entries(a_ref, e_ref, init_ref, reverse=False, seg_len=rows // SCAN_SEGS)
            carry_ref[...] = init_ref[...]
            product(u_ref, even_ref)

        def phase(cur_ref, next_ref):
            product(un_ref, next_ref)
            sr, si = _scan_groups(a_ref, cur_ref, s_ref, (carry_ref[:, :cb], carry_ref[:, cb:]), reverse=False, tt=tt,
                                  unroll=True)
            carry_ref[...] = jnp.concatenate([sr, si], axis=1)

        @pl.when(kk % 2 == 0)
        def _():
            phase(even_ref, odd_ref)

        @pl.when(kk % 2 == 1)
        def _():
            phase(odd_ref, even_ref)

        y_ref[...] = _dot(s_ref[...].astype(BF16), c_ref[...].astype(BF16), 1, 0)

    return pl.pallas_call(
        body, name=name, grid=(nblk, nt),
        in_specs=[sp["a"], sp["seg"], sp["chan"], sp["next"], sp["b"], sp["c"]],
        out_specs=[sp["state"], sp["chan"], sp["seg"]],
        out_shape=[jax.ShapeDtypeStruct((rows, nblk * 2 * cb), F32), jax.ShapeDtypeStruct((rows, nblk * nch), F32),
                   jax.ShapeDtypeStruct((SCAN_SEGS, nblk * 2 * cb), F32)],
        scratch_shapes=[pltpu.VMEM((SCAN_SEGS, 2 * cb), F32), pltpu.VMEM((tt, 2 * cb), F32), pltpu.VMEM((tt, 2 * cb), F32)],
        compiler_params=_params(("parallel", "arbitrary")),
    )(a_lay, ends, u, u, b_blk, c_blk)


def _ssm_bwd(a_conj, dy, u, s, s_entry, b_blk, c_blk, dd, ends, *, tt, name):
    rows = u.shape[0]
    nblk = b_blk.shape[0]
    nch = u.shape[1] // nblk
    cb = SCAN_CB
    nt = rows // tt
    sp = _ssm_specs(nt, tt, nch, True)
    groups_per_tile = tt // SCAN_SEGS
    before = pl.BlockSpec((SCAN_SEGS, 2 * cb), lambda j, kk: (jnp.maximum((nt - 1 - kk) * groups_per_tile - 1, 0), j))

    def body(a_ref, e_ref, dy_ref, dyn_ref, u_ref, s_ref, before_ref, entry_ref, b_ref, c_ref, dd_ref,
             du_ref, db_ref, dc_ref, da_ref, carry_ref, even_ref, odd_ref):
        kk = pl.program_id(1)

        def product(src_ref, dst_ref):
            dst_ref[...] = _dot(src_ref[...].astype(BF16), c_ref[...].astype(BF16), 1, 1)

        @pl.when(kk == 0)
        def _():
            _segment_entries(a_ref, e_ref, carry_ref, reverse=True, seg_len=rows // SCAN_SEGS)
            db_ref[...] = jnp.zeros_like(db_ref)
            dc_ref[...] = jnp.zeros_like(dc_ref)
            da_ref[...] = jnp.zeros_like(da_ref)
            product(dy_ref, even_ref)

        def pair(lv, pv):
            lre, lim, pre, pim = lv[:, :cb], lv[:, cb:], pv[:, :cb], pv[:, cb:]
            return (jnp.sum(lre * pre + lim * pim, axis=0, keepdims=True),
                    jnp.sum(lim * pre - lre * pim, axis=0, keepdims=True))

        def phase(lam_ref, next_ref):
            product(dyn_ref, next_ref)
            lr, li = _scan_groups(a_ref, lam_ref, lam_ref, (carry_ref[:, :cb], carry_ref[:, cb:]), reverse=True, tt=tt,
                                  unroll=True)
            carry_ref[...] = jnp.concatenate([lr, li], axis=1)
            first = jnp.where(kk == nt - 1, entry_ref[...], before_ref[...])
            rest = tt - SCAN_SEGS
            r1, i1 = pair(lam_ref[pl.ds(SCAN_SEGS, rest), :], s_ref[pl.ds(0, rest), :])
            r0, i0 = pair(lam_ref[pl.ds(0, SCAN_SEGS), :], first)
            da_ref[...] += jnp.concatenate([r1 + r0, i1 + i0], axis=1)
            dyv = dy_ref[...]
            lamb = lam_ref[...].astype(BF16)
            du_ref[...] = _dot(lamb, b_ref[...].astype(BF16), 1, 1) + dd_ref[...] * dyv
            db_ref[...] += _dot(u_ref[...].astype(BF16), lamb, 0, 0)
            dc_ref[...] += _dot(s_ref[...].astype(BF16), dyv.astype(BF16), 0, 0)

        @pl.when(kk % 2 == 0)
        def _():
            phase(even_ref, odd_ref)

        @pl.when(kk % 2 == 1)
        def _():
            phase(odd_ref, even_ref)

    return pl.pallas_call(
        body, name=name, grid=(nblk, nt),
        in_specs=[sp["a"], sp["seg"], sp["chan"], sp["next"], sp["chan"], sp["state"], before, sp["seg"], sp["b"], sp["c"],
                  pl.BlockSpec((1, nch), lambda j, kk: (0, j))],
        out_specs=[sp["chan"], sp["b"], sp["c"], pl.BlockSpec((1, 2 * cb), lambda j, kk: (0, j))],
        out_shape=[jax.ShapeDtypeStruct((rows, nblk * nch), F32), jax.ShapeDtypeStruct(b_blk.shape, F32),
                   jax.ShapeDtypeStruct(c_blk.shape, F32), jax.ShapeDtypeStruct((1, nblk * 2 * cb), F32)],
        scratch_shapes=[pltpu.VMEM((SCAN_SEGS, 2 * cb), F32), pltpu.VMEM((tt, 2 * cb), F32), pltpu.VMEM((tt, 2 * cb), F32)],
        compiler_params=_params(("parallel", "arbitrary")),
    )(a_conj, ends, dy, dy, u, s, s, s_entry, b_blk, c_blk, dd)


def _glu_fwd(ys, u, dd, w_glu, b_glu, *, tm, name):
    rows, w = ys.shape

    def body(ys_ref, u_ref, dd_ref, w_ref, b_ref, y0_ref, t_ref, y2_ref):
        y0 = ys_ref[...] + dd_ref[...] * u_ref[...]
        y1 = _gelu(y0)
        t = _dot(y1.astype(BF16), w_ref[...], 1, 0) + b_ref[...]
        y0_ref[...] = y0
        t_ref[...] = t
        y2_ref[...] = (y1 * _sigmoid(t)).astype(BF16)

    row = pl.BlockSpec((tm, w), lambda i: (i, 0))
    vec = pl.BlockSpec((1, w), lambda i: (0, 0))
    return pl.pallas_call(
        body, name=name, grid=(rows // tm,),
        in_specs=[row, row, vec, pl.BlockSpec((w, w), lambda i: (0, 0)), vec],
        out_specs=[row, row, row],
        out_shape=[jax.ShapeDtypeStruct((rows, w), F32), jax.ShapeDtypeStruct((rows, w), F32),
                   jax.ShapeDtypeStruct((rows, w), BF16)],
        compiler_params=_params(("parallel",)),
    )(ys, u, dd, w_glu, b_glu)


def _glu_bwd(dy2, y0, t, u, w_glu, *, tm, name):
    rows, w = y0.shape

    def body(dy2_ref, y0_ref, t_ref, u_ref, w_ref, dy0_ref, dt_ref, y1_ref, db_ref, dd_ref):
        i = pl.program_id(0)
        y0 = y0_ref[...]
        y1 = _gelu(y0)
        sg = _sigmoid(t_ref[...])
        dy2v = dy2_ref[...]
        dt = dy2v * y1 * sg * (1.0 - sg)
        dy1 = dy2v * sg + _dot(dt.astype(BF16), w_ref[...], 1, 1)
        dy0 = dy1 * _gelu_grad(y0)
        dy0_ref[...] = dy0
        dt_ref[...] = dt.astype(BF16)
        y1_ref[...] = y1.astype(BF16)

        @pl.when(i == 0)
        def _():
            db_ref[...] = jnp.zeros_like(db_ref)
            dd_ref[...] = jnp.zeros_like(dd_ref)

        db_ref[...] += jnp.sum(dt, axis=0, keepdims=True)
        dd_ref[...] += jnp.sum(dy0 * u_ref[...], axis=0, keepdims=True)

    row = pl.BlockSpec((tm, w), lambda i: (i, 0))
    vec = pl.BlockSpec((1, w), lambda i: (0, 0))
    return pl.pallas_call(
        body, name=name, grid=(rows // tm,),
        in_specs=[row, row, row, row, pl.BlockSpec((w, w), lambda i: (0, 0))],
        out_specs=[row, row, row, vec, vec],
        out_shape=[jax.ShapeDtypeStruct((rows, w), F32), jax.ShapeDtypeStruct((rows, w), BF16),
                   jax.ShapeDtypeStruct((rows, w), BF16), jax.ShapeDtypeStruct((1, w), F32),
                   jax.ShapeDtypeStruct((1, w), F32)],
        compiler_params=_params(("arbitrary",)),
    )(dy2, y0, t, u, w_glu)


ATTN_TILE = 2048


def _attn_geometry(rows, d):
    sb = ATTN_Q * d
    tr = max(sb, min(ATTN_TILE, rows))
    assert rows % tr == 0 and tr % sb == 0, (rows, d)
    return sb, tr, rows // tr, tr // sb


def _attn_masks():
    qi = lax.broadcasted_iota(jnp.int32, (2 * ATTN_Q, 2 * ATTN_Q), 0) % ATTN_Q
    kj = lax.broadcasted_iota(jnp.int32, (2 * ATTN_Q, 2 * ATTN_Q), 1)
    own_ok = jnp.logical_and(kj >= ATTN_Q, kj - ATTN_Q <= qi)
    prev_ok = jnp.logical_and(kj < ATTN_Q, kj >= qi)
    bias_first = jnp.where(own_ok, 0.0, NEG_INF)
    bias_other = jnp.where(jnp.logical_or(own_ok, prev_ok), 0.0, NEG_INF)
    head0 = lax.broadcasted_iota(jnp.int32, (ATTN_Q, LANES), 1) < ATTN_HEAD_DIM
    return bias_first, bias_other, head0


def _attn_rows(base, n, d):
    return pl.ds(pl.multiple_of(base, ATTN_Q), n) if d == 1 else pl.ds(base, n, stride=d)


def _stack_heads(v, head0):
    return jnp.concatenate([jnp.where(head0, v, 0.0), jnp.where(head0, 0.0, v)], axis=0)


def _unstack_heads(v, head0):
    return jnp.where(head0, v[:ATTN_Q], v[ATTN_Q:])


def _fill_keys(buf, prev_ref, cur_ref, sb):
    buf[pl.ds(0, sb), :] = prev_ref[...]
    buf[pl.ds(sb, cur_ref.shape[0]), :] = cur_ref[...]


def _attn_fwd(qkv, g, d, *, name):
    rows = qkv.shape[0]
    sb, tr, ntiles, nsub = _attn_geometry(rows, d)
    qc, kc, vc = 2 * g, 6 + 2 * g, 12 + 2 * g
    scale = ATTN_HEAD_DIM ** -0.5

    def body(q_ref, kc_ref, kp_ref, vc_ref, vp_ref, o_ref, lse_ref, kbuf, vbuf):
        n = pl.program_id(0)
        _fill_keys(kbuf, kp_ref, kc_ref, sb)
        _fill_keys(vbuf, vp_ref, vc_ref, sb)
        bias_first, bias_other, head0 = _attn_masks()

        def per_block(idx, carry):
            j, r = idx // d, idx % d
            base = j * sb + r
            bias = jnp.where(jnp.logical_and(n == 0, j == 0), bias_first, bias_other)
            qrows = _attn_rows(base, ATTN_Q, d)
            krows = _attn_rows(base, 2 * ATTN_Q, d)
            qs = (_stack_heads(q_ref[qrows, :], head0) * scale).astype(BF16)
            s = _dot(qs, kbuf[krows, :].astype(BF16), 1, 1) + bias
            mx = jnp.max(s, axis=-1, keepdims=True)
            p = jnp.exp(s - mx)
            den = jnp.sum(p, axis=-1, keepdims=True)
            pv = _dot(p.astype(BF16), vbuf[krows, :].astype(BF16), 1, 0) / den
            o_ref[qrows, :] = _unstack_heads(pv, head0)
            lse_ref[qrows, :] = _unstack_heads(jnp.broadcast_to(mx + jnp.log(den), (2 * ATTN_Q, LANES)), head0)
            return carry

        lax.fori_loop(0, nsub * d, per_block, 0, unroll=8)

    def cur(col):
        return pl.BlockSpec((tr, LANES), lambda n, hp: (n, col + hp))

    def prev(col):
        return pl.BlockSpec((sb, LANES), lambda n, hp: (jnp.maximum(n * nsub - 1, 0), col + hp))

    out_spec = pl.BlockSpec((tr, LANES), lambda n, hp: (n, hp))
    return pl.pallas_call(
        body, name=name, grid=(ntiles, 2),
        in_specs=[cur(qc), cur(kc), prev(kc), cur(vc), prev(vc)],
        out_specs=[out_spec, out_spec],
        out_shape=[jax.ShapeDtypeStruct((rows, 2 * LANES), F32), jax.ShapeDtypeStruct((rows, 2 * LANES), F32)],
        scratch_shapes=[pltpu.VMEM((sb + tr, LANES), F32), pltpu.VMEM((sb + tr, LANES), F32)],
        compiler_params=_params(("parallel", "parallel")),
    )(qkv, qkv, qkv, qkv, qkv)


def _attn_merge(outs, lses, *, tm, name):
    rows, w = outs[0].shape

    def body(o0, o1, o2, l0, l1, l2, o_ref, lse_ref):
        a0, a1, a2 = l0[...], l1[...], l2[...]
        mx = jnp.maximum(jnp.maximum(a0, a1), a2)
        e0, e1, e2 = jnp.exp(a0 - mx), jnp.exp(a1 - mx), jnp.exp(a2 - mx)
        den = e0 + e1 + e2
        o_ref[...] = (e0 / den) * o0[...] + (e1 / den) * o1[...] + (e2 / den) * o2[...]
        lse_ref[...] = mx + jnp.log(den)

    row = pl.BlockSpec((tm, w), lambda i: (i, 0))
    return pl.pallas_call(
        body, name=name, grid=(rows // tm,), in_specs=[row] * 6, out_specs=[row, row],
        out_shape=[jax.ShapeDtypeStruct((rows, w), F32), jax.ShapeDtypeStruct((rows, w), F32)],
        compiler_params=_params(("parallel",)),
    )(*outs, *lses)


def _attn_bwd(qkv, do, o, lse, g, d, prev, *, name):
    rows = qkv.shape[0]
    sb, tr, ntiles, nsub = _attn_geometry(rows, d)
    qc, kc, vc = 2 * g, 6 + 2 * g, 12 + 2 * g
    scale = ATTN_HEAD_DIM ** -0.5

    def body(q_ref, kc_ref, kp_ref, vc_ref, vp_ref, do_ref, o_ref, lse_ref, dq_ref, dk_ref, dv_ref,
             kbuf, vbuf, dk_acc, dv_acc):
        n = pl.program_id(1)

        @pl.when(n == 0)
        def _():
            dk_acc[pl.ds(0, tr), :] = jnp.zeros((tr, LANES), F32)
            dv_acc[pl.ds(0, tr), :] = jnp.zeros((tr, LANES), F32)

        @pl.when(n < ntiles)
        def _():
            dk_acc[pl.ds(tr, tr), :] = jnp.zeros((tr, LANES), F32)
            dv_acc[pl.ds(tr, tr), :] = jnp.zeros((tr, LANES), F32)
            _fill_keys(kbuf, kp_ref, kc_ref, sb)
            _fill_keys(vbuf, vp_ref, vc_ref, sb)
            bias_first, bias_other, head0 = _attn_masks()
            lane = lax.broadcasted_iota(jnp.int32, (ATTN_Q, LANES), 1)

            def per_block(idx, carry):
                j, r = idx // d, idx % d
                base = j * sb + r
                bias = jnp.where(jnp.logical_and(n == 0, j == 0), bias_first, bias_other)
                qrows = _attn_rows(base, ATTN_Q, d)
                krows = _attn_rows(base, 2 * ATTN_Q, d)
                arows = _attn_rows(base + (tr - sb), 2 * ATTN_Q, d)
                qs = (_stack_heads(q_ref[qrows, :], head0) * scale).astype(BF16)
                dos = _stack_heads(do_ref[qrows, :], head0)
                dosb = dos.astype(BF16)
                ov = o_ref[qrows, :]
                delta = jnp.sum(dos * jnp.concatenate([ov, ov], axis=0), axis=-1, keepdims=True)
                lsev = lse_ref[qrows, :]
                lse_s = jnp.concatenate(
                    [jnp.sum(jnp.where(lane == h * ATTN_HEAD_DIM, lsev, 0.0), axis=-1, keepdims=True) for h in range(2)], axis=0)
                kb = kbuf[krows, :].astype(BF16)
                vb = vbuf[krows, :].astype(BF16)
                p = jnp.exp(_dot(qs, kb, 1, 1) + bias - lse_s)
                ds = (p * (_dot(dosb, vb, 1, 1) - delta)).astype(BF16)
                dq_ref[qrows, :] = _unstack_heads(_dot(ds, kb, 1, 0), head0) * scale
                dk_acc[arows, :] += _dot(ds, qs, 0, 0)
                dv_acc[arows, :] += _dot(p.astype(BF16), dosb, 0, 0)
                return carry

            lax.fori_loop(0, nsub * d, per_block, 0, unroll=8)

        dk_ref[...] = dk_acc[pl.ds(0, tr), :]
        dv_ref[...] = dv_acc[pl.ds(0, tr), :]
        dk_acc[pl.ds(0, tr), :] = dk_acc[pl.ds(tr, tr), :]
        dv_acc[pl.ds(0, tr), :] = dv_acc[pl.ds(tr, tr), :]

    def cur(n):
        return jnp.minimum(n, ntiles - 1)

    def spec(col, prev):
        if prev:
            return pl.BlockSpec((sb, LANES), lambda hp, n: (jnp.maximum(cur(n) * nsub - 1, 0), col + hp))
        return pl.BlockSpec((tr, LANES), lambda hp, n: (cur(n), col + hp))

    row_spec = pl.BlockSpec((tr, LANES), lambda hp, n: (cur(n), hp))
    dq_out = pl.BlockSpec((tr, LANES), lambda hp, n: (cur(n), 2 * g + hp))
    kv_out = pl.BlockSpec((tr, LANES), lambda hp, n: (jnp.maximum(n - 1, 0), 2 * g + hp))
    shape = jax.ShapeDtypeStruct((rows, len(ATTN_PATTERNS) * 2 * LANES), F32)
    ins = [qkv, qkv, qkv, qkv, qkv, do, o, lse]
    in_specs = [spec(qc, False), spec(kc, False), spec(kc, True), spec(vc, False), spec(vc, True),
                row_spec, row_spec, row_spec]
    aliases = {}
    if prev is not None:
        aliases = {len(ins) + t: t for t in range(3)}
        ins = ins + list(prev)
        in_specs = in_specs + [ANY] * 3
    n_in = len(ins)

    def entry(*refs):
        body(*refs[:8], *refs[n_in:])

    return pl.pallas_call(
        entry, name=name, grid=(2, ntiles + 1),
        in_specs=in_specs,
        out_specs=[dq_out, kv_out, kv_out],
        out_shape=[shape, shape, shape],
        input_output_aliases=aliases,
        scratch_shapes=[pltpu.VMEM((sb + tr, LANES), F32), pltpu.VMEM((sb + tr, LANES), F32),
                        pltpu.VMEM((2 * tr, LANES), F32), pltpu.VMEM((2 * tr, LANES), F32)],
        compiler_params=_params(("parallel", "arbitrary")),
    )(*ins)


def _mem_probs(q, k):
    s = _dot(q.astype(BF16), k.astype(BF16), 1, 1) * (MEM_HEAD_DIM ** -0.5)
    e = jnp.exp(s - jnp.max(s, axis=-1, keepdims=True))
    return e / jnp.sum(e, axis=-1, keepdims=True)


def _mem_attn_fwd(mq, kv, *, tq, name):
    rows = mq.shape[0]

    def body(q_ref, k_ref, v_ref, o_ref):
        p = _mem_probs(q_ref[...], k_ref[...])
        o_ref[...] = _dot(p.astype(BF16), v_ref[...].astype(BF16), 1, 0)

    return pl.pallas_call(
        body, name=name, grid=(rows // tq, MEM_HEADS),
        in_specs=[pl.BlockSpec((tq, LANES), lambda i, h: (i, h)),
                  pl.BlockSpec((MEM_LEN, LANES), lambda i, h: (0, h)),
                  pl.BlockSpec((MEM_LEN, LANES), lambda i, h: (0, MEM_HEADS + h))],
        out_specs=pl.BlockSpec((tq, LANES), lambda i, h: (i, h)),
        out_shape=jax.ShapeDtypeStruct((rows, MEM_HEADS * LANES), F32),
        compiler_params=_params(("parallel", "parallel")),
    )(mq, kv, kv)


def _mem_attn_bwd(mq, kv, dmo, *, tq, name):
    rows = mq.shape[0]
    scale = MEM_HEAD_DIM ** -0.5

    def body(q_ref, k_ref, v_ref, do_ref, dq_ref, dk_ref, dv_ref):
        i = pl.program_id(1)
        qb = q_ref[...].astype(BF16)
        kb = k_ref[...].astype(BF16)
        vb = v_ref[...].astype(BF16)
        dob = do_ref[...].astype(BF16)
        p = _mem_probs(q_ref[...], k_ref[...])
        dp = _dot(dob, vb, 1, 1)
        ds = (p * (dp - jnp.sum(p * dp, axis=-1, keepdims=True)) * scale).astype(BF16)
        dq_ref[...] = _dot(ds, kb, 1, 0).astype(dq_ref.dtype)

        @pl.when(i == 0)
        def _():
            dk_ref[...] = jnp.zeros_like(dk_ref)
            dv_ref[...] = jnp.zeros_like(dv_ref)

        dk_ref[...] += _dot(ds, qb, 0, 0)
        dv_ref[...] += _dot(p.astype(BF16), dob, 0, 0)

    kv_out = pl.BlockSpec((MEM_LEN, LANES), lambda h, i: (0, h))
    kv_shape = jax.ShapeDtypeStruct((MEM_LEN, MEM_HEADS * LANES), F32)
    return pl.pallas_call(
        body, name=name, grid=(MEM_HEADS, rows // tq),
        in_specs=[pl.BlockSpec((tq, LANES), lambda h, i: (i, h)),
                  pl.BlockSpec((MEM_LEN, LANES), lambda h, i: (0, h)),
                  pl.BlockSpec((MEM_LEN, LANES), lambda h, i: (0, MEM_HEADS + h)),
                  pl.BlockSpec((tq, LANES), lambda h, i: (i, h))],
        out_specs=[pl.BlockSpec((tq, LANES), lambda h, i: (i, h)), kv_out, kv_out],
        out_shape=[jax.ShapeDtypeStruct((rows, MEM_HEADS * LANES), BF16), kv_shape, kv_shape],
        compiler_params=_params(("parallel", "arbitrary")),
    )(mq, kv, kv, dmo)


def _resident(shape):
    return pl.BlockSpec(shape, lambda i: (0, 0), pipeline_mode=pl.Buffered(1))


def _branch_merge_fwd(acts, wts, zg, b_gate, *, tm, name):
    rows = zg.shape[0]
    d = wts[0].shape[0]

    def body(s_ref, a_ref, m_ref, ws_ref, wa_ref, wm_ref, zg_ref, b_ref, o_ref):
        gt = _sigmoid(zg_ref[...] + b_ref[...])
        acc = None
        for k, (x_ref, w_ref) in enumerate(((s_ref, ws_ref), (a_ref, wa_ref), (m_ref, wm_ref))):
            term = gt[:, k * d:(k + 1) * d] * _dot(x_ref[...].astype(BF16), w_ref[...], 1, 1)
            acc = term if acc is None else acc + term
        o_ref[...] = acc.astype(BF16)

    return pl.pallas_call(
        body, name=name, grid=(rows // tm,),
        in_specs=[pl.BlockSpec((tm, x.shape[1]), lambda i: (i, 0)) for x in acts] + [_resident(w.shape) for w in wts]
        + [pl.BlockSpec((tm, 3 * d), lambda i: (i, 0)), pl.BlockSpec((1, 3 * d), lambda i: (0, 0))],
        out_specs=pl.BlockSpec((tm, d), lambda i: (i, 0)), out_shape=jax.ShapeDtypeStruct((rows, d), BF16),
        compiler_params=_params(("parallel",)),
    )(*acts, *wts, zg, b_gate)


def _branch_merge_bwd(dmerged, acts, wts, zg, b_gate, *, tm, name, carry=None):
    rows = zg.shape[0]
    d = wts[0].shape[0]

    def body(dm_ref, s_ref, a_ref, m_ref, ws_ref, wa_ref, wm_ref, zg_ref, b_ref,
             ds_ref, da_ref, dmm_ref, dws_ref, dwa_ref, dwm_ref, dzg_ref, db_ref):
        i = pl.program_id(0)

        @pl.when(i == 0)
        def _():
            for r in (dws_ref, dwa_ref, dwm_ref, db_ref):
                r[...] = jnp.zeros_like(r)

        gt = _sigmoid(zg_ref[...] + b_ref[...])
        dm = dm_ref[...]
        groups = ((s_ref, ws_ref, ds_ref, dws_ref), (a_ref, wa_ref, da_ref, dwa_ref), (m_ref, wm_ref, dmm_ref, dwm_ref))
        for k, (x_ref, w_ref, dx_ref, dw_ref) in enumerate(groups):
            cs = pl.ds(k * d, d)
            gk = gt[:, k * d:(k + 1) * d]
            xb = x_ref[...].astype(BF16)
            br = _dot(xb, w_ref[...], 1, 1)
            dbr = (dm * gk).astype(BF16)
            dx_ref[...] = _dot(dbr, w_ref[...], 1, 0)
            dw_ref[...] += _dot(dbr, xb, 0, 0)
            dzg = dm * br * gk * (1.0 - gk)
            dzg_ref[:, cs] = dzg.astype(BF16)
            db_ref[:, cs] += jnp.sum(dzg, axis=0, keepdims=True)

    row = lambda w: pl.BlockSpec((tm, w), lambda i: (i, 0))
    whole = lambda shape: pl.BlockSpec(shape, lambda i: (0, 0))
    res = _call_with_carry(
        body, carry, name=name, grid=(rows // tm,),
        in_specs=[row(d)] + [row(x.shape[1]) for x in acts] + [_resident(w.shape) for w in wts] + [row(3 * d), whole((1, 3 * d))],
        out_specs=[row(x.shape[1]) for x in acts] + [whole(w.shape) for w in wts] + [row(3 * d), whole((1, 3 * d))],
        out_shape=[jax.ShapeDtypeStruct(x.shape, F32) for x in acts] + [jax.ShapeDtypeStruct(w.shape, F32) for w in wts]
        + [jax.ShapeDtypeStruct((rows, 3 * d), BF16), jax.ShapeDtypeStruct((1, 3 * d), F32)],
        scratch=[], operands=[dmerged, *acts, *wts, zg, b_gate], semantics=("arbitrary",))
    return tuple(res) if carry is None else (tuple(res[:8]), list(res[8:]))


def _adamw(w, g, m, v, *, tr, name):
    rows, cols = w.shape[-2:]
    assert rows % tr == 0, (name, rows, tr)

    def body(w_ref, g_ref, m_ref, v_ref, g_out, d_ref, nm_ref, nv_ref):
        gv = g_ref[...]
        m2 = ADAM_B1 * m_ref[...] + (1.0 - ADAM_B1) * gv
        v2 = ADAM_B2 * v_ref[...] + (1.0 - ADAM_B2) * (gv * gv)
        m_hat = m2 / (1.0 - ADAM_B1 ** ADAM_STEP)
        v_hat = v2 / (1.0 - ADAM_B2 ** ADAM_STEP)
        g_out[...] = gv
        d_ref[...] = -ADAM_LR * (m_hat / (jnp.sqrt(v_hat) + ADAM_EPS) + ADAM_WD * w_ref[...])
        nm_ref[...] = m2
        nv_ref[...] = v2

    flat = pl.BlockSpec((tr, cols), lambda i: (i, 0))
    blk = flat if w.ndim == 2 else pl.BlockSpec((None, tr, cols), lambda i: (0, i, 0))
    shape = jax.ShapeDtypeStruct(w.shape, F32)
    return pl.pallas_call(
        body, name=name, grid=(rows // tr,), in_specs=[blk, flat, blk, blk], out_specs=[blk] * 4,
        out_shape=[shape] * 4, compiler_params=_params(("parallel",)),
    )(w, g, m, v)


ANY = pl.BlockSpec(memory_space=pl.ANY)


def _position():
    return lax.axis_index("x"), lax.axis_index("y"), lax.axis_index("c")


def _other_chips(x, y):
    return ((1 - x, y), (x, 1 - y), (1 - x, 1 - y))


def _remote(src, dst, send_sem, recv_sem, dev):
    return pltpu.make_async_remote_copy(src_ref=src, dst_ref=dst, send_sem=send_sem, recv_sem=recv_sem,
                                        device_id=dev, device_id_type=MESH)


def _gather_exchange(shards):
    nb = len(shards)

    def rows_of(i, owner, core):
        rs = shards[i].shape[0]
        return pl.ds(pl.multiple_of(owner * rs + core * (rs // 2), 16), rs // 2)

    def first_leg(ins, outs, send_sems, recv_sems, i, j):
        x, y, c = _position()
        px, py = _other_chips(x, y)[j]
        half = shards[i].shape[0] // 2
        mine = ins[i].at[pl.ds(pl.multiple_of(c * half, 16), half)]
        return _remote(mine, outs[i].at[rows_of(i, 2 * x + y, c)], send_sems.at[i, j], recv_sems.at[i, j], (px, py, c))

    def passed_on(outs, send_sems, recv_sems, i, j, core):
        x, y, c = _position()
        px, py = _other_chips(x, y)[j]
        rows = outs[i].at[rows_of(i, 2 * px + py, core)]
        return _remote(rows, rows, send_sems.at[i, 3 + j], recv_sems.at[i, 3 + j], (x, y, 1 - c))

    def own_block(ins, outs, send_sems, recv_sems, i):
        x, y, c = _position()
        rs = shards[i].shape[0]
        place = outs[i].at[pl.ds(pl.multiple_of((2 * x + y) * rs, 16), rs)]
        return _remote(ins[i], place, send_sems.at[i, 6], recv_sems.at[i, 6], (x, y, 1 - c))

    def start(ins, outs, send_sems, recv_sems):
        for i in range(nb):
            own_block(ins, outs, send_sems, recv_sems, i).start()
            for j in range(3):
                first_leg(ins, outs, send_sems, recv_sems, i, j).start()

    def finish(ins, outs, send_sems, recv_sems):
        x, y, c = _position()
        for i in range(nb):
            for j, (px, py) in enumerate(_other_chips(x, y)):
                landed = outs[i].at[rows_of(i, 2 * px + py, c)]
                _remote(landed, landed, send_sems.at[i, j], recv_sems.at[i, j], (px, py, c)).wait_recv()
                passed_on(outs, send_sems, recv_sems, i, j, c).start()
        for i in range(nb):
            own_block(ins, outs, send_sems, recv_sems, i).wait()
            for j in range(3):
                passed_on(outs, send_sems, recv_sems, i, j, 1 - c).wait_recv()
        for i in range(nb):
            for j in range(3):
                first_leg(ins, outs, send_sems, recv_sems, i, j).wait_send()
                passed_on(outs, send_sems, recv_sems, i, j, c).wait_send()

    return _Exchange(ins=list(shards), outs=[jax.ShapeDtypeStruct((N_CHIPS * s.shape[0], s.shape[1]), s.dtype) for s in shards],
                     aliases={}, sems=[(nb, 7), (nb, 7)], start=start, finish=finish)


def _run_exchange(ex, *, name):
    n_in, n_out = len(ex.ins), len(ex.outs)

    def body(*refs):
        c_in, c_out, sems = refs[:n_in], refs[n_in:n_in + n_out], refs[n_in + n_out:]
        ex.start(c_in, c_out, *sems)
        ex.finish(c_in, c_out, *sems)

    return pl.pallas_call(
        body, name=name, in_specs=[ANY] * n_in, out_specs=[ANY] * n_out, out_shape=list(ex.outs),
        input_output_aliases=dict(ex.aliases),
        scratch_shapes=[pltpu.SemaphoreType.DMA(s) for s in ex.sems],
    )(*ex.ins)


def _row_tile(rows):
    return max(t for t in range(16, min(rows, 512) + 1, 16) if rows % t == 0)


def _halves_exchange(grads):
    nb = len(grads)

    def copies(ins, outs, send_sems, recv_sems):
        x, y, c = _position()
        return [_remote(ins[i].at[:, 1 - c], outs[i], send_sems.at[i], recv_sems.at[i], (x, y, 1 - c)) for i in range(nb)]

    def start(ins, outs, send_sems, recv_sems):
        for cp in copies(ins, outs, send_sems, recv_sems):
            cp.start()

    def finish(ins, outs, send_sems, recv_sems):
        for cp in copies(ins, outs, send_sems, recv_sems):
            cp.wait()

    return _Exchange(ins=list(grads), outs=[jax.ShapeDtypeStruct((N_CHIPS, g.shape[2], g.shape[3]), F32) for g in grads],
                     aliases={}, sems=[(nb,), (nb,)], start=start, finish=finish)


def _join_exchanges(parts):
    assert all(not ex.aliases for ex in parts)

    def split(refs, counts):
        out, at = [], 0
        for k in counts:
            out.append(refs[at:at + k])
            at += k
        return out

    def run(which):
        def go(ins, outs, *sems):
            for ex, i, o, s in zip(parts, split(ins, [len(ex.ins) for ex in parts]), split(outs, [len(ex.outs) for ex in parts]),
                                   split(sems, [len(ex.sems) for ex in parts])):
                getattr(ex, which)(i, o, *s)
        return go

    return _Exchange(ins=[a for ex in parts for a in ex.ins], outs=[a for ex in parts for a in ex.outs], aliases={},
                     sems=[s for ex in parts for s in ex.sems], start=run("start"), finish=run("finish"))


def _pair_sum(g4, got, c_arr, *, name):
    _, _, half, cols = g4.shape
    tr = _row_tile(half)

    def body(c_ref, g_ref, t_ref, p_ref, pb_ref):
        sm = g_ref[...] + t_ref[...]
        p_ref[...] = sm
        pb_ref[...] = sm.astype(BF16)

    blk = pl.BlockSpec((None, tr, cols), lambda j, i, c_ref: (j, i, 0))
    grid_spec = pltpu.PrefetchScalarGridSpec(
        num_scalar_prefetch=1, grid=(N_CHIPS, half // tr),
        in_specs=[pl.BlockSpec((None, None, tr, cols), lambda j, i, c_ref: (j, c_ref[0], i, 0)), blk],
        out_specs=[blk, blk])
    return pl.pallas_call(
        body, name=name, grid_spec=grid_spec,
        out_shape=[jax.ShapeDtypeStruct((N_CHIPS, half, cols), F32), jax.ShapeDtypeStruct((N_CHIPS, half, cols), BF16)],
        compiler_params=_params(("parallel", "parallel")),
    )(c_arr, g4, got)


def _scatter_exchange(parts):
    nb = len(parts)

    def copies(ins, outs, send_sems, recv_sems):
        x, y, c = _position()
        return [_remote(ins[i].at[2 * px + py], outs[i].at[j], send_sems.at[i, j], recv_sems.at[i, j], (px, py, c))
                for i in range(nb) for j, (px, py) in enumerate(_other_chips(x, y))]

    def start(ins, outs, send_sems, recv_sems):
        for cp in copies(ins, outs, send_sems, recv_sems):
            cp.start()

    def finish(ins, outs, send_sems, recv_sems):
        for cp in copies(ins, outs, send_sems, recv_sems):
            cp.wait()

    return _Exchange(ins=list(parts), outs=[jax.ShapeDtypeStruct((3,) + p.shape[1:], p.dtype) for p in parts],
                     aliases={}, sems=[(nb, 3), (nb, 3)], start=start, finish=finish)


def _owner_sum(p, got, chip_arr, c_arr, *, replicated, name):
    _, half, cols = p.shape
    tr = _row_tile(half)

    def body(chip_ref, c_ref, p_ref, r_ref, o_ref):
        o_ref[...] = ((p_ref[...] + r_ref[0].astype(F32)) + r_ref[1].astype(F32)) + r_ref[2].astype(F32)

    if replicated:
        out_spec = pl.BlockSpec((None, None, tr, cols), lambda i, chip_ref, c_ref: (chip_ref[0], c_ref[0], i, 0))
        out_shape = jax.ShapeDtypeStruct((N_CHIPS, 2, half, cols), F32)
    else:
        out_spec = pl.BlockSpec((None, tr, cols), lambda i, chip_ref, c_ref: (c_ref[0], i, 0))
        out_shape = jax.ShapeDtypeStruct((2, half, cols), F32)
    grid_spec = pltpu.PrefetchScalarGridSpec(
        num_scalar_prefetch=2, grid=(half // tr,),
        in_specs=[pl.BlockSpec((None, tr, cols), lambda i, chip_ref, c_ref: (chip_ref[0], i, 0)),
                  pl.BlockSpec((3, tr, cols), lambda i, chip_ref, c_ref: (0, i, 0))],
        out_specs=out_spec)
    return pl.pallas_call(
        body, name=name, grid_spec=grid_spec, out_shape=out_shape,
        compiler_params=_params(("parallel",)),
    )(chip_arr, c_arr, p, got)


def _share_reduced(bufs):
    nb = len(bufs) - 1

    def body(*refs):
        outs = refs[nb + 1:2 * nb + 2]
        send_sems, recv_sems = refs[2 * nb + 2:]
        x, y, c = _position()
        chip = 2 * x + y
        sends = []
        for i in range(nb):
            cp = _remote(outs[i].at[c], outs[i].at[c], send_sems.at[i], recv_sems.at[i], (x, y, 1 - c))
            cp.start()
            sends.append(cp)
        small = outs[nb]
        peers = [(fx, fy, fc) for fx in (0, 1) for fy in (0, 1) for fc in (0, 1) if fx + fy + fc > 0]
        for k, (fx, fy, fc) in enumerate(peers):
            dev = (x ^ fx, y ^ fy, c ^ fc)
            cp = _remote(small.at[chip, c], small.at[chip, c], send_sems.at[nb + k], recv_sems.at[nb + k], dev)
            cp.start()
            sends.append(cp)
        for i in range(nb):
            dst = outs[i].at[1 - c]
            _remote(dst, dst, send_sems.at[i], recv_sems.at[i], (x, y, 1 - c)).wait_recv()
        for k, (fx, fy, fc) in enumerate(peers):
            dst = small.at[2 * (x ^ fx) + (y ^ fy), c ^ fc]
            _remote(dst, dst, send_sems.at[nb + k], recv_sems.at[nb + k], (x ^ fx, y ^ fy, c ^ fc)).wait_recv()
        for cp in sends:
            cp.wait_send()

    n_all = nb + 1
    return pl.pallas_call(
        body, name="grad_share_reduced", in_specs=[ANY] * n_all, out_specs=[ANY] * n_all,
        out_shape=[jax.ShapeDtypeStruct(b.shape, b.dtype) for b in bufs],
        input_output_aliases={i: i for i in range(n_all)},
        scratch_shapes=[pltpu.SemaphoreType.DMA((nb + 7,)), pltpu.SemaphoreType.DMA((nb + 7,))],
    )(*bufs)


class _GradReducer:
    def __init__(self, c_arr, chip_arr):
        self.c_arr, self.chip_arr = c_arr, chip_arr
        self.full, self.pairs, self.landed = {}, {}, {}

    def swap(self, names, grads):
        for n, g in zip(names, grads):
            self.full[n] = g.reshape(N_CHIPS, 2, g.shape[0] // (2 * N_CHIPS), g.shape[1])
        return _halves_exchange([self.full[n] for n in names])

    def swapped(self, names, bufs):
        for n, t in zip(names, bufs):
            self.pairs[n] = _pair_sum(self.full[n], t, self.c_arr, name="grad_pair_sum_" + n)

    def scatter(self, names):
        return _scatter_exchange([self.pairs[n][1] for n in names])

    def collect(self, names, bufs):
        self.landed.update(zip(names, bufs))

    def swap_now(self, names, grads):
        self.swapped(names, _run_exchange(self.swap(names, grads), name="grad_exchange_" + names[0]))

    def finish(self, names, grads, order):
        self.swap_now(names, grads)
        self.collect(names, _run_exchange(self.scatter(names), name="grad_scatter_" + names[0]))
        totals = [_owner_sum(self.pairs[n][0], self.landed[n], self.chip_arr, self.c_arr, replicated=(n == order[-1]),
                             name="grad_owner_sum_" + n) for n in order]
        return _share_reduced(totals)


def _pack_small(vals):
    flat = jnp.concatenate([vals[name].reshape(-1) for name, _ in SMALL])
    return jnp.pad(flat, (0, N_CHIPS * SMALL_ROWS * 1024 - SMALL_ELEMS)).reshape(N_CHIPS * SMALL_ROWS, 1024)


def _unpack_small(buf):
    flat = buf.reshape(-1)
    out, off = {}, 0
    for name, shape in SMALL:
        n = int(np.prod(shape))
        out[name] = flat[off:off + n].reshape(shape)
        off += n
    return out


EARLY_REDUCED = (("w_down",), ("w_up",), ("w_o", "w_ssm_br", "w_attn_br", "w_mem_br", "w_glu", "w_mem_kv"), ("w_in",))


def _device_step(x, mem, tgt, w, p, *, shards, reducer):
    rows = x.shape[0]
    w = dict(w)
    early = EARLY_REDUCED
    gb = {}
    gather_pending = shards is not None

    def riding(*stages):
        if reducer is None or not stages:
            return None
        return _join_exchanges([reducer.swap(names, [gb[n] for n in names]) if kind == "swap" else reducer.scatter(names)
                                for kind, names in stages])

    def arrived(stages, res):
        if reducer is None or not stages:
            return res
        main, bufs = res
        for kind, names in stages:
            (reducer.swapped if kind == "swap" else reducer.collect)(names, bufs[:len(names)])
            bufs = bufs[len(names):]
        return main

    def fetching(names):
        return _gather_exchange([shards[n] for n in names]) if gather_pending else None

    def fetched(names, res):
        if not gather_pending:
            return res
        w.update(zip(names, res[1]))
        return res[0]

    first_use = (("w_in",), ("w_glu", "w_ssm_br", "w_attn_br", "w_mem_kv", "w_mem_br", "w_o", "w_up"), ("w_down",))
    g1, gm, g2 = p["norm1_g"], p["mem_norm_g"], p["norm2_g"]
    gf = p["final_g"].reshape(1, D_MODEL)
    ssm_args = (p["ssm_lambda_re"][0], p["ssm_lambda_im"][0], p["ssm_log_dt"][0], p["ssm_b_re"][0],
                p["ssm_b_im"][0], p["ssm_c_re"][0], p["ssm_c_im"][0])
    (a_lay, b_blk, c_blk), ssm_vjp = jax.vjp(_ssm_matrices, *ssm_args)
    a_conj = a_lay * _to_scan_layout(jnp.stack([jnp.ones((N_STATES,), F32), -jnp.ones((N_STATES,), F32)]))[None, :]
    dd = p["ssm_d"].reshape(1, SSM_WIDTH)
    mm = _matmul

    n1 = fetched(first_use[0], _rmsnorm_fwd(x, g1, tm=512, carry=fetching(first_use[0]), name="norm1"))
    win_t = w["w_in"]
    splits = ((OFF_U, OFF_QKV - OFF_U), (OFF_QKV, OFF_MQ - OFF_QKV), (OFF_MQ, OFF_ZG - OFF_MQ), (OFF_ZG, IN_WIDTH - OFF_ZG))
    u, qkv, mq, zg = fetched(first_use[1], _split_matmul(n1, win_t, splits, tm=512, carry=fetching(first_use[1]),
                                                         vmem=VMEM_LIMIT_WIDE_BYTES, name="in_proj"))

    u_i = _interleave(u)
    ends = _ssm_ends(a_lay, u_i, b_blk, transpose=False, reverse=False, tt=512, name="ssm_fwd_ends")
    s, ys_i, s_entry = _ssm_fwd(a_lay, u_i, b_blk, c_blk, ends, tt=512, name="ssm_fwd")
    ys = _deinterleave(ys_i)
    y0, tglu, y2 = _glu_fwd(ys, u, dd, w["w_glu"], p["b_glu"], tm=512, name="glu_fwd")

    outs, lses = [], []
    for g, (_, d) in enumerate(ATTN_PATTERNS):
        o_g, lse_g = _attn_fwd(qkv, g, d, name=f"attn_fwd_{g}")
        outs.append(o_g)
        lses.append(lse_g)
    o, lse = _attn_merge(outs, lses, tm=1024, name="attn_merge")

    mn = _rmsnorm_fwd(mem, gm, tm=MEM_LEN, name="mem_norm")
    kv = mm(mn, w["w_mem_kv"], m=MEM_LEN, n=1024, k=1024, tm=MEM_LEN, tn=1024, tk=1024, out_dtypes=(F32,), name="mem_kv")
    mo = _mem_attn_fwd(mq, kv, tq=1024, name="mem_attn_fwd")

    branch_acts = (y2, o, mo)
    branch_wts = (w["w_ssm_br"], w["w_attn_br"], w["w_mem_br"])
    merged = _branch_merge_fwd(branch_acts, branch_wts, zg, p["b_gate"], tm=256, name="branch_merge_fwd")
    h1, n2 = mm(merged, w["w_o"], m=rows, n=1024, k=1024, tm=1024, tn=1024, tk=1024, out_dtypes=(F32, BF16),
                aux=((x, "mn"), (g2, "row")), epilogue=_residual_norm_epilogue, name="out_proj")
    relu2 = lambda acc: (jnp.square(jnp.maximum(acc, 0.0)),)
    act = fetched(first_use[2], _sum_matmul([n2], w["w_up"], [0], tb=True, tm=512, out_dtype=BF16, epilogue=relu2,
                                            carry=fetching(first_use[2]), name="mlp_up"))
    dh2, d_gf, sq_err = _sum_matmul([act], w["w_down"], [0], tm=512, aux=((h1, "mn"), (tgt, "mn"), (gf, "row")),
                                    epilogue=_loss_head_epilogue, n_sums=2, name="mlp_down")
    loss = (0.5 / D_MODEL) * jnp.sum(sq_err)

    gs = {"final_g": d_gf.reshape(D_MODEL)}
    drelu2 = lambda acc, actv: (acc * (2.0 * jnp.sqrt(actv.astype(F32))),)
    dup = mm(dh2, w["w_down"], m=rows, n=D_FF, k=1024, tb=True, tm=1024, tn=2048, tk=1024, out_dtypes=(BF16,),
             aux=((act, "mn"),), epilogue=drelu2, name="d_act")
    gb["w_down"] = mm(act, dh2, m=D_FF, n=1024, k=rows, ta=True, tm=1024, tn=1024, tk=2048, out_dtypes=(F32,), name="dw_down")
    stages = (("swap", early[0]),)
    gb["w_up"] = arrived(stages, mm(dup, n2, m=D_FF, n=1024, k=rows, ta=True, tm=1024, tn=1024, tk=2048,
                                    out_dtypes=(F32,), carry=riding(*stages), name="dw_up"))
    stages = (("scatter", early[0]), ("swap", early[1]))
    dh1, gs["norm2_g"] = arrived(stages, _sum_matmul([dup], w["w_up"], [0], tm=512, aux=((h1, "mn"), (dh2, "mn"), (g2, "row")),
                                                     epilogue=_rmsnorm_bwd_epilogue, n_sums=1, carry=riding(*stages), name="d_n2"))
    dmerged = mm(dh1, w["w_o"], m=rows, n=1024, k=1024, tb=True, tm=1024, tn=1024, tk=1024, out_dtypes=(F32,), name="d_merged")
    gb["w_o"] = mm(merged, dh1, m=1024, n=1024, k=rows, ta=True, tm=1024, tn=1024, tk=2048, out_dtypes=(F32,), name="dw_o")
    stages = (("scatter", early[1]),)
    (dy2, do, dmo, gb["w_ssm_br"], gb["w_attn_br"], gb["w_mem_br"], dzg, gs["b_gate"]) = arrived(stages, _branch_merge_bwd(
        dmerged, branch_acts, branch_wts, zg, p["b_gate"], tm=256, carry=riding(*stages), name="branch_merge_bwd"))

    dy0, dt, y1, gs["b_glu"], d_dd = _glu_bwd(dy2, y0, tglu, u, w["w_glu"], tm=512, name="glu_bwd")
    gs["ssm_d"] = d_dd.reshape(1, SSM_GROUPS, SSM_GROUP_SIZE)
    gb["w_glu"] = mm(y1, dt, m=512, n=512, k=rows, ta=True, tm=512, tn=512, tk=1024, out_dtypes=(F32,), name="dw_glu")
    dy0_i = _interleave(dy0)
    lam_ends = _ssm_ends(a_conj, dy0_i, c_blk, transpose=True, reverse=True, tt=512, name="ssm_bwd_ends")
    du_i, d_b_blk, d_c_blk, d_a_lay = _ssm_bwd(a_conj, dy0_i, u_i, s, s_entry, b_blk, c_blk, dd, lam_ends, tt=512,
                                                name="ssm_bwd")
    du = _deinterleave(du_i)
    d_ssm = ssm_vjp((d_a_lay, d_b_blk, d_c_blk))
    for name, val in zip(("ssm_lambda_re", "ssm_lambda_im", "ssm_log_dt", "ssm_b_re", "ssm_b_im", "ssm_c_re", "ssm_c_im"), d_ssm):
        gs[name] = val[None]

    dqkv = None
    for g, (_, d) in enumerate(ATTN_PATTERNS):
        dqkv = _attn_bwd(qkv, do, o, lse, g, d, dqkv, name=f"attn_bwd_{g}")

    dmq, dmk, dmv = _mem_attn_bwd(mq, kv, dmo, tq=1024, name="mem_attn_bwd")
    dkv = jnp.concatenate([dmk, dmv], axis=1)
    gb["w_mem_kv"] = mm(mn, dkv, m=1024, n=1024, k=MEM_LEN, ta=True, tm=1024, tn=1024, tk=MEM_LEN, out_dtypes=(F32,), name="dw_mem_kv")
    dmn = mm(dkv, w["w_mem_kv"], m=MEM_LEN, n=1024, k=1024, tb=True, tm=MEM_LEN, tn=1024, tk=1024, out_dtypes=(F32,), name="d_mn")
    _, gs["mem_norm_g"] = _rmsnorm_bwd(mem, gm, dmn, None, tm=MEM_LEN, name="mem_norm_bwd")

    pieces = ((du, OFF_U, "u"), (dqkv[0], OFF_QKV, "q"), (dqkv[1], OFF_QKV + 768, "k"), (dqkv[2], OFF_QKV + 1536, "v"),
              (dmq, OFF_MQ, "mq"), (dzg, OFF_ZG, "zg"))
    dw_rows = []
    for piece, off, tag in pieces:
        width = piece.shape[1]
        tmw = 1024 if width % 1024 == 0 else (768 if width == 768 else 512)
        stages = {"q": (("swap", early[2]),), "zg": (("scatter", early[2]),)}.get(tag, ())
        dw_rows.append(arrived(stages, mm(piece, n1, m=width, n=1024, k=rows, ta=True, tm=tmw, tn=1024, tk=2048,
                                          out_dtypes=(F32,), carry=riding(*stages), name="dw_in_" + tag)))
    gb["w_in"] = jnp.concatenate(dw_rows, axis=0)
    if reducer is not None:
        reducer.swap_now(early[3], [gb["w_in"]])
    stages = (("scatter", early[3]),)
    dx, gs["norm1_g"] = arrived(stages, _sum_matmul(
        [piece for piece, _, _ in pieces], win_t, [off for _, off, _ in pieces], tm=512,
        aux=((x, "mn"), (dh1, "mn"), (g1, "row")), epilogue=_rmsnorm_bwd_epilogue, n_sums=1,
        carry=riding(*stages), vmem=VMEM_LIMIT_WIDE_BYTES, name="d_n1"))
    return loss, dx, gb, gs


def kernel(x, mem, norm1_g, mem_norm_g, w_in, b_gate, ssm_lambda_re, ssm_lambda_im, ssm_log_dt, ssm_b_re, ssm_b_im, ssm_c_re, ssm_c_im, ssm_d, w_glu, b_glu, w_ssm_br, w_attn_br, w_mem_kv, w_mem_br, w_o, norm2_g, w_up, w_down, final_g, loss_target, m_norm1_g, m_mem_norm_g, m_w_in, m_b_gate, m_ssm_lambda_re, m_ssm_lambda_im, m_ssm_log_dt, m_ssm_b_re, m_ssm_b_im, m_ssm_c_re, m_ssm_c_im, m_ssm_d, m_w_glu, m_b_glu, m_w_ssm_br, m_w_attn_br, m_w_mem_kv, m_w_mem_br, m_w_o, m_norm2_g, m_w_up, m_w_down, m_final_g, v_norm1_g, v_mem_norm_g, v_w_in, v_b_gate, v_ssm_lambda_re, v_ssm_lambda_im, v_ssm_log_dt, v_ssm_b_re, v_ssm_b_im, v_ssm_c_re, v_ssm_c_im, v_ssm_d, v_w_glu, v_b_glu, v_w_ssm_br, v_w_attn_br, v_w_mem_kv, v_w_mem_br, v_w_o, v_norm2_g, v_w_up, v_w_down, v_final_g):
    env = dict(locals())
    weights = {n: env[n] for n in WEIGHT_ORDER}
    moms = {n: env["m_" + n] for n in WEIGHT_ORDER}
    vels = {n: env["v_" + n] for n in WEIGHT_ORDER}

    chip = 2 * lax.axis_index("x") + lax.axis_index("y")
    wire = [weights[n].reshape(weights[n].shape[-2:]).astype(BF16) for n, _, _ in BIG]
    wire = dict(zip([n for n, _, _ in BIG], [s.T if tr else s for s, (_, tr, _) in zip(wire, BIG)]))
    small = {n: weights[n] for n, _ in SMALL}

    reducer = _GradReducer(lax.axis_index("c").astype(jnp.int32).reshape(1), chip.astype(jnp.int32).reshape(1))
    loss, dx, gb, gs = _device_step(x[0], mem[0], loss_target[0], {}, small, shards=wire, reducer=reducer)
    *shards, small_grad = reducer.finish(["small"], [_pack_small(gs)], [n for n, _, _ in BIG] + ["small"])
    grads = {}
    for (n, tr, _), sh in zip(BIG, shards):
        sh = sh.reshape(2 * sh.shape[1], sh.shape[2])
        grads[n] = sh.T if tr else sh
    small_grad = small_grad.reshape(N_CHIPS * SMALL_ROWS, 1024)
    grads_small = _unpack_small(small_grad)

    delta, new_m, new_v = {}, {}, {}
    for n, _, _ in BIG:
        grads[n], delta[n], new_m[n], new_v[n] = _adamw(weights[n], grads[n], moms[n], vels[n],
                                                        tr=min(weights[n].shape[-2], 256), name="adamw_" + n)
    _, ds_, ms_, vs_ = _adamw(_pack_small(small), small_grad,
                              _pack_small({n: moms[n] for n, _ in SMALL}), _pack_small({n: vels[n] for n, _ in SMALL}),
                              tr=N_CHIPS * SMALL_ROWS, name="adamw_small")
    for dst, buf in ((delta, ds_), (new_m, ms_), (new_v, vs_)):
        dst.update(_unpack_small(buf))
    grads.update(grads_small)

    total_loss = lax.psum(loss, ("x", "y", "c"))
    return (total_loss, dx[None], *[grads[n] for n in WEIGHT_ORDER], *[delta[n] for n in WEIGHT_ORDER],
            *[new_m[n] for n in WEIGHT_ORDER], *[new_v[n] for n in WEIGHT_ORDER])
```

```python
import functools
import math

import numpy as np
import jax
import jax.numpy as jnp
from jax import lax
from jax.experimental import pallas as pl
from jax.experimental.pallas import tpu as pltpu

F32 = jnp.float32
BF16 = jnp.bfloat16

D_MODEL = 1024
SSM_GROUPS = 32
SSM_GROUP_SIZE = 16
SSM_STATE = 64
SSM_WIDTH = 512
N_STATES = SSM_GROUPS * SSM_STATE
SCAN_CB = 1024
ATTN_PATTERNS = ((128, 1), (512, 4), (2048, 16))
ATTN_HEAD_DIM = 64
ATTN_Q = 128
MEM_LEN = 256
MEM_HEAD_DIM = 128
MEM_HEADS = 4
D_FF = 4096
OFF_U, OFF_QKV, OFF_MQ, OFF_ZG = 0, 512, 2816, 3328
IN_WIDTH = 6400
RMS_EPS = 1e-6
NEG_INF = -1e30
ADAM_LR, ADAM_B1, ADAM_B2, ADAM_EPS, ADAM_WD, ADAM_STEP = 0.001, 0.9, 0.999, 1e-08, 0.01, 10

VMEM_LIMIT_BYTES = 48 * 1024 * 1024
VMEM_LIMIT_WIDE_BYTES = 56 * 1024 * 1024
LANES = 128
MESH = pl.DeviceIdType.MESH
N_CHIPS = 4

SCAN_SEGS = 8
SCAN_GROUPS = SCAN_CB // SSM_STATE

BIG = (("w_in", True, (6400, 1024)), ("w_glu", False, (512, 512)), ("w_ssm_br", True, (1024, 512)),
       ("w_attn_br", True, (1024, 256)), ("w_mem_kv", False, (1024, 1024)), ("w_mem_br", True, (1024, 512)),
       ("w_o", False, (1024, 1024)), ("w_up", True, (4096, 1024)), ("w_down", False, (4096, 1024)))
SMALL = (("norm1_g", (1, 1024)), ("mem_norm_g", (1, 1024)), ("b_gate", (1, 3072)),
         ("ssm_lambda_re", (1, 32, 64)), ("ssm_lambda_im", (1, 32, 64)), ("ssm_log_dt", (1, 32)),
         ("ssm_b_re", (1, 32, 64, 16)), ("ssm_b_im", (1, 32, 64, 16)), ("ssm_c_re", (1, 32, 16, 64)),
         ("ssm_c_im", (1, 32, 16, 64)), ("ssm_d", (1, 32, 16)), ("b_glu", (1, 512)),
         ("norm2_g", (1, 1024)), ("final_g", (1024,)))
WEIGHT_ORDER = ("norm1_g", "mem_norm_g", "w_in", "b_gate", "ssm_lambda_re", "ssm_lambda_im", "ssm_log_dt",
                "ssm_b_re", "ssm_b_im", "ssm_c_re", "ssm_c_im", "ssm_d", "w_glu", "b_glu", "w_ssm_br",
                "w_attn_br", "w_mem_kv", "w_mem_br", "w_o", "norm2_g", "w_up", "w_down", "final_g")
SMALL_ELEMS = sum(int(np.prod(s)) for _, s in SMALL)
SMALL_ROWS = 64


def _params(sem, vmem=VMEM_LIMIT_BYTES):
    return pltpu.CompilerParams(dimension_semantics=sem, vmem_limit_bytes=vmem)


def _sigmoid(v):
    return 0.5 * jnp.tanh(0.5 * v) + 0.5


_GELU_C = math.sqrt(2.0 / math.pi)


def _gelu(v):
    return 0.5 * v * (1.0 + jnp.tanh(_GELU_C * (v + 0.044715 * v * v * v)))


def _gelu_grad(v):
    th = jnp.tanh(_GELU_C * (v + 0.044715 * v * v * v))
    return 0.5 * (1.0 + th) + 0.5 * v * (1.0 - th * th) * _GELU_C * (1.0 + 3.0 * 0.044715 * v * v)


def _dot(a, b, ca, cb):
    return lax.dot_general(a, b, (((ca,), (cb,)), ((), ())), preferred_element_type=F32)


class _Exchange:
    def __init__(self, ins, outs, aliases, sems, start, finish):
        self.ins, self.outs, self.aliases, self.sems, self.start, self.finish = ins, outs, aliases, sems, start, finish


def _matmul(a, b, *, m, n, k, ta=False, tb=False, tm, tn, tk, out_dtypes, name,
            aux=(), epilogue=None, n_sums=0, carry=None):
    assert m % tm == 0 and n % tn == 0 and k % tk == 0, (name, m, n, k, tm, tn, tk)
    assert n_sums == 0 or tn == n, name
    nk = k // tk
    n_aux = len(aux)
    n_tiles = len(out_dtypes)
    n_out = n_tiles + n_sums
    a_spec = pl.BlockSpec((tk, tm), lambda i, j, kk: (kk, i)) if ta else pl.BlockSpec((tm, tk), lambda i, j, kk: (i, kk))
    b_spec = pl.BlockSpec((tn, tk), lambda i, j, kk: (j, kk)) if tb else pl.BlockSpec((tk, tn), lambda i, j, kk: (kk, j))
    aux_specs = []
    for _, kind in aux:
        if kind == "mn":
            aux_specs.append(pl.BlockSpec((tm, tn), lambda i, j, kk: (i, j)))
        else:
            aux_specs.append(pl.BlockSpec((1, tn), lambda i, j, kk: (0, j)))
    ca = 0 if ta else 1
    cb = 1 if tb else 0

    def finish(acc, aux_refs, out_refs, row_tile):
        outs = (acc,) if epilogue is None else epilogue(acc, *[r[...] for r in aux_refs])
        for o_ref, o in zip(out_refs[:n_tiles], outs[:n_tiles]):
            o_ref[...] = o.astype(o_ref.dtype)
        _accumulate_over_rows(out_refs[n_tiles:], outs[n_tiles:], row_tile)

    def body(a_ref, b_ref, *rest):
        aux_refs = rest[:n_aux]
        out_refs = rest[n_aux:n_aux + n_out]
        row_tile = pl.program_id(0)
        prod = _dot(a_ref[...].astype(BF16), b_ref[...].astype(BF16), ca, cb)
        if nk == 1:
            finish(prod, aux_refs, out_refs, row_tile)
            return
        acc_ref = rest[n_aux + n_out]
        kk = pl.program_id(2)

        @pl.when(kk == 0)
        def _():
            acc_ref[...] = prod

        @pl.when(jnp.logical_and(kk > 0, kk < nk - 1))
        def _():
            acc_ref[...] += prod

        @pl.when(kk == nk - 1)
        def _():
            finish(acc_ref[...] + prod, aux_refs, out_refs, row_tile)

    tile = pl.BlockSpec((tm, tn), lambda i, j, kk: (i, j))
    col_sum = pl.BlockSpec((1, tn), lambda i, j, kk: (0, j))
    res = _call_with_carry(
        body, carry, name=name, grid=(m // tm, n // tn, nk), in_specs=[a_spec, b_spec] + aux_specs,
        out_specs=[tile] * n_tiles + [col_sum] * n_sums,
        out_shape=[jax.ShapeDtypeStruct((m, n), dt) for dt in out_dtypes] + [jax.ShapeDtypeStruct((1, n), F32)] * n_sums,
        scratch=[pltpu.VMEM((tm, tn), F32)] if nk > 1 else [], operands=[a, b] + [x for x, _ in aux],
        semantics=("arbitrary" if n_sums else "parallel", "parallel", "arbitrary"))
    main = res[0] if n_out == 1 else tuple(res[:n_out])
    return main if carry is None else (main, list(res[n_out:]))


def _accumulate_over_rows(sum_refs, terms, row_tile):
    for s_ref, term in zip(sum_refs, terms):
        @pl.when(row_tile == 0)
        def _():
            s_ref[...] = term

        @pl.when(row_tile > 0)
        def _():
            s_ref[...] += term


def _call_with_carry(body, carry, *, name, grid, in_specs, out_specs, out_shape, scratch, operands, semantics,
                     vmem=VMEM_LIMIT_BYTES):
    if carry is None:
        return pl.pallas_call(body, name=name, grid=grid, in_specs=in_specs, out_specs=out_specs, out_shape=out_shape,
                              scratch_shapes=scratch, compiler_params=_params(semantics, vmem))(*operands)
    n_in, n_cin, n_out, n_cout, n_scr = len(operands), len(carry.ins), len(out_shape), len(carry.outs), len(scratch)

    def hosted(*refs):
        main_in, c_in = refs[:n_in], refs[n_in:n_in + n_cin]
        main_out = refs[n_in + n_cin:n_in + n_cin + n_out]
        c_out = refs[n_in + n_cin + n_out:n_in + n_cin + n_out + n_cout]
        rest = refs[n_in + n_cin + n_out + n_cout:]
        ids = [pl.program_id(t) for t in range(len(grid))]
        first = functools.reduce(jnp.logical_and, [i == 0 for i in ids])
        last = functools.reduce(jnp.logical_and, [i == g - 1 for i, g in zip(ids, grid)])

        @pl.when(first)
        def _():
            carry.start(c_in, c_out, *rest[n_scr:])

        body(*main_in, *main_out, *rest[:n_scr])

        @pl.when(last)
        def _():
            carry.finish(c_in, c_out, *rest[n_scr:])

    return pl.pallas_call(
        hosted, name=name, grid=grid,
        in_specs=list(in_specs) + [ANY] * n_cin, out_specs=list(out_specs) + [ANY] * n_cout,
        out_shape=list(out_shape) + list(carry.outs),
        input_output_aliases={n_in + i: n_out + o for i, o in carry.aliases.items()},
        scratch_shapes=list(scratch) + [pltpu.SemaphoreType.DMA(s) for s in carry.sems],
        compiler_params=_params(("arbitrary",) * len(grid), vmem),
    )(*operands, *carry.ins)


def _sum_matmul(pieces, b, offs, *, tm, name, tb=False, out_dtype=F32, aux=(), epilogue=None, n_sums=0, carry=None,
                vmem=VMEM_LIMIT_BYTES):
    m = pieces[0].shape[0]
    n = b.shape[0] if tb else b.shape[1]
    npieces, n_aux = len(pieces), len(aux)
    assert not tb or npieces == 1

    def body(*refs):
        b_ref = refs[npieces]
        aux_refs = refs[npieces + 1:npieces + 1 + n_aux]
        out_refs = refs[npieces + 1 + n_aux:]
        acc = None
        for p_ref, off in zip(refs[:npieces], offs):
            lhs = p_ref[...].astype(BF16)
            part = _dot(lhs, b_ref[...], 1, 1) if tb else _dot(lhs, b_ref[pl.ds(off, p_ref.shape[1]), :], 1, 0)
            acc = part if acc is None else acc + part
        outs = (acc,) if epilogue is None else epilogue(acc, *[r[...] for r in aux_refs])
        out_refs[0][...] = outs[0].astype(out_dtype)
        _accumulate_over_rows(out_refs[1:], outs[1:], pl.program_id(0))

    row = pl.BlockSpec((tm, n), lambda i: (i, 0))
    vec = pl.BlockSpec((1, n), lambda i: (0, 0))
    res = _call_with_carry(
        body, carry, name=name, grid=(m // tm,),
        in_specs=[pl.BlockSpec((tm, p.shape[1]), lambda i: (i, 0)) for p in pieces] + [_resident(b.shape)]
        + [row if kind == "mn" else vec for _, kind in aux],
        out_specs=[row] + [vec] * n_sums,
        out_shape=[jax.ShapeDtypeStruct((m, n), out_dtype)] + [jax.ShapeDtypeStruct((1, n), F32)] * n_sums,
        scratch=[], operands=list(pieces) + [b] + [x for x, _ in aux], semantics=("arbitrary" if n_sums else "parallel",),
        vmem=vmem)
    main = res[0] if n_sums == 0 else tuple(res[:1 + n_sums])
    return main if carry is None else (main, list(res[1 + n_sums:]))


def _split_matmul(a, b_t, splits, *, tm, name, carry=None, vmem=VMEM_LIMIT_BYTES):
    m, k = a.shape

    def body(a_ref, b_ref, *out_refs):
        av = a_ref[...].astype(BF16)
        for (row0, width), o_ref in zip(splits, out_refs):
            o_ref[...] = _dot(av, b_ref[pl.ds(row0, width), :], 1, 1)

    res = _call_with_carry(
        body, carry, name=name, grid=(m // tm,),
        in_specs=[pl.BlockSpec((tm, k), lambda i: (i, 0)), _resident(b_t.shape)],
        out_specs=[pl.BlockSpec((tm, width), lambda i: (i, 0)) for _, width in splits],
        out_shape=[jax.ShapeDtypeStruct((m, width), F32) for _, width in splits],
        scratch=[], operands=[a, b_t], semantics=("parallel",), vmem=vmem)
    outs = tuple(res[:len(splits)])
    return outs if carry is None else (outs, list(res[len(splits):]))


def _rmsnorm_fwd(x, g, *, tm, name, carry=None):
    rows, d = x.shape

    def body(x_ref, g_ref, o_ref):
        xv = x_ref[...]
        r = lax.rsqrt(jnp.mean(xv * xv, axis=-1, keepdims=True) + RMS_EPS)
        o_ref[...] = (xv * r * g_ref[...]).astype(o_ref.dtype)

    res = _call_with_carry(
        body, carry, name=name, grid=(rows // tm,),
        in_specs=[pl.BlockSpec((tm, d), lambda i: (i, 0)), pl.BlockSpec((1, d), lambda i: (0, 0))],
        out_specs=[pl.BlockSpec((tm, d), lambda i: (i, 0))], out_shape=[jax.ShapeDtypeStruct((rows, d), BF16)],
        scratch=[], operands=[x, g], semantics=("parallel",))
    return res[0] if carry is None else (res[0], list(res[1:]))


def _residual_norm_epilogue(acc, xv, gv):
    h = acc + xv
    r = lax.rsqrt(jnp.mean(h * h, axis=-1, keepdims=True) + RMS_EPS)
    return h, h * r * gv


def _rmsnorm_bwd_epilogue(dy, xv, resv, gv):
    r = lax.rsqrt(jnp.mean(xv * xv, axis=-1, keepdims=True) + RMS_EPS)
    xhat = xv * r
    dyg = dy * gv
    dx = r * (dyg - xhat * jnp.mean(dyg * xhat, axis=-1, keepdims=True)) + resv
    return dx, jnp.sum(dy * xhat, axis=0, keepdims=True)


def _rmsnorm_bwd(x, g, dy, res, *, tm, name):
    rows, d = x.shape
    has_res = res is not None

    def body(x_ref, g_ref, dy_ref, *rest):
        if has_res:
            res_ref, dx_ref, dg_ref = rest
        else:
            dx_ref, dg_ref = rest
        i = pl.program_id(0)
        xv = x_ref[...]
        r = lax.rsqrt(jnp.mean(xv * xv, axis=-1, keepdims=True) + RMS_EPS)
        xhat = xv * r
        dyv = dy_ref[...]
        dyg = dyv * g_ref[...]
        dx = r * (dyg - xhat * jnp.mean(dyg * xhat, axis=-1, keepdims=True))
        if has_res:
            dx = dx + res_ref[...]
        dx_ref[...] = dx

        @pl.when(i == 0)
        def _():
            dg_ref[...] = jnp.zeros_like(dg_ref)

        dg_ref[...] += jnp.sum(dyv * xhat, axis=0, keepdims=True)

    row_spec = pl.BlockSpec((tm, d), lambda i: (i, 0))
    vec_spec = pl.BlockSpec((1, d), lambda i: (0, 0))
    ins = [x, g, dy] + ([res] if has_res else [])
    return pl.pallas_call(
        body, name=name, grid=(rows // tm,),
        in_specs=[row_spec, vec_spec, row_spec] + ([row_spec] if has_res else []),
        out_specs=[row_spec, vec_spec],
        out_shape=[jax.ShapeDtypeStruct((rows, d), F32), jax.ShapeDtypeStruct((1, d), F32)],
        compiler_params=_params(("arbitrary",)),
    )(*ins)


def _loss_head_epilogue(acc, hv, tgtv, gv):
    xv = acc + hv
    r = lax.rsqrt(jnp.mean(xv * xv, axis=-1, keepdims=True) + RMS_EPS)
    xhat = xv * r
    err = xhat * gv - tgtv
    dyv = err * (1.0 / D_MODEL)
    dyg = dyv * gv
    dh = r * (dyg - xhat * jnp.mean(dyg * xhat, axis=-1, keepdims=True))
    return dh, jnp.sum(dyv * xhat, axis=0, keepdims=True), jnp.sum(err * err, axis=0, keepdims=True)


def _to_scan_layout(v):
    lead = v.shape[:-2]
    v = v.reshape(lead + (2, N_STATES // SCAN_CB, SCAN_CB))
    v = jnp.swapaxes(v, -3, -2)
    return v.reshape(lead + (2 * N_STATES,))


def _ssm_matrices(lam_re, lam_im, log_dt, b_re, b_im, c_re, c_im):
    dt = jnp.exp(log_dt)[:, None]
    mag = jnp.exp(lam_re * dt)
    a_re, a_im = mag * jnp.cos(lam_im * dt), mag * jnp.sin(lam_im * dt)
    nr, ni = a_re - 1.0, a_im
    den = lam_re * lam_re + lam_im * lam_im
    coef_re = (nr * lam_re + ni * lam_im) / den
    coef_im = (ni * lam_re - nr * lam_im) / den
    bb_re = coef_re[..., None] * b_re - coef_im[..., None] * b_im
    bb_im = coef_re[..., None] * b_im + coef_im[..., None] * b_re
    a_lay = _to_scan_layout(jnp.stack([a_re.reshape(-1), a_im.reshape(-1)], axis=0))[None, :]
    nblk = SSM_GROUPS // SCAN_GROUPS
    eye = jnp.eye(SCAN_GROUPS, dtype=F32)

    def b_block(bb):
        bb = bb.reshape(nblk, SCAN_GROUPS, SSM_STATE, SSM_GROUP_SIZE)
        return jnp.einsum("gk,jkph->jghkp", eye, bb).reshape(nblk, SCAN_GROUPS * SSM_GROUP_SIZE, SCAN_CB)

    b_blk = jnp.concatenate([b_block(bb_re), b_block(bb_im)], axis=2)

    def c_block(cc):
        cc = cc.reshape(nblk, SCAN_GROUPS, SSM_GROUP_SIZE, SSM_STATE)
        return jnp.einsum("gk,jghp->jkpgh", eye, cc).reshape(nblk, SCAN_CB, SCAN_GROUPS * SSM_GROUP_SIZE)

    c_blk = jnp.concatenate([c_block(c_re), -c_block(c_im)], axis=1)
    return a_lay, b_blk, c_blk


def _interleave(v):
    rows, c = v.shape
    return v.reshape(SCAN_SEGS, rows // SCAN_SEGS, c).transpose(1, 0, 2).reshape(rows, c)


def _deinterleave(v):
    rows, c = v.shape
    return v.reshape(rows // SCAN_SEGS, SCAN_SEGS, c).transpose(1, 0, 2).reshape(rows, c)


def _scan_groups(a_ref, bu_ref, o_ref, state, *, reverse, tt, unroll=4):
    cb = SCAN_CB
    ar = jnp.broadcast_to(a_ref[:, :cb], (SCAN_SEGS, cb))
    ai = jnp.broadcast_to(a_ref[:, cb:], (SCAN_SEGS, cb))
    ngroups = tt // SCAN_SEGS

    def step(i, st):
        sr, si = st
        r0 = pl.multiple_of(((ngroups - 1 - i) if reverse else i) * SCAN_SEGS, SCAN_SEGS)
        blk = bu_ref[pl.ds(r0, SCAN_SEGS), :]
        nr = ar * sr - ai * si + blk[:, :cb]
        ni = ar * si + ai * sr + blk[:, cb:]
        if o_ref is not None:
            o_ref[pl.ds(r0, SCAN_SEGS), :] = jnp.concatenate([nr, ni], axis=1)
        return nr, ni

    return lax.fori_loop(0, ngroups, step, state, unroll=unroll)


def _segment_entries(a_ref, e_ref, init_ref, *, reverse, seg_len):
    cb = SCAN_CB
    n_sq = seg_len.bit_length() - 1
    assert 1 << n_sq == seg_len, seg_len
    pr, pi = a_ref[:, :cb], a_ref[:, cb:]
    for _ in range(n_sq):
        pr, pi = pr * pr - pi * pi, 2.0 * pr * pi
    cr = jnp.zeros((1, cb), F32)
    ci = jnp.zeros((1, cb), F32)
    order = range(SCAN_SEGS - 1, -1, -1) if reverse else range(SCAN_SEGS)
    for k, seg in enumerate(order):
        if k > 0:
            prev = seg + 1 if reverse else seg - 1
            er, ei = e_ref[prev:prev + 1, :cb], e_ref[prev:prev + 1, cb:]
            cr, ci = pr * cr - pi * ci + er, pr * ci + pi * cr + ei
        init_ref[seg:seg + 1, :] = jnp.concatenate([cr, ci], axis=1)


def _ssm_specs(nt, tt, nch, reverse):
    cb = SCAN_CB
    tmap = (lambda j, kk: (nt - 1 - kk, j)) if reverse else (lambda j, kk: (kk, j))
    nmap = (lambda j, kk: (jnp.maximum(nt - 2 - kk, 0), j)) if reverse else (lambda j, kk: (jnp.minimum(kk + 1, nt - 1), j))
    return dict(a=pl.BlockSpec((1, 2 * cb), lambda j, kk: (0, j)),
                seg=pl.BlockSpec((SCAN_SEGS, 2 * cb), lambda j, kk: (0, j)),
                chan=pl.BlockSpec((tt, nch), tmap),
                next=pl.BlockSpec((tt, nch), nmap),
                state=pl.BlockSpec((tt, 2 * cb), tmap),
                b=pl.BlockSpec((None, nch, 2 * cb), lambda j, kk: (j, 0, 0)),
                c=pl.BlockSpec((None, 2 * cb, nch), lambda j, kk: (j, 0, 0)))


def _ssm_ends(a_lay, x, blocks, *, transpose, reverse, tt, name):
    rows = x.shape[0]
    nblk = blocks.shape[0]
    nch = x.shape[1] // nblk
    cb = SCAN_CB
    nt = rows // tt
    sp = _ssm_specs(nt, tt, nch, reverse)

    def body(a_ref, x_ref, xn_ref, w_ref, e_ref, even_ref, odd_ref):
        kk = pl.program_id(1)

        def product(src_ref, dst_ref):
            dst_ref[...] = _dot(src_ref[...].astype(BF16), w_ref[...].astype(BF16), 1, 1 if transpose else 0)

        @pl.when(kk == 0)
        def _():
            e_ref[...] = jnp.zeros_like(e_ref)
            product(x_ref, even_ref)

        def phase(cur_ref, next_ref):
            product(xn_ref, next_ref)
            sr, si = _scan_groups(a_ref, cur_ref, None, (e_ref[:, :cb], e_ref[:, cb:]), reverse=reverse, tt=tt, unroll=True)
            e_ref[...] = jnp.concatenate([sr, si], axis=1)

        @pl.when(kk % 2 == 0)
        def _():
            phase(even_ref, odd_ref)

        @pl.when(kk % 2 == 1)
        def _():
            phase(odd_ref, even_ref)

    return pl.pallas_call(
        body, name=name, grid=(nblk, nt),
        in_specs=[sp["a"], sp["chan"], sp["next"], sp["c"] if transpose else sp["b"]],
        out_specs=sp["seg"],
        out_shape=jax.ShapeDtypeStruct((SCAN_SEGS, nblk * 2 * cb), F32),
        scratch_shapes=[pltpu.VMEM((tt, 2 * cb), F32), pltpu.VMEM((tt, 2 * cb), F32)],
        compiler_params=_params(("parallel", "arbitrary")),
    )(a_lay, x, x, blocks)


def _ssm_fwd(a_lay, u, b_blk, c_blk, ends, *, tt, name):
    rows = u.shape[0]
    nblk = b_blk.shape[0]
    nch = u.shape[1] // nblk
    cb = SCAN_CB
    nt = rows // tt
    sp = _ssm_specs(nt, tt, nch, False)

    def body(a_ref, e_ref, u_ref, un_ref, b_ref, c_ref, s_ref, y_ref, init_ref, carry_ref, even_ref, odd_ref):
        kk = pl.program_id(1)

        def product(src_ref, dst_ref):
            dst_ref[...] = _dot(src_ref[...].astype(BF16), b_ref[...].astype(BF16), 1, 0)

        @pl.when(kk == 0)
        def _():
            _segment_entries(a_ref, e_ref, init_ref, reverse=False, seg_len=rows // SCAN_SEGS)
            carry_ref[...] = init_ref[...]
            product(u_ref, even_ref)

        def phase(cur_ref, next_ref):
            product(un_ref, next_ref)
            sr, si = _scan_groups(a_ref, cur_ref, s_ref, (carry_ref[:, :cb], carry_ref[:, cb:]), reverse=False, tt=tt,
                                  unroll=True)
            carry_ref[...] = jnp.concatenate([sr, si], axis=1)

        @pl.when(kk % 2 == 0)
        def _():
            phase(even_ref, odd_ref)

        @pl.when(kk % 2 == 1)
        def _():
            phase(odd_ref, even_ref)

        y_ref[...] = _dot(s_ref[...].astype(BF16), c_ref[...].astype(BF16), 1, 0)

    return pl.pallas_call(
        body, name=name, grid=(nblk, nt),
        in_specs=[sp["a"], sp["seg"], sp["chan"], sp["next"], sp["b"], sp["c"]],
        out_specs=[sp["state"], sp["chan"], sp["seg"]],
        out_shape=[jax.ShapeDtypeStruct((rows, nblk * 2 * cb), F32), jax.ShapeDtypeStruct((rows, nblk * nch), F32),
                   jax.ShapeDtypeStruct((SCAN_SEGS, nblk * 2 * cb), F32)],
        scratch_shapes=[pltpu.VMEM((SCAN_SEGS, 2 * cb), F32), pltpu.VMEM((tt, 2 * cb), F32), pltpu.VMEM((tt, 2 * cb), F32)],
        compiler_params=_params(("parallel", "arbitrary")),
    )(a_lay, ends, u, u, b_blk, c_blk)


def _ssm_bwd(a_conj, dy, u, s, s_entry, b_blk, c_blk, dd, ends, *, tt, name):
    rows = u.shape[0]
    nblk = b_blk.shape[0]
    nch = u.shape[1] // nblk
    cb = SCAN_CB
    nt = rows // tt
    sp = _ssm_specs(nt, tt, nch, True)
    groups_per_tile = tt // SCAN_SEGS
    before = pl.BlockSpec((SCAN_SEGS, 2 * cb), lambda j, kk: (jnp.maximum((nt - 1 - kk) * groups_per_tile - 1, 0), j))

    def body(a_ref, e_ref, dy_ref, dyn_ref, u_ref, s_ref, before_ref, entry_ref, b_ref, c_ref, dd_ref,
             du_ref, db_ref, dc_ref, da_ref, carry_ref, even_ref, odd_ref):
        kk = pl.program_id(1)

        def product(src_ref, dst_ref):
            dst_ref[...] = _dot(src_ref[...].astype(BF16), c_ref[...].astype(BF16), 1, 1)

        @pl.when(kk == 0)
        def _():
            _segment_entries(a_ref, e_ref, carry_ref, reverse=True, seg_len=rows // SCAN_SEGS)
            db_ref[...] = jnp.zeros_like(db_ref)
            dc_ref[...] = jnp.zeros_like(dc_ref)
            da_ref[...] = jnp.zeros_like(da_ref)
            product(dy_ref, even_ref)

        def pair(lv, pv):
            lre, lim, pre, pim = lv[:, :cb], lv[:, cb:], pv[:, :cb], pv[:, cb:]
            return (jnp.sum(lre * pre + lim * pim, axis=0, keepdims=True),
                    jnp.sum(lim * pre - lre * pim, axis=0, keepdims=True))

        def phase(lam_ref, next_ref):
            product(dyn_ref, next_ref)
            lr, li = _scan_groups(a_ref, lam_ref, lam_ref, (carry_ref[:, :cb], carry_ref[:, cb:]), reverse=True, tt=tt,
                                  unroll=True)
            carry_ref[...] = jnp.concatenate([lr, li], axis=1)
            first = jnp.where(kk == nt - 1, entry_ref[...], before_ref[...])
            rest = tt - SCAN_SEGS
            r1, i1 = pair(lam_ref[pl.ds(SCAN_SEGS, rest), :], s_ref[pl.ds(0, rest), :])
            r0, i0 = pair(lam_ref[pl.ds(0, SCAN_SEGS), :], first)
            da_ref[...] += jnp.concatenate([r1 + r0, i1 + i0], axis=1)
            dyv = dy_ref[...]
            lamb = lam_ref[...].astype(BF16)
            du_ref[...] = _dot(lamb, b_ref[...].astype(BF16), 1, 1) + dd_ref[...] * dyv
            db_ref[...] += _dot(u_ref[...].astype(BF16), lamb, 0, 0)
            dc_ref[...] += _dot(s_ref[...].astype(BF16), dyv.astype(BF16), 0, 0)

        @pl.when(kk % 2 == 0)
        def _():
            phase(even_ref, odd_ref)

        @pl.when(kk % 2 == 1)
        def _():
            phase(odd_ref, even_ref)

    return pl.pallas_call(
        body, name=name, grid=(nblk, nt),
        in_specs=[sp["a"], sp["seg"], sp["chan"], sp["next"], sp["chan"], sp["state"], before, sp["seg"], sp["b"], sp["c"],
                  pl.BlockSpec((1, nch), lambda j, kk: (0, j))],
        out_specs=[sp["chan"], sp["b"], sp["c"], pl.BlockSpec((1, 2 * cb), lambda j, kk: (0, j))],
        out_shape=[jax.ShapeDtypeStruct((rows, nblk * nch), F32), jax.ShapeDtypeStruct(b_blk.shape, F32),
                   jax.ShapeDtypeStruct(c_blk.shape, F32), jax.ShapeDtypeStruct((1, nblk * 2 * cb), F32)],
        scratch_shapes=[pltpu.VMEM((SCAN_SEGS, 2 * cb), F32), pltpu.VMEM((tt, 2 * cb), F32), pltpu.VMEM((tt, 2 * cb), F32)],
        compiler_params=_params(("parallel", "arbitrary")),
    )(a_conj, ends, dy, dy, u, s, s, s_entry, b_blk, c_blk, dd)


def _glu_fwd(ys, u, dd, w_glu, b_glu, *, tm, name):
    rows, w = ys.shape

    def body(ys_ref, u_ref, dd_ref, w_ref, b_ref, y0_ref, t_ref, y2_ref):
        y0 = ys_ref[...] + dd_ref[...] * u_ref[...]
        y1 = _gelu(y0)
        t = _dot(y1.astype(BF16), w_ref[...], 1, 0) + b_ref[...]
        y0_ref[...] = y0
        t_ref[...] = t
        y2_ref[...] = (y1 * _sigmoid(t)).astype(BF16)

    row = pl.BlockSpec((tm, w), lambda i: (i, 0))
    vec = pl.BlockSpec((1, w), lambda i: (0, 0))
    return pl.pallas_call(
        body, name=name, grid=(rows // tm,),
        in_specs=[row, row, vec, pl.BlockSpec((w, w), lambda i: (0, 0)), vec],
        out_specs=[row, row, row],
        out_shape=[jax.ShapeDtypeStruct((rows, w), F32), jax.ShapeDtypeStruct((rows, w), F32),
                   jax.ShapeDtypeStruct((rows, w), BF16)],
        compiler_params=_params(("parallel",)),
    )(ys, u, dd, w_glu, b_glu)


def _glu_bwd(dy2, y0, t, u, w_glu, *, tm, name):
    rows, w = y0.shape

    def body(dy2_ref, y0_ref, t_ref, u_ref, w_ref, dy0_ref, dt_ref, y1_ref, db_ref, dd_ref):
        i = pl.program_id(0)
        y0 = y0_ref[...]
        y1 = _gelu(y0)
        sg = _sigmoid(t_ref[...])
        dy2v = dy2_ref[...]
        dt = dy2v * y1 * sg * (1.0 - sg)
        dy1 = dy2v * sg + _dot(dt.astype(BF16), w_ref[...], 1, 1)
        dy0 = dy1 * _gelu_grad(y0)
        dy0_ref[...] = dy0
        dt_ref[...] = dt.astype(BF16)
        y1_ref[...] = y1.astype(BF16)

        @pl.when(i == 0)
        def _():
            db_ref[...] = jnp.zeros_like(db_ref)
            dd_ref[...] = jnp.zeros_like(dd_ref)

        db_ref[...] += jnp.sum(dt, axis=0, keepdims=True)
        dd_ref[...] += jnp.sum(dy0 * u_ref[...], axis=0, keepdims=True)

    row = pl.BlockSpec((tm, w), lambda i: (i, 0))
    vec = pl.BlockSpec((1, w), lambda i: (0, 0))
    return pl.pallas_call(
        body, name=name, grid=(rows // tm,),
        in_specs=[row, row, row, row, pl.BlockSpec((w, w), lambda i: (0, 0))],
        out_specs=[row, row, row, vec, vec],
        out_shape=[jax.ShapeDtypeStruct((rows, w), F32), jax.ShapeDtypeStruct((rows, w), BF16),
                   jax.ShapeDtypeStruct((rows, w), BF16), jax.ShapeDtypeStruct((1, w), F32),
                   jax.ShapeDtypeStruct((1, w), F32)],
        compiler_params=_params(("arbitrary",)),
    )(dy2, y0, t, u, w_glu)


ATTN_TILE = 2048


def _attn_geometry(rows, d):
    sb = ATTN_Q * d
    tr = max(sb, min(ATTN_TILE, rows))
    assert rows % tr == 0 and tr % sb == 0, (rows, d)
    return sb, tr, rows // tr, tr // sb


def _attn_masks():
    qi = lax.broadcasted_iota(jnp.int32, (2 * ATTN_Q, 2 * ATTN_Q), 0) % ATTN_Q
    kj = lax.broadcasted_iota(jnp.int32, (2 * ATTN_Q, 2 * ATTN_Q), 1)
    own_ok = jnp.logical_and(kj >= ATTN_Q, kj - ATTN_Q <= qi)
    prev_ok = jnp.logical_and(kj < ATTN_Q, kj >= qi)
    bias_first = jnp.where(own_ok, 0.0, NEG_INF)
    bias_other = jnp.where(jnp.logical_or(own_ok, prev_ok), 0.0, NEG_INF)
    head0 = lax.broadcasted_iota(jnp.int32, (ATTN_Q, LANES), 1) < ATTN_HEAD_DIM
    return bias_first, bias_other, head0


def _attn_rows(base, n, d):
    return pl.ds(pl.multiple_of(base, ATTN_Q), n) if d == 1 else pl.ds(base, n, stride=d)


def _stack_heads(v, head0):
    return jnp.concatenate([jnp.where(head0, v, 0.0), jnp.where(head0, 0.0, v)], axis=0)


def _unstack_heads(v, head0):
    return jnp.where(head0, v[:ATTN_Q], v[ATTN_Q:])


def _fill_keys(buf, prev_ref, cur_ref, sb):
    buf[pl.ds(0, sb), :] = prev_ref[...]
    buf[pl.ds(sb, cur_ref.shape[0]), :] = cur_ref[...]


def _attn_fwd(qkv, g, d, *, name):
    rows = qkv.shape[0]
    sb, tr, ntiles, nsub = _attn_geometry(rows, d)
    qc, kc, vc = 2 * g, 6 + 2 * g, 12 + 2 * g
    scale = ATTN_HEAD_DIM ** -0.5

    def body(q_ref, kc_ref, kp_ref, vc_ref, vp_ref, o_ref, lse_ref, kbuf, vbuf):
        n = pl.program_id(0)
        _fill_keys(kbuf, kp_ref, kc_ref, sb)
        _fill_keys(vbuf, vp_ref, vc_ref, sb)
        bias_first, bias_other, head0 = _attn_masks()

        def per_block(idx, carry):
            j, r = idx // d, idx % d
            base = j * sb + r
            bias = jnp.where(jnp.logical_and(n == 0, j == 0), bias_first, bias_other)
            qrows = _attn_rows(base, ATTN_Q, d)
            krows = _attn_rows(base, 2 * ATTN_Q, d)
            qs = (_stack_heads(q_ref[qrows, :], head0) * scale).astype(BF16)
            s = _dot(qs, kbuf[krows, :].astype(BF16), 1, 1) + bias
            mx = jnp.max(s, axis=-1, keepdims=True)
            p = jnp.exp(s - mx)
            den = jnp.sum(p, axis=-1, keepdims=True)
            pv = _dot(p.astype(BF16), vbuf[krows, :].astype(BF16), 1, 0) / den
            o_ref[qrows, :] = _unstack_heads(pv, head0)
            lse_ref[qrows, :] = _unstack_heads(jnp.broadcast_to(mx + jnp.log(den), (2 * ATTN_Q, LANES)), head0)
            return carry

        lax.fori_loop(0, nsub * d, per_block, 0, unroll=True)

    def cur(col):
        return pl.BlockSpec((tr, LANES), lambda n, hp: (n, col + hp))

    def prev(col):
        return pl.BlockSpec((sb, LANES), lambda n, hp: (jnp.maximum(n * nsub - 1, 0), col + hp))

    out_spec = pl.BlockSpec((tr, LANES), lambda n, hp: (n, hp))
    return pl.pallas_call(
        body, name=name, grid=(ntiles, 2),
        in_specs=[cur(qc), cur(kc), prev(kc), cur(vc), prev(vc)],
        out_specs=[out_spec, out_spec],
        out_shape=[jax.ShapeDtypeStruct((rows, 2 * LANES), F32), jax.ShapeDtypeStruct((rows, 2 * LANES), F32)],
        scratch_shapes=[pltpu.VMEM((sb + tr, LANES), F32), pltpu.VMEM((sb + tr, LANES), F32)],
        compiler_params=_params(("parallel", "parallel")),
    )(qkv, qkv, qkv, qkv, qkv)


def _attn_merge(outs, lses, *, tm, name):
    rows, w = outs[0].shape

    def body(o0, o1, o2, l0, l1, l2, o_ref, lse_ref):
        a0, a1, a2 = l0[...], l1[...], l2[...]
        mx = jnp.maximum(jnp.maximum(a0, a1), a2)
        e0, e1, e2 = jnp.exp(a0 - mx), jnp.exp(a1 - mx), jnp.exp(a2 - mx)
        den = e0 + e1 + e2
        o_ref[...] = (e0 / den) * o0[...] + (e1 / den) * o1[...] + (e2 / den) * o2[...]
        lse_ref[...] = mx + jnp.log(den)

    row = pl.BlockSpec((tm, w), lambda i: (i, 0))
    return pl.pallas_call(
        body, name=name, grid=(rows // tm,), in_specs=[row] * 6, out_specs=[row, row],
        out_shape=[jax.ShapeDtypeStruct((rows, w), F32), jax.ShapeDtypeStruct((rows, w), F32)],
        compiler_params=_params(("parallel",)),
    )(*outs, *lses)


def _attn_bwd(qkv, do, o, lse, g, d, prev, *, name):
    rows = qkv.shape[0]
    sb, tr, ntiles, nsub = _attn_geometry(rows, d)
    qc, kc, vc = 2 * g, 6 + 2 * g, 12 + 2 * g
    scale = ATTN_HEAD_DIM ** -0.5

    def body(q_ref, kc_ref, kp_ref, vc_ref, vp_ref, do_ref, o_ref, lse_ref, dq_ref, dk_ref, dv_ref,
             kbuf, vbuf, dk_acc, dv_acc):
        n = pl.program_id(1)

        @pl.when(n == 0)
        def _():
            dk_acc[pl.ds(0, tr), :] = jnp.zeros((tr, LANES), F32)
            dv_acc[pl.ds(0, tr), :] = jnp.zeros((tr, LANES), F32)

        @pl.when(n < ntiles)
        def _():
            dk_acc[pl.ds(tr, tr), :] = jnp.zeros((tr, LANES), F32)
            dv_acc[pl.ds(tr, tr), :] = jnp.zeros((tr, LANES), F32)
            _fill_keys(kbuf, kp_ref, kc_ref, sb)
            _fill_keys(vbuf, vp_ref, vc_ref, sb)
            bias_first, bias_other, head0 = _attn_masks()
            lane = lax.broadcasted_iota(jnp.int32, (ATTN_Q, LANES), 1)

            def per_block(idx, carry):
                j, r = idx // d, idx % d
                base = j * sb + r
                bias = jnp.where(jnp.logical_and(n == 0, j == 0), bias_first, bias_other)
                qrows = _attn_rows(base, ATTN_Q, d)
                krows = _attn_rows(base, 2 * ATTN_Q, d)
                arows = _attn_rows(base + (tr - sb), 2 * ATTN_Q, d)
                qs = (_stack_heads(q_ref[qrows, :], head0) * scale).astype(BF16)
                dos = _stack_heads(do_ref[qrows, :], head0)
                dosb = dos.astype(BF16)
                ov = o_ref[qrows, :]
                delta = jnp.sum(dos * jnp.concatenate([ov, ov], axis=0), axis=-1, keepdims=True)
                lsev = lse_ref[qrows, :]
                lse_s = jnp.concatenate(
                    [jnp.sum(jnp.where(lane == h * ATTN_HEAD_DIM, lsev, 0.0), axis=-1, keepdims=True) for h in range(2)], axis=0)
                kb = kbuf[krows, :].astype(BF16)
                vb = vbuf[krows, :].astype(BF16)
                p = jnp.exp(_dot(qs, kb, 1, 1) + bias - lse_s)
                ds = (p * (_dot(dosb, vb, 1, 1) - delta)).astype(BF16)
                dq_ref[qrows, :] = _unstack_heads(_dot(ds, kb, 1, 0), head0) * scale
                dk_acc[arows, :] += _dot(ds, qs, 0, 0)
                dv_acc[arows, :] += _dot(p.astype(BF16), dosb, 0, 0)
                return carry

            lax.fori_loop(0, nsub * d, per_block, 0, unroll=True)

        dk_ref[...] = dk_acc[pl.ds(0, tr), :]
        dv_ref[...] = dv_acc[pl.ds(0, tr), :]
        dk_acc[pl.ds(0, tr), :] = dk_acc[pl.ds(tr, tr), :]
        dv_acc[pl.ds(0, tr), :] = dv_acc[pl.ds(tr, tr), :]

    def cur(n):
        return jnp.minimum(n, ntiles - 1)

    def spec(col, prev):
        if prev:
            return pl.BlockSpec((sb, LANES), lambda hp, n: (jnp.maximum(cur(n) * nsub - 1, 0), col + hp))
        return pl.BlockSpec((tr, LANES), lambda hp, n: (cur(n), col + hp))

    row_spec = pl.BlockSpec((tr, LANES), lambda hp, n: (cur(n), hp))
    dq_out = pl.BlockSpec((tr, LANES), lambda hp, n: (cur(n), 2 * g + hp))
    kv_out = pl.BlockSpec((tr, LANES), lambda hp, n: (jnp.maximum(n - 1, 0), 2 * g + hp))
    shape = jax.ShapeDtypeStruct((rows, len(ATTN_PATTERNS) * 2 * LANES), F32)
    ins = [qkv, qkv, qkv, qkv, qkv, do, o, lse]
    in_specs = [spec(qc, False), spec(kc, False), spec(kc, True), spec(vc, False), spec(vc, True),
                row_spec, row_spec, row_spec]
    aliases = {}
    if prev is not None:
        aliases = {len(ins) + t: t for t in range(3)}
        ins = ins + list(prev)
        in_specs = in_specs + [ANY] * 3
    n_in = len(ins)

    def entry(*refs):
        body(*refs[:8], *refs[n_in:])

    return pl.pallas_call(
        entry, name=name, grid=(2, ntiles + 1),
        in_specs=in_specs,
        out_specs=[dq_out, kv_out, kv_out],
        out_shape=[shape, shape, shape],
        input_output_aliases=aliases,
        scratch_shapes=[pltpu.VMEM((sb + tr, LANES), F32), pltpu.VMEM((sb + tr, LANES), F32),
                        pltpu.VMEM((2 * tr, LANES), F32), pltpu.VMEM((2 * tr, LANES), F32)],
        compiler_params=_params(("parallel", "arbitrary")),
    )(*ins)


def _mem_probs(q, k):
    s = _dot(q.astype(BF16), k.astype(BF16), 1, 1) * (MEM_HEAD_DIM ** -0.5)
    e = jnp.exp(s - jnp.max(s, axis=-1, keepdims=True))
    return e / jnp.sum(e, axis=-1, keepdims=True)


def _mem_attn_fwd(mq, kv, *, tq, name):
    rows = mq.shape[0]

    def body(q_ref, k_ref, v_ref, o_ref):
        p = _mem_probs(q_ref[...], k_ref[...])
        o_ref[...] = _dot(p.astype(BF16), v_ref[...].astype(BF16), 1, 0)

    return pl.pallas_call(
        body, name=name, grid=(rows // tq, MEM_HEADS),
        in_specs=[pl.BlockSpec((tq, LANES), lambda i, h: (i, h)),
                  pl.BlockSpec((MEM_LEN, LANES), lambda i, h: (0, h)),
                  pl.BlockSpec((MEM_LEN, LANES), lambda i, h: (0, MEM_HEADS + h))],
        out_specs=pl.BlockSpec((tq, LANES), lambda i, h: (i, h)),
        out_shape=jax.ShapeDtypeStruct((rows, MEM_HEADS * LANES), F32),
        compiler_params=_params(("parallel", "parallel")),
    )(mq, kv, kv)


def _mem_attn_bwd(mq, kv, dmo, *, tq, name):
    rows = mq.shape[0]
    scale = MEM_HEAD_DIM ** -0.5

    def body(q_ref, k_ref, v_ref, do_ref, dq_ref, dk_ref, dv_ref):
        i = pl.program_id(1)
        qb = q_ref[...].astype(BF16)
        kb = k_ref[...].astype(BF16)
        vb = v_ref[...].astype(BF16)
        dob = do_ref[...].astype(BF16)
        p = _mem_probs(q_ref[...], k_ref[...])
        dp = _dot(dob, vb, 1, 1)
        ds = (p * (dp - jnp.sum(p * dp, axis=-1, keepdims=True)) * scale).astype(BF16)
        dq_ref[...] = _dot(ds, kb, 1, 0).astype(dq_ref.dtype)

        @pl.when(i == 0)
        def _():
            dk_ref[...] = jnp.zeros_like(dk_ref)
            dv_ref[...] = jnp.zeros_like(dv_ref)

        dk_ref[...] += _dot(ds, qb, 0, 0)
        dv_ref[...] += _dot(p.astype(BF16), dob, 0, 0)

    kv_out = pl.BlockSpec((MEM_LEN, LANES), lambda h, i: (0, h))
    kv_shape = jax.ShapeDtypeStruct((MEM_LEN, MEM_HEADS * LANES), F32)
    return pl.pallas_call(
        body, name=name, grid=(MEM_HEADS, rows // tq),
        in_specs=[pl.BlockSpec((tq, LANES), lambda h, i: (i, h)),
                  pl.BlockSpec((MEM_LEN, LANES), lambda h, i: (0, h)),
                  pl.BlockSpec((MEM_LEN, LANES), lambda h, i: (0, MEM_HEADS + h)),
                  pl.BlockSpec((tq, LANES), lambda h, i: (i, h))],
        out_specs=[pl.BlockSpec((tq, LANES), lambda h, i: (i, h)), kv_out, kv_out],
        out_shape=[jax.ShapeDtypeStruct((rows, MEM_HEADS * LANES), BF16), kv_shape, kv_shape],
        compiler_params=_params(("parallel", "arbitrary")),
    )(mq, kv, kv, dmo)


def _resident(shape):
    return pl.BlockSpec(shape, lambda i: (0, 0), pipeline_mode=pl.Buffered(1))


def _branch_merge_fwd(acts, wts, zg, b_gate, *, tm, name):
    rows = zg.shape[0]
    d = wts[0].shape[0]

    def body(s_ref, a_ref, m_ref, ws_ref, wa_ref, wm_ref, zg_ref, b_ref, o_ref):
        gt = _sigmoid(zg_ref[...] + b_ref[...])
        acc = None
        for k, (x_ref, w_ref) in enumerate(((s_ref, ws_ref), (a_ref, wa_ref), (m_ref, wm_ref))):
            term = gt[:, k * d:(k + 1) * d] * _dot(x_ref[...].astype(BF16), w_ref[...], 1, 1)
            acc = term if acc is None else acc + term
        o_ref[...] = acc.astype(BF16)

    return pl.pallas_call(
        body, name=name, grid=(rows // tm,),
        in_specs=[pl.BlockSpec((tm, x.shape[1]), lambda i: (i, 0)) for x in acts] + [_resident(w.shape) for w in wts]
        + [pl.BlockSpec((tm, 3 * d), lambda i: (i, 0)), pl.BlockSpec((1, 3 * d), lambda i: (0, 0))],
        out_specs=pl.BlockSpec((tm, d), lambda i: (i, 0)), out_shape=jax.ShapeDtypeStruct((rows, d), BF16),
        compiler_params=_params(("parallel",)),
    )(*acts, *wts, zg, b_gate)


def _branch_merge_bwd(dmerged, acts, wts, zg, b_gate, *, tm, name, carry=None):
    rows = zg.shape[0]
    d = wts[0].shape[0]

    def body(dm_ref, s_ref, a_ref, m_ref, ws_ref, wa_ref, wm_ref, zg_ref, b_ref,
             ds_ref, da_ref, dmm_ref, dws_ref, dwa_ref, dwm_ref, dzg_ref, db_ref):
        i = pl.program_id(0)

        @pl.when(i == 0)
        def _():
            for r in (dws_ref, dwa_ref, dwm_ref, db_ref):
                r[...] = jnp.zeros_like(r)

        gt = _sigmoid(zg_ref[...] + b_ref[...])
        dm = dm_ref[...]
        groups = ((s_ref, ws_ref, ds_ref, dws_ref), (a_ref, wa_ref, da_ref, dwa_ref), (m_ref, wm_ref, dmm_ref, dwm_ref))
        for k, (x_ref, w_ref, dx_ref, dw_ref) in enumerate(groups):
            cs = pl.ds(k * d, d)
            gk = gt[:, k * d:(k + 1) * d]
            xb = x_ref[...].astype(BF16)
            br = _dot(xb, w_ref[...], 1, 1)
            dbr = (dm * gk).astype(BF16)
            dx_ref[...] = _dot(dbr, w_ref[...], 1, 0)
            dw_ref[...] += _dot(dbr, xb, 0, 0)
            dzg = dm * br * gk * (1.0 - gk)
            dzg_ref[:, cs] = dzg.astype(BF16)
            db_ref[:, cs] += jnp.sum(dzg, axis=0, keepdims=True)

    row = lambda w: pl.BlockSpec((tm, w), lambda i: (i, 0))
    whole = lambda shape: pl.BlockSpec(shape, lambda i: (0, 0))
    res = _call_with_carry(
        body, carry, name=name, grid=(rows // tm,),
        in_specs=[row(d)] + [row(x.shape[1]) for x in acts] + [_resident(w.shape) for w in wts] + [row(3 * d), whole((1, 3 * d))],
        out_specs=[row(x.shape[1]) for x in acts] + [whole(w.shape) for w in wts] + [row(3 * d), whole((1, 3 * d))],
        out_shape=[jax.ShapeDtypeStruct(x.shape, F32) for x in acts] + [jax.ShapeDtypeStruct(w.shape, F32) for w in wts]
        + [jax.ShapeDtypeStruct((rows, 3 * d), BF16), jax.ShapeDtypeStruct((1, 3 * d), F32)],
        scratch=[], operands=[dmerged, *acts, *wts, zg, b_gate], semantics=("arbitrary",))
    return tuple(res) if carry is None else (tuple(res[:8]), list(res[8:]))


def _adamw(w, g, m, v, *, tr, name):
    rows, cols = w.shape[-2:]
    assert rows % tr == 0, (name, rows, tr)

    def body(w_ref, g_ref, m_ref, v_ref, g_out, d_ref, nm_ref, nv_ref):
        gv = g_ref[...]
        m2 = ADAM_B1 * m_ref[...] + (1.0 - ADAM_B1) * gv
        v2 = ADAM_B2 * v_ref[...] + (1.0 - ADAM_B2) * (gv * gv)
        m_hat = m2 / (1.0 - ADAM_B1 ** ADAM_STEP)
        v_hat = v2 / (1.0 - ADAM_B2 ** ADAM_STEP)
        g_out[...] = gv
        d_ref[...] = -ADAM_LR * (m_hat / (jnp.sqrt(v_hat) + ADAM_EPS) + ADAM_WD * w_ref[...])
        nm_ref[...] = m2
        nv_ref[...] = v2

    flat = pl.BlockSpec((tr, cols), lambda i: (i, 0))
    blk = flat if w.ndim == 2 else pl.BlockSpec((None, tr, cols), lambda i: (0, i, 0))
    shape = jax.ShapeDtypeStruct(w.shape, F32)
    return pl.pallas_call(
        body, name=name, grid=(rows // tr,), in_specs=[blk, flat, blk, blk], out_specs=[blk] * 4,
        out_shape=[shape] * 4, compiler_params=_params(("parallel",)),
    )(w, g, m, v)


ANY = pl.BlockSpec(memory_space=pl.ANY)


def _position():
    return lax.axis_index("x"), lax.axis_index("y"), lax.axis_index("c")


def _other_chips(x, y):
    return ((1 - x, y), (x, 1 - y), (1 - x, 1 - y))


def _remote(src, dst, send_sem, recv_sem, dev):
    return pltpu.make_async_remote_copy(src_ref=src, dst_ref=dst, send_sem=send_sem, recv_sem=recv_sem,
                                        device_id=dev, device_id_type=MESH)


def _gather_exchange(shards):
    nb = len(shards)

    def rows_of(i, owner, core):
        rs = shards[i].shape[0]
        return pl.ds(pl.multiple_of(owner * rs + core * (rs // 2), 16), rs // 2)

    def first_leg(ins, outs, send_sems, recv_sems, i, j):
        x, y, c = _position()
        px, py = _other_chips(x, y)[j]
        half = shards[i].shape[0] // 2
        mine = ins[i].at[pl.ds(pl.multiple_of(c * half, 16), half)]
        return _remote(mine, outs[i].at[rows_of(i, 2 * x + y, c)], send_sems.at[i, j], recv_sems.at[i, j], (px, py, c))

    def passed_on(outs, send_sems, recv_sems, i, j, core):
        x, y, c = _position()
        px, py = _other_chips(x, y)[j]
        rows = outs[i].at[rows_of(i, 2 * px + py, core)]
        return _remote(rows, rows, send_sems.at[i, 3 + j], recv_sems.at[i, 3 + j], (x, y, 1 - c))

    def own_block(ins, outs, send_sems, recv_sems, i):
        x, y, c = _position()
        rs = shards[i].shape[0]
        place = outs[i].at[pl.ds(pl.multiple_of((2 * x + y) * rs, 16), rs)]
        return _remote(ins[i], place, send_sems.at[i, 6], recv_sems.at[i, 6], (x, y, 1 - c))

    def start(ins, outs, send_sems, recv_sems):
        for i in range(nb):
            own_block(ins, outs, send_sems, recv_sems, i).start()
            for j in range(3):
                first_leg(ins, outs, send_sems, recv_sems, i, j).start()

    def finish(ins, outs, send_sems, recv_sems):
        x, y, c = _position()
        for i in range(nb):
            for j, (px, py) in enumerate(_other_chips(x, y)):
                landed = outs[i].at[rows_of(i, 2 * px + py, c)]
                _remote(landed, landed, send_sems.at[i, j], recv_sems.at[i, j], (px, py, c)).wait_recv()
                passed_on(outs, send_sems, recv_sems, i, j, c).start()
        for i in range(nb):
            own_block(ins, outs, send_sems, recv_sems, i).wait()
            for j in range(3):
                passed_on(outs, send_sems, recv_sems, i, j, 1 - c).wait_recv()
        for i in range(nb):
            for j in range(3):
                first_leg(ins, outs, send_sems, recv_sems, i, j).wait_send()
                passed_on(outs, send_sems, recv_sems, i, j, c).wait_send()

    return _Exchange(ins=list(shards), outs=[jax.ShapeDtypeStruct((N_CHIPS * s.shape[0], s.shape[1]), s.dtype) for s in shards],
                     aliases={}, sems=[(nb, 7), (nb, 7)], start=start, finish=finish)


def _run_exchange(ex, *, name):
    n_in, n_out = len(ex.ins), len(ex.outs)

    def body(*refs):
        c_in, c_out, sems = refs[:n_in], refs[n_in:n_in + n_out], refs[n_in + n_out:]
        ex.start(c_in, c_out, *sems)
        ex.finish(c_in, c_out, *sems)

    return pl.pallas_call(
        body, name=name, in_specs=[ANY] * n_in, out_specs=[ANY] * n_out, out_shape=list(ex.outs),
        input_output_aliases=dict(ex.aliases),
        scratch_shapes=[pltpu.SemaphoreType.DMA(s) for s in ex.sems],
    )(*ex.ins)


def _row_tile(rows):
    return max(t for t in range(16, min(rows, 512) + 1, 16) if rows % t == 0)


def _halves_exchange(grads):
    nb = len(grads)

    def copies(ins, outs, send_sems, recv_sems):
        x, y, c = _position()
        return [_remote(ins[i].at[:, 1 - c], outs[i], send_sems.at[i], recv_sems.at[i], (x, y, 1 - c)) for i in range(nb)]

    def start(ins, outs, send_sems, recv_sems):
        for cp in copies(ins, outs, send_sems, recv_sems):
            cp.start()

    def finish(ins, outs, send_sems, recv_sems):
        for cp in copies(ins, outs, send_sems, recv_sems):
            cp.wait()

    return _Exchange(ins=list(grads), outs=[jax.ShapeDtypeStruct((N_CHIPS, g.shape[2], g.shape[3]), F32) for g in grads],
                     aliases={}, sems=[(nb,), (nb,)], start=start, finish=finish)


def _join_exchanges(parts):
    assert all(not ex.aliases for ex in parts)

    def split(refs, counts):
        out, at = [], 0
        for k in counts:
            out.append(refs[at:at + k])
            at += k
        return out

    def run(which):
        def go(ins, outs, *sems):
            for ex, i, o, s in zip(parts, split(ins, [len(ex.ins) for ex in parts]), split(outs, [len(ex.outs) for ex in parts]),
                                   split(sems, [len(ex.sems) for ex in parts])):
                getattr(ex, which)(i, o, *s)
        return go

    return _Exchange(ins=[a for ex in parts for a in ex.ins], outs=[a for ex in parts for a in ex.outs], aliases={},
                     sems=[s for ex in parts for s in ex.sems], start=run("start"), finish=run("finish"))


def _pair_sum(g4, got, c_arr, *, name):
    _, _, half, cols = g4.shape
    tr = _row_tile(half)

    def body(c_ref, g_ref, t_ref, p_ref, pb_ref):
        sm = g_ref[...] + t_ref[...]
        p_ref[...] = sm
        pb_ref[...] = sm.astype(BF16)

    blk = pl.BlockSpec((None, tr, cols), lambda j, i, c_ref: (j, i, 0))
    grid_spec = pltpu.PrefetchScalarGridSpec(
        num_scalar_prefetch=1, grid=(N_CHIPS, half // tr),
        in_specs=[pl.BlockSpec((None, None, tr, cols), lambda j, i, c_ref: (j, c_ref[0], i, 0)), blk],
        out_specs=[blk, blk])
    return pl.pallas_call(
        body, name=name, grid_spec=grid_spec,
        out_shape=[jax.ShapeDtypeStruct((N_CHIPS, half, cols), F32), jax.ShapeDtypeStruct((N_CHIPS, half, cols), BF16)],
        compiler_params=_params(("parallel", "parallel")),
    )(c_arr, g4, got)


def _scatter_exchange(parts):
    nb = len(parts)

    def copies(ins, outs, send_sems, recv_sems):
        x, y, c = _position()
        return [_remote(ins[i].at[2 * px + py], outs[i].at[j], send_sems.at[i, j], recv_sems.at[i, j], (px, py, c))
                for i in range(nb) for j, (px, py) in enumerate(_other_chips(x, y))]

    def start(ins, outs, send_sems, recv_sems):
        for cp in copies(ins, outs, send_sems, recv_sems):
            cp.start()

    def finish(ins, outs, send_sems, recv_sems):
        for cp in copies(ins, outs, send_sems, recv_sems):
            cp.wait()

    return _Exchange(ins=list(parts), outs=[jax.ShapeDtypeStruct((3,) + p.shape[1:], p.dtype) for p in parts],
                     aliases={}, sems=[(nb, 3), (nb, 3)], start=start, finish=finish)


def _owner_sum(p, got, chip_arr, c_arr, *, replicated, name):
    _, half, cols = p.shape
    tr = _row_tile(half)

    def body(chip_ref, c_ref, p_ref, r_ref, o_ref):
        o_ref[...] = ((p_ref[...] + r_ref[0].astype(F32)) + r_ref[1].astype(F32)) + r_ref[2].astype(F32)

    if replicated:
        out_spec = pl.BlockSpec((None, None, tr, cols), lambda i, chip_ref, c_ref: (chip_ref[0], c_ref[0], i, 0))
        out_shape = jax.ShapeDtypeStruct((N_CHIPS, 2, half, cols), F32)
    else:
        out_spec = pl.BlockSpec((None, tr, cols), lambda i, chip_ref, c_ref: (c_ref[0], i, 0))
        out_shape = jax.ShapeDtypeStruct((2, half, cols), F32)
    grid_spec = pltpu.PrefetchScalarGridSpec(
        num_scalar_prefetch=2, grid=(half // tr,),
        in_specs=[pl.BlockSpec((None, tr, cols), lambda i, chip_ref, c_ref: (chip_ref[0], i, 0)),
                  pl.BlockSpec((3, tr, cols), lambda i, chip_ref, c_ref: (0, i, 0))],
        out_specs=out_spec)
    return pl.pallas_call(
        body, name=name, grid_spec=grid_spec, out_shape=out_shape,
        compiler_params=_params(("parallel",)),
    )(chip_arr, c_arr, p, got)


def _share_reduced(bufs):
    nb = len(bufs) - 1

    def body(*refs):
        outs = refs[nb + 1:2 * nb + 2]
        send_sems, recv_sems = refs[2 * nb + 2:]
        x, y, c = _position()
        chip = 2 * x + y
        sends = []
        for i in range(nb):
            cp = _remote(outs[i].at[c], outs[i].at[c], send_sems.at[i], recv_sems.at[i], (x, y, 1 - c))
            cp.start()
            sends.append(cp)
        small = outs[nb]
        peers = [(fx, fy, fc) for fx in (0, 1) for fy in (0, 1) for fc in (0, 1) if fx + fy + fc > 0]
        for k, (fx, fy, fc) in enumerate(peers):
            dev = (x ^ fx, y ^ fy, c ^ fc)
            cp = _remote(small.at[chip, c], small.at[chip, c], send_sems.at[nb + k], recv_sems.at[nb + k], dev)
            cp.start()
            sends.append(cp)
        for i in range(nb):
            dst = outs[i].at[1 - c]
            _remote(dst, dst, send_sems.at[i], recv_sems.at[i], (x, y, 1 - c)).wait_recv()
        for k, (fx, fy, fc) in enumerate(peers):
            dst = small.at[2 * (x ^ fx) + (y ^ fy), c ^ fc]
            _remote(dst, dst, send_sems.at[nb + k], recv_sems.at[nb + k], (x ^ fx, y ^ fy, c ^ fc)).wait_recv()
        for cp in sends:
            cp.wait_send()

    n_all = nb + 1
    return pl.pallas_call(
        body, name="grad_share_reduced", in_specs=[ANY] * n_all, out_specs=[ANY] * n_all,
        out_shape=[jax.ShapeDtypeStruct(b.shape, b.dtype) for b in bufs],
        input_output_aliases={i: i for i in range(n_all)},
        scratch_shapes=[pltpu.SemaphoreType.DMA((nb + 7,)), pltpu.SemaphoreType.DMA((nb + 7,))],
    )(*bufs)


class _GradReducer:
    def __init__(self, c_arr, chip_arr):
        self.c_arr, self.chip_arr = c_arr, chip_arr
        self.full, self.pairs, self.landed = {}, {}, {}

    def swap(self, names, grads):
        for n, g in zip(names, grads):
            self.full[n] = g.reshape(N_CHIPS, 2, g.shape[0] // (2 * N_CHIPS), g.shape[1])
        return _halves_exchange([self.full[n] for n in names])

    def swapped(self, names, bufs):
        for n, t in zip(names, bufs):
            self.pairs[n] = _pair_sum(self.full[n], t, self.c_arr, name="grad_pair_sum_" + n)

    def scatter(self, names):
        return _scatter_exchange([self.pairs[n][1] for n in names])

    def collect(self, names, bufs):
        self.landed.update(zip(names, bufs))

    def swap_now(self, names, grads):
        self.swapped(names, _run_exchange(self.swap(names, grads), name="grad_exchange_" + names[0]))

    def finish(self, names, grads, order):
        self.swap_now(names, grads)
        self.collect(names, _run_exchange(self.scatter(names), name="grad_scatter_" + names[0]))
        totals = [_owner_sum(self.pairs[n][0], self.landed[n], self.chip_arr, self.c_arr, replicated=(n == order[-1]),
                             name="grad_owner_sum_" + n) for n in order]
        return _share_reduced(totals)


def _pack_small(vals):
    flat = jnp.concatenate([vals[name].reshape(-1) for name, _ in SMALL])
    return jnp.pad(flat, (0, N_CHIPS * SMALL_ROWS * 1024 - SMALL_ELEMS)).reshape(N_CHIPS * SMALL_ROWS, 1024)


def _unpack_small(buf):
    flat = buf.reshape(-1)
    out, off = {}, 0
    for name, shape in SMALL:
        n = int(np.prod(shape))
        out[name] = flat[off:off + n].reshape(shape)
        off += n
    return out


EARLY_REDUCED = (("w_down",), ("w_up",), ("w_o", "w_ssm_br", "w_attn_br", "w_mem_br", "w_glu", "w_mem_kv"), ("w_in",))


def _device_step(x, mem, tgt, w, p, *, shards, reducer):
    rows = x.shape[0]
    w = dict(w)
    early = EARLY_REDUCED
    gb = {}
    gather_pending = shards is not None

    def riding(*stages):
        if reducer is None or not stages:
            return None
        return _join_exchanges([reducer.swap(names, [gb[n] for n in names]) if kind == "swap" else reducer.scatter(names)
                                for kind, names in stages])

    def arrived(stages, res):
        if reducer is None or not stages:
            return res
        main, bufs = res
        for kind, names in stages:
            (reducer.swapped if kind == "swap" else reducer.collect)(names, bufs[:len(names)])
            bufs = bufs[len(names):]
        return main

    def fetching(names):
        return _gather_exchange([shards[n] for n in names]) if gather_pending else None

    def fetched(names, res):
        if not gather_pending:
            return res
        w.update(zip(names, res[1]))
        return res[0]

    first_use = (("w_in",), ("w_glu", "w_ssm_br", "w_attn_br", "w_mem_kv", "w_mem_br", "w_o", "w_up"), ("w_down",))
    g1, gm, g2 = p["norm1_g"], p["mem_norm_g"], p["norm2_g"]
    gf = p["final_g"].reshape(1, D_MODEL)
    ssm_args = (p["ssm_lambda_re"][0], p["ssm_lambda_im"][0], p["ssm_log_dt"][0], p["ssm_b_re"][0],
                p["ssm_b_im"][0], p["ssm_c_re"][0], p["ssm_c_im"][0])
    (a_lay, b_blk, c_blk), ssm_vjp = jax.vjp(_ssm_matrices, *ssm_args)
    a_conj = a_lay * _to_scan_layout(jnp.stack([jnp.ones((N_STATES,), F32), -jnp.ones((N_STATES,), F32)]))[None, :]
    dd = p["ssm_d"].reshape(1, SSM_WIDTH)
    mm = _matmul

    n1 = fetched(first_use[0], _rmsnorm_fwd(x, g1, tm=512, carry=fetching(first_use[0]), name="norm1"))
    win_t = w["w_in"]
    splits = ((OFF_U, OFF_QKV - OFF_U), (OFF_QKV, OFF_MQ - OFF_QKV), (OFF_MQ, OFF_ZG - OFF_MQ), (OFF_ZG, IN_WIDTH - OFF_ZG))
    u, qkv, mq, zg = fetched(first_use[1], _split_matmul(n1, win_t, splits, tm=512, carry=fetching(first_use[1]),
                                                         vmem=VMEM_LIMIT_WIDE_BYTES, name="in_proj"))

    u_i = _interleave(u)
    ends = _ssm_ends(a_lay, u_i, b_blk, transpose=False, reverse=False, tt=512, name="ssm_fwd_ends")
    s, ys_i, s_entry = _ssm_fwd(a_lay, u_i, b_blk, c_blk, ends, tt=512, name="ssm_fwd")
    ys = _deinterleave(ys_i)
    y0, tglu, y2 = _glu_fwd(ys, u, dd, w["w_glu"], p["b_glu"], tm=512, name="glu_fwd")

    outs, lses = [], []
    for g, (_, d) in enumerate(ATTN_PATTERNS):
        o_g, lse_g = _attn_fwd(qkv, g, d, name=f"attn_fwd_{g}")
        outs.append(o_g)
        lses.append(lse_g)
    o, lse = _attn_merge(outs, lses, tm=1024, name="attn_merge")

    mn = _rmsnorm_fwd(mem, gm, tm=MEM_LEN, name="mem_norm")
    kv = mm(mn, w["w_mem_kv"], m=MEM_LEN, n=1024, k=1024, tm=MEM_LEN, tn=1024, tk=1024, out_dtypes=(F32,), name="mem_kv")
    mo = _mem_attn_fwd(mq, kv, tq=1024, name="mem_attn_fwd")

    branch_acts = (y2, o, mo)
    branch_wts = (w["w_ssm_br"], w["w_attn_br"], w["w_mem_br"])
    merged = _branch_merge_fwd(branch_acts, branch_wts, zg, p["b_gate"], tm=256, name="branch_merge_fwd")
    h1, n2 = mm(merged, w["w_o"], m=rows, n=1024, k=1024, tm=1024, tn=1024, tk=1024, out_dtypes=(F32, BF16),
                aux=((x, "mn"), (g2, "row")), epilogue=_residual_norm_epilogue, name="out_proj")
    relu2 = lambda acc: (jnp.square(jnp.maximum(acc, 0.0)),)
    act = fetched(first_use[2], _sum_matmul([n2], w["w_up"], [0], tb=True, tm=512, out_dtype=BF16, epilogue=relu2,
                                            carry=fetching(first_use[2]), name="mlp_up"))
    dh2, d_gf, sq_err = _sum_matmul([act], w["w_down"], [0], tm=512, aux=((h1, "mn"), (tgt, "mn"), (gf, "row")),
                                    epilogue=_loss_head_epilogue, n_sums=2, name="mlp_down")
    loss = (0.5 / D_MODEL) * jnp.sum(sq_err)

    gs = {"final_g": d_gf.reshape(D_MODEL)}
    drelu2 = lambda acc, actv: (acc * (2.0 * jnp.sqrt(actv.astype(F32))),)
    dup = mm(dh2, w["w_down"], m=rows, n=D_FF, k=1024, tb=True, tm=1024, tn=2048, tk=1024, out_dtypes=(BF16,),
             aux=((act, "mn"),), epilogue=drelu2, name="d_act")
    gb["w_down"] = mm(act, dh2, m=D_FF, n=1024, k=rows, ta=True, tm=1024, tn=1024, tk=2048, out_dtypes=(F32,), name="dw_down")
    stages = (("swap", early[0]),)
    gb["w_up"] = arrived(stages, mm(dup, n2, m=D_FF, n=1024, k=rows, ta=True, tm=1024, tn=1024, tk=2048,
                                    out_dtypes=(F32,), carry=riding(*stages), name="dw_up"))
    stages = (("scatter", early[0]), ("swap", early[1]))
    dh1, gs["norm2_g"] = arrived(stages, _sum_matmul([dup], w["w_up"], [0], tm=512, aux=((h1, "mn"), (dh2, "mn"), (g2, "row")),
                                                     epilogue=_rmsnorm_bwd_epilogue, n_sums=1, carry=riding(*stages), name="d_n2"))
    dmerged = mm(dh1, w["w_o"], m=rows, n=1024, k=1024, tb=True, tm=1024, tn=1024, tk=1024, out_dtypes=(F32,), name="d_merged")
    gb["w_o"] = mm(merged, dh1, m=1024, n=1024, k=rows, ta=True, tm=1024, tn=1024, tk=2048, out_dtypes=(F32,), name="dw_o")
    stages = (("scatter", early[1]),)
    (dy2, do, dmo, gb["w_ssm_br"], gb["w_attn_br"], gb["w_mem_br"], dzg, gs["b_gate"]) = arrived(stages, _branch_merge_bwd(
        dmerged, branch_acts, branch_wts, zg, p["b_gate"], tm=256, carry=riding(*stages), name="branch_merge_bwd"))

    dy0, dt, y1, gs["b_glu"], d_dd = _glu_bwd(dy2, y0, tglu, u, w["w_glu"], tm=512, name="glu_bwd")
    gs["ssm_d"] = d_dd.reshape(1, SSM_GROUPS, SSM_GROUP_SIZE)
    gb["w_glu"] = mm(y1, dt, m=512, n=512, k=rows, ta=True, tm=512, tn=512, tk=1024, out_dtypes=(F32,), name="dw_glu")
    dy0_i = _interleave(dy0)
    lam_ends = _ssm_ends(a_conj, dy0_i, c_blk, transpose=True, reverse=True, tt=512, name="ssm_bwd_ends")
    du_i, d_b_blk, d_c_blk, d_a_lay = _ssm_bwd(a_conj, dy0_i, u_i, s, s_entry, b_blk, c_blk, dd, lam_ends, tt=512,
                                                name="ssm_bwd")
    du = _deinterleave(du_i)
    d_ssm = ssm_vjp((d_a_lay, d_b_blk, d_c_blk))
    for name, val in zip(("ssm_lambda_re", "ssm_lambda_im", "ssm_log_dt", "ssm_b_re", "ssm_b_im", "ssm_c_re", "ssm_c_im"), d_ssm):
        gs[name] = val[None]

    dqkv = None
    for g, (_, d) in enumerate(ATTN_PATTERNS):
        dqkv = _attn_bwd(qkv, do, o, lse, g, d, dqkv, name=f"attn_bwd_{g}")

    dmq, dmk, dmv = _mem_attn_bwd(mq, kv, dmo, tq=1024, name="mem_attn_bwd")
    dkv = jnp.concatenate([dmk, dmv], axis=1)
    gb["w_mem_kv"] = mm(mn, dkv, m=1024, n=1024, k=MEM_LEN, ta=True, tm=1024, tn=1024, tk=MEM_LEN, out_dtypes=(F32,), name="dw_mem_kv")
    dmn = mm(dkv, w["w_mem_kv"], m=MEM_LEN, n=1024, k=1024, tb=True, tm=MEM_LEN, tn=1024, tk=1024, out_dtypes=(F32,), name="d_mn")
    _, gs["mem_norm_g"] = _rmsnorm_bwd(mem, gm, dmn, None, tm=MEM_LEN, name="mem_norm_bwd")

    pieces = ((du, OFF_U, "u"), (dqkv[0], OFF_QKV, "q"), (dqkv[1], OFF_QKV + 768, "k"), (dqkv[2], OFF_QKV + 1536, "v"),
              (dmq, OFF_MQ, "mq"), (dzg, OFF_ZG, "zg"))
    dw_rows = []
    for piece, off, tag in pieces:
        width = piece.shape[1]
        tmw = 1024 if width % 1024 == 0 else (768 if width == 768 else 512)
        stages = {"q": (("swap", early[2]),), "zg": (("scatter", early[2]),)}.get(tag, ())
        dw_rows.append(arrived(stages, mm(piece, n1, m=width, n=1024, k=rows, ta=True, tm=tmw, tn=1024, tk=2048,
                                          out_dtypes=(F32,), carry=riding(*stages), name="dw_in_" + tag)))
    gb["w_in"] = jnp.concatenate(dw_rows, axis=0)
    if reducer is not None:
        reducer.swap_now(early[3], [gb["w_in"]])
    stages = (("scatter", early[3]),)
    dx, gs["norm1_g"] = arrived(stages, _sum_matmul(
        [piece for piece, _, _ in pieces], win_t, [off for _, off, _ in pieces], tm=512,
        aux=((x, "mn"), (dh1, "mn"), (g1, "row")), epilogue=_rmsnorm_bwd_epilogue, n_sums=1,
        carry=riding(*stages), vmem=VMEM_LIMIT_WIDE_BYTES, name="d_n1"))
    return loss, dx, gb, gs


def kernel(x, mem, norm1_g, mem_norm_g, w_in, b_gate, ssm_lambda_re, ssm_lambda_im, ssm_log_dt, ssm_b_re, ssm_b_im, ssm_c_re, ssm_c_im, ssm_d, w_glu, b_glu, w_ssm_br, w_attn_br, w_mem_kv, w_mem_br, w_o, norm2_g, w_up, w_down, final_g, loss_target, m_norm1_g, m_mem_norm_g, m_w_in, m_b_gate, m_ssm_lambda_re, m_ssm_lambda_im, m_ssm_log_dt, m_ssm_b_re, m_ssm_b_im, m_ssm_c_re, m_ssm_c_im, m_ssm_d, m_w_glu, m_b_glu, m_w_ssm_br, m_w_attn_br, m_w_mem_kv, m_w_mem_br, m_w_o, m_norm2_g, m_w_up, m_w_down, m_final_g, v_norm1_g, v_mem_norm_g, v_w_in, v_b_gate, v_ssm_lambda_re, v_ssm_lambda_im, v_ssm_log_dt, v_ssm_b_re, v_ssm_b_im, v_ssm_c_re, v_ssm_c_im, v_ssm_d, v_w_glu, v_b_glu, v_w_ssm_br, v_w_attn_br, v_w_mem_kv, v_w_mem_br, v_w_o, v_norm2_g, v_w_up, v_w_down, v_final_g):
    env = dict(locals())
    weights = {n: env[n] for n in WEIGHT_ORDER}
    moms = {n: env["m_" + n] for n in WEIGHT_ORDER}
    vels = {n: env["v_" + n] for n in WEIGHT_ORDER}

    chip = 2 * lax.axis_index("x") + lax.axis_index("y")
    wire = [weights[n].reshape(weights[n].shape[-2:]).astype(BF16) for n, _, _ in BIG]
    wire = dict(zip([n for n, _, _ in BIG], [s.T if tr else s for s, (_, tr, _) in zip(wire, BIG)]))
    small = {n: weights[n] for n, _ in SMALL}

    reducer = _GradReducer(lax.axis_index("c").astype(jnp.int32).reshape(1), chip.astype(jnp.int32).reshape(1))
    loss, dx, gb, gs = _device_step(x[0], mem[0], loss_target[0], {}, small, shards=wire, reducer=reducer)
    *shards, small_grad = reducer.finish(["small"], [_pack_small(gs)], [n for n, _, _ in BIG] + ["small"])
    grads = {}
    for (n, tr, _), sh in zip(BIG, shards):
        sh = sh.reshape(2 * sh.shape[1], sh.shape[2])
        grads[n] = sh.T if tr else sh
    small_grad = small_grad.reshape(N_CHIPS * SMALL_ROWS, 1024)
    grads_small = _unpack_small(small_grad)

    delta, new_m, new_v = {}, {}, {}
    for n, _, _ in BIG:
        grads[n], delta[n], new_m[n], new_v[n] = _adamw(weights[n], grads[n], moms[n], vels[n],
                                                        tr=min(weights[n].shape[-2], 256), name="adamw_" + n)
    _, ds_, ms_, vs_ = _adamw(_pack_small(small), small_grad,
                              _pack_small({n: moms[n] for n, _ in SMALL}), _pack_small({n: vels[n] for n, _ in SMALL}),
                              tr=N_CHIPS * SMALL_ROWS, name="adamw_small")
    for dst, buf in ((delta, ds_), (new_m, ms_), (new_v, vs_)):
        dst.update(_unpack_small(buf))
    grads.update(grads_small)

    total_loss = lax.psum(loss, ("x", "y", "c"))
    return (total_loss, dx[None], *[grads[n] for n in WEIGHT_ORDER], *[delta[n] for n in WEIGHT_ORDER],
            *[new_m[n] for n in WEIGHT_ORDER], *[new_v[n] for n in WEIGHT_ORDER])
```

```python
import functools
import math

import numpy as np
import jax
import jax.numpy as jnp
from jax import lax
from jax.experimental import pallas as pl
from jax.experimental.pallas import tpu as pltpu

F32 = jnp.float32
BF16 = jnp.bfloat16

D_MODEL = 1024
SSM_GROUPS = 32
SSM_GROUP_SIZE = 16
SSM_STATE = 64
SSM_WIDTH = 512
N_STATES = SSM_GROUPS * SSM_STATE
SCAN_CB = 1024
ATTN_PATTERNS = ((128, 1), (512, 4), (2048, 16))
ATTN_HEAD_DIM = 64
ATTN_Q = 128
MEM_LEN = 256
MEM_HEAD_DIM = 128
MEM_HEADS = 4
D_FF = 4096
OFF_U, OFF_QKV, OFF_MQ, OFF_ZG = 0, 512, 2816, 3328
IN_WIDTH = 6400
RMS_EPS = 1e-6
NEG_INF = -1e30
ADAM_LR, ADAM_B1, ADAM_B2, ADAM_EPS, ADAM_WD, ADAM_STEP = 0.001, 0.9, 0.999, 1e-08, 0.01, 10

VMEM_LIMIT_BYTES = 48 * 1024 * 1024
VMEM_LIMIT_WIDE_BYTES = 56 * 1024 * 1024
LANES = 128
MESH = pl.DeviceIdType.MESH
N_CHIPS = 4

SCAN_SEGS = 8
SCAN_GROUPS = SCAN_CB // SSM_STATE

BIG = (("w_in", True, (6400, 1024)), ("w_glu", False, (512, 512)), ("w_ssm_br", True, (1024, 512)),
       ("w_attn_br", True, (1024, 256)), ("w_mem_kv", False, (1024, 1024)), ("w_mem_br", True, (1024, 512)),
       ("w_o", False, (1024, 1024)), ("w_up", True, (4096, 1024)), ("w_down", False, (4096, 1024)))
SMALL = (("norm1_g", (1, 1024)), ("mem_norm_g", (1, 1024)), ("b_gate", (1, 3072)),
         ("ssm_lambda_re", (1, 32, 64)), ("ssm_lambda_im", (1, 32, 64)), ("ssm_log_dt", (1, 32)),
         ("ssm_b_re", (1, 32, 64, 16)), ("ssm_b_im", (1, 32, 64, 16)), ("ssm_c_re", (1, 32, 16, 64)),
         ("ssm_c_im", (1, 32, 16, 64)), ("ssm_d", (1, 32, 16)), ("b_glu", (1, 512)),
         ("norm2_g", (1, 1024)), ("final_g", (1024,)))
WEIGHT_ORDER = ("norm1_g", "mem_norm_g", "w_in", "b_gate", "ssm_lambda_re", "ssm_lambda_im", "ssm_log_dt",
                "ssm_b_re", "ssm_b_im", "ssm_c_re", "ssm_c_im", "ssm_d", "w_glu", "b_glu", "w_ssm_br",
                "w_attn_br", "w_mem_kv", "w_mem_br", "w_o", "norm2_g", "w_up", "w_down", "final_g")
SMALL_ELEMS = sum(int(np.prod(s)) for _, s in SMALL)
SMALL_ROWS = 64


def _params(sem, vmem=VMEM_LIMIT_BYTES):
    return pltpu.CompilerParams(dimension_semantics=sem, vmem_limit_bytes=vmem)


def _sigmoid(v):
    return 0.5 * jnp.tanh(0.5 * v) + 0.5


_GELU_C = math.sqrt(2.0 / math.pi)


def _gelu(v):
    return 0.5 * v * (1.0 + jnp.tanh(_GELU_C * (v + 0.044715 * v * v * v)))


def _gelu_grad(v):
    th = jnp.tanh(_GELU_C * (v + 0.044715 * v * v * v))
    return 0.5 * (1.0 + th) + 0.5 * v * (1.0 - th * th) * _GELU_C * (1.0 + 3.0 * 0.044715 * v * v)


def _dot(a, b, ca, cb):
    return lax.dot_general(a, b, (((ca,), (cb,)), ((), ())), preferred_element_type=F32)


class _Exchange:
    def __init__(self, ins, outs, aliases, sems, start, finish):
        self.ins, self.outs, self.aliases, self.sems, self.start, self.finish = ins, outs, aliases, sems, start, finish


def _matmul(a, b, *, m, n, k, ta=False, tb=False, tm, tn, tk, out_dtypes, name,
            aux=(), epilogue=None, n_sums=0, carry=None):
    assert m % tm == 0 and n % tn == 0 and k % tk == 0, (name, m, n, k, tm, tn, tk)
    assert n_sums == 0 or tn == n, name
    nk = k // tk
    n_aux = len(aux)
    n_tiles = len(out_dtypes)
    n_out = n_tiles + n_sums
    a_spec = pl.BlockSpec((tk, tm), lambda i, j, kk: (kk, i)) if ta else pl.BlockSpec((tm, tk), lambda i, j, kk: (i, kk))
    b_spec = pl.BlockSpec((tn, tk), lambda i, j, kk: (j, kk)) if tb else pl.BlockSpec((tk, tn), lambda i, j, kk: (kk, j))
    aux_specs = []
    for _, kind in aux:
        if kind == "mn":
            aux_specs.append(pl.BlockSpec((tm, tn), lambda i, j, kk: (i, j)))
        else:
            aux_specs.append(pl.BlockSpec((1, tn), lambda i, j, kk: (0, j)))
    ca = 0 if ta else 1
    cb = 1 if tb else 0

    def finish(acc, aux_refs, out_refs, row_tile):
        outs = (acc,) if epilogue is None else epilogue(acc, *[r[...] for r in aux_refs])
        for o_ref, o in zip(out_refs[:n_tiles], outs[:n_tiles]):
            o_ref[...] = o.astype(o_ref.dtype)
        _accumulate_over_rows(out_refs[n_tiles:], outs[n_tiles:], row_tile)

    def body(a_ref, b_ref, *rest):
        aux_refs = rest[:n_aux]
        out_refs = rest[n_aux:n_aux + n_out]
        row_tile = pl.program_id(0)
        prod = _dot(a_ref[...].astype(BF16), b_ref[...].astype(BF16), ca, cb)
        if nk == 1:
            finish(prod, aux_refs, out_refs, row_tile)
            return
        acc_ref = rest[n_aux + n_out]
        kk = pl.program_id(2)

        @pl.when(kk == 0)
        def _():
            acc_ref[...] = prod

        @pl.when(jnp.logical_and(kk > 0, kk < nk - 1))
        def _():
            acc_ref[...] += prod

        @pl.when(kk == nk - 1)
        def _():
            finish(acc_ref[...] + prod, aux_refs, out_refs, row_tile)

    tile = pl.BlockSpec((tm, tn), lambda i, j, kk: (i, j))
    col_sum = pl.BlockSpec((1, tn), lambda i, j, kk: (0, j))
    res = _call_with_carry(
        body, carry, name=name, grid=(m // tm, n // tn, nk), in_specs=[a_spec, b_spec] + aux_specs,
        out_specs=[tile] * n_tiles + [col_sum] * n_sums,
        out_shape=[jax.ShapeDtypeStruct((m, n), dt) for dt in out_dtypes] + [jax.ShapeDtypeStruct((1, n), F32)] * n_sums,
        scratch=[pltpu.VMEM((tm, tn), F32)] if nk > 1 else [], operands=[a, b] + [x for x, _ in aux],
        semantics=("arbitrary" if n_sums else "parallel", "parallel", "arbitrary"))
    main = res[0] if n_out == 1 else tuple(res[:n_out])
    return main if carry is None else (main, list(res[n_out:]))


def _accumulate_over_rows(sum_refs, terms, row_tile):
    for s_ref, term in zip(sum_refs, terms):
        @pl.when(row_tile == 0)
        def _():
            s_ref[...] = term

        @pl.when(row_tile > 0)
        def _():
            s_ref[...] += term


def _call_with_carry(body, carry, *, name, grid, in_specs, out_specs, out_shape, scratch, operands, semantics,
                     vmem=VMEM_LIMIT_BYTES):
    if carry is None:
        return pl.pallas_call(body, name=name, grid=grid, in_specs=in_specs, out_specs=out_specs, out_shape=out_shape,
                              scratch_shapes=scratch, compiler_params=_params(semantics, vmem))(*operands)
    n_in, n_cin, n_out, n_cout, n_scr = len(operands), len(carry.ins), len(out_shape), len(carry.outs), len(scratch)

    def hosted(*refs):
        main_in, c_in = refs[:n_in], refs[n_in:n_in + n_cin]
        main_out = refs[n_in + n_cin:n_in + n_cin + n_out]
        c_out = refs[n_in + n_cin + n_out:n_in + n_cin + n_out + n_cout]
        rest = refs[n_in + n_cin + n_out + n_cout:]
        ids = [pl.program_id(t) for t in range(len(grid))]
        first = functools.reduce(jnp.logical_and, [i == 0 for i in ids])
        last = functools.reduce(jnp.logical_and, [i == g - 1 for i, g in zip(ids, grid)])

        @pl.when(first)
        def _():
            carry.start(c_in, c_out, *rest[n_scr:])

        body(*main_in, *main_out, *rest[:n_scr])

        @pl.when(last)
        def _():
            carry.finish(c_in, c_out, *rest[n_scr:])

    return pl.pallas_call(
        hosted, name=name, grid=grid,
        in_specs=list(in_specs) + [ANY] * n_cin, out_specs=list(out_specs) + [ANY] * n_cout,
        out_shape=list(out_shape) + list(carry.outs),
        input_output_aliases={n_in + i: n_out + o for i, o in carry.aliases.items()},
        scratch_shapes=list(scratch) + [pltpu.SemaphoreType.DMA(s) for s in carry.sems],
        compiler_params=_params(("arbitrary",) * len(grid), vmem),
    )(*operands, *carry.ins)


def _sum_matmul(pieces, b, offs, *, tm, name, tb=False, out_dtype=F32, aux=(), epilogue=None, n_sums=0, carry=None,
                vmem=VMEM_LIMIT_BYTES):
    m = pieces[0].shape[0]
    n = b.shape[0] if tb else b.shape[1]
    npieces, n_aux = len(pieces), len(aux)
    assert not tb or npieces == 1

    def body(*refs):
        b_ref = refs[npieces]
        aux_refs = refs[npieces + 1:npieces + 1 + n_aux]
        out_refs = refs[npieces + 1 + n_aux:]
        acc = None
        for p_ref, off in zip(refs[:npieces], offs):
            lhs = p_ref[...].astype(BF16)
            part = _dot(lhs, b_ref[...], 1, 1) if tb else _dot(lhs, b_ref[pl.ds(off, p_ref.shape[1]), :], 1, 0)
            acc = part if acc is None else acc + part
        outs = (acc,) if epilogue is None else epilogue(acc, *[r[...] for r in aux_refs])
        out_refs[0][...] = outs[0].astype(out_dtype)
        _accumulate_over_rows(out_refs[1:], outs[1:], pl.program_id(0))

    row = pl.BlockSpec((tm, n), lambda i: (i, 0))
    vec = pl.BlockSpec((1, n), lambda i: (0, 0))
    res = _call_with_carry(
        body, carry, name=name, grid=(m // tm,),
        in_specs=[pl.BlockSpec((tm, p.shape[1]), lambda i: (i, 0)) for p in pieces] + [_resident(b.shape)]
        + [row if kind == "mn" else vec for _, kind in aux],
        out_specs=[row] + [vec] * n_sums,
        out_shape=[jax.ShapeDtypeStruct((m, n), out_dtype)] + [jax.ShapeDtypeStruct((1, n), F32)] * n_sums,
        scratch=[], operands=list(pieces) + [b] + [x for x, _ in aux], semantics=("arbitrary" if n_sums else "parallel",),
        vmem=vmem)
    main = res[0] if n_sums == 0 else tuple(res[:1 + n_sums])
    return main if carry is None else (main, list(res[1 + n_sums:]))


def _split_matmul(a, b_t, splits, *, tm, name, carry=None, vmem=VMEM_LIMIT_BYTES):
    m, k = a.shape

    def body(a_ref, b_ref, *out_refs):
        av = a_ref[...].astype(BF16)
        for (row0, width), o_ref in zip(splits, out_refs):
            o_ref[...] = _dot(av, b_ref[pl.ds(row0, width), :], 1, 1)

    res = _call_with_carry(
        body, carry, name=name, grid=(m // tm,),
        in_specs=[pl.BlockSpec((tm, k), lambda i: (i, 0)), _resident(b_t.shape)],
        out_specs=[pl.BlockSpec((tm, width), lambda i: (i, 0)) for _, width in splits],
        out_shape=[jax.ShapeDtypeStruct((m, width), F32) for _, width in splits],
        scratch=[], operands=[a, b_t], semantics=("parallel",), vmem=vmem)
    outs = tuple(res[:len(splits)])
    return outs if carry is None else (outs, list(res[len(splits):]))


def _rmsnorm_fwd(x, g, *, tm, name, carry=None):
    rows, d = x.shape

    def body(x_ref, g_ref, o_ref):
        xv = x_ref[...]
        r = lax.rsqrt(jnp.mean(xv * xv, axis=-1, keepdims=True) + RMS_EPS)
        o_ref[...] = (xv * r * g_ref[...]).astype(o_ref.dtype)

    res = _call_with_carry(
        body, carry, name=name, grid=(rows // tm,),
        in_specs=[pl.BlockSpec((tm, d), lambda i: (i, 0)), pl.BlockSpec((1, d), lambda i: (0, 0))],
        out_specs=[pl.BlockSpec((tm, d), lambda i: (i, 0))], out_shape=[jax.ShapeDtypeStruct((rows, d), BF16)],
        scratch=[], operands=[x, g], semantics=("parallel",))
    return res[0] if carry is None else (res[0], list(res[1:]))


def _residual_norm_epilogue(acc, xv, gv):
    h = acc + xv
    r = lax.rsqrt(jnp.mean(h * h, axis=-1, keepdims=True) + RMS_EPS)
    return h, h * r * gv


def _rmsnorm_bwd_epilogue(dy, xv, resv, gv):
    r = lax.rsqrt(jnp.mean(xv * xv, axis=-1, keepdims=True) + RMS_EPS)
    xhat = xv * r
    dyg = dy * gv
    dx = r * (dyg - xhat * jnp.mean(dyg * xhat, axis=-1, keepdims=True)) + resv
    return dx, jnp.sum(dy * xhat, axis=0, keepdims=True)


def _rmsnorm_bwd(x, g, dy, res, *, tm, name):
    rows, d = x.shape
    has_res = res is not None

    def body(x_ref, g_ref, dy_ref, *rest):
        if has_res:
            res_ref, dx_ref, dg_ref = rest
        else:
            dx_ref, dg_ref = rest
        i = pl.program_id(0)
        xv = x_ref[...]
        r = lax.rsqrt(jnp.mean(xv * xv, axis=-1, keepdims=True) + RMS_EPS)
        xhat = xv * r
        dyv = dy_ref[...]
        dyg = dyv * g_ref[...]
        dx = r * (dyg - xhat * jnp.mean(dyg * xhat, axis=-1, keepdims=True))
        if has_res:
            dx = dx + res_ref[...]
        dx_ref[...] = dx

        @pl.when(i == 0)
        def _():
            dg_ref[...] = jnp.zeros_like(dg_ref)

        dg_ref[...] += jnp.sum(dyv * xhat, axis=0, keepdims=True)

    row_spec = pl.BlockSpec((tm, d), lambda i: (i, 0))
    vec_spec = pl.BlockSpec((1, d), lambda i: (0, 0))
    ins = [x, g, dy] + ([res] if has_res else [])
    return pl.pallas_call(
        body, name=name, grid=(rows // tm,),
        in_specs=[row_spec, vec_spec, row_spec] + ([row_spec] if has_res else []),
        out_specs=[row_spec, vec_spec],
        out_shape=[jax.ShapeDtypeStruct((rows, d), F32), jax.ShapeDtypeStruct((1, d), F32)],
        compiler_params=_params(("arbitrary",)),
    )(*ins)


def _loss_head_epilogue(acc, hv, tgtv, gv):
    xv = acc + hv
    r = lax.rsqrt(jnp.mean(xv * xv, axis=-1, keepdims=True) + RMS_EPS)
    xhat = xv * r
    err = xhat * gv - tgtv
    dyv = err * (1.0 / D_MODEL)
    dyg = dyv * gv
    dh = r * (dyg - xhat * jnp.mean(dyg * xhat, axis=-1, keepdims=True))
    return dh, jnp.sum(dyv * xhat, axis=0, keepdims=True), jnp.sum(err * err, axis=0, keepdims=True)


def _to_scan_layout(v):
    lead = v.shape[:-2]
    v = v.reshape(lead + (2, N_STATES // SCAN_CB, SCAN_CB))
    v = jnp.swapaxes(v, -3, -2)
    return v.reshape(lead + (2 * N_STATES,))


def _ssm_matrices(lam_re, lam_im, log_dt, b_re, b_im, c_re, c_im):
    dt = jnp.exp(log_dt)[:, None]
    mag = jnp.exp(lam_re * dt)
    a_re, a_im = mag * jnp.cos(lam_im * dt), mag * jnp.sin(lam_im * dt)
    nr, ni = a_re - 1.0, a_im
    den = lam_re * lam_re + lam_im * lam_im
    coef_re = (nr * lam_re + ni * lam_im) / den
    coef_im = (ni * lam_re - nr * lam_im) / den
    bb_re = coef_re[..., None] * b_re - coef_im[..., None] * b_im
    bb_im = coef_re[..., None] * b_im + coef_im[..., None] * b_re
    a_lay = _to_scan_layout(jnp.stack([a_re.reshape(-1), a_im.reshape(-1)], axis=0))[None, :]
    nblk = SSM_GROUPS // SCAN_GROUPS
    eye = jnp.eye(SCAN_GROUPS, dtype=F32)

    def b_block(bb):
        bb = bb.reshape(nblk, SCAN_GROUPS, SSM_STATE, SSM_GROUP_SIZE)
        return jnp.einsum("gk,jkph->jghkp", eye, bb).reshape(nblk, SCAN_GROUPS * SSM_GROUP_SIZE, SCAN_CB)

    b_blk = jnp.concatenate([b_block(bb_re), b_block(bb_im)], axis=2)

    def c_block(cc):
        cc = cc.reshape(nblk, SCAN_GROUPS, SSM_GROUP_SIZE, SSM_STATE)
        return jnp.einsum("gk,jghp->jkpgh", eye, cc).reshape(nblk, SCAN_CB, SCAN_GROUPS * SSM_GROUP_SIZE)

    c_blk = jnp.concatenate([c_block(c_re), -c_block(c_im)], axis=1)
    return a_lay, b_blk, c_blk


def _interleave(v):
    rows, c = v.shape
    return v.reshape(SCAN_SEGS, rows // SCAN_SEGS, c).transpose(1, 0, 2).reshape(rows, c)


def _deinterleave(v):
    rows, c = v.shape
    return v.reshape(rows // SCAN_SEGS, SCAN_SEGS, c).transpose(1, 0, 2).reshape(rows, c)


def _scan_groups(a_ref, bu_ref, o_ref, state, *, reverse, tt, unroll=4):
    cb = SCAN_CB
    ar = jnp.broadcast_to(a_ref[:, :cb], (SCAN_SEGS, cb))
    ai = jnp.broadcast_to(a_ref[:, cb:], (SCAN_SEGS, cb))
    ngroups = tt // SCAN_SEGS

    def step(i, st):
        sr, si = st
        r0 = pl.multiple_of(((ngroups - 1 - i) if reverse else i) * SCAN_SEGS, SCAN_SEGS)
        blk = bu_ref[pl.ds(r0, SCAN_SEGS), :]
        nr = ar * sr - ai * si + blk[:, :cb]
        ni = ar * si + ai * sr + blk[:, cb:]
        if o_ref is not None:
            o_ref[pl.ds(r0, SCAN_SEGS), :] = jnp.concatenate([nr, ni], axis=1)
        return nr, ni

    return lax.fori_loop(0, ngroups, step, state, unroll=unroll)


def _segment_entries(a_ref, e_ref, init_ref, *, reverse, seg_len):
    cb = SCAN_CB
    n_sq = seg_len.bit_length() - 1
    assert 1 << n_sq == seg_len, seg_len
    pr, pi = a_ref[:, :cb], a_ref[:, cb:]
    for _ in range(n_sq):
        pr, pi = pr * pr - pi * pi, 2.0 * pr * pi
    cr = jnp.zeros((1, cb), F32)
    ci = jnp.zeros((1, cb), F32)
    order = range(SCAN_SEGS - 1, -1, -1) if reverse else range(SCAN_SEGS)
    for k, seg in enumerate(order):
        if k > 0:
            prev = seg + 1 if reverse else seg - 1
            er, ei = e_ref[prev:prev + 1, :cb], e_ref[prev:prev + 1, cb:]
            cr, ci = pr * cr - pi * ci + er, pr * ci + pi * cr + ei
        init_ref[seg:seg + 1, :] = jnp.concatenate([cr, ci], axis=1)


def _ssm_specs(nt, tt, nch, reverse):
    cb = SCAN_CB
    tmap = (lambda j, kk: (nt - 1 - kk, j)) if reverse else (lambda j, kk: (kk, j))
    nmap = (lambda j, kk: (jnp.maximum(nt - 2 - kk, 0), j)) if reverse else (lambda j, kk: (jnp.minimum(kk + 1, nt - 1), j))
    return dict(a=pl.BlockSpec((1, 2 * cb), lambda j, kk: (0, j)),
                seg=pl.BlockSpec((SCAN_SEGS, 2 * cb), lambda j, kk: (0, j)),
                chan=pl.BlockSpec((tt, nch), tmap),
                next=pl.BlockSpec((tt, nch), nmap),
                state=pl.BlockSpec((tt, 2 * cb), tmap),
                b=pl.BlockSpec((None, nch, 2 * cb), lambda j, kk: (j, 0, 0)),
                c=pl.BlockSpec((None, 2 * cb, nch), lambda j, kk: (j, 0, 0)))


def _ssm_ends(a_lay, x, blocks, *, transpose, reverse, tt, name):
    rows = x.shape[0]
    nblk = blocks.shape[0]
    nch = x.shape[1] // nblk
    cb = SCAN_CB
    nt = rows // tt
    sp = _ssm_specs(nt, tt, nch, reverse)

    def body(a_ref, x_ref, xn_ref, w_ref, e_ref, even_ref, odd_ref):
        kk = pl.program_id(1)

        def product(src_ref, dst_ref):
            dst_ref[...] = _dot(src_ref[...].astype(BF16), w_ref[...].astype(BF16), 1, 1 if transpose else 0)

        @pl.when(kk == 0)
        def _():
            e_ref[...] = jnp.zeros_like(e_ref)
            product(x_ref, even_ref)

        def phase(cur_ref, next_ref):
            product(xn_ref, next_ref)
            sr, si = _scan_groups(a_ref, cur_ref, None, (e_ref[:, :cb], e_ref[:, cb:]), reverse=reverse, tt=tt, unroll=True)
            e_ref[...] = jnp.concatenate([sr, si], axis=1)

        @pl.when(kk % 2 == 0)
        def _():
            phase(even_ref, odd_ref)

        @pl.when(kk % 2 == 1)
        def _():
            phase(odd_ref, even_ref)

    return pl.pallas_call(
        body, name=name, grid=(nblk, nt),
        in_specs=[sp["a"], sp["chan"], sp["next"], sp["c"] if transpose else sp["b"]],
        out_specs=sp["seg"],
        out_shape=jax.ShapeDtypeStruct((SCAN_SEGS, nblk * 2 * cb), F32),
        scratch_shapes=[pltpu.VMEM((tt, 2 * cb), F32), pltpu.VMEM((tt, 2 * cb), F32)],
        compiler_params=_params(("parallel", "arbitrary")),
    )(a_lay, x, x, blocks)


def _ssm_fwd(a_lay, u, b_blk, c_blk, ends, *, tt, name):
    rows = u.shape[0]
    nblk = b_blk.shape[0]
    nch = u.shape[1] // nblk
    cb = SCAN_CB
    nt = rows // tt
    sp = _ssm_specs(nt, tt, nch, False)

    def body(a_ref, e_ref, u_ref, un_ref, b_ref, c_ref, s_ref, y_ref, init_ref, carry_ref, even_ref, odd_ref, sb_ref):
        kk = pl.program_id(1)

        def product(src_ref, dst_ref):
            dst_ref[...] = _dot(src_ref[...].astype(BF16), b_ref[...].astype(BF16), 1, 0)

        def output():
            y_ref[...] = _dot(sb_ref[...], c_ref[...].astype(BF16), 1, 0)

        @pl.when(kk == 0)
        def _():
            _segment_entries(a_ref, e_ref, init_ref, reverse=False, seg_len=rows // SCAN_SEGS)
            carry_ref[...] = init_ref[...]
            sb_ref[...] = jnp.zeros_like(sb_ref)
            product(u_ref, even_ref)

        def phase(cur_ref, next_ref):
            product(un_ref, next_ref)
            output()
            sr, si = _scan_groups(a_ref, cur_ref, s_ref, (carry_ref[:, :cb], carry_ref[:, cb:]), reverse=False, tt=tt,
                                  unroll=True)
            carry_ref[...] = jnp.concatenate([sr, si], axis=1)
            sb_ref[...] = s_ref[...].astype(BF16)

        @pl.when(jnp.logical_and(kk < nt, kk % 2 == 0))
        def _():
            phase(even_ref, odd_ref)

        @pl.when(jnp.logical_and(kk < nt, kk % 2 == 1))
        def _():
            phase(odd_ref, even_ref)

        @pl.when(kk == nt)
        def _():
            output()

    def clamped(spec_map):
        return lambda j, kk: spec_map(j, jnp.minimum(kk, nt - 1))

    tile = lambda width, imap: pl.BlockSpec((tt, width), imap)
    return pl.pallas_call(
        body, name=name, grid=(nblk, nt + 1),
        in_specs=[sp["a"], sp["seg"], tile(nch, clamped(lambda j, kk: (kk, j))), tile(nch, clamped(lambda j, kk: (jnp.minimum(kk + 1, nt - 1), j))),
                  sp["b"], sp["c"]],
        out_specs=[tile(2 * cb, clamped(lambda j, kk: (kk, j))), tile(nch, lambda j, kk: (jnp.maximum(kk - 1, 0), j)), sp["seg"]],
        out_shape=[jax.ShapeDtypeStruct((rows, nblk * 2 * cb), F32), jax.ShapeDtypeStruct((rows, nblk * nch), F32),
                   jax.ShapeDtypeStruct((SCAN_SEGS, nblk * 2 * cb), F32)],
        scratch_shapes=[pltpu.VMEM((SCAN_SEGS, 2 * cb), F32), pltpu.VMEM((tt, 2 * cb), F32), pltpu.VMEM((tt, 2 * cb), F32),
                        pltpu.VMEM((tt, 2 * cb), BF16)],
        compiler_params=_params(("parallel", "arbitrary")),
    )(a_lay, ends, u, u, b_blk, c_blk)


def _ssm_bwd(a_conj, dy, u, s, s_entry, b_blk, c_blk, dd, ends, *, tt, name):
    rows = u.shape[0]
    nblk = b_blk.shape[0]
    nch = u.shape[1] // nblk
    cb = SCAN_CB
    nt = rows // tt
    sp = _ssm_specs(nt, tt, nch, True)
    groups_per_tile = tt // SCAN_SEGS
    before = pl.BlockSpec((SCAN_SEGS, 2 * cb), lambda j, kk: (jnp.maximum((nt - 1 - kk) * groups_per_tile - 1, 0), j))

    def body(a_ref, e_ref, dy_ref, dyn_ref, u_ref, s_ref, before_ref, entry_ref, b_ref, c_ref, dd_ref,
             du_ref, db_ref, dc_ref, da_ref, carry_ref, even_ref, odd_ref):
        kk = pl.program_id(1)

        def product(src_ref, dst_ref):
            dst_ref[...] = _dot(src_ref[...].astype(BF16), c_ref[...].astype(BF16), 1, 1)

        @pl.when(kk == 0)
        def _():
            _segment_entries(a_ref, e_ref, carry_ref, reverse=True, seg_len=rows // SCAN_SEGS)
            db_ref[...] = jnp.zeros_like(db_ref)
            dc_ref[...] = jnp.zeros_like(dc_ref)
            da_ref[...] = jnp.zeros_like(da_ref)
            product(dy_ref, even_ref)

        def pair(lv, pv):
            lre, lim, pre, pim = lv[:, :cb], lv[:, cb:], pv[:, :cb], pv[:, cb:]
            return (jnp.sum(lre * pre + lim * pim, axis=0, keepdims=True),
                    jnp.sum(lim * pre - lre * pim, axis=0, keepdims=True))

        def phase(lam_ref, next_ref):
            product(dyn_ref, next_ref)
            lr, li = _scan_groups(a_ref, lam_ref, lam_ref, (carry_ref[:, :cb], carry_ref[:, cb:]), reverse=True, tt=tt,
                                  unroll=True)
            carry_ref[...] = jnp.concatenate([lr, li], axis=1)
            first = jnp.where(kk == nt - 1, entry_ref[...], before_ref[...])
            rest = tt - SCAN_SEGS
            r1, i1 = pair(lam_ref[pl.ds(SCAN_SEGS, rest), :], s_ref[pl.ds(0, rest), :])
            r0, i0 = pair(lam_ref[pl.ds(0, SCAN_SEGS), :], first)
            da_ref[...] += jnp.concatenate([r1 + r0, i1 + i0], axis=1)
            dyv = dy_ref[...]
            lamb = lam_ref[...].astype(BF16)
            du_ref[...] = _dot(lamb, b_ref[...].astype(BF16), 1, 1) + dd_ref[...] * dyv
            db_ref[...] += _dot(u_ref[...].astype(BF16), lamb, 0, 0)
            dc_ref[...] += _dot(s_ref[...].astype(BF16), dyv.astype(BF16), 0, 0)

        @pl.when(kk % 2 == 0)
        def _():
            phase(even_ref, odd_ref)

        @pl.when(kk % 2 == 1)
        def _():
            phase(odd_ref, even_ref)

    return pl.pallas_call(
        body, name=name, grid=(nblk, nt),
        in_specs=[sp["a"], sp["seg"], sp["chan"], sp["next"], sp["chan"], sp["state"], before, sp["seg"], sp["b"], sp["c"],
                  pl.BlockSpec((1, nch), lambda j, kk: (0, j))],
        out_specs=[sp["chan"], sp["b"], sp["c"], pl.BlockSpec((1, 2 * cb), lambda j, kk: (0, j))],
        out_shape=[jax.ShapeDtypeStruct((rows, nblk * nch), F32), jax.ShapeDtypeStruct(b_blk.shape, F32),
                   jax.ShapeDtypeStruct(c_blk.shape, F32), jax.ShapeDtypeStruct((1, nblk * 2 * cb), F32)],
        scratch_shapes=[pltpu.VMEM((SCAN_SEGS, 2 * cb), F32), pltpu.VMEM((tt, 2 * cb), F32), pltpu.VMEM((tt, 2 * cb), F32)],
        compiler_params=_params(("parallel", "arbitrary")),
    )(a_conj, ends, dy, dy, u, s, s, s_entry, b_blk, c_blk, dd)


def _glu_fwd(ys, u, dd, w_glu, b_glu, *, tm, name):
    rows, w = ys.shape

    def body(ys_ref, u_ref, dd_ref, w_ref, b_ref, y0_ref, t_ref, y2_ref):
        y0 = ys_ref[...] + dd_ref[...] * u_ref[...]
        y1 = _gelu(y0)
        t = _dot(y1.astype(BF16), w_ref[...], 1, 0) + b_ref[...]
        y0_ref[...] = y0
        t_ref[...] = t
        y2_ref[...] = (y1 * _sigmoid(t)).astype(BF16)

    row = pl.BlockSpec((tm, w), lambda i: (i, 0))
    vec = pl.BlockSpec((1, w), lambda i: (0, 0))
    return pl.pallas_call(
        body, name=name, grid=(rows // tm,),
        in_specs=[row, row, vec, pl.BlockSpec((w, w), lambda i: (0, 0)), vec],
        out_specs=[row, row, row],
        out_shape=[jax.ShapeDtypeStruct((rows, w), F32), jax.ShapeDtypeStruct((rows, w), F32),
                   jax.ShapeDtypeStruct((rows, w), BF16)],
        compiler_params=_params(("parallel",)),
    )(ys, u, dd, w_glu, b_glu)


def _glu_bwd(dy2, y0, t, u, w_glu, *, tm, name):
    rows, w = y0.shape

    def body(dy2_ref, y0_ref, t_ref, u_ref, w_ref, dy0_ref, dt_ref, y1_ref, db_ref, dd_ref):
        i = pl.program_id(0)
        y0 = y0_ref[...]
        y1 = _gelu(y0)
        sg = _sigmoid(t_ref[...])
        dy2v = dy2_ref[...]
        dt = dy2v * y1 * sg * (1.0 - sg)
        dy1 = dy2v * sg + _dot(dt.astype(BF16), w_ref[...], 1, 1)
        dy0 = dy1 * _gelu_grad(y0)
        dy0_ref[...] = dy0
        dt_ref[...] = dt.astype(BF16)
        y1_ref[...] = y1.astype(BF16)

        @pl.when(i == 0)
        def _():
            db_ref[...] = jnp.zeros_like(db_ref)
            dd_ref[...] = jnp.zeros_like(dd_ref)

        db_ref[...] += jnp.sum(dt, axis=0, keepdims=True)
        dd_ref[...] += jnp.sum(dy0 * u_ref[...], axis=0, keepdims=True)

    row = pl.BlockSpec((tm, w), lambda i: (i, 0))
    vec = pl.BlockSpec((1, w), lambda i: (0, 0))
    return pl.pallas_call(
        body, name=name, grid=(rows // tm,),
        in_specs=[row, row, row, row, pl.BlockSpec((w, w), lambda i: (0, 0))],
        out_specs=[row, row, row, vec, vec],
        out_shape=[jax.ShapeDtypeStruct((rows, w), F32), jax.ShapeDtypeStruct((rows, w), BF16),
                   jax.ShapeDtypeStruct((rows, w), BF16), jax.ShapeDtypeStruct((1, w), F32),
                   jax.ShapeDtypeStruct((1, w), F32)],
        compiler_params=_params(("arbitrary",)),
    )(dy2, y0, t, u, w_glu)


ATTN_TILE = 2048


def _attn_geometry(rows, d):
    sb = ATTN_Q * d
    tr = max(sb, min(ATTN_TILE, rows))
    assert rows % tr == 0 and tr % sb == 0, (rows, d)
    return sb, tr, rows // tr, tr // sb


def _attn_masks():
    qi = lax.broadcasted_iota(jnp.int32, (2 * ATTN_Q, 2 * ATTN_Q), 0) % ATTN_Q
    kj = lax.broadcasted_iota(jnp.int32, (2 * ATTN_Q, 2 * ATTN_Q), 1)
    own_ok = jnp.logical_and(kj >= ATTN_Q, kj - ATTN_Q <= qi)
    prev_ok = jnp.logical_and(kj < ATTN_Q, kj >= qi)
    bias_first = jnp.where(own_ok, 0.0, NEG_INF)
    bias_other = jnp.where(jnp.logical_or(own_ok, prev_ok), 0.0, NEG_INF)
    head0 = lax.broadcasted_iota(jnp.int32, (ATTN_Q, LANES), 1) < ATTN_HEAD_DIM
    return bias_first, bias_other, head0


def _attn_rows(base, n, d):
    return pl.ds(pl.multiple_of(base, ATTN_Q), n) if d == 1 else pl.ds(base, n, stride=d)


def _stack_heads(v, head0):
    return jnp.concatenate([jnp.where(head0, v, 0.0), jnp.where(head0, 0.0, v)], axis=0)


def _unstack_heads(v, head0):
    return jnp.where(head0, v[:ATTN_Q], v[ATTN_Q:])


def _fill_keys(buf, prev_ref, cur_ref, sb):
    buf[pl.ds(0, sb), :] = prev_ref[...]
    buf[pl.ds(sb, cur_ref.shape[0]), :] = cur_ref[...]


def _attn_fwd(qkv, g, d, *, name):
    rows = qkv.shape[0]
    sb, tr, ntiles, nsub = _attn_geometry(rows, d)
    qc, kc, vc = 2 * g, 6 + 2 * g, 12 + 2 * g
    scale = ATTN_HEAD_DIM ** -0.5

    def body(q_ref, kc_ref, kp_ref, vc_ref, vp_ref, o_ref, lse_ref, kbuf, vbuf):
        n = pl.program_id(0)
        _fill_keys(kbuf, kp_ref, kc_ref, sb)
        _fill_keys(vbuf, vp_ref, vc_ref, sb)
        bias_first, bias_other, head0 = _attn_masks()

        def per_block(idx, carry):
            j, r = idx // d, idx % d
            base = j * sb + r
            bias = jnp.where(jnp.logical_and(n == 0, j == 0), bias_first, bias_other)
            qrows = _attn_rows(base, ATTN_Q, d)
            krows = _attn_rows(base, 2 * ATTN_Q, d)
            qs = (_stack_heads(q_ref[qrows, :], head0) * scale).astype(BF16)
            s = _dot(qs, kbuf[krows, :].astype(BF16), 1, 1) + bias
            mx = jnp.max(s, axis=-1, keepdims=True)
            p = jnp.exp(s - mx)
            den = jnp.sum(p, axis=-1, keepdims=True)
            pv = _dot(p.astype(BF16), vbuf[krows, :].astype(BF16), 1, 0) / den
            o_ref[qrows, :] = _unstack_heads(pv, head0)
            lse_ref[qrows, :] = _unstack_heads(jnp.broadcast_to(mx + jnp.log(den), (2 * ATTN_Q, LANES)), head0)
            return carry

        lax.fori_loop(0, nsub * d, per_block, 0, unroll=True)

    def cur(col):
        return pl.BlockSpec((tr, LANES), lambda n, hp: (n, col + hp))

    def prev(col):
        return pl.BlockSpec((sb, LANES), lambda n, hp: (jnp.maximum(n * nsub - 1, 0), col + hp))

    out_spec = pl.BlockSpec((tr, LANES), lambda n, hp: (n, hp))
    return pl.pallas_call(
        body, name=name, grid=(ntiles, 2),
        in_specs=[cur(qc), cur(kc), prev(kc), cur(vc), prev(vc)],
        out_specs=[out_spec, out_spec],
        out_shape=[jax.ShapeDtypeStruct((rows, 2 * LANES), F32), jax.ShapeDtypeStruct((rows, 2 * LANES), F32)],
        scratch_shapes=[pltpu.VMEM((sb + tr, LANES), F32), pltpu.VMEM((sb + tr, LANES), F32)],
        compiler_params=_params(("parallel", "parallel")),
    )(qkv, qkv, qkv, qkv, qkv)


def _attn_merge(outs, lses, *, tm, name):
    rows, w = outs[0].shape

    def body(o0, o1, o2, l0, l1, l2, o_ref, lse_ref):
        a0, a1, a2 = l0[...], l1[...], l2[...]
        mx = jnp.maximum(jnp.maximum(a0, a1), a2)
        e0, e1, e2 = jnp.exp(a0 - mx), jnp.exp(a1 - mx), jnp.exp(a2 - mx)
        den = e0 + e1 + e2
        o_ref[...] = (e0 / den) * o0[...] + (e1 / den) * o1[...] + (e2 / den) * o2[...]
        lse_ref[...] = mx + jnp.log(den)

    row = pl.BlockSpec((tm, w), lambda i: (i, 0))
    return pl.pallas_call(
        body, name=name, grid=(rows // tm,), in_specs=[row] * 6, out_specs=[row, row],
        out_shape=[jax.ShapeDtypeStruct((rows, w), F32), jax.ShapeDtypeStruct((rows, w), F32)],
        compiler_params=_params(("parallel",)),
    )(*outs, *lses)


def _attn_bwd(qkv, do, o, lse, g, d, prev, *, name):
    rows = qkv.shape[0]
    sb, tr, ntiles, nsub = _attn_geometry(rows, d)
    qc, kc, vc = 2 * g, 6 + 2 * g, 12 + 2 * g
    scale = ATTN_HEAD_DIM ** -0.5

    def body(q_ref, kc_ref, kp_ref, vc_ref, vp_ref, do_ref, o_ref, lse_ref, dq_ref, dk_ref, dv_ref,
             kbuf, vbuf, dk_acc, dv_acc):
        n = pl.program_id(1)

        @pl.when(n == 0)
        def _():
            dk_acc[pl.ds(0, tr), :] = jnp.zeros((tr, LANES), F32)
            dv_acc[pl.ds(0, tr), :] = jnp.zeros((tr, LANES), F32)

        @pl.when(n < ntiles)
        def _():
            dk_acc[pl.ds(tr, tr), :] = jnp.zeros((tr, LANES), F32)
            dv_acc[pl.ds(tr, tr), :] = jnp.zeros((tr, LANES), F32)
            _fill_keys(kbuf, kp_ref, kc_ref, sb)
            _fill_keys(vbuf, vp_ref, vc_ref, sb)
            bias_first, bias_other, head0 = _attn_masks()
            lane = lax.broadcasted_iota(jnp.int32, (ATTN_Q, LANES), 1)

            def per_block(idx, carry):
                j, r = idx // d, idx % d
                base = j * sb + r
                bias = jnp.where(jnp.logical_and(n == 0, j == 0), bias_first, bias_other)
                qrows = _attn_rows(base, ATTN_Q, d)
                krows = _attn_rows(base, 2 * ATTN_Q, d)
                arows = _attn_rows(base + (tr - sb), 2 * ATTN_Q, d)
                qs = (_stack_heads(q_ref[qrows, :], head0) * scale).astype(BF16)
                dos = _stack_heads(do_ref[qrows, :], head0)
                dosb = dos.astype(BF16)
                ov = o_ref[qrows, :]
                delta = jnp.sum(dos * jnp.concatenate([ov, ov], axis=0), axis=-1, keepdims=True)
                lsev = lse_ref[qrows, :]
                lse_s = jnp.concatenate(
                    [jnp.sum(jnp.where(lane == h * ATTN_HEAD_DIM, lsev, 0.0), axis=-1, keepdims=True) for h in range(2)], axis=0)
                kb = kbuf[krows, :].astype(BF16)
                vb = vbuf[krows, :].astype(BF16)
                p = jnp.exp(_dot(qs, kb, 1, 1) + bias - lse_s)
                ds = (p * (_dot(dosb, vb, 1, 1) - delta)).astype(BF16)
                dq_ref[qrows, :] = _unstack_heads(_dot(ds, kb, 1, 0), head0) * scale
                dk_acc[arows, :] += _dot(ds, qs, 0, 0)
                dv_acc[arows, :] += _dot(p.astype(BF16), dosb, 0, 0)
                return carry

            lax.fori_loop(0, nsub * d, per_block, 0, unroll=True)

        dk_ref[...] = dk_acc[pl.ds(0, tr), :]
        dv_ref[...] = dv_acc[pl.ds(0, tr), :]
        dk_acc[pl.ds(0, tr), :] = dk_acc[pl.ds(tr, tr), :]
        dv_acc[pl.ds(0, tr), :] = dv_acc[pl.ds(tr, tr), :]

    def cur(n):
        return jnp.minimum(n, ntiles - 1)

    def spec(col, prev):
        if prev:
            return pl.BlockSpec((sb, LANES), lambda hp, n: (jnp.maximum(cur(n) * nsub - 1, 0), col + hp))
        return pl.BlockSpec((tr, LANES), lambda hp, n: (cur(n), col + hp))

    row_spec = pl.BlockSpec((tr, LANES), lambda hp, n: (cur(n), hp))
    dq_out = pl.BlockSpec((tr, LANES), lambda hp, n: (cur(n), 2 * g + hp))
    kv_out = pl.BlockSpec((tr, LANES), lambda hp, n: (jnp.maximum(n - 1, 0), 2 * g + hp))
    shape = jax.ShapeDtypeStruct((rows, len(ATTN_PATTERNS) * 2 * LANES), F32)
    ins = [qkv, qkv, qkv, qkv, qkv, do, o, lse]
    in_specs = [spec(qc, False), spec(kc, False), spec(kc, True), spec(vc, False), spec(vc, True),
                row_spec, row_spec, row_spec]
    aliases = {}
    if prev is not None:
        aliases = {len(ins) + t: t for t in range(3)}
        ins = ins + list(prev)
        in_specs = in_specs + [ANY] * 3
    n_in = len(ins)

    def entry(*refs):
        body(*refs[:8], *refs[n_in:])

    return pl.pallas_call(
        entry, name=name, grid=(2, ntiles + 1),
        in_specs=in_specs,
        out_specs=[dq_out, kv_out, kv_out],
        out_shape=[shape, shape, shape],
        input_output_aliases=aliases,
        scratch_shapes=[pltpu.VMEM((sb + tr, LANES), F32), pltpu.VMEM((sb + tr, LANES), F32),
                        pltpu.VMEM((2 * tr, LANES), F32), pltpu.VMEM((2 * tr, LANES), F32)],
        compiler_params=_params(("parallel", "arbitrary")),
    )(*ins)


def _mem_probs(q, k):
    s = _dot(q.astype(BF16), k.astype(BF16), 1, 1) * (MEM_HEAD_DIM ** -0.5)
    e = jnp.exp(s - jnp.max(s, axis=-1, keepdims=True))
    return e / jnp.sum(e, axis=-1, keepdims=True)


def _mem_attn_fwd(mq, kv, *, tq, name):
    rows = mq.shape[0]

    def body(q_ref, k_ref, v_ref, o_ref):
        p = _mem_probs(q_ref[...], k_ref[...])
        o_ref[...] = _dot(p.astype(BF16), v_ref[...].astype(BF16), 1, 0)

    return pl.pallas_call(
        body, name=name, grid=(rows // tq, MEM_HEADS),
        in_specs=[pl.BlockSpec((tq, LANES), lambda i, h: (i, h)),
                  pl.BlockSpec((MEM_LEN, LANES), lambda i, h: (0, h)),
                  pl.BlockSpec((MEM_LEN, LANES), lambda i, h: (0, MEM_HEADS + h))],
        out_specs=pl.BlockSpec((tq, LANES), lambda i, h: (i, h)),
        out_shape=jax.ShapeDtypeStruct((rows, MEM_HEADS * LANES), F32),
        compiler_params=_params(("parallel", "parallel")),
    )(mq, kv, kv)


def _mem_attn_bwd(mq, kv, dmo, *, tq, name):
    rows = mq.shape[0]
    scale = MEM_HEAD_DIM ** -0.5

    def body(q_ref, k_ref, v_ref, do_ref, dq_ref, dk_ref, dv_ref):
        i = pl.program_id(1)
        qb = q_ref[...].astype(BF16)
        kb = k_ref[...].astype(BF16)
        vb = v_ref[...].astype(BF16)
        dob = do_ref[...].astype(BF16)
        p = _mem_probs(q_ref[...], k_ref[...])
        dp = _dot(dob, vb, 1, 1)
        ds = (p * (dp - jnp.sum(p * dp, axis=-1, keepdims=True)) * scale).astype(BF16)
        dq_ref[...] = _dot(ds, kb, 1, 0).astype(dq_ref.dtype)

        @pl.when(i == 0)
        def _():
            dk_ref[...] = jnp.zeros_like(dk_ref)
            dv_ref[...] = jnp.zeros_like(dv_ref)

        dk_ref[...] += _dot(ds, qb, 0, 0)
        dv_ref[...] += _dot(p.astype(BF16), dob, 0, 0)

    kv_out = pl.BlockSpec((MEM_LEN, LANES), lambda h, i: (0, h))
    kv_shape = jax.ShapeDtypeStruct((MEM_LEN, MEM_HEADS * LANES), F32)
    return pl.pallas_call(
        body, name=name, grid=(MEM_HEADS, rows // tq),
        in_specs=[pl.BlockSpec((tq, LANES), lambda h, i: (i, h)),
                  pl.BlockSpec((MEM_LEN, LANES), lambda h, i: (0, h)),
                  pl.BlockSpec((MEM_LEN, LANES), lambda h, i: (0, MEM_HEADS + h)),
                  pl.BlockSpec((tq, LANES), lambda h, i: (i, h))],
        out_specs=[pl.BlockSpec((tq, LANES), lambda h, i: (i, h)), kv_out, kv_out],
        out_shape=[jax.ShapeDtypeStruct((rows, MEM_HEADS * LANES), BF16), kv_shape, kv_shape],
        compiler_params=_params(("parallel", "arbitrary")),
    )(mq, kv, kv, dmo)


def _resident(shape):
    return pl.BlockSpec(shape, lambda i: (0, 0), pipeline_mode=pl.Buffered(1))


def _branch_merge_fwd(acts, wts, zg, b_gate, *, tm, name):
    rows = zg.shape[0]
    d = wts[0].shape[0]

    def body(s_ref, a_ref, m_ref, ws_ref, wa_ref, wm_ref, zg_ref, b_ref, o_ref):
        gt = _sigmoid(zg_ref[...] + b_ref[...])
        acc = None
        for k, (x_ref, w_ref) in enumerate(((s_ref, ws_ref), (a_ref, wa_ref), (m_ref, wm_ref))):
            term = gt[:, k * d:(k + 1) * d] * _dot(x_ref[...].astype(BF16), w_ref[...], 1, 1)
            acc = term if acc is None else acc + term
        o_ref[...] = acc.astype(BF16)

    return pl.pallas_call(
        body, name=name, grid=(rows // tm,),
        in_specs=[pl.BlockSpec((tm, x.shape[1]), lambda i: (i, 0)) for x in acts] + [_resident(w.shape) for w in wts]
        + [pl.BlockSpec((tm, 3 * d), lambda i: (i, 0)), pl.BlockSpec((1, 3 * d), lambda i: (0, 0))],
        out_specs=pl.BlockSpec((tm, d), lambda i: (i, 0)), out_shape=jax.ShapeDtypeStruct((rows, d), BF16),
        compiler_params=_params(("parallel",)),
    )(*acts, *wts, zg, b_gate)


def _branch_merge_bwd(dmerged, acts, wts, zg, b_gate, *, tm, name, carry=None):
    rows = zg.shape[0]
    d = wts[0].shape[0]

    def body(dm_ref, s_ref, a_ref, m_ref, ws_ref, wa_ref, wm_ref, zg_ref, b_ref,
             ds_ref, da_ref, dmm_ref, dws_ref, dwa_ref, dwm_ref, dzg_ref, db_ref):
        i = pl.program_id(0)

        @pl.when(i == 0)
        def _():
            for r in (dws_ref, dwa_ref, dwm_ref, db_ref):
                r[...] = jnp.zeros_like(r)

        gt = _sigmoid(zg_ref[...] + b_ref[...])
        dm = dm_ref[...]
        groups = ((s_ref, ws_ref, ds_ref, dws_ref), (a_ref, wa_ref, da_ref, dwa_ref), (m_ref, wm_ref, dmm_ref, dwm_ref))
        for k, (x_ref, w_ref, dx_ref, dw_ref) in enumerate(groups):
            cs = pl.ds(k * d, d)
            gk = gt[:, k * d:(k + 1) * d]
            xb = x_ref[...].astype(BF16)
            br = _dot(xb, w_ref[...], 1, 1)
            dbr = (dm * gk).astype(BF16)
            dx_ref[...] = _dot(dbr, w_ref[...], 1, 0)
            dw_ref[...] += _dot(dbr, xb, 0, 0)
            dzg = dm * br * gk * (1.0 - gk)
            dzg_ref[:, cs] = dzg.astype(BF16)
            db_ref[:, cs] += jnp.sum(dzg, axis=0, keepdims=True)

    row = lambda w: pl.BlockSpec((tm, w), lambda i: (i, 0))
    whole = lambda shape: pl.BlockSpec(shape, lambda i: (0, 0))
    res = _call_with_carry(
        body, carry, name=name, grid=(rows // tm,),
        in_specs=[row(d)] + [row(x.shape[1]) for x in acts] + [_resident(w.shape) for w in wts] + [row(3 * d), whole((1, 3 * d))],
        out_specs=[row(x.shape[1]) for x in acts] + [whole(w.shape) for w in wts] + [row(3 * d), whole((1, 3 * d))],
        out_shape=[jax.ShapeDtypeStruct(x.shape, F32) for x in acts] + [jax.ShapeDtypeStruct(w.shape, F32) for w in wts]
        + [jax.ShapeDtypeStruct((rows, 3 * d), BF16), jax.ShapeDtypeStruct((1, 3 * d), F32)],
        scratch=[], operands=[dmerged, *acts, *wts, zg, b_gate], semantics=("arbitrary",))
    return tuple(res) if carry is None else (tuple(res[:8]), list(res[8:]))


def _adamw(w, g, m, v, *, tr, name):
    rows, cols = w.shape[-2:]
    assert rows % tr == 0, (name, rows, tr)

    def body(w_ref, g_ref, m_ref, v_ref, g_out, d_ref, nm_ref, nv_ref):
        gv = g_ref[...]
        m2 = ADAM_B1 * m_ref[...] + (1.0 - ADAM_B1) * gv
        v2 = ADAM_B2 * v_ref[...] + (1.0 - ADAM_B2) * (gv * gv)
        m_hat = m2 / (1.0 - ADAM_B1 ** ADAM_STEP)
        v_hat = v2 / (1.0 - ADAM_B2 ** ADAM_STEP)
        g_out[...] = gv
        d_ref[...] = -ADAM_LR * (m_hat / (jnp.sqrt(v_hat) + ADAM_EPS) + ADAM_WD * w_ref[...])
        nm_ref[...] = m2
        nv_ref[...] = v2

    flat = pl.BlockSpec((tr, cols), lambda i: (i, 0))
    blk = flat if w.ndim == 2 else pl.BlockSpec((None, tr, cols), lambda i: (0, i, 0))
    shape = jax.ShapeDtypeStruct(w.shape, F32)
    return pl.pallas_call(
        body, name=name, grid=(rows // tr,), in_specs=[blk, flat, blk, blk], out_specs=[blk] * 4,
        out_shape=[shape] * 4, compiler_params=_params(("parallel",)),
    )(w, g, m, v)


ANY = pl.BlockSpec(memory_space=pl.ANY)


def _position():
    return lax.axis_index("x"), lax.axis_index("y"), lax.axis_index("c")


def _other_chips(x, y):
    return ((1 - x, y), (x, 1 - y), (1 - x, 1 - y))


def _remote(src, dst, send_sem, recv_sem, dev):
    return pltpu.make_async_remote_copy(src_ref=src, dst_ref=dst, send_sem=send_sem, recv_sem=recv_sem,
                                        device_id=dev, device_id_type=MESH)


def _gather_exchange(shards):
    nb = len(shards)

    def rows_of(i, owner, core):
        rs = shards[i].shape[0]
        return pl.ds(pl.multiple_of(owner * rs + core * (rs // 2), 16), rs // 2)

    def first_leg(ins, outs, send_sems, recv_sems, i, j):
        x, y, c = _position()
        px, py = _other_chips(x, y)[j]
        half = shards[i].shape[0] // 2
        mine = ins[i].at[pl.ds(pl.multiple_of(c * half, 16), half)]
        return _remote(mine, outs[i].at[rows_of(i, 2 * x + y, c)], send_sems.at[i, j], recv_sems.at[i, j], (px, py, c))

    def passed_on(outs, send_sems, recv_sems, i, j, core):
        x, y, c = _position()
        px, py = _other_chips(x, y)[j]
        rows = outs[i].at[rows_of(i, 2 * px + py, core)]
        return _remote(rows, rows, send_sems.at[i, 3 + j], recv_sems.at[i, 3 + j], (x, y, 1 - c))

    def own_block(ins, outs, send_sems, recv_sems, i):
        x, y, c = _position()
        rs = shards[i].shape[0]
        place = outs[i].at[pl.ds(pl.multiple_of((2 * x + y) * rs, 16), rs)]
        return _remote(ins[i], place, send_sems.at[i, 6], recv_sems.at[i, 6], (x, y, 1 - c))

    def start(ins, outs, send_sems, recv_sems):
        for i in range(nb):
            own_block(ins, outs, send_sems, recv_sems, i).start()
            for j in range(3):
                first_leg(ins, outs, send_sems, recv_sems, i, j).start()

    def finish(ins, outs, send_sems, recv_sems):
        x, y, c = _position()
        for i in range(nb):
            for j, (px, py) in enumerate(_other_chips(x, y)):
                landed = outs[i].at[rows_of(i, 2 * px + py, c)]
                _remote(landed, landed, send_sems.at[i, j], recv_sems.at[i, j], (px, py, c)).wait_recv()
                passed_on(outs, send_sems, recv_sems, i, j, c).start()
        for i in range(nb):
            own_block(ins, outs, send_sems, recv_sems, i).wait()
            for j in range(3):
                passed_on(outs, send_sems, recv_sems, i, j, 1 - c).wait_recv()
        for i in range(nb):
            for j in range(3):
                first_leg(ins, outs, send_sems, recv_sems, i, j).wait_send()
                passed_on(outs, send_sems, recv_sems, i, j, c).wait_send()

    return _Exchange(ins=list(shards), outs=[jax.ShapeDtypeStruct((N_CHIPS * s.shape[0], s.shape[1]), s.dtype) for s in shards],
                     aliases={}, sems=[(nb, 7), (nb, 7)], start=start, finish=finish)


def _run_exchange(ex, *, name):
    n_in, n_out = len(ex.ins), len(ex.outs)

    def body(*refs):
        c_in, c_out, sems = refs[:n_in], refs[n_in:n_in + n_out], refs[n_in + n_out:]
        ex.start(c_in, c_out, *sems)
        ex.finish(c_in, c_out, *sems)

    return pl.pallas_call(
        body, name=name, in_specs=[ANY] * n_in, out_specs=[ANY] * n_out, out_shape=list(ex.outs),
        input_output_aliases=dict(ex.aliases),
        scratch_shapes=[pltpu.SemaphoreType.DMA(s) for s in ex.sems],
    )(*ex.ins)


def _row_tile(rows):
    return max(t for t in range(16, min(rows, 512) + 1, 16) if rows % t == 0)


def _halves_exchange(grads):
    nb = len(grads)

    def copies(ins, outs, send_sems, recv_sems):
        x, y, c = _position()
        return [_remote(ins[i].at[:, 1 - c], outs[i], send_sems.at[i], recv_sems.at[i], (x, y, 1 - c)) for i in range(nb)]

    def start(ins, outs, send_sems, recv_sems):
        for cp in copies(ins, outs, send_sems, recv_sems):
            cp.start()

    def finish(ins, outs, send_sems, recv_sems):
        for cp in copies(ins, outs, send_sems, recv_sems):
            cp.wait()

    return _Exchange(ins=list(grads), outs=[jax.ShapeDtypeStruct((N_CHIPS, g.shape[2], g.shape[3]), F32) for g in grads],
                     aliases={}, sems=[(nb,), (nb,)], start=start, finish=finish)


def _join_exchanges(parts):
    assert all(not ex.aliases for ex in parts)

    def split(refs, counts):
        out, at = [], 0
        for k in counts:
            out.append(refs[at:at + k])
            at += k
        return out

    def run(which):
        def go(ins, outs, *sems):
            for ex, i, o, s in zip(parts, split(ins, [len(ex.ins) for ex in parts]), split(outs, [len(ex.outs) for ex in parts]),
                                   split(sems, [len(ex.sems) for ex in parts])):
                getattr(ex, which)(i, o, *s)
        return go

    return _Exchange(ins=[a for ex in parts for a in ex.ins], outs=[a for ex in parts for a in ex.outs], aliases={},
                     sems=[s for ex in parts for s in ex.sems], start=run("start"), finish=run("finish"))


def _pair_sum(g4, got, c_arr, *, name):
    _, _, half, cols = g4.shape
    tr = _row_tile(half)

    def body(c_ref, g_ref, t_ref, p_ref, pb_ref):
        sm = g_ref[...] + t_ref[...]
        p_ref[...] = sm
        pb_ref[...] = sm.astype(BF16)

    blk = pl.BlockSpec((None, tr, cols), lambda j, i, c_ref: (j, i, 0))
    grid_spec = pltpu.PrefetchScalarGridSpec(
        num_scalar_prefetch=1, grid=(N_CHIPS, half // tr),
        in_specs=[pl.BlockSpec((None, None, tr, cols), lambda j, i, c_ref: (j, c_ref[0], i, 0)), blk],
        out_specs=[blk, blk])
    return pl.pallas_call(
        body, name=name, grid_spec=grid_spec,
        out_shape=[jax.ShapeDtypeStruct((N_CHIPS, half, cols), F32), jax.ShapeDtypeStruct((N_CHIPS, half, cols), BF16)],
        compiler_params=_params(("parallel", "parallel")),
    )(c_arr, g4, got)


def _scatter_exchange(parts):
    nb = len(parts)

    def copies(ins, outs, send_sems, recv_sems):
        x, y, c = _position()
        return [_remote(ins[i].at[2 * px + py], outs[i].at[j], send_sems.at[i, j], recv_sems.at[i, j], (px, py, c))
                for i in range(nb) for j, (px, py) in enumerate(_other_chips(x, y))]

    def start(ins, outs, send_sems, recv_sems):
        for cp in copies(ins, outs, send_sems, recv_sems):
            cp.start()

    def finish(ins, outs, send_sems, recv_sems):
        for cp in copies(ins, outs, send_sems, recv_sems):
            cp.wait()

    return _Exchange(ins=list(parts), outs=[jax.ShapeDtypeStruct((3,) + p.shape[1:], p.dtype) for p in parts],
                     aliases={}, sems=[(nb, 3), (nb, 3)], start=start, finish=finish)


def _owner_sum(p, got, chip_arr, c_arr, *, replicated, name):
    _, half, cols = p.shape
    tr = _row_tile(half)

    def body(chip_ref, c_ref, p_ref, r_ref, o_ref):
        o_ref[...] = ((p_ref[...] + r_ref[0].astype(F32)) + r_ref[1].astype(F32)) + r_ref[2].astype(F32)

    if replicated:
        out_spec = pl.BlockSpec((None, None, tr, cols), lambda i, chip_ref, c_ref: (chip_ref[0], c_ref[0], i, 0))
        out_shape = jax.ShapeDtypeStruct((N_CHIPS, 2, half, cols), F32)
    else:
        out_spec = pl.BlockSpec((None, tr, cols), lambda i, chip_ref, c_ref: (c_ref[0], i, 0))
        out_shape = jax.ShapeDtypeStruct((2, half, cols), F32)
    grid_spec = pltpu.PrefetchScalarGridSpec(
        num_scalar_prefetch=2, grid=(half // tr,),
        in_specs=[pl.BlockSpec((None, tr, cols), lambda i, chip_ref, c_ref: (chip_ref[0], i, 0)),
                  pl.BlockSpec((3, tr, cols), lambda i, chip_ref, c_ref: (0, i, 0))],
        out_specs=out_spec)
    return pl.pallas_call(
        body, name=name, grid_spec=grid_spec, out_shape=out_shape,
        compiler_params=_params(("parallel",)),
    )(chip_arr, c_arr, p, got)


def _share_reduced(bufs):
    nb = len(bufs) - 1

    def body(*refs):
        outs = refs[nb + 1:2 * nb + 2]
        send_sems, recv_sems = refs[2 * nb + 2:]
        x, y, c = _position()
        chip = 2 * x + y
        sends = []
        for i in range(nb):
            cp = _remote(outs[i].at[c], outs[i].at[c], send_sems.at[i], recv_sems.at[i], (x, y, 1 - c))
            cp.start()
            sends.append(cp)
        small = outs[nb]
        peers = [(fx, fy, fc) for fx in (0, 1) for fy in (0, 1) for fc in (0, 1) if fx + fy + fc > 0]
        for k, (fx, fy, fc) in enumerate(peers):
            dev = (x ^ fx, y ^ fy, c ^ fc)
            cp = _remote(small.at[chip, c], small.at[chip, c], send_sems.at[nb + k], recv_sems.at[nb + k], dev)
            cp.start()
            sends.append(cp)
        for i in range(nb):
            dst = outs[i].at[1 - c]
            _remote(dst, dst, send_sems.at[i], recv_sems.at[i], (x, y, 1 - c)).wait_recv()
        for k, (fx, fy, fc) in enumerate(peers):
            dst = small.at[2 * (x ^ fx) + (y ^ fy), c ^ fc]
            _remote(dst, dst, send_sems.at[nb + k], recv_sems.at[nb + k], (x ^ fx, y ^ fy, c ^ fc)).wait_recv()
        for cp in sends:
            cp.wait_send()

    n_all = nb + 1
    return pl.pallas_call(
        body, name="grad_share_reduced", in_specs=[ANY] * n_all, out_specs=[ANY] * n_all,
        out_shape=[jax.ShapeDtypeStruct(b.shape, b.dtype) for b in bufs],
        input_output_aliases={i: i for i in range(n_all)},
        scratch_shapes=[pltpu.SemaphoreType.DMA((nb + 7,)), pltpu.SemaphoreType.DMA((nb + 7,))],
    )(*bufs)


class _GradReducer:
    def __init__(self, c_arr, chip_arr):
        self.c_arr, self.chip_arr = c_arr, chip_arr
        self.full, self.pairs, self.landed = {}, {}, {}

    def swap(self, names, grads):
        for n, g in zip(names, grads):
            self.full[n] = g.reshape(N_CHIPS, 2, g.shape[0] // (2 * N_CHIPS), g.shape[1])
        return _halves_exchange([self.full[n] for n in names])

    def swapped(self, names, bufs):
        for n, t in zip(names, bufs):
            self.pairs[n] = _pair_sum(self.full[n], t, self.c_arr, name="grad_pair_sum_" + n)

    def scatter(self, names):
        return _scatter_exchange([self.pairs[n][1] for n in names])

    def collect(self, names, bufs):
        self.landed.update(zip(names, bufs))

    def swap_now(self, names, grads):
        self.swapped(names, _run_exchange(self.swap(names, grads), name="grad_exchange_" + names[0]))

    def finish(self, names, grads, order):
        self.swap_now(names, grads)
        self.collect(names, _run_exchange(self.scatter(names), name="grad_scatter_" + names[0]))
        totals = [_owner_sum(self.pairs[n][0], self.landed[n], self.chip_arr, self.c_arr, replicated=(n == order[-1]),
                             name="grad_owner_sum_" + n) for n in order]
        return _share_reduced(totals)


def _pack_small(vals):
    flat = jnp.concatenate([vals[name].reshape(-1) for name, _ in SMALL])
    return jnp.pad(flat, (0, N_CHIPS * SMALL_ROWS * 1024 - SMALL_ELEMS)).reshape(N_CHIPS * SMALL_ROWS, 1024)


def _unpack_small(buf):
    flat = buf.reshape(-1)
    out, off = {}, 0
    for name, shape in SMALL:
        n = int(np.prod(shape))
        out[name] = flat[off:off + n].reshape(shape)
        off += n
    return out


EARLY_REDUCED = (("w_down",), ("w_up",), ("w_o", "w_ssm_br", "w_attn_br", "w_mem_br", "w_glu", "w_mem_kv"), ("w_in",))


def _device_step(x, mem, tgt, w, p, *, shards, reducer):
    rows = x.shape[0]
    w = dict(w)
    early = EARLY_REDUCED
    gb = {}
    gather_pending = shards is not None

    def riding(*stages):
        if reducer is None or not stages:
            return None
        return _join_exchanges([reducer.swap(names, [gb[n] for n in names]) if kind == "swap" else reducer.scatter(names)
                                for kind, names in stages])

    def arrived(stages, res):
        if reducer is None or not stages:
            return res
        main, bufs = res
        for kind, names in stages:
            (reducer.swapped if kind == "swap" else reducer.collect)(names, bufs[:len(names)])
            bufs = bufs[len(names):]
        return main

    def fetching(names):
        return _gather_exchange([shards[n] for n in names]) if gather_pending else None

    def fetched(names, res):
        if not gather_pending:
            return res
        w.update(zip(names, res[1]))
        return res[0]

    first_use = (("w_in",), ("w_glu", "w_ssm_br", "w_attn_br", "w_mem_kv", "w_mem_br", "w_o", "w_up"), ("w_down",))
    g1, gm, g2 = p["norm1_g"], p["mem_norm_g"], p["norm2_g"]
    gf = p["final_g"].reshape(1, D_MODEL)
    ssm_args = (p["ssm_lambda_re"][0], p["ssm_lambda_im"][0], p["ssm_log_dt"][0], p["ssm_b_re"][0],
                p["ssm_b_im"][0], p["ssm_c_re"][0], p["ssm_c_im"][0])
    (a_lay, b_blk, c_blk), ssm_vjp = jax.vjp(_ssm_matrices, *ssm_args)
    a_conj = a_lay * _to_scan_layout(jnp.stack([jnp.ones((N_STATES,), F32), -jnp.ones((N_STATES,), F32)]))[None, :]
    dd = p["ssm_d"].reshape(1, SSM_WIDTH)
    mm = _matmul

    n1 = fetched(first_use[0], _rmsnorm_fwd(x, g1, tm=512, carry=fetching(first_use[0]), name="norm1"))
    win_t = w["w_in"]
    splits = ((OFF_U, OFF_QKV - OFF_U), (OFF_QKV, OFF_MQ - OFF_QKV), (OFF_MQ, OFF_ZG - OFF_MQ), (OFF_ZG, IN_WIDTH - OFF_ZG))
    u, qkv, mq, zg = fetched(first_use[1], _split_matmul(n1, win_t, splits, tm=512, carry=fetching(first_use[1]),
                                                         vmem=VMEM_LIMIT_WIDE_BYTES, name="in_proj"))

    u_i = _interleave(u)
    ends = _ssm_ends(a_lay, u_i, b_blk, transpose=False, reverse=False, tt=512, name="ssm_fwd_ends")
    s, ys_i, s_entry = _ssm_fwd(a_lay, u_i, b_blk, c_blk, ends, tt=512, name="ssm_fwd")
    ys = _deinterleave(ys_i)
    y0, tglu, y2 = _glu_fwd(ys, u, dd, w["w_glu"], p["b_glu"], tm=512, name="glu_fwd")

    outs, lses = [], []
    for g, (_, d) in enumerate(ATTN_PATTERNS):
        o_g, lse_g = _attn_fwd(qkv, g, d, name=f"attn_fwd_{g}")
        outs.append(o_g)
        lses.append(lse_g)
    o, lse = _attn_merge(outs, lses, tm=1024, name="attn_merge")

    mn = _rmsnorm_fwd(mem, gm, tm=MEM_LEN, name="mem_norm")
    kv = mm(mn, w["w_mem_kv"], m=MEM_LEN, n=1024, k=1024, tm=MEM_LEN, tn=1024, tk=1024, out_dtypes=(F32,), name="mem_kv")
    mo = _mem_attn_fwd(mq, kv, tq=1024, name="mem_attn_fwd")

    branch_acts = (y2, o, mo)
    branch_wts = (w["w_ssm_br"], w["w_attn_br"], w["w_mem_br"])
    merged = _branch_merge_fwd(branch_acts, branch_wts, zg, p["b_gate"], tm=256, name="branch_merge_fwd")
    h1, n2 = mm(merged, w["w_o"], m=rows, n=1024, k=1024, tm=1024, tn=1024, tk=1024, out_dtypes=(F32, BF16),
                aux=((x, "mn"), (g2, "row")), epilogue=_residual_norm_epilogue, name="out_proj")
    relu2 = lambda acc: (jnp.square(jnp.maximum(acc, 0.0)),)
    act = fetched(first_use[2], _sum_matmul([n2], w["w_up"], [0], tb=True, tm=512, out_dtype=BF16, epilogue=relu2,
                                            carry=fetching(first_use[2]), name="mlp_up"))
    dh2, d_gf, sq_err = _sum_matmul([act], w["w_down"], [0], tm=512, aux=((h1, "mn"), (tgt, "mn"), (gf, "row")),
                                    epilogue=_loss_head_epilogue, n_sums=2, name="mlp_down")
    loss = (0.5 / D_MODEL) * jnp.sum(sq_err)

    gs = {"final_g": d_gf.reshape(D_MODEL)}
    drelu2 = lambda acc, actv: (acc * (2.0 * jnp.sqrt(actv.astype(F32))),)
    dup = mm(dh2, w["w_down"], m=rows, n=D_FF, k=1024, tb=True, tm=1024, tn=2048, tk=1024, out_dtypes=(BF16,),
             aux=((act, "mn"),), epilogue=drelu2, name="d_act")
    gb["w_down"] = mm(act, dh2, m=D_FF, n=1024, k=rows, ta=True, tm=1024, tn=1024, tk=2048, out_dtypes=(F32,), name="dw_down")
    stages = (("swap", early[0]),)
    gb["w_up"] = arrived(stages, mm(dup, n2, m=D_FF, n=1024, k=rows, ta=True, tm=1024, tn=1024, tk=2048,
                                    out_dtypes=(F32,), carry=riding(*stages), name="dw_up"))
    stages = (("scatter", early[0]), ("swap", early[1]))
    dh1, gs["norm2_g"] = arrived(stages, _sum_matmul([dup], w["w_up"], [0], tm=512, aux=((h1, "mn"), (dh2, "mn"), (g2, "row")),
                                                     epilogue=_rmsnorm_bwd_epilogue, n_sums=1, carry=riding(*stages), name="d_n2"))
    dmerged = mm(dh1, w["w_o"], m=rows, n=1024, k=1024, tb=True, tm=1024, tn=1024, tk=1024, out_dtypes=(F32,), name="d_merged")
    gb["w_o"] = mm(merged, dh1, m=1024, n=1024, k=rows, ta=True, tm=1024, tn=1024, tk=2048, out_dtypes=(F32,), name="dw_o")
    stages = (("scatter", early[1]),)
    (dy2, do, dmo, gb["w_ssm_br"], gb["w_attn_br"], gb["w_mem_br"], dzg, gs["b_gate"]) = arrived(stages, _branch_merge_bwd(
        dmerged, branch_acts, branch_wts, zg, p["b_gate"], tm=256, carry=riding(*stages), name="branch_merge_bwd"))

    dy0, dt, y1, gs["b_glu"], d_dd = _glu_bwd(dy2, y0, tglu, u, w["w_glu"], tm=512, name="glu_bwd")
    gs["ssm_d"] = d_dd.reshape(1, SSM_GROUPS, SSM_GROUP_SIZE)
    gb["w_glu"] = mm(y1, dt, m=512, n=512, k=rows, ta=True, tm=512, tn=512, tk=1024, out_dtypes=(F32,), name="dw_glu")
    dy0_i = _interleave(dy0)
    lam_ends = _ssm_ends(a_conj, dy0_i, c_blk, transpose=True, reverse=True, tt=512, name="ssm_bwd_ends")
    du_i, d_b_blk, d_c_blk, d_a_lay = _ssm_bwd(a_conj, dy0_i, u_i, s, s_entry, b_blk, c_blk, dd, lam_ends, tt=512,
                                                name="ssm_bwd")
    du = _deinterleave(du_i)
    d_ssm = ssm_vjp((d_a_lay, d_b_blk, d_c_blk))
    for name, val in zip(("ssm_lambda_re", "ssm_lambda_im", "ssm_log_dt", "ssm_b_re", "ssm_b_im", "ssm_c_re", "ssm_c_im"), d_ssm):
        gs[name] = val[None]

    dqkv = None
    for g, (_, d) in enumerate(ATTN_PATTERNS):
        dqkv = _attn_bwd(qkv, do, o, lse, g, d, dqkv, name=f"attn_bwd_{g}")

    dmq, dmk, dmv = _mem_attn_bwd(mq, kv, dmo, tq=1024, name="mem_attn_bwd")
    dkv = jnp.concatenate([dmk, dmv], axis=1)
    gb["w_mem_kv"] = mm(mn, dkv, m=1024, n=1024, k=MEM_LEN, ta=True, tm=1024, tn=1024, tk=MEM_LEN, out_dtypes=(F32,), name="dw_mem_kv")
    dmn = mm(dkv, w["w_mem_kv"], m=MEM_LEN, n=1024, k=1024, tb=True, tm=MEM_LEN, tn=1024, tk=1024, out_dtypes=(F32,), name="d_mn")
    _, gs["mem_norm_g"] = _rmsnorm_bwd(mem, gm, dmn, None, tm=MEM_LEN, name="mem_norm_bwd")

    pieces = ((du, OFF_U, "u"), (dqkv[0], OFF_QKV, "q"), (dqkv[1], OFF_QKV + 768, "k"), (dqkv[2], OFF_QKV + 1536, "v"),
              (dmq, OFF_MQ, "mq"), (dzg, OFF_ZG, "zg"))
    dw_rows = []
    for piece, off, tag in pieces:
        width = piece.shape[1]
        tmw = 1024 if width % 1024 == 0 else (768 if width == 768 else 512)
        stages = {"q": (("swap", early[2]),), "zg": (("scatter", early[2]),)}.get(tag, ())
        dw_rows.append(arrived(stages, mm(piece, n1, m=width, n=1024, k=rows, ta=True, tm=tmw, tn=1024, tk=2048,
                                          out_dtypes=(F32,), carry=riding(*stages), name="dw_in_" + tag)))
    gb["w_in"] = jnp.concatenate(dw_rows, axis=0)
    if reducer is not None:
        reducer.swap_now(early[3], [gb["w_in"]])
    stages = (("scatter", early[3]),)
    dx, gs["norm1_g"] = arrived(stages, _sum_matmul(
        [piece for piece, _, _ in pieces], win_t, [off for _, off, _ in pieces], tm=512,
        aux=((x, "mn"), (dh1, "mn"), (g1, "row")), epilogue=_rmsnorm_bwd_epilogue, n_sums=1,
        carry=riding(*stages), vmem=VMEM_LIMIT_WIDE_BYTES, name="d_n1"))
    return loss, dx, gb, gs


def kernel(x, mem, norm1_g, mem_norm_g, w_in, b_gate, ssm_lambda_re, ssm_lambda_im, ssm_log_dt, ssm_b_re, ssm_b_im, ssm_c_re, ssm_c_im, ssm_d, w_glu, b_glu, w_ssm_br, w_attn_br, w_mem_kv, w_mem_br, w_o, norm2_g, w_up, w_down, final_g, loss_target, m_norm1_g, m_mem_norm_g, m_w_in, m_b_gate, m_ssm_lambda_re, m_ssm_lambda_im, m_ssm_log_dt, m_ssm_b_re, m_ssm_b_im, m_ssm_c_re, m_ssm_c_im, m_ssm_d, m_w_glu, m_b_glu, m_w_ssm_br, m_w_attn_br, m_w_mem_kv, m_w_mem_br, m_w_o, m_norm2_g, m_w_up, m_w_down, m_final_g, v_norm1_g, v_mem_norm_g, v_w_in, v_b_gate, v_ssm_lambda_re, v_ssm_lambda_im, v_ssm_log_dt, v_ssm_b_re, v_ssm_b_im, v_ssm_c_re, v_ssm_c_im, v_ssm_d, v_w_glu, v_b_glu, v_w_ssm_br, v_w_attn_br, v_w_mem_kv, v_w_mem_br, v_w_o, v_norm2_g, v_w_up, v_w_down, v_final_g):
    env = dict(locals())
    weights = {n: env[n] for n in WEIGHT_ORDER}
    moms = {n: env["m_" + n] for n in WEIGHT_ORDER}
    vels = {n: env["v_" + n] for n in WEIGHT_ORDER}

    chip = 2 * lax.axis_index("x") + lax.axis_index("y")
    wire = [weights[n].reshape(weights[n].shape[-2:]).astype(BF16) for n, _, _ in BIG]
    wire = dict(zip([n for n, _, _ in BIG], [s.T if tr else s for s, (_, tr, _) in zip(wire, BIG)]))
    small = {n: weights[n] for n, _ in SMALL}

    reducer = _GradReducer(lax.axis_index("c").astype(jnp.int32).reshape(1), chip.astype(jnp.int32).reshape(1))
    loss, dx, gb, gs = _device_step(x[0], mem[0], loss_target[0], {}, small, shards=wire, reducer=reducer)
    *shards, small_grad = reducer.finish(["small"], [_pack_small(gs)], [n for n, _, _ in BIG] + ["small"])
    grads = {}
    for (n, tr, _), sh in zip(BIG, shards):
        sh = sh.reshape(2 * sh.shape[1], sh.shape[2])
        grads[n] = sh.T if tr else sh
    small_grad = small_grad.reshape(N_CHIPS * SMALL_ROWS, 1024)
    grads_small = _unpack_small(small_grad)

    delta, new_m, new_v = {}, {}, {}
    for n, _, _ in BIG:
        grads[n], delta[n], new_m[n], new_v[n] = _adamw(weights[n], grads[n], moms[n], vels[n],
                                                        tr=min(weights[n].shape[-2], 256), name="adamw_" + n)
    _, ds_, ms_, vs_ = _adamw(_pack_small(small), small_grad,
                              _pack_small({n: moms[n] for n, _ in SMALL}), _pack_small({n: vels[n] for n, _ in SMALL}),
                              tr=N_CHIPS * SMALL_ROWS, name="adamw_small")
    for dst, buf in ((delta, ds_), (new_m, ms_), (new_v, vs_)):
        dst.update(_unpack_small(buf))
    grads.update(grads_small)

    total_loss = lax.psum(loss, ("x", "y", "c"))
    return (total_loss, dx[None], *[grads[n] for n in WEIGHT_ORDER], *[delta[n] for n in WEIGHT_ORDER],
            *[new_m[n] for n in WEIGHT_ORDER], *[new_v[n] for n in WEIGHT_ORDER])
```

```python
import functools
import math

import numpy as np
import jax
import jax.numpy as jnp
from jax import lax
from jax.experimental import pallas as pl
from jax.experimental.pallas import tpu as pltpu

F32 = jnp.float32
BF16 = jnp.bfloat16

D_MODEL = 1024
SSM_GROUPS = 32
SSM_GROUP_SIZE = 16
SSM_STATE = 64
SSM_WIDTH = 512
N_STATES = SSM_GROUPS * SSM_STATE
SCAN_CB = 1024
ATTN_PATTERNS = ((128, 1), (512, 4), (2048, 16))
ATTN_HEAD_DIM = 64
ATTN_Q = 128
MEM_LEN = 256
MEM_HEAD_DIM = 128
MEM_HEADS = 4
D_FF = 4096
OFF_U, OFF_QKV, OFF_MQ, OFF_ZG = 0, 512, 2816, 3328
IN_WIDTH = 6400
RMS_EPS = 1e-6
NEG_INF = -1e30
ADAM_LR, ADAM_B1, ADAM_B2, ADAM_EPS, ADAM_WD, ADAM_STEP = 0.001, 0.9, 0.999, 1e-08, 0.01, 10

VMEM_LIMIT_BYTES = 48 * 1024 * 1024
VMEM_LIMIT_WIDE_BYTES = 56 * 1024 * 1024
LANES = 128
MESH = pl.DeviceIdType.MESH
N_CHIPS = 4

SCAN_SEGS = 8
SCAN_GROUPS = SCAN_CB // SSM_STATE

BIG = (("w_in", True, (6400, 1024)), ("w_glu", False, (512, 512)), ("w_ssm_br", True, (1024, 512)),
       ("w_attn_br", True, (1024, 256)), ("w_mem_kv", False, (1024, 1024)), ("w_mem_br", True, (1024, 512)),
       ("w_o", False, (1024, 1024)), ("w_up", True, (4096, 1024)), ("w_down", False, (4096, 1024)))
SMALL = (("norm1_g", (1, 1024)), ("mem_norm_g", (1, 1024)), ("b_gate", (1, 3072)),
         ("ssm_lambda_re", (1, 32, 64)), ("ssm_lambda_im", (1, 32, 64)), ("ssm_log_dt", (1, 32)),
         ("ssm_b_re", (1, 32, 64, 16)), ("ssm_b_im", (1, 32, 64, 16)), ("ssm_c_re", (1, 32, 16, 64)),
         ("ssm_c_im", (1, 32, 16, 64)), ("ssm_d", (1, 32, 16)), ("b_glu", (1, 512)),
         ("norm2_g", (1, 1024)), ("final_g", (1024,)))
WEIGHT_ORDER = ("norm1_g", "mem_norm_g", "w_in", "b_gate", "ssm_lambda_re", "ssm_lambda_im", "ssm_log_dt",
                "ssm_b_re", "ssm_b_im", "ssm_c_re", "ssm_c_im", "ssm_d", "w_glu", "b_glu", "w_ssm_br",
                "w_attn_br", "w_mem_kv", "w_mem_br", "w_o", "norm2_g", "w_up", "w_down", "final_g")
SMALL_ELEMS = sum(int(np.prod(s)) for _, s in SMALL)
SMALL_ROWS = 64


def _params(sem, vmem=VMEM_LIMIT_BYTES):
    return pltpu.CompilerParams(dimension_semantics=sem, vmem_limit_bytes=vmem)


def _sigmoid(v):
    return 0.5 * jnp.tanh(0.5 * v) + 0.5


_GELU_C = math.sqrt(2.0 / math.pi)


def _gelu(v):
    return 0.5 * v * (1.0 + jnp.tanh(_GELU_C * (v + 0.044715 * v * v * v)))


def _gelu_grad(v):
    th = jnp.tanh(_GELU_C * (v + 0.044715 * v * v * v))
    return 0.5 * (1.0 + th) + 0.5 * v * (1.0 - th * th) * _GELU_C * (1.0 + 3.0 * 0.044715 * v * v)


def _dot(a, b, ca, cb):
    return lax.dot_general(a, b, (((ca,), (cb,)), ((), ())), preferred_element_type=F32)


class _Exchange:
    def __init__(self, ins, outs, aliases, sems, start, finish):
        self.ins, self.outs, self.aliases, self.sems, self.start, self.finish = ins, outs, aliases, sems, start, finish


def _matmul(a, b, *, m, n, k, ta=False, tb=False, tm, tn, tk, out_dtypes, name,
            aux=(), epilogue=None, n_sums=0, carry=None):
    assert m % tm == 0 and n % tn == 0 and k % tk == 0, (name, m, n, k, tm, tn, tk)
    assert n_sums == 0 or tn == n, name
    nk = k // tk
    n_aux = len(aux)
    n_tiles = len(out_dtypes)
    n_out = n_tiles + n_sums
    a_spec = pl.BlockSpec((tk, tm), lambda i, j, kk: (kk, i)) if ta else pl.BlockSpec((tm, tk), lambda i, j, kk: (i, kk))
    b_spec = pl.BlockSpec((tn, tk), lambda i, j, kk: (j, kk)) if tb else pl.BlockSpec((tk, tn), lambda i, j, kk: (kk, j))
    aux_specs = []
    for _, kind in aux:
        if kind == "mn":
            aux_specs.append(pl.BlockSpec((tm, tn), lambda i, j, kk: (i, j)))
        else:
            aux_specs.append(pl.BlockSpec((1, tn), lambda i, j, kk: (0, j)))
    ca = 0 if ta else 1
    cb = 1 if tb else 0

    def finish(acc, aux_refs, out_refs, row_tile):
        outs = (acc,) if epilogue is None else epilogue(acc, *[r[...] for r in aux_refs])
        for o_ref, o in zip(out_refs[:n_tiles], outs[:n_tiles]):
            o_ref[...] = o.astype(o_ref.dtype)
        _accumulate_over_rows(out_refs[n_tiles:], outs[n_tiles:], row_tile)

    def body(a_ref, b_ref, *rest):
        aux_refs = rest[:n_aux]
        out_refs = rest[n_aux:n_aux + n_out]
        row_tile = pl.program_id(0)
        prod = _dot(a_ref[...].astype(BF16), b_ref[...].astype(BF16), ca, cb)
        if nk == 1:
            finish(prod, aux_refs, out_refs, row_tile)
            return
        acc_ref = rest[n_aux + n_out]
        kk = pl.program_id(2)

        @pl.when(kk == 0)
        def _():
            acc_ref[...] = prod

        @pl.when(jnp.logical_and(kk > 0, kk < nk - 1))
        def _():
            acc_ref[...] += prod

        @pl.when(kk == nk - 1)
        def _():
            finish(acc_ref[...] + prod, aux_refs, out_refs, row_tile)

    tile = pl.BlockSpec((tm, tn), lambda i, j, kk: (i, j))
    col_sum = pl.BlockSpec((1, tn), lambda i, j, kk: (0, j))
    res = _call_with_carry(
        body, carry, name=name, grid=(m // tm, n // tn, nk), in_specs=[a_spec, b_spec] + aux_specs,
        out_specs=[tile] * n_tiles + [col_sum] * n_sums,
        out_shape=[jax.ShapeDtypeStruct((m, n), dt) for dt in out_dtypes] + [jax.ShapeDtypeStruct((1, n), F32)] * n_sums,
        scratch=[pltpu.VMEM((tm, tn), F32)] if nk > 1 else [], operands=[a, b] + [x for x, _ in aux],
        semantics=("arbitrary" if n_sums else "parallel", "parallel", "arbitrary"))
    main = res[0] if n_out == 1 else tuple(res[:n_out])
    return main if carry is None else (main, list(res[n_out:]))


def _accumulate_over_rows(sum_refs, terms, row_tile):
    for s_ref, term in zip(sum_refs, terms):
        @pl.when(row_tile == 0)
        def _():
            s_ref[...] = term

        @pl.when(row_tile > 0)
        def _():
            s_ref[...] += term


def _call_with_carry(body, carry, *, name, grid, in_specs, out_specs, out_shape, scratch, operands, semantics,
                     vmem=VMEM_LIMIT_BYTES):
    if carry is None:
        return pl.pallas_call(body, name=name, grid=grid, in_specs=in_specs, out_specs=out_specs, out_shape=out_shape,
                              scratch_shapes=scratch, compiler_params=_params(semantics, vmem))(*operands)
    n_in, n_cin, n_out, n_cout, n_scr = len(operands), len(carry.ins), len(out_shape), len(carry.outs), len(scratch)

    def hosted(*refs):
        main_in, c_in = refs[:n_in], refs[n_in:n_in + n_cin]
        main_out = refs[n_in + n_cin:n_in + n_cin + n_out]
        c_out = refs[n_in + n_cin + n_out:n_in + n_cin + n_out + n_cout]
        rest = refs[n_in + n_cin + n_out + n_cout:]
        ids = [pl.program_id(t) for t in range(len(grid))]
        first = functools.reduce(jnp.logical_and, [i == 0 for i in ids])
        last = functools.reduce(jnp.logical_and, [i == g - 1 for i, g in zip(ids, grid)])

        @pl.when(first)
        def _():
            carry.start(c_in, c_out, *rest[n_scr:])

        body(*main_in, *main_out, *rest[:n_scr])

        @pl.when(last)
        def _():
            carry.finish(c_in, c_out, *rest[n_scr:])

    return pl.pallas_call(
        hosted, name=name, grid=grid,
        in_specs=list(in_specs) + [ANY] * n_cin, out_specs=list(out_specs) + [ANY] * n_cout,
        out_shape=list(out_shape) + list(carry.outs),
        input_output_aliases={n_in + i: n_out + o for i, o in carry.aliases.items()},
        scratch_shapes=list(scratch) + [pltpu.SemaphoreType.DMA(s) for s in carry.sems],
        compiler_params=_params(("arbitrary",) * len(grid), vmem),
    )(*operands, *carry.ins)


def _sum_matmul(pieces, b, offs, *, tm, name, tb=False, out_dtype=F32, aux=(), epilogue=None, n_sums=0, carry=None,
                vmem=VMEM_LIMIT_BYTES):
    m = pieces[0].shape[0]
    n = b.shape[0] if tb else b.shape[1]
    npieces, n_aux = len(pieces), len(aux)
    assert not tb or npieces == 1

    def body(*refs):
        b_ref = refs[npieces]
        aux_refs = refs[npieces + 1:npieces + 1 + n_aux]
        out_refs = refs[npieces + 1 + n_aux:]
        acc = None
        for p_ref, off in zip(refs[:npieces], offs):
            lhs = p_ref[...].astype(BF16)
            part = _dot(lhs, b_ref[...], 1, 1) if tb else _dot(lhs, b_ref[pl.ds(off, p_ref.shape[1]), :], 1, 0)
            acc = part if acc is None else acc + part
        outs = (acc,) if epilogue is None else epilogue(acc, *[r[...] for r in aux_refs])
        out_refs[0][...] = outs[0].astype(out_dtype)
        _accumulate_over_rows(out_refs[1:], outs[1:], pl.program_id(0))

    row = pl.BlockSpec((tm, n), lambda i: (i, 0))
    vec = pl.BlockSpec((1, n), lambda i: (0, 0))
    res = _call_with_carry(
        body, carry, name=name, grid=(m // tm,),
        in_specs=[pl.BlockSpec((tm, p.shape[1]), lambda i: (i, 0)) for p in pieces] + [_resident(b.shape)]
        + [row if kind == "mn" else vec for _, kind in aux],
        out_specs=[row] + [vec] * n_sums,
        out_shape=[jax.ShapeDtypeStruct((m, n), out_dtype)] + [jax.ShapeDtypeStruct((1, n), F32)] * n_sums,
        scratch=[], operands=list(pieces) + [b] + [x for x, _ in aux], semantics=("arbitrary" if n_sums else "parallel",),
        vmem=vmem)
    main = res[0] if n_sums == 0 else tuple(res[:1 + n_sums])
    return main if carry is None else (main, list(res[1 + n_sums:]))


def _split_matmul(a, b_t, splits, *, tm, name, carry=None, vmem=VMEM_LIMIT_BYTES):
    m, k = a.shape

    def body(a_ref, b_ref, *out_refs):
        av = a_ref[...].astype(BF16)
        for (row0, width), o_ref in zip(splits, out_refs):
            o_ref[...] = _dot(av, b_ref[pl.ds(row0, width), :], 1, 1)

    res = _call_with_carry(
        body, carry, name=name, grid=(m // tm,),
        in_specs=[pl.BlockSpec((tm, k), lambda i: (i, 0)), _resident(b_t.shape)],
        out_specs=[pl.BlockSpec((tm, width), lambda i: (i, 0)) for _, width in splits],
        out_shape=[jax.ShapeDtypeStruct((m, width), F32) for _, width in splits],
        scratch=[], operands=[a, b_t], semantics=("parallel",), vmem=vmem)
    outs = tuple(res[:len(splits)])
    return outs if carry is None else (outs, list(res[len(splits):]))


def _rmsnorm_fwd(x, g, *, tm, name, carry=None):
    rows, d = x.shape

    def body(x_ref, g_ref, o_ref):
        xv = x_ref[...]
        r = lax.rsqrt(jnp.mean(xv * xv, axis=-1, keepdims=True) + RMS_EPS)
        o_ref[...] = (xv * r * g_ref[...]).astype(o_ref.dtype)

    res = _call_with_carry(
        body, carry, name=name, grid=(rows // tm,),
        in_specs=[pl.BlockSpec((tm, d), lambda i: (i, 0)), pl.BlockSpec((1, d), lambda i: (0, 0))],
        out_specs=[pl.BlockSpec((tm, d), lambda i: (i, 0))], out_shape=[jax.ShapeDtypeStruct((rows, d), BF16)],
        scratch=[], operands=[x, g], semantics=("parallel",))
    return res[0] if carry is None else (res[0], list(res[1:]))


def _residual_norm_epilogue(acc, xv, gv):
    h = acc + xv
    r = lax.rsqrt(jnp.mean(h * h, axis=-1, keepdims=True) + RMS_EPS)
    return h, h * r * gv


def _rmsnorm_bwd_epilogue(dy, xv, resv, gv):
    r = lax.rsqrt(jnp.mean(xv * xv, axis=-1, keepdims=True) + RMS_EPS)
    xhat = xv * r
    dyg = dy * gv
    dx = r * (dyg - xhat * jnp.mean(dyg * xhat, axis=-1, keepdims=True)) + resv
    return dx, jnp.sum(dy * xhat, axis=0, keepdims=True)


def _rmsnorm_bwd(x, g, dy, res, *, tm, name):
    rows, d = x.shape
    has_res = res is not None

    def body(x_ref, g_ref, dy_ref, *rest):
        if has_res:
            res_ref, dx_ref, dg_ref = rest
        else:
            dx_ref, dg_ref = rest
        i = pl.program_id(0)
        xv = x_ref[...]
        r = lax.rsqrt(jnp.mean(xv * xv, axis=-1, keepdims=True) + RMS_EPS)
        xhat = xv * r
        dyv = dy_ref[...]
        dyg = dyv * g_ref[...]
        dx = r * (dyg - xhat * jnp.mean(dyg * xhat, axis=-1, keepdims=True))
        if has_res:
            dx = dx + res_ref[...]
        dx_ref[...] = dx

        @pl.when(i == 0)
        def _():
            dg_ref[...] = jnp.zeros_like(dg_ref)

        dg_ref[...] += jnp.sum(dyv * xhat, axis=0, keepdims=True)

    row_spec = pl.BlockSpec((tm, d), lambda i: (i, 0))
    vec_spec = pl.BlockSpec((1, d), lambda i: (0, 0))
    ins = [x, g, dy] + ([res] if has_res else [])
    return pl.pallas_call(
        body, name=name, grid=(rows // tm,),
        in_specs=[row_spec, vec_spec, row_spec] + ([row_spec] if has_res else []),
        out_specs=[row_spec, vec_spec],
        out_shape=[jax.ShapeDtypeStruct((rows, d), F32), jax.ShapeDtypeStruct((1, d), F32)],
        compiler_params=_params(("arbitrary",)),
    )(*ins)


def _loss_head_epilogue(acc, hv, tgtv, gv):
    xv = acc + hv
    r = lax.rsqrt(jnp.mean(xv * xv, axis=-1, keepdims=True) + RMS_EPS)
    xhat = xv * r
    err = xhat * gv - tgtv
    dyv = err * (1.0 / D_MODEL)
    dyg = dyv * gv
    dh = r * (dyg - xhat * jnp.mean(dyg * xhat, axis=-1, keepdims=True))
    return dh, jnp.sum(dyv * xhat, axis=0, keepdims=True), jnp.sum(err * err, axis=0, keepdims=True)


def _to_scan_layout(v):
    lead = v.shape[:-2]
    v = v.reshape(lead + (2, N_STATES // SCAN_CB, SCAN_CB))
    v = jnp.swapaxes(v, -3, -2)
    return v.reshape(lead + (2 * N_STATES,))


def _ssm_matrices(lam_re, lam_im, log_dt, b_re, b_im, c_re, c_im):
    dt = jnp.exp(log_dt)[:, None]
    mag = jnp.exp(lam_re * dt)
    a_re, a_im = mag * jnp.cos(lam_im * dt), mag * jnp.sin(lam_im * dt)
    nr, ni = a_re - 1.0, a_im
    den = lam_re * lam_re + lam_im * lam_im
    coef_re = (nr * lam_re + ni * lam_im) / den
    coef_im = (ni * lam_re - nr * lam_im) / den
    bb_re = coef_re[..., None] * b_re - coef_im[..., None] * b_im
    bb_im = coef_re[..., None] * b_im + coef_im[..., None] * b_re
    a_lay = _to_scan_layout(jnp.stack([a_re.reshape(-1), a_im.reshape(-1)], axis=0))[None, :]
    nblk = SSM_GROUPS // SCAN_GROUPS
    eye = jnp.eye(SCAN_GROUPS, dtype=F32)

    def b_block(bb):
        bb = bb.reshape(nblk, SCAN_GROUPS, SSM_STATE, SSM_GROUP_SIZE)
        return jnp.einsum("gk,jkph->jghkp", eye, bb).reshape(nblk, SCAN_GROUPS * SSM_GROUP_SIZE, SCAN_CB)

    b_blk = jnp.concatenate([b_block(bb_re), b_block(bb_im)], axis=2)

    def c_block(cc):
        cc = cc.reshape(nblk, SCAN_GROUPS, SSM_GROUP_SIZE, SSM_STATE)
        return jnp.einsum("gk,jghp->jkpgh", eye, cc).reshape(nblk, SCAN_CB, SCAN_GROUPS * SSM_GROUP_SIZE)

    c_blk = jnp.concatenate([c_block(c_re), -c_block(c_im)], axis=1)
    return a_lay, b_blk, c_blk


def _interleave(v):
    rows, c = v.shape
    return v.reshape(SCAN_SEGS, rows // SCAN_SEGS, c).transpose(1, 0, 2).reshape(rows, c)


def _deinterleave(v):
    rows, c = v.shape
    return v.reshape(rows // SCAN_SEGS, SCAN_SEGS, c).transpose(1, 0, 2).reshape(rows, c)


def _scan_groups(a_ref, bu_ref, o_ref, state, *, reverse, tt, unroll=4):
    cb = SCAN_CB
    ar = jnp.broadcast_to(a_ref[:, :cb], (SCAN_SEGS, cb))
    ai = jnp.broadcast_to(a_ref[:, cb:], (SCAN_SEGS, cb))
    ngroups = tt // SCAN_SEGS

    def step(i, st):
        sr, si = st
        r0 = pl.multiple_of(((ngroups - 1 - i) if reverse else i) * SCAN_SEGS, SCAN_SEGS)
        blk = bu_ref[pl.ds(r0, SCAN_SEGS), :]
        nr = ar * sr - ai * si + blk[:, :cb]
        ni = ar * si + ai * sr + blk[:, cb:]
        if o_ref is not None:
            o_ref[pl.ds(r0, SCAN_SEGS), :] = jnp.concatenate([nr, ni], axis=1)
        return nr, ni

    return lax.fori_loop(0, ngroups, step, state, unroll=unroll)


def _segment_entries(a_ref, e_ref, init_ref, *, reverse, seg_len):
    cb = SCAN_CB
    n_sq = seg_len.bit_length() - 1
    assert 1 << n_sq == seg_len, seg_len
    pr, pi = a_ref[:, :cb], a_ref[:, cb:]
    for _ in range(n_sq):
        pr, pi = pr * pr - pi * pi, 2.0 * pr * pi
    cr = jnp.zeros((1, cb), F32)
    ci = jnp.zeros((1, cb), F32)
    order = range(SCAN_SEGS - 1, -1, -1) if reverse else range(SCAN_SEGS)
    for k, seg in enumerate(order):
        if k > 0:
            prev = seg + 1 if reverse else seg - 1
            er, ei = e_ref[prev:prev + 1, :cb], e_ref[prev:prev + 1, cb:]
            cr, ci = pr * cr - pi * ci + er, pr * ci + pi * cr + ei
        init_ref[seg:seg + 1, :] = jnp.concatenate([cr, ci], axis=1)


def _ssm_specs(nt, tt, nch, reverse):
    cb = SCAN_CB
    tmap = (lambda j, kk: (nt - 1 - kk, j)) if reverse else (lambda j, kk: (kk, j))
    nmap = (lambda j, kk: (jnp.maximum(nt - 2 - kk, 0), j)) if reverse else (lambda j, kk: (jnp.minimum(kk + 1, nt - 1), j))
    return dict(a=pl.BlockSpec((1, 2 * cb), lambda j, kk: (0, j)),
                seg=pl.BlockSpec((SCAN_SEGS, 2 * cb), lambda j, kk: (0, j)),
                chan=pl.BlockSpec((tt, nch), tmap),
                next=pl.BlockSpec((tt, nch), nmap),
                state=pl.BlockSpec((tt, 2 * cb), tmap),
                b=pl.BlockSpec((None, nch, 2 * cb), lambda j, kk: (j, 0, 0)),
                c=pl.BlockSpec((None, 2 * cb, nch), lambda j, kk: (j, 0, 0)))


def _ssm_ends(a_lay, x, blocks, *, transpose, reverse, tt, name):
    rows = x.shape[0]
    nblk = blocks.shape[0]
    nch = x.shape[1] // nblk
    cb = SCAN_CB
    nt = rows // tt
    sp = _ssm_specs(nt, tt, nch, reverse)

    def body(a_ref, x_ref, xn_ref, w_ref, e_ref, even_ref, odd_ref):
        kk = pl.program_id(1)

        def product(src_ref, dst_ref):
            dst_ref[...] = _dot(src_ref[...].astype(BF16), w_ref[...].astype(BF16), 1, 1 if transpose else 0)

        @pl.when(kk == 0)
        def _():
            e_ref[...] = jnp.zeros_like(e_ref)
            product(x_ref, even_ref)

        def phase(cur_ref, next_ref):
            product(xn_ref, next_ref)
            sr, si = _scan_groups(a_ref, cur_ref, None, (e_ref[:, :cb], e_ref[:, cb:]), reverse=reverse, tt=tt, unroll=True)
            e_ref[...] = jnp.concatenate([sr, si], axis=1)

        @pl.when(kk % 2 == 0)
        def _():
            phase(even_ref, odd_ref)

        @pl.when(kk % 2 == 1)
        def _():
            phase(odd_ref, even_ref)

    return pl.pallas_call(
        body, name=name, grid=(nblk, nt),
        in_specs=[sp["a"], sp["chan"], sp["next"], sp["c"] if transpose else sp["b"]],
        out_specs=sp["seg"],
        out_shape=jax.ShapeDtypeStruct((SCAN_SEGS, nblk * 2 * cb), F32),
        scratch_shapes=[pltpu.VMEM((tt, 2 * cb), F32), pltpu.VMEM((tt, 2 * cb), F32)],
        compiler_params=_params(("parallel", "arbitrary")),
    )(a_lay, x, x, blocks)


def _ssm_fwd(a_lay, u, b_blk, c_blk, ends, *, tt, name):
    rows = u.shape[0]
    nblk = b_blk.shape[0]
    nch = u.shape[1] // nblk
    cb = SCAN_CB
    nt = rows // tt
    sp = _ssm_specs(nt, tt, nch, False)

    def body(a_ref, e_ref, u_ref, un_ref, b_ref, c_ref, s_ref, y_ref, init_ref, carry_ref, even_ref, odd_ref):
        kk = pl.program_id(1)

        def product(src_ref, dst_ref):
            dst_ref[...] = _dot(src_ref[...].astype(BF16), b_ref[...].astype(BF16), 1, 0)

        @pl.when(kk == 0)
        def _():
            _segment_entries(a_ref, e_ref, init_ref, reverse=False, seg_len=rows // SCAN_SEGS)
            carry_ref[...] = init_ref[...]
            product(u_ref, even_ref)

        def phase(cur_ref, next_ref):
            product(un_ref, next_ref)
            sr, si = _scan_groups(a_ref, cur_ref, cur_ref, (carry_ref[:, :cb], carry_ref[:, cb:]), reverse=False, tt=tt,
                                  unroll=True)
            carry_ref[...] = jnp.concatenate([sr, si], axis=1)
            sb = cur_ref[...].astype(BF16)
            s_ref[...] = sb
            y_ref[...] = _dot(sb, c_ref[...].astype(BF16), 1, 0)

        @pl.when(kk % 2 == 0)
        def _():
            phase(even_ref, odd_ref)

        @pl.when(kk % 2 == 1)
        def _():
            phase(odd_ref, even_ref)

    return pl.pallas_call(
        body, name=name, grid=(nblk, nt),
        in_specs=[sp["a"], sp["seg"], sp["chan"], sp["next"], sp["b"], sp["c"]],
        out_specs=[sp["state"], sp["chan"], sp["seg"]],
        out_shape=[jax.ShapeDtypeStruct((rows, nblk * 2 * cb), BF16), jax.ShapeDtypeStruct((rows, nblk * nch), F32),
                   jax.ShapeDtypeStruct((SCAN_SEGS, nblk * 2 * cb), F32)],
        scratch_shapes=[pltpu.VMEM((SCAN_SEGS, 2 * cb), F32), pltpu.VMEM((tt, 2 * cb), F32), pltpu.VMEM((tt, 2 * cb), F32)],
        compiler_params=_params(("parallel", "arbitrary")),
    )(a_lay, ends, u, u, b_blk, c_blk)


def _ssm_bwd(a_conj, dy, u, s, s_entry, b_blk, c_blk, dd, ends, *, tt, name):
    rows = u.shape[0]
    nblk = b_blk.shape[0]
    nch = u.shape[1] // nblk
    cb = SCAN_CB
    nt = rows // tt
    sp = _ssm_specs(nt, tt, nch, True)
    groups_per_tile = tt // SCAN_SEGS
    before = pl.BlockSpec((2 * SCAN_SEGS, 2 * cb),
                          lambda j, kk: (jnp.maximum((nt - 1 - kk) * (groups_per_tile // 2) - 1, 0), j))

    def body(a_ref, e_ref, dy_ref, dyn_ref, u_ref, s_ref, before_ref, entry_ref, b_ref, c_ref, dd_ref,
             du_ref, db_ref, dc_ref, da_ref, carry_ref, even_ref, odd_ref):
        kk = pl.program_id(1)

        def product(src_ref, dst_ref):
            dst_ref[...] = _dot(src_ref[...].astype(BF16), c_ref[...].astype(BF16), 1, 1)

        @pl.when(kk == 0)
        def _():
            _segment_entries(a_ref, e_ref, carry_ref, reverse=True, seg_len=rows // SCAN_SEGS)
            db_ref[...] = jnp.zeros_like(db_ref)
            dc_ref[...] = jnp.zeros_like(dc_ref)
            da_ref[...] = jnp.zeros_like(da_ref)
            product(dy_ref, even_ref)

        def pair(lv, pv):
            lre, lim, pre, pim = lv[:, :cb], lv[:, cb:], pv[:, :cb], pv[:, cb:]
            return (jnp.sum(lre * pre + lim * pim, axis=0, keepdims=True),
                    jnp.sum(lim * pre - lre * pim, axis=0, keepdims=True))

        def phase(lam_ref, next_ref):
            product(dyn_ref, next_ref)
            lr, li = _scan_groups(a_ref, lam_ref, lam_ref, (carry_ref[:, :cb], carry_ref[:, cb:]), reverse=True, tt=tt,
                                  unroll=True)
            carry_ref[...] = jnp.concatenate([lr, li], axis=1)
            first = jnp.where(kk == nt - 1, entry_ref[...], before_ref[...].astype(F32)[SCAN_SEGS:])
            rest = tt - SCAN_SEGS
            r1, i1 = pair(lam_ref[pl.ds(SCAN_SEGS, rest), :], s_ref[...].astype(F32)[:rest])
            r0, i0 = pair(lam_ref[pl.ds(0, SCAN_SEGS), :], first)
            da_ref[...] += jnp.concatenate([r1 + r0, i1 + i0], axis=1)
            dyv = dy_ref[...]
            lamb = lam_ref[...].astype(BF16)
            du_ref[...] = _dot(lamb, b_ref[...].astype(BF16), 1, 1) + dd_ref[...] * dyv
            db_ref[...] += _dot(u_ref[...].astype(BF16), lamb, 0, 0)
            dc_ref[...] += _dot(s_ref[...], dyv.astype(BF16), 0, 0)

        @pl.when(kk % 2 == 0)
        def _():
            phase(even_ref, odd_ref)

        @pl.when(kk % 2 == 1)
        def _():
            phase(odd_ref, even_ref)

    return pl.pallas_call(
        body, name=name, grid=(nblk, nt),
        in_specs=[sp["a"], sp["seg"], sp["chan"], sp["next"], sp["chan"], sp["state"], before, sp["seg"], sp["b"], sp["c"],
                  pl.BlockSpec((1, nch), lambda j, kk: (0, j))],
        out_specs=[sp["chan"], sp["b"], sp["c"], pl.BlockSpec((1, 2 * cb), lambda j, kk: (0, j))],
        out_shape=[jax.ShapeDtypeStruct((rows, nblk * nch), F32), jax.ShapeDtypeStruct(b_blk.shape, F32),
                   jax.ShapeDtypeStruct(c_blk.shape, F32), jax.ShapeDtypeStruct((1, nblk * 2 * cb), F32)],
        scratch_shapes=[pltpu.VMEM((SCAN_SEGS, 2 * cb), F32), pltpu.VMEM((tt, 2 * cb), F32), pltpu.VMEM((tt, 2 * cb), F32)],
        compiler_params=_params(("parallel", "arbitrary")),
    )(a_conj, ends, dy, dy, u, s, s, s_entry, b_blk, c_blk, dd)


def _glu_fwd(ys, u, dd, w_glu, b_glu, *, tm, name):
    rows, w = ys.shape

    def body(ys_ref, u_ref, dd_ref, w_ref, b_ref, y0_ref, t_ref, y2_ref):
        y0 = ys_ref[...] + dd_ref[...] * u_ref[...]
        y1 = _gelu(y0)
        t = _dot(y1.astype(BF16), w_ref[...], 1, 0) + b_ref[...]
        y0_ref[...] = y0
        t_ref[...] = t
        y2_ref[...] = (y1 * _sigmoid(t)).astype(BF16)

    row = pl.BlockSpec((tm, w), lambda i: (i, 0))
    vec = pl.BlockSpec((1, w), lambda i: (0, 0))
    return pl.pallas_call(
        body, name=name, grid=(rows // tm,),
        in_specs=[row, row, vec, pl.BlockSpec((w, w), lambda i: (0, 0)), vec],
        out_specs=[row, row, row],
        out_shape=[jax.ShapeDtypeStruct((rows, w), F32), jax.ShapeDtypeStruct((rows, w), F32),
                   jax.ShapeDtypeStruct((rows, w), BF16)],
        compiler_params=_params(("parallel",)),
    )(ys, u, dd, w_glu, b_glu)


def _glu_bwd(dy2, y0, t, u, w_glu, *, tm, name):
    rows, w = y0.shape

    def body(dy2_ref, y0_ref, t_ref, u_ref, w_ref, dy0_ref, dt_ref, y1_ref, db_ref, dd_ref):
        i = pl.program_id(0)
        y0 = y0_ref[...]
        y1 = _gelu(y0)
        sg = _sigmoid(t_ref[...])
        dy2v = dy2_ref[...]
        dt = dy2v * y1 * sg * (1.0 - sg)
        dy1 = dy2v * sg + _dot(dt.astype(BF16), w_ref[...], 1, 1)
        dy0 = dy1 * _gelu_grad(y0)
        dy0_ref[...] = dy0
        dt_ref[...] = dt.astype(BF16)
        y1_ref[...] = y1.astype(BF16)

        @pl.when(i == 0)
        def _():
            db_ref[...] = jnp.zeros_like(db_ref)
            dd_ref[...] = jnp.zeros_like(dd_ref)

        db_ref[...] += jnp.sum(dt, axis=0, keepdims=True)
        dd_ref[...] += jnp.sum(dy0 * u_ref[...], axis=0, keepdims=True)

    row = pl.BlockSpec((tm, w), lambda i: (i, 0))
    vec = pl.BlockSpec((1, w), lambda i: (0, 0))
    return pl.pallas_call(
        body, name=name, grid=(rows // tm,),
        in_specs=[row, row, row, row, pl.BlockSpec((w, w), lambda i: (0, 0))],
        out_specs=[row, row, row, vec, vec],
        out_shape=[jax.ShapeDtypeStruct((rows, w), F32), jax.ShapeDtypeStruct((rows, w), BF16),
                   jax.ShapeDtypeStruct((rows, w), BF16), jax.ShapeDtypeStruct((1, w), F32),
                   jax.ShapeDtypeStruct((1, w), F32)],
        compiler_params=_params(("arbitrary",)),
    )(dy2, y0, t, u, w_glu)


ATTN_TILE = 2048


def _attn_geometry(rows, d):
    sb = ATTN_Q * d
    tr = max(sb, min(ATTN_TILE, rows))
    assert rows % tr == 0 and tr % sb == 0, (rows, d)
    return sb, tr, rows // tr, tr // sb


def _attn_masks():
    qi = lax.broadcasted_iota(jnp.int32, (2 * ATTN_Q, 2 * ATTN_Q), 0) % ATTN_Q
    kj = lax.broadcasted_iota(jnp.int32, (2 * ATTN_Q, 2 * ATTN_Q), 1)
    own_ok = jnp.logical_and(kj >= ATTN_Q, kj - ATTN_Q <= qi)
    prev_ok = jnp.logical_and(kj < ATTN_Q, kj >= qi)
    bias_first = jnp.where(own_ok, 0.0, NEG_INF)
    bias_other = jnp.where(jnp.logical_or(own_ok, prev_ok), 0.0, NEG_INF)
    head0 = lax.broadcasted_iota(jnp.int32, (ATTN_Q, LANES), 1) < ATTN_HEAD_DIM
    return bias_first, bias_other, head0


def _attn_rows(base, n, d):
    return pl.ds(pl.multiple_of(base, ATTN_Q), n) if d == 1 else pl.ds(base, n, stride=d)


def _stack_heads(v, head0):
    return jnp.concatenate([jnp.where(head0, v, 0.0), jnp.where(head0, 0.0, v)], axis=0)


def _unstack_heads(v, head0):
    return jnp.where(head0, v[:ATTN_Q], v[ATTN_Q:])


def _fill_keys(buf, prev_ref, cur_ref, sb):
    buf[pl.ds(0, sb), :] = prev_ref[...]
    buf[pl.ds(sb, cur_ref.shape[0]), :] = cur_ref[...]


def _attn_fwd(qkv, g, d, *, name):
    rows = qkv.shape[0]
    sb, tr, ntiles, nsub = _attn_geometry(rows, d)
    qc, kc, vc = 2 * g, 6 + 2 * g, 12 + 2 * g
    scale = ATTN_HEAD_DIM ** -0.5

    def body(q_ref, kc_ref, kp_ref, vc_ref, vp_ref, o_ref, lse_ref, kbuf, vbuf):
        n = pl.program_id(0)
        _fill_keys(kbuf, kp_ref, kc_ref, sb)
        _fill_keys(vbuf, vp_ref, vc_ref, sb)
        bias_first, bias_other, head0 = _attn_masks()

        def per_block(idx, carry):
            j, r = idx // d, idx % d
            base = j * sb + r
            bias = jnp.where(jnp.logical_and(n == 0, j == 0), bias_first, bias_other)
            qrows = _attn_rows(base, ATTN_Q, d)
            krows = _attn_rows(base, 2 * ATTN_Q, d)
            qs = (_stack_heads(q_ref[qrows, :], head0) * scale).astype(BF16)
            s = _dot(qs, kbuf[krows, :].astype(BF16), 1, 1) + bias
            mx = jnp.max(s, axis=-1, keepdims=True)
            p = jnp.exp(s - mx)
            den = jnp.sum(p, axis=-1, keepdims=True)
            pv = _dot(p.astype(BF16), vbuf[krows, :].astype(BF16), 1, 0) / den
            o_ref[qrows, :] = _unstack_heads(pv, head0)
            lse_ref[qrows, :] = _unstack_heads(jnp.broadcast_to(mx + jnp.log(den), (2 * ATTN_Q, LANES)), head0)
            return carry

        lax.fori_loop(0, nsub * d, per_block, 0, unroll=True)

    def cur(col):
        return pl.BlockSpec((tr, LANES), lambda n, hp: (n, col + hp))

    def prev(col):
        return pl.BlockSpec((sb, LANES), lambda n, hp: (jnp.maximum(n * nsub - 1, 0), col + hp))

    out_spec = pl.BlockSpec((tr, LANES), lambda n, hp: (n, hp))
    return pl.pallas_call(
        body, name=name, grid=(ntiles, 2),
        in_specs=[cur(qc), cur(kc), prev(kc), cur(vc), prev(vc)],
        out_specs=[out_spec, out_spec],
        out_shape=[jax.ShapeDtypeStruct((rows, 2 * LANES), F32), jax.ShapeDtypeStruct((rows, 2 * LANES), F32)],
        scratch_shapes=[pltpu.VMEM((sb + tr, LANES), F32), pltpu.VMEM((sb + tr, LANES), F32)],
        compiler_params=_params(("parallel", "parallel")),
    )(qkv, qkv, qkv, qkv, qkv)


def _attn_merge(outs, lses, *, tm, name):
    rows, w = outs[0].shape

    def body(o0, o1, o2, l0, l1, l2, o_ref, lse_ref):
        a0, a1, a2 = l0[...], l1[...], l2[...]
        mx = jnp.maximum(jnp.maximum(a0, a1), a2)
        e0, e1, e2 = jnp.exp(a0 - mx), jnp.exp(a1 - mx), jnp.exp(a2 - mx)
        den = e0 + e1 + e2
        o_ref[...] = (e0 / den) * o0[...] + (e1 / den) * o1[...] + (e2 / den) * o2[...]
        lse_ref[...] = mx + jnp.log(den)

    row = pl.BlockSpec((tm, w), lambda i: (i, 0))
    return pl.pallas_call(
        body, name=name, grid=(rows // tm,), in_specs=[row] * 6, out_specs=[row, row],
        out_shape=[jax.ShapeDtypeStruct((rows, w), F32), jax.ShapeDtypeStruct((rows, w), F32)],
        compiler_params=_params(("parallel",)),
    )(*outs, *lses)


def _attn_bwd(qkv, do, o, lse, g, d, prev, *, name):
    rows = qkv.shape[0]
    sb, tr, ntiles, nsub = _attn_geometry(rows, d)
    qc, kc, vc = 2 * g, 6 + 2 * g, 12 + 2 * g
    scale = ATTN_HEAD_DIM ** -0.5

    def body(q_ref, kc_ref, kp_ref, vc_ref, vp_ref, do_ref, o_ref, lse_ref, dq_ref, dk_ref, dv_ref,
             kbuf, vbuf, dk_acc, dv_acc):
        n = pl.program_id(1)

        @pl.when(n == 0)
        def _():
            dk_acc[pl.ds(0, tr), :] = jnp.zeros((tr, LANES), F32)
            dv_acc[pl.ds(0, tr), :] = jnp.zeros((tr, LANES), F32)

        @pl.when(n < ntiles)
        def _():
            dk_acc[pl.ds(tr, tr), :] = jnp.zeros((tr, LANES), F32)
            dv_acc[pl.ds(tr, tr), :] = jnp.zeros((tr, LANES), F32)
            _fill_keys(kbuf, kp_ref, kc_ref, sb)
            _fill_keys(vbuf, vp_ref, vc_ref, sb)
            bias_first, bias_other, head0 = _attn_masks()
            lane = lax.broadcasted_iota(jnp.int32, (ATTN_Q, LANES), 1)

            def per_block(idx, carry):
                j, r = idx // d, idx % d
                base = j * sb + r
                bias = jnp.where(jnp.logical_and(n == 0, j == 0), bias_first, bias_other)
                qrows = _attn_rows(base, ATTN_Q, d)
                krows = _attn_rows(base, 2 * ATTN_Q, d)
                arows = _attn_rows(base + (tr - sb), 2 * ATTN_Q, d)
                qs = (_stack_heads(q_ref[qrows, :], head0) * scale).astype(BF16)
                dos = _stack_heads(do_ref[qrows, :], head0)
                dosb = dos.astype(BF16)
                ov = o_ref[qrows, :]
                delta = jnp.sum(dos * jnp.concatenate([ov, ov], axis=0), axis=-1, keepdims=True)
                lsev = lse_ref[qrows, :]
                lse_s = jnp.concatenate(
                    [jnp.sum(jnp.where(lane == h * ATTN_HEAD_DIM, lsev, 0.0), axis=-1, keepdims=True) for h in range(2)], axis=0)
                kb = kbuf[krows, :].astype(BF16)
                vb = vbuf[krows, :].astype(BF16)
                p = jnp.exp(_dot(qs, kb, 1, 1) + bias - lse_s)
                ds = (p * (_dot(dosb, vb, 1, 1) - delta)).astype(BF16)
                dq_ref[qrows, :] = _unstack_heads(_dot(ds, kb, 1, 0), head0) * scale
                dk_acc[arows, :] += _dot(ds, qs, 0, 0)
                dv_acc[arows, :] += _dot(p.astype(BF16), dosb, 0, 0)
                return carry

            lax.fori_loop(0, nsub * d, per_block, 0, unroll=True)

        dk_ref[...] = dk_acc[pl.ds(0, tr), :]
        dv_ref[...] = dv_acc[pl.ds(0, tr), :]
        dk_acc[pl.ds(0, tr), :] = dk_acc[pl.ds(tr, tr), :]
        dv_acc[pl.ds(0, tr), :] = dv_acc[pl.ds(tr, tr), :]

    def cur(n):
        return jnp.minimum(n, ntiles - 1)

    def spec(col, prev):
        if prev:
            return pl.BlockSpec((sb, LANES), lambda hp, n: (jnp.maximum(cur(n) * nsub - 1, 0), col + hp))
        return pl.BlockSpec((tr, LANES), lambda hp, n: (cur(n), col + hp))

    row_spec = pl.BlockSpec((tr, LANES), lambda hp, n: (cur(n), hp))
    dq_out = pl.BlockSpec((tr, LANES), lambda hp, n: (cur(n), 2 * g + hp))
    kv_out = pl.BlockSpec((tr, LANES), lambda hp, n: (jnp.maximum(n - 1, 0), 2 * g + hp))
    shape = jax.ShapeDtypeStruct((rows, len(ATTN_PATTERNS) * 2 * LANES), F32)
    ins = [qkv, qkv, qkv, qkv, qkv, do, o, lse]
    in_specs = [spec(qc, False), spec(kc, False), spec(kc, True), spec(vc, False), spec(vc, True),
                row_spec, row_spec, row_spec]
    aliases = {}
    if prev is not None:
        aliases = {len(ins) + t: t for t in range(3)}
        ins = ins + list(prev)
        in_specs = in_specs + [ANY] * 3
    n_in = len(ins)

    def entry(*refs):
        body(*refs[:8], *refs[n_in:])

    return pl.pallas_call(
        entry, name=name, grid=(2, ntiles + 1),
        in_specs=in_specs,
        out_specs=[dq_out, kv_out, kv_out],
        out_shape=[shape, shape, shape],
        input_output_aliases=aliases,
        scratch_shapes=[pltpu.VMEM((sb + tr, LANES), F32), pltpu.VMEM((sb + tr, LANES), F32),
                        pltpu.VMEM((2 * tr, LANES), F32), pltpu.VMEM((2 * tr, LANES), F32)],
        compiler_params=_params(("parallel", "arbitrary")),
    )(*ins)


def _mem_probs(q, k):
    s = _dot(q.astype(BF16), k.astype(BF16), 1, 1) * (MEM_HEAD_DIM ** -0.5)
    e = jnp.exp(s - jnp.max(s, axis=-1, keepdims=True))
    return e / jnp.sum(e, axis=-1, keepdims=True)


def _mem_attn_fwd(mq, kv, *, tq, name):
    rows = mq.shape[0]

    def body(q_ref, k_ref, v_ref, o_ref):
        p = _mem_probs(q_ref[...], k_ref[...])
        o_ref[...] = _dot(p.astype(BF16), v_ref[...].astype(BF16), 1, 0)

    return pl.pallas_call(
        body, name=name, grid=(rows // tq, MEM_HEADS),
        in_specs=[pl.BlockSpec((tq, LANES), lambda i, h: (i, h)),
                  pl.BlockSpec((MEM_LEN, LANES), lambda i, h: (0, h)),
                  pl.BlockSpec((MEM_LEN, LANES), lambda i, h: (0, MEM_HEADS + h))],
        out_specs=pl.BlockSpec((tq, LANES), lambda i, h: (i, h)),
        out_shape=jax.ShapeDtypeStruct((rows, MEM_HEADS * LANES), F32),
        compiler_params=_params(("parallel", "parallel")),
    )(mq, kv, kv)


def _mem_attn_bwd(mq, kv, dmo, *, tq, name):
    rows = mq.shape[0]
    scale = MEM_HEAD_DIM ** -0.5

    def body(q_ref, k_ref, v_ref, do_ref, dq_ref, dk_ref, dv_ref):
        i = pl.program_id(1)
        qb = q_ref[...].astype(BF16)
        kb = k_ref[...].astype(BF16)
        vb = v_ref[...].astype(BF16)
        dob = do_ref[...].astype(BF16)
        p = _mem_probs(q_ref[...], k_ref[...])
        dp = _dot(dob, vb, 1, 1)
        ds = (p * (dp - jnp.sum(p * dp, axis=-1, keepdims=True)) * scale).astype(BF16)
        dq_ref[...] = _dot(ds, kb, 1, 0).astype(dq_ref.dtype)

        @pl.when(i == 0)
        def _():
            dk_ref[...] = jnp.zeros_like(dk_ref)
            dv_ref[...] = jnp.zeros_like(dv_ref)

        dk_ref[...] += _dot(ds, qb, 0, 0)
        dv_ref[...] += _dot(p.astype(BF16), dob, 0, 0)

    kv_out = pl.BlockSpec((MEM_LEN, LANES), lambda h, i: (0, h))
    kv_shape = jax.ShapeDtypeStruct((MEM_LEN, MEM_HEADS * LANES), F32)
    return pl.pallas_call(
        body, name=name, grid=(MEM_HEADS, rows // tq),
        in_specs=[pl.BlockSpec((tq, LANES), lambda h, i: (i, h)),
                  pl.BlockSpec((MEM_LEN, LANES), lambda h, i: (0, h)),
                  pl.BlockSpec((MEM_LEN, LANES), lambda h, i: (0, MEM_HEADS + h)),
                  pl.BlockSpec((tq, LANES), lambda h, i: (i, h))],
        out_specs=[pl.BlockSpec((tq, LANES), lambda h, i: (i, h)), kv_out, kv_out],
        out_shape=[jax.ShapeDtypeStruct((rows, MEM_HEADS * LANES), BF16), kv_shape, kv_shape],
        compiler_params=_params(("parallel", "arbitrary")),
    )(mq, kv, kv, dmo)


def _resident(shape):
    return pl.BlockSpec(shape, lambda i: (0, 0), pipeline_mode=pl.Buffered(1))


def _branch_merge_fwd(acts, wts, zg, b_gate, *, tm, name):
    rows = zg.shape[0]
    d = wts[0].shape[0]

    def body(s_ref, a_ref, m_ref, ws_ref, wa_ref, wm_ref, zg_ref, b_ref, o_ref):
        gt = _sigmoid(zg_ref[...] + b_ref[...])
        acc = None
        for k, (x_ref, w_ref) in enumerate(((s_ref, ws_ref), (a_ref, wa_ref), (m_ref, wm_ref))):
            term = gt[:, k * d:(k + 1) * d] * _dot(x_ref[...].astype(BF16), w_ref[...], 1, 1)
            acc = term if acc is None else acc + term
        o_ref[...] = acc.astype(BF16)

    return pl.pallas_call(
        body, name=name, grid=(rows // tm,),
        in_specs=[pl.BlockSpec((tm, x.shape[1]), lambda i: (i, 0)) for x in acts] + [_resident(w.shape) for w in wts]
        + [pl.BlockSpec((tm, 3 * d), lambda i: (i, 0)), pl.BlockSpec((1, 3 * d), lambda i: (0, 0))],
        out_specs=pl.BlockSpec((tm, d), lambda i: (i, 0)), out_shape=jax.ShapeDtypeStruct((rows, d), BF16),
        compiler_params=_params(("parallel",)),
    )(*acts, *wts, zg, b_gate)


def _branch_merge_bwd(dmerged, acts, wts, zg, b_gate, *, tm, name, carry=None):
    rows = zg.shape[0]
    d = wts[0].shape[0]

    def body(dm_ref, s_ref, a_ref, m_ref, ws_ref, wa_ref, wm_ref, zg_ref, b_ref,
             ds_ref, da_ref, dmm_ref, dws_ref, dwa_ref, dwm_ref, dzg_ref, db_ref):
        i = pl.program_id(0)

        @pl.when(i == 0)
        def _():
            for r in (dws_ref, dwa_ref, dwm_ref, db_ref):
                r[...] = jnp.zeros_like(r)

        gt = _sigmoid(zg_ref[...] + b_ref[...])
        dm = dm_ref[...]
        groups = ((s_ref, ws_ref, ds_ref, dws_ref), (a_ref, wa_ref, da_ref, dwa_ref), (m_ref, wm_ref, dmm_ref, dwm_ref))
        for k, (x_ref, w_ref, dx_ref, dw_ref) in enumerate(groups):
            cs = pl.ds(k * d, d)
            gk = gt[:, k * d:(k + 1) * d]
            xb = x_ref[...].astype(BF16)
            br = _dot(xb, w_ref[...], 1, 1)
            dbr = (dm * gk).astype(BF16)
            dx_ref[...] = _dot(dbr, w_ref[...], 1, 0)
            dw_ref[...] += _dot(dbr, xb, 0, 0)
            dzg = dm * br * gk * (1.0 - gk)
            dzg_ref[:, cs] = dzg.astype(BF16)
            db_ref[:, cs] += jnp.sum(dzg, axis=0, keepdims=True)

    row = lambda w: pl.BlockSpec((tm, w), lambda i: (i, 0))
    whole = lambda shape: pl.BlockSpec(shape, lambda i: (0, 0))
    res = _call_with_carry(
        body, carry, name=name, grid=(rows // tm,),
        in_specs=[row(d)] + [row(x.shape[1]) for x in acts] + [_resident(w.shape) for w in wts] + [row(3 * d), whole((1, 3 * d))],
        out_specs=[row(x.shape[1]) for x in acts] + [whole(w.shape) for w in wts] + [row(3 * d), whole((1, 3 * d))],
        out_shape=[jax.ShapeDtypeStruct(x.shape, F32) for x in acts] + [jax.ShapeDtypeStruct(w.shape, F32) for w in wts]
        + [jax.ShapeDtypeStruct((rows, 3 * d), BF16), jax.ShapeDtypeStruct((1, 3 * d), F32)],
        scratch=[], operands=[dmerged, *acts, *wts, zg, b_gate], semantics=("arbitrary",))
    return tuple(res) if carry is None else (tuple(res[:8]), list(res[8:]))


def _adamw(w, g, m, v, *, tr, name):
    rows, cols = w.shape[-2:]
    assert rows % tr == 0, (name, rows, tr)

    def body(w_ref, g_ref, m_ref, v_ref, g_out, d_ref, nm_ref, nv_ref):
        gv = g_ref[...]
        m2 = ADAM_B1 * m_ref[...] + (1.0 - ADAM_B1) * gv
        v2 = ADAM_B2 * v_ref[...] + (1.0 - ADAM_B2) * (gv * gv)
        m_hat = m2 / (1.0 - ADAM_B1 ** ADAM_STEP)
        v_hat = v2 / (1.0 - ADAM_B2 ** ADAM_STEP)
        g_out[...] = gv
        d_ref[...] = -ADAM_LR * (m_hat / (jnp.sqrt(v_hat) + ADAM_EPS) + ADAM_WD * w_ref[...])
        nm_ref[...] = m2
        nv_ref[...] = v2

    flat = pl.BlockSpec((tr, cols), lambda i: (i, 0))
    blk = flat if w.ndim == 2 else pl.BlockSpec((None, tr, cols), lambda i: (0, i, 0))
    shape = jax.ShapeDtypeStruct(w.shape, F32)
    return pl.pallas_call(
        body, name=name, grid=(rows // tr,), in_specs=[blk, flat, blk, blk], out_specs=[blk] * 4,
        out_shape=[shape] * 4, compiler_params=_params(("parallel",)),
    )(w, g, m, v)


ANY = pl.BlockSpec(memory_space=pl.ANY)


def _position():
    return lax.axis_index("x"), lax.axis_index("y"), lax.axis_index("c")


def _other_chips(x, y):
    return ((1 - x, y), (x, 1 - y), (1 - x, 1 - y))


def _remote(src, dst, send_sem, recv_sem, dev):
    return pltpu.make_async_remote_copy(src_ref=src, dst_ref=dst, send_sem=send_sem, recv_sem=recv_sem,
                                        device_id=dev, device_id_type=MESH)


def _gather_exchange(shards):
    nb = len(shards)

    def rows_of(i, owner, core):
        rs = shards[i].shape[0]
        return pl.ds(pl.multiple_of(owner * rs + core * (rs // 2), 16), rs // 2)

    def first_leg(ins, outs, send_sems, recv_sems, i, j):
        x, y, c = _position()
        px, py = _other_chips(x, y)[j]
        half = shards[i].shape[0] // 2
        mine = ins[i].at[pl.ds(pl.multiple_of(c * half, 16), half)]
        return _remote(mine, outs[i].at[rows_of(i, 2 * x + y, c)], send_sems.at[i, j], recv_sems.at[i, j], (px, py, c))

    def passed_on(outs, send_sems, recv_sems, i, j, core):
        x, y, c = _position()
        px, py = _other_chips(x, y)[j]
        rows = outs[i].at[rows_of(i, 2 * px + py, core)]
        return _remote(rows, rows, send_sems.at[i, 3 + j], recv_sems.at[i, 3 + j], (x, y, 1 - c))

    def own_block(ins, outs, send_sems, recv_sems, i):
        x, y, c = _position()
        rs = shards[i].shape[0]
        place = outs[i].at[pl.ds(pl.multiple_of((2 * x + y) * rs, 16), rs)]
        return _remote(ins[i], place, send_sems.at[i, 6], recv_sems.at[i, 6], (x, y, 1 - c))

    def start(ins, outs, send_sems, recv_sems):
        for i in range(nb):
            own_block(ins, outs, send_sems, recv_sems, i).start()
            for j in range(3):
                first_leg(ins, outs, send_sems, recv_sems, i, j).start()

    def finish(ins, outs, send_sems, recv_sems):
        x, y, c = _position()
        for i in range(nb):
            for j, (px, py) in enumerate(_other_chips(x, y)):
                landed = outs[i].at[rows_of(i, 2 * px + py, c)]
                _remote(landed, landed, send_sems.at[i, j], recv_sems.at[i, j], (px, py, c)).wait_recv()
                passed_on(outs, send_sems, recv_sems, i, j, c).start()
        for i in range(nb):
            own_block(ins, outs, send_sems, recv_sems, i).wait()
            for j in range(3):
                passed_on(outs, send_sems, recv_sems, i, j, 1 - c).wait_recv()
        for i in range(nb):
            for j in range(3):
                first_leg(ins, outs, send_sems, recv_sems, i, j).wait_send()
                passed_on(outs, send_sems, recv_sems, i, j, c).wait_send()

    return _Exchange(ins=list(shards), outs=[jax.ShapeDtypeStruct((N_CHIPS * s.shape[0], s.shape[1]), s.dtype) for s in shards],
                     aliases={}, sems=[(nb, 7), (nb, 7)], start=start, finish=finish)


def _run_exchange(ex, *, name):
    n_in, n_out = len(ex.ins), len(ex.outs)

    def body(*refs):
        c_in, c_out, sems = refs[:n_in], refs[n_in:n_in + n_out], refs[n_in + n_out:]
        ex.start(c_in, c_out, *sems)
        ex.finish(c_in, c_out, *sems)

    return pl.pallas_call(
        body, name=name, in_specs=[ANY] * n_in, out_specs=[ANY] * n_out, out_shape=list(ex.outs),
        input_output_aliases=dict(ex.aliases),
        scratch_shapes=[pltpu.SemaphoreType.DMA(s) for s in ex.sems],
    )(*ex.ins)


def _row_tile(rows):
    return max(t for t in range(16, min(rows, 512) + 1, 16) if rows % t == 0)


def _halves_exchange(grads):
    nb = len(grads)

    def copies(ins, outs, send_sems, recv_sems):
        x, y, c = _position()
        return [_remote(ins[i].at[:, 1 - c], outs[i], send_sems.at[i], recv_sems.at[i], (x, y, 1 - c)) for i in range(nb)]

    def start(ins, outs, send_sems, recv_sems):
        for cp in copies(ins, outs, send_sems, recv_sems):
            cp.start()

    def finish(ins, outs, send_sems, recv_sems):
        for cp in copies(ins, outs, send_sems, recv_sems):
            cp.wait()

    return _Exchange(ins=list(grads), outs=[jax.ShapeDtypeStruct((N_CHIPS, g.shape[2], g.shape[3]), F32) for g in grads],
                     aliases={}, sems=[(nb,), (nb,)], start=start, finish=finish)


def _join_exchanges(parts):
    assert all(not ex.aliases for ex in parts)

    def split(refs, counts):
        out, at = [], 0
        for k in counts:
            out.append(refs[at:at + k])
            at += k
        return out

    def run(which):
        def go(ins, outs, *sems):
            for ex, i, o, s in zip(parts, split(ins, [len(ex.ins) for ex in parts]), split(outs, [len(ex.outs) for ex in parts]),
                                   split(sems, [len(ex.sems) for ex in parts])):
                getattr(ex, which)(i, o, *s)
        return go

    return _Exchange(ins=[a for ex in parts for a in ex.ins], outs=[a for ex in parts for a in ex.outs], aliases={},
                     sems=[s for ex in parts for s in ex.sems], start=run("start"), finish=run("finish"))


def _pair_sum(g4, got, c_arr, *, name):
    _, _, half, cols = g4.shape
    tr = _row_tile(half)

    def body(c_ref, g_ref, t_ref, p_ref, pb_ref):
        sm = g_ref[...] + t_ref[...]
        p_ref[...] = sm
        pb_ref[...] = sm.astype(BF16)

    blk = pl.BlockSpec((None, tr, cols), lambda j, i, c_ref: (j, i, 0))
    grid_spec = pltpu.PrefetchScalarGridSpec(
        num_scalar_prefetch=1, grid=(N_CHIPS, half // tr),
        in_specs=[pl.BlockSpec((None, None, tr, cols), lambda j, i, c_ref: (j, c_ref[0], i, 0)), blk],
        out_specs=[blk, blk])
    return pl.pallas_call(
        body, name=name, grid_spec=grid_spec,
        out_shape=[jax.ShapeDtypeStruct((N_CHIPS, half, cols), F32), jax.ShapeDtypeStruct((N_CHIPS, half, cols), BF16)],
        compiler_params=_params(("parallel", "parallel")),
    )(c_arr, g4, got)


def _scatter_exchange(parts):
    nb = len(parts)

    def copies(ins, outs, send_sems, recv_sems):
        x, y, c = _position()
        return [_remote(ins[i].at[2 * px + py], outs[i].at[j], send_sems.at[i, j], recv_sems.at[i, j], (px, py, c))
                for i in range(nb) for j, (px, py) in enumerate(_other_chips(x, y))]

    def start(ins, outs, send_sems, recv_sems):
        for cp in copies(ins, outs, send_sems, recv_sems):
            cp.start()

    def finish(ins, outs, send_sems, recv_sems):
        for cp in copies(ins, outs, send_sems, recv_sems):
            cp.wait()

    return _Exchange(ins=list(parts), outs=[jax.ShapeDtypeStruct((3,) + p.shape[1:], p.dtype) for p in parts],
                     aliases={}, sems=[(nb, 3), (nb, 3)], start=start, finish=finish)


def _owner_sum(p, got, chip_arr, c_arr, *, replicated, name):
    _, half, cols = p.shape
    tr = _row_tile(half)

    def body(chip_ref, c_ref, p_ref, r_ref, o_ref):
        o_ref[...] = ((p_ref[...] + r_ref[0].astype(F32)) + r_ref[1].astype(F32)) + r_ref[2].astype(F32)

    if replicated:
        out_spec = pl.BlockSpec((None, None, tr, cols), lambda i, chip_ref, c_ref: (chip_ref[0], c_ref[0], i, 0))
        out_shape = jax.ShapeDtypeStruct((N_CHIPS, 2, half, cols), F32)
    else:
        out_spec = pl.BlockSpec((None, tr, cols), lambda i, chip_ref, c_ref: (c_ref[0], i, 0))
        out_shape = jax.ShapeDtypeStruct((2, half, cols), F32)
    grid_spec = pltpu.PrefetchScalarGridSpec(
        num_scalar_prefetch=2, grid=(half // tr,),
        in_specs=[pl.BlockSpec((None, tr, cols), lambda i, chip_ref, c_ref: (chip_ref[0], i, 0)),
                  pl.BlockSpec((3, tr, cols), lambda i, chip_ref, c_ref: (0, i, 0))],
        out_specs=out_spec)
    return pl.pallas_call(
        body, name=name, grid_spec=grid_spec, out_shape=out_shape,
        compiler_params=_params(("parallel",)),
    )(chip_arr, c_arr, p, got)


def _share_reduced(bufs):
    nb = len(bufs) - 1

    def body(*refs):
        outs = refs[nb + 1:2 * nb + 2]
        send_sems, recv_sems = refs[2 * nb + 2:]
        x, y, c = _position()
        chip = 2 * x + y
        sends = []
        for i in range(nb):
            cp = _remote(outs[i].at[c], outs[i].at[c], send_sems.at[i], recv_sems.at[i], (x, y, 1 - c))
            cp.start()
            sends.append(cp)
        small = outs[nb]
        peers = [(fx, fy, fc) for fx in (0, 1) for fy in (0, 1) for fc in (0, 1) if fx + fy + fc > 0]
        for k, (fx, fy, fc) in enumerate(peers):
            dev = (x ^ fx, y ^ fy, c ^ fc)
            cp = _remote(small.at[chip, c], small.at[chip, c], send_sems.at[nb + k], recv_sems.at[nb + k], dev)
            cp.start()
            sends.append(cp)
        for i in range(nb):
            dst = outs[i].at[1 - c]
            _remote(dst, dst, send_sems.at[i], recv_sems.at[i], (x, y, 1 - c)).wait_recv()
        for k, (fx, fy, fc) in enumerate(peers):
            dst = small.at[2 * (x ^ fx) + (y ^ fy), c ^ fc]
            _remote(dst, dst, send_sems.at[nb + k], recv_sems.at[nb + k], (x ^ fx, y ^ fy, c ^ fc)).wait_recv()
        for cp in sends:
            cp.wait_send()

    n_all = nb + 1
    return pl.pallas_call(
        body, name="grad_share_reduced", in_specs=[ANY] * n_all, out_specs=[ANY] * n_all,
        out_shape=[jax.ShapeDtypeStruct(b.shape, b.dtype) for b in bufs],
        input_output_aliases={i: i for i in range(n_all)},
        scratch_shapes=[pltpu.SemaphoreType.DMA((nb + 7,)), pltpu.SemaphoreType.DMA((nb + 7,))],
    )(*bufs)


class _GradReducer:
    def __init__(self, c_arr, chip_arr):
        self.c_arr, self.chip_arr = c_arr, chip_arr
        self.full, self.pairs, self.landed = {}, {}, {}

    def swap(self, names, grads):
        for n, g in zip(names, grads):
            self.full[n] = g.reshape(N_CHIPS, 2, g.shape[0] // (2 * N_CHIPS), g.shape[1])
        return _halves_exchange([self.full[n] for n in names])

    def swapped(self, names, bufs):
        for n, t in zip(names, bufs):
            self.pairs[n] = _pair_sum(self.full[n], t, self.c_arr, name="grad_pair_sum_" + n)

    def scatter(self, names):
        return _scatter_exchange([self.pairs[n][1] for n in names])

    def collect(self, names, bufs):
        self.landed.update(zip(names, bufs))

    def swap_now(self, names, grads):
        self.swapped(names, _run_exchange(self.swap(names, grads), name="grad_exchange_" + names[0]))

    def finish(self, names, grads, order):
        self.swap_now(names, grads)
        self.collect(names, _run_exchange(self.scatter(names), name="grad_scatter_" + names[0]))
        totals = [_owner_sum(self.pairs[n][0], self.landed[n], self.chip_arr, self.c_arr, replicated=(n == order[-1]),
                             name="grad_owner_sum_" + n) for n in order]
        return _share_reduced(totals)


def _pack_small(vals):
    flat = jnp.concatenate([vals[name].reshape(-1) for name, _ in SMALL])
    return jnp.pad(flat, (0, N_CHIPS * SMALL_ROWS * 1024 - SMALL_ELEMS)).reshape(N_CHIPS * SMALL_ROWS, 1024)


def _unpack_small(buf):
    flat = buf.reshape(-1)
    out, off = {}, 0
    for name, shape in SMALL:
        n = int(np.prod(shape))
        out[name] = flat[off:off + n].reshape(shape)
        off += n
    return out


EARLY_REDUCED = (("w_down",), ("w_up",), ("w_o", "w_ssm_br", "w_attn_br", "w_mem_br", "w_glu", "w_mem_kv"), ("w_in",))


def _device_step(x, mem, tgt, w, p, *, shards, reducer):
    rows = x.shape[0]
    w = dict(w)
    early = EARLY_REDUCED
    gb = {}
    gather_pending = shards is not None

    def riding(*stages):
        if reducer is None or not stages:
            return None
        return _join_exchanges([reducer.swap(names, [gb[n] for n in names]) if kind == "swap" else reducer.scatter(names)
                                for kind, names in stages])

    def arrived(stages, res):
        if reducer is None or not stages:
            return res
        main, bufs = res
        for kind, names in stages:
            (reducer.swapped if kind == "swap" else reducer.collect)(names, bufs[:len(names)])
            bufs = bufs[len(names):]
        return main

    def fetching(names):
        return _gather_exchange([shards[n] for n in names]) if gather_pending else None

    def fetched(names, res):
        if not gather_pending:
            return res
        w.update(zip(names, res[1]))
        return res[0]

    first_use = (("w_in",), ("w_glu", "w_ssm_br", "w_attn_br", "w_mem_kv", "w_mem_br", "w_o", "w_up"), ("w_down",))
    g1, gm, g2 = p["norm1_g"], p["mem_norm_g"], p["norm2_g"]
    gf = p["final_g"].reshape(1, D_MODEL)
    ssm_args = (p["ssm_lambda_re"][0], p["ssm_lambda_im"][0], p["ssm_log_dt"][0], p["ssm_b_re"][0],
                p["ssm_b_im"][0], p["ssm_c_re"][0], p["ssm_c_im"][0])
    (a_lay, b_blk, c_blk), ssm_vjp = jax.vjp(_ssm_matrices, *ssm_args)
    a_conj = a_lay * _to_scan_layout(jnp.stack([jnp.ones((N_STATES,), F32), -jnp.ones((N_STATES,), F32)]))[None, :]
    dd = p["ssm_d"].reshape(1, SSM_WIDTH)
    mm = _matmul

    n1 = fetched(first_use[0], _rmsnorm_fwd(x, g1, tm=512, carry=fetching(first_use[0]), name="norm1"))
    win_t = w["w_in"]
    splits = ((OFF_U, OFF_QKV - OFF_U), (OFF_QKV, OFF_MQ - OFF_QKV), (OFF_MQ, OFF_ZG - OFF_MQ), (OFF_ZG, IN_WIDTH - OFF_ZG))
    u, qkv, mq, zg = fetched(first_use[1], _split_matmul(n1, win_t, splits, tm=512, carry=fetching(first_use[1]),
                                                         vmem=VMEM_LIMIT_WIDE_BYTES, name="in_proj"))

    u_i = _interleave(u)
    ends = _ssm_ends(a_lay, u_i, b_blk, transpose=False, reverse=False, tt=512, name="ssm_fwd_ends")
    s, ys_i, s_entry = _ssm_fwd(a_lay, u_i, b_blk, c_blk, ends, tt=512, name="ssm_fwd")
    ys = _deinterleave(ys_i)
    y0, tglu, y2 = _glu_fwd(ys, u, dd, w["w_glu"], p["b_glu"], tm=512, name="glu_fwd")

    outs, lses = [], []
    for g, (_, d) in enumerate(ATTN_PATTERNS):
        o_g, lse_g = _attn_fwd(qkv, g, d, name=f"attn_fwd_{g}")
        outs.append(o_g)
        lses.append(lse_g)
    o, lse = _attn_merge(outs, lses, tm=1024, name="attn_merge")

    mn = _rmsnorm_fwd(mem, gm, tm=MEM_LEN, name="mem_norm")
    kv = mm(mn, w["w_mem_kv"], m=MEM_LEN, n=1024, k=1024, tm=MEM_LEN, tn=1024, tk=1024, out_dtypes=(F32,), name="mem_kv")
    mo = _mem_attn_fwd(mq, kv, tq=1024, name="mem_attn_fwd")

    branch_acts = (y2, o, mo)
    branch_wts = (w["w_ssm_br"], w["w_attn_br"], w["w_mem_br"])
    merged = _branch_merge_fwd(branch_acts, branch_wts, zg, p["b_gate"], tm=256, name="branch_merge_fwd")
    h1, n2 = mm(merged, w["w_o"], m=rows, n=1024, k=1024, tm=1024, tn=1024, tk=1024, out_dtypes=(F32, BF16),
                aux=((x, "mn"), (g2, "row")), epilogue=_residual_norm_epilogue, name="out_proj")
    relu2 = lambda acc: (jnp.square(jnp.maximum(acc, 0.0)),)
    act = fetched(first_use[2], _sum_matmul([n2], w["w_up"], [0], tb=True, tm=512, out_dtype=BF16, epilogue=relu2,
                                            carry=fetching(first_use[2]), name="mlp_up"))
    dh2, d_gf, sq_err = _sum_matmul([act], w["w_down"], [0], tm=512, aux=((h1, "mn"), (tgt, "mn"), (gf, "row")),
                                    epilogue=_loss_head_epilogue, n_sums=2, name="mlp_down")
    loss = (0.5 / D_MODEL) * jnp.sum(sq_err)

    gs = {"final_g": d_gf.reshape(D_MODEL)}
    drelu2 = lambda acc, actv: (acc * (2.0 * jnp.sqrt(actv.astype(F32))),)
    dup = mm(dh2, w["w_down"], m=rows, n=D_FF, k=1024, tb=True, tm=1024, tn=2048, tk=1024, out_dtypes=(BF16,),
             aux=((act, "mn"),), epilogue=drelu2, name="d_act")
    gb["w_down"] = mm(act, dh2, m=D_FF, n=1024, k=rows, ta=True, tm=1024, tn=1024, tk=2048, out_dtypes=(F32,), name="dw_down")
    stages = (("swap", early[0]),)
    gb["w_up"] = arrived(stages, mm(dup, n2, m=D_FF, n=1024, k=rows, ta=True, tm=1024, tn=1024, tk=2048,
                                    out_dtypes=(F32,), carry=riding(*stages), name="dw_up"))
    stages = (("scatter", early[0]), ("swap", early[1]))
    dh1, gs["norm2_g"] = arrived(stages, _sum_matmul([dup], w["w_up"], [0], tm=512, aux=((h1, "mn"), (dh2, "mn"), (g2, "row")),
                                                     epilogue=_rmsnorm_bwd_epilogue, n_sums=1, carry=riding(*stages), name="d_n2"))
    dmerged = mm(dh1, w["w_o"], m=rows, n=1024, k=1024, tb=True, tm=1024, tn=1024, tk=1024, out_dtypes=(F32,), name="d_merged")
    gb["w_o"] = mm(merged, dh1, m=1024, n=1024, k=rows, ta=True, tm=1024, tn=1024, tk=2048, out_dtypes=(F32,), name="dw_o")
    stages = (("scatter", early[1]),)
    (dy2, do, dmo, gb["w_ssm_br"], gb["w_attn_br"], gb["w_mem_br"], dzg, gs["b_gate"]) = arrived(stages, _branch_merge_bwd(
        dmerged, branch_acts, branch_wts, zg, p["b_gate"], tm=256, carry=riding(*stages), name="branch_merge_bwd"))

    dy0, dt, y1, gs["b_glu"], d_dd = _glu_bwd(dy2, y0, tglu, u, w["w_glu"], tm=512, name="glu_bwd")
    gs["ssm_d"] = d_dd.reshape(1, SSM_GROUPS, SSM_GROUP_SIZE)
    gb["w_glu"] = mm(y1, dt, m=512, n=512, k=rows, ta=True, tm=512, tn=512, tk=1024, out_dtypes=(F32,), name="dw_glu")
    dy0_i = _interleave(dy0)
    lam_ends = _ssm_ends(a_conj, dy0_i, c_blk, transpose=True, reverse=True, tt=512, name="ssm_bwd_ends")
    du_i, d_b_blk, d_c_blk, d_a_lay = _ssm_bwd(a_conj, dy0_i, u_i, s, s_entry, b_blk, c_blk, dd, lam_ends, tt=512,
                                                name="ssm_bwd")
    du = _deinterleave(du_i)
    d_ssm = ssm_vjp((d_a_lay, d_b_blk, d_c_blk))
    for name, val in zip(("ssm_lambda_re", "ssm_lambda_im", "ssm_log_dt", "ssm_b_re", "ssm_b_im", "ssm_c_re", "ssm_c_im"), d_ssm):
        gs[name] = val[None]

    dqkv = None
    for g, (_, d) in enumerate(ATTN_PATTERNS):
        dqkv = _attn_bwd(qkv, do, o, lse, g, d, dqkv, name=f"attn_bwd_{g}")

    dmq, dmk, dmv = _mem_attn_bwd(mq, kv, dmo, tq=1024, name="mem_attn_bwd")
    dkv = jnp.concatenate([dmk, dmv], axis=1)
    gb["w_mem_kv"] = mm(mn, dkv, m=1024, n=1024, k=MEM_LEN, ta=True, tm=1024, tn=1024, tk=MEM_LEN, out_dtypes=(F32,), name="dw_mem_kv")
    dmn = mm(dkv, w["w_mem_kv"], m=MEM_LEN, n=1024, k=1024, tb=True, tm=MEM_LEN, tn=1024, tk=1024, out_dtypes=(F32,), name="d_mn")
    _, gs["mem_norm_g"] = _rmsnorm_bwd(mem, gm, dmn, None, tm=MEM_LEN, name="mem_norm_bwd")

    pieces = ((du, OFF_U, "u"), (dqkv[0], OFF_QKV, "q"), (dqkv[1], OFF_QKV + 768, "k"), (dqkv[2], OFF_QKV + 1536, "v"),
              (dmq, OFF_MQ, "mq"), (dzg, OFF_ZG, "zg"))
    dw_rows = []
    for piece, off, tag in pieces:
        width = piece.shape[1]
        tmw = 1024 if width % 1024 == 0 else (768 if width == 768 else 512)
        stages = {"q": (("swap", early[2]),), "zg": (("scatter", early[2]),)}.get(tag, ())
        dw_rows.append(arrived(stages, mm(piece, n1, m=width, n=1024, k=rows, ta=True, tm=tmw, tn=1024, tk=2048,
                                          out_dtypes=(F32,), carry=riding(*stages), name="dw_in_" + tag)))
    gb["w_in"] = jnp.concatenate(dw_rows, axis=0)
    if reducer is not None:
        reducer.swap_now(early[3], [gb["w_in"]])
    stages = (("scatter", early[3]),)
    dx, gs["norm1_g"] = arrived(stages, _sum_matmul(
        [piece for piece, _, _ in pieces], win_t, [off for _, off, _ in pieces], tm=512,
        aux=((x, "mn"), (dh1, "mn"), (g1, "row")), epilogue=_rmsnorm_bwd_epilogue, n_sums=1,
        carry=riding(*stages), vmem=VMEM_LIMIT_WIDE_BYTES, name="d_n1"))
    return loss, dx, gb, gs


def kernel(x, mem, norm1_g, mem_norm_g, w_in, b_gate, ssm_lambda_re, ssm_lambda_im, ssm_log_dt, ssm_b_re, ssm_b_im, ssm_c_re, ssm_c_im, ssm_d, w_glu, b_glu, w_ssm_br, w_attn_br, w_mem_kv, w_mem_br, w_o, norm2_g, w_up, w_down, final_g, loss_target, m_norm1_g, m_mem_norm_g, m_w_in, m_b_gate, m_ssm_lambda_re, m_ssm_lambda_im, m_ssm_log_dt, m_ssm_b_re, m_ssm_b_im, m_ssm_c_re, m_ssm_c_im, m_ssm_d, m_w_glu, m_b_glu, m_w_ssm_br, m_w_attn_br, m_w_mem_kv, m_w_mem_br, m_w_o, m_norm2_g, m_w_up, m_w_down, m_final_g, v_norm1_g, v_mem_norm_g, v_w_in, v_b_gate, v_ssm_lambda_re, v_ssm_lambda_im, v_ssm_log_dt, v_ssm_b_re, v_ssm_b_im, v_ssm_c_re, v_ssm_c_im, v_ssm_d, v_w_glu, v_b_glu, v_w_ssm_br, v_w_attn_br, v_w_mem_kv, v_w_mem_br, v_w_o, v_norm2_g, v_w_up, v_w_down, v_final_g):
    env = dict(locals())
    weights = {n: env[n] for n in WEIGHT_ORDER}
    moms = {n: env["m_" + n] for n in WEIGHT_ORDER}
    vels = {n: env["v_" + n] for n in WEIGHT_ORDER}

    chip = 2 * lax.axis_index("x") + lax.axis_index("y")
    wire = [weights[n].reshape(weights[n].shape[-2:]).astype(BF16) for n, _, _ in BIG]
    wire = dict(zip([n for n, _, _ in BIG], [s.T if tr else s for s, (_, tr, _) in zip(wire, BIG)]))
    small = {n: weights[n] for n, _ in SMALL}

    reducer = _GradReducer(lax.axis_index("c").astype(jnp.int32).reshape(1), chip.astype(jnp.int32).reshape(1))
    loss, dx, gb, gs = _device_step(x[0], mem[0], loss_target[0], {}, small, shards=wire, reducer=reducer)
    *shards, small_grad = reducer.finish(["small"], [_pack_small(gs)], [n for n, _, _ in BIG] + ["small"])
    grads = {}
    for (n, tr, _), sh in zip(BIG, shards):
        sh = sh.reshape(2 * sh.shape[1], sh.shape[2])
        grads[n] = sh.T if tr else sh
    small_grad = small_grad.reshape(N_CHIPS * SMALL_ROWS, 1024)
    grads_small = _unpack_small(small_grad)

    delta, new_m, new_v = {}, {}, {}
    for n, _, _ in BIG:
        grads[n], delta[n], new_m[n], new_v[n] = _adamw(weights[n], grads[n], moms[n], vels[n],
                                                        tr=min(weights[n].shape[-2], 256), name="adamw_" + n)
    _, ds_, ms_, vs_ = _adamw(_pack_small(small), small_grad,
                              _pack_small({n: moms[n] for n, _ in SMALL}), _pack_small({n: vels[n] for n, _ in SMALL}),
                              tr=N_CHIPS * SMALL_ROWS, name="adamw_small")
    for dst, buf in ((delta, ds_), (new_m, ms_), (new_v, vs_)):
        dst.update(_unpack_small(buf))
    grads.update(grads_small)

    total_loss = lax.psum(loss, ("x", "y", "c"))
    return (total_loss, dx[None], *[grads[n] for n in WEIGHT_ORDER], *[delta[n] for n in WEIGHT_ORDER],
            *[new_m[n] for n in WEIGHT_ORDER], *[new_v[n] for n in WEIGHT_ORDER])
```

```python
import functools
import math

import numpy as np
import jax
import jax.numpy as jnp
from jax import lax
from jax.experimental import pallas as pl
from jax.experimental.pallas import tpu as pltpu

F32 = jnp.float32
BF16 = jnp.bfloat16

D_MODEL = 1024
SSM_GROUPS = 32
SSM_GROUP_SIZE = 16
SSM_STATE = 64
SSM_WIDTH = 512
N_STATES = SSM_GROUPS * SSM_STATE
SCAN_CB = 1024
ATTN_PATTERNS = ((128, 1), (512, 4), (2048, 16))
ATTN_HEAD_DIM = 64
ATTN_Q = 128
MEM_LEN = 256
MEM_HEAD_DIM = 128
MEM_HEADS = 4
D_FF = 4096
OFF_U, OFF_QKV, OFF_MQ, OFF_ZG = 0, 512, 2816, 3328
IN_WIDTH = 6400
RMS_EPS = 1e-6
NEG_INF = -1e30
ADAM_LR, ADAM_B1, ADAM_B2, ADAM_EPS, ADAM_WD, ADAM_STEP = 0.001, 0.9, 0.999, 1e-08, 0.01, 10

VMEM_LIMIT_BYTES = 48 * 1024 * 1024
VMEM_LIMIT_WIDE_BYTES = 56 * 1024 * 1024
LANES = 128
MESH = pl.DeviceIdType.MESH
N_CHIPS = 4

SCAN_SEGS = 8
SCAN_GROUPS = SCAN_CB // SSM_STATE

BIG = (("w_in", True, (6400, 1024)), ("w_glu", False, (512, 512)), ("w_ssm_br", True, (1024, 512)),
       ("w_attn_br", True, (1024, 256)), ("w_mem_kv", False, (1024, 1024)), ("w_mem_br", True, (1024, 512)),
       ("w_o", False, (1024, 1024)), ("w_up", True, (4096, 1024)), ("w_down", False, (4096, 1024)))
SMALL = (("norm1_g", (1, 1024)), ("mem_norm_g", (1, 1024)), ("b_gate", (1, 3072)),
         ("ssm_lambda_re", (1, 32, 64)), ("ssm_lambda_im", (1, 32, 64)), ("ssm_log_dt", (1, 32)),
         ("ssm_b_re", (1, 32, 64, 16)), ("ssm_b_im", (1, 32, 64, 16)), ("ssm_c_re", (1, 32, 16, 64)),
         ("ssm_c_im", (1, 32, 16, 64)), ("ssm_d", (1, 32, 16)), ("b_glu", (1, 512)),
         ("norm2_g", (1, 1024)), ("final_g", (1024,)))
WEIGHT_ORDER = ("norm1_g", "mem_norm_g", "w_in", "b_gate", "ssm_lambda_re", "ssm_lambda_im", "ssm_log_dt",
                "ssm_b_re", "ssm_b_im", "ssm_c_re", "ssm_c_im", "ssm_d", "w_glu", "b_glu", "w_ssm_br",
                "w_attn_br", "w_mem_kv", "w_mem_br", "w_o", "norm2_g", "w_up", "w_down", "final_g")
SMALL_ELEMS = sum(int(np.prod(s)) for _, s in SMALL)
SMALL_ROWS = 64


def _params(sem, vmem=VMEM_LIMIT_BYTES):
    return pltpu.CompilerParams(dimension_semantics=sem, vmem_limit_bytes=vmem)


def _sigmoid(v):
    return 0.5 * jnp.tanh(0.5 * v) + 0.5


_GELU_C = math.sqrt(2.0 / math.pi)


def _gelu(v):
    return 0.5 * v * (1.0 + jnp.tanh(_GELU_C * (v + 0.044715 * v * v * v)))


def _gelu_grad(v):
    th = jnp.tanh(_GELU_C * (v + 0.044715 * v * v * v))
    return 0.5 * (1.0 + th) + 0.5 * v * (1.0 - th * th) * _GELU_C * (1.0 + 3.0 * 0.044715 * v * v)


def _dot(a, b, ca, cb):
    return lax.dot_general(a, b, (((ca,), (cb,)), ((), ())), preferred_element_type=F32)


class _Exchange:
    def __init__(self, ins, outs, aliases, sems, start, finish):
        self.ins, self.outs, self.aliases, self.sems, self.start, self.finish = ins, outs, aliases, sems, start, finish


def _matmul(a, b, *, m, n, k, ta=False, tb=False, tm, tn, tk, out_dtypes, name,
            aux=(), epilogue=None, n_sums=0, carry=None):
    assert m % tm == 0 and n % tn == 0 and k % tk == 0, (name, m, n, k, tm, tn, tk)
    assert n_sums == 0 or tn == n, name
    nk = k // tk
    n_aux = len(aux)
    n_tiles = len(out_dtypes)
    n_out = n_tiles + n_sums
    a_spec = pl.BlockSpec((tk, tm), lambda i, j, kk: (kk, i)) if ta else pl.BlockSpec((tm, tk), lambda i, j, kk: (i, kk))
    b_spec = pl.BlockSpec((tn, tk), lambda i, j, kk: (j, kk)) if tb else pl.BlockSpec((tk, tn), lambda i, j, kk: (kk, j))
    aux_specs = []
    for _, kind in aux:
        if kind == "mn":
            aux_specs.append(pl.BlockSpec((tm, tn), lambda i, j, kk: (i, j)))
        else:
            aux_specs.append(pl.BlockSpec((1, tn), lambda i, j, kk: (0, j)))
    ca = 0 if ta else 1
    cb = 1 if tb else 0

    def finish(acc, aux_refs, out_refs, row_tile):
        outs = (acc,) if epilogue is None else epilogue(acc, *[r[...] for r in aux_refs])
        for o_ref, o in zip(out_refs[:n_tiles], outs[:n_tiles]):
            o_ref[...] = o.astype(o_ref.dtype)
        _accumulate_over_rows(out_refs[n_tiles:], outs[n_tiles:], row_tile)

    def body(a_ref, b_ref, *rest):
        aux_refs = rest[:n_aux]
        out_refs = rest[n_aux:n_aux + n_out]
        row_tile = pl.program_id(0)
        prod = _dot(a_ref[...].astype(BF16), b_ref[...].astype(BF16), ca, cb)
        if nk == 1:
            finish(prod, aux_refs, out_refs, row_tile)
            return
        acc_ref = rest[n_aux + n_out]
        kk = pl.program_id(2)

        @pl.when(kk == 0)
        def _():
            acc_ref[...] = prod

        @pl.when(jnp.logical_and(kk > 0, kk < nk - 1))
        def _():
            acc_ref[...] += prod

        @pl.when(kk == nk - 1)
        def _():
            finish(acc_ref[...] + prod, aux_refs, out_refs, row_tile)

    tile = pl.BlockSpec((tm, tn), lambda i, j, kk: (i, j))
    col_sum = pl.BlockSpec((1, tn), lambda i, j, kk: (0, j))
    res = _call_with_carry(
        body, carry, name=name, grid=(m // tm, n // tn, nk), in_specs=[a_spec, b_spec] + aux_specs,
        out_specs=[tile] * n_tiles + [col_sum] * n_sums,
        out_shape=[jax.ShapeDtypeStruct((m, n), dt) for dt in out_dtypes] + [jax.ShapeDtypeStruct((1, n), F32)] * n_sums,
        scratch=[pltpu.VMEM((tm, tn), F32)] if nk > 1 else [], operands=[a, b] + [x for x, _ in aux],
        semantics=("arbitrary" if n_sums else "parallel", "parallel", "arbitrary"))
    main = res[0] if n_out == 1 else tuple(res[:n_out])
    return main if carry is None else (main, list(res[n_out:]))


def _accumulate_over_rows(sum_refs, terms, row_tile):
    for s_ref, term in zip(sum_refs, terms):
        @pl.when(row_tile == 0)
        def _():
            s_ref[...] = term

        @pl.when(row_tile > 0)
        def _():
            s_ref[...] += term


def _call_with_carry(body, carry, *, name, grid, in_specs, out_specs, out_shape, scratch, operands, semantics,
                     vmem=VMEM_LIMIT_BYTES):
    if carry is None:
        return pl.pallas_call(body, name=name, grid=grid, in_specs=in_specs, out_specs=out_specs, out_shape=out_shape,
                              scratch_shapes=scratch, compiler_params=_params(semantics, vmem))(*operands)
    n_in, n_cin, n_out, n_cout, n_scr = len(operands), len(carry.ins), len(out_shape), len(carry.outs), len(scratch)

    def hosted(*refs):
        main_in, c_in = refs[:n_in], refs[n_in:n_in + n_cin]
        main_out = refs[n_in + n_cin:n_in + n_cin + n_out]
        c_out = refs[n_in + n_cin + n_out:n_in + n_cin + n_out + n_cout]
        rest = refs[n_in + n_cin + n_out + n_cout:]
        ids = [pl.program_id(t) for t in range(len(grid))]
        first = functools.reduce(jnp.logical_and, [i == 0 for i in ids])
        last = functools.reduce(jnp.logical_and, [i == g - 1 for i, g in zip(ids, grid)])

        @pl.when(first)
        def _():
            carry.start(c_in, c_out, *rest[n_scr:])

        body(*main_in, *main_out, *rest[:n_scr])

        @pl.when(last)
        def _():
            carry.finish(c_in, c_out, *rest[n_scr:])

    return pl.pallas_call(
        hosted, name=name, grid=grid,
        in_specs=list(in_specs) + [ANY] * n_cin, out_specs=list(out_specs) + [ANY] * n_cout,
        out_shape=list(out_shape) + list(carry.outs),
        input_output_aliases={n_in + i: n_out + o for i, o in carry.aliases.items()},
        scratch_shapes=list(scratch) + [pltpu.SemaphoreType.DMA(s) for s in carry.sems],
        compiler_params=_params(("arbitrary",) * len(grid), vmem),
    )(*operands, *carry.ins)


def _sum_matmul(pieces, b, offs, *, tm, name, tb=False, out_dtype=F32, aux=(), epilogue=None, n_sums=0, carry=None,
                vmem=VMEM_LIMIT_BYTES):
    m = pieces[0].shape[0]
    n = b.shape[0] if tb else b.shape[1]
    npieces, n_aux = len(pieces), len(aux)
    assert not tb or npieces == 1

    def body(*refs):
        b_ref = refs[npieces]
        aux_refs = refs[npieces + 1:npieces + 1 + n_aux]
        out_refs = refs[npieces + 1 + n_aux:]
        acc = None
        for p_ref, off in zip(refs[:npieces], offs):
            lhs = p_ref[...].astype(BF16)
            part = _dot(lhs, b_ref[...], 1, 1) if tb else _dot(lhs, b_ref[pl.ds(off, p_ref.shape[1]), :], 1, 0)
            acc = part if acc is None else acc + part
        outs = (acc,) if epilogue is None else epilogue(acc, *[r[...] for r in aux_refs])
        out_refs[0][...] = outs[0].astype(out_dtype)
        _accumulate_over_rows(out_refs[1:], outs[1:], pl.program_id(0))

    row = pl.BlockSpec((tm, n), lambda i: (i, 0))
    vec = pl.BlockSpec((1, n), lambda i: (0, 0))
    res = _call_with_carry(
        body, carry, name=name, grid=(m // tm,),
        in_specs=[pl.BlockSpec((tm, p.shape[1]), lambda i: (i, 0)) for p in pieces] + [_resident(b.shape)]
        + [row if kind == "mn" else vec for _, kind in aux],
        out_specs=[row] + [vec] * n_sums,
        out_shape=[jax.ShapeDtypeStruct((m, n), out_dtype)] + [jax.ShapeDtypeStruct((1, n), F32)] * n_sums,
        scratch=[], operands=list(pieces) + [b] + [x for x, _ in aux], semantics=("arbitrary" if n_sums else "parallel",),
        vmem=vmem)
    main = res[0] if n_sums == 0 else tuple(res[:1 + n_sums])
    return main if carry is None else (main, list(res[1 + n_sums:]))


def _split_matmul(a, b_t, splits, *, tm, name, carry=None, vmem=VMEM_LIMIT_BYTES):
    m, k = a.shape

    def body(a_ref, b_ref, *out_refs):
        av = a_ref[...].astype(BF16)
        for (row0, width), o_ref in zip(splits, out_refs):
            o_ref[...] = _dot(av, b_ref[pl.ds(row0, width), :], 1, 1)

    res = _call_with_carry(
        body, carry, name=name, grid=(m // tm,),
        in_specs=[pl.BlockSpec((tm, k), lambda i: (i, 0)), _resident(b_t.shape)],
        out_specs=[pl.BlockSpec((tm, width), lambda i: (i, 0)) for _, width in splits],
        out_shape=[jax.ShapeDtypeStruct((m, width), F32) for _, width in splits],
        scratch=[], operands=[a, b_t], semantics=("parallel",), vmem=vmem)
    outs = tuple(res[:len(splits)])
    return outs if carry is None else (outs, list(res[len(splits):]))


def _rmsnorm_fwd(x, g, *, tm, name, carry=None):
    rows, d = x.shape

    def body(x_ref, g_ref, o_ref):
        xv = x_ref[...]
        r = lax.rsqrt(jnp.mean(xv * xv, axis=-1, keepdims=True) + RMS_EPS)
        o_ref[...] = (xv * r * g_ref[...]).astype(o_ref.dtype)

    res = _call_with_carry(
        body, carry, name=name, grid=(rows // tm,),
        in_specs=[pl.BlockSpec((tm, d), lambda i: (i, 0)), pl.BlockSpec((1, d), lambda i: (0, 0))],
        out_specs=[pl.BlockSpec((tm, d), lambda i: (i, 0))], out_shape=[jax.ShapeDtypeStruct((rows, d), BF16)],
        scratch=[], operands=[x, g], semantics=("parallel",))
    return res[0] if carry is None else (res[0], list(res[1:]))


def _residual_norm_epilogue(acc, xv, gv):
    h = acc + xv
    r = lax.rsqrt(jnp.mean(h * h, axis=-1, keepdims=True) + RMS_EPS)
    return h, h * r * gv


def _rmsnorm_bwd_epilogue(dy, xv, resv, gv):
    r = lax.rsqrt(jnp.mean(xv * xv, axis=-1, keepdims=True) + RMS_EPS)
    xhat = xv * r
    dyg = dy * gv
    dx = r * (dyg - xhat * jnp.mean(dyg * xhat, axis=-1, keepdims=True)) + resv
    return dx, jnp.sum(dy * xhat, axis=0, keepdims=True)


def _rmsnorm_bwd(x, g, dy, res, *, tm, name):
    rows, d = x.shape
    has_res = res is not None

    def body(x_ref, g_ref, dy_ref, *rest):
        if has_res:
            res_ref, dx_ref, dg_ref = rest
        else:
            dx_ref, dg_ref = rest
        i = pl.program_id(0)
        xv = x_ref[...]
        r = lax.rsqrt(jnp.mean(xv * xv, axis=-1, keepdims=True) + RMS_EPS)
        xhat = xv * r
        dyv = dy_ref[...]
        dyg = dyv * g_ref[...]
        dx = r * (dyg - xhat * jnp.mean(dyg * xhat, axis=-1, keepdims=True))
        if has_res:
            dx = dx + res_ref[...]
        dx_ref[...] = dx

        @pl.when(i == 0)
        def _():
            dg_ref[...] = jnp.zeros_like(dg_ref)

        dg_ref[...] += jnp.sum(dyv * xhat, axis=0, keepdims=True)

    row_spec = pl.BlockSpec((tm, d), lambda i: (i, 0))
    vec_spec = pl.BlockSpec((1, d), lambda i: (0, 0))
    ins = [x, g, dy] + ([res] if has_res else [])
    return pl.pallas_call(
        body, name=name, grid=(rows // tm,),
        in_specs=[row_spec, vec_spec, row_spec] + ([row_spec] if has_res else []),
        out_specs=[row_spec, vec_spec],
        out_shape=[jax.ShapeDtypeStruct((rows, d), F32), jax.ShapeDtypeStruct((1, d), F32)],
        compiler_params=_params(("arbitrary",)),
    )(*ins)


def _loss_head_epilogue(acc, hv, tgtv, gv):
    xv = acc + hv
    r = lax.rsqrt(jnp.mean(xv * xv, axis=-1, keepdims=True) + RMS_EPS)
    xhat = xv * r
    err = xhat * gv - tgtv
    dyv = err * (1.0 / D_MODEL)
    dyg = dyv * gv
    dh = r * (dyg - xhat * jnp.mean(dyg * xhat, axis=-1, keepdims=True))
    return dh, jnp.sum(dyv * xhat, axis=0, keepdims=True), jnp.sum(err * err, axis=0, keepdims=True)


def _to_scan_layout(v):
    lead = v.shape[:-2]
    v = v.reshape(lead + (2, N_STATES // SCAN_CB, SCAN_CB))
    v = jnp.swapaxes(v, -3, -2)
    return v.reshape(lead + (2 * N_STATES,))


def _ssm_matrices(lam_re, lam_im, log_dt, b_re, b_im, c_re, c_im):
    dt = jnp.exp(log_dt)[:, None]
    mag = jnp.exp(lam_re * dt)
    a_re, a_im = mag * jnp.cos(lam_im * dt), mag * jnp.sin(lam_im * dt)
    nr, ni = a_re - 1.0, a_im
    den = lam_re * lam_re + lam_im * lam_im
    coef_re = (nr * lam_re + ni * lam_im) / den
    coef_im = (ni * lam_re - nr * lam_im) / den
    bb_re = coef_re[..., None] * b_re - coef_im[..., None] * b_im
    bb_im = coef_re[..., None] * b_im + coef_im[..., None] * b_re
    a_lay = _to_scan_layout(jnp.stack([a_re.reshape(-1), a_im.reshape(-1)], axis=0))[None, :]
    nblk = SSM_GROUPS // SCAN_GROUPS
    eye = jnp.eye(SCAN_GROUPS, dtype=F32)

    def b_block(bb):
        bb = bb.reshape(nblk, SCAN_GROUPS, SSM_STATE, SSM_GROUP_SIZE)
        return jnp.einsum("gk,jkph->jghkp", eye, bb).reshape(nblk, SCAN_GROUPS * SSM_GROUP_SIZE, SCAN_CB)

    b_blk = jnp.concatenate([b_block(bb_re), b_block(bb_im)], axis=2)

    def c_block(cc):
        cc = cc.reshape(nblk, SCAN_GROUPS, SSM_GROUP_SIZE, SSM_STATE)
        return jnp.einsum("gk,jghp->jkpgh", eye, cc).reshape(nblk, SCAN_CB, SCAN_GROUPS * SSM_GROUP_SIZE)

    c_blk = jnp.concatenate([c_block(c_re), -c_block(c_im)], axis=1)
    return a_lay, b_blk, c_blk


def _interleave(v):
    rows, c = v.shape
    return v.reshape(SCAN_SEGS, rows // SCAN_SEGS, c).transpose(1, 0, 2).reshape(rows, c)


def _deinterleave(v):
    rows, c = v.shape
    return v.reshape(rows // SCAN_SEGS, SCAN_SEGS, c).transpose(1, 0, 2).reshape(rows, c)


def _scan_groups(a_ref, bu_ref, o_ref, state, *, reverse, tt, unroll=4):
    cb = SCAN_CB
    ar = jnp.broadcast_to(a_ref[:, :cb], (SCAN_SEGS, cb))
    ai = jnp.broadcast_to(a_ref[:, cb:], (SCAN_SEGS, cb))
    ngroups = tt // SCAN_SEGS

    def step(i, st):
        sr, si = st
        r0 = pl.multiple_of(((ngroups - 1 - i) if reverse else i) * SCAN_SEGS, SCAN_SEGS)
        blk = bu_ref[pl.ds(r0, SCAN_SEGS), :]
        nr = ar * sr - ai * si + blk[:, :cb]
        ni = ar * si + ai * sr + blk[:, cb:]
        if o_ref is not None:
            o_ref[pl.ds(r0, SCAN_SEGS), :] = jnp.concatenate([nr, ni], axis=1)
        return nr, ni

    return lax.fori_loop(0, ngroups, step, state, unroll=unroll)


def _segment_entries(a_ref, e_ref, init_ref, *, reverse, seg_len):
    cb = SCAN_CB
    n_sq = seg_len.bit_length() - 1
    assert 1 << n_sq == seg_len, seg_len
    pr, pi = a_ref[:, :cb], a_ref[:, cb:]
    for _ in range(n_sq):
        pr, pi = pr * pr - pi * pi, 2.0 * pr * pi
    cr = jnp.zeros((1, cb), F32)
    ci = jnp.zeros((1, cb), F32)
    order = range(SCAN_SEGS - 1, -1, -1) if reverse else range(SCAN_SEGS)
    for k, seg in enumerate(order):
        if k > 0:
            prev = seg + 1 if reverse else seg - 1
            er, ei = e_ref[prev:prev + 1, :cb], e_ref[prev:prev + 1, cb:]
            cr, ci = pr * cr - pi * ci + er, pr * ci + pi * cr + ei
        init_ref[seg:seg + 1, :] = jnp.concatenate([cr, ci], axis=1)


def _ssm_specs(nt, tt, nch, reverse):
    cb = SCAN_CB
    tmap = (lambda j, kk: (nt - 1 - kk, j)) if reverse else (lambda j, kk: (kk, j))
    nmap = (lambda j, kk: (jnp.maximum(nt - 2 - kk, 0), j)) if reverse else (lambda j, kk: (jnp.minimum(kk + 1, nt - 1), j))
    return dict(a=pl.BlockSpec((1, 2 * cb), lambda j, kk: (0, j)),
                seg=pl.BlockSpec((SCAN_SEGS, 2 * cb), lambda j, kk: (0, j)),
                chan=pl.BlockSpec((tt, nch), tmap),
                next=pl.BlockSpec((tt, nch), nmap),
                state=pl.BlockSpec((tt, 2 * cb), tmap),
                b=pl.BlockSpec((None, nch, 2 * cb), lambda j, kk: (j, 0, 0)),
                c=pl.BlockSpec((None, 2 * cb, nch), lambda j, kk: (j, 0, 0)))


def _ssm_ends(a_lay, x, blocks, *, transpose, reverse, tt, name):
    rows = x.shape[0]
    nblk = blocks.shape[0]
    nch = x.shape[1] // nblk
    cb = SCAN_CB
    nt = rows // tt
    sp = _ssm_specs(nt, tt, nch, reverse)

    def body(a_ref, x_ref, xn_ref, w_ref, e_ref, even_ref, odd_ref):
        kk = pl.program_id(1)

        def product(src_ref, dst_ref):
            dst_ref[...] = _dot(src_ref[...].astype(BF16), w_ref[...].astype(BF16), 1, 1 if transpose else 0)

        @pl.when(kk == 0)
        def _():
            e_ref[...] = jnp.zeros_like(e_ref)
            product(x_ref, even_ref)

        def phase(cur_ref, next_ref):
            product(xn_ref, next_ref)
            sr, si = _scan_groups(a_ref, cur_ref, None, (e_ref[:, :cb], e_ref[:, cb:]), reverse=reverse, tt=tt, unroll=True)
            e_ref[...] = jnp.concatenate([sr, si], axis=1)

        @pl.when(kk % 2 == 0)
        def _():
            phase(even_ref, odd_ref)

        @pl.when(kk % 2 == 1)
        def _():
            phase(odd_ref, even_ref)

    return pl.pallas_call(
        body, name=name, grid=(nblk, nt),
        in_specs=[sp["a"], sp["chan"], sp["next"], sp["c"] if transpose else sp["b"]],
        out_specs=sp["seg"],
        out_shape=jax.ShapeDtypeStruct((SCAN_SEGS, nblk * 2 * cb), F32),
        scratch_shapes=[pltpu.VMEM((tt, 2 * cb), F32), pltpu.VMEM((tt, 2 * cb), F32)],
        compiler_params=_params(("parallel", "arbitrary")),
    )(a_lay, x, x, blocks)


def _ssm_fwd(a_lay, u, b_blk, c_blk, ends, *, tt, name):
    rows = u.shape[0]
    nblk = b_blk.shape[0]
    nch = u.shape[1] // nblk
    cb = SCAN_CB
    nt = rows // tt
    sp = _ssm_specs(nt, tt, nch, False)

    def body(a_ref, e_ref, u_ref, un_ref, b_ref, c_ref, s_ref, y_ref, init_ref, carry_ref, even_ref, odd_ref, sb_ref):
        kk = pl.program_id(1)

        def product(src_ref, dst_ref):
            dst_ref[...] = _dot(src_ref[...].astype(BF16), b_ref[...].astype(BF16), 1, 0)

        def output():
            y_ref[...] = _dot(sb_ref[...], c_ref[...].astype(BF16), 1, 0)

        @pl.when(kk == 0)
        def _():
            _segment_entries(a_ref, e_ref, init_ref, reverse=False, seg_len=rows // SCAN_SEGS)
            carry_ref[...] = init_ref[...]
            sb_ref[...] = jnp.zeros_like(sb_ref)
            product(u_ref, even_ref)

        def phase(cur_ref, next_ref):
            product(un_ref, next_ref)
            output()
            sr, si = _scan_groups(a_ref, cur_ref, cur_ref, (carry_ref[:, :cb], carry_ref[:, cb:]), reverse=False, tt=tt,
                                  unroll=True)
            carry_ref[...] = jnp.concatenate([sr, si], axis=1)
            sb = cur_ref[...].astype(BF16)
            sb_ref[...] = sb
            s_ref[...] = sb

        @pl.when(jnp.logical_and(kk < nt, kk % 2 == 0))
        def _():
            phase(even_ref, odd_ref)

        @pl.when(jnp.logical_and(kk < nt, kk % 2 == 1))
        def _():
            phase(odd_ref, even_ref)

        @pl.when(kk == nt)
        def _():
            output()

    def clamped(spec_map):
        return lambda j, kk: spec_map(j, jnp.minimum(kk, nt - 1))

    tile = lambda width, imap: pl.BlockSpec((tt, width), imap)
    return pl.pallas_call(
        body, name=name, grid=(nblk, nt + 1),
        in_specs=[sp["a"], sp["seg"], tile(nch, clamped(lambda j, kk: (kk, j))), tile(nch, clamped(lambda j, kk: (jnp.minimum(kk + 1, nt - 1), j))),
                  sp["b"], sp["c"]],
        out_specs=[tile(2 * cb, clamped(lambda j, kk: (kk, j))), tile(nch, lambda j, kk: (jnp.maximum(kk - 1, 0), j)), sp["seg"]],
        out_shape=[jax.ShapeDtypeStruct((rows, nblk * 2 * cb), BF16), jax.ShapeDtypeStruct((rows, nblk * nch), F32),
                   jax.ShapeDtypeStruct((SCAN_SEGS, nblk * 2 * cb), F32)],
        scratch_shapes=[pltpu.VMEM((SCAN_SEGS, 2 * cb), F32), pltpu.VMEM((tt, 2 * cb), F32), pltpu.VMEM((tt, 2 * cb), F32),
                        pltpu.VMEM((tt, 2 * cb), BF16)],
        compiler_params=_params(("parallel", "arbitrary")),
    )(a_lay, ends, u, u, b_blk, c_blk)


def _ssm_bwd(a_conj, dy, u, s, s_entry, b_blk, c_blk, dd, ends, *, tt, name):
    rows = u.shape[0]
    nblk = b_blk.shape[0]
    nch = u.shape[1] // nblk
    cb = SCAN_CB
    nt = rows // tt
    sp = _ssm_specs(nt, tt, nch, True)
    groups_per_tile = tt // SCAN_SEGS
    before = pl.BlockSpec((2 * SCAN_SEGS, 2 * cb),
                          lambda j, kk: (jnp.maximum((nt - 1 - kk) * (groups_per_tile // 2) - 1, 0), j))

    def body(a_ref, e_ref, dy_ref, dyn_ref, u_ref, s_ref, before_ref, entry_ref, b_ref, c_ref, dd_ref,
             du_ref, db_ref, dc_ref, da_ref, carry_ref, even_ref, odd_ref):
        kk = pl.program_id(1)

        def product(src_ref, dst_ref):
            dst_ref[...] = _dot(src_ref[...].astype(BF16), c_ref[...].astype(BF16), 1, 1)

        @pl.when(kk == 0)
        def _():
            _segment_entries(a_ref, e_ref, carry_ref, reverse=True, seg_len=rows // SCAN_SEGS)
            db_ref[...] = jnp.zeros_like(db_ref)
            dc_ref[...] = jnp.zeros_like(dc_ref)
            da_ref[...] = jnp.zeros_like(da_ref)
            product(dy_ref, even_ref)

        def pair(lv, pv):
            lre, lim, pre, pim = lv[:, :cb], lv[:, cb:], pv[:, :cb], pv[:, cb:]
            return (jnp.sum(lre * pre + lim * pim, axis=0, keepdims=True),
                    jnp.sum(lim * pre - lre * pim, axis=0, keepdims=True))

        def phase(lam_ref, next_ref):
            product(dyn_ref, next_ref)
            lr, li = _scan_groups(a_ref, lam_ref, lam_ref, (carry_ref[:, :cb], carry_ref[:, cb:]), reverse=True, tt=tt,
                                  unroll=True)
            carry_ref[...] = jnp.concatenate([lr, li], axis=1)
            first = jnp.where(kk == nt - 1, entry_ref[...], before_ref[...].astype(F32)[SCAN_SEGS:])
            rest = tt - SCAN_SEGS
            r1, i1 = pair(lam_ref[pl.ds(SCAN_SEGS, rest), :], s_ref[...].astype(F32)[:rest])
            r0, i0 = pair(lam_ref[pl.ds(0, SCAN_SEGS), :], first)
            da_ref[...] += jnp.concatenate([r1 + r0, i1 + i0], axis=1)
            dyv = dy_ref[...]
            lamb = lam_ref[...].astype(BF16)
            du_ref[...] = _dot(lamb, b_ref[...].astype(BF16), 1, 1) + dd_ref[...] * dyv
            db_ref[...] += _dot(u_ref[...].astype(BF16), lamb, 0, 0)
            dc_ref[...] += _dot(s_ref[...], dyv.astype(BF16), 0, 0)

        @pl.when(kk % 2 == 0)
        def _():
            phase(even_ref, odd_ref)

        @pl.when(kk % 2 == 1)
        def _():
            phase(odd_ref, even_ref)

    return pl.pallas_call(
        body, name=name, grid=(nblk, nt),
        in_specs=[sp["a"], sp["seg"], sp["chan"], sp["next"], sp["chan"], sp["state"], before, sp["seg"], sp["b"], sp["c"],
                  pl.BlockSpec((1, nch), lambda j, kk: (0, j))],
        out_specs=[sp["chan"], sp["b"], sp["c"], pl.BlockSpec((1, 2 * cb), lambda j, kk: (0, j))],
        out_shape=[jax.ShapeDtypeStruct((rows, nblk * nch), F32), jax.ShapeDtypeStruct(b_blk.shape, F32),
                   jax.ShapeDtypeStruct(c_blk.shape, F32), jax.ShapeDtypeStruct((1, nblk * 2 * cb), F32)],
        scratch_shapes=[pltpu.VMEM((SCAN_SEGS, 2 * cb), F32), pltpu.VMEM((tt, 2 * cb), F32), pltpu.VMEM((tt, 2 * cb), F32)],
        compiler_params=_params(("parallel", "arbitrary")),
    )(a_conj, ends, dy, dy, u, s, s, s_entry, b_blk, c_blk, dd)


def _glu_fwd(ys, u, dd, w_glu, b_glu, *, tm, name):
    rows, w = ys.shape

    def body(ys_ref, u_ref, dd_ref, w_ref, b_ref, y0_ref, t_ref, y2_ref):
        y0 = ys_ref[...] + dd_ref[...] * u_ref[...]
        y1 = _gelu(y0)
        t = _dot(y1.astype(BF16), w_ref[...], 1, 0) + b_ref[...]
        y0_ref[...] = y0
        t_ref[...] = t
        y2_ref[...] = (y1 * _sigmoid(t)).astype(BF16)

    row = pl.BlockSpec((tm, w), lambda i: (i, 0))
    vec = pl.BlockSpec((1, w), lambda i: (0, 0))
    return pl.pallas_call(
        body, name=name, grid=(rows // tm,),
        in_specs=[row, row, vec, pl.BlockSpec((w, w), lambda i: (0, 0)), vec],
        out_specs=[row, row, row],
        out_shape=[jax.ShapeDtypeStruct((rows, w), F32), jax.ShapeDtypeStruct((rows, w), F32),
                   jax.ShapeDtypeStruct((rows, w), BF16)],
        compiler_params=_params(("parallel",)),
    )(ys, u, dd, w_glu, b_glu)


def _glu_bwd(dy2, y0, t, u, w_glu, *, tm, name):
    rows, w = y0.shape

    def body(dy2_ref, y0_ref, t_ref, u_ref, w_ref, dy0_ref, dt_ref, y1_ref, db_ref, dd_ref):
        i = pl.program_id(0)
        y0 = y0_ref[...]
        y1 = _gelu(y0)
        sg = _sigmoid(t_ref[...])
        dy2v = dy2_ref[...]
        dt = dy2v * y1 * sg * (1.0 - sg)
        dy1 = dy2v * sg + _dot(dt.astype(BF16), w_ref[...], 1, 1)
        dy0 = dy1 * _gelu_grad(y0)
        dy0_ref[...] = dy0
        dt_ref[...] = dt.astype(BF16)
        y1_ref[...] = y1.astype(BF16)

        @pl.when(i == 0)
        def _():
            db_ref[...] = jnp.zeros_like(db_ref)
            dd_ref[...] = jnp.zeros_like(dd_ref)

        db_ref[...] += jnp.sum(dt, axis=0, keepdims=True)
        dd_ref[...] += jnp.sum(dy0 * u_ref[...], axis=0, keepdims=True)

    row = pl.BlockSpec((tm, w), lambda i: (i, 0))
    vec = pl.BlockSpec((1, w), lambda i: (0, 0))
    return pl.pallas_call(
        body, name=name, grid=(rows // tm,),
        in_specs=[row, row, row, row, pl.BlockSpec((w, w), lambda i: (0, 0))],
        out_specs=[row, row, row, vec, vec],
        out_shape=[jax.ShapeDtypeStruct((rows, w), F32), jax.ShapeDtypeStruct((rows, w), BF16),
                   jax.ShapeDtypeStruct((rows, w), BF16), jax.ShapeDtypeStruct((1, w), F32),
                   jax.ShapeDtypeStruct((1, w), F32)],
        compiler_params=_params(("arbitrary",)),
    )(dy2, y0, t, u, w_glu)


ATTN_TILE = 2048


def _attn_geometry(rows, d):
    sb = ATTN_Q * d
    tr = max(sb, min(ATTN_TILE, rows))
    assert rows % tr == 0 and tr % sb == 0, (rows, d)
    return sb, tr, rows // tr, tr // sb


def _attn_masks():
    qi = lax.broadcasted_iota(jnp.int32, (2 * ATTN_Q, 2 * ATTN_Q), 0) % ATTN_Q
    kj = lax.broadcasted_iota(jnp.int32, (2 * ATTN_Q, 2 * ATTN_Q), 1)
    own_ok = jnp.logical_and(kj >= ATTN_Q, kj - ATTN_Q <= qi)
    prev_ok = jnp.logical_and(kj < ATTN_Q, kj >= qi)
    bias_first = jnp.where(own_ok, 0.0, NEG_INF)
    bias_other = jnp.where(jnp.logical_or(own_ok, prev_ok), 0.0, NEG_INF)
    head0 = lax.broadcasted_iota(jnp.int32, (ATTN_Q, LANES), 1) < ATTN_HEAD_DIM
    return bias_first, bias_other, head0


def _attn_rows(base, n, d):
    return pl.ds(pl.multiple_of(base, ATTN_Q), n) if d == 1 else pl.ds(base, n, stride=d)


def _stack_heads(v, head0):
    return jnp.concatenate([jnp.where(head0, v, 0.0), jnp.where(head0, 0.0, v)], axis=0)


def _unstack_heads(v, head0):
    return jnp.where(head0, v[:ATTN_Q], v[ATTN_Q:])


def _fill_keys(buf, prev_ref, cur_ref, sb):
    buf[pl.ds(0, sb), :] = prev_ref[...]
    buf[pl.ds(sb, cur_ref.shape[0]), :] = cur_ref[...]


def _attn_fwd(qkv, g, d, *, name):
    rows = qkv.shape[0]
    sb, tr, ntiles, nsub = _attn_geometry(rows, d)
    qc, kc, vc = 2 * g, 6 + 2 * g, 12 + 2 * g
    scale = ATTN_HEAD_DIM ** -0.5

    def body(q_ref, kc_ref, kp_ref, vc_ref, vp_ref, o_ref, lse_ref, kbuf, vbuf):
        n = pl.program_id(0)
        _fill_keys(kbuf, kp_ref, kc_ref, sb)
        _fill_keys(vbuf, vp_ref, vc_ref, sb)
        bias_first, bias_other, head0 = _attn_masks()

        def per_block(idx, carry):
            j, r = idx // d, idx % d
            base = j * sb + r
            bias = jnp.where(jnp.logical_and(n == 0, j == 0), bias_first, bias_other)
            qrows = _attn_rows(base, ATTN_Q, d)
            krows = _attn_rows(base, 2 * ATTN_Q, d)
            qs = (_stack_heads(q_ref[qrows, :], head0) * scale).astype(BF16)
            s = _dot(qs, kbuf[krows, :].astype(BF16), 1, 1) + bias
            mx = jnp.max(s, axis=-1, keepdims=True)
            p = jnp.exp(s - mx)
            den = jnp.sum(p, axis=-1, keepdims=True)
            pv = _dot(p.astype(BF16), vbuf[krows, :].astype(BF16), 1, 0) / den
            o_ref[qrows, :] = _unstack_heads(pv, head0)
            lse_ref[qrows, :] = _unstack_heads(jnp.broadcast_to(mx + jnp.log(den), (2 * ATTN_Q, LANES)), head0)
            return carry

        lax.fori_loop(0, nsub * d, per_block, 0, unroll=True)

    def cur(col):
        return pl.BlockSpec((tr, LANES), lambda n, hp: (n, col + hp))

    def prev(col):
        return pl.BlockSpec((sb, LANES), lambda n, hp: (jnp.maximum(n * nsub - 1, 0), col + hp))

    out_spec = pl.BlockSpec((tr, LANES), lambda n, hp: (n, hp))
    return pl.pallas_call(
        body, name=name, grid=(ntiles, 2),
        in_specs=[cur(qc), cur(kc), prev(kc), cur(vc), prev(vc)],
        out_specs=[out_spec, out_spec],
        out_shape=[jax.ShapeDtypeStruct((rows, 2 * LANES), F32), jax.ShapeDtypeStruct((rows, 2 * LANES), F32)],
        scratch_shapes=[pltpu.VMEM((sb + tr, LANES), F32), pltpu.VMEM((sb + tr, LANES), F32)],
        compiler_params=_params(("parallel", "parallel")),
    )(qkv, qkv, qkv, qkv, qkv)


def _attn_merge(outs, lses, *, tm, name):
    rows, w = outs[0].shape

    def body(o0, o1, o2, l0, l1, l2, o_ref, lse_ref):
        a0, a1, a2 = l0[...], l1[...], l2[...]
        mx = jnp.maximum(jnp.maximum(a0, a1), a2)
        e0, e1, e2 = jnp.exp(a0 - mx), jnp.exp(a1 - mx), jnp.exp(a2 - mx)
        den = e0 + e1 + e2
        o_ref[...] = (e0 / den) * o0[...] + (e1 / den) * o1[...] + (e2 / den) * o2[...]
        lse_ref[...] = mx + jnp.log(den)

    row = pl.BlockSpec((tm, w), lambda i: (i, 0))
    return pl.pallas_call(
        body, name=name, grid=(rows // tm,), in_specs=[row] * 6, out_specs=[row, row],
        out_shape=[jax.ShapeDtypeStruct((rows, w), F32), jax.ShapeDtypeStruct((rows, w), F32)],
        compiler_params=_params(("parallel",)),
    )(*outs, *lses)


def _attn_bwd(qkv, do, o, lse, g, d, prev, *, name):
    rows = qkv.shape[0]
    sb, tr, ntiles, nsub = _attn_geometry(rows, d)
    qc, kc, vc = 2 * g, 6 + 2 * g, 12 + 2 * g
    scale = ATTN_HEAD_DIM ** -0.5

    def body(q_ref, kc_ref, kp_ref, vc_ref, vp_ref, do_ref, o_ref, lse_ref, dq_ref, dk_ref, dv_ref,
             kbuf, vbuf, dk_acc, dv_acc):
        n = pl.program_id(1)

        @pl.when(n == 0)
        def _():
            dk_acc[pl.ds(0, tr), :] = jnp.zeros((tr, LANES), F32)
            dv_acc[pl.ds(0, tr), :] = jnp.zeros((tr, LANES), F32)

        @pl.when(n < ntiles)
        def _():
            dk_acc[pl.ds(tr, tr), :] = jnp.zeros((tr, LANES), F32)
            dv_acc[pl.ds(tr, tr), :] = jnp.zeros((tr, LANES), F32)
            _fill_keys(kbuf, kp_ref, kc_ref, sb)
            _fill_keys(vbuf, vp_ref, vc_ref, sb)
            bias_first, bias_other, head0 = _attn_masks()
            lane = lax.broadcasted_iota(jnp.int32, (ATTN_Q, LANES), 1)

            def per_block(idx, carry):
                j, r = idx // d, idx % d
                base = j * sb + r
                bias = jnp.where(jnp.logical_and(n == 0, j == 0), bias_first, bias_other)
                qrows = _attn_rows(base, ATTN_Q, d)
                krows = _attn_rows(base, 2 * ATTN_Q, d)
                arows = _attn_rows(base + (tr - sb), 2 * ATTN_Q, d)
                qs = (_stack_heads(q_ref[qrows, :], head0) * scale).astype(BF16)
                dos = _stack_heads(do_ref[qrows, :], head0)
                dosb = dos.astype(BF16)
                ov = o_ref[qrows, :]
                delta = jnp.sum(dos * jnp.concatenate([ov, ov], axis=0), axis=-1, keepdims=True)
                lsev = lse_ref[qrows, :]
                lse_s = jnp.concatenate(
                    [jnp.sum(jnp.where(lane == h * ATTN_HEAD_DIM, lsev, 0.0), axis=-1, keepdims=True) for h in range(2)], axis=0)
                kb = kbuf[krows, :].astype(BF16)
                vb = vbuf[krows, :].astype(BF16)
                p = jnp.exp(_dot(qs, kb, 1, 1) + bias - lse_s)
                ds = (p * (_dot(dosb, vb, 1, 1) - delta)).astype(BF16)
                dq_ref[qrows, :] = _unstack_heads(_dot(ds, kb, 1, 0), head0) * scale
                dk_acc[arows, :] += _dot(ds, qs, 0, 0)
                dv_acc[arows, :] += _dot(p.astype(BF16), dosb, 0, 0)
                return carry

            lax.fori_loop(0, nsub * d, per_block, 0, unroll=True)

        dk_ref[...] = dk_acc[pl.ds(0, tr), :]
        dv_ref[...] = dv_acc[pl.ds(0, tr), :]
        dk_acc[pl.ds(0, tr), :] = dk_acc[pl.ds(tr, tr), :]
        dv_acc[pl.ds(0, tr), :] = dv_acc[pl.ds(tr, tr), :]

    def cur(n):
        return jnp.minimum(n, ntiles - 1)

    def spec(col, prev):
        if prev:
            return pl.BlockSpec((sb, LANES), lambda hp, n: (jnp.maximum(cur(n) * nsub - 1, 0), col + hp))
        return pl.BlockSpec((tr, LANES), lambda hp, n: (cur(n), col + hp))

    row_spec = pl.BlockSpec((tr, LANES), lambda hp, n: (cur(n), hp))
    dq_out = pl.BlockSpec((tr, LANES), lambda hp, n: (cur(n), 2 * g + hp))
    kv_out = pl.BlockSpec((tr, LANES), lambda hp, n: (jnp.maximum(n - 1, 0), 2 * g + hp))
    shape = jax.ShapeDtypeStruct((rows, len(ATTN_PATTERNS) * 2 * LANES), F32)
    ins = [qkv, qkv, qkv, qkv, qkv, do, o, lse]
    in_specs = [spec(qc, False), spec(kc, False), spec(kc, True), spec(vc, False), spec(vc, True),
                row_spec, row_spec, row_spec]
    aliases = {}
    if prev is not None:
        aliases = {len(ins) + t: t for t in range(3)}
        ins = ins + list(prev)
        in_specs = in_specs + [ANY] * 3
    n_in = len(ins)

    def entry(*refs):
        body(*refs[:8], *refs[n_in:])

    return pl.pallas_call(
        entry, name=name, grid=(2, ntiles + 1),
        in_specs=in_specs,
        out_specs=[dq_out, kv_out, kv_out],
        out_shape=[shape, shape, shape],
        input_output_aliases=aliases,
        scratch_shapes=[pltpu.VMEM((sb + tr, LANES), F32), pltpu.VMEM((sb + tr, LANES), F32),
                        pltpu.VMEM((2 * tr, LANES), F32), pltpu.VMEM((2 * tr, LANES), F32)],
        compiler_params=_params(("parallel", "arbitrary")),
    )(*ins)


def _mem_probs(q, k):
    s = _dot(q.astype(BF16), k.astype(BF16), 1, 1) * (MEM_HEAD_DIM ** -0.5)
    e = jnp.exp(s - jnp.max(s, axis=-1, keepdims=True))
    return e / jnp.sum(e, axis=-1, keepdims=True)


def _mem_attn_fwd(mq, kv, *, tq, name):
    rows = mq.shape[0]

    def body(q_ref, k_ref, v_ref, o_ref):
        p = _mem_probs(q_ref[...], k_ref[...])
        o_ref[...] = _dot(p.astype(BF16), v_ref[...].astype(BF16), 1, 0)

    return pl.pallas_call(
        body, name=name, grid=(rows // tq, MEM_HEADS),
        in_specs=[pl.BlockSpec((tq, LANES), lambda i, h: (i, h)),
                  pl.BlockSpec((MEM_LEN, LANES), lambda i, h: (0, h)),
                  pl.BlockSpec((MEM_LEN, LANES), lambda i, h: (0, MEM_HEADS + h))],
        out_specs=pl.BlockSpec((tq, LANES), lambda i, h: (i, h)),
        out_shape=jax.ShapeDtypeStruct((rows, MEM_HEADS * LANES), F32),
        compiler_params=_params(("parallel", "parallel")),
    )(mq, kv, kv)


def _mem_attn_bwd(mq, kv, dmo, *, tq, name):
    rows = mq.shape[0]
    scale = MEM_HEAD_DIM ** -0.5

    def body(q_ref, k_ref, v_ref, do_ref, dq_ref, dk_ref, dv_ref):
        i = pl.program_id(1)
        qb = q_ref[...].astype(BF16)
        kb = k_ref[...].astype(BF16)
        vb = v_ref[...].astype(BF16)
        dob = do_ref[...].astype(BF16)
        p = _mem_probs(q_ref[...], k_ref[...])
        dp = _dot(dob, vb, 1, 1)
        ds = (p * (dp - jnp.sum(p * dp, axis=-1, keepdims=True)) * scale).astype(BF16)
        dq_ref[...] = _dot(ds, kb, 1, 0).astype(dq_ref.dtype)

        @pl.when(i == 0)
        def _():
            dk_ref[...] = jnp.zeros_like(dk_ref)
            dv_ref[...] = jnp.zeros_like(dv_ref)

        dk_ref[...] += _dot(ds, qb, 0, 0)
        dv_ref[...] += _dot(p.astype(BF16), dob, 0, 0)

    kv_out = pl.BlockSpec((MEM_LEN, LANES), lambda h, i: (0, h))
    kv_shape = jax.ShapeDtypeStruct((MEM_LEN, MEM_HEADS * LANES), F32)
    return pl.pallas_call(
        body, name=name, grid=(MEM_HEADS, rows // tq),
        in_specs=[pl.BlockSpec((tq, LANES), lambda h, i: (i, h)),
                  pl.BlockSpec((MEM_LEN, LANES), lambda h, i: (0, h)),
                  pl.BlockSpec((MEM_LEN, LANES), lambda h, i: (0, MEM_HEADS + h)),
                  pl.BlockSpec((tq, LANES), lambda h, i: (i, h))],
        out_specs=[pl.BlockSpec((tq, LANES), lambda h, i: (i, h)), kv_out, kv_out],
        out_shape=[jax.ShapeDtypeStruct((rows, MEM_HEADS * LANES), BF16), kv_shape, kv_shape],
        compiler_params=_params(("parallel", "arbitrary")),
    )(mq, kv, kv, dmo)


def _resident(shape):
    return pl.BlockSpec(shape, lambda i: (0, 0), pipeline_mode=pl.Buffered(1))


def _branch_merge_fwd(acts, wts, zg, b_gate, *, tm, name):
    rows = zg.shape[0]
    d = wts[0].shape[0]

    def body(s_ref, a_ref, m_ref, ws_ref, wa_ref, wm_ref, zg_ref, b_ref, o_ref):
        gt = _sigmoid(zg_ref[...] + b_ref[...])
        acc = None
        for k, (x_ref, w_ref) in enumerate(((s_ref, ws_ref), (a_ref, wa_ref), (m_ref, wm_ref))):
            term = gt[:, k * d:(k + 1) * d] * _dot(x_ref[...].astype(BF16), w_ref[...], 1, 1)
            acc = term if acc is None else acc + term
        o_ref[...] = acc.astype(BF16)

    return pl.pallas_call(
        body, name=name, grid=(rows // tm,),
        in_specs=[pl.BlockSpec((tm, x.shape[1]), lambda i: (i, 0)) for x in acts] + [_resident(w.shape) for w in wts]
        + [pl.BlockSpec((tm, 3 * d), lambda i: (i, 0)), pl.BlockSpec((1, 3 * d), lambda i: (0, 0))],
        out_specs=pl.BlockSpec((tm, d), lambda i: (i, 0)), out_shape=jax.ShapeDtypeStruct((rows, d), BF16),
        compiler_params=_params(("parallel",)),
    )(*acts, *wts, zg, b_gate)


def _branch_merge_bwd(dmerged, acts, wts, zg, b_gate, *, tm, name, carry=None):
    rows = zg.shape[0]
    d = wts[0].shape[0]

    def body(dm_ref, s_ref, a_ref, m_ref, ws_ref, wa_ref, wm_ref, zg_ref, b_ref,
             ds_ref, da_ref, dmm_ref, dws_ref, dwa_ref, dwm_ref, dzg_ref, db_ref):
        i = pl.program_id(0)

        @pl.when(i == 0)
        def _():
            for r in (dws_ref, dwa_ref, dwm_ref, db_ref):
                r[...] = jnp.zeros_like(r)

        gt = _sigmoid(zg_ref[...] + b_ref[...])
        dm = dm_ref[...]
        groups = ((s_ref, ws_ref, ds_ref, dws_ref), (a_ref, wa_ref, da_ref, dwa_ref), (m_ref, wm_ref, dmm_ref, dwm_ref))
        for k, (x_ref, w_ref, dx_ref, dw_ref) in enumerate(groups):
            cs = pl.ds(k * d, d)
            gk = gt[:, k * d:(k + 1) * d]
            xb = x_ref[...].astype(BF16)
            br = _dot(xb, w_ref[...], 1, 1)
            dbr = (dm * gk).astype(BF16)
            dx_ref[...] = _dot(dbr, w_ref[...], 1, 0)
            dw_ref[...] += _dot(dbr, xb, 0, 0)
            dzg = dm * br * gk * (1.0 - gk)
            dzg_ref[:, cs] = dzg.astype(BF16)
            db_ref[:, cs] += jnp.sum(dzg, axis=0, keepdims=True)

    row = lambda w: pl.BlockSpec((tm, w), lambda i: (i, 0))
    whole = lambda shape: pl.BlockSpec(shape, lambda i: (0, 0))
    res = _call_with_carry(
        body, carry, name=name, grid=(rows // tm,),
        in_specs=[row(d)] + [row(x.shape[1]) for x in acts] + [_resident(w.shape) for w in wts] + [row(3 * d), whole((1, 3 * d))],
        out_specs=[row(x.shape[1]) for x in acts] + [whole(w.shape) for w in wts] + [row(3 * d), whole((1, 3 * d))],
        out_shape=[jax.ShapeDtypeStruct(x.shape, F32) for x in acts] + [jax.ShapeDtypeStruct(w.shape, F32) for w in wts]
        + [jax.ShapeDtypeStruct((rows, 3 * d), BF16), jax.ShapeDtypeStruct((1, 3 * d), F32)],
        scratch=[], operands=[dmerged, *acts, *wts, zg, b_gate], semantics=("arbitrary",))
    return tuple(res) if carry is None else (tuple(res[:8]), list(res[8:]))


def _adamw(w, g, m, v, *, tr, name):
    rows, cols = w.shape[-2:]
    assert rows % tr == 0, (name, rows, tr)

    def body(w_ref, g_ref, m_ref, v_ref, g_out, d_ref, nm_ref, nv_ref):
        gv = g_ref[...]
        m2 = ADAM_B1 * m_ref[...] + (1.0 - ADAM_B1) * gv
        v2 = ADAM_B2 * v_ref[...] + (1.0 - ADAM_B2) * (gv * gv)
        m_hat = m2 / (1.0 - ADAM_B1 ** ADAM_STEP)
        v_hat = v2 / (1.0 - ADAM_B2 ** ADAM_STEP)
        g_out[...] = gv
        d_ref[...] = -ADAM_LR * (m_hat / (jnp.sqrt(v_hat) + ADAM_EPS) + ADAM_WD * w_ref[...])
        nm_ref[...] = m2
        nv_ref[...] = v2

    flat = pl.BlockSpec((tr, cols), lambda i: (i, 0))
    blk = flat if w.ndim == 2 else pl.BlockSpec((None, tr, cols), lambda i: (0, i, 0))
    shape = jax.ShapeDtypeStruct(w.shape, F32)
    return pl.pallas_call(
        body, name=name, grid=(rows // tr,), in_specs=[blk, flat, blk, blk], out_specs=[blk] * 4,
        out_shape=[shape] * 4, compiler_params=_params(("parallel",)),
    )(w, g, m, v)


ANY = pl.BlockSpec(memory_space=pl.ANY)


def _position():
    return lax.axis_index("x"), lax.axis_index("y"), lax.axis_index("c")


def _other_chips(x, y):
    return ((1 - x, y), (x, 1 - y), (1 - x, 1 - y))


def _remote(src, dst, send_sem, recv_sem, dev):
    return pltpu.make_async_remote_copy(src_ref=src, dst_ref=dst, send_sem=send_sem, recv_sem=recv_sem,
                                        device_id=dev, device_id_type=MESH)


def _gather_exchange(shards):
    nb = len(shards)

    def rows_of(i, owner, core):
        rs = shards[i].shape[0]
        return pl.ds(pl.multiple_of(owner * rs + core * (rs // 2), 16), rs // 2)

    def first_leg(ins, outs, send_sems, recv_sems, i, j):
        x, y, c = _position()
        px, py = _other_chips(x, y)[j]
        half = shards[i].shape[0] // 2
        mine = ins[i].at[pl.ds(pl.multiple_of(c * half, 16), half)]
        return _remote(mine, outs[i].at[rows_of(i, 2 * x + y, c)], send_sems.at[i, j], recv_sems.at[i, j], (px, py, c))

    def passed_on(outs, send_sems, recv_sems, i, j, core):
        x, y, c = _position()
        px, py = _other_chips(x, y)[j]
        rows = outs[i].at[rows_of(i, 2 * px + py, core)]
        return _remote(rows, rows, send_sems.at[i, 3 + j], recv_sems.at[i, 3 + j], (x, y, 1 - c))

    def own_block(ins, outs, send_sems, recv_sems, i):
        x, y, c = _position()
        rs = shards[i].shape[0]
        place = outs[i].at[pl.ds(pl.multiple_of((2 * x + y) * rs, 16), rs)]
        return _remote(ins[i], place, send_sems.at[i, 6], recv_sems.at[i, 6], (x, y, 1 - c))

    def start(ins, outs, send_sems, recv_sems):
        for i in range(nb):
            own_block(ins, outs, send_sems, recv_sems, i).start()
            for j in range(3):
                first_leg(ins, outs, send_sems, recv_sems, i, j).start()

    def finish(ins, outs, send_sems, recv_sems):
        x, y, c = _position()
        for i in range(nb):
            for j, (px, py) in enumerate(_other_chips(x, y)):
                landed = outs[i].at[rows_of(i, 2 * px + py, c)]
                _remote(landed, landed, send_sems.at[i, j], recv_sems.at[i, j], (px, py, c)).wait_recv()
                passed_on(outs, send_sems, recv_sems, i, j, c).start()
        for i in range(nb):
            own_block(ins, outs, send_sems, recv_sems, i).wait()
            for j in range(3):
                passed_on(outs, send_sems, recv_sems, i, j, 1 - c).wait_recv()
        for i in range(nb):
            for j in range(3):
                first_leg(ins, outs, send_sems, recv_sems, i, j).wait_send()
                passed_on(outs, send_sems, recv_sems, i, j, c).wait_send()

    return _Exchange(ins=list(shards), outs=[jax.ShapeDtypeStruct((N_CHIPS * s.shape[0], s.shape[1]), s.dtype) for s in shards],
                     aliases={}, sems=[(nb, 7), (nb, 7)], start=start, finish=finish)


def _run_exchange(ex, *, name):
    n_in, n_out = len(ex.ins), len(ex.outs)

    def body(*refs):
        c_in, c_out, sems = refs[:n_in], refs[n_in:n_in + n_out], refs[n_in + n_out:]
        ex.start(c_in, c_out, *sems)
        ex.finish(c_in, c_out, *sems)

    return pl.pallas_call(
        body, name=name, in_specs=[ANY] * n_in, out_specs=[ANY] * n_out, out_shape=list(ex.outs),
        input_output_aliases=dict(ex.aliases),
        scratch_shapes=[pltpu.SemaphoreType.DMA(s) for s in ex.sems],
    )(*ex.ins)


def _row_tile(rows):
    return max(t for t in range(16, min(rows, 512) + 1, 16) if rows % t == 0)


def _halves_exchange(grads):
    nb = len(grads)

    def copies(ins, outs, send_sems, recv_sems):
        x, y, c = _position()
        return [_remote(ins[i].at[:, 1 - c], outs[i], send_sems.at[i], recv_sems.at[i], (x, y, 1 - c)) for i in range(nb)]

    def start(ins, outs, send_sems, recv_sems):
        for cp in copies(ins, outs, send_sems, recv_sems):
            cp.start()

    def finish(ins, outs, send_sems, recv_sems):
        for cp in copies(ins, outs, send_sems, recv_sems):
            cp.wait()

    return _Exchange(ins=list(grads), outs=[jax.ShapeDtypeStruct((N_CHIPS, g.shape[2], g.shape[3]), F32) for g in grads],
                     aliases={}, sems=[(nb,), (nb,)], start=start, finish=finish)


def _join_exchanges(parts):
    assert all(not ex.aliases for ex in parts)

    def split(refs, counts):
        out, at = [], 0
        for k in counts:
            out.append(refs[at:at + k])
            at += k
        return out

    def run(which):
        def go(ins, outs, *sems):
            for ex, i, o, s in zip(parts, split(ins, [len(ex.ins) for ex in parts]), split(outs, [len(ex.outs) for ex in parts]),
                                   split(sems, [len(ex.sems) for ex in parts])):
                getattr(ex, which)(i, o, *s)
        return go

    return _Exchange(ins=[a for ex in parts for a in ex.ins], outs=[a for ex in parts for a in ex.outs], aliases={},
                     sems=[s for ex in parts for s in ex.sems], start=run("start"), finish=run("finish"))


def _pair_sum(g4, got, c_arr, *, name):
    _, _, half, cols = g4.shape
    tr = _row_tile(half)

    def body(c_ref, g_ref, t_ref, p_ref, pb_ref):
        sm = g_ref[...] + t_ref[...]
        p_ref[...] = sm
        pb_ref[...] = sm.astype(BF16)

    blk = pl.BlockSpec((None, tr, cols), lambda j, i, c_ref: (j, i, 0))
    grid_spec = pltpu.PrefetchScalarGridSpec(
        num_scalar_prefetch=1, grid=(N_CHIPS, half // tr),
        in_specs=[pl.BlockSpec((None, None, tr, cols), lambda j, i, c_ref: (j, c_ref[0], i, 0)), blk],
        out_specs=[blk, blk])
    return pl.pallas_call(
        body, name=name, grid_spec=grid_spec,
        out_shape=[jax.ShapeDtypeStruct((N_CHIPS, half, cols), F32), jax.ShapeDtypeStruct((N_CHIPS, half, cols), BF16)],
        compiler_params=_params(("parallel", "parallel")),
    )(c_arr, g4, got)


def _scatter_exchange(parts):
    nb = len(parts)

    def copies(ins, outs, send_sems, recv_sems):
        x, y, c = _position()
        return [_remote(ins[i].at[2 * px + py], outs[i].at[j], send_sems.at[i, j], recv_sems.at[i, j], (px, py, c))
                for i in range(nb) for j, (px, py) in enumerate(_other_chips(x, y))]

    def start(ins, outs, send_sems, recv_sems):
        for cp in copies(ins, outs, send_sems, recv_sems):
            cp.start()

    def finish(ins, outs, send_sems, recv_sems):
        for cp in copies(ins, outs, send_sems, recv_sems):
            cp.wait()

    return _Exchange(ins=list(parts), outs=[jax.ShapeDtypeStruct((3,) + p.shape[1:], p.dtype) for p in parts],
                     aliases={}, sems=[(nb, 3), (nb, 3)], start=start, finish=finish)


def _owner_sum(p, got, chip_arr, c_arr, *, replicated, name):
    _, half, cols = p.shape
    tr = _row_tile(half)

    def body(chip_ref, c_ref, p_ref, r_ref, o_ref):
        o_ref[...] = ((p_ref[...] + r_ref[0].astype(F32)) + r_ref[1].astype(F32)) + r_ref[2].astype(F32)

    if replicated:
        out_spec = pl.BlockSpec((None, None, tr, cols), lambda i, chip_ref, c_ref: (chip_ref[0], c_ref[0], i, 0))
        out_shape = jax.ShapeDtypeStruct((N_CHIPS, 2, half, cols), F32)
    else:
        out_spec = pl.BlockSpec((None, tr, cols), lambda i, chip_ref, c_ref: (c_ref[0], i, 0))
        out_shape = jax.ShapeDtypeStruct((2, half, cols), F32)
    grid_spec = pltpu.PrefetchScalarGridSpec(
        num_scalar_prefetch=2, grid=(half // tr,),
        in_specs=[pl.BlockSpec((None, tr, cols), lambda i, chip_ref, c_ref: (chip_ref[0], i, 0)),
                  pl.BlockSpec((3, tr, cols), lambda i, chip_ref, c_ref: (0, i, 0))],
        out_specs=out_spec)
    return pl.pallas_call(
        body, name=name, grid_spec=grid_spec, out_shape=out_shape,
        compiler_params=_params(("parallel",)),
    )(chip_arr, c_arr, p, got)


def _share_reduced(bufs):
    nb = len(bufs) - 1

    def body(*refs):
        outs = refs[nb + 1:2 * nb + 2]
        send_sems, recv_sems = refs[2 * nb + 2:]
        x, y, c = _position()
        chip = 2 * x + y
        sends = []
        for i in range(nb):
            cp = _remote(outs[i].at[c], outs[i].at[c], send_sems.at[i], recv_sems.at[i], (x, y, 1 - c))
            cp.start()
            sends.append(cp)
        small = outs[nb]
        peers = [(fx, fy, fc) for fx in (0, 1) for fy in (0, 1) for fc in (0, 1) if fx + fy + fc > 0]
        for k, (fx, fy, fc) in enumerate(peers):
            dev = (x ^ fx, y ^ fy, c ^ fc)
            cp = _remote(small.at[chip, c], small.at[chip, c], send_sems.at[nb + k], recv_sems.at[nb + k], dev)
            cp.start()
            sends.append(cp)
        for i in range(nb):
            dst = outs[i].at[1 - c]
            _remote(dst, dst, send_sems.at[i], recv_sems.at[i], (x, y, 1 - c)).wait_recv()
        for k, (fx, fy, fc) in enumerate(peers):
            dst = small.at[2 * (x ^ fx) + (y ^ fy), c ^ fc]
            _remote(dst, dst, send_sems.at[nb + k], recv_sems.at[nb + k], (x ^ fx, y ^ fy, c ^ fc)).wait_recv()
        for cp in sends:
            cp.wait_send()

    n_all = nb + 1
    return pl.pallas_call(
        body, name="grad_share_reduced", in_specs=[ANY] * n_all, out_specs=[ANY] * n_all,
        out_shape=[jax.ShapeDtypeStruct(b.shape, b.dtype) for b in bufs],
        input_output_aliases={i: i for i in range(n_all)},
        scratch_shapes=[pltpu.SemaphoreType.DMA((nb + 7,)), pltpu.SemaphoreType.DMA((nb + 7,))],
    )(*bufs)


class _GradReducer:
    def __init__(self, c_arr, chip_arr):
        self.c_arr, self.chip_arr = c_arr, chip_arr
        self.full, self.pairs, self.landed = {}, {}, {}

    def swap(self, names, grads):
        for n, g in zip(names, grads):
            self.full[n] = g.reshape(N_CHIPS, 2, g.shape[0] // (2 * N_CHIPS), g.shape[1])
        return _halves_exchange([self.full[n] for n in names])

    def swapped(self, names, bufs):
        for n, t in zip(names, bufs):
            self.pairs[n] = _pair_sum(self.full[n], t, self.c_arr, name="grad_pair_sum_" + n)

    def scatter(self, names):
        return _scatter_exchange([self.pairs[n][1] for n in names])

    def collect(self, names, bufs):
        self.landed.update(zip(names, bufs))

    def swap_now(self, names, grads):
        self.swapped(names, _run_exchange(self.swap(names, grads), name="grad_exchange_" + names[0]))

    def finish(self, names, grads, order):
        self.swap_now(names, grads)
        self.collect(names, _run_exchange(self.scatter(names), name="grad_scatter_" + names[0]))
        totals = [_owner_sum(self.pairs[n][0], self.landed[n], self.chip_arr, self.c_arr, replicated=(n == order[-1]),
                             name="grad_owner_sum_" + n) for n in order]
        return _share_reduced(totals)


def _pack_small(vals):
    flat = jnp.concatenate([vals[name].reshape(-1) for name, _ in SMALL])
    return jnp.pad(flat, (0, N_CHIPS * SMALL_ROWS * 1024 - SMALL_ELEMS)).reshape(N_CHIPS * SMALL_ROWS, 1024)


def _unpack_small(buf):
    flat = buf.reshape(-1)
    out, off = {}, 0
    for name, shape in SMALL:
        n = int(np.prod(shape))
        out[name] = flat[off:off + n].reshape(shape)
        off += n
    return out


EARLY_REDUCED = (("w_down",), ("w_up",), ("w_o", "w_ssm_br", "w_attn_br", "w_mem_br", "w_glu", "w_mem_kv"), ("w_in",))


def _device_step(x, mem, tgt, w, p, *, shards, reducer):
    rows = x.shape[0]
    w = dict(w)
    early = EARLY_REDUCED
    gb = {}
    gather_pending = shards is not None

    def riding(*stages):
        if reducer is None or not stages:
            return None
        return _join_exchanges([reducer.swap(names, [gb[n] for n in names]) if kind == "swap" else reducer.scatter(names)
                                for kind, names in stages])

    def arrived(stages, res):
        if reducer is None or not stages:
            return res
        main, bufs = res
        for kind, names in stages:
            (reducer.swapped if kind == "swap" else reducer.collect)(names, bufs[:len(names)])
            bufs = bufs[len(names):]
        return main

    def fetching(names):
        return _gather_exchange([shards[n] for n in names]) if gather_pending else None

    def fetched(names, res):
        if not gather_pending:
            return res
        w.update(zip(names, res[1]))
        return res[0]

    first_use = (("w_in",), ("w_glu", "w_ssm_br", "w_attn_br", "w_mem_kv", "w_mem_br", "w_o", "w_up"), ("w_down",))
    g1, gm, g2 = p["norm1_g"], p["mem_norm_g"], p["norm2_g"]
    gf = p["final_g"].reshape(1, D_MODEL)
    ssm_args = (p["ssm_lambda_re"][0], p["ssm_lambda_im"][0], p["ssm_log_dt"][0], p["ssm_b_re"][0],
                p["ssm_b_im"][0], p["ssm_c_re"][0], p["ssm_c_im"][0])
    (a_lay, b_blk, c_blk), ssm_vjp = jax.vjp(_ssm_matrices, *ssm_args)
    a_conj = a_lay * _to_scan_layout(jnp.stack([jnp.ones((N_STATES,), F32), -jnp.ones((N_STATES,), F32)]))[None, :]
    dd = p["ssm_d"].reshape(1, SSM_WIDTH)
    mm = _matmul

    n1 = fetched(first_use[0], _rmsnorm_fwd(x, g1, tm=512, carry=fetching(first_use[0]), name="norm1"))
    win_t = w["w_in"]
    splits = ((OFF_U, OFF_QKV - OFF_U), (OFF_QKV, OFF_MQ - OFF_QKV), (OFF_MQ, OFF_ZG - OFF_MQ), (OFF_ZG, IN_WIDTH - OFF_ZG))
    u, qkv, mq, zg = fetched(first_use[1], _split_matmul(n1, win_t, splits, tm=512, carry=fetching(first_use[1]),
                                                         vmem=VMEM_LIMIT_WIDE_BYTES, name="in_proj"))

    u_i = _interleave(u)
    ends = _ssm_ends(a_lay, u_i, b_blk, transpose=False, reverse=False, tt=512, name="ssm_fwd_ends")
    s, ys_i, s_entry = _ssm_fwd(a_lay, u_i, b_blk, c_blk, ends, tt=512, name="ssm_fwd")
    ys = _deinterleave(ys_i)
    y0, tglu, y2 = _glu_fwd(ys, u, dd, w["w_glu"], p["b_glu"], tm=512, name="glu_fwd")

    outs, lses = [], []
    for g, (_, d) in enumerate(ATTN_PATTERNS):
        o_g, lse_g = _attn_fwd(qkv, g, d, name=f"attn_fwd_{g}")
        outs.append(o_g)
        lses.append(lse_g)
    o, lse = _attn_merge(outs, lses, tm=1024, name="attn_merge")

    mn = _rmsnorm_fwd(mem, gm, tm=MEM_LEN, name="mem_norm")
    kv = mm(mn, w["w_mem_kv"], m=MEM_LEN, n=1024, k=1024, tm=MEM_LEN, tn=1024, tk=1024, out_dtypes=(F32,), name="mem_kv")
    mo = _mem_attn_fwd(mq, kv, tq=1024, name="mem_attn_fwd")

    branch_acts = (y2, o, mo)
    branch_wts = (w["w_ssm_br"], w["w_attn_br"], w["w_mem_br"])
    merged = _branch_merge_fwd(branch_acts, branch_wts, zg, p["b_gate"], tm=256, name="branch_merge_fwd")
    h1, n2 = mm(merged, w["w_o"], m=rows, n=1024, k=1024, tm=1024, tn=1024, tk=1024, out_dtypes=(F32, BF16),
                aux=((x, "mn"), (g2, "row")), epilogue=_residual_norm_epilogue, name="out_proj")
    relu2 = lambda acc: (jnp.square(jnp.maximum(acc, 0.0)),)
    act = fetched(first_use[2], _sum_matmul([n2], w["w_up"], [0], tb=True, tm=512, out_dtype=BF16, epilogue=relu2,
                                            carry=fetching(first_use[2]), name="mlp_up"))
    dh2, d_gf, sq_err = _sum_matmul([act], w["w_down"], [0], tm=512, aux=((h1, "mn"), (tgt, "mn"), (gf, "row")),
                                    epilogue=_loss_head_epilogue, n_sums=2, name="mlp_down")
    loss = (0.5 / D_MODEL) * jnp.sum(sq_err)

    gs = {"final_g": d_gf.reshape(D_MODEL)}
    drelu2 = lambda acc, actv: (acc * (2.0 * jnp.sqrt(actv.astype(F32))),)
    dup = mm(dh2, w["w_down"], m=rows, n=D_FF, k=1024, tb=True, tm=1024, tn=2048, tk=1024, out_dtypes=(BF16,),
             aux=((act, "mn"),), epilogue=drelu2, name="d_act")
    gb["w_down"] = mm(act, dh2, m=D_FF, n=1024, k=rows, ta=True, tm=1024, tn=1024, tk=2048, out_dtypes=(F32,), name="dw_down")
    stages = (("swap", early[0]),)
    gb["w_up"] = arrived(stages, mm(dup, n2, m=D_FF, n=1024, k=rows, ta=True, tm=1024, tn=1024, tk=2048,
                                    out_dtypes=(F32,), carry=riding(*stages), name="dw_up"))
    stages = (("scatter", early[0]), ("swap", early[1]))
    dh1, gs["norm2_g"] = arrived(stages, _sum_matmul([dup], w["w_up"], [0], tm=512, aux=((h1, "mn"), (dh2, "mn"), (g2, "row")),
                                                     epilogue=_rmsnorm_bwd_epilogue, n_sums=1, carry=riding(*stages), name="d_n2"))
    dmerged = mm(dh1, w["w_o"], m=rows, n=1024, k=1024, tb=True, tm=1024, tn=1024, tk=1024, out_dtypes=(F32,), name="d_merged")
    gb["w_o"] = mm(merged, dh1, m=1024, n=1024, k=rows, ta=True, tm=1024, tn=1024, tk=2048, out_dtypes=(F32,), name="dw_o")
    stages = (("scatter", early[1]),)
    (dy2, do, dmo, gb["w_ssm_br"], gb["w_attn_br"], gb["w_mem_br"], dzg, gs["b_gate"]) = arrived(stages, _branch_merge_bwd(
        dmerged, branch_acts, branch_wts, zg, p["b_gate"], tm=256, carry=riding(*stages), name="branch_merge_bwd"))

    dy0, dt, y1, gs["b_glu"], d_dd = _glu_bwd(dy2, y0, tglu, u, w["w_glu"], tm=512, name="glu_bwd")
    gs["ssm_d"] = d_dd.reshape(1, SSM_GROUPS, SSM_GROUP_SIZE)
    gb["w_glu"] = mm(y1, dt, m=512, n=512, k=rows, ta=True, tm=512, tn=512, tk=1024, out_dtypes=(F32,), name="dw_glu")
    dy0_i = _interleave(dy0)
    lam_ends = _ssm_ends(a_conj, dy0_i, c_blk, transpose=True, reverse=True, tt=512, name="ssm_bwd_ends")
    du_i, d_b_blk, d_c_blk, d_a_lay = _ssm_bwd(a_conj, dy0_i, u_i, s, s_entry, b_blk, c_blk, dd, lam_ends, tt=512,
                                                name="ssm_bwd")
    du = _deinterleave(du_i)
    d_ssm = ssm_vjp((d_a_lay, d_b_blk, d_c_blk))
    for name, val in zip(("ssm_lambda_re", "ssm_lambda_im", "ssm_log_dt", "ssm_b_re", "ssm_b_im", "ssm_c_re", "ssm_c_im"), d_ssm):
        gs[name] = val[None]

    dqkv = None
    for g, (_, d) in enumerate(ATTN_PATTERNS):
        dqkv = _attn_bwd(qkv, do, o, lse, g, d, dqkv, name=f"attn_bwd_{g}")

    dmq, dmk, dmv = _mem_attn_bwd(mq, kv, dmo, tq=1024, name="mem_attn_bwd")
    dkv = jnp.concatenate([dmk, dmv], axis=1)
    gb["w_mem_kv"] = mm(mn, dkv, m=1024, n=1024, k=MEM_LEN, ta=True, tm=1024, tn=1024, tk=MEM_LEN, out_dtypes=(F32,), name="dw_mem_kv")
    dmn = mm(dkv, w["w_mem_kv"], m=MEM_LEN, n=1024, k=1024, tb=True, tm=MEM_LEN, tn=1024, tk=1024, out_dtypes=(F32,), name="d_mn")
    _, gs["mem_norm_g"] = _rmsnorm_bwd(mem, gm, dmn, None, tm=MEM_LEN, name="mem_norm_bwd")

    pieces = ((du, OFF_U, "u"), (dqkv[0], OFF_QKV, "q"), (dqkv[1], OFF_QKV + 768, "k"), (dqkv[2], OFF_QKV + 1536, "v"),
              (dmq, OFF_MQ, "mq"), (dzg, OFF_ZG, "zg"))
    dw_rows = []
    for piece, off, tag in pieces:
        width = piece.shape[1]
        tmw = 1024 if width % 1024 == 0 else (768 if width == 768 else 512)
        stages = {"q": (("swap", early[2]),), "zg": (("scatter", early[2]),)}.get(tag, ())
        dw_rows.append(arrived(stages, mm(piece, n1, m=width, n=1024, k=rows, ta=True, tm=tmw, tn=1024, tk=2048,
                                          out_dtypes=(F32,), carry=riding(*stages), name="dw_in_" + tag)))
    gb["w_in"] = jnp.concatenate(dw_rows, axis=0)
    if reducer is not None:
        reducer.swap_now(early[3], [gb["w_in"]])
    stages = (("scatter", early[3]),)
    dx, gs["norm1_g"] = arrived(stages, _sum_matmul(
        [piece for piece, _, _ in pieces], win_t, [off for _, off, _ in pieces], tm=512,
        aux=((x, "mn"), (dh1, "mn"), (g1, "row")), epilogue=_rmsnorm_bwd_epilogue, n_sums=1,
        carry=riding(*stages), vmem=VMEM_LIMIT_WIDE_BYTES, name="d_n1"))
    return loss, dx, gb, gs


def kernel(x, mem, norm1_g, mem_norm_g, w_in, b_gate, ssm_lambda_re, ssm_lambda_im, ssm_log_dt, ssm_b_re, ssm_b_im, ssm_c_re, ssm_c_im, ssm_d, w_glu, b_glu, w_ssm_br, w_attn_br, w_mem_kv, w_mem_br, w_o, norm2_g, w_up, w_down, final_g, loss_target, m_norm1_g, m_mem_norm_g, m_w_in, m_b_gate, m_ssm_lambda_re, m_ssm_lambda_im, m_ssm_log_dt, m_ssm_b_re, m_ssm_b_im, m_ssm_c_re, m_ssm_c_im, m_ssm_d, m_w_glu, m_b_glu, m_w_ssm_br, m_w_attn_br, m_w_mem_kv, m_w_mem_br, m_w_o, m_norm2_g, m_w_up, m_w_down, m_final_g, v_norm1_g, v_mem_norm_g, v_w_in, v_b_gate, v_ssm_lambda_re, v_ssm_lambda_im, v_ssm_log_dt, v_ssm_b_re, v_ssm_b_im, v_ssm_c_re, v_ssm_c_im, v_ssm_d, v_w_glu, v_b_glu, v_w_ssm_br, v_w_attn_br, v_w_mem_kv, v_w_mem_br, v_w_o, v_norm2_g, v_w_up, v_w_down, v_final_g):
    env = dict(locals())
    weights = {n: env[n] for n in WEIGHT_ORDER}
    moms = {n: env["m_" + n] for n in WEIGHT_ORDER}
    vels = {n: env["v_" + n] for n in WEIGHT_ORDER}

    chip = 2 * lax.axis_index("x") + lax.axis_index("y")
    wire = [weights[n].reshape(weights[n].shape[-2:]).astype(BF16) for n, _, _ in BIG]
    wire = dict(zip([n for n, _, _ in BIG], [s.T if tr else s for s, (_, tr, _) in zip(wire, BIG)]))
    small = {n: weights[n] for n, _ in SMALL}

    reducer = _GradReducer(lax.axis_index("c").astype(jnp.int32).reshape(1), chip.astype(jnp.int32).reshape(1))
    loss, dx, gb, gs = _device_step(x[0], mem[0], loss_target[0], {}, small, shards=wire, reducer=reducer)
    *shards, small_grad = reducer.finish(["small"], [_pack_small(gs)], [n for n, _, _ in BIG] + ["small"])
    grads = {}
    for (n, tr, _), sh in zip(BIG, shards):
        sh = sh.reshape(2 * sh.shape[1], sh.shape[2])
        grads[n] = sh.T if tr else sh
    small_grad = small_grad.reshape(N_CHIPS * SMALL_ROWS, 1024)
    grads_small = _unpack_small(small_grad)

    delta, new_m, new_v = {}, {}, {}
    for n, _, _ in BIG:
        grads[n], delta[n], new_m[n], new_v[n] = _adamw(weights[n], grads[n], moms[n], vels[n],
                                                        tr=min(weights[n].shape[-2], 256), name="adamw_" + n)
    _, ds_, ms_, vs_ = _adamw(_pack_small(small), small_grad,
                              _pack_small({n: moms[n] for n, _ in SMALL}), _pack_small({n: vels[n] for n, _ in SMALL}),
                              tr=N_CHIPS * SMALL_ROWS, name="adamw_small")
    for dst, buf in ((delta, ds_), (new_m, ms_), (new_v, vs_)):
        dst.update(_unpack_small(buf))
    grads.update(grads_small)

    total_loss = lax.psum(loss, ("x", "y", "c"))
    return (total_loss, dx[None], *[grads[n] for n in WEIGHT_ORDER], *[delta[n] for n in WEIGHT_ORDER],
            *[new_m[n] for n in WEIGHT_ORDER], *[new_v[n] for n in WEIGHT_ORDER])
```

```python
import functools
import math

import numpy as np
import jax
import jax.numpy as jnp
from jax import lax
from jax.experimental import pallas as pl
from jax.experimental.pallas import tpu as pltpu

F32 = jnp.float32
BF16 = jnp.bfloat16

D_MODEL = 1024
SSM_GROUPS = 32
SSM_GROUP_SIZE = 16
SSM_STATE = 64
SSM_WIDTH = 512
N_STATES = SSM_GROUPS * SSM_STATE
SCAN_CB = 1024
ATTN_PATTERNS = ((128, 1), (512, 4), (2048, 16))
ATTN_HEAD_DIM = 64
ATTN_Q = 128
MEM_LEN = 256
MEM_HEAD_DIM = 128
MEM_HEADS = 4
D_FF = 4096
OFF_U, OFF_QKV, OFF_MQ, OFF_ZG = 0, 512, 2816, 3328
IN_WIDTH = 6400
RMS_EPS = 1e-6
NEG_INF = -1e30
ADAM_LR, ADAM_B1, ADAM_B2, ADAM_EPS, ADAM_WD, ADAM_STEP = 0.001, 0.9, 0.999, 1e-08, 0.01, 10

VMEM_LIMIT_BYTES = 48 * 1024 * 1024
VMEM_LIMIT_WIDE_BYTES = 56 * 1024 * 1024
LANES = 128
MESH = pl.DeviceIdType.MESH
N_CHIPS = 4

SCAN_SEGS = 8
SCAN_GROUPS = SCAN_CB // SSM_STATE

BIG = (("w_in", True, (6400, 1024)), ("w_glu", False, (512, 512)), ("w_ssm_br", True, (1024, 512)),
       ("w_attn_br", True, (1024, 256)), ("w_mem_kv", False, (1024, 1024)), ("w_mem_br", True, (1024, 512)),
       ("w_o", False, (1024, 1024)), ("w_up", True, (4096, 1024)), ("w_down", False, (4096, 1024)))
SMALL = (("norm1_g", (1, 1024)), ("mem_norm_g", (1, 1024)), ("b_gate", (1, 3072)),
         ("ssm_lambda_re", (1, 32, 64)), ("ssm_lambda_im", (1, 32, 64)), ("ssm_log_dt", (1, 32)),
         ("ssm_b_re", (1, 32, 64, 16)), ("ssm_b_im", (1, 32, 64, 16)), ("ssm_c_re", (1, 32, 16, 64)),
         ("ssm_c_im", (1, 32, 16, 64)), ("ssm_d", (1, 32, 16)), ("b_glu", (1, 512)),
         ("norm2_g", (1, 1024)), ("final_g", (1024,)))
WEIGHT_ORDER = ("norm1_g", "mem_norm_g", "w_in", "b_gate", "ssm_lambda_re", "ssm_lambda_im", "ssm_log_dt",
                "ssm_b_re", "ssm_b_im", "ssm_c_re", "ssm_c_im", "ssm_d", "w_glu", "b_glu", "w_ssm_br",
                "w_attn_br", "w_mem_kv", "w_mem_br", "w_o", "norm2_g", "w_up", "w_down", "final_g")
SMALL_ELEMS = sum(int(np.prod(s)) for _, s in SMALL)
SMALL_ROWS = 64


def _params(sem, vmem=VMEM_LIMIT_BYTES):
    return pltpu.CompilerParams(dimension_semantics=sem, vmem_limit_bytes=vmem)


def _sigmoid(v):
    return 0.5 * jnp.tanh(0.5 * v) + 0.5


_GELU_C = math.sqrt(2.0 / math.pi)


def _gelu(v):
    return 0.5 * v * (1.0 + jnp.tanh(_GELU_C * (v + 0.044715 * v * v * v)))


def _gelu_grad(v):
    th = jnp.tanh(_GELU_C * (v + 0.044715 * v * v * v))
    return 0.5 * (1.0 + th) + 0.5 * v * (1.0 - th * th) * _GELU_C * (1.0 + 3.0 * 0.044715 * v * v)


def _dot(a, b, ca, cb):
    return lax.dot_general(a, b, (((ca,), (cb,)), ((), ())), preferred_element_type=F32)


class _Exchange:
    def __init__(self, ins, outs, aliases, sems, start, finish):
        self.ins, self.outs, self.aliases, self.sems, self.start, self.finish = ins, outs, aliases, sems, start, finish


def _matmul(a, b, *, m, n, k, ta=False, tb=False, tm, tn, tk, out_dtypes, name,
            aux=(), epilogue=None, n_sums=0, carry=None):
    assert m % tm == 0 and n % tn == 0 and k % tk == 0, (name, m, n, k, tm, tn, tk)
    assert n_sums == 0 or tn == n, name
    nk = k // tk
    n_aux = len(aux)
    n_tiles = len(out_dtypes)
    n_out = n_tiles + n_sums
    a_spec = pl.BlockSpec((tk, tm), lambda i, j, kk: (kk, i)) if ta else pl.BlockSpec((tm, tk), lambda i, j, kk: (i, kk))
    b_spec = pl.BlockSpec((tn, tk), lambda i, j, kk: (j, kk)) if tb else pl.BlockSpec((tk, tn), lambda i, j, kk: (kk, j))
    aux_specs = []
    for _, kind in aux:
        if kind == "mn":
            aux_specs.append(pl.BlockSpec((tm, tn), lambda i, j, kk: (i, j)))
        else:
            aux_specs.append(pl.BlockSpec((1, tn), lambda i, j, kk: (0, j)))
    ca = 0 if ta else 1
    cb = 1 if tb else 0

    def finish(acc, aux_refs, out_refs, row_tile):
        outs = (acc,) if epilogue is None else epilogue(acc, *[r[...] for r in aux_refs])
        for o_ref, o in zip(out_refs[:n_tiles], outs[:n_tiles]):
            o_ref[...] = o.astype(o_ref.dtype)
        _accumulate_over_rows(out_refs[n_tiles:], outs[n_tiles:], row_tile)

    def body(a_ref, b_ref, *rest):
        aux_refs = rest[:n_aux]
        out_refs = rest[n_aux:n_aux + n_out]
        row_tile = pl.program_id(0)
        prod = _dot(a_ref[...].astype(BF16), b_ref[...].astype(BF16), ca, cb)
        if nk == 1:
            finish(prod, aux_refs, out_refs, row_tile)
            return
        acc_ref = rest[n_aux + n_out]
        kk = pl.program_id(2)

        @pl.when(kk == 0)
        def _():
            acc_ref[...] = prod

        @pl.when(jnp.logical_and(kk > 0, kk < nk - 1))
        def _():
            acc_ref[...] += prod

        @pl.when(kk == nk - 1)
        def _():
            finish(acc_ref[...] + prod, aux_refs, out_refs, row_tile)

    tile = pl.BlockSpec((tm, tn), lambda i, j, kk: (i, j))
    col_sum = pl.BlockSpec((1, tn), lambda i, j, kk: (0, j))
    res = _call_with_carry(
        body, carry, name=name, grid=(m // tm, n // tn, nk), in_specs=[a_spec, b_spec] + aux_specs,
        out_specs=[tile] * n_tiles + [col_sum] * n_sums,
        out_shape=[jax.ShapeDtypeStruct((m, n), dt) for dt in out_dtypes] + [jax.ShapeDtypeStruct((1, n), F32)] * n_sums,
        scratch=[pltpu.VMEM((tm, tn), F32)] if nk > 1 else [], operands=[a, b] + [x for x, _ in aux],
        semantics=("arbitrary" if n_sums else "parallel", "parallel", "arbitrary"))
    main = res[0] if n_out == 1 else tuple(res[:n_out])
    return main if carry is None else (main, list(res[n_out:]))


def _accumulate_over_rows(sum_refs, terms, row_tile):
    for s_ref, term in zip(sum_refs, terms):
        @pl.when(row_tile == 0)
        def _():
            s_ref[...] = term

        @pl.when(row_tile > 0)
        def _():
            s_ref[...] += term


def _call_with_carry(body, carry, *, name, grid, in_specs, out_specs, out_shape, scratch, operands, semantics,
                     vmem=VMEM_LIMIT_BYTES):
    if carry is None:
        return pl.pallas_call(body, name=name, grid=grid, in_specs=in_specs, out_specs=out_specs, out_shape=out_shape,
                              scratch_shapes=scratch, compiler_params=_params(semantics, vmem))(*operands)
    n_in, n_cin, n_out, n_cout, n_scr = len(operands), len(carry.ins), len(out_shape), len(carry.outs), len(scratch)

    def hosted(*refs):
        main_in, c_in = refs[:n_in], refs[n_in:n_in + n_cin]
        main_out = refs[n_in + n_cin:n_in + n_cin + n_out]
        c_out = refs[n_in + n_cin + n_out:n_in + n_cin + n_out + n_cout]
        rest = refs[n_in + n_cin + n_out + n_cout:]
        ids = [pl.program_id(t) for t in range(len(grid))]
        first = functools.reduce(jnp.logical_and, [i == 0 for i in ids])
        last = functools.reduce(jnp.logical_and, [i == g - 1 for i, g in zip(ids, grid)])

        @pl.when(first)
        def _():
            carry.start(c_in, c_out, *rest[n_scr:])

        body(*main_in, *main_out, *rest[:n_scr])

        @pl.when(last)
        def _():
            carry.finish(c_in, c_out, *rest[n_scr:])

    return pl.pallas_call(
        hosted, name=name, grid=grid,
        in_specs=list(in_specs) + [ANY] * n_cin, out_specs=list(out_specs) + [ANY] * n_cout,
        out_shape=list(out_shape) + list(carry.outs),
        input_output_aliases={n_in + i: n_out + o for i, o in carry.aliases.items()},
        scratch_shapes=list(scratch) + [pltpu.SemaphoreType.DMA(s) for s in carry.sems],
        compiler_params=_params(("arbitrary",) * len(grid), vmem),
    )(*operands, *carry.ins)


def _sum_matmul(pieces, b, offs, *, tm, name, tb=False, out_dtype=F32, aux=(), epilogue=None, n_sums=0, carry=None,
                vmem=VMEM_LIMIT_BYTES):
    m = pieces[0].shape[0]
    n = b.shape[0] if tb else b.shape[1]
    npieces, n_aux = len(pieces), len(aux)
    assert not tb or npieces == 1

    def body(*refs):
        b_ref = refs[npieces]
        aux_refs = refs[npieces + 1:npieces + 1 + n_aux]
        out_refs = refs[npieces + 1 + n_aux:]
        acc = None
        for p_ref, off in zip(refs[:npieces], offs):
            lhs = p_ref[...].astype(BF16)
            part = _dot(lhs, b_ref[...], 1, 1) if tb else _dot(lhs, b_ref[pl.ds(off, p_ref.shape[1]), :], 1, 0)
            acc = part if acc is None else acc + part
        outs = (acc,) if epilogue is None else epilogue(acc, *[r[...] for r in aux_refs])
        out_refs[0][...] = outs[0].astype(out_dtype)
        _accumulate_over_rows(out_refs[1:], outs[1:], pl.program_id(0))

    row = pl.BlockSpec((tm, n), lambda i: (i, 0))
    vec = pl.BlockSpec((1, n), lambda i: (0, 0))
    res = _call_with_carry(
        body, carry, name=name, grid=(m // tm,),
        in_specs=[pl.BlockSpec((tm, p.shape[1]), lambda i: (i, 0)) for p in pieces] + [_resident(b.shape)]
        + [row if kind == "mn" else vec for _, kind in aux],
        out_specs=[row] + [vec] * n_sums,
        out_shape=[jax.ShapeDtypeStruct((m, n), out_dtype)] + [jax.ShapeDtypeStruct((1, n), F32)] * n_sums,
        scratch=[], operands=list(pieces) + [b] + [x for x, _ in aux], semantics=("arbitrary" if n_sums else "parallel",),
        vmem=vmem)
    main = res[0] if n_sums == 0 else tuple(res[:1 + n_sums])
    return main if carry is None else (main, list(res[1 + n_sums:]))


def _split_matmul(a, b_t, splits, *, tm, name, carry=None, vmem=VMEM_LIMIT_BYTES):
    m, k = a.shape

    def body(a_ref, b_ref, *out_refs):
        av = a_ref[...].astype(BF16)
        for (row0, width), o_ref in zip(splits, out_refs):
            o_ref[...] = _dot(av, b_ref[pl.ds(row0, width), :], 1, 1)

    res = _call_with_carry(
        body, carry, name=name, grid=(m // tm,),
        in_specs=[pl.BlockSpec((tm, k), lambda i: (i, 0)), _resident(b_t.shape)],
        out_specs=[pl.BlockSpec((tm, width), lambda i: (i, 0)) for _, width in splits],
        out_shape=[jax.ShapeDtypeStruct((m, width), F32) for _, width in splits],
        scratch=[], operands=[a, b_t], semantics=("parallel",), vmem=vmem)
    outs = tuple(res[:len(splits)])
    return outs if carry is None else (outs, list(res[len(splits):]))


def _rmsnorm_fwd(x, g, *, tm, name, carry=None):
    rows, d = x.shape

    def body(x_ref, g_ref, o_ref):
        xv = x_ref[...]
        r = lax.rsqrt(jnp.mean(xv * xv, axis=-1, keepdims=True) + RMS_EPS)
        o_ref[...] = (xv * r * g_ref[...]).astype(o_ref.dtype)

    res = _call_with_carry(
        body, carry, name=name, grid=(rows // tm,),
        in_specs=[pl.BlockSpec((tm, d), lambda i: (i, 0)), pl.BlockSpec((1, d), lambda i: (0, 0))],
        out_specs=[pl.BlockSpec((tm, d), lambda i: (i, 0))], out_shape=[jax.ShapeDtypeStruct((rows, d), BF16)],
        scratch=[], operands=[x, g], semantics=("parallel",))
    return res[0] if carry is None else (res[0], list(res[1:]))


def _residual_norm_epilogue(acc, xv, gv):
    h = acc + xv
    r = lax.rsqrt(jnp.mean(h * h, axis=-1, keepdims=True) + RMS_EPS)
    return h, h * r * gv


def _rmsnorm_bwd_epilogue(dy, xv, resv, gv):
    r = lax.rsqrt(jnp.mean(xv * xv, axis=-1, keepdims=True) + RMS_EPS)
    xhat = xv * r
    dyg = dy * gv
    dx = r * (dyg - xhat * jnp.mean(dyg * xhat, axis=-1, keepdims=True)) + resv
    return dx, jnp.sum(dy * xhat, axis=0, keepdims=True)


def _rmsnorm_bwd(x, g, dy, res, *, tm, name):
    rows, d = x.shape
    has_res = res is not None

    def body(x_ref, g_ref, dy_ref, *rest):
        if has_res:
            res_ref, dx_ref, dg_ref = rest
        else:
            dx_ref, dg_ref = rest
        i = pl.program_id(0)
        xv = x_ref[...]
        r = lax.rsqrt(jnp.mean(xv * xv, axis=-1, keepdims=True) + RMS_EPS)
        xhat = xv * r
        dyv = dy_ref[...]
        dyg = dyv * g_ref[...]
        dx = r * (dyg - xhat * jnp.mean(dyg * xhat, axis=-1, keepdims=True))
        if has_res:
            dx = dx + res_ref[...]
        dx_ref[...] = dx

        @pl.when(i == 0)
        def _():
            dg_ref[...] = jnp.zeros_like(dg_ref)

        dg_ref[...] += jnp.sum(dyv * xhat, axis=0, keepdims=True)

    row_spec = pl.BlockSpec((tm, d), lambda i: (i, 0))
    vec_spec = pl.BlockSpec((1, d), lambda i: (0, 0))
    ins = [x, g, dy] + ([res] if has_res else [])
    return pl.pallas_call(
        body, name=name, grid=(rows // tm,),
        in_specs=[row_spec, vec_spec, row_spec] + ([row_spec] if has_res else []),
        out_specs=[row_spec, vec_spec],
        out_shape=[jax.ShapeDtypeStruct((rows, d), F32), jax.ShapeDtypeStruct((1, d), F32)],
        compiler_params=_params(("arbitrary",)),
    )(*ins)


def _loss_head_epilogue(acc, hv, tgtv, gv):
    xv = acc + hv
    r = lax.rsqrt(jnp.mean(xv * xv, axis=-1, keepdims=True) + RMS_EPS)
    xhat = xv * r
    err = xhat * gv - tgtv
    dyv = err * (1.0 / D_MODEL)
    dyg = dyv * gv
    dh = r * (dyg - xhat * jnp.mean(dyg * xhat, axis=-1, keepdims=True))
    return dh, jnp.sum(dyv * xhat, axis=0, keepdims=True), jnp.sum(err * err, axis=0, keepdims=True)


def _to_scan_layout(v):
    lead = v.shape[:-2]
    v = v.reshape(lead + (2, N_STATES // SCAN_CB, SCAN_CB))
    v = jnp.swapaxes(v, -3, -2)
    return v.reshape(lead + (2 * N_STATES,))


def _ssm_matrices(lam_re, lam_im, log_dt, b_re, b_im, c_re, c_im):
    dt = jnp.exp(log_dt)[:, None]
    mag = jnp.exp(lam_re * dt)
    a_re, a_im = mag * jnp.cos(lam_im * dt), mag * jnp.sin(lam_im * dt)
    nr, ni = a_re - 1.0, a_im
    den = lam_re * lam_re + lam_im * lam_im
    coef_re = (nr * lam_re + ni * lam_im) / den
    coef_im = (ni * lam_re - nr * lam_im) / den
    bb_re = coef_re[..., None] * b_re - coef_im[..., None] * b_im
    bb_im = coef_re[..., None] * b_im + coef_im[..., None] * b_re
    a_lay = _to_scan_layout(jnp.stack([a_re.reshape(-1), a_im.reshape(-1)], axis=0))[None, :]
    nblk = SSM_GROUPS // SCAN_GROUPS
    eye = jnp.eye(SCAN_GROUPS, dtype=F32)

    def b_block(bb):
        bb = bb.reshape(nblk, SCAN_GROUPS, SSM_STATE, SSM_GROUP_SIZE)
        return jnp.einsum("gk,jkph->jghkp", eye, bb).reshape(nblk, SCAN_GROUPS * SSM_GROUP_SIZE, SCAN_CB)

    b_blk = jnp.concatenate([b_block(bb_re), b_block(bb_im)], axis=2)

    def c_block(cc):
        cc = cc.reshape(nblk, SCAN_GROUPS, SSM_GROUP_SIZE, SSM_STATE)
        return jnp.einsum("gk,jghp->jkpgh", eye, cc).reshape(nblk, SCAN_CB, SCAN_GROUPS * SSM_GROUP_SIZE)

    c_blk = jnp.concatenate([c_block(c_re), -c_block(c_im)], axis=1)
    return a_lay, b_blk, c_blk


def _interleave(v):
    rows, c = v.shape
    return v.reshape(SCAN_SEGS, rows // SCAN_SEGS, c).transpose(1, 0, 2).reshape(rows, c)


def _deinterleave(v):
    rows, c = v.shape
    return v.reshape(rows // SCAN_SEGS, SCAN_SEGS, c).transpose(1, 0, 2).reshape(rows, c)


def _scan_groups(a_ref, bu_ref, o_ref, state, *, reverse, tt, unroll=4):
    cb = SCAN_CB
    ar = jnp.broadcast_to(a_ref[:, :cb], (SCAN_SEGS, cb))
    ai = jnp.broadcast_to(a_ref[:, cb:], (SCAN_SEGS, cb))
    ngroups = tt // SCAN_SEGS

    def step(i, st):
        sr, si = st
        r0 = pl.multiple_of(((ngroups - 1 - i) if reverse else i) * SCAN_SEGS, SCAN_SEGS)
        blk = bu_ref[pl.ds(r0, SCAN_SEGS), :]
        nr = ar * sr - ai * si + blk[:, :cb]
        ni = ar * si + ai * sr + blk[:, cb:]
        if o_ref is not None:
            o_ref[pl.ds(r0, SCAN_SEGS), :] = jnp.concatenate([nr, ni], axis=1)
        return nr, ni

    return lax.fori_loop(0, ngroups, step, state, unroll=unroll)


def _segment_entries(a_ref, e_ref, init_ref, *, reverse, seg_len):
    cb = SCAN_CB
    n_sq = seg_len.bit_length() - 1
    assert 1 << n_sq == seg_len, seg_len
    pr, pi = a_ref[:, :cb], a_ref[:, cb:]
    for _ in range(n_sq):
        pr, pi = pr * pr - pi * pi, 2.0 * pr * pi
    cr = jnp.zeros((1, cb), F32)
    ci = jnp.zeros((1, cb), F32)
    order = range(SCAN_SEGS - 1, -1, -1) if reverse else range(SCAN_SEGS)
    for k, seg in enumerate(order):
        if k > 0:
            prev = seg + 1 if reverse else seg - 1
            er, ei = e_ref[prev:prev + 1, :cb], e_ref[prev:prev + 1, cb:]
            cr, ci = pr * cr - pi * ci + er, pr * ci + pi * cr + ei
        init_ref[seg:seg + 1, :] = jnp.concatenate([cr, ci], axis=1)


def _ssm_specs(nt, tt, nch, reverse):
    cb = SCAN_CB
    tmap = (lambda j, kk: (nt - 1 - kk, j)) if reverse else (lambda j, kk: (kk, j))
    nmap = (lambda j, kk: (jnp.maximum(nt - 2 - kk, 0), j)) if reverse else (lambda j, kk: (jnp.minimum(kk + 1, nt - 1), j))
    return dict(a=pl.BlockSpec((1, 2 * cb), lambda j, kk: (0, j)),
                seg=pl.BlockSpec((SCAN_SEGS, 2 * cb), lambda j, kk: (0, j)),
                chan=pl.BlockSpec((tt, nch), tmap),
                next=pl.BlockSpec((tt, nch), nmap),
                state=pl.BlockSpec((tt, 2 * cb), tmap),
                b=pl.BlockSpec((None, nch, 2 * cb), lambda j, kk: (j, 0, 0)),
                c=pl.BlockSpec((None, 2 * cb, nch), lambda j, kk: (j, 0, 0)))


def _ssm_ends(a_lay, x, blocks, *, transpose, reverse, tt, name):
    rows = x.shape[0]
    nblk = blocks.shape[0]
    nch = x.shape[1] // nblk
    cb = SCAN_CB
    nt = rows // tt
    sp = _ssm_specs(nt, tt, nch, reverse)

    def body(a_ref, x_ref, xn_ref, w_ref, e_ref, even_ref, odd_ref):
        kk = pl.program_id(1)

        def product(src_ref, dst_ref):
            dst_ref[...] = _dot(src_ref[...].astype(BF16), w_ref[...].astype(BF16), 1, 1 if transpose else 0)

        @pl.when(kk == 0)
        def _():
            e_ref[...] = jnp.zeros_like(e_ref)
            product(x_ref, even_ref)

        def phase(cur_ref, next_ref):
            product(xn_ref, next_ref)
            sr, si = _scan_groups(a_ref, cur_ref, None, (e_ref[:, :cb], e_ref[:, cb:]), reverse=reverse, tt=tt, unroll=True)
            e_ref[...] = jnp.concatenate([sr, si], axis=1)

        @pl.when(kk % 2 == 0)
        def _():
            phase(even_ref, odd_ref)

        @pl.when(kk % 2 == 1)
        def _():
            phase(odd_ref, even_ref)

    return pl.pallas_call(
        body, name=name, grid=(nblk, nt),
        in_specs=[sp["a"], sp["chan"], sp["next"], sp["c"] if transpose else sp["b"]],
        out_specs=sp["seg"],
        out_shape=jax.ShapeDtypeStruct((SCAN_SEGS, nblk * 2 * cb), F32),
        scratch_shapes=[pltpu.VMEM((tt, 2 * cb), F32), pltpu.VMEM((tt, 2 * cb), F32)],
        compiler_params=_params(("parallel", "arbitrary")),
    )(a_lay, x, x, blocks)


def _ssm_fwd(a_lay, u, b_blk, c_blk, ends, *, tt, name):
    rows = u.shape[0]
    nblk = b_blk.shape[0]
    nch = u.shape[1] // nblk
    cb = SCAN_CB
    nt = rows // tt
    sp = _ssm_specs(nt, tt, nch, False)

    def body(a_ref, e_ref, u_ref, un_ref, b_ref, c_ref, s_ref, y_ref, init_ref, carry_ref, even_ref, odd_ref):
        kk = pl.program_id(1)

        def product(src_ref, dst_ref):
            dst_ref[...] = _dot(src_ref[...].astype(BF16), b_ref[...].astype(BF16), 1, 0)

        @pl.when(kk == 0)
        def _():
            _segment_entries(a_ref, e_ref, init_ref, reverse=False, seg_len=rows // SCAN_SEGS)
            carry_ref[...] = init_ref[...]
            product(u_ref, even_ref)

        def phase(cur_ref, next_ref):
            product(un_ref, next_ref)
            sr, si = _scan_groups(a_ref, cur_ref, cur_ref, (carry_ref[:, :cb], carry_ref[:, cb:]), reverse=False, tt=tt,
                                  unroll=True)
            carry_ref[...] = jnp.concatenate([sr, si], axis=1)
            sb = cur_ref[...].astype(BF16)
            s_ref[...] = sb
            y_ref[...] = _dot(sb, c_ref[...].astype(BF16), 1, 0)

        @pl.when(kk % 2 == 0)
        def _():
            phase(even_ref, odd_ref)

        @pl.when(kk % 2 == 1)
        def _():
            phase(odd_ref, even_ref)

    return pl.pallas_call(
        body, name=name, grid=(nblk, nt),
        in_specs=[sp["a"], sp["seg"], sp["chan"], sp["next"], sp["b"], sp["c"]],
        out_specs=[sp["state"], sp["chan"], sp["seg"]],
        out_shape=[jax.ShapeDtypeStruct((rows, nblk * 2 * cb), BF16), jax.ShapeDtypeStruct((rows, nblk * nch), F32),
                   jax.ShapeDtypeStruct((SCAN_SEGS, nblk * 2 * cb), F32)],
        scratch_shapes=[pltpu.VMEM((SCAN_SEGS, 2 * cb), F32), pltpu.VMEM((tt, 2 * cb), F32), pltpu.VMEM((tt, 2 * cb), F32)],
        compiler_params=_params(("parallel", "arbitrary")),
    )(a_lay, ends, u, u, b_blk, c_blk)


def _ssm_bwd(a_conj, dy, u, s, s_entry, b_blk, c_blk, dd, ends, *, tt, name):
    rows = u.shape[0]
    nblk = b_blk.shape[0]
    nch = u.shape[1] // nblk
    cb = SCAN_CB
    nt = rows // tt
    sp = _ssm_specs(nt, tt, nch, True)
    groups_per_tile = tt // SCAN_SEGS
    before = pl.BlockSpec((2 * SCAN_SEGS, 2 * cb),
                          lambda j, kk: (jnp.maximum((nt - 1 - kk) * (groups_per_tile // 2) - 1, 0), j))

    def body(a_ref, e_ref, dy_ref, dyn_ref, u_ref, s_ref, before_ref, entry_ref, b_ref, c_ref, dd_ref,
             du_ref, db_ref, dc_ref, da_ref, carry_ref, even_ref, odd_ref):
        kk = pl.program_id(1)

        def product(src_ref, dst_ref):
            dst_ref[...] = _dot(src_ref[...].astype(BF16), c_ref[...].astype(BF16), 1, 1)

        @pl.when(kk == 0)
        def _():
            _segment_entries(a_ref, e_ref, carry_ref, reverse=True, seg_len=rows // SCAN_SEGS)
            db_ref[...] = jnp.zeros_like(db_ref)
            dc_ref[...] = jnp.zeros_like(dc_ref)
            da_ref[...] = jnp.zeros_like(da_ref)
            product(dy_ref, even_ref)

        def pair(lv, pv):
            lre, lim, pre, pim = lv[:, :cb], lv[:, cb:], pv[:, :cb], pv[:, cb:]
            return (jnp.sum(lre * pre + lim * pim, axis=0, keepdims=True),
                    jnp.sum(lim * pre - lre * pim, axis=0, keepdims=True))

        def phase(lam_ref, next_ref):
            product(dyn_ref, next_ref)
            lr, li = _scan_groups(a_ref, lam_ref, lam_ref, (carry_ref[:, :cb], carry_ref[:, cb:]), reverse=True, tt=tt,
                                  unroll=True)
            carry_ref[...] = jnp.concatenate([lr, li], axis=1)
            first = jnp.where(kk == nt - 1, entry_ref[...], before_ref[...].astype(F32)[SCAN_SEGS:])
            rest = tt - SCAN_SEGS
            r1, i1 = pair(lam_ref[pl.ds(SCAN_SEGS, rest), :], s_ref[...].astype(F32)[:rest])
            r0, i0 = pair(lam_ref[pl.ds(0, SCAN_SEGS), :], first)
            da_ref[...] += jnp.concatenate([r1 + r0, i1 + i0], axis=1)
            dyv = dy_ref[...]
            lamb = lam_ref[...].astype(BF16)
            du_ref[...] = _dot(lamb, b_ref[...].astype(BF16), 1, 1) + dd_ref[...] * dyv
            db_ref[...] += _dot(u_ref[...].astype(BF16), lamb, 0, 0)
            dc_ref[...] += _dot(s_ref[...], dyv.astype(BF16), 0, 0)

        @pl.when(kk % 2 == 0)
        def _():
            phase(even_ref, odd_ref)

        @pl.when(kk % 2 == 1)
        def _():
            phase(odd_ref, even_ref)

    return pl.pallas_call(
        body, name=name, grid=(nblk, nt),
        in_specs=[sp["a"], sp["seg"], sp["chan"], sp["next"], sp["chan"], sp["state"], before, sp["seg"], sp["b"], sp["c"],
                  pl.BlockSpec((1, nch), lambda j, kk: (0, j))],
        out_specs=[sp["chan"], sp["b"], sp["c"], pl.BlockSpec((1, 2 * cb), lambda j, kk: (0, j))],
        out_shape=[jax.ShapeDtypeStruct((rows, nblk * nch), F32), jax.ShapeDtypeStruct(b_blk.shape, F32),
                   jax.ShapeDtypeStruct(c_blk.shape, F32), jax.ShapeDtypeStruct((1, nblk * 2 * cb), F32)],
        scratch_shapes=[pltpu.VMEM((SCAN_SEGS, 2 * cb), F32), pltpu.VMEM((tt, 2 * cb), F32), pltpu.VMEM((tt, 2 * cb), F32)],
        compiler_params=_params(("parallel", "arbitrary")),
    )(a_conj, ends, dy, dy, u, s, s, s_entry, b_blk, c_blk, dd)


def _glu_fwd(ys, u, dd, w_glu, b_glu, *, tm, name):
    rows, w = ys.shape

    def body(ys_ref, u_ref, dd_ref, w_ref, b_ref, y2_ref):
        y0 = ys_ref[...] + dd_ref[...] * u_ref[...]
        y1 = _gelu(y0)
        t = _dot(y1.astype(BF16), w_ref[...], 1, 0) + b_ref[...]
        y2_ref[...] = (y1 * _sigmoid(t)).astype(BF16)

    row = pl.BlockSpec((tm, w), lambda i: (i, 0))
    vec = pl.BlockSpec((1, w), lambda i: (0, 0))
    return pl.pallas_call(
        body, name=name, grid=(rows // tm,),
        in_specs=[row, row, vec, pl.BlockSpec((w, w), lambda i: (0, 0)), vec],
        out_specs=row,
        out_shape=jax.ShapeDtypeStruct((rows, w), BF16),
        compiler_params=_params(("parallel",)),
    )(ys, u, dd, w_glu, b_glu)


def _glu_bwd(dy2, ys, u, dd, w_glu, b_glu, *, tm, name):
    rows, w = ys.shape

    def body(dy2_ref, ys_ref, u_ref, d_ref, w_ref, b_ref, dy0_ref, dt_ref, y1_ref, db_ref, dd_ref):
        i = pl.program_id(0)
        y0 = ys_ref[...] + d_ref[...] * u_ref[...]
        y1 = _gelu(y0)
        sg = _sigmoid(_dot(y1.astype(BF16), w_ref[...], 1, 0) + b_ref[...])
        dy2v = dy2_ref[...]
        dt = dy2v * y1 * sg * (1.0 - sg)
        dy1 = dy2v * sg + _dot(dt.astype(BF16), w_ref[...], 1, 1)
        dy0 = dy1 * _gelu_grad(y0)
        dy0_ref[...] = dy0
        dt_ref[...] = dt.astype(BF16)
        y1_ref[...] = y1.astype(BF16)

        @pl.when(i == 0)
        def _():
            db_ref[...] = jnp.zeros_like(db_ref)
            dd_ref[...] = jnp.zeros_like(dd_ref)

        db_ref[...] += jnp.sum(dt, axis=0, keepdims=True)
        dd_ref[...] += jnp.sum(dy0 * u_ref[...], axis=0, keepdims=True)

    row = pl.BlockSpec((tm, w), lambda i: (i, 0))
    vec = pl.BlockSpec((1, w), lambda i: (0, 0))
    return pl.pallas_call(
        body, name=name, grid=(rows // tm,),
        in_specs=[row, row, row, vec, pl.BlockSpec((w, w), lambda i: (0, 0)), vec],
        out_specs=[row, row, row, vec, vec],
        out_shape=[jax.ShapeDtypeStruct((rows, w), F32), jax.ShapeDtypeStruct((rows, w), BF16),
                   jax.ShapeDtypeStruct((rows, w), BF16), jax.ShapeDtypeStruct((1, w), F32),
                   jax.ShapeDtypeStruct((1, w), F32)],
        compiler_params=_params(("arbitrary",)),
    )(dy2, ys, u, dd, w_glu, b_glu)


ATTN_TILE = 2048


def _attn_geometry(rows, d):
    sb = ATTN_Q * d
    tr = max(sb, min(ATTN_TILE, rows))
    assert rows % tr == 0 and tr % sb == 0, (rows, d)
    return sb, tr, rows // tr, tr // sb


def _attn_masks():
    qi = lax.broadcasted_iota(jnp.int32, (2 * ATTN_Q, 2 * ATTN_Q), 0) % ATTN_Q
    kj = lax.broadcasted_iota(jnp.int32, (2 * ATTN_Q, 2 * ATTN_Q), 1)
    own_ok = jnp.logical_and(kj >= ATTN_Q, kj - ATTN_Q <= qi)
    prev_ok = jnp.logical_and(kj < ATTN_Q, kj >= qi)
    bias_first = jnp.where(own_ok, 0.0, NEG_INF)
    bias_other = jnp.where(jnp.logical_or(own_ok, prev_ok), 0.0, NEG_INF)
    head0 = lax.broadcasted_iota(jnp.int32, (ATTN_Q, LANES), 1) < ATTN_HEAD_DIM
    return bias_first, bias_other, head0


def _attn_rows(base, n, d):
    return pl.ds(pl.multiple_of(base, ATTN_Q), n) if d == 1 else pl.ds(base, n, stride=d)


def _stack_heads(v, head0):
    return jnp.concatenate([jnp.where(head0, v, 0.0), jnp.where(head0, 0.0, v)], axis=0)


def _unstack_heads(v, head0):
    return jnp.where(head0, v[:ATTN_Q], v[ATTN_Q:])


def _fill_keys(buf, prev_ref, cur_ref, sb):
    buf[pl.ds(0, sb), :] = prev_ref[...]
    buf[pl.ds(sb, cur_ref.shape[0]), :] = cur_ref[...]


def _attn_fwd(qkv, g, d, *, name):
    rows = qkv.shape[0]
    sb, tr, ntiles, nsub = _attn_geometry(rows, d)
    qc, kc, vc = 2 * g, 6 + 2 * g, 12 + 2 * g
    scale = ATTN_HEAD_DIM ** -0.5

    def body(q_ref, kc_ref, kp_ref, vc_ref, vp_ref, o_ref, lse_ref, kbuf, vbuf):
        n = pl.program_id(0)
        _fill_keys(kbuf, kp_ref, kc_ref, sb)
        _fill_keys(vbuf, vp_ref, vc_ref, sb)
        bias_first, bias_other, head0 = _attn_masks()

        def per_block(idx, carry):
            j, r = idx // d, idx % d
            base = j * sb + r
            bias = jnp.where(jnp.logical_and(n == 0, j == 0), bias_first, bias_other)
            qrows = _attn_rows(base, ATTN_Q, d)
            krows = _attn_rows(base, 2 * ATTN_Q, d)
            qs = (_stack_heads(q_ref[qrows, :], head0) * scale).astype(BF16)
            s = _dot(qs, kbuf[krows, :].astype(BF16), 1, 1) + bias
            mx = jnp.max(s, axis=-1, keepdims=True)
            p = jnp.exp(s - mx)
            den = jnp.sum(p, axis=-1, keepdims=True)
            pv = _dot(p.astype(BF16), vbuf[krows, :].astype(BF16), 1, 0) / den
            o_ref[qrows, :] = _unstack_heads(pv, head0)
            lse_ref[qrows, :] = _unstack_heads(jnp.broadcast_to(mx + jnp.log(den), (2 * ATTN_Q, LANES)), head0)
            return carry

        lax.fori_loop(0, nsub * d, per_block, 0, unroll=True)

    def cur(col):
        return pl.BlockSpec((tr, LANES), lambda n, hp: (n, col + hp))

    def prev(col):
        return pl.BlockSpec((sb, LANES), lambda n, hp: (jnp.maximum(n * nsub - 1, 0), col + hp))

    out_spec = pl.BlockSpec((tr, LANES), lambda n, hp: (n, hp))
    return pl.pallas_call(
        body, name=name, grid=(ntiles, 2),
        in_specs=[cur(qc), cur(kc), prev(kc), cur(vc), prev(vc)],
        out_specs=[out_spec, out_spec],
        out_shape=[jax.ShapeDtypeStruct((rows, 2 * LANES), F32), jax.ShapeDtypeStruct((rows, 2 * LANES), F32)],
        scratch_shapes=[pltpu.VMEM((sb + tr, LANES), F32), pltpu.VMEM((sb + tr, LANES), F32)],
        compiler_params=_params(("parallel", "parallel")),
    )(qkv, qkv, qkv, qkv, qkv)


def _attn_merge(outs, lses, *, tm, name):
    rows, w = outs[0].shape

    def body(o0, o1, o2, l0, l1, l2, o_ref, lse_ref):
        a0, a1, a2 = l0[...], l1[...], l2[...]
        mx = jnp.maximum(jnp.maximum(a0, a1), a2)
        e0, e1, e2 = jnp.exp(a0 - mx), jnp.exp(a1 - mx), jnp.exp(a2 - mx)
        den = e0 + e1 + e2
        o_ref[...] = (e0 / den) * o0[...] + (e1 / den) * o1[...] + (e2 / den) * o2[...]
        lse_ref[...] = mx + jnp.log(den)

    row = pl.BlockSpec((tm, w), lambda i: (i, 0))
    return pl.pallas_call(
        body, name=name, grid=(rows // tm,), in_specs=[row] * 6, out_specs=[row, row],
        out_shape=[jax.ShapeDtypeStruct((rows, w), F32), jax.ShapeDtypeStruct((rows, w), F32)],
        compiler_params=_params(("parallel",)),
    )(*outs, *lses)


def _attn_bwd(qkv, do, o, lse, g, d, prev, *, name):
    rows = qkv.shape[0]
    sb, tr, ntiles, nsub = _attn_geometry(rows, d)
    qc, kc, vc = 2 * g, 6 + 2 * g, 12 + 2 * g
    scale = ATTN_HEAD_DIM ** -0.5

    def body(q_ref, kc_ref, kp_ref, vc_ref, vp_ref, do_ref, o_ref, lse_ref, dq_ref, dk_ref, dv_ref,
             kbuf, vbuf, dk_acc, dv_acc):
        n = pl.program_id(1)

        @pl.when(n == 0)
        def _():
            dk_acc[pl.ds(0, tr), :] = jnp.zeros((tr, LANES), F32)
            dv_acc[pl.ds(0, tr), :] = jnp.zeros((tr, LANES), F32)

        @pl.when(n < ntiles)
        def _():
            dk_acc[pl.ds(tr, tr), :] = jnp.zeros((tr, LANES), F32)
            dv_acc[pl.ds(tr, tr), :] = jnp.zeros((tr, LANES), F32)
            _fill_keys(kbuf, kp_ref, kc_ref, sb)
            _fill_keys(vbuf, vp_ref, vc_ref, sb)
            bias_first, bias_other, head0 = _attn_masks()
            lane = lax.broadcasted_iota(jnp.int32, (ATTN_Q, LANES), 1)

            def per_block(idx, carry):
                j, r = idx // d, idx % d
                base = j * sb + r
                bias = jnp.where(jnp.logical_and(n == 0, j == 0), bias_first, bias_other)
                qrows = _attn_rows(base, ATTN_Q, d)
                krows = _attn_rows(base, 2 * ATTN_Q, d)
                arows = _attn_rows(base + (tr - sb), 2 * ATTN_Q, d)
                qs = (_stack_heads(q_ref[qrows, :], head0) * scale).astype(BF16)
                dos = _stack_heads(do_ref[qrows, :], head0)
                dosb = dos.astype(BF16)
                ov = o_ref[qrows, :]
                delta = jnp.sum(dos * jnp.concatenate([ov, ov], axis=0), axis=-1, keepdims=True)
                lsev = lse_ref[qrows, :]
                lse_s = jnp.concatenate(
                    [jnp.sum(jnp.where(lane == h * ATTN_HEAD_DIM, lsev, 0.0), axis=-1, keepdims=True) for h in range(2)], axis=0)
                kb = kbuf[krows, :].astype(BF16)
                vb = vbuf[krows, :].astype(BF16)
                p = jnp.exp(_dot(qs, kb, 1, 1) + bias - lse_s)
                ds = (p * (_dot(dosb, vb, 1, 1) - delta)).astype(BF16)
                dq_ref[qrows, :] = _unstack_heads(_dot(ds, kb, 1, 0), head0) * scale
                dk_acc[arows, :] += _dot(ds, qs, 0, 0)
                dv_acc[arows, :] += _dot(p.astype(BF16), dosb, 0, 0)
                return carry

            lax.fori_loop(0, nsub * d, per_block, 0, unroll=True)

        dk_ref[...] = dk_acc[pl.ds(0, tr), :]
        dv_ref[...] = dv_acc[pl.ds(0, tr), :]
        dk_acc[pl.ds(0, tr), :] = dk_acc[pl.ds(tr, tr), :]
        dv_acc[pl.ds(0, tr), :] = dv_acc[pl.ds(tr, tr), :]

    def cur(n):
        return jnp.minimum(n, ntiles - 1)

    def spec(col, prev):
        if prev:
            return pl.BlockSpec((sb, LANES), lambda hp, n: (jnp.maximum(cur(n) * nsub - 1, 0), col + hp))
        return pl.BlockSpec((tr, LANES), lambda hp, n: (cur(n), col + hp))

    row_spec = pl.BlockSpec((tr, LANES), lambda hp, n: (cur(n), hp))
    dq_out = pl.BlockSpec((tr, LANES), lambda hp, n: (cur(n), 2 * g + hp))
    kv_out = pl.BlockSpec((tr, LANES), lambda hp, n: (jnp.maximum(n - 1, 0), 2 * g + hp))
    shape = jax.ShapeDtypeStruct((rows, len(ATTN_PATTERNS) * 2 * LANES), F32)
    ins = [qkv, qkv, qkv, qkv, qkv, do, o, lse]
    in_specs = [spec(qc, False), spec(kc, False), spec(kc, True), spec(vc, False), spec(vc, True),
                row_spec, row_spec, row_spec]
    aliases = {}
    if prev is not None:
        aliases = {len(ins) + t: t for t in range(3)}
        ins = ins + list(prev)
        in_specs = in_specs + [ANY] * 3
    n_in = len(ins)

    def entry(*refs):
        body(*refs[:8], *refs[n_in:])

    return pl.pallas_call(
        entry, name=name, grid=(2, ntiles + 1),
        in_specs=in_specs,
        out_specs=[dq_out, kv_out, kv_out],
        out_shape=[shape, shape, shape],
        input_output_aliases=aliases,
        scratch_shapes=[pltpu.VMEM((sb + tr, LANES), F32), pltpu.VMEM((sb + tr, LANES), F32),
                        pltpu.VMEM((2 * tr, LANES), F32), pltpu.VMEM((2 * tr, LANES), F32)],
        compiler_params=_params(("parallel", "arbitrary")),
    )(*ins)


def _mem_probs(q, k):
    s = _dot(q.astype(BF16), k.astype(BF16), 1, 1) * (MEM_HEAD_DIM ** -0.5)
    e = jnp.exp(s - jnp.max(s, axis=-1, keepdims=True))
    return e / jnp.sum(e, axis=-1, keepdims=True)


def _mem_attn_fwd(mq, kv, *, tq, name):
    rows = mq.shape[0]

    def body(q_ref, k_ref, v_ref, o_ref):
        p = _mem_probs(q_ref[...], k_ref[...])
        o_ref[...] = _dot(p.astype(BF16), v_ref[...].astype(BF16), 1, 0)

    return pl.pallas_call(
        body, name=name, grid=(rows // tq, MEM_HEADS),
        in_specs=[pl.BlockSpec((tq, LANES), lambda i, h: (i, h)),
                  pl.BlockSpec((MEM_LEN, LANES), lambda i, h: (0, h)),
                  pl.BlockSpec((MEM_LEN, LANES), lambda i, h: (0, MEM_HEADS + h))],
        out_specs=pl.BlockSpec((tq, LANES), lambda i, h: (i, h)),
        out_shape=jax.ShapeDtypeStruct((rows, MEM_HEADS * LANES), F32),
        compiler_params=_params(("parallel", "parallel")),
    )(mq, kv, kv)


def _mem_attn_bwd(mq, kv, dmo, *, tq, name):
    rows = mq.shape[0]
    scale = MEM_HEAD_DIM ** -0.5

    def body(q_ref, k_ref, v_ref, do_ref, dq_ref, dk_ref, dv_ref):
        i = pl.program_id(1)
        qb = q_ref[...].astype(BF16)
        kb = k_ref[...].astype(BF16)
        vb = v_ref[...].astype(BF16)
        dob = do_ref[...].astype(BF16)
        p = _mem_probs(q_ref[...], k_ref[...])
        dp = _dot(dob, vb, 1, 1)
        ds = (p * (dp - jnp.sum(p * dp, axis=-1, keepdims=True)) * scale).astype(BF16)
        dq_ref[...] = _dot(ds, kb, 1, 0).astype(dq_ref.dtype)

        @pl.when(i == 0)
        def _():
            dk_ref[...] = jnp.zeros_like(dk_ref)
            dv_ref[...] = jnp.zeros_like(dv_ref)

        dk_ref[...] += _dot(ds, qb, 0, 0)
        dv_ref[...] += _dot(p.astype(BF16), dob, 0, 0)

    kv_out = pl.BlockSpec((MEM_LEN, LANES), lambda h, i: (0, h))
    kv_shape = jax.ShapeDtypeStruct((MEM_LEN, MEM_HEADS * LANES), F32)
    return pl.pallas_call(
        body, name=name, grid=(MEM_HEADS, rows // tq),
        in_specs=[pl.BlockSpec((tq, LANES), lambda h, i: (i, h)),
                  pl.BlockSpec((MEM_LEN, LANES), lambda h, i: (0, h)),
                  pl.BlockSpec((MEM_LEN, LANES), lambda h, i: (0, MEM_HEADS + h)),
                  pl.BlockSpec((tq, LANES), lambda h, i: (i, h))],
        out_specs=[pl.BlockSpec((tq, LANES), lambda h, i: (i, h)), kv_out, kv_out],
        out_shape=[jax.ShapeDtypeStruct((rows, MEM_HEADS * LANES), BF16), kv_shape, kv_shape],
        compiler_params=_params(("parallel", "arbitrary")),
    )(mq, kv, kv, dmo)


def _resident(shape):
    return pl.BlockSpec(shape, lambda i: (0, 0), pipeline_mode=pl.Buffered(1))


def _branch_merge_fwd(acts, wts, zg, b_gate, *, tm, name):
    rows = zg.shape[0]
    d = wts[0].shape[0]

    def body(s_ref, a_ref, m_ref, ws_ref, wa_ref, wm_ref, zg_ref, b_ref, o_ref):
        gt = _sigmoid(zg_ref[...] + b_ref[...])
        acc = None
        for k, (x_ref, w_ref) in enumerate(((s_ref, ws_ref), (a_ref, wa_ref), (m_ref, wm_ref))):
            term = gt[:, k * d:(k + 1) * d] * _dot(x_ref[...].astype(BF16), w_ref[...], 1, 1)
            acc = term if acc is None else acc + term
        o_ref[...] = acc.astype(BF16)

    return pl.pallas_call(
        body, name=name, grid=(rows // tm,),
        in_specs=[pl.BlockSpec((tm, x.shape[1]), lambda i: (i, 0)) for x in acts] + [_resident(w.shape) for w in wts]
        + [pl.BlockSpec((tm, 3 * d), lambda i: (i, 0)), pl.BlockSpec((1, 3 * d), lambda i: (0, 0))],
        out_specs=pl.BlockSpec((tm, d), lambda i: (i, 0)), out_shape=jax.ShapeDtypeStruct((rows, d), BF16),
        compiler_params=_params(("parallel",)),
    )(*acts, *wts, zg, b_gate)


def _branch_merge_bwd(dmerged, acts, wts, zg, b_gate, *, tm, name, carry=None):
    rows = zg.shape[0]
    d = wts[0].shape[0]

    def body(dm_ref, s_ref, a_ref, m_ref, ws_ref, wa_ref, wm_ref, zg_ref, b_ref,
             ds_ref, da_ref, dmm_ref, dws_ref, dwa_ref, dwm_ref, dzg_ref, db_ref):
        i = pl.program_id(0)

        @pl.when(i == 0)
        def _():
            for r in (dws_ref, dwa_ref, dwm_ref, db_ref):
                r[...] = jnp.zeros_like(r)

        gt = _sigmoid(zg_ref[...] + b_ref[...])
        dm = dm_ref[...]
        groups = ((s_ref, ws_ref, ds_ref, dws_ref), (a_ref, wa_ref, da_ref, dwa_ref), (m_ref, wm_ref, dmm_ref, dwm_ref))
        for k, (x_ref, w_ref, dx_ref, dw_ref) in enumerate(groups):
            cs = pl.ds(k * d, d)
            gk = gt[:, k * d:(k + 1) * d]
            xb = x_ref[...].astype(BF16)
            br = _dot(xb, w_ref[...], 1, 1)
            dbr = (dm * gk).astype(BF16)
            dx_ref[...] = _dot(dbr, w_ref[...], 1, 0)
            dw_ref[...] += _dot(dbr, xb, 0, 0)
            dzg = dm * br * gk * (1.0 - gk)
            dzg_ref[:, cs] = dzg.astype(BF16)
            db_ref[:, cs] += jnp.sum(dzg, axis=0, keepdims=True)

    row = lambda w: pl.BlockSpec((tm, w), lambda i: (i, 0))
    whole = lambda shape: pl.BlockSpec(shape, lambda i: (0, 0))
    res = _call_with_carry(
        body, carry, name=name, grid=(rows // tm,),
        in_specs=[row(d)] + [row(x.shape[1]) for x in acts] + [_resident(w.shape) for w in wts] + [row(3 * d), whole((1, 3 * d))],
        out_specs=[row(x.shape[1]) for x in acts] + [whole(w.shape) for w in wts] + [row(3 * d), whole((1, 3 * d))],
        out_shape=[jax.ShapeDtypeStruct(x.shape, F32) for x in acts] + [jax.ShapeDtypeStruct(w.shape, F32) for w in wts]
        + [jax.ShapeDtypeStruct((rows, 3 * d), BF16), jax.ShapeDtypeStruct((1, 3 * d), F32)],
        scratch=[], operands=[dmerged, *acts, *wts, zg, b_gate], semantics=("arbitrary",))
    return tuple(res) if carry is None else (tuple(res[:8]), list(res[8:]))


def _adamw(w, g, m, v, *, tr, name):
    rows, cols = w.shape[-2:]
    assert rows % tr == 0, (name, rows, tr)

    def body(w_ref, g_ref, m_ref, v_ref, g_out, d_ref, nm_ref, nv_ref):
        gv = g_ref[...]
        m2 = ADAM_B1 * m_ref[...] + (1.0 - ADAM_B1) * gv
        v2 = ADAM_B2 * v_ref[...] + (1.0 - ADAM_B2) * (gv * gv)
        m_hat = m2 / (1.0 - ADAM_B1 ** ADAM_STEP)
        v_hat = v2 / (1.0 - ADAM_B2 ** ADAM_STEP)
        g_out[...] = gv
        d_ref[...] = -ADAM_LR * (m_hat / (jnp.sqrt(v_hat) + ADAM_EPS) + ADAM_WD * w_ref[...])
        nm_ref[...] = m2
        nv_ref[...] = v2

    flat = pl.BlockSpec((tr, cols), lambda i: (i, 0))
    blk = flat if w.ndim == 2 else pl.BlockSpec((None, tr, cols), lambda i: (0, i, 0))
    shape = jax.ShapeDtypeStruct(w.shape, F32)
    return pl.pallas_call(
        body, name=name, grid=(rows // tr,), in_specs=[blk, flat, blk, blk], out_specs=[blk] * 4,
        out_shape=[shape] * 4, compiler_params=_params(("parallel",)),
    )(w, g, m, v)


ANY = pl.BlockSpec(memory_space=pl.ANY)


def _position():
    return lax.axis_index("x"), lax.axis_index("y"), lax.axis_index("c")


def _other_chips(x, y):
    return ((1 - x, y), (x, 1 - y), (1 - x, 1 - y))


def _remote(src, dst, send_sem, recv_sem, dev):
    return pltpu.make_async_remote_copy(src_ref=src, dst_ref=dst, send_sem=send_sem, recv_sem=recv_sem,
                                        device_id=dev, device_id_type=MESH)


def _gather_exchange(shards):
    nb = len(shards)

    def rows_of(i, owner, core):
        rs = shards[i].shape[0]
        return pl.ds(pl.multiple_of(owner * rs + core * (rs // 2), 16), rs // 2)

    def first_leg(ins, outs, send_sems, recv_sems, i, j):
        x, y, c = _position()
        px, py = _other_chips(x, y)[j]
        half = shards[i].shape[0] // 2
        mine = ins[i].at[pl.ds(pl.multiple_of(c * half, 16), half)]
        return _remote(mine, outs[i].at[rows_of(i, 2 * x + y, c)], send_sems.at[i, j], recv_sems.at[i, j], (px, py, c))

    def passed_on(outs, send_sems, recv_sems, i, j, core):
        x, y, c = _position()
        px, py = _other_chips(x, y)[j]
        rows = outs[i].at[rows_of(i, 2 * px + py, core)]
        return _remote(rows, rows, send_sems.at[i, 3 + j], recv_sems.at[i, 3 + j], (x, y, 1 - c))

    def own_block(ins, outs, send_sems, recv_sems, i):
        x, y, c = _position()
        rs = shards[i].shape[0]
        place = outs[i].at[pl.ds(pl.multiple_of((2 * x + y) * rs, 16), rs)]
        return _remote(ins[i], place, send_sems.at[i, 6], recv_sems.at[i, 6], (x, y, 1 - c))

    def start(ins, outs, send_sems, recv_sems):
        for i in range(nb):
            own_block(ins, outs, send_sems, recv_sems, i).start()
            for j in range(3):
                first_leg(ins, outs, send_sems, recv_sems, i, j).start()

    def finish(ins, outs, send_sems, recv_sems):
        x, y, c = _position()
        for i in range(nb):
            for j, (px, py) in enumerate(_other_chips(x, y)):
                landed = outs[i].at[rows_of(i, 2 * px + py, c)]
                _remote(landed, landed, send_sems.at[i, j], recv_sems.at[i, j], (px, py, c)).wait_recv()
                passed_on(outs, send_sems, recv_sems, i, j, c).start()
        for i in range(nb):
            own_block(ins, outs, send_sems, recv_sems, i).wait()
            for j in range(3):
                passed_on(outs, send_sems, recv_sems, i, j, 1 - c).wait_recv()
        for i in range(nb):
            for j in range(3):
                first_leg(ins, outs, send_sems, recv_sems, i, j).wait_send()
                passed_on(outs, send_sems, recv_sems, i, j, c).wait_send()

    return _Exchange(ins=list(shards), outs=[jax.ShapeDtypeStruct((N_CHIPS * s.shape[0], s.shape[1]), s.dtype) for s in shards],
                     aliases={}, sems=[(nb, 7), (nb, 7)], start=start, finish=finish)


def _run_exchange(ex, *, name):
    n_in, n_out = len(ex.ins), len(ex.outs)

    def body(*refs):
        c_in, c_out, sems = refs[:n_in], refs[n_in:n_in + n_out], refs[n_in + n_out:]
        ex.start(c_in, c_out, *sems)
        ex.finish(c_in, c_out, *sems)

    return pl.pallas_call(
        body, name=name, in_specs=[ANY] * n_in, out_specs=[ANY] * n_out, out_shape=list(ex.outs),
        input_output_aliases=dict(ex.aliases),
        scratch_shapes=[pltpu.SemaphoreType.DMA(s) for s in ex.sems],
    )(*ex.ins)


def _row_tile(rows):
    return max(t for t in range(16, min(rows, 512) + 1, 16) if rows % t == 0)


def _halves_exchange(grads):
    nb = len(grads)

    def copies(ins, outs, send_sems, recv_sems):
        x, y, c = _position()
        return [_remote(ins[i].at[:, 1 - c], outs[i], send_sems.at[i], recv_sems.at[i], (x, y, 1 - c)) for i in range(nb)]

    def start(ins, outs, send_sems, recv_sems):
        for cp in copies(ins, outs, send_sems, recv_sems):
            cp.start()

    def finish(ins, outs, send_sems, recv_sems):
        for cp in copies(ins, outs, send_sems, recv_sems):
            cp.wait()

    return _Exchange(ins=list(grads), outs=[jax.ShapeDtypeStruct((N_CHIPS, g.shape[2], g.shape[3]), F32) for g in grads],
                     aliases={}, sems=[(nb,), (nb,)], start=start, finish=finish)


def _join_exchanges(parts):
    assert all(not ex.aliases for ex in parts)

    def split(refs, counts):
        out, at = [], 0
        for k in counts:
            out.append(refs[at:at + k])
            at += k
        return out

    def run(which):
        def go(ins, outs, *sems):
            for ex, i, o, s in zip(parts, split(ins, [len(ex.ins) for ex in parts]), split(outs, [len(ex.outs) for ex in parts]),
                                   split(sems, [len(ex.sems) for ex in parts])):
                getattr(ex, which)(i, o, *s)
        return go

    return _Exchange(ins=[a for ex in parts for a in ex.ins], outs=[a for ex in parts for a in ex.outs], aliases={},
                     sems=[s for ex in parts for s in ex.sems], start=run("start"), finish=run("finish"))


def _pair_sum(g4, got, c_arr, *, name):
    _, _, half, cols = g4.shape
    tr = _row_tile(half)

    def body(c_ref, g_ref, t_ref, p_ref, pb_ref):
        sm = g_ref[...] + t_ref[...]
        p_ref[...] = sm
        pb_ref[...] = sm.astype(BF16)

    blk = pl.BlockSpec((None, tr, cols), lambda j, i, c_ref: (j, i, 0))
    grid_spec = pltpu.PrefetchScalarGridSpec(
        num_scalar_prefetch=1, grid=(N_CHIPS, half // tr),
        in_specs=[pl.BlockSpec((None, None, tr, cols), lambda j, i, c_ref: (j, c_ref[0], i, 0)), blk],
        out_specs=[blk, blk])
    return pl.pallas_call(
        body, name=name, grid_spec=grid_spec,
        out_shape=[jax.ShapeDtypeStruct((N_CHIPS, half, cols), F32), jax.ShapeDtypeStruct((N_CHIPS, half, cols), BF16)],
        compiler_params=_params(("parallel", "parallel")),
    )(c_arr, g4, got)


def _scatter_exchange(parts):
    nb = len(parts)

    def copies(ins, outs, send_sems, recv_sems):
        x, y, c = _position()
        return [_remote(ins[i].at[2 * px + py], outs[i].at[j], send_sems.at[i, j], recv_sems.at[i, j], (px, py, c))
                for i in range(nb) for j, (px, py) in enumerate(_other_chips(x, y))]

    def start(ins, outs, send_sems, recv_sems):
        for cp in copies(ins, outs, send_sems, recv_sems):
            cp.start()

    def finish(ins, outs, send_sems, recv_sems):
        for cp in copies(ins, outs, send_sems, recv_sems):
            cp.wait()

    return _Exchange(ins=list(parts), outs=[jax.ShapeDtypeStruct((3,) + p.shape[1:], p.dtype) for p in parts],
                     aliases={}, sems=[(nb, 3), (nb, 3)], start=start, finish=finish)


def _owner_sum(p, got, chip_arr, c_arr, *, replicated, name):
    _, half, cols = p.shape
    tr = _row_tile(half)

    def body(chip_ref, c_ref, p_ref, r_ref, o_ref):
        o_ref[...] = ((p_ref[...] + r_ref[0].astype(F32)) + r_ref[1].astype(F32)) + r_ref[2].astype(F32)

    if replicated:
        out_spec = pl.BlockSpec((None, None, tr, cols), lambda i, chip_ref, c_ref: (chip_ref[0], c_ref[0], i, 0))
        out_shape = jax.ShapeDtypeStruct((N_CHIPS, 2, half, cols), F32)
    else:
        out_spec = pl.BlockSpec((None, tr, cols), lambda i, chip_ref, c_ref: (c_ref[0], i, 0))
        out_shape = jax.ShapeDtypeStruct((2, half, cols), F32)
    grid_spec = pltpu.PrefetchScalarGridSpec(
        num_scalar_prefetch=2, grid=(half // tr,),
        in_specs=[pl.BlockSpec((None, tr, cols), lambda i, chip_ref, c_ref: (chip_ref[0], i, 0)),
                  pl.BlockSpec((3, tr, cols), lambda i, chip_ref, c_ref: (0, i, 0))],
        out_specs=out_spec)
    return pl.pallas_call(
        body, name=name, grid_spec=grid_spec, out_shape=out_shape,
        compiler_params=_params(("parallel",)),
    )(chip_arr, c_arr, p, got)


def _share_reduced(bufs):
    nb = len(bufs) - 1

    def body(*refs):
        outs = refs[nb + 1:2 * nb + 2]
        send_sems, recv_sems = refs[2 * nb + 2:]
        x, y, c = _position()
        chip = 2 * x + y
        sends = []
        for i in range(nb):
            cp = _remote(outs[i].at[c], outs[i].at[c], send_sems.at[i], recv_sems.at[i], (x, y, 1 - c))
            cp.start()
            sends.append(cp)
        small = outs[nb]
        peers = [(fx, fy, fc) for fx in (0, 1) for fy in (0, 1) for fc in (0, 1) if fx + fy + fc > 0]
        for k, (fx, fy, fc) in enumerate(peers):
            dev = (x ^ fx, y ^ fy, c ^ fc)
            cp = _remote(small.at[chip, c], small.at[chip, c], send_sems.at[nb + k], recv_sems.at[nb + k], dev)
            cp.start()
            sends.append(cp)
        for i in range(nb):
            dst = outs[i].at[1 - c]
            _remote(dst, dst, send_sems.at[i], recv_sems.at[i], (x, y, 1 - c)).wait_recv()
        for k, (fx, fy, fc) in enumerate(peers):
            dst = small.at[2 * (x ^ fx) + (y ^ fy), c ^ fc]
            _remote(dst, dst, send_sems.at[nb + k], recv_sems.at[nb + k], (x ^ fx, y ^ fy, c ^ fc)).wait_recv()
        for cp in sends:
            cp.wait_send()

    n_all = nb + 1
    return pl.pallas_call(
        body, name="grad_share_reduced", in_specs=[ANY] * n_all, out_specs=[ANY] * n_all,
        out_shape=[jax.ShapeDtypeStruct(b.shape, b.dtype) for b in bufs],
        input_output_aliases={i: i for i in range(n_all)},
        scratch_shapes=[pltpu.SemaphoreType.DMA((nb + 7,)), pltpu.SemaphoreType.DMA((nb + 7,))],
    )(*bufs)


class _GradReducer:
    def __init__(self, c_arr, chip_arr):
        self.c_arr, self.chip_arr = c_arr, chip_arr
        self.full, self.pairs, self.landed = {}, {}, {}

    def swap(self, names, grads):
        for n, g in zip(names, grads):
            self.full[n] = g.reshape(N_CHIPS, 2, g.shape[0] // (2 * N_CHIPS), g.shape[1])
        return _halves_exchange([self.full[n] for n in names])

    def swapped(self, names, bufs):
        for n, t in zip(names, bufs):
            self.pairs[n] = _pair_sum(self.full[n], t, self.c_arr, name="grad_pair_sum_" + n)

    def scatter(self, names):
        return _scatter_exchange([self.pairs[n][1] for n in names])

    def collect(self, names, bufs):
        self.landed.update(zip(names, bufs))

    def swap_now(self, names, grads):
        self.swapped(names, _run_exchange(self.swap(names, grads), name="grad_exchange_" + names[0]))

    def finish(self, names, grads, order):
        self.swap_now(names, grads)
        self.collect(names, _run_exchange(self.scatter(names), name="grad_scatter_" + names[0]))
        totals = [_owner_sum(self.pairs[n][0], self.landed[n], self.chip_arr, self.c_arr, replicated=(n == order[-1]),
                             name="grad_owner_sum_" + n) for n in order]
        return _share_reduced(totals)


def _pack_small(vals):
    flat = jnp.concatenate([vals[name].reshape(-1) for name, _ in SMALL])
    return jnp.pad(flat, (0, N_CHIPS * SMALL_ROWS * 1024 - SMALL_ELEMS)).reshape(N_CHIPS * SMALL_ROWS, 1024)


def _unpack_small(buf):
    flat = buf.reshape(-1)
    out, off = {}, 0
    for name, shape in SMALL:
        n = int(np.prod(shape))
        out[name] = flat[off:off + n].reshape(shape)
        off += n
    return out


EARLY_REDUCED = (("w_down",), ("w_up",), ("w_o", "w_ssm_br", "w_attn_br", "w_mem_br", "w_glu", "w_mem_kv"), ("w_in",))


def _device_step(x, mem, tgt, w, p, *, shards, reducer):
    rows = x.shape[0]
    w = dict(w)
    early = EARLY_REDUCED
    gb = {}
    gather_pending = shards is not None

    def riding(*stages):
        if reducer is None or not stages:
            return None
        return _join_exchanges([reducer.swap(names, [gb[n] for n in names]) if kind == "swap" else reducer.scatter(names)
                                for kind, names in stages])

    def arrived(stages, res):
        if reducer is None or not stages:
            return res
        main, bufs = res
        for kind, names in stages:
            (reducer.swapped if kind == "swap" else reducer.collect)(names, bufs[:len(names)])
            bufs = bufs[len(names):]
        return main

    def fetching(names):
        return _gather_exchange([shards[n] for n in names]) if gather_pending else None

    def fetched(names, res):
        if not gather_pending:
            return res
        w.update(zip(names, res[1]))
        return res[0]

    first_use = (("w_in",), ("w_glu", "w_ssm_br", "w_attn_br", "w_mem_kv", "w_mem_br", "w_o", "w_up"), ("w_down",))
    g1, gm, g2 = p["norm1_g"], p["mem_norm_g"], p["norm2_g"]
    gf = p["final_g"].reshape(1, D_MODEL)
    ssm_args = (p["ssm_lambda_re"][0], p["ssm_lambda_im"][0], p["ssm_log_dt"][0], p["ssm_b_re"][0],
                p["ssm_b_im"][0], p["ssm_c_re"][0], p["ssm_c_im"][0])
    (a_lay, b_blk, c_blk), ssm_vjp = jax.vjp(_ssm_matrices, *ssm_args)
    a_conj = a_lay * _to_scan_layout(jnp.stack([jnp.ones((N_STATES,), F32), -jnp.ones((N_STATES,), F32)]))[None, :]
    dd = p["ssm_d"].reshape(1, SSM_WIDTH)
    mm = _matmul

    n1 = fetched(first_use[0], _rmsnorm_fwd(x, g1, tm=512, carry=fetching(first_use[0]), name="norm1"))
    win_t = w["w_in"]
    splits = ((OFF_U, OFF_QKV - OFF_U), (OFF_QKV, OFF_MQ - OFF_QKV), (OFF_MQ, OFF_ZG - OFF_MQ), (OFF_ZG, IN_WIDTH - OFF_ZG))
    u, qkv, mq, zg = fetched(first_use[1], _split_matmul(n1, win_t, splits, tm=512, carry=fetching(first_use[1]),
                                                         vmem=VMEM_LIMIT_WIDE_BYTES, name="in_proj"))

    u_i = _interleave(u)
    ends = _ssm_ends(a_lay, u_i, b_blk, transpose=False, reverse=False, tt=512, name="ssm_fwd_ends")
    s, ys_i, s_entry = _ssm_fwd(a_lay, u_i, b_blk, c_blk, ends, tt=512, name="ssm_fwd")
    ys = _deinterleave(ys_i)
    y2 = _glu_fwd(ys, u, dd, w["w_glu"], p["b_glu"], tm=512, name="glu_fwd")

    outs, lses = [], []
    for g, (_, d) in enumerate(ATTN_PATTERNS):
        o_g, lse_g = _attn_fwd(qkv, g, d, name=f"attn_fwd_{g}")
        outs.append(o_g)
        lses.append(lse_g)
    o, lse = _attn_merge(outs, lses, tm=1024, name="attn_merge")

    mn = _rmsnorm_fwd(mem, gm, tm=MEM_LEN, name="mem_norm")
    kv = mm(mn, w["w_mem_kv"], m=MEM_LEN, n=1024, k=1024, tm=MEM_LEN, tn=1024, tk=1024, out_dtypes=(F32,), name="mem_kv")
    mo = _mem_attn_fwd(mq, kv, tq=1024, name="mem_attn_fwd")

    branch_acts = (y2, o, mo)
    branch_wts = (w["w_ssm_br"], w["w_attn_br"], w["w_mem_br"])
    merged = _branch_merge_fwd(branch_acts, branch_wts, zg, p["b_gate"], tm=256, name="branch_merge_fwd")
    h1, n2 = mm(merged, w["w_o"], m=rows, n=1024, k=1024, tm=1024, tn=1024, tk=1024, out_dtypes=(F32, BF16),
                aux=((x, "mn"), (g2, "row")), epilogue=_residual_norm_epilogue, name="out_proj")
    relu2 = lambda acc: (jnp.square(jnp.maximum(acc, 0.0)),)
    act = fetched(first_use[2], _sum_matmul([n2], w["w_up"], [0], tb=True, tm=512, out_dtype=BF16, epilogue=relu2,
                                            carry=fetching(first_use[2]), name="mlp_up"))
    dh2, d_gf, sq_err = _sum_matmul([act], w["w_down"], [0], tm=512, aux=((h1, "mn"), (tgt, "mn"), (gf, "row")),
                                    epilogue=_loss_head_epilogue, n_sums=2, name="mlp_down")
    loss = (0.5 / D_MODEL) * jnp.sum(sq_err)

    gs = {"final_g": d_gf.reshape(D_MODEL)}
    drelu2 = lambda acc, actv: (acc * (2.0 * jnp.sqrt(actv.astype(F32))),)
    dup = mm(dh2, w["w_down"], m=rows, n=D_FF, k=1024, tb=True, tm=1024, tn=2048, tk=1024, out_dtypes=(BF16,),
             aux=((act, "mn"),), epilogue=drelu2, name="d_act")
    gb["w_down"] = mm(act, dh2, m=D_FF, n=1024, k=rows, ta=True, tm=1024, tn=1024, tk=2048, out_dtypes=(F32,), name="dw_down")
    stages = (("swap", early[0]),)
    gb["w_up"] = arrived(stages, mm(dup, n2, m=D_FF, n=1024, k=rows, ta=True, tm=1024, tn=1024, tk=2048,
                                    out_dtypes=(F32,), carry=riding(*stages), name="dw_up"))
    stages = (("scatter", early[0]), ("swap", early[1]))
    dh1, gs["norm2_g"] = arrived(stages, _sum_matmul([dup], w["w_up"], [0], tm=512, aux=((h1, "mn"), (dh2, "mn"), (g2, "row")),
                                                     epilogue=_rmsnorm_bwd_epilogue, n_sums=1, carry=riding(*stages), name="d_n2"))
    dmerged = mm(dh1, w["w_o"], m=rows, n=1024, k=1024, tb=True, tm=1024, tn=1024, tk=1024, out_dtypes=(F32,), name="d_merged")
    gb["w_o"] = mm(merged, dh1, m=1024, n=1024, k=rows, ta=True, tm=1024, tn=1024, tk=2048, out_dtypes=(F32,), name="dw_o")
    stages = (("scatter", early[1]),)
    (dy2, do, dmo, gb["w_ssm_br"], gb["w_attn_br"], gb["w_mem_br"], dzg, gs["b_gate"]) = arrived(stages, _branch_merge_bwd(
        dmerged, branch_acts, branch_wts, zg, p["b_gate"], tm=256, carry=riding(*stages), name="branch_merge_bwd"))

    dy0, dt, y1, gs["b_glu"], d_dd = _glu_bwd(dy2, ys, u, dd, w["w_glu"], p["b_glu"], tm=512, name="glu_bwd")
    gs["ssm_d"] = d_dd.reshape(1, SSM_GROUPS, SSM_GROUP_SIZE)
    gb["w_glu"] = mm(y1, dt, m=512, n=512, k=rows, ta=True, tm=512, tn=512, tk=1024, out_dtypes=(F32,), name="dw_glu")
    dy0_i = _interleave(dy0)
    lam_ends = _ssm_ends(a_conj, dy0_i, c_blk, transpose=True, reverse=True, tt=512, name="ssm_bwd_ends")
    du_i, d_b_blk, d_c_blk, d_a_lay = _ssm_bwd(a_conj, dy0_i, u_i, s, s_entry, b_blk, c_blk, dd, lam_ends, tt=512,
                                                name="ssm_bwd")
    du = _deinterleave(du_i)
    d_ssm = ssm_vjp((d_a_lay, d_b_blk, d_c_blk))
    for name, val in zip(("ssm_lambda_re", "ssm_lambda_im", "ssm_log_dt", "ssm_b_re", "ssm_b_im", "ssm_c_re", "ssm_c_im"), d_ssm):
        gs[name] = val[None]

    dqkv = None
    for g, (_, d) in enumerate(ATTN_PATTERNS):
        dqkv = _attn_bwd(qkv, do, o, lse, g, d, dqkv, name=f"attn_bwd_{g}")

    dmq, dmk, dmv = _mem_attn_bwd(mq, kv, dmo, tq=1024, name="mem_attn_bwd")
    dkv = jnp.concatenate([dmk, dmv], axis=1)
    gb["w_mem_kv"] = mm(mn, dkv, m=1024, n=1024, k=MEM_LEN, ta=True, tm=1024, tn=1024, tk=MEM_LEN, out_dtypes=(F32,), name="dw_mem_kv")
    dmn = mm(dkv, w["w_mem_kv"], m=MEM_LEN, n=1024, k=1024, tb=True, tm=MEM_LEN, tn=1024, tk=1024, out_dtypes=(F32,), name="d_mn")
    _, gs["mem_norm_g"] = _rmsnorm_bwd(mem, gm, dmn, None, tm=MEM_LEN, name="mem_norm_bwd")

    pieces = ((du, OFF_U, "u"), (dqkv[0], OFF_QKV, "q"), (dqkv[1], OFF_QKV + 768, "k"), (dqkv[2], OFF_QKV + 1536, "v"),
              (dmq, OFF_MQ, "mq"), (dzg, OFF_ZG, "zg"))
    dw_rows = []
    for piece, off, tag in pieces:
        width = piece.shape[1]
        tmw = 1024 if width % 1024 == 0 else (768 if width == 768 else 512)
        stages = {"q": (("swap", early[2]),), "zg": (("scatter", early[2]),)}.get(tag, ())
        dw_rows.append(arrived(stages, mm(piece, n1, m=width, n=1024, k=rows, ta=True, tm=tmw, tn=1024, tk=2048,
                                          out_dtypes=(F32,), carry=riding(*stages), name="dw_in_" + tag)))
    gb["w_in"] = jnp.concatenate(dw_rows, axis=0)
    if reducer is not None:
        reducer.swap_now(early[3], [gb["w_in"]])
    stages = (("scatter", early[3]),)
    dx, gs["norm1_g"] = arrived(stages, _sum_matmul(
        [piece for piece, _, _ in pieces], win_t, [off for _, off, _ in pieces], tm=512,
        aux=((x, "mn"), (dh1, "mn"), (g1, "row")), epilogue=_rmsnorm_bwd_epilogue, n_sums=1,
        carry=riding(*stages), vmem=VMEM_LIMIT_WIDE_BYTES, name="d_n1"))
    return loss, dx, gb, gs


def kernel(x, mem, norm1_g, mem_norm_g, w_in, b_gate, ssm_lambda_re, ssm_lambda_im, ssm_log_dt, ssm_b_re, ssm_b_im, ssm_c_re, ssm_c_im, ssm_d, w_glu, b_glu, w_ssm_br, w_attn_br, w_mem_kv, w_mem_br, w_o, norm2_g, w_up, w_down, final_g, loss_target, m_norm1_g, m_mem_norm_g, m_w_in, m_b_gate, m_ssm_lambda_re, m_ssm_lambda_im, m_ssm_log_dt, m_ssm_b_re, m_ssm_b_im, m_ssm_c_re, m_ssm_c_im, m_ssm_d, m_w_glu, m_b_glu, m_w_ssm_br, m_w_attn_br, m_w_mem_kv, m_w_mem_br, m_w_o, m_norm2_g, m_w_up, m_w_down, m_final_g, v_norm1_g, v_mem_norm_g, v_w_in, v_b_gate, v_ssm_lambda_re, v_ssm_lambda_im, v_ssm_log_dt, v_ssm_b_re, v_ssm_b_im, v_ssm_c_re, v_ssm_c_im, v_ssm_d, v_w_glu, v_b_glu, v_w_ssm_br, v_w_attn_br, v_w_mem_kv, v_w_mem_br, v_w_o, v_norm2_g, v_w_up, v_w_down, v_final_g):
    env = dict(locals())
    weights = {n: env[n] for n in WEIGHT_ORDER}
    moms = {n: env["m_" + n] for n in WEIGHT_ORDER}
    vels = {n: env["v_" + n] for n in WEIGHT_ORDER}

    chip = 2 * lax.axis_index("x") + lax.axis_index("y")
    wire = [weights[n].reshape(weights[n].shape[-2:]).astype(BF16) for n, _, _ in BIG]
    wire = dict(zip([n for n, _, _ in BIG], [s.T if tr else s for s, (_, tr, _) in zip(wire, BIG)]))
    small = {n: weights[n] for n, _ in SMALL}

    reducer = _GradReducer(lax.axis_index("c").astype(jnp.int32).reshape(1), chip.astype(jnp.int32).reshape(1))
    loss, dx, gb, gs = _device_step(x[0], mem[0], loss_target[0], {}, small, shards=wire, reducer=reducer)
    *shards, small_grad = reducer.finish(["small"], [_pack_small(gs)], [n for n, _, _ in BIG] + ["small"])
    grads = {}
    for (n, tr, _), sh in zip(BIG, shards):
        sh = sh.reshape(2 * sh.shape[1], sh.shape[2])
        grads[n] = sh.T if tr else sh
    small_grad = small_grad.reshape(N_CHIPS * SMALL_ROWS, 1024)
    grads_small = _unpack_small(small_grad)

    delta, new_m, new_v = {}, {}, {}
    for n, _, _ in BIG:
        grads[n], delta[n], new_m[n], new_v[n] = _adamw(weights[n], grads[n], moms[n], vels[n],
                                                        tr=min(weights[n].shape[-2], 256), name="adamw_" + n)
    _, ds_, ms_, vs_ = _adamw(_pack_small(small), small_grad,
                              _pack_small({n: moms[n] for n, _ in SMALL}), _pack_small({n: vels[n] for n, _ in SMALL}),
                              tr=N_CHIPS * SMALL_ROWS, name="adamw_small")
    for dst, buf in ((delta, ds_), (new_m, ms_), (new_v, vs_)):
        dst.update(_unpack_small(buf))
    grads.update(grads_small)

    total_loss = lax.psum(loss, ("x", "y", "c"))
    return (total_loss, dx[None], *[grads[n] for n in WEIGHT_ORDER], *[delta[n] for n in WEIGHT_ORDER],
            *[new_m[n] for n in WEIGHT_ORDER], *[new_v[n] for n in WEIGHT_ORDER])
```
